```python
import jax, jax.numpy as jnp
from jax import lax
import numpy as np

D_MODEL = 1024
BATCH = 8
SEQ = 4096
DEPTH = 1

CHUNK = 64
N_LEFT_CHUNKS = 8
BAND = (N_LEFT_CHUNKS + 1) * CHUNK
ATT_HEADS = 8
HEAD_DIM = 64
D_ATT = ATT_HEADS * HEAD_DIM
REL_CLIP = 256
N_REL = 2 * REL_CLIP + 1
SGU_BLOCK = 128
SGU_GROUPS = 8
SGU_GROUP_DIM = 64
D_SGU = SGU_GROUPS * SGU_GROUP_DIM
D_FF = 2816
D_IN = 3 * D_ATT + 2 * D_SGU + 2 * D_MODEL
EPS = 1e-6
NEG_INF = -1e30

kernel_name = "macaron_gated_chunkattn_gmlp_block"


def rmsnorm(x, g):
    xf = x.astype(jnp.float32)
    y = xf * lax.rsqrt(jnp.mean(xf * xf, axis=-1, keepdims=True) + EPS)
    return (y * g.astype(jnp.float32)).astype(x.dtype)


def layernorm(x, g, b):
    xf = x.astype(jnp.float32)
    mu = jnp.mean(xf, axis=-1, keepdims=True)
    var = jnp.mean(jnp.square(xf - mu), axis=-1, keepdims=True)
    y = (xf - mu) * lax.rsqrt(var + EPS)
    return (y * g.astype(jnp.float32) + b.astype(jnp.float32)).astype(x.dtype)


def swiglu(h, w_gate, w_up, w_down):
    return (jax.nn.silu(h @ w_gate) * (h @ w_up)) @ w_down


def chunked_rel_attention(q, k, v, rel_table):
    B, S = q.shape[0], q.shape[1]
    n_c = S // CHUNK
    pad = N_LEFT_CHUNKS * CHUNK
    qc = q.reshape(B, n_c, CHUNK, ATT_HEADS, HEAD_DIM)
    kp = jnp.pad(k, ((0, 0), (pad, 0), (0, 0), (0, 0))).reshape(B, n_c + N_LEFT_CHUNKS, CHUNK, ATT_HEADS, HEAD_DIM)
    vp = jnp.pad(v, ((0, 0), (pad, 0), (0, 0), (0, 0))).reshape(B, n_c + N_LEFT_CHUNKS, CHUNK, ATT_HEADS, HEAD_DIM)
    k_band = jnp.concatenate([kp[:, i:i + n_c] for i in range(N_LEFT_CHUNKS + 1)], axis=2)
    v_band = jnp.concatenate([vp[:, i:i + n_c] for i in range(N_LEFT_CHUNKS + 1)], axis=2)
    scores = jnp.einsum('bcqhd,bckhd->bchqk', qc, k_band).astype(jnp.float32) * (HEAD_DIM ** -0.5)
    qi = jnp.arange(CHUNK)[:, None]
    kj = jnp.arange(BAND)[None, :]
    rel = jnp.clip(qi + pad - kj, -REL_CLIP, REL_CLIP) + REL_CLIP
    bias = rel_table.astype(jnp.float32)[:, rel]
    key_pos = jnp.arange(n_c)[:, None] * CHUNK + jnp.arange(BAND)[None, :] - pad
    valid = key_pos >= 0
    scores = jnp.where(valid[None, :, None, None, :], scores + bias[None, None], NEG_INF)
    probs = jax.nn.softmax(scores, axis=-1).astype(v.dtype)
    out = jnp.einsum('bchqk,bckhd->bcqhd', probs, v_band)
    return out.reshape(B, S, D_ATT)


def spatial_gating(z, ln_g, ln_b, w_s, b_s):
    B, S = z.shape[0], z.shape[1]
    n_blk = S // SGU_BLOCK
    u, vs = z[..., :D_SGU], z[..., D_SGU:]
    vs = layernorm(vs, ln_g, ln_b).reshape(B, n_blk, SGU_BLOCK, SGU_GROUPS, SGU_GROUP_DIM)
    pos = jnp.arange(SGU_BLOCK)
    mask = (pos[:, None] // CHUNK) >= (pos[None, :] // CHUNK)
    w_m = jnp.where(mask[None], w_s, jnp.zeros_like(w_s))
    s = jnp.einsum('gij,bnjgd->bnigd', w_m, vs) + b_s.T[None, None, :, :, None]
    return u * s.reshape(B, S, D_SGU)


def _fwd_setup_inputs(seed: int = 0) -> dict:
    key = jax.random.key(seed)
    ks = jax.random.split(key, 24)
    L = DEPTH
    f32 = jnp.float32

    def nrm(k, shape, scale):
        return jax.random.normal(k, shape, f32) * scale

    def gain(k, shape):
        return 1.0 + 0.05 * jax.random.normal(k, shape, f32)

    return {
        "x": jax.random.normal(ks[0], (BATCH, SEQ, D_MODEL), f32),
        "norm_ffn1": gain(ks[1], (L, D_MODEL)),
        "ffn1_w_gate": nrm(ks[2], (L, D_MODEL, D_FF), D_MODEL ** -0.5),
        "ffn1_w_up": nrm(ks[3], (L, D_MODEL, D_FF), D_MODEL ** -0.5),
        "ffn1_w_down": nrm(ks[4], (L, D_FF, D_MODEL), D_FF ** -0.5),
        "norm_mix": gain(ks[5], (L, D_MODEL)),
        "w_in": nrm(ks[6], (L, D_MODEL, D_IN), D_MODEL ** -0.5),
        "b_gate": nrm(ks[7], (L, 2 * D_MODEL), 0.01),
        "rel_bias": nrm(ks[8], (L, ATT_HEADS, N_REL), 0.1),
        "sgu_ln_g": gain(ks[9], (L, D_SGU)),
        "sgu_ln_b": nrm(ks[10], (L, D_SGU), 0.01),
        "sgu_w_s": nrm(ks[11], (L, SGU_GROUPS, SGU_BLOCK, SGU_BLOCK), SGU_BLOCK ** -0.5),
        "sgu_b_s": gain(ks[12], (L, SGU_GROUPS, SGU_BLOCK)),
        "w_branch_att": nrm(ks[13], (L, D_ATT, D_MODEL), D_ATT ** -0.5),
        "w_branch_sgu": nrm(ks[14], (L, D_SGU, D_MODEL), D_SGU ** -0.5),
        "w_out": nrm(ks[15], (L, D_MODEL, D_MODEL), D_MODEL ** -0.5),
        "norm_ffn2": gain(ks[16], (L, D_MODEL)),
        "ffn2_w_gate": nrm(ks[17], (L, D_MODEL, D_FF), D_MODEL ** -0.5),
        "ffn2_w_up": nrm(ks[18], (L, D_MODEL, D_FF), D_MODEL ** -0.5),
        "ffn2_w_down": nrm(ks[19], (L, D_FF, D_MODEL), D_FF ** -0.5),
        "norm_final": gain(ks[20], (D_MODEL,)),
    }


def _fwd_reference(x, norm_ffn1, ffn1_w_gate, ffn1_w_up, ffn1_w_down, norm_mix, w_in, b_gate,
              rel_bias, sgu_ln_g, sgu_ln_b, sgu_w_s, sgu_b_s, w_branch_att, w_branch_sgu,
              w_out, norm_ffn2, ffn2_w_gate, ffn2_w_up, ffn2_w_down, norm_final):
    B, S = x.shape[0], x.shape[1]
    for l in range(DEPTH):
        x = x + 0.5 * swiglu(rmsnorm(x, norm_ffn1[l]), ffn1_w_gate[l], ffn1_w_up[l], ffn1_w_down[l])
        h = rmsnorm(x, norm_mix[l])
        z = h @ w_in[l]
        o = 0
        q = z[..., o:o + D_ATT].reshape(B, S, ATT_HEADS, HEAD_DIM); o += D_ATT
        k = z[..., o:o + D_ATT].reshape(B, S, ATT_HEADS, HEAD_DIM); o += D_ATT
        v = z[..., o:o + D_ATT].reshape(B, S, ATT_HEADS, HEAD_DIM); o += D_ATT
        z_sgu = jax.nn.gelu(z[..., o:o + 2 * D_SGU]); o += 2 * D_SGU
        g = jax.nn.sigmoid(z[..., o:o + 2 * D_MODEL] + b_gate[l])
        g_att, g_sgu = g[..., :D_MODEL], g[..., D_MODEL:]
        y_att = chunked_rel_attention(q, k, v, rel_bias[l])
        y_sgu = spatial_gating(z_sgu, sgu_ln_g[l], sgu_ln_b[l], sgu_w_s[l], sgu_b_s[l])
        merged = g_att * (y_att @ w_branch_att[l]) + g_sgu * (y_sgu @ w_branch_sgu[l])
        x = x + merged @ w_out[l]
        x = x + 0.5 * swiglu(rmsnorm(x, norm_ffn2[l]), ffn2_w_gate[l], ffn2_w_up[l], ffn2_w_down[l])
    return rmsnorm(x, norm_final)


import jax as _jax
import jax.numpy as _jnp

TWIN_FORMAT = 'train_step'
FWD_PARAMS = ['x', 'norm_ffn1', 'ffn1_w_gate', 'ffn1_w_up', 'ffn1_w_down', 'norm_mix', 'w_in', 'b_gate', 'rel_bias', 'sgu_ln_g', 'sgu_ln_b', 'sgu_w_s', 'sgu_b_s', 'w_branch_att', 'w_branch_sgu', 'w_out', 'norm_ffn2', 'ffn2_w_gate', 'ffn2_w_up', 'ffn2_w_down', 'norm_final']
TWIN_WEIGHTS = ['norm_ffn1', 'ffn1_w_gate', 'ffn1_w_up', 'ffn1_w_down', 'norm_mix', 'w_in', 'b_gate', 'rel_bias', 'sgu_ln_g', 'sgu_ln_b', 'sgu_w_s', 'sgu_b_s', 'w_branch_att', 'w_branch_sgu', 'w_out', 'norm_ffn2', 'ffn2_w_gate', 'ffn2_w_up', 'ffn2_w_down', 'norm_final']
TWIN_DIFF_INPUT = 'x'
TWIN_INPUTS = ['x', 'norm_ffn1', 'ffn1_w_gate', 'ffn1_w_up', 'ffn1_w_down', 'norm_mix', 'w_in', 'b_gate', 'rel_bias', 'sgu_ln_g', 'sgu_ln_b', 'sgu_w_s', 'sgu_b_s', 'w_branch_att', 'w_branch_sgu', 'w_out', 'norm_ffn2', 'ffn2_w_gate', 'ffn2_w_up', 'ffn2_w_down', 'norm_final', 'loss_target', 'm_norm_ffn1', 'm_ffn1_w_gate', 'm_ffn1_w_up', 'm_ffn1_w_down', 'm_norm_mix', 'm_w_in', 'm_b_gate', 'm_rel_bias', 'm_sgu_ln_g', 'm_sgu_ln_b', 'm_sgu_w_s', 'm_sgu_b_s', 'm_w_branch_att', 'm_w_branch_sgu', 'm_w_out', 'm_norm_ffn2', 'm_ffn2_w_gate', 'm_ffn2_w_up', 'm_ffn2_w_down', 'm_norm_final', 'v_norm_ffn1', 'v_ffn1_w_gate', 'v_ffn1_w_up', 'v_ffn1_w_down', 'v_norm_mix', 'v_w_in', 'v_b_gate', 'v_rel_bias', 'v_sgu_ln_g', 'v_sgu_ln_b', 'v_sgu_w_s', 'v_sgu_b_s', 'v_w_branch_att', 'v_w_branch_sgu', 'v_w_out', 'v_norm_ffn2', 'v_ffn2_w_gate', 'v_ffn2_w_up', 'v_ffn2_w_down', 'v_norm_final']
TWIN_OUTPUTS = ['loss', 'grad_x', 'grad_norm_ffn1', 'grad_ffn1_w_gate', 'grad_ffn1_w_up', 'grad_ffn1_w_down', 'grad_norm_mix', 'grad_w_in', 'grad_b_gate', 'grad_rel_bias', 'grad_sgu_ln_g', 'grad_sgu_ln_b', 'grad_sgu_w_s', 'grad_sgu_b_s', 'grad_w_branch_att', 'grad_w_branch_sgu', 'grad_w_out', 'grad_norm_ffn2', 'grad_ffn2_w_gate', 'grad_ffn2_w_up', 'grad_ffn2_w_down', 'grad_norm_final', 'delta_norm_ffn1', 'delta_ffn1_w_gate', 'delta_ffn1_w_up', 'delta_ffn1_w_down', 'delta_norm_mix', 'delta_w_in', 'delta_b_gate', 'delta_rel_bias', 'delta_sgu_ln_g', 'delta_sgu_ln_b', 'delta_sgu_w_s', 'delta_sgu_b_s', 'delta_w_branch_att', 'delta_w_branch_sgu', 'delta_w_out', 'delta_norm_ffn2', 'delta_ffn2_w_gate', 'delta_ffn2_w_up', 'delta_ffn2_w_down', 'delta_norm_final', 'new_m_norm_ffn1', 'new_m_ffn1_w_gate', 'new_m_ffn1_w_up', 'new_m_ffn1_w_down', 'new_m_norm_mix', 'new_m_w_in', 'new_m_b_gate', 'new_m_rel_bias', 'new_m_sgu_ln_g', 'new_m_sgu_ln_b', 'new_m_sgu_w_s', 'new_m_sgu_b_s', 'new_m_w_branch_att', 'new_m_w_branch_sgu', 'new_m_w_out', 'new_m_norm_ffn2', 'new_m_ffn2_w_gate', 'new_m_ffn2_w_up', 'new_m_ffn2_w_down', 'new_m_norm_final', 'new_v_norm_ffn1', 'new_v_ffn1_w_gate', 'new_v_ffn1_w_up', 'new_v_ffn1_w_down', 'new_v_norm_mix', 'new_v_w_in', 'new_v_b_gate', 'new_v_rel_bias', 'new_v_sgu_ln_g', 'new_v_sgu_ln_b', 'new_v_sgu_w_s', 'new_v_sgu_b_s', 'new_v_w_branch_att', 'new_v_w_branch_sgu', 'new_v_w_out', 'new_v_norm_ffn2', 'new_v_ffn2_w_gate', 'new_v_ffn2_w_up', 'new_v_ffn2_w_down', 'new_v_norm_final']
TWIN_LEAF_KINDS = {'loss': 'loss', 'grad_x': 'grad_x', 'grad_norm_ffn1': 'grad_w', 'grad_ffn1_w_gate': 'grad_w', 'grad_ffn1_w_up': 'grad_w', 'grad_ffn1_w_down': 'grad_w', 'grad_norm_mix': 'grad_w', 'grad_w_in': 'grad_w', 'grad_b_gate': 'grad_w', 'grad_rel_bias': 'grad_w', 'grad_sgu_ln_g': 'grad_w', 'grad_sgu_ln_b': 'grad_w', 'grad_sgu_w_s': 'grad_w', 'grad_sgu_b_s': 'grad_w', 'grad_w_branch_att': 'grad_w', 'grad_w_branch_sgu': 'grad_w', 'grad_w_out': 'grad_w', 'grad_norm_ffn2': 'grad_w', 'grad_ffn2_w_gate': 'grad_w', 'grad_ffn2_w_up': 'grad_w', 'grad_ffn2_w_down': 'grad_w', 'grad_norm_final': 'grad_w', 'delta_norm_ffn1': 'delta_w', 'delta_ffn1_w_gate': 'delta_w', 'delta_ffn1_w_up': 'delta_w', 'delta_ffn1_w_down': 'delta_w', 'delta_norm_mix': 'delta_w', 'delta_w_in': 'delta_w', 'delta_b_gate': 'delta_w', 'delta_rel_bias': 'delta_w', 'delta_sgu_ln_g': 'delta_w', 'delta_sgu_ln_b': 'delta_w', 'delta_sgu_w_s': 'delta_w', 'delta_sgu_b_s': 'delta_w', 'delta_w_branch_att': 'delta_w', 'delta_w_branch_sgu': 'delta_w', 'delta_w_out': 'delta_w', 'delta_norm_ffn2': 'delta_w', 'delta_ffn2_w_gate': 'delta_w', 'delta_ffn2_w_up': 'delta_w', 'delta_ffn2_w_down': 'delta_w', 'delta_norm_final': 'delta_w', 'new_m_norm_ffn1': 'new_m', 'new_m_ffn1_w_gate': 'new_m', 'new_m_ffn1_w_up': 'new_m', 'new_m_ffn1_w_down': 'new_m', 'new_m_norm_mix': 'new_m', 'new_m_w_in': 'new_m', 'new_m_b_gate': 'new_m', 'new_m_rel_bias': 'new_m', 'new_m_sgu_ln_g': 'new_m', 'new_m_sgu_ln_b': 'new_m', 'new_m_sgu_w_s': 'new_m', 'new_m_sgu_b_s': 'new_m', 'new_m_w_branch_att': 'new_m', 'new_m_w_branch_sgu': 'new_m', 'new_m_w_out': 'new_m', 'new_m_norm_ffn2': 'new_m', 'new_m_ffn2_w_gate': 'new_m', 'new_m_ffn2_w_up': 'new_m', 'new_m_ffn2_w_down': 'new_m', 'new_m_norm_final': 'new_m', 'new_v_norm_ffn1': 'new_v', 'new_v_ffn1_w_gate': 'new_v', 'new_v_ffn1_w_up': 'new_v', 'new_v_ffn1_w_down': 'new_v', 'new_v_norm_mix': 'new_v', 'new_v_w_in': 'new_v', 'new_v_b_gate': 'new_v', 'new_v_rel_bias': 'new_v', 'new_v_sgu_ln_g': 'new_v', 'new_v_sgu_ln_b': 'new_v', 'new_v_sgu_w_s': 'new_v', 'new_v_sgu_b_s': 'new_v', 'new_v_w_branch_att': 'new_v', 'new_v_w_branch_sgu': 'new_v', 'new_v_w_out': 'new_v', 'new_v_norm_ffn2': 'new_v', 'new_v_ffn2_w_gate': 'new_v', 'new_v_ffn2_w_up': 'new_v', 'new_v_ffn2_w_down': 'new_v', 'new_v_norm_final': 'new_v'}


def _forward(args):
    return _fwd_reference(*[args[k] for k in FWD_PARAMS])


def _output_shape():
    out = _jax.eval_shape(lambda: _forward(_fwd_setup_inputs(0)))
    return out.shape, out.dtype

N_MICROBATCH = 1
ADAM_LR = 0.001
ADAM_B1 = 0.9
ADAM_B2 = 0.999
ADAM_EPS = 1e-08
ADAM_WD = 0.01
ADAM_STEP = 10
PER_EXAMPLE_BATCH_AXIS = {'x': 0, 'loss_target': 0}
SHARED_INPUTS = []
_WEIGHT_DTYPES = {'norm_ffn1': _jnp.float32, 'ffn1_w_gate': _jnp.float32, 'ffn1_w_up': _jnp.float32, 'ffn1_w_down': _jnp.float32, 'norm_mix': _jnp.float32, 'w_in': _jnp.float32, 'b_gate': _jnp.float32, 'rel_bias': _jnp.float32, 'sgu_ln_g': _jnp.float32, 'sgu_ln_b': _jnp.float32, 'sgu_w_s': _jnp.float32, 'sgu_b_s': _jnp.float32, 'w_branch_att': _jnp.float32, 'w_branch_sgu': _jnp.float32, 'w_out': _jnp.float32, 'norm_ffn2': _jnp.float32, 'ffn2_w_gate': _jnp.float32, 'ffn2_w_up': _jnp.float32, 'ffn2_w_down': _jnp.float32, 'norm_final': _jnp.float32}
MOMENT_SCALE = {'norm_ffn1': 8.711233e-02, 'ffn1_w_gate': 3.510948e-02, 'ffn1_w_up': 3.396440e-02, 'ffn1_w_down': 5.630642e-02, 'norm_mix': 1.025267e-01, 'w_in': 4.861336e-02, 'b_gate': 2.224487e-02, 'rel_bias': 6.162061e-03, 'sgu_ln_g': 7.897734e-02, 'sgu_ln_b': 6.812879e-02, 'sgu_w_s': 4.944951e-02, 'sgu_b_s': 5.883266e-02, 'w_branch_att': 1.306163e-02, 'w_branch_sgu': 8.009378e-02, 'w_out': 7.962534e-02, 'norm_ffn2': 6.554827e-02, 'ffn2_w_gate': 2.829006e-02, 'ffn2_w_up': 2.755351e-02, 'ffn2_w_down': 4.572905e-02, 'norm_final': 3.209535e+01}


def _to_microbatches(a, axis):
    t = _jnp.moveaxis(a, axis, 0)
    t = t.reshape((N_MICROBATCH, t.shape[0] // N_MICROBATCH) + t.shape[1:])
    return _jnp.moveaxis(t, 1, axis + 1)


def setup_inputs(seed: int = 0) -> dict:
    inp = _fwd_setup_inputs(seed)
    key = _jax.random.fold_in(_jax.random.key(seed), 7919)
    shape, _ = _output_shape()
    out = dict(inp)
    out["loss_target"] = _jax.random.normal(_jax.random.fold_in(key, 0), shape, _jnp.float32)
    for i, name in enumerate(TWIN_WEIGHTS):
        w = inp[name].astype(_jnp.float32)
        if MOMENT_SCALE is None:
            s = _jnp.sqrt(_jnp.mean(_jnp.square(w)) + 1e-30)
        else:
            s = MOMENT_SCALE[name]
        km, kv = _jax.random.split(_jax.random.fold_in(key, i + 1))
        out[name] = w
        out["m_" + name] = s * _jax.random.normal(km, w.shape, _jnp.float32)
        out["v_" + name] = (s * s) * _jax.random.uniform(kv, w.shape, _jnp.float32, 0.5, 1.5)
    if N_MICROBATCH > 1:
        for name, axis in PER_EXAMPLE_BATCH_AXIS.items():
            out[name] = _to_microbatches(out[name], axis)
    return {'x': out['x'], 'norm_ffn1': out['norm_ffn1'], 'ffn1_w_gate': out['ffn1_w_gate'], 'ffn1_w_up': out['ffn1_w_up'], 'ffn1_w_down': out['ffn1_w_down'], 'norm_mix': out['norm_mix'], 'w_in': out['w_in'], 'b_gate': out['b_gate'], 'rel_bias': out['rel_bias'], 'sgu_ln_g': out['sgu_ln_g'], 'sgu_ln_b': out['sgu_ln_b'], 'sgu_w_s': out['sgu_w_s'], 'sgu_b_s': out['sgu_b_s'], 'w_branch_att': out['w_branch_att'], 'w_branch_sgu': out['w_branch_sgu'], 'w_out': out['w_out'], 'norm_ffn2': out['norm_ffn2'], 'ffn2_w_gate': out['ffn2_w_gate'], 'ffn2_w_up': out['ffn2_w_up'], 'ffn2_w_down': out['ffn2_w_down'], 'norm_final': out['norm_final'], 'loss_target': out['loss_target'], 'm_norm_ffn1': out['m_norm_ffn1'], 'm_ffn1_w_gate': out['m_ffn1_w_gate'], 'm_ffn1_w_up': out['m_ffn1_w_up'], 'm_ffn1_w_down': out['m_ffn1_w_down'], 'm_norm_mix': out['m_norm_mix'], 'm_w_in': out['m_w_in'], 'm_b_gate': out['m_b_gate'], 'm_rel_bias': out['m_rel_bias'], 'm_sgu_ln_g': out['m_sgu_ln_g'], 'm_sgu_ln_b': out['m_sgu_ln_b'], 'm_sgu_w_s': out['m_sgu_w_s'], 'm_sgu_b_s': out['m_sgu_b_s'], 'm_w_branch_att': out['m_w_branch_att'], 'm_w_branch_sgu': out['m_w_branch_sgu'], 'm_w_out': out['m_w_out'], 'm_norm_ffn2': out['m_norm_ffn2'], 'm_ffn2_w_gate': out['m_ffn2_w_gate'], 'm_ffn2_w_up': out['m_ffn2_w_up'], 'm_ffn2_w_down': out['m_ffn2_w_down'], 'm_norm_final': out['m_norm_final'], 'v_norm_ffn1': out['v_norm_ffn1'], 'v_ffn1_w_gate': out['v_ffn1_w_gate'], 'v_ffn1_w_up': out['v_ffn1_w_up'], 'v_ffn1_w_down': out['v_ffn1_w_down'], 'v_norm_mix': out['v_norm_mix'], 'v_w_in': out['v_w_in'], 'v_b_gate': out['v_b_gate'], 'v_rel_bias': out['v_rel_bias'], 'v_sgu_ln_g': out['v_sgu_ln_g'], 'v_sgu_ln_b': out['v_sgu_ln_b'], 'v_sgu_w_s': out['v_sgu_w_s'], 'v_sgu_b_s': out['v_sgu_b_s'], 'v_w_branch_att': out['v_w_branch_att'], 'v_w_branch_sgu': out['v_w_branch_sgu'], 'v_w_out': out['v_w_out'], 'v_norm_ffn2': out['v_norm_ffn2'], 'v_ffn2_w_gate': out['v_ffn2_w_gate'], 'v_ffn2_w_up': out['v_ffn2_w_up'], 'v_ffn2_w_down': out['v_ffn2_w_down'], 'v_norm_final': out['v_norm_final']}


def _loss(weights, diff, rest, loss_target):
    with _jax.named_scope("forward"):
        args = {**rest, TWIN_DIFF_INPUT: diff, **{k: w.astype(_WEIGHT_DTYPES[k]) for k, w in weights.items()}}
        y = _forward(args)
    with _jax.named_scope("loss_head"):
        err = _jnp.square(y.astype(_jnp.float32) - loss_target)
        return 0.5 * _jnp.sum(_jnp.mean(err, axis=-1)) if err.ndim else 0.5 * err


def _adamw(w, g, m, v):
    m = ADAM_B1 * m + (1.0 - ADAM_B1) * g
    v = ADAM_B2 * v + (1.0 - ADAM_B2) * _jnp.square(g)
    m_hat = m / (1.0 - ADAM_B1 ** ADAM_STEP)
    v_hat = v / (1.0 - ADAM_B2 ** ADAM_STEP)
    delta = -ADAM_LR * (m_hat / (_jnp.sqrt(v_hat) + ADAM_EPS) + ADAM_WD * w)
    return delta, m, v


def reference(x, norm_ffn1, ffn1_w_gate, ffn1_w_up, ffn1_w_down, norm_mix, w_in, b_gate, rel_bias, sgu_ln_g, sgu_ln_b, sgu_w_s, sgu_b_s, w_branch_att, w_branch_sgu, w_out, norm_ffn2, ffn2_w_gate, ffn2_w_up, ffn2_w_down, norm_final, loss_target, m_norm_ffn1, m_ffn1_w_gate, m_ffn1_w_up, m_ffn1_w_down, m_norm_mix, m_w_in, m_b_gate, m_rel_bias, m_sgu_ln_g, m_sgu_ln_b, m_sgu_w_s, m_sgu_b_s, m_w_branch_att, m_w_branch_sgu, m_w_out, m_norm_ffn2, m_ffn2_w_gate, m_ffn2_w_up, m_ffn2_w_down, m_norm_final, v_norm_ffn1, v_ffn1_w_gate, v_ffn1_w_up, v_ffn1_w_down, v_norm_mix, v_w_in, v_b_gate, v_rel_bias, v_sgu_ln_g, v_sgu_ln_b, v_sgu_w_s, v_sgu_b_s, v_w_branch_att, v_w_branch_sgu, v_w_out, v_norm_ffn2, v_ffn2_w_gate, v_ffn2_w_up, v_ffn2_w_down, v_norm_final):
    given = dict(x=x, norm_ffn1=norm_ffn1, ffn1_w_gate=ffn1_w_gate, ffn1_w_up=ffn1_w_up, ffn1_w_down=ffn1_w_down, norm_mix=norm_mix, w_in=w_in, b_gate=b_gate, rel_bias=rel_bias, sgu_ln_g=sgu_ln_g, sgu_ln_b=sgu_ln_b, sgu_w_s=sgu_w_s, sgu_b_s=sgu_b_s, w_branch_att=w_branch_att, w_branch_sgu=w_branch_sgu, w_out=w_out, norm_ffn2=norm_ffn2, ffn2_w_gate=ffn2_w_gate, ffn2_w_up=ffn2_w_up, ffn2_w_down=ffn2_w_down, norm_final=norm_final, loss_target=loss_target, m_norm_ffn1=m_norm_ffn1, m_ffn1_w_gate=m_ffn1_w_gate, m_ffn1_w_up=m_ffn1_w_up, m_ffn1_w_down=m_ffn1_w_down, m_norm_mix=m_norm_mix, m_w_in=m_w_in, m_b_gate=m_b_gate, m_rel_bias=m_rel_bias, m_sgu_ln_g=m_sgu_ln_g, m_sgu_ln_b=m_sgu_ln_b, m_sgu_w_s=m_sgu_w_s, m_sgu_b_s=m_sgu_b_s, m_w_branch_att=m_w_branch_att, m_w_branch_sgu=m_w_branch_sgu, m_w_out=m_w_out, m_norm_ffn2=m_norm_ffn2, m_ffn2_w_gate=m_ffn2_w_gate, m_ffn2_w_up=m_ffn2_w_up, m_ffn2_w_down=m_ffn2_w_down, m_norm_final=m_norm_final, v_norm_ffn1=v_norm_ffn1, v_ffn1_w_gate=v_ffn1_w_gate, v_ffn1_w_up=v_ffn1_w_up, v_ffn1_w_down=v_ffn1_w_down, v_norm_mix=v_norm_mix, v_w_in=v_w_in, v_b_gate=v_b_gate, v_rel_bias=v_rel_bias, v_sgu_ln_g=v_sgu_ln_g, v_sgu_ln_b=v_sgu_ln_b, v_sgu_w_s=v_sgu_w_s, v_sgu_b_s=v_sgu_b_s, v_w_branch_att=v_w_branch_att, v_w_branch_sgu=v_w_branch_sgu, v_w_out=v_w_out, v_norm_ffn2=v_norm_ffn2, v_ffn2_w_gate=v_ffn2_w_gate, v_ffn2_w_up=v_ffn2_w_up, v_ffn2_w_down=v_ffn2_w_down, v_norm_final=v_norm_final)
    weights = {n: given[n] for n in TWIN_WEIGHTS}
    shared = {n: given[n] for n in SHARED_INPUTS}
    per_example = {n: given[n] for n in ['x']}
    grad_fn = _jax.value_and_grad(_loss, argnums=(0, 1))

    def one_microbatch(ex, loss_target):
        ex = dict(ex)
        diff = ex.pop(TWIN_DIFF_INPUT)
        return grad_fn(weights, diff, {**shared, **ex}, loss_target)

    if N_MICROBATCH == 1:
        loss, (grad_w, grad_x) = one_microbatch(per_example, given["loss_target"])
    else:
        def body(carry, xs):
            loss_sum, grad_sum = carry
            l_k, (gw_k, gx_k) = one_microbatch(xs[0], xs[1])
            with _jax.named_scope("update"):
                return (loss_sum + l_k, _jax.tree.map(_jnp.add, grad_sum, gw_k)), gx_k

        init = (_jnp.zeros((), _jnp.float32), _jax.tree.map(_jnp.zeros_like, weights))
        (loss, grad_w), grad_x = _jax.lax.scan(body, init, (per_example, given["loss_target"]))
    with _jax.named_scope("update"):
        delta_w, new_m, new_v = {}, {}, {}
        for n in TWIN_WEIGHTS:
            delta_w[n], new_m[n], new_v[n] = _adamw(weights[n], grad_w[n], given["m_" + n], given["v_" + n])
    return (loss, grad_x, *[grad_w[n] for n in TWIN_WEIGHTS], *[delta_w[n] for n in TWIN_WEIGHTS],
            *[new_m[n] for n in TWIN_WEIGHTS], *[new_v[n] for n in TWIN_WEIGHTS])
```

```python
import functools

import numpy as np
import jax
import jax.numpy as jnp
from jax import lax
from jax.experimental import pallas as pl
from jax.experimental.pallas import tpu as pltpu

F32 = jnp.float32
BF16 = jnp.bfloat16

N_DEV = 8
D = 1024
F = 2816
D_ATT = 512
D_SGU = 512
D_IN = 4608
HEADS = 8
CHUNK = 64
N_LEFT = 8
REL_CLIP = 256
N_REL = 2 * REL_CLIP + 1
SGU_BLOCK = 128
EPS = 1e-6
NEG_INF = -1e30
QB = 256
KW = 3 * QB

R_FF, R_IN, R_BR, R_WO = F // N_DEV, D_IN // N_DEV, D // N_DEV, D // N_DEV
OFF_G1, OFF_U1, OFF_D1 = 0, R_FF, 2 * R_FF
OFF_G2, OFF_U2, OFF_D2 = 3 * R_FF, 4 * R_FF, 5 * R_FF
OFF_IN = 6 * R_FF
OFF_BR = OFF_IN + R_IN
OFF_WO = OFF_BR + R_BR
ROWS = OFF_WO + R_WO

FC = 256
TM = 512
TM_FFN_BWD = 256
VMEM_LIMIT = 56 * 1024 * 1024

ADAM_LR, ADAM_B1, ADAM_B2, ADAM_EPS, ADAM_WD, ADAM_STEP = 0.001, 0.9, 0.999, 1e-08, 0.01, 10

ROW_LOSS, ROW_N1, ROW_NM, ROW_N2, ROW_NF, ROW_BG, ROW_LN, ROW_BS, ROW_REL, ROW_WS = 0, 1, 2, 3, 4, 5, 7, 8, 9, 24
SMALL_ROWS = ROW_WS + 128

MESH = pl.DeviceIdType.MESH
ANY = pl.BlockSpec(memory_space=pl.ANY)


def _nt(a, b):
    return lax.dot_general(a, b, (((1,), (1,)), ((), ())), preferred_element_type=F32)


def _tn(a, b):
    return lax.dot_general(a, b, (((0,), (0,)), ((), ())), preferred_element_type=F32)


def _nn(a, b):
    return jnp.dot(a, b, preferred_element_type=F32)


def _cparams(sem=None):
    return pltpu.CompilerParams(dimension_semantics=sem, vmem_limit_bytes=VMEM_LIMIT)


def _load_rows(gw_ref, dst, off, rows, sems):
    copies = [pltpu.make_async_copy(gw_ref.at[k, pl.ds(off, rows), :], dst.at[pl.ds(k * rows, rows), :], sems.at[k])
              for k in range(N_DEV)]
    for cp in copies:
        cp.start()
    return copies


def _rms(xv):
    r = lax.rsqrt(jnp.mean(xv * xv, axis=-1, keepdims=True) + EPS)
    return xv * r, r


def _rms_bwd(dh, xn, r, gain):
    dxn = dh * gain
    dx = r * (dxn - xn * jnp.mean(dxn * xn, axis=-1, keepdims=True))
    return dx, jnp.sum(dh * xn, axis=0, keepdims=True)


def _gelu(x):
    t = jnp.tanh(0.7978845608028654 * (x + 0.044715 * x * x * x))
    return 0.5 * x * (1.0 + t), t


def _gelu_grad(x, t):
    return 0.5 * (1.0 + t) + 0.5 * x * (1.0 - t * t) * 0.7978845608028654 * (1.0 + 3.0 * 0.044715 * x * x)


def _all_gather_two_level(shard):
    r, c = shard.shape

    def body(x_ref, out_ref, send_sems, recv_sems, local_sem):
        x, y, cc = lax.axis_index("x"), lax.axis_index("y"), lax.axis_index("c")
        me, sibling = (x, y, cc), (x, y, 1 - cc)
        chips = [(1 - x, y), (x, 1 - y), (1 - x, 1 - y)]

        def slab(px, py, pc):
            return out_ref.at[4 * px + 2 * py + pc]

        def copy(k, block, to, src=None):
            return pltpu.make_async_remote_copy(
                src_ref=slab(*block) if src is None else src, dst_ref=slab(*block),
                send_sem=send_sems.at[k], recv_sem=recv_sems.at[k], device_id=to, device_id_type=MESH)

        mine = pltpu.make_async_copy(x_ref, slab(*me), local_sem)
        mine.start()
        first = [copy(0, me, sibling, src=x_ref)]
        first += [copy(1 + j, me, (*chip, cc), src=x_ref) for j, chip in enumerate(chips)]
        for cp in first:
            cp.start()
        passed = [copy(4 + j, (*chip, cc), sibling) for j, chip in enumerate(chips)]
        for j, chip in enumerate(chips):
            copy(1 + j, (*chip, cc), me).wait_recv()
            passed[j].start()
        copy(0, sibling, me).wait_recv()
        for j, chip in enumerate(chips):
            copy(4 + j, (*chip, 1 - cc), me).wait_recv()
        for cp in first + passed:
            cp.wait_send()
        mine.wait()

    return pl.pallas_call(
        body, name="all_gather_weights",
        out_shape=jax.ShapeDtypeStruct((N_DEV, r, c), shard.dtype),
        in_specs=[ANY], out_specs=ANY,
        scratch_shapes=[pltpu.SemaphoreType.DMA((7,)), pltpu.SemaphoreType.DMA((7,)), pltpu.SemaphoreType.DMA],
    )(shard)


def _exchange(src, scatter, name):
    r, c = src.shape[-2:]

    def body(src_ref, out_ref, send_sems, recv_sems, local_sem):
        x, y, cc = lax.axis_index("x"), lax.axis_index("y"), lax.axis_index("c")
        me = 4 * x + 2 * y + cc

        def block(p):
            return src_ref.at[p] if scatter else src_ref

        mine = pltpu.make_async_copy(block(me), out_ref.at[me], local_sem)
        mine.start()
        copies = []
        for k in range(1, N_DEV):
            px, py, pc = x ^ ((k >> 2) & 1), y ^ ((k >> 1) & 1), cc ^ (k & 1)
            cp = pltpu.make_async_remote_copy(
                src_ref=block(4 * px + 2 * py + pc), dst_ref=out_ref.at[me],
                send_sem=send_sems.at[k - 1], recv_sem=recv_sems.at[k - 1], device_id=(px, py, pc), device_id_type=MESH)
            cp.start()
            copies.append(cp)
        for cp in copies:
            cp.wait_recv()
        for cp in copies:
            cp.wait_send()
        mine.wait()

    return pl.pallas_call(
        body, name=name,
        out_shape=jax.ShapeDtypeStruct((N_DEV, r, c), src.dtype),
        in_specs=[ANY], out_specs=ANY,
        scratch_shapes=[pltpu.SemaphoreType.DMA((7,)), pltpu.SemaphoreType.DMA((7,)), pltpu.SemaphoreType.DMA],
    )(src)


def _ffn_fwd(x, gain, gw, off_g, off_u, off_d, name):
    t = x.shape[0]

    def body(x_ref, g_ref, gw_ref, o_ref, wg, wu, wd, sems):
        @pl.when(pl.program_id(0) == 0)
        def _():
            cps = _load_rows(gw_ref, wg, off_g, R_FF, sems.at[0]) + _load_rows(gw_ref, wu, off_u, R_FF, sems.at[1]) \
                + _load_rows(gw_ref, wd, off_d, R_FF, sems.at[2])
            for cp in cps:
                cp.wait()

        xv = x_ref[...]
        xn, _ = _rms(xv)
        h = (xn * g_ref[...]).astype(BF16)
        acc = jnp.zeros((TM, D), F32)
        for c in range(F // FC):
            rows = pl.ds(c * FC, FC)
            a = _nt(h, wg[rows, :])
            b = _nt(h, wu[rows, :])
            s = (a * jax.nn.sigmoid(a) * b).astype(BF16)
            acc = acc + _nn(s, wd[rows, :])
        o_ref[...] = xv + 0.5 * acc

    tile = pl.BlockSpec((TM, D), lambda i: (i, 0))
    return pl.pallas_call(
        body, name=name, grid=(t // TM,),
        in_specs=[tile, pl.BlockSpec((1, D), lambda i: (0, 0)), ANY], out_specs=tile,
        out_shape=jax.ShapeDtypeStruct((t, D), F32),
        scratch_shapes=[pltpu.VMEM((F, D), BF16)] * 3 + [pltpu.SemaphoreType.DMA((3, N_DEV))],
        compiler_params=_cparams(("arbitrary",)),
    )(x, gain, gw)


def _ffn_bwd(x, gain, dout, gw, off_g, off_u, off_d, name):
    t = x.shape[0]

    def body(x_ref, g_ref, do_ref, gw_ref, dx_ref, dg_ref, h_ref, dab_ref, s_ref, df_ref, wg, wu, wd, sems):
        @pl.when(pl.program_id(0) == 0)
        def _():
            cps = _load_rows(gw_ref, wg, off_g, R_FF, sems.at[0]) + _load_rows(gw_ref, wu, off_u, R_FF, sems.at[1]) \
                + _load_rows(gw_ref, wd, off_d, R_FF, sems.at[2])
            dg_ref[...] = jnp.zeros_like(dg_ref)
            for cp in cps:
                cp.wait()

        xv = x_ref[...]
        gain_v = g_ref[...]
        xn, r = _rms(xv)
        h = (xn * gain_v).astype(BF16)
        do = do_ref[...]
        df = (0.5 * do).astype(BF16)
        dh = jnp.zeros((TM_FFN_BWD, D), F32)
        for c in range(F // FC):
            rows = pl.ds(c * FC, FC)
            a = _nt(h, wg[rows, :])
            b = _nt(h, wu[rows, :])
            sg = jax.nn.sigmoid(a)
            sl = a * sg
            ds = _nt(df, wd[rows, :])
            da = (ds * b * (sg * (1.0 + a * (1.0 - sg)))).astype(BF16)
            db = (ds * sl).astype(BF16)
            dh = dh + _nn(da, wg[rows, :]) + _nn(db, wu[rows, :])
            dab_ref[:, c * FC:(c + 1) * FC] = da
            dab_ref[:, F + c * FC:F + (c + 1) * FC] = db
            s_ref[:, c * FC:(c + 1) * FC] = (sl * b).astype(BF16)
        dxn, dg = _rms_bwd(dh, xn, r, gain_v)
        dg_ref[...] += dg
        dx_ref[...] = do + dxn
        h_ref[...] = h
        df_ref[...] = df

    def tile(w):
        return pl.BlockSpec((TM_FFN_BWD, w), lambda i: (i, 0))

    row = pl.BlockSpec((1, D), lambda i: (0, 0))
    return pl.pallas_call(
        body, name=name, grid=(t // TM_FFN_BWD,),
        in_specs=[tile(D), row, tile(D), ANY],
        out_specs=[tile(D), row, tile(D), tile(2 * F), tile(F), tile(D)],
        out_shape=[jax.ShapeDtypeStruct((t, D), F32), jax.ShapeDtypeStruct((1, D), F32), jax.ShapeDtypeStruct((t, D), BF16),
                   jax.ShapeDtypeStruct((t, 2 * F), BF16), jax.ShapeDtypeStruct((t, F), BF16), jax.ShapeDtypeStruct((t, D), BF16)],
        scratch_shapes=[pltpu.VMEM((F, D), BF16)] * 3 + [pltpu.SemaphoreType.DMA((3, N_DEV))],
        compiler_params=_cparams(("arbitrary",)),
    )(x, gain, dout, gw)


def _weight_grad(a, b, name, tmm=256):
    t, m = a.shape
    n = b.shape[1]

    def body(a_ref, b_ref, o_ref):
        o_ref[...] = _tn(a_ref[...], b_ref[...]).astype(BF16)

    return pl.pallas_call(
        body, name=name, grid=(m // tmm,),
        in_specs=[pl.BlockSpec((t, tmm), lambda i: (0, i)), pl.BlockSpec((t, n), lambda i: (0, 0))],
        out_specs=pl.BlockSpec((tmm, n), lambda i: (i, 0)),
        out_shape=jax.ShapeDtypeStruct((m, n), BF16),
        compiler_params=_cparams(("arbitrary",)),
    )(a, b)


def _mix_proj_fwd(x, gain, b_gate, gw):
    t = x.shape[0]

    def body(x_ref, g_ref, bg_ref, gw_ref, q_ref, k_ref, v_ref, zs_ref, gt_ref, h_ref, win, sems):
        @pl.when(pl.program_id(0) == 0)
        def _():
            for cp in _load_rows(gw_ref, win, OFF_IN, R_IN, sems):
                cp.wait()

        xn, _ = _rms(x_ref[...])
        h = (xn * g_ref[...]).astype(BF16)
        h_ref[...] = h
        q_ref[...] = (_nt(h, win[0:512, :]) * 0.125).astype(BF16)
        k_ref[...] = _nt(h, win[512:1024, :]).astype(BF16)
        v_ref[...] = _nt(h, win[1024:1536, :]).astype(BF16)
        for c in range(2):
            zs_ref[:, c * 512:(c + 1) * 512] = _nt(h, win[1536 + c * 512:2048 + c * 512, :]).astype(BF16)
        for c in range(4):
            zg = _nt(h, win[2560 + c * 512:3072 + c * 512, :]) + bg_ref[:, c * 512:(c + 1) * 512]
            gt_ref[:, c * 512:(c + 1) * 512] = jax.nn.sigmoid(zg).astype(BF16)

    def tile(w):
        return pl.BlockSpec((TM, w), lambda i: (i, 0))

    return pl.pallas_call(
        body, name="mix_proj_fwd", grid=(t // TM,),
        in_specs=[tile(D), pl.BlockSpec((1, D), lambda i: (0, 0)), pl.BlockSpec((1, 2 * D), lambda i: (0, 0)), ANY],
        out_specs=[tile(D_ATT), tile(D_ATT), tile(D_ATT), tile(2 * D_SGU), tile(2 * D), tile(D)],
        out_shape=[jax.ShapeDtypeStruct((t, D_ATT), BF16)] * 3 + [jax.ShapeDtypeStruct((t, 2 * D_SGU), BF16),
                                                                   jax.ShapeDtypeStruct((t, 2 * D), BF16),
                                                                   jax.ShapeDtypeStruct((t, D), BF16)],
        scratch_shapes=[pltpu.VMEM((D_IN, D), BF16), pltpu.SemaphoreType.DMA((N_DEV,))],
        compiler_params=_cparams(("arbitrary",)),
    )(x, gain, b_gate, gw)


def _mix_proj_bwd(dz, x, gain, dres, gw):
    t = x.shape[0]

    def body(dz_ref, x_ref, g_ref, dr_ref, gw_ref, dx_ref, dg_ref, win, sems):
        @pl.when(pl.program_id(0) == 0)
        def _():
            cps = _load_rows(gw_ref, win, OFF_IN, R_IN, sems)
            dg_ref[...] = jnp.zeros_like(dg_ref)
            for cp in cps:
                cp.wait()

        dh = jnp.zeros((TM, D), F32)
        for c in range(D_IN // 512):
            dh = dh + _nn(dz_ref[:, c * 512:(c + 1) * 512], win[c * 512:(c + 1) * 512, :])
        xn, r = _rms(x_ref[...])
        dxn, dg = _rms_bwd(dh, xn, r, g_ref[...])
        dg_ref[...] += dg
        dx_ref[...] = dr_ref[...] + dxn

    def tile(w):
        return pl.BlockSpec((TM, w), lambda i: (i, 0))

    row = pl.BlockSpec((1, D), lambda i: (0, 0))
    return pl.pallas_call(
        body, name="mix_proj_bwd", grid=(t // TM,),
        in_specs=[tile(D_IN), tile(D), row, tile(D), ANY], out_specs=[tile(D), row],
        out_shape=[jax.ShapeDtypeStruct((t, D), F32), jax.ShapeDtypeStruct((1, D), F32)],
        scratch_shapes=[pltpu.VMEM((D_IN, D), BF16), pltpu.SemaphoreType.DMA((N_DEV,))],
        compiler_params=_cparams(("arbitrary",)),
    )(dz, x, gain, dres, gw)


def _band_bias(rel_bias):
    r = np.arange(QB)[:, None]
    col = np.arange(KW)[None, :]
    idx = np.clip(r - col + 2 * QB, -REL_CLIP, REL_CLIP) + REL_CLIP
    lag = col // CHUNK - r // CHUNK
    band = (lag >= 0) & (lag <= N_LEFT)
    return jnp.where(jnp.asarray(band)[None], rel_bias[:, jnp.asarray(idx)], NEG_INF).astype(F32)


def _att_specs():
    qspec = pl.BlockSpec((QB, D_ATT), lambda g: (g, 0))
    kspecs = [pl.BlockSpec((QB, D_ATT), lambda g: (jnp.maximum(g - 2, 0), 0)),
              pl.BlockSpec((QB, D_ATT), lambda g: (jnp.maximum(g - 1, 0), 0)), qspec]
    bspec = pl.BlockSpec((HEADS, QB, KW), lambda g: (0, 0, 0))
    return qspec, kspecs, bspec


def _att_probs(qm, kp, bias, valid):
    s = jnp.where(valid, _nt(qm, kp) + bias, NEG_INF)
    e = jnp.exp(s - jnp.max(s, axis=-1, keepdims=True))
    return e / jnp.sum(e, axis=-1, keepdims=True)


def _att_valid():
    g = pl.program_id(0)
    blk = lax.broadcasted_iota(jnp.int32, (QB, KW), 1) // QB
    return (blk + g) >= 2


def _att_fwd(q, k, v, bias):
    t = q.shape[0]

    def body(q_ref, k0, k1, k2, v0, v1, v2, b_ref, y_ref):
        valid = _att_valid()
        first = lax.broadcasted_iota(jnp.int32, (1, 128), 1) < 64
        for p in range(HEADS // 2):
            lanes = slice(p * 128, (p + 1) * 128)
            qp = q_ref[:, lanes]
            kp = jnp.concatenate([k0[:, lanes], k1[:, lanes], k2[:, lanes]], axis=0)
            vp = jnp.concatenate([v0[:, lanes], v1[:, lanes], v2[:, lanes]], axis=0)
            out = jnp.zeros((QB, 128), F32)
            for hh in range(2):
                mask = first if hh == 0 else jnp.logical_not(first)
                pr = _att_probs(jnp.where(mask, qp, 0), kp, b_ref[2 * p + hh], valid)
                out = out + _nn(pr.astype(BF16), jnp.where(mask, vp, 0))
            y_ref[:, lanes] = out.astype(BF16)

    qspec, kspecs, bspec = _att_specs()
    return pl.pallas_call(
        body, name="att_fwd", grid=(t // QB,),
        in_specs=[qspec] + kspecs + kspecs + [bspec], out_specs=qspec,
        out_shape=jax.ShapeDtypeStruct((t, D_ATT), BF16),
        compiler_params=_cparams(("arbitrary",)),
    )(q, k, k, k, v, v, v, bias)


def _att_bwd(q, k, v, bias, dy):
    t = q.shape[0]
    n_blocks = t // QB

    def body(q_ref, k0, k1, k2, v0, v1, v2, b_ref, dy_ref, dq_ref, dk_ref, dv_ref, db_ref, dk_acc, dv_acc):
        g = pl.program_id(0)

        @pl.when(g == 0)
        def _():
            db_ref[...] = jnp.zeros_like(db_ref)
            dk_acc[...] = jnp.zeros_like(dk_acc)
            dv_acc[...] = jnp.zeros_like(dv_acc)

        valid = _att_valid()
        first = lax.broadcasted_iota(jnp.int32, (1, 128), 1) < 64
        for p in range(HEADS // 2):
            lanes = slice(p * 128, (p + 1) * 128)
            qp = q_ref[:, lanes]
            dyp = dy_ref[:, lanes]
            kp = jnp.concatenate([k0[:, lanes], k1[:, lanes], k2[:, lanes]], axis=0)
            vp = jnp.concatenate([v0[:, lanes], v1[:, lanes], v2[:, lanes]], axis=0)
            dq = jnp.zeros((QB, 128), F32)
            dk = jnp.zeros((KW, 128), F32)
            dv = jnp.zeros((KW, 128), F32)
            for hh in range(2):
                mask = first if hh == 0 else jnp.logical_not(first)
                qm = jnp.where(mask, qp, 0)
                dym = jnp.where(mask, dyp, 0)
                pr = _att_probs(qm, kp, b_ref[2 * p + hh], valid)
                dp = _nt(dym, vp)
                ds = pr * (dp - jnp.sum(dp * pr, axis=-1, keepdims=True))
                db_ref[2 * p + hh] += ds
                dsb = ds.astype(BF16)
                dq = dq + _nn(dsb, jnp.where(mask, kp, 0))
                dk = dk + _tn(dsb, qm)
                dv = dv + _tn(pr.astype(BF16), dym)
            dq_ref[:, lanes] = (dq * 0.125).astype(BF16)
            for j in range(3):
                rows = pl.ds(pl.multiple_of(jnp.maximum(g - 2 + j, 0) * QB, QB), QB)
                dk_acc[rows, lanes] += dk[j * QB:(j + 1) * QB]
                dv_acc[rows, lanes] += dv[j * QB:(j + 1) * QB]

        @pl.when(g == n_blocks - 1)
        def _():
            dk_ref[...] = dk_acc[...].astype(BF16)
            dv_ref[...] = dv_acc[...].astype(BF16)

    qspec, kspecs, bspec = _att_specs()
    full = pl.BlockSpec((t, D_ATT), lambda g: (0, 0))
    return pl.pallas_call(
        body, name="att_bwd", grid=(n_blocks,),
        in_specs=[qspec] + kspecs + kspecs + [bspec, qspec], out_specs=[qspec, full, full, bspec],
        out_shape=[jax.ShapeDtypeStruct((t, D_ATT), BF16)] * 3 + [jax.ShapeDtypeStruct((HEADS, QB, KW), F32)],
        scratch_shapes=[pltpu.VMEM((t, D_ATT), F32)] * 2,
        compiler_params=_cparams(("arbitrary",)),
    )(q, k, k, k, v, v, v, bias, dy)


def _rel_bias_grad(dbias):
    width = KW + QB
    skew = jnp.pad(dbias[:, ::-1, :], ((0, 0), (0, 0), (0, QB))).reshape(HEADS, QB * width)[:, :QB * (width - 1)]
    skew = jnp.pad(skew.reshape(HEADS, QB, width - 1), ((0, 0), (0, 0), (0, 1)))

    def body(s_ref, cs_ref, tot_ref):
        cs = jnp.sum(s_ref[...], axis=1)
        cs_ref[...] = cs
        low = lax.broadcasted_iota(jnp.int32, (HEADS, width), 1) < (width - 2 * REL_CLIP)
        tot_ref[...] = jnp.broadcast_to(jnp.sum(jnp.where(low, cs, 0.0), axis=1, keepdims=True), (HEADS, 128))

    cs, tot = pl.pallas_call(
        body, name="rel_bias_grad",
        out_shape=[jax.ShapeDtypeStruct((HEADS, width), F32), jax.ShapeDtypeStruct((HEADS, 128), F32)],
    )(skew)
    return jnp.concatenate([cs[:, ::-1][:, :2 * REL_CLIP], tot[:, :1]], axis=1)


def _sgu_mask():
    pos = np.arange(SGU_BLOCK)
    return (pos[:, None] // CHUNK) >= (pos[None, :] // CHUNK)


def _group_stack(blk, first):
    return jnp.concatenate([jnp.where(first, blk, 0), jnp.where(first, 0, blk)], axis=0)


def _sgu_norm(zs_ref, lng, lnb):
    zs = zs_ref[...].astype(F32)
    ga, th = _gelu(zs)
    u, vs = ga[:, :D_SGU], ga[:, D_SGU:]
    mu = jnp.mean(vs, axis=-1, keepdims=True)
    cen = vs - mu
    rstd = lax.rsqrt(jnp.mean(cen * cen, axis=-1, keepdims=True) + EPS)
    xhat = cen * rstd
    return zs, th, u, xhat, rstd, xhat * lng + lnb


def _sgu_mix(vb, wm2_ref, bsx, s_ref):
    first = lax.broadcasted_iota(jnp.int32, (1, 128), 1) < 64
    for n in range(TM // SGU_BLOCK):
        for p in range(4):
            blk = vb[n * 128:(n + 1) * 128, p * 128:(p + 1) * 128]
            s_ref[n * 128:(n + 1) * 128, p * 128:(p + 1) * 128] = _nn(wm2_ref[p], _group_stack(blk, first)) + bsx[:, p * 128:(p + 1) * 128]


def _merge_fwd(x, zs, gt, y_att, lng, lnb, wm2, bsx, gw):
    t = x.shape[0]

    def body(x_ref, zs_ref, gt_ref, ya_ref, lng_ref, lnb_ref, wm2_ref, bsx_ref, gw_ref, xo_ref, ys_ref, mg_ref,
             wbr, wo, s_scr, sems):
        @pl.when(pl.program_id(0) == 0)
        def _():
            for cp in _load_rows(gw_ref, wbr, OFF_BR, R_BR, sems.at[0]) + _load_rows(gw_ref, wo, OFF_WO, R_WO, sems.at[1]):
                cp.wait()

        _, _, u, _, _, vsn = _sgu_norm(zs_ref, lng_ref[...], lnb_ref[...])
        _sgu_mix(vsn.astype(BF16), wm2_ref, bsx_ref[...], s_scr)
        ys = (u * s_scr[...]).astype(BF16)
        ys_ref[...] = ys
        pa = _nt(ya_ref[...], wbr[:, :D_ATT])
        ps = _nt(ys, wbr[:, D_ATT:])
        mg = (gt_ref[:, :D].astype(F32) * pa + gt_ref[:, D:].astype(F32) * ps).astype(BF16)
        mg_ref[...] = mg
        xo_ref[...] = x_ref[...] + _nn(mg, wo[...])

    def tile(w):
        return pl.BlockSpec((TM, w), lambda i: (i, 0))

    def const(shape):
        return pl.BlockSpec(shape, lambda i: (0,) * len(shape))

    return pl.pallas_call(
        body, name="merge_fwd", grid=(t // TM,),
        in_specs=[tile(D), tile(2 * D_SGU), tile(2 * D), tile(D_ATT), const((1, D_SGU)), const((1, D_SGU)),
                  const((4, 128, 256)), const((128, D_SGU)), ANY],
        out_specs=[tile(D), tile(D_SGU), tile(D)],
        out_shape=[jax.ShapeDtypeStruct((t, D), F32), jax.ShapeDtypeStruct((t, D_SGU), BF16), jax.ShapeDtypeStruct((t, D), BF16)],
        scratch_shapes=[pltpu.VMEM((D, D), BF16), pltpu.VMEM((D, D), BF16), pltpu.VMEM((TM, D_SGU), F32),
                        pltpu.SemaphoreType.DMA((2, N_DEV))],
        compiler_params=_cparams(("arbitrary",)),
    )(x, zs, gt, y_att, lng, lnb, wm2, bsx, gw)


def _merge_bwd(dx, gt, y_att, y_sgu, gw):
    t = dx.shape[0]

    def body(dx_ref, gt_ref, ya_ref, ys_ref, gw_ref, dzg_ref, dya_ref, dys_ref, dpp_ref, dxb_ref, dbg_ref, wbr, wo, sems):
        @pl.when(pl.program_id(0) == 0)
        def _():
            cps = _load_rows(gw_ref, wbr, OFF_BR, R_BR, sems.at[0]) + _load_rows(gw_ref, wo, OFF_WO, R_WO, sems.at[1])
            dbg_ref[...] = jnp.zeros_like(dbg_ref)
            for cp in cps:
                cp.wait()

        dxb = dx_ref[...].astype(BF16)
        dxb_ref[...] = dxb
        dm = _nt(dxb, wo[...])
        for half, y_ref, w in ((0, ya_ref, wbr.at[:, :D_ATT]), (1, ys_ref, wbr.at[:, D_ATT:])):
            cols = slice(half * D, (half + 1) * D)
            gate = gt_ref[:, cols].astype(F32)
            branch = _nt(y_ref[...], w[...])
            dzg = dm * branch * gate * (1.0 - gate)
            dbg_ref[:, cols] += jnp.sum(dzg, axis=0, keepdims=True)
            dzg_ref[:, cols] = dzg.astype(BF16)
            dbr = (dm * gate).astype(BF16)
            dpp_ref[:, cols] = dbr
            dy = _nn(dbr, w[...])
            if half == 0:
                dya_ref[...] = dy.astype(BF16)
            else:
                dys_ref[...] = dy

    def tile(w):
        return pl.BlockSpec((TM, w), lambda i: (i, 0))

    return pl.pallas_call(
        body, name="merge_bwd", grid=(t // TM,),
        in_specs=[tile(D), tile(2 * D), tile(D_ATT), tile(D_SGU), ANY],
        out_specs=[tile(2 * D), tile(D_ATT), tile(D_SGU), tile(2 * D), tile(D), pl.BlockSpec((1, 2 * D), lambda i: (0, 0))],
        out_shape=[jax.ShapeDtypeStruct((t, 2 * D), BF16), jax.ShapeDtypeStruct((t, D_ATT), BF16), jax.ShapeDtypeStruct((t, D_SGU), F32),
                   jax.ShapeDtypeStruct((t, 2 * D), BF16), jax.ShapeDtypeStruct((t, D), BF16), jax.ShapeDtypeStruct((1, 2 * D), F32)],
        scratch_shapes=[pltpu.VMEM((D, D), BF16), pltpu.VMEM((D, D), BF16), pltpu.SemaphoreType.DMA((2, N_DEV))],
        compiler_params=_cparams(("arbitrary",)),
    )(dx, gt, y_att, y_sgu, gw)


def _sgu_bwd(zs, dys, lng, lnb, wm2, wmt2, bsx):
    t = zs.shape[0]
    n_steps = t // TM

    def body(zs_ref, dys_ref, lng_ref, lnb_ref, wm2_ref, wmt2_ref, bsx_ref, dzs_ref, dw_ref, dbs_ref, dlg_ref, dlb_ref,
             s_scr, dv_scr, ds_acc):
        i = pl.program_id(0)

        @pl.when(i == 0)
        def _():
            dw_ref[...] = jnp.zeros_like(dw_ref)
            dlg_ref[...] = jnp.zeros_like(dlg_ref)
            dlb_ref[...] = jnp.zeros_like(dlb_ref)
            ds_acc[...] = jnp.zeros_like(ds_acc)

        lng = lng_ref[...]
        zs, th, u, xhat, rstd, vsn = _sgu_norm(zs_ref, lng, lnb_ref[...])
        vb = vsn.astype(BF16)
        _sgu_mix(vb, wm2_ref, bsx_ref[...], s_scr)
        dys = dys_ref[...]
        du = dys * s_scr[...]
        ds = dys * u
        dsb = ds.astype(BF16)
        first = lax.broadcasted_iota(jnp.int32, (1, 128), 1) < 64
        acc = jnp.zeros((SGU_BLOCK, D_SGU), F32)
        for n in range(TM // SGU_BLOCK):
            rows = slice(n * 128, (n + 1) * 128)
            acc = acc + ds[rows]
            for p in range(4):
                lanes = slice(p * 128, (p + 1) * 128)
                stack = _group_stack(dsb[rows, lanes], first)
                dv_scr[rows, lanes] = _nn(wmt2_ref[p], stack)
                dw_ref[p] += _nt(stack, vb[rows, lanes])
        ds_acc[...] += acc
        dvsn = dv_scr[...]
        dlg_ref[...] += jnp.sum(dvsn * xhat, axis=0, keepdims=True)
        dlb_ref[...] += jnp.sum(dvsn, axis=0, keepdims=True)
        dxh = dvsn * lng
        dvs = rstd * (dxh - jnp.mean(dxh, axis=-1, keepdims=True) - xhat * jnp.mean(dxh * xhat, axis=-1, keepdims=True))
        dga = jnp.concatenate([du, dvs], axis=1)
        dzs_ref[...] = (dga * _gelu_grad(zs, th)).astype(BF16)

        @pl.when(i == n_steps - 1)
        def _():
            r = lax.broadcasted_iota(jnp.int32, (256, 128), 0) % SGU_BLOCK
            c = lax.broadcasted_iota(jnp.int32, (256, 128), 1)
            keep = (r // CHUNK) >= (c // CHUNK)
            for p in range(4):
                dw_ref[p] = jnp.where(keep, dw_ref[p], 0.0)
            total = ds_acc[...]
            grp = lax.broadcasted_iota(jnp.int32, (SGU_BLOCK, D_SGU), 1) // 64
            lane = lax.broadcasted_iota(jnp.int32, (SGU_BLOCK, 128), 1)
            out = jnp.zeros((SGU_BLOCK, 128), F32)
            for gi in range(8):
                out = jnp.where(lane == gi, jnp.sum(jnp.where(grp == gi, total, 0.0), axis=1, keepdims=True), out)
            dbs_ref[...] = out

    def tile(w):
        return pl.BlockSpec((TM, w), lambda i: (i, 0))

    def const(shape):
        return pl.BlockSpec(shape, lambda i: (0,) * len(shape))

    return pl.pallas_call(
        body, name="sgu_bwd", grid=(n_steps,),
        in_specs=[tile(2 * D_SGU), tile(D_SGU), const((1, D_SGU)), const((1, D_SGU)), const((4, 128, 256)), const((4, 128, 256)),
                  const((128, D_SGU))],
        out_specs=[tile(2 * D_SGU), const((4, 256, 128)), const((128, 128)), const((1, D_SGU)), const((1, D_SGU))],
        out_shape=[jax.ShapeDtypeStruct((t, 2 * D_SGU), BF16), jax.ShapeDtypeStruct((4, 256, 128), F32),
                   jax.ShapeDtypeStruct((128, 128), F32), jax.ShapeDtypeStruct((1, D_SGU), F32), jax.ShapeDtypeStruct((1, D_SGU), F32)],
        scratch_shapes=[pltpu.VMEM((TM, D_SGU), F32), pltpu.VMEM((TM, D_SGU), F32), pltpu.VMEM((SGU_BLOCK, D_SGU), F32)],
        compiler_params=_cparams(("arbitrary",)),
    )(zs, dys, lng, lnb, wm2, wmt2, bsx)


def _loss_head(x, gain, target):
    t = x.shape[0]

    def body(x_ref, g_ref, t_ref, dx_ref, dg_ref, loss_ref):
        @pl.when(pl.program_id(0) == 0)
        def _():
            dg_ref[...] = jnp.zeros_like(dg_ref)
            loss_ref[...] = jnp.zeros_like(loss_ref)

        gain_v = g_ref[...]
        xn, r = _rms(x_ref[...])
        err = xn * gain_v - t_ref[...]
        loss_ref[...] += 0.5 * jnp.sum(jnp.mean(err * err, axis=-1, keepdims=True), axis=0, keepdims=True)
        dxn, dg = _rms_bwd(err * (1.0 / D), xn, r, gain_v)
        dg_ref[...] += dg
        dx_ref[...] = dxn

    tile = pl.BlockSpec((TM, D), lambda i: (i, 0))
    row = pl.BlockSpec((1, D), lambda i: (0, 0))
    return pl.pallas_call(
        body, name="loss_head", grid=(t // TM,),
        in_specs=[tile, row, tile], out_specs=[tile, row, pl.BlockSpec((1, 128), lambda i: (0, 0))],
        out_shape=[jax.ShapeDtypeStruct((t, D), F32), jax.ShapeDtypeStruct((1, D), F32), jax.ShapeDtypeStruct((1, 128), F32)],
        compiler_params=_cparams(("arbitrary",)),
    )(x, gain, target)


def _adamw(g, w, m, v):
    m = ADAM_B1 * m + (1.0 - ADAM_B1) * g
    v = ADAM_B2 * v + (1.0 - ADAM_B2) * (g * g)
    m_hat = m / (1.0 - ADAM_B1 ** ADAM_STEP)
    v_hat = v / (1.0 - ADAM_B2 ** ADAM_STEP)
    return -ADAM_LR * (m_hat / (jnp.sqrt(v_hat) + ADAM_EPS) + ADAM_WD * w), m, v


def _sum_adamw(parts, w, m, v, rows_per_step, name):
    rows = w.shape[0]

    def body(p_ref, w_ref, m_ref, v_ref, g_ref, d_ref, mo_ref, vo_ref):
        g = p_ref[0].astype(F32)
        for k in range(1, N_DEV):
            g = g + p_ref[k].astype(F32)
        g_ref[...] = g
        d_ref[...], mo_ref[...], vo_ref[...] = _adamw(g, w_ref[...], m_ref[...], v_ref[...])

    tile = pl.BlockSpec((rows_per_step, D), lambda i: (i, 0))
    return pl.pallas_call(
        body, name=name, grid=(rows // rows_per_step,),
        in_specs=[pl.BlockSpec((N_DEV, rows_per_step, D), lambda i: (0, i, 0)), tile, tile, tile], out_specs=[tile] * 4,
        out_shape=[jax.ShapeDtypeStruct((rows, D), F32)] * 4,
        compiler_params=_cparams(("arbitrary",)),
    )(parts, w, m, v)


_BIG = ("ffn1_w_gate", "ffn1_w_up", "ffn1_w_down", "ffn2_w_gate", "ffn2_w_up", "ffn2_w_down", "w_in", "w_branch", "w_out")


def _to_rows(p):
    return jnp.concatenate([
        p["ffn1_w_gate"][0].T, p["ffn1_w_up"][0].T, p["ffn1_w_down"][0],
        p["ffn2_w_gate"][0].T, p["ffn2_w_up"][0].T, p["ffn2_w_down"][0],
        p["w_in"][0].T, jnp.concatenate([p["w_branch_att"][0].T, p["w_branch_sgu"][0].T], axis=1), p["w_out"][0]], axis=0)


def _from_rows(slab):
    def part(off, rows):
        return slab[off:off + rows]
    br = part(OFF_BR, R_BR)
    return {
        "ffn1_w_gate": part(OFF_G1, R_FF).T[None], "ffn1_w_up": part(OFF_U1, R_FF).T[None], "ffn1_w_down": part(OFF_D1, R_FF)[None],
        "ffn2_w_gate": part(OFF_G2, R_FF).T[None], "ffn2_w_up": part(OFF_U2, R_FF).T[None], "ffn2_w_down": part(OFF_D2, R_FF)[None],
        "w_in": part(OFF_IN, R_IN).T[None], "w_branch_att": br[:, :D_ATT].T[None], "w_branch_sgu": br[:, D_ATT:].T[None],
        "w_out": part(OFF_WO, R_WO)[None]}


def _grad_slabs(gt1, gd1, gt2, gd2, g_in, g_ba, g_bs, g_wo):
    def split(a, rows):
        return a.reshape(N_DEV, rows, a.shape[-1])
    return jnp.concatenate([
        split(gt1[:F], R_FF), split(gt1[F:], R_FF), split(gd1, R_FF), split(gt2[:F], R_FF), split(gt2[F:], R_FF), split(gd2, R_FF),
        split(g_in, R_IN), jnp.concatenate([split(g_ba, R_BR), split(g_bs, R_BR)], axis=2), split(g_wo, R_WO)], axis=1)


def _pad_row(a):
    a = a.reshape(1, -1)
    return jnp.pad(a, ((0, 0), (0, D - a.shape[1])))


def _pack_small(loss, n1, nm, n2, nf, bg, lng, lnb, bs, rel, ws):
    rows = [_pad_row(loss), n1.reshape(1, D), nm.reshape(1, D), n2.reshape(1, D), nf.reshape(1, D), bg.reshape(2, D),
            jnp.concatenate([lng.reshape(1, D_SGU), lnb.reshape(1, D_SGU)], axis=1), bs.reshape(1, D),
            jnp.pad(rel.reshape(HEADS, N_REL), ((0, 0), (0, D - N_REL))), jnp.zeros((ROW_WS - ROW_REL - HEADS, D), F32),
            ws.reshape(128, D)]
    return jnp.concatenate(rows, axis=0).astype(F32)


def _unpack_small(s):
    return {"loss": s[ROW_LOSS, 0], "norm_ffn1": s[ROW_N1][None], "norm_mix": s[ROW_NM][None], "norm_ffn2": s[ROW_N2][None],
            "norm_final": s[ROW_NF], "b_gate": s[ROW_BG:ROW_BG + 2].reshape(1, 2 * D),
            "sgu_ln_g": s[ROW_LN, :D_SGU][None], "sgu_ln_b": s[ROW_LN, D_SGU:][None], "sgu_b_s": s[ROW_BS].reshape(1, 8, 128),
            "rel_bias": s[ROW_REL:ROW_REL + HEADS, :N_REL][None], "sgu_w_s": s[ROW_WS:].reshape(1, 8, 128, 128)}


_SMALL = ("norm_ffn1", "norm_mix", "norm_ffn2", "norm_final", "b_gate", "sgu_ln_g", "sgu_ln_b", "sgu_b_s", "rel_bias", "sgu_w_s")


def _pack_small_params(p, prefix=""):
    g = lambda n: p[prefix + n]
    return _pack_small(jnp.zeros((1,), F32), g("norm_ffn1"), g("norm_mix"), g("norm_ffn2"), g("norm_final"), g("b_gate"),
                       g("sgu_ln_g"), g("sgu_ln_b"), g("sgu_b_s"), g("rel_bias"), g("sgu_w_s"))


def _local_step(x, target, gw, p):
    n1, nm, n2 = p["norm_ffn1"], p["norm_mix"], p["norm_ffn2"]
    nf = p["norm_final"].reshape(1, D)
    lng, lnb = p["sgu_ln_g"], p["sgu_ln_b"]
    w_m = jnp.where(jnp.asarray(_sgu_mask())[None], p["sgu_w_s"][0], 0.0).astype(BF16)
    wm2 = jnp.concatenate([w_m[0::2], w_m[1::2]], axis=2)
    w_mt = w_m.transpose(0, 2, 1)
    wmt2 = jnp.concatenate([w_mt[0::2], w_mt[1::2]], axis=2)
    bsx = jnp.repeat(p["sgu_b_s"][0].T, 64, axis=1)
    bias = _band_bias(p["rel_bias"][0])

    x1 = _ffn_fwd(x, n1, gw, OFF_G1, OFF_U1, OFF_D1, "ffn1_fwd")
    q, k, v, zs, gt, h2 = _mix_proj_fwd(x1, nm, p["b_gate"], gw)
    y_att = _att_fwd(q, k, v, bias)
    x2, y_sgu, merged = _merge_fwd(x1, zs, gt, y_att, lng, lnb, wm2, bsx, gw)
    x3 = _ffn_fwd(x2, n2, gw, OFF_G2, OFF_U2, OFF_D2, "ffn2_fwd")
    dx3, d_nf, loss = _loss_head(x3, nf, target)

    dx2, d_n2, hb, dab, sb, dfb = _ffn_bwd(x2, n2, dx3, gw, OFF_G2, OFF_U2, OFF_D2, "ffn2_bwd")
    gt2 = _weight_grad(dab, hb, "ffn2_dw_gate_up")
    gd2 = _weight_grad(sb, dfb, "ffn2_dw_down")

    dzg, dya, dys, dpp, dxb, d_bg = _merge_bwd(dx2, gt, y_att, y_sgu, gw)
    g_wo = _weight_grad(merged, dxb, "dw_out")
    g_ba = _weight_grad(dpp[:, :D], y_att, "dw_branch_att")
    g_bs = _weight_grad(dpp[:, D:], y_sgu, "dw_branch_sgu")
    dzs, d_wm, d_bs, d_lng, d_lnb = _sgu_bwd(zs, dys, lng, lnb, wm2, wmt2, bsx)
    dq, dk, dv, d_bias = _att_bwd(q, k, v, bias, dya)
    d_rel = _rel_bias_grad(d_bias)
    dz = jnp.concatenate([dq, dk, dv, dzs, dzg], axis=1)
    g_in = _weight_grad(dz, h2, "dw_in")
    dx1, d_nm = _mix_proj_bwd(dz, x1, nm, dx2, gw)

    dx0, d_n1, hb, dab, sb, dfb = _ffn_bwd(x, n1, dx1, gw, OFF_G1, OFF_U1, OFF_D1, "ffn1_bwd")
    gt1 = _weight_grad(dab, hb, "ffn1_dw_gate_up")
    gd1 = _weight_grad(sb, dfb, "ffn1_dw_down")

    slabs = _grad_slabs(gt1, gd1, gt2, gd2, g_in, g_ba, g_bs, g_wo)
    small = _pack_small(loss[0, :1], d_n1, d_nm, d_n2, d_nf, d_bg, d_lng, d_lnb, d_bs[:, :8].T, d_rel,
                        d_wm.reshape(8, 128, 128))
    return dx0, slabs, small


_OUT_ORDER = ("norm_ffn1", "ffn1_w_gate", "ffn1_w_up", "ffn1_w_down", "norm_mix", "w_in", "b_gate", "rel_bias", "sgu_ln_g", "sgu_ln_b",
              "sgu_w_s", "sgu_b_s", "w_branch_att", "w_branch_sgu", "w_out", "norm_ffn2", "ffn2_w_gate", "ffn2_w_up", "ffn2_w_down",
              "norm_final")


def kernel(x, norm_ffn1, ffn1_w_gate, ffn1_w_up, ffn1_w_down, norm_mix, w_in, b_gate, rel_bias, sgu_ln_g, sgu_ln_b, sgu_w_s, sgu_b_s, w_branch_att, w_branch_sgu, w_out, norm_ffn2, ffn2_w_gate, ffn2_w_up, ffn2_w_down, norm_final, loss_target, m_norm_ffn1, m_ffn1_w_gate, m_ffn1_w_up, m_ffn1_w_down, m_norm_mix, m_w_in, m_b_gate, m_rel_bias, m_sgu_ln_g, m_sgu_ln_b, m_sgu_w_s, m_sgu_b_s, m_w_branch_att, m_w_branch_sgu, m_w_out, m_norm_ffn2, m_ffn2_w_gate, m_ffn2_w_up, m_ffn2_w_down, m_norm_final, v_norm_ffn1, v_ffn1_w_gate, v_ffn1_w_up, v_ffn1_w_down, v_norm_mix, v_w_in, v_b_gate, v_rel_bias, v_sgu_ln_g, v_sgu_ln_b, v_sgu_w_s, v_sgu_b_s, v_w_branch_att, v_w_branch_sgu, v_w_out, v_norm_ffn2, v_ffn2_w_gate, v_ffn2_w_up, v_ffn2_w_down, v_norm_final):
    args = dict(locals())
    w = {n: args[n] for n in _OUT_ORDER}
    mom = {pre + n: args[pre + n] for pre in ("m_", "v_") for n in _OUT_ORDER}

    gw = _all_gather_two_level(_to_rows(w).astype(BF16))
    dx, slabs, small = _local_step(x[0], loss_target[0], gw, w)

    parts = _exchange(slabs, True, "scatter_grads")
    big = _sum_adamw(parts, _to_rows(w), _to_rows({n: mom["m_" + n] for n in w}), _to_rows({n: mom["v_" + n] for n in w}),
                     128, "adamw_large")
    small_parts = _exchange(small, False, "gather_small_grads")
    sm = _sum_adamw(small_parts, _pack_small_params(w), _pack_small_params(mom, "m_"), _pack_small_params(mom, "v_"),
                    SMALL_ROWS, "adamw_small")

    outs = [{**_from_rows(b), **_unpack_small(s)} for b, s in zip(big, sm)]
    loss = outs[0]["loss"]
    return (loss, dx[None], *[o[n] for o in outs for n in _OUT_ORDER])
```

```python
import functools

import numpy as np
import jax
import jax.numpy as jnp
from jax import lax
from jax.experimental import pallas as pl
from jax.experimental.pallas import tpu as pltpu

F32 = jnp.float32
BF16 = jnp.bfloat16

N_DEV = 8
D = 1024
F = 2816
D_ATT = 512
D_SGU = 512
D_IN = 4608
HEADS = 8
CHUNK = 64
N_LEFT = 8
REL_CLIP = 256
N_REL = 2 * REL_CLIP + 1
SGU_BLOCK = 128
EPS = 1e-6
NEG_INF = -1e30
QB = 256
KW = 3 * QB

R_FF, R_IN, R_BR, R_WO = F // N_DEV, D_IN // N_DEV, D // N_DEV, D // N_DEV
OFF_G1, OFF_U1, OFF_D1 = 0, R_FF, 2 * R_FF
OFF_G2, OFF_U2, OFF_D2 = 3 * R_FF, 4 * R_FF, 5 * R_FF
OFF_IN = 6 * R_FF
OFF_BR = OFF_IN + R_IN
OFF_WO = OFF_BR + R_BR
ROWS = OFF_WO + R_WO

FC = 256
TM = 512
TM_FFN_BWD = 256
VMEM_LIMIT = 56 * 1024 * 1024

ADAM_LR, ADAM_B1, ADAM_B2, ADAM_EPS, ADAM_WD, ADAM_STEP = 0.001, 0.9, 0.999, 1e-08, 0.01, 10

ROW_LOSS, ROW_N1, ROW_NM, ROW_N2, ROW_NF, ROW_BG, ROW_LN, ROW_BS, ROW_REL, ROW_WS = 0, 1, 2, 3, 4, 5, 7, 8, 9, 24
SMALL_ROWS = ROW_WS + 128

MESH = pl.DeviceIdType.MESH
ANY = pl.BlockSpec(memory_space=pl.ANY)


def _nt(a, b):
    return lax.dot_general(a, b, (((1,), (1,)), ((), ())), preferred_element_type=F32)


def _tn(a, b):
    return lax.dot_general(a, b, (((0,), (0,)), ((), ())), preferred_element_type=F32)


def _nn(a, b):
    return jnp.dot(a, b, preferred_element_type=F32)


def _cparams(sem=None):
    return pltpu.CompilerParams(dimension_semantics=sem, vmem_limit_bytes=VMEM_LIMIT)


def _load_rows(gw_ref, dst, off, rows, sems):
    copies = [pltpu.make_async_copy(gw_ref.at[k, pl.ds(off, rows), :], dst.at[pl.ds(k * rows, rows), :], sems.at[k])
              for k in range(N_DEV)]
    for cp in copies:
        cp.start()
    return copies


def _rms(xv):
    r = lax.rsqrt(jnp.mean(xv * xv, axis=-1, keepdims=True) + EPS)
    return xv * r, r


def _rms_bwd(dh, xn, r, gain):
    dxn = dh * gain
    dx = r * (dxn - xn * jnp.mean(dxn * xn, axis=-1, keepdims=True))
    return dx, jnp.sum(dh * xn, axis=0, keepdims=True)


def _gelu(x):
    t = jnp.tanh(0.7978845608028654 * (x + 0.044715 * x * x * x))
    return 0.5 * x * (1.0 + t), t


def _gelu_grad(x, t):
    return 0.5 * (1.0 + t) + 0.5 * x * (1.0 - t * t) * 0.7978845608028654 * (1.0 + 3.0 * 0.044715 * x * x)


def _all_gather_two_level(shard):
    r, c = shard.shape

    def body(x_ref, out_ref, send_sems, recv_sems, local_sem):
        x, y, cc = lax.axis_index("x"), lax.axis_index("y"), lax.axis_index("c")
        me, sibling = (x, y, cc), (x, y, 1 - cc)
        chips = [(1 - x, y), (x, 1 - y), (1 - x, 1 - y)]

        def slab(px, py, pc):
            return out_ref.at[4 * px + 2 * py + pc]

        def copy(k, block, to, src=None):
            return pltpu.make_async_remote_copy(
                src_ref=slab(*block) if src is None else src, dst_ref=slab(*block),
                send_sem=send_sems.at[k], recv_sem=recv_sems.at[k], device_id=to, device_id_type=MESH)

        mine = pltpu.make_async_copy(x_ref, slab(*me), local_sem)
        mine.start()
        first = [copy(0, me, sibling, src=x_ref)]
        first += [copy(1 + j, me, (*chip, cc), src=x_ref) for j, chip in enumerate(chips)]
        for cp in first:
            cp.start()
        passed = [copy(4 + j, (*chip, cc), sibling) for j, chip in enumerate(chips)]
        for j, chip in enumerate(chips):
            copy(1 + j, (*chip, cc), me).wait_recv()
            passed[j].start()
        copy(0, sibling, me).wait_recv()
        for j, chip in enumerate(chips):
            copy(4 + j, (*chip, 1 - cc), me).wait_recv()
        for cp in first + passed:
            cp.wait_send()
        mine.wait()

    return pl.pallas_call(
        body, name="all_gather_weights",
        out_shape=jax.ShapeDtypeStruct((N_DEV, r, c), shard.dtype),
        in_specs=[ANY], out_specs=ANY,
        scratch_shapes=[pltpu.SemaphoreType.DMA((7,)), pltpu.SemaphoreType.DMA((7,)), pltpu.SemaphoreType.DMA],
    )(shard)


def _exchange(src, scatter, name):
    r, c = src.shape[-2:]

    def body(src_ref, out_ref, send_sems, recv_sems, local_sem):
        x, y, cc = lax.axis_index("x"), lax.axis_index("y"), lax.axis_index("c")
        me = 4 * x + 2 * y + cc

        def block(p):
            return src_ref.at[p] if scatter else src_ref

        mine = pltpu.make_async_copy(block(me), out_ref.at[me], local_sem)
        mine.start()
        copies = []
        for k in range(1, N_DEV):
            px, py, pc = x ^ ((k >> 2) & 1), y ^ ((k >> 1) & 1), cc ^ (k & 1)
            cp = pltpu.make_async_remote_copy(
                src_ref=block(4 * px + 2 * py + pc), dst_ref=out_ref.at[me],
                send_sem=send_sems.at[k - 1], recv_sem=recv_sems.at[k - 1], device_id=(px, py, pc), device_id_type=MESH)
            cp.start()
            copies.append(cp)
        for cp in copies:
            cp.wait_recv()
        for cp in copies:
            cp.wait_send()
        mine.wait()

    return pl.pallas_call(
        body, name=name,
        out_shape=jax.ShapeDtypeStruct((N_DEV, r, c), src.dtype),
        in_specs=[ANY], out_specs=ANY,
        scratch_shapes=[pltpu.SemaphoreType.DMA((7,)), pltpu.SemaphoreType.DMA((7,)), pltpu.SemaphoreType.DMA],
    )(src)


def _ffn_fwd(x, gain, gw, off_g, off_u, off_d, name):
    t = x.shape[0]

    def body(x_ref, g_ref, gw_ref, o_ref, wg, wu, wd, sems):
        @pl.when(pl.program_id(0) == 0)
        def _():
            cps = _load_rows(gw_ref, wg, off_g, R_FF, sems.at[0]) + _load_rows(gw_ref, wu, off_u, R_FF, sems.at[1]) \
                + _load_rows(gw_ref, wd, off_d, R_FF, sems.at[2])
            for cp in cps:
                cp.wait()

        xv = x_ref[...]
        xn, _ = _rms(xv)
        h = (xn * g_ref[...]).astype(BF16)
        acc = jnp.zeros((TM, D), F32)
        for c in range(F // FC):
            rows = pl.ds(c * FC, FC)
            a = _nt(h, wg[rows, :])
            b = _nt(h, wu[rows, :])
            s = (a * jax.nn.sigmoid(a) * b).astype(BF16)
            acc = acc + _nn(s, wd[rows, :])
        o_ref[...] = xv + 0.5 * acc

    tile = pl.BlockSpec((TM, D), lambda i: (i, 0))
    return pl.pallas_call(
        body, name=name, grid=(t // TM,),
        in_specs=[tile, pl.BlockSpec((1, D), lambda i: (0, 0)), ANY], out_specs=tile,
        out_shape=jax.ShapeDtypeStruct((t, D), F32),
        scratch_shapes=[pltpu.VMEM((F, D), BF16)] * 3 + [pltpu.SemaphoreType.DMA((3, N_DEV))],
        compiler_params=_cparams(("arbitrary",)),
    )(x, gain, gw)


def _ffn_bwd(x, gain, dout, gw, off_g, off_u, off_d, name):
    t = x.shape[0]

    def body(x_ref, g_ref, do_ref, gw_ref, dx_ref, dg_ref, h_ref, dab_ref, s_ref, df_ref, wg, wu, wd, sems):
        @pl.when(pl.program_id(0) == 0)
        def _():
            cps = _load_rows(gw_ref, wg, off_g, R_FF, sems.at[0]) + _load_rows(gw_ref, wu, off_u, R_FF, sems.at[1]) \
                + _load_rows(gw_ref, wd, off_d, R_FF, sems.at[2])
            dg_ref[...] = jnp.zeros_like(dg_ref)
            for cp in cps:
                cp.wait()

        xv = x_ref[...]
        gain_v = g_ref[...]
        xn, r = _rms(xv)
        h = (xn * gain_v).astype(BF16)
        do = do_ref[...]
        df = (0.5 * do).astype(BF16)
        dh = jnp.zeros((TM_FFN_BWD, D), F32)
        for c in range(F // FC):
            rows = pl.ds(c * FC, FC)
            a = _nt(h, wg[rows, :])
            b = _nt(h, wu[rows, :])
            sg = jax.nn.sigmoid(a)
            sl = a * sg
            ds = _nt(df, wd[rows, :])
            da = (ds * b * (sg * (1.0 + a * (1.0 - sg)))).astype(BF16)
            db = (ds * sl).astype(BF16)
            dh = dh + _nn(da, wg[rows, :]) + _nn(db, wu[rows, :])
            dab_ref[:, c * FC:(c + 1) * FC] = da
            dab_ref[:, F + c * FC:F + (c + 1) * FC] = db
            s_ref[:, c * FC:(c + 1) * FC] = (sl * b).astype(BF16)
        dxn, dg = _rms_bwd(dh, xn, r, gain_v)
        dg_ref[...] += dg
        dx_ref[...] = do + dxn
        h_ref[...] = h
        df_ref[...] = df

    def tile(w):
        return pl.BlockSpec((TM_FFN_BWD, w), lambda i: (i, 0))

    row = pl.BlockSpec((1, D), lambda i: (0, 0))
    return pl.pallas_call(
        body, name=name, grid=(t // TM_FFN_BWD,),
        in_specs=[tile(D), row, tile(D), ANY],
        out_specs=[tile(D), row, tile(D), tile(2 * F), tile(F), tile(D)],
        out_shape=[jax.ShapeDtypeStruct((t, D), F32), jax.ShapeDtypeStruct((1, D), F32), jax.ShapeDtypeStruct((t, D), BF16),
                   jax.ShapeDtypeStruct((t, 2 * F), BF16), jax.ShapeDtypeStruct((t, F), BF16), jax.ShapeDtypeStruct((t, D), BF16)],
        scratch_shapes=[pltpu.VMEM((F, D), BF16)] * 3 + [pltpu.SemaphoreType.DMA((3, N_DEV))],
        compiler_params=_cparams(("arbitrary",)),
    )(x, gain, dout, gw)


def _weight_grad(a, b, name, tmm=256):
    t, m = a.shape
    n = b.shape[1]

    def body(a_ref, b_ref, o_ref):
        o_ref[...] = _tn(a_ref[...], b_ref[...]).astype(BF16)

    return pl.pallas_call(
        body, name=name, grid=(m // tmm,),
        in_specs=[pl.BlockSpec((t, tmm), lambda i: (0, i)), pl.BlockSpec((t, n), lambda i: (0, 0))],
        out_specs=pl.BlockSpec((tmm, n), lambda i: (i, 0)),
        out_shape=jax.ShapeDtypeStruct((m, n), BF16),
        compiler_params=_cparams(("arbitrary",)),
    )(a, b)


def _mix_proj_fwd(x, gain, b_gate, gw):
    t = x.shape[0]

    def body(x_ref, g_ref, bg_ref, gw_ref, q_ref, k_ref, v_ref, zs_ref, gt_ref, h_ref, win, sems):
        @pl.when(pl.program_id(0) == 0)
        def _():
            for cp in _load_rows(gw_ref, win, OFF_IN, R_IN, sems):
                cp.wait()

        xn, _ = _rms(x_ref[...])
        h = (xn * g_ref[...]).astype(BF16)
        h_ref[...] = h
        q_ref[...] = (_nt(h, win[0:512, :]) * 0.125).astype(BF16)
        k_ref[...] = _nt(h, win[512:1024, :]).astype(BF16)
        v_ref[...] = _nt(h, win[1024:1536, :]).astype(BF16)
        for c in range(2):
            zs_ref[:, c * 512:(c + 1) * 512] = _nt(h, win[1536 + c * 512:2048 + c * 512, :]).astype(BF16)
        for c in range(4):
            zg = _nt(h, win[2560 + c * 512:3072 + c * 512, :]) + bg_ref[:, c * 512:(c + 1) * 512]
            gt_ref[:, c * 512:(c + 1) * 512] = jax.nn.sigmoid(zg).astype(BF16)

    def tile(w):
        return pl.BlockSpec((TM, w), lambda i: (i, 0))

    return pl.pallas_call(
        body, name="mix_proj_fwd", grid=(t // TM,),
        in_specs=[tile(D), pl.BlockSpec((1, D), lambda i: (0, 0)), pl.BlockSpec((1, 2 * D), lambda i: (0, 0)), ANY],
        out_specs=[tile(D_ATT), tile(D_ATT), tile(D_ATT), tile(2 * D_SGU), tile(2 * D), tile(D)],
        out_shape=[jax.ShapeDtypeStruct((t, D_ATT), BF16)] * 3 + [jax.ShapeDtypeStruct((t, 2 * D_SGU), BF16),
                                                                   jax.ShapeDtypeStruct((t, 2 * D), BF16),
                                                                   jax.ShapeDtypeStruct((t, D), BF16)],
        scratch_shapes=[pltpu.VMEM((D_IN, D), BF16), pltpu.SemaphoreType.DMA((N_DEV,))],
        compiler_params=_cparams(("arbitrary",)),
    )(x, gain, b_gate, gw)


def _mix_proj_bwd(dz, x, gain, dres, gw):
    t = x.shape[0]

    def body(dz_ref, x_ref, g_ref, dr_ref, gw_ref, dx_ref, dg_ref, win, sems):
        @pl.when(pl.program_id(0) == 0)
        def _():
            cps = _load_rows(gw_ref, win, OFF_IN, R_IN, sems)
            dg_ref[...] = jnp.zeros_like(dg_ref)
            for cp in cps:
                cp.wait()

        dh = jnp.zeros((TM, D), F32)
        for c in range(D_IN // 512):
            dh = dh + _nn(dz_ref[:, c * 512:(c + 1) * 512], win[c * 512:(c + 1) * 512, :])
        xn, r = _rms(x_ref[...])
        dxn, dg = _rms_bwd(dh, xn, r, g_ref[...])
        dg_ref[...] += dg
        dx_ref[...] = dr_ref[...] + dxn

    def tile(w):
        return pl.BlockSpec((TM, w), lambda i: (i, 0))

    row = pl.BlockSpec((1, D), lambda i: (0, 0))
    return pl.pallas_call(
        body, name="mix_proj_bwd", grid=(t // TM,),
        in_specs=[tile(D_IN), tile(D), row, tile(D), ANY], out_specs=[tile(D), row],
        out_shape=[jax.ShapeDtypeStruct((t, D), F32), jax.ShapeDtypeStruct((1, D), F32)],
        scratch_shapes=[pltpu.VMEM((D_IN, D), BF16), pltpu.SemaphoreType.DMA((N_DEV,))],
        compiler_params=_cparams(("arbitrary",)),
    )(dz, x, gain, dres, gw)


SKEW_W = KW + QB
N_CAP = 2 * QB - REL_CLIP + 1


def _band_bias(rel_bias):
    cap = rel_bias[:, 2 * REL_CLIP:]
    diag = jnp.concatenate([jnp.broadcast_to(cap, (HEADS, N_CAP)), rel_bias[:, 2 * REL_CLIP - 1::-1],
                            jnp.broadcast_to(cap, (HEADS, SKEW_W - N_CAP - 2 * REL_CLIP))], axis=1)

    def body(d_ref, o_ref):
        lag = lax.broadcasted_iota(jnp.int32, (QB, KW), 1) // CHUNK - lax.broadcasted_iota(jnp.int32, (QB, KW), 0) // CHUNK
        band = (lag >= 0) & (lag <= N_LEFT)
        for h in range(HEADS):
            rows = jnp.broadcast_to(d_ref[h:h + 1, :], (QB, SKEW_W))
            o_ref[h] = jnp.where(band, pltpu.roll(rows, 0, 1, stride=1, stride_axis=0)[:, :KW], NEG_INF)

    return pl.pallas_call(body, name="band_bias", out_shape=jax.ShapeDtypeStruct((HEADS, QB, KW), F32))(diag)


def _att_specs():
    qspec = pl.BlockSpec((QB, D_ATT), lambda g: (g, 0))
    kspecs = [pl.BlockSpec((QB, D_ATT), lambda g: (jnp.maximum(g - 2, 0), 0)),
              pl.BlockSpec((QB, D_ATT), lambda g: (jnp.maximum(g - 1, 0), 0)), qspec]
    bspec = pl.BlockSpec((HEADS, QB, KW), lambda g: (0, 0, 0))
    return qspec, kspecs, bspec


def _att_probs(qm, kp, bias, valid):
    s = jnp.where(valid, _nt(qm, kp) + bias, NEG_INF)
    e = jnp.exp(s - jnp.max(s, axis=-1, keepdims=True))
    return e / jnp.sum(e, axis=-1, keepdims=True)


def _att_valid():
    g = pl.program_id(0)
    blk = lax.broadcasted_iota(jnp.int32, (QB, KW), 1) // QB
    return (blk + g) >= 2


def _att_fwd(q, k, v, bias):
    t = q.shape[0]

    def body(q_ref, k0, k1, k2, v0, v1, v2, b_ref, y_ref):
        valid = _att_valid()
        first = lax.broadcasted_iota(jnp.int32, (1, 128), 1) < 64
        for p in range(HEADS // 2):
            lanes = slice(p * 128, (p + 1) * 128)
            qp = q_ref[:, lanes]
            kp = jnp.concatenate([k0[:, lanes], k1[:, lanes], k2[:, lanes]], axis=0)
            vp = jnp.concatenate([v0[:, lanes], v1[:, lanes], v2[:, lanes]], axis=0)
            out = jnp.zeros((QB, 128), F32)
            for hh in range(2):
                mask = first if hh == 0 else jnp.logical_not(first)
                pr = _att_probs(jnp.where(mask, qp, 0), kp, b_ref[2 * p + hh], valid)
                out = out + _nn(pr.astype(BF16), jnp.where(mask, vp, 0))
            y_ref[:, lanes] = out.astype(BF16)

    qspec, kspecs, bspec = _att_specs()
    return pl.pallas_call(
        body, name="att_fwd", grid=(t // QB,),
        in_specs=[qspec] + kspecs + kspecs + [bspec], out_specs=qspec,
        out_shape=jax.ShapeDtypeStruct((t, D_ATT), BF16),
        compiler_params=_cparams(("arbitrary",)),
    )(q, k, k, k, v, v, v, bias)


def _att_bwd(q, k, v, bias, dy):
    t = q.shape[0]
    n_blocks = t // QB

    def body(q_ref, k0, k1, k2, v0, v1, v2, b_ref, dy_ref, dq_ref, dk_ref, dv_ref, db_ref, dk_acc, dv_acc):
        g = pl.program_id(0)

        @pl.when(g == 0)
        def _():
            db_ref[...] = jnp.zeros_like(db_ref)
            dk_acc[...] = jnp.zeros_like(dk_acc)
            dv_acc[...] = jnp.zeros_like(dv_acc)

        valid = _att_valid()
        first = lax.broadcasted_iota(jnp.int32, (1, 128), 1) < 64
        for p in range(HEADS // 2):
            lanes = slice(p * 128, (p + 1) * 128)
            qp = q_ref[:, lanes]
            dyp = dy_ref[:, lanes]
            kp = jnp.concatenate([k0[:, lanes], k1[:, lanes], k2[:, lanes]], axis=0)
            vp = jnp.concatenate([v0[:, lanes], v1[:, lanes], v2[:, lanes]], axis=0)
            dq = jnp.zeros((QB, 128), F32)
            dk = jnp.zeros((KW, 128), F32)
            dv = jnp.zeros((KW, 128), F32)
            for hh in range(2):
                mask = first if hh == 0 else jnp.logical_not(first)
                qm = jnp.where(mask, qp, 0)
                dym = jnp.where(mask, dyp, 0)
                pr = _att_probs(qm, kp, b_ref[2 * p + hh], valid)
                dp = _nt(dym, vp)
                ds = pr * (dp - jnp.sum(dp * pr, axis=-1, keepdims=True))
                db_ref[2 * p + hh] += ds
                dsb = ds.astype(BF16)
                dq = dq + _nn(dsb, jnp.where(mask, kp, 0))
                dk = dk + _tn(dsb, qm)
                dv = dv + _tn(pr.astype(BF16), dym)
            dq_ref[:, lanes] = (dq * 0.125).astype(BF16)
            for j in range(3):
                rows = pl.ds(pl.multiple_of(jnp.maximum(g - 2 + j, 0) * QB, QB), QB)
                dk_acc[rows, lanes] += dk[j * QB:(j + 1) * QB]
                dv_acc[rows, lanes] += dv[j * QB:(j + 1) * QB]

        @pl.when(g == n_blocks - 1)
        def _():
            dk_ref[...] = dk_acc[...].astype(BF16)
            dv_ref[...] = dv_acc[...].astype(BF16)

    qspec, kspecs, bspec = _att_specs()
    full = pl.BlockSpec((t, D_ATT), lambda g: (0, 0))
    return pl.pallas_call(
        body, name="att_bwd", grid=(n_blocks,),
        in_specs=[qspec] + kspecs + kspecs + [bspec, qspec], out_specs=[qspec, full, full, bspec],
        out_shape=[jax.ShapeDtypeStruct((t, D_ATT), BF16)] * 3 + [jax.ShapeDtypeStruct((HEADS, QB, KW), F32)],
        scratch_shapes=[pltpu.VMEM((t, D_ATT), F32)] * 2,
        compiler_params=_cparams(("arbitrary",)),
    )(q, k, k, k, v, v, v, bias, dy)


def _rel_bias_grad(dbias):
    def body(db_ref, cs_ref, tot_ref):
        lane = lax.broadcasted_iota(jnp.int32, (1, SKEW_W), 1)
        capped = (lane < N_CAP) | (lane > KW)
        pad = jnp.zeros((8, QB), F32)
        for h in range(HEADS):
            def add_rows(a, z):
                rows = jnp.concatenate([db_ref[h, pl.ds(pl.multiple_of(8 * a, 8), 8), :], pad], axis=1)
                return z + pltpu.roll(rows, (SKEW_W - 8 * a) % SKEW_W, 1)

            z = lax.fori_loop(0, QB // 8, add_rows, jnp.zeros((8, SKEW_W), F32))
            cs = z[0:1, :]
            for b in range(1, 8):
                cs = cs + pltpu.roll(z[b:b + 1, :], SKEW_W - b, 1)
            cs_ref[h:h + 1, :] = cs
            tot_ref[h:h + 1, :] = jnp.broadcast_to(jnp.sum(jnp.where(capped, cs, 0.0), axis=1, keepdims=True), (1, 128))

    cs, tot = pl.pallas_call(
        body, name="rel_bias_grad",
        out_shape=[jax.ShapeDtypeStruct((HEADS, SKEW_W), F32), jax.ShapeDtypeStruct((HEADS, 128), F32)],
    )(dbias)
    return jnp.concatenate([cs[:, KW:N_CAP - 1:-1], tot[:, :1]], axis=1)


def _sgu_mask():
    pos = np.arange(SGU_BLOCK)
    return (pos[:, None] // CHUNK) >= (pos[None, :] // CHUNK)


def _group_stack(blk, first):
    return jnp.concatenate([jnp.where(first, blk, 0), jnp.where(first, 0, blk)], axis=0)


def _sgu_norm(zs_ref, lng, lnb):
    zs = zs_ref[...].astype(F32)
    ga, th = _gelu(zs)
    u, vs = ga[:, :D_SGU], ga[:, D_SGU:]
    mu = jnp.mean(vs, axis=-1, keepdims=True)
    cen = vs - mu
    rstd = lax.rsqrt(jnp.mean(cen * cen, axis=-1, keepdims=True) + EPS)
    xhat = cen * rstd
    return zs, th, u, xhat, rstd, xhat * lng + lnb


def _sgu_mix(vb, wm2_ref, bsx, s_ref):
    first = lax.broadcasted_iota(jnp.int32, (1, 128), 1) < 64
    for n in range(TM // SGU_BLOCK):
        for p in range(4):
            blk = vb[n * 128:(n + 1) * 128, p * 128:(p + 1) * 128]
            s_ref[n * 128:(n + 1) * 128, p * 128:(p + 1) * 128] = _nn(wm2_ref[p], _group_stack(blk, first)) + bsx[:, p * 128:(p + 1) * 128]


def _merge_fwd(x, zs, gt, y_att, lng, lnb, wm2, bsx, gw):
    t = x.shape[0]

    def body(x_ref, zs_ref, gt_ref, ya_ref, lng_ref, lnb_ref, wm2_ref, bsx_ref, gw_ref, xo_ref, ys_ref, mg_ref,
             wbr, wo, s_scr, sems):
        @pl.when(pl.program_id(0) == 0)
        def _():
            for cp in _load_rows(gw_ref, wbr, OFF_BR, R_BR, sems.at[0]) + _load_rows(gw_ref, wo, OFF_WO, R_WO, sems.at[1]):
                cp.wait()

        _, _, u, _, _, vsn = _sgu_norm(zs_ref, lng_ref[...], lnb_ref[...])
        _sgu_mix(vsn.astype(BF16), wm2_ref, bsx_ref[...], s_scr)
        ys = (u * s_scr[...]).astype(BF16)
        ys_ref[...] = ys
        pa = _nt(ya_ref[...], wbr[:, :D_ATT])
        ps = _nt(ys, wbr[:, D_ATT:])
        mg = (gt_ref[:, :D].astype(F32) * pa + gt_ref[:, D:].astype(F32) * ps).astype(BF16)
        mg_ref[...] = mg
        xo_ref[...] = x_ref[...] + _nn(mg, wo[...])

    def tile(w):
        return pl.BlockSpec((TM, w), lambda i: (i, 0))

    def const(shape):
        return pl.BlockSpec(shape, lambda i: (0,) * len(shape))

    return pl.pallas_call(
        body, name="merge_fwd", grid=(t // TM,),
        in_specs=[tile(D), tile(2 * D_SGU), tile(2 * D), tile(D_ATT), const((1, D_SGU)), const((1, D_SGU)),
                  const((4, 128, 256)), const((128, D_SGU)), ANY],
        out_specs=[tile(D), tile(D_SGU), tile(D)],
        out_shape=[jax.ShapeDtypeStruct((t, D), F32), jax.ShapeDtypeStruct((t, D_SGU), BF16), jax.ShapeDtypeStruct((t, D), BF16)],
        scratch_shapes=[pltpu.VMEM((D, D), BF16), pltpu.VMEM((D, D), BF16), pltpu.VMEM((TM, D_SGU), F32),
                        pltpu.SemaphoreType.DMA((2, N_DEV))],
        compiler_params=_cparams(("arbitrary",)),
    )(x, zs, gt, y_att, lng, lnb, wm2, bsx, gw)


def _merge_bwd(dx, gt, y_att, y_sgu, gw):
    t = dx.shape[0]

    def body(dx_ref, gt_ref, ya_ref, ys_ref, gw_ref, dzg_ref, dya_ref, dys_ref, dpp_ref, dxb_ref, dbg_ref, wbr, wo, sems):
        @pl.when(pl.program_id(0) == 0)
        def _():
            cps = _load_rows(gw_ref, wbr, OFF_BR, R_BR, sems.at[0]) + _load_rows(gw_ref, wo, OFF_WO, R_WO, sems.at[1])
            dbg_ref[...] = jnp.zeros_like(dbg_ref)
            for cp in cps:
                cp.wait()

        dxb = dx_ref[...].astype(BF16)
        dxb_ref[...] = dxb
        dm = _nt(dxb, wo[...])
        for half, y_ref, w in ((0, ya_ref, wbr.at[:, :D_ATT]), (1, ys_ref, wbr.at[:, D_ATT:])):
            cols = slice(half * D, (half + 1) * D)
            gate = gt_ref[:, cols].astype(F32)
            branch = _nt(y_ref[...], w[...])
            dzg = dm * branch * gate * (1.0 - gate)
            dbg_ref[:, cols] += jnp.sum(dzg, axis=0, keepdims=True)
            dzg_ref[:, cols] = dzg.astype(BF16)
            dbr = (dm * gate).astype(BF16)
            dpp_ref[:, cols] = dbr
            dy = _nn(dbr, w[...])
            if half == 0:
                dya_ref[...] = dy.astype(BF16)
            else:
                dys_ref[...] = dy

    def tile(w):
        return pl.BlockSpec((TM, w), lambda i: (i, 0))

    return pl.pallas_call(
        body, name="merge_bwd", grid=(t // TM,),
        in_specs=[tile(D), tile(2 * D), tile(D_ATT), tile(D_SGU), ANY],
        out_specs=[tile(2 * D), tile(D_ATT), tile(D_SGU), tile(2 * D), tile(D), pl.BlockSpec((1, 2 * D), lambda i: (0, 0))],
        out_shape=[jax.ShapeDtypeStruct((t, 2 * D), BF16), jax.ShapeDtypeStruct((t, D_ATT), BF16), jax.ShapeDtypeStruct((t, D_SGU), F32),
                   jax.ShapeDtypeStruct((t, 2 * D), BF16), jax.ShapeDtypeStruct((t, D), BF16), jax.ShapeDtypeStruct((1, 2 * D), F32)],
        scratch_shapes=[pltpu.VMEM((D, D), BF16), pltpu.VMEM((D, D), BF16), pltpu.SemaphoreType.DMA((2, N_DEV))],
        compiler_params=_cparams(("arbitrary",)),
    )(dx, gt, y_att, y_sgu, gw)


def _sgu_bwd(zs, dys, lng, lnb, wm2, wmt2, bsx):
    t = zs.shape[0]
    n_steps = t // TM

    def body(zs_ref, dys_ref, lng_ref, lnb_ref, wm2_ref, wmt2_ref, bsx_ref, dzs_ref, dw_ref, dbs_ref, dlg_ref, dlb_ref,
             s_scr, dv_scr, ds_acc):
        i = pl.program_id(0)

        @pl.when(i == 0)
        def _():
            dw_ref[...] = jnp.zeros_like(dw_ref)
            dlg_ref[...] = jnp.zeros_like(dlg_ref)
            dlb_ref[...] = jnp.zeros_like(dlb_ref)
            ds_acc[...] = jnp.zeros_like(ds_acc)

        lng = lng_ref[...]
        zs, th, u, xhat, rstd, vsn = _sgu_norm(zs_ref, lng, lnb_ref[...])
        vb = vsn.astype(BF16)
        _sgu_mix(vb, wm2_ref, bsx_ref[...], s_scr)
        dys = dys_ref[...]
        du = dys * s_scr[...]
        ds = dys * u
        dsb = ds.astype(BF16)
        first = lax.broadcasted_iota(jnp.int32, (1, 128), 1) < 64
        acc = jnp.zeros((SGU_BLOCK, D_SGU), F32)
        for n in range(TM // SGU_BLOCK):
            rows = slice(n * 128, (n + 1) * 128)
            acc = acc + ds[rows]
            for p in range(4):
                lanes = slice(p * 128, (p + 1) * 128)
                stack = _group_stack(dsb[rows, lanes], first)
                dv_scr[rows, lanes] = _nn(wmt2_ref[p], stack)
                dw_ref[p] += _nt(stack, vb[rows, lanes])
        ds_acc[...] += acc
        dvsn = dv_scr[...]
        dlg_ref[...] += jnp.sum(dvsn * xhat, axis=0, keepdims=True)
        dlb_ref[...] += jnp.sum(dvsn, axis=0, keepdims=True)
        dxh = dvsn * lng
        dvs = rstd * (dxh - jnp.mean(dxh, axis=-1, keepdims=True) - xhat * jnp.mean(dxh * xhat, axis=-1, keepdims=True))
        dga = jnp.concatenate([du, dvs], axis=1)
        dzs_ref[...] = (dga * _gelu_grad(zs, th)).astype(BF16)

        @pl.when(i == n_steps - 1)
        def _():
            r = lax.broadcasted_iota(jnp.int32, (256, 128), 0) % SGU_BLOCK
            c = lax.broadcasted_iota(jnp.int32, (256, 128), 1)
            keep = (r // CHUNK) >= (c // CHUNK)
            for p in range(4):
                dw_ref[p] = jnp.where(keep, dw_ref[p], 0.0)
            total = ds_acc[...]
            grp = lax.broadcasted_iota(jnp.int32, (SGU_BLOCK, D_SGU), 1) // 64
            lane = lax.broadcasted_iota(jnp.int32, (SGU_BLOCK, 128), 1)
            out = jnp.zeros((SGU_BLOCK, 128), F32)
            for gi in range(8):
                out = jnp.where(lane == gi, jnp.sum(jnp.where(grp == gi, total, 0.0), axis=1, keepdims=True), out)
            dbs_ref[...] = out

    def tile(w):
        return pl.BlockSpec((TM, w), lambda i: (i, 0))

    def const(shape):
        return pl.BlockSpec(shape, lambda i: (0,) * len(shape))

    return pl.pallas_call(
        body, name="sgu_bwd", grid=(n_steps,),
        in_specs=[tile(2 * D_SGU), tile(D_SGU), const((1, D_SGU)), const((1, D_SGU)), const((4, 128, 256)), const((4, 128, 256)),
                  const((128, D_SGU))],
        out_specs=[tile(2 * D_SGU), const((4, 256, 128)), const((128, 128)), const((1, D_SGU)), const((1, D_SGU))],
        out_shape=[jax.ShapeDtypeStruct((t, 2 * D_SGU), BF16), jax.ShapeDtypeStruct((4, 256, 128), F32),
                   jax.ShapeDtypeStruct((128, 128), F32), jax.ShapeDtypeStruct((1, D_SGU), F32), jax.ShapeDtypeStruct((1, D_SGU), F32)],
        scratch_shapes=[pltpu.VMEM((TM, D_SGU), F32), pltpu.VMEM((TM, D_SGU), F32), pltpu.VMEM((SGU_BLOCK, D_SGU), F32)],
        compiler_params=_cparams(("arbitrary",)),
    )(zs, dys, lng, lnb, wm2, wmt2, bsx)


def _loss_head(x, gain, target):
    t = x.shape[0]

    def body(x_ref, g_ref, t_ref, dx_ref, dg_ref, loss_ref):
        @pl.when(pl.program_id(0) == 0)
        def _():
            dg_ref[...] = jnp.zeros_like(dg_ref)
            loss_ref[...] = jnp.zeros_like(loss_ref)

        gain_v = g_ref[...]
        xn, r = _rms(x_ref[...])
        err = xn * gain_v - t_ref[...]
        loss_ref[...] += 0.5 * jnp.sum(jnp.mean(err * err, axis=-1, keepdims=True), axis=0, keepdims=True)
        dxn, dg = _rms_bwd(err * (1.0 / D), xn, r, gain_v)
        dg_ref[...] += dg
        dx_ref[...] = dxn

    tile = pl.BlockSpec((TM, D), lambda i: (i, 0))
    row = pl.BlockSpec((1, D), lambda i: (0, 0))
    return pl.pallas_call(
        body, name="loss_head", grid=(t // TM,),
        in_specs=[tile, row, tile], out_specs=[tile, row, pl.BlockSpec((1, 128), lambda i: (0, 0))],
        out_shape=[jax.ShapeDtypeStruct((t, D), F32), jax.ShapeDtypeStruct((1, D), F32), jax.ShapeDtypeStruct((1, 128), F32)],
        compiler_params=_cparams(("arbitrary",)),
    )(x, gain, target)


def _adamw(g, w, m, v):
    m = ADAM_B1 * m + (1.0 - ADAM_B1) * g
    v = ADAM_B2 * v + (1.0 - ADAM_B2) * (g * g)
    m_hat = m / (1.0 - ADAM_B1 ** ADAM_STEP)
    v_hat = v / (1.0 - ADAM_B2 ** ADAM_STEP)
    return -ADAM_LR * (m_hat / (jnp.sqrt(v_hat) + ADAM_EPS) + ADAM_WD * w), m, v


def _sum_adamw(parts, w, m, v, rows_per_step, name):
    rows = w.shape[0]

    def body(p_ref, w_ref, m_ref, v_ref, g_ref, d_ref, mo_ref, vo_ref):
        g = p_ref[0].astype(F32)
        for k in range(1, N_DEV):
            g = g + p_ref[k].astype(F32)
        g_ref[...] = g
        d_ref[...], mo_ref[...], vo_ref[...] = _adamw(g, w_ref[...], m_ref[...], v_ref[...])

    tile = pl.BlockSpec((rows_per_step, D), lambda i: (i, 0))
    return pl.pallas_call(
        body, name=name, grid=(rows // rows_per_step,),
        in_specs=[pl.BlockSpec((N_DEV, rows_per_step, D), lambda i: (0, i, 0)), tile, tile, tile], out_specs=[tile] * 4,
        out_shape=[jax.ShapeDtypeStruct((rows, D), F32)] * 4,
        compiler_params=_cparams(("arbitrary",)),
    )(parts, w, m, v)


_BIG = ("ffn1_w_gate", "ffn1_w_up", "ffn1_w_down", "ffn2_w_gate", "ffn2_w_up", "ffn2_w_down", "w_in", "w_branch", "w_out")


def _to_rows(p):
    return jnp.concatenate([
        p["ffn1_w_gate"][0].T, p["ffn1_w_up"][0].T, p["ffn1_w_down"][0],
        p["ffn2_w_gate"][0].T, p["ffn2_w_up"][0].T, p["ffn2_w_down"][0],
        p["w_in"][0].T, jnp.concatenate([p["w_branch_att"][0].T, p["w_branch_sgu"][0].T], axis=1), p["w_out"][0]], axis=0)


def _from_rows(slab):
    def part(off, rows):
        return slab[off:off + rows]
    br = part(OFF_BR, R_BR)
    return {
        "ffn1_w_gate": part(OFF_G1, R_FF).T[None], "ffn1_w_up": part(OFF_U1, R_FF).T[None], "ffn1_w_down": part(OFF_D1, R_FF)[None],
        "ffn2_w_gate": part(OFF_G2, R_FF).T[None], "ffn2_w_up": part(OFF_U2, R_FF).T[None], "ffn2_w_down": part(OFF_D2, R_FF)[None],
        "w_in": part(OFF_IN, R_IN).T[None], "w_branch_att": br[:, :D_ATT].T[None], "w_branch_sgu": br[:, D_ATT:].T[None],
        "w_out": part(OFF_WO, R_WO)[None]}


def _grad_slabs(gt1, gd1, gt2, gd2, g_in, g_ba, g_bs, g_wo):
    def split(a, rows):
        return a.reshape(N_DEV, rows, a.shape[-1])
    return jnp.concatenate([
        split(gt1[:F], R_FF), split(gt1[F:], R_FF), split(gd1, R_FF), split(gt2[:F], R_FF), split(gt2[F:], R_FF), split(gd2, R_FF),
        split(g_in, R_IN), jnp.concatenate([split(g_ba, R_BR), split(g_bs, R_BR)], axis=2), split(g_wo, R_WO)], axis=1)


def _pad_row(a):
    a = a.reshape(1, -1)
    return jnp.pad(a, ((0, 0), (0, D - a.shape[1])))


def _pack_small(loss, n1, nm, n2, nf, bg, lng, lnb, bs, rel, ws):
    rows = [_pad_row(loss), n1.reshape(1, D), nm.reshape(1, D), n2.reshape(1, D), nf.reshape(1, D), bg.reshape(2, D),
            jnp.concatenate([lng.reshape(1, D_SGU), lnb.reshape(1, D_SGU)], axis=1), bs.reshape(1, D),
            jnp.pad(rel.reshape(HEADS, N_REL), ((0, 0), (0, D - N_REL))), jnp.zeros((ROW_WS - ROW_REL - HEADS, D), F32),
            ws.reshape(128, D)]
    return jnp.concatenate(rows, axis=0).astype(F32)


def _unpack_small(s):
    return {"loss": s[ROW_LOSS, 0], "norm_ffn1": s[ROW_N1][None], "norm_mix": s[ROW_NM][None], "norm_ffn2": s[ROW_N2][None],
            "norm_final": s[ROW_NF], "b_gate": s[ROW_BG:ROW_BG + 2].reshape(1, 2 * D),
            "sgu_ln_g": s[ROW_LN, :D_SGU][None], "sgu_ln_b": s[ROW_LN, D_SGU:][None], "sgu_b_s": s[ROW_BS].reshape(1, 8, 128),
            "rel_bias": s[ROW_REL:ROW_REL + HEADS, :N_REL][None], "sgu_w_s": s[ROW_WS:].reshape(1, 8, 128, 128)}


_SMALL = ("norm_ffn1", "norm_mix", "norm_ffn2", "norm_final", "b_gate", "sgu_ln_g", "sgu_ln_b", "sgu_b_s", "rel_bias", "sgu_w_s")


def _pack_small_params(p, prefix=""):
    g = lambda n: p[prefix + n]
    return _pack_small(jnp.zeros((1,), F32), g("norm_ffn1"), g("norm_mix"), g("norm_ffn2"), g("norm_final"), g("b_gate"),
                       g("sgu_ln_g"), g("sgu_ln_b"), g("sgu_b_s"), g("rel_bias"), g("sgu_w_s"))


def _local_step(x, target, gw, p):
    n1, nm, n2 = p["norm_ffn1"], p["norm_mix"], p["norm_ffn2"]
    nf = p["norm_final"].reshape(1, D)
    lng, lnb = p["sgu_ln_g"], p["sgu_ln_b"]
    w_m = jnp.where(jnp.asarray(_sgu_mask())[None], p["sgu_w_s"][0], 0.0).astype(BF16)
    wm2 = jnp.concatenate([w_m[0::2], w_m[1::2]], axis=2)
    w_mt = w_m.transpose(0, 2, 1)
    wmt2 = jnp.concatenate([w_mt[0::2], w_mt[1::2]], axis=2)
    bsx = jnp.repeat(p["sgu_b_s"][0].T, 64, axis=1)
    bias = _band_bias(p["rel_bias"][0])

    x1 = _ffn_fwd(x, n1, gw, OFF_G1, OFF_U1, OFF_D1, "ffn1_fwd")
    q, k, v, zs, gt, h2 = _mix_proj_fwd(x1, nm, p["b_gate"], gw)
    y_att = _att_fwd(q, k, v, bias)
    x2, y_sgu, merged = _merge_fwd(x1, zs, gt, y_att, lng, lnb, wm2, bsx, gw)
    x3 = _ffn_fwd(x2, n2, gw, OFF_G2, OFF_U2, OFF_D2, "ffn2_fwd")
    dx3, d_nf, loss = _loss_head(x3, nf, target)

    dx2, d_n2, hb, dab, sb, dfb = _ffn_bwd(x2, n2, dx3, gw, OFF_G2, OFF_U2, OFF_D2, "ffn2_bwd")
    gt2 = _weight_grad(dab, hb, "ffn2_dw_gate_up")
    gd2 = _weight_grad(sb, dfb, "ffn2_dw_down")

    dzg, dya, dys, dpp, dxb, d_bg = _merge_bwd(dx2, gt, y_att, y_sgu, gw)
    g_wo = _weight_grad(merged, dxb, "dw_out")
    g_ba = _weight_grad(dpp[:, :D], y_att, "dw_branch_att")
    g_bs = _weight_grad(dpp[:, D:], y_sgu, "dw_branch_sgu")
    dzs, d_wm, d_bs, d_lng, d_lnb = _sgu_bwd(zs, dys, lng, lnb, wm2, wmt2, bsx)
    dq, dk, dv, d_bias = _att_bwd(q, k, v, bias, dya)
    d_rel = _rel_bias_grad(d_bias)
    dz = jnp.concatenate([dq, dk, dv, dzs, dzg], axis=1)
    g_in = _weight_grad(dz, h2, "dw_in")
    dx1, d_nm = _mix_proj_bwd(dz, x1, nm, dx2, gw)

    dx0, d_n1, hb, dab, sb, dfb = _ffn_bwd(x, n1, dx1, gw, OFF_G1, OFF_U1, OFF_D1, "ffn1_bwd")
    gt1 = _weight_grad(dab, hb, "ffn1_dw_gate_up")
    gd1 = _weight_grad(sb, dfb, "ffn1_dw_down")

    slabs = _grad_slabs(gt1, gd1, gt2, gd2, g_in, g_ba, g_bs, g_wo)
    small = _pack_small(loss[0, :1], d_n1, d_nm, d_n2, d_nf, d_bg, d_lng, d_lnb, d_bs[:, :8].T, d_rel,
                        d_wm.reshape(8, 128, 128))
    return dx0, slabs, small


_OUT_ORDER = ("norm_ffn1", "ffn1_w_gate", "ffn1_w_up", "ffn1_w_down", "norm_mix", "w_in", "b_gate", "rel_bias", "sgu_ln_g", "sgu_ln_b",
              "sgu_w_s", "sgu_b_s", "w_branch_att", "w_branch_sgu", "w_out", "norm_ffn2", "ffn2_w_gate", "ffn2_w_up", "ffn2_w_down",
              "norm_final")


def kernel(x, norm_ffn1, ffn1_w_gate, ffn1_w_up, ffn1_w_down, norm_mix, w_in, b_gate, rel_bias, sgu_ln_g, sgu_ln_b, sgu_w_s, sgu_b_s, w_branch_att, w_branch_sgu, w_out, norm_ffn2, ffn2_w_gate, ffn2_w_up, ffn2_w_down, norm_final, loss_target, m_norm_ffn1, m_ffn1_w_gate, m_ffn1_w_up, m_ffn1_w_down, m_norm_mix, m_w_in, m_b_gate, m_rel_bias, m_sgu_ln_g, m_sgu_ln_b, m_sgu_w_s, m_sgu_b_s, m_w_branch_att, m_w_branch_sgu, m_w_out, m_norm_ffn2, m_ffn2_w_gate, m_ffn2_w_up, m_ffn2_w_down, m_norm_final, v_norm_ffn1, v_ffn1_w_gate, v_ffn1_w_up, v_ffn1_w_down, v_norm_mix, v_w_in, v_b_gate, v_rel_bias, v_sgu_ln_g, v_sgu_ln_b, v_sgu_w_s, v_sgu_b_s, v_w_branch_att, v_w_branch_sgu, v_w_out, v_norm_ffn2, v_ffn2_w_gate, v_ffn2_w_up, v_ffn2_w_down, v_norm_final):
    args = dict(locals())
    w = {n: args[n] for n in _OUT_ORDER}
    mom = {pre + n: args[pre + n] for pre in ("m_", "v_") for n in _OUT_ORDER}

    gw = _all_gather_two_level(_to_rows(w).astype(BF16))
    dx, slabs, small = _local_step(x[0], loss_target[0], gw, w)

    parts = _exchange(slabs, True, "scatter_grads")
    big = _sum_adamw(parts, _to_rows(w), _to_rows({n: mom["m_" + n] for n in w}), _to_rows({n: mom["v_" + n] for n in w}),
                     128, "adamw_large")
    small_parts = _exchange(small, False, "gather_small_grads")
    sm = _sum_adamw(small_parts, _pack_small_params(w), _pack_small_params(mom, "m_"), _pack_small_params(mom, "v_"),
                    SMALL_ROWS, "adamw_small")

    outs = [{**_from_rows(b), **_unpack_small(s)} for b, s in zip(big, sm)]
    loss = outs[0]["loss"]
    return (loss, dx[None], *[o[n] for o in outs for n in _OUT_ORDER])
```

```python
import functools

import numpy as np
import jax
import jax.numpy as jnp
from jax import lax
from jax.experimental import pallas as pl
from jax.experimental.pallas import tpu as pltpu

F32 = jnp.float32
BF16 = jnp.bfloat16

N_DEV = 8
D = 1024
F = 2816
D_ATT = 512
D_SGU = 512
D_IN = 4608
HEADS = 8
CHUNK = 64
N_LEFT = 8
REL_CLIP = 256
N_REL = 2 * REL_CLIP + 1
SGU_BLOCK = 128
EPS = 1e-6
NEG_INF = -1e30
QB = 256
KW = 3 * QB

R_FF, R_IN, R_BR, R_WO = F // N_DEV, D_IN // N_DEV, D // N_DEV, D // N_DEV
OFF_IN, OFF_BR, OFF_WO = 0, R_IN, R_IN + R_BR
FFN_ROWS = 3 * R_FF
MIX_ROWS = R_IN + R_BR + R_WO

FC = 256
TM = 512
TM_FFN_BWD = 256
VMEM_LIMIT = 56 * 1024 * 1024

ADAM_LR, ADAM_B1, ADAM_B2, ADAM_EPS, ADAM_WD, ADAM_STEP = 0.001, 0.9, 0.999, 1e-08, 0.01, 10

ROW_LOSS, ROW_N1, ROW_NM, ROW_N2, ROW_NF, ROW_BG, ROW_LN, ROW_BS, ROW_REL, ROW_WS = 0, 1, 2, 3, 4, 5, 7, 8, 9, 24
SMALL_ROWS = ROW_WS + 128

MESH = pl.DeviceIdType.MESH
ANY = pl.BlockSpec(memory_space=pl.ANY)


def _nt(a, b):
    return lax.dot_general(a, b, (((1,), (1,)), ((), ())), preferred_element_type=F32)


def _tn(a, b):
    return lax.dot_general(a, b, (((0,), (0,)), ((), ())), preferred_element_type=F32)


def _nn(a, b):
    return jnp.dot(a, b, preferred_element_type=F32)


def _cparams(sem=None):
    return pltpu.CompilerParams(dimension_semantics=sem, vmem_limit_bytes=VMEM_LIMIT)


def _load_rows(gw_ref, dst, off, rows, sems):
    copies = [pltpu.make_async_copy(gw_ref.at[k, pl.ds(off, rows), :], dst.at[pl.ds(k * rows, rows), :], sems.at[k])
              for k in range(N_DEV)]
    for cp in copies:
        cp.start()
    return copies


def _rms(xv):
    r = lax.rsqrt(jnp.mean(xv * xv, axis=-1, keepdims=True) + EPS)
    return xv * r, r


def _rms_bwd(dh, xn, r, gain):
    dxn = dh * gain
    dx = r * (dxn - xn * jnp.mean(dxn * xn, axis=-1, keepdims=True))
    return dx, jnp.sum(dh * xn, axis=0, keepdims=True)


def _gelu(x):
    t = jnp.tanh(0.7978845608028654 * (x + 0.044715 * x * x * x))
    return 0.5 * x * (1.0 + t), t


def _gelu_grad(x, t):
    return 0.5 * (1.0 + t) + 0.5 * x * (1.0 - t * t) * 0.7978845608028654 * (1.0 + 3.0 * 0.044715 * x * x)


def _place():
    x, y, cc = lax.axis_index("x"), lax.axis_index("y"), lax.axis_index("c")
    return x, y, cc, [(1 - x, y), (x, 1 - y), (1 - x, 1 - y)]


class _Gather:
    def __init__(self, shard):
        self.inputs = [shard]
        self.out_shape = [jax.ShapeDtypeStruct((N_DEV,) + shard.shape, shard.dtype)]
        self.scratch = [pltpu.SemaphoreType.DMA((7,)), pltpu.SemaphoreType.DMA((7,)), pltpu.SemaphoreType.DMA]

    def _copies(self, ins, outs, scr):
        (x_ref,), (out_ref,), (send_sems, recv_sems, local_sem) = ins, outs, scr
        x, y, cc, chips = _place()

        def slab(px, py, pc):
            return out_ref.at[4 * px + 2 * py + pc]

        def copy(k, block, to, src=None):
            return pltpu.make_async_remote_copy(
                src_ref=slab(*block) if src is None else src, dst_ref=slab(*block),
                send_sem=send_sems.at[k], recv_sem=recv_sems.at[k], device_id=to, device_id_type=MESH)

        me, sibling = (x, y, cc), (x, y, 1 - cc)
        mine = pltpu.make_async_copy(x_ref, slab(*me), local_sem)
        first = [copy(0, me, sibling, src=x_ref)] + [copy(1 + j, me, (*chip, cc), src=x_ref) for j, chip in enumerate(chips)]
        landed = [copy(1 + j, (*chip, cc), me) for j, chip in enumerate(chips)]
        passed = [copy(4 + j, (*chip, cc), sibling) for j, chip in enumerate(chips)]
        from_sibling = [copy(0, sibling, me)] + [copy(4 + j, (*chip, 1 - cc), me) for j, chip in enumerate(chips)]
        return mine, first, landed, passed, from_sibling

    def begin(self, *refs):
        mine, first, _, _, _ = self._copies(*refs)
        mine.start()
        for cp in first:
            cp.start()

    def relay(self, *refs):
        _, _, landed, passed, _ = self._copies(*refs)
        for arrived, onward in zip(landed, passed):
            arrived.wait_recv()
            onward.start()

    def end(self, *refs):
        mine, first, _, passed, from_sibling = self._copies(*refs)
        for cp in from_sibling:
            cp.wait_recv()
        for cp in first + passed:
            cp.wait_send()
        mine.wait()


class _SiblingSwap:
    def __init__(self, slabs):
        shape = jax.ShapeDtypeStruct((4,) + slabs.shape[1:], slabs.dtype)
        self.inputs = [slabs]
        self.out_shape = [shape, shape]
        self.scratch = [pltpu.SemaphoreType.DMA((4,)), pltpu.SemaphoreType.DMA((4,)), pltpu.SemaphoreType.DMA((4,))]

    def _copies(self, ins, outs, scr):
        (g_ref,), (got_ref, mine_ref), (send_sems, recv_sems, local_sems) = ins, outs, scr
        x, y, cc, _ = _place()
        keep = [pltpu.make_async_copy(g_ref.at[2 * j + cc], mine_ref.at[j], local_sems.at[j]) for j in range(4)]
        give = [pltpu.make_async_remote_copy(src_ref=g_ref.at[2 * j + 1 - cc], dst_ref=got_ref.at[j], send_sem=send_sems.at[j],
                                             recv_sem=recv_sems.at[j], device_id=(x, y, 1 - cc), device_id_type=MESH) for j in range(4)]
        return keep, give

    def begin(self, *refs):
        keep, give = self._copies(*refs)
        for cp in keep + give:
            cp.start()

    def relay(self, *refs):
        pass

    def end(self, *refs):
        keep, give = self._copies(*refs)
        for cp in give:
            cp.wait_recv()
        for cp in give:
            cp.wait_send()
        for cp in keep:
            cp.wait()


class _ChipScatter:
    def __init__(self, sums):
        self.inputs = [sums]
        self.out_shape = [jax.ShapeDtypeStruct(sums.shape, sums.dtype)]
        self.scratch = [pltpu.SemaphoreType.DMA((3,)), pltpu.SemaphoreType.DMA((3,)), pltpu.SemaphoreType.DMA]

    def _copies(self, ins, outs, scr):
        (s_ref,), (got_ref,), (send_sems, recv_sems, local_sem) = ins, outs, scr
        x, y, cc, chips = _place()
        keep = pltpu.make_async_copy(s_ref.at[2 * x + y], got_ref.at[3], local_sem)
        give = [pltpu.make_async_remote_copy(src_ref=s_ref.at[2 * px + py], dst_ref=got_ref.at[j], send_sem=send_sems.at[j],
                                             recv_sem=recv_sems.at[j], device_id=(px, py, cc), device_id_type=MESH)
                for j, (px, py) in enumerate(chips)]
        return keep, give

    def begin(self, *refs):
        keep, give = self._copies(*refs)
        keep.start()
        for cp in give:
            cp.start()

    def relay(self, *refs):
        pass

    def end(self, *refs):
        keep, give = self._copies(*refs)
        for cp in give:
            cp.wait_recv()
        for cp in give:
            cp.wait_send()
        keep.wait()


class _AllToAll:
    def __init__(self, block):
        self.inputs = [block]
        self.out_shape = [jax.ShapeDtypeStruct((N_DEV,) + block.shape, block.dtype)]
        self.scratch = [pltpu.SemaphoreType.DMA((7,)), pltpu.SemaphoreType.DMA((7,)), pltpu.SemaphoreType.DMA]

    def _copies(self, ins, outs, scr):
        (b_ref,), (got_ref,), (send_sems, recv_sems, local_sem) = ins, outs, scr
        x, y, cc, _ = _place()
        me = 4 * x + 2 * y + cc
        keep = pltpu.make_async_copy(b_ref, got_ref.at[me], local_sem)
        give = [pltpu.make_async_remote_copy(src_ref=b_ref, dst_ref=got_ref.at[me], send_sem=send_sems.at[k - 1], recv_sem=recv_sems.at[k - 1],
                                             device_id=(x ^ ((k >> 2) & 1), y ^ ((k >> 1) & 1), cc ^ (k & 1)), device_id_type=MESH)
                for k in range(1, N_DEV)]
        return keep, give

    begin = _ChipScatter.begin
    relay = _ChipScatter.relay
    end = _ChipScatter.end


def _split_refs(refs, counts):
    out, pos = [], 0
    for n in counts:
        out.append(list(refs[pos:pos + n]))
        pos += n
    return out


def _bind(comms, c_in, c_out, c_scr):
    ins = _split_refs(c_in, [len(c.inputs) for c in comms])
    outs = _split_refs(c_out, [len(c.out_shape) for c in comms])
    scr = _split_refs(c_scr, [len(c.scratch) for c in comms])
    return [(c, (i, o, s)) for c, i, o, s in zip(comms, ins, outs, scr)]


def _call(body, *, name, grid, in_specs, out_specs, out_shape, scratch_shapes, args, comms=()):
    c_in = [a for c in comms for a in c.inputs]
    c_out = [s for c in comms for s in c.out_shape]
    c_scr = [s for c in comms for s in c.scratch]
    counts = [len(in_specs), len(c_in), len(out_shape), len(c_out), len(scratch_shapes), len(c_scr)]
    last = grid[0] - 1

    def full(*refs):
        ins, cin, outs, cout, scr, cscr = _split_refs(refs, counts)
        bound = _bind(comms, cin, cout, cscr)
        step = pl.program_id(0)
        if comms:
            @pl.when(step == 0)
            def _():
                for c, r in bound:
                    c.begin(*r)

            @pl.when(step == last)
            def _():
                for c, r in bound:
                    c.relay(*r)

        body(*ins, *outs, *scr)
        if comms:
            @pl.when(step == last)
            def _():
                for c, r in bound:
                    c.end(*r)

    res = pl.pallas_call(
        full, name=name, grid=grid,
        in_specs=list(in_specs) + [ANY] * len(c_in), out_specs=list(out_specs) + [ANY] * len(c_out),
        out_shape=list(out_shape) + c_out, scratch_shapes=list(scratch_shapes) + c_scr,
        compiler_params=_cparams(("arbitrary",)),
    )(*args, *c_in)
    return res[:len(out_shape)], res[len(out_shape):]


def _comm_only(comms, name):
    c_in = [a for c in comms for a in c.inputs]
    c_out = [s for c in comms for s in c.out_shape]
    c_scr = [s for c in comms for s in c.scratch]

    def full(*refs):
        cin, cout, cscr = _split_refs(refs, [len(c_in), len(c_out), len(c_scr)])
        bound = _bind(comms, cin, cout, cscr)
        for phase in ("begin", "relay", "end"):
            for c, r in bound:
                getattr(c, phase)(*r)

    return pl.pallas_call(full, name=name, in_specs=[ANY] * len(c_in), out_specs=[ANY] * len(c_out), out_shape=c_out,
                          scratch_shapes=c_scr)(*c_in)


def _pair_sum(mine, got):
    _, r, d = mine.shape
    tile = pl.BlockSpec((1, r // 2, d), lambda j, i: (j, i, 0))

    def body(a_ref, b_ref, o_ref):
        o_ref[...] = (a_ref[...].astype(F32) + b_ref[...].astype(F32)).astype(BF16)

    return pl.pallas_call(body, name="pair_sum", grid=(4, 2), in_specs=[tile, tile], out_specs=tile,
                          out_shape=jax.ShapeDtypeStruct(mine.shape, BF16),
                          compiler_params=_cparams(("arbitrary", "arbitrary")))(mine, got)


def _ffn_fwd(x, gain, gw_gu, gw_d, off_d, name, comms=()):
    t = x.shape[0]

    def body(x_ref, g_ref, gu_ref, d_ref, o_ref, wg, wu, wd, sems):
        @pl.when(pl.program_id(0) == 0)
        def _():
            cps = _load_rows(gu_ref, wg, 0, R_FF, sems.at[0]) + _load_rows(gu_ref, wu, R_FF, R_FF, sems.at[1]) \
                + _load_rows(d_ref, wd, off_d, R_FF, sems.at[2])
            for cp in cps:
                cp.wait()

        xv = x_ref[...]
        xn, _ = _rms(xv)
        h = (xn * g_ref[...]).astype(BF16)
        acc = jnp.zeros((TM, D), F32)
        for c in range(F // FC):
            rows = pl.ds(c * FC, FC)
            a = _nt(h, wg[rows, :])
            b = _nt(h, wu[rows, :])
            s = (a * jax.nn.sigmoid(a) * b).astype(BF16)
            acc = acc + _nn(s, wd[rows, :])
        o_ref[...] = xv + 0.5 * acc

    tile = pl.BlockSpec((TM, D), lambda i: (i, 0))
    (out,), got = _call(
        body, name=name, grid=(t // TM,),
        in_specs=[tile, pl.BlockSpec((1, D), lambda i: (0, 0)), ANY, ANY], out_specs=[tile],
        out_shape=[jax.ShapeDtypeStruct((t, D), F32)],
        scratch_shapes=[pltpu.VMEM((F, D), BF16)] * 3 + [pltpu.SemaphoreType.DMA((3, N_DEV))],
        args=(x, gain, gw_gu, gw_d), comms=comms)
    return out, got


def _ffn_bwd(x, gain, dout, gw_gu, gw_d, off_d, name, comms=()):
    t = x.shape[0]

    def body(x_ref, g_ref, do_ref, gu_ref, d_ref, dx_ref, dg_ref, h_ref, dab_ref, s_ref, df_ref, wg, wu, wd, sems):
        @pl.when(pl.program_id(0) == 0)
        def _():
            cps = _load_rows(gu_ref, wg, 0, R_FF, sems.at[0]) + _load_rows(gu_ref, wu, R_FF, R_FF, sems.at[1]) \
                + _load_rows(d_ref, wd, off_d, R_FF, sems.at[2])
            dg_ref[...] = jnp.zeros_like(dg_ref)
            for cp in cps:
                cp.wait()

        xv = x_ref[...]
        gain_v = g_ref[...]
        xn, r = _rms(xv)
        h = (xn * gain_v).astype(BF16)
        do = do_ref[...]
        df = (0.5 * do).astype(BF16)
        dh = jnp.zeros((TM_FFN_BWD, D), F32)
        for c in range(F // FC):
            rows = pl.ds(c * FC, FC)
            a = _nt(h, wg[rows, :])
            b = _nt(h, wu[rows, :])
            sg = jax.nn.sigmoid(a)
            sl = a * sg
            ds = _nt(df, wd[rows, :])
            da = (ds * b * (sg * (1.0 + a * (1.0 - sg)))).astype(BF16)
            db = (ds * sl).astype(BF16)
            dh = dh + _nn(da, wg[rows, :]) + _nn(db, wu[rows, :])
            dab_ref[:, c * FC:(c + 1) * FC] = da
            dab_ref[:, F + c * FC:F + (c + 1) * FC] = db
            s_ref[:, c * FC:(c + 1) * FC] = (sl * b).astype(BF16)
        dxn, dg = _rms_bwd(dh, xn, r, gain_v)
        dg_ref[...] += dg
        dx_ref[...] = do + dxn
        h_ref[...] = h
        df_ref[...] = df

    def tile(w):
        return pl.BlockSpec((TM_FFN_BWD, w), lambda i: (i, 0))

    row = pl.BlockSpec((1, D), lambda i: (0, 0))
    return _call(
        body, name=name, grid=(t // TM_FFN_BWD,),
        in_specs=[tile(D), row, tile(D), ANY, ANY],
        out_specs=[tile(D), row, tile(D), tile(2 * F), tile(F), tile(D)],
        out_shape=[jax.ShapeDtypeStruct((t, D), F32), jax.ShapeDtypeStruct((1, D), F32), jax.ShapeDtypeStruct((t, D), BF16),
                   jax.ShapeDtypeStruct((t, 2 * F), BF16), jax.ShapeDtypeStruct((t, F), BF16), jax.ShapeDtypeStruct((t, D), BF16)],
        scratch_shapes=[pltpu.VMEM((F, D), BF16)] * 3 + [pltpu.SemaphoreType.DMA((3, N_DEV))],
        args=(x, gain, dout, gw_gu, gw_d), comms=comms)


def _weight_grad(a, b, name, tmm=256):
    t, m = a.shape
    n = b.shape[1]

    def body(a_ref, b_ref, o_ref):
        o_ref[...] = _tn(a_ref[...], b_ref[...]).astype(BF16)

    return pl.pallas_call(
        body, name=name, grid=(m // tmm,),
        in_specs=[pl.BlockSpec((t, tmm), lambda i: (0, i)), pl.BlockSpec((t, n), lambda i: (0, 0))],
        out_specs=pl.BlockSpec((tmm, n), lambda i: (i, 0)),
        out_shape=jax.ShapeDtypeStruct((m, n), BF16),
        compiler_params=_cparams(("arbitrary",)),
    )(a, b)


def _mix_proj_fwd(x, gain, b_gate, gw, comms=()):
    t = x.shape[0]

    def body(x_ref, g_ref, bg_ref, gw_ref, q_ref, k_ref, v_ref, zs_ref, gt_ref, h_ref, win, sems):
        @pl.when(pl.program_id(0) == 0)
        def _():
            for cp in _load_rows(gw_ref, win, OFF_IN, R_IN, sems):
                cp.wait()

        xn, _ = _rms(x_ref[...])
        h = (xn * g_ref[...]).astype(BF16)
        h_ref[...] = h
        q_ref[...] = (_nt(h, win[0:512, :]) * 0.125).astype(BF16)
        k_ref[...] = _nt(h, win[512:1024, :]).astype(BF16)
        v_ref[...] = _nt(h, win[1024:1536, :]).astype(BF16)
        for c in range(2):
            zs_ref[:, c * 512:(c + 1) * 512] = _nt(h, win[1536 + c * 512:2048 + c * 512, :]).astype(BF16)
        for c in range(4):
            zg = _nt(h, win[2560 + c * 512:3072 + c * 512, :]) + bg_ref[:, c * 512:(c + 1) * 512]
            gt_ref[:, c * 512:(c + 1) * 512] = jax.nn.sigmoid(zg).astype(BF16)

    def tile(w):
        return pl.BlockSpec((TM, w), lambda i: (i, 0))

    return _call(
        body, name="mix_proj_fwd", grid=(t // TM,),
        in_specs=[tile(D), pl.BlockSpec((1, D), lambda i: (0, 0)), pl.BlockSpec((1, 2 * D), lambda i: (0, 0)), ANY],
        out_specs=[tile(D_ATT), tile(D_ATT), tile(D_ATT), tile(2 * D_SGU), tile(2 * D), tile(D)],
        out_shape=[jax.ShapeDtypeStruct((t, D_ATT), BF16)] * 3 + [jax.ShapeDtypeStruct((t, 2 * D_SGU), BF16),
                                                                   jax.ShapeDtypeStruct((t, 2 * D), BF16),
                                                                   jax.ShapeDtypeStruct((t, D), BF16)],
        scratch_shapes=[pltpu.VMEM((D_IN, D), BF16), pltpu.SemaphoreType.DMA((N_DEV,))],
        args=(x, gain, b_gate, gw), comms=comms)


def _mix_proj_bwd(dz, x, gain, dres, gw):
    t = x.shape[0]

    def body(dz_ref, x_ref, g_ref, dr_ref, gw_ref, dx_ref, dg_ref, win, sems):
        @pl.when(pl.program_id(0) == 0)
        def _():
            cps = _load_rows(gw_ref, win, OFF_IN, R_IN, sems)
            dg_ref[...] = jnp.zeros_like(dg_ref)
            for cp in cps:
                cp.wait()

        dh = jnp.zeros((TM, D), F32)
        for c in range(D_IN // 512):
            dh = dh + _nn(dz_ref[:, c * 512:(c + 1) * 512], win[c * 512:(c + 1) * 512, :])
        xn, r = _rms(x_ref[...])
        dxn, dg = _rms_bwd(dh, xn, r, g_ref[...])
        dg_ref[...] += dg
        dx_ref[...] = dr_ref[...] + dxn

    def tile(w):
        return pl.BlockSpec((TM, w), lambda i: (i, 0))

    row = pl.BlockSpec((1, D), lambda i: (0, 0))
    return pl.pallas_call(
        body, name="mix_proj_bwd", grid=(t // TM,),
        in_specs=[tile(D_IN), tile(D), row, tile(D), ANY], out_specs=[tile(D), row],
        out_shape=[jax.ShapeDtypeStruct((t, D), F32), jax.ShapeDtypeStruct((1, D), F32)],
        scratch_shapes=[pltpu.VMEM((D_IN, D), BF16), pltpu.SemaphoreType.DMA((N_DEV,))],
        compiler_params=_cparams(("arbitrary",)),
    )(dz, x, gain, dres, gw)


SKEW_W = KW + QB
N_CAP = 2 * QB - REL_CLIP + 1


def _band_bias(rel_bias):
    cap = rel_bias[:, 2 * REL_CLIP:]
    diag = jnp.concatenate([jnp.broadcast_to(cap, (HEADS, N_CAP)), rel_bias[:, 2 * REL_CLIP - 1::-1],
                            jnp.broadcast_to(cap, (HEADS, SKEW_W - N_CAP - 2 * REL_CLIP))], axis=1)

    def body(d_ref, o_ref):
        lag = lax.broadcasted_iota(jnp.int32, (QB, KW), 1) // CHUNK - lax.broadcasted_iota(jnp.int32, (QB, KW), 0) // CHUNK
        band = (lag >= 0) & (lag <= N_LEFT)
        for h in range(HEADS):
            rows = jnp.broadcast_to(d_ref[h:h + 1, :], (QB, SKEW_W))
            o_ref[h] = jnp.where(band, pltpu.roll(rows, 0, 1, stride=1, stride_axis=0)[:, :KW], NEG_INF)

    return pl.pallas_call(body, name="band_bias", out_shape=jax.ShapeDtypeStruct((HEADS, QB, KW), F32))(diag)


def _att_specs():
    qspec = pl.BlockSpec((QB, D_ATT), lambda g: (g, 0))
    kspecs = [pl.BlockSpec((QB, D_ATT), lambda g: (jnp.maximum(g - 2, 0), 0)),
              pl.BlockSpec((QB, D_ATT), lambda g: (jnp.maximum(g - 1, 0), 0)), qspec]
    bspec = pl.BlockSpec((HEADS, QB, KW), lambda g: (0, 0, 0))
    return qspec, kspecs, bspec


def _att_probs(qm, kp, bias, valid):
    s = jnp.where(valid, _nt(qm, kp) + bias, NEG_INF)
    e = jnp.exp(s - jnp.max(s, axis=-1, keepdims=True))
    return e / jnp.sum(e, axis=-1, keepdims=True)


def _att_valid():
    g = pl.program_id(0)
    blk = lax.broadcasted_iota(jnp.int32, (QB, KW), 1) // QB
    return (blk + g) >= 2


def _att_fwd(q, k, v, bias, comms=()):
    t = q.shape[0]

    def body(q_ref, k0, k1, k2, v0, v1, v2, b_ref, y_ref):
        valid = _att_valid()
        first = lax.broadcasted_iota(jnp.int32, (1, 128), 1) < 64
        for p in range(HEADS // 2):
            lanes = slice(p * 128, (p + 1) * 128)
            qp = q_ref[:, lanes]
            kp = jnp.concatenate([k0[:, lanes], k1[:, lanes], k2[:, lanes]], axis=0)
            vp = jnp.concatenate([v0[:, lanes], v1[:, lanes], v2[:, lanes]], axis=0)
            out = jnp.zeros((QB, 128), F32)
            for hh in range(2):
                mask = first if hh == 0 else jnp.logical_not(first)
                pr = _att_probs(jnp.where(mask, qp, 0), kp, b_ref[2 * p + hh], valid)
                out = out + _nn(pr.astype(BF16), jnp.where(mask, vp, 0))
            y_ref[:, lanes] = out.astype(BF16)

    qspec, kspecs, bspec = _att_specs()
    (out,), got = _call(
        body, name="att_fwd", grid=(t // QB,),
        in_specs=[qspec] + kspecs + kspecs + [bspec], out_specs=[qspec],
        out_shape=[jax.ShapeDtypeStruct((t, D_ATT), BF16)], scratch_shapes=[],
        args=(q, k, k, k, v, v, v, bias), comms=comms)
    return out, got


def _att_bwd(q, k, v, bias, dy, comms=()):
    t = q.shape[0]
    n_blocks = t // QB

    def body(q_ref, k0, k1, k2, v0, v1, v2, b_ref, dy_ref, dq_ref, dk_ref, dv_ref, db_ref, dk_acc, dv_acc):
        g = pl.program_id(0)

        @pl.when(g == 0)
        def _():
            db_ref[...] = jnp.zeros_like(db_ref)
            dk_acc[...] = jnp.zeros_like(dk_acc)
            dv_acc[...] = jnp.zeros_like(dv_acc)

        valid = _att_valid()
        first = lax.broadcasted_iota(jnp.int32, (1, 128), 1) < 64
        for p in range(HEADS // 2):
            lanes = slice(p * 128, (p + 1) * 128)
            qp = q_ref[:, lanes]
            dyp = dy_ref[:, lanes]
            kp = jnp.concatenate([k0[:, lanes], k1[:, lanes], k2[:, lanes]], axis=0)
            vp = jnp.concatenate([v0[:, lanes], v1[:, lanes], v2[:, lanes]], axis=0)
            dq = jnp.zeros((QB, 128), F32)
            dk = jnp.zeros((KW, 128), F32)
            dv = jnp.zeros((KW, 128), F32)
            for hh in range(2):
                mask = first if hh == 0 else jnp.logical_not(first)
                qm = jnp.where(mask, qp, 0)
                dym = jnp.where(mask, dyp, 0)
                pr = _att_probs(qm, kp, b_ref[2 * p + hh], valid)
                dp = _nt(dym, vp)
                ds = pr * (dp - jnp.sum(dp * pr, axis=-1, keepdims=True))
                db_ref[2 * p + hh] += ds
                dsb = ds.astype(BF16)
                dq = dq + _nn(dsb, jnp.where(mask, kp, 0))
                dk = dk + _tn(dsb, qm)
                dv = dv + _tn(pr.astype(BF16), dym)
            dq_ref[:, lanes] = (dq * 0.125).astype(BF16)
            for j in range(3):
                rows = pl.ds(pl.multiple_of(jnp.maximum(g - 2 + j, 0) * QB, QB), QB)
                dk_acc[rows, lanes] += dk[j * QB:(j + 1) * QB]
                dv_acc[rows, lanes] += dv[j * QB:(j + 1) * QB]

        @pl.when(g == n_blocks - 1)
        def _():
            dk_ref[...] = dk_acc[...].astype(BF16)
            dv_ref[...] = dv_acc[...].astype(BF16)

    qspec, kspecs, bspec = _att_specs()
    full = pl.BlockSpec((t, D_ATT), lambda g: (0, 0))
    return _call(
        body, name="att_bwd", grid=(n_blocks,),
        in_specs=[qspec] + kspecs + kspecs + [bspec, qspec], out_specs=[qspec, full, full, bspec],
        out_shape=[jax.ShapeDtypeStruct((t, D_ATT), BF16)] * 3 + [jax.ShapeDtypeStruct((HEADS, QB, KW), F32)],
        scratch_shapes=[pltpu.VMEM((t, D_ATT), F32)] * 2,
        args=(q, k, k, k, v, v, v, bias, dy), comms=comms)


def _rel_bias_grad(dbias):
    def body(db_ref, cs_ref, tot_ref):
        lane = lax.broadcasted_iota(jnp.int32, (1, SKEW_W), 1)
        capped = (lane < N_CAP) | (lane > KW)
        pad = jnp.zeros((8, QB), F32)
        for h in range(HEADS):
            def add_rows(a, z):
                rows = jnp.concatenate([db_ref[h, pl.ds(pl.multiple_of(8 * a, 8), 8), :], pad], axis=1)
                return z + pltpu.roll(rows, (SKEW_W - 8 * a) % SKEW_W, 1)

            z = lax.fori_loop(0, QB // 8, add_rows, jnp.zeros((8, SKEW_W), F32))
            cs = z[0:1, :]
            for b in range(1, 8):
                cs = cs + pltpu.roll(z[b:b + 1, :], SKEW_W - b, 1)
            cs_ref[h:h + 1, :] = cs
            tot_ref[h:h + 1, :] = jnp.broadcast_to(jnp.sum(jnp.where(capped, cs, 0.0), axis=1, keepdims=True), (1, 128))

    cs, tot = pl.pallas_call(
        body, name="rel_bias_grad",
        out_shape=[jax.ShapeDtypeStruct((HEADS, SKEW_W), F32), jax.ShapeDtypeStruct((HEADS, 128), F32)],
    )(dbias)
    return jnp.concatenate([cs[:, KW:N_CAP - 1:-1], tot[:, :1]], axis=1)


def _sgu_mask():
    pos = np.arange(SGU_BLOCK)
    return (pos[:, None] // CHUNK) >= (pos[None, :] // CHUNK)


def _group_stack(blk, first):
    return jnp.concatenate([jnp.where(first, blk, 0), jnp.where(first, 0, blk)], axis=0)


def _sgu_norm(zs_ref, lng, lnb):
    zs = zs_ref[...].astype(F32)
    ga, th = _gelu(zs)
    u, vs = ga[:, :D_SGU], ga[:, D_SGU:]
    mu = jnp.mean(vs, axis=-1, keepdims=True)
    cen = vs - mu
    rstd = lax.rsqrt(jnp.mean(cen * cen, axis=-1, keepdims=True) + EPS)
    xhat = cen * rstd
    return zs, th, u, xhat, rstd, xhat * lng + lnb


def _sgu_mix(vb, wm2_ref, bsx, s_ref):
    first = lax.broadcasted_iota(jnp.int32, (1, 128), 1) < 64
    for n in range(TM // SGU_BLOCK):
        for p in range(4):
            blk = vb[n * 128:(n + 1) * 128, p * 128:(p + 1) * 128]
            s_ref[n * 128:(n + 1) * 128, p * 128:(p + 1) * 128] = _nn(wm2_ref[p], _group_stack(blk, first)) + bsx[:, p * 128:(p + 1) * 128]


def _merge_fwd(x, zs, gt, y_att, lng, lnb, wm2, bsx, gw):
    t = x.shape[0]

    def body(x_ref, zs_ref, gt_ref, ya_ref, lng_ref, lnb_ref, wm2_ref, bsx_ref, gw_ref, xo_ref, ys_ref, mg_ref,
             wbr, wo, s_scr, sems):
        @pl.when(pl.program_id(0) == 0)
        def _():
            for cp in _load_rows(gw_ref, wbr, OFF_BR, R_BR, sems.at[0]) + _load_rows(gw_ref, wo, OFF_WO, R_WO, sems.at[1]):
                cp.wait()

        _, _, u, _, _, vsn = _sgu_norm(zs_ref, lng_ref[...], lnb_ref[...])
        _sgu_mix(vsn.astype(BF16), wm2_ref, bsx_ref[...], s_scr)
        ys = (u * s_scr[...]).astype(BF16)
        ys_ref[...] = ys
        pa = _nt(ya_ref[...], wbr[:, :D_ATT])
        ps = _nt(ys, wbr[:, D_ATT:])
        mg = (gt_ref[:, :D].astype(F32) * pa + gt_ref[:, D:].astype(F32) * ps).astype(BF16)
        mg_ref[...] = mg
        xo_ref[...] = x_ref[...] + _nn(mg, wo[...])

    def tile(w):
        return pl.BlockSpec((TM, w), lambda i: (i, 0))

    def const(shape):
        return pl.BlockSpec(shape, lambda i: (0,) * len(shape))

    return pl.pallas_call(
        body, name="merge_fwd", grid=(t // TM,),
        in_specs=[tile(D), tile(2 * D_SGU), tile(2 * D), tile(D_ATT), const((1, D_SGU)), const((1, D_SGU)),
                  const((4, 128, 256)), const((128, D_SGU)), ANY],
        out_specs=[tile(D), tile(D_SGU), tile(D)],
        out_shape=[jax.ShapeDtypeStruct((t, D), F32), jax.ShapeDtypeStruct((t, D_SGU), BF16), jax.ShapeDtypeStruct((t, D), BF16)],
        scratch_shapes=[pltpu.VMEM((D, D), BF16), pltpu.VMEM((D, D), BF16), pltpu.VMEM((TM, D_SGU), F32),
                        pltpu.SemaphoreType.DMA((2, N_DEV))],
        compiler_params=_cparams(("arbitrary",)),
    )(x, zs, gt, y_att, lng, lnb, wm2, bsx, gw)


def _merge_bwd(dx, gt, y_att, y_sgu, gw):
    t = dx.shape[0]

    def body(dx_ref, gt_ref, ya_ref, ys_ref, gw_ref, dzg_ref, dya_ref, dys_ref, dpp_ref, dxb_ref, dbg_ref, wbr, wo, sems):
        @pl.when(pl.program_id(0) == 0)
        def _():
            cps = _load_rows(gw_ref, wbr, OFF_BR, R_BR, sems.at[0]) + _load_rows(gw_ref, wo, OFF_WO, R_WO, sems.at[1])
            dbg_ref[...] = jnp.zeros_like(dbg_ref)
            for cp in cps:
                cp.wait()

        dxb = dx_ref[...].astype(BF16)
        dxb_ref[...] = dxb
        dm = _nt(dxb, wo[...])
        for half, y_ref, w in ((0, ya_ref, wbr.at[:, :D_ATT]), (1, ys_ref, wbr.at[:, D_ATT:])):
            cols = slice(half * D, (half + 1) * D)
            gate = gt_ref[:, cols].astype(F32)
            branch = _nt(y_ref[...], w[...])
            dzg = dm * branch * gate * (1.0 - gate)
            dbg_ref[:, cols] += jnp.sum(dzg, axis=0, keepdims=True)
            dzg_ref[:, cols] = dzg.astype(BF16)
            dbr = (dm * gate).astype(BF16)
            dpp_ref[:, cols] = dbr
            dy = _nn(dbr, w[...])
            if half == 0:
                dya_ref[...] = dy.astype(BF16)
            else:
                dys_ref[...] = dy

    def tile(w):
        return pl.BlockSpec((TM, w), lambda i: (i, 0))

    return pl.pallas_call(
        body, name="merge_bwd", grid=(t // TM,),
        in_specs=[tile(D), tile(2 * D), tile(D_ATT), tile(D_SGU), ANY],
        out_specs=[tile(2 * D), tile(D_ATT), tile(D_SGU), tile(2 * D), tile(D), pl.BlockSpec((1, 2 * D), lambda i: (0, 0))],
        out_shape=[jax.ShapeDtypeStruct((t, 2 * D), BF16), jax.ShapeDtypeStruct((t, D_ATT), BF16), jax.ShapeDtypeStruct((t, D_SGU), F32),
                   jax.ShapeDtypeStruct((t, 2 * D), BF16), jax.ShapeDtypeStruct((t, D), BF16), jax.ShapeDtypeStruct((1, 2 * D), F32)],
        scratch_shapes=[pltpu.VMEM((D, D), BF16), pltpu.VMEM((D, D), BF16), pltpu.SemaphoreType.DMA((2, N_DEV))],
        compiler_params=_cparams(("arbitrary",)),
    )(dx, gt, y_att, y_sgu, gw)


def _sgu_bwd(zs, dys, lng, lnb, wm2, wmt2, bsx):
    t = zs.shape[0]
    n_steps = t // TM

    def body(zs_ref, dys_ref, lng_ref, lnb_ref, wm2_ref, wmt2_ref, bsx_ref, dzs_ref, dw_ref, dbs_ref, dlg_ref, dlb_ref,
             s_scr, dv_scr, ds_acc):
        i = pl.program_id(0)

        @pl.when(i == 0)
        def _():
            dw_ref[...] = jnp.zeros_like(dw_ref)
            dlg_ref[...] = jnp.zeros_like(dlg_ref)
            dlb_ref[...] = jnp.zeros_like(dlb_ref)
            ds_acc[...] = jnp.zeros_like(ds_acc)

        lng = lng_ref[...]
        zs, th, u, xhat, rstd, vsn = _sgu_norm(zs_ref, lng, lnb_ref[...])
        vb = vsn.astype(BF16)
        _sgu_mix(vb, wm2_ref, bsx_ref[...], s_scr)
        dys = dys_ref[...]
        du = dys * s_scr[...]
        ds = dys * u
        dsb = ds.astype(BF16)
        first = lax.broadcasted_iota(jnp.int32, (1, 128), 1) < 64
        acc = jnp.zeros((SGU_BLOCK, D_SGU), F32)
        for n in range(TM // SGU_BLOCK):
            rows = slice(n * 128, (n + 1) * 128)
            acc = acc + ds[rows]
            for p in range(4):
                lanes = slice(p * 128, (p + 1) * 128)
                stack = _group_stack(dsb[rows, lanes], first)
                dv_scr[rows, lanes] = _nn(wmt2_ref[p], stack)
                dw_ref[p] += _nt(stack, vb[rows, lanes])
        ds_acc[...] += acc
        dvsn = dv_scr[...]
        dlg_ref[...] += jnp.sum(dvsn * xhat, axis=0, keepdims=True)
        dlb_ref[...] += jnp.sum(dvsn, axis=0, keepdims=True)
        dxh = dvsn * lng
        dvs = rstd * (dxh - jnp.mean(dxh, axis=-1, keepdims=True) - xhat * jnp.mean(dxh * xhat, axis=-1, keepdims=True))
        dga = jnp.concatenate([du, dvs], axis=1)
        dzs_ref[...] = (dga * _gelu_grad(zs, th)).astype(BF16)

        @pl.when(i == n_steps - 1)
        def _():
            r = lax.broadcasted_iota(jnp.int32, (256, 128), 0) % SGU_BLOCK
            c = lax.broadcasted_iota(jnp.int32, (256, 128), 1)
            keep = (r // CHUNK) >= (c // CHUNK)
            for p in range(4):
                dw_ref[p] = jnp.where(keep, dw_ref[p], 0.0)
            total = ds_acc[...]
            grp = lax.broadcasted_iota(jnp.int32, (SGU_BLOCK, D_SGU), 1) // 64
            lane = lax.broadcasted_iota(jnp.int32, (SGU_BLOCK, 128), 1)
            out = jnp.zeros((SGU_BLOCK, 128), F32)
            for gi in range(8):
                out = jnp.where(lane == gi, jnp.sum(jnp.where(grp == gi, total, 0.0), axis=1, keepdims=True), out)
            dbs_ref[...] = out

    def tile(w):
        return pl.BlockSpec((TM, w), lambda i: (i, 0))

    def const(shape):
        return pl.BlockSpec(shape, lambda i: (0,) * len(shape))

    return pl.pallas_call(
        body, name="sgu_bwd", grid=(n_steps,),
        in_specs=[tile(2 * D_SGU), tile(D_SGU), const((1, D_SGU)), const((1, D_SGU)), const((4, 128, 256)), const((4, 128, 256)),
                  const((128, D_SGU))],
        out_specs=[tile(2 * D_SGU), const((4, 256, 128)), const((128, 128)), const((1, D_SGU)), const((1, D_SGU))],
        out_shape=[jax.ShapeDtypeStruct((t, 2 * D_SGU), BF16), jax.ShapeDtypeStruct((4, 256, 128), F32),
                   jax.ShapeDtypeStruct((128, 128), F32), jax.ShapeDtypeStruct((1, D_SGU), F32), jax.ShapeDtypeStruct((1, D_SGU), F32)],
        scratch_shapes=[pltpu.VMEM((TM, D_SGU), F32), pltpu.VMEM((TM, D_SGU), F32), pltpu.VMEM((SGU_BLOCK, D_SGU), F32)],
        compiler_params=_cparams(("arbitrary",)),
    )(zs, dys, lng, lnb, wm2, wmt2, bsx)


def _loss_head(x, gain, target):
    t = x.shape[0]

    def body(x_ref, g_ref, t_ref, dx_ref, dg_ref, loss_ref):
        @pl.when(pl.program_id(0) == 0)
        def _():
            dg_ref[...] = jnp.zeros_like(dg_ref)
            loss_ref[...] = jnp.zeros_like(loss_ref)

        gain_v = g_ref[...]
        xn, r = _rms(x_ref[...])
        err = xn * gain_v - t_ref[...]
        loss_ref[...] += 0.5 * jnp.sum(jnp.mean(err * err, axis=-1, keepdims=True), axis=0, keepdims=True)
        dxn, dg = _rms_bwd(err * (1.0 / D), xn, r, gain_v)
        dg_ref[...] += dg
        dx_ref[...] = dxn

    tile = pl.BlockSpec((TM, D), lambda i: (i, 0))
    row = pl.BlockSpec((1, D), lambda i: (0, 0))
    return pl.pallas_call(
        body, name="loss_head", grid=(t // TM,),
        in_specs=[tile, row, tile], out_specs=[tile, row, pl.BlockSpec((1, 128), lambda i: (0, 0))],
        out_shape=[jax.ShapeDtypeStruct((t, D), F32), jax.ShapeDtypeStruct((1, D), F32), jax.ShapeDtypeStruct((1, 128), F32)],
        compiler_params=_cparams(("arbitrary",)),
    )(x, gain, target)


def _adamw(g, w, m, v):
    m = ADAM_B1 * m + (1.0 - ADAM_B1) * g
    v = ADAM_B2 * v + (1.0 - ADAM_B2) * (g * g)
    m_hat = m / (1.0 - ADAM_B1 ** ADAM_STEP)
    v_hat = v / (1.0 - ADAM_B2 ** ADAM_STEP)
    return -ADAM_LR * (m_hat / (jnp.sqrt(v_hat) + ADAM_EPS) + ADAM_WD * w), m, v


def _sum_adamw(parts, w, m, v, rows_per_step, name):
    rows = w.shape[0]
    n_parts = parts.shape[0]

    def body(p_ref, w_ref, m_ref, v_ref, g_ref, d_ref, mo_ref, vo_ref):
        g = p_ref[0].astype(F32)
        for k in range(1, n_parts):
            g = g + p_ref[k].astype(F32)
        g_ref[...] = g
        d_ref[...], mo_ref[...], vo_ref[...] = _adamw(g, w_ref[...], m_ref[...], v_ref[...])

    tile = pl.BlockSpec((rows_per_step, D), lambda i: (i, 0))
    return pl.pallas_call(
        body, name=name, grid=(rows // rows_per_step,),
        in_specs=[pl.BlockSpec((n_parts, rows_per_step, D), lambda i: (0, i, 0)), tile, tile, tile], out_specs=[tile] * 4,
        out_shape=[jax.ShapeDtypeStruct((rows, D), F32)] * 4,
        compiler_params=_cparams(("arbitrary",)),
    )(parts, w, m, v)


def _ffn_rows(p, ffn, pre=""):
    return jnp.concatenate([p[pre + ffn + "_w_gate"][0].T, p[pre + ffn + "_w_up"][0].T, p[pre + ffn + "_w_down"][0]], axis=0)


def _mix_rows(p, pre=""):
    return jnp.concatenate([p[pre + "w_in"][0].T, jnp.concatenate([p[pre + "w_branch_att"][0].T, p[pre + "w_branch_sgu"][0].T], axis=1),
                            p[pre + "w_out"][0]], axis=0)


def _from_ffn_rows(slab, ffn):
    return {ffn + "_w_gate": slab[:R_FF].T[None], ffn + "_w_up": slab[R_FF:2 * R_FF].T[None], ffn + "_w_down": slab[2 * R_FF:][None]}


def _from_mix_rows(slab):
    br = slab[OFF_BR:OFF_WO]
    return {"w_in": slab[:R_IN].T[None], "w_branch_att": br[:, :D_ATT].T[None], "w_branch_sgu": br[:, D_ATT:].T[None],
            "w_out": slab[OFF_WO:][None]}


def _split(a, rows):
    return a.reshape(N_DEV, rows, a.shape[-1])


def _ffn_grad_slabs(g_gate_up, g_down):
    return jnp.concatenate([_split(g_gate_up[:F], R_FF), _split(g_gate_up[F:], R_FF), _split(g_down, R_FF)], axis=1)


def _mix_grad_slabs(g_in, g_ba, g_bs, g_wo):
    return jnp.concatenate([_split(g_in, R_IN), jnp.concatenate([_split(g_ba, R_BR), _split(g_bs, R_BR)], axis=2), _split(g_wo, R_WO)],
                           axis=1)


def _pad_row(a):
    a = a.reshape(1, -1)
    return jnp.pad(a, ((0, 0), (0, D - a.shape[1])))


def _pack_small(loss, n1, nm, n2, nf, bg, lng, lnb, bs, rel, ws):
    rows = [_pad_row(loss), n1.reshape(1, D), nm.reshape(1, D), n2.reshape(1, D), nf.reshape(1, D), bg.reshape(2, D),
            jnp.concatenate([lng.reshape(1, D_SGU), lnb.reshape(1, D_SGU)], axis=1), bs.reshape(1, D),
            jnp.pad(rel.reshape(HEADS, N_REL), ((0, 0), (0, D - N_REL))), jnp.zeros((ROW_WS - ROW_REL - HEADS, D), F32),
            ws.reshape(128, D)]
    return jnp.concatenate(rows, axis=0).astype(F32)


def _unpack_small(s):
    return {"loss": s[ROW_LOSS, 0], "norm_ffn1": s[ROW_N1][None], "norm_mix": s[ROW_NM][None], "norm_ffn2": s[ROW_N2][None],
            "norm_final": s[ROW_NF], "b_gate": s[ROW_BG:ROW_BG + 2].reshape(1, 2 * D),
            "sgu_ln_g": s[ROW_LN, :D_SGU][None], "sgu_ln_b": s[ROW_LN, D_SGU:][None], "sgu_b_s": s[ROW_BS].reshape(1, 8, 128),
            "rel_bias": s[ROW_REL:ROW_REL + HEADS, :N_REL][None], "sgu_w_s": s[ROW_WS:].reshape(1, 8, 128, 128)}


_SMALL = ("norm_ffn1", "norm_mix", "norm_ffn2", "norm_final", "b_gate", "sgu_ln_g", "sgu_ln_b", "sgu_b_s", "rel_bias", "sgu_w_s")


def _pack_small_params(p, prefix=""):
    g = lambda n: p[prefix + n]
    return _pack_small(jnp.zeros((1,), F32), g("norm_ffn1"), g("norm_mix"), g("norm_ffn2"), g("norm_final"), g("b_gate"),
                       g("sgu_ln_g"), g("sgu_ln_b"), g("sgu_b_s"), g("rel_bias"), g("sgu_w_s"))


def _step(x, target, p):
    n1, nm, n2 = p["norm_ffn1"], p["norm_mix"], p["norm_ffn2"]
    nf = p["norm_final"].reshape(1, D)
    lng, lnb = p["sgu_ln_g"], p["sgu_ln_b"]
    w_m = jnp.where(jnp.asarray(_sgu_mask())[None], p["sgu_w_s"][0], 0.0).astype(BF16)
    wm2 = jnp.concatenate([w_m[0::2], w_m[1::2]], axis=2)
    w_mt = w_m.transpose(0, 2, 1)
    wmt2 = jnp.concatenate([w_mt[0::2], w_mt[1::2]], axis=2)
    bsx = jnp.repeat(p["sgu_b_s"][0].T, 64, axis=1)
    bias = _band_bias(p["rel_bias"][0])

    def chip_sums(slabs, name):
        got, mine = _comm_only([_SiblingSwap(slabs)], name)
        return _pair_sum(mine, got)

    (gw1,) = _comm_only([_Gather(_ffn_rows(p, "ffn1").astype(BF16))], "gather_ffn1")
    x1, (gwm,) = _ffn_fwd(x, n1, gw1, gw1, 2 * R_FF, "ffn1_fwd", [_Gather(_mix_rows(p).astype(BF16))])
    (q, k, v, zs, gt, h2), (gw2d,) = _mix_proj_fwd(x1, nm, p["b_gate"], gwm, [_Gather(p["ffn2_w_down"][0].astype(BF16))])
    gate_up2 = jnp.concatenate([p["ffn2_w_gate"][0].T, p["ffn2_w_up"][0].T], axis=0).astype(BF16)
    y_att, (gw2gu,) = _att_fwd(q, k, v, bias, [_Gather(gate_up2)])
    x2, y_sgu, merged = _merge_fwd(x1, zs, gt, y_att, lng, lnb, wm2, bsx, gwm)
    x3, _ = _ffn_fwd(x2, n2, gw2gu, gw2d, 0, "ffn2_fwd")
    dx3, d_nf, loss = _loss_head(x3, nf, target)

    (dx2, d_n2, hb, dab, sb, dfb), _ = _ffn_bwd(x2, n2, dx3, gw2gu, gw2d, 0, "ffn2_bwd")
    sums2 = chip_sums(_ffn_grad_slabs(_weight_grad(dab, hb, "ffn2_dw_gate_up"), _weight_grad(sb, dfb, "ffn2_dw_down")), "swap_ffn2")

    dzg, dya, dys, dpp, dxb, d_bg = _merge_bwd(dx2, gt, y_att, y_sgu, gwm)
    g_wo = _weight_grad(merged, dxb, "dw_out")
    g_ba = _weight_grad(dpp[:, :D], y_att, "dw_branch_att")
    g_bs = _weight_grad(dpp[:, D:], y_sgu, "dw_branch_sgu")
    dzs, d_wm, d_bs, d_lng, d_lnb = _sgu_bwd(zs, dys, lng, lnb, wm2, wmt2, bsx)
    (dq, dk, dv, d_bias), (parts2,) = _att_bwd(q, k, v, bias, dya, [_ChipScatter(sums2)])
    out2 = _sum_adamw(parts2, _ffn_rows(p, "ffn2"), _ffn_rows(p, "ffn2", "m_"), _ffn_rows(p, "ffn2", "v_"), R_FF, "adamw_ffn2")
    d_rel = _rel_bias_grad(d_bias)
    dz = jnp.concatenate([dq, dk, dv, dzs, dzg], axis=1)
    sums_m = chip_sums(_mix_grad_slabs(_weight_grad(dz, h2, "dw_in"), g_ba, g_bs, g_wo), "swap_mixer")
    dx1, d_nm = _mix_proj_bwd(dz, x1, nm, dx2, gwm)

    (dx0, d_n1, hb, dab, sb, dfb), (parts_m,) = _ffn_bwd(x, n1, dx1, gw1, gw1, 2 * R_FF, "ffn1_bwd", [_ChipScatter(sums_m)])
    out_m = _sum_adamw(parts_m, _mix_rows(p), _mix_rows(p, "m_"), _mix_rows(p, "v_"), MIX_ROWS // 4, "adamw_mixer")
    sums1 = chip_sums(_ffn_grad_slabs(_weight_grad(dab, hb, "ffn1_dw_gate_up"), _weight_grad(sb, dfb, "ffn1_dw_down")), "swap_ffn1")
    small = _pack_small(loss[0, :1], d_n1, d_nm, d_n2, d_nf, d_bg, d_lng, d_lnb, d_bs[:, :8].T, d_rel,
                        d_wm.reshape(8, 128, 128))
    parts1, small_parts = _comm_only([_ChipScatter(sums1), _AllToAll(small)], "scatter_ffn1_gather_small")
    out1 = _sum_adamw(parts1, _ffn_rows(p, "ffn1"), _ffn_rows(p, "ffn1", "m_"), _ffn_rows(p, "ffn1", "v_"), R_FF, "adamw_ffn1")
    out_s = _sum_adamw(small_parts, _pack_small_params(p), _pack_small_params(p, "m_"), _pack_small_params(p, "v_"), SMALL_ROWS,
                   "adamw_small")
    return dx0, [{**_from_ffn_rows(a, "ffn1"), **_from_mix_rows(b), **_from_ffn_rows(c, "ffn2"), **_unpack_small(d)}
                 for a, b, c, d in zip(out1, out_m, out2, out_s)]


_OUT_ORDER = ("norm_ffn1", "ffn1_w_gate", "ffn1_w_up", "ffn1_w_down", "norm_mix", "w_in", "b_gate", "rel_bias", "sgu_ln_g", "sgu_ln_b",
              "sgu_w_s", "sgu_b_s", "w_branch_att", "w_branch_sgu", "w_out", "norm_ffn2", "ffn2_w_gate", "ffn2_w_up", "ffn2_w_down",
              "norm_final")


def kernel(x, norm_ffn1, ffn1_w_gate, ffn1_w_up, ffn1_w_down, norm_mix, w_in, b_gate, rel_bias, sgu_ln_g, sgu_ln_b, sgu_w_s, sgu_b_s, w_branch_att, w_branch_sgu, w_out, norm_ffn2, ffn2_w_gate, ffn2_w_up, ffn2_w_down, norm_final, loss_target, m_norm_ffn1, m_ffn1_w_gate, m_ffn1_w_up, m_ffn1_w_down, m_norm_mix, m_w_in, m_b_gate, m_rel_bias, m_sgu_ln_g, m_sgu_ln_b, m_sgu_w_s, m_sgu_b_s, m_w_branch_att, m_w_branch_sgu, m_w_out, m_norm_ffn2, m_ffn2_w_gate, m_ffn2_w_up, m_ffn2_w_down, m_norm_final, v_norm_ffn1, v_ffn1_w_gate, v_ffn1_w_up, v_ffn1_w_down, v_norm_mix, v_w_in, v_b_gate, v_rel_bias, v_sgu_ln_g, v_sgu_ln_b, v_sgu_w_s, v_sgu_b_s, v_w_branch_att, v_w_branch_sgu, v_w_out, v_norm_ffn2, v_ffn2_w_gate, v_ffn2_w_up, v_ffn2_w_down, v_norm_final):
    args = dict(locals())
    dx, outs = _step(x[0], loss_target[0], {pre + n: args[pre + n] for pre in ("", "m_", "v_") for n in _OUT_ORDER})
    return (outs[0]["loss"], dx[None], *[o[n] for o in outs for n in _OUT_ORDER])
```

```python
import functools

import numpy as np
import jax
import jax.numpy as jnp
from jax import lax
from jax.experimental import pallas as pl
from jax.experimental.pallas import tpu as pltpu

F32 = jnp.float32
BF16 = jnp.bfloat16

N_DEV = 8
D = 1024
F = 2816
D_ATT = 512
D_SGU = 512
D_IN = 4608
HEADS = 8
CHUNK = 64
N_LEFT = 8
REL_CLIP = 256
N_REL = 2 * REL_CLIP + 1
SGU_BLOCK = 128
EPS = 1e-6
NEG_INF = -1e30
QB = 256
KW = 3 * QB

R_FF, R_IN, R_BR, R_WO = F // N_DEV, D_IN // N_DEV, D // N_DEV, D // N_DEV
OFF_IN, OFF_BR, OFF_WO = 0, R_IN, R_IN + R_BR
FFN_ROWS = 3 * R_FF
MIX_ROWS = R_IN + R_BR + R_WO

FC = 256
TM = 512
TM_FFN_BWD = 256
VMEM_LIMIT = 56 * 1024 * 1024

ADAM_LR, ADAM_B1, ADAM_B2, ADAM_EPS, ADAM_WD, ADAM_STEP = 0.001, 0.9, 0.999, 1e-08, 0.01, 10

ROW_LOSS, ROW_N1, ROW_NM, ROW_N2, ROW_NF, ROW_BG, ROW_LN, ROW_BS, ROW_REL, ROW_WS = 0, 1, 2, 3, 4, 5, 7, 8, 9, 24
SMALL_ROWS = ROW_WS + 128

MESH = pl.DeviceIdType.MESH
ANY = pl.BlockSpec(memory_space=pl.ANY)


def _nt(a, b):
    return lax.dot_general(a, b, (((1,), (1,)), ((), ())), preferred_element_type=F32)


def _tn(a, b):
    return lax.dot_general(a, b, (((0,), (0,)), ((), ())), preferred_element_type=F32)


def _nn(a, b):
    return jnp.dot(a, b, preferred_element_type=F32)


def _cparams(sem=None):
    return pltpu.CompilerParams(dimension_semantics=sem, vmem_limit_bytes=VMEM_LIMIT)


def _load_rows(gw_ref, dst, off, rows, sems):
    copies = [pltpu.make_async_copy(gw_ref.at[k, pl.ds(off, rows), :], dst.at[pl.ds(k * rows, rows), :], sems.at[k])
              for k in range(N_DEV)]
    for cp in copies:
        cp.start()
    return copies


def _rms(xv):
    r = lax.rsqrt(jnp.mean(xv * xv, axis=-1, keepdims=True) + EPS)
    return xv * r, r


def _rms_bwd(dh, xn, r, gain):
    dxn = dh * gain
    dx = r * (dxn - xn * jnp.mean(dxn * xn, axis=-1, keepdims=True))
    return dx, jnp.sum(dh * xn, axis=0, keepdims=True)


def _gelu(x):
    t = jnp.tanh(0.7978845608028654 * (x + 0.044715 * x * x * x))
    return 0.5 * x * (1.0 + t), t


def _gelu_grad(x, t):
    return 0.5 * (1.0 + t) + 0.5 * x * (1.0 - t * t) * 0.7978845608028654 * (1.0 + 3.0 * 0.044715 * x * x)


def _place():
    x, y, cc = lax.axis_index("x"), lax.axis_index("y"), lax.axis_index("c")
    return x, y, cc, [(1 - x, y), (x, 1 - y), (1 - x, 1 - y)]


class _Gather:
    def __init__(self, shard):
        self.inputs = [shard]
        self.out_shape = [jax.ShapeDtypeStruct((N_DEV,) + shard.shape, shard.dtype)]
        self.scratch = [pltpu.SemaphoreType.DMA((7,)), pltpu.SemaphoreType.DMA((7,)), pltpu.SemaphoreType.DMA]

    def _copies(self, ins, outs, scr):
        (x_ref,), (out_ref,), (send_sems, recv_sems, local_sem) = ins, outs, scr
        x, y, cc, chips = _place()

        def slab(px, py, pc):
            return out_ref.at[4 * px + 2 * py + pc]

        def copy(k, block, to, src=None):
            return pltpu.make_async_remote_copy(
                src_ref=slab(*block) if src is None else src, dst_ref=slab(*block),
                send_sem=send_sems.at[k], recv_sem=recv_sems.at[k], device_id=to, device_id_type=MESH)

        me, sibling = (x, y, cc), (x, y, 1 - cc)
        mine = pltpu.make_async_copy(x_ref, slab(*me), local_sem)
        first = [copy(0, me, sibling, src=x_ref)] + [copy(1 + j, me, (*chip, cc), src=x_ref) for j, chip in enumerate(chips)]
        landed = [copy(1 + j, (*chip, cc), me) for j, chip in enumerate(chips)]
        passed = [copy(4 + j, (*chip, cc), sibling) for j, chip in enumerate(chips)]
        from_sibling = [copy(0, sibling, me)] + [copy(4 + j, (*chip, 1 - cc), me) for j, chip in enumerate(chips)]
        return mine, first, landed, passed, from_sibling

    def begin(self, *refs):
        mine, first, _, _, _ = self._copies(*refs)
        mine.start()
        for cp in first:
            cp.start()

    def relay(self, *refs):
        _, _, landed, passed, _ = self._copies(*refs)
        for arrived, onward in zip(landed, passed):
            arrived.wait_recv()
            onward.start()

    def end(self, *refs):
        mine, first, _, passed, from_sibling = self._copies(*refs)
        for cp in from_sibling:
            cp.wait_recv()
        for cp in first + passed:
            cp.wait_send()
        mine.wait()


class _Direct:
    def begin(self, *refs):
        keep, give = self._copies(*refs)
        for cp in keep + give:
            cp.start()

    def relay(self, *refs):
        pass

    def end(self, *refs):
        keep, give = self._copies(*refs)
        for cp in give:
            cp.wait_recv()
        for cp in give:
            cp.wait_send()
        for cp in keep:
            cp.wait()


class _SiblingSwap(_Direct):
    def __init__(self, slabs):
        self.inputs = [slabs]
        self.out_shape = [jax.ShapeDtypeStruct((4,) + slabs.shape[1:], slabs.dtype)]
        self.scratch = [pltpu.SemaphoreType.DMA((4,)), pltpu.SemaphoreType.DMA((4,))]

    def _copies(self, ins, outs, scr):
        (g_ref,), (got_ref,), (send_sems, recv_sems) = ins, outs, scr
        x, y, cc, _ = _place()
        return [], [pltpu.make_async_remote_copy(src_ref=g_ref.at[2 * j + 1 - cc], dst_ref=got_ref.at[j], send_sem=send_sems.at[j],
                                                 recv_sem=recv_sems.at[j], device_id=(x, y, 1 - cc), device_id_type=MESH) for j in range(4)]


class _ChipScatter(_Direct):
    def __init__(self, sums):
        self.inputs = [sums]
        self.out_shape = [jax.ShapeDtypeStruct((3,) + sums.shape[1:], sums.dtype)]
        self.scratch = [pltpu.SemaphoreType.DMA((3,)), pltpu.SemaphoreType.DMA((3,))]

    def _copies(self, ins, outs, scr):
        (s_ref,), (got_ref,), (send_sems, recv_sems) = ins, outs, scr
        _, _, cc, chips = _place()
        return [], [pltpu.make_async_remote_copy(src_ref=s_ref.at[2 * px + py], dst_ref=got_ref.at[j], send_sem=send_sems.at[j],
                                                 recv_sem=recv_sems.at[j], device_id=(px, py, cc), device_id_type=MESH)
                    for j, (px, py) in enumerate(chips)]


class _AllToAll(_Direct):
    def __init__(self, block):
        self.inputs = [block]
        self.out_shape = [jax.ShapeDtypeStruct((N_DEV,) + block.shape, block.dtype)]
        self.scratch = [pltpu.SemaphoreType.DMA((7,)), pltpu.SemaphoreType.DMA((7,)), pltpu.SemaphoreType.DMA]

    def _copies(self, ins, outs, scr):
        (b_ref,), (got_ref,), (send_sems, recv_sems, local_sem) = ins, outs, scr
        x, y, cc, _ = _place()
        me = 4 * x + 2 * y + cc
        keep = [pltpu.make_async_copy(b_ref, got_ref.at[me], local_sem)]
        give = [pltpu.make_async_remote_copy(src_ref=b_ref, dst_ref=got_ref.at[me], send_sem=send_sems.at[k - 1], recv_sem=recv_sems.at[k - 1],
                                             device_id=(x ^ ((k >> 2) & 1), y ^ ((k >> 1) & 1), cc ^ (k & 1)), device_id_type=MESH)
                for k in range(1, N_DEV)]
        return keep, give


def _split_refs(refs, counts):
    out, pos = [], 0
    for n in counts:
        out.append(list(refs[pos:pos + n]))
        pos += n
    return out


def _bind(comms, c_in, c_out, c_scr):
    ins = _split_refs(c_in, [len(c.inputs) for c in comms])
    outs = _split_refs(c_out, [len(c.out_shape) for c in comms])
    scr = _split_refs(c_scr, [len(c.scratch) for c in comms])
    return [(c, (i, o, s)) for c, i, o, s in zip(comms, ins, outs, scr)]


def _call(body, *, name, grid, in_specs, out_specs, out_shape, scratch_shapes, args, comms=()):
    c_in = [a for c in comms for a in c.inputs]
    c_out = [s for c in comms for s in c.out_shape]
    c_scr = [s for c in comms for s in c.scratch]
    counts = [len(in_specs), len(c_in), len(out_shape), len(c_out), len(scratch_shapes), len(c_scr)]
    last = grid[0] - 1

    def full(*refs):
        ins, cin, outs, cout, scr, cscr = _split_refs(refs, counts)
        bound = _bind(comms, cin, cout, cscr)
        step = pl.program_id(0)
        if comms:
            @pl.when(step == 0)
            def _():
                for c, r in bound:
                    c.begin(*r)

            @pl.when(step == last)
            def _():
                for c, r in bound:
                    c.relay(*r)

        body(*ins, *outs, *scr)
        if comms:
            @pl.when(step == last)
            def _():
                for c, r in bound:
                    c.end(*r)

    res = pl.pallas_call(
        full, name=name, grid=grid,
        in_specs=list(in_specs) + [ANY] * len(c_in), out_specs=list(out_specs) + [ANY] * len(c_out),
        out_shape=list(out_shape) + c_out, scratch_shapes=list(scratch_shapes) + c_scr,
        compiler_params=_cparams(("arbitrary",)),
    )(*args, *c_in)
    return res[:len(out_shape)], res[len(out_shape):]


def _comm_only(comms, name):
    c_in = [a for c in comms for a in c.inputs]
    c_out = [s for c in comms for s in c.out_shape]
    c_scr = [s for c in comms for s in c.scratch]

    def full(*refs):
        cin, cout, cscr = _split_refs(refs, [len(c_in), len(c_out), len(c_scr)])
        bound = _bind(comms, cin, cout, cscr)
        for phase in ("begin", "relay", "end"):
            for c, r in bound:
                getattr(c, phase)(*r)

    return pl.pallas_call(full, name=name, in_specs=[ANY] * len(c_in), out_specs=[ANY] * len(c_out), out_shape=c_out,
                          scratch_shapes=c_scr)(*c_in)


def _my_index(*axes_and_weights):
    return sum(w * lax.axis_index(a) for a, w in axes_and_weights).astype(jnp.int32).reshape(1)


def _pair_sum(slabs, got):
    _, r, d = got.shape

    def body(c_ref, a_ref, b_ref, o_ref):
        o_ref[...] = (a_ref[...].astype(F32) + b_ref[...].astype(F32)).astype(BF16)

    tile = pl.BlockSpec((1, r // 2, d), lambda j, i, c_ref: (j, i, 0))
    mine = pl.BlockSpec((1, None, r // 2, d), lambda j, i, c_ref: (j, c_ref[0], i, 0))
    return pl.pallas_call(
        body, name="pair_sum",
        grid_spec=pltpu.PrefetchScalarGridSpec(num_scalar_prefetch=1, grid=(4, 2), in_specs=[mine, tile], out_specs=tile),
        out_shape=jax.ShapeDtypeStruct(got.shape, BF16),
        compiler_params=_cparams(("arbitrary", "arbitrary")))(_my_index(("c", 1)), slabs.reshape(4, 2, r, d), got)


def _ffn_fwd(x, gain, gw_gu, gw_d, off_d, name, comms=()):
    t = x.shape[0]

    def body(x_ref, g_ref, gu_ref, d_ref, o_ref, wg, wu, wd, sems):
        @pl.when(pl.program_id(0) == 0)
        def _():
            cps = _load_rows(gu_ref, wg, 0, R_FF, sems.at[0]) + _load_rows(gu_ref, wu, R_FF, R_FF, sems.at[1]) \
                + _load_rows(d_ref, wd, off_d, R_FF, sems.at[2])
            for cp in cps:
                cp.wait()

        xv = x_ref[...]
        xn, _ = _rms(xv)
        h = (xn * g_ref[...]).astype(BF16)
        acc = jnp.zeros((TM, D), F32)
        for c in range(F // FC):
            rows = pl.ds(c * FC, FC)
            a = _nt(h, wg[rows, :])
            b = _nt(h, wu[rows, :])
            s = (a * jax.nn.sigmoid(a) * b).astype(BF16)
            acc = acc + _nn(s, wd[rows, :])
        o_ref[...] = xv + 0.5 * acc

    tile = pl.BlockSpec((TM, D), lambda i: (i, 0))
    (out,), got = _call(
        body, name=name, grid=(t // TM,),
        in_specs=[tile, pl.BlockSpec((1, D), lambda i: (0, 0)), ANY, ANY], out_specs=[tile],
        out_shape=[jax.ShapeDtypeStruct((t, D), F32)],
        scratch_shapes=[pltpu.VMEM((F, D), BF16)] * 3 + [pltpu.SemaphoreType.DMA((3, N_DEV))],
        args=(x, gain, gw_gu, gw_d), comms=comms)
    return out, got


def _ffn_bwd(x, gain, dout, gw_gu, gw_d, off_d, name, comms=()):
    t = x.shape[0]

    def body(x_ref, g_ref, do_ref, gu_ref, d_ref, dx_ref, dg_ref, h_ref, dab_ref, s_ref, df_ref, wg, wu, wd, sems):
        @pl.when(pl.program_id(0) == 0)
        def _():
            cps = _load_rows(gu_ref, wg, 0, R_FF, sems.at[0]) + _load_rows(gu_ref, wu, R_FF, R_FF, sems.at[1]) \
                + _load_rows(d_ref, wd, off_d, R_FF, sems.at[2])
            dg_ref[...] = jnp.zeros_like(dg_ref)
            for cp in cps:
                cp.wait()

        xv = x_ref[...]
        gain_v = g_ref[...]
        xn, r = _rms(xv)
        h = (xn * gain_v).astype(BF16)
        do = do_ref[...]
        df = (0.5 * do).astype(BF16)
        dh = jnp.zeros((TM_FFN_BWD, D), F32)
        for c in range(F // FC):
            rows = pl.ds(c * FC, FC)
            a = _nt(h, wg[rows, :])
            b = _nt(h, wu[rows, :])
            sg = jax.nn.sigmoid(a)
            sl = a * sg
            ds = _nt(df, wd[rows, :])
            da = (ds * b * (sg * (1.0 + a * (1.0 - sg)))).astype(BF16)
            db = (ds * sl).astype(BF16)
            dh = dh + _nn(da, wg[rows, :]) + _nn(db, wu[rows, :])
            dab_ref[:, c * FC:(c + 1) * FC] = da
            dab_ref[:, F + c * FC:F + (c + 1) * FC] = db
            s_ref[:, c * FC:(c + 1) * FC] = (sl * b).astype(BF16)
        dxn, dg = _rms_bwd(dh, xn, r, gain_v)
        dg_ref[...] += dg
        dx_ref[...] = do + dxn
        h_ref[...] = h
        df_ref[...] = df

    def tile(w):
        return pl.BlockSpec((TM_FFN_BWD, w), lambda i: (i, 0))

    row = pl.BlockSpec((1, D), lambda i: (0, 0))
    return _call(
        body, name=name, grid=(t // TM_FFN_BWD,),
        in_specs=[tile(D), row, tile(D), ANY, ANY],
        out_specs=[tile(D), row, tile(D), tile(2 * F), tile(F), tile(D)],
        out_shape=[jax.ShapeDtypeStruct((t, D), F32), jax.ShapeDtypeStruct((1, D), F32), jax.ShapeDtypeStruct((t, D), BF16),
                   jax.ShapeDtypeStruct((t, 2 * F), BF16), jax.ShapeDtypeStruct((t, F), BF16), jax.ShapeDtypeStruct((t, D), BF16)],
        scratch_shapes=[pltpu.VMEM((F, D), BF16)] * 3 + [pltpu.SemaphoreType.DMA((3, N_DEV))],
        args=(x, gain, dout, gw_gu, gw_d), comms=comms)


def _weight_grad(a, b, name, tmm=256):
    t, m = a.shape
    n = b.shape[1]

    def body(a_ref, b_ref, o_ref):
        o_ref[...] = _tn(a_ref[...], b_ref[...]).astype(BF16)

    return pl.pallas_call(
        body, name=name, grid=(m // tmm,),
        in_specs=[pl.BlockSpec((t, tmm), lambda i: (0, i)), pl.BlockSpec((t, n), lambda i: (0, 0))],
        out_specs=pl.BlockSpec((tmm, n), lambda i: (i, 0)),
        out_shape=jax.ShapeDtypeStruct((m, n), BF16),
        compiler_params=_cparams(("arbitrary",)),
    )(a, b)


def _mix_proj_fwd(x, gain, b_gate, gw, comms=()):
    t = x.shape[0]

    def body(x_ref, g_ref, bg_ref, gw_ref, q_ref, k_ref, v_ref, zs_ref, gt_ref, h_ref, win, sems):
        @pl.when(pl.program_id(0) == 0)
        def _():
            for cp in _load_rows(gw_ref, win, OFF_IN, R_IN, sems):
                cp.wait()

        xn, _ = _rms(x_ref[...])
        h = (xn * g_ref[...]).astype(BF16)
        h_ref[...] = h
        q_ref[...] = (_nt(h, win[0:512, :]) * 0.125).astype(BF16)
        k_ref[...] = _nt(h, win[512:1024, :]).astype(BF16)
        v_ref[...] = _nt(h, win[1024:1536, :]).astype(BF16)
        for c in range(2):
            zs_ref[:, c * 512:(c + 1) * 512] = _nt(h, win[1536 + c * 512:2048 + c * 512, :]).astype(BF16)
        for c in range(4):
            zg = _nt(h, win[2560 + c * 512:3072 + c * 512, :]) + bg_ref[:, c * 512:(c + 1) * 512]
            gt_ref[:, c * 512:(c + 1) * 512] = jax.nn.sigmoid(zg).astype(BF16)

    def tile(w):
        return pl.BlockSpec((TM, w), lambda i: (i, 0))

    return _call(
        body, name="mix_proj_fwd", grid=(t // TM,),
        in_specs=[tile(D), pl.BlockSpec((1, D), lambda i: (0, 0)), pl.BlockSpec((1, 2 * D), lambda i: (0, 0)), ANY],
        out_specs=[tile(D_ATT), tile(D_ATT), tile(D_ATT), tile(2 * D_SGU), tile(2 * D), tile(D)],
        out_shape=[jax.ShapeDtypeStruct((t, D_ATT), BF16)] * 3 + [jax.ShapeDtypeStruct((t, 2 * D_SGU), BF16),
                                                                   jax.ShapeDtypeStruct((t, 2 * D), BF16),
                                                                   jax.ShapeDtypeStruct((t, D), BF16)],
        scratch_shapes=[pltpu.VMEM((D_IN, D), BF16), pltpu.SemaphoreType.DMA((N_DEV,))],
        args=(x, gain, b_gate, gw), comms=comms)


def _mix_proj_bwd(dz, x, gain, dres, gw):
    t = x.shape[0]

    def body(dz_ref, x_ref, g_ref, dr_ref, gw_ref, dx_ref, dg_ref, win, sems):
        @pl.when(pl.program_id(0) == 0)
        def _():
            cps = _load_rows(gw_ref, win, OFF_IN, R_IN, sems)
            dg_ref[...] = jnp.zeros_like(dg_ref)
            for cp in cps:
                cp.wait()

        dh = jnp.zeros((TM, D), F32)
        for c in range(D_IN // 512):
            dh = dh + _nn(dz_ref[:, c * 512:(c + 1) * 512], win[c * 512:(c + 1) * 512, :])
        xn, r = _rms(x_ref[...])
        dxn, dg = _rms_bwd(dh, xn, r, g_ref[...])
        dg_ref[...] += dg
        dx_ref[...] = dr_ref[...] + dxn

    def tile(w):
        return pl.BlockSpec((TM, w), lambda i: (i, 0))

    row = pl.BlockSpec((1, D), lambda i: (0, 0))
    return pl.pallas_call(
        body, name="mix_proj_bwd", grid=(t // TM,),
        in_specs=[tile(D_IN), tile(D), row, tile(D), ANY], out_specs=[tile(D), row],
        out_shape=[jax.ShapeDtypeStruct((t, D), F32), jax.ShapeDtypeStruct((1, D), F32)],
        scratch_shapes=[pltpu.VMEM((D_IN, D), BF16), pltpu.SemaphoreType.DMA((N_DEV,))],
        compiler_params=_cparams(("arbitrary",)),
    )(dz, x, gain, dres, gw)


SKEW_W = KW + QB
N_CAP = 2 * QB - REL_CLIP + 1


def _band_bias(rel_bias):
    cap = rel_bias[:, 2 * REL_CLIP:]
    diag = jnp.concatenate([jnp.broadcast_to(cap, (HEADS, N_CAP)), rel_bias[:, 2 * REL_CLIP - 1::-1],
                            jnp.broadcast_to(cap, (HEADS, SKEW_W - N_CAP - 2 * REL_CLIP))], axis=1)

    def body(d_ref, o_ref):
        lag = lax.broadcasted_iota(jnp.int32, (QB, KW), 1) // CHUNK - lax.broadcasted_iota(jnp.int32, (QB, KW), 0) // CHUNK
        band = (lag >= 0) & (lag <= N_LEFT)
        for h in range(HEADS):
            rows = jnp.broadcast_to(d_ref[h:h + 1, :], (QB, SKEW_W))
            o_ref[h] = jnp.where(band, pltpu.roll(rows, 0, 1, stride=1, stride_axis=0)[:, :KW], NEG_INF)

    return pl.pallas_call(body, name="band_bias", out_shape=jax.ShapeDtypeStruct((HEADS, QB, KW), F32))(diag)


def _att_specs():
    qspec = pl.BlockSpec((QB, D_ATT), lambda g: (g, 0))
    kspecs = [pl.BlockSpec((QB, D_ATT), lambda g: (jnp.maximum(g - 2, 0), 0)),
              pl.BlockSpec((QB, D_ATT), lambda g: (jnp.maximum(g - 1, 0), 0)), qspec]
    bspec = pl.BlockSpec((HEADS, QB, KW), lambda g: (0, 0, 0))
    return qspec, kspecs, bspec


def _att_probs(qm, kp, bias, valid):
    s = jnp.where(valid, _nt(qm, kp) + bias, NEG_INF)
    e = jnp.exp(s - jnp.max(s, axis=-1, keepdims=True))
    return e / jnp.sum(e, axis=-1, keepdims=True)


def _att_valid():
    g = pl.program_id(0)
    blk = lax.broadcasted_iota(jnp.int32, (QB, KW), 1) // QB
    return (blk + g) >= 2


def _att_fwd(q, k, v, bias, comms=()):
    t = q.shape[0]

    def body(q_ref, k0, k1, k2, v0, v1, v2, b_ref, y_ref):
        valid = _att_valid()
        first = lax.broadcasted_iota(jnp.int32, (1, 128), 1) < 64
        for p in range(HEADS // 2):
            lanes = slice(p * 128, (p + 1) * 128)
            qp = q_ref[:, lanes]
            kp = jnp.concatenate([k0[:, lanes], k1[:, lanes], k2[:, lanes]], axis=0)
            vp = jnp.concatenate([v0[:, lanes], v1[:, lanes], v2[:, lanes]], axis=0)
            out = jnp.zeros((QB, 128), F32)
            for hh in range(2):
                mask = first if hh == 0 else jnp.logical_not(first)
                pr = _att_probs(jnp.where(mask, qp, 0), kp, b_ref[2 * p + hh], valid)
                out = out + _nn(pr.astype(BF16), jnp.where(mask, vp, 0))
            y_ref[:, lanes] = out.astype(BF16)

    qspec, kspecs, bspec = _att_specs()
    (out,), got = _call(
        body, name="att_fwd", grid=(t // QB,),
        in_specs=[qspec] + kspecs + kspecs + [bspec], out_specs=[qspec],
        out_shape=[jax.ShapeDtypeStruct((t, D_ATT), BF16)], scratch_shapes=[],
        args=(q, k, k, k, v, v, v, bias), comms=comms)
    return out, got


def _att_bwd(q, k, v, bias, dy, comms=()):
    t = q.shape[0]
    n_blocks = t // QB

    def body(q_ref, k0, k1, k2, v0, v1, v2, b_ref, dy_ref, dq_ref, dk_ref, dv_ref, db_ref, dk_acc, dv_acc):
        g = pl.program_id(0)

        @pl.when(g == 0)
        def _():
            db_ref[...] = jnp.zeros_like(db_ref)
            dk_acc[...] = jnp.zeros_like(dk_acc)
            dv_acc[...] = jnp.zeros_like(dv_acc)

        valid = _att_valid()
        first = lax.broadcasted_iota(jnp.int32, (1, 128), 1) < 64
        for p in range(HEADS // 2):
            lanes = slice(p * 128, (p + 1) * 128)
            qp = q_ref[:, lanes]
            dyp = dy_ref[:, lanes]
            kp = jnp.concatenate([k0[:, lanes], k1[:, lanes], k2[:, lanes]], axis=0)
            vp = jnp.concatenate([v0[:, lanes], v1[:, lanes], v2[:, lanes]], axis=0)
            dq = jnp.zeros((QB, 128), F32)
            dk = jnp.zeros((KW, 128), F32)
            dv = jnp.zeros((KW, 128), F32)
            for hh in range(2):
                mask = first if hh == 0 else jnp.logical_not(first)
                qm = jnp.where(mask, qp, 0)
                dym = jnp.where(mask, dyp, 0)
                pr = _att_probs(qm, kp, b_ref[2 * p + hh], valid)
                dp = _nt(dym, vp)
                ds = pr * (dp - jnp.sum(dp * pr, axis=-1, keepdims=True))
                db_ref[2 * p + hh] += ds
                dsb = ds.astype(BF16)
                dq = dq + _nn(dsb, jnp.where(mask, kp, 0))
                dk = dk + _tn(dsb, qm)
                dv = dv + _tn(pr.astype(BF16), dym)
            dq_ref[:, lanes] = (dq * 0.125).astype(BF16)
            for j in range(3):
                rows = pl.ds(pl.multiple_of(jnp.maximum(g - 2 + j, 0) * QB, QB), QB)
                dk_acc[rows, lanes] += dk[j * QB:(j + 1) * QB]
                dv_acc[rows, lanes] += dv[j * QB:(j + 1) * QB]

        @pl.when(g == n_blocks - 1)
        def _():
            dk_ref[...] = dk_acc[...].astype(BF16)
            dv_ref[...] = dv_acc[...].astype(BF16)

    qspec, kspecs, bspec = _att_specs()
    full = pl.BlockSpec((t, D_ATT), lambda g: (0, 0))
    return _call(
        body, name="att_bwd", grid=(n_blocks,),
        in_specs=[qspec] + kspecs + kspecs + [bspec, qspec], out_specs=[qspec, full, full, bspec],
        out_shape=[jax.ShapeDtypeStruct((t, D_ATT), BF16)] * 3 + [jax.ShapeDtypeStruct((HEADS, QB, KW), F32)],
        scratch_shapes=[pltpu.VMEM((t, D_ATT), F32)] * 2,
        args=(q, k, k, k, v, v, v, bias, dy), comms=comms)


def _rel_bias_grad(dbias):
    def body(db_ref, cs_ref, tot_ref):
        lane = lax.broadcasted_iota(jnp.int32, (1, SKEW_W), 1)
        capped = (lane < N_CAP) | (lane > KW)
        pad = jnp.zeros((8, QB), F32)
        for h in range(HEADS):
            def add_rows(a, z):
                rows = jnp.concatenate([db_ref[h, pl.ds(pl.multiple_of(8 * a, 8), 8), :], pad], axis=1)
                return z + pltpu.roll(rows, (SKEW_W - 8 * a) % SKEW_W, 1)

            z = lax.fori_loop(0, QB // 8, add_rows, jnp.zeros((8, SKEW_W), F32))
            cs = z[0:1, :]
            for b in range(1, 8):
                cs = cs + pltpu.roll(z[b:b + 1, :], SKEW_W - b, 1)
            cs_ref[h:h + 1, :] = cs
            tot_ref[h:h + 1, :] = jnp.broadcast_to(jnp.sum(jnp.where(capped, cs, 0.0), axis=1, keepdims=True), (1, 128))

    cs, tot = pl.pallas_call(
        body, name="rel_bias_grad",
        out_shape=[jax.ShapeDtypeStruct((HEADS, SKEW_W), F32), jax.ShapeDtypeStruct((HEADS, 128), F32)],
    )(dbias)
    return jnp.concatenate([cs[:, KW:N_CAP - 1:-1], tot[:, :1]], axis=1)


def _sgu_mask():
    pos = np.arange(SGU_BLOCK)
    return (pos[:, None] // CHUNK) >= (pos[None, :] // CHUNK)


def _group_stack(blk, first):
    return jnp.concatenate([jnp.where(first, blk, 0), jnp.where(first, 0, blk)], axis=0)


def _sgu_norm(zs_ref, lng, lnb):
    zs = zs_ref[...].astype(F32)
    ga, th = _gelu(zs)
    u, vs = ga[:, :D_SGU], ga[:, D_SGU:]
    mu = jnp.mean(vs, axis=-1, keepdims=True)
    cen = vs - mu
    rstd = lax.rsqrt(jnp.mean(cen * cen, axis=-1, keepdims=True) + EPS)
    xhat = cen * rstd
    return zs, th, u, xhat, rstd, xhat * lng + lnb


def _sgu_mix(vb, wm2_ref, bsx, s_ref):
    first = lax.broadcasted_iota(jnp.int32, (1, 128), 1) < 64
    for n in range(TM // SGU_BLOCK):
        for p in range(4):
            blk = vb[n * 128:(n + 1) * 128, p * 128:(p + 1) * 128]
            s_ref[n * 128:(n + 1) * 128, p * 128:(p + 1) * 128] = _nn(wm2_ref[p], _group_stack(blk, first)) + bsx[:, p * 128:(p + 1) * 128]


def _merge_fwd(x, zs, gt, y_att, lng, lnb, wm2, bsx, gw):
    t = x.shape[0]

    def body(x_ref, zs_ref, gt_ref, ya_ref, lng_ref, lnb_ref, wm2_ref, bsx_ref, gw_ref, xo_ref, ys_ref, mg_ref,
             wbr, wo, s_scr, sems):
        @pl.when(pl.program_id(0) == 0)
        def _():
            for cp in _load_rows(gw_ref, wbr, OFF_BR, R_BR, sems.at[0]) + _load_rows(gw_ref, wo, OFF_WO, R_WO, sems.at[1]):
                cp.wait()

        _, _, u, _, _, vsn = _sgu_norm(zs_ref, lng_ref[...], lnb_ref[...])
        _sgu_mix(vsn.astype(BF16), wm2_ref, bsx_ref[...], s_scr)
        ys = (u * s_scr[...]).astype(BF16)
        ys_ref[...] = ys
        pa = _nt(ya_ref[...], wbr[:, :D_ATT])
        ps = _nt(ys, wbr[:, D_ATT:])
        mg = (gt_ref[:, :D].astype(F32) * pa + gt_ref[:, D:].astype(F32) * ps).astype(BF16)
        mg_ref[...] = mg
        xo_ref[...] = x_ref[...] + _nn(mg, wo[...])

    def tile(w):
        return pl.BlockSpec((TM, w), lambda i: (i, 0))

    def const(shape):
        return pl.BlockSpec(shape, lambda i: (0,) * len(shape))

    return pl.pallas_call(
        body, name="merge_fwd", grid=(t // TM,),
        in_specs=[tile(D), tile(2 * D_SGU), tile(2 * D), tile(D_ATT), const((1, D_SGU)), const((1, D_SGU)),
                  const((4, 128, 256)), const((128, D_SGU)), ANY],
        out_specs=[tile(D), tile(D_SGU), tile(D)],
        out_shape=[jax.ShapeDtypeStruct((t, D), F32), jax.ShapeDtypeStruct((t, D_SGU), BF16), jax.ShapeDtypeStruct((t, D), BF16)],
        scratch_shapes=[pltpu.VMEM((D, D), BF16), pltpu.VMEM((D, D), BF16), pltpu.VMEM((TM, D_SGU), F32),
                        pltpu.SemaphoreType.DMA((2, N_DEV))],
        compiler_params=_cparams(("arbitrary",)),
    )(x, zs, gt, y_att, lng, lnb, wm2, bsx, gw)


def _merge_bwd(dx, gt, y_att, y_sgu, gw):
    t = dx.shape[0]

    def body(dx_ref, gt_ref, ya_ref, ys_ref, gw_ref, dzg_ref, dya_ref, dys_ref, dpp_ref, dxb_ref, dbg_ref, wbr, wo, sems):
        @pl.when(pl.program_id(0) == 0)
        def _():
            cps = _load_rows(gw_ref, wbr, OFF_BR, R_BR, sems.at[0]) + _load_rows(gw_ref, wo, OFF_WO, R_WO, sems.at[1])
            dbg_ref[...] = jnp.zeros_like(dbg_ref)
            for cp in cps:
                cp.wait()

        dxb = dx_ref[...].astype(BF16)
        dxb_ref[...] = dxb
        dm = _nt(dxb, wo[...])
        for half, y_ref, w in ((0, ya_ref, wbr.at[:, :D_ATT]), (1, ys_ref, wbr.at[:, D_ATT:])):
            cols = slice(half * D, (half + 1) * D)
            gate = gt_ref[:, cols].astype(F32)
            branch = _nt(y_ref[...], w[...])
            dzg = dm * branch * gate * (1.0 - gate)
            dbg_ref[:, cols] += jnp.sum(dzg, axis=0, keepdims=True)
            dzg_ref[:, cols] = dzg.astype(BF16)
            dbr = (dm * gate).astype(BF16)
            dpp_ref[:, cols] = dbr
            dy = _nn(dbr, w[...])
            if half == 0:
                dya_ref[...] = dy.astype(BF16)
            else:
                dys_ref[...] = dy

    def tile(w):
        return pl.BlockSpec((TM, w), lambda i: (i, 0))

    return pl.pallas_call(
        body, name="merge_bwd", grid=(t // TM,),
        in_specs=[tile(D), tile(2 * D), tile(D_ATT), tile(D_SGU), ANY],
        out_specs=[tile(2 * D), tile(D_ATT), tile(D_SGU), tile(2 * D), tile(D), pl.BlockSpec((1, 2 * D), lambda i: (0, 0))],
        out_shape=[jax.ShapeDtypeStruct((t, 2 * D), BF16), jax.ShapeDtypeStruct((t, D_ATT), BF16), jax.ShapeDtypeStruct((t, D_SGU), F32),
                   jax.ShapeDtypeStruct((t, 2 * D), BF16), jax.ShapeDtypeStruct((t, D), BF16), jax.ShapeDtypeStruct((1, 2 * D), F32)],
        scratch_shapes=[pltpu.VMEM((D, D), BF16), pltpu.VMEM((D, D), BF16), pltpu.SemaphoreType.DMA((2, N_DEV))],
        compiler_params=_cparams(("arbitrary",)),
    )(dx, gt, y_att, y_sgu, gw)


def _sgu_bwd(zs, dys, lng, lnb, wm2, wmt2, bsx):
    t = zs.shape[0]
    n_steps = t // TM

    def body(zs_ref, dys_ref, lng_ref, lnb_ref, wm2_ref, wmt2_ref, bsx_ref, dzs_ref, dw_ref, dbs_ref, dlg_ref, dlb_ref,
             s_scr, dv_scr, ds_acc):
        i = pl.program_id(0)

        @pl.when(i == 0)
        def _():
            dw_ref[...] = jnp.zeros_like(dw_ref)
            dlg_ref[...] = jnp.zeros_like(dlg_ref)
            dlb_ref[...] = jnp.zeros_like(dlb_ref)
            ds_acc[...] = jnp.zeros_like(ds_acc)

        lng = lng_ref[...]
        zs, th, u, xhat, rstd, vsn = _sgu_norm(zs_ref, lng, lnb_ref[...])
        vb = vsn.astype(BF16)
        _sgu_mix(vb, wm2_ref, bsx_ref[...], s_scr)
        dys = dys_ref[...]
        du = dys * s_scr[...]
        ds = dys * u
        dsb = ds.astype(BF16)
        first = lax.broadcasted_iota(jnp.int32, (1, 128), 1) < 64
        acc = jnp.zeros((SGU_BLOCK, D_SGU), F32)
        for n in range(TM // SGU_BLOCK):
            rows = slice(n * 128, (n + 1) * 128)
            acc = acc + ds[rows]
            for p in range(4):
                lanes = slice(p * 128, (p + 1) * 128)
                stack = _group_stack(dsb[rows, lanes], first)
                dv_scr[rows, lanes] = _nn(wmt2_ref[p], stack)
                dw_ref[p] += _nt(stack, vb[rows, lanes])
        ds_acc[...] += acc
        dvsn = dv_scr[...]
        dlg_ref[...] += jnp.sum(dvsn * xhat, axis=0, keepdims=True)
        dlb_ref[...] += jnp.sum(dvsn, axis=0, keepdims=True)
        dxh = dvsn * lng
        dvs = rstd * (dxh - jnp.mean(dxh, axis=-1, keepdims=True) - xhat * jnp.mean(dxh * xhat, axis=-1, keepdims=True))
        dga = jnp.concatenate([du, dvs], axis=1)
        dzs_ref[...] = (dga * _gelu_grad(zs, th)).astype(BF16)

        @pl.when(i == n_steps - 1)
        def _():
            r = lax.broadcasted_iota(jnp.int32, (256, 128), 0) % SGU_BLOCK
            c = lax.broadcasted_iota(jnp.int32, (256, 128), 1)
            keep = (r // CHUNK) >= (c // CHUNK)
            for p in range(4):
                dw_ref[p] = jnp.where(keep, dw_ref[p], 0.0)
            total = ds_acc[...]
            grp = lax.broadcasted_iota(jnp.int32, (SGU_BLOCK, D_SGU), 1) // 64
            lane = lax.broadcasted_iota(jnp.int32, (SGU_BLOCK, 128), 1)
            out = jnp.zeros((SGU_BLOCK, 128), F32)
            for gi in range(8):
                out = jnp.where(lane == gi, jnp.sum(jnp.where(grp == gi, total, 0.0), axis=1, keepdims=True), out)
            dbs_ref[...] = out

    def tile(w):
        return pl.BlockSpec((TM, w), lambda i: (i, 0))

    def const(shape):
        return pl.BlockSpec(shape, lambda i: (0,) * len(shape))

    return pl.pallas_call(
        body, name="sgu_bwd", grid=(n_steps,),
        in_specs=[tile(2 * D_SGU), tile(D_SGU), const((1, D_SGU)), const((1, D_SGU)), const((4, 128, 256)), const((4, 128, 256)),
                  const((128, D_SGU))],
        out_specs=[tile(2 * D_SGU), const((4, 256, 128)), const((128, 128)), const((1, D_SGU)), const((1, D_SGU))],
        out_shape=[jax.ShapeDtypeStruct((t, 2 * D_SGU), BF16), jax.ShapeDtypeStruct((4, 256, 128), F32),
                   jax.ShapeDtypeStruct((128, 128), F32), jax.ShapeDtypeStruct((1, D_SGU), F32), jax.ShapeDtypeStruct((1, D_SGU), F32)],
        scratch_shapes=[pltpu.VMEM((TM, D_SGU), F32), pltpu.VMEM((TM, D_SGU), F32), pltpu.VMEM((SGU_BLOCK, D_SGU), F32)],
        compiler_params=_cparams(("arbitrary",)),
    )(zs, dys, lng, lnb, wm2, wmt2, bsx)


def _loss_head(x, gain, target):
    t = x.shape[0]

    def body(x_ref, g_ref, t_ref, dx_ref, dg_ref, loss_ref):
        @pl.when(pl.program_id(0) == 0)
        def _():
            dg_ref[...] = jnp.zeros_like(dg_ref)
            loss_ref[...] = jnp.zeros_like(loss_ref)

        gain_v = g_ref[...]
        xn, r = _rms(x_ref[...])
        err = xn * gain_v - t_ref[...]
        loss_ref[...] += 0.5 * jnp.sum(jnp.mean(err * err, axis=-1, keepdims=True), axis=0, keepdims=True)
        dxn, dg = _rms_bwd(err * (1.0 / D), xn, r, gain_v)
        dg_ref[...] += dg
        dx_ref[...] = dxn

    tile = pl.BlockSpec((TM, D), lambda i: (i, 0))
    row = pl.BlockSpec((1, D), lambda i: (0, 0))
    return pl.pallas_call(
        body, name="loss_head", grid=(t // TM,),
        in_specs=[tile, row, tile], out_specs=[tile, row, pl.BlockSpec((1, 128), lambda i: (0, 0))],
        out_shape=[jax.ShapeDtypeStruct((t, D), F32), jax.ShapeDtypeStruct((1, D), F32), jax.ShapeDtypeStruct((1, 128), F32)],
        compiler_params=_cparams(("arbitrary",)),
    )(x, gain, target)


def _adamw(g, w, m, v):
    m = ADAM_B1 * m + (1.0 - ADAM_B1) * g
    v = ADAM_B2 * v + (1.0 - ADAM_B2) * (g * g)
    m_hat = m / (1.0 - ADAM_B1 ** ADAM_STEP)
    v_hat = v / (1.0 - ADAM_B2 ** ADAM_STEP)
    return -ADAM_LR * (m_hat / (jnp.sqrt(v_hat) + ADAM_EPS) + ADAM_WD * w), m, v


def _sum_adamw(parts, sums, w, m, v, rows_per_step, name):
    rows = w.shape[0]
    n_parts = parts.shape[0]

    def body(own_ref, p_ref, s_ref, w_ref, m_ref, v_ref, g_ref, d_ref, mo_ref, vo_ref):
        g = p_ref[0].astype(F32)
        for k in range(1, n_parts):
            g = g + p_ref[k].astype(F32)
        g = g + s_ref[...].astype(F32)
        g_ref[...] = g
        d_ref[...], mo_ref[...], vo_ref[...] = _adamw(g, w_ref[...], m_ref[...], v_ref[...])

    tile = pl.BlockSpec((rows_per_step, D), lambda i, own: (i, 0))
    return pl.pallas_call(
        body, name=name,
        grid_spec=pltpu.PrefetchScalarGridSpec(
            num_scalar_prefetch=1, grid=(rows // rows_per_step,),
            in_specs=[pl.BlockSpec((n_parts, rows_per_step, D), lambda i, own: (0, i, 0)),
                      pl.BlockSpec((None, rows_per_step, D), lambda i, own: (own[0], i, 0)), tile, tile, tile],
            out_specs=[tile] * 4),
        out_shape=[jax.ShapeDtypeStruct((rows, D), F32)] * 4,
        compiler_params=_cparams(("arbitrary",)),
    )(_my_index(("x", 2), ("y", 1)), parts, sums, w, m, v)


def _sum_adamw_small(parts, w, m, v):
    def body(p_ref, w_ref, m_ref, v_ref, g_ref, d_ref, mo_ref, vo_ref):
        g = p_ref[0]
        for k in range(1, N_DEV):
            g = g + p_ref[k]
        g_ref[...] = g
        d_ref[...], mo_ref[...], vo_ref[...] = _adamw(g, w_ref[...], m_ref[...], v_ref[...])

    return pl.pallas_call(body, name="adamw_small", out_shape=[jax.ShapeDtypeStruct(w.shape, F32)] * 4,
                          compiler_params=_cparams())(parts, w, m, v)


def _ffn_rows(p, ffn, pre=""):
    return jnp.concatenate([p[pre + ffn + "_w_gate"][0].T, p[pre + ffn + "_w_up"][0].T, p[pre + ffn + "_w_down"][0]], axis=0)


def _mix_rows(p, pre=""):
    return jnp.concatenate([p[pre + "w_in"][0].T, jnp.concatenate([p[pre + "w_branch_att"][0].T, p[pre + "w_branch_sgu"][0].T], axis=1),
                            p[pre + "w_out"][0]], axis=0)


def _from_ffn_rows(slab, ffn):
    return {ffn + "_w_gate": slab[:R_FF].T[None], ffn + "_w_up": slab[R_FF:2 * R_FF].T[None], ffn + "_w_down": slab[2 * R_FF:][None]}


def _from_mix_rows(slab):
    br = slab[OFF_BR:OFF_WO]
    return {"w_in": slab[:R_IN].T[None], "w_branch_att": br[:, :D_ATT].T[None], "w_branch_sgu": br[:, D_ATT:].T[None],
            "w_out": slab[OFF_WO:][None]}


def _split(a, rows):
    return a.reshape(N_DEV, rows, a.shape[-1])


def _ffn_grad_slabs(g_gate_up, g_down):
    return jnp.concatenate([_split(g_gate_up[:F], R_FF), _split(g_gate_up[F:], R_FF), _split(g_down, R_FF)], axis=1)


def _mix_grad_slabs(g_in, g_ba, g_bs, g_wo):
    return jnp.concatenate([_split(g_in, R_IN), jnp.concatenate([_split(g_ba, R_BR), _split(g_bs, R_BR)], axis=2), _split(g_wo, R_WO)],
                           axis=1)


def _pad_row(a):
    a = a.reshape(1, -1)
    return jnp.pad(a, ((0, 0), (0, D - a.shape[1])))


def _pack_small(loss, n1, nm, n2, nf, bg, lng, lnb, bs, rel, ws):
    rows = [_pad_row(loss), n1.reshape(1, D), nm.reshape(1, D), n2.reshape(1, D), nf.reshape(1, D), bg.reshape(2, D),
            jnp.concatenate([lng.reshape(1, D_SGU), lnb.reshape(1, D_SGU)], axis=1), bs.reshape(1, D),
            jnp.pad(rel.reshape(HEADS, N_REL), ((0, 0), (0, D - N_REL))), jnp.zeros((ROW_WS - ROW_REL - HEADS, D), F32),
            ws.reshape(128, D)]
    return jnp.concatenate(rows, axis=0).astype(F32)


def _unpack_small(s):
    return {"loss": s[ROW_LOSS, 0], "norm_ffn1": s[ROW_N1][None], "norm_mix": s[ROW_NM][None], "norm_ffn2": s[ROW_N2][None],
            "norm_final": s[ROW_NF], "b_gate": s[ROW_BG:ROW_BG + 2].reshape(1, 2 * D),
            "sgu_ln_g": s[ROW_LN, :D_SGU][None], "sgu_ln_b": s[ROW_LN, D_SGU:][None], "sgu_b_s": s[ROW_BS].reshape(1, 8, 128),
            "rel_bias": s[ROW_REL:ROW_REL + HEADS, :N_REL][None], "sgu_w_s": s[ROW_WS:].reshape(1, 8, 128, 128)}


_SMALL = ("norm_ffn1", "norm_mix", "norm_ffn2", "norm_final", "b_gate", "sgu_ln_g", "sgu_ln_b", "sgu_b_s", "rel_bias", "sgu_w_s")


def _pack_small_params(p, prefix=""):
    g = lambda n: p[prefix + n]
    return _pack_small(jnp.zeros((1,), F32), g("norm_ffn1"), g("norm_mix"), g("norm_ffn2"), g("norm_final"), g("b_gate"),
                       g("sgu_ln_g"), g("sgu_ln_b"), g("sgu_b_s"), g("rel_bias"), g("sgu_w_s"))


def _step(x, target, p):
    n1, nm, n2 = p["norm_ffn1"], p["norm_mix"], p["norm_ffn2"]
    nf = p["norm_final"].reshape(1, D)
    lng, lnb = p["sgu_ln_g"], p["sgu_ln_b"]
    w_m = jnp.where(jnp.asarray(_sgu_mask())[None], p["sgu_w_s"][0], 0.0).astype(BF16)
    wm2 = jnp.concatenate([w_m[0::2], w_m[1::2]], axis=2)
    w_mt = w_m.transpose(0, 2, 1)
    wmt2 = jnp.concatenate([w_mt[0::2], w_mt[1::2]], axis=2)
    bsx = jnp.repeat(p["sgu_b_s"][0].T, 64, axis=1)
    bias = _band_bias(p["rel_bias"][0])

    def chip_sums(slabs, name):
        (got,) = _comm_only([_SiblingSwap(slabs)], name)
        return _pair_sum(slabs, got)

    (gw1,) = _comm_only([_Gather(_ffn_rows(p, "ffn1").astype(BF16))], "gather_ffn1")
    x1, (gwm,) = _ffn_fwd(x, n1, gw1, gw1, 2 * R_FF, "ffn1_fwd", [_Gather(_mix_rows(p).astype(BF16))])
    (q, k, v, zs, gt, h2), (gw2d,) = _mix_proj_fwd(x1, nm, p["b_gate"], gwm, [_Gather(p["ffn2_w_down"][0].astype(BF16))])
    gate_up2 = jnp.concatenate([p["ffn2_w_gate"][0].T, p["ffn2_w_up"][0].T], axis=0).astype(BF16)
    y_att, (gw2gu,) = _att_fwd(q, k, v, bias, [_Gather(gate_up2)])
    x2, y_sgu, merged = _merge_fwd(x1, zs, gt, y_att, lng, lnb, wm2, bsx, gwm)
    x3, _ = _ffn_fwd(x2, n2, gw2gu, gw2d, 0, "ffn2_fwd")
    dx3, d_nf, loss = _loss_head(x3, nf, target)

    (dx2, d_n2, hb, dab, sb, dfb), _ = _ffn_bwd(x2, n2, dx3, gw2gu, gw2d, 0, "ffn2_bwd")
    sums2 = chip_sums(_ffn_grad_slabs(_weight_grad(dab, hb, "ffn2_dw_gate_up"), _weight_grad(sb, dfb, "ffn2_dw_down")), "swap_ffn2")

    dzg, dya, dys, dpp, dxb, d_bg = _merge_bwd(dx2, gt, y_att, y_sgu, gwm)
    g_wo = _weight_grad(merged, dxb, "dw_out")
    g_ba = _weight_grad(dpp[:, :D], y_att, "dw_branch_att")
    g_bs = _weight_grad(dpp[:, D:], y_sgu, "dw_branch_sgu")
    dzs, d_wm, d_bs, d_lng, d_lnb = _sgu_bwd(zs, dys, lng, lnb, wm2, wmt2, bsx)
    (dq, dk, dv, d_bias), (parts2,) = _att_bwd(q, k, v, bias, dya, [_ChipScatter(sums2)])
    out2 = _sum_adamw(parts2, sums2, _ffn_rows(p, "ffn2"), _ffn_rows(p, "ffn2", "m_"), _ffn_rows(p, "ffn2", "v_"), R_FF, "adamw_ffn2")
    d_rel = _rel_bias_grad(d_bias)
    dz = jnp.concatenate([dq, dk, dv, dzs, dzg], axis=1)
    sums_m = chip_sums(_mix_grad_slabs(_weight_grad(dz, h2, "dw_in"), g_ba, g_bs, g_wo), "swap_mixer")
    dx1, d_nm = _mix_proj_bwd(dz, x1, nm, dx2, gwm)

    (dx0, d_n1, hb, dab, sb, dfb), (parts_m,) = _ffn_bwd(x, n1, dx1, gw1, gw1, 2 * R_FF, "ffn1_bwd", [_ChipScatter(sums_m)])
    out_m = _sum_adamw(parts_m, sums_m, _mix_rows(p), _mix_rows(p, "m_"), _mix_rows(p, "v_"), MIX_ROWS // 4, "adamw_mixer")
    sums1 = chip_sums(_ffn_grad_slabs(_weight_grad(dab, hb, "ffn1_dw_gate_up"), _weight_grad(sb, dfb, "ffn1_dw_down")), "swap_ffn1")
    small = _pack_small(loss[0, :1], d_n1, d_nm, d_n2, d_nf, d_bg, d_lng, d_lnb, d_bs[:, :8].T, d_rel,
                        d_wm.reshape(8, 128, 128))
    parts1, small_parts = _comm_only([_ChipScatter(sums1), _AllToAll(small)], "scatter_ffn1_gather_small")
    out1 = _sum_adamw(parts1, sums1, _ffn_rows(p, "ffn1"), _ffn_rows(p, "ffn1", "m_"), _ffn_rows(p, "ffn1", "v_"), R_FF, "adamw_ffn1")
    out_s = _sum_adamw_small(small_parts, _pack_small_params(p), _pack_small_params(p, "m_"), _pack_small_params(p, "v_"))
    return dx0, [{**_from_ffn_rows(a, "ffn1"), **_from_mix_rows(b), **_from_ffn_rows(c, "ffn2"), **_unpack_small(d)}
                 for a, b, c, d in zip(out1, out_m, out2, out_s)]


_OUT_ORDER = ("norm_ffn1", "ffn1_w_gate", "ffn1_w_up", "ffn1_w_down", "norm_mix", "w_in", "b_gate", "rel_bias", "sgu_ln_g", "sgu_ln_b",
              "sgu_w_s", "sgu_b_s", "w_branch_att", "w_branch_sgu", "w_out", "norm_ffn2", "ffn2_w_gate", "ffn2_w_up", "ffn2_w_down",
              "norm_final")


def kernel(x, norm_ffn1, ffn1_w_gate, ffn1_w_up, ffn1_w_down, norm_mix, w_in, b_gate, rel_bias, sgu_ln_g, sgu_ln_b, sgu_w_s, sgu_b_s, w_branch_att, w_branch_sgu, w_out, norm_ffn2, ffn2_w_gate, ffn2_w_up, ffn2_w_down, norm_final, loss_target, m_norm_ffn1, m_ffn1_w_gate, m_ffn1_w_up, m_ffn1_w_down, m_norm_mix, m_w_in, m_b_gate, m_rel_bias, m_sgu_ln_g, m_sgu_ln_b, m_sgu_w_s, m_sgu_b_s, m_w_branch_att, m_w_branch_sgu, m_w_out, m_norm_ffn2, m_ffn2_w_gate, m_ffn2_w_up, m_ffn2_w_down, m_norm_final, v_norm_ffn1, v_ffn1_w_gate, v_ffn1_w_up, v_ffn1_w_down, v_norm_mix, v_w_in, v_b_gate, v_rel_bias, v_sgu_ln_g, v_sgu_ln_b, v_sgu_w_s, v_sgu_b_s, v_w_branch_att, v_w_branch_sgu, v_w_out, v_norm_ffn2, v_ffn2_w_gate, v_ffn2_w_up, v_ffn2_w_down, v_norm_final):
    args = dict(locals())
    dx, outs = _step(x[0], loss_target[0], {pre + n: args[pre + n] for pre in ("", "m_", "v_") for n in _OUT_ORDER})
    return (outs[0]["loss"], dx[None], *[o[n] for o in outs for n in _OUT_ORDER])
```

```python
import functools

import numpy as np
import jax
import jax.numpy as jnp
from jax import lax
from jax.experimental import pallas as pl
from jax.experimental.pallas import tpu as pltpu

F32 = jnp.float32
BF16 = jnp.bfloat16

N_DEV = 8
D = 1024
F = 2816
D_ATT = 512
D_SGU = 512
D_IN = 4608
HEADS = 8
CHUNK = 64
N_LEFT = 8
REL_CLIP = 256
N_REL = 2 * REL_CLIP + 1
SGU_BLOCK = 128
EPS = 1e-6
NEG_INF = -1e30
QB = 256
KW = 3 * QB

R_FF, R_IN, R_BR, R_WO = F // N_DEV, D_IN // N_DEV, D // N_DEV, D // N_DEV
OFF_IN, OFF_BR, OFF_WO = 0, R_IN, R_IN + R_BR
FFN_ROWS = 3 * R_FF
MIX_ROWS = R_IN + R_BR + R_WO

FC = 256
TM = 512
TM_FFN_BWD = 256
VMEM_LIMIT = 56 * 1024 * 1024

ADAM_LR, ADAM_B1, ADAM_B2, ADAM_EPS, ADAM_WD, ADAM_STEP = 0.001, 0.9, 0.999, 1e-08, 0.01, 10

ROW_LOSS, ROW_N1, ROW_NM, ROW_N2, ROW_NF, ROW_BG, ROW_LN, ROW_BS, ROW_REL, ROW_WS = 0, 1, 2, 3, 4, 5, 7, 8, 9, 24
SMALL_ROWS = ROW_WS + 128

MESH = pl.DeviceIdType.MESH
ANY = pl.BlockSpec(memory_space=pl.ANY)


def _nt(a, b):
    return lax.dot_general(a, b, (((1,), (1,)), ((), ())), preferred_element_type=F32)


def _tn(a, b):
    return lax.dot_general(a, b, (((0,), (0,)), ((), ())), preferred_element_type=F32)


def _nn(a, b):
    return jnp.dot(a, b, preferred_element_type=F32)


def _cparams(sem=None):
    return pltpu.CompilerParams(dimension_semantics=sem, vmem_limit_bytes=VMEM_LIMIT)


def _load_rows(gw_ref, dst, off, rows, sems):
    copies = [pltpu.make_async_copy(gw_ref.at[k, pl.ds(off, rows), :], dst.at[pl.ds(k * rows, rows), :], sems.at[k])
              for k in range(N_DEV)]
    for cp in copies:
        cp.start()
    return copies


def _rms(xv):
    r = lax.rsqrt(jnp.mean(xv * xv, axis=-1, keepdims=True) + EPS)
    return xv * r, r


def _rms_bwd(dh, xn, r, gain):
    dxn = dh * gain
    dx = r * (dxn - xn * jnp.mean(dxn * xn, axis=-1, keepdims=True))
    return dx, jnp.sum(dh * xn, axis=0, keepdims=True)


def _gelu(x):
    t = jnp.tanh(0.7978845608028654 * (x + 0.044715 * x * x * x))
    return 0.5 * x * (1.0 + t), t


def _gelu_grad(x, t):
    return 0.5 * (1.0 + t) + 0.5 * x * (1.0 - t * t) * 0.7978845608028654 * (1.0 + 3.0 * 0.044715 * x * x)


def _place():
    x, y, cc = lax.axis_index("x"), lax.axis_index("y"), lax.axis_index("c")
    return x, y, cc, [(1 - x, y), (x, 1 - y), (1 - x, 1 - y)]


class _Gather:
    def __init__(self, shard):
        self.inputs = [shard]
        self.out_shape = [jax.ShapeDtypeStruct((N_DEV,) + shard.shape, shard.dtype)]
        self.scratch = [pltpu.SemaphoreType.DMA((7,)), pltpu.SemaphoreType.DMA((7,)), pltpu.SemaphoreType.DMA]

    def _copies(self, ins, outs, scr):
        (x_ref,), (out_ref,), (send_sems, recv_sems, local_sem) = ins, outs, scr
        x, y, cc, chips = _place()

        def slab(px, py, pc):
            return out_ref.at[4 * px + 2 * py + pc]

        def copy(k, block, to, src=None):
            return pltpu.make_async_remote_copy(
                src_ref=slab(*block) if src is None else src, dst_ref=slab(*block),
                send_sem=send_sems.at[k], recv_sem=recv_sems.at[k], device_id=to, device_id_type=MESH)

        me, sibling = (x, y, cc), (x, y, 1 - cc)
        mine = pltpu.make_async_copy(x_ref, slab(*me), local_sem)
        first = [copy(0, me, sibling, src=x_ref)] + [copy(1 + j, me, (*chip, cc), src=x_ref) for j, chip in enumerate(chips)]
        landed = [copy(1 + j, (*chip, cc), me) for j, chip in enumerate(chips)]
        passed = [copy(4 + j, (*chip, cc), sibling) for j, chip in enumerate(chips)]
        from_sibling = [copy(0, sibling, me)] + [copy(4 + j, (*chip, 1 - cc), me) for j, chip in enumerate(chips)]
        return mine, first, landed, passed, from_sibling

    def begin(self, *refs):
        mine, first, _, _, _ = self._copies(*refs)
        mine.start()
        for cp in first:
            cp.start()

    def relay(self, *refs):
        _, _, landed, passed, _ = self._copies(*refs)
        for arrived, onward in zip(landed, passed):
            arrived.wait_recv()
            onward.start()

    def end(self, *refs):
        mine, first, _, passed, from_sibling = self._copies(*refs)
        for cp in from_sibling:
            cp.wait_recv()
        for cp in first + passed:
            cp.wait_send()
        mine.wait()


class _Direct:
    def begin(self, *refs):
        keep, give = self._copies(*refs)
        for cp in keep + give:
            cp.start()

    def relay(self, *refs):
        pass

    def end(self, *refs):
        keep, give = self._copies(*refs)
        for cp in give:
            cp.wait_recv()
        for cp in give:
            cp.wait_send()
        for cp in keep:
            cp.wait()


class _SiblingSwap(_Direct):
    def __init__(self, grads):
        n = len(grads)
        self.inputs = list(grads)
        self.out_shape = [jax.ShapeDtypeStruct((4,) + g.shape[1:], g.dtype) for g in grads]
        self.scratch = [pltpu.SemaphoreType.DMA((n, 4)), pltpu.SemaphoreType.DMA((n, 4))]

    def _copies(self, ins, outs, scr):
        send_sems, recv_sems = scr
        x, y, cc, _ = _place()
        return [], [pltpu.make_async_remote_copy(src_ref=g_ref.at[2 * j + 1 - cc], dst_ref=got_ref.at[j], send_sem=send_sems.at[i, j],
                                                 recv_sem=recv_sems.at[i, j], device_id=(x, y, 1 - cc), device_id_type=MESH)
                    for i, (g_ref, got_ref) in enumerate(zip(ins, outs)) for j in range(4)]


class _ChipScatter(_Direct):
    def __init__(self, sums):
        n = len(sums)
        self.inputs = list(sums)
        self.out_shape = [jax.ShapeDtypeStruct((3,) + s.shape[1:], s.dtype) for s in sums]
        self.scratch = [pltpu.SemaphoreType.DMA((n, 3)), pltpu.SemaphoreType.DMA((n, 3))]

    def _copies(self, ins, outs, scr):
        send_sems, recv_sems = scr
        _, _, cc, chips = _place()
        return [], [pltpu.make_async_remote_copy(src_ref=s_ref.at[2 * px + py], dst_ref=got_ref.at[j], send_sem=send_sems.at[i, j],
                                                 recv_sem=recv_sems.at[i, j], device_id=(px, py, cc), device_id_type=MESH)
                    for i, (s_ref, got_ref) in enumerate(zip(ins, outs)) for j, (px, py) in enumerate(chips)]


class _AllToAll(_Direct):
    def __init__(self, block):
        self.inputs = [block]
        self.out_shape = [jax.ShapeDtypeStruct((N_DEV,) + block.shape, block.dtype)]
        self.scratch = [pltpu.SemaphoreType.DMA((7,)), pltpu.SemaphoreType.DMA((7,)), pltpu.SemaphoreType.DMA]

    def _copies(self, ins, outs, scr):
        (b_ref,), (got_ref,), (send_sems, recv_sems, local_sem) = ins, outs, scr
        x, y, cc, _ = _place()
        me = 4 * x + 2 * y + cc
        keep = [pltpu.make_async_copy(b_ref, got_ref.at[me], local_sem)]
        give = [pltpu.make_async_remote_copy(src_ref=b_ref, dst_ref=got_ref.at[me], send_sem=send_sems.at[k - 1], recv_sem=recv_sems.at[k - 1],
                                             device_id=(x ^ ((k >> 2) & 1), y ^ ((k >> 1) & 1), cc ^ (k & 1)), device_id_type=MESH)
                for k in range(1, N_DEV)]
        return keep, give


def _split_refs(refs, counts):
    out, pos = [], 0
    for n in counts:
        out.append(list(refs[pos:pos + n]))
        pos += n
    return out


def _bind(comms, c_in, c_out, c_scr):
    ins = _split_refs(c_in, [len(c.inputs) for c in comms])
    outs = _split_refs(c_out, [len(c.out_shape) for c in comms])
    scr = _split_refs(c_scr, [len(c.scratch) for c in comms])
    return [(c, (i, o, s)) for c, i, o, s in zip(comms, ins, outs, scr)]


def _call(body, *, name, grid, in_specs, out_specs, out_shape, scratch_shapes, args, comms=()):
    c_in = [a for c in comms for a in c.inputs]
    c_out = [s for c in comms for s in c.out_shape]
    c_scr = [s for c in comms for s in c.scratch]
    counts = [len(in_specs), len(c_in), len(out_shape), len(c_out), len(scratch_shapes), len(c_scr)]
    last = grid[0] - 1

    def full(*refs):
        ins, cin, outs, cout, scr, cscr = _split_refs(refs, counts)
        bound = _bind(comms, cin, cout, cscr)
        step = pl.program_id(0)
        if comms:
            @pl.when(step == 0)
            def _():
                for c, r in bound:
                    c.begin(*r)

            @pl.when(step == last)
            def _():
                for c, r in bound:
                    c.relay(*r)

        body(*ins, *outs, *scr)
        if comms:
            @pl.when(step == last)
            def _():
                for c, r in bound:
                    c.end(*r)

    res = pl.pallas_call(
        full, name=name, grid=grid,
        in_specs=list(in_specs) + [ANY] * len(c_in), out_specs=list(out_specs) + [ANY] * len(c_out),
        out_shape=list(out_shape) + c_out, scratch_shapes=list(scratch_shapes) + c_scr,
        compiler_params=_cparams(("arbitrary",)),
    )(*args, *c_in)
    return res[:len(out_shape)], res[len(out_shape):]


def _comm_only(comms, name):
    c_in = [a for c in comms for a in c.inputs]
    c_out = [s for c in comms for s in c.out_shape]
    c_scr = [s for c in comms for s in c.scratch]

    def full(*refs):
        cin, cout, cscr = _split_refs(refs, [len(c_in), len(c_out), len(c_scr)])
        bound = _bind(comms, cin, cout, cscr)
        for phase in ("begin", "relay", "end"):
            for c, r in bound:
                getattr(c, phase)(*r)

    return pl.pallas_call(full, name=name, in_specs=[ANY] * len(c_in), out_specs=[ANY] * len(c_out), out_shape=c_out,
                          scratch_shapes=c_scr)(*c_in)


def _my_index(*axes_and_weights):
    return sum(w * lax.axis_index(a) for a, w in axes_and_weights).astype(jnp.int32).reshape(1)


def _pair_sums(grads, gots, name):
    n = len(grads)

    def body(c_ref, *refs):
        for a_ref, b_ref, o_ref in zip(refs[:n], refs[n:2 * n], refs[2 * n:]):
            o_ref[...] = (a_ref[...].astype(F32) + b_ref[...].astype(F32)).astype(BF16)

    def tile(g):
        return pl.BlockSpec((1,) + g.shape[1:], lambda j, c_ref: (j, 0, 0))

    def mine(g):
        return pl.BlockSpec((1, None) + g.shape[1:], lambda j, c_ref: (j, c_ref[0], 0, 0))

    return pl.pallas_call(
        body, name=name,
        grid_spec=pltpu.PrefetchScalarGridSpec(num_scalar_prefetch=1, grid=(4,), in_specs=[mine(g) for g in gots] + [tile(g) for g in gots],
                                               out_specs=[tile(g) for g in gots]),
        out_shape=[jax.ShapeDtypeStruct(g.shape, BF16) for g in gots],
        compiler_params=_cparams(("arbitrary",)))(_my_index(("c", 1)), *[g.reshape((4, 2) + g.shape[1:]) for g in grads], *gots)


def _ffn_fwd(x, gain, gw_gu, gw_d, off_d, name, comms=()):
    t = x.shape[0]

    def body(x_ref, g_ref, gu_ref, d_ref, o_ref, wg, wu, wd, sems):
        @pl.when(pl.program_id(0) == 0)
        def _():
            cps = _load_rows(gu_ref, wg, 0, R_FF, sems.at[0]) + _load_rows(gu_ref, wu, R_FF, R_FF, sems.at[1]) \
                + _load_rows(d_ref, wd, off_d, R_FF, sems.at[2])
            for cp in cps:
                cp.wait()

        xv = x_ref[...]
        xn, _ = _rms(xv)
        h = (xn * g_ref[...]).astype(BF16)
        acc = jnp.zeros((TM, D), F32)
        for c in range(F // FC):
            rows = pl.ds(c * FC, FC)
            a = _nt(h, wg[rows, :])
            b = _nt(h, wu[rows, :])
            s = (a * jax.nn.sigmoid(a) * b).astype(BF16)
            acc = acc + _nn(s, wd[rows, :])
        o_ref[...] = xv + 0.5 * acc

    tile = pl.BlockSpec((TM, D), lambda i: (i, 0))
    (out,), got = _call(
        body, name=name, grid=(t // TM,),
        in_specs=[tile, pl.BlockSpec((1, D), lambda i: (0, 0)), ANY, ANY], out_specs=[tile],
        out_shape=[jax.ShapeDtypeStruct((t, D), F32)],
        scratch_shapes=[pltpu.VMEM((F, D), BF16)] * 3 + [pltpu.SemaphoreType.DMA((3, N_DEV))],
        args=(x, gain, gw_gu, gw_d), comms=comms)
    return out, got


def _ffn_bwd(x, gain, dout, gw_gu, gw_d, off_d, name, comms=()):
    t = x.shape[0]

    def body(x_ref, g_ref, do_ref, gu_ref, d_ref, dx_ref, dg_ref, h_ref, dab_ref, s_ref, df_ref, wg, wu, wd, sems):
        @pl.when(pl.program_id(0) == 0)
        def _():
            cps = _load_rows(gu_ref, wg, 0, R_FF, sems.at[0]) + _load_rows(gu_ref, wu, R_FF, R_FF, sems.at[1]) \
                + _load_rows(d_ref, wd, off_d, R_FF, sems.at[2])
            dg_ref[...] = jnp.zeros_like(dg_ref)
            for cp in cps:
                cp.wait()

        xv = x_ref[...]
        gain_v = g_ref[...]
        xn, r = _rms(xv)
        h = (xn * gain_v).astype(BF16)
        do = do_ref[...]
        df = (0.5 * do).astype(BF16)
        dh = jnp.zeros((TM_FFN_BWD, D), F32)
        for c in range(F // FC):
            rows = pl.ds(c * FC, FC)
            a = _nt(h, wg[rows, :])
            b = _nt(h, wu[rows, :])
            sg = jax.nn.sigmoid(a)
            sl = a * sg
            ds = _nt(df, wd[rows, :])
            da = (ds * b * (sg * (1.0 + a * (1.0 - sg)))).astype(BF16)
            db = (ds * sl).astype(BF16)
            dh = dh + _nn(da, wg[rows, :]) + _nn(db, wu[rows, :])
            dab_ref[:, c * FC:(c + 1) * FC] = da
            dab_ref[:, F + c * FC:F + (c + 1) * FC] = db
            s_ref[:, c * FC:(c + 1) * FC] = (sl * b).astype(BF16)
        dxn, dg = _rms_bwd(dh, xn, r, gain_v)
        dg_ref[...] += dg
        dx_ref[...] = do + dxn
        h_ref[...] = h
        df_ref[...] = df

    def tile(w):
        return pl.BlockSpec((TM_FFN_BWD, w), lambda i: (i, 0))

    row = pl.BlockSpec((1, D), lambda i: (0, 0))
    return _call(
        body, name=name, grid=(t // TM_FFN_BWD,),
        in_specs=[tile(D), row, tile(D), ANY, ANY],
        out_specs=[tile(D), row, tile(D), tile(2 * F), tile(F), tile(D)],
        out_shape=[jax.ShapeDtypeStruct((t, D), F32), jax.ShapeDtypeStruct((1, D), F32), jax.ShapeDtypeStruct((t, D), BF16),
                   jax.ShapeDtypeStruct((t, 2 * F), BF16), jax.ShapeDtypeStruct((t, F), BF16), jax.ShapeDtypeStruct((t, D), BF16)],
        scratch_shapes=[pltpu.VMEM((F, D), BF16)] * 3 + [pltpu.SemaphoreType.DMA((3, N_DEV))],
        args=(x, gain, dout, gw_gu, gw_d), comms=comms)


def _weight_grad(a, b, name, col_off=0, m=None, tmm=256):
    t = a.shape[0]
    m = a.shape[1] if m is None else m
    n = b.shape[1]
    first = col_off // tmm

    def body(a_ref, b_ref, o_ref):
        o_ref[...] = _tn(a_ref[...], b_ref[...]).astype(BF16)

    return pl.pallas_call(
        body, name=name, grid=(m // tmm,),
        in_specs=[pl.BlockSpec((t, tmm), lambda i: (0, first + i)), pl.BlockSpec((t, n), lambda i: (0, 0))],
        out_specs=pl.BlockSpec((tmm, n), lambda i: (i, 0)),
        out_shape=jax.ShapeDtypeStruct((m, n), BF16),
        compiler_params=_cparams(("arbitrary",)),
    )(a, b).reshape(N_DEV, m // N_DEV, n)


def _mix_proj_fwd(x, gain, b_gate, gw, comms=()):
    t = x.shape[0]

    def body(x_ref, g_ref, bg_ref, gw_ref, q_ref, k_ref, v_ref, zs_ref, gt_ref, h_ref, win, sems):
        @pl.when(pl.program_id(0) == 0)
        def _():
            for cp in _load_rows(gw_ref, win, OFF_IN, R_IN, sems):
                cp.wait()

        xn, _ = _rms(x_ref[...])
        h = (xn * g_ref[...]).astype(BF16)
        h_ref[...] = h
        q_ref[...] = (_nt(h, win[0:512, :]) * 0.125).astype(BF16)
        k_ref[...] = _nt(h, win[512:1024, :]).astype(BF16)
        v_ref[...] = _nt(h, win[1024:1536, :]).astype(BF16)
        for c in range(2):
            zs_ref[:, c * 512:(c + 1) * 512] = _nt(h, win[1536 + c * 512:2048 + c * 512, :]).astype(BF16)
        for c in range(4):
            zg = _nt(h, win[2560 + c * 512:3072 + c * 512, :]) + bg_ref[:, c * 512:(c + 1) * 512]
            gt_ref[:, c * 512:(c + 1) * 512] = jax.nn.sigmoid(zg).astype(BF16)

    def tile(w):
        return pl.BlockSpec((TM, w), lambda i: (i, 0))

    return _call(
        body, name="mix_proj_fwd", grid=(t // TM,),
        in_specs=[tile(D), pl.BlockSpec((1, D), lambda i: (0, 0)), pl.BlockSpec((1, 2 * D), lambda i: (0, 0)), ANY],
        out_specs=[tile(D_ATT), tile(D_ATT), tile(D_ATT), tile(2 * D_SGU), tile(2 * D), tile(D)],
        out_shape=[jax.ShapeDtypeStruct((t, D_ATT), BF16)] * 3 + [jax.ShapeDtypeStruct((t, 2 * D_SGU), BF16),
                                                                   jax.ShapeDtypeStruct((t, 2 * D), BF16),
                                                                   jax.ShapeDtypeStruct((t, D), BF16)],
        scratch_shapes=[pltpu.VMEM((D_IN, D), BF16), pltpu.SemaphoreType.DMA((N_DEV,))],
        args=(x, gain, b_gate, gw), comms=comms)


def _mix_proj_bwd(dz, x, gain, dres, gw):
    t = x.shape[0]

    def body(dz_ref, x_ref, g_ref, dr_ref, gw_ref, dx_ref, dg_ref, win, sems):
        @pl.when(pl.program_id(0) == 0)
        def _():
            cps = _load_rows(gw_ref, win, OFF_IN, R_IN, sems)
            dg_ref[...] = jnp.zeros_like(dg_ref)
            for cp in cps:
                cp.wait()

        dh = jnp.zeros((TM, D), F32)
        for c in range(D_IN // 512):
            dh = dh + _nn(dz_ref[:, c * 512:(c + 1) * 512], win[c * 512:(c + 1) * 512, :])
        xn, r = _rms(x_ref[...])
        dxn, dg = _rms_bwd(dh, xn, r, g_ref[...])
        dg_ref[...] += dg
        dx_ref[...] = dr_ref[...] + dxn

    def tile(w):
        return pl.BlockSpec((TM, w), lambda i: (i, 0))

    row = pl.BlockSpec((1, D), lambda i: (0, 0))
    return pl.pallas_call(
        body, name="mix_proj_bwd", grid=(t // TM,),
        in_specs=[tile(D_IN), tile(D), row, tile(D), ANY], out_specs=[tile(D), row],
        out_shape=[jax.ShapeDtypeStruct((t, D), F32), jax.ShapeDtypeStruct((1, D), F32)],
        scratch_shapes=[pltpu.VMEM((D_IN, D), BF16), pltpu.SemaphoreType.DMA((N_DEV,))],
        compiler_params=_cparams(("arbitrary",)),
    )(dz, x, gain, dres, gw)


SKEW_W = KW + QB
N_CAP = 2 * QB - REL_CLIP + 1


def _band_bias(rel_bias):
    cap = rel_bias[:, 2 * REL_CLIP:]
    diag = jnp.concatenate([jnp.broadcast_to(cap, (HEADS, N_CAP)), rel_bias[:, 2 * REL_CLIP - 1::-1],
                            jnp.broadcast_to(cap, (HEADS, SKEW_W - N_CAP - 2 * REL_CLIP))], axis=1)

    def body(d_ref, o_ref):
        lag = lax.broadcasted_iota(jnp.int32, (QB, KW), 1) // CHUNK - lax.broadcasted_iota(jnp.int32, (QB, KW), 0) // CHUNK
        band = (lag >= 0) & (lag <= N_LEFT)
        for h in range(HEADS):
            rows = jnp.broadcast_to(d_ref[h:h + 1, :], (QB, SKEW_W))
            o_ref[h] = jnp.where(band, pltpu.roll(rows, 0, 1, stride=1, stride_axis=0)[:, :KW], NEG_INF)

    return pl.pallas_call(body, name="band_bias", out_shape=jax.ShapeDtypeStruct((HEADS, QB, KW), F32))(diag)


def _att_specs():
    qspec = pl.BlockSpec((QB, D_ATT), lambda g: (g, 0))
    kspecs = [pl.BlockSpec((QB, D_ATT), lambda g: (jnp.maximum(g - 2, 0), 0)),
              pl.BlockSpec((QB, D_ATT), lambda g: (jnp.maximum(g - 1, 0), 0)), qspec]
    bspec = pl.BlockSpec((HEADS, QB, KW), lambda g: (0, 0, 0))
    return qspec, kspecs, bspec


def _att_probs(qm, kp, bias, valid):
    s = jnp.where(valid, _nt(qm, kp) + bias, NEG_INF)
    e = jnp.exp(s - jnp.max(s, axis=-1, keepdims=True))
    return e / jnp.sum(e, axis=-1, keepdims=True)


def _att_valid():
    g = pl.program_id(0)
    blk = lax.broadcasted_iota(jnp.int32, (QB, KW), 1) // QB
    return (blk + g) >= 2


def _att_fwd(q, k, v, bias, comms=()):
    t = q.shape[0]

    def body(q_ref, k0, k1, k2, v0, v1, v2, b_ref, y_ref):
        valid = _att_valid()
        first = lax.broadcasted_iota(jnp.int32, (1, 128), 1) < 64
        for p in range(HEADS // 2):
            lanes = slice(p * 128, (p + 1) * 128)
            qp = q_ref[:, lanes]
            kp = jnp.concatenate([k0[:, lanes], k1[:, lanes], k2[:, lanes]], axis=0)
            vp = jnp.concatenate([v0[:, lanes], v1[:, lanes], v2[:, lanes]], axis=0)
            out = jnp.zeros((QB, 128), F32)
            for hh in range(2):
                mask = first if hh == 0 else jnp.logical_not(first)
                pr = _att_probs(jnp.where(mask, qp, 0), kp, b_ref[2 * p + hh], valid)
                out = out + _nn(pr.astype(BF16), jnp.where(mask, vp, 0))
            y_ref[:, lanes] = out.astype(BF16)

    qspec, kspecs, bspec = _att_specs()
    (out,), got = _call(
        body, name="att_fwd", grid=(t // QB,),
        in_specs=[qspec] + kspecs + kspecs + [bspec], out_specs=[qspec],
        out_shape=[jax.ShapeDtypeStruct((t, D_ATT), BF16)], scratch_shapes=[],
        args=(q, k, k, k, v, v, v, bias), comms=comms)
    return out, got


def _att_bwd(q, k, v, bias, dy, comms=()):
    t = q.shape[0]
    n_blocks = t // QB

    def body(q_ref, k0, k1, k2, v0, v1, v2, b_ref, dy_ref, dq_ref, dk_ref, dv_ref, db_ref, dk_acc, dv_acc):
        g = pl.program_id(0)

        @pl.when(g == 0)
        def _():
            db_ref[...] = jnp.zeros_like(db_ref)
            dk_acc[...] = jnp.zeros_like(dk_acc)
            dv_acc[...] = jnp.zeros_like(dv_acc)

        valid = _att_valid()
        first = lax.broadcasted_iota(jnp.int32, (1, 128), 1) < 64
        for p in range(HEADS // 2):
            lanes = slice(p * 128, (p + 1) * 128)
            qp = q_ref[:, lanes]
            dyp = dy_ref[:, lanes]
            kp = jnp.concatenate([k0[:, lanes], k1[:, lanes], k2[:, lanes]], axis=0)
            vp = jnp.concatenate([v0[:, lanes], v1[:, lanes], v2[:, lanes]], axis=0)
            dq = jnp.zeros((QB, 128), F32)
            dk = jnp.zeros((KW, 128), F32)
            dv = jnp.zeros((KW, 128), F32)
            for hh in range(2):
                mask = first if hh == 0 else jnp.logical_not(first)
                qm = jnp.where(mask, qp, 0)
                dym = jnp.where(mask, dyp, 0)
                pr = _att_probs(qm, kp, b_ref[2 * p + hh], valid)
                dp = _nt(dym, vp)
                ds = pr * (dp - jnp.sum(dp * pr, axis=-1, keepdims=True))
                db_ref[2 * p + hh] += ds
                dsb = ds.astype(BF16)
                dq = dq + _nn(dsb, jnp.where(mask, kp, 0))
                dk = dk + _tn(dsb, qm)
                dv = dv + _tn(pr.astype(BF16), dym)
            dq_ref[:, lanes] = (dq * 0.125).astype(BF16)
            for j in range(3):
                rows = pl.ds(pl.multiple_of(jnp.maximum(g - 2 + j, 0) * QB, QB), QB)
                dk_acc[rows, lanes] += dk[j * QB:(j + 1) * QB]
                dv_acc[rows, lanes] += dv[j * QB:(j + 1) * QB]

        @pl.when(g == n_blocks - 1)
        def _():
            dk_ref[...] = dk_acc[...].astype(BF16)
            dv_ref[...] = dv_acc[...].astype(BF16)

    qspec, kspecs, bspec = _att_specs()
    full = pl.BlockSpec((t, D_ATT), lambda g: (0, 0))
    return _call(
        body, name="att_bwd", grid=(n_blocks,),
        in_specs=[qspec] + kspecs + kspecs + [bspec, qspec], out_specs=[qspec, full, full, bspec],
        out_shape=[jax.ShapeDtypeStruct((t, D_ATT), BF16)] * 3 + [jax.ShapeDtypeStruct((HEADS, QB, KW), F32)],
        scratch_shapes=[pltpu.VMEM((t, D_ATT), F32)] * 2,
        args=(q, k, k, k, v, v, v, bias, dy), comms=comms)


def _rel_bias_grad(dbias):
    def body(db_ref, cs_ref, tot_ref):
        lane = lax.broadcasted_iota(jnp.int32, (1, SKEW_W), 1)
        capped = (lane < N_CAP) | (lane > KW)
        pad = jnp.zeros((8, QB), F32)
        for h in range(HEADS):
            def add_rows(a, z):
                rows = jnp.concatenate([db_ref[h, pl.ds(pl.multiple_of(8 * a, 8), 8), :], pad], axis=1)
                return z + pltpu.roll(rows, (SKEW_W - 8 * a) % SKEW_W, 1)

            z = lax.fori_loop(0, QB // 8, add_rows, jnp.zeros((8, SKEW_W), F32))
            cs = z[0:1, :]
            for b in range(1, 8):
                cs = cs + pltpu.roll(z[b:b + 1, :], SKEW_W - b, 1)
            cs_ref[h:h + 1, :] = cs
            tot_ref[h:h + 1, :] = jnp.broadcast_to(jnp.sum(jnp.where(capped, cs, 0.0), axis=1, keepdims=True), (1, 128))

    cs, tot = pl.pallas_call(
        body, name="rel_bias_grad",
        out_shape=[jax.ShapeDtypeStruct((HEADS, SKEW_W), F32), jax.ShapeDtypeStruct((HEADS, 128), F32)],
    )(dbias)
    return jnp.concatenate([cs[:, KW:N_CAP - 1:-1], tot[:, :1]], axis=1)


def _sgu_mask():
    pos = np.arange(SGU_BLOCK)
    return (pos[:, None] // CHUNK) >= (pos[None, :] // CHUNK)


def _group_stack(blk, first):
    return jnp.concatenate([jnp.where(first, blk, 0), jnp.where(first, 0, blk)], axis=0)


def _sgu_norm(zs_ref, lng, lnb):
    zs = zs_ref[...].astype(F32)
    ga, th = _gelu(zs)
    u, vs = ga[:, :D_SGU], ga[:, D_SGU:]
    mu = jnp.mean(vs, axis=-1, keepdims=True)
    cen = vs - mu
    rstd = lax.rsqrt(jnp.mean(cen * cen, axis=-1, keepdims=True) + EPS)
    xhat = cen * rstd
    return zs, th, u, xhat, rstd, xhat * lng + lnb


def _sgu_mix(vb, wm2_ref, bsx, s_ref):
    first = lax.broadcasted_iota(jnp.int32, (1, 128), 1) < 64
    for n in range(TM // SGU_BLOCK):
        for p in range(4):
            blk = vb[n * 128:(n + 1) * 128, p * 128:(p + 1) * 128]
            s_ref[n * 128:(n + 1) * 128, p * 128:(p + 1) * 128] = _nn(wm2_ref[p], _group_stack(blk, first)) + bsx[:, p * 128:(p + 1) * 128]


def _merge_fwd(x, zs, gt, y_att, lng, lnb, wm2, bsx, gw):
    t = x.shape[0]

    def body(x_ref, zs_ref, gt_ref, ya_ref, lng_ref, lnb_ref, wm2_ref, bsx_ref, gw_ref, xo_ref, ys_ref, mg_ref,
             wbr, wo, s_scr, sems):
        @pl.when(pl.program_id(0) == 0)
        def _():
            for cp in _load_rows(gw_ref, wbr, OFF_BR, R_BR, sems.at[0]) + _load_rows(gw_ref, wo, OFF_WO, R_WO, sems.at[1]):
                cp.wait()

        _, _, u, _, _, vsn = _sgu_norm(zs_ref, lng_ref[...], lnb_ref[...])
        _sgu_mix(vsn.astype(BF16), wm2_ref, bsx_ref[...], s_scr)
        ys = (u * s_scr[...]).astype(BF16)
        ys_ref[...] = ys
        pa = _nt(ya_ref[...], wbr[:, :D_ATT])
        ps = _nt(ys, wbr[:, D_ATT:])
        mg = (gt_ref[:, :D].astype(F32) * pa + gt_ref[:, D:].astype(F32) * ps).astype(BF16)
        mg_ref[...] = mg
        xo_ref[...] = x_ref[...] + _nn(mg, wo[...])

    def tile(w):
        return pl.BlockSpec((TM, w), lambda i: (i, 0))

    def const(shape):
        return pl.BlockSpec(shape, lambda i: (0,) * len(shape))

    return pl.pallas_call(
        body, name="merge_fwd", grid=(t // TM,),
        in_specs=[tile(D), tile(2 * D_SGU), tile(2 * D), tile(D_ATT), const((1, D_SGU)), const((1, D_SGU)),
                  const((4, 128, 256)), const((128, D_SGU)), ANY],
        out_specs=[tile(D), tile(D_SGU), tile(D)],
        out_shape=[jax.ShapeDtypeStruct((t, D), F32), jax.ShapeDtypeStruct((t, D_SGU), BF16), jax.ShapeDtypeStruct((t, D), BF16)],
        scratch_shapes=[pltpu.VMEM((D, D), BF16), pltpu.VMEM((D, D), BF16), pltpu.VMEM((TM, D_SGU), F32),
                        pltpu.SemaphoreType.DMA((2, N_DEV))],
        compiler_params=_cparams(("arbitrary",)),
    )(x, zs, gt, y_att, lng, lnb, wm2, bsx, gw)


def _merge_bwd(dx, gt, y_att, y_sgu, gw):
    t = dx.shape[0]

    def body(dx_ref, gt_ref, ya_ref, ys_ref, gw_ref, dzg_ref, dya_ref, dys_ref, dpp_ref, dxb_ref, dbg_ref, wbr, wo, sems):
        @pl.when(pl.program_id(0) == 0)
        def _():
            cps = _load_rows(gw_ref, wbr, OFF_BR, R_BR, sems.at[0]) + _load_rows(gw_ref, wo, OFF_WO, R_WO, sems.at[1])
            dbg_ref[...] = jnp.zeros_like(dbg_ref)
            for cp in cps:
                cp.wait()

        dxb = dx_ref[...].astype(BF16)
        dxb_ref[...] = dxb
        dm = _nt(dxb, wo[...])
        for half, y_ref, w in ((0, ya_ref, wbr.at[:, :D_ATT]), (1, ys_ref, wbr.at[:, D_ATT:])):
            cols = slice(half * D, (half + 1) * D)
            gate = gt_ref[:, cols].astype(F32)
            branch = _nt(y_ref[...], w[...])
            dzg = dm * branch * gate * (1.0 - gate)
            dbg_ref[:, cols] += jnp.sum(dzg, axis=0, keepdims=True)
            dzg_ref[:, cols] = dzg.astype(BF16)
            dbr = (dm * gate).astype(BF16)
            dpp_ref[:, cols] = dbr
            dy = _nn(dbr, w[...])
            if half == 0:
                dya_ref[...] = dy.astype(BF16)
            else:
                dys_ref[...] = dy

    def tile(w):
        return pl.BlockSpec((TM, w), lambda i: (i, 0))

    return pl.pallas_call(
        body, name="merge_bwd", grid=(t // TM,),
        in_specs=[tile(D), tile(2 * D), tile(D_ATT), tile(D_SGU), ANY],
        out_specs=[tile(2 * D), tile(D_ATT), tile(D_SGU), tile(2 * D), tile(D), pl.BlockSpec((1, 2 * D), lambda i: (0, 0))],
        out_shape=[jax.ShapeDtypeStruct((t, 2 * D), BF16), jax.ShapeDtypeStruct((t, D_ATT), BF16), jax.ShapeDtypeStruct((t, D_SGU), F32),
                   jax.ShapeDtypeStruct((t, 2 * D), BF16), jax.ShapeDtypeStruct((t, D), BF16), jax.ShapeDtypeStruct((1, 2 * D), F32)],
        scratch_shapes=[pltpu.VMEM((D, D), BF16), pltpu.VMEM((D, D), BF16), pltpu.SemaphoreType.DMA((2, N_DEV))],
        compiler_params=_cparams(("arbitrary",)),
    )(dx, gt, y_att, y_sgu, gw)


def _sgu_bwd(zs, dys, lng, lnb, wm2, wmt2, bsx):
    t = zs.shape[0]
    n_steps = t // TM

    def body(zs_ref, dys_ref, lng_ref, lnb_ref, wm2_ref, wmt2_ref, bsx_ref, dzs_ref, dw_ref, dbs_ref, dlg_ref, dlb_ref,
             s_scr, dv_scr, ds_acc):
        i = pl.program_id(0)

        @pl.when(i == 0)
        def _():
            dw_ref[...] = jnp.zeros_like(dw_ref)
            dlg_ref[...] = jnp.zeros_like(dlg_ref)
            dlb_ref[...] = jnp.zeros_like(dlb_ref)
            ds_acc[...] = jnp.zeros_like(ds_acc)

        lng = lng_ref[...]
        zs, th, u, xhat, rstd, vsn = _sgu_norm(zs_ref, lng, lnb_ref[...])
        vb = vsn.astype(BF16)
        _sgu_mix(vb, wm2_ref, bsx_ref[...], s_scr)
        dys = dys_ref[...]
        du = dys * s_scr[...]
        ds = dys * u
        dsb = ds.astype(BF16)
        first = lax.broadcasted_iota(jnp.int32, (1, 128), 1) < 64
        acc = jnp.zeros((SGU_BLOCK, D_SGU), F32)
        for n in range(TM // SGU_BLOCK):
            rows = slice(n * 128, (n + 1) * 128)
            acc = acc + ds[rows]
            for p in range(4):
                lanes = slice(p * 128, (p + 1) * 128)
                stack = _group_stack(dsb[rows, lanes], first)
                dv_scr[rows, lanes] = _nn(wmt2_ref[p], stack)
                dw_ref[p] += _nt(stack, vb[rows, lanes])
        ds_acc[...] += acc
        dvsn = dv_scr[...]
        dlg_ref[...] += jnp.sum(dvsn * xhat, axis=0, keepdims=True)
        dlb_ref[...] += jnp.sum(dvsn, axis=0, keepdims=True)
        dxh = dvsn * lng
        dvs = rstd * (dxh - jnp.mean(dxh, axis=-1, keepdims=True) - xhat * jnp.mean(dxh * xhat, axis=-1, keepdims=True))
        dga = jnp.concatenate([du, dvs], axis=1)
        dzs_ref[...] = (dga * _gelu_grad(zs, th)).astype(BF16)

        @pl.when(i == n_steps - 1)
        def _():
            r = lax.broadcasted_iota(jnp.int32, (256, 128), 0) % SGU_BLOCK
            c = lax.broadcasted_iota(jnp.int32, (256, 128), 1)
            keep = (r // CHUNK) >= (c // CHUNK)
            for p in range(4):
                dw_ref[p] = jnp.where(keep, dw_ref[p], 0.0)
            total = ds_acc[...]
            grp = lax.broadcasted_iota(jnp.int32, (SGU_BLOCK, D_SGU), 1) // 64
            lane = lax.broadcasted_iota(jnp.int32, (SGU_BLOCK, 128), 1)
            out = jnp.zeros((SGU_BLOCK, 128), F32)
            for gi in range(8):
                out = jnp.where(lane == gi, jnp.sum(jnp.where(grp == gi, total, 0.0), axis=1, keepdims=True), out)
            dbs_ref[...] = out

    def tile(w):
        return pl.BlockSpec((TM, w), lambda i: (i, 0))

    def const(shape):
        return pl.BlockSpec(shape, lambda i: (0,) * len(shape))

    return pl.pallas_call(
        body, name="sgu_bwd", grid=(n_steps,),
        in_specs=[tile(2 * D_SGU), tile(D_SGU), const((1, D_SGU)), const((1, D_SGU)), const((4, 128, 256)), const((4, 128, 256)),
                  const((128, D_SGU))],
        out_specs=[tile(2 * D_SGU), const((4, 256, 128)), const((128, 128)), const((1, D_SGU)), const((1, D_SGU))],
        out_shape=[jax.ShapeDtypeStruct((t, 2 * D_SGU), BF16), jax.ShapeDtypeStruct((4, 256, 128), F32),
                   jax.ShapeDtypeStruct((128, 128), F32), jax.ShapeDtypeStruct((1, D_SGU), F32), jax.ShapeDtypeStruct((1, D_SGU), F32)],
        scratch_shapes=[pltpu.VMEM((TM, D_SGU), F32), pltpu.VMEM((TM, D_SGU), F32), pltpu.VMEM((SGU_BLOCK, D_SGU), F32)],
        compiler_params=_cparams(("arbitrary",)),
    )(zs, dys, lng, lnb, wm2, wmt2, bsx)


def _loss_head(x, gain, target):
    t = x.shape[0]

    def body(x_ref, g_ref, t_ref, dx_ref, dg_ref, loss_ref):
        @pl.when(pl.program_id(0) == 0)
        def _():
            dg_ref[...] = jnp.zeros_like(dg_ref)
            loss_ref[...] = jnp.zeros_like(loss_ref)

        gain_v = g_ref[...]
        xn, r = _rms(x_ref[...])
        err = xn * gain_v - t_ref[...]
        loss_ref[...] += 0.5 * jnp.sum(jnp.mean(err * err, axis=-1, keepdims=True), axis=0, keepdims=True)
        dxn, dg = _rms_bwd(err * (1.0 / D), xn, r, gain_v)
        dg_ref[...] += dg
        dx_ref[...] = dxn

    tile = pl.BlockSpec((TM, D), lambda i: (i, 0))
    row = pl.BlockSpec((1, D), lambda i: (0, 0))
    return pl.pallas_call(
        body, name="loss_head", grid=(t // TM,),
        in_specs=[tile, row, tile], out_specs=[tile, row, pl.BlockSpec((1, 128), lambda i: (0, 0))],
        out_shape=[jax.ShapeDtypeStruct((t, D), F32), jax.ShapeDtypeStruct((1, D), F32), jax.ShapeDtypeStruct((1, 128), F32)],
        compiler_params=_cparams(("arbitrary",)),
    )(x, gain, target)


def _adamw(g, w, m, v):
    m = ADAM_B1 * m + (1.0 - ADAM_B1) * g
    v = ADAM_B2 * v + (1.0 - ADAM_B2) * (g * g)
    m_hat = m / (1.0 - ADAM_B1 ** ADAM_STEP)
    v_hat = v / (1.0 - ADAM_B2 ** ADAM_STEP)
    return -ADAM_LR * (m_hat / (jnp.sqrt(v_hat) + ADAM_EPS) + ADAM_WD * w), m, v


def _adamw_matrix(parts, sums, w, m, v, transposed, name):
    _, r, c = parts.shape
    tc = 256

    def body(own_ref, p_ref, s_ref, w_ref, m_ref, v_ref, g_ref, d_ref, mo_ref, vo_ref):
        g = p_ref[0].astype(F32) + p_ref[1].astype(F32) + p_ref[2].astype(F32) + s_ref[...].astype(F32)
        g = g.T if transposed else g
        g_ref[...] = g
        d_ref[...], mo_ref[...], vo_ref[...] = _adamw(g, w_ref[...], m_ref[...], v_ref[...])

    own = pl.BlockSpec((None, tc, r), lambda i, o: (0, i, 0)) if transposed else pl.BlockSpec((None, r, tc), lambda i, o: (0, 0, i))
    return pl.pallas_call(
        body, name=name,
        grid_spec=pltpu.PrefetchScalarGridSpec(
            num_scalar_prefetch=1, grid=(c // tc,),
            in_specs=[pl.BlockSpec((3, r, tc), lambda i, o: (0, 0, i)), pl.BlockSpec((None, r, tc), lambda i, o: (o[0], 0, i)), own, own, own],
            out_specs=[own] * 4),
        out_shape=[jax.ShapeDtypeStruct(w.shape, F32)] * 4,
        compiler_params=_cparams(("arbitrary",)),
    )(_my_index(("x", 2), ("y", 1)), parts, sums, w, m, v)


def _sum_adamw_small(parts, w, m, v):
    def body(p_ref, w_ref, m_ref, v_ref, g_ref, d_ref, mo_ref, vo_ref):
        g = p_ref[0]
        for k in range(1, N_DEV):
            g = g + p_ref[k]
        g_ref[...] = g
        d_ref[...], mo_ref[...], vo_ref[...] = _adamw(g, w_ref[...], m_ref[...], v_ref[...])

    return pl.pallas_call(body, name="adamw_small", out_shape=[jax.ShapeDtypeStruct(w.shape, F32)] * 4,
                          compiler_params=_cparams())(parts, w, m, v)


def _pack_rows(groups, name):
    flat = [a for grp in groups for a, _ in grp]
    rows = [grp[0][0].shape[2] if grp[0][1] else grp[0][0].shape[1] for grp in groups]

    def body(*refs):
        o_ref, pos, off = refs[-1], 0, 0
        for grp, r in zip(groups, rows):
            vals = []
            for _, transposed in grp:
                val = refs[pos][0]
                vals.append(val.T if transposed else val)
                pos += 1
            o_ref[off:off + r, :] = (vals[0] if len(vals) == 1 else jnp.concatenate(vals, axis=1)).astype(BF16)
            off += r

    return pl.pallas_call(body, name=name, out_shape=jax.ShapeDtypeStruct((sum(rows), D), BF16), compiler_params=_cparams())(*flat)


def _pad_row(a):
    a = a.reshape(1, -1)
    return jnp.pad(a, ((0, 0), (0, D - a.shape[1])))


def _pack_small(loss, n1, nm, n2, nf, bg, lng, lnb, bs, rel, ws):
    rows = [_pad_row(loss), n1.reshape(1, D), nm.reshape(1, D), n2.reshape(1, D), nf.reshape(1, D), bg.reshape(2, D),
            jnp.concatenate([lng.reshape(1, D_SGU), lnb.reshape(1, D_SGU)], axis=1), bs.reshape(1, D),
            jnp.pad(rel.reshape(HEADS, N_REL), ((0, 0), (0, D - N_REL))), jnp.zeros((ROW_WS - ROW_REL - HEADS, D), F32),
            ws.reshape(128, D)]
    return jnp.concatenate(rows, axis=0).astype(F32)


def _unpack_small(s):
    return {"loss": s[ROW_LOSS, 0], "norm_ffn1": s[ROW_N1][None], "norm_mix": s[ROW_NM][None], "norm_ffn2": s[ROW_N2][None],
            "norm_final": s[ROW_NF], "b_gate": s[ROW_BG:ROW_BG + 2].reshape(1, 2 * D),
            "sgu_ln_g": s[ROW_LN, :D_SGU][None], "sgu_ln_b": s[ROW_LN, D_SGU:][None], "sgu_b_s": s[ROW_BS].reshape(1, 8, 128),
            "rel_bias": s[ROW_REL:ROW_REL + HEADS, :N_REL][None], "sgu_w_s": s[ROW_WS:].reshape(1, 8, 128, 128)}


_SMALL = ("norm_ffn1", "norm_mix", "norm_ffn2", "norm_final", "b_gate", "sgu_ln_g", "sgu_ln_b", "sgu_b_s", "rel_bias", "sgu_w_s")


def _pack_small_params(p, prefix=""):
    g = lambda n: p[prefix + n]
    return _pack_small(jnp.zeros((1,), F32), g("norm_ffn1"), g("norm_mix"), g("norm_ffn2"), g("norm_final"), g("b_gate"),
                       g("sgu_ln_g"), g("sgu_ln_b"), g("sgu_b_s"), g("rel_bias"), g("sgu_w_s"))


def _step(x, target, p):
    n1, nm, n2 = p["norm_ffn1"], p["norm_mix"], p["norm_ffn2"]
    nf = p["norm_final"].reshape(1, D)
    lng, lnb = p["sgu_ln_g"], p["sgu_ln_b"]
    w_m = jnp.where(jnp.asarray(_sgu_mask())[None], p["sgu_w_s"][0], 0.0).astype(BF16)
    wm2 = jnp.concatenate([w_m[0::2], w_m[1::2]], axis=2)
    w_mt = w_m.transpose(0, 2, 1)
    wmt2 = jnp.concatenate([w_mt[0::2], w_mt[1::2]], axis=2)
    bsx = jnp.repeat(p["sgu_b_s"][0].T, 64, axis=1)
    bias = _band_bias(p["rel_bias"][0])

    def chip_sums(grads, name):
        gots = _comm_only([_SiblingSwap(grads)], "swap_" + name)
        return _pair_sums(grads, gots, "pair_sums_" + name)

    def updates(parts, sums, names, transposed):
        return {n: _adamw_matrix(pt, sm, p[n], p["m_" + n], p["v_" + n], tr, "adamw_" + n)
                for pt, sm, n, tr in zip(parts, sums, names, transposed)}

    rows1 = _pack_rows([[(p["ffn1_w_gate"], True)], [(p["ffn1_w_up"], True)], [(p["ffn1_w_down"], False)]], "pack_ffn1")
    rows_m = _pack_rows([[(p["w_in"], True)], [(p["w_branch_att"], True), (p["w_branch_sgu"], True)], [(p["w_out"], False)]], "pack_mixer")
    rows2d = _pack_rows([[(p["ffn2_w_down"], False)]], "pack_ffn2_down")
    rows2gu = _pack_rows([[(p["ffn2_w_gate"], True)], [(p["ffn2_w_up"], True)]], "pack_ffn2_gate_up")
    (gw1,) = _comm_only([_Gather(rows1)], "gather_ffn1")
    x1, (gwm,) = _ffn_fwd(x, n1, gw1, gw1, 2 * R_FF, "ffn1_fwd", [_Gather(rows_m)])
    (q, k, v, zs, gt, h2), (gw2d,) = _mix_proj_fwd(x1, nm, p["b_gate"], gwm, [_Gather(rows2d)])
    y_att, (gw2gu,) = _att_fwd(q, k, v, bias, [_Gather(rows2gu)])
    x2, y_sgu, merged = _merge_fwd(x1, zs, gt, y_att, lng, lnb, wm2, bsx, gwm)
    x3, _ = _ffn_fwd(x2, n2, gw2gu, gw2d, 0, "ffn2_fwd")
    dx3, d_nf, loss = _loss_head(x3, nf, target)

    (dx2, d_n2, hb, dab, sb, dfb), _ = _ffn_bwd(x2, n2, dx3, gw2gu, gw2d, 0, "ffn2_bwd")
    ffn_t = (True, True, False)
    sums2 = chip_sums([_weight_grad(dab, hb, "ffn2_dw_gate", 0, F), _weight_grad(dab, hb, "ffn2_dw_up", F, F),
                       _weight_grad(sb, dfb, "ffn2_dw_down")], "ffn2")

    dzg, dya, dys, dpp, dxb, d_bg = _merge_bwd(dx2, gt, y_att, y_sgu, gwm)
    g_wo = _weight_grad(merged, dxb, "dw_out")
    g_ba = _weight_grad(dpp, y_att, "dw_branch_att", 0, D)
    g_bs = _weight_grad(dpp, y_sgu, "dw_branch_sgu", D, D)
    dzs, d_wm, d_bs, d_lng, d_lnb = _sgu_bwd(zs, dys, lng, lnb, wm2, wmt2, bsx)
    (dq, dk, dv, d_bias), parts2 = _att_bwd(q, k, v, bias, dya, [_ChipScatter(sums2)])
    big = updates(parts2, sums2, ("ffn2_w_gate", "ffn2_w_up", "ffn2_w_down"), ffn_t)
    d_rel = _rel_bias_grad(d_bias)
    dz = jnp.concatenate([dq, dk, dv, dzs, dzg], axis=1)
    sums_m = chip_sums([_weight_grad(dz, h2, "dw_in"), g_ba, g_bs, g_wo], "mixer")
    dx1, d_nm = _mix_proj_bwd(dz, x1, nm, dx2, gwm)

    (dx0, d_n1, hb, dab, sb, dfb), parts_m = _ffn_bwd(x, n1, dx1, gw1, gw1, 2 * R_FF, "ffn1_bwd", [_ChipScatter(sums_m)])
    big.update(updates(parts_m, sums_m, ("w_in", "w_branch_att", "w_branch_sgu", "w_out"), (True, True, True, False)))
    sums1 = chip_sums([_weight_grad(dab, hb, "ffn1_dw_gate", 0, F), _weight_grad(dab, hb, "ffn1_dw_up", F, F),
                       _weight_grad(sb, dfb, "ffn1_dw_down")], "ffn1")
    small = _pack_small(loss[0, :1], d_n1, d_nm, d_n2, d_nf, d_bg, d_lng, d_lnb, d_bs[:, :8].T, d_rel,
                        d_wm.reshape(8, 128, 128))
    *parts1, small_parts = _comm_only([_ChipScatter(sums1), _AllToAll(small)], "scatter_ffn1_gather_small")
    big.update(updates(parts1, sums1, ("ffn1_w_gate", "ffn1_w_up", "ffn1_w_down"), ffn_t))
    out_s = _sum_adamw_small(small_parts, _pack_small_params(p), _pack_small_params(p, "m_"), _pack_small_params(p, "v_"))
    return dx0, [{**{n: four[i] for n, four in big.items()}, **_unpack_small(s)} for i, s in enumerate(out_s)]


_OUT_ORDER = ("norm_ffn1", "ffn1_w_gate", "ffn1_w_up", "ffn1_w_down", "norm_mix", "w_in", "b_gate", "rel_bias", "sgu_ln_g", "sgu_ln_b",
              "sgu_w_s", "sgu_b_s", "w_branch_att", "w_branch_sgu", "w_out", "norm_ffn2", "ffn2_w_gate", "ffn2_w_up", "ffn2_w_down",
              "norm_final")


def kernel(x, norm_ffn1, ffn1_w_gate, ffn1_w_up, ffn1_w_down, norm_mix, w_in, b_gate, rel_bias, sgu_ln_g, sgu_ln_b, sgu_w_s, sgu_b_s, w_branch_att, w_branch_sgu, w_out, norm_ffn2, ffn2_w_gate, ffn2_w_up, ffn2_w_down, norm_final, loss_target, m_norm_ffn1, m_ffn1_w_gate, m_ffn1_w_up, m_ffn1_w_down, m_norm_mix, m_w_in, m_b_gate, m_rel_bias, m_sgu_ln_g, m_sgu_ln_b, m_sgu_w_s, m_sgu_b_s, m_w_branch_att, m_w_branch_sgu, m_w_out, m_norm_ffn2, m_ffn2_w_gate, m_ffn2_w_up, m_ffn2_w_down, m_norm_final, v_norm_ffn1, v_ffn1_w_gate, v_ffn1_w_up, v_ffn1_w_down, v_norm_mix, v_w_in, v_b_gate, v_rel_bias, v_sgu_ln_g, v_sgu_ln_b, v_sgu_w_s, v_sgu_b_s, v_w_branch_att, v_w_branch_sgu, v_w_out, v_norm_ffn2, v_ffn2_w_gate, v_ffn2_w_up, v_ffn2_w_down, v_norm_final):
    args = dict(locals())
    dx, outs = _step(x[0], loss_target[0], {pre + n: args[pre + n] for pre in ("", "m_", "v_") for n in _OUT_ORDER})
    return (outs[0]["loss"], dx[None], *[o[n] for o in outs for n in _OUT_ORDER])
```

```python
import functools

import numpy as np
import jax
import jax.numpy as jnp
from jax import lax
from jax.experimental import pallas as pl
from jax.experimental.pallas import tpu as pltpu

F32 = jnp.float32
BF16 = jnp.bfloat16

N_DEV = 8
D = 1024
F = 2816
D_ATT = 512
D_SGU = 512
D_IN = 4608
HEADS = 8
CHUNK = 64
N_LEFT = 8
REL_CLIP = 256
N_REL = 2 * REL_CLIP + 1
SGU_BLOCK = 128
EPS = 1e-6
NEG_INF = -1e30
QB = 256
KW = 3 * QB

R_FF, R_IN, R_BR, R_WO = F // N_DEV, D_IN // N_DEV, D // N_DEV, D // N_DEV
OFF_IN, OFF_BR, OFF_WO = 0, R_IN, R_IN + R_BR
FFN_ROWS = 3 * R_FF
MIX_ROWS = R_IN + R_BR + R_WO

FC = 256
TM = 512
TM_FFN_BWD = 256
VMEM_LIMIT = 56 * 1024 * 1024

ADAM_LR, ADAM_B1, ADAM_B2, ADAM_EPS, ADAM_WD, ADAM_STEP = 0.001, 0.9, 0.999, 1e-08, 0.01, 10

ROW_LOSS, ROW_N1, ROW_NM, ROW_N2, ROW_NF, ROW_BG, ROW_LN, ROW_BS, ROW_REL, ROW_WS = 0, 1, 2, 3, 4, 5, 7, 8, 9, 24
SMALL_ROWS = ROW_WS + 128

MESH = pl.DeviceIdType.MESH
ANY = pl.BlockSpec(memory_space=pl.ANY)


def _nt(a, b):
    return lax.dot_general(a, b, (((1,), (1,)), ((), ())), preferred_element_type=F32)


def _tn(a, b):
    return lax.dot_general(a, b, (((0,), (0,)), ((), ())), preferred_element_type=F32)


def _nn(a, b):
    return jnp.dot(a, b, preferred_element_type=F32)


def _cparams(sem=None):
    return pltpu.CompilerParams(dimension_semantics=sem, vmem_limit_bytes=VMEM_LIMIT)


def _load_rows(gw_ref, dst, off, rows, sems):
    copies = [pltpu.make_async_copy(gw_ref.at[k, pl.ds(off, rows), :], dst.at[pl.ds(k * rows, rows), :], sems.at[k])
              for k in range(N_DEV)]
    for cp in copies:
        cp.start()
    return copies


def _rms(xv):
    r = lax.rsqrt(jnp.mean(xv * xv, axis=-1, keepdims=True) + EPS)
    return xv * r, r


def _rms_bwd(dh, xn, r, gain):
    dxn = dh * gain
    dx = r * (dxn - xn * jnp.mean(dxn * xn, axis=-1, keepdims=True))
    return dx, jnp.sum(dh * xn, axis=0, keepdims=True)


def _gelu(x):
    t = jnp.tanh(0.7978845608028654 * (x + 0.044715 * x * x * x))
    return 0.5 * x * (1.0 + t), t


def _gelu_grad(x, t):
    return 0.5 * (1.0 + t) + 0.5 * x * (1.0 - t * t) * 0.7978845608028654 * (1.0 + 3.0 * 0.044715 * x * x)


def _place():
    x, y, cc = lax.axis_index("x"), lax.axis_index("y"), lax.axis_index("c")
    return x, y, cc, [(1 - x, y), (x, 1 - y), (1 - x, 1 - y)]


class _Gather:
    def __init__(self, shard):
        self.inputs = [shard]
        self.out_shape = [jax.ShapeDtypeStruct((N_DEV,) + shard.shape, shard.dtype)]
        self.scratch = [pltpu.SemaphoreType.DMA((7,)), pltpu.SemaphoreType.DMA((7,)), pltpu.SemaphoreType.DMA]

    def _copies(self, ins, outs, scr):
        (x_ref,), (out_ref,), (send_sems, recv_sems, local_sem) = ins, outs, scr
        x, y, cc, chips = _place()

        def slab(px, py, pc):
            return out_ref.at[4 * px + 2 * py + pc]

        def copy(k, block, to, src=None):
            return pltpu.make_async_remote_copy(
                src_ref=slab(*block) if src is None else src, dst_ref=slab(*block),
                send_sem=send_sems.at[k], recv_sem=recv_sems.at[k], device_id=to, device_id_type=MESH)

        me, sibling = (x, y, cc), (x, y, 1 - cc)
        mine = pltpu.make_async_copy(x_ref, slab(*me), local_sem)
        first = [copy(0, me, sibling, src=x_ref)] + [copy(1 + j, me, (*chip, cc), src=x_ref) for j, chip in enumerate(chips)]
        landed = [copy(1 + j, (*chip, cc), me) for j, chip in enumerate(chips)]
        passed = [copy(4 + j, (*chip, cc), sibling) for j, chip in enumerate(chips)]
        from_sibling = [copy(0, sibling, me)] + [copy(4 + j, (*chip, 1 - cc), me) for j, chip in enumerate(chips)]
        return mine, first, landed, passed, from_sibling

    def begin(self, *refs):
        mine, first, _, _, _ = self._copies(*refs)
        mine.start()
        for cp in first:
            cp.start()

    def relay(self, *refs):
        _, _, landed, passed, _ = self._copies(*refs)
        for arrived, onward in zip(landed, passed):
            arrived.wait_recv()
            onward.start()

    def end(self, *refs):
        mine, first, _, passed, from_sibling = self._copies(*refs)
        for cp in from_sibling:
            cp.wait_recv()
        for cp in first + passed:
            cp.wait_send()
        mine.wait()


class _Direct:
    def begin(self, *refs):
        keep, give = self._copies(*refs)
        for cp in keep + give:
            cp.start()

    def relay(self, *refs):
        pass

    def end(self, *refs):
        keep, give = self._copies(*refs)
        for cp in give:
            cp.wait_recv()
        for cp in give:
            cp.wait_send()
        for cp in keep:
            cp.wait()


class _SiblingSwap(_Direct):
    def __init__(self, grads):
        n = len(grads)
        self.inputs = list(grads)
        self.out_shape = [jax.ShapeDtypeStruct((4,) + g.shape[1:], g.dtype) for g in grads]
        self.scratch = [pltpu.SemaphoreType.DMA((n, 4)), pltpu.SemaphoreType.DMA((n, 4))]

    def _copies(self, ins, outs, scr):
        send_sems, recv_sems = scr
        x, y, cc, _ = _place()
        return [], [pltpu.make_async_remote_copy(src_ref=g_ref.at[2 * j + 1 - cc], dst_ref=got_ref.at[j], send_sem=send_sems.at[i, j],
                                                 recv_sem=recv_sems.at[i, j], device_id=(x, y, 1 - cc), device_id_type=MESH)
                    for i, (g_ref, got_ref) in enumerate(zip(ins, outs)) for j in range(4)]


class _ChipScatter(_Direct):
    def __init__(self, sums):
        n = len(sums)
        self.inputs = list(sums)
        self.out_shape = [jax.ShapeDtypeStruct((3,) + s.shape[1:], s.dtype) for s in sums]
        self.scratch = [pltpu.SemaphoreType.DMA((n, 3)), pltpu.SemaphoreType.DMA((n, 3))]

    def _copies(self, ins, outs, scr):
        send_sems, recv_sems = scr
        _, _, cc, chips = _place()
        return [], [pltpu.make_async_remote_copy(src_ref=s_ref.at[2 * px + py], dst_ref=got_ref.at[j], send_sem=send_sems.at[i, j],
                                                 recv_sem=recv_sems.at[i, j], device_id=(px, py, cc), device_id_type=MESH)
                    for i, (s_ref, got_ref) in enumerate(zip(ins, outs)) for j, (px, py) in enumerate(chips)]


class _AllToAll(_Direct):
    def __init__(self, block):
        self.inputs = [block]
        self.out_shape = [jax.ShapeDtypeStruct((N_DEV,) + block.shape, block.dtype)]
        self.scratch = [pltpu.SemaphoreType.DMA((7,)), pltpu.SemaphoreType.DMA((7,)), pltpu.SemaphoreType.DMA]

    def _copies(self, ins, outs, scr):
        (b_ref,), (got_ref,), (send_sems, recv_sems, local_sem) = ins, outs, scr
        x, y, cc, _ = _place()
        me = 4 * x + 2 * y + cc
        keep = [pltpu.make_async_copy(b_ref, got_ref.at[me], local_sem)]
        give = [pltpu.make_async_remote_copy(src_ref=b_ref, dst_ref=got_ref.at[me], send_sem=send_sems.at[k - 1], recv_sem=recv_sems.at[k - 1],
                                             device_id=(x ^ ((k >> 2) & 1), y ^ ((k >> 1) & 1), cc ^ (k & 1)), device_id_type=MESH)
                for k in range(1, N_DEV)]
        return keep, give


def _split_refs(refs, counts):
    out, pos = [], 0
    for n in counts:
        out.append(list(refs[pos:pos + n]))
        pos += n
    return out


def _bind(comms, c_in, c_out, c_scr):
    ins = _split_refs(c_in, [len(c.inputs) for c in comms])
    outs = _split_refs(c_out, [len(c.out_shape) for c in comms])
    scr = _split_refs(c_scr, [len(c.scratch) for c in comms])
    return [(c, (i, o, s)) for c, i, o, s in zip(comms, ins, outs, scr)]


def _call(body, *, name, grid, in_specs, out_specs, out_shape, scratch_shapes, args, comms=()):
    c_in = [a for c in comms for a in c.inputs]
    c_out = [s for c in comms for s in c.out_shape]
    c_scr = [s for c in comms for s in c.scratch]
    counts = [len(in_specs), len(c_in), len(out_shape), len(c_out), len(scratch_shapes), len(c_scr)]
    last = grid[0] - 1

    def full(*refs):
        ins, cin, outs, cout, scr, cscr = _split_refs(refs, counts)
        bound = _bind(comms, cin, cout, cscr)
        step = pl.program_id(0)
        if comms:
            @pl.when(step == 0)
            def _():
                for c, r in bound:
                    c.begin(*r)

            @pl.when(step == last)
            def _():
                for c, r in bound:
                    c.relay(*r)

        body(*ins, *outs, *scr)
        if comms:
            @pl.when(step == last)
            def _():
                for c, r in bound:
                    c.end(*r)

    res = pl.pallas_call(
        full, name=name, grid=grid,
        in_specs=list(in_specs) + [ANY] * len(c_in), out_specs=list(out_specs) + [ANY] * len(c_out),
        out_shape=list(out_shape) + c_out, scratch_shapes=list(scratch_shapes) + c_scr,
        compiler_params=_cparams(("arbitrary",)),
    )(*args, *c_in)
    return res[:len(out_shape)], res[len(out_shape):]


def _comm_only(comms, name):
    c_in = [a for c in comms for a in c.inputs]
    c_out = [s for c in comms for s in c.out_shape]
    c_scr = [s for c in comms for s in c.scratch]

    def full(*refs):
        cin, cout, cscr = _split_refs(refs, [len(c_in), len(c_out), len(c_scr)])
        bound = _bind(comms, cin, cout, cscr)
        for phase in ("begin", "relay", "end"):
            for c, r in bound:
                getattr(c, phase)(*r)

    return pl.pallas_call(full, name=name, in_specs=[ANY] * len(c_in), out_specs=[ANY] * len(c_out), out_shape=c_out,
                          scratch_shapes=c_scr)(*c_in)


def _my_index(*axes_and_weights):
    return sum(w * lax.axis_index(a) for a, w in axes_and_weights).astype(jnp.int32).reshape(1)


def _pair_sums(grads, gots, name):
    n = len(grads)

    def body(c_ref, *refs):
        for a_ref, b_ref, o_ref in zip(refs[:n], refs[n:2 * n], refs[2 * n:]):
            o_ref[...] = (a_ref[...].astype(F32) + b_ref[...].astype(F32)).astype(BF16)

    def tile(g):
        return pl.BlockSpec((1,) + g.shape[1:], lambda j, c_ref: (j, 0, 0))

    def mine(g):
        return pl.BlockSpec((1, None) + g.shape[1:], lambda j, c_ref: (j, c_ref[0], 0, 0))

    return pl.pallas_call(
        body, name=name,
        grid_spec=pltpu.PrefetchScalarGridSpec(num_scalar_prefetch=1, grid=(4,), in_specs=[mine(g) for g in gots] + [tile(g) for g in gots],
                                               out_specs=[tile(g) for g in gots]),
        out_shape=[jax.ShapeDtypeStruct(g.shape, BF16) for g in gots],
        compiler_params=_cparams(("arbitrary",)))(_my_index(("c", 1)), *[g.reshape((4, 2) + g.shape[1:]) for g in grads], *gots)


def _ffn_fwd(x, gain, gw_gu, gw_d, off_d, name, comms=()):
    t = x.shape[0]

    def body(x_ref, g_ref, gu_ref, d_ref, o_ref, wg, wu, wd, sems):
        @pl.when(pl.program_id(0) == 0)
        def _():
            cps = _load_rows(gu_ref, wg, 0, R_FF, sems.at[0]) + _load_rows(gu_ref, wu, R_FF, R_FF, sems.at[1]) \
                + _load_rows(d_ref, wd, off_d, R_FF, sems.at[2])
            for cp in cps:
                cp.wait()

        xv = x_ref[...]
        xn, _ = _rms(xv)
        h = (xn * g_ref[...]).astype(BF16)
        acc = jnp.zeros((TM, D), F32)
        for c in range(F // FC):
            rows = pl.ds(c * FC, FC)
            a = _nt(h, wg[rows, :])
            b = _nt(h, wu[rows, :])
            s = (a * jax.nn.sigmoid(a) * b).astype(BF16)
            acc = acc + _nn(s, wd[rows, :])
        o_ref[...] = xv + 0.5 * acc

    tile = pl.BlockSpec((TM, D), lambda i: (i, 0))
    (out,), got = _call(
        body, name=name, grid=(t // TM,),
        in_specs=[tile, pl.BlockSpec((1, D), lambda i: (0, 0)), ANY, ANY], out_specs=[tile],
        out_shape=[jax.ShapeDtypeStruct((t, D), F32)],
        scratch_shapes=[pltpu.VMEM((F, D), BF16)] * 3 + [pltpu.SemaphoreType.DMA((3, N_DEV))],
        args=(x, gain, gw_gu, gw_d), comms=comms)
    return out, got


def _ffn_bwd(x, gain, dout, gw_gu, gw_d, off_d, name, comms=()):
    t = x.shape[0]

    def body(x_ref, g_ref, do_ref, gu_ref, d_ref, dx_ref, dg_ref, h_ref, dab_ref, s_ref, df_ref, wg, wu, wd, sems):
        @pl.when(pl.program_id(0) == 0)
        def _():
            cps = _load_rows(gu_ref, wg, 0, R_FF, sems.at[0]) + _load_rows(gu_ref, wu, R_FF, R_FF, sems.at[1]) \
                + _load_rows(d_ref, wd, off_d, R_FF, sems.at[2])
            dg_ref[...] = jnp.zeros_like(dg_ref)
            for cp in cps:
                cp.wait()

        xv = x_ref[...]
        gain_v = g_ref[...]
        xn, r = _rms(xv)
        h = (xn * gain_v).astype(BF16)
        do = do_ref[...]
        df = (0.5 * do).astype(BF16)
        dh = jnp.zeros((TM_FFN_BWD, D), F32)
        for c in range(F // FC):
            rows = pl.ds(c * FC, FC)
            a = _nt(h, wg[rows, :])
            b = _nt(h, wu[rows, :])
            sg = jax.nn.sigmoid(a)
            sl = a * sg
            ds = _nt(df, wd[rows, :])
            da = (ds * b * (sg * (1.0 + a * (1.0 - sg)))).astype(BF16)
            db = (ds * sl).astype(BF16)
            dh = dh + _nn(da, wg[rows, :]) + _nn(db, wu[rows, :])
            dab_ref[:, c * FC:(c + 1) * FC] = da
            dab_ref[:, F + c * FC:F + (c + 1) * FC] = db
            s_ref[:, c * FC:(c + 1) * FC] = (sl * b).astype(BF16)
        dxn, dg = _rms_bwd(dh, xn, r, gain_v)
        dg_ref[...] += dg
        dx_ref[...] = do + dxn
        h_ref[...] = h
        df_ref[...] = df

    def tile(w):
        return pl.BlockSpec((TM_FFN_BWD, w), lambda i: (i, 0))

    row = pl.BlockSpec((1, D), lambda i: (0, 0))
    return _call(
        body, name=name, grid=(t // TM_FFN_BWD,),
        in_specs=[tile(D), row, tile(D), ANY, ANY],
        out_specs=[tile(D), row, tile(D), tile(2 * F), tile(F), tile(D)],
        out_shape=[jax.ShapeDtypeStruct((t, D), F32), jax.ShapeDtypeStruct((1, D), F32), jax.ShapeDtypeStruct((t, D), BF16),
                   jax.ShapeDtypeStruct((t, 2 * F), BF16), jax.ShapeDtypeStruct((t, F), BF16), jax.ShapeDtypeStruct((t, D), BF16)],
        scratch_shapes=[pltpu.VMEM((F, D), BF16)] * 3 + [pltpu.SemaphoreType.DMA((3, N_DEV))],
        args=(x, gain, dout, gw_gu, gw_d), comms=comms)


def _weight_grad(a, b, name, col_off=0, m=None, comms=None, tmm=256):
    t = a.shape[0]
    m = a.shape[1] if m is None else m
    n = b.shape[1]
    first = col_off // tmm

    def body(a_ref, b_ref, o_ref):
        o_ref[...] = _tn(a_ref[...], b_ref[...]).astype(BF16)

    (out,), got = _call(
        body, name=name, grid=(m // tmm,),
        in_specs=[pl.BlockSpec((t, tmm), lambda i: (0, first + i)), pl.BlockSpec((t, n), lambda i: (0, 0))],
        out_specs=[pl.BlockSpec((tmm, n), lambda i: (i, 0))],
        out_shape=[jax.ShapeDtypeStruct((m, n), BF16)], scratch_shapes=[], args=(a, b), comms=comms or ())
    out = out.reshape(N_DEV, m // N_DEV, n)
    return out if comms is None else (out, got)


def _mix_proj_fwd(x, gain, b_gate, gw, comms=()):
    t = x.shape[0]

    def body(x_ref, g_ref, bg_ref, gw_ref, q_ref, k_ref, v_ref, zs_ref, gt_ref, h_ref, win, sems):
        @pl.when(pl.program_id(0) == 0)
        def _():
            for cp in _load_rows(gw_ref, win, OFF_IN, R_IN, sems):
                cp.wait()

        xn, _ = _rms(x_ref[...])
        h = (xn * g_ref[...]).astype(BF16)
        h_ref[...] = h
        q_ref[...] = (_nt(h, win[0:512, :]) * 0.125).astype(BF16)
        k_ref[...] = _nt(h, win[512:1024, :]).astype(BF16)
        v_ref[...] = _nt(h, win[1024:1536, :]).astype(BF16)
        for c in range(2):
            zs_ref[:, c * 512:(c + 1) * 512] = _nt(h, win[1536 + c * 512:2048 + c * 512, :]).astype(BF16)
        for c in range(4):
            zg = _nt(h, win[2560 + c * 512:3072 + c * 512, :]) + bg_ref[:, c * 512:(c + 1) * 512]
            gt_ref[:, c * 512:(c + 1) * 512] = jax.nn.sigmoid(zg).astype(BF16)

    def tile(w):
        return pl.BlockSpec((TM, w), lambda i: (i, 0))

    return _call(
        body, name="mix_proj_fwd", grid=(t // TM,),
        in_specs=[tile(D), pl.BlockSpec((1, D), lambda i: (0, 0)), pl.BlockSpec((1, 2 * D), lambda i: (0, 0)), ANY],
        out_specs=[tile(D_ATT), tile(D_ATT), tile(D_ATT), tile(2 * D_SGU), tile(2 * D), tile(D)],
        out_shape=[jax.ShapeDtypeStruct((t, D_ATT), BF16)] * 3 + [jax.ShapeDtypeStruct((t, 2 * D_SGU), BF16),
                                                                   jax.ShapeDtypeStruct((t, 2 * D), BF16),
                                                                   jax.ShapeDtypeStruct((t, D), BF16)],
        scratch_shapes=[pltpu.VMEM((D_IN, D), BF16), pltpu.SemaphoreType.DMA((N_DEV,))],
        args=(x, gain, b_gate, gw), comms=comms)


def _mix_proj_bwd(dz, x, gain, dres, gw, comms=()):
    t = x.shape[0]

    def body(dz_ref, x_ref, g_ref, dr_ref, gw_ref, dx_ref, dg_ref, win, sems):
        @pl.when(pl.program_id(0) == 0)
        def _():
            cps = _load_rows(gw_ref, win, OFF_IN, R_IN, sems)
            dg_ref[...] = jnp.zeros_like(dg_ref)
            for cp in cps:
                cp.wait()

        dh = jnp.zeros((TM, D), F32)
        for c in range(D_IN // 512):
            dh = dh + _nn(dz_ref[:, c * 512:(c + 1) * 512], win[c * 512:(c + 1) * 512, :])
        xn, r = _rms(x_ref[...])
        dxn, dg = _rms_bwd(dh, xn, r, g_ref[...])
        dg_ref[...] += dg
        dx_ref[...] = dr_ref[...] + dxn

    def tile(w):
        return pl.BlockSpec((TM, w), lambda i: (i, 0))

    row = pl.BlockSpec((1, D), lambda i: (0, 0))
    return _call(
        body, name="mix_proj_bwd", grid=(t // TM,),
        in_specs=[tile(D_IN), tile(D), row, tile(D), ANY], out_specs=[tile(D), row],
        out_shape=[jax.ShapeDtypeStruct((t, D), F32), jax.ShapeDtypeStruct((1, D), F32)],
        scratch_shapes=[pltpu.VMEM((D_IN, D), BF16), pltpu.SemaphoreType.DMA((N_DEV,))],
        args=(dz, x, gain, dres, gw), comms=comms)


SKEW_W = KW + QB
N_CAP = 2 * QB - REL_CLIP + 1


def _band_bias(rel_bias):
    cap = rel_bias[:, 2 * REL_CLIP:]
    diag = jnp.concatenate([jnp.broadcast_to(cap, (HEADS, N_CAP)), rel_bias[:, 2 * REL_CLIP - 1::-1],
                            jnp.broadcast_to(cap, (HEADS, SKEW_W - N_CAP - 2 * REL_CLIP))], axis=1)

    def body(d_ref, o_ref):
        lag = lax.broadcasted_iota(jnp.int32, (QB, KW), 1) // CHUNK - lax.broadcasted_iota(jnp.int32, (QB, KW), 0) // CHUNK
        band = (lag >= 0) & (lag <= N_LEFT)
        for h in range(HEADS):
            rows = jnp.broadcast_to(d_ref[h:h + 1, :], (QB, SKEW_W))
            o_ref[h] = jnp.where(band, pltpu.roll(rows, 0, 1, stride=1, stride_axis=0)[:, :KW], NEG_INF)

    return pl.pallas_call(body, name="band_bias", out_shape=jax.ShapeDtypeStruct((HEADS, QB, KW), F32))(diag)


def _att_specs():
    qspec = pl.BlockSpec((QB, D_ATT), lambda g: (g, 0))
    kspecs = [pl.BlockSpec((QB, D_ATT), lambda g: (jnp.maximum(g - 2, 0), 0)),
              pl.BlockSpec((QB, D_ATT), lambda g: (jnp.maximum(g - 1, 0), 0)), qspec]
    bspec = pl.BlockSpec((HEADS, QB, KW), lambda g: (0, 0, 0))
    return qspec, kspecs, bspec


def _att_probs(qm, kp, bias, valid):
    s = jnp.where(valid, _nt(qm, kp) + bias, NEG_INF)
    e = jnp.exp(s - jnp.max(s, axis=-1, keepdims=True))
    return e / jnp.sum(e, axis=-1, keepdims=True)


def _att_valid():
    g = pl.program_id(0)
    blk = lax.broadcasted_iota(jnp.int32, (QB, KW), 1) // QB
    return (blk + g) >= 2


def _att_fwd(q, k, v, bias, comms=()):
    t = q.shape[0]

    def body(q_ref, k0, k1, k2, v0, v1, v2, b_ref, y_ref):
        valid = _att_valid()
        first = lax.broadcasted_iota(jnp.int32, (1, 128), 1) < 64
        for p in range(HEADS // 2):
            lanes = slice(p * 128, (p + 1) * 128)
            qp = q_ref[:, lanes]
            kp = jnp.concatenate([k0[:, lanes], k1[:, lanes], k2[:, lanes]], axis=0)
            vp = jnp.concatenate([v0[:, lanes], v1[:, lanes], v2[:, lanes]], axis=0)
            out = jnp.zeros((QB, 128), F32)
            for hh in range(2):
                mask = first if hh == 0 else jnp.logical_not(first)
                pr = _att_probs(jnp.where(mask, qp, 0), kp, b_ref[2 * p + hh], valid)
                out = out + _nn(pr.astype(BF16), jnp.where(mask, vp, 0))
            y_ref[:, lanes] = out.astype(BF16)

    qspec, kspecs, bspec = _att_specs()
    (out,), got = _call(
        body, name="att_fwd", grid=(t // QB,),
        in_specs=[qspec] + kspecs + kspecs + [bspec], out_specs=[qspec],
        out_shape=[jax.ShapeDtypeStruct((t, D_ATT), BF16)], scratch_shapes=[],
        args=(q, k, k, k, v, v, v, bias), comms=comms)
    return out, got


def _att_bwd(q, k, v, bias, dy, comms=()):
    t = q.shape[0]
    n_blocks = t // QB

    def body(q_ref, k0, k1, k2, v0, v1, v2, b_ref, dy_ref, dq_ref, dk_ref, dv_ref, db_ref, dk_acc, dv_acc):
        g = pl.program_id(0)

        @pl.when(g == 0)
        def _():
            db_ref[...] = jnp.zeros_like(db_ref)
            dk_acc[...] = jnp.zeros_like(dk_acc)
            dv_acc[...] = jnp.zeros_like(dv_acc)

        valid = _att_valid()
        first = lax.broadcasted_iota(jnp.int32, (1, 128), 1) < 64
        for p in range(HEADS // 2):
            lanes = slice(p * 128, (p + 1) * 128)
            qp = q_ref[:, lanes]
            dyp = dy_ref[:, lanes]
            kp = jnp.concatenate([k0[:, lanes], k1[:, lanes], k2[:, lanes]], axis=0)
            vp = jnp.concatenate([v0[:, lanes], v1[:, lanes], v2[:, lanes]], axis=0)
            dq = jnp.zeros((QB, 128), F32)
            dk = jnp.zeros((KW, 128), F32)
            dv = jnp.zeros((KW, 128), F32)
            for hh in range(2):
                mask = first if hh == 0 else jnp.logical_not(first)
                qm = jnp.where(mask, qp, 0)
                dym = jnp.where(mask, dyp, 0)
                pr = _att_probs(qm, kp, b_ref[2 * p + hh], valid)
                dp = _nt(dym, vp)
                ds = pr * (dp - jnp.sum(dp * pr, axis=-1, keepdims=True))
                db_ref[2 * p + hh] += ds
                dsb = ds.astype(BF16)
                dq = dq + _nn(dsb, jnp.where(mask, kp, 0))
                dk = dk + _tn(dsb, qm)
                dv = dv + _tn(pr.astype(BF16), dym)
            dq_ref[:, lanes] = (dq * 0.125).astype(BF16)
            for j in range(3):
                rows = pl.ds(pl.multiple_of(jnp.maximum(g - 2 + j, 0) * QB, QB), QB)
                dk_acc[rows, lanes] += dk[j * QB:(j + 1) * QB]
                dv_acc[rows, lanes] += dv[j * QB:(j + 1) * QB]

        @pl.when(g == n_blocks - 1)
        def _():
            dk_ref[...] = dk_acc[...].astype(BF16)
            dv_ref[...] = dv_acc[...].astype(BF16)

    qspec, kspecs, bspec = _att_specs()
    full = pl.BlockSpec((t, D_ATT), lambda g: (0, 0))
    return _call(
        body, name="att_bwd", grid=(n_blocks,),
        in_specs=[qspec] + kspecs + kspecs + [bspec, qspec], out_specs=[qspec, full, full, bspec],
        out_shape=[jax.ShapeDtypeStruct((t, D_ATT), BF16)] * 3 + [jax.ShapeDtypeStruct((HEADS, QB, KW), F32)],
        scratch_shapes=[pltpu.VMEM((t, D_ATT), F32)] * 2,
        args=(q, k, k, k, v, v, v, bias, dy), comms=comms)


def _rel_bias_grad(dbias):
    def body(db_ref, cs_ref, tot_ref):
        lane = lax.broadcasted_iota(jnp.int32, (1, SKEW_W), 1)
        capped = (lane < N_CAP) | (lane > KW)
        pad = jnp.zeros((8, QB), F32)
        for h in range(HEADS):
            def add_rows(a, z):
                rows = jnp.concatenate([db_ref[h, pl.ds(pl.multiple_of(8 * a, 8), 8), :], pad], axis=1)
                return z + pltpu.roll(rows, (SKEW_W - 8 * a) % SKEW_W, 1)

            z = lax.fori_loop(0, QB // 8, add_rows, jnp.zeros((8, SKEW_W), F32))
            cs = z[0:1, :]
            for b in range(1, 8):
                cs = cs + pltpu.roll(z[b:b + 1, :], SKEW_W - b, 1)
            cs_ref[h:h + 1, :] = cs
            tot_ref[h:h + 1, :] = jnp.broadcast_to(jnp.sum(jnp.where(capped, cs, 0.0), axis=1, keepdims=True), (1, 128))

    cs, tot = pl.pallas_call(
        body, name="rel_bias_grad",
        out_shape=[jax.ShapeDtypeStruct((HEADS, SKEW_W), F32), jax.ShapeDtypeStruct((HEADS, 128), F32)],
    )(dbias)
    return jnp.concatenate([cs[:, KW:N_CAP - 1:-1], tot[:, :1]], axis=1)


def _sgu_mask():
    pos = np.arange(SGU_BLOCK)
    return (pos[:, None] // CHUNK) >= (pos[None, :] // CHUNK)


def _group_stack(blk, first):
    return jnp.concatenate([jnp.where(first, blk, 0), jnp.where(first, 0, blk)], axis=0)


def _sgu_norm(zs_ref, lng, lnb):
    zs = zs_ref[...].astype(F32)
    ga, th = _gelu(zs)
    u, vs = ga[:, :D_SGU], ga[:, D_SGU:]
    mu = jnp.mean(vs, axis=-1, keepdims=True)
    cen = vs - mu
    rstd = lax.rsqrt(jnp.mean(cen * cen, axis=-1, keepdims=True) + EPS)
    xhat = cen * rstd
    return zs, th, u, xhat, rstd, xhat * lng + lnb


def _sgu_mix(vb, wm2_ref, bsx, s_ref):
    first = lax.broadcasted_iota(jnp.int32, (1, 128), 1) < 64
    for n in range(TM // SGU_BLOCK):
        for p in range(4):
            blk = vb[n * 128:(n + 1) * 128, p * 128:(p + 1) * 128]
            s_ref[n * 128:(n + 1) * 128, p * 128:(p + 1) * 128] = _nn(wm2_ref[p], _group_stack(blk, first)) + bsx[:, p * 128:(p + 1) * 128]


def _merge_fwd(x, zs, gt, y_att, lng, lnb, wm2, bsx, gw):
    t = x.shape[0]

    def body(x_ref, zs_ref, gt_ref, ya_ref, lng_ref, lnb_ref, wm2_ref, bsx_ref, gw_ref, xo_ref, ys_ref, mg_ref,
             wbr, wo, s_scr, sems):
        @pl.when(pl.program_id(0) == 0)
        def _():
            for cp in _load_rows(gw_ref, wbr, OFF_BR, R_BR, sems.at[0]) + _load_rows(gw_ref, wo, OFF_WO, R_WO, sems.at[1]):
                cp.wait()

        _, _, u, _, _, vsn = _sgu_norm(zs_ref, lng_ref[...], lnb_ref[...])
        _sgu_mix(vsn.astype(BF16), wm2_ref, bsx_ref[...], s_scr)
        ys = (u * s_scr[...]).astype(BF16)
        ys_ref[...] = ys
        pa = _nt(ya_ref[...], wbr[:, :D_ATT])
        ps = _nt(ys, wbr[:, D_ATT:])
        mg = (gt_ref[:, :D].astype(F32) * pa + gt_ref[:, D:].astype(F32) * ps).astype(BF16)
        mg_ref[...] = mg
        xo_ref[...] = x_ref[...] + _nn(mg, wo[...])

    def tile(w):
        return pl.BlockSpec((TM, w), lambda i: (i, 0))

    def const(shape):
        return pl.BlockSpec(shape, lambda i: (0,) * len(shape))

    return pl.pallas_call(
        body, name="merge_fwd", grid=(t // TM,),
        in_specs=[tile(D), tile(2 * D_SGU), tile(2 * D), tile(D_ATT), const((1, D_SGU)), const((1, D_SGU)),
                  const((4, 128, 256)), const((128, D_SGU)), ANY],
        out_specs=[tile(D), tile(D_SGU), tile(D)],
        out_shape=[jax.ShapeDtypeStruct((t, D), F32), jax.ShapeDtypeStruct((t, D_SGU), BF16), jax.ShapeDtypeStruct((t, D), BF16)],
        scratch_shapes=[pltpu.VMEM((D, D), BF16), pltpu.VMEM((D, D), BF16), pltpu.VMEM((TM, D_SGU), F32),
                        pltpu.SemaphoreType.DMA((2, N_DEV))],
        compiler_params=_cparams(("arbitrary",)),
    )(x, zs, gt, y_att, lng, lnb, wm2, bsx, gw)


def _merge_bwd(dx, gt, y_att, y_sgu, gw):
    t = dx.shape[0]

    def body(dx_ref, gt_ref, ya_ref, ys_ref, gw_ref, dzg_ref, dya_ref, dys_ref, dpp_ref, dxb_ref, dbg_ref, wbr, wo, sems):
        @pl.when(pl.program_id(0) == 0)
        def _():
            cps = _load_rows(gw_ref, wbr, OFF_BR, R_BR, sems.at[0]) + _load_rows(gw_ref, wo, OFF_WO, R_WO, sems.at[1])
            dbg_ref[...] = jnp.zeros_like(dbg_ref)
            for cp in cps:
                cp.wait()

        dxb = dx_ref[...].astype(BF16)
        dxb_ref[...] = dxb
        dm = _nt(dxb, wo[...])
        for half, y_ref, w in ((0, ya_ref, wbr.at[:, :D_ATT]), (1, ys_ref, wbr.at[:, D_ATT:])):
            cols = slice(half * D, (half + 1) * D)
            gate = gt_ref[:, cols].astype(F32)
            branch = _nt(y_ref[...], w[...])
            dzg = dm * branch * gate * (1.0 - gate)
            dbg_ref[:, cols] += jnp.sum(dzg, axis=0, keepdims=True)
            dzg_ref[:, cols] = dzg.astype(BF16)
            dbr = (dm * gate).astype(BF16)
            dpp_ref[:, cols] = dbr
            dy = _nn(dbr, w[...])
            if half == 0:
                dya_ref[...] = dy.astype(BF16)
            else:
                dys_ref[...] = dy

    def tile(w):
        return pl.BlockSpec((TM, w), lambda i: (i, 0))

    return pl.pallas_call(
        body, name="merge_bwd", grid=(t // TM,),
        in_specs=[tile(D), tile(2 * D), tile(D_ATT), tile(D_SGU), ANY],
        out_specs=[tile(2 * D), tile(D_ATT), tile(D_SGU), tile(2 * D), tile(D), pl.BlockSpec((1, 2 * D), lambda i: (0, 0))],
        out_shape=[jax.ShapeDtypeStruct((t, 2 * D), BF16), jax.ShapeDtypeStruct((t, D_ATT), BF16), jax.ShapeDtypeStruct((t, D_SGU), F32),
                   jax.ShapeDtypeStruct((t, 2 * D), BF16), jax.ShapeDtypeStruct((t, D), BF16), jax.ShapeDtypeStruct((1, 2 * D), F32)],
        scratch_shapes=[pltpu.VMEM((D, D), BF16), pltpu.VMEM((D, D), BF16), pltpu.SemaphoreType.DMA((2, N_DEV))],
        compiler_params=_cparams(("arbitrary",)),
    )(dx, gt, y_att, y_sgu, gw)


def _sgu_bwd(zs, dys, lng, lnb, wm2, wmt2, bsx):
    t = zs.shape[0]
    n_steps = t // TM

    def body(zs_ref, dys_ref, lng_ref, lnb_ref, wm2_ref, wmt2_ref, bsx_ref, dzs_ref, dw_ref, dbs_ref, dlg_ref, dlb_ref,
             s_scr, dv_scr, ds_acc):
        i = pl.program_id(0)

        @pl.when(i == 0)
        def _():
            dw_ref[...] = jnp.zeros_like(dw_ref)
            dlg_ref[...] = jnp.zeros_like(dlg_ref)
            dlb_ref[...] = jnp.zeros_like(dlb_ref)
            ds_acc[...] = jnp.zeros_like(ds_acc)

        lng = lng_ref[...]
        zs, th, u, xhat, rstd, vsn = _sgu_norm(zs_ref, lng, lnb_ref[...])
        vb = vsn.astype(BF16)
        _sgu_mix(vb, wm2_ref, bsx_ref[...], s_scr)
        dys = dys_ref[...]
        du = dys * s_scr[...]
        ds = dys * u
        dsb = ds.astype(BF16)
        first = lax.broadcasted_iota(jnp.int32, (1, 128), 1) < 64
        acc = jnp.zeros((SGU_BLOCK, D_SGU), F32)
        for n in range(TM // SGU_BLOCK):
            rows = slice(n * 128, (n + 1) * 128)
            acc = acc + ds[rows]
            for p in range(4):
                lanes = slice(p * 128, (p + 1) * 128)
                stack = _group_stack(dsb[rows, lanes], first)
                dv_scr[rows, lanes] = _nn(wmt2_ref[p], stack)
                dw_ref[p] += _nt(stack, vb[rows, lanes])
        ds_acc[...] += acc
        dvsn = dv_scr[...]
        dlg_ref[...] += jnp.sum(dvsn * xhat, axis=0, keepdims=True)
        dlb_ref[...] += jnp.sum(dvsn, axis=0, keepdims=True)
        dxh = dvsn * lng
        dvs = rstd * (dxh - jnp.mean(dxh, axis=-1, keepdims=True) - xhat * jnp.mean(dxh * xhat, axis=-1, keepdims=True))
        dga = jnp.concatenate([du, dvs], axis=1)
        dzs_ref[...] = (dga * _gelu_grad(zs, th)).astype(BF16)

        @pl.when(i == n_steps - 1)
        def _():
            r = lax.broadcasted_iota(jnp.int32, (256, 128), 0) % SGU_BLOCK
            c = lax.broadcasted_iota(jnp.int32, (256, 128), 1)
            keep = (r // CHUNK) >= (c // CHUNK)
            for p in range(4):
                dw_ref[p] = jnp.where(keep, dw_ref[p], 0.0)
            total = ds_acc[...]
            grp = lax.broadcasted_iota(jnp.int32, (SGU_BLOCK, D_SGU), 1) // 64
            lane = lax.broadcasted_iota(jnp.int32, (SGU_BLOCK, 128), 1)
            out = jnp.zeros((SGU_BLOCK, 128), F32)
            for gi in range(8):
                out = jnp.where(lane == gi, jnp.sum(jnp.where(grp == gi, total, 0.0), axis=1, keepdims=True), out)
            dbs_ref[...] = out

    def tile(w):
        return pl.BlockSpec((TM, w), lambda i: (i, 0))

    def const(shape):
        return pl.BlockSpec(shape, lambda i: (0,) * len(shape))

    return pl.pallas_call(
        body, name="sgu_bwd", grid=(n_steps,),
        in_specs=[tile(2 * D_SGU), tile(D_SGU), const((1, D_SGU)), const((1, D_SGU)), const((4, 128, 256)), const((4, 128, 256)),
                  const((128, D_SGU))],
        out_specs=[tile(2 * D_SGU), const((4, 256, 128)), const((128, 128)), const((1, D_SGU)), const((1, D_SGU))],
        out_shape=[jax.ShapeDtypeStruct((t, 2 * D_SGU), BF16), jax.ShapeDtypeStruct((4, 256, 128), F32),
                   jax.ShapeDtypeStruct((128, 128), F32), jax.ShapeDtypeStruct((1, D_SGU), F32), jax.ShapeDtypeStruct((1, D_SGU), F32)],
        scratch_shapes=[pltpu.VMEM((TM, D_SGU), F32), pltpu.VMEM((TM, D_SGU), F32), pltpu.VMEM((SGU_BLOCK, D_SGU), F32)],
        compiler_params=_cparams(("arbitrary",)),
    )(zs, dys, lng, lnb, wm2, wmt2, bsx)


def _loss_head(x, gain, target):
    t = x.shape[0]

    def body(x_ref, g_ref, t_ref, dx_ref, dg_ref, loss_ref):
        @pl.when(pl.program_id(0) == 0)
        def _():
            dg_ref[...] = jnp.zeros_like(dg_ref)
            loss_ref[...] = jnp.zeros_like(loss_ref)

        gain_v = g_ref[...]
        xn, r = _rms(x_ref[...])
        err = xn * gain_v - t_ref[...]
        loss_ref[...] += 0.5 * jnp.sum(jnp.mean(err * err, axis=-1, keepdims=True), axis=0, keepdims=True)
        dxn, dg = _rms_bwd(err * (1.0 / D), xn, r, gain_v)
        dg_ref[...] += dg
        dx_ref[...] = dxn

    tile = pl.BlockSpec((TM, D), lambda i: (i, 0))
    row = pl.BlockSpec((1, D), lambda i: (0, 0))
    return pl.pallas_call(
        body, name="loss_head", grid=(t // TM,),
        in_specs=[tile, row, tile], out_specs=[tile, row, pl.BlockSpec((1, 128), lambda i: (0, 0))],
        out_shape=[jax.ShapeDtypeStruct((t, D), F32), jax.ShapeDtypeStruct((1, D), F32), jax.ShapeDtypeStruct((1, 128), F32)],
        compiler_params=_cparams(("arbitrary",)),
    )(x, gain, target)


def _adamw(g, w, m, v):
    m = ADAM_B1 * m + (1.0 - ADAM_B1) * g
    v = ADAM_B2 * v + (1.0 - ADAM_B2) * (g * g)
    m_hat = m / (1.0 - ADAM_B1 ** ADAM_STEP)
    v_hat = v / (1.0 - ADAM_B2 ** ADAM_STEP)
    return -ADAM_LR * (m_hat / (jnp.sqrt(v_hat) + ADAM_EPS) + ADAM_WD * w), m, v


def _adamw_matrix(parts, sums, w, m, v, transposed, name):
    _, r, c = parts.shape
    tc = 256

    def body(own_ref, p_ref, s_ref, w_ref, m_ref, v_ref, g_ref, d_ref, mo_ref, vo_ref):
        g = p_ref[0].astype(F32) + p_ref[1].astype(F32) + p_ref[2].astype(F32) + s_ref[...].astype(F32)
        g = g.T if transposed else g
        g_ref[...] = g
        d_ref[...], mo_ref[...], vo_ref[...] = _adamw(g, w_ref[...], m_ref[...], v_ref[...])

    own = pl.BlockSpec((None, tc, r), lambda i, o: (0, i, 0)) if transposed else pl.BlockSpec((None, r, tc), lambda i, o: (0, 0, i))
    return pl.pallas_call(
        body, name=name,
        grid_spec=pltpu.PrefetchScalarGridSpec(
            num_scalar_prefetch=1, grid=(c // tc,),
            in_specs=[pl.BlockSpec((3, r, tc), lambda i, o: (0, 0, i)), pl.BlockSpec((None, r, tc), lambda i, o: (o[0], 0, i)), own, own, own],
            out_specs=[own] * 4),
        out_shape=[jax.ShapeDtypeStruct(w.shape, F32)] * 4,
        compiler_params=_cparams(("arbitrary",)),
    )(_my_index(("x", 2), ("y", 1)), parts, sums, w, m, v)


def _sum_adamw_small(parts, w, m, v):
    def body(p_ref, w_ref, m_ref, v_ref, g_ref, d_ref, mo_ref, vo_ref):
        g = p_ref[0]
        for k in range(1, N_DEV):
            g = g + p_ref[k]
        g_ref[...] = g
        d_ref[...], mo_ref[...], vo_ref[...] = _adamw(g, w_ref[...], m_ref[...], v_ref[...])

    return pl.pallas_call(body, name="adamw_small", out_shape=[jax.ShapeDtypeStruct(w.shape, F32)] * 4,
                          compiler_params=_cparams())(parts, w, m, v)


def _pack_rows(groups, name):
    flat = [a for grp in groups for a, _ in grp]
    rows = [grp[0][0].shape[2] if grp[0][1] else grp[0][0].shape[1] for grp in groups]

    def body(*refs):
        o_ref, pos, off = refs[-1], 0, 0
        for grp, r in zip(groups, rows):
            vals = []
            for _, transposed in grp:
                val = refs[pos][0]
                vals.append(val.T if transposed else val)
                pos += 1
            o_ref[off:off + r, :] = (vals[0] if len(vals) == 1 else jnp.concatenate(vals, axis=1)).astype(BF16)
            off += r

    return pl.pallas_call(body, name=name, out_shape=jax.ShapeDtypeStruct((sum(rows), D), BF16), compiler_params=_cparams())(*flat)


def _pad_row(a):
    a = a.reshape(1, -1)
    return jnp.pad(a, ((0, 0), (0, D - a.shape[1])))


def _pack_small(loss, n1, nm, n2, nf, bg, lng, lnb, bs, rel, ws):
    rows = [_pad_row(loss), n1.reshape(1, D), nm.reshape(1, D), n2.reshape(1, D), nf.reshape(1, D), bg.reshape(2, D),
            jnp.concatenate([lng.reshape(1, D_SGU), lnb.reshape(1, D_SGU)], axis=1), bs.reshape(1, D),
            jnp.pad(rel.reshape(HEADS, N_REL), ((0, 0), (0, D - N_REL))), jnp.zeros((ROW_WS - ROW_REL - HEADS, D), F32),
            ws.reshape(128, D)]
    return jnp.concatenate(rows, axis=0).astype(F32)


def _unpack_small(s):
    return {"loss": s[ROW_LOSS, 0], "norm_ffn1": s[ROW_N1][None], "norm_mix": s[ROW_NM][None], "norm_ffn2": s[ROW_N2][None],
            "norm_final": s[ROW_NF], "b_gate": s[ROW_BG:ROW_BG + 2].reshape(1, 2 * D),
            "sgu_ln_g": s[ROW_LN, :D_SGU][None], "sgu_ln_b": s[ROW_LN, D_SGU:][None], "sgu_b_s": s[ROW_BS].reshape(1, 8, 128),
            "rel_bias": s[ROW_REL:ROW_REL + HEADS, :N_REL][None], "sgu_w_s": s[ROW_WS:].reshape(1, 8, 128, 128)}


_SMALL = ("norm_ffn1", "norm_mix", "norm_ffn2", "norm_final", "b_gate", "sgu_ln_g", "sgu_ln_b", "sgu_b_s", "rel_bias", "sgu_w_s")


def _pack_small_params(p, prefix=""):
    g = lambda n: p[prefix + n]
    return _pack_small(jnp.zeros((1,), F32), g("norm_ffn1"), g("norm_mix"), g("norm_ffn2"), g("norm_final"), g("b_gate"),
                       g("sgu_ln_g"), g("sgu_ln_b"), g("sgu_b_s"), g("rel_bias"), g("sgu_w_s"))


def _step(x, target, p):
    n1, nm, n2 = p["norm_ffn1"], p["norm_mix"], p["norm_ffn2"]
    nf = p["norm_final"].reshape(1, D)
    lng, lnb = p["sgu_ln_g"], p["sgu_ln_b"]
    w_m = jnp.where(jnp.asarray(_sgu_mask())[None], p["sgu_w_s"][0], 0.0).astype(BF16)
    wm2 = jnp.concatenate([w_m[0::2], w_m[1::2]], axis=2)
    w_mt = w_m.transpose(0, 2, 1)
    wmt2 = jnp.concatenate([w_mt[0::2], w_mt[1::2]], axis=2)
    bsx = jnp.repeat(p["sgu_b_s"][0].T, 64, axis=1)
    bias = _band_bias(p["rel_bias"][0])

    def chip_sums(grads, name):
        gots = _comm_only([_SiblingSwap(grads)], "swap_" + name)
        return _pair_sums(grads, gots, "pair_sums_" + name)

    def as_rows(a):
        return jnp.swapaxes(a, 1, 2)

    def updates(parts, sums, names):
        res = {}
        for pt, sm, n in zip(parts, sums, names):
            if p[n].shape[1:] == pt.shape[1:]:
                res[n] = _adamw_matrix(pt, sm, p[n], p["m_" + n], p["v_" + n], False, "adamw_" + n)
            elif p[n].shape[2] > 128:
                res[n] = [as_rows(o) for o in _adamw_matrix(pt, sm, as_rows(p[n]), as_rows(p["m_" + n]), as_rows(p["v_" + n]), False,
                                                            "adamw_" + n)]
            else:
                res[n] = _adamw_matrix(pt, sm, p[n], p["m_" + n], p["v_" + n], True, "adamw_" + n)
        return res

    rows1 = _pack_rows([[(as_rows(p["ffn1_w_gate"]), False)], [(as_rows(p["ffn1_w_up"]), False)], [(p["ffn1_w_down"], False)]], "pack_ffn1")
    rows_m = _pack_rows([[(as_rows(p["w_in"]), False)], [(p["w_branch_att"], True), (p["w_branch_sgu"], True)], [(p["w_out"], False)]],
                        "pack_mixer")
    rows2d = _pack_rows([[(p["ffn2_w_down"], False)]], "pack_ffn2_down")
    rows2gu = _pack_rows([[(as_rows(p["ffn2_w_gate"]), False)], [(as_rows(p["ffn2_w_up"]), False)]], "pack_ffn2_gate_up")
    (gw1,) = _comm_only([_Gather(rows1)], "gather_ffn1")
    x1, (gwm,) = _ffn_fwd(x, n1, gw1, gw1, 2 * R_FF, "ffn1_fwd", [_Gather(rows_m)])
    (q, k, v, zs, gt, h2), (gw2d,) = _mix_proj_fwd(x1, nm, p["b_gate"], gwm, [_Gather(rows2d)])
    y_att, (gw2gu,) = _att_fwd(q, k, v, bias, [_Gather(rows2gu)])
    x2, y_sgu, merged = _merge_fwd(x1, zs, gt, y_att, lng, lnb, wm2, bsx, gwm)
    x3, _ = _ffn_fwd(x2, n2, gw2gu, gw2d, 0, "ffn2_fwd")
    dx3, d_nf, loss = _loss_head(x3, nf, target)

    (dx2, d_n2, hb, dab, sb, dfb), _ = _ffn_bwd(x2, n2, dx3, gw2gu, gw2d, 0, "ffn2_bwd")
    g2 = [_weight_grad(dab, hb, "ffn2_dw_gate", 0, F), _weight_grad(dab, hb, "ffn2_dw_up", F, F), _weight_grad(sb, dfb, "ffn2_dw_down")]
    dzg, dya, dys, dpp, dxb, d_bg = _merge_bwd(dx2, gt, y_att, y_sgu, gwm)
    g_late = [_weight_grad(dpp, y_att, "dw_branch_att", 0, D), _weight_grad(dpp, y_sgu, "dw_branch_sgu", D, D),
              _weight_grad(merged, dxb, "dw_out")]
    late = ("ffn2_w_gate", "ffn2_w_up", "ffn2_w_down", "w_branch_att", "w_branch_sgu", "w_out")
    sums_late = chip_sums(g2 + g_late, "late")
    dzs, d_wm, d_bs, d_lng, d_lnb = _sgu_bwd(zs, dys, lng, lnb, wm2, wmt2, bsx)
    (dq, dk, dv, d_bias), parts_late = _att_bwd(q, k, v, bias, dya, [_ChipScatter(sums_late)])
    big = updates(parts_late, sums_late, late)
    d_rel = _rel_bias_grad(d_bias)
    dz = jnp.concatenate([dq, dk, dv, dzs, dzg], axis=1)
    sums_in = chip_sums([_weight_grad(dz, h2, "dw_in")], "w_in")
    (dx1, d_nm), parts_in = _mix_proj_bwd(dz, x1, nm, dx2, gwm, [_ChipScatter(sums_in)])
    big.update(updates(parts_in, sums_in, ("w_in",)))

    (dx0, d_n1, hb, dab, sb, dfb), _ = _ffn_bwd(x, n1, dx1, gw1, gw1, 2 * R_FF, "ffn1_bwd")
    small = _pack_small(loss[0, :1], d_n1, d_nm, d_n2, d_nf, d_bg, d_lng, d_lnb, d_bs[:, :8].T, d_rel,
                        d_wm.reshape(8, 128, 128))
    sums_d = chip_sums([_weight_grad(sb, dfb, "ffn1_dw_down")], "ffn1_down")
    g_gate, parts_d = _weight_grad(dab, hb, "ffn1_dw_gate", 0, F, [_ChipScatter(sums_d)])
    g_up, (small_parts,) = _weight_grad(dab, hb, "ffn1_dw_up", F, F, [_AllToAll(small)])
    big.update(updates(parts_d, sums_d, ("ffn1_w_down",)))
    sums_gu = chip_sums([g_gate, g_up], "ffn1_gate_up")
    parts_gu = _comm_only([_ChipScatter(sums_gu)], "scatter_ffn1_gate_up")
    big.update(updates(parts_gu, sums_gu, ("ffn1_w_gate", "ffn1_w_up")))
    out_s = _sum_adamw_small(small_parts, _pack_small_params(p), _pack_small_params(p, "m_"), _pack_small_params(p, "v_"))
    return dx0, [{**{n: four[i] for n, four in big.items()}, **_unpack_small(s)} for i, s in enumerate(out_s)]


_OUT_ORDER = ("norm_ffn1", "ffn1_w_gate", "ffn1_w_up", "ffn1_w_down", "norm_mix", "w_in", "b_gate", "rel_bias", "sgu_ln_g", "sgu_ln_b",
              "sgu_w_s", "sgu_b_s", "w_branch_att", "w_branch_sgu", "w_out", "norm_ffn2", "ffn2_w_gate", "ffn2_w_up", "ffn2_w_down",
              "norm_final")


def kernel(x, norm_ffn1, ffn1_w_gate, ffn1_w_up, ffn1_w_down, norm_mix, w_in, b_gate, rel_bias, sgu_ln_g, sgu_ln_b, sgu_w_s, sgu_b_s, w_branch_att, w_branch_sgu, w_out, norm_ffn2, ffn2_w_gate, ffn2_w_up, ffn2_w_down, norm_final, loss_target, m_norm_ffn1, m_ffn1_w_gate, m_ffn1_w_up, m_ffn1_w_down, m_norm_mix, m_w_in, m_b_gate, m_rel_bias, m_sgu_ln_g, m_sgu_ln_b, m_sgu_w_s, m_sgu_b_s, m_w_branch_att, m_w_branch_sgu, m_w_out, m_norm_ffn2, m_ffn2_w_gate, m_ffn2_w_up, m_ffn2_w_down, m_norm_final, v_norm_ffn1, v_ffn1_w_gate, v_ffn1_w_up, v_ffn1_w_down, v_norm_mix, v_w_in, v_b_gate, v_rel_bias, v_sgu_ln_g, v_sgu_ln_b, v_sgu_w_s, v_sgu_b_s, v_w_branch_att, v_w_branch_sgu, v_w_out, v_norm_ffn2, v_ffn2_w_gate, v_ffn2_w_up, v_ffn2_w_down, v_norm_final):
    args = dict(locals())
    dx, outs = _step(x[0], loss_target[0], {pre + n: args[pre + n] for pre in ("", "m_", "v_") for n in _OUT_ORDER})
    return (outs[0]["loss"], dx[None], *[o[n] for o in outs for n in _OUT_ORDER])
```

```python
import functools

import numpy as np
import jax
import jax.numpy as jnp
from jax import lax
from jax.experimental import pallas as pl
from jax.experimental.pallas import tpu as pltpu

F32 = jnp.float32
BF16 = jnp.bfloat16

N_DEV = 8
D = 1024
F = 2816
D_ATT = 512
D_SGU = 512
D_IN = 4608
HEADS = 8
CHUNK = 64
N_LEFT = 8
REL_CLIP = 256
N_REL = 2 * REL_CLIP + 1
SGU_BLOCK = 128
EPS = 1e-6
NEG_INF = -1e30
QB = 256
KW = 3 * QB

R_FF, R_IN, R_BR, R_WO = F // N_DEV, D_IN // N_DEV, D // N_DEV, D // N_DEV
OFF_IN, OFF_BR, OFF_WO = 0, R_IN, R_IN + R_BR
FFN_ROWS = 3 * R_FF
MIX_ROWS = R_IN + R_BR + R_WO

FC = 256
TM = 512
TM_FFN_BWD = 256
VMEM_LIMIT = 56 * 1024 * 1024

ADAM_LR, ADAM_B1, ADAM_B2, ADAM_EPS, ADAM_WD, ADAM_STEP = 0.001, 0.9, 0.999, 1e-08, 0.01, 10

ROW_LOSS, ROW_N1, ROW_NM, ROW_N2, ROW_NF, ROW_BG, ROW_LN, ROW_BS, ROW_REL, ROW_WS = 0, 1, 2, 3, 4, 5, 7, 8, 9, 24
SMALL_ROWS = ROW_WS + 128

MESH = pl.DeviceIdType.MESH
ANY = pl.BlockSpec(memory_space=pl.ANY)


def _nt(a, b):
    return lax.dot_general(a, b, (((1,), (1,)), ((), ())), preferred_element_type=F32)


def _tn(a, b):
    return lax.dot_general(a, b, (((0,), (0,)), ((), ())), preferred_element_type=F32)


def _nn(a, b):
    return jnp.dot(a, b, preferred_element_type=F32)


def _cparams(sem=None):
    return pltpu.CompilerParams(dimension_semantics=sem, vmem_limit_bytes=VMEM_LIMIT)


def _load_rows(gw_ref, dst, off, rows, sems):
    copies = [pltpu.make_async_copy(gw_ref.at[k, pl.ds(off, rows), :], dst.at[pl.ds(k * rows, rows), :], sems.at[k])
              for k in range(N_DEV)]
    for cp in copies:
        cp.start()
    return copies


def _rms(xv):
    r = lax.rsqrt(jnp.mean(xv * xv, axis=-1, keepdims=True) + EPS)
    return xv * r, r


def _rms_bwd(dh, xn, r, gain):
    dxn = dh * gain
    dx = r * (dxn - xn * jnp.mean(dxn * xn, axis=-1, keepdims=True))
    return dx, jnp.sum(dh * xn, axis=0, keepdims=True)


def _gelu(x):
    t = jnp.tanh(0.7978845608028654 * (x + 0.044715 * x * x * x))
    return 0.5 * x * (1.0 + t), t


def _gelu_grad(x, t):
    return 0.5 * (1.0 + t) + 0.5 * x * (1.0 - t * t) * 0.7978845608028654 * (1.0 + 3.0 * 0.044715 * x * x)


def _place():
    x, y, cc = lax.axis_index("x"), lax.axis_index("y"), lax.axis_index("c")
    return x, y, cc, [(1 - x, y), (x, 1 - y), (1 - x, 1 - y)]


class _Gather:
    def __init__(self, shard):
        self.inputs = [shard]
        self.out_shape = [jax.ShapeDtypeStruct((N_DEV,) + shard.shape, shard.dtype)]
        self.scratch = [pltpu.SemaphoreType.DMA((7,)), pltpu.SemaphoreType.DMA((7,)), pltpu.SemaphoreType.DMA]

    def _copies(self, ins, outs, scr):
        (x_ref,), (out_ref,), (send_sems, recv_sems, local_sem) = ins, outs, scr
        x, y, cc, chips = _place()

        def slab(px, py, pc):
            return out_ref.at[4 * px + 2 * py + pc]

        def copy(k, block, to, src=None):
            return pltpu.make_async_remote_copy(
                src_ref=slab(*block) if src is None else src, dst_ref=slab(*block),
                send_sem=send_sems.at[k], recv_sem=recv_sems.at[k], device_id=to, device_id_type=MESH)

        me, sibling = (x, y, cc), (x, y, 1 - cc)
        mine = pltpu.make_async_copy(x_ref, slab(*me), local_sem)
        first = [copy(0, me, sibling, src=x_ref)] + [copy(1 + j, me, (*chip, cc), src=x_ref) for j, chip in enumerate(chips)]
        landed = [copy(1 + j, (*chip, cc), me) for j, chip in enumerate(chips)]
        passed = [copy(4 + j, (*chip, cc), sibling) for j, chip in enumerate(chips)]
        from_sibling = [copy(0, sibling, me)] + [copy(4 + j, (*chip, 1 - cc), me) for j, chip in enumerate(chips)]
        return mine, first, landed, passed, from_sibling

    def begin(self, *refs):
        mine, first, _, _, _ = self._copies(*refs)
        mine.start()
        for cp in first:
            cp.start()

    def relay(self, *refs):
        _, _, landed, passed, _ = self._copies(*refs)
        for arrived, onward in zip(landed, passed):
            arrived.wait_recv()
            onward.start()

    def end(self, *refs):
        mine, first, _, passed, from_sibling = self._copies(*refs)
        for cp in from_sibling:
            cp.wait_recv()
        for cp in first + passed:
            cp.wait_send()
        mine.wait()


class _Direct:
    def begin(self, *refs):
        keep, give = self._copies(*refs)
        for cp in keep + give:
            cp.start()

    def relay(self, *refs):
        pass

    def end(self, *refs):
        keep, give = self._copies(*refs)
        for cp in give:
            cp.wait_recv()
        for cp in give:
            cp.wait_send()
        for cp in keep:
            cp.wait()


class _SiblingSwap(_Direct):
    def __init__(self, grads):
        n = len(grads)
        self.inputs = list(grads)
        self.out_shape = [jax.ShapeDtypeStruct((4,) + g.shape[1:], g.dtype) for g in grads]
        self.scratch = [pltpu.SemaphoreType.DMA((n, 4)), pltpu.SemaphoreType.DMA((n, 4))]

    def _copies(self, ins, outs, scr):
        send_sems, recv_sems = scr
        x, y, cc, _ = _place()
        return [], [pltpu.make_async_remote_copy(src_ref=g_ref.at[2 * j + 1 - cc], dst_ref=got_ref.at[j], send_sem=send_sems.at[i, j],
                                                 recv_sem=recv_sems.at[i, j], device_id=(x, y, 1 - cc), device_id_type=MESH)
                    for i, (g_ref, got_ref) in enumerate(zip(ins, outs)) for j in range(4)]


class _ChipScatter(_Direct):
    def __init__(self, sums):
        n = len(sums)
        self.inputs = list(sums)
        self.out_shape = [jax.ShapeDtypeStruct((3,) + s.shape[1:], s.dtype) for s in sums]
        self.scratch = [pltpu.SemaphoreType.DMA((n, 3)), pltpu.SemaphoreType.DMA((n, 3))]

    def _copies(self, ins, outs, scr):
        send_sems, recv_sems = scr
        _, _, cc, chips = _place()
        return [], [pltpu.make_async_remote_copy(src_ref=s_ref.at[2 * px + py], dst_ref=got_ref.at[j], send_sem=send_sems.at[i, j],
                                                 recv_sem=recv_sems.at[i, j], device_id=(px, py, cc), device_id_type=MESH)
                    for i, (s_ref, got_ref) in enumerate(zip(ins, outs)) for j, (px, py) in enumerate(chips)]


class _AllToAll(_Direct):
    def __init__(self, block):
        self.inputs = [block]
        self.out_shape = [jax.ShapeDtypeStruct((N_DEV,) + block.shape, block.dtype)]
        self.scratch = [pltpu.SemaphoreType.DMA((7,)), pltpu.SemaphoreType.DMA((7,)), pltpu.SemaphoreType.DMA]

    def _copies(self, ins, outs, scr):
        (b_ref,), (got_ref,), (send_sems, recv_sems, local_sem) = ins, outs, scr
        x, y, cc, _ = _place()
        me = 4 * x + 2 * y + cc
        keep = [pltpu.make_async_copy(b_ref, got_ref.at[me], local_sem)]
        give = [pltpu.make_async_remote_copy(src_ref=b_ref, dst_ref=got_ref.at[me], send_sem=send_sems.at[k - 1], recv_sem=recv_sems.at[k - 1],
                                             device_id=(x ^ ((k >> 2) & 1), y ^ ((k >> 1) & 1), cc ^ (k & 1)), device_id_type=MESH)
                for k in range(1, N_DEV)]
        return keep, give


def _split_refs(refs, counts):
    out, pos = [], 0
    for n in counts:
        out.append(list(refs[pos:pos + n]))
        pos += n
    return out


def _bind(comms, c_in, c_out, c_scr):
    ins = _split_refs(c_in, [len(c.inputs) for c in comms])
    outs = _split_refs(c_out, [len(c.out_shape) for c in comms])
    scr = _split_refs(c_scr, [len(c.scratch) for c in comms])
    return [(c, (i, o, s)) for c, i, o, s in zip(comms, ins, outs, scr)]


def _call(body, *, name, grid, in_specs, out_specs, out_shape, scratch_shapes, args, comms=()):
    c_in = [a for c in comms for a in c.inputs]
    c_out = [s for c in comms for s in c.out_shape]
    c_scr = [s for c in comms for s in c.scratch]
    counts = [len(in_specs), len(c_in), len(out_shape), len(c_out), len(scratch_shapes), len(c_scr)]
    last = grid[0] - 1

    def full(*refs):
        ins, cin, outs, cout, scr, cscr = _split_refs(refs, counts)
        bound = _bind(comms, cin, cout, cscr)
        step = pl.program_id(0)
        if comms:
            @pl.when(step == 0)
            def _():
                for c, r in bound:
                    c.begin(*r)

            @pl.when(step == last)
            def _():
                for c, r in bound:
                    c.relay(*r)

        body(*ins, *outs, *scr)
        if comms:
            @pl.when(step == last)
            def _():
                for c, r in bound:
                    c.end(*r)

    res = pl.pallas_call(
        full, name=name, grid=grid,
        in_specs=list(in_specs) + [ANY] * len(c_in), out_specs=list(out_specs) + [ANY] * len(c_out),
        out_shape=list(out_shape) + c_out, scratch_shapes=list(scratch_shapes) + c_scr,
        compiler_params=_cparams(("arbitrary",)),
    )(*args, *c_in)
    return list(res[:len(out_shape)]), list(res[len(out_shape):])


def _comm_only(comms, name):
    c_in = [a for c in comms for a in c.inputs]
    c_out = [s for c in comms for s in c.out_shape]
    c_scr = [s for c in comms for s in c.scratch]

    def full(*refs):
        cin, cout, cscr = _split_refs(refs, [len(c_in), len(c_out), len(c_scr)])
        bound = _bind(comms, cin, cout, cscr)
        for phase in ("begin", "relay", "end"):
            for c, r in bound:
                getattr(c, phase)(*r)

    return list(pl.pallas_call(full, name=name, in_specs=[ANY] * len(c_in), out_specs=[ANY] * len(c_out), out_shape=c_out,
                               scratch_shapes=c_scr)(*c_in))


def _my_index(*axes_and_weights):
    return sum(w * lax.axis_index(a) for a, w in axes_and_weights).astype(jnp.int32).reshape(1)


def _pair_sums(grads, gots, name):
    n = len(grads)

    def body(c_ref, *refs):
        for a_ref, b_ref, o_ref in zip(refs[:n], refs[n:2 * n], refs[2 * n:]):
            o_ref[...] = (a_ref[...].astype(F32) + b_ref[...].astype(F32)).astype(BF16)

    def tile(g):
        return pl.BlockSpec((1,) + g.shape[1:], lambda j, c_ref: (j, 0, 0))

    def mine(g):
        return pl.BlockSpec((1, None) + g.shape[1:], lambda j, c_ref: (j, c_ref[0], 0, 0))

    return list(pl.pallas_call(
        body, name=name,
        grid_spec=pltpu.PrefetchScalarGridSpec(num_scalar_prefetch=1, grid=(4,), in_specs=[mine(g) for g in gots] + [tile(g) for g in gots],
                                               out_specs=[tile(g) for g in gots]),
        out_shape=[jax.ShapeDtypeStruct(g.shape, BF16) for g in gots],
        compiler_params=_cparams(("arbitrary",)))(_my_index(("c", 1)), *[g.reshape((4, 2) + g.shape[1:]) for g in grads], *gots))


def _ffn_fwd(x, gain, gw_gu, gw_d, off_d, name, comms=()):
    t = x.shape[0]

    def body(x_ref, g_ref, gu_ref, d_ref, o_ref, ab_ref, wg, wu, wd, sems):
        @pl.when(pl.program_id(0) == 0)
        def _():
            cps = _load_rows(gu_ref, wg, 0, R_FF, sems.at[0]) + _load_rows(gu_ref, wu, R_FF, R_FF, sems.at[1]) \
                + _load_rows(d_ref, wd, off_d, R_FF, sems.at[2])
            for cp in cps:
                cp.wait()

        xv = x_ref[...]
        xn, _ = _rms(xv)
        h = (xn * g_ref[...]).astype(BF16)
        acc = jnp.zeros((TM, D), F32)
        for c in range(F // FC):
            rows = pl.ds(c * FC, FC)
            a = _nt(h, wg[rows, :])
            b = _nt(h, wu[rows, :])
            ab_ref[:, c * FC:(c + 1) * FC] = a.astype(BF16)
            ab_ref[:, F + c * FC:F + (c + 1) * FC] = b.astype(BF16)
            s = (a * jax.nn.sigmoid(a) * b).astype(BF16)
            acc = acc + _nn(s, wd[rows, :])
        o_ref[...] = xv + 0.5 * acc

    tile = pl.BlockSpec((TM, D), lambda i: (i, 0))
    (out, ab), got = _call(
        body, name=name, grid=(t // TM,),
        in_specs=[tile, pl.BlockSpec((1, D), lambda i: (0, 0)), ANY, ANY], out_specs=[tile, pl.BlockSpec((TM, 2 * F), lambda i: (i, 0))],
        out_shape=[jax.ShapeDtypeStruct((t, D), F32), jax.ShapeDtypeStruct((t, 2 * F), BF16)],
        scratch_shapes=[pltpu.VMEM((F, D), BF16)] * 3 + [pltpu.SemaphoreType.DMA((3, N_DEV))],
        args=(x, gain, gw_gu, gw_d), comms=comms)
    return out, ab, got


def _ffn_bwd(x, gain, ab, dout, gw_gu, gw_d, off_d, name, comms=()):
    t = x.shape[0]

    def body(x_ref, g_ref, ab_ref, do_ref, gu_ref, d_ref, dx_ref, dg_ref, h_ref, dab_ref, s_ref, df_ref, wg, wu, wd, sems):
        @pl.when(pl.program_id(0) == 0)
        def _():
            cps = _load_rows(gu_ref, wg, 0, R_FF, sems.at[0]) + _load_rows(gu_ref, wu, R_FF, R_FF, sems.at[1]) \
                + _load_rows(d_ref, wd, off_d, R_FF, sems.at[2])
            dg_ref[...] = jnp.zeros_like(dg_ref)
            for cp in cps:
                cp.wait()

        xv = x_ref[...]
        gain_v = g_ref[...]
        xn, r = _rms(xv)
        h = (xn * gain_v).astype(BF16)
        do = do_ref[...]
        df = (0.5 * do).astype(BF16)
        dh = jnp.zeros((TM_FFN_BWD, D), F32)
        for c in range(F // FC):
            rows = pl.ds(c * FC, FC)
            a = ab_ref[:, c * FC:(c + 1) * FC].astype(F32)
            b = ab_ref[:, F + c * FC:F + (c + 1) * FC].astype(F32)
            sg = jax.nn.sigmoid(a)
            sl = a * sg
            ds = _nt(df, wd[rows, :])
            da = (ds * b * (sg * (1.0 + a * (1.0 - sg)))).astype(BF16)
            db = (ds * sl).astype(BF16)
            dh = dh + _nn(da, wg[rows, :]) + _nn(db, wu[rows, :])
            dab_ref[:, c * FC:(c + 1) * FC] = da
            dab_ref[:, F + c * FC:F + (c + 1) * FC] = db
            s_ref[:, c * FC:(c + 1) * FC] = (sl * b).astype(BF16)
        dxn, dg = _rms_bwd(dh, xn, r, gain_v)
        dg_ref[...] += dg
        dx_ref[...] = do + dxn
        h_ref[...] = h
        df_ref[...] = df

    def tile(w):
        return pl.BlockSpec((TM_FFN_BWD, w), lambda i: (i, 0))

    row = pl.BlockSpec((1, D), lambda i: (0, 0))
    return _call(
        body, name=name, grid=(t // TM_FFN_BWD,),
        in_specs=[tile(D), row, tile(2 * F), tile(D), ANY, ANY],
        out_specs=[tile(D), row, tile(D), tile(2 * F), tile(F), tile(D)],
        out_shape=[jax.ShapeDtypeStruct((t, D), F32), jax.ShapeDtypeStruct((1, D), F32), jax.ShapeDtypeStruct((t, D), BF16),
                   jax.ShapeDtypeStruct((t, 2 * F), BF16), jax.ShapeDtypeStruct((t, F), BF16), jax.ShapeDtypeStruct((t, D), BF16)],
        scratch_shapes=[pltpu.VMEM((F, D), BF16)] * 3 + [pltpu.SemaphoreType.DMA((3, N_DEV))],
        args=(x, gain, ab, dout, gw_gu, gw_d), comms=comms)


def _weight_grad(a, b, name, col_off=0, m=None, comms=None, tmm=256):
    t = a.shape[0]
    m = a.shape[1] if m is None else m
    n = b.shape[1]
    first = col_off // tmm

    def body(a_ref, b_ref, o_ref):
        o_ref[...] = _tn(a_ref[...], b_ref[...]).astype(BF16)

    (out,), got = _call(
        body, name=name, grid=(m // tmm,),
        in_specs=[pl.BlockSpec((t, tmm), lambda i: (0, first + i)), pl.BlockSpec((t, n), lambda i: (0, 0))],
        out_specs=[pl.BlockSpec((tmm, n), lambda i: (i, 0))],
        out_shape=[jax.ShapeDtypeStruct((m, n), BF16)], scratch_shapes=[], args=(a, b), comms=comms or ())
    out = out.reshape(N_DEV, m // N_DEV, n)
    return out if comms is None else (out, got)


def _mix_proj_fwd(x, gain, b_gate, gw, comms=()):
    t = x.shape[0]

    def body(x_ref, g_ref, bg_ref, gw_ref, q_ref, k_ref, v_ref, zs_ref, gt_ref, h_ref, win, sems):
        @pl.when(pl.program_id(0) == 0)
        def _():
            for cp in _load_rows(gw_ref, win, OFF_IN, R_IN, sems):
                cp.wait()

        xn, _ = _rms(x_ref[...])
        h = (xn * g_ref[...]).astype(BF16)
        h_ref[...] = h
        q_ref[...] = (_nt(h, win[0:512, :]) * 0.125).astype(BF16)
        k_ref[...] = _nt(h, win[512:1024, :]).astype(BF16)
        v_ref[...] = _nt(h, win[1024:1536, :]).astype(BF16)
        for c in range(2):
            zs_ref[:, c * 512:(c + 1) * 512] = _nt(h, win[1536 + c * 512:2048 + c * 512, :]).astype(BF16)
        for c in range(4):
            zg = _nt(h, win[2560 + c * 512:3072 + c * 512, :]) + bg_ref[:, c * 512:(c + 1) * 512]
            gt_ref[:, c * 512:(c + 1) * 512] = jax.nn.sigmoid(zg).astype(BF16)

    def tile(w):
        return pl.BlockSpec((TM, w), lambda i: (i, 0))

    return _call(
        body, name="mix_proj_fwd", grid=(t // TM,),
        in_specs=[tile(D), pl.BlockSpec((1, D), lambda i: (0, 0)), pl.BlockSpec((1, 2 * D), lambda i: (0, 0)), ANY],
        out_specs=[tile(D_ATT), tile(D_ATT), tile(D_ATT), tile(2 * D_SGU), tile(2 * D), tile(D)],
        out_shape=[jax.ShapeDtypeStruct((t, D_ATT), BF16)] * 3 + [jax.ShapeDtypeStruct((t, 2 * D_SGU), BF16),
                                                                   jax.ShapeDtypeStruct((t, 2 * D), BF16),
                                                                   jax.ShapeDtypeStruct((t, D), BF16)],
        scratch_shapes=[pltpu.VMEM((D_IN, D), BF16), pltpu.SemaphoreType.DMA((N_DEV,))],
        args=(x, gain, b_gate, gw), comms=comms)


def _mix_proj_bwd(dz, x, gain, dres, gw, comms=()):
    t = x.shape[0]

    def body(dz_ref, x_ref, g_ref, dr_ref, gw_ref, dx_ref, dg_ref, win, sems):
        @pl.when(pl.program_id(0) == 0)
        def _():
            cps = _load_rows(gw_ref, win, OFF_IN, R_IN, sems)
            dg_ref[...] = jnp.zeros_like(dg_ref)
            for cp in cps:
                cp.wait()

        dh = jnp.zeros((TM, D), F32)
        for c in range(D_IN // 512):
            dh = dh + _nn(dz_ref[:, c * 512:(c + 1) * 512], win[c * 512:(c + 1) * 512, :])
        xn, r = _rms(x_ref[...])
        dxn, dg = _rms_bwd(dh, xn, r, g_ref[...])
        dg_ref[...] += dg
        dx_ref[...] = dr_ref[...] + dxn

    def tile(w):
        return pl.BlockSpec((TM, w), lambda i: (i, 0))

    row = pl.BlockSpec((1, D), lambda i: (0, 0))
    return _call(
        body, name="mix_proj_bwd", grid=(t // TM,),
        in_specs=[tile(D_IN), tile(D), row, tile(D), ANY], out_specs=[tile(D), row],
        out_shape=[jax.ShapeDtypeStruct((t, D), F32), jax.ShapeDtypeStruct((1, D), F32)],
        scratch_shapes=[pltpu.VMEM((D_IN, D), BF16), pltpu.SemaphoreType.DMA((N_DEV,))],
        args=(dz, x, gain, dres, gw), comms=comms)


SKEW_W = KW + QB
N_CAP = 2 * QB - REL_CLIP + 1


def _band_bias(rel_bias):
    cap = rel_bias[:, 2 * REL_CLIP:]
    diag = jnp.concatenate([jnp.broadcast_to(cap, (HEADS, N_CAP)), rel_bias[:, 2 * REL_CLIP - 1::-1],
                            jnp.broadcast_to(cap, (HEADS, SKEW_W - N_CAP - 2 * REL_CLIP))], axis=1)

    def body(d_ref, o_ref):
        lag = lax.broadcasted_iota(jnp.int32, (QB, KW), 1) // CHUNK - lax.broadcasted_iota(jnp.int32, (QB, KW), 0) // CHUNK
        band = (lag >= 0) & (lag <= N_LEFT)
        for h in range(HEADS):
            rows = jnp.broadcast_to(d_ref[h:h + 1, :], (QB, SKEW_W))
            o_ref[h] = jnp.where(band, pltpu.roll(rows, 0, 1, stride=1, stride_axis=0)[:, :KW], NEG_INF)

    return pl.pallas_call(body, name="band_bias", out_shape=jax.ShapeDtypeStruct((HEADS, QB, KW), F32))(diag)


def _att_specs():
    qspec = pl.BlockSpec((QB, D_ATT), lambda g: (g, 0))
    kspecs = [pl.BlockSpec((QB, D_ATT), lambda g: (jnp.maximum(g - 2, 0), 0)),
              pl.BlockSpec((QB, D_ATT), lambda g: (jnp.maximum(g - 1, 0), 0)), qspec]
    bspec = pl.BlockSpec((HEADS, QB, KW), lambda g: (0, 0, 0))
    return qspec, kspecs, bspec


def _att_probs(qm, kp, bias, valid):
    s = jnp.where(valid, _nt(qm, kp) + bias, NEG_INF)
    e = jnp.exp(s - jnp.max(s, axis=-1, keepdims=True))
    return e / jnp.sum(e, axis=-1, keepdims=True)


def _att_valid():
    g = pl.program_id(0)
    blk = lax.broadcasted_iota(jnp.int32, (QB, KW), 1) // QB
    return (blk + g) >= 2


def _att_fwd(q, k, v, bias, comms=()):
    t = q.shape[0]

    def body(q_ref, k0, k1, k2, v0, v1, v2, b_ref, y_ref):
        valid = _att_valid()
        first = lax.broadcasted_iota(jnp.int32, (1, 128), 1) < 64
        for p in range(HEADS // 2):
            lanes = slice(p * 128, (p + 1) * 128)
            qp = q_ref[:, lanes]
            kp = jnp.concatenate([k0[:, lanes], k1[:, lanes], k2[:, lanes]], axis=0)
            vp = jnp.concatenate([v0[:, lanes], v1[:, lanes], v2[:, lanes]], axis=0)
            out = jnp.zeros((QB, 128), F32)
            for hh in range(2):
                mask = first if hh == 0 else jnp.logical_not(first)
                pr = _att_probs(jnp.where(mask, qp, 0), kp, b_ref[2 * p + hh], valid)
                out = out + _nn(pr.astype(BF16), jnp.where(mask, vp, 0))
            y_ref[:, lanes] = out.astype(BF16)

    qspec, kspecs, bspec = _att_specs()
    (out,), got = _call(
        body, name="att_fwd", grid=(t // QB,),
        in_specs=[qspec] + kspecs + kspecs + [bspec], out_specs=[qspec],
        out_shape=[jax.ShapeDtypeStruct((t, D_ATT), BF16)], scratch_shapes=[],
        args=(q, k, k, k, v, v, v, bias), comms=comms)
    return out, got


def _att_bwd(q, k, v, bias, dy, comms=()):
    t = q.shape[0]
    n_blocks = t // QB

    def body(q_ref, k0, k1, k2, v0, v1, v2, b_ref, dy_ref, dq_ref, dk_ref, dv_ref, db_ref, dk_acc, dv_acc):
        g = pl.program_id(0)

        @pl.when(g == 0)
        def _():
            db_ref[...] = jnp.zeros_like(db_ref)
            dk_acc[...] = jnp.zeros_like(dk_acc)
            dv_acc[...] = jnp.zeros_like(dv_acc)

        valid = _att_valid()
        first = lax.broadcasted_iota(jnp.int32, (1, 128), 1) < 64
        for p in range(HEADS // 2):
            lanes = slice(p * 128, (p + 1) * 128)
            qp = q_ref[:, lanes]
            dyp = dy_ref[:, lanes]
            kp = jnp.concatenate([k0[:, lanes], k1[:, lanes], k2[:, lanes]], axis=0)
            vp = jnp.concatenate([v0[:, lanes], v1[:, lanes], v2[:, lanes]], axis=0)
            dq = jnp.zeros((QB, 128), F32)
            dk = jnp.zeros((KW, 128), F32)
            dv = jnp.zeros((KW, 128), F32)
            for hh in range(2):
                mask = first if hh == 0 else jnp.logical_not(first)
                qm = jnp.where(mask, qp, 0)
                dym = jnp.where(mask, dyp, 0)
                pr = _att_probs(qm, kp, b_ref[2 * p + hh], valid)
                dp = _nt(dym, vp)
                ds = pr * (dp - jnp.sum(dp * pr, axis=-1, keepdims=True))
                db_ref[2 * p + hh] += ds
                dsb = ds.astype(BF16)
                dq = dq + _nn(dsb, jnp.where(mask, kp, 0))
                dk = dk + _tn(dsb, qm)
                dv = dv + _tn(pr.astype(BF16), dym)
            dq_ref[:, lanes] = (dq * 0.125).astype(BF16)
            for j in range(3):
                rows = pl.ds(pl.multiple_of(jnp.maximum(g - 2 + j, 0) * QB, QB), QB)
                dk_acc[rows, lanes] += dk[j * QB:(j + 1) * QB]
                dv_acc[rows, lanes] += dv[j * QB:(j + 1) * QB]

        @pl.when(g == n_blocks - 1)
        def _():
            dk_ref[...] = dk_acc[...].astype(BF16)
            dv_ref[...] = dv_acc[...].astype(BF16)

    qspec, kspecs, bspec = _att_specs()
    full = pl.BlockSpec((t, D_ATT), lambda g: (0, 0))
    return _call(
        body, name="att_bwd", grid=(n_blocks,),
        in_specs=[qspec] + kspecs + kspecs + [bspec, qspec], out_specs=[qspec, full, full, bspec],
        out_shape=[jax.ShapeDtypeStruct((t, D_ATT), BF16)] * 3 + [jax.ShapeDtypeStruct((HEADS, QB, KW), F32)],
        scratch_shapes=[pltpu.VMEM((t, D_ATT), F32)] * 2,
        args=(q, k, k, k, v, v, v, bias, dy), comms=comms)


def _rel_bias_grad(dbias):
    def body(db_ref, cs_ref, tot_ref):
        lane = lax.broadcasted_iota(jnp.int32, (1, SKEW_W), 1)
        capped = (lane < N_CAP) | (lane > KW)
        pad = jnp.zeros((8, QB), F32)
        for h in range(HEADS):
            def add_rows(a, z):
                rows = jnp.concatenate([db_ref[h, pl.ds(pl.multiple_of(8 * a, 8), 8), :], pad], axis=1)
                return z + pltpu.roll(rows, (SKEW_W - 8 * a) % SKEW_W, 1)

            z = lax.fori_loop(0, QB // 8, add_rows, jnp.zeros((8, SKEW_W), F32))
            cs = z[0:1, :]
            for b in range(1, 8):
                cs = cs + pltpu.roll(z[b:b + 1, :], SKEW_W - b, 1)
            cs_ref[h:h + 1, :] = cs
            tot_ref[h:h + 1, :] = jnp.broadcast_to(jnp.sum(jnp.where(capped, cs, 0.0), axis=1, keepdims=True), (1, 128))

    cs, tot = pl.pallas_call(
        body, name="rel_bias_grad",
        out_shape=[jax.ShapeDtypeStruct((HEADS, SKEW_W), F32), jax.ShapeDtypeStruct((HEADS, 128), F32)],
    )(dbias)
    return jnp.concatenate([cs[:, KW:N_CAP - 1:-1], tot[:, :1]], axis=1)


def _sgu_mask():
    pos = np.arange(SGU_BLOCK)
    return (pos[:, None] // CHUNK) >= (pos[None, :] // CHUNK)


def _group_stack(blk, first):
    return jnp.concatenate([jnp.where(first, blk, 0), jnp.where(first, 0, blk)], axis=0)


def _sgu_norm(zs_ref, lng, lnb):
    zs = zs_ref[...].astype(F32)
    ga, th = _gelu(zs)
    u, vs = ga[:, :D_SGU], ga[:, D_SGU:]
    mu = jnp.mean(vs, axis=-1, keepdims=True)
    cen = vs - mu
    rstd = lax.rsqrt(jnp.mean(cen * cen, axis=-1, keepdims=True) + EPS)
    xhat = cen * rstd
    return zs, th, u, xhat, rstd, xhat * lng + lnb


def _sgu_mix(vb, wm2_ref, bsx, s_ref):
    first = lax.broadcasted_iota(jnp.int32, (1, 128), 1) < 64
    for n in range(TM // SGU_BLOCK):
        for p in range(4):
            blk = vb[n * 128:(n + 1) * 128, p * 128:(p + 1) * 128]
            s_ref[n * 128:(n + 1) * 128, p * 128:(p + 1) * 128] = _nn(wm2_ref[p], _group_stack(blk, first)) + bsx[:, p * 128:(p + 1) * 128]


def _merge_fwd(x, zs, gt, y_att, lng, lnb, wm2, bsx, gw):
    t = x.shape[0]

    def body(x_ref, zs_ref, gt_ref, ya_ref, lng_ref, lnb_ref, wm2_ref, bsx_ref, gw_ref, xo_ref, ys_ref, mg_ref,
             wbr, wo, s_scr, sems):
        @pl.when(pl.program_id(0) == 0)
        def _():
            for cp in _load_rows(gw_ref, wbr, OFF_BR, R_BR, sems.at[0]) + _load_rows(gw_ref, wo, OFF_WO, R_WO, sems.at[1]):
                cp.wait()

        _, _, u, _, _, vsn = _sgu_norm(zs_ref, lng_ref[...], lnb_ref[...])
        _sgu_mix(vsn.astype(BF16), wm2_ref, bsx_ref[...], s_scr)
        ys = (u * s_scr[...]).astype(BF16)
        ys_ref[...] = ys
        pa = _nt(ya_ref[...], wbr[:, :D_ATT])
        ps = _nt(ys, wbr[:, D_ATT:])
        mg = (gt_ref[:, :D].astype(F32) * pa + gt_ref[:, D:].astype(F32) * ps).astype(BF16)
        mg_ref[...] = mg
        xo_ref[...] = x_ref[...] + _nn(mg, wo[...])

    def tile(w):
        return pl.BlockSpec((TM, w), lambda i: (i, 0))

    def const(shape):
        return pl.BlockSpec(shape, lambda i: (0,) * len(shape))

    return pl.pallas_call(
        body, name="merge_fwd", grid=(t // TM,),
        in_specs=[tile(D), tile(2 * D_SGU), tile(2 * D), tile(D_ATT), const((1, D_SGU)), const((1, D_SGU)),
                  const((4, 128, 256)), const((128, D_SGU)), ANY],
        out_specs=[tile(D), tile(D_SGU), tile(D)],
        out_shape=[jax.ShapeDtypeStruct((t, D), F32), jax.ShapeDtypeStruct((t, D_SGU), BF16), jax.ShapeDtypeStruct((t, D), BF16)],
        scratch_shapes=[pltpu.VMEM((D, D), BF16), pltpu.VMEM((D, D), BF16), pltpu.VMEM((TM, D_SGU), F32),
                        pltpu.SemaphoreType.DMA((2, N_DEV))],
        compiler_params=_cparams(("arbitrary",)),
    )(x, zs, gt, y_att, lng, lnb, wm2, bsx, gw)


def _merge_bwd(dx, gt, y_att, y_sgu, gw):
    t = dx.shape[0]

    def body(dx_ref, gt_ref, ya_ref, ys_ref, gw_ref, dzg_ref, dya_ref, dys_ref, dpp_ref, dxb_ref, dbg_ref, wbr, wo, sems):
        @pl.when(pl.program_id(0) == 0)
        def _():
            cps = _load_rows(gw_ref, wbr, OFF_BR, R_BR, sems.at[0]) + _load_rows(gw_ref, wo, OFF_WO, R_WO, sems.at[1])
            dbg_ref[...] = jnp.zeros_like(dbg_ref)
            for cp in cps:
                cp.wait()

        dxb = dx_ref[...].astype(BF16)
        dxb_ref[...] = dxb
        dm = _nt(dxb, wo[...])
        for half, y_ref, w in ((0, ya_ref, wbr.at[:, :D_ATT]), (1, ys_ref, wbr.at[:, D_ATT:])):
            cols = slice(half * D, (half + 1) * D)
            gate = gt_ref[:, cols].astype(F32)
            branch = _nt(y_ref[...], w[...])
            dzg = dm * branch * gate * (1.0 - gate)
            dbg_ref[:, cols] += jnp.sum(dzg, axis=0, keepdims=True)
            dzg_ref[:, cols] = dzg.astype(BF16)
            dbr = (dm * gate).astype(BF16)
            dpp_ref[:, cols] = dbr
            dy = _nn(dbr, w[...])
            if half == 0:
                dya_ref[...] = dy.astype(BF16)
            else:
                dys_ref[...] = dy

    def tile(w):
        return pl.BlockSpec((TM, w), lambda i: (i, 0))

    return pl.pallas_call(
        body, name="merge_bwd", grid=(t // TM,),
        in_specs=[tile(D), tile(2 * D), tile(D_ATT), tile(D_SGU), ANY],
        out_specs=[tile(2 * D), tile(D_ATT), tile(D_SGU), tile(2 * D), tile(D), pl.BlockSpec((1, 2 * D), lambda i: (0, 0))],
        out_shape=[jax.ShapeDtypeStruct((t, 2 * D), BF16), jax.ShapeDtypeStruct((t, D_ATT), BF16), jax.ShapeDtypeStruct((t, D_SGU), F32),
                   jax.ShapeDtypeStruct((t, 2 * D), BF16), jax.ShapeDtypeStruct((t, D), BF16), jax.ShapeDtypeStruct((1, 2 * D), F32)],
        scratch_shapes=[pltpu.VMEM((D, D), BF16), pltpu.VMEM((D, D), BF16), pltpu.SemaphoreType.DMA((2, N_DEV))],
        compiler_params=_cparams(("arbitrary",)),
    )(dx, gt, y_att, y_sgu, gw)


def _sgu_bwd(zs, dys, lng, lnb, wm2, wmt2, bsx, comms=()):
    t = zs.shape[0]
    n_steps = t // TM

    def body(zs_ref, dys_ref, lng_ref, lnb_ref, wm2_ref, wmt2_ref, bsx_ref, dzs_ref, dw_ref, dbs_ref, dlg_ref, dlb_ref,
             s_scr, dv_scr, ds_acc):
        i = pl.program_id(0)

        @pl.when(i == 0)
        def _():
            dw_ref[...] = jnp.zeros_like(dw_ref)
            dlg_ref[...] = jnp.zeros_like(dlg_ref)
            dlb_ref[...] = jnp.zeros_like(dlb_ref)
            ds_acc[...] = jnp.zeros_like(ds_acc)

        lng = lng_ref[...]
        zs, th, u, xhat, rstd, vsn = _sgu_norm(zs_ref, lng, lnb_ref[...])
        vb = vsn.astype(BF16)
        _sgu_mix(vb, wm2_ref, bsx_ref[...], s_scr)
        dys = dys_ref[...]
        du = dys * s_scr[...]
        ds = dys * u
        dsb = ds.astype(BF16)
        first = lax.broadcasted_iota(jnp.int32, (1, 128), 1) < 64
        acc = jnp.zeros((SGU_BLOCK, D_SGU), F32)
        for n in range(TM // SGU_BLOCK):
            rows = slice(n * 128, (n + 1) * 128)
            acc = acc + ds[rows]
            for p in range(4):
                lanes = slice(p * 128, (p + 1) * 128)
                stack = _group_stack(dsb[rows, lanes], first)
                dv_scr[rows, lanes] = _nn(wmt2_ref[p], stack)
                dw_ref[p] += _nt(stack, vb[rows, lanes])
        ds_acc[...] += acc
        dvsn = dv_scr[...]
        dlg_ref[...] += jnp.sum(dvsn * xhat, axis=0, keepdims=True)
        dlb_ref[...] += jnp.sum(dvsn, axis=0, keepdims=True)
        dxh = dvsn * lng
        dvs = rstd * (dxh - jnp.mean(dxh, axis=-1, keepdims=True) - xhat * jnp.mean(dxh * xhat, axis=-1, keepdims=True))
        dga = jnp.concatenate([du, dvs], axis=1)
        dzs_ref[...] = (dga * _gelu_grad(zs, th)).astype(BF16)

        @pl.when(i == n_steps - 1)
        def _():
            r = lax.broadcasted_iota(jnp.int32, (256, 128), 0) % SGU_BLOCK
            c = lax.broadcasted_iota(jnp.int32, (256, 128), 1)
            keep = (r // CHUNK) >= (c // CHUNK)
            for p in range(4):
                dw_ref[p] = jnp.where(keep, dw_ref[p], 0.0)
            total = ds_acc[...]
            grp = lax.broadcasted_iota(jnp.int32, (SGU_BLOCK, D_SGU), 1) // 64
            lane = lax.broadcasted_iota(jnp.int32, (SGU_BLOCK, 128), 1)
            out = jnp.zeros((SGU_BLOCK, 128), F32)
            for gi in range(8):
                out = jnp.where(lane == gi, jnp.sum(jnp.where(grp == gi, total, 0.0), axis=1, keepdims=True), out)
            dbs_ref[...] = out

    def tile(w):
        return pl.BlockSpec((TM, w), lambda i: (i, 0))

    def const(shape):
        return pl.BlockSpec(shape, lambda i: (0,) * len(shape))

    return _call(
        body, name="sgu_bwd", grid=(n_steps,),
        in_specs=[tile(2 * D_SGU), tile(D_SGU), const((1, D_SGU)), const((1, D_SGU)), const((4, 128, 256)), const((4, 128, 256)),
                  const((128, D_SGU))],
        out_specs=[tile(2 * D_SGU), const((4, 256, 128)), const((128, 128)), const((1, D_SGU)), const((1, D_SGU))],
        out_shape=[jax.ShapeDtypeStruct((t, 2 * D_SGU), BF16), jax.ShapeDtypeStruct((4, 256, 128), F32),
                   jax.ShapeDtypeStruct((128, 128), F32), jax.ShapeDtypeStruct((1, D_SGU), F32), jax.ShapeDtypeStruct((1, D_SGU), F32)],
        scratch_shapes=[pltpu.VMEM((TM, D_SGU), F32), pltpu.VMEM((TM, D_SGU), F32), pltpu.VMEM((SGU_BLOCK, D_SGU), F32)],
        args=(zs, dys, lng, lnb, wm2, wmt2, bsx), comms=comms)


def _loss_head(x, gain, target):
    t = x.shape[0]

    def body(x_ref, g_ref, t_ref, dx_ref, dg_ref, loss_ref):
        @pl.when(pl.program_id(0) == 0)
        def _():
            dg_ref[...] = jnp.zeros_like(dg_ref)
            loss_ref[...] = jnp.zeros_like(loss_ref)

        gain_v = g_ref[...]
        xn, r = _rms(x_ref[...])
        err = xn * gain_v - t_ref[...]
        loss_ref[...] += 0.5 * jnp.sum(jnp.mean(err * err, axis=-1, keepdims=True), axis=0, keepdims=True)
        dxn, dg = _rms_bwd(err * (1.0 / D), xn, r, gain_v)
        dg_ref[...] += dg
        dx_ref[...] = dxn

    tile = pl.BlockSpec((TM, D), lambda i: (i, 0))
    row = pl.BlockSpec((1, D), lambda i: (0, 0))
    return pl.pallas_call(
        body, name="loss_head", grid=(t // TM,),
        in_specs=[tile, row, tile], out_specs=[tile, row, pl.BlockSpec((1, 128), lambda i: (0, 0))],
        out_shape=[jax.ShapeDtypeStruct((t, D), F32), jax.ShapeDtypeStruct((1, D), F32), jax.ShapeDtypeStruct((1, 128), F32)],
        compiler_params=_cparams(("arbitrary",)),
    )(x, gain, target)


def _adamw(g, w, m, v):
    m = ADAM_B1 * m + (1.0 - ADAM_B1) * g
    v = ADAM_B2 * v + (1.0 - ADAM_B2) * (g * g)
    m_hat = m / (1.0 - ADAM_B1 ** ADAM_STEP)
    v_hat = v / (1.0 - ADAM_B2 ** ADAM_STEP)
    return -ADAM_LR * (m_hat / (jnp.sqrt(v_hat) + ADAM_EPS) + ADAM_WD * w), m, v


def _adamw_matrix(parts, sums, w, m, v, transposed, name):
    _, r, c = parts.shape
    tc = 256

    def body(own_ref, p_ref, s_ref, w_ref, m_ref, v_ref, g_ref, d_ref, mo_ref, vo_ref):
        g = p_ref[0].astype(F32) + p_ref[1].astype(F32) + p_ref[2].astype(F32) + s_ref[...].astype(F32)
        g = g.T if transposed else g
        g_ref[...] = g
        d_ref[...], mo_ref[...], vo_ref[...] = _adamw(g, w_ref[...], m_ref[...], v_ref[...])

    own = pl.BlockSpec((None, tc, r), lambda i, o: (0, i, 0)) if transposed else pl.BlockSpec((None, r, tc), lambda i, o: (0, 0, i))
    return pl.pallas_call(
        body, name=name,
        grid_spec=pltpu.PrefetchScalarGridSpec(
            num_scalar_prefetch=1, grid=(c // tc,),
            in_specs=[pl.BlockSpec((3, r, tc), lambda i, o: (0, 0, i)), pl.BlockSpec((None, r, tc), lambda i, o: (o[0], 0, i)), own, own, own],
            out_specs=[own] * 4),
        out_shape=[jax.ShapeDtypeStruct(w.shape, F32)] * 4,
        compiler_params=_cparams(("arbitrary",)),
    )(_my_index(("x", 2), ("y", 1)), parts, sums, w, m, v)


def _sum_adamw_small(parts, late, w, m, v):
    def body(p_ref, l_ref, w_ref, m_ref, v_ref, *outs):
        g = p_ref[0]
        top = l_ref[0]
        for k in range(1, N_DEV):
            g = g + p_ref[k]
            top = top + l_ref[k]
        for rows, gr in ((slice(0, 8), g[0:8] + top), (slice(8, None), g[8:])):
            res = (gr,) + _adamw(gr, w_ref[rows, :], m_ref[rows, :], v_ref[rows, :])
            for o_ref, val in zip(outs, res):
                o_ref[rows, :] = val

    return pl.pallas_call(body, name="adamw_small", out_shape=[jax.ShapeDtypeStruct(w.shape, F32)] * 4,
                          compiler_params=_cparams())(parts, late, w, m, v)


def _pack_rows(groups, name):
    flat = [a for grp in groups for a, _ in grp]
    rows = [grp[0][0].shape[2] if grp[0][1] else grp[0][0].shape[1] for grp in groups]

    def body(*refs):
        o_ref, pos, off = refs[-1], 0, 0
        for grp, r in zip(groups, rows):
            vals = []
            for _, transposed in grp:
                val = refs[pos][0]
                vals.append(val.T if transposed else val)
                pos += 1
            o_ref[off:off + r, :] = (vals[0] if len(vals) == 1 else jnp.concatenate(vals, axis=1)).astype(BF16)
            off += r

    return pl.pallas_call(body, name=name, out_shape=jax.ShapeDtypeStruct((sum(rows), D), BF16), compiler_params=_cparams())(*flat)


def _pad_row(a):
    a = a.reshape(1, -1)
    return jnp.pad(a, ((0, 0), (0, D - a.shape[1])))


def _pack_small(loss, n1, nm, n2, nf, bg, lng, lnb, bs, rel, ws):
    rows = [_pad_row(loss), n1.reshape(1, D), nm.reshape(1, D), n2.reshape(1, D), nf.reshape(1, D), bg.reshape(2, D),
            jnp.concatenate([lng.reshape(1, D_SGU), lnb.reshape(1, D_SGU)], axis=1), bs.reshape(1, D),
            jnp.pad(rel.reshape(HEADS, N_REL), ((0, 0), (0, D - N_REL))), jnp.zeros((ROW_WS - ROW_REL - HEADS, D), F32),
            ws.reshape(128, D)]
    return jnp.concatenate(rows, axis=0).astype(F32)


def _unpack_small(s):
    return {"loss": s[ROW_LOSS, 0], "norm_ffn1": s[ROW_N1][None], "norm_mix": s[ROW_NM][None], "norm_ffn2": s[ROW_N2][None],
            "norm_final": s[ROW_NF], "b_gate": s[ROW_BG:ROW_BG + 2].reshape(1, 2 * D),
            "sgu_ln_g": s[ROW_LN, :D_SGU][None], "sgu_ln_b": s[ROW_LN, D_SGU:][None], "sgu_b_s": s[ROW_BS].reshape(1, 8, 128),
            "rel_bias": s[ROW_REL:ROW_REL + HEADS, :N_REL][None], "sgu_w_s": s[ROW_WS:].reshape(1, 8, 128, 128)}


_SMALL = ("norm_ffn1", "norm_mix", "norm_ffn2", "norm_final", "b_gate", "sgu_ln_g", "sgu_ln_b", "sgu_b_s", "rel_bias", "sgu_w_s")


def _pack_small_params(p, prefix=""):
    g = lambda n: p[prefix + n]
    return _pack_small(jnp.zeros((1,), F32), g("norm_ffn1"), g("norm_mix"), g("norm_ffn2"), g("norm_final"), g("b_gate"),
                       g("sgu_ln_g"), g("sgu_ln_b"), g("sgu_b_s"), g("rel_bias"), g("sgu_w_s"))


def _step(x, target, p):
    n1, nm, n2 = p["norm_ffn1"], p["norm_mix"], p["norm_ffn2"]
    nf = p["norm_final"].reshape(1, D)
    lng, lnb = p["sgu_ln_g"], p["sgu_ln_b"]
    w_m = jnp.where(jnp.asarray(_sgu_mask())[None], p["sgu_w_s"][0], 0.0).astype(BF16)
    wm2 = jnp.concatenate([w_m[0::2], w_m[1::2]], axis=2)
    w_mt = w_m.transpose(0, 2, 1)
    wmt2 = jnp.concatenate([w_mt[0::2], w_mt[1::2]], axis=2)
    bsx = jnp.repeat(p["sgu_b_s"][0].T, 64, axis=1)
    bias = _band_bias(p["rel_bias"][0])

    def chip_sums(grads, name):
        gots = _comm_only([_SiblingSwap(grads)], "swap_" + name)
        return _pair_sums(grads, gots, "pair_sums_" + name)

    def as_rows(a):
        return jnp.swapaxes(a, 1, 2)

    def updates(parts, sums, names):
        res = {}
        for pt, sm, n in zip(parts, sums, names):
            if p[n].shape[1:] == pt.shape[1:]:
                res[n] = _adamw_matrix(pt, sm, p[n], p["m_" + n], p["v_" + n], False, "adamw_" + n)
            elif p[n].shape[2] > 128:
                res[n] = [as_rows(o) for o in _adamw_matrix(pt, sm, as_rows(p[n]), as_rows(p["m_" + n]), as_rows(p["v_" + n]), False,
                                                            "adamw_" + n)]
            else:
                res[n] = _adamw_matrix(pt, sm, p[n], p["m_" + n], p["v_" + n], True, "adamw_" + n)
        return res

    rows1 = _pack_rows([[(as_rows(p["ffn1_w_gate"]), False)], [(as_rows(p["ffn1_w_up"]), False)], [(p["ffn1_w_down"], False)]], "pack_ffn1")
    rows_m = _pack_rows([[(as_rows(p["w_in"]), False)], [(p["w_branch_att"], True), (p["w_branch_sgu"], True)], [(p["w_out"], False)]],
                        "pack_mixer")
    rows2d = _pack_rows([[(p["ffn2_w_down"], False)]], "pack_ffn2_down")
    rows2gu = _pack_rows([[(as_rows(p["ffn2_w_gate"]), False)], [(as_rows(p["ffn2_w_up"]), False)]], "pack_ffn2_gate_up")
    (gw1,) = _comm_only([_Gather(rows1)], "gather_ffn1")
    x1, ab1, (gwm,) = _ffn_fwd(x, n1, gw1, gw1, 2 * R_FF, "ffn1_fwd", [_Gather(rows_m)])
    (q, k, v, zs, gt, h2), (gw2d,) = _mix_proj_fwd(x1, nm, p["b_gate"], gwm, [_Gather(rows2d)])
    y_att, (gw2gu,) = _att_fwd(q, k, v, bias, [_Gather(rows2gu)])
    x2, y_sgu, merged = _merge_fwd(x1, zs, gt, y_att, lng, lnb, wm2, bsx, gwm)
    x3, ab2, _ = _ffn_fwd(x2, n2, gw2gu, gw2d, 0, "ffn2_fwd")
    dx3, d_nf, loss = _loss_head(x3, nf, target)

    (dx2, d_n2, hb, dab, sb, dfb), _ = _ffn_bwd(x2, n2, ab2, dx3, gw2gu, gw2d, 0, "ffn2_bwd")
    g2 = [_weight_grad(dab, hb, "ffn2_dw_gate", 0, F), _weight_grad(dab, hb, "ffn2_dw_up", F, F), _weight_grad(sb, dfb, "ffn2_dw_down")]
    dzg, dya, dys, dpp, dxb, d_bg = _merge_bwd(dx2, gt, y_att, y_sgu, gwm)
    g_late = g2 + [_weight_grad(dpp, y_att, "dw_branch_att", 0, D), _weight_grad(dpp, y_sgu, "dw_branch_sgu", D, D),
                   _weight_grad(merged, dxb, "dw_out")]
    late = ("ffn2_w_gate", "ffn2_w_up", "ffn2_w_down", "w_branch_att", "w_branch_sgu", "w_out")
    (dzs, d_wm, d_bs, d_lng, d_lnb), gots_late = _sgu_bwd(zs, dys, lng, lnb, wm2, wmt2, bsx, [_SiblingSwap(g_late)])
    sums_late = _pair_sums(g_late, gots_late, "pair_sums_late")
    (dq, dk, dv, d_bias), parts_late = _att_bwd(q, k, v, bias, dya, [_ChipScatter(sums_late)])
    big = updates(parts_late, sums_late, late)
    d_rel = _rel_bias_grad(d_bias)
    dz = jnp.concatenate([dq, dk, dv, dzs, dzg], axis=1)
    sums_in = chip_sums([_weight_grad(dz, h2, "dw_in")], "w_in")
    (dx1, d_nm), parts_in = _mix_proj_bwd(dz, x1, nm, dx2, gwm, [_ChipScatter(sums_in)])
    big.update(updates(parts_in, sums_in, ("w_in",)))

    (dx0, d_n1, hb, dab, sb, dfb), _ = _ffn_bwd(x, n1, ab1, dx1, gw1, gw1, 2 * R_FF, "ffn1_bwd")
    small = _pack_small(loss[0, :1], jnp.zeros_like(d_n1), d_nm, d_n2, d_nf, d_bg, d_lng, d_lnb, d_bs[:, :8].T, d_rel,
                        d_wm.reshape(8, 128, 128))
    small_late = jnp.zeros((8, D), F32).at[ROW_N1].set(d_n1[0])
    g_down, (small_parts,) = _weight_grad(sb, dfb, "ffn1_dw_down", comms=[_AllToAll(small)])
    sums_d = chip_sums([g_down], "ffn1_down")
    g_gate, parts_d = _weight_grad(dab, hb, "ffn1_dw_gate", 0, F, [_ChipScatter(sums_d)])
    sums_g = chip_sums([g_gate], "ffn1_gate")
    g_up, (parts_g, late_parts) = _weight_grad(dab, hb, "ffn1_dw_up", F, F, [_ChipScatter(sums_g), _AllToAll(small_late)])
    sums_u = chip_sums([g_up], "ffn1_up")
    parts_u = _comm_only([_ChipScatter(sums_u)], "scatter_ffn1_up")
    big.update(updates(parts_d + [parts_g] + parts_u, sums_d + sums_g + sums_u, ("ffn1_w_down", "ffn1_w_gate", "ffn1_w_up")))
    out_s = _sum_adamw_small(small_parts, late_parts, _pack_small_params(p), _pack_small_params(p, "m_"), _pack_small_params(p, "v_"))
    return dx0, [{**{n: four[i] for n, four in big.items()}, **_unpack_small(s)} for i, s in enumerate(out_s)]


_OUT_ORDER = ("norm_ffn1", "ffn1_w_gate", "ffn1_w_up", "ffn1_w_down", "norm_mix", "w_in", "b_gate", "rel_bias", "sgu_ln_g", "sgu_ln_b",
              "sgu_w_s", "sgu_b_s", "w_branch_att", "w_branch_sgu", "w_out", "norm_ffn2", "ffn2_w_gate", "ffn2_w_up", "ffn2_w_down",
              "norm_final")


def kernel(x, norm_ffn1, ffn1_w_gate, ffn1_w_up, ffn1_w_down, norm_mix, w_in, b_gate, rel_bias, sgu_ln_g, sgu_ln_b, sgu_w_s, sgu_b_s, w_branch_att, w_branch_sgu, w_out, norm_ffn2, ffn2_w_gate, ffn2_w_up, ffn2_w_down, norm_final, loss_target, m_norm_ffn1, m_ffn1_w_gate, m_ffn1_w_up, m_ffn1_w_down, m_norm_mix, m_w_in, m_b_gate, m_rel_bias, m_sgu_ln_g, m_sgu_ln_b, m_sgu_w_s, m_sgu_b_s, m_w_branch_att, m_w_branch_sgu, m_w_out, m_norm_ffn2, m_ffn2_w_gate, m_ffn2_w_up, m_ffn2_w_down, m_norm_final, v_norm_ffn1, v_ffn1_w_gate, v_ffn1_w_up, v_ffn1_w_down, v_norm_mix, v_w_in, v_b_gate, v_rel_bias, v_sgu_ln_g, v_sgu_ln_b, v_sgu_w_s, v_sgu_b_s, v_w_branch_att, v_w_branch_sgu, v_w_out, v_norm_ffn2, v_ffn2_w_gate, v_ffn2_w_up, v_ffn2_w_down, v_norm_final):
    args = dict(locals())
    dx, outs = _step(x[0], loss_target[0], {pre + n: args[pre + n] for pre in ("", "m_", "v_") for n in _OUT_ORDER})
    return (outs[0]["loss"], dx[None], *[o[n] for o in outs for n in _OUT_ORDER])
```

```python
import functools

import numpy as np
import jax
import jax.numpy as jnp
from jax import lax
from jax.experimental import pallas as pl
from jax.experimental.pallas import tpu as pltpu

F32 = jnp.float32
BF16 = jnp.bfloat16

N_DEV = 8
D = 1024
F = 2816
D_ATT = 512
D_SGU = 512
D_IN = 4608
HEADS = 8
CHUNK = 64
N_LEFT = 8
REL_CLIP = 256
N_REL = 2 * REL_CLIP + 1
SGU_BLOCK = 128
EPS = 1e-6
NEG_INF = -1e30
QB = 256
KW = 3 * QB

R_FF, R_IN, R_BR, R_WO = F // N_DEV, D_IN // N_DEV, D // N_DEV, D // N_DEV
OFF_IN, OFF_BR, OFF_WO = 0, R_IN, R_IN + R_BR
FFN_ROWS = 3 * R_FF
MIX_ROWS = R_IN + R_BR + R_WO

FC = 256
TM = 512
VMEM_LIMIT = 56 * 1024 * 1024

ADAM_LR, ADAM_B1, ADAM_B2, ADAM_EPS, ADAM_WD, ADAM_STEP = 0.001, 0.9, 0.999, 1e-08, 0.01, 10

ROW_LOSS, ROW_N1, ROW_NM, ROW_N2, ROW_NF, ROW_BG, ROW_LN, ROW_BS, ROW_REL, ROW_WS = 0, 1, 2, 3, 4, 5, 7, 8, 9, 24
SMALL_ROWS = ROW_WS + 128

MESH = pl.DeviceIdType.MESH
ANY = pl.BlockSpec(memory_space=pl.ANY)


def _nt(a, b):
    return lax.dot_general(a, b, (((1,), (1,)), ((), ())), preferred_element_type=F32)


def _tn(a, b):
    return lax.dot_general(a, b, (((0,), (0,)), ((), ())), preferred_element_type=F32)


def _nn(a, b):
    return jnp.dot(a, b, preferred_element_type=F32)


def _cparams(sem=None):
    return pltpu.CompilerParams(dimension_semantics=sem, vmem_limit_bytes=VMEM_LIMIT)


def _load_rows(gw_ref, dst, off, rows, sems):
    copies = [pltpu.make_async_copy(gw_ref.at[k, pl.ds(off, rows), :], dst.at[pl.ds(k * rows, rows), :], sems.at[k])
              for k in range(N_DEV)]
    for cp in copies:
        cp.start()
    return copies


def _rms(xv):
    r = lax.rsqrt(jnp.mean(xv * xv, axis=-1, keepdims=True) + EPS)
    return xv * r, r


def _rms_bwd(dh, xn, r, gain):
    dxn = dh * gain
    dx = r * (dxn - xn * jnp.mean(dxn * xn, axis=-1, keepdims=True))
    return dx, jnp.sum(dh * xn, axis=0, keepdims=True)


def _gelu(x):
    t = jnp.tanh(0.7978845608028654 * (x + 0.044715 * x * x * x))
    return 0.5 * x * (1.0 + t), t


def _gelu_grad(x, t):
    return 0.5 * (1.0 + t) + 0.5 * x * (1.0 - t * t) * 0.7978845608028654 * (1.0 + 3.0 * 0.044715 * x * x)


def _place():
    x, y, cc = lax.axis_index("x"), lax.axis_index("y"), lax.axis_index("c")
    return x, y, cc, [(1 - x, y), (x, 1 - y), (1 - x, 1 - y)]


class _Gather:
    def __init__(self, shard):
        self.inputs = [shard]
        self.out_shape = [jax.ShapeDtypeStruct((N_DEV,) + shard.shape, shard.dtype)]
        self.scratch = [pltpu.SemaphoreType.DMA((7,)), pltpu.SemaphoreType.DMA((7,)), pltpu.SemaphoreType.DMA]

    def _copies(self, ins, outs, scr):
        (x_ref,), (out_ref,), (send_sems, recv_sems, local_sem) = ins, outs, scr
        x, y, cc, chips = _place()

        def slab(px, py, pc):
            return out_ref.at[4 * px + 2 * py + pc]

        def copy(k, block, to, src=None):
            return pltpu.make_async_remote_copy(
                src_ref=slab(*block) if src is None else src, dst_ref=slab(*block),
                send_sem=send_sems.at[k], recv_sem=recv_sems.at[k], device_id=to, device_id_type=MESH)

        me, sibling = (x, y, cc), (x, y, 1 - cc)
        mine = pltpu.make_async_copy(x_ref, slab(*me), local_sem)
        first = [copy(0, me, sibling, src=x_ref)] + [copy(1 + j, me, (*chip, cc), src=x_ref) for j, chip in enumerate(chips)]
        landed = [copy(1 + j, (*chip, cc), me) for j, chip in enumerate(chips)]
        passed = [copy(4 + j, (*chip, cc), sibling) for j, chip in enumerate(chips)]
        from_sibling = [copy(0, sibling, me)] + [copy(4 + j, (*chip, 1 - cc), me) for j, chip in enumerate(chips)]
        return mine, first, landed, passed, from_sibling

    def begin(self, *refs):
        mine, first, _, _, _ = self._copies(*refs)
        mine.start()
        for cp in first:
            cp.start()

    def relay(self, *refs):
        _, _, landed, passed, _ = self._copies(*refs)
        for arrived, onward in zip(landed, passed):
            arrived.wait_recv()
            onward.start()

    def end(self, *refs):
        mine, first, _, passed, from_sibling = self._copies(*refs)
        for cp in from_sibling:
            cp.wait_recv()
        for cp in first + passed:
            cp.wait_send()
        mine.wait()


class _Direct:
    def begin(self, *refs):
        keep, give = self._copies(*refs)
        for cp in keep + give:
            cp.start()

    def relay(self, *refs):
        pass

    def end(self, *refs):
        keep, give = self._copies(*refs)
        for cp in give:
            cp.wait_recv()
        for cp in give:
            cp.wait_send()
        for cp in keep:
            cp.wait()


class _SiblingSwap(_Direct):
    def __init__(self, grads):
        n = len(grads)
        self.inputs = list(grads)
        self.out_shape = [jax.ShapeDtypeStruct((4,) + g.shape[1:], g.dtype) for g in grads]
        self.scratch = [pltpu.SemaphoreType.DMA((n, 4)), pltpu.SemaphoreType.DMA((n, 4))]

    def _copies(self, ins, outs, scr):
        send_sems, recv_sems = scr
        x, y, cc, _ = _place()
        return [], [pltpu.make_async_remote_copy(src_ref=g_ref.at[2 * j + 1 - cc], dst_ref=got_ref.at[j], send_sem=send_sems.at[i, j],
                                                 recv_sem=recv_sems.at[i, j], device_id=(x, y, 1 - cc), device_id_type=MESH)
                    for i, (g_ref, got_ref) in enumerate(zip(ins, outs)) for j in range(4)]


class _ChipScatter(_Direct):
    def __init__(self, sums):
        n = len(sums)
        self.inputs = list(sums)
        self.out_shape = [jax.ShapeDtypeStruct((3,) + s.shape[1:], s.dtype) for s in sums]
        self.scratch = [pltpu.SemaphoreType.DMA((n, 3)), pltpu.SemaphoreType.DMA((n, 3))]

    def _copies(self, ins, outs, scr):
        send_sems, recv_sems = scr
        _, _, cc, chips = _place()
        return [], [pltpu.make_async_remote_copy(src_ref=s_ref.at[2 * px + py], dst_ref=got_ref.at[j], send_sem=send_sems.at[i, j],
                                                 recv_sem=recv_sems.at[i, j], device_id=(px, py, cc), device_id_type=MESH)
                    for i, (s_ref, got_ref) in enumerate(zip(ins, outs)) for j, (px, py) in enumerate(chips)]


class _AllToAll(_Direct):
    def __init__(self, block):
        self.inputs = [block]
        self.out_shape = [jax.ShapeDtypeStruct((N_DEV,) + block.shape, block.dtype)]
        self.scratch = [pltpu.SemaphoreType.DMA((7,)), pltpu.SemaphoreType.DMA((7,)), pltpu.SemaphoreType.DMA]

    def _copies(self, ins, outs, scr):
        (b_ref,), (got_ref,), (send_sems, recv_sems, local_sem) = ins, outs, scr
        x, y, cc, _ = _place()
        me = 4 * x + 2 * y + cc
        keep = [pltpu.make_async_copy(b_ref, got_ref.at[me], local_sem)]
        give = [pltpu.make_async_remote_copy(src_ref=b_ref, dst_ref=got_ref.at[me], send_sem=send_sems.at[k - 1], recv_sem=recv_sems.at[k - 1],
                                             device_id=(x ^ ((k >> 2) & 1), y ^ ((k >> 1) & 1), cc ^ (k & 1)), device_id_type=MESH)
                for k in range(1, N_DEV)]
        return keep, give


def _split_refs(refs, counts):
    out, pos = [], 0
    for n in counts:
        out.append(list(refs[pos:pos + n]))
        pos += n
    return out


def _bind(comms, c_in, c_out, c_scr):
    ins = _split_refs(c_in, [len(c.inputs) for c in comms])
    outs = _split_refs(c_out, [len(c.out_shape) for c in comms])
    scr = _split_refs(c_scr, [len(c.scratch) for c in comms])
    return [(c, (i, o, s)) for c, i, o, s in zip(comms, ins, outs, scr)]


def _call(body, *, name, grid, in_specs, out_specs, out_shape, scratch_shapes, args, comms=()):
    c_in = [a for c in comms for a in c.inputs]
    c_out = [s for c in comms for s in c.out_shape]
    c_scr = [s for c in comms for s in c.scratch]
    counts = [len(in_specs), len(c_in), len(out_shape), len(c_out), len(scratch_shapes), len(c_scr)]
    last = grid[0] - 1

    def full(*refs):
        ins, cin, outs, cout, scr, cscr = _split_refs(refs, counts)
        bound = _bind(comms, cin, cout, cscr)
        step = pl.program_id(0)
        if comms:
            @pl.when(step == 0)
            def _():
                for c, r in bound:
                    c.begin(*r)

            @pl.when(step == last)
            def _():
                for c, r in bound:
                    c.relay(*r)

        body(*ins, *outs, *scr)
        if comms:
            @pl.when(step == last)
            def _():
                for c, r in bound:
                    c.end(*r)

    res = pl.pallas_call(
        full, name=name, grid=grid,
        in_specs=list(in_specs) + [ANY] * len(c_in), out_specs=list(out_specs) + [ANY] * len(c_out),
        out_shape=list(out_shape) + c_out, scratch_shapes=list(scratch_shapes) + c_scr,
        compiler_params=_cparams(("arbitrary",)),
    )(*args, *c_in)
    return list(res[:len(out_shape)]), list(res[len(out_shape):])


def _comm_only(comms, name):
    c_in = [a for c in comms for a in c.inputs]
    c_out = [s for c in comms for s in c.out_shape]
    c_scr = [s for c in comms for s in c.scratch]

    def full(*refs):
        cin, cout, cscr = _split_refs(refs, [len(c_in), len(c_out), len(c_scr)])
        bound = _bind(comms, cin, cout, cscr)
        for phase in ("begin", "relay", "end"):
            for c, r in bound:
                getattr(c, phase)(*r)

    return list(pl.pallas_call(full, name=name, in_specs=[ANY] * len(c_in), out_specs=[ANY] * len(c_out), out_shape=c_out,
                               scratch_shapes=c_scr)(*c_in))


def _my_index(*axes_and_weights):
    return sum(w * lax.axis_index(a) for a, w in axes_and_weights).astype(jnp.int32).reshape(1)


def _pair_sums(grads, gots, name):
    n = len(grads)

    def body(c_ref, *refs):
        for a_ref, b_ref, o_ref in zip(refs[:n], refs[n:2 * n], refs[2 * n:]):
            o_ref[...] = (a_ref[...].astype(F32) + b_ref[...].astype(F32)).astype(BF16)

    def tile(g):
        return pl.BlockSpec((1,) + g.shape[1:], lambda j, c_ref: (j, 0, 0))

    def mine(g):
        return pl.BlockSpec((1, None) + g.shape[1:], lambda j, c_ref: (j, c_ref[0], 0, 0))

    return list(pl.pallas_call(
        body, name=name,
        grid_spec=pltpu.PrefetchScalarGridSpec(num_scalar_prefetch=1, grid=(4,), in_specs=[mine(g) for g in gots] + [tile(g) for g in gots],
                                               out_specs=[tile(g) for g in gots]),
        out_shape=[jax.ShapeDtypeStruct(g.shape, BF16) for g in gots],
        compiler_params=_cparams(("arbitrary",)))(_my_index(("c", 1)), *[g.reshape((4, 2) + g.shape[1:]) for g in grads], *gots))


def _ffn_fwd(x, gain, gw_gu, gw_d, off_d, name, comms=()):
    t = x.shape[0]

    def body(x_ref, g_ref, gu_ref, d_ref, o_ref, ab_ref, wg, wu, wd, s_scr, sems):
        @pl.when(pl.program_id(0) == 0)
        def _():
            cps = _load_rows(gu_ref, wg, 0, R_FF, sems.at[0]) + _load_rows(gu_ref, wu, R_FF, R_FF, sems.at[1]) \
                + _load_rows(d_ref, wd, off_d, R_FF, sems.at[2])
            for cp in cps:
                cp.wait()

        xv = x_ref[...]
        xn, _ = _rms(xv)
        h = (xn * g_ref[...]).astype(BF16)
        for c in range(F // FC):
            rows = pl.ds(c * FC, FC)
            a = _nt(h, wg[rows, :])
            b = _nt(h, wu[rows, :])
            ab_ref[:, c * FC:(c + 1) * FC] = a.astype(BF16)
            ab_ref[:, F + c * FC:F + (c + 1) * FC] = b.astype(BF16)
            s_scr[:, c * FC:(c + 1) * FC] = (a * jax.nn.sigmoid(a) * b).astype(BF16)
        o_ref[...] = xv + 0.5 * _nn(s_scr[...], wd[...])

    tile = pl.BlockSpec((TM, D), lambda i: (i, 0))
    (out, ab), got = _call(
        body, name=name, grid=(t // TM,),
        in_specs=[tile, pl.BlockSpec((1, D), lambda i: (0, 0)), ANY, ANY], out_specs=[tile, pl.BlockSpec((TM, 2 * F), lambda i: (i, 0))],
        out_shape=[jax.ShapeDtypeStruct((t, D), F32), jax.ShapeDtypeStruct((t, 2 * F), BF16)],
        scratch_shapes=[pltpu.VMEM((F, D), BF16)] * 3 + [pltpu.VMEM((TM, F), BF16), pltpu.SemaphoreType.DMA((3, N_DEV))],
        args=(x, gain, gw_gu, gw_d), comms=comms)
    return out, ab, got


def _ffn_bwd(x, gain, ab, dout, gw_gu, gw_d, off_d, name, comms=()):
    t = x.shape[0]

    def tile(w):
        return pl.BlockSpec((TM, w), lambda i: (i, 0))

    def hidden(ab_ref, do_ref, d_ref, dab_ref, s_ref, df_ref, wd, sems):
        @pl.when(pl.program_id(0) == 0)
        def _():
            for cp in _load_rows(d_ref, wd, off_d, R_FF, sems):
                cp.wait()

        df = (0.5 * do_ref[...]).astype(BF16)
        df_ref[...] = df
        for c in range(F // FC):
            a = ab_ref[:, c * FC:(c + 1) * FC].astype(F32)
            b = ab_ref[:, F + c * FC:F + (c + 1) * FC].astype(F32)
            sg = jax.nn.sigmoid(a)
            sl = a * sg
            ds = _nt(df, wd[pl.ds(c * FC, FC), :])
            dab_ref[:, c * FC:(c + 1) * FC] = (ds * b * (sg * (1.0 + a * (1.0 - sg)))).astype(BF16)
            dab_ref[:, F + c * FC:F + (c + 1) * FC] = (ds * sl).astype(BF16)
            s_ref[:, c * FC:(c + 1) * FC] = (sl * b).astype(BF16)

    (dab, s, df), _ = _call(
        hidden, name=name + "_hidden", grid=(t // TM,),
        in_specs=[tile(2 * F), tile(D), ANY], out_specs=[tile(2 * F), tile(F), tile(D)],
        out_shape=[jax.ShapeDtypeStruct((t, 2 * F), BF16), jax.ShapeDtypeStruct((t, F), BF16), jax.ShapeDtypeStruct((t, D), BF16)],
        scratch_shapes=[pltpu.VMEM((F, D), BF16), pltpu.SemaphoreType.DMA((N_DEV,))],
        args=(ab, dout, gw_d))

    def body(x_ref, g_ref, dab_ref, do_ref, gu_ref, dx_ref, dg_ref, h_ref, wg, wu, sems):
        @pl.when(pl.program_id(0) == 0)
        def _():
            cps = _load_rows(gu_ref, wg, 0, R_FF, sems.at[0]) + _load_rows(gu_ref, wu, R_FF, R_FF, sems.at[1])
            dg_ref[...] = jnp.zeros_like(dg_ref)
            for cp in cps:
                cp.wait()

        gain_v = g_ref[...]
        xn, r = _rms(x_ref[...])
        h_ref[...] = (xn * gain_v).astype(BF16)
        dh = _nn(dab_ref[:, :F], wg[...]) + _nn(dab_ref[:, F:], wu[...])
        dxn, dg = _rms_bwd(dh, xn, r, gain_v)
        dg_ref[...] += dg
        dx_ref[...] = do_ref[...] + dxn

    row = pl.BlockSpec((1, D), lambda i: (0, 0))
    (dx, dg, h), got = _call(
        body, name=name, grid=(t // TM,),
        in_specs=[tile(D), row, tile(2 * F), tile(D), ANY], out_specs=[tile(D), row, tile(D)],
        out_shape=[jax.ShapeDtypeStruct((t, D), F32), jax.ShapeDtypeStruct((1, D), F32), jax.ShapeDtypeStruct((t, D), BF16)],
        scratch_shapes=[pltpu.VMEM((F, D), BF16)] * 2 + [pltpu.SemaphoreType.DMA((2, N_DEV))],
        args=(x, gain, dab, dout, gw_gu), comms=comms)
    return (dx, dg, h, dab, s, df), got


def _weight_grad(a, b, name, col_off=0, m=None, comms=None, tmm=256):
    t = a.shape[0]
    m = a.shape[1] if m is None else m
    n = b.shape[1]
    first = col_off // tmm

    def body(a_ref, b_ref, o_ref):
        o_ref[...] = _tn(a_ref[...], b_ref[...]).astype(BF16)

    (out,), got = _call(
        body, name=name, grid=(m // tmm,),
        in_specs=[pl.BlockSpec((t, tmm), lambda i: (0, first + i)), pl.BlockSpec((t, n), lambda i: (0, 0))],
        out_specs=[pl.BlockSpec((tmm, n), lambda i: (i, 0))],
        out_shape=[jax.ShapeDtypeStruct((m, n), BF16)], scratch_shapes=[], args=(a, b), comms=comms or ())
    out = out.reshape(N_DEV, m // N_DEV, n)
    return out if comms is None else (out, got)


def _mix_proj_fwd(x, gain, b_gate, gw, comms=()):
    t = x.shape[0]

    def body(x_ref, g_ref, bg_ref, gw_ref, q_ref, k_ref, v_ref, zs_ref, gt_ref, h_ref, win, sems):
        @pl.when(pl.program_id(0) == 0)
        def _():
            for cp in _load_rows(gw_ref, win, OFF_IN, R_IN, sems):
                cp.wait()

        xn, _ = _rms(x_ref[...])
        h = (xn * g_ref[...]).astype(BF16)
        h_ref[...] = h
        q_ref[...] = (_nt(h, win[0:512, :]) * 0.125).astype(BF16)
        k_ref[...] = _nt(h, win[512:1024, :]).astype(BF16)
        v_ref[...] = _nt(h, win[1024:1536, :]).astype(BF16)
        for c in range(2):
            zs_ref[:, c * 512:(c + 1) * 512] = _nt(h, win[1536 + c * 512:2048 + c * 512, :]).astype(BF16)
        for c in range(4):
            zg = _nt(h, win[2560 + c * 512:3072 + c * 512, :]) + bg_ref[:, c * 512:(c + 1) * 512]
            gt_ref[:, c * 512:(c + 1) * 512] = jax.nn.sigmoid(zg).astype(BF16)

    def tile(w):
        return pl.BlockSpec((TM, w), lambda i: (i, 0))

    return _call(
        body, name="mix_proj_fwd", grid=(t // TM,),
        in_specs=[tile(D), pl.BlockSpec((1, D), lambda i: (0, 0)), pl.BlockSpec((1, 2 * D), lambda i: (0, 0)), ANY],
        out_specs=[tile(D_ATT), tile(D_ATT), tile(D_ATT), tile(2 * D_SGU), tile(2 * D), tile(D)],
        out_shape=[jax.ShapeDtypeStruct((t, D_ATT), BF16)] * 3 + [jax.ShapeDtypeStruct((t, 2 * D_SGU), BF16),
                                                                   jax.ShapeDtypeStruct((t, 2 * D), BF16),
                                                                   jax.ShapeDtypeStruct((t, D), BF16)],
        scratch_shapes=[pltpu.VMEM((D_IN, D), BF16), pltpu.SemaphoreType.DMA((N_DEV,))],
        args=(x, gain, b_gate, gw), comms=comms)


def _mix_proj_bwd(dz, x, gain, dres, gw, comms=()):
    t = x.shape[0]

    def body(dz_ref, x_ref, g_ref, dr_ref, gw_ref, dx_ref, dg_ref, win, sems):
        @pl.when(pl.program_id(0) == 0)
        def _():
            cps = _load_rows(gw_ref, win, OFF_IN, R_IN, sems)
            dg_ref[...] = jnp.zeros_like(dg_ref)
            for cp in cps:
                cp.wait()

        dh = _nn(dz_ref[...], win[...])
        xn, r = _rms(x_ref[...])
        dxn, dg = _rms_bwd(dh, xn, r, g_ref[...])
        dg_ref[...] += dg
        dx_ref[...] = dr_ref[...] + dxn

    def tile(w):
        return pl.BlockSpec((TM, w), lambda i: (i, 0))

    row = pl.BlockSpec((1, D), lambda i: (0, 0))
    return _call(
        body, name="mix_proj_bwd", grid=(t // TM,),
        in_specs=[tile(D_IN), tile(D), row, tile(D), ANY], out_specs=[tile(D), row],
        out_shape=[jax.ShapeDtypeStruct((t, D), F32), jax.ShapeDtypeStruct((1, D), F32)],
        scratch_shapes=[pltpu.VMEM((D_IN, D), BF16), pltpu.SemaphoreType.DMA((N_DEV,))],
        args=(dz, x, gain, dres, gw), comms=comms)


SKEW_W = KW + QB
N_CAP = 2 * QB - REL_CLIP + 1


def _band_bias(rel_bias):
    cap = rel_bias[:, 2 * REL_CLIP:]
    diag = jnp.concatenate([jnp.broadcast_to(cap, (HEADS, N_CAP)), rel_bias[:, 2 * REL_CLIP - 1::-1],
                            jnp.broadcast_to(cap, (HEADS, SKEW_W - N_CAP - 2 * REL_CLIP))], axis=1)

    def body(d_ref, o_ref):
        lag = lax.broadcasted_iota(jnp.int32, (QB, KW), 1) // CHUNK - lax.broadcasted_iota(jnp.int32, (QB, KW), 0) // CHUNK
        band = (lag >= 0) & (lag <= N_LEFT)
        for h in range(HEADS):
            rows = jnp.broadcast_to(d_ref[h:h + 1, :], (QB, SKEW_W))
            o_ref[h] = jnp.where(band, pltpu.roll(rows, 0, 1, stride=1, stride_axis=0)[:, :KW], NEG_INF)

    return pl.pallas_call(body, name="band_bias", out_shape=jax.ShapeDtypeStruct((HEADS, QB, KW), F32))(diag)


def _att_specs():
    qspec = pl.BlockSpec((QB, D_ATT), lambda g: (g, 0))
    kspecs = [pl.BlockSpec((QB, D_ATT), lambda g: (jnp.maximum(g - 2, 0), 0)),
              pl.BlockSpec((QB, D_ATT), lambda g: (jnp.maximum(g - 1, 0), 0)), qspec]
    bspec = pl.BlockSpec((HEADS, QB, KW), lambda g: (0, 0, 0))
    return qspec, kspecs, bspec


def _att_probs(qm, kp, bias, valid):
    s = jnp.where(valid, _nt(qm, kp) + bias, NEG_INF)
    e = jnp.exp(s - jnp.max(s, axis=-1, keepdims=True))
    return e / jnp.sum(e, axis=-1, keepdims=True)


def _att_valid():
    g = pl.program_id(0)
    blk = lax.broadcasted_iota(jnp.int32, (QB, KW), 1) // QB
    return (blk + g) >= 2


def _att_fwd(q, k, v, bias, comms=()):
    t = q.shape[0]

    def body(q_ref, k0, k1, k2, v0, v1, v2, b_ref, y_ref):
        valid = _att_valid()
        first = lax.broadcasted_iota(jnp.int32, (1, 128), 1) < 64
        for p in range(HEADS // 2):
            lanes = slice(p * 128, (p + 1) * 128)
            qp = q_ref[:, lanes]
            kp = jnp.concatenate([k0[:, lanes], k1[:, lanes], k2[:, lanes]], axis=0)
            vp = jnp.concatenate([v0[:, lanes], v1[:, lanes], v2[:, lanes]], axis=0)
            out = jnp.zeros((QB, 128), F32)
            for hh in range(2):
                mask = first if hh == 0 else jnp.logical_not(first)
                pr = _att_probs(jnp.where(mask, qp, 0), kp, b_ref[2 * p + hh], valid)
                out = out + _nn(pr.astype(BF16), jnp.where(mask, vp, 0))
            y_ref[:, lanes] = out.astype(BF16)

    qspec, kspecs, bspec = _att_specs()
    (out,), got = _call(
        body, name="att_fwd", grid=(t // QB,),
        in_specs=[qspec] + kspecs + kspecs + [bspec], out_specs=[qspec],
        out_shape=[jax.ShapeDtypeStruct((t, D_ATT), BF16)], scratch_shapes=[],
        args=(q, k, k, k, v, v, v, bias), comms=comms)
    return out, got


def _att_bwd(q, k, v, bias, dy, comms=()):
    t = q.shape[0]
    n_blocks = t // QB

    def body(q_ref, k0, k1, k2, v0, v1, v2, b_ref, dy_ref, dq_ref, dk_ref, dv_ref, db_ref, dk_acc, dv_acc):
        g = pl.program_id(0)

        @pl.when(g == 0)
        def _():
            db_ref[...] = jnp.zeros_like(db_ref)
            dk_acc[...] = jnp.zeros_like(dk_acc)
            dv_acc[...] = jnp.zeros_like(dv_acc)

        valid = _att_valid()
        first = lax.broadcasted_iota(jnp.int32, (1, 128), 1) < 64
        for p in range(HEADS // 2):
            lanes = slice(p * 128, (p + 1) * 128)
            qp = q_ref[:, lanes]
            dyp = dy_ref[:, lanes]
            kp = jnp.concatenate([k0[:, lanes], k1[:, lanes], k2[:, lanes]], axis=0)
            vp = jnp.concatenate([v0[:, lanes], v1[:, lanes], v2[:, lanes]], axis=0)
            dq = jnp.zeros((QB, 128), F32)
            dk = jnp.zeros((KW, 128), F32)
            dv = jnp.zeros((KW, 128), F32)
            for hh in range(2):
                mask = first if hh == 0 else jnp.logical_not(first)
                qm = jnp.where(mask, qp, 0)
                dym = jnp.where(mask, dyp, 0)
                pr = _att_probs(qm, kp, b_ref[2 * p + hh], valid)
                dp = _nt(dym, vp)
                ds = pr * (dp - jnp.sum(dp * pr, axis=-1, keepdims=True))
                db_ref[2 * p + hh] += ds
                dsb = ds.astype(BF16)
                dq = dq + _nn(dsb, jnp.where(mask, kp, 0))
                dk = dk + _tn(dsb, qm)
                dv = dv + _tn(pr.astype(BF16), dym)
            dq_ref[:, lanes] = (dq * 0.125).astype(BF16)
            for j in range(3):
                rows = pl.ds(pl.multiple_of(jnp.maximum(g - 2 + j, 0) * QB, QB), QB)
                dk_acc[rows, lanes] += dk[j * QB:(j + 1) * QB]
                dv_acc[rows, lanes] += dv[j * QB:(j + 1) * QB]

        @pl.when(g == n_blocks - 1)
        def _():
            dk_ref[...] = dk_acc[...].astype(BF16)
            dv_ref[...] = dv_acc[...].astype(BF16)

    qspec, kspecs, bspec = _att_specs()
    full = pl.BlockSpec((t, D_ATT), lambda g: (0, 0))
    return _call(
        body, name="att_bwd", grid=(n_blocks,),
        in_specs=[qspec] + kspecs + kspecs + [bspec, qspec], out_specs=[qspec, full, full, bspec],
        out_shape=[jax.ShapeDtypeStruct((t, D_ATT), BF16)] * 3 + [jax.ShapeDtypeStruct((HEADS, QB, KW), F32)],
        scratch_shapes=[pltpu.VMEM((t, D_ATT), F32)] * 2,
        args=(q, k, k, k, v, v, v, bias, dy), comms=comms)


def _rel_bias_grad(dbias):
    def body(db_ref, cs_ref, tot_ref):
        lane = lax.broadcasted_iota(jnp.int32, (1, SKEW_W), 1)
        capped = (lane < N_CAP) | (lane > KW)
        pad = jnp.zeros((8, QB), F32)
        for h in range(HEADS):
            def add_rows(a, z):
                rows = jnp.concatenate([db_ref[h, pl.ds(pl.multiple_of(8 * a, 8), 8), :], pad], axis=1)
                return z + pltpu.roll(rows, (SKEW_W - 8 * a) % SKEW_W, 1)

            z = lax.fori_loop(0, QB // 8, add_rows, jnp.zeros((8, SKEW_W), F32))
            cs = z[0:1, :]
            for b in range(1, 8):
                cs = cs + pltpu.roll(z[b:b + 1, :], SKEW_W - b, 1)
            cs_ref[h:h + 1, :] = cs
            tot_ref[h:h + 1, :] = jnp.broadcast_to(jnp.sum(jnp.where(capped, cs, 0.0), axis=1, keepdims=True), (1, 128))

    cs, tot = pl.pallas_call(
        body, name="rel_bias_grad",
        out_shape=[jax.ShapeDtypeStruct((HEADS, SKEW_W), F32), jax.ShapeDtypeStruct((HEADS, 128), F32)],
    )(dbias)
    return jnp.concatenate([cs[:, KW:N_CAP - 1:-1], tot[:, :1]], axis=1)


def _sgu_mask():
    pos = np.arange(SGU_BLOCK)
    return (pos[:, None] // CHUNK) >= (pos[None, :] // CHUNK)


def _group_stack(blk, first):
    return jnp.concatenate([jnp.where(first, blk, 0), jnp.where(first, 0, blk)], axis=0)


def _sgu_norm(zs_ref, lng, lnb):
    zs = zs_ref[...].astype(F32)
    ga, th = _gelu(zs)
    u, vs = ga[:, :D_SGU], ga[:, D_SGU:]
    mu = jnp.mean(vs, axis=-1, keepdims=True)
    cen = vs - mu
    rstd = lax.rsqrt(jnp.mean(cen * cen, axis=-1, keepdims=True) + EPS)
    xhat = cen * rstd
    return zs, th, u, xhat, rstd, xhat * lng + lnb


def _sgu_mix(vb, wm2_ref, bsx, s_ref):
    first = lax.broadcasted_iota(jnp.int32, (1, 128), 1) < 64
    for n in range(TM // SGU_BLOCK):
        for p in range(4):
            blk = vb[n * 128:(n + 1) * 128, p * 128:(p + 1) * 128]
            s_ref[n * 128:(n + 1) * 128, p * 128:(p + 1) * 128] = _nn(wm2_ref[p], _group_stack(blk, first)) + bsx[:, p * 128:(p + 1) * 128]


def _merge_fwd(x, zs, gt, y_att, lng, lnb, wm2, bsx, gw):
    t = x.shape[0]

    def body(x_ref, zs_ref, gt_ref, ya_ref, lng_ref, lnb_ref, wm2_ref, bsx_ref, gw_ref, xo_ref, ys_ref, mg_ref,
             wbr, wo, s_scr, sems):
        @pl.when(pl.program_id(0) == 0)
        def _():
            for cp in _load_rows(gw_ref, wbr, OFF_BR, R_BR, sems.at[0]) + _load_rows(gw_ref, wo, OFF_WO, R_WO, sems.at[1]):
                cp.wait()

        _, _, u, _, _, vsn = _sgu_norm(zs_ref, lng_ref[...], lnb_ref[...])
        _sgu_mix(vsn.astype(BF16), wm2_ref, bsx_ref[...], s_scr)
        ys = (u * s_scr[...]).astype(BF16)
        ys_ref[...] = ys
        pa = _nt(ya_ref[...], wbr[:, :D_ATT])
        ps = _nt(ys, wbr[:, D_ATT:])
        mg = (gt_ref[:, :D].astype(F32) * pa + gt_ref[:, D:].astype(F32) * ps).astype(BF16)
        mg_ref[...] = mg
        xo_ref[...] = x_ref[...] + _nn(mg, wo[...])

    def tile(w):
        return pl.BlockSpec((TM, w), lambda i: (i, 0))

    def const(shape):
        return pl.BlockSpec(shape, lambda i: (0,) * len(shape))

    return pl.pallas_call(
        body, name="merge_fwd", grid=(t // TM,),
        in_specs=[tile(D), tile(2 * D_SGU), tile(2 * D), tile(D_ATT), const((1, D_SGU)), const((1, D_SGU)),
                  const((4, 128, 256)), const((128, D_SGU)), ANY],
        out_specs=[tile(D), tile(D_SGU), tile(D)],
        out_shape=[jax.ShapeDtypeStruct((t, D), F32), jax.ShapeDtypeStruct((t, D_SGU), BF16), jax.ShapeDtypeStruct((t, D), BF16)],
        scratch_shapes=[pltpu.VMEM((D, D), BF16), pltpu.VMEM((D, D), BF16), pltpu.VMEM((TM, D_SGU), F32),
                        pltpu.SemaphoreType.DMA((2, N_DEV))],
        compiler_params=_cparams(("arbitrary",)),
    )(x, zs, gt, y_att, lng, lnb, wm2, bsx, gw)


def _merge_bwd(dx, gt, y_att, y_sgu, gw):
    t = dx.shape[0]

    def body(dx_ref, gt_ref, ya_ref, ys_ref, gw_ref, dzg_ref, dya_ref, dys_ref, dpp_ref, dxb_ref, dbg_ref, wbr, wo, sems):
        @pl.when(pl.program_id(0) == 0)
        def _():
            cps = _load_rows(gw_ref, wbr, OFF_BR, R_BR, sems.at[0]) + _load_rows(gw_ref, wo, OFF_WO, R_WO, sems.at[1])
            dbg_ref[...] = jnp.zeros_like(dbg_ref)
            for cp in cps:
                cp.wait()

        dxb = dx_ref[...].astype(BF16)
        dxb_ref[...] = dxb
        dm = _nt(dxb, wo[...])
        for half, y_ref, w in ((0, ya_ref, wbr.at[:, :D_ATT]), (1, ys_ref, wbr.at[:, D_ATT:])):
            cols = slice(half * D, (half + 1) * D)
            gate = gt_ref[:, cols].astype(F32)
            branch = _nt(y_ref[...], w[...])
            dzg = dm * branch * gate * (1.0 - gate)
            dbg_ref[:, cols] += jnp.sum(dzg, axis=0, keepdims=True)
            dzg_ref[:, cols] = dzg.astype(BF16)
            dbr = (dm * gate).astype(BF16)
            dpp_ref[:, cols] = dbr
            dy = _nn(dbr, w[...])
            if half == 0:
                dya_ref[...] = dy.astype(BF16)
            else:
                dys_ref[...] = dy

    def tile(w):
        return pl.BlockSpec((TM, w), lambda i: (i, 0))

    return pl.pallas_call(
        body, name="merge_bwd", grid=(t // TM,),
        in_specs=[tile(D), tile(2 * D), tile(D_ATT), tile(D_SGU), ANY],
        out_specs=[tile(2 * D), tile(D_ATT), tile(D_SGU), tile(2 * D), tile(D), pl.BlockSpec((1, 2 * D), lambda i: (0, 0))],
        out_shape=[jax.ShapeDtypeStruct((t, 2 * D), BF16), jax.ShapeDtypeStruct((t, D_ATT), BF16), jax.ShapeDtypeStruct((t, D_SGU), F32),
                   jax.ShapeDtypeStruct((t, 2 * D), BF16), jax.ShapeDtypeStruct((t, D), BF16), jax.ShapeDtypeStruct((1, 2 * D), F32)],
        scratch_shapes=[pltpu.VMEM((D, D), BF16), pltpu.VMEM((D, D), BF16), pltpu.SemaphoreType.DMA((2, N_DEV))],
        compiler_params=_cparams(("arbitrary",)),
    )(dx, gt, y_att, y_sgu, gw)


def _sgu_bwd(zs, dys, lng, lnb, wm2, wmt2, bsx, comms=()):
    t = zs.shape[0]
    n_steps = t // TM

    def body(zs_ref, dys_ref, lng_ref, lnb_ref, wm2_ref, wmt2_ref, bsx_ref, dzs_ref, dw_ref, dbs_ref, dlg_ref, dlb_ref,
             s_scr, dv_scr, ds_acc):
        i = pl.program_id(0)

        @pl.when(i == 0)
        def _():
            dw_ref[...] = jnp.zeros_like(dw_ref)
            dlg_ref[...] = jnp.zeros_like(dlg_ref)
            dlb_ref[...] = jnp.zeros_like(dlb_ref)
            ds_acc[...] = jnp.zeros_like(ds_acc)

        lng = lng_ref[...]
        zs, th, u, xhat, rstd, vsn = _sgu_norm(zs_ref, lng, lnb_ref[...])
        vb = vsn.astype(BF16)
        _sgu_mix(vb, wm2_ref, bsx_ref[...], s_scr)
        dys = dys_ref[...]
        du = dys * s_scr[...]
        ds = dys * u
        dsb = ds.astype(BF16)
        first = lax.broadcasted_iota(jnp.int32, (1, 128), 1) < 64
        acc = jnp.zeros((SGU_BLOCK, D_SGU), F32)
        for n in range(TM // SGU_BLOCK):
            rows = slice(n * 128, (n + 1) * 128)
            acc = acc + ds[rows]
            for p in range(4):
                lanes = slice(p * 128, (p + 1) * 128)
                stack = _group_stack(dsb[rows, lanes], first)
                dv_scr[rows, lanes] = _nn(wmt2_ref[p], stack)
                dw_ref[p] += _nt(stack, vb[rows, lanes])
        ds_acc[...] += acc
        dvsn = dv_scr[...]
        dlg_ref[...] += jnp.sum(dvsn * xhat, axis=0, keepdims=True)
        dlb_ref[...] += jnp.sum(dvsn, axis=0, keepdims=True)
        dxh = dvsn * lng
        dvs = rstd * (dxh - jnp.mean(dxh, axis=-1, keepdims=True) - xhat * jnp.mean(dxh * xhat, axis=-1, keepdims=True))
        dga = jnp.concatenate([du, dvs], axis=1)
        dzs_ref[...] = (dga * _gelu_grad(zs, th)).astype(BF16)

        @pl.when(i == n_steps - 1)
        def _():
            r = lax.broadcasted_iota(jnp.int32, (256, 128), 0) % SGU_BLOCK
            c = lax.broadcasted_iota(jnp.int32, (256, 128), 1)
            keep = (r // CHUNK) >= (c // CHUNK)
            for p in range(4):
                dw_ref[p] = jnp.where(keep, dw_ref[p], 0.0)
            total = ds_acc[...]
            grp = lax.broadcasted_iota(jnp.int32, (SGU_BLOCK, D_SGU), 1) // 64
            lane = lax.broadcasted_iota(jnp.int32, (SGU_BLOCK, 128), 1)
            out = jnp.zeros((SGU_BLOCK, 128), F32)
            for gi in range(8):
                out = jnp.where(lane == gi, jnp.sum(jnp.where(grp == gi, total, 0.0), axis=1, keepdims=True), out)
            dbs_ref[...] = out

    def tile(w):
        return pl.BlockSpec((TM, w), lambda i: (i, 0))

    def const(shape):
        return pl.BlockSpec(shape, lambda i: (0,) * len(shape))

    return _call(
        body, name="sgu_bwd", grid=(n_steps,),
        in_specs=[tile(2 * D_SGU), tile(D_SGU), const((1, D_SGU)), const((1, D_SGU)), const((4, 128, 256)), const((4, 128, 256)),
                  const((128, D_SGU))],
        out_specs=[tile(2 * D_SGU), const((4, 256, 128)), const((128, 128)), const((1, D_SGU)), const((1, D_SGU))],
        out_shape=[jax.ShapeDtypeStruct((t, 2 * D_SGU), BF16), jax.ShapeDtypeStruct((4, 256, 128), F32),
                   jax.ShapeDtypeStruct((128, 128), F32), jax.ShapeDtypeStruct((1, D_SGU), F32), jax.ShapeDtypeStruct((1, D_SGU), F32)],
        scratch_shapes=[pltpu.VMEM((TM, D_SGU), F32), pltpu.VMEM((TM, D_SGU), F32), pltpu.VMEM((SGU_BLOCK, D_SGU), F32)],
        args=(zs, dys, lng, lnb, wm2, wmt2, bsx), comms=comms)


def _loss_head(x, gain, target):
    t = x.shape[0]

    def body(x_ref, g_ref, t_ref, dx_ref, dg_ref, loss_ref):
        @pl.when(pl.program_id(0) == 0)
        def _():
            dg_ref[...] = jnp.zeros_like(dg_ref)
            loss_ref[...] = jnp.zeros_like(loss_ref)

        gain_v = g_ref[...]
        xn, r = _rms(x_ref[...])
        err = xn * gain_v - t_ref[...]
        loss_ref[...] += 0.5 * jnp.sum(jnp.mean(err * err, axis=-1, keepdims=True), axis=0, keepdims=True)
        dxn, dg = _rms_bwd(err * (1.0 / D), xn, r, gain_v)
        dg_ref[...] += dg
        dx_ref[...] = dxn

    tile = pl.BlockSpec((TM, D), lambda i: (i, 0))
    row = pl.BlockSpec((1, D), lambda i: (0, 0))
    return pl.pallas_call(
        body, name="loss_head", grid=(t // TM,),
        in_specs=[tile, row, tile], out_specs=[tile, row, pl.BlockSpec((1, 128), lambda i: (0, 0))],
        out_shape=[jax.ShapeDtypeStruct((t, D), F32), jax.ShapeDtypeStruct((1, D), F32), jax.ShapeDtypeStruct((1, 128), F32)],
        compiler_params=_cparams(("arbitrary",)),
    )(x, gain, target)


def _adamw(g, w, m, v):
    m = ADAM_B1 * m + (1.0 - ADAM_B1) * g
    v = ADAM_B2 * v + (1.0 - ADAM_B2) * (g * g)
    m_hat = m / (1.0 - ADAM_B1 ** ADAM_STEP)
    v_hat = v / (1.0 - ADAM_B2 ** ADAM_STEP)
    return -ADAM_LR * (m_hat / (jnp.sqrt(v_hat) + ADAM_EPS) + ADAM_WD * w), m, v


def _adamw_matrix(parts, sums, w, m, v, transposed, name):
    _, r, c = parts.shape
    tc = 256

    def body(own_ref, p_ref, s_ref, w_ref, m_ref, v_ref, g_ref, d_ref, mo_ref, vo_ref):
        g = p_ref[0].astype(F32) + p_ref[1].astype(F32) + p_ref[2].astype(F32) + s_ref[...].astype(F32)
        g = g.T if transposed else g
        g_ref[...] = g
        d_ref[...], mo_ref[...], vo_ref[...] = _adamw(g, w_ref[...], m_ref[...], v_ref[...])

    own = pl.BlockSpec((None, tc, r), lambda i, o: (0, i, 0)) if transposed else pl.BlockSpec((None, r, tc), lambda i, o: (0, 0, i))
    return pl.pallas_call(
        body, name=name,
        grid_spec=pltpu.PrefetchScalarGridSpec(
            num_scalar_prefetch=1, grid=(c // tc,),
            in_specs=[pl.BlockSpec((3, r, tc), lambda i, o: (0, 0, i)), pl.BlockSpec((None, r, tc), lambda i, o: (o[0], 0, i)), own, own, own],
            out_specs=[own] * 4),
        out_shape=[jax.ShapeDtypeStruct(w.shape, F32)] * 4,
        compiler_params=_cparams(("arbitrary",)),
    )(_my_index(("x", 2), ("y", 1)), parts, sums, w, m, v)


def _sum_adamw_small(parts, late, w, m, v):
    def body(p_ref, l_ref, w_ref, m_ref, v_ref, *outs):
        g = p_ref[0]
        top = l_ref[0]
        for k in range(1, N_DEV):
            g = g + p_ref[k]
            top = top + l_ref[k]
        for rows, gr in ((slice(0, 8), g[0:8] + top), (slice(8, None), g[8:])):
            res = (gr,) + _adamw(gr, w_ref[rows, :], m_ref[rows, :], v_ref[rows, :])
            for o_ref, val in zip(outs, res):
                o_ref[rows, :] = val

    return pl.pallas_call(body, name="adamw_small", out_shape=[jax.ShapeDtypeStruct(w.shape, F32)] * 4,
                          compiler_params=_cparams())(parts, late, w, m, v)


def _pack_rows(groups, name):
    flat = [a for grp in groups for a, _ in grp]
    rows = [grp[0][0].shape[2] if grp[0][1] else grp[0][0].shape[1] for grp in groups]

    def body(*refs):
        o_ref, pos, off = refs[-1], 0, 0
        for grp, r in zip(groups, rows):
            vals = []
            for _, transposed in grp:
                val = refs[pos][0]
                vals.append(val.T if transposed else val)
                pos += 1
            o_ref[off:off + r, :] = (vals[0] if len(vals) == 1 else jnp.concatenate(vals, axis=1)).astype(BF16)
            off += r

    return pl.pallas_call(body, name=name, out_shape=jax.ShapeDtypeStruct((sum(rows), D), BF16), compiler_params=_cparams())(*flat)


def _pad_row(a):
    a = a.reshape(1, -1)
    return jnp.pad(a, ((0, 0), (0, D - a.shape[1])))


def _pack_small(loss, n1, nm, n2, nf, bg, lng, lnb, bs, rel, ws):
    rows = [_pad_row(loss), n1.reshape(1, D), nm.reshape(1, D), n2.reshape(1, D), nf.reshape(1, D), bg.reshape(2, D),
            jnp.concatenate([lng.reshape(1, D_SGU), lnb.reshape(1, D_SGU)], axis=1), bs.reshape(1, D),
            jnp.pad(rel.reshape(HEADS, N_REL), ((0, 0), (0, D - N_REL))), jnp.zeros((ROW_WS - ROW_REL - HEADS, D), F32),
            ws.reshape(128, D)]
    return jnp.concatenate(rows, axis=0).astype(F32)


def _unpack_small(s):
    return {"loss": s[ROW_LOSS, 0], "norm_ffn1": s[ROW_N1][None], "norm_mix": s[ROW_NM][None], "norm_ffn2": s[ROW_N2][None],
            "norm_final": s[ROW_NF], "b_gate": s[ROW_BG:ROW_BG + 2].reshape(1, 2 * D),
            "sgu_ln_g": s[ROW_LN, :D_SGU][None], "sgu_ln_b": s[ROW_LN, D_SGU:][None], "sgu_b_s": s[ROW_BS].reshape(1, 8, 128),
            "rel_bias": s[ROW_REL:ROW_REL + HEADS, :N_REL][None], "sgu_w_s": s[ROW_WS:].reshape(1, 8, 128, 128)}


_SMALL = ("norm_ffn1", "norm_mix", "norm_ffn2", "norm_final", "b_gate", "sgu_ln_g", "sgu_ln_b", "sgu_b_s", "rel_bias", "sgu_w_s")


def _pack_small_params(p, prefix=""):
    g = lambda n: p[prefix + n]
    return _pack_small(jnp.zeros((1,), F32), g("norm_ffn1"), g("norm_mix"), g("norm_ffn2"), g("norm_final"), g("b_gate"),
                       g("sgu_ln_g"), g("sgu_ln_b"), g("sgu_b_s"), g("rel_bias"), g("sgu_w_s"))


def _step(x, target, p):
    n1, nm, n2 = p["norm_ffn1"], p["norm_mix"], p["norm_ffn2"]
    nf = p["norm_final"].reshape(1, D)
    lng, lnb = p["sgu_ln_g"], p["sgu_ln_b"]
    w_m = jnp.where(jnp.asarray(_sgu_mask())[None], p["sgu_w_s"][0], 0.0).astype(BF16)
    wm2 = jnp.concatenate([w_m[0::2], w_m[1::2]], axis=2)
    w_mt = w_m.transpose(0, 2, 1)
    wmt2 = jnp.concatenate([w_mt[0::2], w_mt[1::2]], axis=2)
    bsx = jnp.repeat(p["sgu_b_s"][0].T, 64, axis=1)
    bias = _band_bias(p["rel_bias"][0])

    def chip_sums(grads, name):
        gots = _comm_only([_SiblingSwap(grads)], "swap_" + name)
        return _pair_sums(grads, gots, "pair_sums_" + name)

    def as_rows(a):
        return jnp.swapaxes(a, 1, 2)

    def updates(parts, sums, names):
        res = {}
        for pt, sm, n in zip(parts, sums, names):
            if p[n].shape[1:] == pt.shape[1:]:
                res[n] = _adamw_matrix(pt, sm, p[n], p["m_" + n], p["v_" + n], False, "adamw_" + n)
            elif p[n].shape[2] > 128:
                res[n] = [as_rows(o) for o in _adamw_matrix(pt, sm, as_rows(p[n]), as_rows(p["m_" + n]), as_rows(p["v_" + n]), False,
                                                            "adamw_" + n)]
            else:
                res[n] = _adamw_matrix(pt, sm, p[n], p["m_" + n], p["v_" + n], True, "adamw_" + n)
        return res

    rows1 = _pack_rows([[(as_rows(p["ffn1_w_gate"]), False)], [(as_rows(p["ffn1_w_up"]), False)], [(p["ffn1_w_down"], False)]], "pack_ffn1")
    rows_m = _pack_rows([[(as_rows(p["w_in"]), False)], [(p["w_branch_att"], True), (p["w_branch_sgu"], True)], [(p["w_out"], False)]],
                        "pack_mixer")
    rows2d = _pack_rows([[(p["ffn2_w_down"], False)]], "pack_ffn2_down")
    rows2gu = _pack_rows([[(as_rows(p["ffn2_w_gate"]), False)], [(as_rows(p["ffn2_w_up"]), False)]], "pack_ffn2_gate_up")
    (gw1,) = _comm_only([_Gather(rows1)], "gather_ffn1")
    x1, ab1, (gwm,) = _ffn_fwd(x, n1, gw1, gw1, 2 * R_FF, "ffn1_fwd", [_Gather(rows_m)])
    (q, k, v, zs, gt, h2), (gw2d,) = _mix_proj_fwd(x1, nm, p["b_gate"], gwm, [_Gather(rows2d)])
    y_att, (gw2gu,) = _att_fwd(q, k, v, bias, [_Gather(rows2gu)])
    x2, y_sgu, merged = _merge_fwd(x1, zs, gt, y_att, lng, lnb, wm2, bsx, gwm)
    x3, ab2, _ = _ffn_fwd(x2, n2, gw2gu, gw2d, 0, "ffn2_fwd")
    dx3, d_nf, loss = _loss_head(x3, nf, target)

    (dx2, d_n2, hb, dab, sb, dfb), _ = _ffn_bwd(x2, n2, ab2, dx3, gw2gu, gw2d, 0, "ffn2_bwd")
    g2 = [_weight_grad(dab, hb, "ffn2_dw_gate", 0, F), _weight_grad(dab, hb, "ffn2_dw_up", F, F), _weight_grad(sb, dfb, "ffn2_dw_down")]
    dzg, dya, dys, dpp, dxb, d_bg = _merge_bwd(dx2, gt, y_att, y_sgu, gwm)
    g_late = g2 + [_weight_grad(dpp, y_att, "dw_branch_att", 0, D), _weight_grad(dpp, y_sgu, "dw_branch_sgu", D, D),
                   _weight_grad(merged, dxb, "dw_out")]
    late = ("ffn2_w_gate", "ffn2_w_up", "ffn2_w_down", "w_branch_att", "w_branch_sgu", "w_out")
    (dzs, d_wm, d_bs, d_lng, d_lnb), gots_late = _sgu_bwd(zs, dys, lng, lnb, wm2, wmt2, bsx, [_SiblingSwap(g_late)])
    sums_late = _pair_sums(g_late, gots_late, "pair_sums_late")
    (dq, dk, dv, d_bias), parts_late = _att_bwd(q, k, v, bias, dya, [_ChipScatter(sums_late)])
    big = updates(parts_late, sums_late, late)
    d_rel = _rel_bias_grad(d_bias)
    dz = jnp.concatenate([dq, dk, dv, dzs, dzg], axis=1)
    small = _pack_small(loss[0, :1], jnp.zeros((1, D), F32), jnp.zeros((1, D), F32), d_n2, d_nf, d_bg, d_lng, d_lnb, d_bs[:, :8].T, d_rel,
                        d_wm.reshape(8, 128, 128))
    g_in, (small_parts,) = _weight_grad(dz, h2, "dw_in", comms=[_AllToAll(small)])
    sums_in = chip_sums([g_in], "w_in")
    (dx1, d_nm), parts_in = _mix_proj_bwd(dz, x1, nm, dx2, gwm, [_ChipScatter(sums_in)])
    big.update(updates(parts_in, sums_in, ("w_in",)))

    (dx0, d_n1, hb, dab, sb, dfb), _ = _ffn_bwd(x, n1, ab1, dx1, gw1, gw1, 2 * R_FF, "ffn1_bwd")
    small_late = jnp.zeros((8, D), F32).at[ROW_N1].set(d_n1[0]).at[ROW_NM].set(d_nm[0])
    sums_d = chip_sums([_weight_grad(sb, dfb, "ffn1_dw_down")], "ffn1_down")
    g_gate, parts_d = _weight_grad(dab, hb, "ffn1_dw_gate", 0, F, [_ChipScatter(sums_d)])
    sums_g = chip_sums([g_gate], "ffn1_gate")
    g_up, (parts_g, late_parts) = _weight_grad(dab, hb, "ffn1_dw_up", F, F, [_ChipScatter(sums_g), _AllToAll(small_late)])
    sums_u = chip_sums([g_up], "ffn1_up")
    parts_u = _comm_only([_ChipScatter(sums_u)], "scatter_ffn1_up")
    big.update(updates(parts_d + [parts_g] + parts_u, sums_d + sums_g + sums_u, ("ffn1_w_down", "ffn1_w_gate", "ffn1_w_up")))
    out_s = _sum_adamw_small(small_parts, late_parts, _pack_small_params(p), _pack_small_params(p, "m_"), _pack_small_params(p, "v_"))
    return dx0, [{**{n: four[i] for n, four in big.items()}, **_unpack_small(s)} for i, s in enumerate(out_s)]


_OUT_ORDER = ("norm_ffn1", "ffn1_w_gate", "ffn1_w_up", "ffn1_w_down", "norm_mix", "w_in", "b_gate", "rel_bias", "sgu_ln_g", "sgu_ln_b",
              "sgu_w_s", "sgu_b_s", "w_branch_att", "w_branch_sgu", "w_out", "norm_ffn2", "ffn2_w_gate", "ffn2_w_up", "ffn2_w_down",
              "norm_final")


def kernel(x, norm_ffn1, ffn1_w_gate, ffn1_w_up, ffn1_w_down, norm_mix, w_in, b_gate, rel_bias, sgu_ln_g, sgu_ln_b, sgu_w_s, sgu_b_s, w_branch_att, w_branch_sgu, w_out, norm_ffn2, ffn2_w_gate, ffn2_w_up, ffn2_w_down, norm_final, loss_target, m_norm_ffn1, m_ffn1_w_gate, m_ffn1_w_up, m_ffn1_w_down, m_norm_mix, m_w_in, m_b_gate, m_rel_bias, m_sgu_ln_g, m_sgu_ln_b, m_sgu_w_s, m_sgu_b_s, m_w_branch_att, m_w_branch_sgu, m_w_out, m_norm_ffn2, m_ffn2_w_gate, m_ffn2_w_up, m_ffn2_w_down, m_norm_final, v_norm_ffn1, v_ffn1_w_gate, v_ffn1_w_up, v_ffn1_w_down, v_norm_mix, v_w_in, v_b_gate, v_rel_bias, v_sgu_ln_g, v_sgu_ln_b, v_sgu_w_s, v_sgu_b_s, v_w_branch_att, v_w_branch_sgu, v_w_out, v_norm_ffn2, v_ffn2_w_gate, v_ffn2_w_up, v_ffn2_w_down, v_norm_final):
    args = dict(locals())
    dx, outs = _step(x[0], loss_target[0], {pre + n: args[pre + n] for pre in ("", "m_", "v_") for n in _OUT_ORDER})
    return (outs[0]["loss"], dx[None], *[o[n] for o in outs for n in _OUT_ORDER])
```

```python
import functools

import numpy as np
import jax
import jax.numpy as jnp
from jax import lax
from jax.experimental import pallas as pl
from jax.experimental.pallas import tpu as pltpu

F32 = jnp.float32
BF16 = jnp.bfloat16

N_DEV = 8
D = 1024
F = 2816
D_ATT = 512
D_SGU = 512
D_IN = 4608
HEADS = 8
CHUNK = 64
N_LEFT = 8
REL_CLIP = 256
N_REL = 2 * REL_CLIP + 1
SGU_BLOCK = 128
EPS = 1e-6
NEG_INF = -1e30
QB = 256
KW = 3 * QB

R_FF, R_IN, R_BR, R_WO = F // N_DEV, D_IN // N_DEV, D // N_DEV, D // N_DEV
OFF_IN, OFF_BR, OFF_WO = 0, R_IN, R_IN + R_BR
FFN_ROWS = 3 * R_FF
MIX_ROWS = R_IN + R_BR + R_WO

FC = 256
TM = 512
VMEM_LIMIT = 56 * 1024 * 1024

ADAM_LR, ADAM_B1, ADAM_B2, ADAM_EPS, ADAM_WD, ADAM_STEP = 0.001, 0.9, 0.999, 1e-08, 0.01, 10

ROW_LOSS, ROW_N1, ROW_NM, ROW_N2, ROW_NF, ROW_BG, ROW_LN, ROW_BS, ROW_REL, ROW_WS = 0, 1, 2, 3, 4, 5, 7, 8, 9, 24
SMALL_ROWS = ROW_WS + 128

MESH = pl.DeviceIdType.MESH
ANY = pl.BlockSpec(memory_space=pl.ANY)


def _nt(a, b):
    return lax.dot_general(a, b, (((1,), (1,)), ((), ())), preferred_element_type=F32)


def _tn(a, b):
    return lax.dot_general(a, b, (((0,), (0,)), ((), ())), preferred_element_type=F32)


def _nn(a, b):
    return jnp.dot(a, b, preferred_element_type=F32)


def _cparams(sem=None):
    return pltpu.CompilerParams(dimension_semantics=sem, vmem_limit_bytes=VMEM_LIMIT)


def _load_rows(gw_ref, dst, off, rows, sems):
    copies = [pltpu.make_async_copy(gw_ref.at[k, pl.ds(off, rows), :], dst.at[pl.ds(k * rows, rows), :], sems.at[k])
              for k in range(N_DEV)]
    for cp in copies:
        cp.start()
    return copies


def _rms(xv):
    r = lax.rsqrt(jnp.mean(xv * xv, axis=-1, keepdims=True) + EPS)
    return xv * r, r


def _rms_bwd(dh, xn, r, gain):
    dxn = dh * gain
    dx = r * (dxn - xn * jnp.mean(dxn * xn, axis=-1, keepdims=True))
    return dx, jnp.sum(dh * xn, axis=0, keepdims=True)


def _gelu(x):
    t = jnp.tanh(0.7978845608028654 * (x + 0.044715 * x * x * x))
    return 0.5 * x * (1.0 + t), t


def _gelu_grad(x, t):
    return 0.5 * (1.0 + t) + 0.5 * x * (1.0 - t * t) * 0.7978845608028654 * (1.0 + 3.0 * 0.044715 * x * x)


def _place():
    x, y, cc = lax.axis_index("x"), lax.axis_index("y"), lax.axis_index("c")
    return x, y, cc, [(1 - x, y), (x, 1 - y), (1 - x, 1 - y)]


class _Gather:
    def __init__(self, shard):
        self.inputs = [shard]
        self.out_shape = [jax.ShapeDtypeStruct((N_DEV,) + shard.shape, shard.dtype)]
        self.scratch = [pltpu.SemaphoreType.DMA((7,)), pltpu.SemaphoreType.DMA((7,)), pltpu.SemaphoreType.DMA]

    def _copies(self, ins, outs, scr):
        (x_ref,), (out_ref,), (send_sems, recv_sems, local_sem) = ins, outs, scr
        x, y, cc, chips = _place()

        def slab(px, py, pc):
            return out_ref.at[4 * px + 2 * py + pc]

        def copy(k, block, to, src=None):
            return pltpu.make_async_remote_copy(
                src_ref=slab(*block) if src is None else src, dst_ref=slab(*block),
                send_sem=send_sems.at[k], recv_sem=recv_sems.at[k], device_id=to, device_id_type=MESH)

        me, sibling = (x, y, cc), (x, y, 1 - cc)
        mine = pltpu.make_async_copy(x_ref, slab(*me), local_sem)
        first = [copy(0, me, sibling, src=x_ref)] + [copy(1 + j, me, (*chip, cc), src=x_ref) for j, chip in enumerate(chips)]
        landed = [copy(1 + j, (*chip, cc), me) for j, chip in enumerate(chips)]
        passed = [copy(4 + j, (*chip, cc), sibling) for j, chip in enumerate(chips)]
        from_sibling = [copy(0, sibling, me)] + [copy(4 + j, (*chip, 1 - cc), me) for j, chip in enumerate(chips)]
        return mine, first, landed, passed, from_sibling

    def begin(self, *refs):
        mine, first, _, _, _ = self._copies(*refs)
        mine.start()
        for cp in first:
            cp.start()

    def relay(self, *refs):
        _, _, landed, passed, _ = self._copies(*refs)
        for arrived, onward in zip(landed, passed):
            arrived.wait_recv()
            onward.start()

    def end(self, *refs):
        mine, first, _, passed, from_sibling = self._copies(*refs)
        for cp in from_sibling:
            cp.wait_recv()
        for cp in first + passed:
            cp.wait_send()
        mine.wait()


class _Direct:
    def begin(self, *refs):
        keep, give = self._copies(*refs)
        for cp in keep + give:
            cp.start()

    def relay(self, *refs):
        pass

    def end(self, *refs):
        keep, give = self._copies(*refs)
        for cp in give:
            cp.wait_recv()
        for cp in give:
            cp.wait_send()
        for cp in keep:
            cp.wait()


class _SiblingSwap(_Direct):
    def __init__(self, grads):
        n = len(grads)
        self.inputs = list(grads)
        self.out_shape = [jax.ShapeDtypeStruct((4,) + g.shape[1:], g.dtype) for g in grads]
        self.scratch = [pltpu.SemaphoreType.DMA((n, 4)), pltpu.SemaphoreType.DMA((n, 4))]

    def _copies(self, ins, outs, scr):
        send_sems, recv_sems = scr
        x, y, cc, _ = _place()
        return [], [pltpu.make_async_remote_copy(src_ref=g_ref.at[2 * j + 1 - cc], dst_ref=got_ref.at[j], send_sem=send_sems.at[i, j],
                                                 recv_sem=recv_sems.at[i, j], device_id=(x, y, 1 - cc), device_id_type=MESH)
                    for i, (g_ref, got_ref) in enumerate(zip(ins, outs)) for j in range(4)]


class _ChipScatter(_Direct):
    def __init__(self, sums):
        n = len(sums)
        self.inputs = list(sums)
        self.out_shape = [jax.ShapeDtypeStruct((3,) + s.shape[1:], s.dtype) for s in sums]
        self.scratch = [pltpu.SemaphoreType.DMA((n, 3)), pltpu.SemaphoreType.DMA((n, 3))]

    def _copies(self, ins, outs, scr):
        send_sems, recv_sems = scr
        _, _, cc, chips = _place()
        return [], [pltpu.make_async_remote_copy(src_ref=s_ref.at[2 * px + py], dst_ref=got_ref.at[j], send_sem=send_sems.at[i, j],
                                                 recv_sem=recv_sems.at[i, j], device_id=(px, py, cc), device_id_type=MESH)
                    for i, (s_ref, got_ref) in enumerate(zip(ins, outs)) for j, (px, py) in enumerate(chips)]


class _AllToAll(_Direct):
    def __init__(self, block):
        self.inputs = [block]
        self.out_shape = [jax.ShapeDtypeStruct((N_DEV,) + block.shape, block.dtype)]
        self.scratch = [pltpu.SemaphoreType.DMA((7,)), pltpu.SemaphoreType.DMA((7,)), pltpu.SemaphoreType.DMA]

    def _copies(self, ins, outs, scr):
        (b_ref,), (got_ref,), (send_sems, recv_sems, local_sem) = ins, outs, scr
        x, y, cc, _ = _place()
        me = 4 * x + 2 * y + cc
        keep = [pltpu.make_async_copy(b_ref, got_ref.at[me], local_sem)]
        give = [pltpu.make_async_remote_copy(src_ref=b_ref, dst_ref=got_ref.at[me], send_sem=send_sems.at[k - 1], recv_sem=recv_sems.at[k - 1],
                                             device_id=(x ^ ((k >> 2) & 1), y ^ ((k >> 1) & 1), cc ^ (k & 1)), device_id_type=MESH)
                for k in range(1, N_DEV)]
        return keep, give


def _split_refs(refs, counts):
    out, pos = [], 0
    for n in counts:
        out.append(list(refs[pos:pos + n]))
        pos += n
    return out


def _bind(comms, c_in, c_out, c_scr):
    ins = _split_refs(c_in, [len(c.inputs) for c in comms])
    outs = _split_refs(c_out, [len(c.out_shape) for c in comms])
    scr = _split_refs(c_scr, [len(c.scratch) for c in comms])
    return [(c, (i, o, s)) for c, i, o, s in zip(comms, ins, outs, scr)]


def _call(body, *, name, grid, in_specs, out_specs, out_shape, scratch_shapes, args, comms=()):
    c_in = [a for c in comms for a in c.inputs]
    c_out = [s for c in comms for s in c.out_shape]
    c_scr = [s for c in comms for s in c.scratch]
    counts = [len(in_specs), len(c_in), len(out_shape), len(c_out), len(scratch_shapes), len(c_scr)]
    last = grid[0] - 1

    def full(*refs):
        ins, cin, outs, cout, scr, cscr = _split_refs(refs, counts)
        bound = _bind(comms, cin, cout, cscr)
        step = pl.program_id(0)
        if comms:
            @pl.when(step == 0)
            def _():
                for c, r in bound:
                    c.begin(*r)

            @pl.when(step == last)
            def _():
                for c, r in bound:
                    c.relay(*r)

        body(*ins, *outs, *scr)
        if comms:
            @pl.when(step == last)
            def _():
                for c, r in bound:
                    c.end(*r)

    res = pl.pallas_call(
        full, name=name, grid=grid,
        in_specs=list(in_specs) + [ANY] * len(c_in), out_specs=list(out_specs) + [ANY] * len(c_out),
        out_shape=list(out_shape) + c_out, scratch_shapes=list(scratch_shapes) + c_scr,
        compiler_params=_cparams(("arbitrary",)),
    )(*args, *c_in)
    return list(res[:len(out_shape)]), list(res[len(out_shape):])


def _comm_only(comms, name):
    c_in = [a for c in comms for a in c.inputs]
    c_out = [s for c in comms for s in c.out_shape]
    c_scr = [s for c in comms for s in c.scratch]

    def full(*refs):
        cin, cout, cscr = _split_refs(refs, [len(c_in), len(c_out), len(c_scr)])
        bound = _bind(comms, cin, cout, cscr)
        for phase in ("begin", "relay", "end"):
            for c, r in bound:
                getattr(c, phase)(*r)

    return list(pl.pallas_call(full, name=name, in_specs=[ANY] * len(c_in), out_specs=[ANY] * len(c_out), out_shape=c_out,
                               scratch_shapes=c_scr)(*c_in))


def _my_index(*axes_and_weights):
    return sum(w * lax.axis_index(a) for a, w in axes_and_weights).astype(jnp.int32).reshape(1)


def _pair_sums(grads, gots, name):
    n = len(grads)

    def body(c_ref, *refs):
        for a_ref, b_ref, o_ref in zip(refs[:n], refs[n:2 * n], refs[2 * n:]):
            o_ref[...] = (a_ref[...].astype(F32) + b_ref[...].astype(F32)).astype(BF16)

    def tile(g):
        return pl.BlockSpec((1,) + g.shape[1:], lambda j, c_ref: (j, 0, 0))

    def mine(g):
        return pl.BlockSpec((1, None) + g.shape[1:], lambda j, c_ref: (j, c_ref[0], 0, 0))

    return list(pl.pallas_call(
        body, name=name,
        grid_spec=pltpu.PrefetchScalarGridSpec(num_scalar_prefetch=1, grid=(4,), in_specs=[mine(g) for g in gots] + [tile(g) for g in gots],
                                               out_specs=[tile(g) for g in gots]),
        out_shape=[jax.ShapeDtypeStruct(g.shape, BF16) for g in gots],
        compiler_params=_cparams(("arbitrary",)))(_my_index(("c", 1)), *[g.reshape((4, 2) + g.shape[1:]) for g in grads], *gots))


def _ffn_fwd(x, gain, gw_gu, gw_d, off_d, name, comms=(), head=None):
    t = x.shape[0]
    n_head = 0 if head is None else 2

    def body(x_ref, g_ref, gu_ref, d_ref, *refs):
        head_in, (o_ref, ab_ref), head_out = refs[:n_head], refs[n_head:n_head + 2], refs[n_head + 2:2 * n_head + 2]
        wg, wu, wd, s_scr, sems = refs[2 * n_head + 2:]

        @pl.when(pl.program_id(0) == 0)
        def _():
            cps = _load_rows(gu_ref, wg, 0, R_FF, sems.at[0]) + _load_rows(gu_ref, wu, R_FF, R_FF, sems.at[1]) \
                + _load_rows(d_ref, wd, off_d, R_FF, sems.at[2])
            for o in head_out:
                o[...] = jnp.zeros_like(o)
            for cp in cps:
                cp.wait()

        xv = x_ref[...]
        xn, _ = _rms(xv)
        h = (xn * g_ref[...]).astype(BF16)
        for c in range(F // FC):
            rows = pl.ds(c * FC, FC)
            a = _nt(h, wg[rows, :])
            b = _nt(h, wu[rows, :])
            ab_ref[:, c * FC:(c + 1) * FC] = a.astype(BF16)
            ab_ref[:, F + c * FC:F + (c + 1) * FC] = b.astype(BF16)
            s_scr[:, c * FC:(c + 1) * FC] = (a * jax.nn.sigmoid(a) * b).astype(BF16)
        out = xv + 0.5 * _nn(s_scr[...], wd[...])
        if head is None:
            o_ref[...] = out
        else:
            (gf_ref, t_ref), (dg_ref, loss_ref) = head_in, head_out
            gain_f = gf_ref[...]
            yn, r = _rms(out)
            err = yn * gain_f - t_ref[...]
            loss_ref[...] += 0.5 * jnp.sum(jnp.mean(err * err, axis=-1, keepdims=True), axis=0, keepdims=True)
            o_ref[...], dg = _rms_bwd(err * (1.0 / D), yn, r, gain_f)
            dg_ref[...] += dg

    tile = pl.BlockSpec((TM, D), lambda i: (i, 0))
    row = pl.BlockSpec((1, D), lambda i: (0, 0))
    head_specs = [] if head is None else [row, pl.BlockSpec((1, 128), lambda i: (0, 0))]
    head_shapes = [] if head is None else [jax.ShapeDtypeStruct((1, D), F32), jax.ShapeDtypeStruct((1, 128), F32)]
    res, got = _call(
        body, name=name, grid=(t // TM,),
        in_specs=[tile, row, ANY, ANY] + ([] if head is None else [row, tile]),
        out_specs=[tile, pl.BlockSpec((TM, 2 * F), lambda i: (i, 0))] + head_specs,
        out_shape=[jax.ShapeDtypeStruct((t, D), F32), jax.ShapeDtypeStruct((t, 2 * F), BF16)] + head_shapes,
        scratch_shapes=[pltpu.VMEM((F, D), BF16)] * 3 + [pltpu.VMEM((TM, F), BF16), pltpu.SemaphoreType.DMA((3, N_DEV))],
        args=(x, gain, gw_gu, gw_d) + (() if head is None else tuple(head)), comms=comms)
    return (res[0], res[1], got, *res[2:])


def _ffn_bwd(x, gain, ab, dout, gw_gu, gw_d, off_d, name, comms=()):
    t = x.shape[0]

    def tile(w):
        return pl.BlockSpec((TM, w), lambda i: (i, 0))

    def hidden(ab_ref, do_ref, d_ref, dab_ref, s_ref, df_ref, wd, sems):
        @pl.when(pl.program_id(0) == 0)
        def _():
            for cp in _load_rows(d_ref, wd, off_d, R_FF, sems):
                cp.wait()

        df = (0.5 * do_ref[...]).astype(BF16)
        df_ref[...] = df
        for c in range(F // FC):
            a = ab_ref[:, c * FC:(c + 1) * FC].astype(F32)
            b = ab_ref[:, F + c * FC:F + (c + 1) * FC].astype(F32)
            sg = jax.nn.sigmoid(a)
            sl = a * sg
            ds = _nt(df, wd[pl.ds(c * FC, FC), :])
            dab_ref[:, c * FC:(c + 1) * FC] = (ds * b * (sg * (1.0 + a * (1.0 - sg)))).astype(BF16)
            dab_ref[:, F + c * FC:F + (c + 1) * FC] = (ds * sl).astype(BF16)
            s_ref[:, c * FC:(c + 1) * FC] = (sl * b).astype(BF16)

    (dab, s, df), _ = _call(
        hidden, name=name + "_hidden", grid=(t // TM,),
        in_specs=[tile(2 * F), tile(D), ANY], out_specs=[tile(2 * F), tile(F), tile(D)],
        out_shape=[jax.ShapeDtypeStruct((t, 2 * F), BF16), jax.ShapeDtypeStruct((t, F), BF16), jax.ShapeDtypeStruct((t, D), BF16)],
        scratch_shapes=[pltpu.VMEM((F, D), BF16), pltpu.SemaphoreType.DMA((N_DEV,))],
        args=(ab, dout, gw_d))

    def body(x_ref, g_ref, dab_ref, do_ref, gu_ref, dx_ref, dg_ref, h_ref, wg, wu, sems):
        @pl.when(pl.program_id(0) == 0)
        def _():
            cps = _load_rows(gu_ref, wg, 0, R_FF, sems.at[0]) + _load_rows(gu_ref, wu, R_FF, R_FF, sems.at[1])
            dg_ref[...] = jnp.zeros_like(dg_ref)
            for cp in cps:
                cp.wait()

        gain_v = g_ref[...]
        xn, r = _rms(x_ref[...])
        h_ref[...] = (xn * gain_v).astype(BF16)
        dh = _nn(dab_ref[:, :F], wg[...]) + _nn(dab_ref[:, F:], wu[...])
        dxn, dg = _rms_bwd(dh, xn, r, gain_v)
        dg_ref[...] += dg
        dx_ref[...] = do_ref[...] + dxn

    row = pl.BlockSpec((1, D), lambda i: (0, 0))
    (dx, dg, h), got = _call(
        body, name=name, grid=(t // TM,),
        in_specs=[tile(D), row, tile(2 * F), tile(D), ANY], out_specs=[tile(D), row, tile(D)],
        out_shape=[jax.ShapeDtypeStruct((t, D), F32), jax.ShapeDtypeStruct((1, D), F32), jax.ShapeDtypeStruct((t, D), BF16)],
        scratch_shapes=[pltpu.VMEM((F, D), BF16)] * 2 + [pltpu.SemaphoreType.DMA((2, N_DEV))],
        args=(x, gain, dab, dout, gw_gu), comms=comms)
    return (dx, dg, h, dab, s, df), got


def _weight_grad(a, b, name, col_off=0, m=None, comms=None, tmm=256):
    t = a.shape[0]
    m = a.shape[1] if m is None else m
    n = b.shape[1]
    first = col_off // tmm

    def body(a_ref, b_ref, o_ref):
        o_ref[...] = _tn(a_ref[...], b_ref[...]).astype(BF16)

    (out,), got = _call(
        body, name=name, grid=(m // tmm,),
        in_specs=[pl.BlockSpec((t, tmm), lambda i: (0, first + i)), pl.BlockSpec((t, n), lambda i: (0, 0))],
        out_specs=[pl.BlockSpec((tmm, n), lambda i: (i, 0))],
        out_shape=[jax.ShapeDtypeStruct((m, n), BF16)], scratch_shapes=[], args=(a, b), comms=comms or ())
    out = out.reshape(N_DEV, m // N_DEV, n)
    return out if comms is None else (out, got)


def _mix_proj_fwd(x, gain, b_gate, gw, comms=()):
    t = x.shape[0]

    def body(x_ref, g_ref, bg_ref, gw_ref, q_ref, k_ref, v_ref, zs_ref, gt_ref, h_ref, win, sems):
        @pl.when(pl.program_id(0) == 0)
        def _():
            for cp in _load_rows(gw_ref, win, OFF_IN, R_IN, sems):
                cp.wait()

        xn, _ = _rms(x_ref[...])
        h = (xn * g_ref[...]).astype(BF16)
        h_ref[...] = h
        q_ref[...] = (_nt(h, win[0:512, :]) * 0.125).astype(BF16)
        k_ref[...] = _nt(h, win[512:1024, :]).astype(BF16)
        v_ref[...] = _nt(h, win[1024:1536, :]).astype(BF16)
        for c in range(2):
            zs_ref[:, c * 512:(c + 1) * 512] = _nt(h, win[1536 + c * 512:2048 + c * 512, :]).astype(BF16)
        for c in range(4):
            zg = _nt(h, win[2560 + c * 512:3072 + c * 512, :]) + bg_ref[:, c * 512:(c + 1) * 512]
            gt_ref[:, c * 512:(c + 1) * 512] = jax.nn.sigmoid(zg).astype(BF16)

    def tile(w):
        return pl.BlockSpec((TM, w), lambda i: (i, 0))

    return _call(
        body, name="mix_proj_fwd", grid=(t // TM,),
        in_specs=[tile(D), pl.BlockSpec((1, D), lambda i: (0, 0)), pl.BlockSpec((1, 2 * D), lambda i: (0, 0)), ANY],
        out_specs=[tile(D_ATT), tile(D_ATT), tile(D_ATT), tile(2 * D_SGU), tile(2 * D), tile(D)],
        out_shape=[jax.ShapeDtypeStruct((t, D_ATT), BF16)] * 3 + [jax.ShapeDtypeStruct((t, 2 * D_SGU), BF16),
                                                                   jax.ShapeDtypeStruct((t, 2 * D), BF16),
                                                                   jax.ShapeDtypeStruct((t, D), BF16)],
        scratch_shapes=[pltpu.VMEM((D_IN, D), BF16), pltpu.SemaphoreType.DMA((N_DEV,))],
        args=(x, gain, b_gate, gw), comms=comms)


def _mix_proj_bwd(dz, x, gain, dres, gw, comms=()):
    t = x.shape[0]

    def body(dz_ref, x_ref, g_ref, dr_ref, gw_ref, dx_ref, dg_ref, win, sems):
        @pl.when(pl.program_id(0) == 0)
        def _():
            cps = _load_rows(gw_ref, win, OFF_IN, R_IN, sems)
            dg_ref[...] = jnp.zeros_like(dg_ref)
            for cp in cps:
                cp.wait()

        dh = _nn(dz_ref[...], win[...])
        xn, r = _rms(x_ref[...])
        dxn, dg = _rms_bwd(dh, xn, r, g_ref[...])
        dg_ref[...] += dg
        dx_ref[...] = dr_ref[...] + dxn

    def tile(w):
        return pl.BlockSpec((TM, w), lambda i: (i, 0))

    row = pl.BlockSpec((1, D), lambda i: (0, 0))
    return _call(
        body, name="mix_proj_bwd", grid=(t // TM,),
        in_specs=[tile(D_IN), tile(D), row, tile(D), ANY], out_specs=[tile(D), row],
        out_shape=[jax.ShapeDtypeStruct((t, D), F32), jax.ShapeDtypeStruct((1, D), F32)],
        scratch_shapes=[pltpu.VMEM((D_IN, D), BF16), pltpu.SemaphoreType.DMA((N_DEV,))],
        args=(dz, x, gain, dres, gw), comms=comms)


SKEW_W = KW + QB
N_CAP = 2 * QB - REL_CLIP + 1


def _band_bias(rel_bias):
    cap = rel_bias[:, 2 * REL_CLIP:]
    diag = jnp.concatenate([jnp.broadcast_to(cap, (HEADS, N_CAP)), rel_bias[:, 2 * REL_CLIP - 1::-1],
                            jnp.broadcast_to(cap, (HEADS, SKEW_W - N_CAP - 2 * REL_CLIP))], axis=1)

    def body(d_ref, o_ref):
        lag = lax.broadcasted_iota(jnp.int32, (QB, KW), 1) // CHUNK - lax.broadcasted_iota(jnp.int32, (QB, KW), 0) // CHUNK
        band = (lag >= 0) & (lag <= N_LEFT)
        for h in range(HEADS):
            rows = jnp.broadcast_to(d_ref[h:h + 1, :], (QB, SKEW_W))
            o_ref[h] = jnp.where(band, pltpu.roll(rows, 0, 1, stride=1, stride_axis=0)[:, :KW], NEG_INF)

    return pl.pallas_call(body, name="band_bias", out_shape=jax.ShapeDtypeStruct((HEADS, QB, KW), F32))(diag)


def _att_specs():
    qspec = pl.BlockSpec((QB, D_ATT), lambda g: (g, 0))
    kspecs = [pl.BlockSpec((QB, D_ATT), lambda g: (jnp.maximum(g - 2, 0), 0)),
              pl.BlockSpec((QB, D_ATT), lambda g: (jnp.maximum(g - 1, 0), 0)), qspec]
    bspec = pl.BlockSpec((HEADS, QB, KW), lambda g: (0, 0, 0))
    return qspec, kspecs, bspec


def _att_probs(qm, kp, bias, valid):
    s = jnp.where(valid, _nt(qm, kp) + bias, NEG_INF)
    e = jnp.exp(s - jnp.max(s, axis=-1, keepdims=True))
    return e / jnp.sum(e, axis=-1, keepdims=True)


def _att_valid():
    g = pl.program_id(0)
    blk = lax.broadcasted_iota(jnp.int32, (QB, KW), 1) // QB
    return (blk + g) >= 2


def _att_fwd(q, k, v, bias, comms=()):
    t = q.shape[0]

    def body(q_ref, k0, k1, k2, v0, v1, v2, b_ref, y_ref):
        valid = _att_valid()
        first = lax.broadcasted_iota(jnp.int32, (1, 128), 1) < 64
        for p in range(HEADS // 2):
            lanes = slice(p * 128, (p + 1) * 128)
            qp = q_ref[:, lanes]
            kp = jnp.concatenate([k0[:, lanes], k1[:, lanes], k2[:, lanes]], axis=0)
            vp = jnp.concatenate([v0[:, lanes], v1[:, lanes], v2[:, lanes]], axis=0)
            out = jnp.zeros((QB, 128), F32)
            for hh in range(2):
                mask = first if hh == 0 else jnp.logical_not(first)
                pr = _att_probs(jnp.where(mask, qp, 0), kp, b_ref[2 * p + hh], valid)
                out = out + _nn(pr.astype(BF16), jnp.where(mask, vp, 0))
            y_ref[:, lanes] = out.astype(BF16)

    qspec, kspecs, bspec = _att_specs()
    (out,), got = _call(
        body, name="att_fwd", grid=(t // QB,),
        in_specs=[qspec] + kspecs + kspecs + [bspec], out_specs=[qspec],
        out_shape=[jax.ShapeDtypeStruct((t, D_ATT), BF16)], scratch_shapes=[],
        args=(q, k, k, k, v, v, v, bias), comms=comms)
    return out, got


def _att_bwd(q, k, v, bias, dy, comms=()):
    t = q.shape[0]
    n_blocks = t // QB

    def body(q_ref, k0, k1, k2, v0, v1, v2, b_ref, dy_ref, dq_ref, dk_ref, dv_ref, db_ref, dk_acc, dv_acc):
        g = pl.program_id(0)

        @pl.when(g == 0)
        def _():
            db_ref[...] = jnp.zeros_like(db_ref)
            dk_acc[...] = jnp.zeros_like(dk_acc)
            dv_acc[...] = jnp.zeros_like(dv_acc)

        valid = _att_valid()
        first = lax.broadcasted_iota(jnp.int32, (1, 128), 1) < 64
        for p in range(HEADS // 2):
            lanes = slice(p * 128, (p + 1) * 128)
            qp = q_ref[:, lanes]
            dyp = dy_ref[:, lanes]
            kp = jnp.concatenate([k0[:, lanes], k1[:, lanes], k2[:, lanes]], axis=0)
            vp = jnp.concatenate([v0[:, lanes], v1[:, lanes], v2[:, lanes]], axis=0)
            dq = jnp.zeros((QB, 128), F32)
            dk = jnp.zeros((KW, 128), F32)
            dv = jnp.zeros((KW, 128), F32)
            for hh in range(2):
                mask = first if hh == 0 else jnp.logical_not(first)
                qm = jnp.where(mask, qp, 0)
                dym = jnp.where(mask, dyp, 0)
                pr = _att_probs(qm, kp, b_ref[2 * p + hh], valid)
                dp = _nt(dym, vp)
                ds = pr * (dp - jnp.sum(dp * pr, axis=-1, keepdims=True))
                db_ref[2 * p + hh] += ds
                dsb = ds.astype(BF16)
                dq = dq + _nn(dsb, jnp.where(mask, kp, 0))
                dk = dk + _tn(dsb, qm)
                dv = dv + _tn(pr.astype(BF16), dym)
            dq_ref[:, lanes] = (dq * 0.125).astype(BF16)
            for j in range(3):
                rows = pl.ds(pl.multiple_of(jnp.maximum(g - 2 + j, 0) * QB, QB), QB)
                dk_acc[rows, lanes] += dk[j * QB:(j + 1) * QB]
                dv_acc[rows, lanes] += dv[j * QB:(j + 1) * QB]

        @pl.when(g == n_blocks - 1)
        def _():
            dk_ref[...] = dk_acc[...].astype(BF16)
            dv_ref[...] = dv_acc[...].astype(BF16)

    qspec, kspecs, bspec = _att_specs()
    full = pl.BlockSpec((t, D_ATT), lambda g: (0, 0))
    return _call(
        body, name="att_bwd", grid=(n_blocks,),
        in_specs=[qspec] + kspecs + kspecs + [bspec, qspec], out_specs=[qspec, full, full, bspec],
        out_shape=[jax.ShapeDtypeStruct((t, D_ATT), BF16)] * 3 + [jax.ShapeDtypeStruct((HEADS, QB, KW), F32)],
        scratch_shapes=[pltpu.VMEM((t, D_ATT), F32)] * 2,
        args=(q, k, k, k, v, v, v, bias, dy), comms=comms)


def _rel_bias_grad(dbias):
    def body(db_ref, cs_ref, tot_ref):
        lane = lax.broadcasted_iota(jnp.int32, (1, SKEW_W), 1)
        capped = (lane < N_CAP) | (lane > KW)
        pad = jnp.zeros((8, QB), F32)
        for h in range(HEADS):
            z = jnp.concatenate([db_ref[h, 0:8, :], pad], axis=1)
            for a in range(1, QB // 8):
                z = z + pltpu.roll(jnp.concatenate([db_ref[h, 8 * a:8 * a + 8, :], pad], axis=1), SKEW_W - 8 * a, 1)
            cs = z[0:1, :]
            for b in range(1, 8):
                cs = cs + pltpu.roll(z[b:b + 1, :], SKEW_W - b, 1)
            cs_ref[h:h + 1, :] = cs
            tot_ref[h:h + 1, :] = jnp.broadcast_to(jnp.sum(jnp.where(capped, cs, 0.0), axis=1, keepdims=True), (1, 128))

    cs, tot = pl.pallas_call(
        body, name="rel_bias_grad",
        out_shape=[jax.ShapeDtypeStruct((HEADS, SKEW_W), F32), jax.ShapeDtypeStruct((HEADS, 128), F32)],
    )(dbias)
    return jnp.concatenate([cs[:, KW:N_CAP - 1:-1], tot[:, :1]], axis=1)


def _sgu_mask():
    pos = np.arange(SGU_BLOCK)
    return (pos[:, None] // CHUNK) >= (pos[None, :] // CHUNK)


def _group_stack(blk, first):
    return jnp.concatenate([jnp.where(first, blk, 0), jnp.where(first, 0, blk)], axis=0)


def _sgu_norm(zs_ref, lng, lnb):
    zs = zs_ref[...].astype(F32)
    ga, th = _gelu(zs)
    u, vs = ga[:, :D_SGU], ga[:, D_SGU:]
    mu = jnp.mean(vs, axis=-1, keepdims=True)
    cen = vs - mu
    rstd = lax.rsqrt(jnp.mean(cen * cen, axis=-1, keepdims=True) + EPS)
    xhat = cen * rstd
    return zs, th, u, xhat, rstd, xhat * lng + lnb


def _sgu_mix(vb, wm2_ref, bsx, s_ref):
    first = lax.broadcasted_iota(jnp.int32, (1, 128), 1) < 64
    for n in range(TM // SGU_BLOCK):
        for p in range(4):
            blk = vb[n * 128:(n + 1) * 128, p * 128:(p + 1) * 128]
            s_ref[n * 128:(n + 1) * 128, p * 128:(p + 1) * 128] = _nn(wm2_ref[p], _group_stack(blk, first)) + bsx[:, p * 128:(p + 1) * 128]


def _merge_fwd(x, zs, gt, y_att, lng, lnb, wm2, bsx, gw):
    t = x.shape[0]

    def body(x_ref, zs_ref, gt_ref, ya_ref, lng_ref, lnb_ref, wm2_ref, bsx_ref, gw_ref, xo_ref, ys_ref, mg_ref,
             wbr, wo, s_scr, sems):
        @pl.when(pl.program_id(0) == 0)
        def _():
            for cp in _load_rows(gw_ref, wbr, OFF_BR, R_BR, sems.at[0]) + _load_rows(gw_ref, wo, OFF_WO, R_WO, sems.at[1]):
                cp.wait()

        _, _, u, _, _, vsn = _sgu_norm(zs_ref, lng_ref[...], lnb_ref[...])
        _sgu_mix(vsn.astype(BF16), wm2_ref, bsx_ref[...], s_scr)
        ys = (u * s_scr[...]).astype(BF16)
        ys_ref[...] = ys
        pa = _nt(ya_ref[...], wbr[:, :D_ATT])
        ps = _nt(ys, wbr[:, D_ATT:])
        mg = (gt_ref[:, :D].astype(F32) * pa + gt_ref[:, D:].astype(F32) * ps).astype(BF16)
        mg_ref[...] = mg
        xo_ref[...] = x_ref[...] + _nn(mg, wo[...])

    def tile(w):
        return pl.BlockSpec((TM, w), lambda i: (i, 0))

    def const(shape):
        return pl.BlockSpec(shape, lambda i: (0,) * len(shape))

    return pl.pallas_call(
        body, name="merge_fwd", grid=(t // TM,),
        in_specs=[tile(D), tile(2 * D_SGU), tile(2 * D), tile(D_ATT), const((1, D_SGU)), const((1, D_SGU)),
                  const((4, 128, 256)), const((128, D_SGU)), ANY],
        out_specs=[tile(D), tile(D_SGU), tile(D)],
        out_shape=[jax.ShapeDtypeStruct((t, D), F32), jax.ShapeDtypeStruct((t, D_SGU), BF16), jax.ShapeDtypeStruct((t, D), BF16)],
        scratch_shapes=[pltpu.VMEM((D, D), BF16), pltpu.VMEM((D, D), BF16), pltpu.VMEM((TM, D_SGU), F32),
                        pltpu.SemaphoreType.DMA((2, N_DEV))],
        compiler_params=_cparams(("arbitrary",)),
    )(x, zs, gt, y_att, lng, lnb, wm2, bsx, gw)


def _merge_bwd(dx, gt, y_att, y_sgu, gw):
    t = dx.shape[0]

    def body(dx_ref, gt_ref, ya_ref, ys_ref, gw_ref, dzg_ref, dya_ref, dys_ref, dpp_ref, dxb_ref, dbg_ref, wbr, wo, sems):
        @pl.when(pl.program_id(0) == 0)
        def _():
            cps = _load_rows(gw_ref, wbr, OFF_BR, R_BR, sems.at[0]) + _load_rows(gw_ref, wo, OFF_WO, R_WO, sems.at[1])
            dbg_ref[...] = jnp.zeros_like(dbg_ref)
            for cp in cps:
                cp.wait()

        dxb = dx_ref[...].astype(BF16)
        dxb_ref[...] = dxb
        dm = _nt(dxb, wo[...])
        for half, y_ref, w in ((0, ya_ref, wbr.at[:, :D_ATT]), (1, ys_ref, wbr.at[:, D_ATT:])):
            cols = slice(half * D, (half + 1) * D)
            gate = gt_ref[:, cols].astype(F32)
            branch = _nt(y_ref[...], w[...])
            dzg = dm * branch * gate * (1.0 - gate)
            dbg_ref[:, cols] += jnp.sum(dzg, axis=0, keepdims=True)
            dzg_ref[:, cols] = dzg.astype(BF16)
            dbr = (dm * gate).astype(BF16)
            dpp_ref[:, cols] = dbr
            dy = _nn(dbr, w[...])
            if half == 0:
                dya_ref[...] = dy.astype(BF16)
            else:
                dys_ref[...] = dy

    def tile(w):
        return pl.BlockSpec((TM, w), lambda i: (i, 0))

    return pl.pallas_call(
        body, name="merge_bwd", grid=(t // TM,),
        in_specs=[tile(D), tile(2 * D), tile(D_ATT), tile(D_SGU), ANY],
        out_specs=[tile(2 * D), tile(D_ATT), tile(D_SGU), tile(2 * D), tile(D), pl.BlockSpec((1, 2 * D), lambda i: (0, 0))],
        out_shape=[jax.ShapeDtypeStruct((t, 2 * D), BF16), jax.ShapeDtypeStruct((t, D_ATT), BF16), jax.ShapeDtypeStruct((t, D_SGU), F32),
                   jax.ShapeDtypeStruct((t, 2 * D), BF16), jax.ShapeDtypeStruct((t, D), BF16), jax.ShapeDtypeStruct((1, 2 * D), F32)],
        scratch_shapes=[pltpu.VMEM((D, D), BF16), pltpu.VMEM((D, D), BF16), pltpu.SemaphoreType.DMA((2, N_DEV))],
        compiler_params=_cparams(("arbitrary",)),
    )(dx, gt, y_att, y_sgu, gw)


def _sgu_bwd(zs, dys, lng, lnb, wm2, wmt2, bsx, comms=()):
    t = zs.shape[0]
    n_steps = t // TM

    def body(zs_ref, dys_ref, lng_ref, lnb_ref, wm2_ref, wmt2_ref, bsx_ref, dzs_ref, dw_ref, dbs_ref, dlg_ref, dlb_ref,
             s_scr, dv_scr, ds_acc):
        i = pl.program_id(0)

        @pl.when(i == 0)
        def _():
            dw_ref[...] = jnp.zeros_like(dw_ref)
            dlg_ref[...] = jnp.zeros_like(dlg_ref)
            dlb_ref[...] = jnp.zeros_like(dlb_ref)
            ds_acc[...] = jnp.zeros_like(ds_acc)

        lng = lng_ref[...]
        zs, th, u, xhat, rstd, vsn = _sgu_norm(zs_ref, lng, lnb_ref[...])
        vb = vsn.astype(BF16)
        _sgu_mix(vb, wm2_ref, bsx_ref[...], s_scr)
        dys = dys_ref[...]
        du = dys * s_scr[...]
        ds = dys * u
        dsb = ds.astype(BF16)
        first = lax.broadcasted_iota(jnp.int32, (1, 128), 1) < 64
        acc = jnp.zeros((SGU_BLOCK, D_SGU), F32)
        for n in range(TM // SGU_BLOCK):
            rows = slice(n * 128, (n + 1) * 128)
            acc = acc + ds[rows]
            for p in range(4):
                lanes = slice(p * 128, (p + 1) * 128)
                stack = _group_stack(dsb[rows, lanes], first)
                dv_scr[rows, lanes] = _nn(wmt2_ref[p], stack)
                dw_ref[p] += _nt(stack, vb[rows, lanes])
        ds_acc[...] += acc
        dvsn = dv_scr[...]
        dlg_ref[...] += jnp.sum(dvsn * xhat, axis=0, keepdims=True)
        dlb_ref[...] += jnp.sum(dvsn, axis=0, keepdims=True)
        dxh = dvsn * lng
        dvs = rstd * (dxh - jnp.mean(dxh, axis=-1, keepdims=True) - xhat * jnp.mean(dxh * xhat, axis=-1, keepdims=True))
        dga = jnp.concatenate([du, dvs], axis=1)
        dzs_ref[...] = (dga * _gelu_grad(zs, th)).astype(BF16)

        @pl.when(i == n_steps - 1)
        def _():
            r = lax.broadcasted_iota(jnp.int32, (256, 128), 0) % SGU_BLOCK
            c = lax.broadcasted_iota(jnp.int32, (256, 128), 1)
            keep = (r // CHUNK) >= (c // CHUNK)
            for p in range(4):
                dw_ref[p] = jnp.where(keep, dw_ref[p], 0.0)
            total = ds_acc[...]
            grp = lax.broadcasted_iota(jnp.int32, (SGU_BLOCK, D_SGU), 1) // 64
            lane = lax.broadcasted_iota(jnp.int32, (SGU_BLOCK, 128), 1)
            out = jnp.zeros((SGU_BLOCK, 128), F32)
            for gi in range(8):
                out = jnp.where(lane == gi, jnp.sum(jnp.where(grp == gi, total, 0.0), axis=1, keepdims=True), out)
            dbs_ref[...] = out

    def tile(w):
        return pl.BlockSpec((TM, w), lambda i: (i, 0))

    def const(shape):
        return pl.BlockSpec(shape, lambda i: (0,) * len(shape))

    return _call(
        body, name="sgu_bwd", grid=(n_steps,),
        in_specs=[tile(2 * D_SGU), tile(D_SGU), const((1, D_SGU)), const((1, D_SGU)), const((4, 128, 256)), const((4, 128, 256)),
                  const((128, D_SGU))],
        out_specs=[tile(2 * D_SGU), const((4, 256, 128)), const((128, 128)), const((1, D_SGU)), const((1, D_SGU))],
        out_shape=[jax.ShapeDtypeStruct((t, 2 * D_SGU), BF16), jax.ShapeDtypeStruct((4, 256, 128), F32),
                   jax.ShapeDtypeStruct((128, 128), F32), jax.ShapeDtypeStruct((1, D_SGU), F32), jax.ShapeDtypeStruct((1, D_SGU), F32)],
        scratch_shapes=[pltpu.VMEM((TM, D_SGU), F32), pltpu.VMEM((TM, D_SGU), F32), pltpu.VMEM((SGU_BLOCK, D_SGU), F32)],
        args=(zs, dys, lng, lnb, wm2, wmt2, bsx), comms=comms)


def _adamw(g, w, m, v):
    m = ADAM_B1 * m + (1.0 - ADAM_B1) * g
    v = ADAM_B2 * v + (1.0 - ADAM_B2) * (g * g)
    m_hat = m / (1.0 - ADAM_B1 ** ADAM_STEP)
    v_hat = v / (1.0 - ADAM_B2 ** ADAM_STEP)
    return -ADAM_LR * (m_hat / (jnp.sqrt(v_hat) + ADAM_EPS) + ADAM_WD * w), m, v


def _adamw_matrix(parts, sums, w, m, v, transposed, name):
    _, r, c = parts.shape
    tc = 256

    def body(own_ref, p_ref, s_ref, w_ref, m_ref, v_ref, g_ref, d_ref, mo_ref, vo_ref):
        g = p_ref[0].astype(F32) + p_ref[1].astype(F32) + p_ref[2].astype(F32) + s_ref[...].astype(F32)
        g = g.T if transposed else g
        g_ref[...] = g
        d_ref[...], mo_ref[...], vo_ref[...] = _adamw(g, w_ref[...], m_ref[...], v_ref[...])

    own = pl.BlockSpec((None, tc, r), lambda i, o: (0, i, 0)) if transposed else pl.BlockSpec((None, r, tc), lambda i, o: (0, 0, i))
    return pl.pallas_call(
        body, name=name,
        grid_spec=pltpu.PrefetchScalarGridSpec(
            num_scalar_prefetch=1, grid=(c // tc,),
            in_specs=[pl.BlockSpec((3, r, tc), lambda i, o: (0, 0, i)), pl.BlockSpec((None, r, tc), lambda i, o: (o[0], 0, i)), own, own, own],
            out_specs=[own] * 4),
        out_shape=[jax.ShapeDtypeStruct(w.shape, F32)] * 4,
        compiler_params=_cparams(("arbitrary",)),
    )(_my_index(("x", 2), ("y", 1)), parts, sums, w, m, v)


def _sum_adamw_small(parts, late, w, m, v):
    def body(p_ref, l_ref, w_ref, m_ref, v_ref, *outs):
        g = p_ref[0]
        top = l_ref[0]
        for k in range(1, N_DEV):
            g = g + p_ref[k]
            top = top + l_ref[k]
        for rows, gr in ((slice(0, 8), g[0:8] + top), (slice(8, None), g[8:])):
            res = (gr,) + _adamw(gr, w_ref[rows, :], m_ref[rows, :], v_ref[rows, :])
            for o_ref, val in zip(outs, res):
                o_ref[rows, :] = val

    return pl.pallas_call(body, name="adamw_small", out_shape=[jax.ShapeDtypeStruct(w.shape, F32)] * 4,
                          compiler_params=_cparams())(parts, late, w, m, v)


def _pack_rows(groups, name):
    flat = [a for grp in groups for a, _ in grp]
    rows = [grp[0][0].shape[2] if grp[0][1] else grp[0][0].shape[1] for grp in groups]

    def body(*refs):
        o_ref, pos, off = refs[-1], 0, 0
        for grp, r in zip(groups, rows):
            vals = []
            for _, transposed in grp:
                val = refs[pos][0]
                vals.append(val.T if transposed else val)
                pos += 1
            o_ref[off:off + r, :] = (vals[0] if len(vals) == 1 else jnp.concatenate(vals, axis=1)).astype(BF16)
            off += r

    return pl.pallas_call(body, name=name, out_shape=jax.ShapeDtypeStruct((sum(rows), D), BF16), compiler_params=_cparams())(*flat)


def _pad_row(a):
    a = a.reshape(1, -1)
    return jnp.pad(a, ((0, 0), (0, D - a.shape[1])))


def _pack_small(loss, n1, nm, n2, nf, bg, lng, lnb, bs, rel, ws):
    rows = [_pad_row(loss), n1.reshape(1, D), nm.reshape(1, D), n2.reshape(1, D), nf.reshape(1, D), bg.reshape(2, D),
            jnp.concatenate([lng.reshape(1, D_SGU), lnb.reshape(1, D_SGU)], axis=1), bs.reshape(1, D),
            jnp.pad(rel.reshape(HEADS, N_REL), ((0, 0), (0, D - N_REL))), jnp.zeros((ROW_WS - ROW_REL - HEADS, D), F32),
            ws.reshape(128, D)]
    return jnp.concatenate(rows, axis=0).astype(F32)


def _unpack_small(s):
    return {"loss": s[ROW_LOSS, 0], "norm_ffn1": s[ROW_N1][None], "norm_mix": s[ROW_NM][None], "norm_ffn2": s[ROW_N2][None],
            "norm_final": s[ROW_NF], "b_gate": s[ROW_BG:ROW_BG + 2].reshape(1, 2 * D),
            "sgu_ln_g": s[ROW_LN, :D_SGU][None], "sgu_ln_b": s[ROW_LN, D_SGU:][None], "sgu_b_s": s[ROW_BS].reshape(1, 8, 128),
            "rel_bias": s[ROW_REL:ROW_REL + HEADS, :N_REL][None], "sgu_w_s": s[ROW_WS:].reshape(1, 8, 128, 128)}


_SMALL = ("norm_ffn1", "norm_mix", "norm_ffn2", "norm_final", "b_gate", "sgu_ln_g", "sgu_ln_b", "sgu_b_s", "rel_bias", "sgu_w_s")


def _pack_small_params(p, prefix=""):
    g = lambda n: p[prefix + n]
    return _pack_small(jnp.zeros((1,), F32), g("norm_ffn1"), g("norm_mix"), g("norm_ffn2"), g("norm_final"), g("b_gate"),
                       g("sgu_ln_g"), g("sgu_ln_b"), g("sgu_b_s"), g("rel_bias"), g("sgu_w_s"))


def _step(x, target, p):
    n1, nm, n2 = p["norm_ffn1"], p["norm_mix"], p["norm_ffn2"]
    nf = p["norm_final"].reshape(1, D)
    lng, lnb = p["sgu_ln_g"], p["sgu_ln_b"]
    w_m = jnp.where(jnp.asarray(_sgu_mask())[None], p["sgu_w_s"][0], 0.0).astype(BF16)
    wm2 = jnp.concatenate([w_m[0::2], w_m[1::2]], axis=2)
    w_mt = w_m.transpose(0, 2, 1)
    wmt2 = jnp.concatenate([w_mt[0::2], w_mt[1::2]], axis=2)
    bsx = jnp.repeat(p["sgu_b_s"][0].T, 64, axis=1)
    bias = _band_bias(p["rel_bias"][0])

    def chip_sums(grads, name):
        gots = _comm_only([_SiblingSwap(grads)], "swap_" + name)
        return _pair_sums(grads, gots, "pair_sums_" + name)

    def as_rows(a):
        return jnp.swapaxes(a, 1, 2)

    def updates(parts, sums, names):
        res = {}
        for pt, sm, n in zip(parts, sums, names):
            if p[n].shape[1:] == pt.shape[1:]:
                res[n] = _adamw_matrix(pt, sm, p[n], p["m_" + n], p["v_" + n], False, "adamw_" + n)
            elif p[n].shape[2] > 128:
                res[n] = [as_rows(o) for o in _adamw_matrix(pt, sm, as_rows(p[n]), as_rows(p["m_" + n]), as_rows(p["v_" + n]), False,
                                                            "adamw_" + n)]
            else:
                res[n] = _adamw_matrix(pt, sm, p[n], p["m_" + n], p["v_" + n], True, "adamw_" + n)
        return res

    rows1 = _pack_rows([[(as_rows(p["ffn1_w_gate"]), False)], [(as_rows(p["ffn1_w_up"]), False)], [(p["ffn1_w_down"], False)]], "pack_ffn1")
    rows_m = _pack_rows([[(as_rows(p["w_in"]), False)], [(p["w_branch_att"], True), (p["w_branch_sgu"], True)], [(p["w_out"], False)]],
                        "pack_mixer")
    rows2d = _pack_rows([[(p["ffn2_w_down"], False)]], "pack_ffn2_down")
    rows2gu = _pack_rows([[(as_rows(p["ffn2_w_gate"]), False)], [(as_rows(p["ffn2_w_up"]), False)]], "pack_ffn2_gate_up")
    (gw1,) = _comm_only([_Gather(rows1)], "gather_ffn1")
    x1, ab1, (gwm,) = _ffn_fwd(x, n1, gw1, gw1, 2 * R_FF, "ffn1_fwd", [_Gather(rows_m)])
    (q, k, v, zs, gt, h2), (gw2d,) = _mix_proj_fwd(x1, nm, p["b_gate"], gwm, [_Gather(rows2d)])
    y_att, (gw2gu,) = _att_fwd(q, k, v, bias, [_Gather(rows2gu)])
    x2, y_sgu, merged = _merge_fwd(x1, zs, gt, y_att, lng, lnb, wm2, bsx, gwm)
    dx3, ab2, _, d_nf, loss = _ffn_fwd(x2, n2, gw2gu, gw2d, 0, "ffn2_fwd", head=(nf, target))

    (dx2, d_n2, hb, dab, sb, dfb), _ = _ffn_bwd(x2, n2, ab2, dx3, gw2gu, gw2d, 0, "ffn2_bwd")
    g2 = [_weight_grad(dab, hb, "ffn2_dw_gate", 0, F), _weight_grad(dab, hb, "ffn2_dw_up", F, F), _weight_grad(sb, dfb, "ffn2_dw_down")]
    dzg, dya, dys, dpp, dxb, d_bg = _merge_bwd(dx2, gt, y_att, y_sgu, gwm)
    g_late = g2 + [_weight_grad(dpp, y_att, "dw_branch_att", 0, D), _weight_grad(dpp, y_sgu, "dw_branch_sgu", D, D),
                   _weight_grad(merged, dxb, "dw_out")]
    late = ("ffn2_w_gate", "ffn2_w_up", "ffn2_w_down", "w_branch_att", "w_branch_sgu", "w_out")
    (dzs, d_wm, d_bs, d_lng, d_lnb), gots_late = _sgu_bwd(zs, dys, lng, lnb, wm2, wmt2, bsx, [_SiblingSwap(g_late)])
    sums_late = _pair_sums(g_late, gots_late, "pair_sums_late")
    (dq, dk, dv, d_bias), parts_late = _att_bwd(q, k, v, bias, dya, [_ChipScatter(sums_late)])
    big = updates(parts_late, sums_late, late)
    d_rel = _rel_bias_grad(d_bias)
    dz = jnp.concatenate([dq, dk, dv, dzs, dzg], axis=1)
    small = _pack_small(loss[0, :1], jnp.zeros((1, D), F32), jnp.zeros((1, D), F32), d_n2, d_nf, d_bg, d_lng, d_lnb, d_bs[:, :8].T, d_rel,
                        d_wm.reshape(8, 128, 128))
    g_in, (small_parts,) = _weight_grad(dz, h2, "dw_in", comms=[_AllToAll(small)])
    sums_in = chip_sums([g_in], "w_in")
    (dx1, d_nm), parts_in = _mix_proj_bwd(dz, x1, nm, dx2, gwm, [_ChipScatter(sums_in)])
    big.update(updates(parts_in, sums_in, ("w_in",)))

    (dx0, d_n1, hb, dab, sb, dfb), _ = _ffn_bwd(x, n1, ab1, dx1, gw1, gw1, 2 * R_FF, "ffn1_bwd")
    small_late = jnp.zeros((8, D), F32).at[ROW_N1].set(d_n1[0]).at[ROW_NM].set(d_nm[0])
    sums_d = chip_sums([_weight_grad(sb, dfb, "ffn1_dw_down")], "ffn1_down")
    g_gate, parts_d = _weight_grad(dab, hb, "ffn1_dw_gate", 0, F, [_ChipScatter(sums_d)])
    sums_g = chip_sums([g_gate], "ffn1_gate")
    g_up, (parts_g, late_parts) = _weight_grad(dab, hb, "ffn1_dw_up", F, F, [_ChipScatter(sums_g), _AllToAll(small_late)])
    sums_u = chip_sums([g_up], "ffn1_up")
    parts_u = _comm_only([_ChipScatter(sums_u)], "scatter_ffn1_up")
    big.update(updates(parts_d + [parts_g] + parts_u, sums_d + sums_g + sums_u, ("ffn1_w_down", "ffn1_w_gate", "ffn1_w_up")))
    out_s = _sum_adamw_small(small_parts, late_parts, _pack_small_params(p), _pack_small_params(p, "m_"), _pack_small_params(p, "v_"))
    return dx0, [{**{n: four[i] for n, four in big.items()}, **_unpack_small(s)} for i, s in enumerate(out_s)]


_OUT_ORDER = ("norm_ffn1", "ffn1_w_gate", "ffn1_w_up", "ffn1_w_down", "norm_mix", "w_in", "b_gate", "rel_bias", "sgu_ln_g", "sgu_ln_b",
              "sgu_w_s", "sgu_b_s", "w_branch_att", "w_branch_sgu", "w_out", "norm_ffn2", "ffn2_w_gate", "ffn2_w_up", "ffn2_w_down",
              "norm_final")


def kernel(x, norm_ffn1, ffn1_w_gate, ffn1_w_up, ffn1_w_down, norm_mix, w_in, b_gate, rel_bias, sgu_ln_g, sgu_ln_b, sgu_w_s, sgu_b_s, w_branch_att, w_branch_sgu, w_out, norm_ffn2, ffn2_w_gate, ffn2_w_up, ffn2_w_down, norm_final, loss_target, m_norm_ffn1, m_ffn1_w_gate, m_ffn1_w_up, m_ffn1_w_down, m_norm_mix, m_w_in, m_b_gate, m_rel_bias, m_sgu_ln_g, m_sgu_ln_b, m_sgu_w_s, m_sgu_b_s, m_w_branch_att, m_w_branch_sgu, m_w_out, m_norm_ffn2, m_ffn2_w_gate, m_ffn2_w_up, m_ffn2_w_down, m_norm_final, v_norm_ffn1, v_ffn1_w_gate, v_ffn1_w_up, v_ffn1_w_down, v_norm_mix, v_w_in, v_b_gate, v_rel_bias, v_sgu_ln_g, v_sgu_ln_b, v_sgu_w_s, v_sgu_b_s, v_w_branch_att, v_w_branch_sgu, v_w_out, v_norm_ffn2, v_ffn2_w_gate, v_ffn2_w_up, v_ffn2_w_down, v_norm_final):
    args = dict(locals())
    dx, outs = _step(x[0], loss_target[0], {pre + n: args[pre + n] for pre in ("", "m_", "v_") for n in _OUT_ORDER})
    return (outs[0]["loss"], dx[None], *[o[n] for o in outs for n in _OUT_ORDER])
```

```python
import functools

import numpy as np
import jax
import jax.numpy as jnp
from jax import lax
from jax.experimental import pallas as pl
from jax.experimental.pallas import tpu as pltpu

F32 = jnp.float32
BF16 = jnp.bfloat16

N_DEV = 8
D = 1024
F = 2816
D_ATT = 512
D_SGU = 512
D_IN = 4608
HEADS = 8
CHUNK = 64
N_LEFT = 8
REL_CLIP = 256
N_REL = 2 * REL_CLIP + 1
SGU_BLOCK = 128
EPS = 1e-6
NEG_INF = -1e30
QB = 256
KW = 3 * QB

R_FF, R_IN, R_BR, R_WO = F // N_DEV, D_IN // N_DEV, D // N_DEV, D // N_DEV
OFF_IN, OFF_BR, OFF_WO = 0, R_IN, R_IN + R_BR
FFN_ROWS = 3 * R_FF
MIX_ROWS = R_IN + R_BR + R_WO

FC = 256
TM = 512
VMEM_LIMIT = 56 * 1024 * 1024

ADAM_LR, ADAM_B1, ADAM_B2, ADAM_EPS, ADAM_WD, ADAM_STEP = 0.001, 0.9, 0.999, 1e-08, 0.01, 10

ROW_LOSS, ROW_N1, ROW_NM, ROW_N2, ROW_NF, ROW_BG, ROW_LN, ROW_BS, ROW_REL, ROW_WS = 0, 1, 2, 3, 4, 5, 7, 8, 9, 24
SMALL_ROWS = ROW_WS + 128

MESH = pl.DeviceIdType.MESH
ANY = pl.BlockSpec(memory_space=pl.ANY)


def _nt(a, b):
    return lax.dot_general(a, b, (((1,), (1,)), ((), ())), preferred_element_type=F32)


def _tn(a, b):
    return lax.dot_general(a, b, (((0,), (0,)), ((), ())), preferred_element_type=F32)


def _nn(a, b):
    return jnp.dot(a, b, preferred_element_type=F32)


def _cparams(sem=None):
    return pltpu.CompilerParams(dimension_semantics=sem, vmem_limit_bytes=VMEM_LIMIT)


def _load_rows(gw_ref, dst, off, rows, sems):
    copies = [pltpu.make_async_copy(gw_ref.at[k, pl.ds(off, rows), :], dst.at[pl.ds(k * rows, rows), :], sems.at[k])
              for k in range(N_DEV)]
    for cp in copies:
        cp.start()
    return copies


def _rms(xv):
    r = lax.rsqrt(jnp.mean(xv * xv, axis=-1, keepdims=True) + EPS)
    return xv * r, r


def _rms_bwd(dh, xn, r, gain):
    dxn = dh * gain
    dx = r * (dxn - xn * jnp.mean(dxn * xn, axis=-1, keepdims=True))
    return dx, jnp.sum(dh * xn, axis=0, keepdims=True)


def _gelu(x):
    t = jnp.tanh(0.7978845608028654 * (x + 0.044715 * x * x * x))
    return 0.5 * x * (1.0 + t), t


def _gelu_grad(x, t):
    return 0.5 * (1.0 + t) + 0.5 * x * (1.0 - t * t) * 0.7978845608028654 * (1.0 + 3.0 * 0.044715 * x * x)


def _place():
    x, y, cc = lax.axis_index("x"), lax.axis_index("y"), lax.axis_index("c")
    return x, y, cc, [(1 - x, y), (x, 1 - y), (1 - x, 1 - y)]


class _Gather:
    def __init__(self, shard):
        self.inputs = [shard]
        self.out_shape = [jax.ShapeDtypeStruct((N_DEV,) + shard.shape, shard.dtype)]
        self.scratch = [pltpu.SemaphoreType.DMA((7,)), pltpu.SemaphoreType.DMA((7,)), pltpu.SemaphoreType.DMA]

    def _copies(self, ins, outs, scr):
        (x_ref,), (out_ref,), (send_sems, recv_sems, local_sem) = ins, outs, scr
        x, y, cc, chips = _place()

        def slab(px, py, pc):
            return out_ref.at[4 * px + 2 * py + pc]

        def copy(k, block, to, src=None):
            return pltpu.make_async_remote_copy(
                src_ref=slab(*block) if src is None else src, dst_ref=slab(*block),
                send_sem=send_sems.at[k], recv_sem=recv_sems.at[k], device_id=to, device_id_type=MESH)

        me, sibling = (x, y, cc), (x, y, 1 - cc)
        mine = pltpu.make_async_copy(x_ref, slab(*me), local_sem)
        first = [copy(0, me, sibling, src=x_ref)] + [copy(1 + j, me, (*chip, cc), src=x_ref) for j, chip in enumerate(chips)]
        landed = [copy(1 + j, (*chip, cc), me) for j, chip in enumerate(chips)]
        passed = [copy(4 + j, (*chip, cc), sibling) for j, chip in enumerate(chips)]
        from_sibling = [copy(0, sibling, me)] + [copy(4 + j, (*chip, 1 - cc), me) for j, chip in enumerate(chips)]
        return mine, first, landed, passed, from_sibling

    def begin(self, *refs):
        mine, first, _, _, _ = self._copies(*refs)
        mine.start()
        for cp in first:
            cp.start()

    def relay(self, *refs):
        _, _, landed, passed, _ = self._copies(*refs)
        for arrived, onward in zip(landed, passed):
            arrived.wait_recv()
            onward.start()

    def end(self, *refs):
        mine, first, _, passed, from_sibling = self._copies(*refs)
        for cp in from_sibling:
            cp.wait_recv()
        for cp in first + passed:
            cp.wait_send()
        mine.wait()


class _Direct:
    def begin(self, *refs):
        keep, give = self._copies(*refs)
        for cp in keep + give:
            cp.start()

    def relay(self, *refs):
        pass

    def end(self, *refs):
        keep, give = self._copies(*refs)
        for cp in give:
            cp.wait_recv()
        for cp in give:
            cp.wait_send()
        for cp in keep:
            cp.wait()


class _SiblingSwap(_Direct):
    def __init__(self, grads):
        n = len(grads)
        self.inputs = list(grads)
        self.out_shape = [jax.ShapeDtypeStruct((4,) + g.shape[1:], g.dtype) for g in grads]
        self.scratch = [pltpu.SemaphoreType.DMA((n, 4)), pltpu.SemaphoreType.DMA((n, 4))]

    def _copies(self, ins, outs, scr):
        send_sems, recv_sems = scr
        x, y, cc, _ = _place()
        return [], [pltpu.make_async_remote_copy(src_ref=g_ref.at[2 * j + 1 - cc], dst_ref=got_ref.at[j], send_sem=send_sems.at[i, j],
                                                 recv_sem=recv_sems.at[i, j], device_id=(x, y, 1 - cc), device_id_type=MESH)
                    for i, (g_ref, got_ref) in enumerate(zip(ins, outs)) for j in range(4)]


class _ChipScatter(_Direct):
    def __init__(self, sums):
        n = len(sums)
        self.inputs = list(sums)
        self.out_shape = [jax.ShapeDtypeStruct((3,) + s.shape[1:], s.dtype) for s in sums]
        self.scratch = [pltpu.SemaphoreType.DMA((n, 3)), pltpu.SemaphoreType.DMA((n, 3))]

    def _copies(self, ins, outs, scr):
        send_sems, recv_sems = scr
        _, _, cc, chips = _place()
        return [], [pltpu.make_async_remote_copy(src_ref=s_ref.at[2 * px + py], dst_ref=got_ref.at[j], send_sem=send_sems.at[i, j],
                                                 recv_sem=recv_sems.at[i, j], device_id=(px, py, cc), device_id_type=MESH)
                    for i, (s_ref, got_ref) in enumerate(zip(ins, outs)) for j, (px, py) in enumerate(chips)]


class _AllToAll(_Direct):
    def __init__(self, block):
        self.inputs = [block]
        self.out_shape = [jax.ShapeDtypeStruct((N_DEV,) + block.shape, block.dtype)]
        self.scratch = [pltpu.SemaphoreType.DMA((7,)), pltpu.SemaphoreType.DMA((7,)), pltpu.SemaphoreType.DMA]

    def _copies(self, ins, outs, scr):
        (b_ref,), (got_ref,), (send_sems, recv_sems, local_sem) = ins, outs, scr
        x, y, cc, _ = _place()
        me = 4 * x + 2 * y + cc
        keep = [pltpu.make_async_copy(b_ref, got_ref.at[me], local_sem)]
        give = [pltpu.make_async_remote_copy(src_ref=b_ref, dst_ref=got_ref.at[me], send_sem=send_sems.at[k - 1], recv_sem=recv_sems.at[k - 1],
                                             device_id=(x ^ ((k >> 2) & 1), y ^ ((k >> 1) & 1), cc ^ (k & 1)), device_id_type=MESH)
                for k in range(1, N_DEV)]
        return keep, give


def _split_refs(refs, counts):
    out, pos = [], 0
    for n in counts:
        out.append(list(refs[pos:pos + n]))
        pos += n
    return out


def _bind(comms, c_in, c_out, c_scr):
    ins = _split_refs(c_in, [len(c.inputs) for c in comms])
    outs = _split_refs(c_out, [len(c.out_shape) for c in comms])
    scr = _split_refs(c_scr, [len(c.scratch) for c in comms])
    return [(c, (i, o, s)) for c, i, o, s in zip(comms, ins, outs, scr)]


def _call(body, *, name, grid, in_specs, out_specs, out_shape, scratch_shapes, args, comms=()):
    c_in = [a for c in comms for a in c.inputs]
    c_out = [s for c in comms for s in c.out_shape]
    c_scr = [s for c in comms for s in c.scratch]
    counts = [len(in_specs), len(c_in), len(out_shape), len(c_out), len(scratch_shapes), len(c_scr)]

    def full(*refs):
        ins, cin, outs, cout, scr, cscr = _split_refs(refs, counts)
        bound = _bind(comms, cin, cout, cscr)
        if comms:
            first = functools.reduce(jnp.logical_and, [pl.program_id(ax) == 0 for ax in range(len(grid))])
            last = functools.reduce(jnp.logical_and, [pl.program_id(ax) == n - 1 for ax, n in enumerate(grid)])

            @pl.when(first)
            def _():
                for c, r in bound:
                    c.begin(*r)

            @pl.when(last)
            def _():
                for c, r in bound:
                    c.relay(*r)

        body(*ins, *outs, *scr)
        if comms:
            @pl.when(last)
            def _():
                for c, r in bound:
                    c.end(*r)

    res = pl.pallas_call(
        full, name=name, grid=grid,
        in_specs=list(in_specs) + [ANY] * len(c_in), out_specs=list(out_specs) + [ANY] * len(c_out),
        out_shape=list(out_shape) + c_out, scratch_shapes=list(scratch_shapes) + c_scr,
        compiler_params=_cparams(("arbitrary",) * len(grid)),
    )(*args, *c_in)
    return list(res[:len(out_shape)]), list(res[len(out_shape):])


def _comm_only(comms, name):
    c_in = [a for c in comms for a in c.inputs]
    c_out = [s for c in comms for s in c.out_shape]
    c_scr = [s for c in comms for s in c.scratch]

    def full(*refs):
        cin, cout, cscr = _split_refs(refs, [len(c_in), len(c_out), len(c_scr)])
        bound = _bind(comms, cin, cout, cscr)
        for phase in ("begin", "relay", "end"):
            for c, r in bound:
                getattr(c, phase)(*r)

    return list(pl.pallas_call(full, name=name, in_specs=[ANY] * len(c_in), out_specs=[ANY] * len(c_out), out_shape=c_out,
                               scratch_shapes=c_scr)(*c_in))


def _my_index(*axes_and_weights):
    return sum(w * lax.axis_index(a) for a, w in axes_and_weights).astype(jnp.int32).reshape(1)


def _pair_sums(grads, gots, name):
    n = len(grads)

    def body(c_ref, *refs):
        for a_ref, b_ref, o_ref in zip(refs[:n], refs[n:2 * n], refs[2 * n:]):
            o_ref[...] = (a_ref[...].astype(F32) + b_ref[...].astype(F32)).astype(BF16)

    def tile(g):
        return pl.BlockSpec((1,) + g.shape[1:], lambda j, c_ref: (j, 0, 0))

    def mine(g):
        return pl.BlockSpec((1, None) + g.shape[1:], lambda j, c_ref: (j, c_ref[0], 0, 0))

    return list(pl.pallas_call(
        body, name=name,
        grid_spec=pltpu.PrefetchScalarGridSpec(num_scalar_prefetch=1, grid=(4,), in_specs=[mine(g) for g in gots] + [tile(g) for g in gots],
                                               out_specs=[tile(g) for g in gots]),
        out_shape=[jax.ShapeDtypeStruct(g.shape, BF16) for g in gots],
        compiler_params=_cparams(("arbitrary",)))(_my_index(("c", 1)), *[g.reshape((4, 2) + g.shape[1:]) for g in grads], *gots))


def _ffn_fwd(x, gain, gw_gu, gw_d, off_d, name, comms=(), head=None):
    t = x.shape[0]
    n_head = 0 if head is None else 2

    def body(x_ref, g_ref, gu_ref, d_ref, *refs):
        head_in, (o_ref, ab_ref), head_out = refs[:n_head], refs[n_head:n_head + 2], refs[n_head + 2:2 * n_head + 2]
        wg, wu, wd, s_scr, sems = refs[2 * n_head + 2:]

        @pl.when(pl.program_id(0) == 0)
        def _():
            cps = _load_rows(gu_ref, wg, 0, R_FF, sems.at[0]) + _load_rows(gu_ref, wu, R_FF, R_FF, sems.at[1]) \
                + _load_rows(d_ref, wd, off_d, R_FF, sems.at[2])
            for o in head_out:
                o[...] = jnp.zeros_like(o)
            for cp in cps:
                cp.wait()

        xv = x_ref[...]
        xn, _ = _rms(xv)
        h = (xn * g_ref[...]).astype(BF16)
        for c in range(F // FC):
            rows = pl.ds(c * FC, FC)
            a = _nt(h, wg[rows, :])
            b = _nt(h, wu[rows, :])
            ab_ref[:, c * FC:(c + 1) * FC] = a.astype(BF16)
            ab_ref[:, F + c * FC:F + (c + 1) * FC] = b.astype(BF16)
            s_scr[:, c * FC:(c + 1) * FC] = (a * jax.nn.sigmoid(a) * b).astype(BF16)
        out = xv + 0.5 * _nn(s_scr[...], wd[...])
        if head is None:
            o_ref[...] = out
        else:
            (gf_ref, t_ref), (dg_ref, loss_ref) = head_in, head_out
            gain_f = gf_ref[...]
            yn, r = _rms(out)
            err = yn * gain_f - t_ref[...]
            loss_ref[...] += 0.5 * jnp.sum(jnp.mean(err * err, axis=-1, keepdims=True), axis=0, keepdims=True)
            o_ref[...], dg = _rms_bwd(err * (1.0 / D), yn, r, gain_f)
            dg_ref[...] += dg

    tile = pl.BlockSpec((TM, D), lambda i: (i, 0))
    row = pl.BlockSpec((1, D), lambda i: (0, 0))
    head_specs = [] if head is None else [row, pl.BlockSpec((1, 128), lambda i: (0, 0))]
    head_shapes = [] if head is None else [jax.ShapeDtypeStruct((1, D), F32), jax.ShapeDtypeStruct((1, 128), F32)]
    res, got = _call(
        body, name=name, grid=(t // TM,),
        in_specs=[tile, row, ANY, ANY] + ([] if head is None else [row, tile]),
        out_specs=[tile, pl.BlockSpec((TM, 2 * F), lambda i: (i, 0))] + head_specs,
        out_shape=[jax.ShapeDtypeStruct((t, D), F32), jax.ShapeDtypeStruct((t, 2 * F), BF16)] + head_shapes,
        scratch_shapes=[pltpu.VMEM((F, D), BF16)] * 3 + [pltpu.VMEM((TM, F), BF16), pltpu.SemaphoreType.DMA((3, N_DEV))],
        args=(x, gain, gw_gu, gw_d) + (() if head is None else tuple(head)), comms=comms)
    return (res[0], res[1], got, *res[2:])


def _ffn_bwd(x, gain, ab, dout, gw_gu, gw_d, off_d, name, comms=()):
    t = x.shape[0]

    def tile(w):
        return pl.BlockSpec((TM, w), lambda i: (i, 0))

    def hidden(ab_ref, do_ref, d_ref, dab_ref, s_ref, df_ref, wd, sems):
        @pl.when(pl.program_id(0) == 0)
        def _():
            for cp in _load_rows(d_ref, wd, off_d, R_FF, sems):
                cp.wait()

        df = (0.5 * do_ref[...]).astype(BF16)
        df_ref[...] = df
        for c in range(F // FC):
            a = ab_ref[:, c * FC:(c + 1) * FC].astype(F32)
            b = ab_ref[:, F + c * FC:F + (c + 1) * FC].astype(F32)
            sg = jax.nn.sigmoid(a)
            sl = a * sg
            ds = _nt(df, wd[pl.ds(c * FC, FC), :])
            dab_ref[:, c * FC:(c + 1) * FC] = (ds * b * (sg * (1.0 + a * (1.0 - sg)))).astype(BF16)
            dab_ref[:, F + c * FC:F + (c + 1) * FC] = (ds * sl).astype(BF16)
            s_ref[:, c * FC:(c + 1) * FC] = (sl * b).astype(BF16)

    (dab, s, df), _ = _call(
        hidden, name=name + "_hidden", grid=(t // TM,),
        in_specs=[tile(2 * F), tile(D), ANY], out_specs=[tile(2 * F), tile(F), tile(D)],
        out_shape=[jax.ShapeDtypeStruct((t, 2 * F), BF16), jax.ShapeDtypeStruct((t, F), BF16), jax.ShapeDtypeStruct((t, D), BF16)],
        scratch_shapes=[pltpu.VMEM((F, D), BF16), pltpu.SemaphoreType.DMA((N_DEV,))],
        args=(ab, dout, gw_d))

    def body(x_ref, g_ref, dab_ref, do_ref, gu_ref, dx_ref, dg_ref, h_ref, wg, wu, sems):
        @pl.when(pl.program_id(0) == 0)
        def _():
            cps = _load_rows(gu_ref, wg, 0, R_FF, sems.at[0]) + _load_rows(gu_ref, wu, R_FF, R_FF, sems.at[1])
            dg_ref[...] = jnp.zeros_like(dg_ref)
            for cp in cps:
                cp.wait()

        gain_v = g_ref[...]
        xn, r = _rms(x_ref[...])
        h_ref[...] = (xn * gain_v).astype(BF16)
        dh = _nn(dab_ref[:, :F], wg[...]) + _nn(dab_ref[:, F:], wu[...])
        dxn, dg = _rms_bwd(dh, xn, r, gain_v)
        dg_ref[...] += dg
        dx_ref[...] = do_ref[...] + dxn

    row = pl.BlockSpec((1, D), lambda i: (0, 0))
    (dx, dg, h), got = _call(
        body, name=name, grid=(t // TM,),
        in_specs=[tile(D), row, tile(2 * F), tile(D), ANY], out_specs=[tile(D), row, tile(D)],
        out_shape=[jax.ShapeDtypeStruct((t, D), F32), jax.ShapeDtypeStruct((1, D), F32), jax.ShapeDtypeStruct((t, D), BF16)],
        scratch_shapes=[pltpu.VMEM((F, D), BF16)] * 2 + [pltpu.SemaphoreType.DMA((2, N_DEV))],
        args=(x, gain, dab, dout, gw_gu), comms=comms)
    return (dx, dg, h, dab, s, df), got


def _weight_grad(a, b, name, col_off=0, m=None, comms=None, tmm=256):
    t = a.shape[0]
    m = a.shape[1] if m is None else m
    n = b.shape[1]
    first = col_off // tmm

    def body(a_ref, b_ref, o_ref, bt):
        @pl.when(pl.program_id(0) == 0)
        def _():
            bt[...] = b_ref[...].T

        o_ref[...] = _nn(bt[...], a_ref[...]).T.astype(BF16)

    (out,), got = _call(
        body, name=name, grid=(m // tmm,),
        in_specs=[pl.BlockSpec((t, tmm), lambda i: (0, first + i)), pl.BlockSpec((t, n), lambda i: (0, 0))],
        out_specs=[pl.BlockSpec((tmm, n), lambda i: (i, 0))],
        out_shape=[jax.ShapeDtypeStruct((m, n), BF16)], scratch_shapes=[pltpu.VMEM((n, t), BF16)], args=(a, b), comms=comms or ())
    out = out.reshape(N_DEV, m // N_DEV, n)
    return out if comms is None else (out, got)


def _mix_proj_fwd(x, gain, b_gate, gw, comms=()):
    t = x.shape[0]

    def body(x_ref, g_ref, bg_ref, gw_ref, q_ref, k_ref, v_ref, zs_ref, gt_ref, h_ref, win, sems):
        @pl.when(pl.program_id(0) == 0)
        def _():
            for cp in _load_rows(gw_ref, win, OFF_IN, R_IN, sems):
                cp.wait()

        xn, _ = _rms(x_ref[...])
        h = (xn * g_ref[...]).astype(BF16)
        h_ref[...] = h
        q_ref[...] = (_nt(h, win[0:512, :]) * 0.125).astype(BF16)
        k_ref[...] = _nt(h, win[512:1024, :]).astype(BF16)
        v_ref[...] = _nt(h, win[1024:1536, :]).astype(BF16)
        for c in range(2):
            zs_ref[:, c * 512:(c + 1) * 512] = _nt(h, win[1536 + c * 512:2048 + c * 512, :]).astype(BF16)
        for c in range(4):
            zg = _nt(h, win[2560 + c * 512:3072 + c * 512, :]) + bg_ref[:, c * 512:(c + 1) * 512]
            gt_ref[:, c * 512:(c + 1) * 512] = jax.nn.sigmoid(zg).astype(BF16)

    def tile(w):
        return pl.BlockSpec((TM, w), lambda i: (i, 0))

    return _call(
        body, name="mix_proj_fwd", grid=(t // TM,),
        in_specs=[tile(D), pl.BlockSpec((1, D), lambda i: (0, 0)), pl.BlockSpec((1, 2 * D), lambda i: (0, 0)), ANY],
        out_specs=[tile(D_ATT), tile(D_ATT), tile(D_ATT), tile(2 * D_SGU), tile(2 * D), tile(D)],
        out_shape=[jax.ShapeDtypeStruct((t, D_ATT), BF16)] * 3 + [jax.ShapeDtypeStruct((t, 2 * D_SGU), BF16),
                                                                   jax.ShapeDtypeStruct((t, 2 * D), BF16),
                                                                   jax.ShapeDtypeStruct((t, D), BF16)],
        scratch_shapes=[pltpu.VMEM((D_IN, D), BF16), pltpu.SemaphoreType.DMA((N_DEV,))],
        args=(x, gain, b_gate, gw), comms=comms)


def _mix_proj_bwd(dz, x, gain, dres, gw, comms=()):
    t = x.shape[0]

    def body(dz_ref, x_ref, g_ref, dr_ref, gw_ref, dx_ref, dg_ref, win, sems):
        @pl.when(pl.program_id(0) == 0)
        def _():
            cps = _load_rows(gw_ref, win, OFF_IN, R_IN, sems)
            dg_ref[...] = jnp.zeros_like(dg_ref)
            for cp in cps:
                cp.wait()

        dh = _nn(dz_ref[...], win[...])
        xn, r = _rms(x_ref[...])
        dxn, dg = _rms_bwd(dh, xn, r, g_ref[...])
        dg_ref[...] += dg
        dx_ref[...] = dr_ref[...] + dxn

    def tile(w):
        return pl.BlockSpec((TM, w), lambda i: (i, 0))

    row = pl.BlockSpec((1, D), lambda i: (0, 0))
    return _call(
        body, name="mix_proj_bwd", grid=(t // TM,),
        in_specs=[tile(D_IN), tile(D), row, tile(D), ANY], out_specs=[tile(D), row],
        out_shape=[jax.ShapeDtypeStruct((t, D), F32), jax.ShapeDtypeStruct((1, D), F32)],
        scratch_shapes=[pltpu.VMEM((D_IN, D), BF16), pltpu.SemaphoreType.DMA((N_DEV,))],
        args=(dz, x, gain, dres, gw), comms=comms)


SKEW_W = KW + QB
N_CAP = 2 * QB - REL_CLIP + 1


def _band_bias(rel_bias):
    cap = rel_bias[:, 2 * REL_CLIP:]
    diag = jnp.concatenate([jnp.broadcast_to(cap, (HEADS, N_CAP)), rel_bias[:, 2 * REL_CLIP - 1::-1],
                            jnp.broadcast_to(cap, (HEADS, SKEW_W - N_CAP - 2 * REL_CLIP))], axis=1)

    def body(d_ref, o_ref):
        lag = lax.broadcasted_iota(jnp.int32, (QB, KW), 1) // CHUNK - lax.broadcasted_iota(jnp.int32, (QB, KW), 0) // CHUNK
        band = (lag >= 0) & (lag <= N_LEFT)
        for h in range(HEADS):
            rows = jnp.broadcast_to(d_ref[h:h + 1, :], (QB, SKEW_W))
            o_ref[h] = jnp.where(band, pltpu.roll(rows, 0, 1, stride=1, stride_axis=0)[:, :KW], NEG_INF)

    return pl.pallas_call(body, name="band_bias", out_shape=jax.ShapeDtypeStruct((HEADS, QB, KW), F32))(diag)


def _att_specs():
    qspec = pl.BlockSpec((QB, D_ATT), lambda g: (g, 0))
    kspecs = [pl.BlockSpec((QB, D_ATT), lambda g: (jnp.maximum(g - 2, 0), 0)),
              pl.BlockSpec((QB, D_ATT), lambda g: (jnp.maximum(g - 1, 0), 0)), qspec]
    bspec = pl.BlockSpec((HEADS, QB, KW), lambda g: (0, 0, 0))
    return qspec, kspecs, bspec


def _att_probs(qm, kp, bias, valid):
    s = jnp.where(valid, _nt(qm, kp) + bias, NEG_INF)
    e = jnp.exp(s - jnp.max(s, axis=-1, keepdims=True))
    return e / jnp.sum(e, axis=-1, keepdims=True)


def _att_valid():
    g = pl.program_id(0)
    blk = lax.broadcasted_iota(jnp.int32, (QB, KW), 1) // QB
    return (blk + g) >= 2


def _att_fwd(q, k, v, bias, comms=()):
    t = q.shape[0]

    def body(q_ref, k0, k1, k2, v0, v1, v2, b_ref, y_ref):
        valid = _att_valid()
        first = lax.broadcasted_iota(jnp.int32, (1, 128), 1) < 64
        for p in range(HEADS // 2):
            lanes = slice(p * 128, (p + 1) * 128)
            qp = q_ref[:, lanes]
            kp = jnp.concatenate([k0[:, lanes], k1[:, lanes], k2[:, lanes]], axis=0)
            vp = jnp.concatenate([v0[:, lanes], v1[:, lanes], v2[:, lanes]], axis=0)
            out = jnp.zeros((QB, 128), F32)
            for hh in range(2):
                mask = first if hh == 0 else jnp.logical_not(first)
                pr = _att_probs(jnp.where(mask, qp, 0), kp, b_ref[2 * p + hh], valid)
                out = out + _nn(pr.astype(BF16), jnp.where(mask, vp, 0))
            y_ref[:, lanes] = out.astype(BF16)

    qspec, kspecs, bspec = _att_specs()
    (out,), got = _call(
        body, name="att_fwd", grid=(t // QB,),
        in_specs=[qspec] + kspecs + kspecs + [bspec], out_specs=[qspec],
        out_shape=[jax.ShapeDtypeStruct((t, D_ATT), BF16)], scratch_shapes=[],
        args=(q, k, k, k, v, v, v, bias), comms=comms)
    return out, got


def _att_bwd(q, k, v, bias, dy, comms=()):
    t = q.shape[0]
    n_blocks = t // QB

    def body(q_ref, k0, k1, k2, v0, v1, v2, b_ref, dy_ref, dq_ref, dk_ref, dv_ref, db_ref, dk_acc, dv_acc):
        g = pl.program_id(0)

        @pl.when(g == 0)
        def _():
            db_ref[...] = jnp.zeros_like(db_ref)
            dk_acc[...] = jnp.zeros_like(dk_acc)
            dv_acc[...] = jnp.zeros_like(dv_acc)

        valid = _att_valid()
        first = lax.broadcasted_iota(jnp.int32, (1, 128), 1) < 64
        for p in range(HEADS // 2):
            lanes = slice(p * 128, (p + 1) * 128)
            qp = q_ref[:, lanes]
            dyp = dy_ref[:, lanes]
            kp = jnp.concatenate([k0[:, lanes], k1[:, lanes], k2[:, lanes]], axis=0)
            vp = jnp.concatenate([v0[:, lanes], v1[:, lanes], v2[:, lanes]], axis=0)
            dq = jnp.zeros((QB, 128), F32)
            dk = jnp.zeros((KW, 128), F32)
            dv = jnp.zeros((KW, 128), F32)
            for hh in range(2):
                mask = first if hh == 0 else jnp.logical_not(first)
                qm = jnp.where(mask, qp, 0)
                dym = jnp.where(mask, dyp, 0)
                pr = _att_probs(qm, kp, b_ref[2 * p + hh], valid)
                dp = _nt(dym, vp)
                ds = pr * (dp - jnp.sum(dp * pr, axis=-1, keepdims=True))
                db_ref[2 * p + hh] += ds
                dsb = ds.astype(BF16)
                dq = dq + _nn(dsb, jnp.where(mask, kp, 0))
                dk = dk + _tn(dsb, qm)
                dv = dv + _tn(pr.astype(BF16), dym)
            dq_ref[:, lanes] = (dq * 0.125).astype(BF16)
            for j in range(3):
                rows = pl.ds(pl.multiple_of(jnp.maximum(g - 2 + j, 0) * QB, QB), QB)
                dk_acc[rows, lanes] += dk[j * QB:(j + 1) * QB]
                dv_acc[rows, lanes] += dv[j * QB:(j + 1) * QB]

        @pl.when(g == n_blocks - 1)
        def _():
            dk_ref[...] = dk_acc[...].astype(BF16)
            dv_ref[...] = dv_acc[...].astype(BF16)

    qspec, kspecs, bspec = _att_specs()
    full = pl.BlockSpec((t, D_ATT), lambda g: (0, 0))
    return _call(
        body, name="att_bwd", grid=(n_blocks,),
        in_specs=[qspec] + kspecs + kspecs + [bspec, qspec], out_specs=[qspec, full, full, bspec],
        out_shape=[jax.ShapeDtypeStruct((t, D_ATT), BF16)] * 3 + [jax.ShapeDtypeStruct((HEADS, QB, KW), F32)],
        scratch_shapes=[pltpu.VMEM((t, D_ATT), F32)] * 2,
        args=(q, k, k, k, v, v, v, bias, dy), comms=comms)


def _rel_bias_grad(dbias):
    def body(db_ref, cs_ref, tot_ref):
        lane = lax.broadcasted_iota(jnp.int32, (1, SKEW_W), 1)
        capped = (lane < N_CAP) | (lane > KW)
        pad = jnp.zeros((8, QB), F32)
        for h in range(HEADS):
            z = jnp.concatenate([db_ref[h, 0:8, :], pad], axis=1)
            for a in range(1, QB // 8):
                z = z + pltpu.roll(jnp.concatenate([db_ref[h, 8 * a:8 * a + 8, :], pad], axis=1), SKEW_W - 8 * a, 1)
            cs = z[0:1, :]
            for b in range(1, 8):
                cs = cs + pltpu.roll(z[b:b + 1, :], SKEW_W - b, 1)
            cs_ref[h:h + 1, :] = cs
            tot_ref[h:h + 1, :] = jnp.broadcast_to(jnp.sum(jnp.where(capped, cs, 0.0), axis=1, keepdims=True), (1, 128))

    cs, tot = pl.pallas_call(
        body, name="rel_bias_grad",
        out_shape=[jax.ShapeDtypeStruct((HEADS, SKEW_W), F32), jax.ShapeDtypeStruct((HEADS, 128), F32)],
    )(dbias)
    return jnp.concatenate([cs[:, KW:N_CAP - 1:-1], tot[:, :1]], axis=1)


def _sgu_mask():
    pos = np.arange(SGU_BLOCK)
    return (pos[:, None] // CHUNK) >= (pos[None, :] // CHUNK)


def _group_stack(blk, first):
    return jnp.concatenate([jnp.where(first, blk, 0), jnp.where(first, 0, blk)], axis=0)


def _sgu_norm(zs_ref, lng, lnb):
    zs = zs_ref[...].astype(F32)
    ga, th = _gelu(zs)
    u, vs = ga[:, :D_SGU], ga[:, D_SGU:]
    mu = jnp.mean(vs, axis=-1, keepdims=True)
    cen = vs - mu
    rstd = lax.rsqrt(jnp.mean(cen * cen, axis=-1, keepdims=True) + EPS)
    xhat = cen * rstd
    return zs, th, u, xhat, rstd, xhat * lng + lnb


def _sgu_mix(vb, wm2_ref, bsx, s_ref):
    first = lax.broadcasted_iota(jnp.int32, (1, 128), 1) < 64
    for n in range(TM // SGU_BLOCK):
        for p in range(4):
            blk = vb[n * 128:(n + 1) * 128, p * 128:(p + 1) * 128]
            s_ref[n * 128:(n + 1) * 128, p * 128:(p + 1) * 128] = _nn(wm2_ref[p], _group_stack(blk, first)) + bsx[:, p * 128:(p + 1) * 128]


def _merge_fwd(x, zs, gt, y_att, lng, lnb, wm2, bsx, gw):
    t = x.shape[0]

    def body(x_ref, zs_ref, gt_ref, ya_ref, lng_ref, lnb_ref, wm2_ref, bsx_ref, gw_ref, xo_ref, ys_ref, mg_ref,
             wbr, wo, s_scr, sems):
        @pl.when(pl.program_id(0) == 0)
        def _():
            for cp in _load_rows(gw_ref, wbr, OFF_BR, R_BR, sems.at[0]) + _load_rows(gw_ref, wo, OFF_WO, R_WO, sems.at[1]):
                cp.wait()

        _, _, u, _, _, vsn = _sgu_norm(zs_ref, lng_ref[...], lnb_ref[...])
        _sgu_mix(vsn.astype(BF16), wm2_ref, bsx_ref[...], s_scr)
        ys = (u * s_scr[...]).astype(BF16)
        ys_ref[...] = ys
        pa = _nt(ya_ref[...], wbr[:, :D_ATT])
        ps = _nt(ys, wbr[:, D_ATT:])
        mg = (gt_ref[:, :D].astype(F32) * pa + gt_ref[:, D:].astype(F32) * ps).astype(BF16)
        mg_ref[...] = mg
        xo_ref[...] = x_ref[...] + _nn(mg, wo[...])

    def tile(w):
        return pl.BlockSpec((TM, w), lambda i: (i, 0))

    def const(shape):
        return pl.BlockSpec(shape, lambda i: (0,) * len(shape))

    return pl.pallas_call(
        body, name="merge_fwd", grid=(t // TM,),
        in_specs=[tile(D), tile(2 * D_SGU), tile(2 * D), tile(D_ATT), const((1, D_SGU)), const((1, D_SGU)),
                  const((4, 128, 256)), const((128, D_SGU)), ANY],
        out_specs=[tile(D), tile(D_SGU), tile(D)],
        out_shape=[jax.ShapeDtypeStruct((t, D), F32), jax.ShapeDtypeStruct((t, D_SGU), BF16), jax.ShapeDtypeStruct((t, D), BF16)],
        scratch_shapes=[pltpu.VMEM((D, D), BF16), pltpu.VMEM((D, D), BF16), pltpu.VMEM((TM, D_SGU), F32),
                        pltpu.SemaphoreType.DMA((2, N_DEV))],
        compiler_params=_cparams(("arbitrary",)),
    )(x, zs, gt, y_att, lng, lnb, wm2, bsx, gw)


def _merge_bwd(dx, gt, y_att, y_sgu, gw):
    t = dx.shape[0]

    def body(dx_ref, gt_ref, ya_ref, ys_ref, gw_ref, dzg_ref, dya_ref, dys_ref, dpp_ref, dxb_ref, dbg_ref, wbr, wo, sems):
        @pl.when(pl.program_id(0) == 0)
        def _():
            cps = _load_rows(gw_ref, wbr, OFF_BR, R_BR, sems.at[0]) + _load_rows(gw_ref, wo, OFF_WO, R_WO, sems.at[1])
            dbg_ref[...] = jnp.zeros_like(dbg_ref)
            for cp in cps:
                cp.wait()

        dxb = dx_ref[...].astype(BF16)
        dxb_ref[...] = dxb
        dm = _nt(dxb, wo[...])
        for half, y_ref, w in ((0, ya_ref, wbr.at[:, :D_ATT]), (1, ys_ref, wbr.at[:, D_ATT:])):
            cols = slice(half * D, (half + 1) * D)
            gate = gt_ref[:, cols].astype(F32)
            branch = _nt(y_ref[...], w[...])
            dzg = dm * branch * gate * (1.0 - gate)
            dbg_ref[:, cols] += jnp.sum(dzg, axis=0, keepdims=True)
            dzg_ref[:, cols] = dzg.astype(BF16)
            dbr = (dm * gate).astype(BF16)
            dpp_ref[:, cols] = dbr
            dy = _nn(dbr, w[...])
            if half == 0:
                dya_ref[...] = dy.astype(BF16)
            else:
                dys_ref[...] = dy

    def tile(w):
        return pl.BlockSpec((TM, w), lambda i: (i, 0))

    return pl.pallas_call(
        body, name="merge_bwd", grid=(t // TM,),
        in_specs=[tile(D), tile(2 * D), tile(D_ATT), tile(D_SGU), ANY],
        out_specs=[tile(2 * D), tile(D_ATT), tile(D_SGU), tile(2 * D), tile(D), pl.BlockSpec((1, 2 * D), lambda i: (0, 0))],
        out_shape=[jax.ShapeDtypeStruct((t, 2 * D), BF16), jax.ShapeDtypeStruct((t, D_ATT), BF16), jax.ShapeDtypeStruct((t, D_SGU), F32),
                   jax.ShapeDtypeStruct((t, 2 * D), BF16), jax.ShapeDtypeStruct((t, D), BF16), jax.ShapeDtypeStruct((1, 2 * D), F32)],
        scratch_shapes=[pltpu.VMEM((D, D), BF16), pltpu.VMEM((D, D), BF16), pltpu.SemaphoreType.DMA((2, N_DEV))],
        compiler_params=_cparams(("arbitrary",)),
    )(dx, gt, y_att, y_sgu, gw)


def _sgu_bwd(zs, dys, lng, lnb, wm2, wmt2, bsx, comms=()):
    t = zs.shape[0]
    n_steps = t // TM

    def body(zs_ref, dys_ref, lng_ref, lnb_ref, wm2_ref, wmt2_ref, bsx_ref, dzs_ref, dw_ref, dbs_ref, dlg_ref, dlb_ref,
             s_scr, dv_scr, ds_acc):
        i = pl.program_id(0)

        @pl.when(i == 0)
        def _():
            dw_ref[...] = jnp.zeros_like(dw_ref)
            dlg_ref[...] = jnp.zeros_like(dlg_ref)
            dlb_ref[...] = jnp.zeros_like(dlb_ref)
            ds_acc[...] = jnp.zeros_like(ds_acc)

        lng = lng_ref[...]
        zs, th, u, xhat, rstd, vsn = _sgu_norm(zs_ref, lng, lnb_ref[...])
        vb = vsn.astype(BF16)
        _sgu_mix(vb, wm2_ref, bsx_ref[...], s_scr)
        dys = dys_ref[...]
        du = dys * s_scr[...]
        ds = dys * u
        dsb = ds.astype(BF16)
        first = lax.broadcasted_iota(jnp.int32, (1, 128), 1) < 64
        acc = jnp.zeros((SGU_BLOCK, D_SGU), F32)
        for n in range(TM // SGU_BLOCK):
            rows = slice(n * 128, (n + 1) * 128)
            acc = acc + ds[rows]
            for p in range(4):
                lanes = slice(p * 128, (p + 1) * 128)
                stack = _group_stack(dsb[rows, lanes], first)
                dv_scr[rows, lanes] = _nn(wmt2_ref[p], stack)
                dw_ref[p] += _nt(stack, vb[rows, lanes])
        ds_acc[...] += acc
        dvsn = dv_scr[...]
        dlg_ref[...] += jnp.sum(dvsn * xhat, axis=0, keepdims=True)
        dlb_ref[...] += jnp.sum(dvsn, axis=0, keepdims=True)
        dxh = dvsn * lng
        dvs = rstd * (dxh - jnp.mean(dxh, axis=-1, keepdims=True) - xhat * jnp.mean(dxh * xhat, axis=-1, keepdims=True))
        dga = jnp.concatenate([du, dvs], axis=1)
        dzs_ref[...] = (dga * _gelu_grad(zs, th)).astype(BF16)

        @pl.when(i == n_steps - 1)
        def _():
            r = lax.broadcasted_iota(jnp.int32, (256, 128), 0) % SGU_BLOCK
            c = lax.broadcasted_iota(jnp.int32, (256, 128), 1)
            keep = (r // CHUNK) >= (c // CHUNK)
            for p in range(4):
                dw_ref[p] = jnp.where(keep, dw_ref[p], 0.0)
            total = ds_acc[...]
            grp = lax.broadcasted_iota(jnp.int32, (SGU_BLOCK, D_SGU), 1) // 64
            lane = lax.broadcasted_iota(jnp.int32, (SGU_BLOCK, 128), 1)
            out = jnp.zeros((SGU_BLOCK, 128), F32)
            for gi in range(8):
                out = jnp.where(lane == gi, jnp.sum(jnp.where(grp == gi, total, 0.0), axis=1, keepdims=True), out)
            dbs_ref[...] = out

    def tile(w):
        return pl.BlockSpec((TM, w), lambda i: (i, 0))

    def const(shape):
        return pl.BlockSpec(shape, lambda i: (0,) * len(shape))

    return _call(
        body, name="sgu_bwd", grid=(n_steps,),
        in_specs=[tile(2 * D_SGU), tile(D_SGU), const((1, D_SGU)), const((1, D_SGU)), const((4, 128, 256)), const((4, 128, 256)),
                  const((128, D_SGU))],
        out_specs=[tile(2 * D_SGU), const((4, 256, 128)), const((128, 128)), const((1, D_SGU)), const((1, D_SGU))],
        out_shape=[jax.ShapeDtypeStruct((t, 2 * D_SGU), BF16), jax.ShapeDtypeStruct((4, 256, 128), F32),
                   jax.ShapeDtypeStruct((128, 128), F32), jax.ShapeDtypeStruct((1, D_SGU), F32), jax.ShapeDtypeStruct((1, D_SGU), F32)],
        scratch_shapes=[pltpu.VMEM((TM, D_SGU), F32), pltpu.VMEM((TM, D_SGU), F32), pltpu.VMEM((SGU_BLOCK, D_SGU), F32)],
        args=(zs, dys, lng, lnb, wm2, wmt2, bsx), comms=comms)


def _adamw(g, w, m, v):
    m = ADAM_B1 * m + (1.0 - ADAM_B1) * g
    v = ADAM_B2 * v + (1.0 - ADAM_B2) * (g * g)
    m_hat = m / (1.0 - ADAM_B1 ** ADAM_STEP)
    v_hat = v / (1.0 - ADAM_B2 ** ADAM_STEP)
    return -ADAM_LR * (m_hat / (jnp.sqrt(v_hat) + ADAM_EPS) + ADAM_WD * w), m, v


def _adamw_matrix(parts, sums, w, m, v, transposed, name):
    _, r, c = parts.shape
    tc = 256

    def body(own_ref, p_ref, s_ref, w_ref, m_ref, v_ref, g_ref, d_ref, mo_ref, vo_ref):
        g = p_ref[0].astype(F32) + p_ref[1].astype(F32) + p_ref[2].astype(F32) + s_ref[...].astype(F32)
        g = g.T if transposed else g
        g_ref[...] = g
        d_ref[...], mo_ref[...], vo_ref[...] = _adamw(g, w_ref[...], m_ref[...], v_ref[...])

    own = pl.BlockSpec((None, tc, r), lambda i, o: (0, i, 0)) if transposed else pl.BlockSpec((None, r, tc), lambda i, o: (0, 0, i))
    return pl.pallas_call(
        body, name=name,
        grid_spec=pltpu.PrefetchScalarGridSpec(
            num_scalar_prefetch=1, grid=(c // tc,),
            in_specs=[pl.BlockSpec((3, r, tc), lambda i, o: (0, 0, i)), pl.BlockSpec((None, r, tc), lambda i, o: (o[0], 0, i)), own, own, own],
            out_specs=[own] * 4),
        out_shape=[jax.ShapeDtypeStruct(w.shape, F32)] * 4,
        compiler_params=_cparams(("arbitrary",)),
    )(_my_index(("x", 2), ("y", 1)), parts, sums, w, m, v)


def _sum_adamw_small(parts, late, w, m, v):
    def body(p_ref, l_ref, w_ref, m_ref, v_ref, *outs):
        g = p_ref[0]
        top = l_ref[0]
        for k in range(1, N_DEV):
            g = g + p_ref[k]
            top = top + l_ref[k]
        for rows, gr in ((slice(0, 8), g[0:8] + top), (slice(8, None), g[8:])):
            res = (gr,) + _adamw(gr, w_ref[rows, :], m_ref[rows, :], v_ref[rows, :])
            for o_ref, val in zip(outs, res):
                o_ref[rows, :] = val

    return pl.pallas_call(body, name="adamw_small", out_shape=[jax.ShapeDtypeStruct(w.shape, F32)] * 4,
                          compiler_params=_cparams())(parts, late, w, m, v)


def _pack_rows(groups, name):
    flat = [a for grp in groups for a, _ in grp]
    rows = [grp[0][0].shape[2] if grp[0][1] else grp[0][0].shape[1] for grp in groups]

    def body(*refs):
        o_ref, pos, off = refs[-1], 0, 0
        for grp, r in zip(groups, rows):
            vals = []
            for _, transposed in grp:
                val = refs[pos][0]
                vals.append(val.T if transposed else val)
                pos += 1
            o_ref[off:off + r, :] = (vals[0] if len(vals) == 1 else jnp.concatenate(vals, axis=1)).astype(BF16)
            off += r

    return pl.pallas_call(body, name=name, out_shape=jax.ShapeDtypeStruct((sum(rows), D), BF16), compiler_params=_cparams())(*flat)


def _pad_row(a):
    a = a.reshape(1, -1)
    return jnp.pad(a, ((0, 0), (0, D - a.shape[1])))


def _pack_small(loss, n1, nm, n2, nf, bg, lng, lnb, bs, rel, ws):
    rows = [_pad_row(loss), n1.reshape(1, D), nm.reshape(1, D), n2.reshape(1, D), nf.reshape(1, D), bg.reshape(2, D),
            jnp.concatenate([lng.reshape(1, D_SGU), lnb.reshape(1, D_SGU)], axis=1), bs.reshape(1, D),
            jnp.pad(rel.reshape(HEADS, N_REL), ((0, 0), (0, D - N_REL))), jnp.zeros((ROW_WS - ROW_REL - HEADS, D), F32),
            ws.reshape(128, D)]
    return jnp.concatenate(rows, axis=0).astype(F32)


def _unpack_small(s):
    return {"loss": s[ROW_LOSS, 0], "norm_ffn1": s[ROW_N1][None], "norm_mix": s[ROW_NM][None], "norm_ffn2": s[ROW_N2][None],
            "norm_final": s[ROW_NF], "b_gate": s[ROW_BG:ROW_BG + 2].reshape(1, 2 * D),
            "sgu_ln_g": s[ROW_LN, :D_SGU][None], "sgu_ln_b": s[ROW_LN, D_SGU:][None], "sgu_b_s": s[ROW_BS].reshape(1, 8, 128),
            "rel_bias": s[ROW_REL:ROW_REL + HEADS, :N_REL][None], "sgu_w_s": s[ROW_WS:].reshape(1, 8, 128, 128)}


_SMALL = ("norm_ffn1", "norm_mix", "norm_ffn2", "norm_final", "b_gate", "sgu_ln_g", "sgu_ln_b", "sgu_b_s", "rel_bias", "sgu_w_s")


def _pack_small_params(p, prefix=""):
    g = lambda n: p[prefix + n]
    return _pack_small(jnp.zeros((1,), F32), g("norm_ffn1"), g("norm_mix"), g("norm_ffn2"), g("norm_final"), g("b_gate"),
                       g("sgu_ln_g"), g("sgu_ln_b"), g("sgu_b_s"), g("rel_bias"), g("sgu_w_s"))


def _step(x, target, p):
    n1, nm, n2 = p["norm_ffn1"], p["norm_mix"], p["norm_ffn2"]
    nf = p["norm_final"].reshape(1, D)
    lng, lnb = p["sgu_ln_g"], p["sgu_ln_b"]
    w_m = jnp.where(jnp.asarray(_sgu_mask())[None], p["sgu_w_s"][0], 0.0).astype(BF16)
    wm2 = jnp.concatenate([w_m[0::2], w_m[1::2]], axis=2)
    w_mt = w_m.transpose(0, 2, 1)
    wmt2 = jnp.concatenate([w_mt[0::2], w_mt[1::2]], axis=2)
    bsx = jnp.repeat(p["sgu_b_s"][0].T, 64, axis=1)
    bias = _band_bias(p["rel_bias"][0])

    def chip_sums(grads, name):
        gots = _comm_only([_SiblingSwap(grads)], "swap_" + name)
        return _pair_sums(grads, gots, "pair_sums_" + name)

    def as_rows(a):
        return jnp.swapaxes(a, 1, 2)

    def updates(parts, sums, names):
        res = {}
        for pt, sm, n in zip(parts, sums, names):
            if p[n].shape[1:] == pt.shape[1:]:
                res[n] = _adamw_matrix(pt, sm, p[n], p["m_" + n], p["v_" + n], False, "adamw_" + n)
            elif p[n].shape[2] > 128:
                res[n] = [as_rows(o) for o in _adamw_matrix(pt, sm, as_rows(p[n]), as_rows(p["m_" + n]), as_rows(p["v_" + n]), False,
                                                            "adamw_" + n)]
            else:
                res[n] = _adamw_matrix(pt, sm, p[n], p["m_" + n], p["v_" + n], True, "adamw_" + n)
        return res

    rows1 = _pack_rows([[(as_rows(p["ffn1_w_gate"]), False)], [(as_rows(p["ffn1_w_up"]), False)], [(p["ffn1_w_down"], False)]], "pack_ffn1")
    rows_m = _pack_rows([[(as_rows(p["w_in"]), False)], [(p["w_branch_att"], True), (p["w_branch_sgu"], True)], [(p["w_out"], False)]],
                        "pack_mixer")
    rows2d = _pack_rows([[(p["ffn2_w_down"], False)]], "pack_ffn2_down")
    rows2gu = _pack_rows([[(as_rows(p["ffn2_w_gate"]), False)], [(as_rows(p["ffn2_w_up"]), False)]], "pack_ffn2_gate_up")
    (gw1,) = _comm_only([_Gather(rows1)], "gather_ffn1")
    x1, ab1, (gwm,) = _ffn_fwd(x, n1, gw1, gw1, 2 * R_FF, "ffn1_fwd", [_Gather(rows_m)])
    (q, k, v, zs, gt, h2), (gw2d,) = _mix_proj_fwd(x1, nm, p["b_gate"], gwm, [_Gather(rows2d)])
    y_att, (gw2gu,) = _att_fwd(q, k, v, bias, [_Gather(rows2gu)])
    x2, y_sgu, merged = _merge_fwd(x1, zs, gt, y_att, lng, lnb, wm2, bsx, gwm)
    dx3, ab2, _, d_nf, loss = _ffn_fwd(x2, n2, gw2gu, gw2d, 0, "ffn2_fwd", head=(nf, target))

    (dx2, d_n2, hb, dab, sb, dfb), _ = _ffn_bwd(x2, n2, ab2, dx3, gw2gu, gw2d, 0, "ffn2_bwd")
    g2 = [_weight_grad(dab, hb, "ffn2_dw_gate", 0, F), _weight_grad(dab, hb, "ffn2_dw_up", F, F), _weight_grad(sb, dfb, "ffn2_dw_down")]
    dzg, dya, dys, dpp, dxb, d_bg = _merge_bwd(dx2, gt, y_att, y_sgu, gwm)
    g_late = g2 + [_weight_grad(dpp, y_att, "dw_branch_att", 0, D), _weight_grad(dpp, y_sgu, "dw_branch_sgu", D, D),
                   _weight_grad(merged, dxb, "dw_out")]
    late = ("ffn2_w_gate", "ffn2_w_up", "ffn2_w_down", "w_branch_att", "w_branch_sgu", "w_out")
    (dzs, d_wm, d_bs, d_lng, d_lnb), gots_late = _sgu_bwd(zs, dys, lng, lnb, wm2, wmt2, bsx, [_SiblingSwap(g_late)])
    sums_late = _pair_sums(g_late, gots_late, "pair_sums_late")
    (dq, dk, dv, d_bias), parts_late = _att_bwd(q, k, v, bias, dya, [_ChipScatter(sums_late)])
    big = updates(parts_late, sums_late, late)
    d_rel = _rel_bias_grad(d_bias)
    dz = jnp.concatenate([dq, dk, dv, dzs, dzg], axis=1)
    small = _pack_small(loss[0, :1], jnp.zeros((1, D), F32), jnp.zeros((1, D), F32), d_n2, d_nf, d_bg, d_lng, d_lnb, d_bs[:, :8].T, d_rel,
                        d_wm.reshape(8, 128, 128))
    g_in, (small_parts,) = _weight_grad(dz, h2, "dw_in", comms=[_AllToAll(small)])
    sums_in = chip_sums([g_in], "w_in")
    (dx1, d_nm), parts_in = _mix_proj_bwd(dz, x1, nm, dx2, gwm, [_ChipScatter(sums_in)])
    big.update(updates(parts_in, sums_in, ("w_in",)))

    (dx0, d_n1, hb, dab, sb, dfb), _ = _ffn_bwd(x, n1, ab1, dx1, gw1, gw1, 2 * R_FF, "ffn1_bwd")
    small_late = jnp.zeros((8, D), F32).at[ROW_N1].set(d_n1[0]).at[ROW_NM].set(d_nm[0])
    sums_d = chip_sums([_weight_grad(sb, dfb, "ffn1_dw_down")], "ffn1_down")
    g_gate, parts_d = _weight_grad(dab, hb, "ffn1_dw_gate", 0, F, [_ChipScatter(sums_d)])
    sums_g = chip_sums([g_gate], "ffn1_gate")
    g_up, (parts_g, late_parts) = _weight_grad(dab, hb, "ffn1_dw_up", F, F, [_ChipScatter(sums_g), _AllToAll(small_late)])
    sums_u = chip_sums([g_up], "ffn1_up")
    parts_u = _comm_only([_ChipScatter(sums_u)], "scatter_ffn1_up")
    big.update(updates(parts_d + [parts_g] + parts_u, sums_d + sums_g + sums_u, ("ffn1_w_down", "ffn1_w_gate", "ffn1_w_up")))
    out_s = _sum_adamw_small(small_parts, late_parts, _pack_small_params(p), _pack_small_params(p, "m_"), _pack_small_params(p, "v_"))
    return dx0, [{**{n: four[i] for n, four in big.items()}, **_unpack_small(s)} for i, s in enumerate(out_s)]


_OUT_ORDER = ("norm_ffn1", "ffn1_w_gate", "ffn1_w_up", "ffn1_w_down", "norm_mix", "w_in", "b_gate", "rel_bias", "sgu_ln_g", "sgu_ln_b",
              "sgu_w_s", "sgu_b_s", "w_branch_att", "w_branch_sgu", "w_out", "norm_ffn2", "ffn2_w_gate", "ffn2_w_up", "ffn2_w_down",
              "norm_final")


def kernel(x, norm_ffn1, ffn1_w_gate, ffn1_w_up, ffn1_w_down, norm_mix, w_in, b_gate, rel_bias, sgu_ln_g, sgu_ln_b, sgu_w_s, sgu_b_s, w_branch_att, w_branch_sgu, w_out, norm_ffn2, ffn2_w_gate, ffn2_w_up, ffn2_w_down, norm_final, loss_target, m_norm_ffn1, m_ffn1_w_gate, m_ffn1_w_up, m_ffn1_w_down, m_norm_mix, m_w_in, m_b_gate, m_rel_bias, m_sgu_ln_g, m_sgu_ln_b, m_sgu_w_s, m_sgu_b_s, m_w_branch_att, m_w_branch_sgu, m_w_out, m_norm_ffn2, m_ffn2_w_gate, m_ffn2_w_up, m_ffn2_w_down, m_norm_final, v_norm_ffn1, v_ffn1_w_gate, v_ffn1_w_up, v_ffn1_w_down, v_norm_mix, v_w_in, v_b_gate, v_rel_bias, v_sgu_ln_g, v_sgu_ln_b, v_sgu_w_s, v_sgu_b_s, v_w_branch_att, v_w_branch_sgu, v_w_out, v_norm_ffn2, v_ffn2_w_gate, v_ffn2_w_up, v_ffn2_w_down, v_norm_final):
    args = dict(locals())
    dx, outs = _step(x[0], loss_target[0], {pre + n: args[pre + n] for pre in ("", "m_", "v_") for n in _OUT_ORDER})
    return (outs[0]["loss"], dx[None], *[o[n] for o in outs for n in _OUT_ORDER])
```

```python
import functools

import numpy as np
import jax
import jax.numpy as jnp
from jax import lax
from jax.experimental import pallas as pl
from jax.experimental.pallas import tpu as pltpu

F32 = jnp.float32
BF16 = jnp.bfloat16

N_DEV = 8
D = 1024
F = 2816
D_ATT = 512
D_SGU = 512
D_IN = 4608
HEADS = 8
CHUNK = 64
N_LEFT = 8
REL_CLIP = 256
N_REL = 2 * REL_CLIP + 1
SGU_BLOCK = 128
EPS = 1e-6
NEG_INF = -1e30
QB = 256
KW = 3 * QB

R_FF, R_IN, R_BR, R_WO = F // N_DEV, D_IN // N_DEV, D // N_DEV, D // N_DEV
OFF_IN, OFF_BR, OFF_WO = 0, R_IN, R_IN + R_BR
FFN_ROWS = 3 * R_FF
MIX_ROWS = R_IN + R_BR + R_WO

FC = 256
TM = 512
VMEM_LIMIT = 56 * 1024 * 1024

ADAM_LR, ADAM_B1, ADAM_B2, ADAM_EPS, ADAM_WD, ADAM_STEP = 0.001, 0.9, 0.999, 1e-08, 0.01, 10

MESH = pl.DeviceIdType.MESH
ANY = pl.BlockSpec(memory_space=pl.ANY)


def _nt(a, b):
    return lax.dot_general(a, b, (((1,), (1,)), ((), ())), preferred_element_type=F32)


def _tn(a, b):
    return lax.dot_general(a, b, (((0,), (0,)), ((), ())), preferred_element_type=F32)


def _nn(a, b):
    return jnp.dot(a, b, preferred_element_type=F32)


def _cparams(sem=None):
    return pltpu.CompilerParams(dimension_semantics=sem, vmem_limit_bytes=VMEM_LIMIT)


def _load_rows(gw_ref, dst, off, rows, sems):
    copies = [pltpu.make_async_copy(gw_ref.at[k, pl.ds(off, rows), :], dst.at[pl.ds(k * rows, rows), :], sems.at[k])
              for k in range(N_DEV)]
    for cp in copies:
        cp.start()
    return copies


def _rms(xv):
    r = lax.rsqrt(jnp.mean(xv * xv, axis=-1, keepdims=True) + EPS)
    return xv * r, r


def _rms_bwd(dh, xn, r, gain):
    dxn = dh * gain
    dx = r * (dxn - xn * jnp.mean(dxn * xn, axis=-1, keepdims=True))
    return dx, jnp.sum(dh * xn, axis=0, keepdims=True)


def _gelu(x):
    t = jnp.tanh(0.7978845608028654 * (x + 0.044715 * x * x * x))
    return 0.5 * x * (1.0 + t), t


def _gelu_grad(x, t):
    return 0.5 * (1.0 + t) + 0.5 * x * (1.0 - t * t) * 0.7978845608028654 * (1.0 + 3.0 * 0.044715 * x * x)


def _place():
    x, y, cc = lax.axis_index("x"), lax.axis_index("y"), lax.axis_index("c")
    return x, y, cc, [(1 - x, y), (x, 1 - y), (1 - x, 1 - y)]


class _Gather:
    def __init__(self, shard):
        self.inputs = [shard]
        self.out_shape = [jax.ShapeDtypeStruct((N_DEV,) + shard.shape, shard.dtype)]
        self.scratch = [pltpu.SemaphoreType.DMA((7,)), pltpu.SemaphoreType.DMA((7,)), pltpu.SemaphoreType.DMA]

    def _copies(self, ins, outs, scr):
        (x_ref,), (out_ref,), (send_sems, recv_sems, local_sem) = ins, outs, scr
        x, y, cc, chips = _place()

        def slab(px, py, pc):
            return out_ref.at[4 * px + 2 * py + pc]

        def copy(k, block, to, src=None):
            return pltpu.make_async_remote_copy(
                src_ref=slab(*block) if src is None else src, dst_ref=slab(*block),
                send_sem=send_sems.at[k], recv_sem=recv_sems.at[k], device_id=to, device_id_type=MESH)

        me, sibling = (x, y, cc), (x, y, 1 - cc)
        mine = pltpu.make_async_copy(x_ref, slab(*me), local_sem)
        first = [copy(0, me, sibling, src=x_ref)] + [copy(1 + j, me, (*chip, cc), src=x_ref) for j, chip in enumerate(chips)]
        landed = [copy(1 + j, (*chip, cc), me) for j, chip in enumerate(chips)]
        passed = [copy(4 + j, (*chip, cc), sibling) for j, chip in enumerate(chips)]
        from_sibling = [copy(0, sibling, me)] + [copy(4 + j, (*chip, 1 - cc), me) for j, chip in enumerate(chips)]
        return mine, first, landed, passed, from_sibling

    def begin(self, *refs):
        mine, first, _, _, _ = self._copies(*refs)
        mine.start()
        for cp in first:
            cp.start()

    def relay(self, *refs):
        _, _, landed, passed, _ = self._copies(*refs)
        for arrived, onward in zip(landed, passed):
            arrived.wait_recv()
            onward.start()

    def end(self, *refs):
        mine, first, _, passed, from_sibling = self._copies(*refs)
        for cp in from_sibling:
            cp.wait_recv()
        for cp in first + passed:
            cp.wait_send()
        mine.wait()


class _Direct:
    def begin(self, *refs):
        keep, give = self._copies(*refs)
        for cp in keep + give:
            cp.start()

    def relay(self, *refs):
        pass

    def end(self, *refs):
        keep, give = self._copies(*refs)
        for cp in give:
            cp.wait_recv()
        for cp in give:
            cp.wait_send()
        for cp in keep:
            cp.wait()


class _SiblingSwap(_Direct):
    def __init__(self, grads):
        n = len(grads)
        self.inputs = list(grads)
        self.out_shape = [jax.ShapeDtypeStruct((4,) + g.shape[1:], g.dtype) for g in grads]
        self.scratch = [pltpu.SemaphoreType.DMA((n, 4)), pltpu.SemaphoreType.DMA((n, 4))]

    def _copies(self, ins, outs, scr):
        send_sems, recv_sems = scr
        x, y, cc, _ = _place()
        return [], [pltpu.make_async_remote_copy(src_ref=g_ref.at[2 * j + 1 - cc], dst_ref=got_ref.at[j], send_sem=send_sems.at[i, j],
                                                 recv_sem=recv_sems.at[i, j], device_id=(x, y, 1 - cc), device_id_type=MESH)
                    for i, (g_ref, got_ref) in enumerate(zip(ins, outs)) for j in range(4)]


class _ChipScatter(_Direct):
    def __init__(self, sums):
        n = len(sums)
        self.inputs = list(sums)
        self.out_shape = [jax.ShapeDtypeStruct((3,) + s.shape[1:], s.dtype) for s in sums]
        self.scratch = [pltpu.SemaphoreType.DMA((n, 3)), pltpu.SemaphoreType.DMA((n, 3))]

    def _copies(self, ins, outs, scr):
        send_sems, recv_sems = scr
        _, _, cc, chips = _place()
        return [], [pltpu.make_async_remote_copy(src_ref=s_ref.at[2 * px + py], dst_ref=got_ref.at[j], send_sem=send_sems.at[i, j],
                                                 recv_sem=recv_sems.at[i, j], device_id=(px, py, cc), device_id_type=MESH)
                    for i, (s_ref, got_ref) in enumerate(zip(ins, outs)) for j, (px, py) in enumerate(chips)]


class _AllToAll(_Direct):
    def __init__(self, blocks):
        n = len(blocks)
        self.inputs = list(blocks)
        self.out_shape = [jax.ShapeDtypeStruct((N_DEV,) + b.shape, b.dtype) for b in blocks]
        self.scratch = [pltpu.SemaphoreType.DMA((n, 7)), pltpu.SemaphoreType.DMA((n, 7)), pltpu.SemaphoreType.DMA((n,))]

    def _copies(self, ins, outs, scr):
        send_sems, recv_sems, local_sems = scr
        x, y, cc, _ = _place()
        me = 4 * x + 2 * y + cc
        keep = [pltpu.make_async_copy(b_ref, got_ref.at[me], local_sems.at[i]) for i, (b_ref, got_ref) in enumerate(zip(ins, outs))]
        give = [pltpu.make_async_remote_copy(src_ref=b_ref, dst_ref=got_ref.at[me], send_sem=send_sems.at[i, k - 1],
                                             recv_sem=recv_sems.at[i, k - 1],
                                             device_id=(x ^ ((k >> 2) & 1), y ^ ((k >> 1) & 1), cc ^ (k & 1)), device_id_type=MESH)
                for i, (b_ref, got_ref) in enumerate(zip(ins, outs)) for k in range(1, N_DEV)]
        return keep, give


def _split_refs(refs, counts):
    out, pos = [], 0
    for n in counts:
        out.append(list(refs[pos:pos + n]))
        pos += n
    return out


def _bind(comms, c_in, c_out, c_scr):
    ins = _split_refs(c_in, [len(c.inputs) for c in comms])
    outs = _split_refs(c_out, [len(c.out_shape) for c in comms])
    scr = _split_refs(c_scr, [len(c.scratch) for c in comms])
    return [(c, (i, o, s)) for c, i, o, s in zip(comms, ins, outs, scr)]


def _call(body, *, name, grid, in_specs, out_specs, out_shape, scratch_shapes, args, comms=()):
    c_in = [a for c in comms for a in c.inputs]
    c_out = [s for c in comms for s in c.out_shape]
    c_scr = [s for c in comms for s in c.scratch]
    counts = [len(in_specs), len(c_in), len(out_shape), len(c_out), len(scratch_shapes), len(c_scr)]

    def full(*refs):
        ins, cin, outs, cout, scr, cscr = _split_refs(refs, counts)
        bound = _bind(comms, cin, cout, cscr)
        if comms:
            first = functools.reduce(jnp.logical_and, [pl.program_id(ax) == 0 for ax in range(len(grid))])
            last = functools.reduce(jnp.logical_and, [pl.program_id(ax) == n - 1 for ax, n in enumerate(grid)])

            @pl.when(first)
            def _():
                for c, r in bound:
                    c.begin(*r)

            @pl.when(last)
            def _():
                for c, r in bound:
                    c.relay(*r)

        body(*ins, *outs, *scr)
        if comms:
            @pl.when(last)
            def _():
                for c, r in bound:
                    c.end(*r)

    res = pl.pallas_call(
        full, name=name, grid=grid,
        in_specs=list(in_specs) + [ANY] * len(c_in), out_specs=list(out_specs) + [ANY] * len(c_out),
        out_shape=list(out_shape) + c_out, scratch_shapes=list(scratch_shapes) + c_scr,
        compiler_params=_cparams(("arbitrary",) * len(grid)),
    )(*args, *c_in)
    return list(res[:len(out_shape)]), list(res[len(out_shape):])


def _comm_only(comms, name):
    c_in = [a for c in comms for a in c.inputs]
    c_out = [s for c in comms for s in c.out_shape]
    c_scr = [s for c in comms for s in c.scratch]

    def full(*refs):
        cin, cout, cscr = _split_refs(refs, [len(c_in), len(c_out), len(c_scr)])
        bound = _bind(comms, cin, cout, cscr)
        for phase in ("begin", "relay", "end"):
            for c, r in bound:
                getattr(c, phase)(*r)

    return list(pl.pallas_call(full, name=name, in_specs=[ANY] * len(c_in), out_specs=[ANY] * len(c_out), out_shape=c_out,
                               scratch_shapes=c_scr)(*c_in))


def _my_index(*axes_and_weights):
    return sum(w * lax.axis_index(a) for a, w in axes_and_weights).astype(jnp.int32).reshape(1)


def _pair_sums(grads, gots, name):
    n = len(grads)

    def body(c_ref, *refs):
        for a_ref, b_ref, o_ref in zip(refs[:n], refs[n:2 * n], refs[2 * n:]):
            o_ref[...] = (a_ref[...].astype(F32) + b_ref[...].astype(F32)).astype(BF16)

    def tile(g):
        return pl.BlockSpec((1,) + g.shape[1:], lambda j, c_ref: (j, 0, 0))

    def mine(g):
        return pl.BlockSpec((1, None) + g.shape[1:], lambda j, c_ref: (j, c_ref[0], 0, 0))

    return list(pl.pallas_call(
        body, name=name,
        grid_spec=pltpu.PrefetchScalarGridSpec(num_scalar_prefetch=1, grid=(4,), in_specs=[mine(g) for g in gots] + [tile(g) for g in gots],
                                               out_specs=[tile(g) for g in gots]),
        out_shape=[jax.ShapeDtypeStruct(g.shape, BF16) for g in gots],
        compiler_params=_cparams(("arbitrary",)))(_my_index(("c", 1)), *[g.reshape((4, 2) + g.shape[1:]) for g in grads], *gots))


def _ffn_fwd(x, gain, gw_gu, gw_d, off_d, name, comms=(), head=None):
    t = x.shape[0]
    n_head = 0 if head is None else 2

    def body(x_ref, g_ref, gu_ref, d_ref, *refs):
        head_in, (o_ref, ab_ref), head_out = refs[:n_head], refs[n_head:n_head + 2], refs[n_head + 2:2 * n_head + 2]
        wg, wu, wd, s_scr, sems = refs[2 * n_head + 2:]

        @pl.when(pl.program_id(0) == 0)
        def _():
            cps = _load_rows(gu_ref, wg, 0, R_FF, sems.at[0]) + _load_rows(gu_ref, wu, R_FF, R_FF, sems.at[1]) \
                + _load_rows(d_ref, wd, off_d, R_FF, sems.at[2])
            for o in head_out:
                o[...] = jnp.zeros_like(o)
            for cp in cps:
                cp.wait()

        xv = x_ref[...]
        xn, _ = _rms(xv)
        h = (xn * g_ref[...]).astype(BF16)
        for c in range(F // FC):
            rows = pl.ds(c * FC, FC)
            a = _nt(h, wg[rows, :])
            b = _nt(h, wu[rows, :])
            ab_ref[:, c * FC:(c + 1) * FC] = a.astype(BF16)
            ab_ref[:, F + c * FC:F + (c + 1) * FC] = b.astype(BF16)
            s_scr[:, c * FC:(c + 1) * FC] = (a * jax.nn.sigmoid(a) * b).astype(BF16)
        out = xv + 0.5 * _nn(s_scr[...], wd[...])
        if head is None:
            o_ref[...] = out
        else:
            (gf_ref, t_ref), (dg_ref, loss_ref) = head_in, head_out
            gain_f = gf_ref[...]
            yn, r = _rms(out)
            err = yn * gain_f - t_ref[...]
            loss_ref[...] += 0.5 * jnp.sum(jnp.mean(err * err, axis=-1, keepdims=True), axis=0, keepdims=True)
            o_ref[...], dg = _rms_bwd(err * (1.0 / D), yn, r, gain_f)
            dg_ref[...] += dg

    tile = pl.BlockSpec((TM, D), lambda i: (i, 0))
    row = pl.BlockSpec((1, D), lambda i: (0, 0))
    head_specs = [] if head is None else [row, pl.BlockSpec((1, 128), lambda i: (0, 0))]
    head_shapes = [] if head is None else [jax.ShapeDtypeStruct((1, D), F32), jax.ShapeDtypeStruct((1, 128), F32)]
    res, got = _call(
        body, name=name, grid=(t // TM,),
        in_specs=[tile, row, ANY, ANY] + ([] if head is None else [row, tile]),
        out_specs=[tile, pl.BlockSpec((TM, 2 * F), lambda i: (i, 0))] + head_specs,
        out_shape=[jax.ShapeDtypeStruct((t, D), F32), jax.ShapeDtypeStruct((t, 2 * F), BF16)] + head_shapes,
        scratch_shapes=[pltpu.VMEM((F, D), BF16)] * 3 + [pltpu.VMEM((TM, F), BF16), pltpu.SemaphoreType.DMA((3, N_DEV))],
        args=(x, gain, gw_gu, gw_d) + (() if head is None else tuple(head)), comms=comms)
    return (res[0], res[1], got, *res[2:])


def _ffn_bwd_hidden(x, gain, ab, dout, gw_d, off_d, name):
    t = x.shape[0]

    def tile(w):
        return pl.BlockSpec((TM, w), lambda i: (i, 0))

    def hidden(x_ref, g_ref, ab_ref, do_ref, d_ref, dab_ref, s_ref, df_ref, h_ref, wd, sems):
        @pl.when(pl.program_id(0) == 0)
        def _():
            for cp in _load_rows(d_ref, wd, off_d, R_FF, sems):
                cp.wait()

        xn, _ = _rms(x_ref[...])
        h_ref[...] = (xn * g_ref[...]).astype(BF16)
        df = (0.5 * do_ref[...]).astype(BF16)
        df_ref[...] = df
        for c in range(F // FC):
            a = ab_ref[:, c * FC:(c + 1) * FC].astype(F32)
            b = ab_ref[:, F + c * FC:F + (c + 1) * FC].astype(F32)
            sg = jax.nn.sigmoid(a)
            sl = a * sg
            ds = _nt(df, wd[pl.ds(c * FC, FC), :])
            dab_ref[:, c * FC:(c + 1) * FC] = (ds * b * (sg * (1.0 + a * (1.0 - sg)))).astype(BF16)
            dab_ref[:, F + c * FC:F + (c + 1) * FC] = (ds * sl).astype(BF16)
            s_ref[:, c * FC:(c + 1) * FC] = (sl * b).astype(BF16)

    res, _ = _call(
        hidden, name=name, grid=(t // TM,),
        in_specs=[tile(D), pl.BlockSpec((1, D), lambda i: (0, 0)), tile(2 * F), tile(D), ANY],
        out_specs=[tile(2 * F), tile(F), tile(D), tile(D)],
        out_shape=[jax.ShapeDtypeStruct((t, 2 * F), BF16), jax.ShapeDtypeStruct((t, F), BF16), jax.ShapeDtypeStruct((t, D), BF16),
                   jax.ShapeDtypeStruct((t, D), BF16)],
        scratch_shapes=[pltpu.VMEM((F, D), BF16), pltpu.SemaphoreType.DMA((N_DEV,))],
        args=(x, gain, ab, dout, gw_d))
    return res


def _ffn_bwd_input(x, gain, dab, dout, gw_gu, name, comms=()):
    t = x.shape[0]

    def body(x_ref, g_ref, dab_ref, do_ref, gu_ref, dx_ref, dg_ref, wg, wu, sems):
        @pl.when(pl.program_id(0) == 0)
        def _():
            cps = _load_rows(gu_ref, wg, 0, R_FF, sems.at[0]) + _load_rows(gu_ref, wu, R_FF, R_FF, sems.at[1])
            dg_ref[...] = jnp.zeros_like(dg_ref)
            for cp in cps:
                cp.wait()

        gain_v = g_ref[...]
        xn, r = _rms(x_ref[...])
        dh = _nn(dab_ref[:, :F], wg[...]) + _nn(dab_ref[:, F:], wu[...])
        dxn, dg = _rms_bwd(dh, xn, r, gain_v)
        dg_ref[...] += dg
        dx_ref[...] = do_ref[...] + dxn

    def tile(w):
        return pl.BlockSpec((TM, w), lambda i: (i, 0))

    row = pl.BlockSpec((1, D), lambda i: (0, 0))
    return _call(
        body, name=name, grid=(t // TM,),
        in_specs=[tile(D), row, tile(2 * F), tile(D), ANY], out_specs=[tile(D), row],
        out_shape=[jax.ShapeDtypeStruct((t, D), F32), jax.ShapeDtypeStruct((1, D), F32)],
        scratch_shapes=[pltpu.VMEM((F, D), BF16)] * 2 + [pltpu.SemaphoreType.DMA((2, N_DEV))],
        args=(x, gain, dab, dout, gw_gu), comms=comms)


def _weight_grad(a, b, name, col_off=0, m=None, comms=None, tmm=256):
    t = a.shape[0]
    m = a.shape[1] if m is None else m
    n = b.shape[1]
    first = col_off // tmm

    def body(a_ref, b_ref, o_ref):
        o_ref[...] = _tn(a_ref[...], b_ref[...]).astype(BF16)

    (out,), got = _call(
        body, name=name, grid=(m // tmm,),
        in_specs=[pl.BlockSpec((t, tmm), lambda i: (0, first + i)), pl.BlockSpec((t, n), lambda i: (0, 0))],
        out_specs=[pl.BlockSpec((tmm, n), lambda i: (i, 0))],
        out_shape=[jax.ShapeDtypeStruct((m, n), BF16)], scratch_shapes=[], args=(a, b), comms=comms or ())
    out = out.reshape(N_DEV, m // N_DEV, n)
    return out if comms is None else (out, got)


def _mix_proj_fwd(x, gain, b_gate, gw, comms=()):
    t = x.shape[0]

    def body(x_ref, g_ref, bg_ref, gw_ref, q_ref, k_ref, v_ref, zs_ref, gt_ref, h_ref, win, sems):
        @pl.when(pl.program_id(0) == 0)
        def _():
            for cp in _load_rows(gw_ref, win, OFF_IN, R_IN, sems):
                cp.wait()

        xn, _ = _rms(x_ref[...])
        h = (xn * g_ref[...]).astype(BF16)
        h_ref[...] = h
        q_ref[...] = (_nt(h, win[0:512, :]) * 0.125).astype(BF16)
        k_ref[...] = _nt(h, win[512:1024, :]).astype(BF16)
        v_ref[...] = _nt(h, win[1024:1536, :]).astype(BF16)
        for c in range(2):
            zs_ref[:, c * 512:(c + 1) * 512] = _nt(h, win[1536 + c * 512:2048 + c * 512, :]).astype(BF16)
        for c in range(4):
            zg = _nt(h, win[2560 + c * 512:3072 + c * 512, :]) + bg_ref[:, c * 512:(c + 1) * 512]
            gt_ref[:, c * 512:(c + 1) * 512] = jax.nn.sigmoid(zg).astype(BF16)

    def tile(w):
        return pl.BlockSpec((TM, w), lambda i: (i, 0))

    return _call(
        body, name="mix_proj_fwd", grid=(t // TM,),
        in_specs=[tile(D), pl.BlockSpec((1, D), lambda i: (0, 0)), pl.BlockSpec((1, 2 * D), lambda i: (0, 0)), ANY],
        out_specs=[tile(D_ATT), tile(D_ATT), tile(D_ATT), tile(2 * D_SGU), tile(2 * D), tile(D)],
        out_shape=[jax.ShapeDtypeStruct((t, D_ATT), BF16)] * 3 + [jax.ShapeDtypeStruct((t, 2 * D_SGU), BF16),
                                                                   jax.ShapeDtypeStruct((t, 2 * D), BF16),
                                                                   jax.ShapeDtypeStruct((t, D), BF16)],
        scratch_shapes=[pltpu.VMEM((D_IN, D), BF16), pltpu.SemaphoreType.DMA((N_DEV,))],
        args=(x, gain, b_gate, gw), comms=comms)


def _mix_proj_bwd(dz, x, gain, dres, gw, comms=()):
    t = x.shape[0]

    def body(dz_ref, x_ref, g_ref, dr_ref, gw_ref, dx_ref, dg_ref, win, sems):
        @pl.when(pl.program_id(0) == 0)
        def _():
            cps = _load_rows(gw_ref, win, OFF_IN, R_IN, sems)
            dg_ref[...] = jnp.zeros_like(dg_ref)
            for cp in cps:
                cp.wait()

        dh = _nn(dz_ref[...], win[...])
        xn, r = _rms(x_ref[...])
        dxn, dg = _rms_bwd(dh, xn, r, g_ref[...])
        dg_ref[...] += dg
        dx_ref[...] = dr_ref[...] + dxn

    def tile(w):
        return pl.BlockSpec((TM, w), lambda i: (i, 0))

    row = pl.BlockSpec((1, D), lambda i: (0, 0))
    return _call(
        body, name="mix_proj_bwd", grid=(t // TM,),
        in_specs=[tile(D_IN), tile(D), row, tile(D), ANY], out_specs=[tile(D), row],
        out_shape=[jax.ShapeDtypeStruct((t, D), F32), jax.ShapeDtypeStruct((1, D), F32)],
        scratch_shapes=[pltpu.VMEM((D_IN, D), BF16), pltpu.SemaphoreType.DMA((N_DEV,))],
        args=(dz, x, gain, dres, gw), comms=comms)


SKEW_W = KW + QB
N_CAP = 2 * QB - REL_CLIP + 1


def _band_bias(rel_bias):
    cap = rel_bias[:, 2 * REL_CLIP:]
    diag = jnp.concatenate([jnp.broadcast_to(cap, (HEADS, N_CAP)), rel_bias[:, 2 * REL_CLIP - 1::-1],
                            jnp.broadcast_to(cap, (HEADS, SKEW_W - N_CAP - 2 * REL_CLIP))], axis=1)

    def body(d_ref, o_ref):
        lag = lax.broadcasted_iota(jnp.int32, (QB, KW), 1) // CHUNK - lax.broadcasted_iota(jnp.int32, (QB, KW), 0) // CHUNK
        band = (lag >= 0) & (lag <= N_LEFT)
        for h in range(HEADS):
            rows = jnp.broadcast_to(d_ref[h:h + 1, :], (QB, SKEW_W))
            o_ref[h] = jnp.where(band, pltpu.roll(rows, 0, 1, stride=1, stride_axis=0)[:, :KW], NEG_INF)

    return pl.pallas_call(body, name="band_bias", out_shape=jax.ShapeDtypeStruct((HEADS, QB, KW), F32))(diag)


def _att_specs():
    qspec = pl.BlockSpec((QB, D_ATT), lambda g: (g, 0))
    kspecs = [pl.BlockSpec((QB, D_ATT), lambda g: (jnp.maximum(g - 2, 0), 0)),
              pl.BlockSpec((QB, D_ATT), lambda g: (jnp.maximum(g - 1, 0), 0)), qspec]
    bspec = pl.BlockSpec((HEADS, QB, KW), lambda g: (0, 0, 0))
    return qspec, kspecs, bspec


def _att_probs(qm, kp, bias, valid):
    s = jnp.where(valid, _nt(qm, kp) + bias, NEG_INF)
    e = jnp.exp(s - jnp.max(s, axis=-1, keepdims=True))
    return e / jnp.sum(e, axis=-1, keepdims=True)


def _att_valid():
    g = pl.program_id(0)
    blk = lax.broadcasted_iota(jnp.int32, (QB, KW), 1) // QB
    return (blk + g) >= 2


def _att_fwd(q, k, v, bias, comms=()):
    t = q.shape[0]

    def body(q_ref, k0, k1, k2, v0, v1, v2, b_ref, y_ref):
        valid = _att_valid()
        first = lax.broadcasted_iota(jnp.int32, (1, 128), 1) < 64
        for p in range(HEADS // 2):
            lanes = slice(p * 128, (p + 1) * 128)
            qp = q_ref[:, lanes]
            kp = jnp.concatenate([k0[:, lanes], k1[:, lanes], k2[:, lanes]], axis=0)
            vp = jnp.concatenate([v0[:, lanes], v1[:, lanes], v2[:, lanes]], axis=0)
            out = jnp.zeros((QB, 128), F32)
            for hh in range(2):
                mask = first if hh == 0 else jnp.logical_not(first)
                pr = _att_probs(jnp.where(mask, qp, 0), kp, b_ref[2 * p + hh], valid)
                out = out + _nn(pr.astype(BF16), jnp.where(mask, vp, 0))
            y_ref[:, lanes] = out.astype(BF16)

    qspec, kspecs, bspec = _att_specs()
    (out,), got = _call(
        body, name="att_fwd", grid=(t // QB,),
        in_specs=[qspec] + kspecs + kspecs + [bspec], out_specs=[qspec],
        out_shape=[jax.ShapeDtypeStruct((t, D_ATT), BF16)], scratch_shapes=[],
        args=(q, k, k, k, v, v, v, bias), comms=comms)
    return out, got


def _att_bwd(q, k, v, bias, dy, comms=()):
    t = q.shape[0]
    n_blocks = t // QB

    def body(q_ref, k0, k1, k2, v0, v1, v2, b_ref, dy_ref, dq_ref, dk_ref, dv_ref, db_ref, dk_acc, dv_acc):
        g = pl.program_id(0)

        @pl.when(g == 0)
        def _():
            db_ref[...] = jnp.zeros_like(db_ref)
            dk_acc[...] = jnp.zeros_like(dk_acc)
            dv_acc[...] = jnp.zeros_like(dv_acc)

        valid = _att_valid()
        first = lax.broadcasted_iota(jnp.int32, (1, 128), 1) < 64
        for p in range(HEADS // 2):
            lanes = slice(p * 128, (p + 1) * 128)
            qp = q_ref[:, lanes]
            dyp = dy_ref[:, lanes]
            kp = jnp.concatenate([k0[:, lanes], k1[:, lanes], k2[:, lanes]], axis=0)
            vp = jnp.concatenate([v0[:, lanes], v1[:, lanes], v2[:, lanes]], axis=0)
            dq = jnp.zeros((QB, 128), F32)
            dk = jnp.zeros((KW, 128), F32)
            dv = jnp.zeros((KW, 128), F32)
            for hh in range(2):
                mask = first if hh == 0 else jnp.logical_not(first)
                qm = jnp.where(mask, qp, 0)
                dym = jnp.where(mask, dyp, 0)
                pr = _att_probs(qm, kp, b_ref[2 * p + hh], valid)
                dp = _nt(dym, vp)
                ds = pr * (dp - jnp.sum(dp * pr, axis=-1, keepdims=True))
                db_ref[2 * p + hh] += ds
                dsb = ds.astype(BF16)
                dq = dq + _nn(dsb, jnp.where(mask, kp, 0))
                dk = dk + _tn(dsb, qm)
                dv = dv + _tn(pr.astype(BF16), dym)
            dq_ref[:, lanes] = (dq * 0.125).astype(BF16)
            for j in range(3):
                rows = pl.ds(pl.multiple_of(jnp.maximum(g - 2 + j, 0) * QB, QB), QB)
                dk_acc[rows, lanes] += dk[j * QB:(j + 1) * QB]
                dv_acc[rows, lanes] += dv[j * QB:(j + 1) * QB]

        @pl.when(g == n_blocks - 1)
        def _():
            dk_ref[...] = dk_acc[...].astype(BF16)
            dv_ref[...] = dv_acc[...].astype(BF16)

    qspec, kspecs, bspec = _att_specs()
    full = pl.BlockSpec((t, D_ATT), lambda g: (0, 0))
    return _call(
        body, name="att_bwd", grid=(n_blocks,),
        in_specs=[qspec] + kspecs + kspecs + [bspec, qspec], out_specs=[qspec, full, full, bspec],
        out_shape=[jax.ShapeDtypeStruct((t, D_ATT), BF16)] * 3 + [jax.ShapeDtypeStruct((HEADS, QB, KW), F32)],
        scratch_shapes=[pltpu.VMEM((t, D_ATT), F32)] * 2,
        args=(q, k, k, k, v, v, v, bias, dy), comms=comms)


def _rel_bias_grad(dbias):
    def body(db_ref, cs_ref, tot_ref):
        lane = lax.broadcasted_iota(jnp.int32, (1, SKEW_W), 1)
        capped = (lane < N_CAP) | (lane > KW)
        pad = jnp.zeros((8, QB), F32)
        for h in range(HEADS):
            z = jnp.concatenate([db_ref[h, 0:8, :], pad], axis=1)
            for a in range(1, QB // 8):
                z = z + pltpu.roll(jnp.concatenate([db_ref[h, 8 * a:8 * a + 8, :], pad], axis=1), SKEW_W - 8 * a, 1)
            cs = z[0:1, :]
            for b in range(1, 8):
                cs = cs + pltpu.roll(z[b:b + 1, :], SKEW_W - b, 1)
            cs_ref[h:h + 1, :] = cs
            tot_ref[h:h + 1, :] = jnp.broadcast_to(jnp.sum(jnp.where(capped, cs, 0.0), axis=1, keepdims=True), (1, 128))

    cs, tot = pl.pallas_call(
        body, name="rel_bias_grad",
        out_shape=[jax.ShapeDtypeStruct((HEADS, SKEW_W), F32), jax.ShapeDtypeStruct((HEADS, 128), F32)],
    )(dbias)
    return jnp.concatenate([cs[:, KW:N_CAP - 1:-1], tot[:, :1]], axis=1)


def _sgu_mask():
    pos = np.arange(SGU_BLOCK)
    return (pos[:, None] // CHUNK) >= (pos[None, :] // CHUNK)


def _group_stack(blk, first):
    return jnp.concatenate([jnp.where(first, blk, 0), jnp.where(first, 0, blk)], axis=0)


def _sgu_norm(zs_ref, lng, lnb):
    zs = zs_ref[...].astype(F32)
    ga, th = _gelu(zs)
    u, vs = ga[:, :D_SGU], ga[:, D_SGU:]
    mu = jnp.mean(vs, axis=-1, keepdims=True)
    cen = vs - mu
    rstd = lax.rsqrt(jnp.mean(cen * cen, axis=-1, keepdims=True) + EPS)
    xhat = cen * rstd
    return zs, th, u, xhat, rstd, xhat * lng + lnb


def _sgu_mix(vb, wm2_ref, bsx, s_ref):
    first = lax.broadcasted_iota(jnp.int32, (1, 128), 1) < 64
    for n in range(TM // SGU_BLOCK):
        for p in range(4):
            blk = vb[n * 128:(n + 1) * 128, p * 128:(p + 1) * 128]
            s_ref[n * 128:(n + 1) * 128, p * 128:(p + 1) * 128] = _nn(wm2_ref[p], _group_stack(blk, first)) + bsx[:, p * 128:(p + 1) * 128]


def _merge_fwd(x, zs, gt, y_att, lng, lnb, wm2, bsx, gw):
    t = x.shape[0]

    def body(x_ref, zs_ref, gt_ref, ya_ref, lng_ref, lnb_ref, wm2_ref, bsx_ref, gw_ref, xo_ref, ys_ref, mg_ref,
             wbr, wo, s_scr, sems):
        @pl.when(pl.program_id(0) == 0)
        def _():
            for cp in _load_rows(gw_ref, wbr, OFF_BR, R_BR, sems.at[0]) + _load_rows(gw_ref, wo, OFF_WO, R_WO, sems.at[1]):
                cp.wait()

        _, _, u, _, _, vsn = _sgu_norm(zs_ref, lng_ref[...], lnb_ref[...])
        _sgu_mix(vsn.astype(BF16), wm2_ref, bsx_ref[...], s_scr)
        ys = (u * s_scr[...]).astype(BF16)
        ys_ref[...] = ys
        pa = _nt(ya_ref[...], wbr[:, :D_ATT])
        ps = _nt(ys, wbr[:, D_ATT:])
        mg = (gt_ref[:, :D].astype(F32) * pa + gt_ref[:, D:].astype(F32) * ps).astype(BF16)
        mg_ref[...] = mg
        xo_ref[...] = x_ref[...] + _nn(mg, wo[...])

    def tile(w):
        return pl.BlockSpec((TM, w), lambda i: (i, 0))

    def const(shape):
        return pl.BlockSpec(shape, lambda i: (0,) * len(shape))

    return pl.pallas_call(
        body, name="merge_fwd", grid=(t // TM,),
        in_specs=[tile(D), tile(2 * D_SGU), tile(2 * D), tile(D_ATT), const((1, D_SGU)), const((1, D_SGU)),
                  const((4, 128, 256)), const((128, D_SGU)), ANY],
        out_specs=[tile(D), tile(D_SGU), tile(D)],
        out_shape=[jax.ShapeDtypeStruct((t, D), F32), jax.ShapeDtypeStruct((t, D_SGU), BF16), jax.ShapeDtypeStruct((t, D), BF16)],
        scratch_shapes=[pltpu.VMEM((D, D), BF16), pltpu.VMEM((D, D), BF16), pltpu.VMEM((TM, D_SGU), F32),
                        pltpu.SemaphoreType.DMA((2, N_DEV))],
        compiler_params=_cparams(("arbitrary",)),
    )(x, zs, gt, y_att, lng, lnb, wm2, bsx, gw)


def _merge_bwd(dx, gt, y_att, y_sgu, gw):
    t = dx.shape[0]

    def body(dx_ref, gt_ref, ya_ref, ys_ref, gw_ref, dzg_ref, dya_ref, dys_ref, dpp_ref, dxb_ref, dbg_ref, wbr, wo, sems):
        @pl.when(pl.program_id(0) == 0)
        def _():
            cps = _load_rows(gw_ref, wbr, OFF_BR, R_BR, sems.at[0]) + _load_rows(gw_ref, wo, OFF_WO, R_WO, sems.at[1])
            dbg_ref[...] = jnp.zeros_like(dbg_ref)
            for cp in cps:
                cp.wait()

        dxb = dx_ref[...].astype(BF16)
        dxb_ref[...] = dxb
        dm = _nt(dxb, wo[...])
        for half, y_ref, w in ((0, ya_ref, wbr.at[:, :D_ATT]), (1, ys_ref, wbr.at[:, D_ATT:])):
            cols = slice(half * D, (half + 1) * D)
            gate = gt_ref[:, cols].astype(F32)
            branch = _nt(y_ref[...], w[...])
            dzg = dm * branch * gate * (1.0 - gate)
            dbg_ref[:, cols] += jnp.sum(dzg, axis=0, keepdims=True)
            dzg_ref[:, cols] = dzg.astype(BF16)
            dbr = (dm * gate).astype(BF16)
            dpp_ref[:, cols] = dbr
            dy = _nn(dbr, w[...])
            if half == 0:
                dya_ref[...] = dy.astype(BF16)
            else:
                dys_ref[...] = dy

    def tile(w):
        return pl.BlockSpec((TM, w), lambda i: (i, 0))

    return pl.pallas_call(
        body, name="merge_bwd", grid=(t // TM,),
        in_specs=[tile(D), tile(2 * D), tile(D_ATT), tile(D_SGU), ANY],
        out_specs=[tile(2 * D), tile(D_ATT), tile(D_SGU), tile(2 * D), tile(D), pl.BlockSpec((1, 2 * D), lambda i: (0, 0))],
        out_shape=[jax.ShapeDtypeStruct((t, 2 * D), BF16), jax.ShapeDtypeStruct((t, D_ATT), BF16), jax.ShapeDtypeStruct((t, D_SGU), F32),
                   jax.ShapeDtypeStruct((t, 2 * D), BF16), jax.ShapeDtypeStruct((t, D), BF16), jax.ShapeDtypeStruct((1, 2 * D), F32)],
        scratch_shapes=[pltpu.VMEM((D, D), BF16), pltpu.VMEM((D, D), BF16), pltpu.SemaphoreType.DMA((2, N_DEV))],
        compiler_params=_cparams(("arbitrary",)),
    )(dx, gt, y_att, y_sgu, gw)


def _sgu_bwd(zs, dys, lng, lnb, wm2, wmt2, bsx, comms=()):
    t = zs.shape[0]
    n_steps = t // TM

    def body(zs_ref, dys_ref, lng_ref, lnb_ref, wm2_ref, wmt2_ref, bsx_ref, dzs_ref, dw_ref, dbs_ref, dlg_ref, dlb_ref,
             s_scr, dv_scr, ds_acc):
        i = pl.program_id(0)

        @pl.when(i == 0)
        def _():
            dw_ref[...] = jnp.zeros_like(dw_ref)
            dlg_ref[...] = jnp.zeros_like(dlg_ref)
            dlb_ref[...] = jnp.zeros_like(dlb_ref)
            ds_acc[...] = jnp.zeros_like(ds_acc)

        lng = lng_ref[...]
        zs, th, u, xhat, rstd, vsn = _sgu_norm(zs_ref, lng, lnb_ref[...])
        vb = vsn.astype(BF16)
        _sgu_mix(vb, wm2_ref, bsx_ref[...], s_scr)
        dys = dys_ref[...]
        du = dys * s_scr[...]
        ds = dys * u
        dsb = ds.astype(BF16)
        first = lax.broadcasted_iota(jnp.int32, (1, 128), 1) < 64
        acc = jnp.zeros((SGU_BLOCK, D_SGU), F32)
        for n in range(TM // SGU_BLOCK):
            rows = slice(n * 128, (n + 1) * 128)
            acc = acc + ds[rows]
            for p in range(4):
                lanes = slice(p * 128, (p + 1) * 128)
                stack = _group_stack(dsb[rows, lanes], first)
                dv_scr[rows, lanes] = _nn(wmt2_ref[p], stack)
                dw_ref[p] += _nt(stack, vb[rows, lanes])
        ds_acc[...] += acc
        dvsn = dv_scr[...]
        dlg_ref[...] += jnp.sum(dvsn * xhat, axis=0, keepdims=True)
        dlb_ref[...] += jnp.sum(dvsn, axis=0, keepdims=True)
        dxh = dvsn * lng
        dvs = rstd * (dxh - jnp.mean(dxh, axis=-1, keepdims=True) - xhat * jnp.mean(dxh * xhat, axis=-1, keepdims=True))
        dga = jnp.concatenate([du, dvs], axis=1)
        dzs_ref[...] = (dga * _gelu_grad(zs, th)).astype(BF16)

        @pl.when(i == n_steps - 1)
        def _():
            r = lax.broadcasted_iota(jnp.int32, (256, 128), 0) % SGU_BLOCK
            c = lax.broadcasted_iota(jnp.int32, (256, 128), 1)
            keep = (r // CHUNK) >= (c // CHUNK)
            for p in range(4):
                dw_ref[p] = jnp.where(keep, dw_ref[p], 0.0)
            total = ds_acc[...]
            grp = lax.broadcasted_iota(jnp.int32, (SGU_BLOCK, D_SGU), 1) // 64
            lane = lax.broadcasted_iota(jnp.int32, (SGU_BLOCK, 128), 1)
            out = jnp.zeros((SGU_BLOCK, 128), F32)
            for gi in range(8):
                out = jnp.where(lane == gi, jnp.sum(jnp.where(grp == gi, total, 0.0), axis=1, keepdims=True), out)
            dbs_ref[...] = out

    def tile(w):
        return pl.BlockSpec((TM, w), lambda i: (i, 0))

    def const(shape):
        return pl.BlockSpec(shape, lambda i: (0,) * len(shape))

    return _call(
        body, name="sgu_bwd", grid=(n_steps,),
        in_specs=[tile(2 * D_SGU), tile(D_SGU), const((1, D_SGU)), const((1, D_SGU)), const((4, 128, 256)), const((4, 128, 256)),
                  const((128, D_SGU))],
        out_specs=[tile(2 * D_SGU), const((4, 256, 128)), const((128, 128)), const((1, D_SGU)), const((1, D_SGU))],
        out_shape=[jax.ShapeDtypeStruct((t, 2 * D_SGU), BF16), jax.ShapeDtypeStruct((4, 256, 128), F32),
                   jax.ShapeDtypeStruct((128, 128), F32), jax.ShapeDtypeStruct((1, D_SGU), F32), jax.ShapeDtypeStruct((1, D_SGU), F32)],
        scratch_shapes=[pltpu.VMEM((TM, D_SGU), F32), pltpu.VMEM((TM, D_SGU), F32), pltpu.VMEM((SGU_BLOCK, D_SGU), F32)],
        args=(zs, dys, lng, lnb, wm2, wmt2, bsx), comms=comms)


def _adamw(g, w, m, v):
    m = ADAM_B1 * m + (1.0 - ADAM_B1) * g
    v = ADAM_B2 * v + (1.0 - ADAM_B2) * (g * g)
    m_hat = m / (1.0 - ADAM_B1 ** ADAM_STEP)
    v_hat = v / (1.0 - ADAM_B2 ** ADAM_STEP)
    return -ADAM_LR * (m_hat / (jnp.sqrt(v_hat) + ADAM_EPS) + ADAM_WD * w), m, v


def _adamw_matrix(parts, sums, w, m, v, transposed, name):
    _, r, c = parts.shape
    tc = 256

    def body(own_ref, p_ref, s_ref, w_ref, m_ref, v_ref, g_ref, d_ref, mo_ref, vo_ref):
        g = p_ref[0].astype(F32) + p_ref[1].astype(F32) + p_ref[2].astype(F32) + s_ref[...].astype(F32)
        g = g.T if transposed else g
        g_ref[...] = g
        d_ref[...], mo_ref[...], vo_ref[...] = _adamw(g, w_ref[...], m_ref[...], v_ref[...])

    own = pl.BlockSpec((None, tc, r), lambda i, o: (0, i, 0)) if transposed else pl.BlockSpec((None, r, tc), lambda i, o: (0, 0, i))
    return pl.pallas_call(
        body, name=name,
        grid_spec=pltpu.PrefetchScalarGridSpec(
            num_scalar_prefetch=1, grid=(c // tc,),
            in_specs=[pl.BlockSpec((3, r, tc), lambda i, o: (0, 0, i)), pl.BlockSpec((None, r, tc), lambda i, o: (o[0], 0, i)), own, own, own],
            out_specs=[own] * 4),
        out_shape=[jax.ShapeDtypeStruct(w.shape, F32)] * 4,
        compiler_params=_cparams(("arbitrary",)),
    )(_my_index(("x", 2), ("y", 1)), parts, sums, w, m, v)


_SMALL_2D = {"norm_ffn1": (1, D), "norm_mix": (1, D), "norm_ffn2": (1, D), "norm_final": (1, D), "b_gate": (1, 2 * D),
             "sgu_ln_g": (1, D_SGU), "sgu_ln_b": (1, D_SGU), "sgu_b_s": (8, SGU_BLOCK), "rel_bias": (HEADS, N_REL),
             "sgu_w_s": (8 * SGU_BLOCK, SGU_BLOCK)}


def _adamw_small(parts, loss_parts, p):
    names = list(parts)
    n = len(names)

    def body(*refs):
        got, loss_got, wmv, outs, loss_out = refs[:n], refs[n], refs[n + 1:4 * n + 1], refs[4 * n + 1:8 * n + 1], refs[8 * n + 1]
        for i, name in enumerate(names):
            g = got[i][0]
            for k in range(1, N_DEV):
                g = g + got[i][k]
            if name == "sgu_b_s":
                g = g.T[0:8, :]
            res = (g,) + _adamw(g, wmv[3 * i][...], wmv[3 * i + 1][...], wmv[3 * i + 2][...])
            for o_ref, val in zip(outs[4 * i:4 * i + 4], res):
                o_ref[...] = val
        total = loss_got[0]
        for k in range(1, N_DEV):
            total = total + loss_got[k]
        loss_out[...] = total

    wmv = [p[pre + name].reshape(_SMALL_2D[name]) for name in names for pre in ("", "m_", "v_")]
    res = pl.pallas_call(
        body, name="adamw_small",
        out_shape=[jax.ShapeDtypeStruct(_SMALL_2D[name], F32) for name in names for _ in range(4)] + [jax.ShapeDtypeStruct((1, 128), F32)],
        compiler_params=_cparams())(*[parts[name] for name in names], loss_parts, *wmv)
    return [{name: res[4 * i + j].reshape(p[name].shape) for i, name in enumerate(names)} for j in range(4)], res[-1]


def _pack_rows(groups, name):
    flat = [a for grp in groups for a, _ in grp]
    rows = [grp[0][0].shape[2] if grp[0][1] else grp[0][0].shape[1] for grp in groups]

    def body(*refs):
        o_ref, pos, off = refs[-1], 0, 0
        for grp, r in zip(groups, rows):
            vals = []
            for _, transposed in grp:
                val = refs[pos][0]
                vals.append(val.T if transposed else val)
                pos += 1
            o_ref[off:off + r, :] = (vals[0] if len(vals) == 1 else jnp.concatenate(vals, axis=1)).astype(BF16)
            off += r

    return pl.pallas_call(body, name=name, out_shape=jax.ShapeDtypeStruct((sum(rows), D), BF16), compiler_params=_cparams())(*flat)


def _step(x, target, p):
    n1, nm, n2 = p["norm_ffn1"], p["norm_mix"], p["norm_ffn2"]
    nf = p["norm_final"].reshape(1, D)
    lng, lnb = p["sgu_ln_g"], p["sgu_ln_b"]
    w_m = jnp.where(jnp.asarray(_sgu_mask())[None], p["sgu_w_s"][0], 0.0).astype(BF16)
    wm2 = jnp.concatenate([w_m[0::2], w_m[1::2]], axis=2)
    w_mt = w_m.transpose(0, 2, 1)
    wmt2 = jnp.concatenate([w_mt[0::2], w_mt[1::2]], axis=2)
    bsx = jnp.repeat(p["sgu_b_s"][0].T, 64, axis=1)
    bias = _band_bias(p["rel_bias"][0])

    def chip_sums(grads, name):
        gots = _comm_only([_SiblingSwap(grads)], "swap_" + name)
        return _pair_sums(grads, gots, "pair_sums_" + name)

    def as_rows(a):
        return jnp.swapaxes(a, 1, 2)

    def updates(parts, sums, names):
        res = {}
        for pt, sm, n in zip(parts, sums, names):
            if p[n].shape[1:] == pt.shape[1:]:
                res[n] = _adamw_matrix(pt, sm, p[n], p["m_" + n], p["v_" + n], False, "adamw_" + n)
            elif p[n].shape[2] > 128:
                res[n] = [as_rows(o) for o in _adamw_matrix(pt, sm, as_rows(p[n]), as_rows(p["m_" + n]), as_rows(p["v_" + n]), False,
                                                            "adamw_" + n)]
            else:
                res[n] = _adamw_matrix(pt, sm, p[n], p["m_" + n], p["v_" + n], True, "adamw_" + n)
        return res

    rows1 = _pack_rows([[(as_rows(p["ffn1_w_gate"]), False)], [(as_rows(p["ffn1_w_up"]), False)], [(p["ffn1_w_down"], False)]], "pack_ffn1")
    rows_m = _pack_rows([[(as_rows(p["w_in"]), False)], [(p["w_branch_att"], True), (p["w_branch_sgu"], True)], [(p["w_out"], False)]],
                        "pack_mixer")
    rows2d = _pack_rows([[(p["ffn2_w_down"], False)]], "pack_ffn2_down")
    rows2gu = _pack_rows([[(as_rows(p["ffn2_w_gate"]), False)], [(as_rows(p["ffn2_w_up"]), False)]], "pack_ffn2_gate_up")
    (gw1,) = _comm_only([_Gather(rows1)], "gather_ffn1")
    x1, ab1, (gwm,) = _ffn_fwd(x, n1, gw1, gw1, 2 * R_FF, "ffn1_fwd", [_Gather(rows_m)])
    (q, k, v, zs, gt, h2), (gw2d,) = _mix_proj_fwd(x1, nm, p["b_gate"], gwm, [_Gather(rows2d)])
    y_att, (gw2gu,) = _att_fwd(q, k, v, bias, [_Gather(rows2gu)])
    x2, y_sgu, merged = _merge_fwd(x1, zs, gt, y_att, lng, lnb, wm2, bsx, gwm)
    dx3, ab2, _, d_nf, loss = _ffn_fwd(x2, n2, gw2gu, gw2d, 0, "ffn2_fwd", head=(nf, target))

    dab, sb, dfb, hb = _ffn_bwd_hidden(x2, n2, ab2, dx3, gw2d, 0, "ffn2_bwd_hidden")
    (dx2, d_n2), _ = _ffn_bwd_input(x2, n2, dab, dx3, gw2gu, "ffn2_bwd")
    g2 = [_weight_grad(dab, hb, "ffn2_dw_gate", 0, F), _weight_grad(dab, hb, "ffn2_dw_up", F, F), _weight_grad(sb, dfb, "ffn2_dw_down")]
    dzg, dya, dys, dpp, dxb, d_bg = _merge_bwd(dx2, gt, y_att, y_sgu, gwm)
    g_late = g2 + [_weight_grad(dpp, y_att, "dw_branch_att", 0, D), _weight_grad(dpp, y_sgu, "dw_branch_sgu", D, D),
                   _weight_grad(merged, dxb, "dw_out")]
    late = ("ffn2_w_gate", "ffn2_w_up", "ffn2_w_down", "w_branch_att", "w_branch_sgu", "w_out")
    (dzs, d_wm, d_bs, d_lng, d_lnb), gots_late = _sgu_bwd(zs, dys, lng, lnb, wm2, wmt2, bsx, [_SiblingSwap(g_late)])
    sums_late = _pair_sums(g_late, gots_late, "pair_sums_late")
    (dq, dk, dv, d_bias), parts_late = _att_bwd(q, k, v, bias, dya, [_ChipScatter(sums_late)])
    big = updates(parts_late, sums_late, late)
    d_rel = _rel_bias_grad(d_bias)
    dz = jnp.concatenate([dq, dk, dv, dzs, dzg], axis=1)
    small = {"norm_ffn2": d_n2, "norm_final": d_nf, "b_gate": d_bg, "sgu_ln_g": d_lng, "sgu_ln_b": d_lnb, "sgu_b_s": d_bs,
             "rel_bias": d_rel, "sgu_w_s": d_wm.reshape(_SMALL_2D["sgu_w_s"])}
    g_in, (*early_parts, loss_parts) = _weight_grad(dz, h2, "dw_in", comms=[_AllToAll(list(small.values()) + [loss])])
    sums_in = chip_sums([g_in], "w_in")
    (dx1, d_nm), parts_in = _mix_proj_bwd(dz, x1, nm, dx2, gwm, [_ChipScatter(sums_in)])
    big.update(updates(parts_in, sums_in, ("w_in",)))

    dab, sb, dfb, hb = _ffn_bwd_hidden(x, n1, ab1, dx1, gw1, 2 * R_FF, "ffn1_bwd_hidden")
    sums_d = chip_sums([_weight_grad(sb, dfb, "ffn1_dw_down")], "ffn1_down")
    g_gate, parts_d = _weight_grad(dab, hb, "ffn1_dw_gate", 0, F, [_ChipScatter(sums_d)])
    sums_g = chip_sums([g_gate], "ffn1_gate")
    g_up, (parts_g, nm_parts) = _weight_grad(dab, hb, "ffn1_dw_up", F, F, [_ChipScatter(sums_g), _AllToAll([d_nm])])
    sums_u = chip_sums([g_up], "ffn1_up")
    (dx0, d_n1), parts_u = _ffn_bwd_input(x, n1, dab, dx1, gw1, "ffn1_bwd", [_ChipScatter(sums_u)])
    (n1_parts,) = _comm_only([_AllToAll([d_n1])], "gather_norm_ffn1")
    big.update(updates(parts_d + [parts_g] + parts_u, sums_d + sums_g + sums_u, ("ffn1_w_down", "ffn1_w_gate", "ffn1_w_up")))
    out_s, loss_sum = _adamw_small(dict(zip(small, early_parts), norm_mix=nm_parts, norm_ffn1=n1_parts), loss_parts, p)
    return dx0, loss_sum[0, 0], [{**{n: four[i] for n, four in big.items()}, **s} for i, s in enumerate(out_s)]


_OUT_ORDER = ("norm_ffn1", "ffn1_w_gate", "ffn1_w_up", "ffn1_w_down", "norm_mix", "w_in", "b_gate", "rel_bias", "sgu_ln_g", "sgu_ln_b",
              "sgu_w_s", "sgu_b_s", "w_branch_att", "w_branch_sgu", "w_out", "norm_ffn2", "ffn2_w_gate", "ffn2_w_up", "ffn2_w_down",
              "norm_final")


def kernel(x, norm_ffn1, ffn1_w_gate, ffn1_w_up, ffn1_w_down, norm_mix, w_in, b_gate, rel_bias, sgu_ln_g, sgu_ln_b, sgu_w_s, sgu_b_s, w_branch_att, w_branch_sgu, w_out, norm_ffn2, ffn2_w_gate, ffn2_w_up, ffn2_w_down, norm_final, loss_target, m_norm_ffn1, m_ffn1_w_gate, m_ffn1_w_up, m_ffn1_w_down, m_norm_mix, m_w_in, m_b_gate, m_rel_bias, m_sgu_ln_g, m_sgu_ln_b, m_sgu_w_s, m_sgu_b_s, m_w_branch_att, m_w_branch_sgu, m_w_out, m_norm_ffn2, m_ffn2_w_gate, m_ffn2_w_up, m_ffn2_w_down, m_norm_final, v_norm_ffn1, v_ffn1_w_gate, v_ffn1_w_up, v_ffn1_w_down, v_norm_mix, v_w_in, v_b_gate, v_rel_bias, v_sgu_ln_g, v_sgu_ln_b, v_sgu_w_s, v_sgu_b_s, v_w_branch_att, v_w_branch_sgu, v_w_out, v_norm_ffn2, v_ffn2_w_gate, v_ffn2_w_up, v_ffn2_w_down, v_norm_final):
    args = dict(locals())
    dx, loss, outs = _step(x[0], loss_target[0], {pre + n: args[pre + n] for pre in ("", "m_", "v_") for n in _OUT_ORDER})
    return (loss, dx[None], *[o[n] for o in outs for n in _OUT_ORDER])
```

```python
import functools

import numpy as np
import jax
import jax.numpy as jnp
from jax import lax
from jax.experimental import pallas as pl
from jax.experimental.pallas import tpu as pltpu

F32 = jnp.float32
BF16 = jnp.bfloat16

N_DEV = 8
D = 1024
F = 2816
D_ATT = 512
D_SGU = 512
D_IN = 4608
HEADS = 8
CHUNK = 64
N_LEFT = 8
REL_CLIP = 256
N_REL = 2 * REL_CLIP + 1
SGU_BLOCK = 128
EPS = 1e-6
NEG_INF = -1e30
QB = 256
KW = 3 * QB

R_FF, R_IN, R_BR, R_WO = F // N_DEV, D_IN // N_DEV, D // N_DEV, D // N_DEV
OFF_IN, OFF_BR, OFF_WO = 0, R_IN, R_IN + R_BR
FFN_ROWS = 3 * R_FF
MIX_ROWS = R_IN + R_BR + R_WO

FC = 256
TM = 512
VMEM_LIMIT = 56 * 1024 * 1024

ADAM_LR, ADAM_B1, ADAM_B2, ADAM_EPS, ADAM_WD, ADAM_STEP = 0.001, 0.9, 0.999, 1e-08, 0.01, 10

MESH = pl.DeviceIdType.MESH
ANY = pl.BlockSpec(memory_space=pl.ANY)


def _nt(a, b):
    return lax.dot_general(a, b, (((1,), (1,)), ((), ())), preferred_element_type=F32)


def _tn(a, b):
    return lax.dot_general(a, b, (((0,), (0,)), ((), ())), preferred_element_type=F32)


def _nn(a, b):
    return jnp.dot(a, b, preferred_element_type=F32)


def _cparams(sem=None):
    return pltpu.CompilerParams(dimension_semantics=sem, vmem_limit_bytes=VMEM_LIMIT)


def _load_rows(gw_ref, dst, off, rows, sems):
    copies = [pltpu.make_async_copy(gw_ref.at[k, pl.ds(off, rows), :], dst.at[pl.ds(k * rows, rows), :], sems.at[k])
              for k in range(N_DEV)]
    for cp in copies:
        cp.start()
    return copies


def _rms(xv):
    r = lax.rsqrt(jnp.mean(xv * xv, axis=-1, keepdims=True) + EPS)
    return xv * r, r


def _rms_bwd(dh, xn, r, gain):
    dxn = dh * gain
    dx = r * (dxn - xn * jnp.mean(dxn * xn, axis=-1, keepdims=True))
    return dx, jnp.sum(dh * xn, axis=0, keepdims=True)


def _gelu(x):
    t = jnp.tanh(0.7978845608028654 * (x + 0.044715 * x * x * x))
    return 0.5 * x * (1.0 + t), t


def _gelu_grad(x, t):
    return 0.5 * (1.0 + t) + 0.5 * x * (1.0 - t * t) * 0.7978845608028654 * (1.0 + 3.0 * 0.044715 * x * x)


def _place():
    x, y, cc = lax.axis_index("x"), lax.axis_index("y"), lax.axis_index("c")
    return x, y, cc, [(1 - x, y), (x, 1 - y), (1 - x, 1 - y)]


class _Gather:
    def __init__(self, shard):
        self.inputs = [shard]
        self.out_shape = [jax.ShapeDtypeStruct((N_DEV,) + shard.shape, shard.dtype)]
        self.scratch = [pltpu.SemaphoreType.DMA((7,)), pltpu.SemaphoreType.DMA((7,)), pltpu.SemaphoreType.DMA]

    def _copies(self, ins, outs, scr):
        (x_ref,), (out_ref,), (send_sems, recv_sems, local_sem) = ins, outs, scr
        x, y, cc, chips = _place()

        def slab(px, py, pc):
            return out_ref.at[4 * px + 2 * py + pc]

        def copy(k, block, to, src=None):
            return pltpu.make_async_remote_copy(
                src_ref=slab(*block) if src is None else src, dst_ref=slab(*block),
                send_sem=send_sems.at[k], recv_sem=recv_sems.at[k], device_id=to, device_id_type=MESH)

        me, sibling = (x, y, cc), (x, y, 1 - cc)
        mine = pltpu.make_async_copy(x_ref, slab(*me), local_sem)
        first = [copy(0, me, sibling, src=x_ref)] + [copy(1 + j, me, (*chip, cc), src=x_ref) for j, chip in enumerate(chips)]
        landed = [copy(1 + j, (*chip, cc), me) for j, chip in enumerate(chips)]
        passed = [copy(4 + j, (*chip, cc), sibling) for j, chip in enumerate(chips)]
        from_sibling = [copy(0, sibling, me)] + [copy(4 + j, (*chip, 1 - cc), me) for j, chip in enumerate(chips)]
        return mine, first, landed, passed, from_sibling

    def begin(self, *refs):
        mine, first, _, _, _ = self._copies(*refs)
        mine.start()
        for cp in first:
            cp.start()

    def relay(self, *refs):
        _, _, landed, passed, _ = self._copies(*refs)
        for arrived, onward in zip(landed, passed):
            arrived.wait_recv()
            onward.start()

    def end(self, *refs):
        mine, first, _, passed, from_sibling = self._copies(*refs)
        for cp in from_sibling:
            cp.wait_recv()
        for cp in first + passed:
            cp.wait_send()
        mine.wait()


class _Direct:
    def begin(self, *refs):
        keep, give = self._copies(*refs)
        for cp in keep + give:
            cp.start()

    def relay(self, *refs):
        pass

    def end(self, *refs):
        keep, give = self._copies(*refs)
        for cp in give:
            cp.wait_recv()
        for cp in give:
            cp.wait_send()
        for cp in keep:
            cp.wait()


class _SiblingSwap(_Direct):
    def __init__(self, grads):
        n = len(grads)
        self.inputs = list(grads)
        self.out_shape = [jax.ShapeDtypeStruct((4,) + g.shape[1:], g.dtype) for g in grads]
        self.scratch = [pltpu.SemaphoreType.DMA((n, 4)), pltpu.SemaphoreType.DMA((n, 4))]

    def _copies(self, ins, outs, scr):
        send_sems, recv_sems = scr
        x, y, cc, _ = _place()
        return [], [pltpu.make_async_remote_copy(src_ref=g_ref.at[2 * j + 1 - cc], dst_ref=got_ref.at[j], send_sem=send_sems.at[i, j],
                                                 recv_sem=recv_sems.at[i, j], device_id=(x, y, 1 - cc), device_id_type=MESH)
                    for i, (g_ref, got_ref) in enumerate(zip(ins, outs)) for j in range(4)]


class _ChipScatter(_Direct):
    def __init__(self, sums):
        n = len(sums)
        self.inputs = list(sums)
        self.out_shape = [jax.ShapeDtypeStruct((3,) + s.shape[1:], s.dtype) for s in sums]
        self.scratch = [pltpu.SemaphoreType.DMA((n, 3)), pltpu.SemaphoreType.DMA((n, 3))]

    def _copies(self, ins, outs, scr):
        send_sems, recv_sems = scr
        _, _, cc, chips = _place()
        return [], [pltpu.make_async_remote_copy(src_ref=s_ref.at[2 * px + py], dst_ref=got_ref.at[j], send_sem=send_sems.at[i, j],
                                                 recv_sem=recv_sems.at[i, j], device_id=(px, py, cc), device_id_type=MESH)
                    for i, (s_ref, got_ref) in enumerate(zip(ins, outs)) for j, (px, py) in enumerate(chips)]


class _AllToAll(_Direct):
    def __init__(self, blocks):
        n = len(blocks)
        self.inputs = list(blocks)
        self.out_shape = [jax.ShapeDtypeStruct((N_DEV,) + b.shape, b.dtype) for b in blocks]
        self.scratch = [pltpu.SemaphoreType.DMA((n, 7)), pltpu.SemaphoreType.DMA((n, 7)), pltpu.SemaphoreType.DMA((n,))]

    def _copies(self, ins, outs, scr):
        send_sems, recv_sems, local_sems = scr
        x, y, cc, _ = _place()
        me = 4 * x + 2 * y + cc
        keep = [pltpu.make_async_copy(b_ref, got_ref.at[me], local_sems.at[i]) for i, (b_ref, got_ref) in enumerate(zip(ins, outs))]
        give = [pltpu.make_async_remote_copy(src_ref=b_ref, dst_ref=got_ref.at[me], send_sem=send_sems.at[i, k - 1],
                                             recv_sem=recv_sems.at[i, k - 1],
                                             device_id=(x ^ ((k >> 2) & 1), y ^ ((k >> 1) & 1), cc ^ (k & 1)), device_id_type=MESH)
                for i, (b_ref, got_ref) in enumerate(zip(ins, outs)) for k in range(1, N_DEV)]
        return keep, give


def _split_refs(refs, counts):
    out, pos = [], 0
    for n in counts:
        out.append(list(refs[pos:pos + n]))
        pos += n
    return out


def _bind(comms, c_in, c_out, c_scr):
    ins = _split_refs(c_in, [len(c.inputs) for c in comms])
    outs = _split_refs(c_out, [len(c.out_shape) for c in comms])
    scr = _split_refs(c_scr, [len(c.scratch) for c in comms])
    return [(c, (i, o, s)) for c, i, o, s in zip(comms, ins, outs, scr)]


def _call(body, *, name, grid, in_specs, out_specs, out_shape, scratch_shapes, args, comms=()):
    c_in = [a for c in comms for a in c.inputs]
    c_out = [s for c in comms for s in c.out_shape]
    c_scr = [s for c in comms for s in c.scratch]
    counts = [len(in_specs), len(c_in), len(out_shape), len(c_out), len(scratch_shapes), len(c_scr)]

    def full(*refs):
        ins, cin, outs, cout, scr, cscr = _split_refs(refs, counts)
        bound = _bind(comms, cin, cout, cscr)
        if comms:
            first = functools.reduce(jnp.logical_and, [pl.program_id(ax) == 0 for ax in range(len(grid))])
            last = functools.reduce(jnp.logical_and, [pl.program_id(ax) == n - 1 for ax, n in enumerate(grid)])

            @pl.when(first)
            def _():
                for c, r in bound:
                    c.begin(*r)

            @pl.when(last)
            def _():
                for c, r in bound:
                    c.relay(*r)

        body(*ins, *outs, *scr)
        if comms:
            @pl.when(last)
            def _():
                for c, r in bound:
                    c.end(*r)

    res = pl.pallas_call(
        full, name=name, grid=grid,
        in_specs=list(in_specs) + [ANY] * len(c_in), out_specs=list(out_specs) + [ANY] * len(c_out),
        out_shape=list(out_shape) + c_out, scratch_shapes=list(scratch_shapes) + c_scr,
        compiler_params=_cparams(("arbitrary",) * len(grid)),
    )(*args, *c_in)
    return list(res[:len(out_shape)]), list(res[len(out_shape):])


def _comm_only(comms, name):
    c_in = [a for c in comms for a in c.inputs]
    c_out = [s for c in comms for s in c.out_shape]
    c_scr = [s for c in comms for s in c.scratch]

    def full(*refs):
        cin, cout, cscr = _split_refs(refs, [len(c_in), len(c_out), len(c_scr)])
        bound = _bind(comms, cin, cout, cscr)
        for phase in ("begin", "relay", "end"):
            for c, r in bound:
                getattr(c, phase)(*r)

    return list(pl.pallas_call(full, name=name, in_specs=[ANY] * len(c_in), out_specs=[ANY] * len(c_out), out_shape=c_out,
                               scratch_shapes=c_scr)(*c_in))


def _my_index(*axes_and_weights):
    return sum(w * lax.axis_index(a) for a, w in axes_and_weights).astype(jnp.int32).reshape(1)


def _pair_sums(grads, gots, name):
    n = len(grads)

    def body(c_ref, *refs):
        for a_ref, b_ref, o_ref in zip(refs[:n], refs[n:2 * n], refs[2 * n:]):
            o_ref[...] = (a_ref[...].astype(F32) + b_ref[...].astype(F32)).astype(BF16)

    def tile(g):
        return pl.BlockSpec((1,) + g.shape[1:], lambda j, c_ref: (j, 0, 0))

    def mine(g):
        return pl.BlockSpec((1, None) + g.shape[1:], lambda j, c_ref: (j, c_ref[0], 0, 0))

    return list(pl.pallas_call(
        body, name=name,
        grid_spec=pltpu.PrefetchScalarGridSpec(num_scalar_prefetch=1, grid=(4,), in_specs=[mine(g) for g in gots] + [tile(g) for g in gots],
                                               out_specs=[tile(g) for g in gots]),
        out_shape=[jax.ShapeDtypeStruct(g.shape, BF16) for g in gots],
        compiler_params=_cparams(("arbitrary",)))(_my_index(("c", 1)), *[g.reshape((4, 2) + g.shape[1:]) for g in grads], *gots))


def _ffn_fwd(x, gain, gw_gu, gw_d, off_d, name, comms=(), head=None):
    t = x.shape[0]
    n_head = 0 if head is None else 2

    def body(x_ref, g_ref, gu_ref, d_ref, *refs):
        head_in, (o_ref, ab_ref, h_ref), head_out = refs[:n_head], refs[n_head:n_head + 3], refs[n_head + 3:2 * n_head + 3]
        wg, wu, wd, s_scr, sems = refs[2 * n_head + 3:]

        @pl.when(pl.program_id(0) == 0)
        def _():
            cps = _load_rows(gu_ref, wg, 0, R_FF, sems.at[0]) + _load_rows(gu_ref, wu, R_FF, R_FF, sems.at[1]) \
                + _load_rows(d_ref, wd, off_d, R_FF, sems.at[2])
            for o in head_out:
                o[...] = jnp.zeros_like(o)
            for cp in cps:
                cp.wait()

        xv = x_ref[...]
        xn, _ = _rms(xv)
        h = (xn * g_ref[...]).astype(BF16)
        h_ref[...] = h
        for c in range(F // FC):
            rows = pl.ds(c * FC, FC)
            a = _nt(h, wg[rows, :])
            b = _nt(h, wu[rows, :])
            ab_ref[:, c * FC:(c + 1) * FC] = a.astype(BF16)
            ab_ref[:, F + c * FC:F + (c + 1) * FC] = b.astype(BF16)
            s_scr[:, c * FC:(c + 1) * FC] = (a * jax.nn.sigmoid(a) * b).astype(BF16)
        out = xv + 0.5 * _nn(s_scr[...], wd[...])
        if head is None:
            o_ref[...] = out
        else:
            (gf_ref, t_ref), (dg_ref, loss_ref) = head_in, head_out
            gain_f = gf_ref[...]
            yn, r = _rms(out)
            err = yn * gain_f - t_ref[...]
            loss_ref[...] += 0.5 * jnp.sum(jnp.mean(err * err, axis=-1, keepdims=True), axis=0, keepdims=True)
            o_ref[...], dg = _rms_bwd(err * (1.0 / D), yn, r, gain_f)
            dg_ref[...] += dg

    tile = pl.BlockSpec((TM, D), lambda i: (i, 0))
    row = pl.BlockSpec((1, D), lambda i: (0, 0))
    head_specs = [] if head is None else [row, pl.BlockSpec((1, 128), lambda i: (0, 0))]
    head_shapes = [] if head is None else [jax.ShapeDtypeStruct((1, D), F32), jax.ShapeDtypeStruct((1, 128), F32)]
    res, got = _call(
        body, name=name, grid=(t // TM,),
        in_specs=[tile, row, ANY, ANY] + ([] if head is None else [row, tile]),
        out_specs=[tile, pl.BlockSpec((TM, 2 * F), lambda i: (i, 0)), tile] + head_specs,
        out_shape=[jax.ShapeDtypeStruct((t, D), F32), jax.ShapeDtypeStruct((t, 2 * F), BF16), jax.ShapeDtypeStruct((t, D), BF16)] + head_shapes,
        scratch_shapes=[pltpu.VMEM((F, D), BF16)] * 3 + [pltpu.VMEM((TM, F), BF16), pltpu.SemaphoreType.DMA((3, N_DEV))],
        args=(x, gain, gw_gu, gw_d) + (() if head is None else tuple(head)), comms=comms)
    return (res[0], res[1], res[2], got, *res[3:])


def _ffn_bwd_hidden(ab, dout, gw_d, off_d, name):
    t = ab.shape[0]

    def tile(w):
        return pl.BlockSpec((TM, w), lambda i: (i, 0))

    def hidden(ab_ref, do_ref, d_ref, dab_ref, s_ref, df_ref, wd, sems):
        @pl.when(pl.program_id(0) == 0)
        def _():
            for cp in _load_rows(d_ref, wd, off_d, R_FF, sems):
                cp.wait()

        df = (0.5 * do_ref[...]).astype(BF16)
        df_ref[...] = df
        for c in range(F // FC):
            a = ab_ref[:, c * FC:(c + 1) * FC].astype(F32)
            b = ab_ref[:, F + c * FC:F + (c + 1) * FC].astype(F32)
            sg = jax.nn.sigmoid(a)
            sl = a * sg
            ds = _nt(df, wd[pl.ds(c * FC, FC), :])
            dab_ref[:, c * FC:(c + 1) * FC] = (ds * b * (sg * (1.0 + a * (1.0 - sg)))).astype(BF16)
            dab_ref[:, F + c * FC:F + (c + 1) * FC] = (ds * sl).astype(BF16)
            s_ref[:, c * FC:(c + 1) * FC] = (sl * b).astype(BF16)

    res, _ = _call(
        hidden, name=name, grid=(t // TM,),
        in_specs=[tile(2 * F), tile(D), ANY], out_specs=[tile(2 * F), tile(F), tile(D)],
        out_shape=[jax.ShapeDtypeStruct((t, 2 * F), BF16), jax.ShapeDtypeStruct((t, F), BF16), jax.ShapeDtypeStruct((t, D), BF16)],
        scratch_shapes=[pltpu.VMEM((F, D), BF16), pltpu.SemaphoreType.DMA((N_DEV,))],
        args=(ab, dout, gw_d))
    return res


def _ffn_bwd_input(x, gain, dab, dout, gw_gu, name, comms=()):
    t = x.shape[0]

    def body(x_ref, g_ref, dab_ref, do_ref, gu_ref, dx_ref, dg_ref, wg, wu, sems):
        @pl.when(pl.program_id(0) == 0)
        def _():
            cps = _load_rows(gu_ref, wg, 0, R_FF, sems.at[0]) + _load_rows(gu_ref, wu, R_FF, R_FF, sems.at[1])
            dg_ref[...] = jnp.zeros_like(dg_ref)
            for cp in cps:
                cp.wait()

        gain_v = g_ref[...]
        xn, r = _rms(x_ref[...])
        dh = _nn(dab_ref[:, :F], wg[...]) + _nn(dab_ref[:, F:], wu[...])
        dxn, dg = _rms_bwd(dh, xn, r, gain_v)
        dg_ref[...] += dg
        dx_ref[...] = do_ref[...] + dxn

    def tile(w):
        return pl.BlockSpec((TM, w), lambda i: (i, 0))

    row = pl.BlockSpec((1, D), lambda i: (0, 0))
    return _call(
        body, name=name, grid=(t // TM,),
        in_specs=[tile(D), row, tile(2 * F), tile(D), ANY], out_specs=[tile(D), row],
        out_shape=[jax.ShapeDtypeStruct((t, D), F32), jax.ShapeDtypeStruct((1, D), F32)],
        scratch_shapes=[pltpu.VMEM((F, D), BF16)] * 2 + [pltpu.SemaphoreType.DMA((2, N_DEV))],
        args=(x, gain, dab, dout, gw_gu), comms=comms)


def _weight_grad(a, b, name, col_off=0, m=None, comms=None, tmm=256):
    t = a.shape[0]
    m = a.shape[1] if m is None else m
    n = b.shape[1]
    first = col_off // tmm

    def body(a_ref, b_ref, o_ref):
        o_ref[...] = _tn(a_ref[...], b_ref[...]).astype(BF16)

    (out,), got = _call(
        body, name=name, grid=(m // tmm,),
        in_specs=[pl.BlockSpec((t, tmm), lambda i: (0, first + i)), pl.BlockSpec((t, n), lambda i: (0, 0))],
        out_specs=[pl.BlockSpec((tmm, n), lambda i: (i, 0))],
        out_shape=[jax.ShapeDtypeStruct((m, n), BF16)], scratch_shapes=[], args=(a, b), comms=comms or ())
    out = out.reshape(N_DEV, m // N_DEV, n)
    return out if comms is None else (out, got)


def _mix_proj_fwd(x, gain, b_gate, gw, comms=()):
    t = x.shape[0]

    def body(x_ref, g_ref, bg_ref, gw_ref, q_ref, k_ref, v_ref, zs_ref, gt_ref, h_ref, win, sems):
        @pl.when(pl.program_id(0) == 0)
        def _():
            for cp in _load_rows(gw_ref, win, OFF_IN, R_IN, sems):
                cp.wait()

        xn, _ = _rms(x_ref[...])
        h = (xn * g_ref[...]).astype(BF16)
        h_ref[...] = h
        q_ref[...] = (_nt(h, win[0:512, :]) * 0.125).astype(BF16)
        k_ref[...] = _nt(h, win[512:1024, :]).astype(BF16)
        v_ref[...] = _nt(h, win[1024:1536, :]).astype(BF16)
        for c in range(2):
            zs_ref[:, c * 512:(c + 1) * 512] = _nt(h, win[1536 + c * 512:2048 + c * 512, :]).astype(BF16)
        for c in range(4):
            zg = _nt(h, win[2560 + c * 512:3072 + c * 512, :]) + bg_ref[:, c * 512:(c + 1) * 512]
            gt_ref[:, c * 512:(c + 1) * 512] = jax.nn.sigmoid(zg).astype(BF16)

    def tile(w):
        return pl.BlockSpec((TM, w), lambda i: (i, 0))

    return _call(
        body, name="mix_proj_fwd", grid=(t // TM,),
        in_specs=[tile(D), pl.BlockSpec((1, D), lambda i: (0, 0)), pl.BlockSpec((1, 2 * D), lambda i: (0, 0)), ANY],
        out_specs=[tile(D_ATT), tile(D_ATT), tile(D_ATT), tile(2 * D_SGU), tile(2 * D), tile(D)],
        out_shape=[jax.ShapeDtypeStruct((t, D_ATT), BF16)] * 3 + [jax.ShapeDtypeStruct((t, 2 * D_SGU), BF16),
                                                                   jax.ShapeDtypeStruct((t, 2 * D), BF16),
                                                                   jax.ShapeDtypeStruct((t, D), BF16)],
        scratch_shapes=[pltpu.VMEM((D_IN, D), BF16), pltpu.SemaphoreType.DMA((N_DEV,))],
        args=(x, gain, b_gate, gw), comms=comms)


def _mix_proj_bwd(dz, x, gain, dres, gw, comms=()):
    t = x.shape[0]

    def body(dz_ref, x_ref, g_ref, dr_ref, gw_ref, dx_ref, dg_ref, win, sems):
        @pl.when(pl.program_id(0) == 0)
        def _():
            cps = _load_rows(gw_ref, win, OFF_IN, R_IN, sems)
            dg_ref[...] = jnp.zeros_like(dg_ref)
            for cp in cps:
                cp.wait()

        dh = _nn(dz_ref[...], win[...])
        xn, r = _rms(x_ref[...])
        dxn, dg = _rms_bwd(dh, xn, r, g_ref[...])
        dg_ref[...] += dg
        dx_ref[...] = dr_ref[...] + dxn

    def tile(w):
        return pl.BlockSpec((TM, w), lambda i: (i, 0))

    row = pl.BlockSpec((1, D), lambda i: (0, 0))
    return _call(
        body, name="mix_proj_bwd", grid=(t // TM,),
        in_specs=[tile(D_IN), tile(D), row, tile(D), ANY], out_specs=[tile(D), row],
        out_shape=[jax.ShapeDtypeStruct((t, D), F32), jax.ShapeDtypeStruct((1, D), F32)],
        scratch_shapes=[pltpu.VMEM((D_IN, D), BF16), pltpu.SemaphoreType.DMA((N_DEV,))],
        args=(dz, x, gain, dres, gw), comms=comms)


SKEW_W = KW + QB
N_CAP = 2 * QB - REL_CLIP + 1


def _band_bias(rel_bias):
    cap = rel_bias[:, 2 * REL_CLIP:]
    diag = jnp.concatenate([jnp.broadcast_to(cap, (HEADS, N_CAP)), rel_bias[:, 2 * REL_CLIP - 1::-1],
                            jnp.broadcast_to(cap, (HEADS, SKEW_W - N_CAP - 2 * REL_CLIP))], axis=1)

    def body(d_ref, o_ref):
        lag = lax.broadcasted_iota(jnp.int32, (QB, KW), 1) // CHUNK - lax.broadcasted_iota(jnp.int32, (QB, KW), 0) // CHUNK
        band = (lag >= 0) & (lag <= N_LEFT)
        for h in range(HEADS):
            rows = jnp.broadcast_to(d_ref[h:h + 1, :], (QB, SKEW_W))
            o_ref[h] = jnp.where(band, pltpu.roll(rows, 0, 1, stride=1, stride_axis=0)[:, :KW], NEG_INF)

    return pl.pallas_call(body, name="band_bias", out_shape=jax.ShapeDtypeStruct((HEADS, QB, KW), F32))(diag)


def _att_specs():
    qspec = pl.BlockSpec((QB, D_ATT), lambda g: (g, 0))
    kspecs = [pl.BlockSpec((QB, D_ATT), lambda g: (jnp.maximum(g - 2, 0), 0)),
              pl.BlockSpec((QB, D_ATT), lambda g: (jnp.maximum(g - 1, 0), 0)), qspec]
    bspec = pl.BlockSpec((HEADS, QB, KW), lambda g: (0, 0, 0))
    return qspec, kspecs, bspec


def _att_probs(qm, kp, bias, valid):
    s = jnp.where(valid, _nt(qm, kp) + bias, NEG_INF)
    e = jnp.exp(s - jnp.max(s, axis=-1, keepdims=True))
    return e / jnp.sum(e, axis=-1, keepdims=True)


def _att_valid():
    g = pl.program_id(0)
    blk = lax.broadcasted_iota(jnp.int32, (QB, KW), 1) // QB
    return (blk + g) >= 2


def _att_fwd(q, k, v, bias, comms=()):
    t = q.shape[0]

    def body(q_ref, k0, k1, k2, v0, v1, v2, b_ref, y_ref):
        valid = _att_valid()
        first = lax.broadcasted_iota(jnp.int32, (1, 128), 1) < 64
        for p in range(HEADS // 2):
            lanes = slice(p * 128, (p + 1) * 128)
            qp = q_ref[:, lanes]
            kp = jnp.concatenate([k0[:, lanes], k1[:, lanes], k2[:, lanes]], axis=0)
            vp = jnp.concatenate([v0[:, lanes], v1[:, lanes], v2[:, lanes]], axis=0)
            out = jnp.zeros((QB, 128), F32)
            for hh in range(2):
                mask = first if hh == 0 else jnp.logical_not(first)
                pr = _att_probs(jnp.where(mask, qp, 0), kp, b_ref[2 * p + hh], valid)
                out = out + _nn(pr.astype(BF16), jnp.where(mask, vp, 0))
            y_ref[:, lanes] = out.astype(BF16)

    qspec, kspecs, bspec = _att_specs()
    (out,), got = _call(
        body, name="att_fwd", grid=(t // QB,),
        in_specs=[qspec] + kspecs + kspecs + [bspec], out_specs=[qspec],
        out_shape=[jax.ShapeDtypeStruct((t, D_ATT), BF16)], scratch_shapes=[],
        args=(q, k, k, k, v, v, v, bias), comms=comms)
    return out, got


def _att_bwd(q, k, v, bias, dy, comms=()):
    t = q.shape[0]
    n_blocks = t // QB

    def body(q_ref, k0, k1, k2, v0, v1, v2, b_ref, dy_ref, dq_ref, dk_ref, dv_ref, db_ref, dk_acc, dv_acc):
        g = pl.program_id(0)

        @pl.when(g == 0)
        def _():
            db_ref[...] = jnp.zeros_like(db_ref)
            dk_acc[...] = jnp.zeros_like(dk_acc)
            dv_acc[...] = jnp.zeros_like(dv_acc)

        valid = _att_valid()
        first = lax.broadcasted_iota(jnp.int32, (1, 128), 1) < 64
        for p in range(HEADS // 2):
            lanes = slice(p * 128, (p + 1) * 128)
            qp = q_ref[:, lanes]
            dyp = dy_ref[:, lanes]
            kp = jnp.concatenate([k0[:, lanes], k1[:, lanes], k2[:, lanes]], axis=0)
            vp = jnp.concatenate([v0[:, lanes], v1[:, lanes], v2[:, lanes]], axis=0)
            dq = jnp.zeros((QB, 128), F32)
            dk = jnp.zeros((KW, 128), F32)
            dv = jnp.zeros((KW, 128), F32)
            for hh in range(2):
                mask = first if hh == 0 else jnp.logical_not(first)
                qm = jnp.where(mask, qp, 0)
                dym = jnp.where(mask, dyp, 0)
                pr = _att_probs(qm, kp, b_ref[2 * p + hh], valid)
                dp = _nt(dym, vp)
                ds = pr * (dp - jnp.sum(dp * pr, axis=-1, keepdims=True))
                db_ref[2 * p + hh] += ds
                dsb = ds.astype(BF16)
                dq = dq + _nn(dsb, jnp.where(mask, kp, 0))
                dk = dk + _tn(dsb, qm)
                dv = dv + _tn(pr.astype(BF16), dym)
            dq_ref[:, lanes] = (dq * 0.125).astype(BF16)
            for j in range(3):
                rows = pl.ds(pl.multiple_of(jnp.maximum(g - 2 + j, 0) * QB, QB), QB)
                dk_acc[rows, lanes] += dk[j * QB:(j + 1) * QB]
                dv_acc[rows, lanes] += dv[j * QB:(j + 1) * QB]

        @pl.when(g == n_blocks - 1)
        def _():
            dk_ref[...] = dk_acc[...].astype(BF16)
            dv_ref[...] = dv_acc[...].astype(BF16)

    qspec, kspecs, bspec = _att_specs()
    full = pl.BlockSpec((t, D_ATT), lambda g: (0, 0))
    return _call(
        body, name="att_bwd", grid=(n_blocks,),
        in_specs=[qspec] + kspecs + kspecs + [bspec, qspec], out_specs=[qspec, full, full, bspec],
        out_shape=[jax.ShapeDtypeStruct((t, D_ATT), BF16)] * 3 + [jax.ShapeDtypeStruct((HEADS, QB, KW), F32)],
        scratch_shapes=[pltpu.VMEM((t, D_ATT), F32)] * 2,
        args=(q, k, k, k, v, v, v, bias, dy), comms=comms)


def _rel_bias_grad(dbias):
    def body(db_ref, cs_ref, tot_ref):
        lane = lax.broadcasted_iota(jnp.int32, (1, SKEW_W), 1)
        capped = (lane < N_CAP) | (lane > KW)
        pad = jnp.zeros((8, QB), F32)
        for h in range(HEADS):
            z = jnp.concatenate([db_ref[h, 0:8, :], pad], axis=1)
            for a in range(1, QB // 8):
                z = z + pltpu.roll(jnp.concatenate([db_ref[h, 8 * a:8 * a + 8, :], pad], axis=1), SKEW_W - 8 * a, 1)
            cs = z[0:1, :]
            for b in range(1, 8):
                cs = cs + pltpu.roll(z[b:b + 1, :], SKEW_W - b, 1)
            cs_ref[h:h + 1, :] = cs
            tot_ref[h:h + 1, :] = jnp.broadcast_to(jnp.sum(jnp.where(capped, cs, 0.0), axis=1, keepdims=True), (1, 128))

    cs, tot = pl.pallas_call(
        body, name="rel_bias_grad",
        out_shape=[jax.ShapeDtypeStruct((HEADS, SKEW_W), F32), jax.ShapeDtypeStruct((HEADS, 128), F32)],
    )(dbias)
    return jnp.concatenate([cs[:, KW:N_CAP - 1:-1], tot[:, :1]], axis=1)


def _sgu_mask():
    pos = np.arange(SGU_BLOCK)
    return (pos[:, None] // CHUNK) >= (pos[None, :] // CHUNK)


def _group_stack(blk, first):
    return jnp.concatenate([jnp.where(first, blk, 0), jnp.where(first, 0, blk)], axis=0)


def _sgu_norm(zs_ref, lng, lnb):
    zs = zs_ref[...].astype(F32)
    ga, th = _gelu(zs)
    u, vs = ga[:, :D_SGU], ga[:, D_SGU:]
    mu = jnp.mean(vs, axis=-1, keepdims=True)
    cen = vs - mu
    rstd = lax.rsqrt(jnp.mean(cen * cen, axis=-1, keepdims=True) + EPS)
    xhat = cen * rstd
    return zs, th, u, xhat, rstd, xhat * lng + lnb


def _sgu_mix(vb, wm2_ref, bsx, s_ref):
    first = lax.broadcasted_iota(jnp.int32, (1, 128), 1) < 64
    for n in range(TM // SGU_BLOCK):
        for p in range(4):
            blk = vb[n * 128:(n + 1) * 128, p * 128:(p + 1) * 128]
            s_ref[n * 128:(n + 1) * 128, p * 128:(p + 1) * 128] = _nn(wm2_ref[p], _group_stack(blk, first)) + bsx[:, p * 128:(p + 1) * 128]


def _merge_fwd(x, zs, gt, y_att, lng, lnb, wm2, bsx, gw):
    t = x.shape[0]

    def body(x_ref, zs_ref, gt_ref, ya_ref, lng_ref, lnb_ref, wm2_ref, bsx_ref, gw_ref, xo_ref, ys_ref, mg_ref,
             wbr, wo, s_scr, sems):
        @pl.when(pl.program_id(0) == 0)
        def _():
            for cp in _load_rows(gw_ref, wbr, OFF_BR, R_BR, sems.at[0]) + _load_rows(gw_ref, wo, OFF_WO, R_WO, sems.at[1]):
                cp.wait()

        _, _, u, _, _, vsn = _sgu_norm(zs_ref, lng_ref[...], lnb_ref[...])
        _sgu_mix(vsn.astype(BF16), wm2_ref, bsx_ref[...], s_scr)
        ys = (u * s_scr[...]).astype(BF16)
        ys_ref[...] = ys
        pa = _nt(ya_ref[...], wbr[:, :D_ATT])
        ps = _nt(ys, wbr[:, D_ATT:])
        mg = (gt_ref[:, :D].astype(F32) * pa + gt_ref[:, D:].astype(F32) * ps).astype(BF16)
        mg_ref[...] = mg
        xo_ref[...] = x_ref[...] + _nn(mg, wo[...])

    def tile(w):
        return pl.BlockSpec((TM, w), lambda i: (i, 0))

    def const(shape):
        return pl.BlockSpec(shape, lambda i: (0,) * len(shape))

    return pl.pallas_call(
        body, name="merge_fwd", grid=(t // TM,),
        in_specs=[tile(D), tile(2 * D_SGU), tile(2 * D), tile(D_ATT), const((1, D_SGU)), const((1, D_SGU)),
                  const((4, 128, 256)), const((128, D_SGU)), ANY],
        out_specs=[tile(D), tile(D_SGU), tile(D)],
        out_shape=[jax.ShapeDtypeStruct((t, D), F32), jax.ShapeDtypeStruct((t, D_SGU), BF16), jax.ShapeDtypeStruct((t, D), BF16)],
        scratch_shapes=[pltpu.VMEM((D, D), BF16), pltpu.VMEM((D, D), BF16), pltpu.VMEM((TM, D_SGU), F32),
                        pltpu.SemaphoreType.DMA((2, N_DEV))],
        compiler_params=_cparams(("arbitrary",)),
    )(x, zs, gt, y_att, lng, lnb, wm2, bsx, gw)


def _merge_bwd(dx, gt, y_att, y_sgu, gw):
    t = dx.shape[0]

    def body(dx_ref, gt_ref, ya_ref, ys_ref, gw_ref, dzg_ref, dya_ref, dys_ref, dpp_ref, dxb_ref, dbg_ref, wbr, wo, sems):
        @pl.when(pl.program_id(0) == 0)
        def _():
            cps = _load_rows(gw_ref, wbr, OFF_BR, R_BR, sems.at[0]) + _load_rows(gw_ref, wo, OFF_WO, R_WO, sems.at[1])
            dbg_ref[...] = jnp.zeros_like(dbg_ref)
            for cp in cps:
                cp.wait()

        dxb = dx_ref[...].astype(BF16)
        dxb_ref[...] = dxb
        dm = _nt(dxb, wo[...])
        for half, y_ref, w in ((0, ya_ref, wbr.at[:, :D_ATT]), (1, ys_ref, wbr.at[:, D_ATT:])):
            cols = slice(half * D, (half + 1) * D)
            gate = gt_ref[:, cols].astype(F32)
            branch = _nt(y_ref[...], w[...])
            dzg = dm * branch * gate * (1.0 - gate)
            dbg_ref[:, cols] += jnp.sum(dzg, axis=0, keepdims=True)
            dzg_ref[:, cols] = dzg.astype(BF16)
            dbr = (dm * gate).astype(BF16)
            dpp_ref[:, cols] = dbr
            dy = _nn(dbr, w[...])
            if half == 0:
                dya_ref[...] = dy.astype(BF16)
            else:
                dys_ref[...] = dy

    def tile(w):
        return pl.BlockSpec((TM, w), lambda i: (i, 0))

    return pl.pallas_call(
        body, name="merge_bwd", grid=(t // TM,),
        in_specs=[tile(D), tile(2 * D), tile(D_ATT), tile(D_SGU), ANY],
        out_specs=[tile(2 * D), tile(D_ATT), tile(D_SGU), tile(2 * D), tile(D), pl.BlockSpec((1, 2 * D), lambda i: (0, 0))],
        out_shape=[jax.ShapeDtypeStruct((t, 2 * D), BF16), jax.ShapeDtypeStruct((t, D_ATT), BF16), jax.ShapeDtypeStruct((t, D_SGU), F32),
                   jax.ShapeDtypeStruct((t, 2 * D), BF16), jax.ShapeDtypeStruct((t, D), BF16), jax.ShapeDtypeStruct((1, 2 * D), F32)],
        scratch_shapes=[pltpu.VMEM((D, D), BF16), pltpu.VMEM((D, D), BF16), pltpu.SemaphoreType.DMA((2, N_DEV))],
        compiler_params=_cparams(("arbitrary",)),
    )(dx, gt, y_att, y_sgu, gw)


def _sgu_bwd(zs, dys, lng, lnb, wm2, wmt2, bsx, comms=()):
    t = zs.shape[0]
    n_steps = t // TM

    def body(zs_ref, dys_ref, lng_ref, lnb_ref, wm2_ref, wmt2_ref, bsx_ref, dzs_ref, dw_ref, dbs_ref, dlg_ref, dlb_ref,
             s_scr, dv_scr, ds_acc):
        i = pl.program_id(0)

        @pl.when(i == 0)
        def _():
            dw_ref[...] = jnp.zeros_like(dw_ref)
            dlg_ref[...] = jnp.zeros_like(dlg_ref)
            dlb_ref[...] = jnp.zeros_like(dlb_ref)
            ds_acc[...] = jnp.zeros_like(ds_acc)

        lng = lng_ref[...]
        zs, th, u, xhat, rstd, vsn = _sgu_norm(zs_ref, lng, lnb_ref[...])
        vb = vsn.astype(BF16)
        _sgu_mix(vb, wm2_ref, bsx_ref[...], s_scr)
        dys = dys_ref[...]
        du = dys * s_scr[...]
        ds = dys * u
        dsb = ds.astype(BF16)
        first = lax.broadcasted_iota(jnp.int32, (1, 128), 1) < 64
        acc = jnp.zeros((SGU_BLOCK, D_SGU), F32)
        for n in range(TM // SGU_BLOCK):
            rows = slice(n * 128, (n + 1) * 128)
            acc = acc + ds[rows]
            for p in range(4):
                lanes = slice(p * 128, (p + 1) * 128)
                stack = _group_stack(dsb[rows, lanes], first)
                dv_scr[rows, lanes] = _nn(wmt2_ref[p], stack)
                dw_ref[p] += _nt(stack, vb[rows, lanes])
        ds_acc[...] += acc
        dvsn = dv_scr[...]
        dlg_ref[...] += jnp.sum(dvsn * xhat, axis=0, keepdims=True)
        dlb_ref[...] += jnp.sum(dvsn, axis=0, keepdims=True)
        dxh = dvsn * lng
        dvs = rstd * (dxh - jnp.mean(dxh, axis=-1, keepdims=True) - xhat * jnp.mean(dxh * xhat, axis=-1, keepdims=True))
        dga = jnp.concatenate([du, dvs], axis=1)
        dzs_ref[...] = (dga * _gelu_grad(zs, th)).astype(BF16)

        @pl.when(i == n_steps - 1)
        def _():
            r = lax.broadcasted_iota(jnp.int32, (256, 128), 0) % SGU_BLOCK
            c = lax.broadcasted_iota(jnp.int32, (256, 128), 1)
            keep = (r // CHUNK) >= (c // CHUNK)
            for p in range(4):
                dw_ref[p] = jnp.where(keep, dw_ref[p], 0.0)
            total = ds_acc[...]
            grp = lax.broadcasted_iota(jnp.int32, (SGU_BLOCK, D_SGU), 1) // 64
            lane = lax.broadcasted_iota(jnp.int32, (SGU_BLOCK, 128), 1)
            out = jnp.zeros((SGU_BLOCK, 128), F32)
            for gi in range(8):
                out = jnp.where(lane == gi, jnp.sum(jnp.where(grp == gi, total, 0.0), axis=1, keepdims=True), out)
            dbs_ref[...] = out

    def tile(w):
        return pl.BlockSpec((TM, w), lambda i: (i, 0))

    def const(shape):
        return pl.BlockSpec(shape, lambda i: (0,) * len(shape))

    return _call(
        body, name="sgu_bwd", grid=(n_steps,),
        in_specs=[tile(2 * D_SGU), tile(D_SGU), const((1, D_SGU)), const((1, D_SGU)), const((4, 128, 256)), const((4, 128, 256)),
                  const((128, D_SGU))],
        out_specs=[tile(2 * D_SGU), const((4, 256, 128)), const((128, 128)), const((1, D_SGU)), const((1, D_SGU))],
        out_shape=[jax.ShapeDtypeStruct((t, 2 * D_SGU), BF16), jax.ShapeDtypeStruct((4, 256, 128), F32),
                   jax.ShapeDtypeStruct((128, 128), F32), jax.ShapeDtypeStruct((1, D_SGU), F32), jax.ShapeDtypeStruct((1, D_SGU), F32)],
        scratch_shapes=[pltpu.VMEM((TM, D_SGU), F32), pltpu.VMEM((TM, D_SGU), F32), pltpu.VMEM((SGU_BLOCK, D_SGU), F32)],
        args=(zs, dys, lng, lnb, wm2, wmt2, bsx), comms=comms)


def _adamw(g, w, m, v):
    m = ADAM_B1 * m + (1.0 - ADAM_B1) * g
    v = ADAM_B2 * v + (1.0 - ADAM_B2) * (g * g)
    m_hat = m / (1.0 - ADAM_B1 ** ADAM_STEP)
    v_hat = v / (1.0 - ADAM_B2 ** ADAM_STEP)
    return -ADAM_LR * (m_hat / (jnp.sqrt(v_hat) + ADAM_EPS) + ADAM_WD * w), m, v


def _adamw_matrix(parts, sums, w, m, v, transposed, name):
    _, r, c = parts.shape
    tc = 256

    def body(own_ref, p_ref, s_ref, w_ref, m_ref, v_ref, g_ref, d_ref, mo_ref, vo_ref):
        g = p_ref[0].astype(F32) + p_ref[1].astype(F32) + p_ref[2].astype(F32) + s_ref[...].astype(F32)
        g = g.T if transposed else g
        g_ref[...] = g
        d_ref[...], mo_ref[...], vo_ref[...] = _adamw(g, w_ref[...], m_ref[...], v_ref[...])

    own = pl.BlockSpec((None, tc, r), lambda i, o: (0, i, 0)) if transposed else pl.BlockSpec((None, r, tc), lambda i, o: (0, 0, i))
    return pl.pallas_call(
        body, name=name,
        grid_spec=pltpu.PrefetchScalarGridSpec(
            num_scalar_prefetch=1, grid=(c // tc,),
            in_specs=[pl.BlockSpec((3, r, tc), lambda i, o: (0, 0, i)), pl.BlockSpec((None, r, tc), lambda i, o: (o[0], 0, i)), own, own, own],
            out_specs=[own] * 4),
        out_shape=[jax.ShapeDtypeStruct(w.shape, F32)] * 4,
        compiler_params=_cparams(("arbitrary",)),
    )(_my_index(("x", 2), ("y", 1)), parts, sums, w, m, v)


_SMALL_2D = {"norm_ffn1": (1, D), "norm_mix": (1, D), "norm_ffn2": (1, D), "norm_final": (1, D), "b_gate": (1, 2 * D),
             "sgu_ln_g": (1, D_SGU), "sgu_ln_b": (1, D_SGU), "sgu_b_s": (8, SGU_BLOCK), "rel_bias": (HEADS, N_REL),
             "sgu_w_s": (8 * SGU_BLOCK, SGU_BLOCK)}


def _adamw_small(parts, loss_parts, p):
    names = list(parts)
    n = len(names)

    def body(*refs):
        got, loss_got, wmv, outs, loss_out = refs[:n], refs[n], refs[n + 1:4 * n + 1], refs[4 * n + 1:8 * n + 1], refs[8 * n + 1]
        for i, name in enumerate(names):
            g = got[i][0]
            for k in range(1, N_DEV):
                g = g + got[i][k]
            if name == "sgu_b_s":
                g = g.T[0:8, :]
            res = (g,) + _adamw(g, wmv[3 * i][...], wmv[3 * i + 1][...], wmv[3 * i + 2][...])
            for o_ref, val in zip(outs[4 * i:4 * i + 4], res):
                o_ref[...] = val
        total = loss_got[0]
        for k in range(1, N_DEV):
            total = total + loss_got[k]
        loss_out[...] = total

    wmv = [p[pre + name].reshape(_SMALL_2D[name]) for name in names for pre in ("", "m_", "v_")]
    res = pl.pallas_call(
        body, name="adamw_small",
        out_shape=[jax.ShapeDtypeStruct(_SMALL_2D[name], F32) for name in names for _ in range(4)] + [jax.ShapeDtypeStruct((1, 128), F32)],
        compiler_params=_cparams())(*[parts[name] for name in names], loss_parts, *wmv)
    return [{name: res[4 * i + j].reshape(p[name].shape) for i, name in enumerate(names)} for j in range(4)], res[-1]


def _pack_rows(groups, name):
    flat = [a for grp in groups for a, _ in grp]
    rows = [grp[0][0].shape[2] if grp[0][1] else grp[0][0].shape[1] for grp in groups]

    def body(*refs):
        o_ref, pos, off = refs[-1], 0, 0
        for grp, r in zip(groups, rows):
            vals = []
            for _, transposed in grp:
                val = refs[pos][0]
                vals.append(val.T if transposed else val)
                pos += 1
            o_ref[off:off + r, :] = (vals[0] if len(vals) == 1 else jnp.concatenate(vals, axis=1)).astype(BF16)
            off += r

    return pl.pallas_call(body, name=name, out_shape=jax.ShapeDtypeStruct((sum(rows), D), BF16), compiler_params=_cparams())(*flat)


def _step(x, target, p):
    n1, nm, n2 = p["norm_ffn1"], p["norm_mix"], p["norm_ffn2"]
    nf = p["norm_final"].reshape(1, D)
    lng, lnb = p["sgu_ln_g"], p["sgu_ln_b"]
    w_m = jnp.where(jnp.asarray(_sgu_mask())[None], p["sgu_w_s"][0], 0.0).astype(BF16)
    wm2 = jnp.concatenate([w_m[0::2], w_m[1::2]], axis=2)
    w_mt = w_m.transpose(0, 2, 1)
    wmt2 = jnp.concatenate([w_mt[0::2], w_mt[1::2]], axis=2)
    bsx = jnp.repeat(p["sgu_b_s"][0].T, 64, axis=1)
    bias = _band_bias(p["rel_bias"][0])

    def chip_sums(grads, name):
        gots = _comm_only([_SiblingSwap(grads)], "swap_" + name)
        return _pair_sums(grads, gots, "pair_sums_" + name)

    def as_rows(a):
        return jnp.swapaxes(a, 1, 2)

    def updates(parts, sums, names):
        res = {}
        for pt, sm, n in zip(parts, sums, names):
            if p[n].shape[1:] == pt.shape[1:]:
                res[n] = _adamw_matrix(pt, sm, p[n], p["m_" + n], p["v_" + n], False, "adamw_" + n)
            elif p[n].shape[2] > 128:
                res[n] = [as_rows(o) for o in _adamw_matrix(pt, sm, as_rows(p[n]), as_rows(p["m_" + n]), as_rows(p["v_" + n]), False,
                                                            "adamw_" + n)]
            else:
                res[n] = _adamw_matrix(pt, sm, p[n], p["m_" + n], p["v_" + n], True, "adamw_" + n)
        return res

    rows1 = _pack_rows([[(as_rows(p["ffn1_w_gate"]), False)], [(as_rows(p["ffn1_w_up"]), False)], [(p["ffn1_w_down"], False)]], "pack_ffn1")
    rows_m = _pack_rows([[(as_rows(p["w_in"]), False)], [(p["w_branch_att"], True), (p["w_branch_sgu"], True)], [(p["w_out"], False)]],
                        "pack_mixer")
    rows2d = _pack_rows([[(p["ffn2_w_down"], False)]], "pack_ffn2_down")
    rows2gu = _pack_rows([[(as_rows(p["ffn2_w_gate"]), False)], [(as_rows(p["ffn2_w_up"]), False)]], "pack_ffn2_gate_up")
    (gw1,) = _comm_only([_Gather(rows1)], "gather_ffn1")
    x1, ab1, h1, (gwm,) = _ffn_fwd(x, n1, gw1, gw1, 2 * R_FF, "ffn1_fwd", [_Gather(rows_m)])
    (q, k, v, zs, gt, h2), (gw2d,) = _mix_proj_fwd(x1, nm, p["b_gate"], gwm, [_Gather(rows2d)])
    y_att, (gw2gu,) = _att_fwd(q, k, v, bias, [_Gather(rows2gu)])
    x2, y_sgu, merged = _merge_fwd(x1, zs, gt, y_att, lng, lnb, wm2, bsx, gwm)
    dx3, ab2, hb, _, d_nf, loss = _ffn_fwd(x2, n2, gw2gu, gw2d, 0, "ffn2_fwd", head=(nf, target))

    dab, sb, dfb = _ffn_bwd_hidden(ab2, dx3, gw2d, 0, "ffn2_bwd_hidden")
    (dx2, d_n2), _ = _ffn_bwd_input(x2, n2, dab, dx3, gw2gu, "ffn2_bwd")
    g2 = [_weight_grad(dab, hb, "ffn2_dw_gate", 0, F), _weight_grad(dab, hb, "ffn2_dw_up", F, F), _weight_grad(sb, dfb, "ffn2_dw_down")]
    dzg, dya, dys, dpp, dxb, d_bg = _merge_bwd(dx2, gt, y_att, y_sgu, gwm)
    g_late = g2 + [_weight_grad(dpp, y_att, "dw_branch_att", 0, D), _weight_grad(dpp, y_sgu, "dw_branch_sgu", D, D),
                   _weight_grad(merged, dxb, "dw_out")]
    late = ("ffn2_w_gate", "ffn2_w_up", "ffn2_w_down", "w_branch_att", "w_branch_sgu", "w_out")
    (dzs, d_wm, d_bs, d_lng, d_lnb), gots_late = _sgu_bwd(zs, dys, lng, lnb, wm2, wmt2, bsx, [_SiblingSwap(g_late)])
    sums_late = _pair_sums(g_late, gots_late, "pair_sums_late")
    (dq, dk, dv, d_bias), parts_late = _att_bwd(q, k, v, bias, dya, [_ChipScatter(sums_late)])
    big = updates(parts_late, sums_late, late)
    d_rel = _rel_bias_grad(d_bias)
    dz = jnp.concatenate([dq, dk, dv, dzs, dzg], axis=1)
    small = {"norm_ffn2": d_n2, "norm_final": d_nf, "b_gate": d_bg, "sgu_ln_g": d_lng, "sgu_ln_b": d_lnb, "sgu_b_s": d_bs,
             "rel_bias": d_rel, "sgu_w_s": d_wm.reshape(_SMALL_2D["sgu_w_s"])}
    g_in, (*early_parts, loss_parts) = _weight_grad(dz, h2, "dw_in", comms=[_AllToAll(list(small.values()) + [loss])])
    sums_in = chip_sums([g_in], "w_in")
    (dx1, d_nm), parts_in = _mix_proj_bwd(dz, x1, nm, dx2, gwm, [_ChipScatter(sums_in)])
    big.update(updates(parts_in, sums_in, ("w_in",)))

    dab, sb, dfb = _ffn_bwd_hidden(ab1, dx1, gw1, 2 * R_FF, "ffn1_bwd_hidden")
    sums_d = chip_sums([_weight_grad(sb, dfb, "ffn1_dw_down")], "ffn1_down")
    g_gate, parts_d = _weight_grad(dab, h1, "ffn1_dw_gate", 0, F, [_ChipScatter(sums_d)])
    sums_g = chip_sums([g_gate], "ffn1_gate")
    g_up, (parts_g, nm_parts) = _weight_grad(dab, h1, "ffn1_dw_up", F, F, [_ChipScatter(sums_g), _AllToAll([d_nm])])
    sums_u = chip_sums([g_up], "ffn1_up")
    (dx0, d_n1), parts_u = _ffn_bwd_input(x, n1, dab, dx1, gw1, "ffn1_bwd", [_ChipScatter(sums_u)])
    (n1_parts,) = _comm_only([_AllToAll([d_n1])], "gather_norm_ffn1")
    big.update(updates(parts_d + [parts_g] + parts_u, sums_d + sums_g + sums_u, ("ffn1_w_down", "ffn1_w_gate", "ffn1_w_up")))
    out_s, loss_sum = _adamw_small(dict(zip(small, early_parts), norm_mix=nm_parts, norm_ffn1=n1_parts), loss_parts, p)
    return dx0, loss_sum[0, 0], [{**{n: four[i] for n, four in big.items()}, **s} for i, s in enumerate(out_s)]


_OUT_ORDER = ("norm_ffn1", "ffn1_w_gate", "ffn1_w_up", "ffn1_w_down", "norm_mix", "w_in", "b_gate", "rel_bias", "sgu_ln_g", "sgu_ln_b",
              "sgu_w_s", "sgu_b_s", "w_branch_att", "w_branch_sgu", "w_out", "norm_ffn2", "ffn2_w_gate", "ffn2_w_up", "ffn2_w_down",
              "norm_final")


def kernel(x, norm_ffn1, ffn1_w_gate, ffn1_w_up, ffn1_w_down, norm_mix, w_in, b_gate, rel_bias, sgu_ln_g, sgu_ln_b, sgu_w_s, sgu_b_s, w_branch_att, w_branch_sgu, w_out, norm_ffn2, ffn2_w_gate, ffn2_w_up, ffn2_w_down, norm_final, loss_target, m_norm_ffn1, m_ffn1_w_gate, m_ffn1_w_up, m_ffn1_w_down, m_norm_mix, m_w_in, m_b_gate, m_rel_bias, m_sgu_ln_g, m_sgu_ln_b, m_sgu_w_s, m_sgu_b_s, m_w_branch_att, m_w_branch_sgu, m_w_out, m_norm_ffn2, m_ffn2_w_gate, m_ffn2_w_up, m_ffn2_w_down, m_norm_final, v_norm_ffn1, v_ffn1_w_gate, v_ffn1_w_up, v_ffn1_w_down, v_norm_mix, v_w_in, v_b_gate, v_rel_bias, v_sgu_ln_g, v_sgu_ln_b, v_sgu_w_s, v_sgu_b_s, v_w_branch_att, v_w_branch_sgu, v_w_out, v_norm_ffn2, v_ffn2_w_gate, v_ffn2_w_up, v_ffn2_w_down, v_norm_final):
    args = dict(locals())
    dx, loss, outs = _step(x[0], loss_target[0], {pre + n: args[pre + n] for pre in ("", "m_", "v_") for n in _OUT_ORDER})
    return (loss, dx[None], *[o[n] for o in outs for n in _OUT_ORDER])
```

```python
import functools

import numpy as np
import jax
import jax.numpy as jnp
from jax import lax
from jax.experimental import pallas as pl
from jax.experimental.pallas import tpu as pltpu

F32 = jnp.float32
BF16 = jnp.bfloat16

N_DEV = 8
D = 1024
F = 2816
D_ATT = 512
D_SGU = 512
D_IN = 4608
HEADS = 8
CHUNK = 64
N_LEFT = 8
REL_CLIP = 256
N_REL = 2 * REL_CLIP + 1
SGU_BLOCK = 128
EPS = 1e-6
NEG_INF = -1e30
QB = 256
KW = 3 * QB

R_FF, R_IN, R_BR, R_WO = F // N_DEV, D_IN // N_DEV, D // N_DEV, D // N_DEV
OFF_IN, OFF_BR, OFF_WO = 0, R_IN, R_IN + R_BR
FFN_ROWS = 3 * R_FF
MIX_ROWS = R_IN + R_BR + R_WO

FC = 256
TM = 512
VMEM_LIMIT = 56 * 1024 * 1024

ADAM_LR, ADAM_B1, ADAM_B2, ADAM_EPS, ADAM_WD, ADAM_STEP = 0.001, 0.9, 0.999, 1e-08, 0.01, 10

MESH = pl.DeviceIdType.MESH
ANY = pl.BlockSpec(memory_space=pl.ANY)


def _nt(a, b):
    return lax.dot_general(a, b, (((1,), (1,)), ((), ())), preferred_element_type=F32)


def _tn(a, b):
    return lax.dot_general(a, b, (((0,), (0,)), ((), ())), preferred_element_type=F32)


def _nn(a, b):
    return jnp.dot(a, b, preferred_element_type=F32)


def _cparams(sem=None):
    return pltpu.CompilerParams(dimension_semantics=sem, vmem_limit_bytes=VMEM_LIMIT)


def _load_rows(gw_ref, dst, off, rows, sems):
    copies = [pltpu.make_async_copy(gw_ref.at[k, pl.ds(off, rows), :], dst.at[pl.ds(k * rows, rows), :], sems.at[k])
              for k in range(N_DEV)]
    for cp in copies:
        cp.start()
    return copies


def _rms(xv):
    r = lax.rsqrt(jnp.mean(xv * xv, axis=-1, keepdims=True) + EPS)
    return xv * r, r


def _rms_bwd(dh, xn, r, gain):
    dxn = dh * gain
    dx = r * (dxn - xn * jnp.mean(dxn * xn, axis=-1, keepdims=True))
    return dx, jnp.sum(dh * xn, axis=0, keepdims=True)


def _gelu(x):
    t = jnp.tanh(0.7978845608028654 * (x + 0.044715 * x * x * x))
    return 0.5 * x * (1.0 + t), t


def _gelu_grad(x, t):
    return 0.5 * (1.0 + t) + 0.5 * x * (1.0 - t * t) * 0.7978845608028654 * (1.0 + 3.0 * 0.044715 * x * x)


def _place():
    x, y, cc = lax.axis_index("x"), lax.axis_index("y"), lax.axis_index("c")
    return x, y, cc, [(1 - x, y), (x, 1 - y), (1 - x, 1 - y)]


class _Gather:
    def __init__(self, shard):
        self.inputs = [shard]
        self.out_shape = [jax.ShapeDtypeStruct((N_DEV,) + shard.shape, shard.dtype)]
        self.scratch = [pltpu.SemaphoreType.DMA((7,)), pltpu.SemaphoreType.DMA((7,)), pltpu.SemaphoreType.DMA]

    def _copies(self, ins, outs, scr):
        (x_ref,), (out_ref,), (send_sems, recv_sems, local_sem) = ins, outs, scr
        x, y, cc, chips = _place()

        def slab(px, py, pc):
            return out_ref.at[4 * px + 2 * py + pc]

        def copy(k, block, to, src=None):
            return pltpu.make_async_remote_copy(
                src_ref=slab(*block) if src is None else src, dst_ref=slab(*block),
                send_sem=send_sems.at[k], recv_sem=recv_sems.at[k], device_id=to, device_id_type=MESH)

        me, sibling = (x, y, cc), (x, y, 1 - cc)
        mine = pltpu.make_async_copy(x_ref, slab(*me), local_sem)
        first = [copy(0, me, sibling, src=x_ref)] + [copy(1 + j, me, (*chip, cc), src=x_ref) for j, chip in enumerate(chips)]
        landed = [copy(1 + j, (*chip, cc), me) for j, chip in enumerate(chips)]
        passed = [copy(4 + j, (*chip, cc), sibling) for j, chip in enumerate(chips)]
        from_sibling = [copy(0, sibling, me)] + [copy(4 + j, (*chip, 1 - cc), me) for j, chip in enumerate(chips)]
        return mine, first, landed, passed, from_sibling

    def begin(self, *refs):
        mine, first, _, _, _ = self._copies(*refs)
        mine.start()
        for cp in first:
            cp.start()

    def relay(self, *refs):
        _, _, landed, passed, _ = self._copies(*refs)
        for arrived, onward in zip(landed, passed):
            arrived.wait_recv()
            onward.start()

    def end(self, *refs):
        mine, first, _, passed, from_sibling = self._copies(*refs)
        for cp in from_sibling:
            cp.wait_recv()
        for cp in first + passed:
            cp.wait_send()
        mine.wait()


class _Direct:
    def begin(self, *refs):
        keep, give = self._copies(*refs)
        for cp in keep + give:
            cp.start()

    def relay(self, *refs):
        pass

    def end(self, *refs):
        keep, give = self._copies(*refs)
        for cp in give:
            cp.wait_recv()
        for cp in give:
            cp.wait_send()
        for cp in keep:
            cp.wait()


class _SiblingSwap(_Direct):
    def __init__(self, grads):
        n = len(grads)
        self.inputs = list(grads)
        self.out_shape = [jax.ShapeDtypeStruct((4,) + g.shape[1:], g.dtype) for g in grads]
        self.scratch = [pltpu.SemaphoreType.DMA((n, 4)), pltpu.SemaphoreType.DMA((n, 4))]

    def _copies(self, ins, outs, scr):
        send_sems, recv_sems = scr
        x, y, cc, _ = _place()
        return [], [pltpu.make_async_remote_copy(src_ref=g_ref.at[2 * j + 1 - cc], dst_ref=got_ref.at[j], send_sem=send_sems.at[i, j],
                                                 recv_sem=recv_sems.at[i, j], device_id=(x, y, 1 - cc), device_id_type=MESH)
                    for i, (g_ref, got_ref) in enumerate(zip(ins, outs)) for j in range(4)]


class _ChipScatter(_Direct):
    def __init__(self, sums):
        n = len(sums)
        self.inputs = list(sums)
        self.out_shape = [jax.ShapeDtypeStruct((3,) + s.shape[1:], s.dtype) for s in sums]
        self.scratch = [pltpu.SemaphoreType.DMA((n, 3)), pltpu.SemaphoreType.DMA((n, 3))]

    def _copies(self, ins, outs, scr):
        send_sems, recv_sems = scr
        _, _, cc, chips = _place()
        return [], [pltpu.make_async_remote_copy(src_ref=s_ref.at[2 * px + py], dst_ref=got_ref.at[j], send_sem=send_sems.at[i, j],
                                                 recv_sem=recv_sems.at[i, j], device_id=(px, py, cc), device_id_type=MESH)
                    for i, (s_ref, got_ref) in enumerate(zip(ins, outs)) for j, (px, py) in enumerate(chips)]


class _AllToAll(_Direct):
    def __init__(self, blocks):
        n = len(blocks)
        self.inputs = list(blocks)
        self.out_shape = [jax.ShapeDtypeStruct((N_DEV,) + b.shape, b.dtype) for b in blocks]
        self.scratch = [pltpu.SemaphoreType.DMA((n, 7)), pltpu.SemaphoreType.DMA((n, 7)), pltpu.SemaphoreType.DMA((n,))]

    def _copies(self, ins, outs, scr):
        send_sems, recv_sems, local_sems = scr
        x, y, cc, _ = _place()
        me = 4 * x + 2 * y + cc
        keep = [pltpu.make_async_copy(b_ref, got_ref.at[me], local_sems.at[i]) for i, (b_ref, got_ref) in enumerate(zip(ins, outs))]
        give = [pltpu.make_async_remote_copy(src_ref=b_ref, dst_ref=got_ref.at[me], send_sem=send_sems.at[i, k - 1],
                                             recv_sem=recv_sems.at[i, k - 1],
                                             device_id=(x ^ ((k >> 2) & 1), y ^ ((k >> 1) & 1), cc ^ (k & 1)), device_id_type=MESH)
                for i, (b_ref, got_ref) in enumerate(zip(ins, outs)) for k in range(1, N_DEV)]
        return keep, give


def _split_refs(refs, counts):
    out, pos = [], 0
    for n in counts:
        out.append(list(refs[pos:pos + n]))
        pos += n
    return out


def _bind(comms, c_in, c_out, c_scr):
    ins = _split_refs(c_in, [len(c.inputs) for c in comms])
    outs = _split_refs(c_out, [len(c.out_shape) for c in comms])
    scr = _split_refs(c_scr, [len(c.scratch) for c in comms])
    return [(c, (i, o, s)) for c, i, o, s in zip(comms, ins, outs, scr)]


def _call(body, *, name, grid, in_specs, out_specs, out_shape, scratch_shapes, args, comms=()):
    c_in = [a for c in comms for a in c.inputs]
    c_out = [s for c in comms for s in c.out_shape]
    c_scr = [s for c in comms for s in c.scratch]
    counts = [len(in_specs), len(c_in), len(out_shape), len(c_out), len(scratch_shapes), len(c_scr)]

    def full(*refs):
        ins, cin, outs, cout, scr, cscr = _split_refs(refs, counts)
        bound = _bind(comms, cin, cout, cscr)
        if comms:
            first = functools.reduce(jnp.logical_and, [pl.program_id(ax) == 0 for ax in range(len(grid))])
            last = functools.reduce(jnp.logical_and, [pl.program_id(ax) == n - 1 for ax, n in enumerate(grid)])

            @pl.when(first)
            def _():
                for c, r in bound:
                    c.begin(*r)

            @pl.when(last)
            def _():
                for c, r in bound:
                    c.relay(*r)

        body(*ins, *outs, *scr)
        if comms:
            @pl.when(last)
            def _():
                for c, r in bound:
                    c.end(*r)

    res = pl.pallas_call(
        full, name=name, grid=grid,
        in_specs=list(in_specs) + [ANY] * len(c_in), out_specs=list(out_specs) + [ANY] * len(c_out),
        out_shape=list(out_shape) + c_out, scratch_shapes=list(scratch_shapes) + c_scr,
        compiler_params=_cparams(("arbitrary",) * len(grid)),
    )(*args, *c_in)
    return list(res[:len(out_shape)]), list(res[len(out_shape):])


def _comm_only(comms, name):
    c_in = [a for c in comms for a in c.inputs]
    c_out = [s for c in comms for s in c.out_shape]
    c_scr = [s for c in comms for s in c.scratch]

    def full(*refs):
        cin, cout, cscr = _split_refs(refs, [len(c_in), len(c_out), len(c_scr)])
        bound = _bind(comms, cin, cout, cscr)
        for phase in ("begin", "relay", "end"):
            for c, r in bound:
                getattr(c, phase)(*r)

    return list(pl.pallas_call(full, name=name, in_specs=[ANY] * len(c_in), out_specs=[ANY] * len(c_out), out_shape=c_out,
                               scratch_shapes=c_scr)(*c_in))


def _my_index(*axes_and_weights):
    return sum(w * lax.axis_index(a) for a, w in axes_and_weights).astype(jnp.int32).reshape(1)


def _pair_sums(grads, gots, name):
    n = len(grads)

    def body(c_ref, *refs):
        for a_ref, b_ref, o_ref in zip(refs[:n], refs[n:2 * n], refs[2 * n:]):
            o_ref[...] = (a_ref[...].astype(F32) + b_ref[...].astype(F32)).astype(BF16)

    def tile(g):
        return pl.BlockSpec((1,) + g.shape[1:], lambda j, c_ref: (j, 0, 0))

    def mine(g):
        return pl.BlockSpec((1, None) + g.shape[1:], lambda j, c_ref: (j, c_ref[0], 0, 0))

    return list(pl.pallas_call(
        body, name=name,
        grid_spec=pltpu.PrefetchScalarGridSpec(num_scalar_prefetch=1, grid=(4,), in_specs=[mine(g) for g in gots] + [tile(g) for g in gots],
                                               out_specs=[tile(g) for g in gots]),
        out_shape=[jax.ShapeDtypeStruct(g.shape, BF16) for g in gots],
        compiler_params=_cparams(("arbitrary",)))(_my_index(("c", 1)), *[g.reshape((4, 2) + g.shape[1:]) for g in grads], *gots))


def _ffn_fwd(x, gain, gw_gu, gw_d, off_d, name, comms=(), head=None):
    t = x.shape[0]
    n_head = 0 if head is None else 2

    def body(x_ref, g_ref, gu_ref, d_ref, *refs):
        head_in, (o_ref, ab_ref, h_ref), head_out = refs[:n_head], refs[n_head:n_head + 3], refs[n_head + 3:2 * n_head + 3]
        wg, wu, wd, s_scr, sems = refs[2 * n_head + 3:]

        @pl.when(pl.program_id(0) == 0)
        def _():
            cps = _load_rows(gu_ref, wg, 0, R_FF, sems.at[0]) + _load_rows(gu_ref, wu, R_FF, R_FF, sems.at[1]) \
                + _load_rows(d_ref, wd, off_d, R_FF, sems.at[2])
            for o in head_out:
                o[...] = jnp.zeros_like(o)
            for cp in cps:
                cp.wait()

        xv = x_ref[...]
        xn, _ = _rms(xv)
        h = (xn * g_ref[...]).astype(BF16)
        h_ref[...] = h
        for c in range(F // FC):
            rows = pl.ds(c * FC, FC)
            a = _nt(h, wg[rows, :])
            b = _nt(h, wu[rows, :])
            ab_ref[:, c * FC:(c + 1) * FC] = a.astype(BF16)
            ab_ref[:, F + c * FC:F + (c + 1) * FC] = b.astype(BF16)
            s_scr[:, c * FC:(c + 1) * FC] = (a * jax.nn.sigmoid(a) * b).astype(BF16)
        out = xv + 0.5 * _nn(s_scr[...], wd[...])
        if head is None:
            o_ref[...] = out
        else:
            (gf_ref, t_ref), (dg_ref, loss_ref) = head_in, head_out
            gain_f = gf_ref[...]
            yn, r = _rms(out)
            err = yn * gain_f - t_ref[...]
            loss_ref[...] += 0.5 * jnp.sum(jnp.mean(err * err, axis=-1, keepdims=True), axis=0, keepdims=True)
            o_ref[...], dg = _rms_bwd(err * (1.0 / D), yn, r, gain_f)
            dg_ref[...] += dg

    tile = pl.BlockSpec((TM, D), lambda i: (i, 0))
    row = pl.BlockSpec((1, D), lambda i: (0, 0))
    head_specs = [] if head is None else [row, pl.BlockSpec((1, 128), lambda i: (0, 0))]
    head_shapes = [] if head is None else [jax.ShapeDtypeStruct((1, D), F32), jax.ShapeDtypeStruct((1, 128), F32)]
    res, got = _call(
        body, name=name, grid=(t // TM,),
        in_specs=[tile, row, ANY, ANY] + ([] if head is None else [row, tile]),
        out_specs=[tile, pl.BlockSpec((TM, 2 * F), lambda i: (i, 0)), tile] + head_specs,
        out_shape=[jax.ShapeDtypeStruct((t, D), F32), jax.ShapeDtypeStruct((t, 2 * F), BF16), jax.ShapeDtypeStruct((t, D), BF16)] + head_shapes,
        scratch_shapes=[pltpu.VMEM((F, D), BF16)] * 3 + [pltpu.VMEM((TM, F), BF16), pltpu.SemaphoreType.DMA((3, N_DEV))],
        args=(x, gain, gw_gu, gw_d) + (() if head is None else tuple(head)), comms=comms)
    return (res[0], res[1], res[2], got, *res[3:])


def _ffn_bwd_hidden(ab, dout, gw_d, off_d, name):
    t = ab.shape[0]

    def tile(w):
        return pl.BlockSpec((TM, w), lambda i: (i, 0))

    def hidden(ab_ref, do_ref, d_ref, dab_ref, s_ref, df_ref, wd, sems):
        @pl.when(pl.program_id(0) == 0)
        def _():
            for cp in _load_rows(d_ref, wd, off_d, R_FF, sems):
                cp.wait()

        df = (0.5 * do_ref[...]).astype(BF16)
        df_ref[...] = df
        for c in range(F // FC):
            a = ab_ref[:, c * FC:(c + 1) * FC]
            b = ab_ref[:, F + c * FC:F + (c + 1) * FC]
            sg = jax.nn.sigmoid(a.astype(F32)).astype(BF16)
            sl = a * sg
            ds = _nt(df, wd[pl.ds(c * FC, FC), :]).astype(BF16)
            dab_ref[:, c * FC:(c + 1) * FC] = (ds * b) * (sg + sl * (1.0 - sg))
            dab_ref[:, F + c * FC:F + (c + 1) * FC] = ds * sl
            s_ref[:, c * FC:(c + 1) * FC] = sl * b

    res, _ = _call(
        hidden, name=name, grid=(t // TM,),
        in_specs=[tile(2 * F), tile(D), ANY], out_specs=[tile(2 * F), tile(F), tile(D)],
        out_shape=[jax.ShapeDtypeStruct((t, 2 * F), BF16), jax.ShapeDtypeStruct((t, F), BF16), jax.ShapeDtypeStruct((t, D), BF16)],
        scratch_shapes=[pltpu.VMEM((F, D), BF16), pltpu.SemaphoreType.DMA((N_DEV,))],
        args=(ab, dout, gw_d))
    return res


def _ffn_bwd_input(x, gain, dab, dout, gw_gu, name, comms=()):
    t = x.shape[0]

    def body(x_ref, g_ref, dab_ref, do_ref, gu_ref, dx_ref, dg_ref, wg, wu, sems):
        @pl.when(pl.program_id(0) == 0)
        def _():
            cps = _load_rows(gu_ref, wg, 0, R_FF, sems.at[0]) + _load_rows(gu_ref, wu, R_FF, R_FF, sems.at[1])
            dg_ref[...] = jnp.zeros_like(dg_ref)
            for cp in cps:
                cp.wait()

        gain_v = g_ref[...]
        xn, r = _rms(x_ref[...])
        dh = _nn(dab_ref[:, :F], wg[...]) + _nn(dab_ref[:, F:], wu[...])
        dxn, dg = _rms_bwd(dh, xn, r, gain_v)
        dg_ref[...] += dg
        dx_ref[...] = do_ref[...] + dxn

    def tile(w):
        return pl.BlockSpec((TM, w), lambda i: (i, 0))

    row = pl.BlockSpec((1, D), lambda i: (0, 0))
    return _call(
        body, name=name, grid=(t // TM,),
        in_specs=[tile(D), row, tile(2 * F), tile(D), ANY], out_specs=[tile(D), row],
        out_shape=[jax.ShapeDtypeStruct((t, D), F32), jax.ShapeDtypeStruct((1, D), F32)],
        scratch_shapes=[pltpu.VMEM((F, D), BF16)] * 2 + [pltpu.SemaphoreType.DMA((2, N_DEV))],
        args=(x, gain, dab, dout, gw_gu), comms=comms)


def _weight_grad(a, b, name, col_off=0, m=None, comms=None, tmm=256):
    t = a.shape[0]
    m = a.shape[1] if m is None else m
    n = b.shape[1]
    first = col_off // tmm

    def body(a_ref, b_ref, o_ref):
        o_ref[...] = _tn(a_ref[...], b_ref[...]).astype(BF16)

    (out,), got = _call(
        body, name=name, grid=(m // tmm,),
        in_specs=[pl.BlockSpec((t, tmm), lambda i: (0, first + i)), pl.BlockSpec((t, n), lambda i: (0, 0))],
        out_specs=[pl.BlockSpec((tmm, n), lambda i: (i, 0))],
        out_shape=[jax.ShapeDtypeStruct((m, n), BF16)], scratch_shapes=[], args=(a, b), comms=comms or ())
    out = out.reshape(N_DEV, m // N_DEV, n)
    return out if comms is None else (out, got)


def _mix_proj_fwd(x, gain, b_gate, gw, comms=()):
    t = x.shape[0]

    def body(x_ref, g_ref, bg_ref, gw_ref, q_ref, k_ref, v_ref, zs_ref, gt_ref, h_ref, win, sems):
        @pl.when(pl.program_id(0) == 0)
        def _():
            for cp in _load_rows(gw_ref, win, OFF_IN, R_IN, sems):
                cp.wait()

        xn, _ = _rms(x_ref[...])
        h = (xn * g_ref[...]).astype(BF16)
        h_ref[...] = h
        q_ref[...] = (_nt(h, win[0:512, :]) * 0.125).astype(BF16)
        k_ref[...] = _nt(h, win[512:1024, :]).astype(BF16)
        v_ref[...] = _nt(h, win[1024:1536, :]).astype(BF16)
        for c in range(2):
            zs_ref[:, c * 512:(c + 1) * 512] = _nt(h, win[1536 + c * 512:2048 + c * 512, :]).astype(BF16)
        for c in range(4):
            zg = _nt(h, win[2560 + c * 512:3072 + c * 512, :]) + bg_ref[:, c * 512:(c + 1) * 512]
            gt_ref[:, c * 512:(c + 1) * 512] = jax.nn.sigmoid(zg).astype(BF16)

    def tile(w):
        return pl.BlockSpec((TM, w), lambda i: (i, 0))

    return _call(
        body, name="mix_proj_fwd", grid=(t // TM,),
        in_specs=[tile(D), pl.BlockSpec((1, D), lambda i: (0, 0)), pl.BlockSpec((1, 2 * D), lambda i: (0, 0)), ANY],
        out_specs=[tile(D_ATT), tile(D_ATT), tile(D_ATT), tile(2 * D_SGU), tile(2 * D), tile(D)],
        out_shape=[jax.ShapeDtypeStruct((t, D_ATT), BF16)] * 3 + [jax.ShapeDtypeStruct((t, 2 * D_SGU), BF16),
                                                                   jax.ShapeDtypeStruct((t, 2 * D), BF16),
                                                                   jax.ShapeDtypeStruct((t, D), BF16)],
        scratch_shapes=[pltpu.VMEM((D_IN, D), BF16), pltpu.SemaphoreType.DMA((N_DEV,))],
        args=(x, gain, b_gate, gw), comms=comms)


def _mix_proj_bwd(dz, x, gain, dres, gw, comms=()):
    t = x.shape[0]

    def body(dz_ref, x_ref, g_ref, dr_ref, gw_ref, dx_ref, dg_ref, win, sems):
        @pl.when(pl.program_id(0) == 0)
        def _():
            cps = _load_rows(gw_ref, win, OFF_IN, R_IN, sems)
            dg_ref[...] = jnp.zeros_like(dg_ref)
            for cp in cps:
                cp.wait()

        dh = _nn(dz_ref[...], win[...])
        xn, r = _rms(x_ref[...])
        dxn, dg = _rms_bwd(dh, xn, r, g_ref[...])
        dg_ref[...] += dg
        dx_ref[...] = dr_ref[...] + dxn

    def tile(w):
        return pl.BlockSpec((TM, w), lambda i: (i, 0))

    row = pl.BlockSpec((1, D), lambda i: (0, 0))
    return _call(
        body, name="mix_proj_bwd", grid=(t // TM,),
        in_specs=[tile(D_IN), tile(D), row, tile(D), ANY], out_specs=[tile(D), row],
        out_shape=[jax.ShapeDtypeStruct((t, D), F32), jax.ShapeDtypeStruct((1, D), F32)],
        scratch_shapes=[pltpu.VMEM((D_IN, D), BF16), pltpu.SemaphoreType.DMA((N_DEV,))],
        args=(dz, x, gain, dres, gw), comms=comms)


SKEW_W = KW + QB
N_CAP = 2 * QB - REL_CLIP + 1


def _band_bias(rel_bias):
    cap = rel_bias[:, 2 * REL_CLIP:]
    diag = jnp.concatenate([jnp.broadcast_to(cap, (HEADS, N_CAP)), rel_bias[:, 2 * REL_CLIP - 1::-1],
                            jnp.broadcast_to(cap, (HEADS, SKEW_W - N_CAP - 2 * REL_CLIP))], axis=1)

    def body(d_ref, o_ref):
        lag = lax.broadcasted_iota(jnp.int32, (QB, KW), 1) // CHUNK - lax.broadcasted_iota(jnp.int32, (QB, KW), 0) // CHUNK
        band = (lag >= 0) & (lag <= N_LEFT)
        for h in range(HEADS):
            rows = jnp.broadcast_to(d_ref[h:h + 1, :], (QB, SKEW_W))
            o_ref[h] = jnp.where(band, pltpu.roll(rows, 0, 1, stride=1, stride_axis=0)[:, :KW], NEG_INF)

    return pl.pallas_call(body, name="band_bias", out_shape=jax.ShapeDtypeStruct((HEADS, QB, KW), F32))(diag)


def _att_specs():
    qspec = pl.BlockSpec((QB, D_ATT), lambda g: (g, 0))
    kspecs = [pl.BlockSpec((QB, D_ATT), lambda g: (jnp.maximum(g - 2, 0), 0)),
              pl.BlockSpec((QB, D_ATT), lambda g: (jnp.maximum(g - 1, 0), 0)), qspec]
    bspec = pl.BlockSpec((HEADS, QB, KW), lambda g: (0, 0, 0))
    return qspec, kspecs, bspec


def _att_probs(qm, kp, bias, valid):
    s = jnp.where(valid, _nt(qm, kp) + bias, NEG_INF)
    e = jnp.exp(s - jnp.max(s, axis=-1, keepdims=True))
    return e / jnp.sum(e, axis=-1, keepdims=True)


def _att_valid():
    g = pl.program_id(0)
    blk = lax.broadcasted_iota(jnp.int32, (QB, KW), 1) // QB
    return (blk + g) >= 2


def _att_fwd(q, k, v, bias, comms=()):
    t = q.shape[0]

    def body(q_ref, k0, k1, k2, v0, v1, v2, b_ref, y_ref):
        valid = _att_valid()
        first = lax.broadcasted_iota(jnp.int32, (1, 128), 1) < 64
        for p in range(HEADS // 2):
            lanes = slice(p * 128, (p + 1) * 128)
            qp = q_ref[:, lanes]
            kp = jnp.concatenate([k0[:, lanes], k1[:, lanes], k2[:, lanes]], axis=0)
            vp = jnp.concatenate([v0[:, lanes], v1[:, lanes], v2[:, lanes]], axis=0)
            out = jnp.zeros((QB, 128), F32)
            for hh in range(2):
                mask = first if hh == 0 else jnp.logical_not(first)
                pr = _att_probs(jnp.where(mask, qp, 0), kp, b_ref[2 * p + hh], valid)
                out = out + _nn(pr.astype(BF16), jnp.where(mask, vp, 0))
            y_ref[:, lanes] = out.astype(BF16)

    qspec, kspecs, bspec = _att_specs()
    (out,), got = _call(
        body, name="att_fwd", grid=(t // QB,),
        in_specs=[qspec] + kspecs + kspecs + [bspec], out_specs=[qspec],
        out_shape=[jax.ShapeDtypeStruct((t, D_ATT), BF16)], scratch_shapes=[],
        args=(q, k, k, k, v, v, v, bias), comms=comms)
    return out, got


def _att_bwd(q, k, v, bias, dy, comms=()):
    t = q.shape[0]
    n_blocks = t // QB

    def body(q_ref, k0, k1, k2, v0, v1, v2, b_ref, dy_ref, dq_ref, dk_ref, dv_ref, db_ref, dk_acc, dv_acc):
        g = pl.program_id(0)

        @pl.when(g == 0)
        def _():
            db_ref[...] = jnp.zeros_like(db_ref)
            dk_acc[...] = jnp.zeros_like(dk_acc)
            dv_acc[...] = jnp.zeros_like(dv_acc)

        valid = _att_valid()
        first = lax.broadcasted_iota(jnp.int32, (1, 128), 1) < 64
        for p in range(HEADS // 2):
            lanes = slice(p * 128, (p + 1) * 128)
            qp = q_ref[:, lanes]
            dyp = dy_ref[:, lanes]
            kp = jnp.concatenate([k0[:, lanes], k1[:, lanes], k2[:, lanes]], axis=0)
            vp = jnp.concatenate([v0[:, lanes], v1[:, lanes], v2[:, lanes]], axis=0)
            dq = jnp.zeros((QB, 128), F32)
            dk = jnp.zeros((KW, 128), F32)
            dv = jnp.zeros((KW, 128), F32)
            for hh in range(2):
                mask = first if hh == 0 else jnp.logical_not(first)
                qm = jnp.where(mask, qp, 0)
                dym = jnp.where(mask, dyp, 0)
                pr = _att_probs(qm, kp, b_ref[2 * p + hh], valid)
                dp = _nt(dym, vp)
                ds = pr * (dp - jnp.sum(dp * pr, axis=-1, keepdims=True))
                db_ref[2 * p + hh] += ds
                dsb = ds.astype(BF16)
                dq = dq + _nn(dsb, jnp.where(mask, kp, 0))
                dk = dk + _tn(dsb, qm)
                dv = dv + _tn(pr.astype(BF16), dym)
            dq_ref[:, lanes] = (dq * 0.125).astype(BF16)
            for j in range(3):
                rows = pl.ds(pl.multiple_of(jnp.maximum(g - 2 + j, 0) * QB, QB), QB)
                dk_acc[rows, lanes] += dk[j * QB:(j + 1) * QB]
                dv_acc[rows, lanes] += dv[j * QB:(j + 1) * QB]

        @pl.when(g == n_blocks - 1)
        def _():
            dk_ref[...] = dk_acc[...].astype(BF16)
            dv_ref[...] = dv_acc[...].astype(BF16)

    qspec, kspecs, bspec = _att_specs()
    full = pl.BlockSpec((t, D_ATT), lambda g: (0, 0))
    return _call(
        body, name="att_bwd", grid=(n_blocks,),
        in_specs=[qspec] + kspecs + kspecs + [bspec, qspec], out_specs=[qspec, full, full, bspec],
        out_shape=[jax.ShapeDtypeStruct((t, D_ATT), BF16)] * 3 + [jax.ShapeDtypeStruct((HEADS, QB, KW), F32)],
        scratch_shapes=[pltpu.VMEM((t, D_ATT), F32)] * 2,
        args=(q, k, k, k, v, v, v, bias, dy), comms=comms)


def _rel_bias_grad(dbias):
    def body(db_ref, cs_ref, tot_ref):
        lane = lax.broadcasted_iota(jnp.int32, (1, SKEW_W), 1)
        capped = (lane < N_CAP) | (lane > KW)
        pad = jnp.zeros((8, QB), F32)
        for h in range(HEADS):
            z = jnp.concatenate([db_ref[h, 0:8, :], pad], axis=1)
            for a in range(1, QB // 8):
                z = z + pltpu.roll(jnp.concatenate([db_ref[h, 8 * a:8 * a + 8, :], pad], axis=1), SKEW_W - 8 * a, 1)
            cs = z[0:1, :]
            for b in range(1, 8):
                cs = cs + pltpu.roll(z[b:b + 1, :], SKEW_W - b, 1)
            cs_ref[h:h + 1, :] = cs
            tot_ref[h:h + 1, :] = jnp.broadcast_to(jnp.sum(jnp.where(capped, cs, 0.0), axis=1, keepdims=True), (1, 128))

    cs, tot = pl.pallas_call(
        body, name="rel_bias_grad",
        out_shape=[jax.ShapeDtypeStruct((HEADS, SKEW_W), F32), jax.ShapeDtypeStruct((HEADS, 128), F32)],
    )(dbias)
    return jnp.concatenate([cs[:, KW:N_CAP - 1:-1], tot[:, :1]], axis=1)


def _sgu_mask():
    pos = np.arange(SGU_BLOCK)
    return (pos[:, None] // CHUNK) >= (pos[None, :] // CHUNK)


def _group_stack(blk, first):
    return jnp.concatenate([jnp.where(first, blk, 0), jnp.where(first, 0, blk)], axis=0)


def _sgu_norm(zs_ref, lng, lnb):
    zs = zs_ref[...].astype(F32)
    ga, th = _gelu(zs)
    u, vs = ga[:, :D_SGU], ga[:, D_SGU:]
    mu = jnp.mean(vs, axis=-1, keepdims=True)
    cen = vs - mu
    rstd = lax.rsqrt(jnp.mean(cen * cen, axis=-1, keepdims=True) + EPS)
    xhat = cen * rstd
    return zs, th, u, xhat, rstd, xhat * lng + lnb


def _sgu_mix(vb, wm2_ref, bsx, s_ref):
    first = lax.broadcasted_iota(jnp.int32, (1, 128), 1) < 64
    for n in range(TM // SGU_BLOCK):
        for p in range(4):
            blk = vb[n * 128:(n + 1) * 128, p * 128:(p + 1) * 128]
            s_ref[n * 128:(n + 1) * 128, p * 128:(p + 1) * 128] = _nn(wm2_ref[p], _group_stack(blk, first)) + bsx[:, p * 128:(p + 1) * 128]


def _merge_fwd(x, zs, gt, y_att, lng, lnb, wm2, bsx, gw):
    t = x.shape[0]

    def body(x_ref, zs_ref, gt_ref, ya_ref, lng_ref, lnb_ref, wm2_ref, bsx_ref, gw_ref, xo_ref, ys_ref, mg_ref,
             wbr, wo, s_scr, sems):
        @pl.when(pl.program_id(0) == 0)
        def _():
            for cp in _load_rows(gw_ref, wbr, OFF_BR, R_BR, sems.at[0]) + _load_rows(gw_ref, wo, OFF_WO, R_WO, sems.at[1]):
                cp.wait()

        _, _, u, _, _, vsn = _sgu_norm(zs_ref, lng_ref[...], lnb_ref[...])
        _sgu_mix(vsn.astype(BF16), wm2_ref, bsx_ref[...], s_scr)
        ys = (u * s_scr[...]).astype(BF16)
        ys_ref[...] = ys
        pa = _nt(ya_ref[...], wbr[:, :D_ATT])
        ps = _nt(ys, wbr[:, D_ATT:])
        mg = (gt_ref[:, :D].astype(F32) * pa + gt_ref[:, D:].astype(F32) * ps).astype(BF16)
        mg_ref[...] = mg
        xo_ref[...] = x_ref[...] + _nn(mg, wo[...])

    def tile(w):
        return pl.BlockSpec((TM, w), lambda i: (i, 0))

    def const(shape):
        return pl.BlockSpec(shape, lambda i: (0,) * len(shape))

    return pl.pallas_call(
        body, name="merge_fwd", grid=(t // TM,),
        in_specs=[tile(D), tile(2 * D_SGU), tile(2 * D), tile(D_ATT), const((1, D_SGU)), const((1, D_SGU)),
                  const((4, 128, 256)), const((128, D_SGU)), ANY],
        out_specs=[tile(D), tile(D_SGU), tile(D)],
        out_shape=[jax.ShapeDtypeStruct((t, D), F32), jax.ShapeDtypeStruct((t, D_SGU), BF16), jax.ShapeDtypeStruct((t, D), BF16)],
        scratch_shapes=[pltpu.VMEM((D, D), BF16), pltpu.VMEM((D, D), BF16), pltpu.VMEM((TM, D_SGU), F32),
                        pltpu.SemaphoreType.DMA((2, N_DEV))],
        compiler_params=_cparams(("arbitrary",)),
    )(x, zs, gt, y_att, lng, lnb, wm2, bsx, gw)


def _merge_bwd(dx, gt, y_att, y_sgu, gw):
    t = dx.shape[0]

    def body(dx_ref, gt_ref, ya_ref, ys_ref, gw_ref, dzg_ref, dya_ref, dys_ref, dpp_ref, dxb_ref, dbg_ref, wbr, wo, sems):
        @pl.when(pl.program_id(0) == 0)
        def _():
            cps = _load_rows(gw_ref, wbr, OFF_BR, R_BR, sems.at[0]) + _load_rows(gw_ref, wo, OFF_WO, R_WO, sems.at[1])
            dbg_ref[...] = jnp.zeros_like(dbg_ref)
            for cp in cps:
                cp.wait()

        dxb = dx_ref[...].astype(BF16)
        dxb_ref[...] = dxb
        dm = _nt(dxb, wo[...])
        for half, y_ref, w in ((0, ya_ref, wbr.at[:, :D_ATT]), (1, ys_ref, wbr.at[:, D_ATT:])):
            cols = slice(half * D, (half + 1) * D)
            gate = gt_ref[:, cols].astype(F32)
            branch = _nt(y_ref[...], w[...])
            dzg = dm * branch * gate * (1.0 - gate)
            dbg_ref[:, cols] += jnp.sum(dzg, axis=0, keepdims=True)
            dzg_ref[:, cols] = dzg.astype(BF16)
            dbr = (dm * gate).astype(BF16)
            dpp_ref[:, cols] = dbr
            dy = _nn(dbr, w[...])
            if half == 0:
                dya_ref[...] = dy.astype(BF16)
            else:
                dys_ref[...] = dy

    def tile(w):
        return pl.BlockSpec((TM, w), lambda i: (i, 0))

    return pl.pallas_call(
        body, name="merge_bwd", grid=(t // TM,),
        in_specs=[tile(D), tile(2 * D), tile(D_ATT), tile(D_SGU), ANY],
        out_specs=[tile(2 * D), tile(D_ATT), tile(D_SGU), tile(2 * D), tile(D), pl.BlockSpec((1, 2 * D), lambda i: (0, 0))],
        out_shape=[jax.ShapeDtypeStruct((t, 2 * D), BF16), jax.ShapeDtypeStruct((t, D_ATT), BF16), jax.ShapeDtypeStruct((t, D_SGU), F32),
                   jax.ShapeDtypeStruct((t, 2 * D), BF16), jax.ShapeDtypeStruct((t, D), BF16), jax.ShapeDtypeStruct((1, 2 * D), F32)],
        scratch_shapes=[pltpu.VMEM((D, D), BF16), pltpu.VMEM((D, D), BF16), pltpu.SemaphoreType.DMA((2, N_DEV))],
        compiler_params=_cparams(("arbitrary",)),
    )(dx, gt, y_att, y_sgu, gw)


def _sgu_bwd(zs, dys, lng, lnb, wm2, wmt2, bsx, comms=()):
    t = zs.shape[0]
    n_steps = t // TM

    def body(zs_ref, dys_ref, lng_ref, lnb_ref, wm2_ref, wmt2_ref, bsx_ref, dzs_ref, dw_ref, dbs_ref, dlg_ref, dlb_ref,
             s_scr, dv_scr, ds_acc):
        i = pl.program_id(0)

        @pl.when(i == 0)
        def _():
            dw_ref[...] = jnp.zeros_like(dw_ref)
            dlg_ref[...] = jnp.zeros_like(dlg_ref)
            dlb_ref[...] = jnp.zeros_like(dlb_ref)
            ds_acc[...] = jnp.zeros_like(ds_acc)

        lng = lng_ref[...]
        zs, th, u, xhat, rstd, vsn = _sgu_norm(zs_ref, lng, lnb_ref[...])
        vb = vsn.astype(BF16)
        _sgu_mix(vb, wm2_ref, bsx_ref[...], s_scr)
        dys = dys_ref[...]
        du = dys * s_scr[...]
        ds = dys * u
        dsb = ds.astype(BF16)
        first = lax.broadcasted_iota(jnp.int32, (1, 128), 1) < 64
        acc = jnp.zeros((SGU_BLOCK, D_SGU), F32)
        for n in range(TM // SGU_BLOCK):
            rows = slice(n * 128, (n + 1) * 128)
            acc = acc + ds[rows]
            for p in range(4):
                lanes = slice(p * 128, (p + 1) * 128)
                stack = _group_stack(dsb[rows, lanes], first)
                dv_scr[rows, lanes] = _nn(wmt2_ref[p], stack)
                dw_ref[p] += _nt(stack, vb[rows, lanes])
        ds_acc[...] += acc
        dvsn = dv_scr[...]
        dlg_ref[...] += jnp.sum(dvsn * xhat, axis=0, keepdims=True)
        dlb_ref[...] += jnp.sum(dvsn, axis=0, keepdims=True)
        dxh = dvsn * lng
        dvs = rstd * (dxh - jnp.mean(dxh, axis=-1, keepdims=True) - xhat * jnp.mean(dxh * xhat, axis=-1, keepdims=True))
        dga = jnp.concatenate([du, dvs], axis=1)
        dzs_ref[...] = (dga * _gelu_grad(zs, th)).astype(BF16)

        @pl.when(i == n_steps - 1)
        def _():
            r = lax.broadcasted_iota(jnp.int32, (256, 128), 0) % SGU_BLOCK
            c = lax.broadcasted_iota(jnp.int32, (256, 128), 1)
            keep = (r // CHUNK) >= (c // CHUNK)
            for p in range(4):
                dw_ref[p] = jnp.where(keep, dw_ref[p], 0.0)
            total = ds_acc[...]
            grp = lax.broadcasted_iota(jnp.int32, (SGU_BLOCK, D_SGU), 1) // 64
            lane = lax.broadcasted_iota(jnp.int32, (SGU_BLOCK, 128), 1)
            out = jnp.zeros((SGU_BLOCK, 128), F32)
            for gi in range(8):
                out = jnp.where(lane == gi, jnp.sum(jnp.where(grp == gi, total, 0.0), axis=1, keepdims=True), out)
            dbs_ref[...] = out

    def tile(w):
        return pl.BlockSpec((TM, w), lambda i: (i, 0))

    def const(shape):
        return pl.BlockSpec(shape, lambda i: (0,) * len(shape))

    return _call(
        body, name="sgu_bwd", grid=(n_steps,),
        in_specs=[tile(2 * D_SGU), tile(D_SGU), const((1, D_SGU)), const((1, D_SGU)), const((4, 128, 256)), const((4, 128, 256)),
                  const((128, D_SGU))],
        out_specs=[tile(2 * D_SGU), const((4, 256, 128)), const((128, 128)), const((1, D_SGU)), const((1, D_SGU))],
        out_shape=[jax.ShapeDtypeStruct((t, 2 * D_SGU), BF16), jax.ShapeDtypeStruct((4, 256, 128), F32),
                   jax.ShapeDtypeStruct((128, 128), F32), jax.ShapeDtypeStruct((1, D_SGU), F32), jax.ShapeDtypeStruct((1, D_SGU), F32)],
        scratch_shapes=[pltpu.VMEM((TM, D_SGU), F32), pltpu.VMEM((TM, D_SGU), F32), pltpu.VMEM((SGU_BLOCK, D_SGU), F32)],
        args=(zs, dys, lng, lnb, wm2, wmt2, bsx), comms=comms)


def _adamw(g, w, m, v):
    m = ADAM_B1 * m + (1.0 - ADAM_B1) * g
    v = ADAM_B2 * v + (1.0 - ADAM_B2) * (g * g)
    m_hat = m / (1.0 - ADAM_B1 ** ADAM_STEP)
    v_hat = v / (1.0 - ADAM_B2 ** ADAM_STEP)
    return -ADAM_LR * (m_hat / (jnp.sqrt(v_hat) + ADAM_EPS) + ADAM_WD * w), m, v


def _adamw_matrix(parts, sums, w, m, v, transposed, name):
    _, r, c = parts.shape
    tc = 256

    def body(own_ref, p_ref, s_ref, w_ref, m_ref, v_ref, g_ref, d_ref, mo_ref, vo_ref):
        g = p_ref[0].astype(F32) + p_ref[1].astype(F32) + p_ref[2].astype(F32) + s_ref[...].astype(F32)
        g = g.T if transposed else g
        g_ref[...] = g
        d_ref[...], mo_ref[...], vo_ref[...] = _adamw(g, w_ref[...], m_ref[...], v_ref[...])

    own = pl.BlockSpec((None, tc, r), lambda i, o: (0, i, 0)) if transposed else pl.BlockSpec((None, r, tc), lambda i, o: (0, 0, i))
    return pl.pallas_call(
        body, name=name,
        grid_spec=pltpu.PrefetchScalarGridSpec(
            num_scalar_prefetch=1, grid=(c // tc,),
            in_specs=[pl.BlockSpec((3, r, tc), lambda i, o: (0, 0, i)), pl.BlockSpec((None, r, tc), lambda i, o: (o[0], 0, i)), own, own, own],
            out_specs=[own] * 4),
        out_shape=[jax.ShapeDtypeStruct(w.shape, F32)] * 4,
        compiler_params=_cparams(("arbitrary",)),
    )(_my_index(("x", 2), ("y", 1)), parts, sums, w, m, v)


_SMALL_2D = {"norm_ffn1": (1, D), "norm_mix": (1, D), "norm_ffn2": (1, D), "norm_final": (1, D), "b_gate": (1, 2 * D),
             "sgu_ln_g": (1, D_SGU), "sgu_ln_b": (1, D_SGU), "sgu_b_s": (8, SGU_BLOCK), "rel_bias": (HEADS, N_REL),
             "sgu_w_s": (8 * SGU_BLOCK, SGU_BLOCK)}


def _adamw_small(parts, loss_parts, p):
    names = list(parts)
    n = len(names)

    def body(*refs):
        got, loss_got, wmv, outs, loss_out = refs[:n], refs[n], refs[n + 1:4 * n + 1], refs[4 * n + 1:8 * n + 1], refs[8 * n + 1]
        for i, name in enumerate(names):
            g = got[i][0]
            for k in range(1, N_DEV):
                g = g + got[i][k]
            if name == "sgu_b_s":
                g = g.T[0:8, :]
            res = (g,) + _adamw(g, wmv[3 * i][...], wmv[3 * i + 1][...], wmv[3 * i + 2][...])
            for o_ref, val in zip(outs[4 * i:4 * i + 4], res):
                o_ref[...] = val
        total = loss_got[0]
        for k in range(1, N_DEV):
            total = total + loss_got[k]
        loss_out[...] = total

    wmv = [p[pre + name].reshape(_SMALL_2D[name]) for name in names for pre in ("", "m_", "v_")]
    res = pl.pallas_call(
        body, name="adamw_small",
        out_shape=[jax.ShapeDtypeStruct(_SMALL_2D[name], F32) for name in names for _ in range(4)] + [jax.ShapeDtypeStruct((1, 128), F32)],
        compiler_params=_cparams())(*[parts[name] for name in names], loss_parts, *wmv)
    return [{name: res[4 * i + j].reshape(p[name].shape) for i, name in enumerate(names)} for j in range(4)], res[-1]


def _pack_rows(groups, name):
    flat = [a for grp in groups for a, _ in grp]
    rows = [grp[0][0].shape[2] if grp[0][1] else grp[0][0].shape[1] for grp in groups]

    def body(*refs):
        o_ref, pos, off = refs[-1], 0, 0
        for grp, r in zip(groups, rows):
            vals = []
            for _, transposed in grp:
                val = refs[pos][0]
                vals.append(val.T if transposed else val)
                pos += 1
            o_ref[off:off + r, :] = (vals[0] if len(vals) == 1 else jnp.concatenate(vals, axis=1)).astype(BF16)
            off += r

    return pl.pallas_call(body, name=name, out_shape=jax.ShapeDtypeStruct((sum(rows), D), BF16), compiler_params=_cparams())(*flat)


def _step(x, target, p):
    n1, nm, n2 = p["norm_ffn1"], p["norm_mix"], p["norm_ffn2"]
    nf = p["norm_final"].reshape(1, D)
    lng, lnb = p["sgu_ln_g"], p["sgu_ln_b"]
    w_m = jnp.where(jnp.asarray(_sgu_mask())[None], p["sgu_w_s"][0], 0.0).astype(BF16)
    wm2 = jnp.concatenate([w_m[0::2], w_m[1::2]], axis=2)
    w_mt = w_m.transpose(0, 2, 1)
    wmt2 = jnp.concatenate([w_mt[0::2], w_mt[1::2]], axis=2)
    bsx = jnp.repeat(p["sgu_b_s"][0].T, 64, axis=1)
    bias = _band_bias(p["rel_bias"][0])

    def chip_sums(grads, name):
        gots = _comm_only([_SiblingSwap(grads)], "swap_" + name)
        return _pair_sums(grads, gots, "pair_sums_" + name)

    def as_rows(a):
        return jnp.swapaxes(a, 1, 2)

    def updates(parts, sums, names):
        res = {}
        for pt, sm, n in zip(parts, sums, names):
            if p[n].shape[1:] == pt.shape[1:]:
                res[n] = _adamw_matrix(pt, sm, p[n], p["m_" + n], p["v_" + n], False, "adamw_" + n)
            elif p[n].shape[2] > 128:
                res[n] = [as_rows(o) for o in _adamw_matrix(pt, sm, as_rows(p[n]), as_rows(p["m_" + n]), as_rows(p["v_" + n]), False,
                                                            "adamw_" + n)]
            else:
                res[n] = _adamw_matrix(pt, sm, p[n], p["m_" + n], p["v_" + n], True, "adamw_" + n)
        return res

    rows1 = _pack_rows([[(as_rows(p["ffn1_w_gate"]), False)], [(as_rows(p["ffn1_w_up"]), False)], [(p["ffn1_w_down"], False)]], "pack_ffn1")
    rows_m = _pack_rows([[(as_rows(p["w_in"]), False)], [(p["w_branch_att"], True), (p["w_branch_sgu"], True)], [(p["w_out"], False)]],
                        "pack_mixer")
    rows2d = _pack_rows([[(p["ffn2_w_down"], False)]], "pack_ffn2_down")
    rows2gu = _pack_rows([[(as_rows(p["ffn2_w_gate"]), False)], [(as_rows(p["ffn2_w_up"]), False)]], "pack_ffn2_gate_up")
    (gw1,) = _comm_only([_Gather(rows1)], "gather_ffn1")
    x1, ab1, h1, (gwm,) = _ffn_fwd(x, n1, gw1, gw1, 2 * R_FF, "ffn1_fwd", [_Gather(rows_m)])
    (q, k, v, zs, gt, h2), (gw2d,) = _mix_proj_fwd(x1, nm, p["b_gate"], gwm, [_Gather(rows2d)])
    y_att, (gw2gu,) = _att_fwd(q, k, v, bias, [_Gather(rows2gu)])
    x2, y_sgu, merged = _merge_fwd(x1, zs, gt, y_att, lng, lnb, wm2, bsx, gwm)
    dx3, ab2, hb, _, d_nf, loss = _ffn_fwd(x2, n2, gw2gu, gw2d, 0, "ffn2_fwd", head=(nf, target))

    dab, sb, dfb = _ffn_bwd_hidden(ab2, dx3, gw2d, 0, "ffn2_bwd_hidden")
    (dx2, d_n2), _ = _ffn_bwd_input(x2, n2, dab, dx3, gw2gu, "ffn2_bwd")
    g2 = [_weight_grad(dab, hb, "ffn2_dw_gate", 0, F), _weight_grad(dab, hb, "ffn2_dw_up", F, F), _weight_grad(sb, dfb, "ffn2_dw_down")]
    dzg, dya, dys, dpp, dxb, d_bg = _merge_bwd(dx2, gt, y_att, y_sgu, gwm)
    g_late = g2 + [_weight_grad(dpp, y_att, "dw_branch_att", 0, D), _weight_grad(dpp, y_sgu, "dw_branch_sgu", D, D),
                   _weight_grad(merged, dxb, "dw_out")]
    late = ("ffn2_w_gate", "ffn2_w_up", "ffn2_w_down", "w_branch_att", "w_branch_sgu", "w_out")
    (dzs, d_wm, d_bs, d_lng, d_lnb), gots_late = _sgu_bwd(zs, dys, lng, lnb, wm2, wmt2, bsx, [_SiblingSwap(g_late)])
    sums_late = _pair_sums(g_late, gots_late, "pair_sums_late")
    (dq, dk, dv, d_bias), parts_late = _att_bwd(q, k, v, bias, dya, [_ChipScatter(sums_late)])
    big = updates(parts_late, sums_late, late)
    d_rel = _rel_bias_grad(d_bias)
    dz = jnp.concatenate([dq, dk, dv, dzs, dzg], axis=1)
    small = {"norm_ffn2": d_n2, "norm_final": d_nf, "b_gate": d_bg, "sgu_ln_g": d_lng, "sgu_ln_b": d_lnb, "sgu_b_s": d_bs,
             "rel_bias": d_rel, "sgu_w_s": d_wm.reshape(_SMALL_2D["sgu_w_s"])}
    g_in, (*early_parts, loss_parts) = _weight_grad(dz, h2, "dw_in", comms=[_AllToAll(list(small.values()) + [loss])])
    sums_in = chip_sums([g_in], "w_in")
    (dx1, d_nm), parts_in = _mix_proj_bwd(dz, x1, nm, dx2, gwm, [_ChipScatter(sums_in)])
    big.update(updates(parts_in, sums_in, ("w_in",)))

    dab, sb, dfb = _ffn_bwd_hidden(ab1, dx1, gw1, 2 * R_FF, "ffn1_bwd_hidden")
    sums_d = chip_sums([_weight_grad(sb, dfb, "ffn1_dw_down")], "ffn1_down")
    g_gate, parts_d = _weight_grad(dab, h1, "ffn1_dw_gate", 0, F, [_ChipScatter(sums_d)])
    sums_g = chip_sums([g_gate], "ffn1_gate")
    g_up, (parts_g, nm_parts) = _weight_grad(dab, h1, "ffn1_dw_up", F, F, [_ChipScatter(sums_g), _AllToAll([d_nm])])
    sums_u = chip_sums([g_up], "ffn1_up")
    (dx0, d_n1), parts_u = _ffn_bwd_input(x, n1, dab, dx1, gw1, "ffn1_bwd", [_ChipScatter(sums_u)])
    (n1_parts,) = _comm_only([_AllToAll([d_n1])], "gather_norm_ffn1")
    big.update(updates(parts_d + [parts_g] + parts_u, sums_d + sums_g + sums_u, ("ffn1_w_down", "ffn1_w_gate", "ffn1_w_up")))
    out_s, loss_sum = _adamw_small(dict(zip(small, early_parts), norm_mix=nm_parts, norm_ffn1=n1_parts), loss_parts, p)
    return dx0, loss_sum[0, 0], [{**{n: four[i] for n, four in big.items()}, **s} for i, s in enumerate(out_s)]


_OUT_ORDER = ("norm_ffn1", "ffn1_w_gate", "ffn1_w_up", "ffn1_w_down", "norm_mix", "w_in", "b_gate", "rel_bias", "sgu_ln_g", "sgu_ln_b",
              "sgu_w_s", "sgu_b_s", "w_branch_att", "w_branch_sgu", "w_out", "norm_ffn2", "ffn2_w_gate", "ffn2_w_up", "ffn2_w_down",
              "norm_final")


def kernel(x, norm_ffn1, ffn1_w_gate, ffn1_w_up, ffn1_w_down, norm_mix, w_in, b_gate, rel_bias, sgu_ln_g, sgu_ln_b, sgu_w_s, sgu_b_s, w_branch_att, w_branch_sgu, w_out, norm_ffn2, ffn2_w_gate, ffn2_w_up, ffn2_w_down, norm_final, loss_target, m_norm_ffn1, m_ffn1_w_gate, m_ffn1_w_up, m_ffn1_w_down, m_norm_mix, m_w_in, m_b_gate, m_rel_bias, m_sgu_ln_g, m_sgu_ln_b, m_sgu_w_s, m_sgu_b_s, m_w_branch_att, m_w_branch_sgu, m_w_out, m_norm_ffn2, m_ffn2_w_gate, m_ffn2_w_up, m_ffn2_w_down, m_norm_final, v_norm_ffn1, v_ffn1_w_gate, v_ffn1_w_up, v_ffn1_w_down, v_norm_mix, v_w_in, v_b_gate, v_rel_bias, v_sgu_ln_g, v_sgu_ln_b, v_sgu_w_s, v_sgu_b_s, v_w_branch_att, v_w_branch_sgu, v_w_out, v_norm_ffn2, v_ffn2_w_gate, v_ffn2_w_up, v_ffn2_w_down, v_norm_final):
    args = dict(locals())
    dx, loss, outs = _step(x[0], loss_target[0], {pre + n: args[pre + n] for pre in ("", "m_", "v_") for n in _OUT_ORDER})
    return (loss, dx[None], *[o[n] for o in outs for n in _OUT_ORDER])
```

```python
import functools

import numpy as np
import jax
import jax.numpy as jnp
from jax import lax
from jax.experimental import pallas as pl
from jax.experimental.pallas import tpu as pltpu

F32 = jnp.float32
BF16 = jnp.bfloat16

N_DEV = 8
D = 1024
F = 2816
D_ATT = 512
D_SGU = 512
D_IN = 4608
HEADS = 8
CHUNK = 64
N_LEFT = 8
REL_CLIP = 256
N_REL = 2 * REL_CLIP + 1
SGU_BLOCK = 128
EPS = 1e-6
NEG_INF = -1e30
QB = 256
KW = 3 * QB

R_FF, R_IN, R_BR, R_WO = F // N_DEV, D_IN // N_DEV, D // N_DEV, D // N_DEV
OFF_IN, OFF_BR, OFF_WO = 0, R_IN, R_IN + R_BR
FFN_ROWS = 3 * R_FF
MIX_ROWS = R_IN + R_BR + R_WO

FC = 256
TM = 512
VMEM_LIMIT = 56 * 1024 * 1024

ADAM_LR, ADAM_B1, ADAM_B2, ADAM_EPS, ADAM_WD, ADAM_STEP = 0.001, 0.9, 0.999, 1e-08, 0.01, 10

MESH = pl.DeviceIdType.MESH
ANY = pl.BlockSpec(memory_space=pl.ANY)


def _nt(a, b):
    return lax.dot_general(a, b, (((1,), (1,)), ((), ())), preferred_element_type=F32)


def _tn(a, b):
    return lax.dot_general(a, b, (((0,), (0,)), ((), ())), preferred_element_type=F32)


def _nn(a, b):
    return jnp.dot(a, b, preferred_element_type=F32)


def _cparams(sem=None):
    return pltpu.CompilerParams(dimension_semantics=sem, vmem_limit_bytes=VMEM_LIMIT)


def _load_rows(gw_ref, dst, off, rows, sems):
    copies = [pltpu.make_async_copy(gw_ref.at[k, pl.ds(off, rows), :], dst.at[pl.ds(k * rows, rows), :], sems.at[k])
              for k in range(N_DEV)]
    for cp in copies:
        cp.start()
    return copies


def _rms(xv):
    r = lax.rsqrt(jnp.mean(xv * xv, axis=-1, keepdims=True) + EPS)
    return xv * r, r


def _rms_bwd(dh, xn, r, gain):
    dxn = dh * gain
    dx = r * (dxn - xn * jnp.mean(dxn * xn, axis=-1, keepdims=True))
    return dx, jnp.sum(dh * xn, axis=0, keepdims=True)


def _gelu(x):
    t = jnp.tanh(0.7978845608028654 * (x + 0.044715 * x * x * x))
    return 0.5 * x * (1.0 + t), t


def _gelu_grad(x, t):
    return 0.5 * (1.0 + t) + 0.5 * x * (1.0 - t * t) * 0.7978845608028654 * (1.0 + 3.0 * 0.044715 * x * x)


def _place():
    x, y, cc = lax.axis_index("x"), lax.axis_index("y"), lax.axis_index("c")
    return x, y, cc, [(1 - x, y), (x, 1 - y), (1 - x, 1 - y)]


class _Gather:
    def __init__(self, shard):
        self.inputs = [shard]
        self.out_shape = [jax.ShapeDtypeStruct((N_DEV,) + shard.shape, shard.dtype)]
        self.scratch = [pltpu.SemaphoreType.DMA((7,)), pltpu.SemaphoreType.DMA((7,)), pltpu.SemaphoreType.DMA]

    def _copies(self, ins, outs, scr):
        (x_ref,), (out_ref,), (send_sems, recv_sems, local_sem) = ins, outs, scr
        x, y, cc, chips = _place()

        def slab(px, py, pc):
            return out_ref.at[4 * px + 2 * py + pc]

        def copy(k, block, to, src=None):
            return pltpu.make_async_remote_copy(
                src_ref=slab(*block) if src is None else src, dst_ref=slab(*block),
                send_sem=send_sems.at[k], recv_sem=recv_sems.at[k], device_id=to, device_id_type=MESH)

        me, sibling = (x, y, cc), (x, y, 1 - cc)
        x_nbr, y_nbr, diagonal = chips
        mine = pltpu.make_async_copy(x_ref, slab(*me), local_sem)
        first = [copy(0, me, sibling, src=x_ref), copy(1, me, (*x_nbr, cc), src=x_ref), copy(2, me, (*y_nbr, cc), src=x_ref)]
        neighbours = [copy(1, (*x_nbr, cc), me), copy(2, (*y_nbr, cc), me)]
        second_hand = copy(3, (x ^ (1 - cc), y ^ cc, cc), (x ^ cc, y ^ (1 - cc), cc))
        from_diagonal = copy(3, (*diagonal, cc), me)
        passed = [copy(4 + j, (*chip, cc), sibling) for j, chip in enumerate(chips)]
        from_sibling = [copy(0, sibling, me)] + [copy(4 + j, (*chip, 1 - cc), me) for j, chip in enumerate(chips)]
        return mine, first, neighbours, second_hand, from_diagonal, passed, from_sibling

    def begin(self, *refs):
        mine, first = self._copies(*refs)[:2]
        mine.start()
        for cp in first:
            cp.start()

    def mid(self, *refs):
        _, _, neighbours, second_hand, _, passed, _ = self._copies(*refs)
        for cp in neighbours:
            cp.wait_recv()
        second_hand.start()
        passed[0].start()
        passed[1].start()

    def relay(self, *refs):
        _, _, _, _, from_diagonal, passed, _ = self._copies(*refs)
        from_diagonal.wait_recv()
        passed[2].start()

    def end(self, *refs):
        mine, first, _, second_hand, _, passed, from_sibling = self._copies(*refs)
        for cp in from_sibling:
            cp.wait_recv()
        for cp in first + [second_hand] + passed:
            cp.wait_send()
        mine.wait()


class _Direct:
    def begin(self, *refs):
        keep, give = self._copies(*refs)
        for cp in keep + give:
            cp.start()

    def mid(self, *refs):
        pass

    def relay(self, *refs):
        pass

    def end(self, *refs):
        keep, give = self._copies(*refs)
        for cp in give:
            cp.wait_recv()
        for cp in give:
            cp.wait_send()
        for cp in keep:
            cp.wait()


class _SiblingSwap(_Direct):
    def __init__(self, grads):
        n = len(grads)
        self.inputs = list(grads)
        self.out_shape = [jax.ShapeDtypeStruct((4,) + g.shape[1:], g.dtype) for g in grads]
        self.scratch = [pltpu.SemaphoreType.DMA((n, 4)), pltpu.SemaphoreType.DMA((n, 4))]

    def _copies(self, ins, outs, scr):
        send_sems, recv_sems = scr
        x, y, cc, _ = _place()
        return [], [pltpu.make_async_remote_copy(src_ref=g_ref.at[2 * j + 1 - cc], dst_ref=got_ref.at[j], send_sem=send_sems.at[i, j],
                                                 recv_sem=recv_sems.at[i, j], device_id=(x, y, 1 - cc), device_id_type=MESH)
                    for i, (g_ref, got_ref) in enumerate(zip(ins, outs)) for j in range(4)]


class _ChipScatter(_Direct):
    def __init__(self, sums):
        n = len(sums)
        self.inputs = list(sums)
        self.out_shape = [jax.ShapeDtypeStruct((3,) + s.shape[1:], s.dtype) for s in sums]
        self.scratch = [pltpu.SemaphoreType.DMA((n, 3)), pltpu.SemaphoreType.DMA((n, 3))]

    def _copies(self, ins, outs, scr):
        send_sems, recv_sems = scr
        _, _, cc, chips = _place()
        return [], [pltpu.make_async_remote_copy(src_ref=s_ref.at[2 * px + py], dst_ref=got_ref.at[j], send_sem=send_sems.at[i, j],
                                                 recv_sem=recv_sems.at[i, j], device_id=(px, py, cc), device_id_type=MESH)
                    for i, (s_ref, got_ref) in enumerate(zip(ins, outs)) for j, (px, py) in enumerate(chips)]


class _AllToAll(_Direct):
    def __init__(self, blocks):
        n = len(blocks)
        self.inputs = list(blocks)
        self.out_shape = [jax.ShapeDtypeStruct((N_DEV,) + b.shape, b.dtype) for b in blocks]
        self.scratch = [pltpu.SemaphoreType.DMA((n, 7)), pltpu.SemaphoreType.DMA((n, 7)), pltpu.SemaphoreType.DMA((n,))]

    def _copies(self, ins, outs, scr):
        send_sems, recv_sems, local_sems = scr
        x, y, cc, _ = _place()
        me = 4 * x + 2 * y + cc
        keep = [pltpu.make_async_copy(b_ref, got_ref.at[me], local_sems.at[i]) for i, (b_ref, got_ref) in enumerate(zip(ins, outs))]
        give = [pltpu.make_async_remote_copy(src_ref=b_ref, dst_ref=got_ref.at[me], send_sem=send_sems.at[i, k - 1],
                                             recv_sem=recv_sems.at[i, k - 1],
                                             device_id=(x ^ ((k >> 2) & 1), y ^ ((k >> 1) & 1), cc ^ (k & 1)), device_id_type=MESH)
                for i, (b_ref, got_ref) in enumerate(zip(ins, outs)) for k in range(1, N_DEV)]
        return keep, give


def _split_refs(refs, counts):
    out, pos = [], 0
    for n in counts:
        out.append(list(refs[pos:pos + n]))
        pos += n
    return out


def _bind(comms, c_in, c_out, c_scr):
    ins = _split_refs(c_in, [len(c.inputs) for c in comms])
    outs = _split_refs(c_out, [len(c.out_shape) for c in comms])
    scr = _split_refs(c_scr, [len(c.scratch) for c in comms])
    return [(c, (i, o, s)) for c, i, o, s in zip(comms, ins, outs, scr)]


def _call(body, *, name, grid, in_specs, out_specs, out_shape, scratch_shapes, args, comms=()):
    c_in = [a for c in comms for a in c.inputs]
    c_out = [s for c in comms for s in c.out_shape]
    c_scr = [s for c in comms for s in c.scratch]
    counts = [len(in_specs), len(c_in), len(out_shape), len(c_out), len(scratch_shapes), len(c_scr)]

    def full(*refs):
        ins, cin, outs, cout, scr, cscr = _split_refs(refs, counts)
        bound = _bind(comms, cin, cout, cscr)
        if comms:
            def at(steps):
                return functools.reduce(jnp.logical_and, [pl.program_id(ax) == s for ax, s in enumerate(steps)])

            first, last = at([0] * len(grid)), at([n - 1 for n in grid])

            @pl.when(first)
            def _():
                for c, r in bound:
                    c.begin(*r)

            @pl.when(at([(grid[0] - 1) // 2] + [0] * (len(grid) - 1)))
            def _():
                for c, r in bound:
                    c.mid(*r)

            @pl.when(last)
            def _():
                for c, r in bound:
                    c.relay(*r)

        body(*ins, *outs, *scr)
        if comms:
            @pl.when(last)
            def _():
                for c, r in bound:
                    c.end(*r)

    res = pl.pallas_call(
        full, name=name, grid=grid,
        in_specs=list(in_specs) + [ANY] * len(c_in), out_specs=list(out_specs) + [ANY] * len(c_out),
        out_shape=list(out_shape) + c_out, scratch_shapes=list(scratch_shapes) + c_scr,
        compiler_params=_cparams(("arbitrary",) * len(grid)),
    )(*args, *c_in)
    return list(res[:len(out_shape)]), list(res[len(out_shape):])


def _comm_only(comms, name):
    c_in = [a for c in comms for a in c.inputs]
    c_out = [s for c in comms for s in c.out_shape]
    c_scr = [s for c in comms for s in c.scratch]

    def full(*refs):
        cin, cout, cscr = _split_refs(refs, [len(c_in), len(c_out), len(c_scr)])
        bound = _bind(comms, cin, cout, cscr)
        for phase in ("begin", "mid", "relay", "end"):
            for c, r in bound:
                getattr(c, phase)(*r)

    return list(pl.pallas_call(full, name=name, in_specs=[ANY] * len(c_in), out_specs=[ANY] * len(c_out), out_shape=c_out,
                               scratch_shapes=c_scr)(*c_in))


def _my_index(*axes_and_weights):
    return sum(w * lax.axis_index(a) for a, w in axes_and_weights).astype(jnp.int32).reshape(1)


def _pair_sums(grads, gots, name):
    n = len(grads)

    def body(c_ref, *refs):
        for a_ref, b_ref, o_ref in zip(refs[:n], refs[n:2 * n], refs[2 * n:]):
            o_ref[...] = (a_ref[...].astype(F32) + b_ref[...].astype(F32)).astype(BF16)

    def tile(g):
        return pl.BlockSpec((1,) + g.shape[1:], lambda j, c_ref: (j, 0, 0))

    def mine(g):
        return pl.BlockSpec((1, None) + g.shape[1:], lambda j, c_ref: (j, c_ref[0], 0, 0))

    return list(pl.pallas_call(
        body, name=name,
        grid_spec=pltpu.PrefetchScalarGridSpec(num_scalar_prefetch=1, grid=(4,), in_specs=[mine(g) for g in gots] + [tile(g) for g in gots],
                                               out_specs=[tile(g) for g in gots]),
        out_shape=[jax.ShapeDtypeStruct(g.shape, BF16) for g in gots],
        compiler_params=_cparams(("arbitrary",)))(_my_index(("c", 1)), *[g.reshape((4, 2) + g.shape[1:]) for g in grads], *gots))


def _ffn_fwd(x, gain, gw_gu, gw_d, off_d, name, comms=(), head=None):
    t = x.shape[0]
    n_head = 0 if head is None else 2

    def body(x_ref, g_ref, gu_ref, d_ref, *refs):
        head_in, (o_ref, ab_ref, h_ref), head_out = refs[:n_head], refs[n_head:n_head + 3], refs[n_head + 3:2 * n_head + 3]
        wg, wu, wd, s_scr, sems = refs[2 * n_head + 3:]

        @pl.when(pl.program_id(0) == 0)
        def _():
            cps = _load_rows(gu_ref, wg, 0, R_FF, sems.at[0]) + _load_rows(gu_ref, wu, R_FF, R_FF, sems.at[1]) \
                + _load_rows(d_ref, wd, off_d, R_FF, sems.at[2])
            for o in head_out:
                o[...] = jnp.zeros_like(o)
            for cp in cps:
                cp.wait()

        xv = x_ref[...]
        xn, _ = _rms(xv)
        h = (xn * g_ref[...]).astype(BF16)
        h_ref[...] = h
        for c in range(F // FC):
            rows = pl.ds(c * FC, FC)
            a = _nt(h, wg[rows, :])
            b = _nt(h, wu[rows, :])
            ab_ref[:, c * FC:(c + 1) * FC] = a.astype(BF16)
            ab_ref[:, F + c * FC:F + (c + 1) * FC] = b.astype(BF16)
            s_scr[:, c * FC:(c + 1) * FC] = (a * jax.nn.sigmoid(a) * b).astype(BF16)
        out = xv + 0.5 * _nn(s_scr[...], wd[...])
        if head is None:
            o_ref[...] = out
        else:
            (gf_ref, t_ref), (dg_ref, loss_ref) = head_in, head_out
            gain_f = gf_ref[...]
            yn, r = _rms(out)
            err = yn * gain_f - t_ref[...]
            loss_ref[...] += 0.5 * jnp.sum(jnp.mean(err * err, axis=-1, keepdims=True), axis=0, keepdims=True)
            o_ref[...], dg = _rms_bwd(err * (1.0 / D), yn, r, gain_f)
            dg_ref[...] += dg

    tile = pl.BlockSpec((TM, D), lambda i: (i, 0))
    row = pl.BlockSpec((1, D), lambda i: (0, 0))
    head_specs = [] if head is None else [row, pl.BlockSpec((1, 128), lambda i: (0, 0))]
    head_shapes = [] if head is None else [jax.ShapeDtypeStruct((1, D), F32), jax.ShapeDtypeStruct((1, 128), F32)]
    res, got = _call(
        body, name=name, grid=(t // TM,),
        in_specs=[tile, row, ANY, ANY] + ([] if head is None else [row, tile]),
        out_specs=[tile, pl.BlockSpec((TM, 2 * F), lambda i: (i, 0)), tile] + head_specs,
        out_shape=[jax.ShapeDtypeStruct((t, D), F32), jax.ShapeDtypeStruct((t, 2 * F), BF16), jax.ShapeDtypeStruct((t, D), BF16)] + head_shapes,
        scratch_shapes=[pltpu.VMEM((F, D), BF16)] * 3 + [pltpu.VMEM((TM, F), BF16), pltpu.SemaphoreType.DMA((3, N_DEV))],
        args=(x, gain, gw_gu, gw_d) + (() if head is None else tuple(head)), comms=comms)
    return (res[0], res[1], res[2], got, *res[3:])


def _ffn_bwd_hidden(ab, dout, gw_d, off_d, name):
    t = ab.shape[0]

    def tile(w):
        return pl.BlockSpec((TM, w), lambda i: (i, 0))

    def hidden(ab_ref, do_ref, d_ref, dab_ref, s_ref, df_ref, wd, sems):
        @pl.when(pl.program_id(0) == 0)
        def _():
            for cp in _load_rows(d_ref, wd, off_d, R_FF, sems):
                cp.wait()

        df = (0.5 * do_ref[...]).astype(BF16)
        df_ref[...] = df
        for c in range(F // FC):
            a = ab_ref[:, c * FC:(c + 1) * FC].astype(F32)
            b = ab_ref[:, F + c * FC:F + (c + 1) * FC].astype(F32)
            sg = jax.nn.sigmoid(a)
            sl = a * sg
            ds = _nt(df, wd[pl.ds(c * FC, FC), :])
            dab_ref[:, c * FC:(c + 1) * FC] = (ds * b * (sg * (1.0 + a * (1.0 - sg)))).astype(BF16)
            dab_ref[:, F + c * FC:F + (c + 1) * FC] = (ds * sl).astype(BF16)
            s_ref[:, c * FC:(c + 1) * FC] = (sl * b).astype(BF16)

    res, _ = _call(
        hidden, name=name, grid=(t // TM,),
        in_specs=[tile(2 * F), tile(D), ANY], out_specs=[tile(2 * F), tile(F), tile(D)],
        out_shape=[jax.ShapeDtypeStruct((t, 2 * F), BF16), jax.ShapeDtypeStruct((t, F), BF16), jax.ShapeDtypeStruct((t, D), BF16)],
        scratch_shapes=[pltpu.VMEM((F, D), BF16), pltpu.SemaphoreType.DMA((N_DEV,))],
        args=(ab, dout, gw_d))
    return res


def _ffn_bwd_input(x, gain, dab, dout, gw_gu, name, comms=()):
    t = x.shape[0]

    def body(x_ref, g_ref, dab_ref, do_ref, gu_ref, dx_ref, dg_ref, wg, wu, sems):
        @pl.when(pl.program_id(0) == 0)
        def _():
            cps = _load_rows(gu_ref, wg, 0, R_FF, sems.at[0]) + _load_rows(gu_ref, wu, R_FF, R_FF, sems.at[1])
            dg_ref[...] = jnp.zeros_like(dg_ref)
            for cp in cps:
                cp.wait()

        gain_v = g_ref[...]
        xn, r = _rms(x_ref[...])
        dh = _nn(dab_ref[:, :F], wg[...]) + _nn(dab_ref[:, F:], wu[...])
        dxn, dg = _rms_bwd(dh, xn, r, gain_v)
        dg_ref[...] += dg
        dx_ref[...] = do_ref[...] + dxn

    def tile(w):
        return pl.BlockSpec((TM, w), lambda i: (i, 0))

    row = pl.BlockSpec((1, D), lambda i: (0, 0))
    return _call(
        body, name=name, grid=(t // TM,),
        in_specs=[tile(D), row, tile(2 * F), tile(D), ANY], out_specs=[tile(D), row],
        out_shape=[jax.ShapeDtypeStruct((t, D), F32), jax.ShapeDtypeStruct((1, D), F32)],
        scratch_shapes=[pltpu.VMEM((F, D), BF16)] * 2 + [pltpu.SemaphoreType.DMA((2, N_DEV))],
        args=(x, gain, dab, dout, gw_gu), comms=comms)


def _weight_grad(a, b, name, col_off=0, m=None, comms=None, tmm=256):
    t = a.shape[0]
    m = a.shape[1] if m is None else m
    n = b.shape[1]
    first = col_off // tmm

    def body(a_ref, b_ref, o_ref):
        o_ref[...] = _tn(a_ref[...], b_ref[...]).astype(BF16)

    (out,), got = _call(
        body, name=name, grid=(m // tmm,),
        in_specs=[pl.BlockSpec((t, tmm), lambda i: (0, first + i)), pl.BlockSpec((t, n), lambda i: (0, 0))],
        out_specs=[pl.BlockSpec((tmm, n), lambda i: (i, 0))],
        out_shape=[jax.ShapeDtypeStruct((m, n), BF16)], scratch_shapes=[], args=(a, b), comms=comms or ())
    out = out.reshape(N_DEV, m // N_DEV, n)
    return out if comms is None else (out, got)


def _mix_proj_fwd(x, gain, b_gate, gw, comms=()):
    t = x.shape[0]

    def body(x_ref, g_ref, bg_ref, gw_ref, q_ref, k_ref, v_ref, zs_ref, gt_ref, h_ref, win, sems):
        @pl.when(pl.program_id(0) == 0)
        def _():
            for cp in _load_rows(gw_ref, win, OFF_IN, R_IN, sems):
                cp.wait()

        xn, _ = _rms(x_ref[...])
        h = (xn * g_ref[...]).astype(BF16)
        h_ref[...] = h
        q_ref[...] = (_nt(h, win[0:512, :]) * 0.125).astype(BF16)
        k_ref[...] = _nt(h, win[512:1024, :]).astype(BF16)
        v_ref[...] = _nt(h, win[1024:1536, :]).astype(BF16)
        for c in range(2):
            zs_ref[:, c * 512:(c + 1) * 512] = _nt(h, win[1536 + c * 512:2048 + c * 512, :]).astype(BF16)
        for c in range(4):
            zg = _nt(h, win[2560 + c * 512:3072 + c * 512, :]) + bg_ref[:, c * 512:(c + 1) * 512]
            gt_ref[:, c * 512:(c + 1) * 512] = jax.nn.sigmoid(zg).astype(BF16)

    def tile(w):
        return pl.BlockSpec((TM, w), lambda i: (i, 0))

    return _call(
        body, name="mix_proj_fwd", grid=(t // TM,),
        in_specs=[tile(D), pl.BlockSpec((1, D), lambda i: (0, 0)), pl.BlockSpec((1, 2 * D), lambda i: (0, 0)), ANY],
        out_specs=[tile(D_ATT), tile(D_ATT), tile(D_ATT), tile(2 * D_SGU), tile(2 * D), tile(D)],
        out_shape=[jax.ShapeDtypeStruct((t, D_ATT), BF16)] * 3 + [jax.ShapeDtypeStruct((t, 2 * D_SGU), BF16),
                                                                   jax.ShapeDtypeStruct((t, 2 * D), BF16),
                                                                   jax.ShapeDtypeStruct((t, D), BF16)],
        scratch_shapes=[pltpu.VMEM((D_IN, D), BF16), pltpu.SemaphoreType.DMA((N_DEV,))],
        args=(x, gain, b_gate, gw), comms=comms)


def _mix_proj_bwd(dz, x, gain, dres, gw, comms=()):
    t = x.shape[0]

    def body(dz_ref, x_ref, g_ref, dr_ref, gw_ref, dx_ref, dg_ref, win, sems):
        @pl.when(pl.program_id(0) == 0)
        def _():
            cps = _load_rows(gw_ref, win, OFF_IN, R_IN, sems)
            dg_ref[...] = jnp.zeros_like(dg_ref)
            for cp in cps:
                cp.wait()

        dh = _nn(dz_ref[...], win[...])
        xn, r = _rms(x_ref[...])
        dxn, dg = _rms_bwd(dh, xn, r, g_ref[...])
        dg_ref[...] += dg
        dx_ref[...] = dr_ref[...] + dxn

    def tile(w):
        return pl.BlockSpec((TM, w), lambda i: (i, 0))

    row = pl.BlockSpec((1, D), lambda i: (0, 0))
    return _call(
        body, name="mix_proj_bwd", grid=(t // TM,),
        in_specs=[tile(D_IN), tile(D), row, tile(D), ANY], out_specs=[tile(D), row],
        out_shape=[jax.ShapeDtypeStruct((t, D), F32), jax.ShapeDtypeStruct((1, D), F32)],
        scratch_shapes=[pltpu.VMEM((D_IN, D), BF16), pltpu.SemaphoreType.DMA((N_DEV,))],
        args=(dz, x, gain, dres, gw), comms=comms)


SKEW_W = KW + QB
N_CAP = 2 * QB - REL_CLIP + 1


def _band_bias(rel_bias):
    cap = rel_bias[:, 2 * REL_CLIP:]
    diag = jnp.concatenate([jnp.broadcast_to(cap, (HEADS, N_CAP)), rel_bias[:, 2 * REL_CLIP - 1::-1],
                            jnp.broadcast_to(cap, (HEADS, SKEW_W - N_CAP - 2 * REL_CLIP))], axis=1)

    def body(d_ref, o_ref):
        lag = lax.broadcasted_iota(jnp.int32, (QB, KW), 1) // CHUNK - lax.broadcasted_iota(jnp.int32, (QB, KW), 0) // CHUNK
        band = (lag >= 0) & (lag <= N_LEFT)
        for h in range(HEADS):
            rows = jnp.broadcast_to(d_ref[h:h + 1, :], (QB, SKEW_W))
            o_ref[h] = jnp.where(band, pltpu.roll(rows, 0, 1, stride=1, stride_axis=0)[:, :KW], NEG_INF)

    return pl.pallas_call(body, name="band_bias", out_shape=jax.ShapeDtypeStruct((HEADS, QB, KW), F32))(diag)


def _att_specs():
    qspec = pl.BlockSpec((QB, D_ATT), lambda g: (g, 0))
    kspecs = [pl.BlockSpec((QB, D_ATT), lambda g: (jnp.maximum(g - 2, 0), 0)),
              pl.BlockSpec((QB, D_ATT), lambda g: (jnp.maximum(g - 1, 0), 0)), qspec]
    bspec = pl.BlockSpec((HEADS, QB, KW), lambda g: (0, 0, 0))
    return qspec, kspecs, bspec


def _att_probs(qm, kp, bias, valid):
    s = jnp.where(valid, _nt(qm, kp) + bias, NEG_INF)
    e = jnp.exp(s - jnp.max(s, axis=-1, keepdims=True))
    return e / jnp.sum(e, axis=-1, keepdims=True)


def _att_valid():
    g = pl.program_id(0)
    blk = lax.broadcasted_iota(jnp.int32, (QB, KW), 1) // QB
    return (blk + g) >= 2


def _att_fwd(q, k, v, bias, comms=()):
    t = q.shape[0]

    def body(q_ref, k0, k1, k2, v0, v1, v2, b_ref, y_ref):
        valid = _att_valid()
        first = lax.broadcasted_iota(jnp.int32, (1, 128), 1) < 64
        for p in range(HEADS // 2):
            lanes = slice(p * 128, (p + 1) * 128)
            qp = q_ref[:, lanes]
            kp = jnp.concatenate([k0[:, lanes], k1[:, lanes], k2[:, lanes]], axis=0)
            vp = jnp.concatenate([v0[:, lanes], v1[:, lanes], v2[:, lanes]], axis=0)
            out = jnp.zeros((QB, 128), F32)
            for hh in range(2):
                mask = first if hh == 0 else jnp.logical_not(first)
                pr = _att_probs(jnp.where(mask, qp, 0), kp, b_ref[2 * p + hh], valid)
                out = out + _nn(pr.astype(BF16), jnp.where(mask, vp, 0))
            y_ref[:, lanes] = out.astype(BF16)

    qspec, kspecs, bspec = _att_specs()
    (out,), got = _call(
        body, name="att_fwd", grid=(t // QB,),
        in_specs=[qspec] + kspecs + kspecs + [bspec], out_specs=[qspec],
        out_shape=[jax.ShapeDtypeStruct((t, D_ATT), BF16)], scratch_shapes=[],
        args=(q, k, k, k, v, v, v, bias), comms=comms)
    return out, got


def _att_bwd(q, k, v, bias, dy, comms=()):
    t = q.shape[0]
    n_blocks = t // QB

    def body(q_ref, k0, k1, k2, v0, v1, v2, b_ref, dy_ref, dq_ref, dk_ref, dv_ref, db_ref, dk_acc, dv_acc):
        g = pl.program_id(0)

        @pl.when(g == 0)
        def _():
            db_ref[...] = jnp.zeros_like(db_ref)
            dk_acc[...] = jnp.zeros_like(dk_acc)
            dv_acc[...] = jnp.zeros_like(dv_acc)

        valid = _att_valid()
        first = lax.broadcasted_iota(jnp.int32, (1, 128), 1) < 64
        for p in range(HEADS // 2):
            lanes = slice(p * 128, (p + 1) * 128)
            qp = q_ref[:, lanes]
            dyp = dy_ref[:, lanes]
            kp = jnp.concatenate([k0[:, lanes], k1[:, lanes], k2[:, lanes]], axis=0)
            vp = jnp.concatenate([v0[:, lanes], v1[:, lanes], v2[:, lanes]], axis=0)
            dq = jnp.zeros((QB, 128), F32)
            dk = jnp.zeros((KW, 128), F32)
            dv = jnp.zeros((KW, 128), F32)
            for hh in range(2):
                mask = first if hh == 0 else jnp.logical_not(first)
                qm = jnp.where(mask, qp, 0)
                dym = jnp.where(mask, dyp, 0)
                pr = _att_probs(qm, kp, b_ref[2 * p + hh], valid)
                dp = _nt(dym, vp)
                ds = pr * (dp - jnp.sum(dp * pr, axis=-1, keepdims=True))
                db_ref[2 * p + hh] += ds
                dsb = ds.astype(BF16)
                dq = dq + _nn(dsb, jnp.where(mask, kp, 0))
                dk = dk + _tn(dsb, qm)
                dv = dv + _tn(pr.astype(BF16), dym)
            dq_ref[:, lanes] = (dq * 0.125).astype(BF16)
            for j in range(3):
                rows = pl.ds(pl.multiple_of(jnp.maximum(g - 2 + j, 0) * QB, QB), QB)
                dk_acc[rows, lanes] += dk[j * QB:(j + 1) * QB]
                dv_acc[rows, lanes] += dv[j * QB:(j + 1) * QB]

        @pl.when(g == n_blocks - 1)
        def _():
            dk_ref[...] = dk_acc[...].astype(BF16)
            dv_ref[...] = dv_acc[...].astype(BF16)

    qspec, kspecs, bspec = _att_specs()
    full = pl.BlockSpec((t, D_ATT), lambda g: (0, 0))
    return _call(
        body, name="att_bwd", grid=(n_blocks,),
        in_specs=[qspec] + kspecs + kspecs + [bspec, qspec], out_specs=[qspec, full, full, bspec],
        out_shape=[jax.ShapeDtypeStruct((t, D_ATT), BF16)] * 3 + [jax.ShapeDtypeStruct((HEADS, QB, KW), F32)],
        scratch_shapes=[pltpu.VMEM((t, D_ATT), F32)] * 2,
        args=(q, k, k, k, v, v, v, bias, dy), comms=comms)


def _rel_bias_grad(dbias):
    def body(db_ref, cs_ref, tot_ref):
        lane = lax.broadcasted_iota(jnp.int32, (1, SKEW_W), 1)
        capped = (lane < N_CAP) | (lane > KW)
        pad = jnp.zeros((8, QB), F32)
        for h in range(HEADS):
            z = jnp.concatenate([db_ref[h, 0:8, :], pad], axis=1)
            for a in range(1, QB // 8):
                z = z + pltpu.roll(jnp.concatenate([db_ref[h, 8 * a:8 * a + 8, :], pad], axis=1), SKEW_W - 8 * a, 1)
            cs = z[0:1, :]
            for b in range(1, 8):
                cs = cs + pltpu.roll(z[b:b + 1, :], SKEW_W - b, 1)
            cs_ref[h:h + 1, :] = cs
            tot_ref[h:h + 1, :] = jnp.broadcast_to(jnp.sum(jnp.where(capped, cs, 0.0), axis=1, keepdims=True), (1, 128))

    cs, tot = pl.pallas_call(
        body, name="rel_bias_grad",
        out_shape=[jax.ShapeDtypeStruct((HEADS, SKEW_W), F32), jax.ShapeDtypeStruct((HEADS, 128), F32)],
    )(dbias)
    return jnp.concatenate([cs[:, KW:N_CAP - 1:-1], tot[:, :1]], axis=1)


def _sgu_mask():
    pos = np.arange(SGU_BLOCK)
    return (pos[:, None] // CHUNK) >= (pos[None, :] // CHUNK)


def _group_stack(blk, first):
    return jnp.concatenate([jnp.where(first, blk, 0), jnp.where(first, 0, blk)], axis=0)


def _sgu_norm(zs_ref, lng, lnb):
    zs = zs_ref[...].astype(F32)
    ga, th = _gelu(zs)
    u, vs = ga[:, :D_SGU], ga[:, D_SGU:]
    mu = jnp.mean(vs, axis=-1, keepdims=True)
    cen = vs - mu
    rstd = lax.rsqrt(jnp.mean(cen * cen, axis=-1, keepdims=True) + EPS)
    xhat = cen * rstd
    return zs, th, u, xhat, rstd, xhat * lng + lnb


def _sgu_mix(vb, wm2_ref, bsx, s_ref):
    first = lax.broadcasted_iota(jnp.int32, (1, 128), 1) < 64
    for n in range(TM // SGU_BLOCK):
        for p in range(4):
            blk = vb[n * 128:(n + 1) * 128, p * 128:(p + 1) * 128]
            s_ref[n * 128:(n + 1) * 128, p * 128:(p + 1) * 128] = _nn(wm2_ref[p], _group_stack(blk, first)) + bsx[:, p * 128:(p + 1) * 128]


def _merge_fwd(x, zs, gt, y_att, lng, lnb, wm2, bsx, gw):
    t = x.shape[0]

    def body(x_ref, zs_ref, gt_ref, ya_ref, lng_ref, lnb_ref, wm2_ref, bsx_ref, gw_ref, xo_ref, ys_ref, mg_ref,
             wbr, wo, s_scr, sems):
        @pl.when(pl.program_id(0) == 0)
        def _():
            for cp in _load_rows(gw_ref, wbr, OFF_BR, R_BR, sems.at[0]) + _load_rows(gw_ref, wo, OFF_WO, R_WO, sems.at[1]):
                cp.wait()

        _, _, u, _, _, vsn = _sgu_norm(zs_ref, lng_ref[...], lnb_ref[...])
        _sgu_mix(vsn.astype(BF16), wm2_ref, bsx_ref[...], s_scr)
        ys = (u * s_scr[...]).astype(BF16)
        ys_ref[...] = ys
        pa = _nt(ya_ref[...], wbr[:, :D_ATT])
        ps = _nt(ys, wbr[:, D_ATT:])
        mg = (gt_ref[:, :D].astype(F32) * pa + gt_ref[:, D:].astype(F32) * ps).astype(BF16)
        mg_ref[...] = mg
        xo_ref[...] = x_ref[...] + _nn(mg, wo[...])

    def tile(w):
        return pl.BlockSpec((TM, w), lambda i: (i, 0))

    def const(shape):
        return pl.BlockSpec(shape, lambda i: (0,) * len(shape))

    return pl.pallas_call(
        body, name="merge_fwd", grid=(t // TM,),
        in_specs=[tile(D), tile(2 * D_SGU), tile(2 * D), tile(D_ATT), const((1, D_SGU)), const((1, D_SGU)),
                  const((4, 128, 256)), const((128, D_SGU)), ANY],
        out_specs=[tile(D), tile(D_SGU), tile(D)],
        out_shape=[jax.ShapeDtypeStruct((t, D), F32), jax.ShapeDtypeStruct((t, D_SGU), BF16), jax.ShapeDtypeStruct((t, D), BF16)],
        scratch_shapes=[pltpu.VMEM((D, D), BF16), pltpu.VMEM((D, D), BF16), pltpu.VMEM((TM, D_SGU), F32),
                        pltpu.SemaphoreType.DMA((2, N_DEV))],
        compiler_params=_cparams(("arbitrary",)),
    )(x, zs, gt, y_att, lng, lnb, wm2, bsx, gw)


def _merge_bwd(dx, gt, y_att, y_sgu, gw):
    t = dx.shape[0]

    def body(dx_ref, gt_ref, ya_ref, ys_ref, gw_ref, dzg_ref, dya_ref, dys_ref, dpp_ref, dxb_ref, dbg_ref, wbr, wo, sems):
        @pl.when(pl.program_id(0) == 0)
        def _():
            cps = _load_rows(gw_ref, wbr, OFF_BR, R_BR, sems.at[0]) + _load_rows(gw_ref, wo, OFF_WO, R_WO, sems.at[1])
            dbg_ref[...] = jnp.zeros_like(dbg_ref)
            for cp in cps:
                cp.wait()

        dxb = dx_ref[...].astype(BF16)
        dxb_ref[...] = dxb
        dm = _nt(dxb, wo[...])
        for half, y_ref, w in ((0, ya_ref, wbr.at[:, :D_ATT]), (1, ys_ref, wbr.at[:, D_ATT:])):
            cols = slice(half * D, (half + 1) * D)
            gate = gt_ref[:, cols].astype(F32)
            branch = _nt(y_ref[...], w[...])
            dzg = dm * branch * gate * (1.0 - gate)
            dbg_ref[:, cols] += jnp.sum(dzg, axis=0, keepdims=True)
            dzg_ref[:, cols] = dzg.astype(BF16)
            dbr = (dm * gate).astype(BF16)
            dpp_ref[:, cols] = dbr
            dy = _nn(dbr, w[...])
            if half == 0:
                dya_ref[...] = dy.astype(BF16)
            else:
                dys_ref[...] = dy

    def tile(w):
        return pl.BlockSpec((TM, w), lambda i: (i, 0))

    return pl.pallas_call(
        body, name="merge_bwd", grid=(t // TM,),
        in_specs=[tile(D), tile(2 * D), tile(D_ATT), tile(D_SGU), ANY],
        out_specs=[tile(2 * D), tile(D_ATT), tile(D_SGU), tile(2 * D), tile(D), pl.BlockSpec((1, 2 * D), lambda i: (0, 0))],
        out_shape=[jax.ShapeDtypeStruct((t, 2 * D), BF16), jax.ShapeDtypeStruct((t, D_ATT), BF16), jax.ShapeDtypeStruct((t, D_SGU), F32),
                   jax.ShapeDtypeStruct((t, 2 * D), BF16), jax.ShapeDtypeStruct((t, D), BF16), jax.ShapeDtypeStruct((1, 2 * D), F32)],
        scratch_shapes=[pltpu.VMEM((D, D), BF16), pltpu.VMEM((D, D), BF16), pltpu.SemaphoreType.DMA((2, N_DEV))],
        compiler_params=_cparams(("arbitrary",)),
    )(dx, gt, y_att, y_sgu, gw)


def _sgu_bwd(zs, dys, lng, lnb, wm2, wmt2, bsx, comms=()):
    t = zs.shape[0]
    n_steps = t // TM

    def body(zs_ref, dys_ref, lng_ref, lnb_ref, wm2_ref, wmt2_ref, bsx_ref, dzs_ref, dw_ref, dbs_ref, dlg_ref, dlb_ref,
             s_scr, dv_scr, ds_acc):
        i = pl.program_id(0)

        @pl.when(i == 0)
        def _():
            dw_ref[...] = jnp.zeros_like(dw_ref)
            dlg_ref[...] = jnp.zeros_like(dlg_ref)
            dlb_ref[...] = jnp.zeros_like(dlb_ref)
            ds_acc[...] = jnp.zeros_like(ds_acc)

        lng = lng_ref[...]
        zs, th, u, xhat, rstd, vsn = _sgu_norm(zs_ref, lng, lnb_ref[...])
        vb = vsn.astype(BF16)
        _sgu_mix(vb, wm2_ref, bsx_ref[...], s_scr)
        dys = dys_ref[...]
        du = dys * s_scr[...]
        ds = dys * u
        dsb = ds.astype(BF16)
        first = lax.broadcasted_iota(jnp.int32, (1, 128), 1) < 64
        acc = jnp.zeros((SGU_BLOCK, D_SGU), F32)
        for n in range(TM // SGU_BLOCK):
            rows = slice(n * 128, (n + 1) * 128)
            acc = acc + ds[rows]
            for p in range(4):
                lanes = slice(p * 128, (p + 1) * 128)
                stack = _group_stack(dsb[rows, lanes], first)
                dv_scr[rows, lanes] = _nn(wmt2_ref[p], stack)
                dw_ref[p] += _nt(stack, vb[rows, lanes])
        ds_acc[...] += acc
        dvsn = dv_scr[...]
        dlg_ref[...] += jnp.sum(dvsn * xhat, axis=0, keepdims=True)
        dlb_ref[...] += jnp.sum(dvsn, axis=0, keepdims=True)
        dxh = dvsn * lng
        dvs = rstd * (dxh - jnp.mean(dxh, axis=-1, keepdims=True) - xhat * jnp.mean(dxh * xhat, axis=-1, keepdims=True))
        dga = jnp.concatenate([du, dvs], axis=1)
        dzs_ref[...] = (dga * _gelu_grad(zs, th)).astype(BF16)

        @pl.when(i == n_steps - 1)
        def _():
            r = lax.broadcasted_iota(jnp.int32, (256, 128), 0) % SGU_BLOCK
            c = lax.broadcasted_iota(jnp.int32, (256, 128), 1)
            keep = (r // CHUNK) >= (c // CHUNK)
            for p in range(4):
                dw_ref[p] = jnp.where(keep, dw_ref[p], 0.0)
            total = ds_acc[...]
            grp = lax.broadcasted_iota(jnp.int32, (SGU_BLOCK, D_SGU), 1) // 64
            lane = lax.broadcasted_iota(jnp.int32, (SGU_BLOCK, 128), 1)
            out = jnp.zeros((SGU_BLOCK, 128), F32)
            for gi in range(8):
                out = jnp.where(lane == gi, jnp.sum(jnp.where(grp == gi, total, 0.0), axis=1, keepdims=True), out)
            dbs_ref[...] = out

    def tile(w):
        return pl.BlockSpec((TM, w), lambda i: (i, 0))

    def const(shape):
        return pl.BlockSpec(shape, lambda i: (0,) * len(shape))

    return _call(
        body, name="sgu_bwd", grid=(n_steps,),
        in_specs=[tile(2 * D_SGU), tile(D_SGU), const((1, D_SGU)), const((1, D_SGU)), const((4, 128, 256)), const((4, 128, 256)),
                  const((128, D_SGU))],
        out_specs=[tile(2 * D_SGU), const((4, 256, 128)), const((128, 128)), const((1, D_SGU)), const((1, D_SGU))],
        out_shape=[jax.ShapeDtypeStruct((t, 2 * D_SGU), BF16), jax.ShapeDtypeStruct((4, 256, 128), F32),
                   jax.ShapeDtypeStruct((128, 128), F32), jax.ShapeDtypeStruct((1, D_SGU), F32), jax.ShapeDtypeStruct((1, D_SGU), F32)],
        scratch_shapes=[pltpu.VMEM((TM, D_SGU), F32), pltpu.VMEM((TM, D_SGU), F32), pltpu.VMEM((SGU_BLOCK, D_SGU), F32)],
        args=(zs, dys, lng, lnb, wm2, wmt2, bsx), comms=comms)


def _adamw(g, w, m, v):
    m = ADAM_B1 * m + (1.0 - ADAM_B1) * g
    v = ADAM_B2 * v + (1.0 - ADAM_B2) * (g * g)
    m_hat = m / (1.0 - ADAM_B1 ** ADAM_STEP)
    v_hat = v / (1.0 - ADAM_B2 ** ADAM_STEP)
    return -ADAM_LR * (m_hat / (jnp.sqrt(v_hat) + ADAM_EPS) + ADAM_WD * w), m, v


def _adamw_matrix(parts, sums, w, m, v, transposed, name):
    _, r, c = parts.shape
    tc = 256

    def body(own_ref, p_ref, s_ref, w_ref, m_ref, v_ref, g_ref, d_ref, mo_ref, vo_ref):
        g = p_ref[0].astype(F32) + p_ref[1].astype(F32) + p_ref[2].astype(F32) + s_ref[...].astype(F32)
        g = g.T if transposed else g
        g_ref[...] = g
        d_ref[...], mo_ref[...], vo_ref[...] = _adamw(g, w_ref[...], m_ref[...], v_ref[...])

    own = pl.BlockSpec((None, tc, r), lambda i, o: (0, i, 0)) if transposed else pl.BlockSpec((None, r, tc), lambda i, o: (0, 0, i))
    return pl.pallas_call(
        body, name=name,
        grid_spec=pltpu.PrefetchScalarGridSpec(
            num_scalar_prefetch=1, grid=(c // tc,),
            in_specs=[pl.BlockSpec((3, r, tc), lambda i, o: (0, 0, i)), pl.BlockSpec((None, r, tc), lambda i, o: (o[0], 0, i)), own, own, own],
            out_specs=[own] * 4),
        out_shape=[jax.ShapeDtypeStruct(w.shape, F32)] * 4,
        compiler_params=_cparams(("arbitrary",)),
    )(_my_index(("x", 2), ("y", 1)), parts, sums, w, m, v)


_SMALL_2D = {"norm_ffn1": (1, D), "norm_mix": (1, D), "norm_ffn2": (1, D), "norm_final": (1, D), "b_gate": (1, 2 * D),
             "sgu_ln_g": (1, D_SGU), "sgu_ln_b": (1, D_SGU), "sgu_b_s": (8, SGU_BLOCK), "rel_bias": (HEADS, N_REL),
             "sgu_w_s": (8 * SGU_BLOCK, SGU_BLOCK)}


def _adamw_small(parts, loss_parts, p):
    names = list(parts)
    n = len(names)

    def body(*refs):
        got, loss_got, wmv, outs, loss_out = refs[:n], refs[n], refs[n + 1:4 * n + 1], refs[4 * n + 1:8 * n + 1], refs[8 * n + 1]
        for i, name in enumerate(names):
            g = got[i][0]
            for k in range(1, N_DEV):
                g = g + got[i][k]
            if name == "sgu_b_s":
                g = g.T[0:8, :]
            res = (g,) + _adamw(g, wmv[3 * i][...], wmv[3 * i + 1][...], wmv[3 * i + 2][...])
            for o_ref, val in zip(outs[4 * i:4 * i + 4], res):
                o_ref[...] = val
        total = loss_got[0]
        for k in range(1, N_DEV):
            total = total + loss_got[k]
        loss_out[...] = total

    wmv = [p[pre + name].reshape(_SMALL_2D[name]) for name in names for pre in ("", "m_", "v_")]
    res = pl.pallas_call(
        body, name="adamw_small",
        out_shape=[jax.ShapeDtypeStruct(_SMALL_2D[name], F32) for name in names for _ in range(4)] + [jax.ShapeDtypeStruct((1, 128), F32)],
        compiler_params=_cparams())(*[parts[name] for name in names], loss_parts, *wmv)
    return [{name: res[4 * i + j].reshape(p[name].shape) for i, name in enumerate(names)} for j in range(4)], res[-1]


def _pack_rows(groups, name):
    flat = [a for grp in groups for a, _ in grp]
    rows = [grp[0][0].shape[2] if grp[0][1] else grp[0][0].shape[1] for grp in groups]

    def body(*refs):
        o_ref, pos, off = refs[-1], 0, 0
        for grp, r in zip(groups, rows):
            vals = []
            for _, transposed in grp:
                val = refs[pos][0]
                vals.append(val.T if transposed else val)
                pos += 1
            o_ref[off:off + r, :] = (vals[0] if len(vals) == 1 else jnp.concatenate(vals, axis=1)).astype(BF16)
            off += r

    return pl.pallas_call(body, name=name, out_shape=jax.ShapeDtypeStruct((sum(rows), D), BF16), compiler_params=_cparams())(*flat)


def _step(x, target, p):
    n1, nm, n2 = p["norm_ffn1"], p["norm_mix"], p["norm_ffn2"]
    nf = p["norm_final"].reshape(1, D)
    lng, lnb = p["sgu_ln_g"], p["sgu_ln_b"]
    w_m = jnp.where(jnp.asarray(_sgu_mask())[None], p["sgu_w_s"][0], 0.0).astype(BF16)
    wm2 = jnp.concatenate([w_m[0::2], w_m[1::2]], axis=2)
    w_mt = w_m.transpose(0, 2, 1)
    wmt2 = jnp.concatenate([w_mt[0::2], w_mt[1::2]], axis=2)
    bsx = jnp.repeat(p["sgu_b_s"][0].T, 64, axis=1)
    bias = _band_bias(p["rel_bias"][0])

    def chip_sums(grads, name):
        gots = _comm_only([_SiblingSwap(grads)], "swap_" + name)
        return _pair_sums(grads, gots, "pair_sums_" + name)

    def as_rows(a):
        return jnp.swapaxes(a, 1, 2)

    def updates(parts, sums, names):
        res = {}
        for pt, sm, n in zip(parts, sums, names):
            if p[n].shape[1:] == pt.shape[1:]:
                res[n] = _adamw_matrix(pt, sm, p[n], p["m_" + n], p["v_" + n], False, "adamw_" + n)
            elif p[n].shape[2] > 128:
                res[n] = [as_rows(o) for o in _adamw_matrix(pt, sm, as_rows(p[n]), as_rows(p["m_" + n]), as_rows(p["v_" + n]), False,
                                                            "adamw_" + n)]
            else:
                res[n] = _adamw_matrix(pt, sm, p[n], p["m_" + n], p["v_" + n], True, "adamw_" + n)
        return res

    rows1 = _pack_rows([[(as_rows(p["ffn1_w_gate"]), False)], [(as_rows(p["ffn1_w_up"]), False)], [(p["ffn1_w_down"], False)]], "pack_ffn1")
    rows_m = _pack_rows([[(as_rows(p["w_in"]), False)], [(p["w_branch_att"], True), (p["w_branch_sgu"], True)], [(p["w_out"], False)]],
                        "pack_mixer")
    rows2d = _pack_rows([[(p["ffn2_w_down"], False)]], "pack_ffn2_down")
    rows2gu = _pack_rows([[(as_rows(p["ffn2_w_gate"]), False)], [(as_rows(p["ffn2_w_up"]), False)]], "pack_ffn2_gate_up")
    (gw1,) = _comm_only([_Gather(rows1)], "gather_ffn1")
    x1, ab1, h1, (gwm,) = _ffn_fwd(x, n1, gw1, gw1, 2 * R_FF, "ffn1_fwd", [_Gather(rows_m)])
    (q, k, v, zs, gt, h2), (gw2d,) = _mix_proj_fwd(x1, nm, p["b_gate"], gwm, [_Gather(rows2d)])
    y_att, (gw2gu,) = _att_fwd(q, k, v, bias, [_Gather(rows2gu)])
    x2, y_sgu, merged = _merge_fwd(x1, zs, gt, y_att, lng, lnb, wm2, bsx, gwm)
    dx3, ab2, hb, _, d_nf, loss = _ffn_fwd(x2, n2, gw2gu, gw2d, 0, "ffn2_fwd", head=(nf, target))

    dab, sb, dfb = _ffn_bwd_hidden(ab2, dx3, gw2d, 0, "ffn2_bwd_hidden")
    (dx2, d_n2), _ = _ffn_bwd_input(x2, n2, dab, dx3, gw2gu, "ffn2_bwd")
    g2 = [_weight_grad(dab, hb, "ffn2_dw_gate", 0, F), _weight_grad(dab, hb, "ffn2_dw_up", F, F), _weight_grad(sb, dfb, "ffn2_dw_down")]
    dzg, dya, dys, dpp, dxb, d_bg = _merge_bwd(dx2, gt, y_att, y_sgu, gwm)
    g_late = g2 + [_weight_grad(dpp, y_att, "dw_branch_att", 0, D), _weight_grad(dpp, y_sgu, "dw_branch_sgu", D, D),
                   _weight_grad(merged, dxb, "dw_out")]
    late = ("ffn2_w_gate", "ffn2_w_up", "ffn2_w_down", "w_branch_att", "w_branch_sgu", "w_out")
    (dzs, d_wm, d_bs, d_lng, d_lnb), gots_late = _sgu_bwd(zs, dys, lng, lnb, wm2, wmt2, bsx, [_SiblingSwap(g_late)])
    sums_late = _pair_sums(g_late, gots_late, "pair_sums_late")
    (dq, dk, dv, d_bias), parts_late = _att_bwd(q, k, v, bias, dya, [_ChipScatter(sums_late)])
    big = updates(parts_late, sums_late, late)
    d_rel = _rel_bias_grad(d_bias)
    dz = jnp.concatenate([dq, dk, dv, dzs, dzg], axis=1)
    small = {"norm_ffn2": d_n2, "norm_final": d_nf, "b_gate": d_bg, "sgu_ln_g": d_lng, "sgu_ln_b": d_lnb, "sgu_b_s": d_bs,
             "rel_bias": d_rel, "sgu_w_s": d_wm.reshape(_SMALL_2D["sgu_w_s"])}
    g_in, (*early_parts, loss_parts) = _weight_grad(dz, h2, "dw_in", comms=[_AllToAll(list(small.values()) + [loss])])
    sums_in = chip_sums([g_in], "w_in")
    (dx1, d_nm), parts_in = _mix_proj_bwd(dz, x1, nm, dx2, gwm, [_ChipScatter(sums_in)])
    big.update(updates(parts_in, sums_in, ("w_in",)))

    dab, sb, dfb = _ffn_bwd_hidden(ab1, dx1, gw1, 2 * R_FF, "ffn1_bwd_hidden")
    sums_d = chip_sums([_weight_grad(sb, dfb, "ffn1_dw_down")], "ffn1_down")
    g_gate, parts_d = _weight_grad(dab, h1, "ffn1_dw_gate", 0, F, [_ChipScatter(sums_d)])
    sums_g = chip_sums([g_gate], "ffn1_gate")
    g_up, (parts_g, nm_parts) = _weight_grad(dab, h1, "ffn1_dw_up", F, F, [_ChipScatter(sums_g), _AllToAll([d_nm])])
    sums_u = chip_sums([g_up], "ffn1_up")
    (dx0, d_n1), parts_u = _ffn_bwd_input(x, n1, dab, dx1, gw1, "ffn1_bwd", [_ChipScatter(sums_u)])
    (n1_parts,) = _comm_only([_AllToAll([d_n1])], "gather_norm_ffn1")
    big.update(updates(parts_d + [parts_g] + parts_u, sums_d + sums_g + sums_u, ("ffn1_w_down", "ffn1_w_gate", "ffn1_w_up")))
    out_s, loss_sum = _adamw_small(dict(zip(small, early_parts), norm_mix=nm_parts, norm_ffn1=n1_parts), loss_parts, p)
    return dx0, loss_sum[0, 0], [{**{n: four[i] for n, four in big.items()}, **s} for i, s in enumerate(out_s)]


_OUT_ORDER = ("norm_ffn1", "ffn1_w_gate", "ffn1_w_up", "ffn1_w_down", "norm_mix", "w_in", "b_gate", "rel_bias", "sgu_ln_g", "sgu_ln_b",
              "sgu_w_s", "sgu_b_s", "w_branch_att", "w_branch_sgu", "w_out", "norm_ffn2", "ffn2_w_gate", "ffn2_w_up", "ffn2_w_down",
              "norm_final")


def kernel(x, norm_ffn1, ffn1_w_gate, ffn1_w_up, ffn1_w_down, norm_mix, w_in, b_gate, rel_bias, sgu_ln_g, sgu_ln_b, sgu_w_s, sgu_b_s, w_branch_att, w_branch_sgu, w_out, norm_ffn2, ffn2_w_gate, ffn2_w_up, ffn2_w_down, norm_final, loss_target, m_norm_ffn1, m_ffn1_w_gate, m_ffn1_w_up, m_ffn1_w_down, m_norm_mix, m_w_in, m_b_gate, m_rel_bias, m_sgu_ln_g, m_sgu_ln_b, m_sgu_w_s, m_sgu_b_s, m_w_branch_att, m_w_branch_sgu, m_w_out, m_norm_ffn2, m_ffn2_w_gate, m_ffn2_w_up, m_ffn2_w_down, m_norm_final, v_norm_ffn1, v_ffn1_w_gate, v_ffn1_w_up, v_ffn1_w_down, v_norm_mix, v_w_in, v_b_gate, v_rel_bias, v_sgu_ln_g, v_sgu_ln_b, v_sgu_w_s, v_sgu_b_s, v_w_branch_att, v_w_branch_sgu, v_w_out, v_norm_ffn2, v_ffn2_w_gate, v_ffn2_w_up, v_ffn2_w_down, v_norm_final):
    args = dict(locals())
    dx, loss, outs = _step(x[0], loss_target[0], {pre + n: args[pre + n] for pre in ("", "m_", "v_") for n in _OUT_ORDER})
    return (loss, dx[None], *[o[n] for o in outs for n in _OUT_ORDER])
```

```python
import functools

import numpy as np
import jax
import jax.numpy as jnp
from jax import lax
from jax.experimental import pallas as pl
from jax.experimental.pallas import tpu as pltpu

F32 = jnp.float32
BF16 = jnp.bfloat16

N_DEV = 8
D = 1024
F = 2816
D_ATT = 512
D_SGU = 512
D_IN = 4608
HEADS = 8
CHUNK = 64
N_LEFT = 8
REL_CLIP = 256
N_REL = 2 * REL_CLIP + 1
SGU_BLOCK = 128
EPS = 1e-6
NEG_INF = -1e30
QB = 256
KW = 3 * QB

R_FF, R_IN, R_BR, R_WO = F // N_DEV, D_IN // N_DEV, D // N_DEV, D // N_DEV
OFF_IN, OFF_BR, OFF_WO = 0, R_IN, R_IN + R_BR
FFN_ROWS = 3 * R_FF
MIX_ROWS = R_IN + R_BR + R_WO

FC = 256
TM = 512
VMEM_LIMIT = 56 * 1024 * 1024

ADAM_LR, ADAM_B1, ADAM_B2, ADAM_EPS, ADAM_WD, ADAM_STEP = 0.001, 0.9, 0.999, 1e-08, 0.01, 10

MESH = pl.DeviceIdType.MESH
ANY = pl.BlockSpec(memory_space=pl.ANY)


def _nt(a, b):
    return lax.dot_general(a, b, (((1,), (1,)), ((), ())), preferred_element_type=F32)


def _tn(a, b):
    return lax.dot_general(a, b, (((0,), (0,)), ((), ())), preferred_element_type=F32)


def _nn(a, b):
    return jnp.dot(a, b, preferred_element_type=F32)


def _cparams(sem=None):
    return pltpu.CompilerParams(dimension_semantics=sem, vmem_limit_bytes=VMEM_LIMIT)


def _load_rows(gw_ref, dst, off, rows, sems):
    copies = [pltpu.make_async_copy(gw_ref.at[k, pl.ds(off, rows), :], dst.at[pl.ds(k * rows, rows), :], sems.at[k])
              for k in range(N_DEV)]
    for cp in copies:
        cp.start()
    return copies


def _rms(xv):
    r = lax.rsqrt(jnp.mean(xv * xv, axis=-1, keepdims=True) + EPS)
    return xv * r, r


def _rms_bwd(dh, xn, r, gain):
    dxn = dh * gain
    dx = r * (dxn - xn * jnp.mean(dxn * xn, axis=-1, keepdims=True))
    return dx, jnp.sum(dh * xn, axis=0, keepdims=True)


def _gelu(x):
    t = jnp.tanh(0.7978845608028654 * (x + 0.044715 * x * x * x))
    return 0.5 * x * (1.0 + t), t


def _gelu_grad(x, t):
    return 0.5 * (1.0 + t) + 0.5 * x * (1.0 - t * t) * 0.7978845608028654 * (1.0 + 3.0 * 0.044715 * x * x)


def _place():
    x, y, cc = lax.axis_index("x"), lax.axis_index("y"), lax.axis_index("c")
    return x, y, cc, [(1 - x, y), (x, 1 - y), (1 - x, 1 - y)]


class _Gather:
    def __init__(self, shard):
        self.inputs = [shard]
        self.out_shape = [jax.ShapeDtypeStruct((N_DEV,) + shard.shape, shard.dtype)]
        self.scratch = [pltpu.SemaphoreType.DMA((7,)), pltpu.SemaphoreType.DMA((7,)), pltpu.SemaphoreType.DMA]

    def _copies(self, ins, outs, scr):
        (x_ref,), (out_ref,), (send_sems, recv_sems, local_sem) = ins, outs, scr
        x, y, cc, chips = _place()

        def slab(px, py, pc):
            return out_ref.at[4 * px + 2 * py + pc]

        def copy(k, block, to, src=None):
            return pltpu.make_async_remote_copy(
                src_ref=slab(*block) if src is None else src, dst_ref=slab(*block),
                send_sem=send_sems.at[k], recv_sem=recv_sems.at[k], device_id=to, device_id_type=MESH)

        me, sibling = (x, y, cc), (x, y, 1 - cc)
        x_nbr, y_nbr, diagonal = chips
        mine = pltpu.make_async_copy(x_ref, slab(*me), local_sem)
        first = [copy(0, me, sibling, src=x_ref), copy(1, me, (*x_nbr, cc), src=x_ref), copy(2, me, (*y_nbr, cc), src=x_ref)]
        neighbours = [copy(1, (*x_nbr, cc), me), copy(2, (*y_nbr, cc), me)]
        second_hand = copy(3, (x ^ (1 - cc), y ^ cc, cc), (x ^ cc, y ^ (1 - cc), cc))
        from_diagonal = copy(3, (*diagonal, cc), me)
        passed = [copy(4 + j, (*chip, cc), sibling) for j, chip in enumerate(chips)]
        from_sibling = [copy(0, sibling, me)] + [copy(4 + j, (*chip, 1 - cc), me) for j, chip in enumerate(chips)]
        return mine, first, neighbours, second_hand, from_diagonal, passed, from_sibling

    def begin(self, *refs):
        mine, first = self._copies(*refs)[:2]
        mine.start()
        for cp in first:
            cp.start()

    def mid(self, *refs):
        _, _, neighbours, second_hand, _, passed, _ = self._copies(*refs)
        for cp in neighbours:
            cp.wait_recv()
        second_hand.start()
        passed[0].start()
        passed[1].start()

    def relay(self, *refs):
        _, _, _, _, from_diagonal, passed, _ = self._copies(*refs)
        from_diagonal.wait_recv()
        passed[2].start()

    def end(self, *refs):
        mine, first, _, second_hand, _, passed, from_sibling = self._copies(*refs)
        for cp in from_sibling:
            cp.wait_recv()
        for cp in first + [second_hand] + passed:
            cp.wait_send()
        mine.wait()


class _Direct:
    def begin(self, *refs):
        keep, give = self._copies(*refs)
        for cp in keep + give:
            cp.start()

    def mid(self, *refs):
        pass

    def relay(self, *refs):
        pass

    def end(self, *refs):
        keep, give = self._copies(*refs)
        for cp in give:
            cp.wait_recv()
        for cp in give:
            cp.wait_send()
        for cp in keep:
            cp.wait()


class _SiblingSwap(_Direct):
    def __init__(self, grads):
        n = len(grads)
        self.which = [w for _, w in grads]
        self.inputs = [g for g, _ in grads]
        self.out_shape = [jax.ShapeDtypeStruct((4,) + g.shape[1:], g.dtype) for g, _ in grads]
        self.scratch = [pltpu.SemaphoreType.DMA((n, 4)), pltpu.SemaphoreType.DMA((n, 4))]

    def _copies(self, ins, outs, scr):
        send_sems, recv_sems = scr
        x, y, cc, _ = _place()
        return [], [pltpu.make_async_remote_copy(src_ref=g_ref.at[N_DEV * w + 2 * j + 1 - cc], dst_ref=got_ref.at[j],
                                                 send_sem=send_sems.at[i, j], recv_sem=recv_sems.at[i, j], device_id=(x, y, 1 - cc),
                                                 device_id_type=MESH)
                    for i, (g_ref, got_ref, w) in enumerate(zip(ins, outs, self.which)) for j in range(4)]


class _ChipScatter(_Direct):
    def __init__(self, sums):
        n = len(sums)
        self.inputs = list(sums)
        self.out_shape = [jax.ShapeDtypeStruct((3,) + s.shape[1:], s.dtype) for s in sums]
        self.scratch = [pltpu.SemaphoreType.DMA((n, 3)), pltpu.SemaphoreType.DMA((n, 3))]

    def _copies(self, ins, outs, scr):
        send_sems, recv_sems = scr
        _, _, cc, chips = _place()
        return [], [pltpu.make_async_remote_copy(src_ref=s_ref.at[2 * px + py], dst_ref=got_ref.at[j], send_sem=send_sems.at[i, j],
                                                 recv_sem=recv_sems.at[i, j], device_id=(px, py, cc), device_id_type=MESH)
                    for i, (s_ref, got_ref) in enumerate(zip(ins, outs)) for j, (px, py) in enumerate(chips)]


class _AllToAll(_Direct):
    def __init__(self, blocks):
        n = len(blocks)
        self.inputs = list(blocks)
        self.out_shape = [jax.ShapeDtypeStruct((N_DEV,) + b.shape, b.dtype) for b in blocks]
        self.scratch = [pltpu.SemaphoreType.DMA((n, 7)), pltpu.SemaphoreType.DMA((n, 7)), pltpu.SemaphoreType.DMA((n,))]

    def _copies(self, ins, outs, scr):
        send_sems, recv_sems, local_sems = scr
        x, y, cc, _ = _place()
        me = 4 * x + 2 * y + cc
        keep = [pltpu.make_async_copy(b_ref, got_ref.at[me], local_sems.at[i]) for i, (b_ref, got_ref) in enumerate(zip(ins, outs))]
        give = [pltpu.make_async_remote_copy(src_ref=b_ref, dst_ref=got_ref.at[me], send_sem=send_sems.at[i, k - 1],
                                             recv_sem=recv_sems.at[i, k - 1],
                                             device_id=(x ^ ((k >> 2) & 1), y ^ ((k >> 1) & 1), cc ^ (k & 1)), device_id_type=MESH)
                for i, (b_ref, got_ref) in enumerate(zip(ins, outs)) for k in range(1, N_DEV)]
        return keep, give


def _split_refs(refs, counts):
    out, pos = [], 0
    for n in counts:
        out.append(list(refs[pos:pos + n]))
        pos += n
    return out


def _bind(comms, c_in, c_out, c_scr):
    ins = _split_refs(c_in, [len(c.inputs) for c in comms])
    outs = _split_refs(c_out, [len(c.out_shape) for c in comms])
    scr = _split_refs(c_scr, [len(c.scratch) for c in comms])
    return [(c, (i, o, s)) for c, i, o, s in zip(comms, ins, outs, scr)]


def _call(body, *, name, grid, in_specs, out_specs, out_shape, scratch_shapes, args, comms=()):
    c_in = [a for c in comms for a in c.inputs]
    c_out = [s for c in comms for s in c.out_shape]
    c_scr = [s for c in comms for s in c.scratch]
    counts = [len(in_specs), len(c_in), len(out_shape), len(c_out), len(scratch_shapes), len(c_scr)]

    def full(*refs):
        ins, cin, outs, cout, scr, cscr = _split_refs(refs, counts)
        bound = _bind(comms, cin, cout, cscr)
        if comms:
            def at(steps):
                return functools.reduce(jnp.logical_and, [pl.program_id(ax) == s for ax, s in enumerate(steps)])

            first, last = at([0] * len(grid)), at([n - 1 for n in grid])

            @pl.when(first)
            def _():
                for c, r in bound:
                    c.begin(*r)

            @pl.when(at([(grid[0] - 1) // 2] + [0] * (len(grid) - 1)))
            def _():
                for c, r in bound:
                    c.mid(*r)

            @pl.when(last)
            def _():
                for c, r in bound:
                    c.relay(*r)

        body(*ins, *outs, *scr)
        if comms:
            @pl.when(last)
            def _():
                for c, r in bound:
                    c.end(*r)

    res = pl.pallas_call(
        full, name=name, grid=grid,
        in_specs=list(in_specs) + [ANY] * len(c_in), out_specs=list(out_specs) + [ANY] * len(c_out),
        out_shape=list(out_shape) + c_out, scratch_shapes=list(scratch_shapes) + c_scr,
        compiler_params=_cparams(("arbitrary",) * len(grid)),
    )(*args, *c_in)
    return list(res[:len(out_shape)]), list(res[len(out_shape):])


def _comm_only(comms, name):
    c_in = [a for c in comms for a in c.inputs]
    c_out = [s for c in comms for s in c.out_shape]
    c_scr = [s for c in comms for s in c.scratch]

    def full(*refs):
        cin, cout, cscr = _split_refs(refs, [len(c_in), len(c_out), len(c_scr)])
        bound = _bind(comms, cin, cout, cscr)
        for phase in ("begin", "mid", "relay", "end"):
            for c, r in bound:
                getattr(c, phase)(*r)

    return list(pl.pallas_call(full, name=name, in_specs=[ANY] * len(c_in), out_specs=[ANY] * len(c_out), out_shape=c_out,
                               scratch_shapes=c_scr)(*c_in))


def _my_index(*axes_and_weights):
    return sum(w * lax.axis_index(a) for a, w in axes_and_weights).astype(jnp.int32).reshape(1)


def _pair_sums(grads, gots, name):
    n = len(grads)

    def body(c_ref, *refs):
        for a_ref, b_ref, o_ref in zip(refs[:n], refs[n:2 * n], refs[2 * n:]):
            o_ref[...] = (a_ref[...].astype(F32) + b_ref[...].astype(F32)).astype(BF16)

    def tile(g):
        return pl.BlockSpec((1,) + g.shape[1:], lambda j, c_ref: (j, 0, 0))

    def mine(g, w):
        return pl.BlockSpec((1, None) + g.shape[1:], lambda j, c_ref: (4 * w + j, c_ref[0], 0, 0))

    return list(pl.pallas_call(
        body, name=name,
        grid_spec=pltpu.PrefetchScalarGridSpec(num_scalar_prefetch=1, grid=(4,),
                                               in_specs=[mine(g, w) for g, w in grads] + [tile(g) for g in gots],
                                               out_specs=[tile(g) for g in gots]),
        out_shape=[jax.ShapeDtypeStruct(g.shape, BF16) for g in gots],
        compiler_params=_cparams(("arbitrary",)))(_my_index(("c", 1)), *[g.reshape((-1, 2) + g.shape[1:]) for g, _ in grads], *gots))


def _ffn_fwd(x, gain, gw_gu, gw_d, off_d, name, comms=(), head=None):
    t = x.shape[0]
    n_head = 0 if head is None else 2

    def body(x_ref, g_ref, gu_ref, d_ref, *refs):
        head_in, (o_ref, ab_ref, h_ref), head_out = refs[:n_head], refs[n_head:n_head + 3], refs[n_head + 3:2 * n_head + 3]
        wg, wu, wd, s_scr, sems = refs[2 * n_head + 3:]

        @pl.when(pl.program_id(0) == 0)
        def _():
            cps = _load_rows(gu_ref, wg, 0, R_FF, sems.at[0]) + _load_rows(gu_ref, wu, R_FF, R_FF, sems.at[1]) \
                + _load_rows(d_ref, wd, off_d, R_FF, sems.at[2])
            for o in head_out:
                o[...] = jnp.zeros_like(o)
            for cp in cps:
                cp.wait()

        xv = x_ref[...]
        xn, _ = _rms(xv)
        h = (xn * g_ref[...]).astype(BF16)
        h_ref[...] = h
        for c in range(F // FC):
            rows = pl.ds(c * FC, FC)
            a = _nt(h, wg[rows, :])
            b = _nt(h, wu[rows, :])
            ab_ref[:, c * FC:(c + 1) * FC] = a.astype(BF16)
            ab_ref[:, F + c * FC:F + (c + 1) * FC] = b.astype(BF16)
            s_scr[:, c * FC:(c + 1) * FC] = (a * jax.nn.sigmoid(a) * b).astype(BF16)
        out = xv + 0.5 * _nn(s_scr[...], wd[...])
        if head is None:
            o_ref[...] = out
        else:
            (gf_ref, t_ref), (dg_ref, loss_ref) = head_in, head_out
            gain_f = gf_ref[...]
            yn, r = _rms(out)
            err = yn * gain_f - t_ref[...]
            loss_ref[...] += 0.5 * jnp.sum(jnp.mean(err * err, axis=-1, keepdims=True), axis=0, keepdims=True)
            o_ref[...], dg = _rms_bwd(err * (1.0 / D), yn, r, gain_f)
            dg_ref[...] += dg

    tile = pl.BlockSpec((TM, D), lambda i: (i, 0))
    row = pl.BlockSpec((1, D), lambda i: (0, 0))
    head_specs = [] if head is None else [row, pl.BlockSpec((1, 128), lambda i: (0, 0))]
    head_shapes = [] if head is None else [jax.ShapeDtypeStruct((1, D), F32), jax.ShapeDtypeStruct((1, 128), F32)]
    res, got = _call(
        body, name=name, grid=(t // TM,),
        in_specs=[tile, row, ANY, ANY] + ([] if head is None else [row, tile]),
        out_specs=[tile, pl.BlockSpec((TM, 2 * F), lambda i: (i, 0)), tile] + head_specs,
        out_shape=[jax.ShapeDtypeStruct((t, D), F32), jax.ShapeDtypeStruct((t, 2 * F), BF16), jax.ShapeDtypeStruct((t, D), BF16)] + head_shapes,
        scratch_shapes=[pltpu.VMEM((F, D), BF16)] * 3 + [pltpu.VMEM((TM, F), BF16), pltpu.SemaphoreType.DMA((3, N_DEV))],
        args=(x, gain, gw_gu, gw_d) + (() if head is None else tuple(head)), comms=comms)
    return (res[0], res[1], res[2], got, *res[3:])


def _ffn_bwd_hidden(ab, dout, gw_d, off_d, name):
    t = ab.shape[0]

    def tile(w):
        return pl.BlockSpec((TM, w), lambda i: (i, 0))

    def hidden(ab_ref, do_ref, d_ref, dab_ref, s_ref, df_ref, wd, sems):
        @pl.when(pl.program_id(0) == 0)
        def _():
            for cp in _load_rows(d_ref, wd, off_d, R_FF, sems):
                cp.wait()

        df = (0.5 * do_ref[...]).astype(BF16)
        df_ref[...] = df
        for c in range(F // FC):
            a = ab_ref[:, c * FC:(c + 1) * FC].astype(F32)
            b = ab_ref[:, F + c * FC:F + (c + 1) * FC].astype(F32)
            sg = jax.nn.sigmoid(a)
            sl = a * sg
            ds = _nt(df, wd[pl.ds(c * FC, FC), :])
            dab_ref[:, c * FC:(c + 1) * FC] = (ds * b * (sg * (1.0 + a * (1.0 - sg)))).astype(BF16)
            dab_ref[:, F + c * FC:F + (c + 1) * FC] = (ds * sl).astype(BF16)
            s_ref[:, c * FC:(c + 1) * FC] = (sl * b).astype(BF16)

    res, _ = _call(
        hidden, name=name, grid=(t // TM,),
        in_specs=[tile(2 * F), tile(D), ANY], out_specs=[tile(2 * F), tile(F), tile(D)],
        out_shape=[jax.ShapeDtypeStruct((t, 2 * F), BF16), jax.ShapeDtypeStruct((t, F), BF16), jax.ShapeDtypeStruct((t, D), BF16)],
        scratch_shapes=[pltpu.VMEM((F, D), BF16), pltpu.SemaphoreType.DMA((N_DEV,))],
        args=(ab, dout, gw_d))
    return res


def _ffn_bwd_input(x, gain, dab, dout, gw_gu, name, comms=()):
    t = x.shape[0]

    def body(x_ref, g_ref, dab_ref, do_ref, gu_ref, dx_ref, dg_ref, wg, wu, sems):
        @pl.when(pl.program_id(0) == 0)
        def _():
            cps = _load_rows(gu_ref, wg, 0, R_FF, sems.at[0]) + _load_rows(gu_ref, wu, R_FF, R_FF, sems.at[1])
            dg_ref[...] = jnp.zeros_like(dg_ref)
            for cp in cps:
                cp.wait()

        gain_v = g_ref[...]
        xn, r = _rms(x_ref[...])
        dh = _nn(dab_ref[:, :F], wg[...]) + _nn(dab_ref[:, F:], wu[...])
        dxn, dg = _rms_bwd(dh, xn, r, gain_v)
        dg_ref[...] += dg
        dx_ref[...] = do_ref[...] + dxn

    def tile(w):
        return pl.BlockSpec((TM, w), lambda i: (i, 0))

    row = pl.BlockSpec((1, D), lambda i: (0, 0))
    return _call(
        body, name=name, grid=(t // TM,),
        in_specs=[tile(D), row, tile(2 * F), tile(D), ANY], out_specs=[tile(D), row],
        out_shape=[jax.ShapeDtypeStruct((t, D), F32), jax.ShapeDtypeStruct((1, D), F32)],
        scratch_shapes=[pltpu.VMEM((F, D), BF16)] * 2 + [pltpu.SemaphoreType.DMA((2, N_DEV))],
        args=(x, gain, dab, dout, gw_gu), comms=comms)


def _weight_grad(a, b, name, col_off=0, m=None, comms=None, tmm=256, mats=1):
    t = a.shape[0]
    m = a.shape[1] if m is None else m
    n = b.shape[1]
    first = col_off // tmm

    def body(a_ref, b_ref, o_ref):
        o_ref[...] = _tn(a_ref[...], b_ref[...]).astype(BF16)

    (out,), got = _call(
        body, name=name, grid=(m // tmm,),
        in_specs=[pl.BlockSpec((t, tmm), lambda i: (0, first + i)), pl.BlockSpec((t, n), lambda i: (0, 0))],
        out_specs=[pl.BlockSpec((tmm, n), lambda i: (i, 0))],
        out_shape=[jax.ShapeDtypeStruct((m, n), BF16)], scratch_shapes=[], args=(a, b), comms=comms or ())
    out = out.reshape(mats * N_DEV, m // (mats * N_DEV), n)
    return out if comms is None else (out, got)


def _mix_proj_fwd(x, gain, b_gate, gw, comms=()):
    t = x.shape[0]

    def body(x_ref, g_ref, bg_ref, gw_ref, q_ref, k_ref, v_ref, zs_ref, gt_ref, h_ref, win, sems):
        @pl.when(pl.program_id(0) == 0)
        def _():
            for cp in _load_rows(gw_ref, win, OFF_IN, R_IN, sems):
                cp.wait()

        xn, _ = _rms(x_ref[...])
        h = (xn * g_ref[...]).astype(BF16)
        h_ref[...] = h
        q_ref[...] = (_nt(h, win[0:512, :]) * 0.125).astype(BF16)
        k_ref[...] = _nt(h, win[512:1024, :]).astype(BF16)
        v_ref[...] = _nt(h, win[1024:1536, :]).astype(BF16)
        for c in range(2):
            zs_ref[:, c * 512:(c + 1) * 512] = _nt(h, win[1536 + c * 512:2048 + c * 512, :]).astype(BF16)
        for c in range(4):
            zg = _nt(h, win[2560 + c * 512:3072 + c * 512, :]) + bg_ref[:, c * 512:(c + 1) * 512]
            gt_ref[:, c * 512:(c + 1) * 512] = jax.nn.sigmoid(zg).astype(BF16)

    def tile(w):
        return pl.BlockSpec((TM, w), lambda i: (i, 0))

    return _call(
        body, name="mix_proj_fwd", grid=(t // TM,),
        in_specs=[tile(D), pl.BlockSpec((1, D), lambda i: (0, 0)), pl.BlockSpec((1, 2 * D), lambda i: (0, 0)), ANY],
        out_specs=[tile(D_ATT), tile(D_ATT), tile(D_ATT), tile(2 * D_SGU), tile(2 * D), tile(D)],
        out_shape=[jax.ShapeDtypeStruct((t, D_ATT), BF16)] * 3 + [jax.ShapeDtypeStruct((t, 2 * D_SGU), BF16),
                                                                   jax.ShapeDtypeStruct((t, 2 * D), BF16),
                                                                   jax.ShapeDtypeStruct((t, D), BF16)],
        scratch_shapes=[pltpu.VMEM((D_IN, D), BF16), pltpu.SemaphoreType.DMA((N_DEV,))],
        args=(x, gain, b_gate, gw), comms=comms)


def _mix_proj_bwd(dz, x, gain, dres, gw, comms=()):
    t = x.shape[0]

    def body(dz_ref, x_ref, g_ref, dr_ref, gw_ref, dx_ref, dg_ref, win, sems):
        @pl.when(pl.program_id(0) == 0)
        def _():
            cps = _load_rows(gw_ref, win, OFF_IN, R_IN, sems)
            dg_ref[...] = jnp.zeros_like(dg_ref)
            for cp in cps:
                cp.wait()

        dh = _nn(dz_ref[...], win[...])
        xn, r = _rms(x_ref[...])
        dxn, dg = _rms_bwd(dh, xn, r, g_ref[...])
        dg_ref[...] += dg
        dx_ref[...] = dr_ref[...] + dxn

    def tile(w):
        return pl.BlockSpec((TM, w), lambda i: (i, 0))

    row = pl.BlockSpec((1, D), lambda i: (0, 0))
    return _call(
        body, name="mix_proj_bwd", grid=(t // TM,),
        in_specs=[tile(D_IN), tile(D), row, tile(D), ANY], out_specs=[tile(D), row],
        out_shape=[jax.ShapeDtypeStruct((t, D), F32), jax.ShapeDtypeStruct((1, D), F32)],
        scratch_shapes=[pltpu.VMEM((D_IN, D), BF16), pltpu.SemaphoreType.DMA((N_DEV,))],
        args=(dz, x, gain, dres, gw), comms=comms)


SKEW_W = KW + QB
N_CAP = 2 * QB - REL_CLIP + 1


def _band_bias(rel_bias):
    cap = rel_bias[:, 2 * REL_CLIP:]
    diag = jnp.concatenate([jnp.broadcast_to(cap, (HEADS, N_CAP)), rel_bias[:, 2 * REL_CLIP - 1::-1],
                            jnp.broadcast_to(cap, (HEADS, SKEW_W - N_CAP - 2 * REL_CLIP))], axis=1)

    def body(d_ref, o_ref):
        lag = lax.broadcasted_iota(jnp.int32, (QB, KW), 1) // CHUNK - lax.broadcasted_iota(jnp.int32, (QB, KW), 0) // CHUNK
        band = (lag >= 0) & (lag <= N_LEFT)
        for h in range(HEADS):
            rows = jnp.broadcast_to(d_ref[h:h + 1, :], (QB, SKEW_W))
            o_ref[h] = jnp.where(band, pltpu.roll(rows, 0, 1, stride=1, stride_axis=0)[:, :KW], NEG_INF)

    return pl.pallas_call(body, name="band_bias", out_shape=jax.ShapeDtypeStruct((HEADS, QB, KW), F32))(diag)


def _att_specs():
    qspec = pl.BlockSpec((QB, D_ATT), lambda g: (g, 0))
    kspecs = [pl.BlockSpec((QB, D_ATT), lambda g: (jnp.maximum(g - 2, 0), 0)),
              pl.BlockSpec((QB, D_ATT), lambda g: (jnp.maximum(g - 1, 0), 0)), qspec]
    bspec = pl.BlockSpec((HEADS, QB, KW), lambda g: (0, 0, 0))
    return qspec, kspecs, bspec


def _att_probs(qm, kp, bias, valid):
    s = jnp.where(valid, _nt(qm, kp) + bias, NEG_INF)
    e = jnp.exp(s - jnp.max(s, axis=-1, keepdims=True))
    return e / jnp.sum(e, axis=-1, keepdims=True)


def _att_valid():
    g = pl.program_id(0)
    blk = lax.broadcasted_iota(jnp.int32, (QB, KW), 1) // QB
    return (blk + g) >= 2


def _att_fwd(q, k, v, bias, comms=()):
    t = q.shape[0]

    def body(q_ref, k0, k1, k2, v0, v1, v2, b_ref, y_ref):
        valid = _att_valid()
        first = lax.broadcasted_iota(jnp.int32, (1, 128), 1) < 64
        for p in range(HEADS // 2):
            lanes = slice(p * 128, (p + 1) * 128)
            qp = q_ref[:, lanes]
            kp = jnp.concatenate([k0[:, lanes], k1[:, lanes], k2[:, lanes]], axis=0)
            vp = jnp.concatenate([v0[:, lanes], v1[:, lanes], v2[:, lanes]], axis=0)
            out = jnp.zeros((QB, 128), F32)
            for hh in range(2):
                mask = first if hh == 0 else jnp.logical_not(first)
                pr = _att_probs(jnp.where(mask, qp, 0), kp, b_ref[2 * p + hh], valid)
                out = out + _nn(pr.astype(BF16), jnp.where(mask, vp, 0))
            y_ref[:, lanes] = out.astype(BF16)

    qspec, kspecs, bspec = _att_specs()
    (out,), got = _call(
        body, name="att_fwd", grid=(t // QB,),
        in_specs=[qspec] + kspecs + kspecs + [bspec], out_specs=[qspec],
        out_shape=[jax.ShapeDtypeStruct((t, D_ATT), BF16)], scratch_shapes=[],
        args=(q, k, k, k, v, v, v, bias), comms=comms)
    return out, got


def _att_bwd(q, k, v, bias, dy, comms=()):
    t = q.shape[0]
    n_blocks = t // QB

    def body(q_ref, k0, k1, k2, v0, v1, v2, b_ref, dy_ref, dq_ref, dk_ref, dv_ref, db_ref, dk_acc, dv_acc):
        g = pl.program_id(0)

        @pl.when(g == 0)
        def _():
            db_ref[...] = jnp.zeros_like(db_ref)
            dk_acc[...] = jnp.zeros_like(dk_acc)
            dv_acc[...] = jnp.zeros_like(dv_acc)

        valid = _att_valid()
        first = lax.broadcasted_iota(jnp.int32, (1, 128), 1) < 64
        for p in range(HEADS // 2):
            lanes = slice(p * 128, (p + 1) * 128)
            qp = q_ref[:, lanes]
            dyp = dy_ref[:, lanes]
            kp = jnp.concatenate([k0[:, lanes], k1[:, lanes], k2[:, lanes]], axis=0)
            vp = jnp.concatenate([v0[:, lanes], v1[:, lanes], v2[:, lanes]], axis=0)
            dq = jnp.zeros((QB, 128), F32)
            dk = jnp.zeros((KW, 128), F32)
            dv = jnp.zeros((KW, 128), F32)
            for hh in range(2):
                mask = first if hh == 0 else jnp.logical_not(first)
                qm = jnp.where(mask, qp, 0)
                dym = jnp.where(mask, dyp, 0)
                pr = _att_probs(qm, kp, b_ref[2 * p + hh], valid)
                dp = _nt(dym, vp)
                ds = pr * (dp - jnp.sum(dp * pr, axis=-1, keepdims=True))
                db_ref[2 * p + hh] += ds
                dsb = ds.astype(BF16)
                dq = dq + _nn(dsb, jnp.where(mask, kp, 0))
                dk = dk + _tn(dsb, qm)
                dv = dv + _tn(pr.astype(BF16), dym)
            dq_ref[:, lanes] = (dq * 0.125).astype(BF16)
            for j in range(3):
                rows = pl.ds(pl.multiple_of(jnp.maximum(g - 2 + j, 0) * QB, QB), QB)
                dk_acc[rows, lanes] += dk[j * QB:(j + 1) * QB]
                dv_acc[rows, lanes] += dv[j * QB:(j + 1) * QB]

        @pl.when(g == n_blocks - 1)
        def _():
            dk_ref[...] = dk_acc[...].astype(BF16)
            dv_ref[...] = dv_acc[...].astype(BF16)

    qspec, kspecs, bspec = _att_specs()
    full = pl.BlockSpec((t, D_ATT), lambda g: (0, 0))
    return _call(
        body, name="att_bwd", grid=(n_blocks,),
        in_specs=[qspec] + kspecs + kspecs + [bspec, qspec], out_specs=[qspec, full, full, bspec],
        out_shape=[jax.ShapeDtypeStruct((t, D_ATT), BF16)] * 3 + [jax.ShapeDtypeStruct((HEADS, QB, KW), F32)],
        scratch_shapes=[pltpu.VMEM((t, D_ATT), F32)] * 2,
        args=(q, k, k, k, v, v, v, bias, dy), comms=comms)


def _rel_bias_grad(dbias):
    def body(db_ref, cs_ref, tot_ref):
        lane = lax.broadcasted_iota(jnp.int32, (1, SKEW_W), 1)
        capped = (lane < N_CAP) | (lane > KW)
        pad = jnp.zeros((8, QB), F32)
        for h in range(HEADS):
            z = jnp.concatenate([db_ref[h, 0:8, :], pad], axis=1)
            for a in range(1, QB // 8):
                z = z + pltpu.roll(jnp.concatenate([db_ref[h, 8 * a:8 * a + 8, :], pad], axis=1), SKEW_W - 8 * a, 1)
            cs = z[0:1, :]
            for b in range(1, 8):
                cs = cs + pltpu.roll(z[b:b + 1, :], SKEW_W - b, 1)
            cs_ref[h:h + 1, :] = cs
            tot_ref[h:h + 1, :] = jnp.broadcast_to(jnp.sum(jnp.where(capped, cs, 0.0), axis=1, keepdims=True), (1, 128))

    cs, tot = pl.pallas_call(
        body, name="rel_bias_grad",
        out_shape=[jax.ShapeDtypeStruct((HEADS, SKEW_W), F32), jax.ShapeDtypeStruct((HEADS, 128), F32)],
    )(dbias)
    return jnp.concatenate([cs[:, KW:N_CAP - 1:-1], tot[:, :1]], axis=1)


def _sgu_mask():
    pos = np.arange(SGU_BLOCK)
    return (pos[:, None] // CHUNK) >= (pos[None, :] // CHUNK)


def _group_stack(blk, first):
    return jnp.concatenate([jnp.where(first, blk, 0), jnp.where(first, 0, blk)], axis=0)


def _sgu_norm(zs_ref, lng, lnb):
    zs = zs_ref[...].astype(F32)
    ga, th = _gelu(zs)
    u, vs = ga[:, :D_SGU], ga[:, D_SGU:]
    mu = jnp.mean(vs, axis=-1, keepdims=True)
    cen = vs - mu
    rstd = lax.rsqrt(jnp.mean(cen * cen, axis=-1, keepdims=True) + EPS)
    xhat = cen * rstd
    return zs, th, u, xhat, rstd, xhat * lng + lnb


def _sgu_mix(vb, wm2_ref, bsx, s_ref):
    first = lax.broadcasted_iota(jnp.int32, (1, 128), 1) < 64
    for n in range(TM // SGU_BLOCK):
        for p in range(4):
            blk = vb[n * 128:(n + 1) * 128, p * 128:(p + 1) * 128]
            s_ref[n * 128:(n + 1) * 128, p * 128:(p + 1) * 128] = _nn(wm2_ref[p], _group_stack(blk, first)) + bsx[:, p * 128:(p + 1) * 128]


def _merge_fwd(x, zs, gt, y_att, lng, lnb, wm2, bsx, gw):
    t = x.shape[0]

    def body(x_ref, zs_ref, gt_ref, ya_ref, lng_ref, lnb_ref, wm2_ref, bsx_ref, gw_ref, xo_ref, ys_ref, mg_ref,
             wbr, wo, s_scr, sems):
        @pl.when(pl.program_id(0) == 0)
        def _():
            for cp in _load_rows(gw_ref, wbr, OFF_BR, R_BR, sems.at[0]) + _load_rows(gw_ref, wo, OFF_WO, R_WO, sems.at[1]):
                cp.wait()

        _, _, u, _, _, vsn = _sgu_norm(zs_ref, lng_ref[...], lnb_ref[...])
        _sgu_mix(vsn.astype(BF16), wm2_ref, bsx_ref[...], s_scr)
        ys = (u * s_scr[...]).astype(BF16)
        ys_ref[...] = ys
        pa = _nt(ya_ref[...], wbr[:, :D_ATT])
        ps = _nt(ys, wbr[:, D_ATT:])
        mg = (gt_ref[:, :D].astype(F32) * pa + gt_ref[:, D:].astype(F32) * ps).astype(BF16)
        mg_ref[...] = mg
        xo_ref[...] = x_ref[...] + _nn(mg, wo[...])

    def tile(w):
        return pl.BlockSpec((TM, w), lambda i: (i, 0))

    def const(shape):
        return pl.BlockSpec(shape, lambda i: (0,) * len(shape))

    return pl.pallas_call(
        body, name="merge_fwd", grid=(t // TM,),
        in_specs=[tile(D), tile(2 * D_SGU), tile(2 * D), tile(D_ATT), const((1, D_SGU)), const((1, D_SGU)),
                  const((4, 128, 256)), const((128, D_SGU)), ANY],
        out_specs=[tile(D), tile(D_SGU), tile(D)],
        out_shape=[jax.ShapeDtypeStruct((t, D), F32), jax.ShapeDtypeStruct((t, D_SGU), BF16), jax.ShapeDtypeStruct((t, D), BF16)],
        scratch_shapes=[pltpu.VMEM((D, D), BF16), pltpu.VMEM((D, D), BF16), pltpu.VMEM((TM, D_SGU), F32),
                        pltpu.SemaphoreType.DMA((2, N_DEV))],
        compiler_params=_cparams(("arbitrary",)),
    )(x, zs, gt, y_att, lng, lnb, wm2, bsx, gw)


def _merge_bwd(dx, gt, y_att, y_sgu, gw):
    t = dx.shape[0]

    def body(dx_ref, gt_ref, ya_ref, ys_ref, gw_ref, dzg_ref, dya_ref, dys_ref, dpp_ref, dxb_ref, dbg_ref, wbr, wo, sems):
        @pl.when(pl.program_id(0) == 0)
        def _():
            cps = _load_rows(gw_ref, wbr, OFF_BR, R_BR, sems.at[0]) + _load_rows(gw_ref, wo, OFF_WO, R_WO, sems.at[1])
            dbg_ref[...] = jnp.zeros_like(dbg_ref)
            for cp in cps:
                cp.wait()

        dxb = dx_ref[...].astype(BF16)
        dxb_ref[...] = dxb
        dm = _nt(dxb, wo[...])
        for half, y_ref, w in ((0, ya_ref, wbr.at[:, :D_ATT]), (1, ys_ref, wbr.at[:, D_ATT:])):
            cols = slice(half * D, (half + 1) * D)
            gate = gt_ref[:, cols].astype(F32)
            branch = _nt(y_ref[...], w[...])
            dzg = dm * branch * gate * (1.0 - gate)
            dbg_ref[:, cols] += jnp.sum(dzg, axis=0, keepdims=True)
            dzg_ref[:, cols] = dzg.astype(BF16)
            dbr = (dm * gate).astype(BF16)
            dpp_ref[:, cols] = dbr
            dy = _nn(dbr, w[...])
            if half == 0:
                dya_ref[...] = dy.astype(BF16)
            else:
                dys_ref[...] = dy

    def tile(w):
        return pl.BlockSpec((TM, w), lambda i: (i, 0))

    return pl.pallas_call(
        body, name="merge_bwd", grid=(t // TM,),
        in_specs=[tile(D), tile(2 * D), tile(D_ATT), tile(D_SGU), ANY],
        out_specs=[tile(2 * D), tile(D_ATT), tile(D_SGU), tile(2 * D), tile(D), pl.BlockSpec((1, 2 * D), lambda i: (0, 0))],
        out_shape=[jax.ShapeDtypeStruct((t, 2 * D), BF16), jax.ShapeDtypeStruct((t, D_ATT), BF16), jax.ShapeDtypeStruct((t, D_SGU), F32),
                   jax.ShapeDtypeStruct((t, 2 * D), BF16), jax.ShapeDtypeStruct((t, D), BF16), jax.ShapeDtypeStruct((1, 2 * D), F32)],
        scratch_shapes=[pltpu.VMEM((D, D), BF16), pltpu.VMEM((D, D), BF16), pltpu.SemaphoreType.DMA((2, N_DEV))],
        compiler_params=_cparams(("arbitrary",)),
    )(dx, gt, y_att, y_sgu, gw)


def _sgu_bwd(zs, dys, lng, lnb, wm2, wmt2, bsx, comms=()):
    t = zs.shape[0]
    n_steps = t // TM

    def body(zs_ref, dys_ref, lng_ref, lnb_ref, wm2_ref, wmt2_ref, bsx_ref, dzs_ref, dw_ref, dbs_ref, dlg_ref, dlb_ref,
             s_scr, dv_scr, ds_acc):
        i = pl.program_id(0)

        @pl.when(i == 0)
        def _():
            dw_ref[...] = jnp.zeros_like(dw_ref)
            dlg_ref[...] = jnp.zeros_like(dlg_ref)
            dlb_ref[...] = jnp.zeros_like(dlb_ref)
            ds_acc[...] = jnp.zeros_like(ds_acc)

        lng = lng_ref[...]
        zs, th, u, xhat, rstd, vsn = _sgu_norm(zs_ref, lng, lnb_ref[...])
        vb = vsn.astype(BF16)
        _sgu_mix(vb, wm2_ref, bsx_ref[...], s_scr)
        dys = dys_ref[...]
        du = dys * s_scr[...]
        ds = dys * u
        dsb = ds.astype(BF16)
        first = lax.broadcasted_iota(jnp.int32, (1, 128), 1) < 64
        acc = jnp.zeros((SGU_BLOCK, D_SGU), F32)
        for n in range(TM // SGU_BLOCK):
            rows = slice(n * 128, (n + 1) * 128)
            acc = acc + ds[rows]
            for p in range(4):
                lanes = slice(p * 128, (p + 1) * 128)
                stack = _group_stack(dsb[rows, lanes], first)
                dv_scr[rows, lanes] = _nn(wmt2_ref[p], stack)
                dw_ref[p] += _nt(stack, vb[rows, lanes])
        ds_acc[...] += acc
        dvsn = dv_scr[...]
        dlg_ref[...] += jnp.sum(dvsn * xhat, axis=0, keepdims=True)
        dlb_ref[...] += jnp.sum(dvsn, axis=0, keepdims=True)
        dxh = dvsn * lng
        dvs = rstd * (dxh - jnp.mean(dxh, axis=-1, keepdims=True) - xhat * jnp.mean(dxh * xhat, axis=-1, keepdims=True))
        dga = jnp.concatenate([du, dvs], axis=1)
        dzs_ref[...] = (dga * _gelu_grad(zs, th)).astype(BF16)

        @pl.when(i == n_steps - 1)
        def _():
            r = lax.broadcasted_iota(jnp.int32, (256, 128), 0) % SGU_BLOCK
            c = lax.broadcasted_iota(jnp.int32, (256, 128), 1)
            keep = (r // CHUNK) >= (c // CHUNK)
            for p in range(4):
                dw_ref[p] = jnp.where(keep, dw_ref[p], 0.0)
            total = ds_acc[...]
            grp = lax.broadcasted_iota(jnp.int32, (SGU_BLOCK, D_SGU), 1) // 64
            lane = lax.broadcasted_iota(jnp.int32, (SGU_BLOCK, 128), 1)
            out = jnp.zeros((SGU_BLOCK, 128), F32)
            for gi in range(8):
                out = jnp.where(lane == gi, jnp.sum(jnp.where(grp == gi, total, 0.0), axis=1, keepdims=True), out)
            dbs_ref[...] = out

    def tile(w):
        return pl.BlockSpec((TM, w), lambda i: (i, 0))

    def const(shape):
        return pl.BlockSpec(shape, lambda i: (0,) * len(shape))

    return _call(
        body, name="sgu_bwd", grid=(n_steps,),
        in_specs=[tile(2 * D_SGU), tile(D_SGU), const((1, D_SGU)), const((1, D_SGU)), const((4, 128, 256)), const((4, 128, 256)),
                  const((128, D_SGU))],
        out_specs=[tile(2 * D_SGU), const((4, 256, 128)), const((128, 128)), const((1, D_SGU)), const((1, D_SGU))],
        out_shape=[jax.ShapeDtypeStruct((t, 2 * D_SGU), BF16), jax.ShapeDtypeStruct((4, 256, 128), F32),
                   jax.ShapeDtypeStruct((128, 128), F32), jax.ShapeDtypeStruct((1, D_SGU), F32), jax.ShapeDtypeStruct((1, D_SGU), F32)],
        scratch_shapes=[pltpu.VMEM((TM, D_SGU), F32), pltpu.VMEM((TM, D_SGU), F32), pltpu.VMEM((SGU_BLOCK, D_SGU), F32)],
        args=(zs, dys, lng, lnb, wm2, wmt2, bsx), comms=comms)


def _adamw(g, w, m, v):
    m = ADAM_B1 * m + (1.0 - ADAM_B1) * g
    v = ADAM_B2 * v + (1.0 - ADAM_B2) * (g * g)
    m_hat = m / (1.0 - ADAM_B1 ** ADAM_STEP)
    v_hat = v / (1.0 - ADAM_B2 ** ADAM_STEP)
    return -ADAM_LR * (m_hat / (jnp.sqrt(v_hat) + ADAM_EPS) + ADAM_WD * w), m, v


def _adamw_matrix(parts, sums, w, m, v, transposed, name):
    _, r, c = parts.shape
    tc = 256

    def body(own_ref, p_ref, s_ref, w_ref, m_ref, v_ref, g_ref, d_ref, mo_ref, vo_ref):
        g = p_ref[0].astype(F32) + p_ref[1].astype(F32) + p_ref[2].astype(F32) + s_ref[...].astype(F32)
        g = g.T if transposed else g
        g_ref[...] = g
        d_ref[...], mo_ref[...], vo_ref[...] = _adamw(g, w_ref[...], m_ref[...], v_ref[...])

    own = pl.BlockSpec((None, tc, r), lambda i, o: (0, i, 0)) if transposed else pl.BlockSpec((None, r, tc), lambda i, o: (0, 0, i))
    return pl.pallas_call(
        body, name=name,
        grid_spec=pltpu.PrefetchScalarGridSpec(
            num_scalar_prefetch=1, grid=(c // tc,),
            in_specs=[pl.BlockSpec((3, r, tc), lambda i, o: (0, 0, i)), pl.BlockSpec((None, r, tc), lambda i, o: (o[0], 0, i)), own, own, own],
            out_specs=[own] * 4),
        out_shape=[jax.ShapeDtypeStruct(w.shape, F32)] * 4,
        compiler_params=_cparams(("arbitrary",)),
    )(_my_index(("x", 2), ("y", 1)), parts, sums, w, m, v)


_SMALL_2D = {"norm_ffn1": (1, D), "norm_mix": (1, D), "norm_ffn2": (1, D), "norm_final": (1, D), "b_gate": (1, 2 * D),
             "sgu_ln_g": (1, D_SGU), "sgu_ln_b": (1, D_SGU), "sgu_b_s": (8, SGU_BLOCK), "rel_bias": (HEADS, N_REL),
             "sgu_w_s": (8 * SGU_BLOCK, SGU_BLOCK)}


def _adamw_small(parts, loss_parts, p):
    names = list(parts)
    n = len(names)

    def body(*refs):
        got, loss_got, wmv, outs, loss_out = refs[:n], refs[n], refs[n + 1:4 * n + 1], refs[4 * n + 1:8 * n + 1], refs[8 * n + 1]
        for i, name in enumerate(names):
            g = got[i][0]
            for k in range(1, N_DEV):
                g = g + got[i][k]
            if name == "sgu_b_s":
                g = g.T[0:8, :]
            res = (g,) + _adamw(g, wmv[3 * i][...], wmv[3 * i + 1][...], wmv[3 * i + 2][...])
            for o_ref, val in zip(outs[4 * i:4 * i + 4], res):
                o_ref[...] = val
        total = loss_got[0]
        for k in range(1, N_DEV):
            total = total + loss_got[k]
        loss_out[...] = total

    wmv = [p[pre + name].reshape(_SMALL_2D[name]) for name in names for pre in ("", "m_", "v_")]
    res = pl.pallas_call(
        body, name="adamw_small",
        out_shape=[jax.ShapeDtypeStruct(_SMALL_2D[name], F32) for name in names for _ in range(4)] + [jax.ShapeDtypeStruct((1, 128), F32)],
        compiler_params=_cparams())(*[parts[name] for name in names], loss_parts, *wmv)
    return [{name: res[4 * i + j].reshape(p[name].shape) for i, name in enumerate(names)} for j in range(4)], res[-1]


def _pack_rows(groups, name):
    flat = [a for grp in groups for a, _ in grp]
    rows = [grp[0][0].shape[2] if grp[0][1] else grp[0][0].shape[1] for grp in groups]

    def body(*refs):
        o_ref, pos, off = refs[-1], 0, 0
        for grp, r in zip(groups, rows):
            vals = []
            for _, transposed in grp:
                val = refs[pos][0]
                vals.append(val.T if transposed else val)
                pos += 1
            o_ref[off:off + r, :] = (vals[0] if len(vals) == 1 else jnp.concatenate(vals, axis=1)).astype(BF16)
            off += r

    return pl.pallas_call(body, name=name, out_shape=jax.ShapeDtypeStruct((sum(rows), D), BF16), compiler_params=_cparams())(*flat)


def _step(x, target, p):
    n1, nm, n2 = p["norm_ffn1"], p["norm_mix"], p["norm_ffn2"]
    nf = p["norm_final"].reshape(1, D)
    lng, lnb = p["sgu_ln_g"], p["sgu_ln_b"]
    w_m = jnp.where(jnp.asarray(_sgu_mask())[None], p["sgu_w_s"][0], 0.0).astype(BF16)
    wm2 = jnp.concatenate([w_m[0::2], w_m[1::2]], axis=2)
    w_mt = w_m.transpose(0, 2, 1)
    wmt2 = jnp.concatenate([w_mt[0::2], w_mt[1::2]], axis=2)
    bsx = jnp.repeat(p["sgu_b_s"][0].T, 64, axis=1)
    bias = _band_bias(p["rel_bias"][0])

    def chip_sums(grads, name):
        gots = _comm_only([_SiblingSwap(grads)], "swap_" + name)
        return _pair_sums(grads, gots, "pair_sums_" + name)
    def as_rows(a):
        return jnp.swapaxes(a, 1, 2)

    def updates(parts, sums, names):
        res = {}
        for pt, sm, n in zip(parts, sums, names):
            if p[n].shape[1:] == pt.shape[1:]:
                res[n] = _adamw_matrix(pt, sm, p[n], p["m_" + n], p["v_" + n], False, "adamw_" + n)
            elif p[n].shape[2] > 128:
                res[n] = [as_rows(o) for o in _adamw_matrix(pt, sm, as_rows(p[n]), as_rows(p["m_" + n]), as_rows(p["v_" + n]), False,
                                                            "adamw_" + n)]
            else:
                res[n] = _adamw_matrix(pt, sm, p[n], p["m_" + n], p["v_" + n], True, "adamw_" + n)
        return res

    rows1 = _pack_rows([[(as_rows(p["ffn1_w_gate"]), False)], [(as_rows(p["ffn1_w_up"]), False)], [(p["ffn1_w_down"], False)]], "pack_ffn1")
    rows_m = _pack_rows([[(as_rows(p["w_in"]), False)], [(p["w_branch_att"], True), (p["w_branch_sgu"], True)], [(p["w_out"], False)]],
                        "pack_mixer")
    rows2d = _pack_rows([[(p["ffn2_w_down"], False)]], "pack_ffn2_down")
    rows2gu = _pack_rows([[(as_rows(p["ffn2_w_gate"]), False)], [(as_rows(p["ffn2_w_up"]), False)]], "pack_ffn2_gate_up")
    (gw1,) = _comm_only([_Gather(rows1)], "gather_ffn1")
    x1, ab1, h1, (gwm,) = _ffn_fwd(x, n1, gw1, gw1, 2 * R_FF, "ffn1_fwd", [_Gather(rows_m)])
    (q, k, v, zs, gt, h2), (gw2d,) = _mix_proj_fwd(x1, nm, p["b_gate"], gwm, [_Gather(rows2d)])
    y_att, (gw2gu,) = _att_fwd(q, k, v, bias, [_Gather(rows2gu)])
    x2, y_sgu, merged = _merge_fwd(x1, zs, gt, y_att, lng, lnb, wm2, bsx, gwm)
    dx3, ab2, hb, _, d_nf, loss = _ffn_fwd(x2, n2, gw2gu, gw2d, 0, "ffn2_fwd", head=(nf, target))

    dab, sb, dfb = _ffn_bwd_hidden(ab2, dx3, gw2d, 0, "ffn2_bwd_hidden")
    (dx2, d_n2), _ = _ffn_bwd_input(x2, n2, dab, dx3, gw2gu, "ffn2_bwd")
    g_gu = _weight_grad(dab, hb, "ffn2_dw_gate_up", tmm=512, mats=2)
    dzg, dya, dys, dpp, dxb, d_bg = _merge_bwd(dx2, gt, y_att, y_sgu, gwm)
    g_late = [(g_gu, 0), (g_gu, 1), (_weight_grad(sb, dfb, "ffn2_dw_down"), 0), (_weight_grad(dpp, y_att, "dw_branch_att", 0, D), 0),
              (_weight_grad(dpp, y_sgu, "dw_branch_sgu", D, D), 0), (_weight_grad(merged, dxb, "dw_out"), 0)]
    late = ("ffn2_w_gate", "ffn2_w_up", "ffn2_w_down", "w_branch_att", "w_branch_sgu", "w_out")
    (dzs, d_wm, d_bs, d_lng, d_lnb), gots_late = _sgu_bwd(zs, dys, lng, lnb, wm2, wmt2, bsx, [_SiblingSwap(g_late)])
    sums_late = _pair_sums(g_late, gots_late, "pair_sums_late")
    (dq, dk, dv, d_bias), parts_late = _att_bwd(q, k, v, bias, dya, [_ChipScatter(sums_late)])
    big = updates(parts_late, sums_late, late)
    d_rel = _rel_bias_grad(d_bias)
    dz = jnp.concatenate([dq, dk, dv, dzs, dzg], axis=1)
    small = {"norm_ffn2": d_n2, "norm_final": d_nf, "b_gate": d_bg, "sgu_ln_g": d_lng, "sgu_ln_b": d_lnb, "sgu_b_s": d_bs,
             "rel_bias": d_rel, "sgu_w_s": d_wm.reshape(_SMALL_2D["sgu_w_s"])}
    g_in, (*early_parts, loss_parts) = _weight_grad(dz, h2, "dw_in", comms=[_AllToAll(list(small.values()) + [loss])])
    sums_in = chip_sums([(g_in, 0)], "w_in")
    (dx1, d_nm), parts_in = _mix_proj_bwd(dz, x1, nm, dx2, gwm, [_ChipScatter(sums_in)])
    big.update(updates(parts_in, sums_in, ("w_in",)))

    dab, sb, dfb = _ffn_bwd_hidden(ab1, dx1, gw1, 2 * R_FF, "ffn1_bwd_hidden")
    sums_d = chip_sums([(_weight_grad(sb, dfb, "ffn1_dw_down"), 0)], "ffn1_down")
    g_gu, (parts_d, nm_parts) = _weight_grad(dab, h1, "ffn1_dw_gate_up", comms=[_ChipScatter(sums_d), _AllToAll([d_nm])], tmm=512, mats=2)
    sums_gu = chip_sums([(g_gu, 0), (g_gu, 1)], "ffn1_gate_up")
    (dx0, d_n1), parts_gu = _ffn_bwd_input(x, n1, dab, dx1, gw1, "ffn1_bwd", [_ChipScatter(sums_gu)])
    (n1_parts,) = _comm_only([_AllToAll([d_n1])], "gather_norm_ffn1")
    big.update(updates([parts_d] + parts_gu, sums_d + sums_gu, ("ffn1_w_down", "ffn1_w_gate", "ffn1_w_up")))
    out_s, loss_sum = _adamw_small(dict(zip(small, early_parts), norm_mix=nm_parts, norm_ffn1=n1_parts), loss_parts, p)
    return dx0, loss_sum[0, 0], [{**{n: four[i] for n, four in big.items()}, **s} for i, s in enumerate(out_s)]


_OUT_ORDER = ("norm_ffn1", "ffn1_w_gate", "ffn1_w_up", "ffn1_w_down", "norm_mix", "w_in", "b_gate", "rel_bias", "sgu_ln_g", "sgu_ln_b",
              "sgu_w_s", "sgu_b_s", "w_branch_att", "w_branch_sgu", "w_out", "norm_ffn2", "ffn2_w_gate", "ffn2_w_up", "ffn2_w_down",
              "norm_final")


def kernel(x, norm_ffn1, ffn1_w_gate, ffn1_w_up, ffn1_w_down, norm_mix, w_in, b_gate, rel_bias, sgu_ln_g, sgu_ln_b, sgu_w_s, sgu_b_s, w_branch_att, w_branch_sgu, w_out, norm_ffn2, ffn2_w_gate, ffn2_w_up, ffn2_w_down, norm_final, loss_target, m_norm_ffn1, m_ffn1_w_gate, m_ffn1_w_up, m_ffn1_w_down, m_norm_mix, m_w_in, m_b_gate, m_rel_bias, m_sgu_ln_g, m_sgu_ln_b, m_sgu_w_s, m_sgu_b_s, m_w_branch_att, m_w_branch_sgu, m_w_out, m_norm_ffn2, m_ffn2_w_gate, m_ffn2_w_up, m_ffn2_w_down, m_norm_final, v_norm_ffn1, v_ffn1_w_gate, v_ffn1_w_up, v_ffn1_w_down, v_norm_mix, v_w_in, v_b_gate, v_rel_bias, v_sgu_ln_g, v_sgu_ln_b, v_sgu_w_s, v_sgu_b_s, v_w_branch_att, v_w_branch_sgu, v_w_out, v_norm_ffn2, v_ffn2_w_gate, v_ffn2_w_up, v_ffn2_w_down, v_norm_final):
    args = dict(locals())
    dx, loss, outs = _step(x[0], loss_target[0], {pre + n: args[pre + n] for pre in ("", "m_", "v_") for n in _OUT_ORDER})
    return (loss, dx[None], *[o[n] for o in outs for n in _OUT_ORDER])
```

```python
import functools

import numpy as np
import jax
import jax.numpy as jnp
from jax import lax
from jax.experimental import pallas as pl
from jax.experimental.pallas import tpu as pltpu

F32 = jnp.float32
BF16 = jnp.bfloat16

N_DEV = 8
D = 1024
F = 2816
D_ATT = 512
D_SGU = 512
D_IN = 4608
HEADS = 8
CHUNK = 64
N_LEFT = 8
REL_CLIP = 256
N_REL = 2 * REL_CLIP + 1
SGU_BLOCK = 128
EPS = 1e-6
NEG_INF = -1e30
QB = 256
KW = 3 * QB

R_FF, R_IN, R_BR, R_WO = F // N_DEV, D_IN // N_DEV, D // N_DEV, D // N_DEV
OFF_IN, OFF_BR, OFF_WO = 0, R_IN, R_IN + R_BR
FFN_ROWS = 3 * R_FF
MIX_ROWS = R_IN + R_BR + R_WO

FC = 256
TM = 512
VMEM_LIMIT = 56 * 1024 * 1024

ADAM_LR, ADAM_B1, ADAM_B2, ADAM_EPS, ADAM_WD, ADAM_STEP = 0.001, 0.9, 0.999, 1e-08, 0.01, 10

MESH = pl.DeviceIdType.MESH
ANY = pl.BlockSpec(memory_space=pl.ANY)


def _nt(a, b):
    return lax.dot_general(a, b, (((1,), (1,)), ((), ())), preferred_element_type=F32)


def _tn(a, b):
    return lax.dot_general(a, b, (((0,), (0,)), ((), ())), preferred_element_type=F32)


def _nn(a, b):
    return jnp.dot(a, b, preferred_element_type=F32)


def _cparams(sem=None):
    return pltpu.CompilerParams(dimension_semantics=sem, vmem_limit_bytes=VMEM_LIMIT)


def _load_rows(gw_ref, dst, off, rows, sems):
    copies = [pltpu.make_async_copy(gw_ref.at[k, pl.ds(off, rows), :], dst.at[pl.ds(k * rows, rows), :], sems.at[k])
              for k in range(N_DEV)]
    for cp in copies:
        cp.start()
    return copies


def _rms(xv):
    r = lax.rsqrt(jnp.mean(xv * xv, axis=-1, keepdims=True) + EPS)
    return xv * r, r


def _rms_bwd(dh, xn, r, gain):
    dxn = dh * gain
    dx = r * (dxn - xn * jnp.mean(dxn * xn, axis=-1, keepdims=True))
    return dx, jnp.sum(dh * xn, axis=0, keepdims=True)


def _gelu(x):
    t = jnp.tanh(0.7978845608028654 * (x + 0.044715 * x * x * x))
    return 0.5 * x * (1.0 + t), t


def _gelu_grad(x, t):
    return 0.5 * (1.0 + t) + 0.5 * x * (1.0 - t * t) * 0.7978845608028654 * (1.0 + 3.0 * 0.044715 * x * x)


def _place():
    x, y, cc = lax.axis_index("x"), lax.axis_index("y"), lax.axis_index("c")
    return x, y, cc, [(1 - x, y), (x, 1 - y), (1 - x, 1 - y)]


class _Gather:
    def __init__(self, shard):
        self.inputs = [shard]
        self.out_shape = [jax.ShapeDtypeStruct((N_DEV,) + shard.shape, shard.dtype)]
        self.scratch = [pltpu.SemaphoreType.DMA((7,)), pltpu.SemaphoreType.DMA((7,)), pltpu.SemaphoreType.DMA]

    def _copies(self, ins, outs, scr):
        (x_ref,), (out_ref,), (send_sems, recv_sems, local_sem) = ins, outs, scr
        x, y, cc, chips = _place()

        def slab(px, py, pc):
            return out_ref.at[4 * px + 2 * py + pc]

        def copy(k, block, to, src=None):
            return pltpu.make_async_remote_copy(
                src_ref=slab(*block) if src is None else src, dst_ref=slab(*block),
                send_sem=send_sems.at[k], recv_sem=recv_sems.at[k], device_id=to, device_id_type=MESH)

        me, sibling = (x, y, cc), (x, y, 1 - cc)
        x_nbr, y_nbr, diagonal = chips
        mine = pltpu.make_async_copy(x_ref, slab(*me), local_sem)
        first = [copy(0, me, sibling, src=x_ref), copy(1, me, (*x_nbr, cc), src=x_ref), copy(2, me, (*y_nbr, cc), src=x_ref)]
        neighbours = [copy(1, (*x_nbr, cc), me), copy(2, (*y_nbr, cc), me)]
        second_hand = copy(3, (x ^ (1 - cc), y ^ cc, cc), (x ^ cc, y ^ (1 - cc), cc))
        from_diagonal = copy(3, (*diagonal, cc), me)
        passed = [copy(4 + j, (*chip, cc), sibling) for j, chip in enumerate(chips)]
        from_sibling = [copy(0, sibling, me)] + [copy(4 + j, (*chip, 1 - cc), me) for j, chip in enumerate(chips)]
        return mine, first, neighbours, second_hand, from_diagonal, passed, from_sibling

    def begin(self, *refs):
        mine, first = self._copies(*refs)[:2]
        mine.start()
        for cp in first:
            cp.start()

    def mid(self, *refs):
        _, _, neighbours, second_hand, _, passed, _ = self._copies(*refs)
        for cp in neighbours:
            cp.wait_recv()
        second_hand.start()
        passed[0].start()
        passed[1].start()

    def relay(self, *refs):
        _, _, _, _, from_diagonal, passed, _ = self._copies(*refs)
        from_diagonal.wait_recv()
        passed[2].start()

    def end(self, *refs):
        mine, first, _, second_hand, _, passed, from_sibling = self._copies(*refs)
        for cp in from_sibling:
            cp.wait_recv()
        for cp in first + [second_hand] + passed:
            cp.wait_send()
        mine.wait()


class _Direct:
    def begin(self, *refs):
        keep, give = self._copies(*refs)
        for cp in keep + give:
            cp.start()

    def mid(self, *refs):
        pass

    def relay(self, *refs):
        pass

    def end(self, *refs):
        keep, give = self._copies(*refs)
        for cp in give:
            cp.wait_recv()
        for cp in give:
            cp.wait_send()
        for cp in keep:
            cp.wait()


class _SiblingSwap(_Direct):
    def __init__(self, grads):
        n = len(grads)
        self.which = [w for _, w in grads]
        self.inputs = [g for g, _ in grads]
        self.out_shape = [jax.ShapeDtypeStruct((4,) + g.shape[1:], g.dtype) for g, _ in grads]
        self.scratch = [pltpu.SemaphoreType.DMA((n, 4)), pltpu.SemaphoreType.DMA((n, 4))]

    def _copies(self, ins, outs, scr):
        send_sems, recv_sems = scr
        x, y, cc, _ = _place()
        return [], [pltpu.make_async_remote_copy(src_ref=g_ref.at[N_DEV * w + 2 * j + 1 - cc], dst_ref=got_ref.at[j],
                                                 send_sem=send_sems.at[i, j], recv_sem=recv_sems.at[i, j], device_id=(x, y, 1 - cc),
                                                 device_id_type=MESH)
                    for i, (g_ref, got_ref, w) in enumerate(zip(ins, outs, self.which)) for j in range(4)]


class _ChipScatter(_Direct):
    def __init__(self, sums):
        n = len(sums)
        self.inputs = list(sums)
        self.out_shape = [jax.ShapeDtypeStruct((3,) + s.shape[1:], s.dtype) for s in sums]
        self.scratch = [pltpu.SemaphoreType.DMA((n, 3)), pltpu.SemaphoreType.DMA((n, 3))]

    def _copies(self, ins, outs, scr):
        send_sems, recv_sems = scr
        _, _, cc, chips = _place()
        return [], [pltpu.make_async_remote_copy(src_ref=s_ref.at[2 * px + py], dst_ref=got_ref.at[j], send_sem=send_sems.at[i, j],
                                                 recv_sem=recv_sems.at[i, j], device_id=(px, py, cc), device_id_type=MESH)
                    for i, (s_ref, got_ref) in enumerate(zip(ins, outs)) for j, (px, py) in enumerate(chips)]


class _AllToAll(_Direct):
    def __init__(self, blocks):
        n = len(blocks)
        self.inputs = list(blocks)
        self.out_shape = [jax.ShapeDtypeStruct((N_DEV,) + b.shape, b.dtype) for b in blocks]
        self.scratch = [pltpu.SemaphoreType.DMA((n, 7)), pltpu.SemaphoreType.DMA((n, 7)), pltpu.SemaphoreType.DMA((n,))]

    def _copies(self, ins, outs, scr):
        send_sems, recv_sems, local_sems = scr
        x, y, cc, _ = _place()
        me = 4 * x + 2 * y + cc
        keep = [pltpu.make_async_copy(b_ref, got_ref.at[me], local_sems.at[i]) for i, (b_ref, got_ref) in enumerate(zip(ins, outs))]
        give = [pltpu.make_async_remote_copy(src_ref=b_ref, dst_ref=got_ref.at[me], send_sem=send_sems.at[i, k - 1],
                                             recv_sem=recv_sems.at[i, k - 1],
                                             device_id=(x ^ ((k >> 2) & 1), y ^ ((k >> 1) & 1), cc ^ (k & 1)), device_id_type=MESH)
                for i, (b_ref, got_ref) in enumerate(zip(ins, outs)) for k in range(1, N_DEV)]
        return keep, give


def _split_refs(refs, counts):
    out, pos = [], 0
    for n in counts:
        out.append(list(refs[pos:pos + n]))
        pos += n
    return out


def _bind(comms, c_in, c_out, c_scr):
    ins = _split_refs(c_in, [len(c.inputs) for c in comms])
    outs = _split_refs(c_out, [len(c.out_shape) for c in comms])
    scr = _split_refs(c_scr, [len(c.scratch) for c in comms])
    return [(c, (i, o, s)) for c, i, o, s in zip(comms, ins, outs, scr)]


def _call(body, *, name, grid, in_specs, out_specs, out_shape, scratch_shapes, args, comms=()):
    c_in = [a for c in comms for a in c.inputs]
    c_out = [s for c in comms for s in c.out_shape]
    c_scr = [s for c in comms for s in c.scratch]
    counts = [len(in_specs), len(c_in), len(out_shape), len(c_out), len(scratch_shapes), len(c_scr)]

    def full(*refs):
        ins, cin, outs, cout, scr, cscr = _split_refs(refs, counts)
        bound = _bind(comms, cin, cout, cscr)
        if comms:
            def at(steps):
                return functools.reduce(jnp.logical_and, [pl.program_id(ax) == s for ax, s in enumerate(steps)])

            first, last = at([0] * len(grid)), at([n - 1 for n in grid])

            @pl.when(first)
            def _():
                for c, r in bound:
                    c.begin(*r)

            @pl.when(at([(grid[0] - 1) // 2] + [0] * (len(grid) - 1)))
            def _():
                for c, r in bound:
                    c.mid(*r)

            @pl.when(last)
            def _():
                for c, r in bound:
                    c.relay(*r)

        body(*ins, *outs, *scr)
        if comms:
            @pl.when(last)
            def _():
                for c, r in bound:
                    c.end(*r)

    res = pl.pallas_call(
        full, name=name, grid=grid,
        in_specs=list(in_specs) + [ANY] * len(c_in), out_specs=list(out_specs) + [ANY] * len(c_out),
        out_shape=list(out_shape) + c_out, scratch_shapes=list(scratch_shapes) + c_scr,
        compiler_params=_cparams(("arbitrary",) * len(grid)),
    )(*args, *c_in)
    return list(res[:len(out_shape)]), list(res[len(out_shape):])


def _comm_only(comms, name):
    c_in = [a for c in comms for a in c.inputs]
    c_out = [s for c in comms for s in c.out_shape]
    c_scr = [s for c in comms for s in c.scratch]

    def full(*refs):
        cin, cout, cscr = _split_refs(refs, [len(c_in), len(c_out), len(c_scr)])
        bound = _bind(comms, cin, cout, cscr)
        for phase in ("begin", "mid", "relay", "end"):
            for c, r in bound:
                getattr(c, phase)(*r)

    return list(pl.pallas_call(full, name=name, in_specs=[ANY] * len(c_in), out_specs=[ANY] * len(c_out), out_shape=c_out,
                               scratch_shapes=c_scr)(*c_in))


def _my_index(*axes_and_weights):
    return sum(w * lax.axis_index(a) for a, w in axes_and_weights).astype(jnp.int32).reshape(1)


def _pair_sums(grads, gots, name):
    n = len(grads)

    def body(c_ref, *refs):
        for a_ref, b_ref, o_ref in zip(refs[:n], refs[n:2 * n], refs[2 * n:]):
            o_ref[...] = (a_ref[...].astype(F32) + b_ref[...].astype(F32)).astype(BF16)

    def tile(g):
        return pl.BlockSpec((1,) + g.shape[1:], lambda j, c_ref: (j, 0, 0))

    def mine(g, w):
        return pl.BlockSpec((1, None) + g.shape[1:], lambda j, c_ref: (4 * w + j, c_ref[0], 0, 0))

    return list(pl.pallas_call(
        body, name=name,
        grid_spec=pltpu.PrefetchScalarGridSpec(num_scalar_prefetch=1, grid=(4,),
                                               in_specs=[mine(g, w) for g, w in grads] + [tile(g) for g in gots],
                                               out_specs=[tile(g) for g in gots]),
        out_shape=[jax.ShapeDtypeStruct(g.shape, BF16) for g in gots],
        compiler_params=_cparams(("arbitrary",)))(_my_index(("c", 1)), *[g.reshape((-1, 2) + g.shape[1:]) for g, _ in grads], *gots))


def _ffn_fwd(x, gain, gw_gu, gw_d, off_d, name, comms=(), head=None):
    t = x.shape[0]
    n_head = 0 if head is None else 2

    def body(x_ref, g_ref, gu_ref, d_ref, *refs):
        head_in, (o_ref, ab_ref, h_ref), head_out = refs[:n_head], refs[n_head:n_head + 3], refs[n_head + 3:2 * n_head + 3]
        wg, wu, wd, s_scr, sems = refs[2 * n_head + 3:]

        @pl.when(pl.program_id(0) == 0)
        def _():
            cps = _load_rows(gu_ref, wg, 0, R_FF, sems.at[0]) + _load_rows(gu_ref, wu, R_FF, R_FF, sems.at[1]) \
                + _load_rows(d_ref, wd, off_d, R_FF, sems.at[2])
            for o in head_out:
                o[...] = jnp.zeros_like(o)
            for cp in cps:
                cp.wait()

        xv = x_ref[...]
        xn, _ = _rms(xv)
        h = (xn * g_ref[...]).astype(BF16)
        h_ref[...] = h
        for c in range(F // FC):
            rows = pl.ds(c * FC, FC)
            a = _nt(h, wg[rows, :])
            b = _nt(h, wu[rows, :])
            ab_ref[:, c * FC:(c + 1) * FC] = a.astype(BF16)
            ab_ref[:, F + c * FC:F + (c + 1) * FC] = b.astype(BF16)
            s_scr[:, c * FC:(c + 1) * FC] = (a * jax.nn.sigmoid(a) * b).astype(BF16)
        out = xv + 0.5 * _nn(s_scr[...], wd[...])
        if head is None:
            o_ref[...] = out
        else:
            (gf_ref, t_ref), (dg_ref, loss_ref) = head_in, head_out
            gain_f = gf_ref[...]
            yn, r = _rms(out)
            err = yn * gain_f - t_ref[...]
            loss_ref[...] += 0.5 * jnp.sum(jnp.mean(err * err, axis=-1, keepdims=True), axis=0, keepdims=True)
            o_ref[...], dg = _rms_bwd(err * (1.0 / D), yn, r, gain_f)
            dg_ref[...] += dg

    tile = pl.BlockSpec((TM, D), lambda i: (i, 0))
    row = pl.BlockSpec((1, D), lambda i: (0, 0))
    head_specs = [] if head is None else [row, pl.BlockSpec((1, 128), lambda i: (0, 0))]
    head_shapes = [] if head is None else [jax.ShapeDtypeStruct((1, D), F32), jax.ShapeDtypeStruct((1, 128), F32)]
    res, got = _call(
        body, name=name, grid=(t // TM,),
        in_specs=[tile, row, ANY, ANY] + ([] if head is None else [row, tile]),
        out_specs=[tile, pl.BlockSpec((TM, 2 * F), lambda i: (i, 0)), tile] + head_specs,
        out_shape=[jax.ShapeDtypeStruct((t, D), F32), jax.ShapeDtypeStruct((t, 2 * F), BF16), jax.ShapeDtypeStruct((t, D), BF16)] + head_shapes,
        scratch_shapes=[pltpu.VMEM((F, D), BF16)] * 3 + [pltpu.VMEM((TM, F), BF16), pltpu.SemaphoreType.DMA((3, N_DEV))],
        args=(x, gain, gw_gu, gw_d) + (() if head is None else tuple(head)), comms=comms)
    return (res[0], res[1], res[2], got, *res[3:])


def _ffn_bwd_hidden(ab, dout, gw_d, off_d, name, comms=()):
    t = ab.shape[0]

    def tile(w):
        return pl.BlockSpec((TM, w), lambda i: (i, 0))

    def hidden(ab_ref, do_ref, d_ref, dab_ref, s_ref, df_ref, wd, sems):
        @pl.when(pl.program_id(0) == 0)
        def _():
            for cp in _load_rows(d_ref, wd, off_d, R_FF, sems):
                cp.wait()

        df = (0.5 * do_ref[...]).astype(BF16)
        df_ref[...] = df
        for c in range(F // FC):
            a = ab_ref[:, c * FC:(c + 1) * FC].astype(F32)
            b = ab_ref[:, F + c * FC:F + (c + 1) * FC].astype(F32)
            sg = jax.nn.sigmoid(a)
            sl = a * sg
            ds = _nt(df, wd[pl.ds(c * FC, FC), :])
            dab_ref[:, c * FC:(c + 1) * FC] = (ds * b * (sg * (1.0 + a * (1.0 - sg)))).astype(BF16)
            dab_ref[:, F + c * FC:F + (c + 1) * FC] = (ds * sl).astype(BF16)
            s_ref[:, c * FC:(c + 1) * FC] = (sl * b).astype(BF16)

    return _call(
        hidden, name=name, grid=(t // TM,),
        in_specs=[tile(2 * F), tile(D), ANY], out_specs=[tile(2 * F), tile(F), tile(D)],
        out_shape=[jax.ShapeDtypeStruct((t, 2 * F), BF16), jax.ShapeDtypeStruct((t, F), BF16), jax.ShapeDtypeStruct((t, D), BF16)],
        scratch_shapes=[pltpu.VMEM((F, D), BF16), pltpu.SemaphoreType.DMA((N_DEV,))],
        args=(ab, dout, gw_d), comms=comms)


def _ffn_bwd_input(x, gain, dab, dout, gw_gu, name, comms=()):
    t = x.shape[0]

    def body(x_ref, g_ref, dab_ref, do_ref, gu_ref, dx_ref, dg_ref, wg, wu, sems):
        @pl.when(pl.program_id(0) == 0)
        def _():
            cps = _load_rows(gu_ref, wg, 0, R_FF, sems.at[0]) + _load_rows(gu_ref, wu, R_FF, R_FF, sems.at[1])
            dg_ref[...] = jnp.zeros_like(dg_ref)
            for cp in cps:
                cp.wait()

        gain_v = g_ref[...]
        xn, r = _rms(x_ref[...])
        dh = _nn(dab_ref[:, :F], wg[...]) + _nn(dab_ref[:, F:], wu[...])
        dxn, dg = _rms_bwd(dh, xn, r, gain_v)
        dg_ref[...] += dg
        dx_ref[...] = do_ref[...] + dxn

    def tile(w):
        return pl.BlockSpec((TM, w), lambda i: (i, 0))

    row = pl.BlockSpec((1, D), lambda i: (0, 0))
    return _call(
        body, name=name, grid=(t // TM,),
        in_specs=[tile(D), row, tile(2 * F), tile(D), ANY], out_specs=[tile(D), row],
        out_shape=[jax.ShapeDtypeStruct((t, D), F32), jax.ShapeDtypeStruct((1, D), F32)],
        scratch_shapes=[pltpu.VMEM((F, D), BF16)] * 2 + [pltpu.SemaphoreType.DMA((2, N_DEV))],
        args=(x, gain, dab, dout, gw_gu), comms=comms)


def _weight_grad(a, b, name, col_off=0, m=None, comms=None, mats=1):
    t = a.shape[0]
    m = a.shape[1] if m is None else m
    n = b.shape[1]
    tmm = 512 if m % 512 == 0 else 256
    first = col_off // tmm

    def body(a_ref, b_ref, o_ref):
        o_ref[...] = _tn(a_ref[...], b_ref[...]).astype(BF16)

    (out,), got = _call(
        body, name=name, grid=(m // tmm,),
        in_specs=[pl.BlockSpec((t, tmm), lambda i: (0, first + i)), pl.BlockSpec((t, n), lambda i: (0, 0))],
        out_specs=[pl.BlockSpec((tmm, n), lambda i: (i, 0))],
        out_shape=[jax.ShapeDtypeStruct((m, n), BF16)], scratch_shapes=[], args=(a, b), comms=comms or ())
    out = out.reshape(mats * N_DEV, m // (mats * N_DEV), n)
    return out if comms is None else (out, got)


def _mix_proj_fwd(x, gain, b_gate, gw, comms=()):
    t = x.shape[0]

    def body(x_ref, g_ref, bg_ref, gw_ref, q_ref, k_ref, v_ref, zs_ref, gt_ref, h_ref, win, sems):
        @pl.when(pl.program_id(0) == 0)
        def _():
            for cp in _load_rows(gw_ref, win, OFF_IN, R_IN, sems):
                cp.wait()

        xn, _ = _rms(x_ref[...])
        h = (xn * g_ref[...]).astype(BF16)
        h_ref[...] = h
        q_ref[...] = (_nt(h, win[0:512, :]) * 0.125).astype(BF16)
        k_ref[...] = _nt(h, win[512:1024, :]).astype(BF16)
        v_ref[...] = _nt(h, win[1024:1536, :]).astype(BF16)
        for c in range(2):
            zs_ref[:, c * 512:(c + 1) * 512] = _nt(h, win[1536 + c * 512:2048 + c * 512, :]).astype(BF16)
        for c in range(4):
            zg = _nt(h, win[2560 + c * 512:3072 + c * 512, :]) + bg_ref[:, c * 512:(c + 1) * 512]
            gt_ref[:, c * 512:(c + 1) * 512] = jax.nn.sigmoid(zg).astype(BF16)

    def tile(w):
        return pl.BlockSpec((TM, w), lambda i: (i, 0))

    return _call(
        body, name="mix_proj_fwd", grid=(t // TM,),
        in_specs=[tile(D), pl.BlockSpec((1, D), lambda i: (0, 0)), pl.BlockSpec((1, 2 * D), lambda i: (0, 0)), ANY],
        out_specs=[tile(D_ATT), tile(D_ATT), tile(D_ATT), tile(2 * D_SGU), tile(2 * D), tile(D)],
        out_shape=[jax.ShapeDtypeStruct((t, D_ATT), BF16)] * 3 + [jax.ShapeDtypeStruct((t, 2 * D_SGU), BF16),
                                                                   jax.ShapeDtypeStruct((t, 2 * D), BF16),
                                                                   jax.ShapeDtypeStruct((t, D), BF16)],
        scratch_shapes=[pltpu.VMEM((D_IN, D), BF16), pltpu.SemaphoreType.DMA((N_DEV,))],
        args=(x, gain, b_gate, gw), comms=comms)


def _mix_proj_bwd(dz, x, gain, dres, gw, comms=()):
    t = x.shape[0]

    def body(dz_ref, x_ref, g_ref, dr_ref, gw_ref, dx_ref, dg_ref, win, sems):
        @pl.when(pl.program_id(0) == 0)
        def _():
            cps = _load_rows(gw_ref, win, OFF_IN, R_IN, sems)
            dg_ref[...] = jnp.zeros_like(dg_ref)
            for cp in cps:
                cp.wait()

        dh = _nn(dz_ref[...], win[...])
        xn, r = _rms(x_ref[...])
        dxn, dg = _rms_bwd(dh, xn, r, g_ref[...])
        dg_ref[...] += dg
        dx_ref[...] = dr_ref[...] + dxn

    def tile(w):
        return pl.BlockSpec((TM, w), lambda i: (i, 0))

    row = pl.BlockSpec((1, D), lambda i: (0, 0))
    return _call(
        body, name="mix_proj_bwd", grid=(t // TM,),
        in_specs=[tile(D_IN), tile(D), row, tile(D), ANY], out_specs=[tile(D), row],
        out_shape=[jax.ShapeDtypeStruct((t, D), F32), jax.ShapeDtypeStruct((1, D), F32)],
        scratch_shapes=[pltpu.VMEM((D_IN, D), BF16), pltpu.SemaphoreType.DMA((N_DEV,))],
        args=(dz, x, gain, dres, gw), comms=comms)


SKEW_W = KW + QB
N_CAP = 2 * QB - REL_CLIP + 1


def _band_bias(rel_bias):
    cap = rel_bias[:, 2 * REL_CLIP:]
    diag = jnp.concatenate([jnp.broadcast_to(cap, (HEADS, N_CAP)), rel_bias[:, 2 * REL_CLIP - 1::-1],
                            jnp.broadcast_to(cap, (HEADS, SKEW_W - N_CAP - 2 * REL_CLIP))], axis=1)

    def body(d_ref, o_ref):
        lag = lax.broadcasted_iota(jnp.int32, (QB, KW), 1) // CHUNK - lax.broadcasted_iota(jnp.int32, (QB, KW), 0) // CHUNK
        band = (lag >= 0) & (lag <= N_LEFT)
        for h in range(HEADS):
            rows = jnp.broadcast_to(d_ref[h:h + 1, :], (QB, SKEW_W))
            o_ref[h] = jnp.where(band, pltpu.roll(rows, 0, 1, stride=1, stride_axis=0)[:, :KW], NEG_INF)

    return pl.pallas_call(body, name="band_bias", out_shape=jax.ShapeDtypeStruct((HEADS, QB, KW), F32))(diag)


def _att_specs():
    qspec = pl.BlockSpec((QB, D_ATT), lambda g: (g, 0))
    kspecs = [pl.BlockSpec((QB, D_ATT), lambda g: (jnp.maximum(g - 2, 0), 0)),
              pl.BlockSpec((QB, D_ATT), lambda g: (jnp.maximum(g - 1, 0), 0)), qspec]
    bspec = pl.BlockSpec((HEADS, QB, KW), lambda g: (0, 0, 0))
    return qspec, kspecs, bspec


def _att_probs(qm, kp, bias, valid):
    s = jnp.where(valid, _nt(qm, kp) + bias, NEG_INF)
    e = jnp.exp(s - jnp.max(s, axis=-1, keepdims=True))
    return e / jnp.sum(e, axis=-1, keepdims=True)


def _att_valid():
    g = pl.program_id(0)
    blk = lax.broadcasted_iota(jnp.int32, (QB, KW), 1) // QB
    return (blk + g) >= 2


def _att_fwd(q, k, v, bias, comms=()):
    t = q.shape[0]

    def body(q_ref, k0, k1, k2, v0, v1, v2, b_ref, y_ref):
        valid = _att_valid()
        first = lax.broadcasted_iota(jnp.int32, (1, 128), 1) < 64
        for p in range(HEADS // 2):
            lanes = slice(p * 128, (p + 1) * 128)
            qp = q_ref[:, lanes]
            kp = jnp.concatenate([k0[:, lanes], k1[:, lanes], k2[:, lanes]], axis=0)
            vp = jnp.concatenate([v0[:, lanes], v1[:, lanes], v2[:, lanes]], axis=0)
            out = jnp.zeros((QB, 128), F32)
            for hh in range(2):
                mask = first if hh == 0 else jnp.logical_not(first)
                pr = _att_probs(jnp.where(mask, qp, 0), kp, b_ref[2 * p + hh], valid)
                out = out + _nn(pr.astype(BF16), jnp.where(mask, vp, 0))
            y_ref[:, lanes] = out.astype(BF16)

    qspec, kspecs, bspec = _att_specs()
    (out,), got = _call(
        body, name="att_fwd", grid=(t // QB,),
        in_specs=[qspec] + kspecs + kspecs + [bspec], out_specs=[qspec],
        out_shape=[jax.ShapeDtypeStruct((t, D_ATT), BF16)], scratch_shapes=[],
        args=(q, k, k, k, v, v, v, bias), comms=comms)
    return out, got


def _att_bwd(q, k, v, bias, dy, comms=()):
    t = q.shape[0]
    n_blocks = t // QB

    def body(q_ref, k0, k1, k2, v0, v1, v2, b_ref, dy_ref, dq_ref, dk_ref, dv_ref, db_ref, dk_acc, dv_acc):
        g = pl.program_id(0)

        @pl.when(g == 0)
        def _():
            db_ref[...] = jnp.zeros_like(db_ref)
            dk_acc[...] = jnp.zeros_like(dk_acc)
            dv_acc[...] = jnp.zeros_like(dv_acc)

        valid = _att_valid()
        first = lax.broadcasted_iota(jnp.int32, (1, 128), 1) < 64
        for p in range(HEADS // 2):
            lanes = slice(p * 128, (p + 1) * 128)
            qp = q_ref[:, lanes]
            dyp = dy_ref[:, lanes]
            kp = jnp.concatenate([k0[:, lanes], k1[:, lanes], k2[:, lanes]], axis=0)
            vp = jnp.concatenate([v0[:, lanes], v1[:, lanes], v2[:, lanes]], axis=0)
            dq = jnp.zeros((QB, 128), F32)
            dk = jnp.zeros((KW, 128), F32)
            dv = jnp.zeros((KW, 128), F32)
            for hh in range(2):
                mask = first if hh == 0 else jnp.logical_not(first)
                qm = jnp.where(mask, qp, 0)
                dym = jnp.where(mask, dyp, 0)
                pr = _att_probs(qm, kp, b_ref[2 * p + hh], valid)
                dp = _nt(dym, vp)
                ds = pr * (dp - jnp.sum(dp * pr, axis=-1, keepdims=True))
                db_ref[2 * p + hh] += ds
                dsb = ds.astype(BF16)
                dq = dq + _nn(dsb, jnp.where(mask, kp, 0))
                dk = dk + _tn(dsb, qm)
                dv = dv + _tn(pr.astype(BF16), dym)
            dq_ref[:, lanes] = (dq * 0.125).astype(BF16)
            for j in range(3):
                rows = pl.ds(pl.multiple_of(jnp.maximum(g - 2 + j, 0) * QB, QB), QB)
                dk_acc[rows, lanes] += dk[j * QB:(j + 1) * QB]
                dv_acc[rows, lanes] += dv[j * QB:(j + 1) * QB]

        @pl.when(g == n_blocks - 1)
        def _():
            dk_ref[...] = dk_acc[...].astype(BF16)
            dv_ref[...] = dv_acc[...].astype(BF16)

    qspec, kspecs, bspec = _att_specs()
    full = pl.BlockSpec((t, D_ATT), lambda g: (0, 0))
    return _call(
        body, name="att_bwd", grid=(n_blocks,),
        in_specs=[qspec] + kspecs + kspecs + [bspec, qspec], out_specs=[qspec, full, full, bspec],
        out_shape=[jax.ShapeDtypeStruct((t, D_ATT), BF16)] * 3 + [jax.ShapeDtypeStruct((HEADS, QB, KW), F32)],
        scratch_shapes=[pltpu.VMEM((t, D_ATT), F32)] * 2,
        args=(q, k, k, k, v, v, v, bias, dy), comms=comms)


def _rel_bias_grad(dbias):
    def body(db_ref, cs_ref, tot_ref):
        lane = lax.broadcasted_iota(jnp.int32, (1, SKEW_W), 1)
        capped = (lane < N_CAP) | (lane > KW)
        pad = jnp.zeros((8, QB), F32)
        for h in range(HEADS):
            z = jnp.concatenate([db_ref[h, 0:8, :], pad], axis=1)
            for a in range(1, QB // 8):
                z = z + pltpu.roll(jnp.concatenate([db_ref[h, 8 * a:8 * a + 8, :], pad], axis=1), SKEW_W - 8 * a, 1)
            cs = z[0:1, :]
            for b in range(1, 8):
                cs = cs + pltpu.roll(z[b:b + 1, :], SKEW_W - b, 1)
            cs_ref[h:h + 1, :] = cs
            tot_ref[h:h + 1, :] = jnp.broadcast_to(jnp.sum(jnp.where(capped, cs, 0.0), axis=1, keepdims=True), (1, 128))

    cs, tot = pl.pallas_call(
        body, name="rel_bias_grad",
        out_shape=[jax.ShapeDtypeStruct((HEADS, SKEW_W), F32), jax.ShapeDtypeStruct((HEADS, 128), F32)],
    )(dbias)
    return jnp.concatenate([cs[:, KW:N_CAP - 1:-1], tot[:, :1]], axis=1)


def _sgu_mask():
    pos = np.arange(SGU_BLOCK)
    return (pos[:, None] // CHUNK) >= (pos[None, :] // CHUNK)


def _group_stack(blk, first):
    return jnp.concatenate([jnp.where(first, blk, 0), jnp.where(first, 0, blk)], axis=0)


def _sgu_norm(zs_ref, lng, lnb):
    zs = zs_ref[...].astype(F32)
    ga, th = _gelu(zs)
    u, vs = ga[:, :D_SGU], ga[:, D_SGU:]
    mu = jnp.mean(vs, axis=-1, keepdims=True)
    cen = vs - mu
    rstd = lax.rsqrt(jnp.mean(cen * cen, axis=-1, keepdims=True) + EPS)
    xhat = cen * rstd
    return zs, th, u, xhat, rstd, xhat * lng + lnb


def _sgu_mix(vb, wm2_ref, bsx, s_ref):
    first = lax.broadcasted_iota(jnp.int32, (1, 128), 1) < 64
    for n in range(TM // SGU_BLOCK):
        for p in range(4):
            blk = vb[n * 128:(n + 1) * 128, p * 128:(p + 1) * 128]
            s_ref[n * 128:(n + 1) * 128, p * 128:(p + 1) * 128] = _nn(wm2_ref[p], _group_stack(blk, first)) + bsx[:, p * 128:(p + 1) * 128]


def _merge_fwd(x, zs, gt, y_att, lng, lnb, wm2, bsx, gw):
    t = x.shape[0]

    def body(x_ref, zs_ref, gt_ref, ya_ref, lng_ref, lnb_ref, wm2_ref, bsx_ref, gw_ref, xo_ref, ys_ref, mg_ref,
             wbr, wo, s_scr, sems):
        @pl.when(pl.program_id(0) == 0)
        def _():
            for cp in _load_rows(gw_ref, wbr, OFF_BR, R_BR, sems.at[0]) + _load_rows(gw_ref, wo, OFF_WO, R_WO, sems.at[1]):
                cp.wait()

        _, _, u, _, _, vsn = _sgu_norm(zs_ref, lng_ref[...], lnb_ref[...])
        _sgu_mix(vsn.astype(BF16), wm2_ref, bsx_ref[...], s_scr)
        ys = (u * s_scr[...]).astype(BF16)
        ys_ref[...] = ys
        pa = _nt(ya_ref[...], wbr[:, :D_ATT])
        ps = _nt(ys, wbr[:, D_ATT:])
        mg = (gt_ref[:, :D].astype(F32) * pa + gt_ref[:, D:].astype(F32) * ps).astype(BF16)
        mg_ref[...] = mg
        xo_ref[...] = x_ref[...] + _nn(mg, wo[...])

    def tile(w):
        return pl.BlockSpec((TM, w), lambda i: (i, 0))

    def const(shape):
        return pl.BlockSpec(shape, lambda i: (0,) * len(shape))

    return pl.pallas_call(
        body, name="merge_fwd", grid=(t // TM,),
        in_specs=[tile(D), tile(2 * D_SGU), tile(2 * D), tile(D_ATT), const((1, D_SGU)), const((1, D_SGU)),
                  const((4, 128, 256)), const((128, D_SGU)), ANY],
        out_specs=[tile(D), tile(D_SGU), tile(D)],
        out_shape=[jax.ShapeDtypeStruct((t, D), F32), jax.ShapeDtypeStruct((t, D_SGU), BF16), jax.ShapeDtypeStruct((t, D), BF16)],
        scratch_shapes=[pltpu.VMEM((D, D), BF16), pltpu.VMEM((D, D), BF16), pltpu.VMEM((TM, D_SGU), F32),
                        pltpu.SemaphoreType.DMA((2, N_DEV))],
        compiler_params=_cparams(("arbitrary",)),
    )(x, zs, gt, y_att, lng, lnb, wm2, bsx, gw)


def _merge_bwd(dx, gt, y_att, y_sgu, gw):
    t = dx.shape[0]

    def body(dx_ref, gt_ref, ya_ref, ys_ref, gw_ref, dzg_ref, dya_ref, dys_ref, dpp_ref, dxb_ref, dbg_ref, wbr, wo, sems):
        @pl.when(pl.program_id(0) == 0)
        def _():
            cps = _load_rows(gw_ref, wbr, OFF_BR, R_BR, sems.at[0]) + _load_rows(gw_ref, wo, OFF_WO, R_WO, sems.at[1])
            dbg_ref[...] = jnp.zeros_like(dbg_ref)
            for cp in cps:
                cp.wait()

        dxb = dx_ref[...].astype(BF16)
        dxb_ref[...] = dxb
        dm = _nt(dxb, wo[...])
        for half, y_ref, w in ((0, ya_ref, wbr.at[:, :D_ATT]), (1, ys_ref, wbr.at[:, D_ATT:])):
            cols = slice(half * D, (half + 1) * D)
            gate = gt_ref[:, cols].astype(F32)
            branch = _nt(y_ref[...], w[...])
            dzg = dm * branch * gate * (1.0 - gate)
            dbg_ref[:, cols] += jnp.sum(dzg, axis=0, keepdims=True)
            dzg_ref[:, cols] = dzg.astype(BF16)
            dbr = (dm * gate).astype(BF16)
            dpp_ref[:, cols] = dbr
            dy = _nn(dbr, w[...])
            if half == 0:
                dya_ref[...] = dy.astype(BF16)
            else:
                dys_ref[...] = dy

    def tile(w):
        return pl.BlockSpec((TM, w), lambda i: (i, 0))

    return pl.pallas_call(
        body, name="merge_bwd", grid=(t // TM,),
        in_specs=[tile(D), tile(2 * D), tile(D_ATT), tile(D_SGU), ANY],
        out_specs=[tile(2 * D), tile(D_ATT), tile(D_SGU), tile(2 * D), tile(D), pl.BlockSpec((1, 2 * D), lambda i: (0, 0))],
        out_shape=[jax.ShapeDtypeStruct((t, 2 * D), BF16), jax.ShapeDtypeStruct((t, D_ATT), BF16), jax.ShapeDtypeStruct((t, D_SGU), F32),
                   jax.ShapeDtypeStruct((t, 2 * D), BF16), jax.ShapeDtypeStruct((t, D), BF16), jax.ShapeDtypeStruct((1, 2 * D), F32)],
        scratch_shapes=[pltpu.VMEM((D, D), BF16), pltpu.VMEM((D, D), BF16), pltpu.SemaphoreType.DMA((2, N_DEV))],
        compiler_params=_cparams(("arbitrary",)),
    )(dx, gt, y_att, y_sgu, gw)


def _sgu_bwd(zs, dys, lng, lnb, wm2, wmt2, bsx, comms=()):
    t = zs.shape[0]
    n_steps = t // TM

    def body(zs_ref, dys_ref, lng_ref, lnb_ref, wm2_ref, wmt2_ref, bsx_ref, dzs_ref, dw_ref, dbs_ref, dlg_ref, dlb_ref,
             s_scr, dv_scr, ds_acc):
        i = pl.program_id(0)

        @pl.when(i == 0)
        def _():
            dw_ref[...] = jnp.zeros_like(dw_ref)
            dlg_ref[...] = jnp.zeros_like(dlg_ref)
            dlb_ref[...] = jnp.zeros_like(dlb_ref)
            ds_acc[...] = jnp.zeros_like(ds_acc)

        lng = lng_ref[...]
        zs, th, u, xhat, rstd, vsn = _sgu_norm(zs_ref, lng, lnb_ref[...])
        vb = vsn.astype(BF16)
        _sgu_mix(vb, wm2_ref, bsx_ref[...], s_scr)
        dys = dys_ref[...]
        du = dys * s_scr[...]
        ds = dys * u
        dsb = ds.astype(BF16)
        first = lax.broadcasted_iota(jnp.int32, (1, 128), 1) < 64
        acc = jnp.zeros((SGU_BLOCK, D_SGU), F32)
        for n in range(TM // SGU_BLOCK):
            rows = slice(n * 128, (n + 1) * 128)
            acc = acc + ds[rows]
            for p in range(4):
                lanes = slice(p * 128, (p + 1) * 128)
                stack = _group_stack(dsb[rows, lanes], first)
                dv_scr[rows, lanes] = _nn(wmt2_ref[p], stack)
                dw_ref[p] += _nt(stack, vb[rows, lanes])
        ds_acc[...] += acc
        dvsn = dv_scr[...]
        dlg_ref[...] += jnp.sum(dvsn * xhat, axis=0, keepdims=True)
        dlb_ref[...] += jnp.sum(dvsn, axis=0, keepdims=True)
        dxh = dvsn * lng
        dvs = rstd * (dxh - jnp.mean(dxh, axis=-1, keepdims=True) - xhat * jnp.mean(dxh * xhat, axis=-1, keepdims=True))
        dga = jnp.concatenate([du, dvs], axis=1)
        dzs_ref[...] = (dga * _gelu_grad(zs, th)).astype(BF16)

        @pl.when(i == n_steps - 1)
        def _():
            r = lax.broadcasted_iota(jnp.int32, (256, 128), 0) % SGU_BLOCK
            c = lax.broadcasted_iota(jnp.int32, (256, 128), 1)
            keep = (r // CHUNK) >= (c // CHUNK)
            for p in range(4):
                dw_ref[p] = jnp.where(keep, dw_ref[p], 0.0)
            total = ds_acc[...]
            grp = lax.broadcasted_iota(jnp.int32, (SGU_BLOCK, D_SGU), 1) // 64
            lane = lax.broadcasted_iota(jnp.int32, (SGU_BLOCK, 128), 1)
            out = jnp.zeros((SGU_BLOCK, 128), F32)
            for gi in range(8):
                out = jnp.where(lane == gi, jnp.sum(jnp.where(grp == gi, total, 0.0), axis=1, keepdims=True), out)
            dbs_ref[...] = out

    def tile(w):
        return pl.BlockSpec((TM, w), lambda i: (i, 0))

    def const(shape):
        return pl.BlockSpec(shape, lambda i: (0,) * len(shape))

    return _call(
        body, name="sgu_bwd", grid=(n_steps,),
        in_specs=[tile(2 * D_SGU), tile(D_SGU), const((1, D_SGU)), const((1, D_SGU)), const((4, 128, 256)), const((4, 128, 256)),
                  const((128, D_SGU))],
        out_specs=[tile(2 * D_SGU), const((4, 256, 128)), const((128, 128)), const((1, D_SGU)), const((1, D_SGU))],
        out_shape=[jax.ShapeDtypeStruct((t, 2 * D_SGU), BF16), jax.ShapeDtypeStruct((4, 256, 128), F32),
                   jax.ShapeDtypeStruct((128, 128), F32), jax.ShapeDtypeStruct((1, D_SGU), F32), jax.ShapeDtypeStruct((1, D_SGU), F32)],
        scratch_shapes=[pltpu.VMEM((TM, D_SGU), F32), pltpu.VMEM((TM, D_SGU), F32), pltpu.VMEM((SGU_BLOCK, D_SGU), F32)],
        args=(zs, dys, lng, lnb, wm2, wmt2, bsx), comms=comms)


def _adamw(g, w, m, v):
    m = ADAM_B1 * m + (1.0 - ADAM_B1) * g
    v = ADAM_B2 * v + (1.0 - ADAM_B2) * (g * g)
    m_hat = m / (1.0 - ADAM_B1 ** ADAM_STEP)
    v_hat = v / (1.0 - ADAM_B2 ** ADAM_STEP)
    return -ADAM_LR * (m_hat / (jnp.sqrt(v_hat) + ADAM_EPS) + ADAM_WD * w), m, v


def _adamw_matrix(parts, sums, w, m, v, transposed, name):
    _, r, c = parts.shape
    tc = 256

    def body(own_ref, p_ref, s_ref, w_ref, m_ref, v_ref, g_ref, d_ref, mo_ref, vo_ref):
        g = p_ref[0].astype(F32) + p_ref[1].astype(F32) + p_ref[2].astype(F32) + s_ref[...].astype(F32)
        g = g.T if transposed else g
        g_ref[...] = g
        d_ref[...], mo_ref[...], vo_ref[...] = _adamw(g, w_ref[...], m_ref[...], v_ref[...])

    own = pl.BlockSpec((None, tc, r), lambda i, o: (0, i, 0)) if transposed else pl.BlockSpec((None, r, tc), lambda i, o: (0, 0, i))
    return pl.pallas_call(
        body, name=name,
        grid_spec=pltpu.PrefetchScalarGridSpec(
            num_scalar_prefetch=1, grid=(c // tc,),
            in_specs=[pl.BlockSpec((3, r, tc), lambda i, o: (0, 0, i)), pl.BlockSpec((None, r, tc), lambda i, o: (o[0], 0, i)), own, own, own],
            out_specs=[own] * 4),
        out_shape=[jax.ShapeDtypeStruct(w.shape, F32)] * 4,
        compiler_params=_cparams(("arbitrary",)),
    )(_my_index(("x", 2), ("y", 1)), parts, sums, w, m, v)


_SMALL_2D = {"norm_ffn1": (1, D), "norm_mix": (1, D), "norm_ffn2": (1, D), "norm_final": (1, D), "b_gate": (1, 2 * D),
             "sgu_ln_g": (1, D_SGU), "sgu_ln_b": (1, D_SGU), "sgu_b_s": (8, SGU_BLOCK), "rel_bias": (HEADS, N_REL),
             "sgu_w_s": (8 * SGU_BLOCK, SGU_BLOCK)}


def _adamw_small(parts, loss_parts, p):
    names = list(parts)
    n = len(names)

    def body(*refs):
        got, loss_got, wmv, outs, loss_out = refs[:n], refs[n], refs[n + 1:4 * n + 1], refs[4 * n + 1:8 * n + 1], refs[8 * n + 1]
        for i, name in enumerate(names):
            g = got[i][0]
            for k in range(1, N_DEV):
                g = g + got[i][k]
            if name == "sgu_b_s":
                g = g.T[0:8, :]
            res = (g,) + _adamw(g, wmv[3 * i][...], wmv[3 * i + 1][...], wmv[3 * i + 2][...])
            for o_ref, val in zip(outs[4 * i:4 * i + 4], res):
                o_ref[...] = val
        total = loss_got[0]
        for k in range(1, N_DEV):
            total = total + loss_got[k]
        loss_out[...] = total

    wmv = [p[pre + name].reshape(_SMALL_2D[name]) for name in names for pre in ("", "m_", "v_")]
    res = pl.pallas_call(
        body, name="adamw_small",
        out_shape=[jax.ShapeDtypeStruct(_SMALL_2D[name], F32) for name in names for _ in range(4)] + [jax.ShapeDtypeStruct((1, 128), F32)],
        compiler_params=_cparams())(*[parts[name] for name in names], loss_parts, *wmv)
    return [{name: res[4 * i + j].reshape(p[name].shape) for i, name in enumerate(names)} for j in range(4)], res[-1]


def _pack_rows(groups, name):
    flat = [a for grp in groups for a, _ in grp]
    rows = [grp[0][0].shape[2] if grp[0][1] else grp[0][0].shape[1] for grp in groups]

    def body(*refs):
        o_ref, pos, off = refs[-1], 0, 0
        for grp, r in zip(groups, rows):
            vals = []
            for _, transposed in grp:
                val = refs[pos][0]
                vals.append(val.T if transposed else val)
                pos += 1
            o_ref[off:off + r, :] = (vals[0] if len(vals) == 1 else jnp.concatenate(vals, axis=1)).astype(BF16)
            off += r

    return pl.pallas_call(body, name=name, out_shape=jax.ShapeDtypeStruct((sum(rows), D), BF16), compiler_params=_cparams())(*flat)


def _step(x, target, p):
    n1, nm, n2 = p["norm_ffn1"], p["norm_mix"], p["norm_ffn2"]
    nf = p["norm_final"].reshape(1, D)
    lng, lnb = p["sgu_ln_g"], p["sgu_ln_b"]
    w_m = jnp.where(jnp.asarray(_sgu_mask())[None], p["sgu_w_s"][0], 0.0).astype(BF16)
    wm2 = jnp.concatenate([w_m[0::2], w_m[1::2]], axis=2)
    w_mt = w_m.transpose(0, 2, 1)
    wmt2 = jnp.concatenate([w_mt[0::2], w_mt[1::2]], axis=2)
    bsx = jnp.repeat(p["sgu_b_s"][0].T, 64, axis=1)
    bias = _band_bias(p["rel_bias"][0])

    def chip_sums(grads, name):
        gots = _comm_only([_SiblingSwap(grads)], "swap_" + name)
        return _pair_sums(grads, gots, "pair_sums_" + name)
    def as_rows(a):
        return jnp.swapaxes(a, 1, 2)

    def updates(parts, sums, names):
        res = {}
        for pt, sm, n in zip(parts, sums, names):
            if p[n].shape[1:] == pt.shape[1:]:
                res[n] = _adamw_matrix(pt, sm, p[n], p["m_" + n], p["v_" + n], False, "adamw_" + n)
            elif p[n].shape[2] > 128:
                res[n] = [as_rows(o) for o in _adamw_matrix(pt, sm, as_rows(p[n]), as_rows(p["m_" + n]), as_rows(p["v_" + n]), False,
                                                            "adamw_" + n)]
            else:
                res[n] = _adamw_matrix(pt, sm, p[n], p["m_" + n], p["v_" + n], True, "adamw_" + n)
        return res

    rows1 = _pack_rows([[(as_rows(p["ffn1_w_gate"]), False)], [(as_rows(p["ffn1_w_up"]), False)], [(p["ffn1_w_down"], False)]], "pack_ffn1")
    rows_m = _pack_rows([[(as_rows(p["w_in"]), False)], [(p["w_branch_att"], True), (p["w_branch_sgu"], True)], [(p["w_out"], False)]],
                        "pack_mixer")
    rows2d = _pack_rows([[(p["ffn2_w_down"], False)]], "pack_ffn2_down")
    rows2gu = _pack_rows([[(as_rows(p["ffn2_w_gate"]), False)], [(as_rows(p["ffn2_w_up"]), False)]], "pack_ffn2_gate_up")
    (gw1,) = _comm_only([_Gather(rows1)], "gather_ffn1")
    x1, ab1, h1, (gwm,) = _ffn_fwd(x, n1, gw1, gw1, 2 * R_FF, "ffn1_fwd", [_Gather(rows_m)])
    (q, k, v, zs, gt, h2), (gw2d,) = _mix_proj_fwd(x1, nm, p["b_gate"], gwm, [_Gather(rows2d)])
    y_att, (gw2gu,) = _att_fwd(q, k, v, bias, [_Gather(rows2gu)])
    x2, y_sgu, merged = _merge_fwd(x1, zs, gt, y_att, lng, lnb, wm2, bsx, gwm)
    dx3, ab2, hb, _, d_nf, loss = _ffn_fwd(x2, n2, gw2gu, gw2d, 0, "ffn2_fwd", head=(nf, target))

    (dab, sb, dfb), _ = _ffn_bwd_hidden(ab2, dx3, gw2d, 0, "ffn2_bwd_hidden")
    (dx2, d_n2), _ = _ffn_bwd_input(x2, n2, dab, dx3, gw2gu, "ffn2_bwd")
    g_gu = _weight_grad(dab, hb, "ffn2_dw_gate_up", mats=2)
    dzg, dya, dys, dpp, dxb, d_bg = _merge_bwd(dx2, gt, y_att, y_sgu, gwm)
    g_late = [(g_gu, 0), (g_gu, 1), (_weight_grad(sb, dfb, "ffn2_dw_down"), 0), (_weight_grad(dpp, y_att, "dw_branch_att", 0, D), 0),
              (_weight_grad(dpp, y_sgu, "dw_branch_sgu", D, D), 0), (_weight_grad(merged, dxb, "dw_out"), 0)]
    late = ("ffn2_w_gate", "ffn2_w_up", "ffn2_w_down", "w_branch_att", "w_branch_sgu", "w_out")
    (dzs, d_wm, d_bs, d_lng, d_lnb), gots_late = _sgu_bwd(zs, dys, lng, lnb, wm2, wmt2, bsx, [_SiblingSwap(g_late)])
    sums_late = _pair_sums(g_late, gots_late, "pair_sums_late")
    (dq, dk, dv, d_bias), parts_late = _att_bwd(q, k, v, bias, dya, [_ChipScatter(sums_late)])
    big = updates(parts_late, sums_late, late)
    d_rel = _rel_bias_grad(d_bias)
    dz = jnp.concatenate([dq, dk, dv, dzs, dzg], axis=1)
    sums_in = chip_sums([(_weight_grad(dz, h2, "dw_in"), 0)], "w_in")
    (dx1, d_nm), parts_in = _mix_proj_bwd(dz, x1, nm, dx2, gwm, [_ChipScatter(sums_in)])
    big.update(updates(parts_in, sums_in, ("w_in",)))
    small = {"norm_ffn2": d_n2, "norm_final": d_nf, "b_gate": d_bg, "sgu_ln_g": d_lng, "sgu_ln_b": d_lnb, "sgu_b_s": d_bs,
             "rel_bias": d_rel, "sgu_w_s": d_wm.reshape(_SMALL_2D["sgu_w_s"]), "norm_mix": d_nm}

    (dab, sb, dfb), (*small_parts, loss_parts) = _ffn_bwd_hidden(ab1, dx1, gw1, 2 * R_FF, "ffn1_bwd_hidden",
                                                                 [_AllToAll(list(small.values()) + [loss])])
    sums_d = chip_sums([(_weight_grad(sb, dfb, "ffn1_dw_down"), 0)], "ffn1_down")
    g_gu, (parts_d,) = _weight_grad(dab, h1, "ffn1_dw_gate_up", comms=[_ChipScatter(sums_d)], mats=2)
    sums_gu = chip_sums([(g_gu, 0), (g_gu, 1)], "ffn1_gate_up")
    (dx0, d_n1), parts_gu = _ffn_bwd_input(x, n1, dab, dx1, gw1, "ffn1_bwd", [_ChipScatter(sums_gu)])
    (n1_parts,) = _comm_only([_AllToAll([d_n1])], "gather_norm_ffn1")
    big.update(updates([parts_d] + parts_gu, sums_d + sums_gu, ("ffn1_w_down", "ffn1_w_gate", "ffn1_w_up")))
    out_s, loss_sum = _adamw_small(dict(zip(small, small_parts), norm_ffn1=n1_parts), loss_parts, p)
    return dx0, loss_sum[0, 0], [{**{n: four[i] for n, four in big.items()}, **s} for i, s in enumerate(out_s)]


_OUT_ORDER = ("norm_ffn1", "ffn1_w_gate", "ffn1_w_up", "ffn1_w_down", "norm_mix", "w_in", "b_gate", "rel_bias", "sgu_ln_g", "sgu_ln_b",
              "sgu_w_s", "sgu_b_s", "w_branch_att", "w_branch_sgu", "w_out", "norm_ffn2", "ffn2_w_gate", "ffn2_w_up", "ffn2_w_down",
              "norm_final")


def kernel(x, norm_ffn1, ffn1_w_gate, ffn1_w_up, ffn1_w_down, norm_mix, w_in, b_gate, rel_bias, sgu_ln_g, sgu_ln_b, sgu_w_s, sgu_b_s, w_branch_att, w_branch_sgu, w_out, norm_ffn2, ffn2_w_gate, ffn2_w_up, ffn2_w_down, norm_final, loss_target, m_norm_ffn1, m_ffn1_w_gate, m_ffn1_w_up, m_ffn1_w_down, m_norm_mix, m_w_in, m_b_gate, m_rel_bias, m_sgu_ln_g, m_sgu_ln_b, m_sgu_w_s, m_sgu_b_s, m_w_branch_att, m_w_branch_sgu, m_w_out, m_norm_ffn2, m_ffn2_w_gate, m_ffn2_w_up, m_ffn2_w_down, m_norm_final, v_norm_ffn1, v_ffn1_w_gate, v_ffn1_w_up, v_ffn1_w_down, v_norm_mix, v_w_in, v_b_gate, v_rel_bias, v_sgu_ln_g, v_sgu_ln_b, v_sgu_w_s, v_sgu_b_s, v_w_branch_att, v_w_branch_sgu, v_w_out, v_norm_ffn2, v_ffn2_w_gate, v_ffn2_w_up, v_ffn2_w_down, v_norm_final):
    args = dict(locals())
    dx, loss, outs = _step(x[0], loss_target[0], {pre + n: args[pre + n] for pre in ("", "m_", "v_") for n in _OUT_ORDER})
    return (loss, dx[None], *[o[n] for o in outs for n in _OUT_ORDER])
```

```python
import functools

import numpy as np
import jax
import jax.numpy as jnp
from jax import lax
from jax.experimental import pallas as pl
from jax.experimental.pallas import tpu as pltpu

F32 = jnp.float32
BF16 = jnp.bfloat16

N_DEV = 8
D = 1024
F = 2816
D_ATT = 512
D_SGU = 512
D_IN = 4608
HEADS = 8
CHUNK = 64
N_LEFT = 8
REL_CLIP = 256
N_REL = 2 * REL_CLIP + 1
SGU_BLOCK = 128
EPS = 1e-6
NEG_INF = -1e30
QB = 256
KW = 3 * QB

R_FF, R_IN, R_BR, R_WO = F // N_DEV, D_IN // N_DEV, D // N_DEV, D // N_DEV
OFF_IN, OFF_BR, OFF_WO = 0, R_IN, R_IN + R_BR

FC = 256
TM = 512
VMEM_LIMIT = 56 * 1024 * 1024

ADAM_LR, ADAM_B1, ADAM_B2, ADAM_EPS, ADAM_WD, ADAM_STEP = 0.001, 0.9, 0.999, 1e-08, 0.01, 10

MESH = pl.DeviceIdType.MESH
ANY = pl.BlockSpec(memory_space=pl.ANY)


def _nt(a, b):
    return lax.dot_general(a, b, (((1,), (1,)), ((), ())), preferred_element_type=F32)


def _tn(a, b):
    return lax.dot_general(a, b, (((0,), (0,)), ((), ())), preferred_element_type=F32)


def _nn(a, b):
    return jnp.dot(a, b, preferred_element_type=F32)


def _cparams(sem=None):
    return pltpu.CompilerParams(dimension_semantics=sem, vmem_limit_bytes=VMEM_LIMIT)


def _load_rows(gw_ref, dst, off, rows, sems):
    copies = [pltpu.make_async_copy(gw_ref.at[k, pl.ds(off, rows), :], dst.at[pl.ds(k * rows, rows), :], sems.at[k])
              for k in range(N_DEV)]
    for cp in copies:
        cp.start()
    return copies


def _rms(xv):
    r = lax.rsqrt(jnp.mean(xv * xv, axis=-1, keepdims=True) + EPS)
    return xv * r, r


def _rms_bwd(dh, xn, r, gain):
    dxn = dh * gain
    dx = r * (dxn - xn * jnp.mean(dxn * xn, axis=-1, keepdims=True))
    return dx, jnp.sum(dh * xn, axis=0, keepdims=True)


def _gelu(x):
    t = jnp.tanh(0.7978845608028654 * (x + 0.044715 * x * x * x))
    return 0.5 * x * (1.0 + t), t


def _gelu_grad(x, t):
    return 0.5 * (1.0 + t) + 0.5 * x * (1.0 - t * t) * 0.7978845608028654 * (1.0 + 3.0 * 0.044715 * x * x)


def _place():
    x, y, cc = lax.axis_index("x"), lax.axis_index("y"), lax.axis_index("c")
    return x, y, cc, [(1 - x, y), (x, 1 - y), (1 - x, 1 - y)]


class _Gather:
    def __init__(self, shard):
        self.inputs = [shard]
        self.out_shape = [jax.ShapeDtypeStruct((N_DEV,) + shard.shape, shard.dtype)]
        self.scratch = [pltpu.SemaphoreType.DMA((7,)), pltpu.SemaphoreType.DMA((7,)), pltpu.SemaphoreType.DMA]

    def _copies(self, ins, outs, scr):
        (x_ref,), (out_ref,), (send_sems, recv_sems, local_sem) = ins, outs, scr
        x, y, cc, chips = _place()

        def slab(px, py, pc):
            return out_ref.at[4 * px + 2 * py + pc]

        def copy(k, block, to, src=None):
            return pltpu.make_async_remote_copy(
                src_ref=slab(*block) if src is None else src, dst_ref=slab(*block),
                send_sem=send_sems.at[k], recv_sem=recv_sems.at[k], device_id=to, device_id_type=MESH)

        me, sibling = (x, y, cc), (x, y, 1 - cc)
        x_nbr, y_nbr, diagonal = chips
        mine = pltpu.make_async_copy(x_ref, slab(*me), local_sem)
        first = [copy(0, me, sibling, src=x_ref), copy(1, me, (*x_nbr, cc), src=x_ref), copy(2, me, (*y_nbr, cc), src=x_ref)]
        neighbours = [copy(1, (*x_nbr, cc), me), copy(2, (*y_nbr, cc), me)]
        second_hand = copy(3, (x ^ (1 - cc), y ^ cc, cc), (x ^ cc, y ^ (1 - cc), cc))
        from_diagonal = copy(3, (*diagonal, cc), me)
        passed = [copy(4 + j, (*chip, cc), sibling) for j, chip in enumerate(chips)]
        from_sibling = [copy(0, sibling, me)] + [copy(4 + j, (*chip, 1 - cc), me) for j, chip in enumerate(chips)]
        return mine, first, neighbours, second_hand, from_diagonal, passed, from_sibling

    def begin(self, *refs):
        mine, first = self._copies(*refs)[:2]
        mine.start()
        for cp in first:
            cp.start()

    def mid(self, *refs):
        _, _, neighbours, second_hand, _, passed, _ = self._copies(*refs)
        for cp in neighbours:
            cp.wait_recv()
        second_hand.start()
        passed[0].start()
        passed[1].start()

    def relay(self, *refs):
        _, _, _, _, from_diagonal, passed, _ = self._copies(*refs)
        from_diagonal.wait_recv()
        passed[2].start()

    def end(self, *refs):
        mine, first, _, second_hand, _, passed, from_sibling = self._copies(*refs)
        for cp in from_sibling:
            cp.wait_recv()
        for cp in first + [second_hand] + passed:
            cp.wait_send()
        mine.wait()


class _Direct:
    def begin(self, *refs):
        keep, give = self._copies(*refs)
        for cp in keep + give:
            cp.start()

    def mid(self, *refs):
        pass

    def relay(self, *refs):
        pass

    def end(self, *refs):
        keep, give = self._copies(*refs)
        for cp in give:
            cp.wait_recv()
        for cp in give:
            cp.wait_send()
        for cp in keep:
            cp.wait()


class _SiblingSwap(_Direct):
    def __init__(self, grads):
        n = len(grads)
        self.which = [w for _, w in grads]
        self.inputs = [g for g, _ in grads]
        self.out_shape = [jax.ShapeDtypeStruct((4,) + g.shape[1:], g.dtype) for g, _ in grads]
        self.scratch = [pltpu.SemaphoreType.DMA((n, 4)), pltpu.SemaphoreType.DMA((n, 4))]

    def _copies(self, ins, outs, scr):
        send_sems, recv_sems = scr
        x, y, cc, _ = _place()
        return [], [pltpu.make_async_remote_copy(src_ref=g_ref.at[N_DEV * w + 2 * j + 1 - cc], dst_ref=got_ref.at[j],
                                                 send_sem=send_sems.at[i, j], recv_sem=recv_sems.at[i, j], device_id=(x, y, 1 - cc),
                                                 device_id_type=MESH)
                    for i, (g_ref, got_ref, w) in enumerate(zip(ins, outs, self.which)) for j in range(4)]


class _ChipScatter(_Direct):
    def __init__(self, sums):
        n = len(sums)
        self.inputs = list(sums)
        self.out_shape = [jax.ShapeDtypeStruct((3,) + s.shape[1:], s.dtype) for s in sums]
        self.scratch = [pltpu.SemaphoreType.DMA((n, 3)), pltpu.SemaphoreType.DMA((n, 3))]

    def _copies(self, ins, outs, scr):
        send_sems, recv_sems = scr
        _, _, cc, chips = _place()
        return [], [pltpu.make_async_remote_copy(src_ref=s_ref.at[2 * px + py], dst_ref=got_ref.at[j], send_sem=send_sems.at[i, j],
                                                 recv_sem=recv_sems.at[i, j], device_id=(px, py, cc), device_id_type=MESH)
                    for i, (s_ref, got_ref) in enumerate(zip(ins, outs)) for j, (px, py) in enumerate(chips)]


class _AllToAll(_Direct):
    def __init__(self, blocks):
        n = len(blocks)
        self.inputs = list(blocks)
        self.out_shape = [jax.ShapeDtypeStruct((N_DEV,) + b.shape, b.dtype) for b in blocks]
        self.scratch = [pltpu.SemaphoreType.DMA((n, 7)), pltpu.SemaphoreType.DMA((n, 7)), pltpu.SemaphoreType.DMA((n,))]

    def _copies(self, ins, outs, scr):
        send_sems, recv_sems, local_sems = scr
        x, y, cc, _ = _place()
        me = 4 * x + 2 * y + cc
        keep = [pltpu.make_async_copy(b_ref, got_ref.at[me], local_sems.at[i]) for i, (b_ref, got_ref) in enumerate(zip(ins, outs))]
        give = [pltpu.make_async_remote_copy(src_ref=b_ref, dst_ref=got_ref.at[me], send_sem=send_sems.at[i, k - 1],
                                             recv_sem=recv_sems.at[i, k - 1],
                                             device_id=(x ^ ((k >> 2) & 1), y ^ ((k >> 1) & 1), cc ^ (k & 1)), device_id_type=MESH)
                for i, (b_ref, got_ref) in enumerate(zip(ins, outs)) for k in range(1, N_DEV)]
        return keep, give


def _split_refs(refs, counts):
    out, pos = [], 0
    for n in counts:
        out.append(list(refs[pos:pos + n]))
        pos += n
    return out


def _bind(comms, c_in, c_out, c_scr):
    ins = _split_refs(c_in, [len(c.inputs) for c in comms])
    outs = _split_refs(c_out, [len(c.out_shape) for c in comms])
    scr = _split_refs(c_scr, [len(c.scratch) for c in comms])
    return [(c, (i, o, s)) for c, i, o, s in zip(comms, ins, outs, scr)]


def _call(body, *, name, grid, in_specs, out_specs, out_shape, scratch_shapes, args, comms=()):
    c_in = [a for c in comms for a in c.inputs]
    c_out = [s for c in comms for s in c.out_shape]
    c_scr = [s for c in comms for s in c.scratch]
    counts = [len(in_specs), len(c_in), len(out_shape), len(c_out), len(scratch_shapes), len(c_scr)]

    def full(*refs):
        ins, cin, outs, cout, scr, cscr = _split_refs(refs, counts)
        bound = _bind(comms, cin, cout, cscr)
        if comms:
            def at(steps):
                return functools.reduce(jnp.logical_and, [pl.program_id(ax) == s for ax, s in enumerate(steps)])

            first, last = at([0] * len(grid)), at([n - 1 for n in grid])

            @pl.when(first)
            def _():
                for c, r in bound:
                    c.begin(*r)

            @pl.when(at([(grid[0] - 1) // 2] + [0] * (len(grid) - 1)))
            def _():
                for c, r in bound:
                    c.mid(*r)

            @pl.when(last)
            def _():
                for c, r in bound:
                    c.relay(*r)

        body(*ins, *outs, *scr)
        if comms:
            @pl.when(last)
            def _():
                for c, r in bound:
                    c.end(*r)

    res = pl.pallas_call(
        full, name=name, grid=grid,
        in_specs=list(in_specs) + [ANY] * len(c_in), out_specs=list(out_specs) + [ANY] * len(c_out),
        out_shape=list(out_shape) + c_out, scratch_shapes=list(scratch_shapes) + c_scr,
        compiler_params=_cparams(("arbitrary",) * len(grid)),
    )(*args, *c_in)
    return list(res[:len(out_shape)]), list(res[len(out_shape):])


def _comm_only(comms, name):
    c_in = [a for c in comms for a in c.inputs]
    c_out = [s for c in comms for s in c.out_shape]
    c_scr = [s for c in comms for s in c.scratch]

    def full(*refs):
        cin, cout, cscr = _split_refs(refs, [len(c_in), len(c_out), len(c_scr)])
        bound = _bind(comms, cin, cout, cscr)
        for phase in ("begin", "mid", "relay", "end"):
            for c, r in bound:
                getattr(c, phase)(*r)

    return list(pl.pallas_call(full, name=name, in_specs=[ANY] * len(c_in), out_specs=[ANY] * len(c_out), out_shape=c_out,
                               scratch_shapes=c_scr)(*c_in))


def _my_index(*axes_and_weights):
    return sum(w * lax.axis_index(a) for a, w in axes_and_weights).astype(jnp.int32).reshape(1)


def _pair_sums(grads, gots, name):
    n = len(grads)

    def body(c_ref, *refs):
        for a_ref, b_ref, o_ref in zip(refs[:n], refs[n:2 * n], refs[2 * n:]):
            o_ref[...] = (a_ref[...].astype(F32) + b_ref[...].astype(F32)).astype(BF16)

    def tile(g):
        return pl.BlockSpec((1,) + g.shape[1:], lambda j, c_ref: (j, 0, 0))

    def mine(g, w):
        return pl.BlockSpec((1, None) + g.shape[1:], lambda j, c_ref: (4 * w + j, c_ref[0], 0, 0))

    return list(pl.pallas_call(
        body, name=name,
        grid_spec=pltpu.PrefetchScalarGridSpec(num_scalar_prefetch=1, grid=(4,),
                                               in_specs=[mine(g, w) for g, w in grads] + [tile(g) for g in gots],
                                               out_specs=[tile(g) for g in gots]),
        out_shape=[jax.ShapeDtypeStruct(g.shape, BF16) for g in gots],
        compiler_params=_cparams(("arbitrary",)))(_my_index(("c", 1)), *[g.reshape((-1, 2) + g.shape[1:]) for g, _ in grads], *gots))


def _ffn_fwd(x, gain, gw_gu, gw_d, off_d, name, comms=(), head=None):
    t = x.shape[0]
    n_head = 0 if head is None else 2

    def body(x_ref, g_ref, gu_ref, d_ref, *refs):
        head_in, (o_ref, ab_ref, h_ref), head_out = refs[:n_head], refs[n_head:n_head + 3], refs[n_head + 3:2 * n_head + 3]
        wg, wu, wd, s_scr, sems = refs[2 * n_head + 3:]

        @pl.when(pl.program_id(0) == 0)
        def _():
            cps = _load_rows(gu_ref, wg, 0, R_FF, sems.at[0]) + _load_rows(gu_ref, wu, R_FF, R_FF, sems.at[1]) \
                + _load_rows(d_ref, wd, off_d, R_FF, sems.at[2])
            for o in head_out:
                o[...] = jnp.zeros_like(o)
            for cp in cps:
                cp.wait()

        xv = x_ref[...]
        xn, _ = _rms(xv)
        h = (xn * g_ref[...]).astype(BF16)
        h_ref[...] = h
        for c in range(F // FC):
            rows = pl.ds(c * FC, FC)
            a = _nt(h, wg[rows, :])
            b = _nt(h, wu[rows, :])
            ab_ref[:, c * FC:(c + 1) * FC] = a.astype(BF16)
            ab_ref[:, F + c * FC:F + (c + 1) * FC] = b.astype(BF16)
            s_scr[:, c * FC:(c + 1) * FC] = (a * jax.nn.sigmoid(a) * b).astype(BF16)
        out = xv + 0.5 * _nn(s_scr[...], wd[...])
        if head is None:
            o_ref[...] = out
        else:
            (gf_ref, t_ref), (dg_ref, loss_ref) = head_in, head_out
            gain_f = gf_ref[...]
            yn, r = _rms(out)
            err = yn * gain_f - t_ref[...]
            loss_ref[...] += 0.5 * jnp.sum(jnp.mean(err * err, axis=-1, keepdims=True), axis=0, keepdims=True)
            o_ref[...], dg = _rms_bwd(err * (1.0 / D), yn, r, gain_f)
            dg_ref[...] += dg

    tile = pl.BlockSpec((TM, D), lambda i: (i, 0))
    row = pl.BlockSpec((1, D), lambda i: (0, 0))
    head_specs = [] if head is None else [row, pl.BlockSpec((1, 128), lambda i: (0, 0))]
    head_shapes = [] if head is None else [jax.ShapeDtypeStruct((1, D), F32), jax.ShapeDtypeStruct((1, 128), F32)]
    res, got = _call(
        body, name=name, grid=(t // TM,),
        in_specs=[tile, row, ANY, ANY] + ([] if head is None else [row, tile]),
        out_specs=[tile, pl.BlockSpec((TM, 2 * F), lambda i: (i, 0)), tile] + head_specs,
        out_shape=[jax.ShapeDtypeStruct((t, D), F32), jax.ShapeDtypeStruct((t, 2 * F), BF16), jax.ShapeDtypeStruct((t, D), BF16)] + head_shapes,
        scratch_shapes=[pltpu.VMEM((F, D), BF16)] * 3 + [pltpu.VMEM((TM, F), BF16), pltpu.SemaphoreType.DMA((3, N_DEV))],
        args=(x, gain, gw_gu, gw_d) + (() if head is None else tuple(head)), comms=comms)
    return (res[0], res[1], res[2], got, *res[3:])


def _ffn_bwd_hidden(ab, dout, gw_d, off_d, name, comms=()):
    t = ab.shape[0]

    def tile(w):
        return pl.BlockSpec((TM, w), lambda i: (i, 0))

    def hidden(ab_ref, do_ref, d_ref, dab_ref, s_ref, df_ref, wd, sems):
        @pl.when(pl.program_id(0) == 0)
        def _():
            for cp in _load_rows(d_ref, wd, off_d, R_FF, sems):
                cp.wait()

        df = (0.5 * do_ref[...]).astype(BF16)
        df_ref[...] = df
        for c in range(F // FC):
            a = ab_ref[:, c * FC:(c + 1) * FC].astype(F32)
            b = ab_ref[:, F + c * FC:F + (c + 1) * FC].astype(F32)
            sg = jax.nn.sigmoid(a)
            sl = a * sg
            ds = _nt(df, wd[pl.ds(c * FC, FC), :])
            dab_ref[:, c * FC:(c + 1) * FC] = (ds * b * (sg * (1.0 + a * (1.0 - sg)))).astype(BF16)
            dab_ref[:, F + c * FC:F + (c + 1) * FC] = (ds * sl).astype(BF16)
            s_ref[:, c * FC:(c + 1) * FC] = (sl * b).astype(BF16)

    return _call(
        hidden, name=name, grid=(t // TM,),
        in_specs=[tile(2 * F), tile(D), ANY], out_specs=[tile(2 * F), tile(F), tile(D)],
        out_shape=[jax.ShapeDtypeStruct((t, 2 * F), BF16), jax.ShapeDtypeStruct((t, F), BF16), jax.ShapeDtypeStruct((t, D), BF16)],
        scratch_shapes=[pltpu.VMEM((F, D), BF16), pltpu.SemaphoreType.DMA((N_DEV,))],
        args=(ab, dout, gw_d), comms=comms)


def _ffn_bwd_input(x, gain, dab, dout, gw_gu, name, comms=()):
    t = x.shape[0]

    def body(x_ref, g_ref, dab_ref, do_ref, gu_ref, dx_ref, dg_ref, wgu, sems):
        @pl.when(pl.program_id(0) == 0)
        def _():
            cps = _load_rows(gu_ref, wgu.at[0:F], 0, R_FF, sems.at[0]) + _load_rows(gu_ref, wgu.at[F:2 * F], R_FF, R_FF, sems.at[1])
            dg_ref[...] = jnp.zeros_like(dg_ref)
            for cp in cps:
                cp.wait()

        gain_v = g_ref[...]
        xn, r = _rms(x_ref[...])
        dh = _nn(dab_ref[...], wgu[...])
        dxn, dg = _rms_bwd(dh, xn, r, gain_v)
        dg_ref[...] += dg
        dx_ref[...] = do_ref[...] + dxn

    def tile(w):
        return pl.BlockSpec((TM, w), lambda i: (i, 0))

    row = pl.BlockSpec((1, D), lambda i: (0, 0))
    return _call(
        body, name=name, grid=(t // TM,),
        in_specs=[tile(D), row, tile(2 * F), tile(D), ANY], out_specs=[tile(D), row],
        out_shape=[jax.ShapeDtypeStruct((t, D), F32), jax.ShapeDtypeStruct((1, D), F32)],
        scratch_shapes=[pltpu.VMEM((2 * F, D), BF16), pltpu.SemaphoreType.DMA((2, N_DEV))],
        args=(x, gain, dab, dout, gw_gu), comms=comms)


def _weight_grad(a, b, name, col_off=0, m=None, comms=None, mats=1):
    t = a.shape[0]
    m = a.shape[1] if m is None else m
    n = b.shape[1]
    tmm = 512 if m % 512 == 0 else 256
    first = col_off // tmm

    def body(a_ref, b_ref, o_ref):
        o_ref[...] = _tn(a_ref[...], b_ref[...]).astype(BF16)

    (out,), got = _call(
        body, name=name, grid=(m // tmm,),
        in_specs=[pl.BlockSpec((t, tmm), lambda i: (0, first + i)), pl.BlockSpec((t, n), lambda i: (0, 0))],
        out_specs=[pl.BlockSpec((tmm, n), lambda i: (i, 0))],
        out_shape=[jax.ShapeDtypeStruct((m, n), BF16)], scratch_shapes=[], args=(a, b), comms=comms or ())
    out = out.reshape(mats * N_DEV, m // (mats * N_DEV), n)
    return out if comms is None else (out, got)


def _mix_proj_fwd(x, gain, b_gate, gw, comms=()):
    t = x.shape[0]

    def body(x_ref, g_ref, bg_ref, gw_ref, q_ref, k_ref, v_ref, zs_ref, gt_ref, h_ref, win, sems):
        @pl.when(pl.program_id(0) == 0)
        def _():
            for cp in _load_rows(gw_ref, win, OFF_IN, R_IN, sems):
                cp.wait()

        xn, _ = _rms(x_ref[...])
        h = (xn * g_ref[...]).astype(BF16)
        h_ref[...] = h
        z = _nt(h, win[...])
        q_ref[...] = (z[:, 0:D_ATT] * 0.125).astype(BF16)
        k_ref[...] = z[:, D_ATT:2 * D_ATT].astype(BF16)
        v_ref[...] = z[:, 2 * D_ATT:3 * D_ATT].astype(BF16)
        zs_ref[...] = z[:, 3 * D_ATT:3 * D_ATT + 2 * D_SGU].astype(BF16)
        gt_ref[...] = jax.nn.sigmoid(z[:, 3 * D_ATT + 2 * D_SGU:] + bg_ref[...]).astype(BF16)

    def tile(w):
        return pl.BlockSpec((TM, w), lambda i: (i, 0))

    return _call(
        body, name="mix_proj_fwd", grid=(t // TM,),
        in_specs=[tile(D), pl.BlockSpec((1, D), lambda i: (0, 0)), pl.BlockSpec((1, 2 * D), lambda i: (0, 0)), ANY],
        out_specs=[tile(D_ATT), tile(D_ATT), tile(D_ATT), tile(2 * D_SGU), tile(2 * D), tile(D)],
        out_shape=[jax.ShapeDtypeStruct((t, D_ATT), BF16)] * 3 + [jax.ShapeDtypeStruct((t, 2 * D_SGU), BF16),
                                                                   jax.ShapeDtypeStruct((t, 2 * D), BF16),
                                                                   jax.ShapeDtypeStruct((t, D), BF16)],
        scratch_shapes=[pltpu.VMEM((D_IN, D), BF16), pltpu.SemaphoreType.DMA((N_DEV,))],
        args=(x, gain, b_gate, gw), comms=comms)


def _mix_proj_bwd(dz, x, gain, dres, gw, comms=()):
    t = x.shape[0]

    def body(dz_ref, x_ref, g_ref, dr_ref, gw_ref, dx_ref, dg_ref, win, sems):
        @pl.when(pl.program_id(0) == 0)
        def _():
            cps = _load_rows(gw_ref, win, OFF_IN, R_IN, sems)
            dg_ref[...] = jnp.zeros_like(dg_ref)
            for cp in cps:
                cp.wait()

        dh = _nn(dz_ref[...], win[...])
        xn, r = _rms(x_ref[...])
        dxn, dg = _rms_bwd(dh, xn, r, g_ref[...])
        dg_ref[...] += dg
        dx_ref[...] = dr_ref[...] + dxn

    def tile(w):
        return pl.BlockSpec((TM, w), lambda i: (i, 0))

    row = pl.BlockSpec((1, D), lambda i: (0, 0))
    return _call(
        body, name="mix_proj_bwd", grid=(t // TM,),
        in_specs=[tile(D_IN), tile(D), row, tile(D), ANY], out_specs=[tile(D), row],
        out_shape=[jax.ShapeDtypeStruct((t, D), F32), jax.ShapeDtypeStruct((1, D), F32)],
        scratch_shapes=[pltpu.VMEM((D_IN, D), BF16), pltpu.SemaphoreType.DMA((N_DEV,))],
        args=(dz, x, gain, dres, gw), comms=comms)


SKEW_W = KW + QB
N_CAP = 2 * QB - REL_CLIP + 1


def _band_bias(rel_bias):
    cap = rel_bias[:, 2 * REL_CLIP:]
    diag = jnp.concatenate([jnp.broadcast_to(cap, (HEADS, N_CAP)), rel_bias[:, 2 * REL_CLIP - 1::-1],
                            jnp.broadcast_to(cap, (HEADS, SKEW_W - N_CAP - 2 * REL_CLIP))], axis=1)

    def body(d_ref, o_ref):
        lag = lax.broadcasted_iota(jnp.int32, (QB, KW), 1) // CHUNK - lax.broadcasted_iota(jnp.int32, (QB, KW), 0) // CHUNK
        band = (lag >= 0) & (lag <= N_LEFT)
        for h in range(HEADS):
            rows = jnp.broadcast_to(d_ref[h:h + 1, :], (QB, SKEW_W))
            o_ref[h] = jnp.where(band, pltpu.roll(rows, 0, 1, stride=1, stride_axis=0)[:, :KW], NEG_INF)

    return pl.pallas_call(body, name="band_bias", out_shape=jax.ShapeDtypeStruct((HEADS, QB, KW), F32))(diag)


def _att_specs():
    qspec = pl.BlockSpec((QB, D_ATT), lambda g: (g, 0))
    kspecs = [pl.BlockSpec((QB, D_ATT), lambda g: (jnp.maximum(g - 2, 0), 0)),
              pl.BlockSpec((QB, D_ATT), lambda g: (jnp.maximum(g - 1, 0), 0)), qspec]
    bspec = pl.BlockSpec((HEADS, QB, KW), lambda g: (0, 0, 0))
    return qspec, kspecs, bspec


def _att_probs(qm, kp, bias, valid):
    s = jnp.where(valid, _nt(qm, kp) + bias, NEG_INF)
    e = jnp.exp(s - jnp.max(s, axis=-1, keepdims=True))
    return e / jnp.sum(e, axis=-1, keepdims=True)


def _att_valid():
    g = pl.program_id(0)
    blk = lax.broadcasted_iota(jnp.int32, (QB, KW), 1) // QB
    return (blk + g) >= 2


def _att_fwd(q, k, v, bias, comms=()):
    t = q.shape[0]

    def body(q_ref, k0, k1, k2, v0, v1, v2, b_ref, y_ref):
        valid = _att_valid()
        first = lax.broadcasted_iota(jnp.int32, (1, 128), 1) < 64
        for p in range(HEADS // 2):
            lanes = slice(p * 128, (p + 1) * 128)
            qp = q_ref[:, lanes]
            kp = jnp.concatenate([k0[:, lanes], k1[:, lanes], k2[:, lanes]], axis=0)
            vp = jnp.concatenate([v0[:, lanes], v1[:, lanes], v2[:, lanes]], axis=0)
            out = jnp.zeros((QB, 128), F32)
            for hh in range(2):
                mask = first if hh == 0 else jnp.logical_not(first)
                pr = _att_probs(jnp.where(mask, qp, 0), kp, b_ref[2 * p + hh], valid)
                out = out + _nn(pr.astype(BF16), jnp.where(mask, vp, 0))
            y_ref[:, lanes] = out.astype(BF16)

    qspec, kspecs, bspec = _att_specs()
    (out,), got = _call(
        body, name="att_fwd", grid=(t // QB,),
        in_specs=[qspec] + kspecs + kspecs + [bspec], out_specs=[qspec],
        out_shape=[jax.ShapeDtypeStruct((t, D_ATT), BF16)], scratch_shapes=[],
        args=(q, k, k, k, v, v, v, bias), comms=comms)
    return out, got


def _att_bwd(q, k, v, bias, dy, comms=()):
    t = q.shape[0]
    n_blocks = t // QB

    def body(q_ref, k0, k1, k2, v0, v1, v2, b_ref, dy_ref, dq_ref, dk_ref, dv_ref, db_ref, dk_acc, dv_acc):
        g = pl.program_id(0)

        @pl.when(g == 0)
        def _():
            db_ref[...] = jnp.zeros_like(db_ref)
            dk_acc[...] = jnp.zeros_like(dk_acc)
            dv_acc[...] = jnp.zeros_like(dv_acc)

        valid = _att_valid()
        first = lax.broadcasted_iota(jnp.int32, (1, 128), 1) < 64
        for p in range(HEADS // 2):
            lanes = slice(p * 128, (p + 1) * 128)
            qp = q_ref[:, lanes]
            dyp = dy_ref[:, lanes]
            kp = jnp.concatenate([k0[:, lanes], k1[:, lanes], k2[:, lanes]], axis=0)
            vp = jnp.concatenate([v0[:, lanes], v1[:, lanes], v2[:, lanes]], axis=0)
            dq = jnp.zeros((QB, 128), F32)
            dk = jnp.zeros((KW, 128), F32)
            dv = jnp.zeros((KW, 128), F32)
            for hh in range(2):
                mask = first if hh == 0 else jnp.logical_not(first)
                qm = jnp.where(mask, qp, 0)
                dym = jnp.where(mask, dyp, 0)
                pr = _att_probs(qm, kp, b_ref[2 * p + hh], valid)
                dp = _nt(dym, vp)
                ds = pr * (dp - jnp.sum(dp * pr, axis=-1, keepdims=True))
                db_ref[2 * p + hh] += ds
                dsb = ds.astype(BF16)
                dq = dq + _nn(dsb, jnp.where(mask, kp, 0))
                dk = dk + _tn(dsb, qm)
                dv = dv + _tn(pr.astype(BF16), dym)
            dq_ref[:, lanes] = (dq * 0.125).astype(BF16)
            for j in range(3):
                rows = pl.ds(pl.multiple_of(jnp.maximum(g - 2 + j, 0) * QB, QB), QB)
                dk_acc[rows, lanes] += dk[j * QB:(j + 1) * QB]
                dv_acc[rows, lanes] += dv[j * QB:(j + 1) * QB]

        @pl.when(g == n_blocks - 1)
        def _():
            dk_ref[...] = dk_acc[...].astype(BF16)
            dv_ref[...] = dv_acc[...].astype(BF16)

    qspec, kspecs, bspec = _att_specs()
    full = pl.BlockSpec((t, D_ATT), lambda g: (0, 0))
    return _call(
        body, name="att_bwd", grid=(n_blocks,),
        in_specs=[qspec] + kspecs + kspecs + [bspec, qspec], out_specs=[qspec, full, full, bspec],
        out_shape=[jax.ShapeDtypeStruct((t, D_ATT), BF16)] * 3 + [jax.ShapeDtypeStruct((HEADS, QB, KW), F32)],
        scratch_shapes=[pltpu.VMEM((t, D_ATT), F32)] * 2,
        args=(q, k, k, k, v, v, v, bias, dy), comms=comms)


def _rel_bias_grad(dbias):
    def body(db_ref, cs_ref, tot_ref):
        lane = lax.broadcasted_iota(jnp.int32, (1, SKEW_W), 1)
        capped = (lane < N_CAP) | (lane > KW)
        pad = jnp.zeros((8, QB), F32)
        for h in range(HEADS):
            z = jnp.concatenate([db_ref[h, 0:8, :], pad], axis=1)
            for a in range(1, QB // 8):
                z = z + pltpu.roll(jnp.concatenate([db_ref[h, 8 * a:8 * a + 8, :], pad], axis=1), SKEW_W - 8 * a, 1)
            cs = z[0:1, :]
            for b in range(1, 8):
                cs = cs + pltpu.roll(z[b:b + 1, :], SKEW_W - b, 1)
            cs_ref[h:h + 1, :] = cs
            tot_ref[h:h + 1, :] = jnp.broadcast_to(jnp.sum(jnp.where(capped, cs, 0.0), axis=1, keepdims=True), (1, 128))

    cs, tot = pl.pallas_call(
        body, name="rel_bias_grad",
        out_shape=[jax.ShapeDtypeStruct((HEADS, SKEW_W), F32), jax.ShapeDtypeStruct((HEADS, 128), F32)],
    )(dbias)
    return jnp.concatenate([cs[:, KW:N_CAP - 1:-1], tot[:, :1]], axis=1)


def _sgu_mask():
    pos = np.arange(SGU_BLOCK)
    return (pos[:, None] // CHUNK) >= (pos[None, :] // CHUNK)


def _group_stack(blk, first):
    return jnp.concatenate([jnp.where(first, blk, 0), jnp.where(first, 0, blk)], axis=0)


def _sgu_norm(zs_ref, lng, lnb):
    zs = zs_ref[...].astype(F32)
    ga, th = _gelu(zs)
    u, vs = ga[:, :D_SGU], ga[:, D_SGU:]
    mu = jnp.mean(vs, axis=-1, keepdims=True)
    cen = vs - mu
    rstd = lax.rsqrt(jnp.mean(cen * cen, axis=-1, keepdims=True) + EPS)
    xhat = cen * rstd
    return zs, th, u, xhat, rstd, xhat * lng + lnb


def _sgu_mix(vb, wm2_ref, bsx, s_ref):
    first = lax.broadcasted_iota(jnp.int32, (1, 128), 1) < 64
    for n in range(TM // SGU_BLOCK):
        for p in range(4):
            blk = vb[n * 128:(n + 1) * 128, p * 128:(p + 1) * 128]
            s_ref[n * 128:(n + 1) * 128, p * 128:(p + 1) * 128] = _nn(wm2_ref[p], _group_stack(blk, first)) + bsx[:, p * 128:(p + 1) * 128]


def _merge_fwd(x, zs, gt, y_att, lng, lnb, wm2, bsx, gw):
    t = x.shape[0]

    def body(x_ref, zs_ref, gt_ref, ya_ref, lng_ref, lnb_ref, wm2_ref, bsx_ref, gw_ref, xo_ref, ys_ref, mg_ref,
             wbr, wo, s_scr, sems):
        @pl.when(pl.program_id(0) == 0)
        def _():
            for cp in _load_rows(gw_ref, wbr, OFF_BR, R_BR, sems.at[0]) + _load_rows(gw_ref, wo, OFF_WO, R_WO, sems.at[1]):
                cp.wait()

        _, _, u, _, _, vsn = _sgu_norm(zs_ref, lng_ref[...], lnb_ref[...])
        _sgu_mix(vsn.astype(BF16), wm2_ref, bsx_ref[...], s_scr)
        ys = (u * s_scr[...]).astype(BF16)
        ys_ref[...] = ys
        pa = _nt(ya_ref[...], wbr[:, :D_ATT])
        ps = _nt(ys, wbr[:, D_ATT:])
        mg = (gt_ref[:, :D].astype(F32) * pa + gt_ref[:, D:].astype(F32) * ps).astype(BF16)
        mg_ref[...] = mg
        xo_ref[...] = x_ref[...] + _nn(mg, wo[...])

    def tile(w):
        return pl.BlockSpec((TM, w), lambda i: (i, 0))

    def const(shape):
        return pl.BlockSpec(shape, lambda i: (0,) * len(shape))

    return pl.pallas_call(
        body, name="merge_fwd", grid=(t // TM,),
        in_specs=[tile(D), tile(2 * D_SGU), tile(2 * D), tile(D_ATT), const((1, D_SGU)), const((1, D_SGU)),
                  const((4, 128, 256)), const((128, D_SGU)), ANY],
        out_specs=[tile(D), tile(D_SGU), tile(D)],
        out_shape=[jax.ShapeDtypeStruct((t, D), F32), jax.ShapeDtypeStruct((t, D_SGU), BF16), jax.ShapeDtypeStruct((t, D), BF16)],
        scratch_shapes=[pltpu.VMEM((D, D), BF16), pltpu.VMEM((D, D), BF16), pltpu.VMEM((TM, D_SGU), F32),
                        pltpu.SemaphoreType.DMA((2, N_DEV))],
        compiler_params=_cparams(("arbitrary",)),
    )(x, zs, gt, y_att, lng, lnb, wm2, bsx, gw)


def _merge_bwd(dx, gt, y_att, y_sgu, gw):
    t = dx.shape[0]

    def body(dx_ref, gt_ref, ya_ref, ys_ref, gw_ref, dzg_ref, dya_ref, dys_ref, dpp_ref, dxb_ref, dbg_ref, wbr, wo, sems):
        @pl.when(pl.program_id(0) == 0)
        def _():
            cps = _load_rows(gw_ref, wbr, OFF_BR, R_BR, sems.at[0]) + _load_rows(gw_ref, wo, OFF_WO, R_WO, sems.at[1])
            dbg_ref[...] = jnp.zeros_like(dbg_ref)
            for cp in cps:
                cp.wait()

        dxb = dx_ref[...].astype(BF16)
        dxb_ref[...] = dxb
        dm = _nt(dxb, wo[...])
        for half, y_ref, w in ((0, ya_ref, wbr.at[:, :D_ATT]), (1, ys_ref, wbr.at[:, D_ATT:])):
            cols = slice(half * D, (half + 1) * D)
            gate = gt_ref[:, cols].astype(F32)
            branch = _nt(y_ref[...], w[...])
            dzg = dm * branch * gate * (1.0 - gate)
            dbg_ref[:, cols] += jnp.sum(dzg, axis=0, keepdims=True)
            dzg_ref[:, cols] = dzg.astype(BF16)
            dbr = (dm * gate).astype(BF16)
            dpp_ref[:, cols] = dbr
            dy = _nn(dbr, w[...])
            if half == 0:
                dya_ref[...] = dy.astype(BF16)
            else:
                dys_ref[...] = dy

    def tile(w):
        return pl.BlockSpec((TM, w), lambda i: (i, 0))

    return pl.pallas_call(
        body, name="merge_bwd", grid=(t // TM,),
        in_specs=[tile(D), tile(2 * D), tile(D_ATT), tile(D_SGU), ANY],
        out_specs=[tile(2 * D), tile(D_ATT), tile(D_SGU), tile(2 * D), tile(D), pl.BlockSpec((1, 2 * D), lambda i: (0, 0))],
        out_shape=[jax.ShapeDtypeStruct((t, 2 * D), BF16), jax.ShapeDtypeStruct((t, D_ATT), BF16), jax.ShapeDtypeStruct((t, D_SGU), F32),
                   jax.ShapeDtypeStruct((t, 2 * D), BF16), jax.ShapeDtypeStruct((t, D), BF16), jax.ShapeDtypeStruct((1, 2 * D), F32)],
        scratch_shapes=[pltpu.VMEM((D, D), BF16), pltpu.VMEM((D, D), BF16), pltpu.SemaphoreType.DMA((2, N_DEV))],
        compiler_params=_cparams(("arbitrary",)),
    )(dx, gt, y_att, y_sgu, gw)


def _sgu_bwd(zs, dys, lng, lnb, wm2, wmt2, bsx, comms=()):
    t = zs.shape[0]
    n_steps = t // TM

    def body(zs_ref, dys_ref, lng_ref, lnb_ref, wm2_ref, wmt2_ref, bsx_ref, dzs_ref, dw_ref, dbs_ref, dlg_ref, dlb_ref,
             s_scr, dv_scr, ds_acc):
        i = pl.program_id(0)

        @pl.when(i == 0)
        def _():
            dw_ref[...] = jnp.zeros_like(dw_ref)
            dlg_ref[...] = jnp.zeros_like(dlg_ref)
            dlb_ref[...] = jnp.zeros_like(dlb_ref)
            ds_acc[...] = jnp.zeros_like(ds_acc)

        lng = lng_ref[...]
        zs, th, u, xhat, rstd, vsn = _sgu_norm(zs_ref, lng, lnb_ref[...])
        vb = vsn.astype(BF16)
        _sgu_mix(vb, wm2_ref, bsx_ref[...], s_scr)
        dys = dys_ref[...]
        du = dys * s_scr[...]
        ds = dys * u
        dsb = ds.astype(BF16)
        first = lax.broadcasted_iota(jnp.int32, (1, 128), 1) < 64
        acc = jnp.zeros((SGU_BLOCK, D_SGU), F32)
        for n in range(TM // SGU_BLOCK):
            rows = slice(n * 128, (n + 1) * 128)
            acc = acc + ds[rows]
            for p in range(4):
                lanes = slice(p * 128, (p + 1) * 128)
                stack = _group_stack(dsb[rows, lanes], first)
                dv_scr[rows, lanes] = _nn(wmt2_ref[p], stack)
                dw_ref[p] += _nt(stack, vb[rows, lanes])
        ds_acc[...] += acc
        dvsn = dv_scr[...]
        dlg_ref[...] += jnp.sum(dvsn * xhat, axis=0, keepdims=True)
        dlb_ref[...] += jnp.sum(dvsn, axis=0, keepdims=True)
        dxh = dvsn * lng
        dvs = rstd * (dxh - jnp.mean(dxh, axis=-1, keepdims=True) - xhat * jnp.mean(dxh * xhat, axis=-1, keepdims=True))
        dga = jnp.concatenate([du, dvs], axis=1)
        dzs_ref[...] = (dga * _gelu_grad(zs, th)).astype(BF16)

        @pl.when(i == n_steps - 1)
        def _():
            r = lax.broadcasted_iota(jnp.int32, (256, 128), 0) % SGU_BLOCK
            c = lax.broadcasted_iota(jnp.int32, (256, 128), 1)
            keep = (r // CHUNK) >= (c // CHUNK)
            for p in range(4):
                dw_ref[p] = jnp.where(keep, dw_ref[p], 0.0)
            total = ds_acc[...]
            grp = lax.broadcasted_iota(jnp.int32, (SGU_BLOCK, D_SGU), 1) // 64
            lane = lax.broadcasted_iota(jnp.int32, (SGU_BLOCK, 128), 1)
            out = jnp.zeros((SGU_BLOCK, 128), F32)
            for gi in range(8):
                out = jnp.where(lane == gi, jnp.sum(jnp.where(grp == gi, total, 0.0), axis=1, keepdims=True), out)
            dbs_ref[...] = out

    def tile(w):
        return pl.BlockSpec((TM, w), lambda i: (i, 0))

    def const(shape):
        return pl.BlockSpec(shape, lambda i: (0,) * len(shape))

    return _call(
        body, name="sgu_bwd", grid=(n_steps,),
        in_specs=[tile(2 * D_SGU), tile(D_SGU), const((1, D_SGU)), const((1, D_SGU)), const((4, 128, 256)), const((4, 128, 256)),
                  const((128, D_SGU))],
        out_specs=[tile(2 * D_SGU), const((4, 256, 128)), const((128, 128)), const((1, D_SGU)), const((1, D_SGU))],
        out_shape=[jax.ShapeDtypeStruct((t, 2 * D_SGU), BF16), jax.ShapeDtypeStruct((4, 256, 128), F32),
                   jax.ShapeDtypeStruct((128, 128), F32), jax.ShapeDtypeStruct((1, D_SGU), F32), jax.ShapeDtypeStruct((1, D_SGU), F32)],
        scratch_shapes=[pltpu.VMEM((TM, D_SGU), F32), pltpu.VMEM((TM, D_SGU), F32), pltpu.VMEM((SGU_BLOCK, D_SGU), F32)],
        args=(zs, dys, lng, lnb, wm2, wmt2, bsx), comms=comms)


def _adamw(g, w, m, v):
    m = ADAM_B1 * m + (1.0 - ADAM_B1) * g
    v = ADAM_B2 * v + (1.0 - ADAM_B2) * (g * g)
    m_hat = m / (1.0 - ADAM_B1 ** ADAM_STEP)
    v_hat = v / (1.0 - ADAM_B2 ** ADAM_STEP)
    return -ADAM_LR * (m_hat / (jnp.sqrt(v_hat) + ADAM_EPS) + ADAM_WD * w), m, v


def _adamw_matrix(parts, sums, w, m, v, transposed, name):
    _, r, c = parts.shape
    tc = 256

    def body(own_ref, p_ref, s_ref, w_ref, m_ref, v_ref, g_ref, d_ref, mo_ref, vo_ref):
        g = p_ref[0].astype(F32) + p_ref[1].astype(F32) + p_ref[2].astype(F32) + s_ref[...].astype(F32)
        g = g.T if transposed else g
        g_ref[...] = g
        d_ref[...], mo_ref[...], vo_ref[...] = _adamw(g, w_ref[...], m_ref[...], v_ref[...])

    own = pl.BlockSpec((None, tc, r), lambda i, o: (0, i, 0)) if transposed else pl.BlockSpec((None, r, tc), lambda i, o: (0, 0, i))
    return pl.pallas_call(
        body, name=name,
        grid_spec=pltpu.PrefetchScalarGridSpec(
            num_scalar_prefetch=1, grid=(c // tc,),
            in_specs=[pl.BlockSpec((3, r, tc), lambda i, o: (0, 0, i)), pl.BlockSpec((None, r, tc), lambda i, o: (o[0], 0, i)), own, own, own],
            out_specs=[own] * 4),
        out_shape=[jax.ShapeDtypeStruct(w.shape, F32)] * 4,
        compiler_params=_cparams(("arbitrary",)),
    )(_my_index(("x", 2), ("y", 1)), parts, sums, w, m, v)


_SMALL_2D = {"norm_ffn1": (1, D), "norm_mix": (1, D), "norm_ffn2": (1, D), "norm_final": (1, D), "b_gate": (1, 2 * D),
             "sgu_ln_g": (1, D_SGU), "sgu_ln_b": (1, D_SGU), "sgu_b_s": (8, SGU_BLOCK), "rel_bias": (HEADS, N_REL),
             "sgu_w_s": (8 * SGU_BLOCK, SGU_BLOCK)}


def _adamw_small(parts, loss_parts, p):
    names = list(parts)
    n = len(names)

    def body(*refs):
        got, loss_got, wmv, outs, loss_out = refs[:n], refs[n], refs[n + 1:4 * n + 1], refs[4 * n + 1:8 * n + 1], refs[8 * n + 1]
        for i, name in enumerate(names):
            g = got[i][0]
            for k in range(1, N_DEV):
                g = g + got[i][k]
            if name == "sgu_b_s":
                g = g.T[0:8, :]
            res = (g,) + _adamw(g, wmv[3 * i][...], wmv[3 * i + 1][...], wmv[3 * i + 2][...])
            for o_ref, val in zip(outs[4 * i:4 * i + 4], res):
                o_ref[...] = val
        total = loss_got[0]
        for k in range(1, N_DEV):
            total = total + loss_got[k]
        loss_out[...] = total

    wmv = [p[pre + name].reshape(_SMALL_2D[name]) for name in names for pre in ("", "m_", "v_")]
    res = pl.pallas_call(
        body, name="adamw_small",
        out_shape=[jax.ShapeDtypeStruct(_SMALL_2D[name], F32) for name in names for _ in range(4)] + [jax.ShapeDtypeStruct((1, 128), F32)],
        compiler_params=_cparams())(*[parts[name] for name in names], loss_parts, *wmv)
    return [{name: res[4 * i + j].reshape(p[name].shape) for i, name in enumerate(names)} for j in range(4)], res[-1]


def _pack_rows(groups, name):
    flat = [a for grp in groups for a, _ in grp]
    rows = [grp[0][0].shape[2] if grp[0][1] else grp[0][0].shape[1] for grp in groups]

    def body(*refs):
        o_ref, pos, off = refs[-1], 0, 0
        for grp, r in zip(groups, rows):
            vals = []
            for _, transposed in grp:
                val = refs[pos][0]
                vals.append(val.T if transposed else val)
                pos += 1
            o_ref[off:off + r, :] = (vals[0] if len(vals) == 1 else jnp.concatenate(vals, axis=1)).astype(BF16)
            off += r

    return pl.pallas_call(body, name=name, out_shape=jax.ShapeDtypeStruct((sum(rows), D), BF16), compiler_params=_cparams())(*flat)


def _step(x, target, p):
    n1, nm, n2 = p["norm_ffn1"], p["norm_mix"], p["norm_ffn2"]
    nf = p["norm_final"].reshape(1, D)
    lng, lnb = p["sgu_ln_g"], p["sgu_ln_b"]
    w_m = jnp.where(jnp.asarray(_sgu_mask())[None], p["sgu_w_s"][0], 0.0).astype(BF16)
    wm2 = jnp.concatenate([w_m[0::2], w_m[1::2]], axis=2)
    w_mt = w_m.transpose(0, 2, 1)
    wmt2 = jnp.concatenate([w_mt[0::2], w_mt[1::2]], axis=2)
    bsx = jnp.repeat(p["sgu_b_s"][0].T, 64, axis=1)
    bias = _band_bias(p["rel_bias"][0])

    def chip_sums(grads, name):
        gots = _comm_only([_SiblingSwap(grads)], "swap_" + name)
        return _pair_sums(grads, gots, "pair_sums_" + name)
    def as_rows(a):
        return jnp.swapaxes(a, 1, 2)

    def updates(parts, sums, names):
        res = {}
        for pt, sm, n in zip(parts, sums, names):
            if p[n].shape[1:] == pt.shape[1:]:
                res[n] = _adamw_matrix(pt, sm, p[n], p["m_" + n], p["v_" + n], False, "adamw_" + n)
            elif p[n].shape[2] > 128:
                res[n] = [as_rows(o) for o in _adamw_matrix(pt, sm, as_rows(p[n]), as_rows(p["m_" + n]), as_rows(p["v_" + n]), False,
                                                            "adamw_" + n)]
            else:
                res[n] = _adamw_matrix(pt, sm, p[n], p["m_" + n], p["v_" + n], True, "adamw_" + n)
        return res

    rows1 = _pack_rows([[(as_rows(p["ffn1_w_gate"]), False)], [(as_rows(p["ffn1_w_up"]), False)], [(p["ffn1_w_down"], False)]], "pack_ffn1")
    rows_m = _pack_rows([[(as_rows(p["w_in"]), False)], [(p["w_branch_att"], True), (p["w_branch_sgu"], True)], [(p["w_out"], False)]],
                        "pack_mixer")
    rows2d = _pack_rows([[(p["ffn2_w_down"], False)]], "pack_ffn2_down")
    rows2gu = _pack_rows([[(as_rows(p["ffn2_w_gate"]), False)], [(as_rows(p["ffn2_w_up"]), False)]], "pack_ffn2_gate_up")
    (gw1,) = _comm_only([_Gather(rows1)], "gather_ffn1")
    x1, ab1, h1, (gwm,) = _ffn_fwd(x, n1, gw1, gw1, 2 * R_FF, "ffn1_fwd", [_Gather(rows_m)])
    (q, k, v, zs, gt, h2), (gw2d,) = _mix_proj_fwd(x1, nm, p["b_gate"], gwm, [_Gather(rows2d)])
    y_att, (gw2gu,) = _att_fwd(q, k, v, bias, [_Gather(rows2gu)])
    x2, y_sgu, merged = _merge_fwd(x1, zs, gt, y_att, lng, lnb, wm2, bsx, gwm)
    dx3, ab2, hb, _, d_nf, loss = _ffn_fwd(x2, n2, gw2gu, gw2d, 0, "ffn2_fwd", head=(nf, target))

    (dab, sb, dfb), _ = _ffn_bwd_hidden(ab2, dx3, gw2d, 0, "ffn2_bwd_hidden")
    (dx2, d_n2), _ = _ffn_bwd_input(x2, n2, dab, dx3, gw2gu, "ffn2_bwd")
    g_gu = _weight_grad(dab, hb, "ffn2_dw_gate_up", mats=2)
    dzg, dya, dys, dpp, dxb, d_bg = _merge_bwd(dx2, gt, y_att, y_sgu, gwm)
    g_late = [(g_gu, 0), (g_gu, 1), (_weight_grad(sb, dfb, "ffn2_dw_down"), 0), (_weight_grad(dpp, y_att, "dw_branch_att", 0, D), 0),
              (_weight_grad(dpp, y_sgu, "dw_branch_sgu", D, D), 0), (_weight_grad(merged, dxb, "dw_out"), 0)]
    late = ("ffn2_w_gate", "ffn2_w_up", "ffn2_w_down", "w_branch_att", "w_branch_sgu", "w_out")
    (dzs, d_wm, d_bs, d_lng, d_lnb), gots_late = _sgu_bwd(zs, dys, lng, lnb, wm2, wmt2, bsx, [_SiblingSwap(g_late)])
    sums_late = _pair_sums(g_late, gots_late, "pair_sums_late")
    (dq, dk, dv, d_bias), parts_late = _att_bwd(q, k, v, bias, dya, [_ChipScatter(sums_late)])
    big = updates(parts_late, sums_late, late)
    d_rel = _rel_bias_grad(d_bias)
    dz = jnp.concatenate([dq, dk, dv, dzs, dzg], axis=1)
    sums_in = chip_sums([(_weight_grad(dz, h2, "dw_in"), 0)], "w_in")
    (dx1, d_nm), parts_in = _mix_proj_bwd(dz, x1, nm, dx2, gwm, [_ChipScatter(sums_in)])
    big.update(updates(parts_in, sums_in, ("w_in",)))
    small = {"norm_ffn2": d_n2, "norm_final": d_nf, "b_gate": d_bg, "sgu_ln_g": d_lng, "sgu_ln_b": d_lnb, "sgu_b_s": d_bs,
             "rel_bias": d_rel, "sgu_w_s": d_wm.reshape(_SMALL_2D["sgu_w_s"]), "norm_mix": d_nm}

    (dab, sb, dfb), (*small_parts, loss_parts) = _ffn_bwd_hidden(ab1, dx1, gw1, 2 * R_FF, "ffn1_bwd_hidden",
                                                                 [_AllToAll(list(small.values()) + [loss])])
    sums_d = chip_sums([(_weight_grad(sb, dfb, "ffn1_dw_down"), 0)], "ffn1_down")
    g_gu, (parts_d,) = _weight_grad(dab, h1, "ffn1_dw_gate_up", comms=[_ChipScatter(sums_d)], mats=2)
    sums_gu = chip_sums([(g_gu, 0), (g_gu, 1)], "ffn1_gate_up")
    (dx0, d_n1), parts_gu = _ffn_bwd_input(x, n1, dab, dx1, gw1, "ffn1_bwd", [_ChipScatter(sums_gu)])
    (n1_parts,) = _comm_only([_AllToAll([d_n1])], "gather_norm_ffn1")
    big.update(updates([parts_d] + parts_gu, sums_d + sums_gu, ("ffn1_w_down", "ffn1_w_gate", "ffn1_w_up")))
    out_s, loss_sum = _adamw_small(dict(zip(small, small_parts), norm_ffn1=n1_parts), loss_parts, p)
    return dx0, loss_sum[0, 0], [{**{n: four[i] for n, four in big.items()}, **s} for i, s in enumerate(out_s)]


_OUT_ORDER = ("norm_ffn1", "ffn1_w_gate", "ffn1_w_up", "ffn1_w_down", "norm_mix", "w_in", "b_gate", "rel_bias", "sgu_ln_g", "sgu_ln_b",
              "sgu_w_s", "sgu_b_s", "w_branch_att", "w_branch_sgu", "w_out", "norm_ffn2", "ffn2_w_gate", "ffn2_w_up", "ffn2_w_down",
              "norm_final")


def kernel(x, norm_ffn1, ffn1_w_gate, ffn1_w_up, ffn1_w_down, norm_mix, w_in, b_gate, rel_bias, sgu_ln_g, sgu_ln_b, sgu_w_s, sgu_b_s, w_branch_att, w_branch_sgu, w_out, norm_ffn2, ffn2_w_gate, ffn2_w_up, ffn2_w_down, norm_final, loss_target, m_norm_ffn1, m_ffn1_w_gate, m_ffn1_w_up, m_ffn1_w_down, m_norm_mix, m_w_in, m_b_gate, m_rel_bias, m_sgu_ln_g, m_sgu_ln_b, m_sgu_w_s, m_sgu_b_s, m_w_branch_att, m_w_branch_sgu, m_w_out, m_norm_ffn2, m_ffn2_w_gate, m_ffn2_w_up, m_ffn2_w_down, m_norm_final, v_norm_ffn1, v_ffn1_w_gate, v_ffn1_w_up, v_ffn1_w_down, v_norm_mix, v_w_in, v_b_gate, v_rel_bias, v_sgu_ln_g, v_sgu_ln_b, v_sgu_w_s, v_sgu_b_s, v_w_branch_att, v_w_branch_sgu, v_w_out, v_norm_ffn2, v_ffn2_w_gate, v_ffn2_w_up, v_ffn2_w_down, v_norm_final):
    args = dict(locals())
    dx, loss, outs = _step(x[0], loss_target[0], {pre + n: args[pre + n] for pre in ("", "m_", "v_") for n in _OUT_ORDER})
    return (loss, dx[None], *[o[n] for o in outs for n in _OUT_ORDER])
```

```python
import functools

import numpy as np
import jax
import jax.numpy as jnp
from jax import lax
from jax.experimental import pallas as pl
from jax.experimental.pallas import tpu as pltpu

F32 = jnp.float32
BF16 = jnp.bfloat16

N_DEV = 8
D = 1024
F = 2816
D_ATT = 512
D_SGU = 512
D_IN = 4608
HEADS = 8
CHUNK = 64
N_LEFT = 8
REL_CLIP = 256
N_REL = 2 * REL_CLIP + 1
SGU_BLOCK = 128
EPS = 1e-6
NEG_INF = -1e30
QB = 256
KW = 3 * QB

R_FF, R_IN, R_BR, R_WO = F // N_DEV, D_IN // N_DEV, D // N_DEV, D // N_DEV
OFF_IN, OFF_BR, OFF_WO = 0, R_IN, R_IN + R_BR

FC = 256
TM = 512
VMEM_LIMIT = 56 * 1024 * 1024

ADAM_LR, ADAM_B1, ADAM_B2, ADAM_EPS, ADAM_WD, ADAM_STEP = 0.001, 0.9, 0.999, 1e-08, 0.01, 10

MESH = pl.DeviceIdType.MESH
ANY = pl.BlockSpec(memory_space=pl.ANY)


def _nt(a, b):
    return lax.dot_general(a, b, (((1,), (1,)), ((), ())), preferred_element_type=F32)


def _tn(a, b):
    return lax.dot_general(a, b, (((0,), (0,)), ((), ())), preferred_element_type=F32)


def _nn(a, b):
    return jnp.dot(a, b, preferred_element_type=F32)


def _cparams(sem=None):
    return pltpu.CompilerParams(dimension_semantics=sem, vmem_limit_bytes=VMEM_LIMIT)


def _load_rows(gw_ref, dst, off, rows, sems):
    copies = [pltpu.make_async_copy(gw_ref.at[k, pl.ds(off, rows), :], dst.at[pl.ds(k * rows, rows), :], sems.at[k])
              for k in range(N_DEV)]
    for cp in copies:
        cp.start()
    return copies


def _rms(xv):
    r = lax.rsqrt(jnp.mean(xv * xv, axis=-1, keepdims=True) + EPS)
    return xv * r, r


def _rms_bwd(dh, xn, r, gain):
    dxn = dh * gain
    dx = r * (dxn - xn * jnp.mean(dxn * xn, axis=-1, keepdims=True))
    return dx, jnp.sum(dh * xn, axis=0, keepdims=True)


def _gelu(x):
    t = jnp.tanh(0.7978845608028654 * (x + 0.044715 * x * x * x))
    return 0.5 * x * (1.0 + t), t


def _gelu_grad(x, t):
    return 0.5 * (1.0 + t) + 0.5 * x * (1.0 - t * t) * 0.7978845608028654 * (1.0 + 3.0 * 0.044715 * x * x)


def _place():
    x, y, cc = lax.axis_index("x"), lax.axis_index("y"), lax.axis_index("c")
    return x, y, cc, [(1 - x, y), (x, 1 - y), (1 - x, 1 - y)]


class _Gather:
    def __init__(self, shard):
        self.inputs = [shard]
        self.out_shape = [jax.ShapeDtypeStruct((N_DEV,) + shard.shape, shard.dtype)]
        self.scratch = [pltpu.SemaphoreType.DMA((7,)), pltpu.SemaphoreType.DMA((7,)), pltpu.SemaphoreType.DMA]

    def _copies(self, ins, outs, scr):
        (x_ref,), (out_ref,), (send_sems, recv_sems, local_sem) = ins, outs, scr
        x, y, cc, chips = _place()

        def slab(px, py, pc):
            return out_ref.at[4 * px + 2 * py + pc]

        def copy(k, block, to, src=None):
            return pltpu.make_async_remote_copy(
                src_ref=slab(*block) if src is None else src, dst_ref=slab(*block),
                send_sem=send_sems.at[k], recv_sem=recv_sems.at[k], device_id=to, device_id_type=MESH)

        me, sibling = (x, y, cc), (x, y, 1 - cc)
        x_nbr, y_nbr, diagonal = chips
        mine = pltpu.make_async_copy(x_ref, slab(*me), local_sem)
        first = [copy(0, me, sibling, src=x_ref), copy(1, me, (*x_nbr, cc), src=x_ref), copy(2, me, (*y_nbr, cc), src=x_ref)]
        neighbours = [copy(1, (*x_nbr, cc), me), copy(2, (*y_nbr, cc), me)]
        second_hand = copy(3, (x ^ (1 - cc), y ^ cc, cc), (x ^ cc, y ^ (1 - cc), cc))
        from_diagonal = copy(3, (*diagonal, cc), me)
        passed = [copy(4 + j, (*chip, cc), sibling) for j, chip in enumerate(chips)]
        from_sibling = [copy(0, sibling, me)] + [copy(4 + j, (*chip, 1 - cc), me) for j, chip in enumerate(chips)]
        return mine, first, neighbours, second_hand, from_diagonal, passed, from_sibling

    def begin(self, *refs):
        mine, first = self._copies(*refs)[:2]
        mine.start()
        for cp in first:
            cp.start()

    def mid(self, *refs):
        _, _, neighbours, second_hand, _, passed, _ = self._copies(*refs)
        for cp in neighbours:
            cp.wait_recv()
        second_hand.start()
        passed[0].start()
        passed[1].start()

    def relay(self, *refs):
        _, _, _, _, from_diagonal, passed, _ = self._copies(*refs)
        from_diagonal.wait_recv()
        passed[2].start()

    def end(self, *refs):
        mine, first, _, second_hand, _, passed, from_sibling = self._copies(*refs)
        for cp in from_sibling:
            cp.wait_recv()
        for cp in first + [second_hand] + passed:
            cp.wait_send()
        mine.wait()


class _Direct:
    def begin(self, *refs):
        keep, give = self._copies(*refs)
        for cp in keep + give:
            cp.start()

    def mid(self, *refs):
        pass

    def relay(self, *refs):
        pass

    def end(self, *refs):
        keep, give = self._copies(*refs)
        for cp in give:
            cp.wait_recv()
        for cp in give:
            cp.wait_send()
        for cp in keep:
            cp.wait()


class _SiblingSwap(_Direct):
    def __init__(self, grads):
        n = len(grads)
        self.which = [w for _, w in grads]
        self.inputs = [g for g, _ in grads]
        self.out_shape = [jax.ShapeDtypeStruct((4,) + g.shape[1:], g.dtype) for g, _ in grads]
        self.scratch = [pltpu.SemaphoreType.DMA((n, 4)), pltpu.SemaphoreType.DMA((n, 4))]

    def _copies(self, ins, outs, scr):
        send_sems, recv_sems = scr
        x, y, cc, _ = _place()
        return [], [pltpu.make_async_remote_copy(src_ref=g_ref.at[N_DEV * w + 2 * j + 1 - cc], dst_ref=got_ref.at[j],
                                                 send_sem=send_sems.at[i, j], recv_sem=recv_sems.at[i, j], device_id=(x, y, 1 - cc),
                                                 device_id_type=MESH)
                    for i, (g_ref, got_ref, w) in enumerate(zip(ins, outs, self.which)) for j in range(4)]


class _ChipScatter(_Direct):
    def __init__(self, sums):
        n = len(sums)
        self.inputs = list(sums)
        self.out_shape = [jax.ShapeDtypeStruct((3,) + s.shape[1:], s.dtype) for s in sums]
        self.scratch = [pltpu.SemaphoreType.DMA((n, 3)), pltpu.SemaphoreType.DMA((n, 3))]

    def _copies(self, ins, outs, scr):
        send_sems, recv_sems = scr
        _, _, cc, chips = _place()
        return [], [pltpu.make_async_remote_copy(src_ref=s_ref.at[2 * px + py], dst_ref=got_ref.at[j], send_sem=send_sems.at[i, j],
                                                 recv_sem=recv_sems.at[i, j], device_id=(px, py, cc), device_id_type=MESH)
                    for i, (s_ref, got_ref) in enumerate(zip(ins, outs)) for j, (px, py) in enumerate(chips)]


class _AllToAll(_Direct):
    def __init__(self, blocks):
        n = len(blocks)
        self.inputs = list(blocks)
        self.out_shape = [jax.ShapeDtypeStruct((N_DEV,) + b.shape, b.dtype) for b in blocks]
        self.scratch = [pltpu.SemaphoreType.DMA((n, 7)), pltpu.SemaphoreType.DMA((n, 7)), pltpu.SemaphoreType.DMA((n,))]

    def _copies(self, ins, outs, scr):
        send_sems, recv_sems, local_sems = scr
        x, y, cc, _ = _place()
        me = 4 * x + 2 * y + cc
        keep = [pltpu.make_async_copy(b_ref, got_ref.at[me], local_sems.at[i]) for i, (b_ref, got_ref) in enumerate(zip(ins, outs))]
        give = [pltpu.make_async_remote_copy(src_ref=b_ref, dst_ref=got_ref.at[me], send_sem=send_sems.at[i, k - 1],
                                             recv_sem=recv_sems.at[i, k - 1],
                                             device_id=(x ^ ((k >> 2) & 1), y ^ ((k >> 1) & 1), cc ^ (k & 1)), device_id_type=MESH)
                for i, (b_ref, got_ref) in enumerate(zip(ins, outs)) for k in range(1, N_DEV)]
        return keep, give


def _split_refs(refs, counts):
    out, pos = [], 0
    for n in counts:
        out.append(list(refs[pos:pos + n]))
        pos += n
    return out


def _bind(comms, c_in, c_out, c_scr):
    ins = _split_refs(c_in, [len(c.inputs) for c in comms])
    outs = _split_refs(c_out, [len(c.out_shape) for c in comms])
    scr = _split_refs(c_scr, [len(c.scratch) for c in comms])
    return [(c, (i, o, s)) for c, i, o, s in zip(comms, ins, outs, scr)]


def _call(body, *, name, grid, in_specs, out_specs, out_shape, scratch_shapes, args, comms=()):
    c_in = [a for c in comms for a in c.inputs]
    c_out = [s for c in comms for s in c.out_shape]
    c_scr = [s for c in comms for s in c.scratch]
    counts = [len(in_specs), len(c_in), len(out_shape), len(c_out), len(scratch_shapes), len(c_scr)]

    def full(*refs):
        ins, cin, outs, cout, scr, cscr = _split_refs(refs, counts)
        bound = _bind(comms, cin, cout, cscr)
        if comms:
            def at(steps):
                return functools.reduce(jnp.logical_and, [pl.program_id(ax) == s for ax, s in enumerate(steps)])

            first, last = at([0] * len(grid)), at([n - 1 for n in grid])

            @pl.when(first)
            def _():
                for c, r in bound:
                    c.begin(*r)

            @pl.when(at([(grid[0] - 1) // 2] + [0] * (len(grid) - 1)))
            def _():
                for c, r in bound:
                    c.mid(*r)

            @pl.when(last)
            def _():
                for c, r in bound:
                    c.relay(*r)

        body(*ins, *outs, *scr)
        if comms:
            @pl.when(last)
            def _():
                for c, r in bound:
                    c.end(*r)

    res = pl.pallas_call(
        full, name=name, grid=grid,
        in_specs=list(in_specs) + [ANY] * len(c_in), out_specs=list(out_specs) + [ANY] * len(c_out),
        out_shape=list(out_shape) + c_out, scratch_shapes=list(scratch_shapes) + c_scr,
        compiler_params=_cparams(("arbitrary",) * len(grid)),
    )(*args, *c_in)
    return list(res[:len(out_shape)]), list(res[len(out_shape):])


def _comm_only(comms, name):
    c_in = [a for c in comms for a in c.inputs]
    c_out = [s for c in comms for s in c.out_shape]
    c_scr = [s for c in comms for s in c.scratch]

    def full(*refs):
        cin, cout, cscr = _split_refs(refs, [len(c_in), len(c_out), len(c_scr)])
        bound = _bind(comms, cin, cout, cscr)
        for phase in ("begin", "mid", "relay", "end"):
            for c, r in bound:
                getattr(c, phase)(*r)

    return list(pl.pallas_call(full, name=name, in_specs=[ANY] * len(c_in), out_specs=[ANY] * len(c_out), out_shape=c_out,
                               scratch_shapes=c_scr)(*c_in))


def _my_index(*axes_and_weights):
    return sum(w * lax.axis_index(a) for a, w in axes_and_weights).astype(jnp.int32).reshape(1)


def _pair_sums(grads, gots, name):
    n = len(grads)

    def body(c_ref, *refs):
        for a_ref, b_ref, o_ref in zip(refs[:n], refs[n:2 * n], refs[2 * n:]):
            o_ref[...] = (a_ref[...].astype(F32) + b_ref[...].astype(F32)).astype(BF16)

    def tile(g):
        return pl.BlockSpec((1,) + g.shape[1:], lambda j, c_ref: (j, 0, 0))

    def mine(g, w):
        return pl.BlockSpec((1, None) + g.shape[1:], lambda j, c_ref: (4 * w + j, c_ref[0], 0, 0))

    return list(pl.pallas_call(
        body, name=name,
        grid_spec=pltpu.PrefetchScalarGridSpec(num_scalar_prefetch=1, grid=(4,),
                                               in_specs=[mine(g, w) for g, w in grads] + [tile(g) for g in gots],
                                               out_specs=[tile(g) for g in gots]),
        out_shape=[jax.ShapeDtypeStruct(g.shape, BF16) for g in gots],
        compiler_params=_cparams(("arbitrary",)))(_my_index(("c", 1)), *[g.reshape((-1, 2) + g.shape[1:]) for g, _ in grads], *gots))


def _ffn_fwd(x, gain, gw_gu, gw_d, off_d, name, comms=(), head=None):
    t = x.shape[0]
    n_head = 0 if head is None else 2

    def body(x_ref, g_ref, gu_ref, d_ref, *refs):
        head_in, (o_ref, ab_ref, h_ref), head_out = refs[:n_head], refs[n_head:n_head + 3], refs[n_head + 3:2 * n_head + 3]
        wg, wu, wd, s_scr, sems = refs[2 * n_head + 3:]

        @pl.when(pl.program_id(0) == 0)
        def _():
            cps = _load_rows(gu_ref, wg, 0, R_FF, sems.at[0]) + _load_rows(gu_ref, wu, R_FF, R_FF, sems.at[1]) \
                + _load_rows(d_ref, wd, off_d, R_FF, sems.at[2])
            for o in head_out:
                o[...] = jnp.zeros_like(o)
            for cp in cps:
                cp.wait()

        xv = x_ref[...]
        xn, _ = _rms(xv)
        h = (xn * g_ref[...]).astype(BF16)
        h_ref[...] = h
        for c in range(F // FC):
            rows = pl.ds(c * FC, FC)
            a = _nt(h, wg[rows, :])
            b = _nt(h, wu[rows, :])
            ab_ref[:, c * FC:(c + 1) * FC] = a.astype(BF16)
            ab_ref[:, F + c * FC:F + (c + 1) * FC] = b.astype(BF16)
            s_scr[:, c * FC:(c + 1) * FC] = (a * jax.nn.sigmoid(a) * b).astype(BF16)
        out = xv + 0.5 * _nn(s_scr[...], wd[...])
        if head is None:
            o_ref[...] = out
        else:
            (gf_ref, t_ref), (dg_ref, loss_ref) = head_in, head_out
            gain_f = gf_ref[...]
            yn, r = _rms(out)
            err = yn * gain_f - t_ref[...]
            loss_ref[...] += 0.5 * jnp.sum(jnp.mean(err * err, axis=-1, keepdims=True), axis=0, keepdims=True)
            o_ref[...], dg = _rms_bwd(err * (1.0 / D), yn, r, gain_f)
            dg_ref[...] += dg

    tile = pl.BlockSpec((TM, D), lambda i: (i, 0))
    row = pl.BlockSpec((1, D), lambda i: (0, 0))
    head_specs = [] if head is None else [row, pl.BlockSpec((1, 128), lambda i: (0, 0))]
    head_shapes = [] if head is None else [jax.ShapeDtypeStruct((1, D), F32), jax.ShapeDtypeStruct((1, 128), F32)]
    res, got = _call(
        body, name=name, grid=(t // TM,),
        in_specs=[tile, row, ANY, ANY] + ([] if head is None else [row, tile]),
        out_specs=[tile, pl.BlockSpec((TM, 2 * F), lambda i: (i, 0)), tile] + head_specs,
        out_shape=[jax.ShapeDtypeStruct((t, D), F32), jax.ShapeDtypeStruct((t, 2 * F), BF16), jax.ShapeDtypeStruct((t, D), BF16)] + head_shapes,
        scratch_shapes=[pltpu.VMEM((F, D), BF16)] * 3 + [pltpu.VMEM((TM, F), BF16), pltpu.SemaphoreType.DMA((3, N_DEV))],
        args=(x, gain, gw_gu, gw_d) + (() if head is None else tuple(head)), comms=comms)
    return (res[0], res[1], res[2], got, *res[3:])


def _ffn_bwd_hidden(ab, dout, gw_d, off_d, name, comms=()):
    t = ab.shape[0]
    n_steps = t // TM
    row_chunks = [(r, min(512, F - r)) for r in range(0, F, 512)]

    def tile(w):
        return pl.BlockSpec((TM, w), lambda i: (i, 0))

    def hidden(ab_ref, do_ref, d_ref, dab_ref, dwd_ref, wd, s_scr, acc, sems, out_sem):
        step = pl.program_id(0)

        @pl.when(step == 0)
        def _():
            cps = _load_rows(d_ref, wd, off_d, R_FF, sems)
            acc[...] = jnp.zeros_like(acc)
            for cp in cps:
                cp.wait()

        df = (0.5 * do_ref[...]).astype(BF16)
        for c in range(F // FC):
            a = ab_ref[:, c * FC:(c + 1) * FC].astype(F32)
            b = ab_ref[:, F + c * FC:F + (c + 1) * FC].astype(F32)
            sg = jax.nn.sigmoid(a)
            sl = a * sg
            ds = _nt(df, wd[pl.ds(c * FC, FC), :])
            dab_ref[:, c * FC:(c + 1) * FC] = (ds * b * (sg * (1.0 + a * (1.0 - sg)))).astype(BF16)
            dab_ref[:, F + c * FC:F + (c + 1) * FC] = (ds * sl).astype(BF16)
            s_scr[:, c * FC:(c + 1) * FC] = (sl * b).astype(BF16)
        for r, n in row_chunks:
            acc[r:r + n, :] += _tn(s_scr[:, r:r + n], df)

        @pl.when(step == n_steps - 1)
        def _():
            wd[...] = acc[...].astype(BF16)
            out = pltpu.make_async_copy(wd, dwd_ref, out_sem)
            out.start()
            out.wait()

    (dab, dwd), got = _call(
        hidden, name=name, grid=(n_steps,),
        in_specs=[tile(2 * F), tile(D), ANY], out_specs=[tile(2 * F), ANY],
        out_shape=[jax.ShapeDtypeStruct((t, 2 * F), BF16), jax.ShapeDtypeStruct((F, D), BF16)],
        scratch_shapes=[pltpu.VMEM((F, D), BF16), pltpu.VMEM((TM, F), BF16), pltpu.VMEM((F, D), F32), pltpu.SemaphoreType.DMA((N_DEV,)),
                        pltpu.SemaphoreType.DMA],
        args=(ab, dout, gw_d), comms=comms)
    return (dab, dwd.reshape(N_DEV, R_FF, D)), got


def _ffn_bwd_input(x, gain, dab, dout, gw_gu, name, comms=()):
    t = x.shape[0]

    def body(x_ref, g_ref, dab_ref, do_ref, gu_ref, dx_ref, dg_ref, wgu, sems):
        @pl.when(pl.program_id(0) == 0)
        def _():
            cps = _load_rows(gu_ref, wgu.at[0:F], 0, R_FF, sems.at[0]) + _load_rows(gu_ref, wgu.at[F:2 * F], R_FF, R_FF, sems.at[1])
            dg_ref[...] = jnp.zeros_like(dg_ref)
            for cp in cps:
                cp.wait()

        gain_v = g_ref[...]
        xn, r = _rms(x_ref[...])
        dh = _nn(dab_ref[...], wgu[...])
        dxn, dg = _rms_bwd(dh, xn, r, gain_v)
        dg_ref[...] += dg
        dx_ref[...] = do_ref[...] + dxn

    def tile(w):
        return pl.BlockSpec((TM, w), lambda i: (i, 0))

    row = pl.BlockSpec((1, D), lambda i: (0, 0))
    return _call(
        body, name=name, grid=(t // TM,),
        in_specs=[tile(D), row, tile(2 * F), tile(D), ANY], out_specs=[tile(D), row],
        out_shape=[jax.ShapeDtypeStruct((t, D), F32), jax.ShapeDtypeStruct((1, D), F32)],
        scratch_shapes=[pltpu.VMEM((2 * F, D), BF16), pltpu.SemaphoreType.DMA((2, N_DEV))],
        args=(x, gain, dab, dout, gw_gu), comms=comms)


def _weight_grad(a, b, name, col_off=0, m=None, comms=None, mats=1):
    t = a.shape[0]
    m = a.shape[1] if m is None else m
    n = b.shape[1]
    tmm = 512 if m % 512 == 0 else 256
    first = col_off // tmm

    def body(a_ref, b_ref, o_ref):
        o_ref[...] = _tn(a_ref[...], b_ref[...]).astype(BF16)

    (out,), got = _call(
        body, name=name, grid=(m // tmm,),
        in_specs=[pl.BlockSpec((t, tmm), lambda i: (0, first + i)), pl.BlockSpec((t, n), lambda i: (0, 0))],
        out_specs=[pl.BlockSpec((tmm, n), lambda i: (i, 0))],
        out_shape=[jax.ShapeDtypeStruct((m, n), BF16)], scratch_shapes=[], args=(a, b), comms=comms or ())
    out = out.reshape(mats * N_DEV, m // (mats * N_DEV), n)
    return out if comms is None else (out, got)


def _mix_proj_fwd(x, gain, b_gate, gw, comms=()):
    t = x.shape[0]

    def body(x_ref, g_ref, bg_ref, gw_ref, q_ref, k_ref, v_ref, zs_ref, gt_ref, h_ref, win, sems):
        @pl.when(pl.program_id(0) == 0)
        def _():
            for cp in _load_rows(gw_ref, win, OFF_IN, R_IN, sems):
                cp.wait()

        xn, _ = _rms(x_ref[...])
        h = (xn * g_ref[...]).astype(BF16)
        h_ref[...] = h
        z = _nt(h, win[...])
        q_ref[...] = (z[:, 0:D_ATT] * 0.125).astype(BF16)
        k_ref[...] = z[:, D_ATT:2 * D_ATT].astype(BF16)
        v_ref[...] = z[:, 2 * D_ATT:3 * D_ATT].astype(BF16)
        zs_ref[...] = z[:, 3 * D_ATT:3 * D_ATT + 2 * D_SGU].astype(BF16)
        gt_ref[...] = jax.nn.sigmoid(z[:, 3 * D_ATT + 2 * D_SGU:] + bg_ref[...]).astype(BF16)

    def tile(w):
        return pl.BlockSpec((TM, w), lambda i: (i, 0))

    return _call(
        body, name="mix_proj_fwd", grid=(t // TM,),
        in_specs=[tile(D), pl.BlockSpec((1, D), lambda i: (0, 0)), pl.BlockSpec((1, 2 * D), lambda i: (0, 0)), ANY],
        out_specs=[tile(D_ATT), tile(D_ATT), tile(D_ATT), tile(2 * D_SGU), tile(2 * D), tile(D)],
        out_shape=[jax.ShapeDtypeStruct((t, D_ATT), BF16)] * 3 + [jax.ShapeDtypeStruct((t, 2 * D_SGU), BF16),
                                                                   jax.ShapeDtypeStruct((t, 2 * D), BF16),
                                                                   jax.ShapeDtypeStruct((t, D), BF16)],
        scratch_shapes=[pltpu.VMEM((D_IN, D), BF16), pltpu.SemaphoreType.DMA((N_DEV,))],
        args=(x, gain, b_gate, gw), comms=comms)


def _mix_proj_bwd(dz, x, gain, dres, gw, comms=()):
    t = x.shape[0]

    def body(dz_ref, x_ref, g_ref, dr_ref, gw_ref, dx_ref, dg_ref, win, sems):
        @pl.when(pl.program_id(0) == 0)
        def _():
            cps = _load_rows(gw_ref, win, OFF_IN, R_IN, sems)
            dg_ref[...] = jnp.zeros_like(dg_ref)
            for cp in cps:
                cp.wait()

        dh = _nn(dz_ref[...], win[...])
        xn, r = _rms(x_ref[...])
        dxn, dg = _rms_bwd(dh, xn, r, g_ref[...])
        dg_ref[...] += dg
        dx_ref[...] = dr_ref[...] + dxn

    def tile(w):
        return pl.BlockSpec((TM, w), lambda i: (i, 0))

    row = pl.BlockSpec((1, D), lambda i: (0, 0))
    return _call(
        body, name="mix_proj_bwd", grid=(t // TM,),
        in_specs=[tile(D_IN), tile(D), row, tile(D), ANY], out_specs=[tile(D), row],
        out_shape=[jax.ShapeDtypeStruct((t, D), F32), jax.ShapeDtypeStruct((1, D), F32)],
        scratch_shapes=[pltpu.VMEM((D_IN, D), BF16), pltpu.SemaphoreType.DMA((N_DEV,))],
        args=(dz, x, gain, dres, gw), comms=comms)


SKEW_W = KW + QB
N_CAP = 2 * QB - REL_CLIP + 1


def _band_bias(rel_bias):
    cap = rel_bias[:, 2 * REL_CLIP:]
    diag = jnp.concatenate([jnp.broadcast_to(cap, (HEADS, N_CAP)), rel_bias[:, 2 * REL_CLIP - 1::-1],
                            jnp.broadcast_to(cap, (HEADS, SKEW_W - N_CAP - 2 * REL_CLIP))], axis=1)

    def body(d_ref, o_ref):
        lag = lax.broadcasted_iota(jnp.int32, (QB, KW), 1) // CHUNK - lax.broadcasted_iota(jnp.int32, (QB, KW), 0) // CHUNK
        band = (lag >= 0) & (lag <= N_LEFT)
        for h in range(HEADS):
            rows = jnp.broadcast_to(d_ref[h:h + 1, :], (QB, SKEW_W))
            o_ref[h] = jnp.where(band, pltpu.roll(rows, 0, 1, stride=1, stride_axis=0)[:, :KW], NEG_INF)

    return pl.pallas_call(body, name="band_bias", out_shape=jax.ShapeDtypeStruct((HEADS, QB, KW), F32))(diag)


def _att_specs():
    qspec = pl.BlockSpec((QB, D_ATT), lambda g: (g, 0))
    kspecs = [pl.BlockSpec((QB, D_ATT), lambda g: (jnp.maximum(g - 2, 0), 0)),
              pl.BlockSpec((QB, D_ATT), lambda g: (jnp.maximum(g - 1, 0), 0)), qspec]
    bspec = pl.BlockSpec((HEADS, QB, KW), lambda g: (0, 0, 0))
    return qspec, kspecs, bspec


def _att_probs(qm, kp, bias, valid):
    s = jnp.where(valid, _nt(qm, kp) + bias, NEG_INF)
    e = jnp.exp(s - jnp.max(s, axis=-1, keepdims=True))
    return e / jnp.sum(e, axis=-1, keepdims=True)


def _att_valid():
    g = pl.program_id(0)
    blk = lax.broadcasted_iota(jnp.int32, (QB, KW), 1) // QB
    return (blk + g) >= 2


def _att_fwd(q, k, v, bias, comms=()):
    t = q.shape[0]

    def body(q_ref, k0, k1, k2, v0, v1, v2, b_ref, y_ref):
        valid = _att_valid()
        first = lax.broadcasted_iota(jnp.int32, (1, 128), 1) < 64
        for p in range(HEADS // 2):
            lanes = slice(p * 128, (p + 1) * 128)
            qp = q_ref[:, lanes]
            kp = jnp.concatenate([k0[:, lanes], k1[:, lanes], k2[:, lanes]], axis=0)
            vp = jnp.concatenate([v0[:, lanes], v1[:, lanes], v2[:, lanes]], axis=0)
            out = jnp.zeros((QB, 128), F32)
            for hh in range(2):
                mask = first if hh == 0 else jnp.logical_not(first)
                pr = _att_probs(jnp.where(mask, qp, 0), kp, b_ref[2 * p + hh], valid)
                out = out + _nn(pr.astype(BF16), jnp.where(mask, vp, 0))
            y_ref[:, lanes] = out.astype(BF16)

    qspec, kspecs, bspec = _att_specs()
    (out,), got = _call(
        body, name="att_fwd", grid=(t // QB,),
        in_specs=[qspec] + kspecs + kspecs + [bspec], out_specs=[qspec],
        out_shape=[jax.ShapeDtypeStruct((t, D_ATT), BF16)], scratch_shapes=[],
        args=(q, k, k, k, v, v, v, bias), comms=comms)
    return out, got


def _att_bwd(q, k, v, bias, dy, comms=()):
    t = q.shape[0]
    n_blocks = t // QB

    def body(q_ref, k0, k1, k2, v0, v1, v2, b_ref, dy_ref, dq_ref, dk_ref, dv_ref, db_ref, dk_acc, dv_acc):
        g = pl.program_id(0)

        @pl.when(g == 0)
        def _():
            db_ref[...] = jnp.zeros_like(db_ref)
            dk_acc[...] = jnp.zeros_like(dk_acc)
            dv_acc[...] = jnp.zeros_like(dv_acc)

        valid = _att_valid()
        first = lax.broadcasted_iota(jnp.int32, (1, 128), 1) < 64
        for p in range(HEADS // 2):
            lanes = slice(p * 128, (p + 1) * 128)
            qp = q_ref[:, lanes]
            dyp = dy_ref[:, lanes]
            kp = jnp.concatenate([k0[:, lanes], k1[:, lanes], k2[:, lanes]], axis=0)
            vp = jnp.concatenate([v0[:, lanes], v1[:, lanes], v2[:, lanes]], axis=0)
            dq = jnp.zeros((QB, 128), F32)
            dk = jnp.zeros((KW, 128), F32)
            dv = jnp.zeros((KW, 128), F32)
            for hh in range(2):
                mask = first if hh == 0 else jnp.logical_not(first)
                qm = jnp.where(mask, qp, 0)
                dym = jnp.where(mask, dyp, 0)
                pr = _att_probs(qm, kp, b_ref[2 * p + hh], valid)
                dp = _nt(dym, vp)
                ds = pr * (dp - jnp.sum(dp * pr, axis=-1, keepdims=True))
                db_ref[2 * p + hh] += ds
                dsb = ds.astype(BF16)
                dq = dq + _nn(dsb, jnp.where(mask, kp, 0))
                dk = dk + _tn(dsb, qm)
                dv = dv + _tn(pr.astype(BF16), dym)
            dq_ref[:, lanes] = (dq * 0.125).astype(BF16)
            for j in range(3):
                rows = pl.ds(pl.multiple_of(jnp.maximum(g - 2 + j, 0) * QB, QB), QB)
                dk_acc[rows, lanes] += dk[j * QB:(j + 1) * QB]
                dv_acc[rows, lanes] += dv[j * QB:(j + 1) * QB]

        @pl.when(g == n_blocks - 1)
        def _():
            dk_ref[...] = dk_acc[...].astype(BF16)
            dv_ref[...] = dv_acc[...].astype(BF16)

    qspec, kspecs, bspec = _att_specs()
    full = pl.BlockSpec((t, D_ATT), lambda g: (0, 0))
    return _call(
        body, name="att_bwd", grid=(n_blocks,),
        in_specs=[qspec] + kspecs + kspecs + [bspec, qspec], out_specs=[qspec, full, full, bspec],
        out_shape=[jax.ShapeDtypeStruct((t, D_ATT), BF16)] * 3 + [jax.ShapeDtypeStruct((HEADS, QB, KW), F32)],
        scratch_shapes=[pltpu.VMEM((t, D_ATT), F32)] * 2,
        args=(q, k, k, k, v, v, v, bias, dy), comms=comms)


def _rel_bias_grad(dbias):
    def body(db_ref, cs_ref, tot_ref):
        lane = lax.broadcasted_iota(jnp.int32, (1, SKEW_W), 1)
        capped = (lane < N_CAP) | (lane > KW)
        pad = jnp.zeros((8, QB), F32)
        for h in range(HEADS):
            z = jnp.concatenate([db_ref[h, 0:8, :], pad], axis=1)
            for a in range(1, QB // 8):
                z = z + pltpu.roll(jnp.concatenate([db_ref[h, 8 * a:8 * a + 8, :], pad], axis=1), SKEW_W - 8 * a, 1)
            cs = z[0:1, :]
            for b in range(1, 8):
                cs = cs + pltpu.roll(z[b:b + 1, :], SKEW_W - b, 1)
            cs_ref[h:h + 1, :] = cs
            tot_ref[h:h + 1, :] = jnp.broadcast_to(jnp.sum(jnp.where(capped, cs, 0.0), axis=1, keepdims=True), (1, 128))

    cs, tot = pl.pallas_call(
        body, name="rel_bias_grad",
        out_shape=[jax.ShapeDtypeStruct((HEADS, SKEW_W), F32), jax.ShapeDtypeStruct((HEADS, 128), F32)],
    )(dbias)
    return jnp.concatenate([cs[:, KW:N_CAP - 1:-1], tot[:, :1]], axis=1)


def _sgu_mask():
    pos = np.arange(SGU_BLOCK)
    return (pos[:, None] // CHUNK) >= (pos[None, :] // CHUNK)


def _group_stack(blk, first):
    return jnp.concatenate([jnp.where(first, blk, 0), jnp.where(first, 0, blk)], axis=0)


def _sgu_norm(zs_ref, lng, lnb):
    zs = zs_ref[...].astype(F32)
    ga, th = _gelu(zs)
    u, vs = ga[:, :D_SGU], ga[:, D_SGU:]
    mu = jnp.mean(vs, axis=-1, keepdims=True)
    cen = vs - mu
    rstd = lax.rsqrt(jnp.mean(cen * cen, axis=-1, keepdims=True) + EPS)
    xhat = cen * rstd
    return zs, th, u, xhat, rstd, xhat * lng + lnb


def _sgu_mix(vb, wm2_ref, bsx, s_ref):
    first = lax.broadcasted_iota(jnp.int32, (1, 128), 1) < 64
    for n in range(TM // SGU_BLOCK):
        for p in range(4):
            blk = vb[n * 128:(n + 1) * 128, p * 128:(p + 1) * 128]
            s_ref[n * 128:(n + 1) * 128, p * 128:(p + 1) * 128] = _nn(wm2_ref[p], _group_stack(blk, first)) + bsx[:, p * 128:(p + 1) * 128]


def _merge_fwd(x, zs, gt, y_att, lng, lnb, wm2, bsx, gw):
    t = x.shape[0]

    def body(x_ref, zs_ref, gt_ref, ya_ref, lng_ref, lnb_ref, wm2_ref, bsx_ref, gw_ref, xo_ref, ys_ref, mg_ref,
             wbr, wo, s_scr, sems):
        @pl.when(pl.program_id(0) == 0)
        def _():
            for cp in _load_rows(gw_ref, wbr, OFF_BR, R_BR, sems.at[0]) + _load_rows(gw_ref, wo, OFF_WO, R_WO, sems.at[1]):
                cp.wait()

        _, _, u, _, _, vsn = _sgu_norm(zs_ref, lng_ref[...], lnb_ref[...])
        _sgu_mix(vsn.astype(BF16), wm2_ref, bsx_ref[...], s_scr)
        ys = (u * s_scr[...]).astype(BF16)
        ys_ref[...] = ys
        pa = _nt(ya_ref[...], wbr[:, :D_ATT])
        ps = _nt(ys, wbr[:, D_ATT:])
        mg = (gt_ref[:, :D].astype(F32) * pa + gt_ref[:, D:].astype(F32) * ps).astype(BF16)
        mg_ref[...] = mg
        xo_ref[...] = x_ref[...] + _nn(mg, wo[...])

    def tile(w):
        return pl.BlockSpec((TM, w), lambda i: (i, 0))

    def const(shape):
        return pl.BlockSpec(shape, lambda i: (0,) * len(shape))

    return pl.pallas_call(
        body, name="merge_fwd", grid=(t // TM,),
        in_specs=[tile(D), tile(2 * D_SGU), tile(2 * D), tile(D_ATT), const((1, D_SGU)), const((1, D_SGU)),
                  const((4, 128, 256)), const((128, D_SGU)), ANY],
        out_specs=[tile(D), tile(D_SGU), tile(D)],
        out_shape=[jax.ShapeDtypeStruct((t, D), F32), jax.ShapeDtypeStruct((t, D_SGU), BF16), jax.ShapeDtypeStruct((t, D), BF16)],
        scratch_shapes=[pltpu.VMEM((D, D), BF16), pltpu.VMEM((D, D), BF16), pltpu.VMEM((TM, D_SGU), F32),
                        pltpu.SemaphoreType.DMA((2, N_DEV))],
        compiler_params=_cparams(("arbitrary",)),
    )(x, zs, gt, y_att, lng, lnb, wm2, bsx, gw)


def _merge_bwd(dx, gt, y_att, y_sgu, gw):
    t = dx.shape[0]

    def body(dx_ref, gt_ref, ya_ref, ys_ref, gw_ref, dzg_ref, dya_ref, dys_ref, dpp_ref, dxb_ref, dbg_ref, wbr, wo, sems):
        @pl.when(pl.program_id(0) == 0)
        def _():
            cps = _load_rows(gw_ref, wbr, OFF_BR, R_BR, sems.at[0]) + _load_rows(gw_ref, wo, OFF_WO, R_WO, sems.at[1])
            dbg_ref[...] = jnp.zeros_like(dbg_ref)
            for cp in cps:
                cp.wait()

        dxb = dx_ref[...].astype(BF16)
        dxb_ref[...] = dxb
        dm = _nt(dxb, wo[...])
        for half, y_ref, w in ((0, ya_ref, wbr.at[:, :D_ATT]), (1, ys_ref, wbr.at[:, D_ATT:])):
            cols = slice(half * D, (half + 1) * D)
            gate = gt_ref[:, cols].astype(F32)
            branch = _nt(y_ref[...], w[...])
            dzg = dm * branch * gate * (1.0 - gate)
            dbg_ref[:, cols] += jnp.sum(dzg, axis=0, keepdims=True)
            dzg_ref[:, cols] = dzg.astype(BF16)
            dbr = (dm * gate).astype(BF16)
            dpp_ref[:, cols] = dbr
            dy = _nn(dbr, w[...])
            if half == 0:
                dya_ref[...] = dy.astype(BF16)
            else:
                dys_ref[...] = dy

    def tile(w):
        return pl.BlockSpec((TM, w), lambda i: (i, 0))

    return pl.pallas_call(
        body, name="merge_bwd", grid=(t // TM,),
        in_specs=[tile(D), tile(2 * D), tile(D_ATT), tile(D_SGU), ANY],
        out_specs=[tile(2 * D), tile(D_ATT), tile(D_SGU), tile(2 * D), tile(D), pl.BlockSpec((1, 2 * D), lambda i: (0, 0))],
        out_shape=[jax.ShapeDtypeStruct((t, 2 * D), BF16), jax.ShapeDtypeStruct((t, D_ATT), BF16), jax.ShapeDtypeStruct((t, D_SGU), F32),
                   jax.ShapeDtypeStruct((t, 2 * D), BF16), jax.ShapeDtypeStruct((t, D), BF16), jax.ShapeDtypeStruct((1, 2 * D), F32)],
        scratch_shapes=[pltpu.VMEM((D, D), BF16), pltpu.VMEM((D, D), BF16), pltpu.SemaphoreType.DMA((2, N_DEV))],
        compiler_params=_cparams(("arbitrary",)),
    )(dx, gt, y_att, y_sgu, gw)


def _sgu_bwd(zs, dys, lng, lnb, wm2, wmt2, bsx, comms=()):
    t = zs.shape[0]
    n_steps = t // TM

    def body(zs_ref, dys_ref, lng_ref, lnb_ref, wm2_ref, wmt2_ref, bsx_ref, dzs_ref, dw_ref, dbs_ref, dlg_ref, dlb_ref,
             s_scr, dv_scr, ds_acc):
        i = pl.program_id(0)

        @pl.when(i == 0)
        def _():
            dw_ref[...] = jnp.zeros_like(dw_ref)
            dlg_ref[...] = jnp.zeros_like(dlg_ref)
            dlb_ref[...] = jnp.zeros_like(dlb_ref)
            ds_acc[...] = jnp.zeros_like(ds_acc)

        lng = lng_ref[...]
        zs, th, u, xhat, rstd, vsn = _sgu_norm(zs_ref, lng, lnb_ref[...])
        vb = vsn.astype(BF16)
        _sgu_mix(vb, wm2_ref, bsx_ref[...], s_scr)
        dys = dys_ref[...]
        du = dys * s_scr[...]
        ds = dys * u
        dsb = ds.astype(BF16)
        first = lax.broadcasted_iota(jnp.int32, (1, 128), 1) < 64
        acc = jnp.zeros((SGU_BLOCK, D_SGU), F32)
        for n in range(TM // SGU_BLOCK):
            rows = slice(n * 128, (n + 1) * 128)
            acc = acc + ds[rows]
            for p in range(4):
                lanes = slice(p * 128, (p + 1) * 128)
                stack = _group_stack(dsb[rows, lanes], first)
                dv_scr[rows, lanes] = _nn(wmt2_ref[p], stack)
                dw_ref[p] += _nt(stack, vb[rows, lanes])
        ds_acc[...] += acc
        dvsn = dv_scr[...]
        dlg_ref[...] += jnp.sum(dvsn * xhat, axis=0, keepdims=True)
        dlb_ref[...] += jnp.sum(dvsn, axis=0, keepdims=True)
        dxh = dvsn * lng
        dvs = rstd * (dxh - jnp.mean(dxh, axis=-1, keepdims=True) - xhat * jnp.mean(dxh * xhat, axis=-1, keepdims=True))
        dga = jnp.concatenate([du, dvs], axis=1)
        dzs_ref[...] = (dga * _gelu_grad(zs, th)).astype(BF16)

        @pl.when(i == n_steps - 1)
        def _():
            r = lax.broadcasted_iota(jnp.int32, (256, 128), 0) % SGU_BLOCK
            c = lax.broadcasted_iota(jnp.int32, (256, 128), 1)
            keep = (r // CHUNK) >= (c // CHUNK)
            for p in range(4):
                dw_ref[p] = jnp.where(keep, dw_ref[p], 0.0)
            total = ds_acc[...]
            grp = lax.broadcasted_iota(jnp.int32, (SGU_BLOCK, D_SGU), 1) // 64
            lane = lax.broadcasted_iota(jnp.int32, (SGU_BLOCK, 128), 1)
            out = jnp.zeros((SGU_BLOCK, 128), F32)
            for gi in range(8):
                out = jnp.where(lane == gi, jnp.sum(jnp.where(grp == gi, total, 0.0), axis=1, keepdims=True), out)
            dbs_ref[...] = out

    def tile(w):
        return pl.BlockSpec((TM, w), lambda i: (i, 0))

    def const(shape):
        return pl.BlockSpec(shape, lambda i: (0,) * len(shape))

    return _call(
        body, name="sgu_bwd", grid=(n_steps,),
        in_specs=[tile(2 * D_SGU), tile(D_SGU), const((1, D_SGU)), const((1, D_SGU)), const((4, 128, 256)), const((4, 128, 256)),
                  const((128, D_SGU))],
        out_specs=[tile(2 * D_SGU), const((4, 256, 128)), const((128, 128)), const((1, D_SGU)), const((1, D_SGU))],
        out_shape=[jax.ShapeDtypeStruct((t, 2 * D_SGU), BF16), jax.ShapeDtypeStruct((4, 256, 128), F32),
                   jax.ShapeDtypeStruct((128, 128), F32), jax.ShapeDtypeStruct((1, D_SGU), F32), jax.ShapeDtypeStruct((1, D_SGU), F32)],
        scratch_shapes=[pltpu.VMEM((TM, D_SGU), F32), pltpu.VMEM((TM, D_SGU), F32), pltpu.VMEM((SGU_BLOCK, D_SGU), F32)],
        args=(zs, dys, lng, lnb, wm2, wmt2, bsx), comms=comms)


def _adamw(g, w, m, v):
    m = ADAM_B1 * m + (1.0 - ADAM_B1) * g
    v = ADAM_B2 * v + (1.0 - ADAM_B2) * (g * g)
    m_hat = m / (1.0 - ADAM_B1 ** ADAM_STEP)
    v_hat = v / (1.0 - ADAM_B2 ** ADAM_STEP)
    return -ADAM_LR * (m_hat / (jnp.sqrt(v_hat) + ADAM_EPS) + ADAM_WD * w), m, v


def _adamw_matrix(parts, sums, w, m, v, transposed, name):
    _, r, c = parts.shape
    tc = 256

    def body(own_ref, p_ref, s_ref, w_ref, m_ref, v_ref, g_ref, d_ref, mo_ref, vo_ref):
        g = p_ref[0].astype(F32) + p_ref[1].astype(F32) + p_ref[2].astype(F32) + s_ref[...].astype(F32)
        g = g.T if transposed else g
        g_ref[...] = g
        d_ref[...], mo_ref[...], vo_ref[...] = _adamw(g, w_ref[...], m_ref[...], v_ref[...])

    own = pl.BlockSpec((None, tc, r), lambda i, o: (0, i, 0)) if transposed else pl.BlockSpec((None, r, tc), lambda i, o: (0, 0, i))
    return pl.pallas_call(
        body, name=name,
        grid_spec=pltpu.PrefetchScalarGridSpec(
            num_scalar_prefetch=1, grid=(c // tc,),
            in_specs=[pl.BlockSpec((3, r, tc), lambda i, o: (0, 0, i)), pl.BlockSpec((None, r, tc), lambda i, o: (o[0], 0, i)), own, own, own],
            out_specs=[own] * 4),
        out_shape=[jax.ShapeDtypeStruct(w.shape, F32)] * 4,
        compiler_params=_cparams(("arbitrary",)),
    )(_my_index(("x", 2), ("y", 1)), parts, sums, w, m, v)


_SMALL_2D = {"norm_ffn1": (1, D), "norm_mix": (1, D), "norm_ffn2": (1, D), "norm_final": (1, D), "b_gate": (1, 2 * D),
             "sgu_ln_g": (1, D_SGU), "sgu_ln_b": (1, D_SGU), "sgu_b_s": (8, SGU_BLOCK), "rel_bias": (HEADS, N_REL),
             "sgu_w_s": (8 * SGU_BLOCK, SGU_BLOCK)}


def _adamw_small(parts, loss_parts, p):
    names = list(parts)
    n = len(names)

    def body(*refs):
        got, loss_got, wmv, outs, loss_out = refs[:n], refs[n], refs[n + 1:4 * n + 1], refs[4 * n + 1:8 * n + 1], refs[8 * n + 1]
        for i, name in enumerate(names):
            g = got[i][0]
            for k in range(1, N_DEV):
                g = g + got[i][k]
            if name == "sgu_b_s":
                g = g.T[0:8, :]
            res = (g,) + _adamw(g, wmv[3 * i][...], wmv[3 * i + 1][...], wmv[3 * i + 2][...])
            for o_ref, val in zip(outs[4 * i:4 * i + 4], res):
                o_ref[...] = val
        total = loss_got[0]
        for k in range(1, N_DEV):
            total = total + loss_got[k]
        loss_out[...] = total

    wmv = [p[pre + name].reshape(_SMALL_2D[name]) for name in names for pre in ("", "m_", "v_")]
    res = pl.pallas_call(
        body, name="adamw_small",
        out_shape=[jax.ShapeDtypeStruct(_SMALL_2D[name], F32) for name in names for _ in range(4)] + [jax.ShapeDtypeStruct((1, 128), F32)],
        compiler_params=_cparams())(*[parts[name] for name in names], loss_parts, *wmv)
    return [{name: res[4 * i + j].reshape(p[name].shape) for i, name in enumerate(names)} for j in range(4)], res[-1]


def _pack_rows(groups, name):
    flat = [a for grp in groups for a, _ in grp]
    rows = [grp[0][0].shape[2] if grp[0][1] else grp[0][0].shape[1] for grp in groups]

    def body(*refs):
        o_ref, pos, off = refs[-1], 0, 0
        for grp, r in zip(groups, rows):
            vals = []
            for _, transposed in grp:
                val = refs[pos][0]
                vals.append(val.T if transposed else val)
                pos += 1
            o_ref[off:off + r, :] = (vals[0] if len(vals) == 1 else jnp.concatenate(vals, axis=1)).astype(BF16)
            off += r

    return pl.pallas_call(body, name=name, out_shape=jax.ShapeDtypeStruct((sum(rows), D), BF16), compiler_params=_cparams())(*flat)


def _step(x, target, p):
    n1, nm, n2 = p["norm_ffn1"], p["norm_mix"], p["norm_ffn2"]
    nf = p["norm_final"].reshape(1, D)
    lng, lnb = p["sgu_ln_g"], p["sgu_ln_b"]
    w_m = jnp.where(jnp.asarray(_sgu_mask())[None], p["sgu_w_s"][0], 0.0).astype(BF16)
    wm2 = jnp.concatenate([w_m[0::2], w_m[1::2]], axis=2)
    w_mt = w_m.transpose(0, 2, 1)
    wmt2 = jnp.concatenate([w_mt[0::2], w_mt[1::2]], axis=2)
    bsx = jnp.repeat(p["sgu_b_s"][0].T, 64, axis=1)
    bias = _band_bias(p["rel_bias"][0])

    def chip_sums(grads, name):
        gots = _comm_only([_SiblingSwap(grads)], "swap_" + name)
        return _pair_sums(grads, gots, "pair_sums_" + name)
    def as_rows(a):
        return jnp.swapaxes(a, 1, 2)

    def updates(parts, sums, names):
        res = {}
        for pt, sm, n in zip(parts, sums, names):
            if p[n].shape[1:] == pt.shape[1:]:
                res[n] = _adamw_matrix(pt, sm, p[n], p["m_" + n], p["v_" + n], False, "adamw_" + n)
            elif p[n].shape[2] > 128:
                res[n] = [as_rows(o) for o in _adamw_matrix(pt, sm, as_rows(p[n]), as_rows(p["m_" + n]), as_rows(p["v_" + n]), False,
                                                            "adamw_" + n)]
            else:
                res[n] = _adamw_matrix(pt, sm, p[n], p["m_" + n], p["v_" + n], True, "adamw_" + n)
        return res

    rows1 = _pack_rows([[(as_rows(p["ffn1_w_gate"]), False)], [(as_rows(p["ffn1_w_up"]), False)], [(p["ffn1_w_down"], False)]], "pack_ffn1")
    rows_m = _pack_rows([[(as_rows(p["w_in"]), False)], [(p["w_branch_att"], True), (p["w_branch_sgu"], True)], [(p["w_out"], False)]],
                        "pack_mixer")
    rows2d = _pack_rows([[(p["ffn2_w_down"], False)]], "pack_ffn2_down")
    rows2gu = _pack_rows([[(as_rows(p["ffn2_w_gate"]), False)], [(as_rows(p["ffn2_w_up"]), False)]], "pack_ffn2_gate_up")
    (gw1,) = _comm_only([_Gather(rows1)], "gather_ffn1")
    x1, ab1, h1, (gwm,) = _ffn_fwd(x, n1, gw1, gw1, 2 * R_FF, "ffn1_fwd", [_Gather(rows_m)])
    (q, k, v, zs, gt, h2), (gw2d,) = _mix_proj_fwd(x1, nm, p["b_gate"], gwm, [_Gather(rows2d)])
    y_att, (gw2gu,) = _att_fwd(q, k, v, bias, [_Gather(rows2gu)])
    x2, y_sgu, merged = _merge_fwd(x1, zs, gt, y_att, lng, lnb, wm2, bsx, gwm)
    dx3, ab2, hb, _, d_nf, loss = _ffn_fwd(x2, n2, gw2gu, gw2d, 0, "ffn2_fwd", head=(nf, target))

    (dab, g_down), _ = _ffn_bwd_hidden(ab2, dx3, gw2d, 0, "ffn2_bwd_hidden")
    (dx2, d_n2), _ = _ffn_bwd_input(x2, n2, dab, dx3, gw2gu, "ffn2_bwd")
    g_gu = _weight_grad(dab, hb, "ffn2_dw_gate_up", mats=2)
    dzg, dya, dys, dpp, dxb, d_bg = _merge_bwd(dx2, gt, y_att, y_sgu, gwm)
    g_late = [(g_gu, 0), (g_gu, 1), (g_down, 0), (_weight_grad(dpp, y_att, "dw_branch_att", 0, D), 0),
              (_weight_grad(dpp, y_sgu, "dw_branch_sgu", D, D), 0), (_weight_grad(merged, dxb, "dw_out"), 0)]
    late = ("ffn2_w_gate", "ffn2_w_up", "ffn2_w_down", "w_branch_att", "w_branch_sgu", "w_out")
    (dzs, d_wm, d_bs, d_lng, d_lnb), gots_late = _sgu_bwd(zs, dys, lng, lnb, wm2, wmt2, bsx, [_SiblingSwap(g_late)])
    sums_late = _pair_sums(g_late, gots_late, "pair_sums_late")
    (dq, dk, dv, d_bias), parts_late = _att_bwd(q, k, v, bias, dya, [_ChipScatter(sums_late)])
    big = updates(parts_late, sums_late, late)
    d_rel = _rel_bias_grad(d_bias)
    dz = jnp.concatenate([dq, dk, dv, dzs, dzg], axis=1)
    sums_in = chip_sums([(_weight_grad(dz, h2, "dw_in"), 0)], "w_in")
    (dx1, d_nm), parts_in = _mix_proj_bwd(dz, x1, nm, dx2, gwm, [_ChipScatter(sums_in)])
    big.update(updates(parts_in, sums_in, ("w_in",)))
    small = {"norm_ffn2": d_n2, "norm_final": d_nf, "b_gate": d_bg, "sgu_ln_g": d_lng, "sgu_ln_b": d_lnb, "sgu_b_s": d_bs,
             "rel_bias": d_rel, "sgu_w_s": d_wm.reshape(_SMALL_2D["sgu_w_s"]), "norm_mix": d_nm}

    (dab, g_down), (*small_parts, loss_parts) = _ffn_bwd_hidden(ab1, dx1, gw1, 2 * R_FF, "ffn1_bwd_hidden",
                                                                [_AllToAll(list(small.values()) + [loss])])
    sums_d = chip_sums([(g_down, 0)], "ffn1_down")
    g_gu, (parts_d,) = _weight_grad(dab, h1, "ffn1_dw_gate_up", comms=[_ChipScatter(sums_d)], mats=2)
    sums_gu = chip_sums([(g_gu, 0), (g_gu, 1)], "ffn1_gate_up")
    (dx0, d_n1), parts_gu = _ffn_bwd_input(x, n1, dab, dx1, gw1, "ffn1_bwd", [_ChipScatter(sums_gu)])
    (n1_parts,) = _comm_only([_AllToAll([d_n1])], "gather_norm_ffn1")
    big.update(updates([parts_d] + parts_gu, sums_d + sums_gu, ("ffn1_w_down", "ffn1_w_gate", "ffn1_w_up")))
    out_s, loss_sum = _adamw_small(dict(zip(small, small_parts), norm_ffn1=n1_parts), loss_parts, p)
    return dx0, loss_sum[0, 0], [{**{n: four[i] for n, four in big.items()}, **s} for i, s in enumerate(out_s)]


_OUT_ORDER = ("norm_ffn1", "ffn1_w_gate", "ffn1_w_up", "ffn1_w_down", "norm_mix", "w_in", "b_gate", "rel_bias", "sgu_ln_g", "sgu_ln_b",
              "sgu_w_s", "sgu_b_s", "w_branch_att", "w_branch_sgu", "w_out", "norm_ffn2", "ffn2_w_gate", "ffn2_w_up", "ffn2_w_down",
              "norm_final")


def kernel(x, norm_ffn1, ffn1_w_gate, ffn1_w_up, ffn1_w_down, norm_mix, w_in, b_gate, rel_bias, sgu_ln_g, sgu_ln_b, sgu_w_s, sgu_b_s, w_branch_att, w_branch_sgu, w_out, norm_ffn2, ffn2_w_gate, ffn2_w_up, ffn2_w_down, norm_final, loss_target, m_norm_ffn1, m_ffn1_w_gate, m_ffn1_w_up, m_ffn1_w_down, m_norm_mix, m_w_in, m_b_gate, m_rel_bias, m_sgu_ln_g, m_sgu_ln_b, m_sgu_w_s, m_sgu_b_s, m_w_branch_att, m_w_branch_sgu, m_w_out, m_norm_ffn2, m_ffn2_w_gate, m_ffn2_w_up, m_ffn2_w_down, m_norm_final, v_norm_ffn1, v_ffn1_w_gate, v_ffn1_w_up, v_ffn1_w_down, v_norm_mix, v_w_in, v_b_gate, v_rel_bias, v_sgu_ln_g, v_sgu_ln_b, v_sgu_w_s, v_sgu_b_s, v_w_branch_att, v_w_branch_sgu, v_w_out, v_norm_ffn2, v_ffn2_w_gate, v_ffn2_w_up, v_ffn2_w_down, v_norm_final):
    args = dict(locals())
    dx, loss, outs = _step(x[0], loss_target[0], {pre + n: args[pre + n] for pre in ("", "m_", "v_") for n in _OUT_ORDER})
    return (loss, dx[None], *[o[n] for o in outs for n in _OUT_ORDER])
```

```python
import functools

import numpy as np
import jax
import jax.numpy as jnp
from jax import lax
from jax.experimental import pallas as pl
from jax.experimental.pallas import tpu as pltpu

F32 = jnp.float32
BF16 = jnp.bfloat16

N_DEV = 8
D = 1024
F = 2816
D_ATT = 512
D_SGU = 512
D_IN = 4608
HEADS = 8
CHUNK = 64
N_LEFT = 8
REL_CLIP = 256
N_REL = 2 * REL_CLIP + 1
SGU_BLOCK = 128
EPS = 1e-6
NEG_INF = -1e30
QB = 256
KW = 3 * QB

R_FF, R_IN, R_BR, R_WO = F // N_DEV, D_IN // N_DEV, D // N_DEV, D // N_DEV
OFF_IN, OFF_BR, OFF_WO = 0, R_IN, R_IN + R_BR

FC = 256
TM = 512
VMEM_LIMIT = 56 * 1024 * 1024

ADAM_LR, ADAM_B1, ADAM_B2, ADAM_EPS, ADAM_WD, ADAM_STEP = 0.001, 0.9, 0.999, 1e-08, 0.01, 10

MESH = pl.DeviceIdType.MESH
ANY = pl.BlockSpec(memory_space=pl.ANY)


def _nt(a, b):
    return lax.dot_general(a, b, (((1,), (1,)), ((), ())), preferred_element_type=F32)


def _tn(a, b):
    return lax.dot_general(a, b, (((0,), (0,)), ((), ())), preferred_element_type=F32)


def _nn(a, b):
    return jnp.dot(a, b, preferred_element_type=F32)


def _cparams(sem=None):
    return pltpu.CompilerParams(dimension_semantics=sem, vmem_limit_bytes=VMEM_LIMIT)


def _load_rows(gw_ref, dst, off, rows, sems):
    copies = [pltpu.make_async_copy(gw_ref.at[k, pl.ds(off, rows), :], dst.at[pl.ds(k * rows, rows), :], sems.at[k])
              for k in range(N_DEV)]
    for cp in copies:
        cp.start()
    return copies


def _rms(xv):
    r = lax.rsqrt(jnp.mean(xv * xv, axis=-1, keepdims=True) + EPS)
    return xv * r, r


def _rms_bwd(dh, xn, r, gain):
    dxn = dh * gain
    dx = r * (dxn - xn * jnp.mean(dxn * xn, axis=-1, keepdims=True))
    return dx, jnp.sum(dh * xn, axis=0, keepdims=True)


def _gelu(x):
    t = jnp.tanh(0.7978845608028654 * (x + 0.044715 * x * x * x))
    return 0.5 * x * (1.0 + t), t


def _gelu_grad(x, t):
    return 0.5 * (1.0 + t) + 0.5 * x * (1.0 - t * t) * 0.7978845608028654 * (1.0 + 3.0 * 0.044715 * x * x)


def _place():
    x, y, cc = lax.axis_index("x"), lax.axis_index("y"), lax.axis_index("c")
    return x, y, cc, [(1 - x, y), (x, 1 - y), (1 - x, 1 - y)]


class _Gather:
    def __init__(self, shard):
        self.inputs = [shard]
        self.out_shape = [jax.ShapeDtypeStruct((N_DEV,) + shard.shape, shard.dtype)]
        self.scratch = [pltpu.SemaphoreType.DMA((7,)), pltpu.SemaphoreType.DMA((7,)), pltpu.SemaphoreType.DMA]

    def _copies(self, ins, outs, scr):
        (x_ref,), (out_ref,), (send_sems, recv_sems, local_sem) = ins, outs, scr
        x, y, cc, chips = _place()

        def slab(px, py, pc):
            return out_ref.at[4 * px + 2 * py + pc]

        def copy(k, block, to, src=None):
            return pltpu.make_async_remote_copy(
                src_ref=slab(*block) if src is None else src, dst_ref=slab(*block),
                send_sem=send_sems.at[k], recv_sem=recv_sems.at[k], device_id=to, device_id_type=MESH)

        me, sibling = (x, y, cc), (x, y, 1 - cc)
        x_nbr, y_nbr, diagonal = chips
        mine = pltpu.make_async_copy(x_ref, slab(*me), local_sem)
        first = [copy(0, me, sibling, src=x_ref), copy(1, me, (*x_nbr, cc), src=x_ref), copy(2, me, (*y_nbr, cc), src=x_ref)]
        neighbours = [copy(1, (*x_nbr, cc), me), copy(2, (*y_nbr, cc), me)]
        second_hand = copy(3, (x ^ (1 - cc), y ^ cc, cc), (x ^ cc, y ^ (1 - cc), cc))
        from_diagonal = copy(3, (*diagonal, cc), me)
        passed = [copy(4 + j, (*chip, cc), sibling) for j, chip in enumerate(chips)]
        from_sibling = [copy(0, sibling, me)] + [copy(4 + j, (*chip, 1 - cc), me) for j, chip in enumerate(chips)]
        return mine, first, neighbours, second_hand, from_diagonal, passed, from_sibling

    def begin(self, *refs):
        mine, first = self._copies(*refs)[:2]
        mine.start()
        for cp in first:
            cp.start()

    def mid(self, *refs):
        _, _, neighbours, second_hand, _, passed, _ = self._copies(*refs)
        for cp in neighbours:
            cp.wait_recv()
        second_hand.start()
        passed[0].start()
        passed[1].start()

    def relay(self, *refs):
        _, _, _, _, from_diagonal, passed, _ = self._copies(*refs)
        from_diagonal.wait_recv()
        passed[2].start()

    def end(self, *refs):
        mine, first, _, second_hand, _, passed, from_sibling = self._copies(*refs)
        for cp in from_sibling:
            cp.wait_recv()
        for cp in first + [second_hand] + passed:
            cp.wait_send()
        mine.wait()


class _Direct:
    def begin(self, *refs):
        keep, give = self._copies(*refs)
        for cp in keep + give:
            cp.start()

    def mid(self, *refs):
        pass

    def relay(self, *refs):
        pass

    def end(self, *refs):
        keep, give = self._copies(*refs)
        for cp in give:
            cp.wait_recv()
        for cp in give:
            cp.wait_send()
        for cp in keep:
            cp.wait()


class _SiblingSwap(_Direct):
    def __init__(self, grads):
        n = len(grads)
        self.which = [w for _, w in grads]
        self.inputs = [g for g, _ in grads]
        self.out_shape = [jax.ShapeDtypeStruct((4,) + g.shape[1:], g.dtype) for g, _ in grads]
        self.scratch = [pltpu.SemaphoreType.DMA((n, 4)), pltpu.SemaphoreType.DMA((n, 4))]

    def _copies(self, ins, outs, scr):
        send_sems, recv_sems = scr
        x, y, cc, _ = _place()
        return [], [pltpu.make_async_remote_copy(src_ref=g_ref.at[N_DEV * w + 2 * j + 1 - cc], dst_ref=got_ref.at[j],
                                                 send_sem=send_sems.at[i, j], recv_sem=recv_sems.at[i, j], device_id=(x, y, 1 - cc),
                                                 device_id_type=MESH)
                    for i, (g_ref, got_ref, w) in enumerate(zip(ins, outs, self.which)) for j in range(4)]


class _ChipScatter(_Direct):
    def __init__(self, sums):
        n = len(sums)
        self.inputs = list(sums)
        self.out_shape = [jax.ShapeDtypeStruct((3,) + s.shape[1:], s.dtype) for s in sums]
        self.scratch = [pltpu.SemaphoreType.DMA((n, 3)), pltpu.SemaphoreType.DMA((n, 3))]

    def _copies(self, ins, outs, scr):
        send_sems, recv_sems = scr
        _, _, cc, chips = _place()
        return [], [pltpu.make_async_remote_copy(src_ref=s_ref.at[2 * px + py], dst_ref=got_ref.at[j], send_sem=send_sems.at[i, j],
                                                 recv_sem=recv_sems.at[i, j], device_id=(px, py, cc), device_id_type=MESH)
                    for i, (s_ref, got_ref) in enumerate(zip(ins, outs)) for j, (px, py) in enumerate(chips)]


class _AllToAll(_Direct):
    def __init__(self, blocks):
        n = len(blocks)
        self.inputs = list(blocks)
        self.out_shape = [jax.ShapeDtypeStruct((N_DEV,) + b.shape, b.dtype) for b in blocks]
        self.scratch = [pltpu.SemaphoreType.DMA((n, 7)), pltpu.SemaphoreType.DMA((n, 7)), pltpu.SemaphoreType.DMA((n,))]

    def _copies(self, ins, outs, scr):
        send_sems, recv_sems, local_sems = scr
        x, y, cc, _ = _place()
        me = 4 * x + 2 * y + cc
        keep = [pltpu.make_async_copy(b_ref, got_ref.at[me], local_sems.at[i]) for i, (b_ref, got_ref) in enumerate(zip(ins, outs))]
        give = [pltpu.make_async_remote_copy(src_ref=b_ref, dst_ref=got_ref.at[me], send_sem=send_sems.at[i, k - 1],
                                             recv_sem=recv_sems.at[i, k - 1],
                                             device_id=(x ^ ((k >> 2) & 1), y ^ ((k >> 1) & 1), cc ^ (k & 1)), device_id_type=MESH)
                for i, (b_ref, got_ref) in enumerate(zip(ins, outs)) for k in range(1, N_DEV)]
        return keep, give


def _split_refs(refs, counts):
    out, pos = [], 0
    for n in counts:
        out.append(list(refs[pos:pos + n]))
        pos += n
    return out


def _bind(comms, c_in, c_out, c_scr):
    ins = _split_refs(c_in, [len(c.inputs) for c in comms])
    outs = _split_refs(c_out, [len(c.out_shape) for c in comms])
    scr = _split_refs(c_scr, [len(c.scratch) for c in comms])
    return [(c, (i, o, s)) for c, i, o, s in zip(comms, ins, outs, scr)]


def _call(body, *, name, grid, in_specs, out_specs, out_shape, scratch_shapes, args, comms=()):
    c_in = [a for c in comms for a in c.inputs]
    c_out = [s for c in comms for s in c.out_shape]
    c_scr = [s for c in comms for s in c.scratch]
    counts = [len(in_specs), len(c_in), len(out_shape), len(c_out), len(scratch_shapes), len(c_scr)]

    def full(*refs):
        ins, cin, outs, cout, scr, cscr = _split_refs(refs, counts)
        bound = _bind(comms, cin, cout, cscr)
        if comms:
            def at(steps):
                return functools.reduce(jnp.logical_and, [pl.program_id(ax) == s for ax, s in enumerate(steps)])

            first, last = at([0] * len(grid)), at([n - 1 for n in grid])

            @pl.when(first)
            def _():
                for c, r in bound:
                    c.begin(*r)

            @pl.when(at([(grid[0] - 1) // 2] + [0] * (len(grid) - 1)))
            def _():
                for c, r in bound:
                    c.mid(*r)

            @pl.when(last)
            def _():
                for c, r in bound:
                    c.relay(*r)

        body(*ins, *outs, *scr)
        if comms:
            @pl.when(last)
            def _():
                for c, r in bound:
                    c.end(*r)

    res = pl.pallas_call(
        full, name=name, grid=grid,
        in_specs=list(in_specs) + [ANY] * len(c_in), out_specs=list(out_specs) + [ANY] * len(c_out),
        out_shape=list(out_shape) + c_out, scratch_shapes=list(scratch_shapes) + c_scr,
        compiler_params=_cparams(("arbitrary",) * len(grid)),
    )(*args, *c_in)
    return list(res[:len(out_shape)]), list(res[len(out_shape):])


def _comm_only(comms, name):
    c_in = [a for c in comms for a in c.inputs]
    c_out = [s for c in comms for s in c.out_shape]
    c_scr = [s for c in comms for s in c.scratch]

    def full(*refs):
        cin, cout, cscr = _split_refs(refs, [len(c_in), len(c_out), len(c_scr)])
        bound = _bind(comms, cin, cout, cscr)
        for phase in ("begin", "mid", "relay", "end"):
            for c, r in bound:
                getattr(c, phase)(*r)

    return list(pl.pallas_call(full, name=name, in_specs=[ANY] * len(c_in), out_specs=[ANY] * len(c_out), out_shape=c_out,
                               scratch_shapes=c_scr)(*c_in))


def _my_index(*axes_and_weights):
    return sum(w * lax.axis_index(a) for a, w in axes_and_weights).astype(jnp.int32).reshape(1)


def _pair_sums(grads, gots, name):
    n = len(grads)

    def body(c_ref, *refs):
        for a_ref, b_ref, o_ref in zip(refs[:n], refs[n:2 * n], refs[2 * n:]):
            o_ref[...] = (a_ref[...].astype(F32) + b_ref[...].astype(F32)).astype(BF16)

    def tile(g):
        return pl.BlockSpec((1,) + g.shape[1:], lambda j, c_ref: (j, 0, 0))

    def mine(g, w):
        return pl.BlockSpec((1, None) + g.shape[1:], lambda j, c_ref: (4 * w + j, c_ref[0], 0, 0))

    return list(pl.pallas_call(
        body, name=name,
        grid_spec=pltpu.PrefetchScalarGridSpec(num_scalar_prefetch=1, grid=(4,),
                                               in_specs=[mine(g, w) for g, w in grads] + [tile(g) for g in gots],
                                               out_specs=[tile(g) for g in gots]),
        out_shape=[jax.ShapeDtypeStruct(g.shape, BF16) for g in gots],
        compiler_params=_cparams(("arbitrary",)))(_my_index(("c", 1)), *[g.reshape((-1, 2) + g.shape[1:]) for g, _ in grads], *gots))


def _ffn_fwd(x, gain, gw_gu, gw_d, off_d, name, comms=(), head=None):
    t = x.shape[0]
    n_head = 0 if head is None else 2

    def body(x_ref, g_ref, gu_ref, d_ref, *refs):
        head_in, (o_ref, ab_ref, h_ref), head_out = refs[:n_head], refs[n_head:n_head + 3], refs[n_head + 3:2 * n_head + 3]
        wg, wu, wd, s_scr, sems = refs[2 * n_head + 3:]

        @pl.when(pl.program_id(0) == 0)
        def _():
            cps = _load_rows(gu_ref, wg, 0, R_FF, sems.at[0]) + _load_rows(gu_ref, wu, R_FF, R_FF, sems.at[1]) \
                + _load_rows(d_ref, wd, off_d, R_FF, sems.at[2])
            for o in head_out:
                o[...] = jnp.zeros_like(o)
            for cp in cps:
                cp.wait()

        xv = x_ref[...]
        xn, _ = _rms(xv)
        h = (xn * g_ref[...]).astype(BF16)
        h_ref[...] = h
        for c in range(F // FC):
            rows = pl.ds(c * FC, FC)
            a = _nt(h, wg[rows, :])
            b = _nt(h, wu[rows, :])
            ab_ref[:, c * FC:(c + 1) * FC] = a.astype(BF16)
            ab_ref[:, F + c * FC:F + (c + 1) * FC] = b.astype(BF16)
            s_scr[:, c * FC:(c + 1) * FC] = (a * jax.nn.sigmoid(a) * b).astype(BF16)
        out = xv + 0.5 * _nn(s_scr[...], wd[...])
        if head is None:
            o_ref[...] = out
        else:
            (gf_ref, t_ref), (dg_ref, loss_ref) = head_in, head_out
            gain_f = gf_ref[...]
            yn, r = _rms(out)
            err = yn * gain_f - t_ref[...]
            loss_ref[...] += 0.5 * jnp.sum(jnp.mean(err * err, axis=-1, keepdims=True), axis=0, keepdims=True)
            o_ref[...], dg = _rms_bwd(err * (1.0 / D), yn, r, gain_f)
            dg_ref[...] += dg

    tile = pl.BlockSpec((TM, D), lambda i: (i, 0))
    row = pl.BlockSpec((1, D), lambda i: (0, 0))
    head_specs = [] if head is None else [row, pl.BlockSpec((1, 128), lambda i: (0, 0))]
    head_shapes = [] if head is None else [jax.ShapeDtypeStruct((1, D), F32), jax.ShapeDtypeStruct((1, 128), F32)]
    res, got = _call(
        body, name=name, grid=(t // TM,),
        in_specs=[tile, row, ANY, ANY] + ([] if head is None else [row, tile]),
        out_specs=[tile, pl.BlockSpec((TM, 2 * F), lambda i: (i, 0)), tile] + head_specs,
        out_shape=[jax.ShapeDtypeStruct((t, D), F32), jax.ShapeDtypeStruct((t, 2 * F), BF16), jax.ShapeDtypeStruct((t, D), BF16)] + head_shapes,
        scratch_shapes=[pltpu.VMEM((F, D), BF16)] * 3 + [pltpu.VMEM((TM, F), BF16), pltpu.SemaphoreType.DMA((3, N_DEV))],
        args=(x, gain, gw_gu, gw_d) + (() if head is None else tuple(head)), comms=comms)
    return (res[0], res[1], res[2], got, *res[3:])


def _ffn_bwd_hidden(ab, dout, gw_d, off_d, name, comms=()):
    t = ab.shape[0]
    n_steps = t // TM
    row_chunks = [(r, min(512, F - r)) for r in range(0, F, 512)]

    def tile(w):
        return pl.BlockSpec((TM, w), lambda i: (i, 0))

    def hidden(ab_ref, do_ref, d_ref, dab_ref, dwd_ref, wd, s_scr, acc, sems, out_sem):
        step = pl.program_id(0)

        @pl.when(step == 0)
        def _():
            cps = _load_rows(d_ref, wd, off_d, R_FF, sems)
            acc[...] = jnp.zeros_like(acc)
            for cp in cps:
                cp.wait()

        df = (0.5 * do_ref[...]).astype(BF16)
        for c in range(F // FC):
            a = ab_ref[:, c * FC:(c + 1) * FC].astype(F32)
            b = ab_ref[:, F + c * FC:F + (c + 1) * FC].astype(F32)
            sg = jax.nn.sigmoid(a)
            sl = a * sg
            ds = _nt(df, wd[pl.ds(c * FC, FC), :])
            dab_ref[:, c * FC:(c + 1) * FC] = (ds * b * (sg * (1.0 + a * (1.0 - sg)))).astype(BF16)
            dab_ref[:, F + c * FC:F + (c + 1) * FC] = (ds * sl).astype(BF16)
            s_scr[:, c * FC:(c + 1) * FC] = (sl * b).astype(BF16)
        for r, n in row_chunks:
            acc[r:r + n, :] += _tn(s_scr[:, r:r + n], df)

        @pl.when(step == n_steps - 1)
        def _():
            wd[...] = acc[...].astype(BF16)
            out = pltpu.make_async_copy(wd, dwd_ref, out_sem)
            out.start()
            out.wait()

    (dab, dwd), got = _call(
        hidden, name=name, grid=(n_steps,),
        in_specs=[tile(2 * F), tile(D), ANY], out_specs=[tile(2 * F), ANY],
        out_shape=[jax.ShapeDtypeStruct((t, 2 * F), BF16), jax.ShapeDtypeStruct((F, D), BF16)],
        scratch_shapes=[pltpu.VMEM((F, D), BF16), pltpu.VMEM((TM, F), BF16), pltpu.VMEM((F, D), F32), pltpu.SemaphoreType.DMA((N_DEV,)),
                        pltpu.SemaphoreType.DMA],
        args=(ab, dout, gw_d), comms=comms)
    return (dab, dwd.reshape(N_DEV, R_FF, D)), got


def _ffn_bwd_input(x, gain, dab, dout, gw_gu, name, comms=()):
    t = x.shape[0]

    def body(x_ref, g_ref, dab_ref, do_ref, gu_ref, dx_ref, dg_ref, wgu, sems):
        @pl.when(pl.program_id(0) == 0)
        def _():
            cps = _load_rows(gu_ref, wgu.at[0:F], 0, R_FF, sems.at[0]) + _load_rows(gu_ref, wgu.at[F:2 * F], R_FF, R_FF, sems.at[1])
            dg_ref[...] = jnp.zeros_like(dg_ref)
            for cp in cps:
                cp.wait()

        gain_v = g_ref[...]
        xn, r = _rms(x_ref[...])
        dh = _nn(dab_ref[...], wgu[...])
        dxn, dg = _rms_bwd(dh, xn, r, gain_v)
        dg_ref[...] += dg
        dx_ref[...] = do_ref[...] + dxn

    def tile(w):
        return pl.BlockSpec((TM, w), lambda i: (i, 0))

    row = pl.BlockSpec((1, D), lambda i: (0, 0))
    return _call(
        body, name=name, grid=(t // TM,),
        in_specs=[tile(D), row, tile(2 * F), tile(D), ANY], out_specs=[tile(D), row],
        out_shape=[jax.ShapeDtypeStruct((t, D), F32), jax.ShapeDtypeStruct((1, D), F32)],
        scratch_shapes=[pltpu.VMEM((2 * F, D), BF16), pltpu.SemaphoreType.DMA((2, N_DEV))],
        args=(x, gain, dab, dout, gw_gu), comms=comms)


def _weight_grad(a, b, name, col_off=0, m=None, comms=None, mats=1):
    t = a.shape[0]
    m = a.shape[1] if m is None else m
    n = b.shape[1]
    tmm = 512 if m % 512 == 0 else 256
    first = col_off // tmm

    def body(a_ref, b_ref, o_ref):
        o_ref[...] = _tn(a_ref[...], b_ref[...]).astype(BF16)

    (out,), got = _call(
        body, name=name, grid=(m // tmm,),
        in_specs=[pl.BlockSpec((t, tmm), lambda i: (0, first + i)), pl.BlockSpec((t, n), lambda i: (0, 0))],
        out_specs=[pl.BlockSpec((tmm, n), lambda i: (i, 0))],
        out_shape=[jax.ShapeDtypeStruct((m, n), BF16)], scratch_shapes=[], args=(a, b), comms=comms or ())
    out = out.reshape(mats * N_DEV, m // (mats * N_DEV), n)
    return out if comms is None else (out, got)


def _mix_proj_fwd(x, gain, b_gate, gw, comms=()):
    t = x.shape[0]

    def body(x_ref, g_ref, bg_ref, gw_ref, q_ref, k_ref, v_ref, zs_ref, gt_ref, h_ref, win, sems):
        @pl.when(pl.program_id(0) == 0)
        def _():
            for cp in _load_rows(gw_ref, win, OFF_IN, R_IN, sems):
                cp.wait()

        xn, _ = _rms(x_ref[...])
        h = (xn * g_ref[...]).astype(BF16)
        h_ref[...] = h
        z = _nt(h, win[...])
        q_ref[...] = (z[:, 0:D_ATT] * 0.125).astype(BF16)
        k_ref[...] = z[:, D_ATT:2 * D_ATT].astype(BF16)
        v_ref[...] = z[:, 2 * D_ATT:3 * D_ATT].astype(BF16)
        zs_ref[...] = z[:, 3 * D_ATT:3 * D_ATT + 2 * D_SGU].astype(BF16)
        gt_ref[...] = jax.nn.sigmoid(z[:, 3 * D_ATT + 2 * D_SGU:] + bg_ref[...]).astype(BF16)

    def tile(w):
        return pl.BlockSpec((TM, w), lambda i: (i, 0))

    return _call(
        body, name="mix_proj_fwd", grid=(t // TM,),
        in_specs=[tile(D), pl.BlockSpec((1, D), lambda i: (0, 0)), pl.BlockSpec((1, 2 * D), lambda i: (0, 0)), ANY],
        out_specs=[tile(D_ATT), tile(D_ATT), tile(D_ATT), tile(2 * D_SGU), tile(2 * D), tile(D)],
        out_shape=[jax.ShapeDtypeStruct((t, D_ATT), BF16)] * 3 + [jax.ShapeDtypeStruct((t, 2 * D_SGU), BF16),
                                                                   jax.ShapeDtypeStruct((t, 2 * D), BF16),
                                                                   jax.ShapeDtypeStruct((t, D), BF16)],
        scratch_shapes=[pltpu.VMEM((D_IN, D), BF16), pltpu.SemaphoreType.DMA((N_DEV,))],
        args=(x, gain, b_gate, gw), comms=comms)


def _mix_proj_bwd(dz, x, gain, dres, gw, comms=()):
    t = x.shape[0]

    def body(dz_ref, x_ref, g_ref, dr_ref, gw_ref, dx_ref, dg_ref, win, sems):
        @pl.when(pl.program_id(0) == 0)
        def _():
            cps = _load_rows(gw_ref, win, OFF_IN, R_IN, sems)
            dg_ref[...] = jnp.zeros_like(dg_ref)
            for cp in cps:
                cp.wait()

        dh = _nn(dz_ref[...], win[...])
        xn, r = _rms(x_ref[...])
        dxn, dg = _rms_bwd(dh, xn, r, g_ref[...])
        dg_ref[...] += dg
        dx_ref[...] = dr_ref[...] + dxn

    def tile(w):
        return pl.BlockSpec((TM, w), lambda i: (i, 0))

    row = pl.BlockSpec((1, D), lambda i: (0, 0))
    return _call(
        body, name="mix_proj_bwd", grid=(t // TM,),
        in_specs=[tile(D_IN), tile(D), row, tile(D), ANY], out_specs=[tile(D), row],
        out_shape=[jax.ShapeDtypeStruct((t, D), F32), jax.ShapeDtypeStruct((1, D), F32)],
        scratch_shapes=[pltpu.VMEM((D_IN, D), BF16), pltpu.SemaphoreType.DMA((N_DEV,))],
        args=(dz, x, gain, dres, gw), comms=comms)


SKEW_W = KW + QB
N_CAP = 2 * QB - REL_CLIP + 1


def _band_bias(rel_bias):
    cap = rel_bias[:, 2 * REL_CLIP:]
    diag = jnp.concatenate([jnp.broadcast_to(cap, (HEADS, N_CAP)), rel_bias[:, 2 * REL_CLIP - 1::-1],
                            jnp.broadcast_to(cap, (HEADS, SKEW_W - N_CAP - 2 * REL_CLIP))], axis=1)

    def body(d_ref, o_ref):
        lag = lax.broadcasted_iota(jnp.int32, (QB, KW), 1) // CHUNK - lax.broadcasted_iota(jnp.int32, (QB, KW), 0) // CHUNK
        band = (lag >= 0) & (lag <= N_LEFT)
        for h in range(HEADS):
            rows = jnp.broadcast_to(d_ref[h:h + 1, :], (QB, SKEW_W))
            o_ref[h] = jnp.where(band, pltpu.roll(rows, 0, 1, stride=1, stride_axis=0)[:, :KW], NEG_INF)

    return pl.pallas_call(body, name="band_bias", out_shape=jax.ShapeDtypeStruct((HEADS, QB, KW), F32))(diag)


def _att_specs():
    qspec = pl.BlockSpec((QB, D_ATT), lambda g: (g, 0))
    kspecs = [pl.BlockSpec((QB, D_ATT), lambda g: (jnp.maximum(g - 2, 0), 0)),
              pl.BlockSpec((QB, D_ATT), lambda g: (jnp.maximum(g - 1, 0), 0)), qspec]
    bspec = pl.BlockSpec((HEADS, QB, KW), lambda g: (0, 0, 0))
    return qspec, kspecs, bspec


def _att_probs(qm, kp, bias, valid):
    s = jnp.where(valid, _nt(qm, kp) + bias, NEG_INF)
    e = jnp.exp(s - jnp.max(s, axis=-1, keepdims=True))
    return e / jnp.sum(e, axis=-1, keepdims=True)


def _att_valid():
    g = pl.program_id(0)
    blk = lax.broadcasted_iota(jnp.int32, (QB, KW), 1) // QB
    return (blk + g) >= 2


def _att_fwd(q, k, v, bias, comms=()):
    t = q.shape[0]

    def body(q_ref, k0, k1, k2, v0, v1, v2, b_ref, y_ref):
        valid = _att_valid()
        first = lax.broadcasted_iota(jnp.int32, (1, 128), 1) < 64
        for p in range(HEADS // 2):
            lanes = slice(p * 128, (p + 1) * 128)
            qp = q_ref[:, lanes]
            kp = jnp.concatenate([k0[:, lanes], k1[:, lanes], k2[:, lanes]], axis=0)
            vp = jnp.concatenate([v0[:, lanes], v1[:, lanes], v2[:, lanes]], axis=0)
            out = jnp.zeros((QB, 128), F32)
            for hh in range(2):
                mask = first if hh == 0 else jnp.logical_not(first)
                pr = _att_probs(jnp.where(mask, qp, 0), kp, b_ref[2 * p + hh], valid)
                out = out + _nn(pr.astype(BF16), jnp.where(mask, vp, 0))
            y_ref[:, lanes] = out.astype(BF16)

    qspec, kspecs, bspec = _att_specs()
    (out,), got = _call(
        body, name="att_fwd", grid=(t // QB,),
        in_specs=[qspec] + kspecs + kspecs + [bspec], out_specs=[qspec],
        out_shape=[jax.ShapeDtypeStruct((t, D_ATT), BF16)], scratch_shapes=[],
        args=(q, k, k, k, v, v, v, bias), comms=comms)
    return out, got


def _att_bwd(q, k, v, bias, dy, comms=()):
    t = q.shape[0]
    n_blocks = t // QB

    def body(q_ref, k0, k1, k2, v0, v1, v2, b_ref, dy_ref, dq_ref, dk_ref, dv_ref, db_ref, dk_acc, dv_acc):
        g = pl.program_id(0)

        @pl.when(g == 0)
        def _():
            db_ref[...] = jnp.zeros_like(db_ref)
            dk_acc[...] = jnp.zeros_like(dk_acc)
            dv_acc[...] = jnp.zeros_like(dv_acc)

        valid = _att_valid()
        first = lax.broadcasted_iota(jnp.int32, (1, 128), 1) < 64
        for p in range(HEADS // 2):
            lanes = slice(p * 128, (p + 1) * 128)
            qp = q_ref[:, lanes]
            dyp = dy_ref[:, lanes]
            kp = jnp.concatenate([k0[:, lanes], k1[:, lanes], k2[:, lanes]], axis=0)
            vp = jnp.concatenate([v0[:, lanes], v1[:, lanes], v2[:, lanes]], axis=0)
            dq = jnp.zeros((QB, 128), F32)
            dk = jnp.zeros((KW, 128), F32)
            dv = jnp.zeros((KW, 128), F32)
            for hh in range(2):
                mask = first if hh == 0 else jnp.logical_not(first)
                qm = jnp.where(mask, qp, 0)
                dym = jnp.where(mask, dyp, 0)
                pr = _att_probs(qm, kp, b_ref[2 * p + hh], valid)
                dp = _nt(dym, vp)
                ds = pr * (dp - jnp.sum(dp * pr, axis=-1, keepdims=True))
                db_ref[2 * p + hh] += ds
                dsb = ds.astype(BF16)
                dq = dq + _nn(dsb, jnp.where(mask, kp, 0))
                dk = dk + _tn(dsb, qm)
                dv = dv + _tn(pr.astype(BF16), dym)
            dq_ref[:, lanes] = (dq * 0.125).astype(BF16)
            for j in range(3):
                rows = pl.ds(pl.multiple_of(jnp.maximum(g - 2 + j, 0) * QB, QB), QB)
                dk_acc[rows, lanes] += dk[j * QB:(j + 1) * QB]
                dv_acc[rows, lanes] += dv[j * QB:(j + 1) * QB]

        @pl.when(g == n_blocks - 1)
        def _():
            dk_ref[...] = dk_acc[...].astype(BF16)
            dv_ref[...] = dv_acc[...].astype(BF16)

    qspec, kspecs, bspec = _att_specs()
    full = pl.BlockSpec((t, D_ATT), lambda g: (0, 0))
    return _call(
        body, name="att_bwd", grid=(n_blocks,),
        in_specs=[qspec] + kspecs + kspecs + [bspec, qspec], out_specs=[qspec, full, full, bspec],
        out_shape=[jax.ShapeDtypeStruct((t, D_ATT), BF16)] * 3 + [jax.ShapeDtypeStruct((HEADS, QB, KW), F32)],
        scratch_shapes=[pltpu.VMEM((t, D_ATT), F32)] * 2,
        args=(q, k, k, k, v, v, v, bias, dy), comms=comms)


def _rel_bias_grad(dbias):
    def body(db_ref, cs_ref, tot_ref):
        lane = lax.broadcasted_iota(jnp.int32, (1, SKEW_W), 1)
        capped = (lane < N_CAP) | (lane > KW)
        pad = jnp.zeros((8, QB), F32)
        for h in range(HEADS):
            z = jnp.concatenate([db_ref[h, 0:8, :], pad], axis=1)
            for a in range(1, QB // 8):
                z = z + pltpu.roll(jnp.concatenate([db_ref[h, 8 * a:8 * a + 8, :], pad], axis=1), SKEW_W - 8 * a, 1)
            cs = z[0:1, :]
            for b in range(1, 8):
                cs = cs + pltpu.roll(z[b:b + 1, :], SKEW_W - b, 1)
            cs_ref[h:h + 1, :] = cs
            tot_ref[h:h + 1, :] = jnp.broadcast_to(jnp.sum(jnp.where(capped, cs, 0.0), axis=1, keepdims=True), (1, 128))

    cs, tot = pl.pallas_call(
        body, name="rel_bias_grad",
        out_shape=[jax.ShapeDtypeStruct((HEADS, SKEW_W), F32), jax.ShapeDtypeStruct((HEADS, 128), F32)],
    )(dbias)
    return jnp.concatenate([cs[:, KW:N_CAP - 1:-1], tot[:, :1]], axis=1)


def _sgu_mask():
    pos = np.arange(SGU_BLOCK)
    return (pos[:, None] // CHUNK) >= (pos[None, :] // CHUNK)


def _group_stack(blk, first):
    return jnp.concatenate([jnp.where(first, blk, 0), jnp.where(first, 0, blk)], axis=0)


def _sgu_norm(zs_ref, lng, lnb):
    zs = zs_ref[...].astype(F32)
    ga, th = _gelu(zs)
    u, vs = ga[:, :D_SGU], ga[:, D_SGU:]
    mu = jnp.mean(vs, axis=-1, keepdims=True)
    cen = vs - mu
    rstd = lax.rsqrt(jnp.mean(cen * cen, axis=-1, keepdims=True) + EPS)
    xhat = cen * rstd
    return zs, th, u, xhat, rstd, xhat * lng + lnb


def _sgu_mix(vb, wm2_ref, bsx, s_ref):
    first = lax.broadcasted_iota(jnp.int32, (1, 128), 1) < 64
    for n in range(TM // SGU_BLOCK):
        for p in range(4):
            blk = vb[n * 128:(n + 1) * 128, p * 128:(p + 1) * 128]
            s_ref[n * 128:(n + 1) * 128, p * 128:(p + 1) * 128] = _nn(wm2_ref[p], _group_stack(blk, first)) + bsx[:, p * 128:(p + 1) * 128]


def _merge_fwd(x, zs, gt, y_att, lng, lnb, wm2, bsx, gw):
    t = x.shape[0]

    def body(x_ref, zs_ref, gt_ref, ya_ref, lng_ref, lnb_ref, wm2_ref, bsx_ref, gw_ref, xo_ref, ys_ref, mg_ref,
             wbr, wo, s_scr, sems):
        @pl.when(pl.program_id(0) == 0)
        def _():
            for cp in _load_rows(gw_ref, wbr, OFF_BR, R_BR, sems.at[0]) + _load_rows(gw_ref, wo, OFF_WO, R_WO, sems.at[1]):
                cp.wait()

        _, _, u, _, _, vsn = _sgu_norm(zs_ref, lng_ref[...], lnb_ref[...])
        _sgu_mix(vsn.astype(BF16), wm2_ref, bsx_ref[...], s_scr)
        ys = (u * s_scr[...]).astype(BF16)
        ys_ref[...] = ys
        pa = _nt(ya_ref[...], wbr[:, :D_ATT])
        ps = _nt(ys, wbr[:, D_ATT:])
        mg = (gt_ref[:, :D].astype(F32) * pa + gt_ref[:, D:].astype(F32) * ps).astype(BF16)
        mg_ref[...] = mg
        xo_ref[...] = x_ref[...] + _nn(mg, wo[...])

    def tile(w):
        return pl.BlockSpec((TM, w), lambda i: (i, 0))

    def const(shape):
        return pl.BlockSpec(shape, lambda i: (0,) * len(shape))

    return pl.pallas_call(
        body, name="merge_fwd", grid=(t // TM,),
        in_specs=[tile(D), tile(2 * D_SGU), tile(2 * D), tile(D_ATT), const((1, D_SGU)), const((1, D_SGU)),
                  const((4, 128, 256)), const((128, D_SGU)), ANY],
        out_specs=[tile(D), tile(D_SGU), tile(D)],
        out_shape=[jax.ShapeDtypeStruct((t, D), F32), jax.ShapeDtypeStruct((t, D_SGU), BF16), jax.ShapeDtypeStruct((t, D), BF16)],
        scratch_shapes=[pltpu.VMEM((D, D), BF16), pltpu.VMEM((D, D), BF16), pltpu.VMEM((TM, D_SGU), F32),
                        pltpu.SemaphoreType.DMA((2, N_DEV))],
        compiler_params=_cparams(("arbitrary",)),
    )(x, zs, gt, y_att, lng, lnb, wm2, bsx, gw)


def _merge_bwd(dx, gt, y_att, y_sgu, merged, gw):
    t = dx.shape[0]
    n_steps = t // TM

    def body(dx_ref, gt_ref, ya_ref, ys_ref, mg_ref, gw_ref, dzg_ref, dya_ref, dys_ref, dbg_ref, ga_ref, gs_ref, go_ref,
             wbr, wo, acc_a, acc_s, acc_o, sems):
        step = pl.program_id(0)

        @pl.when(step == 0)
        def _():
            cps = _load_rows(gw_ref, wbr, OFF_BR, R_BR, sems.at[0]) + _load_rows(gw_ref, wo, OFF_WO, R_WO, sems.at[1])
            dbg_ref[...] = jnp.zeros_like(dbg_ref)
            for acc in (acc_a, acc_s, acc_o):
                acc[...] = jnp.zeros_like(acc)
            for cp in cps:
                cp.wait()

        dxb = dx_ref[...].astype(BF16)
        acc_o[...] += _tn(mg_ref[...], dxb)
        dm = _nt(dxb, wo[...])
        for half, y_ref, w, acc in ((0, ya_ref, wbr.at[:, :D_ATT], acc_a), (1, ys_ref, wbr.at[:, D_ATT:], acc_s)):
            cols = slice(half * D, (half + 1) * D)
            gate = gt_ref[:, cols].astype(F32)
            branch = _nt(y_ref[...], w[...])
            dzg = dm * branch * gate * (1.0 - gate)
            dbg_ref[:, cols] += jnp.sum(dzg, axis=0, keepdims=True)
            dzg_ref[:, cols] = dzg.astype(BF16)
            dbr = (dm * gate).astype(BF16)
            acc[...] += _tn(dbr, y_ref[...])
            dy = _nn(dbr, w[...])
            if half == 0:
                dya_ref[...] = dy.astype(BF16)
            else:
                dys_ref[...] = dy

        @pl.when(step == n_steps - 1)
        def _():
            ga_ref[...] = acc_a[...].astype(BF16)
            gs_ref[...] = acc_s[...].astype(BF16)
            go_ref[...] = acc_o[...].astype(BF16)

    def tile(w):
        return pl.BlockSpec((TM, w), lambda i: (i, 0))

    def whole(r, c):
        return pl.BlockSpec((r, c), lambda i: (0, 0))

    dzg, dya, dys, dbg, g_ba, g_bs, g_wo = pl.pallas_call(
        body, name="merge_bwd", grid=(n_steps,),
        in_specs=[tile(D), tile(2 * D), tile(D_ATT), tile(D_SGU), tile(D), ANY],
        out_specs=[tile(2 * D), tile(D_ATT), tile(D_SGU), whole(1, 2 * D), whole(D, D_ATT), whole(D, D_SGU), whole(D, D)],
        out_shape=[jax.ShapeDtypeStruct((t, 2 * D), BF16), jax.ShapeDtypeStruct((t, D_ATT), BF16), jax.ShapeDtypeStruct((t, D_SGU), F32),
                   jax.ShapeDtypeStruct((1, 2 * D), F32), jax.ShapeDtypeStruct((D, D_ATT), BF16), jax.ShapeDtypeStruct((D, D_SGU), BF16),
                   jax.ShapeDtypeStruct((D, D), BF16)],
        scratch_shapes=[pltpu.VMEM((D, D), BF16), pltpu.VMEM((D, D), BF16), pltpu.VMEM((D, D_ATT), F32), pltpu.VMEM((D, D_SGU), F32),
                        pltpu.VMEM((D, D), F32), pltpu.SemaphoreType.DMA((2, N_DEV))],
        compiler_params=_cparams(("arbitrary",)),
    )(dx, gt, y_att, y_sgu, merged, gw)
    return dzg, dya, dys, dbg, g_ba.reshape(N_DEV, R_BR, D_ATT), g_bs.reshape(N_DEV, R_BR, D_SGU), g_wo.reshape(N_DEV, R_WO, D)


def _sgu_bwd(zs, dys, lng, lnb, wm2, wmt2, bsx, comms=()):
    t = zs.shape[0]
    n_steps = t // TM

    def body(zs_ref, dys_ref, lng_ref, lnb_ref, wm2_ref, wmt2_ref, bsx_ref, dzs_ref, dw_ref, dbs_ref, dlg_ref, dlb_ref,
             s_scr, dv_scr, ds_acc):
        i = pl.program_id(0)

        @pl.when(i == 0)
        def _():
            dw_ref[...] = jnp.zeros_like(dw_ref)
            dlg_ref[...] = jnp.zeros_like(dlg_ref)
            dlb_ref[...] = jnp.zeros_like(dlb_ref)
            ds_acc[...] = jnp.zeros_like(ds_acc)

        lng = lng_ref[...]
        zs, th, u, xhat, rstd, vsn = _sgu_norm(zs_ref, lng, lnb_ref[...])
        vb = vsn.astype(BF16)
        _sgu_mix(vb, wm2_ref, bsx_ref[...], s_scr)
        dys = dys_ref[...]
        du = dys * s_scr[...]
        ds = dys * u
        dsb = ds.astype(BF16)
        first = lax.broadcasted_iota(jnp.int32, (1, 128), 1) < 64
        acc = jnp.zeros((SGU_BLOCK, D_SGU), F32)
        for n in range(TM // SGU_BLOCK):
            rows = slice(n * 128, (n + 1) * 128)
            acc = acc + ds[rows]
            for p in range(4):
                lanes = slice(p * 128, (p + 1) * 128)
                stack = _group_stack(dsb[rows, lanes], first)
                dv_scr[rows, lanes] = _nn(wmt2_ref[p], stack)
                dw_ref[p] += _nt(stack, vb[rows, lanes])
        ds_acc[...] += acc
        dvsn = dv_scr[...]
        dlg_ref[...] += jnp.sum(dvsn * xhat, axis=0, keepdims=True)
        dlb_ref[...] += jnp.sum(dvsn, axis=0, keepdims=True)
        dxh = dvsn * lng
        dvs = rstd * (dxh - jnp.mean(dxh, axis=-1, keepdims=True) - xhat * jnp.mean(dxh * xhat, axis=-1, keepdims=True))
        dga = jnp.concatenate([du, dvs], axis=1)
        dzs_ref[...] = (dga * _gelu_grad(zs, th)).astype(BF16)

        @pl.when(i == n_steps - 1)
        def _():
            r = lax.broadcasted_iota(jnp.int32, (256, 128), 0) % SGU_BLOCK
            c = lax.broadcasted_iota(jnp.int32, (256, 128), 1)
            keep = (r // CHUNK) >= (c // CHUNK)
            for p in range(4):
                dw_ref[p] = jnp.where(keep, dw_ref[p], 0.0)
            total = ds_acc[...]
            grp = lax.broadcasted_iota(jnp.int32, (SGU_BLOCK, D_SGU), 1) // 64
            lane = lax.broadcasted_iota(jnp.int32, (SGU_BLOCK, 128), 1)
            out = jnp.zeros((SGU_BLOCK, 128), F32)
            for gi in range(8):
                out = jnp.where(lane == gi, jnp.sum(jnp.where(grp == gi, total, 0.0), axis=1, keepdims=True), out)
            dbs_ref[...] = out

    def tile(w):
        return pl.BlockSpec((TM, w), lambda i: (i, 0))

    def const(shape):
        return pl.BlockSpec(shape, lambda i: (0,) * len(shape))

    return _call(
        body, name="sgu_bwd", grid=(n_steps,),
        in_specs=[tile(2 * D_SGU), tile(D_SGU), const((1, D_SGU)), const((1, D_SGU)), const((4, 128, 256)), const((4, 128, 256)),
                  const((128, D_SGU))],
        out_specs=[tile(2 * D_SGU), const((4, 256, 128)), const((128, 128)), const((1, D_SGU)), const((1, D_SGU))],
        out_shape=[jax.ShapeDtypeStruct((t, 2 * D_SGU), BF16), jax.ShapeDtypeStruct((4, 256, 128), F32),
                   jax.ShapeDtypeStruct((128, 128), F32), jax.ShapeDtypeStruct((1, D_SGU), F32), jax.ShapeDtypeStruct((1, D_SGU), F32)],
        scratch_shapes=[pltpu.VMEM((TM, D_SGU), F32), pltpu.VMEM((TM, D_SGU), F32), pltpu.VMEM((SGU_BLOCK, D_SGU), F32)],
        args=(zs, dys, lng, lnb, wm2, wmt2, bsx), comms=comms)


def _adamw(g, w, m, v):
    m = ADAM_B1 * m + (1.0 - ADAM_B1) * g
    v = ADAM_B2 * v + (1.0 - ADAM_B2) * (g * g)
    m_hat = m / (1.0 - ADAM_B1 ** ADAM_STEP)
    v_hat = v / (1.0 - ADAM_B2 ** ADAM_STEP)
    return -ADAM_LR * (m_hat / (jnp.sqrt(v_hat) + ADAM_EPS) + ADAM_WD * w), m, v


def _adamw_matrix(parts, sums, w, m, v, transposed, name):
    _, r, c = parts.shape
    tc = 256

    def body(own_ref, p_ref, s_ref, w_ref, m_ref, v_ref, g_ref, d_ref, mo_ref, vo_ref):
        g = p_ref[0].astype(F32) + p_ref[1].astype(F32) + p_ref[2].astype(F32) + s_ref[...].astype(F32)
        g = g.T if transposed else g
        g_ref[...] = g
        d_ref[...], mo_ref[...], vo_ref[...] = _adamw(g, w_ref[...], m_ref[...], v_ref[...])

    own = pl.BlockSpec((None, tc, r), lambda i, o: (0, i, 0)) if transposed else pl.BlockSpec((None, r, tc), lambda i, o: (0, 0, i))
    return pl.pallas_call(
        body, name=name,
        grid_spec=pltpu.PrefetchScalarGridSpec(
            num_scalar_prefetch=1, grid=(c // tc,),
            in_specs=[pl.BlockSpec((3, r, tc), lambda i, o: (0, 0, i)), pl.BlockSpec((None, r, tc), lambda i, o: (o[0], 0, i)), own, own, own],
            out_specs=[own] * 4),
        out_shape=[jax.ShapeDtypeStruct(w.shape, F32)] * 4,
        compiler_params=_cparams(("arbitrary",)),
    )(_my_index(("x", 2), ("y", 1)), parts, sums, w, m, v)


_SMALL_2D = {"norm_ffn1": (1, D), "norm_mix": (1, D), "norm_ffn2": (1, D), "norm_final": (1, D), "b_gate": (1, 2 * D),
             "sgu_ln_g": (1, D_SGU), "sgu_ln_b": (1, D_SGU), "sgu_b_s": (8, SGU_BLOCK), "rel_bias": (HEADS, N_REL),
             "sgu_w_s": (8 * SGU_BLOCK, SGU_BLOCK)}


def _adamw_small(parts, loss_parts, p):
    names = list(parts)
    n = len(names)

    def body(*refs):
        got, loss_got, wmv, outs, loss_out = refs[:n], refs[n], refs[n + 1:4 * n + 1], refs[4 * n + 1:8 * n + 1], refs[8 * n + 1]
        for i, name in enumerate(names):
            g = got[i][0]
            for k in range(1, N_DEV):
                g = g + got[i][k]
            if name == "sgu_b_s":
                g = g.T[0:8, :]
            res = (g,) + _adamw(g, wmv[3 * i][...], wmv[3 * i + 1][...], wmv[3 * i + 2][...])
            for o_ref, val in zip(outs[4 * i:4 * i + 4], res):
                o_ref[...] = val
        total = loss_got[0]
        for k in range(1, N_DEV):
            total = total + loss_got[k]
        loss_out[...] = total

    wmv = [p[pre + name].reshape(_SMALL_2D[name]) for name in names for pre in ("", "m_", "v_")]
    res = pl.pallas_call(
        body, name="adamw_small",
        out_shape=[jax.ShapeDtypeStruct(_SMALL_2D[name], F32) for name in names for _ in range(4)] + [jax.ShapeDtypeStruct((1, 128), F32)],
        compiler_params=_cparams())(*[parts[name] for name in names], loss_parts, *wmv)
    return [{name: res[4 * i + j].reshape(p[name].shape) for i, name in enumerate(names)} for j in range(4)], res[-1]


def _pack_rows(groups, name):
    flat = [a for grp in groups for a, _ in grp]
    rows = [grp[0][0].shape[2] if grp[0][1] else grp[0][0].shape[1] for grp in groups]

    def body(*refs):
        o_ref, pos, off = refs[-1], 0, 0
        for grp, r in zip(groups, rows):
            vals = []
            for _, transposed in grp:
                val = refs[pos][0]
                vals.append(val.T if transposed else val)
                pos += 1
            o_ref[off:off + r, :] = (vals[0] if len(vals) == 1 else jnp.concatenate(vals, axis=1)).astype(BF16)
            off += r

    return pl.pallas_call(body, name=name, out_shape=jax.ShapeDtypeStruct((sum(rows), D), BF16), compiler_params=_cparams())(*flat)


def _step(x, target, p):
    n1, nm, n2 = p["norm_ffn1"], p["norm_mix"], p["norm_ffn2"]
    nf = p["norm_final"].reshape(1, D)
    lng, lnb = p["sgu_ln_g"], p["sgu_ln_b"]
    w_m = jnp.where(jnp.asarray(_sgu_mask())[None], p["sgu_w_s"][0], 0.0).astype(BF16)
    wm2 = jnp.concatenate([w_m[0::2], w_m[1::2]], axis=2)
    w_mt = w_m.transpose(0, 2, 1)
    wmt2 = jnp.concatenate([w_mt[0::2], w_mt[1::2]], axis=2)
    bsx = jnp.repeat(p["sgu_b_s"][0].T, 64, axis=1)
    bias = _band_bias(p["rel_bias"][0])

    def chip_sums(grads, name):
        gots = _comm_only([_SiblingSwap(grads)], "swap_" + name)
        return _pair_sums(grads, gots, "pair_sums_" + name)
    def as_rows(a):
        return jnp.swapaxes(a, 1, 2)

    def updates(parts, sums, names):
        res = {}
        for pt, sm, n in zip(parts, sums, names):
            if p[n].shape[1:] == pt.shape[1:]:
                res[n] = _adamw_matrix(pt, sm, p[n], p["m_" + n], p["v_" + n], False, "adamw_" + n)
            elif p[n].shape[2] > 128:
                res[n] = [as_rows(o) for o in _adamw_matrix(pt, sm, as_rows(p[n]), as_rows(p["m_" + n]), as_rows(p["v_" + n]), False,
                                                            "adamw_" + n)]
            else:
                res[n] = _adamw_matrix(pt, sm, p[n], p["m_" + n], p["v_" + n], True, "adamw_" + n)
        return res

    rows1 = _pack_rows([[(as_rows(p["ffn1_w_gate"]), False)], [(as_rows(p["ffn1_w_up"]), False)], [(p["ffn1_w_down"], False)]], "pack_ffn1")
    rows_m = _pack_rows([[(as_rows(p["w_in"]), False)], [(p["w_branch_att"], True), (p["w_branch_sgu"], True)], [(p["w_out"], False)]],
                        "pack_mixer")
    rows2d = _pack_rows([[(p["ffn2_w_down"], False)]], "pack_ffn2_down")
    rows2gu = _pack_rows([[(as_rows(p["ffn2_w_gate"]), False)], [(as_rows(p["ffn2_w_up"]), False)]], "pack_ffn2_gate_up")
    (gw1,) = _comm_only([_Gather(rows1)], "gather_ffn1")
    x1, ab1, h1, (gwm,) = _ffn_fwd(x, n1, gw1, gw1, 2 * R_FF, "ffn1_fwd", [_Gather(rows_m)])
    (q, k, v, zs, gt, h2), (gw2d,) = _mix_proj_fwd(x1, nm, p["b_gate"], gwm, [_Gather(rows2d)])
    y_att, (gw2gu,) = _att_fwd(q, k, v, bias, [_Gather(rows2gu)])
    x2, y_sgu, merged = _merge_fwd(x1, zs, gt, y_att, lng, lnb, wm2, bsx, gwm)
    dx3, ab2, hb, _, d_nf, loss = _ffn_fwd(x2, n2, gw2gu, gw2d, 0, "ffn2_fwd", head=(nf, target))

    (dab, g_down), _ = _ffn_bwd_hidden(ab2, dx3, gw2d, 0, "ffn2_bwd_hidden")
    (dx2, d_n2), _ = _ffn_bwd_input(x2, n2, dab, dx3, gw2gu, "ffn2_bwd")
    g_gu = _weight_grad(dab, hb, "ffn2_dw_gate_up", mats=2)
    dzg, dya, dys, d_bg, g_ba, g_bs, g_wo = _merge_bwd(dx2, gt, y_att, y_sgu, merged, gwm)
    g_late = [(g_gu, 0), (g_gu, 1), (g_down, 0), (g_ba, 0), (g_bs, 0), (g_wo, 0)]
    late = ("ffn2_w_gate", "ffn2_w_up", "ffn2_w_down", "w_branch_att", "w_branch_sgu", "w_out")
    (dzs, d_wm, d_bs, d_lng, d_lnb), gots_late = _sgu_bwd(zs, dys, lng, lnb, wm2, wmt2, bsx, [_SiblingSwap(g_late)])
    sums_late = _pair_sums(g_late, gots_late, "pair_sums_late")
    (dq, dk, dv, d_bias), parts_late = _att_bwd(q, k, v, bias, dya, [_ChipScatter(sums_late)])
    big = updates(parts_late, sums_late, late)
    d_rel = _rel_bias_grad(d_bias)
    dz = jnp.concatenate([dq, dk, dv, dzs, dzg], axis=1)
    sums_in = chip_sums([(_weight_grad(dz, h2, "dw_in"), 0)], "w_in")
    (dx1, d_nm), parts_in = _mix_proj_bwd(dz, x1, nm, dx2, gwm, [_ChipScatter(sums_in)])
    big.update(updates(parts_in, sums_in, ("w_in",)))
    small = {"norm_ffn2": d_n2, "norm_final": d_nf, "b_gate": d_bg, "sgu_ln_g": d_lng, "sgu_ln_b": d_lnb, "sgu_b_s": d_bs,
             "rel_bias": d_rel, "sgu_w_s": d_wm.reshape(_SMALL_2D["sgu_w_s"]), "norm_mix": d_nm}

    (dab, g_down), (*small_parts, loss_parts) = _ffn_bwd_hidden(ab1, dx1, gw1, 2 * R_FF, "ffn1_bwd_hidden",
                                                                [_AllToAll(list(small.values()) + [loss])])
    sums_d = chip_sums([(g_down, 0)], "ffn1_down")
    g_gu, (parts_d,) = _weight_grad(dab, h1, "ffn1_dw_gate_up", comms=[_ChipScatter(sums_d)], mats=2)
    sums_gu = chip_sums([(g_gu, 0), (g_gu, 1)], "ffn1_gate_up")
    (dx0, d_n1), parts_gu = _ffn_bwd_input(x, n1, dab, dx1, gw1, "ffn1_bwd", [_ChipScatter(sums_gu)])
    (n1_parts,) = _comm_only([_AllToAll([d_n1])], "gather_norm_ffn1")
    big.update(updates([parts_d] + parts_gu, sums_d + sums_gu, ("ffn1_w_down", "ffn1_w_gate", "ffn1_w_up")))
    out_s, loss_sum = _adamw_small(dict(zip(small, small_parts), norm_ffn1=n1_parts), loss_parts, p)
    return dx0, loss_sum[0, 0], [{**{n: four[i] for n, four in big.items()}, **s} for i, s in enumerate(out_s)]


_OUT_ORDER = ("norm_ffn1", "ffn1_w_gate", "ffn1_w_up", "ffn1_w_down", "norm_mix", "w_in", "b_gate", "rel_bias", "sgu_ln_g", "sgu_ln_b",
              "sgu_w_s", "sgu_b_s", "w_branch_att", "w_branch_sgu", "w_out", "norm_ffn2", "ffn2_w_gate", "ffn2_w_up", "ffn2_w_down",
              "norm_final")


def kernel(x, norm_ffn1, ffn1_w_gate, ffn1_w_up, ffn1_w_down, norm_mix, w_in, b_gate, rel_bias, sgu_ln_g, sgu_ln_b, sgu_w_s, sgu_b_s, w_branch_att, w_branch_sgu, w_out, norm_ffn2, ffn2_w_gate, ffn2_w_up, ffn2_w_down, norm_final, loss_target, m_norm_ffn1, m_ffn1_w_gate, m_ffn1_w_up, m_ffn1_w_down, m_norm_mix, m_w_in, m_b_gate, m_rel_bias, m_sgu_ln_g, m_sgu_ln_b, m_sgu_w_s, m_sgu_b_s, m_w_branch_att, m_w_branch_sgu, m_w_out, m_norm_ffn2, m_ffn2_w_gate, m_ffn2_w_up, m_ffn2_w_down, m_norm_final, v_norm_ffn1, v_ffn1_w_gate, v_ffn1_w_up, v_ffn1_w_down, v_norm_mix, v_w_in, v_b_gate, v_rel_bias, v_sgu_ln_g, v_sgu_ln_b, v_sgu_w_s, v_sgu_b_s, v_w_branch_att, v_w_branch_sgu, v_w_out, v_norm_ffn2, v_ffn2_w_gate, v_ffn2_w_up, v_ffn2_w_down, v_norm_final):
    args = dict(locals())
    dx, loss, outs = _step(x[0], loss_target[0], {pre + n: args[pre + n] for pre in ("", "m_", "v_") for n in _OUT_ORDER})
    return (loss, dx[None], *[o[n] for o in outs for n in _OUT_ORDER])
```

```python
import functools

import numpy as np
import jax
import jax.numpy as jnp
from jax import lax
from jax.experimental import pallas as pl
from jax.experimental.pallas import tpu as pltpu

F32 = jnp.float32
BF16 = jnp.bfloat16

N_DEV = 8
D = 1024
F = 2816
D_ATT = 512
D_SGU = 512
D_IN = 4608
HEADS = 8
CHUNK = 64
N_LEFT = 8
REL_CLIP = 256
N_REL = 2 * REL_CLIP + 1
SGU_BLOCK = 128
EPS = 1e-6
NEG_INF = -1e30
QB = 256
KW = 3 * QB

R_FF, R_IN, R_BR, R_WO = F // N_DEV, D_IN // N_DEV, D // N_DEV, D // N_DEV
OFF_IN, OFF_BR, OFF_WO = 0, R_IN, R_IN + R_BR

FC = 256
TM = 512
VMEM_LIMIT = 56 * 1024 * 1024

ADAM_LR, ADAM_B1, ADAM_B2, ADAM_EPS, ADAM_WD, ADAM_STEP = 0.001, 0.9, 0.999, 1e-08, 0.01, 10

MESH = pl.DeviceIdType.MESH
ANY = pl.BlockSpec(memory_space=pl.ANY)


def _nt(a, b):
    return lax.dot_general(a, b, (((1,), (1,)), ((), ())), preferred_element_type=F32)


def _tn(a, b):
    return lax.dot_general(a, b, (((0,), (0,)), ((), ())), preferred_element_type=F32)


def _nn(a, b):
    return jnp.dot(a, b, preferred_element_type=F32)


def _cparams(sem=None):
    return pltpu.CompilerParams(dimension_semantics=sem, vmem_limit_bytes=VMEM_LIMIT)


def _load_rows(gw_ref, dst, off, rows, sems):
    copies = [pltpu.make_async_copy(gw_ref.at[k, pl.ds(off, rows), :], dst.at[pl.ds(k * rows, rows), :], sems.at[k])
              for k in range(N_DEV)]
    for cp in copies:
        cp.start()
    return copies


def _rms(xv):
    r = lax.rsqrt(jnp.mean(xv * xv, axis=-1, keepdims=True) + EPS)
    return xv * r, r


def _rms_bwd(dh, xn, r, gain):
    dxn = dh * gain
    dx = r * (dxn - xn * jnp.mean(dxn * xn, axis=-1, keepdims=True))
    return dx, jnp.sum(dh * xn, axis=0, keepdims=True)


def _gelu(x):
    t = jnp.tanh(0.7978845608028654 * (x + 0.044715 * x * x * x))
    return 0.5 * x * (1.0 + t), t


def _gelu_grad(x, t):
    return 0.5 * (1.0 + t) + 0.5 * x * (1.0 - t * t) * 0.7978845608028654 * (1.0 + 3.0 * 0.044715 * x * x)


def _place():
    x, y, cc = lax.axis_index("x"), lax.axis_index("y"), lax.axis_index("c")
    return x, y, cc, [(1 - x, y), (x, 1 - y), (1 - x, 1 - y)]


class _Gather:
    def __init__(self, shard):
        self.inputs = [shard]
        self.out_shape = [jax.ShapeDtypeStruct((N_DEV,) + shard.shape, shard.dtype)]
        self.scratch = [pltpu.SemaphoreType.DMA((7,)), pltpu.SemaphoreType.DMA((7,)), pltpu.SemaphoreType.DMA]

    def _copies(self, ins, outs, scr):
        (x_ref,), (out_ref,), (send_sems, recv_sems, local_sem) = ins, outs, scr
        x, y, cc, chips = _place()

        def slab(px, py, pc):
            return out_ref.at[4 * px + 2 * py + pc]

        def copy(k, block, to, src=None):
            return pltpu.make_async_remote_copy(
                src_ref=slab(*block) if src is None else src, dst_ref=slab(*block),
                send_sem=send_sems.at[k], recv_sem=recv_sems.at[k], device_id=to, device_id_type=MESH)

        me, sibling = (x, y, cc), (x, y, 1 - cc)
        x_nbr, y_nbr, diagonal = chips
        mine = pltpu.make_async_copy(x_ref, slab(*me), local_sem)
        first = [copy(0, me, sibling, src=x_ref), copy(1, me, (*x_nbr, cc), src=x_ref), copy(2, me, (*y_nbr, cc), src=x_ref)]
        neighbours = [copy(1, (*x_nbr, cc), me), copy(2, (*y_nbr, cc), me)]
        second_hand = copy(3, (x ^ (1 - cc), y ^ cc, cc), (x ^ cc, y ^ (1 - cc), cc))
        from_diagonal = copy(3, (*diagonal, cc), me)
        passed = [copy(4 + j, (*chip, cc), sibling) for j, chip in enumerate(chips)]
        from_sibling = [copy(0, sibling, me)] + [copy(4 + j, (*chip, 1 - cc), me) for j, chip in enumerate(chips)]
        return mine, first, neighbours, second_hand, from_diagonal, passed, from_sibling

    def begin(self, *refs):
        mine, first = self._copies(*refs)[:2]
        mine.start()
        for cp in first:
            cp.start()

    def mid(self, *refs):
        _, _, neighbours, second_hand, _, passed, _ = self._copies(*refs)
        for cp in neighbours:
            cp.wait_recv()
        second_hand.start()
        passed[0].start()
        passed[1].start()

    def relay(self, *refs):
        _, _, _, _, from_diagonal, passed, _ = self._copies(*refs)
        from_diagonal.wait_recv()
        passed[2].start()

    def end(self, *refs):
        mine, first, _, second_hand, _, passed, from_sibling = self._copies(*refs)
        for cp in from_sibling:
            cp.wait_recv()
        for cp in first + [second_hand] + passed:
            cp.wait_send()
        mine.wait()


class _Direct:
    def begin(self, *refs):
        keep, give = self._copies(*refs)
        for cp in keep + give:
            cp.start()

    def mid(self, *refs):
        pass

    def relay(self, *refs):
        pass

    def end(self, *refs):
        keep, give = self._copies(*refs)
        for cp in give:
            cp.wait_recv()
        for cp in give:
            cp.wait_send()
        for cp in keep:
            cp.wait()


class _SiblingSwap(_Direct):
    def __init__(self, grads):
        n = len(grads)
        self.which = [w for _, w in grads]
        self.inputs = [g for g, _ in grads]
        self.out_shape = [jax.ShapeDtypeStruct((4,) + g.shape[1:], g.dtype) for g, _ in grads]
        self.scratch = [pltpu.SemaphoreType.DMA((n, 4)), pltpu.SemaphoreType.DMA((n, 4))]

    def _copies(self, ins, outs, scr):
        send_sems, recv_sems = scr
        x, y, cc, _ = _place()
        return [], [pltpu.make_async_remote_copy(src_ref=g_ref.at[N_DEV * w + 2 * j + 1 - cc], dst_ref=got_ref.at[j],
                                                 send_sem=send_sems.at[i, j], recv_sem=recv_sems.at[i, j], device_id=(x, y, 1 - cc),
                                                 device_id_type=MESH)
                    for i, (g_ref, got_ref, w) in enumerate(zip(ins, outs, self.which)) for j in range(4)]


class _ChipScatter(_Direct):
    def __init__(self, sums):
        n = len(sums)
        self.inputs = list(sums)
        self.out_shape = [jax.ShapeDtypeStruct((3,) + s.shape[1:], s.dtype) for s in sums]
        self.scratch = [pltpu.SemaphoreType.DMA((n, 3)), pltpu.SemaphoreType.DMA((n, 3))]

    def _copies(self, ins, outs, scr):
        send_sems, recv_sems = scr
        _, _, cc, chips = _place()
        return [], [pltpu.make_async_remote_copy(src_ref=s_ref.at[2 * px + py], dst_ref=got_ref.at[j], send_sem=send_sems.at[i, j],
                                                 recv_sem=recv_sems.at[i, j], device_id=(px, py, cc), device_id_type=MESH)
                    for i, (s_ref, got_ref) in enumerate(zip(ins, outs)) for j, (px, py) in enumerate(chips)]


class _AllToAll(_Direct):
    def __init__(self, blocks):
        n = len(blocks)
        self.inputs = list(blocks)
        self.out_shape = [jax.ShapeDtypeStruct((N_DEV,) + b.shape, b.dtype) for b in blocks]
        self.scratch = [pltpu.SemaphoreType.DMA((n, 7)), pltpu.SemaphoreType.DMA((n, 7)), pltpu.SemaphoreType.DMA((n,))]

    def _copies(self, ins, outs, scr):
        send_sems, recv_sems, local_sems = scr
        x, y, cc, _ = _place()
        me = 4 * x + 2 * y + cc
        keep = [pltpu.make_async_copy(b_ref, got_ref.at[me], local_sems.at[i]) for i, (b_ref, got_ref) in enumerate(zip(ins, outs))]
        give = [pltpu.make_async_remote_copy(src_ref=b_ref, dst_ref=got_ref.at[me], send_sem=send_sems.at[i, k - 1],
                                             recv_sem=recv_sems.at[i, k - 1],
                                             device_id=(x ^ ((k >> 2) & 1), y ^ ((k >> 1) & 1), cc ^ (k & 1)), device_id_type=MESH)
                for i, (b_ref, got_ref) in enumerate(zip(ins, outs)) for k in range(1, N_DEV)]
        return keep, give


def _split_refs(refs, counts):
    out, pos = [], 0
    for n in counts:
        out.append(list(refs[pos:pos + n]))
        pos += n
    return out


def _bind(comms, c_in, c_out, c_scr):
    ins = _split_refs(c_in, [len(c.inputs) for c in comms])
    outs = _split_refs(c_out, [len(c.out_shape) for c in comms])
    scr = _split_refs(c_scr, [len(c.scratch) for c in comms])
    return [(c, (i, o, s)) for c, i, o, s in zip(comms, ins, outs, scr)]


def _call(body, *, name, grid, in_specs, out_specs, out_shape, scratch_shapes, args, comms=()):
    c_in = [a for c in comms for a in c.inputs]
    c_out = [s for c in comms for s in c.out_shape]
    c_scr = [s for c in comms for s in c.scratch]
    counts = [len(in_specs), len(c_in), len(out_shape), len(c_out), len(scratch_shapes), len(c_scr)]

    def full(*refs):
        ins, cin, outs, cout, scr, cscr = _split_refs(refs, counts)
        bound = _bind(comms, cin, cout, cscr)
        if comms:
            def at(steps):
                return functools.reduce(jnp.logical_and, [pl.program_id(ax) == s for ax, s in enumerate(steps)])

            first, last = at([0] * len(grid)), at([n - 1 for n in grid])

            @pl.when(first)
            def _():
                for c, r in bound:
                    c.begin(*r)

            @pl.when(at([(grid[0] - 1) // 2] + [0] * (len(grid) - 1)))
            def _():
                for c, r in bound:
                    c.mid(*r)

            @pl.when(last)
            def _():
                for c, r in bound:
                    c.relay(*r)

        body(*ins, *outs, *scr)
        if comms:
            @pl.when(last)
            def _():
                for c, r in bound:
                    c.end(*r)

    res = pl.pallas_call(
        full, name=name, grid=grid,
        in_specs=list(in_specs) + [ANY] * len(c_in), out_specs=list(out_specs) + [ANY] * len(c_out),
        out_shape=list(out_shape) + c_out, scratch_shapes=list(scratch_shapes) + c_scr,
        compiler_params=_cparams(("arbitrary",) * len(grid)),
    )(*args, *c_in)
    return list(res[:len(out_shape)]), list(res[len(out_shape):])


def _comm_only(comms, name):
    c_in = [a for c in comms for a in c.inputs]
    c_out = [s for c in comms for s in c.out_shape]
    c_scr = [s for c in comms for s in c.scratch]

    def full(*refs):
        cin, cout, cscr = _split_refs(refs, [len(c_in), len(c_out), len(c_scr)])
        bound = _bind(comms, cin, cout, cscr)
        for phase in ("begin", "mid", "relay", "end"):
            for c, r in bound:
                getattr(c, phase)(*r)

    return list(pl.pallas_call(full, name=name, in_specs=[ANY] * len(c_in), out_specs=[ANY] * len(c_out), out_shape=c_out,
                               scratch_shapes=c_scr)(*c_in))


def _my_index(*axes_and_weights):
    return sum(w * lax.axis_index(a) for a, w in axes_and_weights).astype(jnp.int32).reshape(1)


def _pair_sums(grads, gots, name):
    n = len(grads)

    def body(c_ref, *refs):
        for a_ref, b_ref, o_ref in zip(refs[:n], refs[n:2 * n], refs[2 * n:]):
            o_ref[...] = (a_ref[...].astype(F32) + b_ref[...].astype(F32)).astype(BF16)

    def tile(g):
        return pl.BlockSpec((1,) + g.shape[1:], lambda j, c_ref: (j, 0, 0))

    def mine(g, w):
        return pl.BlockSpec((1, None) + g.shape[1:], lambda j, c_ref: (4 * w + j, c_ref[0], 0, 0))

    return list(pl.pallas_call(
        body, name=name,
        grid_spec=pltpu.PrefetchScalarGridSpec(num_scalar_prefetch=1, grid=(4,),
                                               in_specs=[mine(g, w) for g, w in grads] + [tile(g) for g in gots],
                                               out_specs=[tile(g) for g in gots]),
        out_shape=[jax.ShapeDtypeStruct(g.shape, BF16) for g in gots],
        compiler_params=_cparams(("arbitrary",)))(_my_index(("c", 1)), *[g.reshape((-1, 2) + g.shape[1:]) for g, _ in grads], *gots))


def _ffn_fwd(x, gain, gw_gu, gw_d, off_d, name, comms=(), head=None):
    t = x.shape[0]
    n_head = 0 if head is None else 2

    def body(x_ref, g_ref, gu_ref, d_ref, *refs):
        head_in, (o_ref, ab_ref, h_ref), head_out = refs[:n_head], refs[n_head:n_head + 3], refs[n_head + 3:2 * n_head + 3]
        wg, wu, wd, s_scr, sems = refs[2 * n_head + 3:]

        @pl.when(pl.program_id(0) == 0)
        def _():
            cps = _load_rows(gu_ref, wg, 0, R_FF, sems.at[0]) + _load_rows(gu_ref, wu, R_FF, R_FF, sems.at[1]) \
                + _load_rows(d_ref, wd, off_d, R_FF, sems.at[2])
            for o in head_out:
                o[...] = jnp.zeros_like(o)
            for cp in cps:
                cp.wait()

        xv = x_ref[...]
        xn, _ = _rms(xv)
        h = (xn * g_ref[...]).astype(BF16)
        h_ref[...] = h
        for c in range(F // FC):
            rows = pl.ds(c * FC, FC)
            a = _nt(h, wg[rows, :])
            b = _nt(h, wu[rows, :])
            ab_ref[:, c * FC:(c + 1) * FC] = a.astype(BF16)
            ab_ref[:, F + c * FC:F + (c + 1) * FC] = b.astype(BF16)
            s_scr[:, c * FC:(c + 1) * FC] = (a * jax.nn.sigmoid(a) * b).astype(BF16)
        out = xv + 0.5 * _nn(s_scr[...], wd[...])
        if head is None:
            o_ref[...] = out
        else:
            (gf_ref, t_ref), (dg_ref, loss_ref) = head_in, head_out
            gain_f = gf_ref[...]
            yn, r = _rms(out)
            err = yn * gain_f - t_ref[...]
            loss_ref[...] += 0.5 * jnp.sum(jnp.mean(err * err, axis=-1, keepdims=True), axis=0, keepdims=True)
            o_ref[...], dg = _rms_bwd(err * (1.0 / D), yn, r, gain_f)
            dg_ref[...] += dg

    tile = pl.BlockSpec((TM, D), lambda i: (i, 0))
    row = pl.BlockSpec((1, D), lambda i: (0, 0))
    head_specs = [] if head is None else [row, pl.BlockSpec((1, 128), lambda i: (0, 0))]
    head_shapes = [] if head is None else [jax.ShapeDtypeStruct((1, D), F32), jax.ShapeDtypeStruct((1, 128), F32)]
    res, got = _call(
        body, name=name, grid=(t // TM,),
        in_specs=[tile, row, ANY, ANY] + ([] if head is None else [row, tile]),
        out_specs=[tile, pl.BlockSpec((TM, 2 * F), lambda i: (i, 0)), tile] + head_specs,
        out_shape=[jax.ShapeDtypeStruct((t, D), F32), jax.ShapeDtypeStruct((t, 2 * F), BF16), jax.ShapeDtypeStruct((t, D), BF16)] + head_shapes,
        scratch_shapes=[pltpu.VMEM((F, D), BF16)] * 3 + [pltpu.VMEM((TM, F), BF16), pltpu.SemaphoreType.DMA((3, N_DEV))],
        args=(x, gain, gw_gu, gw_d) + (() if head is None else tuple(head)), comms=comms)
    return (res[0], res[1], res[2], got, *res[3:])


def _ffn_bwd_hidden(ab, dout, gw_d, off_d, name, comms=()):
    t = ab.shape[0]
    n_steps = t // TM
    row_chunks = [(r, min(512, F - r)) for r in range(0, F, 512)]

    def tile(w):
        return pl.BlockSpec((TM, w), lambda i: (i, 0))

    def hidden(ab_ref, do_ref, d_ref, dab_ref, dwd_ref, wd, s_scr, acc, sems, out_sem):
        step = pl.program_id(0)

        @pl.when(step == 0)
        def _():
            cps = _load_rows(d_ref, wd, off_d, R_FF, sems)
            acc[...] = jnp.zeros_like(acc)
            for cp in cps:
                cp.wait()

        df = (0.5 * do_ref[...]).astype(BF16)
        for c in range(F // FC):
            a = ab_ref[:, c * FC:(c + 1) * FC].astype(F32)
            b = ab_ref[:, F + c * FC:F + (c + 1) * FC].astype(F32)
            sg = jax.nn.sigmoid(a)
            sl = a * sg
            ds = _nt(df, wd[pl.ds(c * FC, FC), :])
            dab_ref[:, c * FC:(c + 1) * FC] = (ds * b * (sg * (1.0 + a * (1.0 - sg)))).astype(BF16)
            dab_ref[:, F + c * FC:F + (c + 1) * FC] = (ds * sl).astype(BF16)
            s_scr[:, c * FC:(c + 1) * FC] = (sl * b).astype(BF16)
        for r, n in row_chunks:
            acc[r:r + n, :] += _tn(s_scr[:, r:r + n], df)

        @pl.when(step == n_steps - 1)
        def _():
            wd[...] = acc[...].astype(BF16)
            out = pltpu.make_async_copy(wd, dwd_ref, out_sem)
            out.start()
            out.wait()

    (dab, dwd), got = _call(
        hidden, name=name, grid=(n_steps,),
        in_specs=[tile(2 * F), tile(D), ANY], out_specs=[tile(2 * F), ANY],
        out_shape=[jax.ShapeDtypeStruct((t, 2 * F), BF16), jax.ShapeDtypeStruct((F, D), BF16)],
        scratch_shapes=[pltpu.VMEM((F, D), BF16), pltpu.VMEM((TM, F), BF16), pltpu.VMEM((F, D), F32), pltpu.SemaphoreType.DMA((N_DEV,)),
                        pltpu.SemaphoreType.DMA],
        args=(ab, dout, gw_d), comms=comms)
    return (dab, dwd.reshape(N_DEV, R_FF, D)), got


def _ffn_bwd_input(x, gain, dab, dout, gw_gu, name, comms=()):
    t = x.shape[0]

    def body(x_ref, g_ref, dab_ref, do_ref, gu_ref, dx_ref, dg_ref, wgu, sems):
        @pl.when(pl.program_id(0) == 0)
        def _():
            cps = _load_rows(gu_ref, wgu.at[0:F], 0, R_FF, sems.at[0]) + _load_rows(gu_ref, wgu.at[F:2 * F], R_FF, R_FF, sems.at[1])
            dg_ref[...] = jnp.zeros_like(dg_ref)
            for cp in cps:
                cp.wait()

        gain_v = g_ref[...]
        xn, r = _rms(x_ref[...])
        dh = _nn(dab_ref[...], wgu[...])
        dxn, dg = _rms_bwd(dh, xn, r, gain_v)
        dg_ref[...] += dg
        dx_ref[...] = do_ref[...] + dxn

    def tile(w):
        return pl.BlockSpec((TM, w), lambda i: (i, 0))

    row = pl.BlockSpec((1, D), lambda i: (0, 0))
    return _call(
        body, name=name, grid=(t // TM,),
        in_specs=[tile(D), row, tile(2 * F), tile(D), ANY], out_specs=[tile(D), row],
        out_shape=[jax.ShapeDtypeStruct((t, D), F32), jax.ShapeDtypeStruct((1, D), F32)],
        scratch_shapes=[pltpu.VMEM((2 * F, D), BF16), pltpu.SemaphoreType.DMA((2, N_DEV))],
        args=(x, gain, dab, dout, gw_gu), comms=comms)


def _weight_grad(a, b, name, col_off=0, m=None, comms=None, mats=1):
    t = a.shape[0]
    m = a.shape[1] if m is None else m
    n = b.shape[1]
    tmm = 512 if m % 512 == 0 else 256
    first = col_off // tmm

    def body(a_ref, b_ref, o_ref):
        o_ref[...] = _tn(a_ref[...], b_ref[...]).astype(BF16)

    (out,), got = _call(
        body, name=name, grid=(m // tmm,),
        in_specs=[pl.BlockSpec((t, tmm), lambda i: (0, first + i)), pl.BlockSpec((t, n), lambda i: (0, 0))],
        out_specs=[pl.BlockSpec((tmm, n), lambda i: (i, 0))],
        out_shape=[jax.ShapeDtypeStruct((m, n), BF16)], scratch_shapes=[], args=(a, b), comms=comms or ())
    out = out.reshape(mats * N_DEV, m // (mats * N_DEV), n)
    return out if comms is None else (out, got)


def _mix_proj_fwd(x, gain, b_gate, gw, comms=()):
    t = x.shape[0]

    def body(x_ref, g_ref, bg_ref, gw_ref, q_ref, k_ref, v_ref, zs_ref, gt_ref, h_ref, win, sems):
        @pl.when(pl.program_id(0) == 0)
        def _():
            for cp in _load_rows(gw_ref, win, OFF_IN, R_IN, sems):
                cp.wait()

        xn, _ = _rms(x_ref[...])
        h = (xn * g_ref[...]).astype(BF16)
        h_ref[...] = h
        z = _nt(h, win[...])
        q_ref[...] = (z[:, 0:D_ATT] * 0.125).astype(BF16)
        k_ref[...] = z[:, D_ATT:2 * D_ATT].astype(BF16)
        v_ref[...] = z[:, 2 * D_ATT:3 * D_ATT].astype(BF16)
        zs_ref[...] = z[:, 3 * D_ATT:3 * D_ATT + 2 * D_SGU].astype(BF16)
        gt_ref[...] = jax.nn.sigmoid(z[:, 3 * D_ATT + 2 * D_SGU:] + bg_ref[...]).astype(BF16)

    def tile(w):
        return pl.BlockSpec((TM, w), lambda i: (i, 0))

    return _call(
        body, name="mix_proj_fwd", grid=(t // TM,),
        in_specs=[tile(D), pl.BlockSpec((1, D), lambda i: (0, 0)), pl.BlockSpec((1, 2 * D), lambda i: (0, 0)), ANY],
        out_specs=[tile(D_ATT), tile(D_ATT), tile(D_ATT), tile(2 * D_SGU), tile(2 * D), tile(D)],
        out_shape=[jax.ShapeDtypeStruct((t, D_ATT), BF16)] * 3 + [jax.ShapeDtypeStruct((t, 2 * D_SGU), BF16),
                                                                   jax.ShapeDtypeStruct((t, 2 * D), BF16),
                                                                   jax.ShapeDtypeStruct((t, D), BF16)],
        scratch_shapes=[pltpu.VMEM((D_IN, D), BF16), pltpu.SemaphoreType.DMA((N_DEV,))],
        args=(x, gain, b_gate, gw), comms=comms)


def _mix_proj_bwd(dz, x, gain, dres, gw, comms=()):
    t = x.shape[0]

    def body(dz_ref, x_ref, g_ref, dr_ref, gw_ref, dx_ref, dg_ref, win, sems):
        @pl.when(pl.program_id(0) == 0)
        def _():
            cps = _load_rows(gw_ref, win, OFF_IN, R_IN, sems)
            dg_ref[...] = jnp.zeros_like(dg_ref)
            for cp in cps:
                cp.wait()

        dh = _nn(dz_ref[...], win[...])
        xn, r = _rms(x_ref[...])
        dxn, dg = _rms_bwd(dh, xn, r, g_ref[...])
        dg_ref[...] += dg
        dx_ref[...] = dr_ref[...] + dxn

    def tile(w):
        return pl.BlockSpec((TM, w), lambda i: (i, 0))

    row = pl.BlockSpec((1, D), lambda i: (0, 0))
    return _call(
        body, name="mix_proj_bwd", grid=(t // TM,),
        in_specs=[tile(D_IN), tile(D), row, tile(D), ANY], out_specs=[tile(D), row],
        out_shape=[jax.ShapeDtypeStruct((t, D), F32), jax.ShapeDtypeStruct((1, D), F32)],
        scratch_shapes=[pltpu.VMEM((D_IN, D), BF16), pltpu.SemaphoreType.DMA((N_DEV,))],
        args=(dz, x, gain, dres, gw), comms=comms)


SKEW_W = KW + QB
N_CAP = 2 * QB - REL_CLIP + 1


def _band_bias(rel_bias):
    cap = rel_bias[:, 2 * REL_CLIP:]
    diag = jnp.concatenate([jnp.broadcast_to(cap, (HEADS, N_CAP)), rel_bias[:, 2 * REL_CLIP - 1::-1],
                            jnp.broadcast_to(cap, (HEADS, SKEW_W - N_CAP - 2 * REL_CLIP))], axis=1)

    def body(d_ref, o_ref):
        lag = lax.broadcasted_iota(jnp.int32, (QB, KW), 1) // CHUNK - lax.broadcasted_iota(jnp.int32, (QB, KW), 0) // CHUNK
        band = (lag >= 0) & (lag <= N_LEFT)
        for h in range(HEADS):
            rows = jnp.broadcast_to(d_ref[h:h + 1, :], (QB, SKEW_W))
            o_ref[h] = jnp.where(band, pltpu.roll(rows, 0, 1, stride=1, stride_axis=0)[:, :KW], NEG_INF)

    return pl.pallas_call(body, name="band_bias", out_shape=jax.ShapeDtypeStruct((HEADS, QB, KW), F32))(diag)


def _att_specs():
    qspec = pl.BlockSpec((QB, D_ATT), lambda g: (g, 0))
    kspecs = [pl.BlockSpec((QB, D_ATT), lambda g: (jnp.maximum(g - 2, 0), 0)),
              pl.BlockSpec((QB, D_ATT), lambda g: (jnp.maximum(g - 1, 0), 0)), qspec]
    bspec = pl.BlockSpec((HEADS, QB, KW), lambda g: (0, 0, 0))
    return qspec, kspecs, bspec


def _att_probs(qm, kp, bias, valid):
    s = jnp.where(valid, _nt(qm, kp) + bias, NEG_INF)
    e = jnp.exp(s - jnp.max(s, axis=-1, keepdims=True))
    return e / jnp.sum(e, axis=-1, keepdims=True)


def _att_valid():
    g = pl.program_id(0)
    blk = lax.broadcasted_iota(jnp.int32, (QB, KW), 1) // QB
    return (blk + g) >= 2


def _att_fwd(q, k, v, bias, comms=()):
    t = q.shape[0]

    def body(q_ref, k0, k1, k2, v0, v1, v2, b_ref, y_ref):
        valid = _att_valid()
        first = lax.broadcasted_iota(jnp.int32, (1, 128), 1) < 64
        for p in range(HEADS // 2):
            lanes = slice(p * 128, (p + 1) * 128)
            qp = q_ref[:, lanes]
            kp = jnp.concatenate([k0[:, lanes], k1[:, lanes], k2[:, lanes]], axis=0)
            vp = jnp.concatenate([v0[:, lanes], v1[:, lanes], v2[:, lanes]], axis=0)
            out = jnp.zeros((QB, 128), F32)
            for hh in range(2):
                mask = first if hh == 0 else jnp.logical_not(first)
                pr = _att_probs(jnp.where(mask, qp, 0), kp, b_ref[2 * p + hh], valid)
                out = out + _nn(pr.astype(BF16), jnp.where(mask, vp, 0))
            y_ref[:, lanes] = out.astype(BF16)

    qspec, kspecs, bspec = _att_specs()
    (out,), got = _call(
        body, name="att_fwd", grid=(t // QB,),
        in_specs=[qspec] + kspecs + kspecs + [bspec], out_specs=[qspec],
        out_shape=[jax.ShapeDtypeStruct((t, D_ATT), BF16)], scratch_shapes=[],
        args=(q, k, k, k, v, v, v, bias), comms=comms)
    return out, got


def _att_bwd(q, k, v, bias, dy, comms=()):
    t = q.shape[0]
    n_blocks = t // QB

    def body(q_ref, k0, k1, k2, v0, v1, v2, b_ref, dy_ref, dq_ref, dk_ref, dv_ref, db_ref, dk_acc, dv_acc):
        g = pl.program_id(0)

        @pl.when(g == 0)
        def _():
            db_ref[...] = jnp.zeros_like(db_ref)
            dk_acc[...] = jnp.zeros_like(dk_acc)
            dv_acc[...] = jnp.zeros_like(dv_acc)

        valid = _att_valid()
        first = lax.broadcasted_iota(jnp.int32, (1, 128), 1) < 64
        for p in range(HEADS // 2):
            lanes = slice(p * 128, (p + 1) * 128)
            qp = q_ref[:, lanes]
            dyp = dy_ref[:, lanes]
            kp = jnp.concatenate([k0[:, lanes], k1[:, lanes], k2[:, lanes]], axis=0)
            vp = jnp.concatenate([v0[:, lanes], v1[:, lanes], v2[:, lanes]], axis=0)
            dq = jnp.zeros((QB, 128), F32)
            dk = jnp.zeros((KW, 128), F32)
            dv = jnp.zeros((KW, 128), F32)
            for hh in range(2):
                mask = first if hh == 0 else jnp.logical_not(first)
                qm = jnp.where(mask, qp, 0)
                dym = jnp.where(mask, dyp, 0)
                pr = _att_probs(qm, kp, b_ref[2 * p + hh], valid)
                dp = _nt(dym, vp)
                ds = pr * (dp - jnp.sum(dp * pr, axis=-1, keepdims=True))
                db_ref[2 * p + hh] += ds
                dsb = ds.astype(BF16)
                dq = dq + _nn(dsb, jnp.where(mask, kp, 0))
                dk = dk + _tn(dsb, qm)
                dv = dv + _tn(pr.astype(BF16), dym)
            dq_ref[:, lanes] = (dq * 0.125).astype(BF16)
            for j in range(3):
                rows = pl.ds(pl.multiple_of(jnp.maximum(g - 2 + j, 0) * QB, QB), QB)
                dk_acc[rows, lanes] += dk[j * QB:(j + 1) * QB]
                dv_acc[rows, lanes] += dv[j * QB:(j + 1) * QB]

        @pl.when(g == n_blocks - 1)
        def _():
            dk_ref[...] = dk_acc[...].astype(BF16)
            dv_ref[...] = dv_acc[...].astype(BF16)

    qspec, kspecs, bspec = _att_specs()
    full = pl.BlockSpec((t, D_ATT), lambda g: (0, 0))
    return _call(
        body, name="att_bwd", grid=(n_blocks,),
        in_specs=[qspec] + kspecs + kspecs + [bspec, qspec], out_specs=[qspec, full, full, bspec],
        out_shape=[jax.ShapeDtypeStruct((t, D_ATT), BF16)] * 3 + [jax.ShapeDtypeStruct((HEADS, QB, KW), F32)],
        scratch_shapes=[pltpu.VMEM((t, D_ATT), F32)] * 2,
        args=(q, k, k, k, v, v, v, bias, dy), comms=comms)


def _rel_bias_grad(dbias):
    def body(db_ref, cs_ref, tot_ref):
        lane = lax.broadcasted_iota(jnp.int32, (1, SKEW_W), 1)
        capped = (lane < N_CAP) | (lane > KW)
        pad = jnp.zeros((8, QB), F32)
        for h in range(HEADS):
            z = jnp.concatenate([db_ref[h, 0:8, :], pad], axis=1)
            for a in range(1, QB // 8):
                z = z + pltpu.roll(jnp.concatenate([db_ref[h, 8 * a:8 * a + 8, :], pad], axis=1), SKEW_W - 8 * a, 1)
            cs = z[0:1, :]
            for b in range(1, 8):
                cs = cs + pltpu.roll(z[b:b + 1, :], SKEW_W - b, 1)
            cs_ref[h:h + 1, :] = cs
            tot_ref[h:h + 1, :] = jnp.broadcast_to(jnp.sum(jnp.where(capped, cs, 0.0), axis=1, keepdims=True), (1, 128))

    cs, tot = pl.pallas_call(
        body, name="rel_bias_grad",
        out_shape=[jax.ShapeDtypeStruct((HEADS, SKEW_W), F32), jax.ShapeDtypeStruct((HEADS, 128), F32)],
    )(dbias)
    return jnp.concatenate([cs[:, KW:N_CAP - 1:-1], tot[:, :1]], axis=1)


def _sgu_mask():
    pos = np.arange(SGU_BLOCK)
    return (pos[:, None] // CHUNK) >= (pos[None, :] // CHUNK)


def _group_stack(blk, first):
    return jnp.concatenate([jnp.where(first, blk, 0), jnp.where(first, 0, blk)], axis=0)


def _sgu_norm(zs_ref, lng, lnb):
    zs = zs_ref[...].astype(F32)
    ga, th = _gelu(zs)
    u, vs = ga[:, :D_SGU], ga[:, D_SGU:]
    mu = jnp.mean(vs, axis=-1, keepdims=True)
    cen = vs - mu
    rstd = lax.rsqrt(jnp.mean(cen * cen, axis=-1, keepdims=True) + EPS)
    xhat = cen * rstd
    return zs, th, u, xhat, rstd, xhat * lng + lnb


def _sgu_mix(vb, wm2_ref, bsx, s_ref):
    first = lax.broadcasted_iota(jnp.int32, (1, 128), 1) < 64
    for n in range(TM // SGU_BLOCK):
        for p in range(4):
            blk = vb[n * 128:(n + 1) * 128, p * 128:(p + 1) * 128]
            s_ref[n * 128:(n + 1) * 128, p * 128:(p + 1) * 128] = _nn(wm2_ref[p], _group_stack(blk, first)) + bsx[:, p * 128:(p + 1) * 128]


def _merge_fwd(x, zs, gt, y_att, lng, lnb, wm2, bsx, gw, comms=()):
    t = x.shape[0]

    def body(x_ref, zs_ref, gt_ref, ya_ref, lng_ref, lnb_ref, wm2_ref, bsx_ref, gw_ref, xo_ref, ys_ref, mg_ref,
             wbr, wo, s_scr, sems):
        @pl.when(pl.program_id(0) == 0)
        def _():
            for cp in _load_rows(gw_ref, wbr, OFF_BR, R_BR, sems.at[0]) + _load_rows(gw_ref, wo, OFF_WO, R_WO, sems.at[1]):
                cp.wait()

        _, _, u, _, _, vsn = _sgu_norm(zs_ref, lng_ref[...], lnb_ref[...])
        _sgu_mix(vsn.astype(BF16), wm2_ref, bsx_ref[...], s_scr)
        ys = (u * s_scr[...]).astype(BF16)
        ys_ref[...] = ys
        pa = _nt(ya_ref[...], wbr[:, :D_ATT])
        ps = _nt(ys, wbr[:, D_ATT:])
        mg = (gt_ref[:, :D].astype(F32) * pa + gt_ref[:, D:].astype(F32) * ps).astype(BF16)
        mg_ref[...] = mg
        xo_ref[...] = x_ref[...] + _nn(mg, wo[...])

    def tile(w):
        return pl.BlockSpec((TM, w), lambda i: (i, 0))

    def const(shape):
        return pl.BlockSpec(shape, lambda i: (0,) * len(shape))

    return _call(
        body, name="merge_fwd", grid=(t // TM,),
        in_specs=[tile(D), tile(2 * D_SGU), tile(2 * D), tile(D_ATT), const((1, D_SGU)), const((1, D_SGU)),
                  const((4, 128, 256)), const((128, D_SGU)), ANY],
        out_specs=[tile(D), tile(D_SGU), tile(D)],
        out_shape=[jax.ShapeDtypeStruct((t, D), F32), jax.ShapeDtypeStruct((t, D_SGU), BF16), jax.ShapeDtypeStruct((t, D), BF16)],
        scratch_shapes=[pltpu.VMEM((D, D), BF16), pltpu.VMEM((D, D), BF16), pltpu.VMEM((TM, D_SGU), F32),
                        pltpu.SemaphoreType.DMA((2, N_DEV))],
        args=(x, zs, gt, y_att, lng, lnb, wm2, bsx, gw), comms=comms)


def _merge_bwd(dx, gt, y_att, y_sgu, merged, gw):
    t = dx.shape[0]
    n_steps = t // TM

    def body(dx_ref, gt_ref, ya_ref, ys_ref, mg_ref, gw_ref, dzg_ref, dya_ref, dys_ref, dbg_ref, ga_ref, gs_ref, go_ref,
             wbr, wo, acc_a, acc_s, acc_o, sems):
        step = pl.program_id(0)

        @pl.when(step == 0)
        def _():
            cps = _load_rows(gw_ref, wbr, OFF_BR, R_BR, sems.at[0]) + _load_rows(gw_ref, wo, OFF_WO, R_WO, sems.at[1])
            dbg_ref[...] = jnp.zeros_like(dbg_ref)
            for acc in (acc_a, acc_s, acc_o):
                acc[...] = jnp.zeros_like(acc)
            for cp in cps:
                cp.wait()

        dxb = dx_ref[...].astype(BF16)
        acc_o[...] += _tn(mg_ref[...], dxb)
        dm = _nt(dxb, wo[...])
        for half, y_ref, w, acc in ((0, ya_ref, wbr.at[:, :D_ATT], acc_a), (1, ys_ref, wbr.at[:, D_ATT:], acc_s)):
            cols = slice(half * D, (half + 1) * D)
            gate = gt_ref[:, cols].astype(F32)
            branch = _nt(y_ref[...], w[...])
            dzg = dm * branch * gate * (1.0 - gate)
            dbg_ref[:, cols] += jnp.sum(dzg, axis=0, keepdims=True)
            dzg_ref[:, cols] = dzg.astype(BF16)
            dbr = (dm * gate).astype(BF16)
            acc[...] += _tn(dbr, y_ref[...])
            dy = _nn(dbr, w[...])
            if half == 0:
                dya_ref[...] = dy.astype(BF16)
            else:
                dys_ref[...] = dy

        @pl.when(step == n_steps - 1)
        def _():
            ga_ref[...] = acc_a[...].astype(BF16)
            gs_ref[...] = acc_s[...].astype(BF16)
            go_ref[...] = acc_o[...].astype(BF16)

    def tile(w):
        return pl.BlockSpec((TM, w), lambda i: (i, 0))

    def whole(r, c):
        return pl.BlockSpec((r, c), lambda i: (0, 0))

    dzg, dya, dys, dbg, g_ba, g_bs, g_wo = pl.pallas_call(
        body, name="merge_bwd", grid=(n_steps,),
        in_specs=[tile(D), tile(2 * D), tile(D_ATT), tile(D_SGU), tile(D), ANY],
        out_specs=[tile(2 * D), tile(D_ATT), tile(D_SGU), whole(1, 2 * D), whole(D, D_ATT), whole(D, D_SGU), whole(D, D)],
        out_shape=[jax.ShapeDtypeStruct((t, 2 * D), BF16), jax.ShapeDtypeStruct((t, D_ATT), BF16), jax.ShapeDtypeStruct((t, D_SGU), F32),
                   jax.ShapeDtypeStruct((1, 2 * D), F32), jax.ShapeDtypeStruct((D, D_ATT), BF16), jax.ShapeDtypeStruct((D, D_SGU), BF16),
                   jax.ShapeDtypeStruct((D, D), BF16)],
        scratch_shapes=[pltpu.VMEM((D, D), BF16), pltpu.VMEM((D, D), BF16), pltpu.VMEM((D, D_ATT), F32), pltpu.VMEM((D, D_SGU), F32),
                        pltpu.VMEM((D, D), F32), pltpu.SemaphoreType.DMA((2, N_DEV))],
        compiler_params=_cparams(("arbitrary",)),
    )(dx, gt, y_att, y_sgu, merged, gw)
    return dzg, dya, dys, dbg, g_ba.reshape(N_DEV, R_BR, D_ATT), g_bs.reshape(N_DEV, R_BR, D_SGU), g_wo.reshape(N_DEV, R_WO, D)


def _sgu_bwd(zs, dys, lng, lnb, wm2, wmt2, bsx, comms=()):
    t = zs.shape[0]
    n_steps = t // TM

    def body(zs_ref, dys_ref, lng_ref, lnb_ref, wm2_ref, wmt2_ref, bsx_ref, dzs_ref, dw_ref, dbs_ref, dlg_ref, dlb_ref,
             s_scr, dv_scr, ds_acc):
        i = pl.program_id(0)

        @pl.when(i == 0)
        def _():
            dw_ref[...] = jnp.zeros_like(dw_ref)
            dlg_ref[...] = jnp.zeros_like(dlg_ref)
            dlb_ref[...] = jnp.zeros_like(dlb_ref)
            ds_acc[...] = jnp.zeros_like(ds_acc)

        lng = lng_ref[...]
        zs, th, u, xhat, rstd, vsn = _sgu_norm(zs_ref, lng, lnb_ref[...])
        vb = vsn.astype(BF16)
        _sgu_mix(vb, wm2_ref, bsx_ref[...], s_scr)
        dys = dys_ref[...]
        du = dys * s_scr[...]
        ds = dys * u
        dsb = ds.astype(BF16)
        first = lax.broadcasted_iota(jnp.int32, (1, 128), 1) < 64
        acc = jnp.zeros((SGU_BLOCK, D_SGU), F32)
        for n in range(TM // SGU_BLOCK):
            rows = slice(n * 128, (n + 1) * 128)
            acc = acc + ds[rows]
            for p in range(4):
                lanes = slice(p * 128, (p + 1) * 128)
                stack = _group_stack(dsb[rows, lanes], first)
                dv_scr[rows, lanes] = _nn(wmt2_ref[p], stack)
                dw_ref[p] += _nt(stack, vb[rows, lanes])
        ds_acc[...] += acc
        dvsn = dv_scr[...]
        dlg_ref[...] += jnp.sum(dvsn * xhat, axis=0, keepdims=True)
        dlb_ref[...] += jnp.sum(dvsn, axis=0, keepdims=True)
        dxh = dvsn * lng
        dvs = rstd * (dxh - jnp.mean(dxh, axis=-1, keepdims=True) - xhat * jnp.mean(dxh * xhat, axis=-1, keepdims=True))
        dga = jnp.concatenate([du, dvs], axis=1)
        dzs_ref[...] = (dga * _gelu_grad(zs, th)).astype(BF16)

        @pl.when(i == n_steps - 1)
        def _():
            r = lax.broadcasted_iota(jnp.int32, (256, 128), 0) % SGU_BLOCK
            c = lax.broadcasted_iota(jnp.int32, (256, 128), 1)
            keep = (r // CHUNK) >= (c // CHUNK)
            for p in range(4):
                dw_ref[p] = jnp.where(keep, dw_ref[p], 0.0)
            total = ds_acc[...]
            grp = lax.broadcasted_iota(jnp.int32, (SGU_BLOCK, D_SGU), 1) // 64
            lane = lax.broadcasted_iota(jnp.int32, (SGU_BLOCK, 128), 1)
            out = jnp.zeros((SGU_BLOCK, 128), F32)
            for gi in range(8):
                out = jnp.where(lane == gi, jnp.sum(jnp.where(grp == gi, total, 0.0), axis=1, keepdims=True), out)
            dbs_ref[...] = out

    def tile(w):
        return pl.BlockSpec((TM, w), lambda i: (i, 0))

    def const(shape):
        return pl.BlockSpec(shape, lambda i: (0,) * len(shape))

    return _call(
        body, name="sgu_bwd", grid=(n_steps,),
        in_specs=[tile(2 * D_SGU), tile(D_SGU), const((1, D_SGU)), const((1, D_SGU)), const((4, 128, 256)), const((4, 128, 256)),
                  const((128, D_SGU))],
        out_specs=[tile(2 * D_SGU), const((4, 256, 128)), const((128, 128)), const((1, D_SGU)), const((1, D_SGU))],
        out_shape=[jax.ShapeDtypeStruct((t, 2 * D_SGU), BF16), jax.ShapeDtypeStruct((4, 256, 128), F32),
                   jax.ShapeDtypeStruct((128, 128), F32), jax.ShapeDtypeStruct((1, D_SGU), F32), jax.ShapeDtypeStruct((1, D_SGU), F32)],
        scratch_shapes=[pltpu.VMEM((TM, D_SGU), F32), pltpu.VMEM((TM, D_SGU), F32), pltpu.VMEM((SGU_BLOCK, D_SGU), F32)],
        args=(zs, dys, lng, lnb, wm2, wmt2, bsx), comms=comms)


def _adamw(g, w, m, v):
    m = ADAM_B1 * m + (1.0 - ADAM_B1) * g
    v = ADAM_B2 * v + (1.0 - ADAM_B2) * (g * g)
    m_hat = m / (1.0 - ADAM_B1 ** ADAM_STEP)
    v_hat = v / (1.0 - ADAM_B2 ** ADAM_STEP)
    return -ADAM_LR * (m_hat / (jnp.sqrt(v_hat) + ADAM_EPS) + ADAM_WD * w), m, v


def _adamw_matrix(parts, sums, w, m, v, transposed, name):
    _, r, c = parts.shape
    tc = 256

    def body(own_ref, p_ref, s_ref, w_ref, m_ref, v_ref, g_ref, d_ref, mo_ref, vo_ref):
        g = p_ref[0].astype(F32) + p_ref[1].astype(F32) + p_ref[2].astype(F32) + s_ref[...].astype(F32)
        g = g.T if transposed else g
        g_ref[...] = g
        d_ref[...], mo_ref[...], vo_ref[...] = _adamw(g, w_ref[...], m_ref[...], v_ref[...])

    own = pl.BlockSpec((None, tc, r), lambda i, o: (0, i, 0)) if transposed else pl.BlockSpec((None, r, tc), lambda i, o: (0, 0, i))
    return pl.pallas_call(
        body, name=name,
        grid_spec=pltpu.PrefetchScalarGridSpec(
            num_scalar_prefetch=1, grid=(c // tc,),
            in_specs=[pl.BlockSpec((3, r, tc), lambda i, o: (0, 0, i)), pl.BlockSpec((None, r, tc), lambda i, o: (o[0], 0, i)), own, own, own],
            out_specs=[own] * 4),
        out_shape=[jax.ShapeDtypeStruct(w.shape, F32)] * 4,
        compiler_params=_cparams(("arbitrary",)),
    )(_my_index(("x", 2), ("y", 1)), parts, sums, w, m, v)


_SMALL_2D = {"norm_ffn1": (1, D), "norm_mix": (1, D), "norm_ffn2": (1, D), "norm_final": (1, D), "b_gate": (1, 2 * D),
             "sgu_ln_g": (1, D_SGU), "sgu_ln_b": (1, D_SGU), "sgu_b_s": (8, SGU_BLOCK), "rel_bias": (HEADS, N_REL),
             "sgu_w_s": (8 * SGU_BLOCK, SGU_BLOCK)}


def _adamw_small(parts, loss_parts, p):
    names = list(parts)
    n = len(names)

    def body(*refs):
        got, loss_got, wmv, outs, loss_out = refs[:n], refs[n], refs[n + 1:4 * n + 1], refs[4 * n + 1:8 * n + 1], refs[8 * n + 1]
        for i, name in enumerate(names):
            g = got[i][0]
            for k in range(1, N_DEV):
                g = g + got[i][k]
            if name == "sgu_b_s":
                g = g.T[0:8, :]
            res = (g,) + _adamw(g, wmv[3 * i][...], wmv[3 * i + 1][...], wmv[3 * i + 2][...])
            for o_ref, val in zip(outs[4 * i:4 * i + 4], res):
                o_ref[...] = val
        total = loss_got[0]
        for k in range(1, N_DEV):
            total = total + loss_got[k]
        loss_out[...] = total

    wmv = [p[pre + name].reshape(_SMALL_2D[name]) for name in names for pre in ("", "m_", "v_")]
    res = pl.pallas_call(
        body, name="adamw_small",
        out_shape=[jax.ShapeDtypeStruct(_SMALL_2D[name], F32) for name in names for _ in range(4)] + [jax.ShapeDtypeStruct((1, 128), F32)],
        compiler_params=_cparams())(*[parts[name] for name in names], loss_parts, *wmv)
    return [{name: res[4 * i + j].reshape(p[name].shape) for i, name in enumerate(names)} for j in range(4)], res[-1]


def _pack_rows(groups, name):
    flat = [a for grp in groups for a, _ in grp]
    rows = [grp[0][0].shape[2] if grp[0][1] else grp[0][0].shape[1] for grp in groups]

    def body(*refs):
        o_ref, pos, off = refs[-1], 0, 0
        for grp, r in zip(groups, rows):
            vals = []
            for _, transposed in grp:
                val = refs[pos][0]
                vals.append(val.T if transposed else val)
                pos += 1
            o_ref[off:off + r, :] = (vals[0] if len(vals) == 1 else jnp.concatenate(vals, axis=1)).astype(BF16)
            off += r

    return pl.pallas_call(body, name=name, out_shape=jax.ShapeDtypeStruct((sum(rows), D), BF16), compiler_params=_cparams())(*flat)


def _step(x, target, p):
    n1, nm, n2 = p["norm_ffn1"], p["norm_mix"], p["norm_ffn2"]
    nf = p["norm_final"].reshape(1, D)
    lng, lnb = p["sgu_ln_g"], p["sgu_ln_b"]
    w_m = jnp.where(jnp.asarray(_sgu_mask())[None], p["sgu_w_s"][0], 0.0).astype(BF16)
    wm2 = jnp.concatenate([w_m[0::2], w_m[1::2]], axis=2)
    w_mt = w_m.transpose(0, 2, 1)
    wmt2 = jnp.concatenate([w_mt[0::2], w_mt[1::2]], axis=2)
    bsx = jnp.repeat(p["sgu_b_s"][0].T, 64, axis=1)
    bias = _band_bias(p["rel_bias"][0])

    def chip_sums(grads, name):
        gots = _comm_only([_SiblingSwap(grads)], "swap_" + name)
        return _pair_sums(grads, gots, "pair_sums_" + name)
    def as_rows(a):
        return jnp.swapaxes(a, 1, 2)

    def updates(parts, sums, names):
        res = {}
        for pt, sm, n in zip(parts, sums, names):
            if p[n].shape[1:] == pt.shape[1:]:
                res[n] = _adamw_matrix(pt, sm, p[n], p["m_" + n], p["v_" + n], False, "adamw_" + n)
            elif p[n].shape[2] > 128:
                res[n] = [as_rows(o) for o in _adamw_matrix(pt, sm, as_rows(p[n]), as_rows(p["m_" + n]), as_rows(p["v_" + n]), False,
                                                            "adamw_" + n)]
            else:
                res[n] = _adamw_matrix(pt, sm, p[n], p["m_" + n], p["v_" + n], True, "adamw_" + n)
        return res

    rows1 = _pack_rows([[(as_rows(p["ffn1_w_gate"]), False)], [(as_rows(p["ffn1_w_up"]), False)], [(p["ffn1_w_down"], False)]], "pack_ffn1")
    rows_m = _pack_rows([[(as_rows(p["w_in"]), False)], [(p["w_branch_att"], True), (p["w_branch_sgu"], True)], [(p["w_out"], False)]],
                        "pack_mixer")
    rows2d = _pack_rows([[(p["ffn2_w_down"], False)]], "pack_ffn2_down")
    rows2gu = _pack_rows([[(as_rows(p["ffn2_w_gate"]), False)], [(as_rows(p["ffn2_w_up"]), False)]], "pack_ffn2_gate_up")
    (gw1,) = _comm_only([_Gather(rows1)], "gather_ffn1")
    x1, ab1, h1, (gwm,) = _ffn_fwd(x, n1, gw1, gw1, 2 * R_FF, "ffn1_fwd", [_Gather(rows_m)])
    (q, k, v, zs, gt, h2), (gw2gu,) = _mix_proj_fwd(x1, nm, p["b_gate"], gwm, [_Gather(rows2gu)])
    y_att, _ = _att_fwd(q, k, v, bias)
    (x2, y_sgu, merged), (gw2d,) = _merge_fwd(x1, zs, gt, y_att, lng, lnb, wm2, bsx, gwm, [_Gather(rows2d)])
    dx3, ab2, hb, _, d_nf, loss = _ffn_fwd(x2, n2, gw2gu, gw2d, 0, "ffn2_fwd", head=(nf, target))

    (dab, g_down), _ = _ffn_bwd_hidden(ab2, dx3, gw2d, 0, "ffn2_bwd_hidden")
    (dx2, d_n2), _ = _ffn_bwd_input(x2, n2, dab, dx3, gw2gu, "ffn2_bwd")
    g_gu = _weight_grad(dab, hb, "ffn2_dw_gate_up", mats=2)
    dzg, dya, dys, d_bg, g_ba, g_bs, g_wo = _merge_bwd(dx2, gt, y_att, y_sgu, merged, gwm)
    g_late = [(g_gu, 0), (g_gu, 1), (g_down, 0), (g_ba, 0), (g_bs, 0), (g_wo, 0)]
    late = ("ffn2_w_gate", "ffn2_w_up", "ffn2_w_down", "w_branch_att", "w_branch_sgu", "w_out")
    (dzs, d_wm, d_bs, d_lng, d_lnb), gots_late = _sgu_bwd(zs, dys, lng, lnb, wm2, wmt2, bsx, [_SiblingSwap(g_late)])
    sums_late = _pair_sums(g_late, gots_late, "pair_sums_late")
    (dq, dk, dv, d_bias), parts_late = _att_bwd(q, k, v, bias, dya, [_ChipScatter(sums_late)])
    big = updates(parts_late, sums_late, late)
    d_rel = _rel_bias_grad(d_bias)
    dz = jnp.concatenate([dq, dk, dv, dzs, dzg], axis=1)
    sums_in = chip_sums([(_weight_grad(dz, h2, "dw_in"), 0)], "w_in")
    (dx1, d_nm), parts_in = _mix_proj_bwd(dz, x1, nm, dx2, gwm, [_ChipScatter(sums_in)])
    big.update(updates(parts_in, sums_in, ("w_in",)))
    small = {"norm_ffn2": d_n2, "norm_final": d_nf, "b_gate": d_bg, "sgu_ln_g": d_lng, "sgu_ln_b": d_lnb, "sgu_b_s": d_bs,
             "rel_bias": d_rel, "sgu_w_s": d_wm.reshape(_SMALL_2D["sgu_w_s"]), "norm_mix": d_nm}

    (dab, g_down), (*small_parts, loss_parts) = _ffn_bwd_hidden(ab1, dx1, gw1, 2 * R_FF, "ffn1_bwd_hidden",
                                                                [_AllToAll(list(small.values()) + [loss])])
    sums_d = chip_sums([(g_down, 0)], "ffn1_down")
    g_gu, (parts_d,) = _weight_grad(dab, h1, "ffn1_dw_gate_up", comms=[_ChipScatter(sums_d)], mats=2)
    sums_gu = chip_sums([(g_gu, 0), (g_gu, 1)], "ffn1_gate_up")
    (dx0, d_n1), parts_gu = _ffn_bwd_input(x, n1, dab, dx1, gw1, "ffn1_bwd", [_ChipScatter(sums_gu)])
    (n1_parts,) = _comm_only([_AllToAll([d_n1])], "gather_norm_ffn1")
    big.update(updates([parts_d] + parts_gu, sums_d + sums_gu, ("ffn1_w_down", "ffn1_w_gate", "ffn1_w_up")))
    out_s, loss_sum = _adamw_small(dict(zip(small, small_parts), norm_ffn1=n1_parts), loss_parts, p)
    return dx0, loss_sum[0, 0], [{**{n: four[i] for n, four in big.items()}, **s} for i, s in enumerate(out_s)]


_OUT_ORDER = ("norm_ffn1", "ffn1_w_gate", "ffn1_w_up", "ffn1_w_down", "norm_mix", "w_in", "b_gate", "rel_bias", "sgu_ln_g", "sgu_ln_b",
              "sgu_w_s", "sgu_b_s", "w_branch_att", "w_branch_sgu", "w_out", "norm_ffn2", "ffn2_w_gate", "ffn2_w_up", "ffn2_w_down",
              "norm_final")


def kernel(x, norm_ffn1, ffn1_w_gate, ffn1_w_up, ffn1_w_down, norm_mix, w_in, b_gate, rel_bias, sgu_ln_g, sgu_ln_b, sgu_w_s, sgu_b_s, w_branch_att, w_branch_sgu, w_out, norm_ffn2, ffn2_w_gate, ffn2_w_up, ffn2_w_down, norm_final, loss_target, m_norm_ffn1, m_ffn1_w_gate, m_ffn1_w_up, m_ffn1_w_down, m_norm_mix, m_w_in, m_b_gate, m_rel_bias, m_sgu_ln_g, m_sgu_ln_b, m_sgu_w_s, m_sgu_b_s, m_w_branch_att, m_w_branch_sgu, m_w_out, m_norm_ffn2, m_ffn2_w_gate, m_ffn2_w_up, m_ffn2_w_down, m_norm_final, v_norm_ffn1, v_ffn1_w_gate, v_ffn1_w_up, v_ffn1_w_down, v_norm_mix, v_w_in, v_b_gate, v_rel_bias, v_sgu_ln_g, v_sgu_ln_b, v_sgu_w_s, v_sgu_b_s, v_w_branch_att, v_w_branch_sgu, v_w_out, v_norm_ffn2, v_ffn2_w_gate, v_ffn2_w_up, v_ffn2_w_down, v_norm_final):
    args = dict(locals())
    dx, loss, outs = _step(x[0], loss_target[0], {pre + n: args[pre + n] for pre in ("", "m_", "v_") for n in _OUT_ORDER})
    return (loss, dx[None], *[o[n] for o in outs for n in _OUT_ORDER])
```

```python
import functools

import numpy as np
import jax
import jax.numpy as jnp
from jax import lax
from jax.experimental import pallas as pl
from jax.experimental.pallas import tpu as pltpu

F32 = jnp.float32
BF16 = jnp.bfloat16

N_DEV = 8
D = 1024
F = 2816
D_ATT = 512
D_SGU = 512
D_IN = 4608
HEADS = 8
CHUNK = 64
N_LEFT = 8
REL_CLIP = 256
N_REL = 2 * REL_CLIP + 1
SGU_BLOCK = 128
EPS = 1e-6
NEG_INF = -1e30
QB = 256
KW = 3 * QB

R_FF, R_IN, R_BR, R_WO = F // N_DEV, D_IN // N_DEV, D // N_DEV, D // N_DEV
OFF_IN, OFF_BR, OFF_WO = 0, R_IN, R_IN + R_BR

FC = 256
TM = 512
VMEM_LIMIT = 56 * 1024 * 1024

ADAM_LR, ADAM_B1, ADAM_B2, ADAM_EPS, ADAM_WD, ADAM_STEP = 0.001, 0.9, 0.999, 1e-08, 0.01, 10

MESH = pl.DeviceIdType.MESH
ANY = pl.BlockSpec(memory_space=pl.ANY)


def _nt(a, b):
    return lax.dot_general(a, b, (((1,), (1,)), ((), ())), preferred_element_type=F32)


def _tn(a, b):
    return lax.dot_general(a, b, (((0,), (0,)), ((), ())), preferred_element_type=F32)


def _nn(a, b):
    return jnp.dot(a, b, preferred_element_type=F32)


def _cparams(sem=None):
    return pltpu.CompilerParams(dimension_semantics=sem, vmem_limit_bytes=VMEM_LIMIT)


def _load_rows(gw_ref, dst, off, rows, sems):
    copies = [pltpu.make_async_copy(gw_ref.at[k, pl.ds(off, rows), :], dst.at[pl.ds(k * rows, rows), :], sems.at[k])
              for k in range(N_DEV)]
    for cp in copies:
        cp.start()
    return copies


def _rms(xv):
    r = lax.rsqrt(jnp.mean(xv * xv, axis=-1, keepdims=True) + EPS)
    return xv * r, r


def _rms_bwd(dh, xn, r, gain):
    dxn = dh * gain
    dx = r * (dxn - xn * jnp.mean(dxn * xn, axis=-1, keepdims=True))
    return dx, jnp.sum(dh * xn, axis=0, keepdims=True)


def _gelu(x):
    t = jnp.tanh(0.7978845608028654 * (x + 0.044715 * x * x * x))
    return 0.5 * x * (1.0 + t), t


def _gelu_grad(x, t):
    return 0.5 * (1.0 + t) + 0.5 * x * (1.0 - t * t) * 0.7978845608028654 * (1.0 + 3.0 * 0.044715 * x * x)


def _place():
    x, y, cc = lax.axis_index("x"), lax.axis_index("y"), lax.axis_index("c")
    return x, y, cc, [(1 - x, y), (x, 1 - y), (1 - x, 1 - y)]


class _Gather:
    def __init__(self, shard):
        self.inputs = [shard]
        self.out_shape = [jax.ShapeDtypeStruct((N_DEV,) + shard.shape, shard.dtype)]
        self.scratch = [pltpu.SemaphoreType.DMA((7,)), pltpu.SemaphoreType.DMA((7,)), pltpu.SemaphoreType.DMA]

    def _copies(self, ins, outs, scr):
        (x_ref,), (out_ref,), (send_sems, recv_sems, local_sem) = ins, outs, scr
        x, y, cc, chips = _place()

        def slab(px, py, pc):
            return out_ref.at[4 * px + 2 * py + pc]

        def copy(k, block, to, src=None):
            return pltpu.make_async_remote_copy(
                src_ref=slab(*block) if src is None else src, dst_ref=slab(*block),
                send_sem=send_sems.at[k], recv_sem=recv_sems.at[k], device_id=to, device_id_type=MESH)

        me, sibling = (x, y, cc), (x, y, 1 - cc)
        x_nbr, y_nbr, diagonal = chips
        mine = pltpu.make_async_copy(x_ref, slab(*me), local_sem)
        first = [copy(0, me, sibling, src=x_ref), copy(1, me, (*x_nbr, cc), src=x_ref), copy(2, me, (*y_nbr, cc), src=x_ref)]
        neighbours = [copy(1, (*x_nbr, cc), me), copy(2, (*y_nbr, cc), me)]
        second_hand = copy(3, (x ^ (1 - cc), y ^ cc, cc), (x ^ cc, y ^ (1 - cc), cc))
        from_diagonal = copy(3, (*diagonal, cc), me)
        passed = [copy(4 + j, (*chip, cc), sibling) for j, chip in enumerate(chips)]
        from_sibling = [copy(0, sibling, me)] + [copy(4 + j, (*chip, 1 - cc), me) for j, chip in enumerate(chips)]
        return mine, first, neighbours, second_hand, from_diagonal, passed, from_sibling

    def begin(self, *refs):
        mine, first = self._copies(*refs)[:2]
        mine.start()
        for cp in first:
            cp.start()

    def mid(self, *refs):
        _, _, neighbours, second_hand, _, passed, _ = self._copies(*refs)
        for cp in neighbours:
            cp.wait_recv()
        second_hand.start()
        passed[0].start()
        passed[1].start()

    def relay(self, *refs):
        _, _, _, _, from_diagonal, passed, _ = self._copies(*refs)
        from_diagonal.wait_recv()
        passed[2].start()

    def end(self, *refs):
        mine, first, _, second_hand, _, passed, from_sibling = self._copies(*refs)
        for cp in from_sibling:
            cp.wait_recv()
        for cp in first + [second_hand] + passed:
            cp.wait_send()
        mine.wait()


class _Direct:
    def begin(self, *refs):
        keep, give = self._copies(*refs)
        for cp in keep + give:
            cp.start()

    def mid(self, *refs):
        pass

    def relay(self, *refs):
        pass

    def end(self, *refs):
        keep, give = self._copies(*refs)
        for cp in give:
            cp.wait_recv()
        for cp in give:
            cp.wait_send()
        for cp in keep:
            cp.wait()


class _SiblingSwap(_Direct):
    def __init__(self, grads):
        n = len(grads)
        self.which = [w for _, w in grads]
        self.inputs = [g for g, _ in grads]
        self.out_shape = [jax.ShapeDtypeStruct((4,) + g.shape[1:], g.dtype) for g, _ in grads]
        self.scratch = [pltpu.SemaphoreType.DMA((n, 4)), pltpu.SemaphoreType.DMA((n, 4))]

    def _copies(self, ins, outs, scr):
        send_sems, recv_sems = scr
        x, y, cc, _ = _place()
        return [], [pltpu.make_async_remote_copy(src_ref=g_ref.at[N_DEV * w + 2 * j + 1 - cc], dst_ref=got_ref.at[j],
                                                 send_sem=send_sems.at[i, j], recv_sem=recv_sems.at[i, j], device_id=(x, y, 1 - cc),
                                                 device_id_type=MESH)
                    for i, (g_ref, got_ref, w) in enumerate(zip(ins, outs, self.which)) for j in range(4)]


class _ChipScatter(_Direct):
    def __init__(self, sums):
        n = len(sums)
        self.inputs = list(sums)
        self.out_shape = [jax.ShapeDtypeStruct((3,) + s.shape[1:], s.dtype) for s in sums]
        self.scratch = [pltpu.SemaphoreType.DMA((n, 3)), pltpu.SemaphoreType.DMA((n, 3))]

    def _copies(self, ins, outs, scr):
        send_sems, recv_sems = scr
        _, _, cc, chips = _place()
        return [], [pltpu.make_async_remote_copy(src_ref=s_ref.at[2 * px + py], dst_ref=got_ref.at[j], send_sem=send_sems.at[i, j],
                                                 recv_sem=recv_sems.at[i, j], device_id=(px, py, cc), device_id_type=MESH)
                    for i, (s_ref, got_ref) in enumerate(zip(ins, outs)) for j, (px, py) in enumerate(chips)]


class _AllToAll(_Direct):
    def __init__(self, blocks):
        n = len(blocks)
        self.inputs = list(blocks)
        self.out_shape = [jax.ShapeDtypeStruct((N_DEV,) + b.shape, b.dtype) for b in blocks]
        self.scratch = [pltpu.SemaphoreType.DMA((n, 7)), pltpu.SemaphoreType.DMA((n, 7)), pltpu.SemaphoreType.DMA((n,))]

    def _copies(self, ins, outs, scr):
        send_sems, recv_sems, local_sems = scr
        x, y, cc, _ = _place()
        me = 4 * x + 2 * y + cc
        keep = [pltpu.make_async_copy(b_ref, got_ref.at[me], local_sems.at[i]) for i, (b_ref, got_ref) in enumerate(zip(ins, outs))]
        give = [pltpu.make_async_remote_copy(src_ref=b_ref, dst_ref=got_ref.at[me], send_sem=send_sems.at[i, k - 1],
                                             recv_sem=recv_sems.at[i, k - 1],
                                             device_id=(x ^ ((k >> 2) & 1), y ^ ((k >> 1) & 1), cc ^ (k & 1)), device_id_type=MESH)
                for i, (b_ref, got_ref) in enumerate(zip(ins, outs)) for k in range(1, N_DEV)]
        return keep, give


def _split_refs(refs, counts):
    out, pos = [], 0
    for n in counts:
        out.append(list(refs[pos:pos + n]))
        pos += n
    return out


def _bind(comms, c_in, c_out, c_scr):
    ins = _split_refs(c_in, [len(c.inputs) for c in comms])
    outs = _split_refs(c_out, [len(c.out_shape) for c in comms])
    scr = _split_refs(c_scr, [len(c.scratch) for c in comms])
    return [(c, (i, o, s)) for c, i, o, s in zip(comms, ins, outs, scr)]


def _call(body, *, name, grid, in_specs, out_specs, out_shape, scratch_shapes, args, comms=()):
    c_in = [a for c in comms for a in c.inputs]
    c_out = [s for c in comms for s in c.out_shape]
    c_scr = [s for c in comms for s in c.scratch]
    counts = [len(in_specs), len(c_in), len(out_shape), len(c_out), len(scratch_shapes), len(c_scr)]

    def full(*refs):
        ins, cin, outs, cout, scr, cscr = _split_refs(refs, counts)
        bound = _bind(comms, cin, cout, cscr)
        if comms:
            def at(steps):
                return functools.reduce(jnp.logical_and, [pl.program_id(ax) == s for ax, s in enumerate(steps)])

            first, last = at([0] * len(grid)), at([n - 1 for n in grid])

            @pl.when(first)
            def _():
                for c, r in bound:
                    c.begin(*r)

            @pl.when(at([(grid[0] - 1) // 2] + [0] * (len(grid) - 1)))
            def _():
                for c, r in bound:
                    c.mid(*r)

            @pl.when(at([3 * (grid[0] - 1) // 4] + [0] * (len(grid) - 1)))
            def _():
                for c, r in bound:
                    c.relay(*r)

        body(*ins, *outs, *scr)
        if comms:
            @pl.when(last)
            def _():
                for c, r in bound:
                    c.end(*r)

    res = pl.pallas_call(
        full, name=name, grid=grid,
        in_specs=list(in_specs) + [ANY] * len(c_in), out_specs=list(out_specs) + [ANY] * len(c_out),
        out_shape=list(out_shape) + c_out, scratch_shapes=list(scratch_shapes) + c_scr,
        compiler_params=_cparams(("arbitrary",) * len(grid)),
    )(*args, *c_in)
    return list(res[:len(out_shape)]), list(res[len(out_shape):])


def _comm_only(comms, name):
    c_in = [a for c in comms for a in c.inputs]
    c_out = [s for c in comms for s in c.out_shape]
    c_scr = [s for c in comms for s in c.scratch]

    def full(*refs):
        cin, cout, cscr = _split_refs(refs, [len(c_in), len(c_out), len(c_scr)])
        bound = _bind(comms, cin, cout, cscr)
        for phase in ("begin", "mid", "relay", "end"):
            for c, r in bound:
                getattr(c, phase)(*r)

    return list(pl.pallas_call(full, name=name, in_specs=[ANY] * len(c_in), out_specs=[ANY] * len(c_out), out_shape=c_out,
                               scratch_shapes=c_scr)(*c_in))


def _my_index(*axes_and_weights):
    return sum(w * lax.axis_index(a) for a, w in axes_and_weights).astype(jnp.int32).reshape(1)


def _pair_sums(grads, gots, name):
    n = len(grads)

    def body(c_ref, *refs):
        for a_ref, b_ref, o_ref in zip(refs[:n], refs[n:2 * n], refs[2 * n:]):
            o_ref[...] = (a_ref[...].astype(F32) + b_ref[...].astype(F32)).astype(BF16)

    def tile(g):
        return pl.BlockSpec((1,) + g.shape[1:], lambda j, c_ref: (j, 0, 0))

    def mine(g, w):
        return pl.BlockSpec((1, None) + g.shape[1:], lambda j, c_ref: (4 * w + j, c_ref[0], 0, 0))

    return list(pl.pallas_call(
        body, name=name,
        grid_spec=pltpu.PrefetchScalarGridSpec(num_scalar_prefetch=1, grid=(4,),
                                               in_specs=[mine(g, w) for g, w in grads] + [tile(g) for g in gots],
                                               out_specs=[tile(g) for g in gots]),
        out_shape=[jax.ShapeDtypeStruct(g.shape, BF16) for g in gots],
        compiler_params=_cparams(("arbitrary",)))(_my_index(("c", 1)), *[g.reshape((-1, 2) + g.shape[1:]) for g, _ in grads], *gots))


def _ffn_fwd(x, gain, gw_gu, gw_d, off_d, name, comms=(), head=None):
    t = x.shape[0]
    n_head = 0 if head is None else 2

    def body(x_ref, g_ref, gu_ref, d_ref, *refs):
        head_in, (o_ref, ab_ref, h_ref), head_out = refs[:n_head], refs[n_head:n_head + 3], refs[n_head + 3:2 * n_head + 3]
        wg, wu, wd, s_scr, sems = refs[2 * n_head + 3:]

        @pl.when(pl.program_id(0) == 0)
        def _():
            cps = _load_rows(gu_ref, wg, 0, R_FF, sems.at[0]) + _load_rows(gu_ref, wu, R_FF, R_FF, sems.at[1]) \
                + _load_rows(d_ref, wd, off_d, R_FF, sems.at[2])
            for o in head_out:
                o[...] = jnp.zeros_like(o)
            for cp in cps:
                cp.wait()

        xv = x_ref[...]
        xn, _ = _rms(xv)
        h = (xn * g_ref[...]).astype(BF16)
        h_ref[...] = h
        for c in range(F // FC):
            rows = pl.ds(c * FC, FC)
            a = _nt(h, wg[rows, :])
            b = _nt(h, wu[rows, :])
            ab_ref[:, c * FC:(c + 1) * FC] = a.astype(BF16)
            ab_ref[:, F + c * FC:F + (c + 1) * FC] = b.astype(BF16)
            s_scr[:, c * FC:(c + 1) * FC] = (a * jax.nn.sigmoid(a) * b).astype(BF16)
        out = xv + 0.5 * _nn(s_scr[...], wd[...])
        if head is None:
            o_ref[...] = out
        else:
            (gf_ref, t_ref), (dg_ref, loss_ref) = head_in, head_out
            gain_f = gf_ref[...]
            yn, r = _rms(out)
            err = yn * gain_f - t_ref[...]
            loss_ref[...] += 0.5 * jnp.sum(jnp.mean(err * err, axis=-1, keepdims=True), axis=0, keepdims=True)
            o_ref[...], dg = _rms_bwd(err * (1.0 / D), yn, r, gain_f)
            dg_ref[...] += dg

    tile = pl.BlockSpec((TM, D), lambda i: (i, 0))
    row = pl.BlockSpec((1, D), lambda i: (0, 0))
    head_specs = [] if head is None else [row, pl.BlockSpec((1, 128), lambda i: (0, 0))]
    head_shapes = [] if head is None else [jax.ShapeDtypeStruct((1, D), F32), jax.ShapeDtypeStruct((1, 128), F32)]
    res, got = _call(
        body, name=name, grid=(t // TM,),
        in_specs=[tile, row, ANY, ANY] + ([] if head is None else [row, tile]),
        out_specs=[tile, pl.BlockSpec((TM, 2 * F), lambda i: (i, 0)), tile] + head_specs,
        out_shape=[jax.ShapeDtypeStruct((t, D), F32), jax.ShapeDtypeStruct((t, 2 * F), BF16), jax.ShapeDtypeStruct((t, D), BF16)] + head_shapes,
        scratch_shapes=[pltpu.VMEM((F, D), BF16)] * 3 + [pltpu.VMEM((TM, F), BF16), pltpu.SemaphoreType.DMA((3, N_DEV))],
        args=(x, gain, gw_gu, gw_d) + (() if head is None else tuple(head)), comms=comms)
    return (res[0], res[1], res[2], got, *res[3:])


def _ffn_bwd_hidden(ab, dout, gw_d, off_d, name, comms=()):
    t = ab.shape[0]
    n_steps = t // TM
    row_chunks = [(r, min(512, F - r)) for r in range(0, F, 512)]

    def tile(w):
        return pl.BlockSpec((TM, w), lambda i: (i, 0))

    def hidden(ab_ref, do_ref, d_ref, dab_ref, dwd_ref, wd, s_scr, acc, sems, out_sem):
        step = pl.program_id(0)

        @pl.when(step == 0)
        def _():
            cps = _load_rows(d_ref, wd, off_d, R_FF, sems)
            acc[...] = jnp.zeros_like(acc)
            for cp in cps:
                cp.wait()

        df = (0.5 * do_ref[...]).astype(BF16)
        for c in range(F // FC):
            a = ab_ref[:, c * FC:(c + 1) * FC].astype(F32)
            b = ab_ref[:, F + c * FC:F + (c + 1) * FC].astype(F32)
            sg = jax.nn.sigmoid(a)
            sl = a * sg
            ds = _nt(df, wd[pl.ds(c * FC, FC), :])
            dab_ref[:, c * FC:(c + 1) * FC] = (ds * b * (sg * (1.0 + a * (1.0 - sg)))).astype(BF16)
            dab_ref[:, F + c * FC:F + (c + 1) * FC] = (ds * sl).astype(BF16)
            s_scr[:, c * FC:(c + 1) * FC] = (sl * b).astype(BF16)
        for r, n in row_chunks:
            acc[r:r + n, :] += _tn(s_scr[:, r:r + n], df)

        @pl.when(step == n_steps - 1)
        def _():
            wd[...] = acc[...].astype(BF16)
            out = pltpu.make_async_copy(wd, dwd_ref, out_sem)
            out.start()
            out.wait()

    (dab, dwd), got = _call(
        hidden, name=name, grid=(n_steps,),
        in_specs=[tile(2 * F), tile(D), ANY], out_specs=[tile(2 * F), ANY],
        out_shape=[jax.ShapeDtypeStruct((t, 2 * F), BF16), jax.ShapeDtypeStruct((F, D), BF16)],
        scratch_shapes=[pltpu.VMEM((F, D), BF16), pltpu.VMEM((TM, F), BF16), pltpu.VMEM((F, D), F32), pltpu.SemaphoreType.DMA((N_DEV,)),
                        pltpu.SemaphoreType.DMA],
        args=(ab, dout, gw_d), comms=comms)
    return (dab, dwd.reshape(N_DEV, R_FF, D)), got


def _ffn_bwd_input(x, gain, dab, dout, gw_gu, name, comms=()):
    t = x.shape[0]

    def body(x_ref, g_ref, dab_ref, do_ref, gu_ref, dx_ref, dg_ref, wgu, sems):
        @pl.when(pl.program_id(0) == 0)
        def _():
            cps = _load_rows(gu_ref, wgu.at[0:F], 0, R_FF, sems.at[0]) + _load_rows(gu_ref, wgu.at[F:2 * F], R_FF, R_FF, sems.at[1])
            dg_ref[...] = jnp.zeros_like(dg_ref)
            for cp in cps:
                cp.wait()

        gain_v = g_ref[...]
        xn, r = _rms(x_ref[...])
        dh = _nn(dab_ref[...], wgu[...])
        dxn, dg = _rms_bwd(dh, xn, r, gain_v)
        dg_ref[...] += dg
        dx_ref[...] = do_ref[...] + dxn

    def tile(w):
        return pl.BlockSpec((TM, w), lambda i: (i, 0))

    row = pl.BlockSpec((1, D), lambda i: (0, 0))
    return _call(
        body, name=name, grid=(t // TM,),
        in_specs=[tile(D), row, tile(2 * F), tile(D), ANY], out_specs=[tile(D), row],
        out_shape=[jax.ShapeDtypeStruct((t, D), F32), jax.ShapeDtypeStruct((1, D), F32)],
        scratch_shapes=[pltpu.VMEM((2 * F, D), BF16), pltpu.SemaphoreType.DMA((2, N_DEV))],
        args=(x, gain, dab, dout, gw_gu), comms=comms)


def _weight_grad(a, b, name, col_off=0, m=None, comms=None, mats=1):
    t = a.shape[0]
    m = a.shape[1] if m is None else m
    n = b.shape[1]
    tmm = 512 if m % 512 == 0 else 256
    first = col_off // tmm

    def body(a_ref, b_ref, o_ref):
        o_ref[...] = _tn(a_ref[...], b_ref[...]).astype(BF16)

    (out,), got = _call(
        body, name=name, grid=(m // tmm,),
        in_specs=[pl.BlockSpec((t, tmm), lambda i: (0, first + i)), pl.BlockSpec((t, n), lambda i: (0, 0))],
        out_specs=[pl.BlockSpec((tmm, n), lambda i: (i, 0))],
        out_shape=[jax.ShapeDtypeStruct((m, n), BF16)], scratch_shapes=[], args=(a, b), comms=comms or ())
    out = out.reshape(mats * N_DEV, m // (mats * N_DEV), n)
    return out if comms is None else (out, got)


def _mix_proj_fwd(x, gain, b_gate, gw, comms=()):
    t = x.shape[0]

    def body(x_ref, g_ref, bg_ref, gw_ref, q_ref, k_ref, v_ref, zs_ref, gt_ref, h_ref, win, sems):
        @pl.when(pl.program_id(0) == 0)
        def _():
            for cp in _load_rows(gw_ref, win, OFF_IN, R_IN, sems):
                cp.wait()

        xn, _ = _rms(x_ref[...])
        h = (xn * g_ref[...]).astype(BF16)
        h_ref[...] = h
        z = _nt(h, win[...])
        q_ref[...] = (z[:, 0:D_ATT] * 0.125).astype(BF16)
        k_ref[...] = z[:, D_ATT:2 * D_ATT].astype(BF16)
        v_ref[...] = z[:, 2 * D_ATT:3 * D_ATT].astype(BF16)
        zs_ref[...] = z[:, 3 * D_ATT:3 * D_ATT + 2 * D_SGU].astype(BF16)
        gt_ref[...] = jax.nn.sigmoid(z[:, 3 * D_ATT + 2 * D_SGU:] + bg_ref[...]).astype(BF16)

    def tile(w):
        return pl.BlockSpec((TM, w), lambda i: (i, 0))

    return _call(
        body, name="mix_proj_fwd", grid=(t // TM,),
        in_specs=[tile(D), pl.BlockSpec((1, D), lambda i: (0, 0)), pl.BlockSpec((1, 2 * D), lambda i: (0, 0)), ANY],
        out_specs=[tile(D_ATT), tile(D_ATT), tile(D_ATT), tile(2 * D_SGU), tile(2 * D), tile(D)],
        out_shape=[jax.ShapeDtypeStruct((t, D_ATT), BF16)] * 3 + [jax.ShapeDtypeStruct((t, 2 * D_SGU), BF16),
                                                                   jax.ShapeDtypeStruct((t, 2 * D), BF16),
                                                                   jax.ShapeDtypeStruct((t, D), BF16)],
        scratch_shapes=[pltpu.VMEM((D_IN, D), BF16), pltpu.SemaphoreType.DMA((N_DEV,))],
        args=(x, gain, b_gate, gw), comms=comms)


def _mix_proj_bwd(dz, x, gain, dres, gw, comms=()):
    t = x.shape[0]

    def body(dz_ref, x_ref, g_ref, dr_ref, gw_ref, dx_ref, dg_ref, win, sems):
        @pl.when(pl.program_id(0) == 0)
        def _():
            cps = _load_rows(gw_ref, win, OFF_IN, R_IN, sems)
            dg_ref[...] = jnp.zeros_like(dg_ref)
            for cp in cps:
                cp.wait()

        dh = _nn(dz_ref[...], win[...])
        xn, r = _rms(x_ref[...])
        dxn, dg = _rms_bwd(dh, xn, r, g_ref[...])
        dg_ref[...] += dg
        dx_ref[...] = dr_ref[...] + dxn

    def tile(w):
        return pl.BlockSpec((TM, w), lambda i: (i, 0))

    row = pl.BlockSpec((1, D), lambda i: (0, 0))
    return _call(
        body, name="mix_proj_bwd", grid=(t // TM,),
        in_specs=[tile(D_IN), tile(D), row, tile(D), ANY], out_specs=[tile(D), row],
        out_shape=[jax.ShapeDtypeStruct((t, D), F32), jax.ShapeDtypeStruct((1, D), F32)],
        scratch_shapes=[pltpu.VMEM((D_IN, D), BF16), pltpu.SemaphoreType.DMA((N_DEV,))],
        args=(dz, x, gain, dres, gw), comms=comms)


SKEW_W = KW + QB
N_CAP = 2 * QB - REL_CLIP + 1


def _band_bias(rel_bias):
    cap = rel_bias[:, 2 * REL_CLIP:]
    diag = jnp.concatenate([jnp.broadcast_to(cap, (HEADS, N_CAP)), rel_bias[:, 2 * REL_CLIP - 1::-1],
                            jnp.broadcast_to(cap, (HEADS, SKEW_W - N_CAP - 2 * REL_CLIP))], axis=1)

    def body(d_ref, o_ref):
        lag = lax.broadcasted_iota(jnp.int32, (QB, KW), 1) // CHUNK - lax.broadcasted_iota(jnp.int32, (QB, KW), 0) // CHUNK
        band = (lag >= 0) & (lag <= N_LEFT)
        for h in range(HEADS):
            rows = jnp.broadcast_to(d_ref[h:h + 1, :], (QB, SKEW_W))
            o_ref[h] = jnp.where(band, pltpu.roll(rows, 0, 1, stride=1, stride_axis=0)[:, :KW], NEG_INF)

    return pl.pallas_call(body, name="band_bias", out_shape=jax.ShapeDtypeStruct((HEADS, QB, KW), F32))(diag)


def _att_specs():
    qspec = pl.BlockSpec((QB, D_ATT), lambda g: (g, 0))
    kspecs = [pl.BlockSpec((QB, D_ATT), lambda g: (jnp.maximum(g - 2, 0), 0)),
              pl.BlockSpec((QB, D_ATT), lambda g: (jnp.maximum(g - 1, 0), 0)), qspec]
    bspec = pl.BlockSpec((HEADS, QB, KW), lambda g: (0, 0, 0))
    return qspec, kspecs, bspec


def _att_probs(qm, kp, bias, valid):
    s = jnp.where(valid, _nt(qm, kp) + bias, NEG_INF)
    e = jnp.exp(s - jnp.max(s, axis=-1, keepdims=True))
    return e / jnp.sum(e, axis=-1, keepdims=True)


def _att_valid():
    g = pl.program_id(0)
    blk = lax.broadcasted_iota(jnp.int32, (QB, KW), 1) // QB
    return (blk + g) >= 2


def _att_fwd(q, k, v, bias, comms=()):
    t = q.shape[0]

    def body(q_ref, k0, k1, k2, v0, v1, v2, b_ref, y_ref):
        valid = _att_valid()
        first = lax.broadcasted_iota(jnp.int32, (1, 128), 1) < 64
        for p in range(HEADS // 2):
            lanes = slice(p * 128, (p + 1) * 128)
            qp = q_ref[:, lanes]
            kp = jnp.concatenate([k0[:, lanes], k1[:, lanes], k2[:, lanes]], axis=0)
            vp = jnp.concatenate([v0[:, lanes], v1[:, lanes], v2[:, lanes]], axis=0)
            out = jnp.zeros((QB, 128), F32)
            for hh in range(2):
                mask = first if hh == 0 else jnp.logical_not(first)
                pr = _att_probs(jnp.where(mask, qp, 0), kp, b_ref[2 * p + hh], valid)
                out = out + _nn(pr.astype(BF16), jnp.where(mask, vp, 0))
            y_ref[:, lanes] = out.astype(BF16)

    qspec, kspecs, bspec = _att_specs()
    (out,), got = _call(
        body, name="att_fwd", grid=(t // QB,),
        in_specs=[qspec] + kspecs + kspecs + [bspec], out_specs=[qspec],
        out_shape=[jax.ShapeDtypeStruct((t, D_ATT), BF16)], scratch_shapes=[],
        args=(q, k, k, k, v, v, v, bias), comms=comms)
    return out, got


def _att_bwd(q, k, v, bias, dy, comms=()):
    t = q.shape[0]
    n_blocks = t // QB

    def body(q_ref, k0, k1, k2, v0, v1, v2, b_ref, dy_ref, dq_ref, dk_ref, dv_ref, db_ref, dk_acc, dv_acc):
        g = pl.program_id(0)

        @pl.when(g == 0)
        def _():
            db_ref[...] = jnp.zeros_like(db_ref)
            dk_acc[...] = jnp.zeros_like(dk_acc)
            dv_acc[...] = jnp.zeros_like(dv_acc)

        valid = _att_valid()
        first = lax.broadcasted_iota(jnp.int32, (1, 128), 1) < 64
        for p in range(HEADS // 2):
            lanes = slice(p * 128, (p + 1) * 128)
            qp = q_ref[:, lanes]
            dyp = dy_ref[:, lanes]
            kp = jnp.concatenate([k0[:, lanes], k1[:, lanes], k2[:, lanes]], axis=0)
            vp = jnp.concatenate([v0[:, lanes], v1[:, lanes], v2[:, lanes]], axis=0)
            dq = jnp.zeros((QB, 128), F32)
            dk = jnp.zeros((KW, 128), F32)
            dv = jnp.zeros((KW, 128), F32)
            for hh in range(2):
                mask = first if hh == 0 else jnp.logical_not(first)
                qm = jnp.where(mask, qp, 0)
                dym = jnp.where(mask, dyp, 0)
                pr = _att_probs(qm, kp, b_ref[2 * p + hh], valid)
                dp = _nt(dym, vp)
                ds = pr * (dp - jnp.sum(dp * pr, axis=-1, keepdims=True))
                db_ref[2 * p + hh] += ds
                dsb = ds.astype(BF16)
                dq = dq + _nn(dsb, jnp.where(mask, kp, 0))
                dk = dk + _tn(dsb, qm)
                dv = dv + _tn(pr.astype(BF16), dym)
            dq_ref[:, lanes] = (dq * 0.125).astype(BF16)
            for j in range(3):
                rows = pl.ds(pl.multiple_of(jnp.maximum(g - 2 + j, 0) * QB, QB), QB)
                dk_acc[rows, lanes] += dk[j * QB:(j + 1) * QB]
                dv_acc[rows, lanes] += dv[j * QB:(j + 1) * QB]

        @pl.when(g == n_blocks - 1)
        def _():
            dk_ref[...] = dk_acc[...].astype(BF16)
            dv_ref[...] = dv_acc[...].astype(BF16)

    qspec, kspecs, bspec = _att_specs()
    full = pl.BlockSpec((t, D_ATT), lambda g: (0, 0))
    return _call(
        body, name="att_bwd", grid=(n_blocks,),
        in_specs=[qspec] + kspecs + kspecs + [bspec, qspec], out_specs=[qspec, full, full, bspec],
        out_shape=[jax.ShapeDtypeStruct((t, D_ATT), BF16)] * 3 + [jax.ShapeDtypeStruct((HEADS, QB, KW), F32)],
        scratch_shapes=[pltpu.VMEM((t, D_ATT), F32)] * 2,
        args=(q, k, k, k, v, v, v, bias, dy), comms=comms)


def _rel_bias_grad(dbias):
    def body(db_ref, cs_ref, tot_ref):
        lane = lax.broadcasted_iota(jnp.int32, (1, SKEW_W), 1)
        capped = (lane < N_CAP) | (lane > KW)
        pad = jnp.zeros((8, QB), F32)
        for h in range(HEADS):
            z = jnp.concatenate([db_ref[h, 0:8, :], pad], axis=1)
            for a in range(1, QB // 8):
                z = z + pltpu.roll(jnp.concatenate([db_ref[h, 8 * a:8 * a + 8, :], pad], axis=1), SKEW_W - 8 * a, 1)
            cs = z[0:1, :]
            for b in range(1, 8):
                cs = cs + pltpu.roll(z[b:b + 1, :], SKEW_W - b, 1)
            cs_ref[h:h + 1, :] = cs
            tot_ref[h:h + 1, :] = jnp.broadcast_to(jnp.sum(jnp.where(capped, cs, 0.0), axis=1, keepdims=True), (1, 128))

    cs, tot = pl.pallas_call(
        body, name="rel_bias_grad",
        out_shape=[jax.ShapeDtypeStruct((HEADS, SKEW_W), F32), jax.ShapeDtypeStruct((HEADS, 128), F32)],
    )(dbias)
    return jnp.concatenate([cs[:, KW:N_CAP - 1:-1], tot[:, :1]], axis=1)


def _sgu_mask():
    pos = np.arange(SGU_BLOCK)
    return (pos[:, None] // CHUNK) >= (pos[None, :] // CHUNK)


def _group_stack(blk, first):
    return jnp.concatenate([jnp.where(first, blk, 0), jnp.where(first, 0, blk)], axis=0)


def _sgu_norm(zs_ref, lng, lnb):
    zs = zs_ref[...].astype(F32)
    ga, th = _gelu(zs)
    u, vs = ga[:, :D_SGU], ga[:, D_SGU:]
    mu = jnp.mean(vs, axis=-1, keepdims=True)
    cen = vs - mu
    rstd = lax.rsqrt(jnp.mean(cen * cen, axis=-1, keepdims=True) + EPS)
    xhat = cen * rstd
    return zs, th, u, xhat, rstd, xhat * lng + lnb


def _sgu_mix(vb, wm2_ref, bsx, s_ref):
    first = lax.broadcasted_iota(jnp.int32, (1, 128), 1) < 64
    for n in range(TM // SGU_BLOCK):
        for p in range(4):
            blk = vb[n * 128:(n + 1) * 128, p * 128:(p + 1) * 128]
            s_ref[n * 128:(n + 1) * 128, p * 128:(p + 1) * 128] = _nn(wm2_ref[p], _group_stack(blk, first)) + bsx[:, p * 128:(p + 1) * 128]


def _merge_fwd(x, zs, gt, y_att, lng, lnb, wm2, bsx, gw, comms=()):
    t = x.shape[0]

    def body(x_ref, zs_ref, gt_ref, ya_ref, lng_ref, lnb_ref, wm2_ref, bsx_ref, gw_ref, xo_ref, ys_ref, mg_ref,
             wbr, wo, s_scr, sems):
        @pl.when(pl.program_id(0) == 0)
        def _():
            for cp in _load_rows(gw_ref, wbr, OFF_BR, R_BR, sems.at[0]) + _load_rows(gw_ref, wo, OFF_WO, R_WO, sems.at[1]):
                cp.wait()

        _, _, u, _, _, vsn = _sgu_norm(zs_ref, lng_ref[...], lnb_ref[...])
        _sgu_mix(vsn.astype(BF16), wm2_ref, bsx_ref[...], s_scr)
        ys = (u * s_scr[...]).astype(BF16)
        ys_ref[...] = ys
        pa = _nt(ya_ref[...], wbr[:, :D_ATT])
        ps = _nt(ys, wbr[:, D_ATT:])
        mg = (gt_ref[:, :D].astype(F32) * pa + gt_ref[:, D:].astype(F32) * ps).astype(BF16)
        mg_ref[...] = mg
        xo_ref[...] = x_ref[...] + _nn(mg, wo[...])

    def tile(w):
        return pl.BlockSpec((TM, w), lambda i: (i, 0))

    def const(shape):
        return pl.BlockSpec(shape, lambda i: (0,) * len(shape))

    return _call(
        body, name="merge_fwd", grid=(t // TM,),
        in_specs=[tile(D), tile(2 * D_SGU), tile(2 * D), tile(D_ATT), const((1, D_SGU)), const((1, D_SGU)),
                  const((4, 128, 256)), const((128, D_SGU)), ANY],
        out_specs=[tile(D), tile(D_SGU), tile(D)],
        out_shape=[jax.ShapeDtypeStruct((t, D), F32), jax.ShapeDtypeStruct((t, D_SGU), BF16), jax.ShapeDtypeStruct((t, D), BF16)],
        scratch_shapes=[pltpu.VMEM((D, D), BF16), pltpu.VMEM((D, D), BF16), pltpu.VMEM((TM, D_SGU), F32),
                        pltpu.SemaphoreType.DMA((2, N_DEV))],
        args=(x, zs, gt, y_att, lng, lnb, wm2, bsx, gw), comms=comms)


def _merge_bwd(dx, gt, y_att, y_sgu, merged, gw):
    t = dx.shape[0]
    n_steps = t // TM

    def body(dx_ref, gt_ref, ya_ref, ys_ref, mg_ref, gw_ref, dzg_ref, dya_ref, dys_ref, dbg_ref, ga_ref, gs_ref, go_ref,
             wbr, wo, acc_a, acc_s, acc_o, sems):
        step = pl.program_id(0)

        @pl.when(step == 0)
        def _():
            cps = _load_rows(gw_ref, wbr, OFF_BR, R_BR, sems.at[0]) + _load_rows(gw_ref, wo, OFF_WO, R_WO, sems.at[1])
            dbg_ref[...] = jnp.zeros_like(dbg_ref)
            for acc in (acc_a, acc_s, acc_o):
                acc[...] = jnp.zeros_like(acc)
            for cp in cps:
                cp.wait()

        dxb = dx_ref[...].astype(BF16)
        acc_o[...] += _tn(mg_ref[...], dxb)
        dm = _nt(dxb, wo[...])
        for half, y_ref, w, acc in ((0, ya_ref, wbr.at[:, :D_ATT], acc_a), (1, ys_ref, wbr.at[:, D_ATT:], acc_s)):
            cols = slice(half * D, (half + 1) * D)
            gate = gt_ref[:, cols].astype(F32)
            branch = _nt(y_ref[...], w[...])
            dzg = dm * branch * gate * (1.0 - gate)
            dbg_ref[:, cols] += jnp.sum(dzg, axis=0, keepdims=True)
            dzg_ref[:, cols] = dzg.astype(BF16)
            dbr = (dm * gate).astype(BF16)
            acc[...] += _tn(dbr, y_ref[...])
            dy = _nn(dbr, w[...])
            if half == 0:
                dya_ref[...] = dy.astype(BF16)
            else:
                dys_ref[...] = dy

        @pl.when(step == n_steps - 1)
        def _():
            ga_ref[...] = acc_a[...].astype(BF16)
            gs_ref[...] = acc_s[...].astype(BF16)
            go_ref[...] = acc_o[...].astype(BF16)

    def tile(w):
        return pl.BlockSpec((TM, w), lambda i: (i, 0))

    def whole(r, c):
        return pl.BlockSpec((r, c), lambda i: (0, 0))

    dzg, dya, dys, dbg, g_ba, g_bs, g_wo = pl.pallas_call(
        body, name="merge_bwd", grid=(n_steps,),
        in_specs=[tile(D), tile(2 * D), tile(D_ATT), tile(D_SGU), tile(D), ANY],
        out_specs=[tile(2 * D), tile(D_ATT), tile(D_SGU), whole(1, 2 * D), whole(D, D_ATT), whole(D, D_SGU), whole(D, D)],
        out_shape=[jax.ShapeDtypeStruct((t, 2 * D), BF16), jax.ShapeDtypeStruct((t, D_ATT), BF16), jax.ShapeDtypeStruct((t, D_SGU), F32),
                   jax.ShapeDtypeStruct((1, 2 * D), F32), jax.ShapeDtypeStruct((D, D_ATT), BF16), jax.ShapeDtypeStruct((D, D_SGU), BF16),
                   jax.ShapeDtypeStruct((D, D), BF16)],
        scratch_shapes=[pltpu.VMEM((D, D), BF16), pltpu.VMEM((D, D), BF16), pltpu.VMEM((D, D_ATT), F32), pltpu.VMEM((D, D_SGU), F32),
                        pltpu.VMEM((D, D), F32), pltpu.SemaphoreType.DMA((2, N_DEV))],
        compiler_params=_cparams(("arbitrary",)),
    )(dx, gt, y_att, y_sgu, merged, gw)
    return dzg, dya, dys, dbg, g_ba.reshape(N_DEV, R_BR, D_ATT), g_bs.reshape(N_DEV, R_BR, D_SGU), g_wo.reshape(N_DEV, R_WO, D)


def _sgu_bwd(zs, dys, lng, lnb, wm2, wmt2, bsx, comms=()):
    t = zs.shape[0]
    n_steps = t // TM

    def body(zs_ref, dys_ref, lng_ref, lnb_ref, wm2_ref, wmt2_ref, bsx_ref, dzs_ref, dw_ref, dbs_ref, dlg_ref, dlb_ref,
             s_scr, dv_scr, ds_acc):
        i = pl.program_id(0)

        @pl.when(i == 0)
        def _():
            dw_ref[...] = jnp.zeros_like(dw_ref)
            dlg_ref[...] = jnp.zeros_like(dlg_ref)
            dlb_ref[...] = jnp.zeros_like(dlb_ref)
            ds_acc[...] = jnp.zeros_like(ds_acc)

        lng = lng_ref[...]
        zs, th, u, xhat, rstd, vsn = _sgu_norm(zs_ref, lng, lnb_ref[...])
        vb = vsn.astype(BF16)
        _sgu_mix(vb, wm2_ref, bsx_ref[...], s_scr)
        dys = dys_ref[...]
        du = dys * s_scr[...]
        ds = dys * u
        dsb = ds.astype(BF16)
        first = lax.broadcasted_iota(jnp.int32, (1, 128), 1) < 64
        acc = jnp.zeros((SGU_BLOCK, D_SGU), F32)
        for n in range(TM // SGU_BLOCK):
            rows = slice(n * 128, (n + 1) * 128)
            acc = acc + ds[rows]
            for p in range(4):
                lanes = slice(p * 128, (p + 1) * 128)
                stack = _group_stack(dsb[rows, lanes], first)
                dv_scr[rows, lanes] = _nn(wmt2_ref[p], stack)
                dw_ref[p] += _nt(stack, vb[rows, lanes])
        ds_acc[...] += acc
        dvsn = dv_scr[...]
        dlg_ref[...] += jnp.sum(dvsn * xhat, axis=0, keepdims=True)
        dlb_ref[...] += jnp.sum(dvsn, axis=0, keepdims=True)
        dxh = dvsn * lng
        dvs = rstd * (dxh - jnp.mean(dxh, axis=-1, keepdims=True) - xhat * jnp.mean(dxh * xhat, axis=-1, keepdims=True))
        dga = jnp.concatenate([du, dvs], axis=1)
        dzs_ref[...] = (dga * _gelu_grad(zs, th)).astype(BF16)

        @pl.when(i == n_steps - 1)
        def _():
            r = lax.broadcasted_iota(jnp.int32, (256, 128), 0) % SGU_BLOCK
            c = lax.broadcasted_iota(jnp.int32, (256, 128), 1)
            keep = (r // CHUNK) >= (c // CHUNK)
            for p in range(4):
                dw_ref[p] = jnp.where(keep, dw_ref[p], 0.0)
            total = ds_acc[...]
            grp = lax.broadcasted_iota(jnp.int32, (SGU_BLOCK, D_SGU), 1) // 64
            lane = lax.broadcasted_iota(jnp.int32, (SGU_BLOCK, 128), 1)
            out = jnp.zeros((SGU_BLOCK, 128), F32)
            for gi in range(8):
                out = jnp.where(lane == gi, jnp.sum(jnp.where(grp == gi, total, 0.0), axis=1, keepdims=True), out)
            dbs_ref[...] = out

    def tile(w):
        return pl.BlockSpec((TM, w), lambda i: (i, 0))

    def const(shape):
        return pl.BlockSpec(shape, lambda i: (0,) * len(shape))

    return _call(
        body, name="sgu_bwd", grid=(n_steps,),
        in_specs=[tile(2 * D_SGU), tile(D_SGU), const((1, D_SGU)), const((1, D_SGU)), const((4, 128, 256)), const((4, 128, 256)),
                  const((128, D_SGU))],
        out_specs=[tile(2 * D_SGU), const((4, 256, 128)), const((128, 128)), const((1, D_SGU)), const((1, D_SGU))],
        out_shape=[jax.ShapeDtypeStruct((t, 2 * D_SGU), BF16), jax.ShapeDtypeStruct((4, 256, 128), F32),
                   jax.ShapeDtypeStruct((128, 128), F32), jax.ShapeDtypeStruct((1, D_SGU), F32), jax.ShapeDtypeStruct((1, D_SGU), F32)],
        scratch_shapes=[pltpu.VMEM((TM, D_SGU), F32), pltpu.VMEM((TM, D_SGU), F32), pltpu.VMEM((SGU_BLOCK, D_SGU), F32)],
        args=(zs, dys, lng, lnb, wm2, wmt2, bsx), comms=comms)


def _adamw(g, w, m, v):
    m = ADAM_B1 * m + (1.0 - ADAM_B1) * g
    v = ADAM_B2 * v + (1.0 - ADAM_B2) * (g * g)
    m_hat = m / (1.0 - ADAM_B1 ** ADAM_STEP)
    v_hat = v / (1.0 - ADAM_B2 ** ADAM_STEP)
    return -ADAM_LR * (m_hat / (jnp.sqrt(v_hat) + ADAM_EPS) + ADAM_WD * w), m, v


def _adamw_matrix(parts, sums, w, m, v, transposed, name):
    _, r, c = parts.shape
    tc = 256

    def body(own_ref, p_ref, s_ref, w_ref, m_ref, v_ref, g_ref, d_ref, mo_ref, vo_ref):
        g = p_ref[0].astype(F32) + p_ref[1].astype(F32) + p_ref[2].astype(F32) + s_ref[...].astype(F32)
        g = g.T if transposed else g
        g_ref[...] = g
        d_ref[...], mo_ref[...], vo_ref[...] = _adamw(g, w_ref[...], m_ref[...], v_ref[...])

    own = pl.BlockSpec((None, tc, r), lambda i, o: (0, i, 0)) if transposed else pl.BlockSpec((None, r, tc), lambda i, o: (0, 0, i))
    return pl.pallas_call(
        body, name=name,
        grid_spec=pltpu.PrefetchScalarGridSpec(
            num_scalar_prefetch=1, grid=(c // tc,),
            in_specs=[pl.BlockSpec((3, r, tc), lambda i, o: (0, 0, i)), pl.BlockSpec((None, r, tc), lambda i, o: (o[0], 0, i)), own, own, own],
            out_specs=[own] * 4),
        out_shape=[jax.ShapeDtypeStruct(w.shape, F32)] * 4,
        compiler_params=_cparams(("arbitrary",)),
    )(_my_index(("x", 2), ("y", 1)), parts, sums, w, m, v)


_SMALL_2D = {"norm_ffn1": (1, D), "norm_mix": (1, D), "norm_ffn2": (1, D), "norm_final": (1, D), "b_gate": (1, 2 * D),
             "sgu_ln_g": (1, D_SGU), "sgu_ln_b": (1, D_SGU), "sgu_b_s": (8, SGU_BLOCK), "rel_bias": (HEADS, N_REL),
             "sgu_w_s": (8 * SGU_BLOCK, SGU_BLOCK)}


def _adamw_small(parts, loss_parts, p):
    names = list(parts)
    n = len(names)

    def body(*refs):
        got, loss_got, wmv, outs, loss_out = refs[:n], refs[n], refs[n + 1:4 * n + 1], refs[4 * n + 1:8 * n + 1], refs[8 * n + 1]
        for i, name in enumerate(names):
            g = got[i][0]
            for k in range(1, N_DEV):
                g = g + got[i][k]
            if name == "sgu_b_s":
                g = g.T[0:8, :]
            res = (g,) + _adamw(g, wmv[3 * i][...], wmv[3 * i + 1][...], wmv[3 * i + 2][...])
            for o_ref, val in zip(outs[4 * i:4 * i + 4], res):
                o_ref[...] = val
        total = loss_got[0]
        for k in range(1, N_DEV):
            total = total + loss_got[k]
        loss_out[...] = total

    wmv = [p[pre + name].reshape(_SMALL_2D[name]) for name in names for pre in ("", "m_", "v_")]
    res = pl.pallas_call(
        body, name="adamw_small",
        out_shape=[jax.ShapeDtypeStruct(_SMALL_2D[name], F32) for name in names for _ in range(4)] + [jax.ShapeDtypeStruct((1, 128), F32)],
        compiler_params=_cparams())(*[parts[name] for name in names], loss_parts, *wmv)
    return [{name: res[4 * i + j].reshape(p[name].shape) for i, name in enumerate(names)} for j in range(4)], res[-1]


def _pack_rows(groups, name):
    flat = [a for grp in groups for a, _ in grp]
    rows = [grp[0][0].shape[2] if grp[0][1] else grp[0][0].shape[1] for grp in groups]

    def body(*refs):
        o_ref, pos, off = refs[-1], 0, 0
        for grp, r in zip(groups, rows):
            vals = []
            for _, transposed in grp:
                val = refs[pos][0]
                vals.append(val.T if transposed else val)
                pos += 1
            o_ref[off:off + r, :] = (vals[0] if len(vals) == 1 else jnp.concatenate(vals, axis=1)).astype(BF16)
            off += r

    return pl.pallas_call(body, name=name, out_shape=jax.ShapeDtypeStruct((sum(rows), D), BF16), compiler_params=_cparams())(*flat)


def _step(x, target, p):
    n1, nm, n2 = p["norm_ffn1"], p["norm_mix"], p["norm_ffn2"]
    nf = p["norm_final"].reshape(1, D)
    lng, lnb = p["sgu_ln_g"], p["sgu_ln_b"]
    w_m = jnp.where(jnp.asarray(_sgu_mask())[None], p["sgu_w_s"][0], 0.0).astype(BF16)
    wm2 = jnp.concatenate([w_m[0::2], w_m[1::2]], axis=2)
    w_mt = w_m.transpose(0, 2, 1)
    wmt2 = jnp.concatenate([w_mt[0::2], w_mt[1::2]], axis=2)
    bsx = jnp.repeat(p["sgu_b_s"][0].T, 64, axis=1)
    bias = _band_bias(p["rel_bias"][0])

    def chip_sums(grads, name):
        gots = _comm_only([_SiblingSwap(grads)], "swap_" + name)
        return _pair_sums(grads, gots, "pair_sums_" + name)
    def as_rows(a):
        return jnp.swapaxes(a, 1, 2)

    def updates(parts, sums, names):
        res = {}
        for pt, sm, n in zip(parts, sums, names):
            if p[n].shape[1:] == pt.shape[1:]:
                res[n] = _adamw_matrix(pt, sm, p[n], p["m_" + n], p["v_" + n], False, "adamw_" + n)
            elif p[n].shape[2] > 128:
                res[n] = [as_rows(o) for o in _adamw_matrix(pt, sm, as_rows(p[n]), as_rows(p["m_" + n]), as_rows(p["v_" + n]), False,
                                                            "adamw_" + n)]
            else:
                res[n] = _adamw_matrix(pt, sm, p[n], p["m_" + n], p["v_" + n], True, "adamw_" + n)
        return res

    rows1 = _pack_rows([[(as_rows(p["ffn1_w_gate"]), False)], [(as_rows(p["ffn1_w_up"]), False)], [(p["ffn1_w_down"], False)]], "pack_ffn1")
    rows_m = _pack_rows([[(as_rows(p["w_in"]), False)], [(p["w_branch_att"], True), (p["w_branch_sgu"], True)], [(p["w_out"], False)]],
                        "pack_mixer")
    rows2d = _pack_rows([[(p["ffn2_w_down"], False)]], "pack_ffn2_down")
    rows2gu = _pack_rows([[(as_rows(p["ffn2_w_gate"]), False)], [(as_rows(p["ffn2_w_up"]), False)]], "pack_ffn2_gate_up")
    (gw1,) = _comm_only([_Gather(rows1)], "gather_ffn1")
    x1, ab1, h1, (gwm,) = _ffn_fwd(x, n1, gw1, gw1, 2 * R_FF, "ffn1_fwd", [_Gather(rows_m)])
    (q, k, v, zs, gt, h2), (gw2d,) = _mix_proj_fwd(x1, nm, p["b_gate"], gwm, [_Gather(rows2d)])
    y_att, (gw2gu,) = _att_fwd(q, k, v, bias, [_Gather(rows2gu)])
    (x2, y_sgu, merged), _ = _merge_fwd(x1, zs, gt, y_att, lng, lnb, wm2, bsx, gwm)
    dx3, ab2, hb, _, d_nf, loss = _ffn_fwd(x2, n2, gw2gu, gw2d, 0, "ffn2_fwd", head=(nf, target))

    (dab, g_down), _ = _ffn_bwd_hidden(ab2, dx3, gw2d, 0, "ffn2_bwd_hidden")
    (dx2, d_n2), _ = _ffn_bwd_input(x2, n2, dab, dx3, gw2gu, "ffn2_bwd")
    g_gu = _weight_grad(dab, hb, "ffn2_dw_gate_up", mats=2)
    dzg, dya, dys, d_bg, g_ba, g_bs, g_wo = _merge_bwd(dx2, gt, y_att, y_sgu, merged, gwm)
    g_late = [(g_gu, 0), (g_gu, 1), (g_down, 0), (g_ba, 0), (g_bs, 0), (g_wo, 0)]
    late = ("ffn2_w_gate", "ffn2_w_up", "ffn2_w_down", "w_branch_att", "w_branch_sgu", "w_out")
    (dzs, d_wm, d_bs, d_lng, d_lnb), gots_late = _sgu_bwd(zs, dys, lng, lnb, wm2, wmt2, bsx, [_SiblingSwap(g_late)])
    sums_late = _pair_sums(g_late, gots_late, "pair_sums_late")
    (dq, dk, dv, d_bias), parts_late = _att_bwd(q, k, v, bias, dya, [_ChipScatter(sums_late)])
    big = updates(parts_late, sums_late, late)
    d_rel = _rel_bias_grad(d_bias)
    dz = jnp.concatenate([dq, dk, dv, dzs, dzg], axis=1)
    sums_in = chip_sums([(_weight_grad(dz, h2, "dw_in"), 0)], "w_in")
    (dx1, d_nm), parts_in = _mix_proj_bwd(dz, x1, nm, dx2, gwm, [_ChipScatter(sums_in)])
    big.update(updates(parts_in, sums_in, ("w_in",)))
    small = {"norm_ffn2": d_n2, "norm_final": d_nf, "b_gate": d_bg, "sgu_ln_g": d_lng, "sgu_ln_b": d_lnb, "sgu_b_s": d_bs,
             "rel_bias": d_rel, "sgu_w_s": d_wm.reshape(_SMALL_2D["sgu_w_s"]), "norm_mix": d_nm}

    (dab, g_down), (*small_parts, loss_parts) = _ffn_bwd_hidden(ab1, dx1, gw1, 2 * R_FF, "ffn1_bwd_hidden",
                                                                [_AllToAll(list(small.values()) + [loss])])
    sums_d = chip_sums([(g_down, 0)], "ffn1_down")
    g_gu, (parts_d,) = _weight_grad(dab, h1, "ffn1_dw_gate_up", comms=[_ChipScatter(sums_d)], mats=2)
    sums_gu = chip_sums([(g_gu, 0), (g_gu, 1)], "ffn1_gate_up")
    (dx0, d_n1), parts_gu = _ffn_bwd_input(x, n1, dab, dx1, gw1, "ffn1_bwd", [_ChipScatter(sums_gu)])
    (n1_parts,) = _comm_only([_AllToAll([d_n1])], "gather_norm_ffn1")
    big.update(updates([parts_d] + parts_gu, sums_d + sums_gu, ("ffn1_w_down", "ffn1_w_gate", "ffn1_w_up")))
    out_s, loss_sum = _adamw_small(dict(zip(small, small_parts), norm_ffn1=n1_parts), loss_parts, p)
    return dx0, loss_sum[0, 0], [{**{n: four[i] for n, four in big.items()}, **s} for i, s in enumerate(out_s)]


_OUT_ORDER = ("norm_ffn1", "ffn1_w_gate", "ffn1_w_up", "ffn1_w_down", "norm_mix", "w_in", "b_gate", "rel_bias", "sgu_ln_g", "sgu_ln_b",
              "sgu_w_s", "sgu_b_s", "w_branch_att", "w_branch_sgu", "w_out", "norm_ffn2", "ffn2_w_gate", "ffn2_w_up", "ffn2_w_down",
              "norm_final")


def kernel(x, norm_ffn1, ffn1_w_gate, ffn1_w_up, ffn1_w_down, norm_mix, w_in, b_gate, rel_bias, sgu_ln_g, sgu_ln_b, sgu_w_s, sgu_b_s, w_branch_att, w_branch_sgu, w_out, norm_ffn2, ffn2_w_gate, ffn2_w_up, ffn2_w_down, norm_final, loss_target, m_norm_ffn1, m_ffn1_w_gate, m_ffn1_w_up, m_ffn1_w_down, m_norm_mix, m_w_in, m_b_gate, m_rel_bias, m_sgu_ln_g, m_sgu_ln_b, m_sgu_w_s, m_sgu_b_s, m_w_branch_att, m_w_branch_sgu, m_w_out, m_norm_ffn2, m_ffn2_w_gate, m_ffn2_w_up, m_ffn2_w_down, m_norm_final, v_norm_ffn1, v_ffn1_w_gate, v_ffn1_w_up, v_ffn1_w_down, v_norm_mix, v_w_in, v_b_gate, v_rel_bias, v_sgu_ln_g, v_sgu_ln_b, v_sgu_w_s, v_sgu_b_s, v_w_branch_att, v_w_branch_sgu, v_w_out, v_norm_ffn2, v_ffn2_w_gate, v_ffn2_w_up, v_ffn2_w_down, v_norm_final):
    args = dict(locals())
    dx, loss, outs = _step(x[0], loss_target[0], {pre + n: args[pre + n] for pre in ("", "m_", "v_") for n in _OUT_ORDER})
    return (loss, dx[None], *[o[n] for o in outs for n in _OUT_ORDER])
```

```python
import functools

import numpy as np
import jax
import jax.numpy as jnp
from jax import lax
from jax.experimental import pallas as pl
from jax.experimental.pallas import tpu as pltpu

F32 = jnp.float32
BF16 = jnp.bfloat16

N_DEV = 8
D = 1024
F = 2816
D_ATT = 512
D_SGU = 512
D_IN = 4608
HEADS = 8
CHUNK = 64
N_LEFT = 8
REL_CLIP = 256
N_REL = 2 * REL_CLIP + 1
SGU_BLOCK = 128
EPS = 1e-6
NEG_INF = -1e30
QB = 256
KW = 3 * QB

R_FF, R_IN, R_BR, R_WO = F // N_DEV, D_IN // N_DEV, D // N_DEV, D // N_DEV
OFF_IN, OFF_BR, OFF_WO = 0, R_IN, R_IN + R_BR

FC = 256
TM = 512
VMEM_LIMIT = 56 * 1024 * 1024

ADAM_LR, ADAM_B1, ADAM_B2, ADAM_EPS, ADAM_WD, ADAM_STEP = 0.001, 0.9, 0.999, 1e-08, 0.01, 10

MESH = pl.DeviceIdType.MESH
ANY = pl.BlockSpec(memory_space=pl.ANY)


def _nt(a, b):
    return lax.dot_general(a, b, (((1,), (1,)), ((), ())), preferred_element_type=F32)


def _tn(a, b):
    return lax.dot_general(a, b, (((0,), (0,)), ((), ())), preferred_element_type=F32)


def _nn(a, b):
    return jnp.dot(a, b, preferred_element_type=F32)


def _cparams(sem=None):
    return pltpu.CompilerParams(dimension_semantics=sem, vmem_limit_bytes=VMEM_LIMIT)


def _load_rows(gw_ref, dst, off, rows, sems):
    copies = [pltpu.make_async_copy(gw_ref.at[k, pl.ds(off, rows), :], dst.at[pl.ds(k * rows, rows), :], sems.at[k])
              for k in range(N_DEV)]
    for cp in copies:
        cp.start()
    return copies


def _rms(xv):
    r = lax.rsqrt(jnp.mean(xv * xv, axis=-1, keepdims=True) + EPS)
    return xv * r, r


def _rms_bwd(dh, xn, r, gain):
    dxn = dh * gain
    dx = r * (dxn - xn * jnp.mean(dxn * xn, axis=-1, keepdims=True))
    return dx, jnp.sum(dh * xn, axis=0, keepdims=True)


def _gelu(x):
    t = jnp.tanh(0.7978845608028654 * (x + 0.044715 * x * x * x))
    return 0.5 * x * (1.0 + t), t


def _gelu_grad(x, t):
    return 0.5 * (1.0 + t) + 0.5 * x * (1.0 - t * t) * 0.7978845608028654 * (1.0 + 3.0 * 0.044715 * x * x)


def _place():
    x, y, cc = lax.axis_index("x"), lax.axis_index("y"), lax.axis_index("c")
    return x, y, cc, [(1 - x, y), (x, 1 - y), (1 - x, 1 - y)]


class _Gather:
    def __init__(self, shard):
        self.inputs = [shard]
        self.out_shape = [jax.ShapeDtypeStruct((N_DEV,) + shard.shape, shard.dtype)]
        self.scratch = [pltpu.SemaphoreType.DMA((7,)), pltpu.SemaphoreType.DMA((7,)), pltpu.SemaphoreType.DMA]

    def _copies(self, ins, outs, scr):
        (x_ref,), (out_ref,), (send_sems, recv_sems, local_sem) = ins, outs, scr
        x, y, cc, chips = _place()

        def slab(px, py, pc):
            return out_ref.at[4 * px + 2 * py + pc]

        def copy(k, block, to, src=None):
            return pltpu.make_async_remote_copy(
                src_ref=slab(*block) if src is None else src, dst_ref=slab(*block),
                send_sem=send_sems.at[k], recv_sem=recv_sems.at[k], device_id=to, device_id_type=MESH)

        me, sibling = (x, y, cc), (x, y, 1 - cc)
        x_nbr, y_nbr, diagonal = chips
        mine = pltpu.make_async_copy(x_ref, slab(*me), local_sem)
        first = [copy(0, me, sibling, src=x_ref), copy(1, me, (*x_nbr, cc), src=x_ref), copy(2, me, (*y_nbr, cc), src=x_ref)]
        neighbours = [copy(1, (*x_nbr, cc), me), copy(2, (*y_nbr, cc), me)]
        second_hand = copy(3, (x ^ (1 - cc), y ^ cc, cc), (x ^ cc, y ^ (1 - cc), cc))
        from_diagonal = copy(3, (*diagonal, cc), me)
        passed = [copy(4 + j, (*chip, cc), sibling) for j, chip in enumerate(chips)]
        from_sibling = [copy(0, sibling, me)] + [copy(4 + j, (*chip, 1 - cc), me) for j, chip in enumerate(chips)]
        return mine, first, neighbours, second_hand, from_diagonal, passed, from_sibling

    def begin(self, *refs):
        mine, first = self._copies(*refs)[:2]
        mine.start()
        for cp in first:
            cp.start()

    def mid(self, *refs):
        _, _, neighbours, second_hand, _, passed, _ = self._copies(*refs)
        for cp in neighbours:
            cp.wait_recv()
        second_hand.start()
        passed[0].start()
        passed[1].start()

    def relay(self, *refs):
        _, _, _, _, from_diagonal, passed, _ = self._copies(*refs)
        from_diagonal.wait_recv()
        passed[2].start()

    def end(self, *refs):
        mine, first, _, second_hand, _, passed, from_sibling = self._copies(*refs)
        for cp in from_sibling:
            cp.wait_recv()
        for cp in first + [second_hand] + passed:
            cp.wait_send()
        mine.wait()


class _Direct:
    def begin(self, *refs):
        keep, give = self._copies(*refs)
        for cp in keep + give:
            cp.start()

    def mid(self, *refs):
        pass

    def relay(self, *refs):
        pass

    def end(self, *refs):
        keep, give = self._copies(*refs)
        for cp in give:
            cp.wait_recv()
        for cp in give:
            cp.wait_send()
        for cp in keep:
            cp.wait()


class _SiblingSwap(_Direct):
    def __init__(self, grads):
        n = len(grads)
        self.which = [w for _, w in grads]
        self.inputs = [g for g, _ in grads]
        self.out_shape = [jax.ShapeDtypeStruct((4,) + g.shape[1:], g.dtype) for g, _ in grads]
        self.scratch = [pltpu.SemaphoreType.DMA((n, 4)), pltpu.SemaphoreType.DMA((n, 4))]

    def _copies(self, ins, outs, scr):
        send_sems, recv_sems = scr
        x, y, cc, _ = _place()
        return [], [pltpu.make_async_remote_copy(src_ref=g_ref.at[N_DEV * w + 2 * j + 1 - cc], dst_ref=got_ref.at[j],
                                                 send_sem=send_sems.at[i, j], recv_sem=recv_sems.at[i, j], device_id=(x, y, 1 - cc),
                                                 device_id_type=MESH)
                    for i, (g_ref, got_ref, w) in enumerate(zip(ins, outs, self.which)) for j in range(4)]


class _ChipScatter(_Direct):
    def __init__(self, sums):
        n = len(sums)
        self.inputs = list(sums)
        self.out_shape = [jax.ShapeDtypeStruct((3,) + s.shape[1:], s.dtype) for s in sums]
        self.scratch = [pltpu.SemaphoreType.DMA((n, 3)), pltpu.SemaphoreType.DMA((n, 3))]

    def _copies(self, ins, outs, scr):
        send_sems, recv_sems = scr
        _, _, cc, chips = _place()
        return [], [pltpu.make_async_remote_copy(src_ref=s_ref.at[2 * px + py], dst_ref=got_ref.at[j], send_sem=send_sems.at[i, j],
                                                 recv_sem=recv_sems.at[i, j], device_id=(px, py, cc), device_id_type=MESH)
                    for i, (s_ref, got_ref) in enumerate(zip(ins, outs)) for j, (px, py) in enumerate(chips)]


class _AllToAll(_Direct):
    def __init__(self, blocks):
        n = len(blocks)
        self.inputs = list(blocks)
        self.out_shape = [jax.ShapeDtypeStruct((N_DEV,) + b.shape, b.dtype) for b in blocks]
        self.scratch = [pltpu.SemaphoreType.DMA((n, 7)), pltpu.SemaphoreType.DMA((n, 7)), pltpu.SemaphoreType.DMA((n,))]

    def _copies(self, ins, outs, scr):
        send_sems, recv_sems, local_sems = scr
        x, y, cc, _ = _place()
        me = 4 * x + 2 * y + cc
        keep = [pltpu.make_async_copy(b_ref, got_ref.at[me], local_sems.at[i]) for i, (b_ref, got_ref) in enumerate(zip(ins, outs))]
        give = [pltpu.make_async_remote_copy(src_ref=b_ref, dst_ref=got_ref.at[me], send_sem=send_sems.at[i, k - 1],
                                             recv_sem=recv_sems.at[i, k - 1],
                                             device_id=(x ^ ((k >> 2) & 1), y ^ ((k >> 1) & 1), cc ^ (k & 1)), device_id_type=MESH)
                for i, (b_ref, got_ref) in enumerate(zip(ins, outs)) for k in range(1, N_DEV)]
        return keep, give


def _split_refs(refs, counts):
    out, pos = [], 0
    for n in counts:
        out.append(list(refs[pos:pos + n]))
        pos += n
    return out


def _bind(comms, c_in, c_out, c_scr):
    ins = _split_refs(c_in, [len(c.inputs) for c in comms])
    outs = _split_refs(c_out, [len(c.out_shape) for c in comms])
    scr = _split_refs(c_scr, [len(c.scratch) for c in comms])
    return [(c, (i, o, s)) for c, i, o, s in zip(comms, ins, outs, scr)]


def _call(body, *, name, grid, in_specs, out_specs, out_shape, scratch_shapes, args, comms=()):
    c_in = [a for c in comms for a in c.inputs]
    c_out = [s for c in comms for s in c.out_shape]
    c_scr = [s for c in comms for s in c.scratch]
    counts = [len(in_specs), len(c_in), len(out_shape), len(c_out), len(scratch_shapes), len(c_scr)]

    def full(*refs):
        ins, cin, outs, cout, scr, cscr = _split_refs(refs, counts)
        bound = _bind(comms, cin, cout, cscr)
        if comms:
            def at(steps):
                return functools.reduce(jnp.logical_and, [pl.program_id(ax) == s for ax, s in enumerate(steps)])

            first, last = at([0] * len(grid)), at([n - 1 for n in grid])

            @pl.when(first)
            def _():
                for c, r in bound:
                    c.begin(*r)

            @pl.when(at([(grid[0] - 1) // 2] + [0] * (len(grid) - 1)))
            def _():
                for c, r in bound:
                    c.mid(*r)

            @pl.when(last)
            def _():
                for c, r in bound:
                    c.relay(*r)

        body(*ins, *outs, *scr)
        if comms:
            @pl.when(last)
            def _():
                for c, r in bound:
                    c.end(*r)

    res = pl.pallas_call(
        full, name=name, grid=grid,
        in_specs=list(in_specs) + [ANY] * len(c_in), out_specs=list(out_specs) + [ANY] * len(c_out),
        out_shape=list(out_shape) + c_out, scratch_shapes=list(scratch_shapes) + c_scr,
        compiler_params=_cparams(("arbitrary",) * len(grid)),
    )(*args, *c_in)
    return list(res[:len(out_shape)]), list(res[len(out_shape):])


def _comm_only(comms, name):
    c_in = [a for c in comms for a in c.inputs]
    c_out = [s for c in comms for s in c.out_shape]
    c_scr = [s for c in comms for s in c.scratch]

    def full(*refs):
        cin, cout, cscr = _split_refs(refs, [len(c_in), len(c_out), len(c_scr)])
        bound = _bind(comms, cin, cout, cscr)
        for phase in ("begin", "mid", "relay", "end"):
            for c, r in bound:
                getattr(c, phase)(*r)

    return list(pl.pallas_call(full, name=name, in_specs=[ANY] * len(c_in), out_specs=[ANY] * len(c_out), out_shape=c_out,
                               scratch_shapes=c_scr)(*c_in))


def _my_index(*axes_and_weights):
    return sum(w * lax.axis_index(a) for a, w in axes_and_weights).astype(jnp.int32).reshape(1)


def _pair_sums(grads, gots, name):
    n = len(grads)

    def body(c_ref, *refs):
        for a_ref, b_ref, o_ref in zip(refs[:n], refs[n:2 * n], refs[2 * n:]):
            o_ref[...] = (a_ref[...].astype(F32) + b_ref[...].astype(F32)).astype(BF16)

    def tile(g):
        return pl.BlockSpec((1,) + g.shape[1:], lambda j, c_ref: (j, 0, 0))

    def mine(g, w):
        return pl.BlockSpec((1, None) + g.shape[1:], lambda j, c_ref: (4 * w + j, c_ref[0], 0, 0))

    return list(pl.pallas_call(
        body, name=name,
        grid_spec=pltpu.PrefetchScalarGridSpec(num_scalar_prefetch=1, grid=(4,),
                                               in_specs=[mine(g, w) for g, w in grads] + [tile(g) for g in gots],
                                               out_specs=[tile(g) for g in gots]),
        out_shape=[jax.ShapeDtypeStruct(g.shape, BF16) for g in gots],
        compiler_params=_cparams(("arbitrary",)))(_my_index(("c", 1)), *[g.reshape((-1, 2) + g.shape[1:]) for g, _ in grads], *gots))


def _ffn_fwd(x, gain, gw_gu, gw_d, off_d, name, comms=(), head=None):
    t = x.shape[0]
    n_head = 0 if head is None else 2

    def body(x_ref, g_ref, gu_ref, d_ref, *refs):
        head_in, (o_ref, ab_ref, h_ref), head_out = refs[:n_head], refs[n_head:n_head + 3], refs[n_head + 3:2 * n_head + 3]
        wg, wu, wd, s_scr, sems = refs[2 * n_head + 3:]

        @pl.when(pl.program_id(0) == 0)
        def _():
            cps = _load_rows(gu_ref, wg, 0, R_FF, sems.at[0]) + _load_rows(gu_ref, wu, R_FF, R_FF, sems.at[1]) \
                + _load_rows(d_ref, wd, off_d, R_FF, sems.at[2])
            for o in head_out:
                o[...] = jnp.zeros_like(o)
            for cp in cps:
                cp.wait()

        xv = x_ref[...]
        xn, _ = _rms(xv)
        h = (xn * g_ref[...]).astype(BF16)
        h_ref[...] = h
        for c in range(F // FC):
            rows = pl.ds(c * FC, FC)
            a = _nt(h, wg[rows, :])
            b = _nt(h, wu[rows, :])
            ab_ref[:, c * FC:(c + 1) * FC] = a.astype(BF16)
            ab_ref[:, F + c * FC:F + (c + 1) * FC] = b.astype(BF16)
            s_scr[:, c * FC:(c + 1) * FC] = (a * jax.nn.sigmoid(a) * b).astype(BF16)
        out = xv + 0.5 * _nn(s_scr[...], wd[...])
        if head is None:
            o_ref[...] = out
        else:
            (gf_ref, t_ref), (dg_ref, loss_ref) = head_in, head_out
            gain_f = gf_ref[...]
            yn, r = _rms(out)
            err = yn * gain_f - t_ref[...]
            loss_ref[...] += 0.5 * jnp.sum(jnp.mean(err * err, axis=-1, keepdims=True), axis=0, keepdims=True)
            o_ref[...], dg = _rms_bwd(err * (1.0 / D), yn, r, gain_f)
            dg_ref[...] += dg

    tile = pl.BlockSpec((TM, D), lambda i: (i, 0))
    row = pl.BlockSpec((1, D), lambda i: (0, 0))
    head_specs = [] if head is None else [row, pl.BlockSpec((1, 128), lambda i: (0, 0))]
    head_shapes = [] if head is None else [jax.ShapeDtypeStruct((1, D), F32), jax.ShapeDtypeStruct((1, 128), F32)]
    res, got = _call(
        body, name=name, grid=(t // TM,),
        in_specs=[tile, row, ANY, ANY] + ([] if head is None else [row, tile]),
        out_specs=[tile, pl.BlockSpec((TM, 2 * F), lambda i: (i, 0)), tile] + head_specs,
        out_shape=[jax.ShapeDtypeStruct((t, D), F32), jax.ShapeDtypeStruct((t, 2 * F), BF16), jax.ShapeDtypeStruct((t, D), BF16)] + head_shapes,
        scratch_shapes=[pltpu.VMEM((F, D), BF16)] * 3 + [pltpu.VMEM((TM, F), BF16), pltpu.SemaphoreType.DMA((3, N_DEV))],
        args=(x, gain, gw_gu, gw_d) + (() if head is None else tuple(head)), comms=comms)
    return (res[0], res[1], res[2], got, *res[3:])


def _ffn_bwd_hidden(ab, dout, gw_d, off_d, name, comms=()):
    t = ab.shape[0]
    n_steps = t // TM
    row_chunks = [(r, min(512, F - r)) for r in range(0, F, 512)]

    def tile(w):
        return pl.BlockSpec((TM, w), lambda i: (i, 0))

    def hidden(ab_ref, do_ref, d_ref, dab_ref, dwd_ref, wd, s_scr, acc, sems, out_sem):
        step = pl.program_id(0)

        @pl.when(step == 0)
        def _():
            cps = _load_rows(d_ref, wd, off_d, R_FF, sems)
            acc[...] = jnp.zeros_like(acc)
            for cp in cps:
                cp.wait()

        df = (0.5 * do_ref[...]).astype(BF16)
        for c in range(F // FC):
            a = ab_ref[:, c * FC:(c + 1) * FC].astype(F32)
            b = ab_ref[:, F + c * FC:F + (c + 1) * FC].astype(F32)
            sg = jax.nn.sigmoid(a)
            sl = a * sg
            ds = _nt(df, wd[pl.ds(c * FC, FC), :])
            dab_ref[:, c * FC:(c + 1) * FC] = (ds * b * (sg * (1.0 + a * (1.0 - sg)))).astype(BF16)
            dab_ref[:, F + c * FC:F + (c + 1) * FC] = (ds * sl).astype(BF16)
            s_scr[:, c * FC:(c + 1) * FC] = (sl * b).astype(BF16)
        for r, n in row_chunks:
            acc[r:r + n, :] += _tn(s_scr[:, r:r + n], df)

        @pl.when(step == n_steps - 1)
        def _():
            wd[...] = acc[...].astype(BF16)
            out = pltpu.make_async_copy(wd, dwd_ref, out_sem)
            out.start()
            out.wait()

    (dab, dwd), got = _call(
        hidden, name=name, grid=(n_steps,),
        in_specs=[tile(2 * F), tile(D), ANY], out_specs=[tile(2 * F), ANY],
        out_shape=[jax.ShapeDtypeStruct((t, 2 * F), BF16), jax.ShapeDtypeStruct((F, D), BF16)],
        scratch_shapes=[pltpu.VMEM((F, D), BF16), pltpu.VMEM((TM, F), BF16), pltpu.VMEM((F, D), F32), pltpu.SemaphoreType.DMA((N_DEV,)),
                        pltpu.SemaphoreType.DMA],
        args=(ab, dout, gw_d), comms=comms)
    return (dab, dwd.reshape(N_DEV, R_FF, D)), got


def _ffn_bwd_input(x, gain, dab, dout, gw_gu, name, comms=()):
    t = x.shape[0]

    def body(x_ref, g_ref, dab_ref, do_ref, gu_ref, dx_ref, dg_ref, wgu, sems):
        @pl.when(pl.program_id(0) == 0)
        def _():
            cps = _load_rows(gu_ref, wgu.at[0:F], 0, R_FF, sems.at[0]) + _load_rows(gu_ref, wgu.at[F:2 * F], R_FF, R_FF, sems.at[1])
            dg_ref[...] = jnp.zeros_like(dg_ref)
            for cp in cps:
                cp.wait()

        gain_v = g_ref[...]
        xn, r = _rms(x_ref[...])
        dh = _nn(dab_ref[...], wgu[...])
        dxn, dg = _rms_bwd(dh, xn, r, gain_v)
        dg_ref[...] += dg
        dx_ref[...] = do_ref[...] + dxn

    def tile(w):
        return pl.BlockSpec((TM, w), lambda i: (i, 0))

    row = pl.BlockSpec((1, D), lambda i: (0, 0))
    return _call(
        body, name=name, grid=(t // TM,),
        in_specs=[tile(D), row, tile(2 * F), tile(D), ANY], out_specs=[tile(D), row],
        out_shape=[jax.ShapeDtypeStruct((t, D), F32), jax.ShapeDtypeStruct((1, D), F32)],
        scratch_shapes=[pltpu.VMEM((2 * F, D), BF16), pltpu.SemaphoreType.DMA((2, N_DEV))],
        args=(x, gain, dab, dout, gw_gu), comms=comms)


def _weight_grad(a, b, name, col_off=0, m=None, comms=None, mats=1):
    t = a.shape[0]
    m = a.shape[1] if m is None else m
    n = b.shape[1]
    tmm = 512 if m % 512 == 0 else 256
    first = col_off // tmm

    def body(a_ref, b_ref, o_ref):
        o_ref[...] = _tn(a_ref[...], b_ref[...]).astype(BF16)

    (out,), got = _call(
        body, name=name, grid=(m // tmm,),
        in_specs=[pl.BlockSpec((t, tmm), lambda i: (0, first + i)), pl.BlockSpec((t, n), lambda i: (0, 0))],
        out_specs=[pl.BlockSpec((tmm, n), lambda i: (i, 0))],
        out_shape=[jax.ShapeDtypeStruct((m, n), BF16)], scratch_shapes=[], args=(a, b), comms=comms or ())
    out = out.reshape(mats * N_DEV, m // (mats * N_DEV), n)
    return out if comms is None else (out, got)


def _mix_proj_fwd(x, gain, b_gate, gw, comms=()):
    t = x.shape[0]

    def body(x_ref, g_ref, bg_ref, gw_ref, q_ref, k_ref, v_ref, zs_ref, gt_ref, h_ref, win, sems):
        @pl.when(pl.program_id(0) == 0)
        def _():
            for cp in _load_rows(gw_ref, win, OFF_IN, R_IN, sems):
                cp.wait()

        xn, _ = _rms(x_ref[...])
        h = (xn * g_ref[...]).astype(BF16)
        h_ref[...] = h
        z = _nt(h, win[...])
        q_ref[...] = (z[:, 0:D_ATT] * 0.125).astype(BF16)
        k_ref[...] = z[:, D_ATT:2 * D_ATT].astype(BF16)
        v_ref[...] = z[:, 2 * D_ATT:3 * D_ATT].astype(BF16)
        zs_ref[...] = z[:, 3 * D_ATT:3 * D_ATT + 2 * D_SGU].astype(BF16)
        gt_ref[...] = jax.nn.sigmoid(z[:, 3 * D_ATT + 2 * D_SGU:] + bg_ref[...]).astype(BF16)

    def tile(w):
        return pl.BlockSpec((TM, w), lambda i: (i, 0))

    return _call(
        body, name="mix_proj_fwd", grid=(t // TM,),
        in_specs=[tile(D), pl.BlockSpec((1, D), lambda i: (0, 0)), pl.BlockSpec((1, 2 * D), lambda i: (0, 0)), ANY],
        out_specs=[tile(D_ATT), tile(D_ATT), tile(D_ATT), tile(2 * D_SGU), tile(2 * D), tile(D)],
        out_shape=[jax.ShapeDtypeStruct((t, D_ATT), BF16)] * 3 + [jax.ShapeDtypeStruct((t, 2 * D_SGU), BF16),
                                                                   jax.ShapeDtypeStruct((t, 2 * D), BF16),
                                                                   jax.ShapeDtypeStruct((t, D), BF16)],
        scratch_shapes=[pltpu.VMEM((D_IN, D), BF16), pltpu.SemaphoreType.DMA((N_DEV,))],
        args=(x, gain, b_gate, gw), comms=comms)


def _mix_proj_bwd(dz, x, gain, dres, gw, comms=()):
    t = x.shape[0]

    def body(dz_ref, x_ref, g_ref, dr_ref, gw_ref, dx_ref, dg_ref, win, sems):
        @pl.when(pl.program_id(0) == 0)
        def _():
            cps = _load_rows(gw_ref, win, OFF_IN, R_IN, sems)
            dg_ref[...] = jnp.zeros_like(dg_ref)
            for cp in cps:
                cp.wait()

        dh = _nn(dz_ref[...], win[...])
        xn, r = _rms(x_ref[...])
        dxn, dg = _rms_bwd(dh, xn, r, g_ref[...])
        dg_ref[...] += dg
        dx_ref[...] = dr_ref[...] + dxn

    def tile(w):
        return pl.BlockSpec((TM, w), lambda i: (i, 0))

    row = pl.BlockSpec((1, D), lambda i: (0, 0))
    return _call(
        body, name="mix_proj_bwd", grid=(t // TM,),
        in_specs=[tile(D_IN), tile(D), row, tile(D), ANY], out_specs=[tile(D), row],
        out_shape=[jax.ShapeDtypeStruct((t, D), F32), jax.ShapeDtypeStruct((1, D), F32)],
        scratch_shapes=[pltpu.VMEM((D_IN, D), BF16), pltpu.SemaphoreType.DMA((N_DEV,))],
        args=(dz, x, gain, dres, gw), comms=comms)


SKEW_W = KW + QB
N_CAP = 2 * QB - REL_CLIP + 1


def _band_bias(rel_bias):
    cap = rel_bias[:, 2 * REL_CLIP:]
    diag = jnp.concatenate([jnp.broadcast_to(cap, (HEADS, N_CAP)), rel_bias[:, 2 * REL_CLIP - 1::-1],
                            jnp.broadcast_to(cap, (HEADS, SKEW_W - N_CAP - 2 * REL_CLIP))], axis=1)

    def body(d_ref, o_ref):
        lag = lax.broadcasted_iota(jnp.int32, (QB, KW), 1) // CHUNK - lax.broadcasted_iota(jnp.int32, (QB, KW), 0) // CHUNK
        band = (lag >= 0) & (lag <= N_LEFT)
        for h in range(HEADS):
            rows = jnp.broadcast_to(d_ref[h:h + 1, :], (QB, SKEW_W))
            o_ref[h] = jnp.where(band, pltpu.roll(rows, 0, 1, stride=1, stride_axis=0)[:, :KW], NEG_INF)

    return pl.pallas_call(body, name="band_bias", out_shape=jax.ShapeDtypeStruct((HEADS, QB, KW), F32))(diag)


def _att_specs():
    qspec = pl.BlockSpec((QB, D_ATT), lambda g: (g, 0))
    kspecs = [pl.BlockSpec((QB, D_ATT), lambda g: (jnp.maximum(g - 2, 0), 0)),
              pl.BlockSpec((QB, D_ATT), lambda g: (jnp.maximum(g - 1, 0), 0)), qspec]
    bspec = pl.BlockSpec((HEADS, QB, KW), lambda g: (0, 0, 0))
    return qspec, kspecs, bspec


def _att_scores(qm, kp, bias, valid):
    return jnp.where(valid, _nt(qm, kp) + bias, NEG_INF)


def _att_valid():
    g = pl.program_id(0)
    blk = lax.broadcasted_iota(jnp.int32, (QB, KW), 1) // QB
    return (blk + g) >= 2


def _att_fwd(q, k, v, bias, comms=()):
    t = q.shape[0]

    def body(q_ref, k0, k1, k2, v0, v1, v2, b_ref, y_ref, lse_ref):
        lse_ref[...] = jnp.zeros_like(lse_ref)
        valid = _att_valid()
        first = lax.broadcasted_iota(jnp.int32, (1, 128), 1) < 64
        for p in range(HEADS // 2):
            lanes = slice(p * 128, (p + 1) * 128)
            qp = q_ref[:, lanes]
            kp = jnp.concatenate([k0[:, lanes], k1[:, lanes], k2[:, lanes]], axis=0)
            vp = jnp.concatenate([v0[:, lanes], v1[:, lanes], v2[:, lanes]], axis=0)
            out = jnp.zeros((QB, 128), F32)
            for hh in range(2):
                mask = first if hh == 0 else jnp.logical_not(first)
                s = _att_scores(jnp.where(mask, qp, 0), kp, b_ref[2 * p + hh], valid)
                top = jnp.max(s, axis=-1, keepdims=True)
                e = jnp.exp(s - top)
                total = jnp.sum(e, axis=-1, keepdims=True)
                lse_ref[:, 2 * p + hh:2 * p + hh + 1] = top + jnp.log(total)
                out = out + _nn((e / total).astype(BF16), jnp.where(mask, vp, 0))
            y_ref[:, lanes] = out.astype(BF16)

    qspec, kspecs, bspec = _att_specs()
    (out, lse), got = _call(
        body, name="att_fwd", grid=(t // QB,),
        in_specs=[qspec] + kspecs + kspecs + [bspec], out_specs=[qspec, pl.BlockSpec((QB, 128), lambda g: (g, 0))],
        out_shape=[jax.ShapeDtypeStruct((t, D_ATT), BF16), jax.ShapeDtypeStruct((t, 128), F32)], scratch_shapes=[],
        args=(q, k, k, k, v, v, v, bias), comms=comms)
    return out, lse, got


def _att_bwd(q, k, v, bias, dy, lse, comms=()):
    t = q.shape[0]
    n_blocks = t // QB

    def body(q_ref, k0, k1, k2, v0, v1, v2, b_ref, dy_ref, lse_ref, dq_ref, dk_ref, dv_ref, db_ref, dk_acc, dv_acc):
        g = pl.program_id(0)

        @pl.when(g == 0)
        def _():
            db_ref[...] = jnp.zeros_like(db_ref)
            dk_acc[...] = jnp.zeros_like(dk_acc)
            dv_acc[...] = jnp.zeros_like(dv_acc)

        valid = _att_valid()
        first = lax.broadcasted_iota(jnp.int32, (1, 128), 1) < 64
        for p in range(HEADS // 2):
            lanes = slice(p * 128, (p + 1) * 128)
            qp = q_ref[:, lanes]
            dyp = dy_ref[:, lanes]
            kp = jnp.concatenate([k0[:, lanes], k1[:, lanes], k2[:, lanes]], axis=0)
            vp = jnp.concatenate([v0[:, lanes], v1[:, lanes], v2[:, lanes]], axis=0)
            dq = jnp.zeros((QB, 128), F32)
            dk = jnp.zeros((KW, 128), F32)
            dv = jnp.zeros((KW, 128), F32)
            for hh in range(2):
                mask = first if hh == 0 else jnp.logical_not(first)
                qm = jnp.where(mask, qp, 0)
                dym = jnp.where(mask, dyp, 0)
                pr = jnp.exp(_att_scores(qm, kp, b_ref[2 * p + hh], valid) - lse_ref[:, 2 * p + hh:2 * p + hh + 1])
                dp = _nt(dym, vp)
                ds = pr * (dp - jnp.sum(dp * pr, axis=-1, keepdims=True))
                db_ref[2 * p + hh] += ds
                dsb = ds.astype(BF16)
                dq = dq + _nn(dsb, jnp.where(mask, kp, 0))
                dk = dk + _tn(dsb, qm)
                dv = dv + _tn(pr.astype(BF16), dym)
            dq_ref[:, lanes] = (dq * 0.125).astype(BF16)
            for j in range(3):
                rows = pl.ds(pl.multiple_of(jnp.maximum(g - 2 + j, 0) * QB, QB), QB)
                dk_acc[rows, lanes] += dk[j * QB:(j + 1) * QB]
                dv_acc[rows, lanes] += dv[j * QB:(j + 1) * QB]

        @pl.when(g == n_blocks - 1)
        def _():
            dk_ref[...] = dk_acc[...].astype(BF16)
            dv_ref[...] = dv_acc[...].astype(BF16)

    qspec, kspecs, bspec = _att_specs()
    full = pl.BlockSpec((t, D_ATT), lambda g: (0, 0))
    return _call(
        body, name="att_bwd", grid=(n_blocks,),
        in_specs=[qspec] + kspecs + kspecs + [bspec, qspec, pl.BlockSpec((QB, 128), lambda g: (g, 0))],
        out_specs=[qspec, full, full, bspec],
        out_shape=[jax.ShapeDtypeStruct((t, D_ATT), BF16)] * 3 + [jax.ShapeDtypeStruct((HEADS, QB, KW), F32)],
        scratch_shapes=[pltpu.VMEM((t, D_ATT), F32)] * 2,
        args=(q, k, k, k, v, v, v, bias, dy, lse), comms=comms)


def _rel_bias_grad(dbias):
    def body(db_ref, cs_ref, tot_ref):
        lane = lax.broadcasted_iota(jnp.int32, (1, SKEW_W), 1)
        capped = (lane < N_CAP) | (lane > KW)
        pad = jnp.zeros((8, QB), F32)
        for h in range(HEADS):
            z = jnp.concatenate([db_ref[h, 0:8, :], pad], axis=1)
            for a in range(1, QB // 8):
                z = z + pltpu.roll(jnp.concatenate([db_ref[h, 8 * a:8 * a + 8, :], pad], axis=1), SKEW_W - 8 * a, 1)
            cs = z[0:1, :]
            for b in range(1, 8):
                cs = cs + pltpu.roll(z[b:b + 1, :], SKEW_W - b, 1)
            cs_ref[h:h + 1, :] = cs
            tot_ref[h:h + 1, :] = jnp.broadcast_to(jnp.sum(jnp.where(capped, cs, 0.0), axis=1, keepdims=True), (1, 128))

    cs, tot = pl.pallas_call(
        body, name="rel_bias_grad",
        out_shape=[jax.ShapeDtypeStruct((HEADS, SKEW_W), F32), jax.ShapeDtypeStruct((HEADS, 128), F32)],
    )(dbias)
    return jnp.concatenate([cs[:, KW:N_CAP - 1:-1], tot[:, :1]], axis=1)


def _sgu_mask():
    pos = np.arange(SGU_BLOCK)
    return (pos[:, None] // CHUNK) >= (pos[None, :] // CHUNK)


def _group_stack(blk, first):
    return jnp.concatenate([jnp.where(first, blk, 0), jnp.where(first, 0, blk)], axis=0)


def _sgu_norm(zs_ref, lng, lnb):
    zs = zs_ref[...].astype(F32)
    ga, th = _gelu(zs)
    u, vs = ga[:, :D_SGU], ga[:, D_SGU:]
    mu = jnp.mean(vs, axis=-1, keepdims=True)
    cen = vs - mu
    rstd = lax.rsqrt(jnp.mean(cen * cen, axis=-1, keepdims=True) + EPS)
    xhat = cen * rstd
    return zs, th, u, xhat, rstd, xhat * lng + lnb


def _sgu_mix(vb, wm2_ref, bsx, s_ref):
    first = lax.broadcasted_iota(jnp.int32, (1, 128), 1) < 64
    for n in range(TM // SGU_BLOCK):
        for p in range(4):
            blk = vb[n * 128:(n + 1) * 128, p * 128:(p + 1) * 128]
            s_ref[n * 128:(n + 1) * 128, p * 128:(p + 1) * 128] = _nn(wm2_ref[p], _group_stack(blk, first)) + bsx[:, p * 128:(p + 1) * 128]


def _merge_fwd(x, zs, gt, y_att, lng, lnb, wm2, bsx, gw, comms=()):
    t = x.shape[0]

    def body(x_ref, zs_ref, gt_ref, ya_ref, lng_ref, lnb_ref, wm2_ref, bsx_ref, gw_ref, xo_ref, ys_ref, mg_ref,
             wbr, wo, s_scr, sems):
        @pl.when(pl.program_id(0) == 0)
        def _():
            for cp in _load_rows(gw_ref, wbr, OFF_BR, R_BR, sems.at[0]) + _load_rows(gw_ref, wo, OFF_WO, R_WO, sems.at[1]):
                cp.wait()

        _, _, u, _, _, vsn = _sgu_norm(zs_ref, lng_ref[...], lnb_ref[...])
        _sgu_mix(vsn.astype(BF16), wm2_ref, bsx_ref[...], s_scr)
        ys = (u * s_scr[...]).astype(BF16)
        ys_ref[...] = ys
        pa = _nt(ya_ref[...], wbr[:, :D_ATT])
        ps = _nt(ys, wbr[:, D_ATT:])
        mg = (gt_ref[:, :D].astype(F32) * pa + gt_ref[:, D:].astype(F32) * ps).astype(BF16)
        mg_ref[...] = mg
        xo_ref[...] = x_ref[...] + _nn(mg, wo[...])

    def tile(w):
        return pl.BlockSpec((TM, w), lambda i: (i, 0))

    def const(shape):
        return pl.BlockSpec(shape, lambda i: (0,) * len(shape))

    return _call(
        body, name="merge_fwd", grid=(t // TM,),
        in_specs=[tile(D), tile(2 * D_SGU), tile(2 * D), tile(D_ATT), const((1, D_SGU)), const((1, D_SGU)),
                  const((4, 128, 256)), const((128, D_SGU)), ANY],
        out_specs=[tile(D), tile(D_SGU), tile(D)],
        out_shape=[jax.ShapeDtypeStruct((t, D), F32), jax.ShapeDtypeStruct((t, D_SGU), BF16), jax.ShapeDtypeStruct((t, D), BF16)],
        scratch_shapes=[pltpu.VMEM((D, D), BF16), pltpu.VMEM((D, D), BF16), pltpu.VMEM((TM, D_SGU), F32),
                        pltpu.SemaphoreType.DMA((2, N_DEV))],
        args=(x, zs, gt, y_att, lng, lnb, wm2, bsx, gw), comms=comms)


def _merge_bwd(dx, gt, y_att, y_sgu, merged, gw):
    t = dx.shape[0]
    n_steps = t // TM

    def body(dx_ref, gt_ref, ya_ref, ys_ref, mg_ref, gw_ref, dzg_ref, dya_ref, dys_ref, dbg_ref, ga_ref, gs_ref, go_ref,
             wbr, wo, acc_a, acc_s, acc_o, sems):
        step = pl.program_id(0)

        @pl.when(step == 0)
        def _():
            cps = _load_rows(gw_ref, wbr, OFF_BR, R_BR, sems.at[0]) + _load_rows(gw_ref, wo, OFF_WO, R_WO, sems.at[1])
            dbg_ref[...] = jnp.zeros_like(dbg_ref)
            for acc in (acc_a, acc_s, acc_o):
                acc[...] = jnp.zeros_like(acc)
            for cp in cps:
                cp.wait()

        dxb = dx_ref[...].astype(BF16)
        acc_o[...] += _tn(mg_ref[...], dxb)
        dm = _nt(dxb, wo[...])
        for half, y_ref, w, acc in ((0, ya_ref, wbr.at[:, :D_ATT], acc_a), (1, ys_ref, wbr.at[:, D_ATT:], acc_s)):
            cols = slice(half * D, (half + 1) * D)
            gate = gt_ref[:, cols].astype(F32)
            branch = _nt(y_ref[...], w[...])
            dzg = dm * branch * gate * (1.0 - gate)
            dbg_ref[:, cols] += jnp.sum(dzg, axis=0, keepdims=True)
            dzg_ref[:, cols] = dzg.astype(BF16)
            dbr = (dm * gate).astype(BF16)
            acc[...] += _tn(dbr, y_ref[...])
            dy = _nn(dbr, w[...])
            if half == 0:
                dya_ref[...] = dy.astype(BF16)
            else:
                dys_ref[...] = dy

        @pl.when(step == n_steps - 1)
        def _():
            ga_ref[...] = acc_a[...].astype(BF16)
            gs_ref[...] = acc_s[...].astype(BF16)
            go_ref[...] = acc_o[...].astype(BF16)

    def tile(w):
        return pl.BlockSpec((TM, w), lambda i: (i, 0))

    def whole(r, c):
        return pl.BlockSpec((r, c), lambda i: (0, 0))

    dzg, dya, dys, dbg, g_ba, g_bs, g_wo = pl.pallas_call(
        body, name="merge_bwd", grid=(n_steps,),
        in_specs=[tile(D), tile(2 * D), tile(D_ATT), tile(D_SGU), tile(D), ANY],
        out_specs=[tile(2 * D), tile(D_ATT), tile(D_SGU), whole(1, 2 * D), whole(D, D_ATT), whole(D, D_SGU), whole(D, D)],
        out_shape=[jax.ShapeDtypeStruct((t, 2 * D), BF16), jax.ShapeDtypeStruct((t, D_ATT), BF16), jax.ShapeDtypeStruct((t, D_SGU), F32),
                   jax.ShapeDtypeStruct((1, 2 * D), F32), jax.ShapeDtypeStruct((D, D_ATT), BF16), jax.ShapeDtypeStruct((D, D_SGU), BF16),
                   jax.ShapeDtypeStruct((D, D), BF16)],
        scratch_shapes=[pltpu.VMEM((D, D), BF16), pltpu.VMEM((D, D), BF16), pltpu.VMEM((D, D_ATT), F32), pltpu.VMEM((D, D_SGU), F32),
                        pltpu.VMEM((D, D), F32), pltpu.SemaphoreType.DMA((2, N_DEV))],
        compiler_params=_cparams(("arbitrary",)),
    )(dx, gt, y_att, y_sgu, merged, gw)
    return dzg, dya, dys, dbg, g_ba.reshape(N_DEV, R_BR, D_ATT), g_bs.reshape(N_DEV, R_BR, D_SGU), g_wo.reshape(N_DEV, R_WO, D)


def _sgu_bwd(zs, dys, lng, lnb, wm2, wmt2, bsx, comms=()):
    t = zs.shape[0]
    n_steps = t // TM

    def body(zs_ref, dys_ref, lng_ref, lnb_ref, wm2_ref, wmt2_ref, bsx_ref, dzs_ref, dw_ref, dbs_ref, dlg_ref, dlb_ref,
             s_scr, dv_scr, ds_acc):
        i = pl.program_id(0)

        @pl.when(i == 0)
        def _():
            dw_ref[...] = jnp.zeros_like(dw_ref)
            dlg_ref[...] = jnp.zeros_like(dlg_ref)
            dlb_ref[...] = jnp.zeros_like(dlb_ref)
            ds_acc[...] = jnp.zeros_like(ds_acc)

        lng = lng_ref[...]
        zs, th, u, xhat, rstd, vsn = _sgu_norm(zs_ref, lng, lnb_ref[...])
        vb = vsn.astype(BF16)
        _sgu_mix(vb, wm2_ref, bsx_ref[...], s_scr)
        dys = dys_ref[...]
        du = dys * s_scr[...]
        ds = dys * u
        dsb = ds.astype(BF16)
        first = lax.broadcasted_iota(jnp.int32, (1, 128), 1) < 64
        acc = jnp.zeros((SGU_BLOCK, D_SGU), F32)
        for n in range(TM // SGU_BLOCK):
            rows = slice(n * 128, (n + 1) * 128)
            acc = acc + ds[rows]
            for p in range(4):
                lanes = slice(p * 128, (p + 1) * 128)
                stack = _group_stack(dsb[rows, lanes], first)
                dv_scr[rows, lanes] = _nn(wmt2_ref[p], stack)
                dw_ref[p] += _nt(stack, vb[rows, lanes])
        ds_acc[...] += acc
        dvsn = dv_scr[...]
        dlg_ref[...] += jnp.sum(dvsn * xhat, axis=0, keepdims=True)
        dlb_ref[...] += jnp.sum(dvsn, axis=0, keepdims=True)
        dxh = dvsn * lng
        dvs = rstd * (dxh - jnp.mean(dxh, axis=-1, keepdims=True) - xhat * jnp.mean(dxh * xhat, axis=-1, keepdims=True))
        dga = jnp.concatenate([du, dvs], axis=1)
        dzs_ref[...] = (dga * _gelu_grad(zs, th)).astype(BF16)

        @pl.when(i == n_steps - 1)
        def _():
            r = lax.broadcasted_iota(jnp.int32, (256, 128), 0) % SGU_BLOCK
            c = lax.broadcasted_iota(jnp.int32, (256, 128), 1)
            keep = (r // CHUNK) >= (c // CHUNK)
            for p in range(4):
                dw_ref[p] = jnp.where(keep, dw_ref[p], 0.0)
            total = ds_acc[...]
            grp = lax.broadcasted_iota(jnp.int32, (SGU_BLOCK, D_SGU), 1) // 64
            lane = lax.broadcasted_iota(jnp.int32, (SGU_BLOCK, 128), 1)
            out = jnp.zeros((SGU_BLOCK, 128), F32)
            for gi in range(8):
                out = jnp.where(lane == gi, jnp.sum(jnp.where(grp == gi, total, 0.0), axis=1, keepdims=True), out)
            dbs_ref[...] = out

    def tile(w):
        return pl.BlockSpec((TM, w), lambda i: (i, 0))

    def const(shape):
        return pl.BlockSpec(shape, lambda i: (0,) * len(shape))

    return _call(
        body, name="sgu_bwd", grid=(n_steps,),
        in_specs=[tile(2 * D_SGU), tile(D_SGU), const((1, D_SGU)), const((1, D_SGU)), const((4, 128, 256)), const((4, 128, 256)),
                  const((128, D_SGU))],
        out_specs=[tile(2 * D_SGU), const((4, 256, 128)), const((128, 128)), const((1, D_SGU)), const((1, D_SGU))],
        out_shape=[jax.ShapeDtypeStruct((t, 2 * D_SGU), BF16), jax.ShapeDtypeStruct((4, 256, 128), F32),
                   jax.ShapeDtypeStruct((128, 128), F32), jax.ShapeDtypeStruct((1, D_SGU), F32), jax.ShapeDtypeStruct((1, D_SGU), F32)],
        scratch_shapes=[pltpu.VMEM((TM, D_SGU), F32), pltpu.VMEM((TM, D_SGU), F32), pltpu.VMEM((SGU_BLOCK, D_SGU), F32)],
        args=(zs, dys, lng, lnb, wm2, wmt2, bsx), comms=comms)


def _adamw(g, w, m, v):
    m = ADAM_B1 * m + (1.0 - ADAM_B1) * g
    v = ADAM_B2 * v + (1.0 - ADAM_B2) * (g * g)
    m_hat = m / (1.0 - ADAM_B1 ** ADAM_STEP)
    v_hat = v / (1.0 - ADAM_B2 ** ADAM_STEP)
    return -ADAM_LR * (m_hat / (jnp.sqrt(v_hat) + ADAM_EPS) + ADAM_WD * w), m, v


def _adamw_matrix(parts, sums, w, m, v, transposed, name):
    _, r, c = parts.shape
    tc = 256

    def body(own_ref, p_ref, s_ref, w_ref, m_ref, v_ref, g_ref, d_ref, mo_ref, vo_ref):
        g = p_ref[0].astype(F32) + p_ref[1].astype(F32) + p_ref[2].astype(F32) + s_ref[...].astype(F32)
        g = g.T if transposed else g
        g_ref[...] = g
        d_ref[...], mo_ref[...], vo_ref[...] = _adamw(g, w_ref[...], m_ref[...], v_ref[...])

    own = pl.BlockSpec((None, tc, r), lambda i, o: (0, i, 0)) if transposed else pl.BlockSpec((None, r, tc), lambda i, o: (0, 0, i))
    return pl.pallas_call(
        body, name=name,
        grid_spec=pltpu.PrefetchScalarGridSpec(
            num_scalar_prefetch=1, grid=(c // tc,),
            in_specs=[pl.BlockSpec((3, r, tc), lambda i, o: (0, 0, i)), pl.BlockSpec((None, r, tc), lambda i, o: (o[0], 0, i)), own, own, own],
            out_specs=[own] * 4),
        out_shape=[jax.ShapeDtypeStruct(w.shape, F32)] * 4,
        compiler_params=_cparams(("arbitrary",)),
    )(_my_index(("x", 2), ("y", 1)), parts, sums, w, m, v)


_SMALL_2D = {"norm_ffn1": (1, D), "norm_mix": (1, D), "norm_ffn2": (1, D), "norm_final": (1, D), "b_gate": (1, 2 * D),
             "sgu_ln_g": (1, D_SGU), "sgu_ln_b": (1, D_SGU), "sgu_b_s": (8, SGU_BLOCK), "rel_bias": (HEADS, N_REL),
             "sgu_w_s": (8 * SGU_BLOCK, SGU_BLOCK)}


def _adamw_small(parts, loss_parts, p):
    names = list(parts)
    n = len(names)

    def body(*refs):
        got, loss_got, wmv, outs, loss_out = refs[:n], refs[n], refs[n + 1:4 * n + 1], refs[4 * n + 1:8 * n + 1], refs[8 * n + 1]
        for i, name in enumerate(names):
            g = got[i][0]
            for k in range(1, N_DEV):
                g = g + got[i][k]
            if name == "sgu_b_s":
                g = g.T[0:8, :]
            res = (g,) + _adamw(g, wmv[3 * i][...], wmv[3 * i + 1][...], wmv[3 * i + 2][...])
            for o_ref, val in zip(outs[4 * i:4 * i + 4], res):
                o_ref[...] = val
        total = loss_got[0]
        for k in range(1, N_DEV):
            total = total + loss_got[k]
        loss_out[...] = total

    wmv = [p[pre + name].reshape(_SMALL_2D[name]) for name in names for pre in ("", "m_", "v_")]
    res = pl.pallas_call(
        body, name="adamw_small",
        out_shape=[jax.ShapeDtypeStruct(_SMALL_2D[name], F32) for name in names for _ in range(4)] + [jax.ShapeDtypeStruct((1, 128), F32)],
        compiler_params=_cparams())(*[parts[name] for name in names], loss_parts, *wmv)
    return [{name: res[4 * i + j].reshape(p[name].shape) for i, name in enumerate(names)} for j in range(4)], res[-1]


def _pack_rows(groups, name):
    flat = [a for grp in groups for a, _ in grp]
    rows = [grp[0][0].shape[2] if grp[0][1] else grp[0][0].shape[1] for grp in groups]

    def body(*refs):
        o_ref, pos, off = refs[-1], 0, 0
        for grp, r in zip(groups, rows):
            vals = []
            for _, transposed in grp:
                val = refs[pos][0]
                vals.append(val.T if transposed else val)
                pos += 1
            o_ref[off:off + r, :] = (vals[0] if len(vals) == 1 else jnp.concatenate(vals, axis=1)).astype(BF16)
            off += r

    return pl.pallas_call(body, name=name, out_shape=jax.ShapeDtypeStruct((sum(rows), D), BF16), compiler_params=_cparams())(*flat)


def _step(x, target, p):
    n1, nm, n2 = p["norm_ffn1"], p["norm_mix"], p["norm_ffn2"]
    nf = p["norm_final"].reshape(1, D)
    lng, lnb = p["sgu_ln_g"], p["sgu_ln_b"]
    w_m = jnp.where(jnp.asarray(_sgu_mask())[None], p["sgu_w_s"][0], 0.0).astype(BF16)
    wm2 = jnp.concatenate([w_m[0::2], w_m[1::2]], axis=2)
    w_mt = w_m.transpose(0, 2, 1)
    wmt2 = jnp.concatenate([w_mt[0::2], w_mt[1::2]], axis=2)
    bsx = jnp.repeat(p["sgu_b_s"][0].T, 64, axis=1)
    bias = _band_bias(p["rel_bias"][0])

    def chip_sums(grads, name):
        gots = _comm_only([_SiblingSwap(grads)], "swap_" + name)
        return _pair_sums(grads, gots, "pair_sums_" + name)
    def as_rows(a):
        return jnp.swapaxes(a, 1, 2)

    def updates(parts, sums, names):
        res = {}
        for pt, sm, n in zip(parts, sums, names):
            if p[n].shape[1:] == pt.shape[1:]:
                res[n] = _adamw_matrix(pt, sm, p[n], p["m_" + n], p["v_" + n], False, "adamw_" + n)
            elif p[n].shape[2] > 128:
                res[n] = [as_rows(o) for o in _adamw_matrix(pt, sm, as_rows(p[n]), as_rows(p["m_" + n]), as_rows(p["v_" + n]), False,
                                                            "adamw_" + n)]
            else:
                res[n] = _adamw_matrix(pt, sm, p[n], p["m_" + n], p["v_" + n], True, "adamw_" + n)
        return res

    rows1 = _pack_rows([[(as_rows(p["ffn1_w_gate"]), False)], [(as_rows(p["ffn1_w_up"]), False)], [(p["ffn1_w_down"], False)]], "pack_ffn1")
    rows_m = _pack_rows([[(as_rows(p["w_in"]), False)], [(p["w_branch_att"], True), (p["w_branch_sgu"], True)], [(p["w_out"], False)]],
                        "pack_mixer")
    rows2d = _pack_rows([[(p["ffn2_w_down"], False)]], "pack_ffn2_down")
    rows2gu = _pack_rows([[(as_rows(p["ffn2_w_gate"]), False)], [(as_rows(p["ffn2_w_up"]), False)]], "pack_ffn2_gate_up")
    (gw1,) = _comm_only([_Gather(rows1)], "gather_ffn1")
    x1, ab1, h1, (gwm,) = _ffn_fwd(x, n1, gw1, gw1, 2 * R_FF, "ffn1_fwd", [_Gather(rows_m)])
    (q, k, v, zs, gt, h2), (gw2d,) = _mix_proj_fwd(x1, nm, p["b_gate"], gwm, [_Gather(rows2d)])
    y_att, lse, (gw2gu,) = _att_fwd(q, k, v, bias, [_Gather(rows2gu)])
    (x2, y_sgu, merged), _ = _merge_fwd(x1, zs, gt, y_att, lng, lnb, wm2, bsx, gwm)
    dx3, ab2, hb, _, d_nf, loss = _ffn_fwd(x2, n2, gw2gu, gw2d, 0, "ffn2_fwd", head=(nf, target))

    (dab, g_down), _ = _ffn_bwd_hidden(ab2, dx3, gw2d, 0, "ffn2_bwd_hidden")
    (dx2, d_n2), _ = _ffn_bwd_input(x2, n2, dab, dx3, gw2gu, "ffn2_bwd")
    g_gu = _weight_grad(dab, hb, "ffn2_dw_gate_up", mats=2)
    dzg, dya, dys, d_bg, g_ba, g_bs, g_wo = _merge_bwd(dx2, gt, y_att, y_sgu, merged, gwm)
    g_late = [(g_gu, 0), (g_gu, 1), (g_down, 0), (g_ba, 0), (g_bs, 0), (g_wo, 0)]
    late = ("ffn2_w_gate", "ffn2_w_up", "ffn2_w_down", "w_branch_att", "w_branch_sgu", "w_out")
    (dzs, d_wm, d_bs, d_lng, d_lnb), gots_late = _sgu_bwd(zs, dys, lng, lnb, wm2, wmt2, bsx, [_SiblingSwap(g_late)])
    sums_late = _pair_sums(g_late, gots_late, "pair_sums_late")
    (dq, dk, dv, d_bias), parts_late = _att_bwd(q, k, v, bias, dya, lse, [_ChipScatter(sums_late)])
    big = updates(parts_late, sums_late, late)
    d_rel = _rel_bias_grad(d_bias)
    dz = jnp.concatenate([dq, dk, dv, dzs, dzg], axis=1)
    sums_in = chip_sums([(_weight_grad(dz, h2, "dw_in"), 0)], "w_in")
    (dx1, d_nm), parts_in = _mix_proj_bwd(dz, x1, nm, dx2, gwm, [_ChipScatter(sums_in)])
    big.update(updates(parts_in, sums_in, ("w_in",)))
    small = {"norm_ffn2": d_n2, "norm_final": d_nf, "b_gate": d_bg, "sgu_ln_g": d_lng, "sgu_ln_b": d_lnb, "sgu_b_s": d_bs,
             "rel_bias": d_rel, "sgu_w_s": d_wm.reshape(_SMALL_2D["sgu_w_s"]), "norm_mix": d_nm}

    (dab, g_down), (*small_parts, loss_parts) = _ffn_bwd_hidden(ab1, dx1, gw1, 2 * R_FF, "ffn1_bwd_hidden",
                                                                [_AllToAll(list(small.values()) + [loss])])
    sums_d = chip_sums([(g_down, 0)], "ffn1_down")
    g_gu, (parts_d,) = _weight_grad(dab, h1, "ffn1_dw_gate_up", comms=[_ChipScatter(sums_d)], mats=2)
    sums_gu = chip_sums([(g_gu, 0), (g_gu, 1)], "ffn1_gate_up")
    (dx0, d_n1), parts_gu = _ffn_bwd_input(x, n1, dab, dx1, gw1, "ffn1_bwd", [_ChipScatter(sums_gu)])
    (n1_parts,) = _comm_only([_AllToAll([d_n1])], "gather_norm_ffn1")
    big.update(updates([parts_d] + parts_gu, sums_d + sums_gu, ("ffn1_w_down", "ffn1_w_gate", "ffn1_w_up")))
    out_s, loss_sum = _adamw_small(dict(zip(small, small_parts), norm_ffn1=n1_parts), loss_parts, p)
    return dx0, loss_sum[0, 0], [{**{n: four[i] for n, four in big.items()}, **s} for i, s in enumerate(out_s)]


_OUT_ORDER = ("norm_ffn1", "ffn1_w_gate", "ffn1_w_up", "ffn1_w_down", "norm_mix", "w_in", "b_gate", "rel_bias", "sgu_ln_g", "sgu_ln_b",
              "sgu_w_s", "sgu_b_s", "w_branch_att", "w_branch_sgu", "w_out", "norm_ffn2", "ffn2_w_gate", "ffn2_w_up", "ffn2_w_down",
              "norm_final")


def kernel(x, norm_ffn1, ffn1_w_gate, ffn1_w_up, ffn1_w_down, norm_mix, w_in, b_gate, rel_bias, sgu_ln_g, sgu_ln_b, sgu_w_s, sgu_b_s, w_branch_att, w_branch_sgu, w_out, norm_ffn2, ffn2_w_gate, ffn2_w_up, ffn2_w_down, norm_final, loss_target, m_norm_ffn1, m_ffn1_w_gate, m_ffn1_w_up, m_ffn1_w_down, m_norm_mix, m_w_in, m_b_gate, m_rel_bias, m_sgu_ln_g, m_sgu_ln_b, m_sgu_w_s, m_sgu_b_s, m_w_branch_att, m_w_branch_sgu, m_w_out, m_norm_ffn2, m_ffn2_w_gate, m_ffn2_w_up, m_ffn2_w_down, m_norm_final, v_norm_ffn1, v_ffn1_w_gate, v_ffn1_w_up, v_ffn1_w_down, v_norm_mix, v_w_in, v_b_gate, v_rel_bias, v_sgu_ln_g, v_sgu_ln_b, v_sgu_w_s, v_sgu_b_s, v_w_branch_att, v_w_branch_sgu, v_w_out, v_norm_ffn2, v_ffn2_w_gate, v_ffn2_w_up, v_ffn2_w_down, v_norm_final):
    args = dict(locals())
    dx, loss, outs = _step(x[0], loss_target[0], {pre + n: args[pre + n] for pre in ("", "m_", "v_") for n in _OUT_ORDER})
    return (loss, dx[None], *[o[n] for o in outs for n in _OUT_ORDER])
```

```python
import functools

import numpy as np
import jax
import jax.numpy as jnp
from jax import lax
from jax.experimental import pallas as pl
from jax.experimental.pallas import tpu as pltpu

F32 = jnp.float32
BF16 = jnp.bfloat16

N_DEV = 8
D = 1024
F = 2816
D_ATT = 512
D_SGU = 512
D_IN = 4608
HEADS = 8
CHUNK = 64
N_LEFT = 8
REL_CLIP = 256
N_REL = 2 * REL_CLIP + 1
SGU_BLOCK = 128
EPS = 1e-6
NEG_INF = -1e30
QB = 256
KW = 3 * QB

R_FF, R_IN, R_BR, R_WO = F // N_DEV, D_IN // N_DEV, D // N_DEV, D // N_DEV
OFF_IN, OFF_BR, OFF_WO = 0, R_IN, R_IN + R_BR

FC = 256
TM = 512
VMEM_LIMIT = 56 * 1024 * 1024

ADAM_LR, ADAM_B1, ADAM_B2, ADAM_EPS, ADAM_WD, ADAM_STEP = 0.001, 0.9, 0.999, 1e-08, 0.01, 10

MESH = pl.DeviceIdType.MESH
ANY = pl.BlockSpec(memory_space=pl.ANY)


def _nt(a, b):
    return lax.dot_general(a, b, (((1,), (1,)), ((), ())), preferred_element_type=F32)


def _tn(a, b):
    return lax.dot_general(a, b, (((0,), (0,)), ((), ())), preferred_element_type=F32)


def _nn(a, b):
    return jnp.dot(a, b, preferred_element_type=F32)


def _cparams(sem=None):
    return pltpu.CompilerParams(dimension_semantics=sem, vmem_limit_bytes=VMEM_LIMIT)


def _load_rows(gw_ref, dst, off, rows, sems):
    copies = [pltpu.make_async_copy(gw_ref.at[k, pl.ds(off, rows), :], dst.at[pl.ds(k * rows, rows), :], sems.at[k])
              for k in range(N_DEV)]
    for cp in copies:
        cp.start()
    return copies


def _rms(xv):
    r = lax.rsqrt(jnp.mean(xv * xv, axis=-1, keepdims=True) + EPS)
    return xv * r, r


def _rms_bwd(dh, xn, r, gain):
    dxn = dh * gain
    dx = r * (dxn - xn * jnp.mean(dxn * xn, axis=-1, keepdims=True))
    return dx, jnp.sum(dh * xn, axis=0, keepdims=True)


def _gelu(x):
    t = jnp.tanh(0.7978845608028654 * (x + 0.044715 * x * x * x))
    return 0.5 * x * (1.0 + t), t


def _gelu_grad(x, t):
    return 0.5 * (1.0 + t) + 0.5 * x * (1.0 - t * t) * 0.7978845608028654 * (1.0 + 3.0 * 0.044715 * x * x)


def _place():
    x, y, cc = lax.axis_index("x"), lax.axis_index("y"), lax.axis_index("c")
    return x, y, cc, [(1 - x, y), (x, 1 - y), (1 - x, 1 - y)]


class _Gather:
    def __init__(self, shard):
        self.inputs = [shard]
        self.out_shape = [jax.ShapeDtypeStruct((N_DEV,) + shard.shape, shard.dtype)]
        self.scratch = [pltpu.SemaphoreType.DMA((7,)), pltpu.SemaphoreType.DMA((7,)), pltpu.SemaphoreType.DMA]

    def _copies(self, ins, outs, scr):
        (x_ref,), (out_ref,), (send_sems, recv_sems, local_sem) = ins, outs, scr
        x, y, cc, chips = _place()

        def slab(px, py, pc):
            return out_ref.at[4 * px + 2 * py + pc]

        def copy(k, block, to, src=None):
            return pltpu.make_async_remote_copy(
                src_ref=slab(*block) if src is None else src, dst_ref=slab(*block),
                send_sem=send_sems.at[k], recv_sem=recv_sems.at[k], device_id=to, device_id_type=MESH)

        me, sibling = (x, y, cc), (x, y, 1 - cc)
        x_nbr, y_nbr, diagonal = chips
        mine = pltpu.make_async_copy(x_ref, slab(*me), local_sem)
        first = [copy(0, me, sibling, src=x_ref), copy(1, me, (*x_nbr, cc), src=x_ref), copy(2, me, (*y_nbr, cc), src=x_ref)]
        neighbours = [copy(1, (*x_nbr, cc), me), copy(2, (*y_nbr, cc), me)]
        second_hand = copy(3, (x ^ (1 - cc), y ^ cc, cc), (x ^ cc, y ^ (1 - cc), cc))
        from_diagonal = copy(3, (*diagonal, cc), me)
        passed = [copy(4 + j, (*chip, cc), sibling) for j, chip in enumerate(chips)]
        from_sibling = [copy(0, sibling, me)] + [copy(4 + j, (*chip, 1 - cc), me) for j, chip in enumerate(chips)]
        return mine, first, neighbours, second_hand, from_diagonal, passed, from_sibling

    def begin(self, *refs):
        mine, first = self._copies(*refs)[:2]
        mine.start()
        for cp in first:
            cp.start()

    def mid(self, *refs):
        _, _, neighbours, second_hand, _, passed, _ = self._copies(*refs)
        for cp in neighbours:
            cp.wait_recv()
        second_hand.start()
        passed[0].start()
        passed[1].start()

    def relay(self, *refs):
        _, _, _, _, from_diagonal, passed, _ = self._copies(*refs)
        from_diagonal.wait_recv()
        passed[2].start()

    def end(self, *refs):
        mine, first, _, second_hand, _, passed, from_sibling = self._copies(*refs)
        for cp in from_sibling:
            cp.wait_recv()
        for cp in first + [second_hand] + passed:
            cp.wait_send()
        mine.wait()


class _Direct:
    def begin(self, *refs):
        keep, give = self._copies(*refs)
        for cp in keep + give:
            cp.start()

    def mid(self, *refs):
        pass

    def relay(self, *refs):
        pass

    def end(self, *refs):
        keep, give = self._copies(*refs)
        for cp in give:
            cp.wait_recv()
        for cp in give:
            cp.wait_send()
        for cp in keep:
            cp.wait()


class _SiblingSwap(_Direct):
    def __init__(self, grads):
        n = len(grads)
        self.which = [w for _, w in grads]
        self.inputs = [g for g, _ in grads]
        self.out_shape = [jax.ShapeDtypeStruct((4,) + g.shape[1:], g.dtype) for g, _ in grads]
        self.scratch = [pltpu.SemaphoreType.DMA((n, 4)), pltpu.SemaphoreType.DMA((n, 4))]

    def _copies(self, ins, outs, scr):
        send_sems, recv_sems = scr
        x, y, cc, _ = _place()
        return [], [pltpu.make_async_remote_copy(src_ref=g_ref.at[N_DEV * w + 2 * j + 1 - cc], dst_ref=got_ref.at[j],
                                                 send_sem=send_sems.at[i, j], recv_sem=recv_sems.at[i, j], device_id=(x, y, 1 - cc),
                                                 device_id_type=MESH)
                    for i, (g_ref, got_ref, w) in enumerate(zip(ins, outs, self.which)) for j in range(4)]


class _ChipScatter(_Direct):
    def __init__(self, sums):
        n = len(sums)
        self.inputs = list(sums)
        self.out_shape = [jax.ShapeDtypeStruct((3,) + s.shape[1:], s.dtype) for s in sums]
        self.scratch = [pltpu.SemaphoreType.DMA((n, 3)), pltpu.SemaphoreType.DMA((n, 3))]

    def _copies(self, ins, outs, scr):
        send_sems, recv_sems = scr
        _, _, cc, chips = _place()
        return [], [pltpu.make_async_remote_copy(src_ref=s_ref.at[2 * px + py], dst_ref=got_ref.at[j], send_sem=send_sems.at[i, j],
                                                 recv_sem=recv_sems.at[i, j], device_id=(px, py, cc), device_id_type=MESH)
                    for i, (s_ref, got_ref) in enumerate(zip(ins, outs)) for j, (px, py) in enumerate(chips)]


class _AllToAll(_Direct):
    def __init__(self, blocks):
        n = len(blocks)
        self.inputs = list(blocks)
        self.out_shape = [jax.ShapeDtypeStruct((N_DEV,) + b.shape, b.dtype) for b in blocks]
        self.scratch = [pltpu.SemaphoreType.DMA((n, 7)), pltpu.SemaphoreType.DMA((n, 7)), pltpu.SemaphoreType.DMA((n,))]

    def _copies(self, ins, outs, scr):
        send_sems, recv_sems, local_sems = scr
        x, y, cc, _ = _place()
        me = 4 * x + 2 * y + cc
        keep = [pltpu.make_async_copy(b_ref, got_ref.at[me], local_sems.at[i]) for i, (b_ref, got_ref) in enumerate(zip(ins, outs))]
        give = [pltpu.make_async_remote_copy(src_ref=b_ref, dst_ref=got_ref.at[me], send_sem=send_sems.at[i, k - 1],
                                             recv_sem=recv_sems.at[i, k - 1],
                                             device_id=(x ^ ((k >> 2) & 1), y ^ ((k >> 1) & 1), cc ^ (k & 1)), device_id_type=MESH)
                for i, (b_ref, got_ref) in enumerate(zip(ins, outs)) for k in range(1, N_DEV)]
        return keep, give


def _split_refs(refs, counts):
    out, pos = [], 0
    for n in counts:
        out.append(list(refs[pos:pos + n]))
        pos += n
    return out


def _bind(comms, c_in, c_out, c_scr):
    ins = _split_refs(c_in, [len(c.inputs) for c in comms])
    outs = _split_refs(c_out, [len(c.out_shape) for c in comms])
    scr = _split_refs(c_scr, [len(c.scratch) for c in comms])
    return [(c, (i, o, s)) for c, i, o, s in zip(comms, ins, outs, scr)]


def _call(body, *, name, grid, in_specs, out_specs, out_shape, scratch_shapes, args, comms=()):
    c_in = [a for c in comms for a in c.inputs]
    c_out = [s for c in comms for s in c.out_shape]
    c_scr = [s for c in comms for s in c.scratch]
    counts = [len(in_specs), len(c_in), len(out_shape), len(c_out), len(scratch_shapes), len(c_scr)]

    def full(*refs):
        ins, cin, outs, cout, scr, cscr = _split_refs(refs, counts)
        bound = _bind(comms, cin, cout, cscr)
        if comms:
            def at(steps):
                return functools.reduce(jnp.logical_and, [pl.program_id(ax) == s for ax, s in enumerate(steps)])

            first, last = at([0] * len(grid)), at([n - 1 for n in grid])

            @pl.when(first)
            def _():
                for c, r in bound:
                    c.begin(*r)

            @pl.when(at([(grid[0] - 1) // 2] + [0] * (len(grid) - 1)))
            def _():
                for c, r in bound:
                    c.mid(*r)

            @pl.when(last)
            def _():
                for c, r in bound:
                    c.relay(*r)

        body(*ins, *outs, *scr)
        if comms:
            @pl.when(last)
            def _():
                for c, r in bound:
                    c.end(*r)

    res = pl.pallas_call(
        full, name=name, grid=grid,
        in_specs=list(in_specs) + [ANY] * len(c_in), out_specs=list(out_specs) + [ANY] * len(c_out),
        out_shape=list(out_shape) + c_out, scratch_shapes=list(scratch_shapes) + c_scr,
        compiler_params=_cparams(("arbitrary",) * len(grid)),
    )(*args, *c_in)
    return list(res[:len(out_shape)]), list(res[len(out_shape):])


def _comm_only(comms, name):
    c_in = [a for c in comms for a in c.inputs]
    c_out = [s for c in comms for s in c.out_shape]
    c_scr = [s for c in comms for s in c.scratch]

    def full(*refs):
        cin, cout, cscr = _split_refs(refs, [len(c_in), len(c_out), len(c_scr)])
        bound = _bind(comms, cin, cout, cscr)
        for phase in ("begin", "mid", "relay", "end"):
            for c, r in bound:
                getattr(c, phase)(*r)

    return list(pl.pallas_call(full, name=name, in_specs=[ANY] * len(c_in), out_specs=[ANY] * len(c_out), out_shape=c_out,
                               scratch_shapes=c_scr)(*c_in))


def _my_index(*axes_and_weights):
    return sum(w * lax.axis_index(a) for a, w in axes_and_weights).astype(jnp.int32).reshape(1)


def _pair_sums(grads, gots, name):
    n = len(grads)

    def body(c_ref, *refs):
        for a_ref, b_ref, o_ref in zip(refs[:n], refs[n:2 * n], refs[2 * n:]):
            o_ref[...] = (a_ref[...].astype(F32) + b_ref[...].astype(F32)).astype(BF16)

    def tile(g):
        return pl.BlockSpec((1,) + g.shape[1:], lambda j, c_ref: (j, 0, 0))

    def mine(g, w):
        return pl.BlockSpec((1, None) + g.shape[1:], lambda j, c_ref: (4 * w + j, c_ref[0], 0, 0))

    return list(pl.pallas_call(
        body, name=name,
        grid_spec=pltpu.PrefetchScalarGridSpec(num_scalar_prefetch=1, grid=(4,),
                                               in_specs=[mine(g, w) for g, w in grads] + [tile(g) for g in gots],
                                               out_specs=[tile(g) for g in gots]),
        out_shape=[jax.ShapeDtypeStruct(g.shape, BF16) for g in gots],
        compiler_params=_cparams(("arbitrary",)))(_my_index(("c", 1)), *[g.reshape((-1, 2) + g.shape[1:]) for g, _ in grads], *gots))


def _ffn_fwd(x, gain, gw_gu, gw_d, off_d, name, comms=(), head=None):
    t = x.shape[0]
    n_head = 0 if head is None else 2

    def body(x_ref, g_ref, gu_ref, d_ref, *refs):
        head_in, (o_ref, ab_ref, h_ref), head_out = refs[:n_head], refs[n_head:n_head + 3], refs[n_head + 3:2 * n_head + 3]
        wg, wu, wd, s_scr, sems = refs[2 * n_head + 3:]

        @pl.when(pl.program_id(0) == 0)
        def _():
            cps = _load_rows(gu_ref, wg, 0, R_FF, sems.at[0]) + _load_rows(gu_ref, wu, R_FF, R_FF, sems.at[1]) \
                + _load_rows(d_ref, wd, off_d, R_FF, sems.at[2])
            for o in head_out:
                o[...] = jnp.zeros_like(o)
            for cp in cps:
                cp.wait()

        xv = x_ref[...]
        xn, _ = _rms(xv)
        h = (xn * g_ref[...]).astype(BF16)
        h_ref[...] = h
        for c in range(F // FC):
            rows = pl.ds(c * FC, FC)
            a = _nt(h, wg[rows, :])
            b = _nt(h, wu[rows, :])
            ab_ref[:, c * FC:(c + 1) * FC] = a.astype(BF16)
            ab_ref[:, F + c * FC:F + (c + 1) * FC] = b.astype(BF16)
            s_scr[:, c * FC:(c + 1) * FC] = (a * jax.nn.sigmoid(a) * b).astype(BF16)
        out = xv + 0.5 * _nn(s_scr[...], wd[...])
        if head is None:
            o_ref[...] = out
        else:
            (gf_ref, t_ref), (dg_ref, loss_ref) = head_in, head_out
            gain_f = gf_ref[...]
            yn, r = _rms(out)
            err = yn * gain_f - t_ref[...]
            loss_ref[...] += 0.5 * jnp.sum(jnp.mean(err * err, axis=-1, keepdims=True), axis=0, keepdims=True)
            o_ref[...], dg = _rms_bwd(err * (1.0 / D), yn, r, gain_f)
            dg_ref[...] += dg

    tile = pl.BlockSpec((TM, D), lambda i: (i, 0))
    row = pl.BlockSpec((1, D), lambda i: (0, 0))
    head_specs = [] if head is None else [row, pl.BlockSpec((1, 128), lambda i: (0, 0))]
    head_shapes = [] if head is None else [jax.ShapeDtypeStruct((1, D), F32), jax.ShapeDtypeStruct((1, 128), F32)]
    res, got = _call(
        body, name=name, grid=(t // TM,),
        in_specs=[tile, row, ANY, ANY] + ([] if head is None else [row, tile]),
        out_specs=[tile, pl.BlockSpec((TM, 2 * F), lambda i: (i, 0)), tile] + head_specs,
        out_shape=[jax.ShapeDtypeStruct((t, D), F32), jax.ShapeDtypeStruct((t, 2 * F), BF16), jax.ShapeDtypeStruct((t, D), BF16)] + head_shapes,
        scratch_shapes=[pltpu.VMEM((F, D), BF16)] * 3 + [pltpu.VMEM((TM, F), BF16), pltpu.SemaphoreType.DMA((3, N_DEV))],
        args=(x, gain, gw_gu, gw_d) + (() if head is None else tuple(head)), comms=comms)
    return (res[0], res[1], res[2], got, *res[3:])


def _ffn_bwd_hidden(ab, dout, gw_d, off_d, name, comms=()):
    t = ab.shape[0]
    n_steps = t // TM
    row_chunks = [(r, min(512, F - r)) for r in range(0, F, 512)]

    def tile(w):
        return pl.BlockSpec((TM, w), lambda i: (i, 0))

    def hidden(ab_ref, do_ref, d_ref, dab_ref, dwd_ref, wd, s_scr, acc, sems, out_sem):
        step = pl.program_id(0)

        @pl.when(step == 0)
        def _():
            cps = _load_rows(d_ref, wd, off_d, R_FF, sems)
            acc[...] = jnp.zeros_like(acc)
            for cp in cps:
                cp.wait()

        df = (0.5 * do_ref[...]).astype(BF16)
        for c in range(F // FC):
            a = ab_ref[:, c * FC:(c + 1) * FC].astype(F32)
            b = ab_ref[:, F + c * FC:F + (c + 1) * FC].astype(F32)
            sg = jax.nn.sigmoid(a)
            sl = a * sg
            ds = _nt(df, wd[pl.ds(c * FC, FC), :])
            dab_ref[:, c * FC:(c + 1) * FC] = (ds * b * (sg * (1.0 + a * (1.0 - sg)))).astype(BF16)
            dab_ref[:, F + c * FC:F + (c + 1) * FC] = (ds * sl).astype(BF16)
            s_scr[:, c * FC:(c + 1) * FC] = (sl * b).astype(BF16)
        for r, n in row_chunks:
            acc[r:r + n, :] += _tn(s_scr[:, r:r + n], df)

        @pl.when(step == n_steps - 1)
        def _():
            wd[...] = acc[...].astype(BF16)
            out = pltpu.make_async_copy(wd, dwd_ref, out_sem)
            out.start()
            out.wait()

    (dab, dwd), got = _call(
        hidden, name=name, grid=(n_steps,),
        in_specs=[tile(2 * F), tile(D), ANY], out_specs=[tile(2 * F), ANY],
        out_shape=[jax.ShapeDtypeStruct((t, 2 * F), BF16), jax.ShapeDtypeStruct((F, D), BF16)],
        scratch_shapes=[pltpu.VMEM((F, D), BF16), pltpu.VMEM((TM, F), BF16), pltpu.VMEM((F, D), F32), pltpu.SemaphoreType.DMA((N_DEV,)),
                        pltpu.SemaphoreType.DMA],
        args=(ab, dout, gw_d), comms=comms)
    return (dab, dwd.reshape(N_DEV, R_FF, D)), got


def _ffn_bwd_input(x, gain, dab, dout, gw_gu, name, comms=()):
    t = x.shape[0]

    def body(x_ref, g_ref, dab_ref, do_ref, gu_ref, dx_ref, dg_ref, wgu, sems):
        @pl.when(pl.program_id(0) == 0)
        def _():
            cps = _load_rows(gu_ref, wgu.at[0:F], 0, R_FF, sems.at[0]) + _load_rows(gu_ref, wgu.at[F:2 * F], R_FF, R_FF, sems.at[1])
            dg_ref[...] = jnp.zeros_like(dg_ref)
            for cp in cps:
                cp.wait()

        gain_v = g_ref[...]
        xn, r = _rms(x_ref[...])
        dh = _nn(dab_ref[...], wgu[...])
        dxn, dg = _rms_bwd(dh, xn, r, gain_v)
        dg_ref[...] += dg
        dx_ref[...] = do_ref[...] + dxn

    def tile(w):
        return pl.BlockSpec((TM, w), lambda i: (i, 0))

    row = pl.BlockSpec((1, D), lambda i: (0, 0))
    return _call(
        body, name=name, grid=(t // TM,),
        in_specs=[tile(D), row, tile(2 * F), tile(D), ANY], out_specs=[tile(D), row],
        out_shape=[jax.ShapeDtypeStruct((t, D), F32), jax.ShapeDtypeStruct((1, D), F32)],
        scratch_shapes=[pltpu.VMEM((2 * F, D), BF16), pltpu.SemaphoreType.DMA((2, N_DEV))],
        args=(x, gain, dab, dout, gw_gu), comms=comms)


def _weight_grad(a, b, name, col_off=0, m=None, comms=None, mats=1):
    t = a.shape[0]
    m = a.shape[1] if m is None else m
    n = b.shape[1]
    tmm = 512 if m % 512 == 0 else 256
    first = col_off // tmm

    def body(a_ref, b_ref, o_ref):
        o_ref[...] = _tn(a_ref[...], b_ref[...]).astype(BF16)

    (out,), got = _call(
        body, name=name, grid=(m // tmm,),
        in_specs=[pl.BlockSpec((t, tmm), lambda i: (0, first + i)), pl.BlockSpec((t, n), lambda i: (0, 0))],
        out_specs=[pl.BlockSpec((tmm, n), lambda i: (i, 0))],
        out_shape=[jax.ShapeDtypeStruct((m, n), BF16)], scratch_shapes=[], args=(a, b), comms=comms or ())
    out = out.reshape(mats * N_DEV, m // (mats * N_DEV), n)
    return out if comms is None else (out, got)


def _mix_proj_fwd(x, gain, b_gate, gw, comms=()):
    t = x.shape[0]

    def body(x_ref, g_ref, bg_ref, gw_ref, q_ref, k_ref, v_ref, zs_ref, gt_ref, h_ref, win, sems):
        @pl.when(pl.program_id(0) == 0)
        def _():
            for cp in _load_rows(gw_ref, win, OFF_IN, R_IN, sems):
                cp.wait()

        xn, _ = _rms(x_ref[...])
        h = (xn * g_ref[...]).astype(BF16)
        h_ref[...] = h
        z = _nt(h, win[...])
        q_ref[...] = (z[:, 0:D_ATT] * 0.125).astype(BF16)
        k_ref[...] = z[:, D_ATT:2 * D_ATT].astype(BF16)
        v_ref[...] = z[:, 2 * D_ATT:3 * D_ATT].astype(BF16)
        zs_ref[...] = z[:, 3 * D_ATT:3 * D_ATT + 2 * D_SGU].astype(BF16)
        gt_ref[...] = jax.nn.sigmoid(z[:, 3 * D_ATT + 2 * D_SGU:] + bg_ref[...]).astype(BF16)

    def tile(w):
        return pl.BlockSpec((TM, w), lambda i: (i, 0))

    return _call(
        body, name="mix_proj_fwd", grid=(t // TM,),
        in_specs=[tile(D), pl.BlockSpec((1, D), lambda i: (0, 0)), pl.BlockSpec((1, 2 * D), lambda i: (0, 0)), ANY],
        out_specs=[tile(D_ATT), tile(D_ATT), tile(D_ATT), tile(2 * D_SGU), tile(2 * D), tile(D)],
        out_shape=[jax.ShapeDtypeStruct((t, D_ATT), BF16)] * 3 + [jax.ShapeDtypeStruct((t, 2 * D_SGU), BF16),
                                                                   jax.ShapeDtypeStruct((t, 2 * D), BF16),
                                                                   jax.ShapeDtypeStruct((t, D), BF16)],
        scratch_shapes=[pltpu.VMEM((D_IN, D), BF16), pltpu.SemaphoreType.DMA((N_DEV,))],
        args=(x, gain, b_gate, gw), comms=comms)


def _mix_proj_bwd(dz, x, gain, dres, gw, comms=()):
    t = x.shape[0]

    def body(dz_ref, x_ref, g_ref, dr_ref, gw_ref, dx_ref, dg_ref, win, sems):
        @pl.when(pl.program_id(0) == 0)
        def _():
            cps = _load_rows(gw_ref, win, OFF_IN, R_IN, sems)
            dg_ref[...] = jnp.zeros_like(dg_ref)
            for cp in cps:
                cp.wait()

        dh = _nn(dz_ref[...], win[...])
        xn, r = _rms(x_ref[...])
        dxn, dg = _rms_bwd(dh, xn, r, g_ref[...])
        dg_ref[...] += dg
        dx_ref[...] = dr_ref[...] + dxn

    def tile(w):
        return pl.BlockSpec((TM, w), lambda i: (i, 0))

    row = pl.BlockSpec((1, D), lambda i: (0, 0))
    return _call(
        body, name="mix_proj_bwd", grid=(t // TM,),
        in_specs=[tile(D_IN), tile(D), row, tile(D), ANY], out_specs=[tile(D), row],
        out_shape=[jax.ShapeDtypeStruct((t, D), F32), jax.ShapeDtypeStruct((1, D), F32)],
        scratch_shapes=[pltpu.VMEM((D_IN, D), BF16), pltpu.SemaphoreType.DMA((N_DEV,))],
        args=(dz, x, gain, dres, gw), comms=comms)


SKEW_W = KW + QB
N_CAP = 2 * QB - REL_CLIP + 1


def _band_bias(rel_bias):
    cap = rel_bias[:, 2 * REL_CLIP:]
    diag = jnp.concatenate([jnp.broadcast_to(cap, (HEADS, N_CAP)), rel_bias[:, 2 * REL_CLIP - 1::-1],
                            jnp.broadcast_to(cap, (HEADS, SKEW_W - N_CAP - 2 * REL_CLIP))], axis=1)

    def body(d_ref, o_ref):
        lag = lax.broadcasted_iota(jnp.int32, (QB, KW), 1) // CHUNK - lax.broadcasted_iota(jnp.int32, (QB, KW), 0) // CHUNK
        band = (lag >= 0) & (lag <= N_LEFT)
        for h in range(HEADS):
            rows = jnp.broadcast_to(d_ref[h:h + 1, :], (QB, SKEW_W))
            o_ref[h] = jnp.where(band, pltpu.roll(rows, 0, 1, stride=1, stride_axis=0)[:, :KW], NEG_INF)

    return pl.pallas_call(body, name="band_bias", out_shape=jax.ShapeDtypeStruct((HEADS, QB, KW), F32))(diag)


def _att_specs():
    qspec = pl.BlockSpec((QB, D_ATT), lambda g: (g, 0))
    kspecs = [pl.BlockSpec((QB, D_ATT), lambda g: (jnp.maximum(g - 2, 0), 0)),
              pl.BlockSpec((QB, D_ATT), lambda g: (jnp.maximum(g - 1, 0), 0)), qspec]
    bspec = pl.BlockSpec((HEADS, QB, KW), lambda g: (0, 0, 0))
    return qspec, kspecs, bspec


def _att_probs(qm, kp, bias, valid):
    s = jnp.where(valid, _nt(qm, kp) + bias, NEG_INF)
    e = jnp.exp(s - jnp.max(s, axis=-1, keepdims=True))
    return e / jnp.sum(e, axis=-1, keepdims=True)


def _att_valid():
    g = pl.program_id(0)
    blk = lax.broadcasted_iota(jnp.int32, (QB, KW), 1) // QB
    return (blk + g) >= 2


def _att_fwd(q, k, v, bias, comms=()):
    t = q.shape[0]

    def body(q_ref, k0, k1, k2, v0, v1, v2, b_ref, y_ref):
        valid = _att_valid()
        first = lax.broadcasted_iota(jnp.int32, (1, 128), 1) < 64
        for p in range(HEADS // 2):
            lanes = slice(p * 128, (p + 1) * 128)
            qp = q_ref[:, lanes]
            kp = jnp.concatenate([k0[:, lanes], k1[:, lanes], k2[:, lanes]], axis=0)
            vp = jnp.concatenate([v0[:, lanes], v1[:, lanes], v2[:, lanes]], axis=0)
            out = jnp.zeros((QB, 128), F32)
            for hh in range(2):
                mask = first if hh == 0 else jnp.logical_not(first)
                pr = _att_probs(jnp.where(mask, qp, 0), kp, b_ref[2 * p + hh], valid)
                out = out + _nn(pr.astype(BF16), jnp.where(mask, vp, 0))
            y_ref[:, lanes] = out.astype(BF16)

    qspec, kspecs, bspec = _att_specs()
    (out,), got = _call(
        body, name="att_fwd", grid=(t // QB,),
        in_specs=[qspec] + kspecs + kspecs + [bspec], out_specs=[qspec],
        out_shape=[jax.ShapeDtypeStruct((t, D_ATT), BF16)], scratch_shapes=[],
        args=(q, k, k, k, v, v, v, bias), comms=comms)
    return out, got


def _att_bwd(q, k, v, bias, dy, comms=()):
    t = q.shape[0]
    n_blocks = t // QB

    def body(q_ref, k0, k1, k2, v0, v1, v2, b_ref, dy_ref, dq_ref, dk_ref, dv_ref, db_ref, dk_acc, dv_acc):
        g = pl.program_id(0)

        @pl.when(g == 0)
        def _():
            db_ref[...] = jnp.zeros_like(db_ref)
            dk_acc[...] = jnp.zeros_like(dk_acc)
            dv_acc[...] = jnp.zeros_like(dv_acc)

        valid = _att_valid()
        first = lax.broadcasted_iota(jnp.int32, (1, 128), 1) < 64
        for p in range(HEADS // 2):
            lanes = slice(p * 128, (p + 1) * 128)
            qp = q_ref[:, lanes]
            dyp = dy_ref[:, lanes]
            kp = jnp.concatenate([k0[:, lanes], k1[:, lanes], k2[:, lanes]], axis=0)
            vp = jnp.concatenate([v0[:, lanes], v1[:, lanes], v2[:, lanes]], axis=0)
            dq = jnp.zeros((QB, 128), F32)
            dk = jnp.zeros((KW, 128), F32)
            dv = jnp.zeros((KW, 128), F32)
            for hh in range(2):
                mask = first if hh == 0 else jnp.logical_not(first)
                qm = jnp.where(mask, qp, 0)
                dym = jnp.where(mask, dyp, 0)
                pr = _att_probs(qm, kp, b_ref[2 * p + hh], valid)
                dp = _nt(dym, vp)
                ds = pr * (dp - jnp.sum(dp * pr, axis=-1, keepdims=True))
                db_ref[2 * p + hh] += ds
                dsb = ds.astype(BF16)
                dq = dq + _nn(dsb, jnp.where(mask, kp, 0))
                dk = dk + _tn(dsb, qm)
                dv = dv + _tn(pr.astype(BF16), dym)
            dq_ref[:, lanes] = (dq * 0.125).astype(BF16)
            for j in range(3):
                rows = pl.ds(pl.multiple_of(jnp.maximum(g - 2 + j, 0) * QB, QB), QB)
                dk_acc[rows, lanes] += dk[j * QB:(j + 1) * QB]
                dv_acc[rows, lanes] += dv[j * QB:(j + 1) * QB]

        @pl.when(g == n_blocks - 1)
        def _():
            dk_ref[...] = dk_acc[...].astype(BF16)
            dv_ref[...] = dv_acc[...].astype(BF16)

    qspec, kspecs, bspec = _att_specs()
    full = pl.BlockSpec((t, D_ATT), lambda g: (0, 0))
    return _call(
        body, name="att_bwd", grid=(n_blocks,),
        in_specs=[qspec] + kspecs + kspecs + [bspec, qspec], out_specs=[qspec, full, full, bspec],
        out_shape=[jax.ShapeDtypeStruct((t, D_ATT), BF16)] * 3 + [jax.ShapeDtypeStruct((HEADS, QB, KW), F32)],
        scratch_shapes=[pltpu.VMEM((t, D_ATT), F32)] * 2,
        args=(q, k, k, k, v, v, v, bias, dy), comms=comms)


def _rel_bias_grad(dbias):
    def body(db_ref, cs_ref, tot_ref):
        lane = lax.broadcasted_iota(jnp.int32, (1, SKEW_W), 1)
        capped = (lane < N_CAP) | (lane > KW)
        pad = jnp.zeros((8, QB), F32)
        for h in range(HEADS):
            z = jnp.concatenate([db_ref[h, 0:8, :], pad], axis=1)
            for a in range(1, QB // 8):
                z = z + pltpu.roll(jnp.concatenate([db_ref[h, 8 * a:8 * a + 8, :], pad], axis=1), SKEW_W - 8 * a, 1)
            cs = z[0:1, :]
            for b in range(1, 8):
                cs = cs + pltpu.roll(z[b:b + 1, :], SKEW_W - b, 1)
            cs_ref[h:h + 1, :] = cs
            tot_ref[h:h + 1, :] = jnp.broadcast_to(jnp.sum(jnp.where(capped, cs, 0.0), axis=1, keepdims=True), (1, 128))

    cs, tot = pl.pallas_call(
        body, name="rel_bias_grad",
        out_shape=[jax.ShapeDtypeStruct((HEADS, SKEW_W), F32), jax.ShapeDtypeStruct((HEADS, 128), F32)],
    )(dbias)
    return jnp.concatenate([cs[:, KW:N_CAP - 1:-1], tot[:, :1]], axis=1)


def _sgu_mask():
    pos = np.arange(SGU_BLOCK)
    return (pos[:, None] // CHUNK) >= (pos[None, :] // CHUNK)


def _group_stack(blk, first):
    return jnp.concatenate([jnp.where(first, blk, 0), jnp.where(first, 0, blk)], axis=0)


def _sgu_norm(zs_ref, lng, lnb):
    zs = zs_ref[...].astype(F32)
    ga, th = _gelu(zs)
    u, vs = ga[:, :D_SGU], ga[:, D_SGU:]
    mu = jnp.mean(vs, axis=-1, keepdims=True)
    cen = vs - mu
    rstd = lax.rsqrt(jnp.mean(cen * cen, axis=-1, keepdims=True) + EPS)
    xhat = cen * rstd
    return zs, th, u, xhat, rstd, xhat * lng + lnb


def _sgu_mix(vb, wm2_ref, bsx, s_ref):
    first = lax.broadcasted_iota(jnp.int32, (1, 128), 1) < 64
    for n in range(TM // SGU_BLOCK):
        for p in range(4):
            blk = vb[n * 128:(n + 1) * 128, p * 128:(p + 1) * 128]
            s_ref[n * 128:(n + 1) * 128, p * 128:(p + 1) * 128] = _nn(wm2_ref[p], _group_stack(blk, first)) + bsx[:, p * 128:(p + 1) * 128]


def _merge_fwd(x, zs, gt, y_att, lng, lnb, wm2, bsx, gw):
    t = x.shape[0]

    def body(x_ref, zs_ref, gt_ref, ya_ref, lng_ref, lnb_ref, wm2_ref, bsx_ref, gw_ref, xo_ref, ys_ref, mg_ref,
             wbr, wo, s_scr, sems):
        @pl.when(pl.program_id(0) == 0)
        def _():
            for cp in _load_rows(gw_ref, wbr, OFF_BR, R_BR, sems.at[0]) + _load_rows(gw_ref, wo, OFF_WO, R_WO, sems.at[1]):
                cp.wait()

        _, _, u, _, _, vsn = _sgu_norm(zs_ref, lng_ref[...], lnb_ref[...])
        _sgu_mix(vsn.astype(BF16), wm2_ref, bsx_ref[...], s_scr)
        ys = (u * s_scr[...]).astype(BF16)
        ys_ref[...] = ys
        pa = _nt(ya_ref[...], wbr[:, :D_ATT])
        ps = _nt(ys, wbr[:, D_ATT:])
        mg = (gt_ref[:, :D].astype(F32) * pa + gt_ref[:, D:].astype(F32) * ps).astype(BF16)
        mg_ref[...] = mg
        xo_ref[...] = x_ref[...] + _nn(mg, wo[...])

    def tile(w):
        return pl.BlockSpec((TM, w), lambda i: (i, 0))

    def const(shape):
        return pl.BlockSpec(shape, lambda i: (0,) * len(shape))

    return pl.pallas_call(
        body, name="merge_fwd", grid=(t // TM,),
        in_specs=[tile(D), tile(2 * D_SGU), tile(2 * D), tile(D_ATT), const((1, D_SGU)), const((1, D_SGU)),
                  const((4, 128, 256)), const((128, D_SGU)), ANY],
        out_specs=[tile(D), tile(D_SGU), tile(D)],
        out_shape=[jax.ShapeDtypeStruct((t, D), F32), jax.ShapeDtypeStruct((t, D_SGU), BF16), jax.ShapeDtypeStruct((t, D), BF16)],
        scratch_shapes=[pltpu.VMEM((D, D), BF16), pltpu.VMEM((D, D), BF16), pltpu.VMEM((TM, D_SGU), F32),
                        pltpu.SemaphoreType.DMA((2, N_DEV))],
        compiler_params=_cparams(("arbitrary",)),
    )(x, zs, gt, y_att, lng, lnb, wm2, bsx, gw)


def _merge_bwd(dx, gt, y_att, y_sgu, merged, gw):
    t = dx.shape[0]
    n_steps = t // TM

    def body(dx_ref, gt_ref, ya_ref, ys_ref, mg_ref, gw_ref, dzg_ref, dya_ref, dys_ref, dbg_ref, ga_ref, gs_ref, go_ref,
             wbr, wo, acc_a, acc_s, acc_o, sems):
        step = pl.program_id(0)

        @pl.when(step == 0)
        def _():
            cps = _load_rows(gw_ref, wbr, OFF_BR, R_BR, sems.at[0]) + _load_rows(gw_ref, wo, OFF_WO, R_WO, sems.at[1])
            dbg_ref[...] = jnp.zeros_like(dbg_ref)
            for acc in (acc_a, acc_s, acc_o):
                acc[...] = jnp.zeros_like(acc)
            for cp in cps:
                cp.wait()

        dxb = dx_ref[...].astype(BF16)
        acc_o[...] += _tn(mg_ref[...], dxb)
        dm = _nt(dxb, wo[...])
        for half, y_ref, w, acc in ((0, ya_ref, wbr.at[:, :D_ATT], acc_a), (1, ys_ref, wbr.at[:, D_ATT:], acc_s)):
            cols = slice(half * D, (half + 1) * D)
            gate = gt_ref[:, cols].astype(F32)
            branch = _nt(y_ref[...], w[...])
            dzg = dm * branch * gate * (1.0 - gate)
            dbg_ref[:, cols] += jnp.sum(dzg, axis=0, keepdims=True)
            dzg_ref[:, cols] = dzg.astype(BF16)
            dbr = (dm * gate).astype(BF16)
            acc[...] += _tn(dbr, y_ref[...])
            dy = _nn(dbr, w[...])
            if half == 0:
                dya_ref[...] = dy.astype(BF16)
            else:
                dys_ref[...] = dy

        @pl.when(step == n_steps - 1)
        def _():
            ga_ref[...] = acc_a[...].astype(BF16)
            gs_ref[...] = acc_s[...].astype(BF16)
            go_ref[...] = acc_o[...].astype(BF16)

    def tile(w):
        return pl.BlockSpec((TM, w), lambda i: (i, 0))

    def whole(r, c):
        return pl.BlockSpec((r, c), lambda i: (0, 0))

    dzg, dya, dys, dbg, g_ba, g_bs, g_wo = pl.pallas_call(
        body, name="merge_bwd", grid=(n_steps,),
        in_specs=[tile(D), tile(2 * D), tile(D_ATT), tile(D_SGU), tile(D), ANY],
        out_specs=[tile(2 * D), tile(D_ATT), tile(D_SGU), whole(1, 2 * D), whole(D, D_ATT), whole(D, D_SGU), whole(D, D)],
        out_shape=[jax.ShapeDtypeStruct((t, 2 * D), BF16), jax.ShapeDtypeStruct((t, D_ATT), BF16), jax.ShapeDtypeStruct((t, D_SGU), F32),
                   jax.ShapeDtypeStruct((1, 2 * D), F32), jax.ShapeDtypeStruct((D, D_ATT), BF16), jax.ShapeDtypeStruct((D, D_SGU), BF16),
                   jax.ShapeDtypeStruct((D, D), BF16)],
        scratch_shapes=[pltpu.VMEM((D, D), BF16), pltpu.VMEM((D, D), BF16), pltpu.VMEM((D, D_ATT), F32), pltpu.VMEM((D, D_SGU), F32),
                        pltpu.VMEM((D, D), F32), pltpu.SemaphoreType.DMA((2, N_DEV))],
        compiler_params=_cparams(("arbitrary",)),
    )(dx, gt, y_att, y_sgu, merged, gw)
    return dzg, dya, dys, dbg, g_ba.reshape(N_DEV, R_BR, D_ATT), g_bs.reshape(N_DEV, R_BR, D_SGU), g_wo.reshape(N_DEV, R_WO, D)


def _sgu_bwd(zs, dys, lng, lnb, wm2, wmt2, bsx, comms=()):
    t = zs.shape[0]
    n_steps = t // TM

    def body(zs_ref, dys_ref, lng_ref, lnb_ref, wm2_ref, wmt2_ref, bsx_ref, dzs_ref, dw_ref, dbs_ref, dlg_ref, dlb_ref,
             s_scr, dv_scr, ds_acc):
        i = pl.program_id(0)

        @pl.when(i == 0)
        def _():
            dw_ref[...] = jnp.zeros_like(dw_ref)
            dlg_ref[...] = jnp.zeros_like(dlg_ref)
            dlb_ref[...] = jnp.zeros_like(dlb_ref)
            ds_acc[...] = jnp.zeros_like(ds_acc)

        lng = lng_ref[...]
        zs, th, u, xhat, rstd, vsn = _sgu_norm(zs_ref, lng, lnb_ref[...])
        vb = vsn.astype(BF16)
        _sgu_mix(vb, wm2_ref, bsx_ref[...], s_scr)
        dys = dys_ref[...]
        du = dys * s_scr[...]
        ds = dys * u
        dsb = ds.astype(BF16)
        first = lax.broadcasted_iota(jnp.int32, (1, 128), 1) < 64
        acc = jnp.zeros((SGU_BLOCK, D_SGU), F32)
        for n in range(TM // SGU_BLOCK):
            rows = slice(n * 128, (n + 1) * 128)
            acc = acc + ds[rows]
            for p in range(4):
                lanes = slice(p * 128, (p + 1) * 128)
                stack = _group_stack(dsb[rows, lanes], first)
                dv_scr[rows, lanes] = _nn(wmt2_ref[p], stack)
                dw_ref[p] += _nt(stack, vb[rows, lanes])
        ds_acc[...] += acc
        dvsn = dv_scr[...]
        dlg_ref[...] += jnp.sum(dvsn * xhat, axis=0, keepdims=True)
        dlb_ref[...] += jnp.sum(dvsn, axis=0, keepdims=True)
        dxh = dvsn * lng
        dvs = rstd * (dxh - jnp.mean(dxh, axis=-1, keepdims=True) - xhat * jnp.mean(dxh * xhat, axis=-1, keepdims=True))
        dga = jnp.concatenate([du, dvs], axis=1)
        dzs_ref[...] = (dga * _gelu_grad(zs, th)).astype(BF16)

        @pl.when(i == n_steps - 1)
        def _():
            r = lax.broadcasted_iota(jnp.int32, (256, 128), 0) % SGU_BLOCK
            c = lax.broadcasted_iota(jnp.int32, (256, 128), 1)
            keep = (r // CHUNK) >= (c // CHUNK)
            for p in range(4):
                dw_ref[p] = jnp.where(keep, dw_ref[p], 0.0)
            total = ds_acc[...]
            grp = lax.broadcasted_iota(jnp.int32, (SGU_BLOCK, D_SGU), 1) // 64
            lane = lax.broadcasted_iota(jnp.int32, (SGU_BLOCK, 128), 1)
            out = jnp.zeros((SGU_BLOCK, 128), F32)
            for gi in range(8):
                out = jnp.where(lane == gi, jnp.sum(jnp.where(grp == gi, total, 0.0), axis=1, keepdims=True), out)
            dbs_ref[...] = out

    def tile(w):
        return pl.BlockSpec((TM, w), lambda i: (i, 0))

    def const(shape):
        return pl.BlockSpec(shape, lambda i: (0,) * len(shape))

    return _call(
        body, name="sgu_bwd", grid=(n_steps,),
        in_specs=[tile(2 * D_SGU), tile(D_SGU), const((1, D_SGU)), const((1, D_SGU)), const((4, 128, 256)), const((4, 128, 256)),
                  const((128, D_SGU))],
        out_specs=[tile(2 * D_SGU), const((4, 256, 128)), const((128, 128)), const((1, D_SGU)), const((1, D_SGU))],
        out_shape=[jax.ShapeDtypeStruct((t, 2 * D_SGU), BF16), jax.ShapeDtypeStruct((4, 256, 128), F32),
                   jax.ShapeDtypeStruct((128, 128), F32), jax.ShapeDtypeStruct((1, D_SGU), F32), jax.ShapeDtypeStruct((1, D_SGU), F32)],
        scratch_shapes=[pltpu.VMEM((TM, D_SGU), F32), pltpu.VMEM((TM, D_SGU), F32), pltpu.VMEM((SGU_BLOCK, D_SGU), F32)],
        args=(zs, dys, lng, lnb, wm2, wmt2, bsx), comms=comms)


def _adamw(g, w, m, v):
    m = ADAM_B1 * m + (1.0 - ADAM_B1) * g
    v = ADAM_B2 * v + (1.0 - ADAM_B2) * (g * g)
    m_hat = m / (1.0 - ADAM_B1 ** ADAM_STEP)
    v_hat = v / (1.0 - ADAM_B2 ** ADAM_STEP)
    return -ADAM_LR * (m_hat / (jnp.sqrt(v_hat) + ADAM_EPS) + ADAM_WD * w), m, v


def _adamw_matrix(parts, sums, w, m, v, transposed, name):
    _, r, c = parts.shape
    tc = 256

    def body(own_ref, p_ref, s_ref, w_ref, m_ref, v_ref, g_ref, d_ref, mo_ref, vo_ref):
        g = p_ref[0].astype(F32) + p_ref[1].astype(F32) + p_ref[2].astype(F32) + s_ref[...].astype(F32)
        g = g.T if transposed else g
        g_ref[...] = g
        d_ref[...], mo_ref[...], vo_ref[...] = _adamw(g, w_ref[...], m_ref[...], v_ref[...])

    own = pl.BlockSpec((None, tc, r), lambda i, o: (0, i, 0)) if transposed else pl.BlockSpec((None, r, tc), lambda i, o: (0, 0, i))
    return pl.pallas_call(
        body, name=name,
        grid_spec=pltpu.PrefetchScalarGridSpec(
            num_scalar_prefetch=1, grid=(c // tc,),
            in_specs=[pl.BlockSpec((3, r, tc), lambda i, o: (0, 0, i)), pl.BlockSpec((None, r, tc), lambda i, o: (o[0], 0, i)), own, own, own],
            out_specs=[own] * 4),
        out_shape=[jax.ShapeDtypeStruct(w.shape, F32)] * 4,
        compiler_params=_cparams(("arbitrary",)),
    )(_my_index(("x", 2), ("y", 1)), parts, sums, w, m, v)


_SMALL_2D = {"norm_ffn1": (1, D), "norm_mix": (1, D), "norm_ffn2": (1, D), "norm_final": (1, D), "b_gate": (1, 2 * D),
             "sgu_ln_g": (1, D_SGU), "sgu_ln_b": (1, D_SGU), "sgu_b_s": (8, SGU_BLOCK), "rel_bias": (HEADS, N_REL),
             "sgu_w_s": (8 * SGU_BLOCK, SGU_BLOCK)}


def _adamw_small(parts, loss_parts, p):
    names = list(parts)
    n = len(names)

    def body(*refs):
        got, loss_got, wmv, outs, loss_out = refs[:n], refs[n], refs[n + 1:4 * n + 1], refs[4 * n + 1:8 * n + 1], refs[8 * n + 1]
        for i, name in enumerate(names):
            g = got[i][0]
            for k in range(1, N_DEV):
                g = g + got[i][k]
            if name == "sgu_b_s":
                g = g.T[0:8, :]
            res = (g,) + _adamw(g, wmv[3 * i][...], wmv[3 * i + 1][...], wmv[3 * i + 2][...])
            for o_ref, val in zip(outs[4 * i:4 * i + 4], res):
                o_ref[...] = val
        total = loss_got[0]
        for k in range(1, N_DEV):
            total = total + loss_got[k]
        loss_out[...] = total

    wmv = [p[pre + name].reshape(_SMALL_2D[name]) for name in names for pre in ("", "m_", "v_")]
    res = pl.pallas_call(
        body, name="adamw_small",
        out_shape=[jax.ShapeDtypeStruct(_SMALL_2D[name], F32) for name in names for _ in range(4)] + [jax.ShapeDtypeStruct((1, 128), F32)],
        compiler_params=_cparams())(*[parts[name] for name in names], loss_parts, *wmv)
    return [{name: res[4 * i + j].reshape(p[name].shape) for i, name in enumerate(names)} for j in range(4)], res[-1]


def _pack_rows(slabs, name, comms=()):
    flat = [a for groups in slabs for grp in groups for a, _ in grp]
    rows = [[grp[0][0].shape[2] if grp[0][1] else grp[0][0].shape[1] for grp in groups] for groups in slabs]
    c_in = [a for c in comms for a in c.inputs]
    c_out = [s for c in comms for s in c.out_shape]
    c_scr = [s for c in comms for s in c.scratch]

    def body(*refs):
        ins, cin, outs, cout, cscr = _split_refs(refs, [len(flat), len(c_in), len(slabs), len(c_out), len(c_scr)])
        bound = _bind(comms, cin, cout, cscr)
        for c, r in bound:
            c.begin(*r)
        pos = 0
        for groups, slab_rows, o_ref in zip(slabs, rows, outs):
            off = 0
            for grp, r in zip(groups, slab_rows):
                vals = []
                for _, transposed in grp:
                    val = ins[pos][0]
                    vals.append(val.T if transposed else val)
                    pos += 1
                o_ref[off:off + r, :] = (vals[0] if len(vals) == 1 else jnp.concatenate(vals, axis=1)).astype(BF16)
                off += r
        for phase in ("mid", "relay", "end"):
            for c, r in bound:
                getattr(c, phase)(*r)

    whole = pl.BlockSpec(memory_space=pltpu.VMEM)
    res = pl.pallas_call(
        body, name=name, in_specs=[whole] * len(flat) + [ANY] * len(c_in), out_specs=[whole] * len(slabs) + [ANY] * len(c_out),
        out_shape=[jax.ShapeDtypeStruct((sum(r), D), BF16) for r in rows] + c_out, scratch_shapes=c_scr,
        compiler_params=_cparams())(*flat, *c_in)
    return list(res[:len(slabs)]), list(res[len(slabs):])


def _step(x, target, p):
    n1, nm, n2 = p["norm_ffn1"], p["norm_mix"], p["norm_ffn2"]
    nf = p["norm_final"].reshape(1, D)
    lng, lnb = p["sgu_ln_g"], p["sgu_ln_b"]
    w_m = jnp.where(jnp.asarray(_sgu_mask())[None], p["sgu_w_s"][0], 0.0).astype(BF16)
    wm2 = jnp.concatenate([w_m[0::2], w_m[1::2]], axis=2)
    w_mt = w_m.transpose(0, 2, 1)
    wmt2 = jnp.concatenate([w_mt[0::2], w_mt[1::2]], axis=2)
    bsx = jnp.repeat(p["sgu_b_s"][0].T, 64, axis=1)
    bias = _band_bias(p["rel_bias"][0])

    def chip_sums(grads, name):
        gots = _comm_only([_SiblingSwap(grads)], "swap_" + name)
        return _pair_sums(grads, gots, "pair_sums_" + name)
    def as_rows(a):
        return jnp.swapaxes(a, 1, 2)

    def updates(parts, sums, names):
        res = {}
        for pt, sm, n in zip(parts, sums, names):
            if p[n].shape[1:] == pt.shape[1:]:
                res[n] = _adamw_matrix(pt, sm, p[n], p["m_" + n], p["v_" + n], False, "adamw_" + n)
            elif p[n].shape[2] > 128:
                res[n] = [as_rows(o) for o in _adamw_matrix(pt, sm, as_rows(p[n]), as_rows(p["m_" + n]), as_rows(p["v_" + n]), False,
                                                            "adamw_" + n)]
            else:
                res[n] = _adamw_matrix(pt, sm, p[n], p["m_" + n], p["v_" + n], True, "adamw_" + n)
        return res

    (rows1,), _ = _pack_rows([[[(as_rows(p["ffn1_w_gate"]), False)], [(as_rows(p["ffn1_w_up"]), False)], [(p["ffn1_w_down"], False)]]],
                             "pack_ffn1")
    (rows_m, rows2d, rows2gu), (gw1,) = _pack_rows(
        [[[(as_rows(p["w_in"]), False)], [(p["w_branch_att"], True), (p["w_branch_sgu"], True)], [(p["w_out"], False)]],
         [[(p["ffn2_w_down"], False)]],
         [[(as_rows(p["ffn2_w_gate"]), False)], [(as_rows(p["ffn2_w_up"]), False)]]],
        "gather_ffn1_pack_rest", [_Gather(rows1)])
    x1, ab1, h1, (gwm,) = _ffn_fwd(x, n1, gw1, gw1, 2 * R_FF, "ffn1_fwd", [_Gather(rows_m)])
    (q, k, v, zs, gt, h2), (gw2d,) = _mix_proj_fwd(x1, nm, p["b_gate"], gwm, [_Gather(rows2d)])
    y_att, (gw2gu,) = _att_fwd(q, k, v, bias, [_Gather(rows2gu)])
    x2, y_sgu, merged = _merge_fwd(x1, zs, gt, y_att, lng, lnb, wm2, bsx, gwm)
    dx3, ab2, hb, _, d_nf, loss = _ffn_fwd(x2, n2, gw2gu, gw2d, 0, "ffn2_fwd", head=(nf, target))

    (dab, g_down), _ = _ffn_bwd_hidden(ab2, dx3, gw2d, 0, "ffn2_bwd_hidden")
    (dx2, d_n2), _ = _ffn_bwd_input(x2, n2, dab, dx3, gw2gu, "ffn2_bwd")
    g_gu = _weight_grad(dab, hb, "ffn2_dw_gate_up", mats=2)
    dzg, dya, dys, d_bg, g_ba, g_bs, g_wo = _merge_bwd(dx2, gt, y_att, y_sgu, merged, gwm)
    g_late = [(g_gu, 0), (g_gu, 1), (g_down, 0), (g_ba, 0), (g_bs, 0), (g_wo, 0)]
    late = ("ffn2_w_gate", "ffn2_w_up", "ffn2_w_down", "w_branch_att", "w_branch_sgu", "w_out")
    (dzs, d_wm, d_bs, d_lng, d_lnb), gots_late = _sgu_bwd(zs, dys, lng, lnb, wm2, wmt2, bsx, [_SiblingSwap(g_late)])
    sums_late = _pair_sums(g_late, gots_late, "pair_sums_late")
    (dq, dk, dv, d_bias), parts_late = _att_bwd(q, k, v, bias, dya, [_ChipScatter(sums_late)])
    big = updates(parts_late, sums_late, late)
    d_rel = _rel_bias_grad(d_bias)
    dz = jnp.concatenate([dq, dk, dv, dzs, dzg], axis=1)
    sums_in = chip_sums([(_weight_grad(dz, h2, "dw_in"), 0)], "w_in")
    (dx1, d_nm), parts_in = _mix_proj_bwd(dz, x1, nm, dx2, gwm, [_ChipScatter(sums_in)])
    big.update(updates(parts_in, sums_in, ("w_in",)))
    small = {"norm_ffn2": d_n2, "norm_final": d_nf, "b_gate": d_bg, "sgu_ln_g": d_lng, "sgu_ln_b": d_lnb, "sgu_b_s": d_bs,
             "rel_bias": d_rel, "sgu_w_s": d_wm.reshape(_SMALL_2D["sgu_w_s"]), "norm_mix": d_nm}

    (dab, g_down), (*small_parts, loss_parts) = _ffn_bwd_hidden(ab1, dx1, gw1, 2 * R_FF, "ffn1_bwd_hidden",
                                                                [_AllToAll(list(small.values()) + [loss])])
    sums_d = chip_sums([(g_down, 0)], "ffn1_down")
    g_gu, (parts_d,) = _weight_grad(dab, h1, "ffn1_dw_gate_up", comms=[_ChipScatter(sums_d)], mats=2)
    sums_gu = chip_sums([(g_gu, 0), (g_gu, 1)], "ffn1_gate_up")
    (dx0, d_n1), parts_gu = _ffn_bwd_input(x, n1, dab, dx1, gw1, "ffn1_bwd", [_ChipScatter(sums_gu)])
    (n1_parts,) = _comm_only([_AllToAll([d_n1])], "gather_norm_ffn1")
    big.update(updates([parts_d] + parts_gu, sums_d + sums_gu, ("ffn1_w_down", "ffn1_w_gate", "ffn1_w_up")))
    out_s, loss_sum = _adamw_small(dict(zip(small, small_parts), norm_ffn1=n1_parts), loss_parts, p)
    return dx0, loss_sum[0, 0], [{**{n: four[i] for n, four in big.items()}, **s} for i, s in enumerate(out_s)]


_OUT_ORDER = ("norm_ffn1", "ffn1_w_gate", "ffn1_w_up", "ffn1_w_down", "norm_mix", "w_in", "b_gate", "rel_bias", "sgu_ln_g", "sgu_ln_b",
              "sgu_w_s", "sgu_b_s", "w_branch_att", "w_branch_sgu", "w_out", "norm_ffn2", "ffn2_w_gate", "ffn2_w_up", "ffn2_w_down",
              "norm_final")


def kernel(x, norm_ffn1, ffn1_w_gate, ffn1_w_up, ffn1_w_down, norm_mix, w_in, b_gate, rel_bias, sgu_ln_g, sgu_ln_b, sgu_w_s, sgu_b_s, w_branch_att, w_branch_sgu, w_out, norm_ffn2, ffn2_w_gate, ffn2_w_up, ffn2_w_down, norm_final, loss_target, m_norm_ffn1, m_ffn1_w_gate, m_ffn1_w_up, m_ffn1_w_down, m_norm_mix, m_w_in, m_b_gate, m_rel_bias, m_sgu_ln_g, m_sgu_ln_b, m_sgu_w_s, m_sgu_b_s, m_w_branch_att, m_w_branch_sgu, m_w_out, m_norm_ffn2, m_ffn2_w_gate, m_ffn2_w_up, m_ffn2_w_down, m_norm_final, v_norm_ffn1, v_ffn1_w_gate, v_ffn1_w_up, v_ffn1_w_down, v_norm_mix, v_w_in, v_b_gate, v_rel_bias, v_sgu_ln_g, v_sgu_ln_b, v_sgu_w_s, v_sgu_b_s, v_w_branch_att, v_w_branch_sgu, v_w_out, v_norm_ffn2, v_ffn2_w_gate, v_ffn2_w_up, v_ffn2_w_down, v_norm_final):
    args = dict(locals())
    dx, loss, outs = _step(x[0], loss_target[0], {pre + n: args[pre + n] for pre in ("", "m_", "v_") for n in _OUT_ORDER})
    return (loss, dx[None], *[o[n] for o in outs for n in _OUT_ORDER])
```

```python
import functools

import numpy as np
import jax
import jax.numpy as jnp
from jax import lax
from jax.experimental import pallas as pl
from jax.experimental.pallas import tpu as pltpu

F32 = jnp.float32
BF16 = jnp.bfloat16

N_DEV = 8
D = 1024
F = 2816
D_ATT = 512
D_SGU = 512
D_IN = 4608
HEADS = 8
CHUNK = 64
N_LEFT = 8
REL_CLIP = 256
N_REL = 2 * REL_CLIP + 1
SGU_BLOCK = 128
EPS = 1e-6
NEG_INF = -1e30
QB = 256
KW = 3 * QB

R_FF, R_IN, R_BR, R_WO = F // N_DEV, D_IN // N_DEV, D // N_DEV, D // N_DEV
OFF_IN, OFF_BR, OFF_WO = 0, R_IN, R_IN + R_BR

FC = 256
TM = 512
VMEM_LIMIT = 56 * 1024 * 1024

ADAM_LR, ADAM_B1, ADAM_B2, ADAM_EPS, ADAM_WD, ADAM_STEP = 0.001, 0.9, 0.999, 1e-08, 0.01, 10

MESH = pl.DeviceIdType.MESH
ANY = pl.BlockSpec(memory_space=pl.ANY)


def _nt(a, b):
    return lax.dot_general(a, b, (((1,), (1,)), ((), ())), preferred_element_type=F32)


def _tn(a, b):
    return lax.dot_general(a, b, (((0,), (0,)), ((), ())), preferred_element_type=F32)


def _nn(a, b):
    return jnp.dot(a, b, preferred_element_type=F32)


def _cparams(sem=None):
    return pltpu.CompilerParams(dimension_semantics=sem, vmem_limit_bytes=VMEM_LIMIT)


def _load_rows(gw_ref, dst, off, rows, sems):
    copies = [pltpu.make_async_copy(gw_ref.at[k, pl.ds(off, rows), :], dst.at[pl.ds(k * rows, rows), :], sems.at[k])
              for k in range(N_DEV)]
    for cp in copies:
        cp.start()
    return copies


def _rms(xv):
    r = lax.rsqrt(jnp.mean(xv * xv, axis=-1, keepdims=True) + EPS)
    return xv * r, r


def _rms_bwd(dh, xn, r, gain):
    dxn = dh * gain
    dx = r * (dxn - xn * jnp.mean(dxn * xn, axis=-1, keepdims=True))
    return dx, jnp.sum(dh * xn, axis=0, keepdims=True)


def _gelu(x):
    t = jnp.tanh(0.7978845608028654 * (x + 0.044715 * x * x * x))
    return 0.5 * x * (1.0 + t), t


def _gelu_grad(x, t):
    return 0.5 * (1.0 + t) + 0.5 * x * (1.0 - t * t) * 0.7978845608028654 * (1.0 + 3.0 * 0.044715 * x * x)


def _place():
    x, y, cc = lax.axis_index("x"), lax.axis_index("y"), lax.axis_index("c")
    return x, y, cc, [(1 - x, y), (x, 1 - y), (1 - x, 1 - y)]


class _Gather:
    def __init__(self, shard):
        self.inputs = [shard]
        self.out_shape = [jax.ShapeDtypeStruct((N_DEV,) + shard.shape, shard.dtype)]
        self.scratch = [pltpu.SemaphoreType.DMA((7,)), pltpu.SemaphoreType.DMA((7,)), pltpu.SemaphoreType.DMA]

    def _copies(self, ins, outs, scr):
        (x_ref,), (out_ref,), (send_sems, recv_sems, local_sem) = ins, outs, scr
        x, y, cc, chips = _place()

        def slab(px, py, pc):
            return out_ref.at[4 * px + 2 * py + pc]

        def copy(k, block, to, src=None):
            return pltpu.make_async_remote_copy(
                src_ref=slab(*block) if src is None else src, dst_ref=slab(*block),
                send_sem=send_sems.at[k], recv_sem=recv_sems.at[k], device_id=to, device_id_type=MESH)

        me, sibling = (x, y, cc), (x, y, 1 - cc)
        x_nbr, y_nbr, diagonal = chips
        mine = pltpu.make_async_copy(x_ref, slab(*me), local_sem)
        first = [copy(0, me, sibling, src=x_ref), copy(1, me, (*x_nbr, cc), src=x_ref), copy(2, me, (*y_nbr, cc), src=x_ref)]
        neighbours = [copy(1, (*x_nbr, cc), me), copy(2, (*y_nbr, cc), me)]
        second_hand = copy(3, (x ^ (1 - cc), y ^ cc, cc), (x ^ cc, y ^ (1 - cc), cc))
        from_diagonal = copy(3, (*diagonal, cc), me)
        passed = [copy(4 + j, (*chip, cc), sibling) for j, chip in enumerate(chips)]
        from_sibling = [copy(0, sibling, me)] + [copy(4 + j, (*chip, 1 - cc), me) for j, chip in enumerate(chips)]
        return mine, first, neighbours, second_hand, from_diagonal, passed, from_sibling

    def begin(self, *refs):
        mine, first = self._copies(*refs)[:2]
        mine.start()
        for cp in first:
            cp.start()

    def mid(self, *refs):
        _, _, neighbours, second_hand, _, passed, _ = self._copies(*refs)
        for cp in neighbours:
            cp.wait_recv()
        second_hand.start()
        passed[0].start()
        passed[1].start()

    def relay(self, *refs):
        _, _, _, _, from_diagonal, passed, _ = self._copies(*refs)
        from_diagonal.wait_recv()
        passed[2].start()

    def end(self, *refs):
        mine, first, _, second_hand, _, passed, from_sibling = self._copies(*refs)
        for cp in from_sibling:
            cp.wait_recv()
        for cp in first + [second_hand] + passed:
            cp.wait_send()
        mine.wait()


class _Direct:
    def begin(self, *refs):
        keep, give = self._copies(*refs)
        for cp in keep + give:
            cp.start()

    def mid(self, *refs):
        pass

    def relay(self, *refs):
        pass

    def end(self, *refs):
        keep, give = self._copies(*refs)
        for cp in give:
            cp.wait_recv()
        for cp in give:
            cp.wait_send()
        for cp in keep:
            cp.wait()


class _SiblingSwap(_Direct):
    def __init__(self, grads):
        n = len(grads)
        self.which = [w for _, w in grads]
        self.inputs = [g for g, _ in grads]
        self.out_shape = [jax.ShapeDtypeStruct((4,) + g.shape[1:], g.dtype) for g, _ in grads]
        self.scratch = [pltpu.SemaphoreType.DMA((n, 4)), pltpu.SemaphoreType.DMA((n, 4))]

    def _copies(self, ins, outs, scr):
        send_sems, recv_sems = scr
        x, y, cc, _ = _place()
        return [], [pltpu.make_async_remote_copy(src_ref=g_ref.at[N_DEV * w + 2 * j + 1 - cc], dst_ref=got_ref.at[j],
                                                 send_sem=send_sems.at[i, j], recv_sem=recv_sems.at[i, j], device_id=(x, y, 1 - cc),
                                                 device_id_type=MESH)
                    for i, (g_ref, got_ref, w) in enumerate(zip(ins, outs, self.which)) for j in range(4)]


class _ChipScatter(_Direct):
    def __init__(self, sums):
        n = len(sums)
        self.inputs = list(sums)
        self.out_shape = [jax.ShapeDtypeStruct((3,) + s.shape[1:], s.dtype) for s in sums]
        self.scratch = [pltpu.SemaphoreType.DMA((n, 3)), pltpu.SemaphoreType.DMA((n, 3))]

    def _copies(self, ins, outs, scr):
        send_sems, recv_sems = scr
        _, _, cc, chips = _place()
        return [], [pltpu.make_async_remote_copy(src_ref=s_ref.at[2 * px + py], dst_ref=got_ref.at[j], send_sem=send_sems.at[i, j],
                                                 recv_sem=recv_sems.at[i, j], device_id=(px, py, cc), device_id_type=MESH)
                    for i, (s_ref, got_ref) in enumerate(zip(ins, outs)) for j, (px, py) in enumerate(chips)]


class _AllToAll(_Direct):
    def __init__(self, blocks):
        n = len(blocks)
        self.inputs = list(blocks)
        self.out_shape = [jax.ShapeDtypeStruct((N_DEV,) + b.shape, b.dtype) for b in blocks]
        self.scratch = [pltpu.SemaphoreType.DMA((n, 7)), pltpu.SemaphoreType.DMA((n, 7)), pltpu.SemaphoreType.DMA((n,))]

    def _copies(self, ins, outs, scr):
        send_sems, recv_sems, local_sems = scr
        x, y, cc, _ = _place()
        me = 4 * x + 2 * y + cc
        keep = [pltpu.make_async_copy(b_ref, got_ref.at[me], local_sems.at[i]) for i, (b_ref, got_ref) in enumerate(zip(ins, outs))]
        give = [pltpu.make_async_remote_copy(src_ref=b_ref, dst_ref=got_ref.at[me], send_sem=send_sems.at[i, k - 1],
                                             recv_sem=recv_sems.at[i, k - 1],
                                             device_id=(x ^ ((k >> 2) & 1), y ^ ((k >> 1) & 1), cc ^ (k & 1)), device_id_type=MESH)
                for i, (b_ref, got_ref) in enumerate(zip(ins, outs)) for k in range(1, N_DEV)]
        return keep, give


def _split_refs(refs, counts):
    out, pos = [], 0
    for n in counts:
        out.append(list(refs[pos:pos + n]))
        pos += n
    return out


def _bind(comms, c_in, c_out, c_scr):
    ins = _split_refs(c_in, [len(c.inputs) for c in comms])
    outs = _split_refs(c_out, [len(c.out_shape) for c in comms])
    scr = _split_refs(c_scr, [len(c.scratch) for c in comms])
    return [(c, (i, o, s)) for c, i, o, s in zip(comms, ins, outs, scr)]


def _call(body, *, name, grid, in_specs, out_specs, out_shape, scratch_shapes, args, comms=()):
    c_in = [a for c in comms for a in c.inputs]
    c_out = [s for c in comms for s in c.out_shape]
    c_scr = [s for c in comms for s in c.scratch]
    counts = [len(in_specs), len(c_in), len(out_shape), len(c_out), len(scratch_shapes), len(c_scr)]

    def full(*refs):
        ins, cin, outs, cout, scr, cscr = _split_refs(refs, counts)
        bound = _bind(comms, cin, cout, cscr)
        if comms:
            def at(steps):
                return functools.reduce(jnp.logical_and, [pl.program_id(ax) == s for ax, s in enumerate(steps)])

            first, last = at([0] * len(grid)), at([n - 1 for n in grid])

            @pl.when(first)
            def _():
                for c, r in bound:
                    c.begin(*r)

            @pl.when(at([(grid[0] - 1) // 2] + [0] * (len(grid) - 1)))
            def _():
                for c, r in bound:
                    c.mid(*r)

            @pl.when(last)
            def _():
                for c, r in bound:
                    c.relay(*r)

        body(*ins, *outs, *scr)
        if comms:
            @pl.when(last)
            def _():
                for c, r in bound:
                    c.end(*r)

    res = pl.pallas_call(
        full, name=name, grid=grid,
        in_specs=list(in_specs) + [ANY] * len(c_in), out_specs=list(out_specs) + [ANY] * len(c_out),
        out_shape=list(out_shape) + c_out, scratch_shapes=list(scratch_shapes) + c_scr,
        compiler_params=_cparams(("arbitrary",) * len(grid)),
    )(*args, *c_in)
    return list(res[:len(out_shape)]), list(res[len(out_shape):])


def _comm_only(comms, name):
    c_in = [a for c in comms for a in c.inputs]
    c_out = [s for c in comms for s in c.out_shape]
    c_scr = [s for c in comms for s in c.scratch]

    def full(*refs):
        cin, cout, cscr = _split_refs(refs, [len(c_in), len(c_out), len(c_scr)])
        bound = _bind(comms, cin, cout, cscr)
        for phase in ("begin", "mid", "relay", "end"):
            for c, r in bound:
                getattr(c, phase)(*r)

    return list(pl.pallas_call(full, name=name, in_specs=[ANY] * len(c_in), out_specs=[ANY] * len(c_out), out_shape=c_out,
                               scratch_shapes=c_scr)(*c_in))


def _my_index(*axes_and_weights):
    return sum(w * lax.axis_index(a) for a, w in axes_and_weights).astype(jnp.int32).reshape(1)


def _pair_sums(grads, gots, name):
    n = len(grads)

    def body(c_ref, *refs):
        for a_ref, b_ref, o_ref in zip(refs[:n], refs[n:2 * n], refs[2 * n:]):
            o_ref[...] = (a_ref[...].astype(F32) + b_ref[...].astype(F32)).astype(BF16)

    def tile(g):
        return pl.BlockSpec((1,) + g.shape[1:], lambda j, c_ref: (j, 0, 0))

    def mine(g, w):
        return pl.BlockSpec((1, None) + g.shape[1:], lambda j, c_ref: (4 * w + j, c_ref[0], 0, 0))

    return list(pl.pallas_call(
        body, name=name,
        grid_spec=pltpu.PrefetchScalarGridSpec(num_scalar_prefetch=1, grid=(4,),
                                               in_specs=[mine(g, w) for g, w in grads] + [tile(g) for g in gots],
                                               out_specs=[tile(g) for g in gots]),
        out_shape=[jax.ShapeDtypeStruct(g.shape, BF16) for g in gots],
        compiler_params=_cparams(("arbitrary",)))(_my_index(("c", 1)), *[g.reshape((-1, 2) + g.shape[1:]) for g, _ in grads], *gots))


def _ffn_fwd(x, gain, gw_gu, gw_d, off_d, name, comms=(), head=None):
    t = x.shape[0]
    n_head = 0 if head is None else 2

    def body(x_ref, g_ref, gu_ref, d_ref, *refs):
        head_in, (o_ref, ab_ref, h_ref), head_out = refs[:n_head], refs[n_head:n_head + 3], refs[n_head + 3:2 * n_head + 3]
        wg, wu, wd, s_scr, sems = refs[2 * n_head + 3:]

        @pl.when(pl.program_id(0) == 0)
        def _():
            cps = _load_rows(gu_ref, wg, 0, R_FF, sems.at[0]) + _load_rows(gu_ref, wu, R_FF, R_FF, sems.at[1]) \
                + _load_rows(d_ref, wd, off_d, R_FF, sems.at[2])
            for o in head_out:
                o[...] = jnp.zeros_like(o)
            for cp in cps:
                cp.wait()

        xv = x_ref[...]
        xn, _ = _rms(xv)
        h = (xn * g_ref[...]).astype(BF16)
        h_ref[...] = h
        for c in range(F // FC):
            rows = pl.ds(c * FC, FC)
            a = _nt(h, wg[rows, :])
            b = _nt(h, wu[rows, :])
            ab_ref[:, c * FC:(c + 1) * FC] = a.astype(BF16)
            ab_ref[:, F + c * FC:F + (c + 1) * FC] = b.astype(BF16)
            s_scr[:, c * FC:(c + 1) * FC] = (a * jax.nn.sigmoid(a) * b).astype(BF16)
        out = xv + 0.5 * _nn(s_scr[...], wd[...])
        if head is None:
            o_ref[...] = out
        else:
            (gf_ref, t_ref), (dg_ref, loss_ref) = head_in, head_out
            gain_f = gf_ref[...]
            yn, r = _rms(out)
            err = yn * gain_f - t_ref[...]
            loss_ref[...] += 0.5 * jnp.sum(jnp.mean(err * err, axis=-1, keepdims=True), axis=0, keepdims=True)
            o_ref[...], dg = _rms_bwd(err * (1.0 / D), yn, r, gain_f)
            dg_ref[...] += dg

    tile = pl.BlockSpec((TM, D), lambda i: (i, 0))
    row = pl.BlockSpec((1, D), lambda i: (0, 0))
    head_specs = [] if head is None else [row, pl.BlockSpec((1, 128), lambda i: (0, 0))]
    head_shapes = [] if head is None else [jax.ShapeDtypeStruct((1, D), F32), jax.ShapeDtypeStruct((1, 128), F32)]
    res, got = _call(
        body, name=name, grid=(t // TM,),
        in_specs=[tile, row, ANY, ANY] + ([] if head is None else [row, tile]),
        out_specs=[tile, pl.BlockSpec((TM, 2 * F), lambda i: (i, 0)), tile] + head_specs,
        out_shape=[jax.ShapeDtypeStruct((t, D), F32), jax.ShapeDtypeStruct((t, 2 * F), BF16), jax.ShapeDtypeStruct((t, D), BF16)] + head_shapes,
        scratch_shapes=[pltpu.VMEM((F, D), BF16)] * 3 + [pltpu.VMEM((TM, F), BF16), pltpu.SemaphoreType.DMA((3, N_DEV))],
        args=(x, gain, gw_gu, gw_d) + (() if head is None else tuple(head)), comms=comms)
    return (res[0], res[1], res[2], got, *res[3:])


def _ffn_bwd_hidden(ab, dout, gw_d, off_d, name, comms=()):
    t = ab.shape[0]
    n_steps = t // TM
    row_chunks = [(r, min(512, F - r)) for r in range(0, F, 512)]

    def tile(w):
        return pl.BlockSpec((TM, w), lambda i: (i, 0))

    def hidden(ab_ref, do_ref, d_ref, dab_ref, dwd_ref, wd, s_scr, acc, sems, out_sem):
        step = pl.program_id(0)

        @pl.when(step == 0)
        def _():
            cps = _load_rows(d_ref, wd, off_d, R_FF, sems)
            acc[...] = jnp.zeros_like(acc)
            for cp in cps:
                cp.wait()

        df = (0.5 * do_ref[...]).astype(BF16)
        for c in range(F // FC):
            a = ab_ref[:, c * FC:(c + 1) * FC].astype(F32)
            b = ab_ref[:, F + c * FC:F + (c + 1) * FC].astype(F32)
            sg = jax.nn.sigmoid(a)
            sl = a * sg
            ds = _nt(df, wd[pl.ds(c * FC, FC), :])
            dab_ref[:, c * FC:(c + 1) * FC] = (ds * b * (sg * (1.0 + a * (1.0 - sg)))).astype(BF16)
            dab_ref[:, F + c * FC:F + (c + 1) * FC] = (ds * sl).astype(BF16)
            s_scr[:, c * FC:(c + 1) * FC] = (sl * b).astype(BF16)
        for r, n in row_chunks:
            acc[r:r + n, :] += _tn(s_scr[:, r:r + n], df)

        @pl.when(step == n_steps - 1)
        def _():
            wd[...] = acc[...].astype(BF16)
            out = pltpu.make_async_copy(wd, dwd_ref, out_sem)
            out.start()
            out.wait()

    (dab, dwd), got = _call(
        hidden, name=name, grid=(n_steps,),
        in_specs=[tile(2 * F), tile(D), ANY], out_specs=[tile(2 * F), ANY],
        out_shape=[jax.ShapeDtypeStruct((t, 2 * F), BF16), jax.ShapeDtypeStruct((F, D), BF16)],
        scratch_shapes=[pltpu.VMEM((F, D), BF16), pltpu.VMEM((TM, F), BF16), pltpu.VMEM((F, D), F32), pltpu.SemaphoreType.DMA((N_DEV,)),
                        pltpu.SemaphoreType.DMA],
        args=(ab, dout, gw_d), comms=comms)
    return (dab, dwd.reshape(N_DEV, R_FF, D)), got


def _ffn_bwd_input(x, gain, dab, dout, gw_gu, name, comms=()):
    t = x.shape[0]

    def body(x_ref, g_ref, dab_ref, do_ref, gu_ref, dx_ref, dg_ref, wgu, sems):
        @pl.when(pl.program_id(0) == 0)
        def _():
            cps = _load_rows(gu_ref, wgu.at[0:F], 0, R_FF, sems.at[0]) + _load_rows(gu_ref, wgu.at[F:2 * F], R_FF, R_FF, sems.at[1])
            dg_ref[...] = jnp.zeros_like(dg_ref)
            for cp in cps:
                cp.wait()

        gain_v = g_ref[...]
        xn, r = _rms(x_ref[...])
        dh = _nn(dab_ref[...], wgu[...])
        dxn, dg = _rms_bwd(dh, xn, r, gain_v)
        dg_ref[...] += dg
        dx_ref[...] = do_ref[...] + dxn

    def tile(w):
        return pl.BlockSpec((TM, w), lambda i: (i, 0))

    row = pl.BlockSpec((1, D), lambda i: (0, 0))
    return _call(
        body, name=name, grid=(t // TM,),
        in_specs=[tile(D), row, tile(2 * F), tile(D), ANY], out_specs=[tile(D), row],
        out_shape=[jax.ShapeDtypeStruct((t, D), F32), jax.ShapeDtypeStruct((1, D), F32)],
        scratch_shapes=[pltpu.VMEM((2 * F, D), BF16), pltpu.SemaphoreType.DMA((2, N_DEV))],
        args=(x, gain, dab, dout, gw_gu), comms=comms)


def _weight_grad(a, b, name, col_off=0, m=None, comms=None, mats=1):
    t = a.shape[0]
    m = a.shape[1] if m is None else m
    n = b.shape[1]
    tmm = 512 if m % 512 == 0 else 256
    first = col_off // tmm

    def body(a_ref, b_ref, o_ref):
        o_ref[...] = _tn(a_ref[...], b_ref[...]).astype(BF16)

    (out,), got = _call(
        body, name=name, grid=(m // tmm,),
        in_specs=[pl.BlockSpec((t, tmm), lambda i: (0, first + i)), pl.BlockSpec((t, n), lambda i: (0, 0))],
        out_specs=[pl.BlockSpec((tmm, n), lambda i: (i, 0))],
        out_shape=[jax.ShapeDtypeStruct((m, n), BF16)], scratch_shapes=[], args=(a, b), comms=comms or ())
    out = out.reshape(mats * N_DEV, m // (mats * N_DEV), n)
    return out if comms is None else (out, got)


def _mix_proj_fwd(x, gain, b_gate, gw, comms=()):
    t = x.shape[0]

    def body(x_ref, g_ref, bg_ref, gw_ref, q_ref, k_ref, v_ref, zs_ref, gt_ref, h_ref, win, sems):
        @pl.when(pl.program_id(0) == 0)
        def _():
            for cp in _load_rows(gw_ref, win, OFF_IN, R_IN, sems):
                cp.wait()

        xn, _ = _rms(x_ref[...])
        h = (xn * g_ref[...]).astype(BF16)
        h_ref[...] = h
        z = _nt(h, win[...])
        q_ref[...] = (z[:, 0:D_ATT] * 0.125).astype(BF16)
        k_ref[...] = z[:, D_ATT:2 * D_ATT].astype(BF16)
        v_ref[...] = z[:, 2 * D_ATT:3 * D_ATT].astype(BF16)
        zs_ref[...] = z[:, 3 * D_ATT:3 * D_ATT + 2 * D_SGU].astype(BF16)
        gt_ref[...] = jax.nn.sigmoid(z[:, 3 * D_ATT + 2 * D_SGU:] + bg_ref[...]).astype(BF16)

    def tile(w):
        return pl.BlockSpec((TM, w), lambda i: (i, 0))

    return _call(
        body, name="mix_proj_fwd", grid=(t // TM,),
        in_specs=[tile(D), pl.BlockSpec((1, D), lambda i: (0, 0)), pl.BlockSpec((1, 2 * D), lambda i: (0, 0)), ANY],
        out_specs=[tile(D_ATT), tile(D_ATT), tile(D_ATT), tile(2 * D_SGU), tile(2 * D), tile(D)],
        out_shape=[jax.ShapeDtypeStruct((t, D_ATT), BF16)] * 3 + [jax.ShapeDtypeStruct((t, 2 * D_SGU), BF16),
                                                                   jax.ShapeDtypeStruct((t, 2 * D), BF16),
                                                                   jax.ShapeDtypeStruct((t, D), BF16)],
        scratch_shapes=[pltpu.VMEM((D_IN, D), BF16), pltpu.SemaphoreType.DMA((N_DEV,))],
        args=(x, gain, b_gate, gw), comms=comms)


def _mix_proj_bwd(dz, x, gain, dres, gw, comms=()):
    t = x.shape[0]

    def body(dz_ref, x_ref, g_ref, dr_ref, gw_ref, dx_ref, dg_ref, win, sems):
        @pl.when(pl.program_id(0) == 0)
        def _():
            cps = _load_rows(gw_ref, win, OFF_IN, R_IN, sems)
            dg_ref[...] = jnp.zeros_like(dg_ref)
            for cp in cps:
                cp.wait()

        dh = _nn(dz_ref[...], win[...])
        xn, r = _rms(x_ref[...])
        dxn, dg = _rms_bwd(dh, xn, r, g_ref[...])
        dg_ref[...] += dg
        dx_ref[...] = dr_ref[...] + dxn

    def tile(w):
        return pl.BlockSpec((TM, w), lambda i: (i, 0))

    row = pl.BlockSpec((1, D), lambda i: (0, 0))
    return _call(
        body, name="mix_proj_bwd", grid=(t // TM,),
        in_specs=[tile(D_IN), tile(D), row, tile(D), ANY], out_specs=[tile(D), row],
        out_shape=[jax.ShapeDtypeStruct((t, D), F32), jax.ShapeDtypeStruct((1, D), F32)],
        scratch_shapes=[pltpu.VMEM((D_IN, D), BF16), pltpu.SemaphoreType.DMA((N_DEV,))],
        args=(dz, x, gain, dres, gw), comms=comms)


SKEW_W = KW + QB
N_CAP = 2 * QB - REL_CLIP + 1


def _band_bias(rel_bias):
    cap = rel_bias[:, 2 * REL_CLIP:]
    diag = jnp.concatenate([jnp.broadcast_to(cap, (HEADS, N_CAP)), rel_bias[:, 2 * REL_CLIP - 1::-1],
                            jnp.broadcast_to(cap, (HEADS, SKEW_W - N_CAP - 2 * REL_CLIP))], axis=1)

    def body(d_ref, o_ref):
        lag = lax.broadcasted_iota(jnp.int32, (QB, KW), 1) // CHUNK - lax.broadcasted_iota(jnp.int32, (QB, KW), 0) // CHUNK
        band = (lag >= 0) & (lag <= N_LEFT)
        for h in range(HEADS):
            rows = jnp.broadcast_to(d_ref[h:h + 1, :], (QB, SKEW_W))
            o_ref[h] = jnp.where(band, pltpu.roll(rows, 0, 1, stride=1, stride_axis=0)[:, :KW], NEG_INF)

    return pl.pallas_call(body, name="band_bias", out_shape=jax.ShapeDtypeStruct((HEADS, QB, KW), F32))(diag)


def _att_specs():
    qspec = pl.BlockSpec((QB, D_ATT), lambda g: (g, 0))
    kspecs = [pl.BlockSpec((QB, D_ATT), lambda g: (jnp.maximum(g - 2, 0), 0)),
              pl.BlockSpec((QB, D_ATT), lambda g: (jnp.maximum(g - 1, 0), 0)), qspec]
    bspec = pl.BlockSpec((HEADS, QB, KW), lambda g: (0, 0, 0))
    return qspec, kspecs, bspec


def _att_probs(qm, kp, bias, valid):
    s = jnp.where(valid, _nt(qm, kp) + bias, NEG_INF)
    e = jnp.exp(s - jnp.max(s, axis=-1, keepdims=True))
    return e / jnp.sum(e, axis=-1, keepdims=True)


def _att_valid():
    g = pl.program_id(0)
    blk = lax.broadcasted_iota(jnp.int32, (QB, KW), 1) // QB
    return (blk + g) >= 2


def _att_fwd(q, k, v, bias, comms=()):
    t = q.shape[0]

    def body(q_ref, k0, k1, k2, v0, v1, v2, b_ref, y_ref):
        valid = _att_valid()
        first = lax.broadcasted_iota(jnp.int32, (1, 128), 1) < 64
        for p in range(HEADS // 2):
            lanes = slice(p * 128, (p + 1) * 128)
            qp = q_ref[:, lanes]
            kp = jnp.concatenate([k0[:, lanes], k1[:, lanes], k2[:, lanes]], axis=0)
            vp = jnp.concatenate([v0[:, lanes], v1[:, lanes], v2[:, lanes]], axis=0)
            out = jnp.zeros((QB, 128), F32)
            for hh in range(2):
                mask = first if hh == 0 else jnp.logical_not(first)
                pr = _att_probs(jnp.where(mask, qp, 0), kp, b_ref[2 * p + hh], valid)
                out = out + _nn(pr.astype(BF16), jnp.where(mask, vp, 0))
            y_ref[:, lanes] = out.astype(BF16)

    qspec, kspecs, bspec = _att_specs()
    (out,), got = _call(
        body, name="att_fwd", grid=(t // QB,),
        in_specs=[qspec] + kspecs + kspecs + [bspec], out_specs=[qspec],
        out_shape=[jax.ShapeDtypeStruct((t, D_ATT), BF16)], scratch_shapes=[],
        args=(q, k, k, k, v, v, v, bias), comms=comms)
    return out, got


def _att_bwd(q, k, v, bias, dy, comms=()):
    t = q.shape[0]
    n_blocks = t // QB

    def body(q_ref, k0, k1, k2, v0, v1, v2, b_ref, dy_ref, dq_ref, dk_ref, dv_ref, db_ref, dk_acc, dv_acc):
        g = pl.program_id(0)

        @pl.when(g == 0)
        def _():
            db_ref[...] = jnp.zeros_like(db_ref)
            dk_acc[...] = jnp.zeros_like(dk_acc)
            dv_acc[...] = jnp.zeros_like(dv_acc)

        valid = _att_valid()
        first = lax.broadcasted_iota(jnp.int32, (1, 128), 1) < 64
        for p in range(HEADS // 2):
            lanes = slice(p * 128, (p + 1) * 128)
            qp = q_ref[:, lanes]
            dyp = dy_ref[:, lanes]
            kp = jnp.concatenate([k0[:, lanes], k1[:, lanes], k2[:, lanes]], axis=0)
            vp = jnp.concatenate([v0[:, lanes], v1[:, lanes], v2[:, lanes]], axis=0)
            dq = jnp.zeros((QB, 128), F32)
            dk = jnp.zeros((KW, 128), F32)
            dv = jnp.zeros((KW, 128), F32)
            for hh in range(2):
                mask = first if hh == 0 else jnp.logical_not(first)
                qm = jnp.where(mask, qp, 0)
                dym = jnp.where(mask, dyp, 0)
                pr = _att_probs(qm, kp, b_ref[2 * p + hh], valid)
                dp = _nt(dym, vp)
                ds = pr * (dp - jnp.sum(dp * pr, axis=-1, keepdims=True))
                db_ref[2 * p + hh] += ds
                dsb = ds.astype(BF16)
                dq = dq + _nn(dsb, jnp.where(mask, kp, 0))
                dk = dk + _tn(dsb, qm)
                dv = dv + _tn(pr.astype(BF16), dym)
            dq_ref[:, lanes] = (dq * 0.125).astype(BF16)
            for j in range(3):
                rows = pl.ds(pl.multiple_of(jnp.maximum(g - 2 + j, 0) * QB, QB), QB)
                dk_acc[rows, lanes] += dk[j * QB:(j + 1) * QB]
                dv_acc[rows, lanes] += dv[j * QB:(j + 1) * QB]

        @pl.when(g == n_blocks - 1)
        def _():
            dk_ref[...] = dk_acc[...].astype(BF16)
            dv_ref[...] = dv_acc[...].astype(BF16)

    qspec, kspecs, bspec = _att_specs()
    full = pl.BlockSpec((t, D_ATT), lambda g: (0, 0))
    return _call(
        body, name="att_bwd", grid=(n_blocks,),
        in_specs=[qspec] + kspecs + kspecs + [bspec, qspec], out_specs=[qspec, full, full, bspec],
        out_shape=[jax.ShapeDtypeStruct((t, D_ATT), BF16)] * 3 + [jax.ShapeDtypeStruct((HEADS, QB, KW), F32)],
        scratch_shapes=[pltpu.VMEM((t, D_ATT), F32)] * 2,
        args=(q, k, k, k, v, v, v, bias, dy), comms=comms)


def _rel_bias_grad(dbias):
    def body(db_ref, cs_ref, tot_ref):
        lane = lax.broadcasted_iota(jnp.int32, (1, SKEW_W), 1)
        capped = (lane < N_CAP) | (lane > KW)
        pad = jnp.zeros((8, QB), F32)
        for h in range(HEADS):
            z = jnp.concatenate([db_ref[h, 0:8, :], pad], axis=1)
            for a in range(1, QB // 8):
                z = z + pltpu.roll(jnp.concatenate([db_ref[h, 8 * a:8 * a + 8, :], pad], axis=1), SKEW_W - 8 * a, 1)
            cs = z[0:1, :]
            for b in range(1, 8):
                cs = cs + pltpu.roll(z[b:b + 1, :], SKEW_W - b, 1)
            cs_ref[h:h + 1, :] = cs
            tot_ref[h:h + 1, :] = jnp.broadcast_to(jnp.sum(jnp.where(capped, cs, 0.0), axis=1, keepdims=True), (1, 128))

    cs, tot = pl.pallas_call(
        body, name="rel_bias_grad",
        out_shape=[jax.ShapeDtypeStruct((HEADS, SKEW_W), F32), jax.ShapeDtypeStruct((HEADS, 128), F32)],
    )(dbias)
    return jnp.concatenate([cs[:, KW:N_CAP - 1:-1], tot[:, :1]], axis=1)


def _sgu_mask():
    pos = np.arange(SGU_BLOCK)
    return (pos[:, None] // CHUNK) >= (pos[None, :] // CHUNK)


def _group_stack(blk, first):
    return jnp.concatenate([jnp.where(first, blk, 0), jnp.where(first, 0, blk)], axis=0)


def _sgu_norm(zs_ref, lng, lnb):
    zs = zs_ref[...].astype(F32)
    ga, th = _gelu(zs)
    u, vs = ga[:, :D_SGU], ga[:, D_SGU:]
    mu = jnp.mean(vs, axis=-1, keepdims=True)
    cen = vs - mu
    rstd = lax.rsqrt(jnp.mean(cen * cen, axis=-1, keepdims=True) + EPS)
    xhat = cen * rstd
    return zs, th, u, xhat, rstd, xhat * lng + lnb


def _sgu_mix(vb, wm2_ref, bsx, s_ref):
    first = lax.broadcasted_iota(jnp.int32, (1, 128), 1) < 64
    for n in range(TM // SGU_BLOCK):
        for p in range(4):
            blk = vb[n * 128:(n + 1) * 128, p * 128:(p + 1) * 128]
            s_ref[n * 128:(n + 1) * 128, p * 128:(p + 1) * 128] = _nn(wm2_ref[p], _group_stack(blk, first)) + bsx[:, p * 128:(p + 1) * 128]


def _merge_fwd(x, zs, gt, y_att, lng, lnb, wm2, bsx, gw):
    t = x.shape[0]

    def body(x_ref, zs_ref, gt_ref, ya_ref, lng_ref, lnb_ref, wm2_ref, bsx_ref, gw_ref, xo_ref, ys_ref, mg_ref,
             wbr, wo, s_scr, sems):
        @pl.when(pl.program_id(0) == 0)
        def _():
            for cp in _load_rows(gw_ref, wbr, OFF_BR, R_BR, sems.at[0]) + _load_rows(gw_ref, wo, OFF_WO, R_WO, sems.at[1]):
                cp.wait()

        _, _, u, _, _, vsn = _sgu_norm(zs_ref, lng_ref[...], lnb_ref[...])
        _sgu_mix(vsn.astype(BF16), wm2_ref, bsx_ref[...], s_scr)
        ys = (u * s_scr[...]).astype(BF16)
        ys_ref[...] = ys
        pa = _nt(ya_ref[...], wbr[:, :D_ATT])
        ps = _nt(ys, wbr[:, D_ATT:])
        mg = (gt_ref[:, :D].astype(F32) * pa + gt_ref[:, D:].astype(F32) * ps).astype(BF16)
        mg_ref[...] = mg
        xo_ref[...] = x_ref[...] + _nn(mg, wo[...])

    def tile(w):
        return pl.BlockSpec((TM, w), lambda i: (i, 0))

    def const(shape):
        return pl.BlockSpec(shape, lambda i: (0,) * len(shape))

    return pl.pallas_call(
        body, name="merge_fwd", grid=(t // TM,),
        in_specs=[tile(D), tile(2 * D_SGU), tile(2 * D), tile(D_ATT), const((1, D_SGU)), const((1, D_SGU)),
                  const((4, 128, 256)), const((128, D_SGU)), ANY],
        out_specs=[tile(D), tile(D_SGU), tile(D)],
        out_shape=[jax.ShapeDtypeStruct((t, D), F32), jax.ShapeDtypeStruct((t, D_SGU), BF16), jax.ShapeDtypeStruct((t, D), BF16)],
        scratch_shapes=[pltpu.VMEM((D, D), BF16), pltpu.VMEM((D, D), BF16), pltpu.VMEM((TM, D_SGU), F32),
                        pltpu.SemaphoreType.DMA((2, N_DEV))],
        compiler_params=_cparams(("arbitrary",)),
    )(x, zs, gt, y_att, lng, lnb, wm2, bsx, gw)


def _merge_bwd(dx, gt, y_att, y_sgu, merged, gw):
    t = dx.shape[0]
    n_steps = t // TM

    def body(dx_ref, gt_ref, ya_ref, ys_ref, mg_ref, gw_ref, dzg_ref, dya_ref, dys_ref, dbg_ref, ga_ref, gs_ref, go_ref,
             wbr, wo, acc_a, acc_s, acc_o, sems):
        step = pl.program_id(0)

        @pl.when(step == 0)
        def _():
            cps = _load_rows(gw_ref, wbr, OFF_BR, R_BR, sems.at[0]) + _load_rows(gw_ref, wo, OFF_WO, R_WO, sems.at[1])
            dbg_ref[...] = jnp.zeros_like(dbg_ref)
            for acc in (acc_a, acc_s, acc_o):
                acc[...] = jnp.zeros_like(acc)
            for cp in cps:
                cp.wait()

        dxb = dx_ref[...].astype(BF16)
        acc_o[...] += _tn(mg_ref[...], dxb)
        dm = _nt(dxb, wo[...])
        for half, y_ref, w, acc in ((0, ya_ref, wbr.at[:, :D_ATT], acc_a), (1, ys_ref, wbr.at[:, D_ATT:], acc_s)):
            cols = slice(half * D, (half + 1) * D)
            gate = gt_ref[:, cols].astype(F32)
            branch = _nt(y_ref[...], w[...])
            dzg = dm * branch * gate * (1.0 - gate)
            dbg_ref[:, cols] += jnp.sum(dzg, axis=0, keepdims=True)
            dzg_ref[:, cols] = dzg.astype(BF16)
            dbr = (dm * gate).astype(BF16)
            acc[...] += _tn(dbr, y_ref[...])
            dy = _nn(dbr, w[...])
            if half == 0:
                dya_ref[...] = dy.astype(BF16)
            else:
                dys_ref[...] = dy

        @pl.when(step == n_steps - 1)
        def _():
            ga_ref[...] = acc_a[...].astype(BF16)
            gs_ref[...] = acc_s[...].astype(BF16)
            go_ref[...] = acc_o[...].astype(BF16)

    def tile(w):
        return pl.BlockSpec((TM, w), lambda i: (i, 0))

    def whole(r, c):
        return pl.BlockSpec((r, c), lambda i: (0, 0))

    dzg, dya, dys, dbg, g_ba, g_bs, g_wo = pl.pallas_call(
        body, name="merge_bwd", grid=(n_steps,),
        in_specs=[tile(D), tile(2 * D), tile(D_ATT), tile(D_SGU), tile(D), ANY],
        out_specs=[tile(2 * D), tile(D_ATT), tile(D_SGU), whole(1, 2 * D), whole(D, D_ATT), whole(D, D_SGU), whole(D, D)],
        out_shape=[jax.ShapeDtypeStruct((t, 2 * D), BF16), jax.ShapeDtypeStruct((t, D_ATT), BF16), jax.ShapeDtypeStruct((t, D_SGU), F32),
                   jax.ShapeDtypeStruct((1, 2 * D), F32), jax.ShapeDtypeStruct((D, D_ATT), BF16), jax.ShapeDtypeStruct((D, D_SGU), BF16),
                   jax.ShapeDtypeStruct((D, D), BF16)],
        scratch_shapes=[pltpu.VMEM((D, D), BF16), pltpu.VMEM((D, D), BF16), pltpu.VMEM((D, D_ATT), F32), pltpu.VMEM((D, D_SGU), F32),
                        pltpu.VMEM((D, D), F32), pltpu.SemaphoreType.DMA((2, N_DEV))],
        compiler_params=_cparams(("arbitrary",)),
    )(dx, gt, y_att, y_sgu, merged, gw)
    return dzg, dya, dys, dbg, g_ba.reshape(N_DEV, R_BR, D_ATT), g_bs.reshape(N_DEV, R_BR, D_SGU), g_wo.reshape(N_DEV, R_WO, D)


def _sgu_bwd(zs, dys, lng, lnb, wm2, wmt2, bsx, comms=()):
    t = zs.shape[0]
    n_steps = t // TM

    def body(zs_ref, dys_ref, lng_ref, lnb_ref, wm2_ref, wmt2_ref, bsx_ref, dzs_ref, dw_ref, dbs_ref, dlg_ref, dlb_ref,
             s_scr, dv_scr, ds_acc):
        i = pl.program_id(0)

        @pl.when(i == 0)
        def _():
            dw_ref[...] = jnp.zeros_like(dw_ref)
            dlg_ref[...] = jnp.zeros_like(dlg_ref)
            dlb_ref[...] = jnp.zeros_like(dlb_ref)
            ds_acc[...] = jnp.zeros_like(ds_acc)

        lng = lng_ref[...]
        zs, th, u, xhat, rstd, vsn = _sgu_norm(zs_ref, lng, lnb_ref[...])
        vb = vsn.astype(BF16)
        _sgu_mix(vb, wm2_ref, bsx_ref[...], s_scr)
        dys = dys_ref[...]
        du = dys * s_scr[...]
        ds = dys * u
        dsb = ds.astype(BF16)
        first = lax.broadcasted_iota(jnp.int32, (1, 128), 1) < 64
        acc = jnp.zeros((SGU_BLOCK, D_SGU), F32)
        for n in range(TM // SGU_BLOCK):
            rows = slice(n * 128, (n + 1) * 128)
            acc = acc + ds[rows]
            for p in range(4):
                lanes = slice(p * 128, (p + 1) * 128)
                stack = _group_stack(dsb[rows, lanes], first)
                dv_scr[rows, lanes] = _nn(wmt2_ref[p], stack)
                dw_ref[p] += _nt(stack, vb[rows, lanes])
        ds_acc[...] += acc
        dvsn = dv_scr[...]
        dlg_ref[...] += jnp.sum(dvsn * xhat, axis=0, keepdims=True)
        dlb_ref[...] += jnp.sum(dvsn, axis=0, keepdims=True)
        dxh = dvsn * lng
        dvs = rstd * (dxh - jnp.mean(dxh, axis=-1, keepdims=True) - xhat * jnp.mean(dxh * xhat, axis=-1, keepdims=True))
        dga = jnp.concatenate([du, dvs], axis=1)
        dzs_ref[...] = (dga * _gelu_grad(zs, th)).astype(BF16)

        @pl.when(i == n_steps - 1)
        def _():
            r = lax.broadcasted_iota(jnp.int32, (256, 128), 0) % SGU_BLOCK
            c = lax.broadcasted_iota(jnp.int32, (256, 128), 1)
            keep = (r // CHUNK) >= (c // CHUNK)
            for p in range(4):
                dw_ref[p] = jnp.where(keep, dw_ref[p], 0.0)
            total = ds_acc[...]
            grp = lax.broadcasted_iota(jnp.int32, (SGU_BLOCK, D_SGU), 1) // 64
            lane = lax.broadcasted_iota(jnp.int32, (SGU_BLOCK, 128), 1)
            out = jnp.zeros((SGU_BLOCK, 128), F32)
            for gi in range(8):
                out = jnp.where(lane == gi, jnp.sum(jnp.where(grp == gi, total, 0.0), axis=1, keepdims=True), out)
            dbs_ref[...] = out

    def tile(w):
        return pl.BlockSpec((TM, w), lambda i: (i, 0))

    def const(shape):
        return pl.BlockSpec(shape, lambda i: (0,) * len(shape))

    return _call(
        body, name="sgu_bwd", grid=(n_steps,),
        in_specs=[tile(2 * D_SGU), tile(D_SGU), const((1, D_SGU)), const((1, D_SGU)), const((4, 128, 256)), const((4, 128, 256)),
                  const((128, D_SGU))],
        out_specs=[tile(2 * D_SGU), const((4, 256, 128)), const((128, 128)), const((1, D_SGU)), const((1, D_SGU))],
        out_shape=[jax.ShapeDtypeStruct((t, 2 * D_SGU), BF16), jax.ShapeDtypeStruct((4, 256, 128), F32),
                   jax.ShapeDtypeStruct((128, 128), F32), jax.ShapeDtypeStruct((1, D_SGU), F32), jax.ShapeDtypeStruct((1, D_SGU), F32)],
        scratch_shapes=[pltpu.VMEM((TM, D_SGU), F32), pltpu.VMEM((TM, D_SGU), F32), pltpu.VMEM((SGU_BLOCK, D_SGU), F32)],
        args=(zs, dys, lng, lnb, wm2, wmt2, bsx), comms=comms)


def _adamw(g, w, m, v):
    m = ADAM_B1 * m + (1.0 - ADAM_B1) * g
    v = ADAM_B2 * v + (1.0 - ADAM_B2) * (g * g)
    m_hat = m / (1.0 - ADAM_B1 ** ADAM_STEP)
    v_hat = v / (1.0 - ADAM_B2 ** ADAM_STEP)
    return -ADAM_LR * (m_hat / (jnp.sqrt(v_hat) + ADAM_EPS) + ADAM_WD * w), m, v


def _adamw_matrices(items, name):
    n = len(items)
    c = items[0][0].shape[2]
    tc = 256

    def body(own_ref, *refs):
        for i, (_, _, _, _, _, transposed) in enumerate(items):
            p_ref, s_ref, w_ref, m_ref, v_ref = refs[5 * i:5 * i + 5]
            g = p_ref[0].astype(F32) + p_ref[1].astype(F32) + p_ref[2].astype(F32) + s_ref[...].astype(F32)
            g = g.T if transposed else g
            res = (g,) + _adamw(g, w_ref[...], m_ref[...], v_ref[...])
            for o_ref, val in zip(refs[5 * n + 4 * i:5 * n + 4 * i + 4], res):
                o_ref[...] = val

    in_specs, out_specs, out_shape, args = [], [], [], []
    for parts, sums, w, m, v, transposed in items:
        r = parts.shape[1]
        own = pl.BlockSpec((None, tc, r), lambda i, o: (0, i, 0)) if transposed else pl.BlockSpec((None, r, tc), lambda i, o: (0, 0, i))
        in_specs += [pl.BlockSpec((3, r, tc), lambda i, o: (0, 0, i)), pl.BlockSpec((None, r, tc), lambda i, o: (o[0], 0, i)), own, own, own]
        out_specs += [own] * 4
        out_shape += [jax.ShapeDtypeStruct(w.shape, F32)] * 4
        args += [parts, sums, w, m, v]
    res = pl.pallas_call(
        body, name=name,
        grid_spec=pltpu.PrefetchScalarGridSpec(num_scalar_prefetch=1, grid=(c // tc,), in_specs=in_specs, out_specs=out_specs),
        out_shape=out_shape, compiler_params=_cparams(("arbitrary",)),
    )(_my_index(("x", 2), ("y", 1)), *args)
    return [list(res[4 * i:4 * i + 4]) for i in range(n)]


_SMALL_2D = {"norm_ffn1": (1, D), "norm_mix": (1, D), "norm_ffn2": (1, D), "norm_final": (1, D), "b_gate": (1, 2 * D),
             "sgu_ln_g": (1, D_SGU), "sgu_ln_b": (1, D_SGU), "sgu_b_s": (8, SGU_BLOCK), "rel_bias": (HEADS, N_REL),
             "sgu_w_s": (8 * SGU_BLOCK, SGU_BLOCK)}


def _adamw_small(parts, loss_parts, p):
    names = list(parts)
    n = len(names)

    def body(*refs):
        got, loss_got, wmv, outs, loss_out = refs[:n], refs[n], refs[n + 1:4 * n + 1], refs[4 * n + 1:8 * n + 1], refs[8 * n + 1]
        for i, name in enumerate(names):
            g = got[i][0]
            for k in range(1, N_DEV):
                g = g + got[i][k]
            if name == "sgu_b_s":
                g = g.T[0:8, :]
            res = (g,) + _adamw(g, wmv[3 * i][...], wmv[3 * i + 1][...], wmv[3 * i + 2][...])
            for o_ref, val in zip(outs[4 * i:4 * i + 4], res):
                o_ref[...] = val
        total = loss_got[0]
        for k in range(1, N_DEV):
            total = total + loss_got[k]
        loss_out[...] = total

    wmv = [p[pre + name].reshape(_SMALL_2D[name]) for name in names for pre in ("", "m_", "v_")]
    res = pl.pallas_call(
        body, name="adamw_small",
        out_shape=[jax.ShapeDtypeStruct(_SMALL_2D[name], F32) for name in names for _ in range(4)] + [jax.ShapeDtypeStruct((1, 128), F32)],
        compiler_params=_cparams())(*[parts[name] for name in names], loss_parts, *wmv)
    return [{name: res[4 * i + j].reshape(p[name].shape) for i, name in enumerate(names)} for j in range(4)], res[-1]


def _pack_rows(slabs, name, comms=()):
    flat = [a for groups in slabs for grp in groups for a, _ in grp]
    rows = [[grp[0][0].shape[2] if grp[0][1] else grp[0][0].shape[1] for grp in groups] for groups in slabs]
    c_in = [a for c in comms for a in c.inputs]
    c_out = [s for c in comms for s in c.out_shape]
    c_scr = [s for c in comms for s in c.scratch]

    def body(*refs):
        ins, cin, outs, cout, cscr = _split_refs(refs, [len(flat), len(c_in), len(slabs), len(c_out), len(c_scr)])
        bound = _bind(comms, cin, cout, cscr)
        for c, r in bound:
            c.begin(*r)
        pos = 0
        for groups, slab_rows, o_ref in zip(slabs, rows, outs):
            off = 0
            for grp, r in zip(groups, slab_rows):
                vals = []
                for _, transposed in grp:
                    val = ins[pos][0]
                    vals.append(val.T if transposed else val)
                    pos += 1
                o_ref[off:off + r, :] = (vals[0] if len(vals) == 1 else jnp.concatenate(vals, axis=1)).astype(BF16)
                off += r
        for phase in ("mid", "relay", "end"):
            for c, r in bound:
                getattr(c, phase)(*r)

    whole = pl.BlockSpec(memory_space=pltpu.VMEM)
    res = pl.pallas_call(
        body, name=name, in_specs=[whole] * len(flat) + [ANY] * len(c_in), out_specs=[whole] * len(slabs) + [ANY] * len(c_out),
        out_shape=[jax.ShapeDtypeStruct((sum(r), D), BF16) for r in rows] + c_out, scratch_shapes=c_scr,
        compiler_params=_cparams())(*flat, *c_in)
    return list(res[:len(slabs)]), list(res[len(slabs):])


def _step(x, target, p):
    n1, nm, n2 = p["norm_ffn1"], p["norm_mix"], p["norm_ffn2"]
    nf = p["norm_final"].reshape(1, D)
    lng, lnb = p["sgu_ln_g"], p["sgu_ln_b"]
    w_m = jnp.where(jnp.asarray(_sgu_mask())[None], p["sgu_w_s"][0], 0.0).astype(BF16)
    wm2 = jnp.concatenate([w_m[0::2], w_m[1::2]], axis=2)
    w_mt = w_m.transpose(0, 2, 1)
    wmt2 = jnp.concatenate([w_mt[0::2], w_mt[1::2]], axis=2)
    bsx = jnp.repeat(p["sgu_b_s"][0].T, 64, axis=1)
    bias = _band_bias(p["rel_bias"][0])

    def chip_sums(grads, name):
        gots = _comm_only([_SiblingSwap(grads)], "swap_" + name)
        return _pair_sums(grads, gots, "pair_sums_" + name)
    def as_rows(a):
        return jnp.swapaxes(a, 1, 2)

    def updates(parts, sums, names):
        res = {}
        for cols in sorted({pt.shape[2] for pt in parts}):
            items, group = [], [(pt, sm, n) for pt, sm, n in zip(parts, sums, names) if pt.shape[2] == cols]
            for pt, sm, n in group:
                view = as_rows if p[n].shape[1:] != pt.shape[1:] and p[n].shape[2] > 128 else (lambda a: a)
                transposed = p[n].shape[1:] != pt.shape[1:] and p[n].shape[2] <= 128
                items.append((pt, sm, view(p[n]), view(p["m_" + n]), view(p["v_" + n]), transposed))
            for (pt, sm, n), four in zip(group, _adamw_matrices(items, "adamw_" + group[0][2])):
                res[n] = [as_rows(o) if o.shape != p[n].shape else o for o in four]
        return res

    (rows1,), _ = _pack_rows([[[(as_rows(p["ffn1_w_gate"]), False)], [(as_rows(p["ffn1_w_up"]), False)], [(p["ffn1_w_down"], False)]]],
                             "pack_ffn1")
    (rows_m, rows2d, rows2gu), (gw1,) = _pack_rows(
        [[[(as_rows(p["w_in"]), False)], [(p["w_branch_att"], True), (p["w_branch_sgu"], True)], [(p["w_out"], False)]],
         [[(p["ffn2_w_down"], False)]],
         [[(as_rows(p["ffn2_w_gate"]), False)], [(as_rows(p["ffn2_w_up"]), False)]]],
        "gather_ffn1_pack_rest", [_Gather(rows1)])
    x1, ab1, h1, (gwm,) = _ffn_fwd(x, n1, gw1, gw1, 2 * R_FF, "ffn1_fwd", [_Gather(rows_m)])
    (q, k, v, zs, gt, h2), (gw2d,) = _mix_proj_fwd(x1, nm, p["b_gate"], gwm, [_Gather(rows2d)])
    y_att, (gw2gu,) = _att_fwd(q, k, v, bias, [_Gather(rows2gu)])
    x2, y_sgu, merged = _merge_fwd(x1, zs, gt, y_att, lng, lnb, wm2, bsx, gwm)
    dx3, ab2, hb, _, d_nf, loss = _ffn_fwd(x2, n2, gw2gu, gw2d, 0, "ffn2_fwd", head=(nf, target))

    (dab, g_down), _ = _ffn_bwd_hidden(ab2, dx3, gw2d, 0, "ffn2_bwd_hidden")
    (dx2, d_n2), _ = _ffn_bwd_input(x2, n2, dab, dx3, gw2gu, "ffn2_bwd")
    g_gu = _weight_grad(dab, hb, "ffn2_dw_gate_up", mats=2)
    dzg, dya, dys, d_bg, g_ba, g_bs, g_wo = _merge_bwd(dx2, gt, y_att, y_sgu, merged, gwm)
    g_late = [(g_gu, 0), (g_gu, 1), (g_down, 0), (g_ba, 0), (g_bs, 0), (g_wo, 0)]
    late = ("ffn2_w_gate", "ffn2_w_up", "ffn2_w_down", "w_branch_att", "w_branch_sgu", "w_out")
    (dzs, d_wm, d_bs, d_lng, d_lnb), gots_late = _sgu_bwd(zs, dys, lng, lnb, wm2, wmt2, bsx, [_SiblingSwap(g_late)])
    sums_late = _pair_sums(g_late, gots_late, "pair_sums_late")
    (dq, dk, dv, d_bias), parts_late = _att_bwd(q, k, v, bias, dya, [_ChipScatter(sums_late)])
    big = updates(parts_late, sums_late, late)
    d_rel = _rel_bias_grad(d_bias)
    dz = jnp.concatenate([dq, dk, dv, dzs, dzg], axis=1)
    sums_in = chip_sums([(_weight_grad(dz, h2, "dw_in"), 0)], "w_in")
    (dx1, d_nm), parts_in = _mix_proj_bwd(dz, x1, nm, dx2, gwm, [_ChipScatter(sums_in)])
    big.update(updates(parts_in, sums_in, ("w_in",)))
    small = {"norm_ffn2": d_n2, "norm_final": d_nf, "b_gate": d_bg, "sgu_ln_g": d_lng, "sgu_ln_b": d_lnb, "sgu_b_s": d_bs,
             "rel_bias": d_rel, "sgu_w_s": d_wm.reshape(_SMALL_2D["sgu_w_s"]), "norm_mix": d_nm}

    (dab, g_down), (*small_parts, loss_parts) = _ffn_bwd_hidden(ab1, dx1, gw1, 2 * R_FF, "ffn1_bwd_hidden",
                                                                [_AllToAll(list(small.values()) + [loss])])
    sums_d = chip_sums([(g_down, 0)], "ffn1_down")
    g_gu, (parts_d,) = _weight_grad(dab, h1, "ffn1_dw_gate_up", comms=[_ChipScatter(sums_d)], mats=2)
    sums_gu = chip_sums([(g_gu, 0), (g_gu, 1)], "ffn1_gate_up")
    (dx0, d_n1), parts_gu = _ffn_bwd_input(x, n1, dab, dx1, gw1, "ffn1_bwd", [_ChipScatter(sums_gu)])
    (n1_parts,) = _comm_only([_AllToAll([d_n1])], "gather_norm_ffn1")
    big.update(updates([parts_d] + parts_gu, sums_d + sums_gu, ("ffn1_w_down", "ffn1_w_gate", "ffn1_w_up")))
    out_s, loss_sum = _adamw_small(dict(zip(small, small_parts), norm_ffn1=n1_parts), loss_parts, p)
    return dx0, loss_sum[0, 0], [{**{n: four[i] for n, four in big.items()}, **s} for i, s in enumerate(out_s)]


_OUT_ORDER = ("norm_ffn1", "ffn1_w_gate", "ffn1_w_up", "ffn1_w_down", "norm_mix", "w_in", "b_gate", "rel_bias", "sgu_ln_g", "sgu_ln_b",
              "sgu_w_s", "sgu_b_s", "w_branch_att", "w_branch_sgu", "w_out", "norm_ffn2", "ffn2_w_gate", "ffn2_w_up", "ffn2_w_down",
              "norm_final")


def kernel(x, norm_ffn1, ffn1_w_gate, ffn1_w_up, ffn1_w_down, norm_mix, w_in, b_gate, rel_bias, sgu_ln_g, sgu_ln_b, sgu_w_s, sgu_b_s, w_branch_att, w_branch_sgu, w_out, norm_ffn2, ffn2_w_gate, ffn2_w_up, ffn2_w_down, norm_final, loss_target, m_norm_ffn1, m_ffn1_w_gate, m_ffn1_w_up, m_ffn1_w_down, m_norm_mix, m_w_in, m_b_gate, m_rel_bias, m_sgu_ln_g, m_sgu_ln_b, m_sgu_w_s, m_sgu_b_s, m_w_branch_att, m_w_branch_sgu, m_w_out, m_norm_ffn2, m_ffn2_w_gate, m_ffn2_w_up, m_ffn2_w_down, m_norm_final, v_norm_ffn1, v_ffn1_w_gate, v_ffn1_w_up, v_ffn1_w_down, v_norm_mix, v_w_in, v_b_gate, v_rel_bias, v_sgu_ln_g, v_sgu_ln_b, v_sgu_w_s, v_sgu_b_s, v_w_branch_att, v_w_branch_sgu, v_w_out, v_norm_ffn2, v_ffn2_w_gate, v_ffn2_w_up, v_ffn2_w_down, v_norm_final):
    args = dict(locals())
    dx, loss, outs = _step(x[0], loss_target[0], {pre + n: args[pre + n] for pre in ("", "m_", "v_") for n in _OUT_ORDER})
    return (loss, dx[None], *[o[n] for o in outs for n in _OUT_ORDER])
```

```python
import functools

import numpy as np
import jax
import jax.numpy as jnp
from jax import lax
from jax.experimental import pallas as pl
from jax.experimental.pallas import tpu as pltpu

F32 = jnp.float32
BF16 = jnp.bfloat16

N_DEV = 8
D = 1024
F = 2816
D_ATT = 512
D_SGU = 512
D_IN = 4608
HEADS = 8
CHUNK = 64
N_LEFT = 8
REL_CLIP = 256
N_REL = 2 * REL_CLIP + 1
SGU_BLOCK = 128
EPS = 1e-6
NEG_INF = -1e30
QB = 256
KW = 3 * QB

R_FF, R_IN, R_BR, R_WO = F // N_DEV, D_IN // N_DEV, D // N_DEV, D // N_DEV
OFF_IN, OFF_BR, OFF_WO = 0, R_IN, R_IN + R_BR

FC = 256
TM = 512
VMEM_LIMIT = 56 * 1024 * 1024

ADAM_LR, ADAM_B1, ADAM_B2, ADAM_EPS, ADAM_WD, ADAM_STEP = 0.001, 0.9, 0.999, 1e-08, 0.01, 10

MESH = pl.DeviceIdType.MESH
ANY = pl.BlockSpec(memory_space=pl.ANY)


def _nt(a, b):
    return lax.dot_general(a, b, (((1,), (1,)), ((), ())), preferred_element_type=F32)


def _tn(a, b):
    return lax.dot_general(a, b, (((0,), (0,)), ((), ())), preferred_element_type=F32)


def _nn(a, b):
    return jnp.dot(a, b, preferred_element_type=F32)


def _cparams(sem=None):
    return pltpu.CompilerParams(dimension_semantics=sem, vmem_limit_bytes=VMEM_LIMIT)


def _load_rows(gw_ref, dst, off, rows, sems):
    copies = [pltpu.make_async_copy(gw_ref.at[k, pl.ds(off, rows), :], dst.at[pl.ds(k * rows, rows), :], sems.at[k])
              for k in range(N_DEV)]
    for cp in copies:
        cp.start()
    return copies


def _rms(xv):
    r = lax.rsqrt(jnp.mean(xv * xv, axis=-1, keepdims=True) + EPS)
    return xv * r, r


def _rms_bwd(dh, xn, r, gain):
    dxn = dh * gain
    dx = r * (dxn - xn * jnp.mean(dxn * xn, axis=-1, keepdims=True))
    return dx, jnp.sum(dh * xn, axis=0, keepdims=True)


def _gelu(x):
    t = jnp.tanh(0.7978845608028654 * (x + 0.044715 * x * x * x))
    return 0.5 * x * (1.0 + t), t


def _gelu_grad(x, t):
    return 0.5 * (1.0 + t) + 0.5 * x * (1.0 - t * t) * 0.7978845608028654 * (1.0 + 3.0 * 0.044715 * x * x)


def _place():
    x, y, cc = lax.axis_index("x"), lax.axis_index("y"), lax.axis_index("c")
    return x, y, cc, [(1 - x, y), (x, 1 - y), (1 - x, 1 - y)]


class _Gather:
    def __init__(self, shard):
        self.inputs = [shard]
        self.out_shape = [jax.ShapeDtypeStruct((N_DEV,) + shard.shape, shard.dtype)]
        self.scratch = [pltpu.SemaphoreType.DMA((7,)), pltpu.SemaphoreType.DMA((7,)), pltpu.SemaphoreType.DMA]

    def _copies(self, ins, outs, scr):
        (x_ref,), (out_ref,), (send_sems, recv_sems, local_sem) = ins, outs, scr
        x, y, cc, chips = _place()

        def slab(px, py, pc):
            return out_ref.at[4 * px + 2 * py + pc]

        def copy(k, block, to, src=None):
            return pltpu.make_async_remote_copy(
                src_ref=slab(*block) if src is None else src, dst_ref=slab(*block),
                send_sem=send_sems.at[k], recv_sem=recv_sems.at[k], device_id=to, device_id_type=MESH)

        me, sibling = (x, y, cc), (x, y, 1 - cc)
        x_nbr, y_nbr, diagonal = chips
        mine = pltpu.make_async_copy(x_ref, slab(*me), local_sem)
        first = [copy(0, me, sibling, src=x_ref), copy(1, me, (*x_nbr, cc), src=x_ref), copy(2, me, (*y_nbr, cc), src=x_ref)]
        neighbours = [copy(1, (*x_nbr, cc), me), copy(2, (*y_nbr, cc), me)]
        second_hand = copy(3, (x ^ (1 - cc), y ^ cc, cc), (x ^ cc, y ^ (1 - cc), cc))
        from_diagonal = copy(3, (*diagonal, cc), me)
        passed = [copy(4 + j, (*chip, cc), sibling) for j, chip in enumerate(chips)]
        from_sibling = [copy(0, sibling, me)] + [copy(4 + j, (*chip, 1 - cc), me) for j, chip in enumerate(chips)]
        return mine, first, neighbours, second_hand, from_diagonal, passed, from_sibling

    def begin(self, *refs):
        mine, first = self._copies(*refs)[:2]
        mine.start()
        for cp in first:
            cp.start()

    def mid(self, *refs):
        _, _, neighbours, second_hand, _, passed, _ = self._copies(*refs)
        for cp in neighbours:
            cp.wait_recv()
        second_hand.start()
        passed[0].start()
        passed[1].start()

    def relay(self, *refs):
        _, _, _, _, from_diagonal, passed, _ = self._copies(*refs)
        from_diagonal.wait_recv()
        passed[2].start()

    def end(self, *refs):
        mine, first, _, second_hand, _, passed, from_sibling = self._copies(*refs)
        for cp in from_sibling:
            cp.wait_recv()
        for cp in first + [second_hand] + passed:
            cp.wait_send()
        mine.wait()


class _Direct:
    def begin(self, *refs):
        keep, give = self._copies(*refs)
        for cp in keep + give:
            cp.start()

    def mid(self, *refs):
        pass

    def relay(self, *refs):
        pass

    def end(self, *refs):
        keep, give = self._copies(*refs)
        for cp in give:
            cp.wait_recv()
        for cp in give:
            cp.wait_send()
        for cp in keep:
            cp.wait()


class _SiblingSwap(_Direct):
    def __init__(self, grads):
        n = len(grads)
        self.which = [w for _, w in grads]
        self.inputs = [g for g, _ in grads]
        self.out_shape = [jax.ShapeDtypeStruct((4,) + g.shape[1:], g.dtype) for g, _ in grads]
        self.scratch = [pltpu.SemaphoreType.DMA((n, 4)), pltpu.SemaphoreType.DMA((n, 4))]

    def _copies(self, ins, outs, scr):
        send_sems, recv_sems = scr
        x, y, cc, _ = _place()
        return [], [pltpu.make_async_remote_copy(src_ref=g_ref.at[N_DEV * w + 2 * j + 1 - cc], dst_ref=got_ref.at[j],
                                                 send_sem=send_sems.at[i, j], recv_sem=recv_sems.at[i, j], device_id=(x, y, 1 - cc),
                                                 device_id_type=MESH)
                    for i, (g_ref, got_ref, w) in enumerate(zip(ins, outs, self.which)) for j in range(4)]


class _ChipScatter(_Direct):
    def __init__(self, sums):
        n = len(sums)
        self.inputs = list(sums)
        self.out_shape = [jax.ShapeDtypeStruct((3,) + s.shape[1:], s.dtype) for s in sums]
        self.scratch = [pltpu.SemaphoreType.DMA((n, 3)), pltpu.SemaphoreType.DMA((n, 3))]

    def _copies(self, ins, outs, scr):
        send_sems, recv_sems = scr
        _, _, cc, chips = _place()
        return [], [pltpu.make_async_remote_copy(src_ref=s_ref.at[2 * px + py], dst_ref=got_ref.at[j], send_sem=send_sems.at[i, j],
                                                 recv_sem=recv_sems.at[i, j], device_id=(px, py, cc), device_id_type=MESH)
                    for i, (s_ref, got_ref) in enumerate(zip(ins, outs)) for j, (px, py) in enumerate(chips)]


class _AllToAll(_Direct):
    def __init__(self, blocks):
        n = len(blocks)
        self.inputs = list(blocks)
        self.out_shape = [jax.ShapeDtypeStruct((N_DEV,) + b.shape, b.dtype) for b in blocks]
        self.scratch = [pltpu.SemaphoreType.DMA((n, 7)), pltpu.SemaphoreType.DMA((n, 7)), pltpu.SemaphoreType.DMA((n,))]

    def _copies(self, ins, outs, scr):
        send_sems, recv_sems, local_sems = scr
        x, y, cc, _ = _place()
        me = 4 * x + 2 * y + cc
        keep = [pltpu.make_async_copy(b_ref, got_ref.at[me], local_sems.at[i]) for i, (b_ref, got_ref) in enumerate(zip(ins, outs))]
        give = [pltpu.make_async_remote_copy(src_ref=b_ref, dst_ref=got_ref.at[me], send_sem=send_sems.at[i, k - 1],
                                             recv_sem=recv_sems.at[i, k - 1],
                                             device_id=(x ^ ((k >> 2) & 1), y ^ ((k >> 1) & 1), cc ^ (k & 1)), device_id_type=MESH)
                for i, (b_ref, got_ref) in enumerate(zip(ins, outs)) for k in range(1, N_DEV)]
        return keep, give


def _split_refs(refs, counts):
    out, pos = [], 0
    for n in counts:
        out.append(list(refs[pos:pos + n]))
        pos += n
    return out


def _bind(comms, c_in, c_out, c_scr):
    ins = _split_refs(c_in, [len(c.inputs) for c in comms])
    outs = _split_refs(c_out, [len(c.out_shape) for c in comms])
    scr = _split_refs(c_scr, [len(c.scratch) for c in comms])
    return [(c, (i, o, s)) for c, i, o, s in zip(comms, ins, outs, scr)]


def _call(body, *, name, grid, in_specs, out_specs, out_shape, scratch_shapes, args, comms=()):
    c_in = [a for c in comms for a in c.inputs]
    c_out = [s for c in comms for s in c.out_shape]
    c_scr = [s for c in comms for s in c.scratch]
    counts = [len(in_specs), len(c_in), len(out_shape), len(c_out), len(scratch_shapes), len(c_scr)]

    def full(*refs):
        ins, cin, outs, cout, scr, cscr = _split_refs(refs, counts)
        bound = _bind(comms, cin, cout, cscr)
        if comms:
            def at(steps):
                return functools.reduce(jnp.logical_and, [pl.program_id(ax) == s for ax, s in enumerate(steps)])

            first, last = at([0] * len(grid)), at([n - 1 for n in grid])

            @pl.when(first)
            def _():
                for c, r in bound:
                    c.begin(*r)

            @pl.when(at([(grid[0] - 1) // 2] + [0] * (len(grid) - 1)))
            def _():
                for c, r in bound:
                    c.mid(*r)

            @pl.when(last)
            def _():
                for c, r in bound:
                    c.relay(*r)

        body(*ins, *outs, *scr)
        if comms:
            @pl.when(last)
            def _():
                for c, r in bound:
                    c.end(*r)

    res = pl.pallas_call(
        full, name=name, grid=grid,
        in_specs=list(in_specs) + [ANY] * len(c_in), out_specs=list(out_specs) + [ANY] * len(c_out),
        out_shape=list(out_shape) + c_out, scratch_shapes=list(scratch_shapes) + c_scr,
        compiler_params=_cparams(("arbitrary",) * len(grid)),
    )(*args, *c_in)
    return list(res[:len(out_shape)]), list(res[len(out_shape):])


def _comm_only(comms, name):
    c_in = [a for c in comms for a in c.inputs]
    c_out = [s for c in comms for s in c.out_shape]
    c_scr = [s for c in comms for s in c.scratch]

    def full(*refs):
        cin, cout, cscr = _split_refs(refs, [len(c_in), len(c_out), len(c_scr)])
        bound = _bind(comms, cin, cout, cscr)
        for phase in ("begin", "mid", "relay", "end"):
            for c, r in bound:
                getattr(c, phase)(*r)

    return list(pl.pallas_call(full, name=name, in_specs=[ANY] * len(c_in), out_specs=[ANY] * len(c_out), out_shape=c_out,
                               scratch_shapes=c_scr)(*c_in))


def _my_index(*axes_and_weights):
    return sum(w * lax.axis_index(a) for a, w in axes_and_weights).astype(jnp.int32).reshape(1)


def _pair_sums(grads, gots, name):
    n = len(grads)

    def body(c_ref, *refs):
        for a_ref, b_ref, o_ref in zip(refs[:n], refs[n:2 * n], refs[2 * n:]):
            o_ref[...] = (a_ref[...].astype(F32) + b_ref[...].astype(F32)).astype(BF16)

    def tile(g):
        return pl.BlockSpec((1,) + g.shape[1:], lambda j, c_ref: (j, 0, 0))

    def mine(g, w):
        return pl.BlockSpec((1, None) + g.shape[1:], lambda j, c_ref: (4 * w + j, c_ref[0], 0, 0))

    return list(pl.pallas_call(
        body, name=name,
        grid_spec=pltpu.PrefetchScalarGridSpec(num_scalar_prefetch=1, grid=(4,),
                                               in_specs=[mine(g, w) for g, w in grads] + [tile(g) for g in gots],
                                               out_specs=[tile(g) for g in gots]),
        out_shape=[jax.ShapeDtypeStruct(g.shape, BF16) for g in gots],
        compiler_params=_cparams(("arbitrary",)))(_my_index(("c", 1)), *[g.reshape((-1, 2) + g.shape[1:]) for g, _ in grads], *gots))


def _ffn_fwd(x, gain, gw_gu, gw_d, off_d, name, comms=(), head=None):
    t = x.shape[0]
    n_head = 0 if head is None else 2

    def body(x_ref, g_ref, gu_ref, d_ref, *refs):
        head_in, (o_ref, ab_ref, h_ref), head_out = refs[:n_head], refs[n_head:n_head + 3], refs[n_head + 3:2 * n_head + 3]
        wg, wu, wd, s_scr, sems = refs[2 * n_head + 3:]

        @pl.when(pl.program_id(0) == 0)
        def _():
            cps = _load_rows(gu_ref, wg, 0, R_FF, sems.at[0]) + _load_rows(gu_ref, wu, R_FF, R_FF, sems.at[1]) \
                + _load_rows(d_ref, wd, off_d, R_FF, sems.at[2])
            for o in head_out:
                o[...] = jnp.zeros_like(o)
            for cp in cps:
                cp.wait()

        xv = x_ref[...]
        xn, _ = _rms(xv)
        h = (xn * g_ref[...]).astype(BF16)
        h_ref[...] = h
        for c in range(F // FC):
            rows = pl.ds(c * FC, FC)
            a = _nt(h, wg[rows, :])
            b = _nt(h, wu[rows, :])
            ab_ref[:, c * FC:(c + 1) * FC] = a.astype(BF16)
            ab_ref[:, F + c * FC:F + (c + 1) * FC] = b.astype(BF16)
            s_scr[:, c * FC:(c + 1) * FC] = (a * jax.nn.sigmoid(a) * b).astype(BF16)
        out = xv + 0.5 * _nn(s_scr[...], wd[...])
        if head is None:
            o_ref[...] = out
        else:
            (gf_ref, t_ref), (dg_ref, loss_ref) = head_in, head_out
            gain_f = gf_ref[...]
            yn, r = _rms(out)
            err = yn * gain_f - t_ref[...]
            loss_ref[...] += 0.5 * jnp.sum(jnp.mean(err * err, axis=-1, keepdims=True), axis=0, keepdims=True)
            o_ref[...], dg = _rms_bwd(err * (1.0 / D), yn, r, gain_f)
            dg_ref[...] += dg

    tile = pl.BlockSpec((TM, D), lambda i: (i, 0))
    row = pl.BlockSpec((1, D), lambda i: (0, 0))
    head_specs = [] if head is None else [row, pl.BlockSpec((1, 128), lambda i: (0, 0))]
    head_shapes = [] if head is None else [jax.ShapeDtypeStruct((1, D), F32), jax.ShapeDtypeStruct((1, 128), F32)]
    res, got = _call(
        body, name=name, grid=(t // TM,),
        in_specs=[tile, row, ANY, ANY] + ([] if head is None else [row, tile]),
        out_specs=[tile, pl.BlockSpec((TM, 2 * F), lambda i: (i, 0)), tile] + head_specs,
        out_shape=[jax.ShapeDtypeStruct((t, D), F32), jax.ShapeDtypeStruct((t, 2 * F), BF16), jax.ShapeDtypeStruct((t, D), BF16)] + head_shapes,
        scratch_shapes=[pltpu.VMEM((F, D), BF16)] * 3 + [pltpu.VMEM((TM, F), BF16), pltpu.SemaphoreType.DMA((3, N_DEV))],
        args=(x, gain, gw_gu, gw_d) + (() if head is None else tuple(head)), comms=comms)
    return (res[0], res[1], res[2], got, *res[3:])


def _ffn_bwd_hidden(ab, dout, gw_d, off_d, name, comms=()):
    t = ab.shape[0]
    n_steps = t // TM
    row_chunks = [(r, min(512, F - r)) for r in range(0, F, 512)]

    def tile(w):
        return pl.BlockSpec((TM, w), lambda i: (i, 0))

    def hidden(ab_ref, do_ref, d_ref, dab_ref, dwd_ref, wd, s_scr, acc, sems, out_sem):
        step = pl.program_id(0)

        @pl.when(step == 0)
        def _():
            cps = _load_rows(d_ref, wd, off_d, R_FF, sems)
            acc[...] = jnp.zeros_like(acc)
            for cp in cps:
                cp.wait()

        df = (0.5 * do_ref[...]).astype(BF16)
        for c in range(F // FC):
            a = ab_ref[:, c * FC:(c + 1) * FC].astype(F32)
            b = ab_ref[:, F + c * FC:F + (c + 1) * FC].astype(F32)
            sg = jax.nn.sigmoid(a)
            sl = a * sg
            ds = _nt(df, wd[pl.ds(c * FC, FC), :])
            dab_ref[:, c * FC:(c + 1) * FC] = (ds * b * (sg * (1.0 + a * (1.0 - sg)))).astype(BF16)
            dab_ref[:, F + c * FC:F + (c + 1) * FC] = (ds * sl).astype(BF16)
            s_scr[:, c * FC:(c + 1) * FC] = (sl * b).astype(BF16)
        for r, n in row_chunks:
            acc[r:r + n, :] += _tn(s_scr[:, r:r + n], df)

        @pl.when(step == n_steps - 1)
        def _():
            wd[...] = acc[...].astype(BF16)
            out = pltpu.make_async_copy(wd, dwd_ref, out_sem)
            out.start()
            out.wait()

    (dab, dwd), got = _call(
        hidden, name=name, grid=(n_steps,),
        in_specs=[tile(2 * F), tile(D), ANY], out_specs=[tile(2 * F), ANY],
        out_shape=[jax.ShapeDtypeStruct((t, 2 * F), BF16), jax.ShapeDtypeStruct((F, D), BF16)],
        scratch_shapes=[pltpu.VMEM((F, D), BF16), pltpu.VMEM((TM, F), BF16), pltpu.VMEM((F, D), F32), pltpu.SemaphoreType.DMA((N_DEV,)),
                        pltpu.SemaphoreType.DMA],
        args=(ab, dout, gw_d), comms=comms)
    return (dab, dwd.reshape(N_DEV, R_FF, D)), got


def _ffn_bwd_input(x, gain, dab, dout, gw_gu, name, comms=()):
    t = x.shape[0]

    def body(x_ref, g_ref, dab_ref, do_ref, gu_ref, dx_ref, dg_ref, wgu, sems):
        @pl.when(pl.program_id(0) == 0)
        def _():
            cps = _load_rows(gu_ref, wgu.at[0:F], 0, R_FF, sems.at[0]) + _load_rows(gu_ref, wgu.at[F:2 * F], R_FF, R_FF, sems.at[1])
            dg_ref[...] = jnp.zeros_like(dg_ref)
            for cp in cps:
                cp.wait()

        gain_v = g_ref[...]
        xn, r = _rms(x_ref[...])
        dh = _nn(dab_ref[...], wgu[...])
        dxn, dg = _rms_bwd(dh, xn, r, gain_v)
        dg_ref[...] += dg
        dx_ref[...] = do_ref[...] + dxn

    def tile(w):
        return pl.BlockSpec((TM, w), lambda i: (i, 0))

    row = pl.BlockSpec((1, D), lambda i: (0, 0))
    return _call(
        body, name=name, grid=(t // TM,),
        in_specs=[tile(D), row, tile(2 * F), tile(D), ANY], out_specs=[tile(D), row],
        out_shape=[jax.ShapeDtypeStruct((t, D), F32), jax.ShapeDtypeStruct((1, D), F32)],
        scratch_shapes=[pltpu.VMEM((2 * F, D), BF16), pltpu.SemaphoreType.DMA((2, N_DEV))],
        args=(x, gain, dab, dout, gw_gu), comms=comms)


def _weight_grad(a, b, name, col_off=0, m=None, comms=None, mats=1):
    t = a.shape[0]
    m = a.shape[1] if m is None else m
    n = b.shape[1]
    tmm = 512 if m % 512 == 0 else 256
    first = col_off // tmm

    def body(a_ref, b_ref, o_ref):
        o_ref[...] = _tn(a_ref[...], b_ref[...]).astype(BF16)

    (out,), got = _call(
        body, name=name, grid=(m // tmm,),
        in_specs=[pl.BlockSpec((t, tmm), lambda i: (0, first + i)), pl.BlockSpec((t, n), lambda i: (0, 0))],
        out_specs=[pl.BlockSpec((tmm, n), lambda i: (i, 0))],
        out_shape=[jax.ShapeDtypeStruct((m, n), BF16)], scratch_shapes=[], args=(a, b), comms=comms or ())
    out = out.reshape(mats * N_DEV, m // (mats * N_DEV), n)
    return out if comms is None else (out, got)


def _mix_proj_fwd(x, gain, b_gate, gw, comms=()):
    t = x.shape[0]

    def body(x_ref, g_ref, bg_ref, gw_ref, q_ref, k_ref, v_ref, zs_ref, gt_ref, h_ref, win, sems):
        @pl.when(pl.program_id(0) == 0)
        def _():
            for cp in _load_rows(gw_ref, win, OFF_IN, R_IN, sems):
                cp.wait()

        xn, _ = _rms(x_ref[...])
        h = (xn * g_ref[...]).astype(BF16)
        h_ref[...] = h
        z = _nt(h, win[...])
        q_ref[...] = (z[:, 0:D_ATT] * 0.125).astype(BF16)
        k_ref[...] = z[:, D_ATT:2 * D_ATT].astype(BF16)
        v_ref[...] = z[:, 2 * D_ATT:3 * D_ATT].astype(BF16)
        zs_ref[...] = z[:, 3 * D_ATT:3 * D_ATT + 2 * D_SGU].astype(BF16)
        gt_ref[...] = jax.nn.sigmoid(z[:, 3 * D_ATT + 2 * D_SGU:] + bg_ref[...]).astype(BF16)

    def tile(w):
        return pl.BlockSpec((TM, w), lambda i: (i, 0))

    return _call(
        body, name="mix_proj_fwd", grid=(t // TM,),
        in_specs=[tile(D), pl.BlockSpec((1, D), lambda i: (0, 0)), pl.BlockSpec((1, 2 * D), lambda i: (0, 0)), ANY],
        out_specs=[tile(D_ATT), tile(D_ATT), tile(D_ATT), tile(2 * D_SGU), tile(2 * D), tile(D)],
        out_shape=[jax.ShapeDtypeStruct((t, D_ATT), BF16)] * 3 + [jax.ShapeDtypeStruct((t, 2 * D_SGU), BF16),
                                                                   jax.ShapeDtypeStruct((t, 2 * D), BF16),
                                                                   jax.ShapeDtypeStruct((t, D), BF16)],
        scratch_shapes=[pltpu.VMEM((D_IN, D), BF16), pltpu.SemaphoreType.DMA((N_DEV,))],
        args=(x, gain, b_gate, gw), comms=comms)


def _mix_proj_bwd(dz, x, gain, dres, gw, comms=()):
    t = x.shape[0]

    def body(dz_ref, x_ref, g_ref, dr_ref, gw_ref, dx_ref, dg_ref, win, sems):
        @pl.when(pl.program_id(0) == 0)
        def _():
            cps = _load_rows(gw_ref, win, OFF_IN, R_IN, sems)
            dg_ref[...] = jnp.zeros_like(dg_ref)
            for cp in cps:
                cp.wait()

        dh = _nn(dz_ref[...], win[...])
        xn, r = _rms(x_ref[...])
        dxn, dg = _rms_bwd(dh, xn, r, g_ref[...])
        dg_ref[...] += dg
        dx_ref[...] = dr_ref[...] + dxn

    def tile(w):
        return pl.BlockSpec((TM, w), lambda i: (i, 0))

    row = pl.BlockSpec((1, D), lambda i: (0, 0))
    return _call(
        body, name="mix_proj_bwd", grid=(t // TM,),
        in_specs=[tile(D_IN), tile(D), row, tile(D), ANY], out_specs=[tile(D), row],
        out_shape=[jax.ShapeDtypeStruct((t, D), F32), jax.ShapeDtypeStruct((1, D), F32)],
        scratch_shapes=[pltpu.VMEM((D_IN, D), BF16), pltpu.SemaphoreType.DMA((N_DEV,))],
        args=(dz, x, gain, dres, gw), comms=comms)


SKEW_W = KW + QB
N_CAP = 2 * QB - REL_CLIP + 1


def _band_bias(rel_bias):
    cap = rel_bias[:, 2 * REL_CLIP:]
    diag = jnp.concatenate([jnp.broadcast_to(cap, (HEADS, N_CAP)), rel_bias[:, 2 * REL_CLIP - 1::-1],
                            jnp.broadcast_to(cap, (HEADS, SKEW_W - N_CAP - 2 * REL_CLIP))], axis=1)

    def body(d_ref, o_ref):
        lag = lax.broadcasted_iota(jnp.int32, (QB, KW), 1) // CHUNK - lax.broadcasted_iota(jnp.int32, (QB, KW), 0) // CHUNK
        band = (lag >= 0) & (lag <= N_LEFT)
        for h in range(HEADS):
            rows = jnp.broadcast_to(d_ref[h:h + 1, :], (QB, SKEW_W))
            o_ref[h] = jnp.where(band, pltpu.roll(rows, 0, 1, stride=1, stride_axis=0)[:, :KW], NEG_INF)

    return pl.pallas_call(body, name="band_bias", out_shape=jax.ShapeDtypeStruct((HEADS, QB, KW), F32))(diag)


def _att_specs():
    qspec = pl.BlockSpec((QB, D_ATT), lambda g: (g, 0))
    kspecs = [pl.BlockSpec((QB, D_ATT), lambda g: (jnp.maximum(g - 2, 0), 0)),
              pl.BlockSpec((QB, D_ATT), lambda g: (jnp.maximum(g - 1, 0), 0)), qspec]
    bspec = pl.BlockSpec((HEADS, QB, KW), lambda g: (0, 0, 0))
    return qspec, kspecs, bspec


def _att_probs(qm, kp, bias, valid):
    s = jnp.where(valid, _nt(qm, kp) + bias, NEG_INF)
    e = jnp.exp(s - jnp.max(s, axis=-1, keepdims=True))
    return e / jnp.sum(e, axis=-1, keepdims=True)


def _att_valid():
    g = pl.program_id(0)
    blk = lax.broadcasted_iota(jnp.int32, (QB, KW), 1) // QB
    return (blk + g) >= 2


def _att_fwd(q, k, v, bias, comms=()):
    t = q.shape[0]

    def body(q_ref, k0, k1, k2, v0, v1, v2, b_ref, y_ref):
        valid = _att_valid()
        first = lax.broadcasted_iota(jnp.int32, (1, 128), 1) < 64
        for p in range(HEADS // 2):
            lanes = slice(p * 128, (p + 1) * 128)
            qp = q_ref[:, lanes]
            kp = jnp.concatenate([k0[:, lanes], k1[:, lanes], k2[:, lanes]], axis=0)
            vp = jnp.concatenate([v0[:, lanes], v1[:, lanes], v2[:, lanes]], axis=0)
            out = jnp.zeros((QB, 128), F32)
            for hh in range(2):
                mask = first if hh == 0 else jnp.logical_not(first)
                pr = _att_probs(jnp.where(mask, qp, 0), kp, b_ref[2 * p + hh], valid)
                out = out + _nn(pr.astype(BF16), jnp.where(mask, vp, 0))
            y_ref[:, lanes] = out.astype(BF16)

    qspec, kspecs, bspec = _att_specs()
    (out,), got = _call(
        body, name="att_fwd", grid=(t // QB,),
        in_specs=[qspec] + kspecs + kspecs + [bspec], out_specs=[qspec],
        out_shape=[jax.ShapeDtypeStruct((t, D_ATT), BF16)], scratch_shapes=[],
        args=(q, k, k, k, v, v, v, bias), comms=comms)
    return out, got


def _att_bwd(q, k, v, bias, dy, comms=()):
    t = q.shape[0]
    n_blocks = t // QB

    def body(q_ref, k0, k1, k2, v0, v1, v2, b_ref, dy_ref, dq_ref, dk_ref, dv_ref, db_ref, dk_acc, dv_acc):
        g = pl.program_id(0)

        @pl.when(g == 0)
        def _():
            db_ref[...] = jnp.zeros_like(db_ref)
            dk_acc[...] = jnp.zeros_like(dk_acc)
            dv_acc[...] = jnp.zeros_like(dv_acc)

        valid = _att_valid()
        first = lax.broadcasted_iota(jnp.int32, (1, 128), 1) < 64
        for p in range(HEADS // 2):
            lanes = slice(p * 128, (p + 1) * 128)
            qp = q_ref[:, lanes]
            dyp = dy_ref[:, lanes]
            kp = jnp.concatenate([k0[:, lanes], k1[:, lanes], k2[:, lanes]], axis=0)
            vp = jnp.concatenate([v0[:, lanes], v1[:, lanes], v2[:, lanes]], axis=0)
            dq = jnp.zeros((QB, 128), F32)
            dk = jnp.zeros((KW, 128), F32)
            dv = jnp.zeros((KW, 128), F32)
            for hh in range(2):
                mask = first if hh == 0 else jnp.logical_not(first)
                qm = jnp.where(mask, qp, 0)
                dym = jnp.where(mask, dyp, 0)
                pr = _att_probs(qm, kp, b_ref[2 * p + hh], valid)
                dp = _nt(dym, vp)
                ds = pr * (dp - jnp.sum(dp * pr, axis=-1, keepdims=True))
                db_ref[2 * p + hh] += ds
                dsb = ds.astype(BF16)
                dq = dq + _nn(dsb, jnp.where(mask, kp, 0))
                dk = dk + _tn(dsb, qm)
                dv = dv + _tn(pr.astype(BF16), dym)
            dq_ref[:, lanes] = (dq * 0.125).astype(BF16)
            for j in range(3):
                rows = pl.ds(pl.multiple_of(jnp.maximum(g - 2 + j, 0) * QB, QB), QB)
                dk_acc[rows, lanes] += dk[j * QB:(j + 1) * QB]
                dv_acc[rows, lanes] += dv[j * QB:(j + 1) * QB]

        @pl.when(g == n_blocks - 1)
        def _():
            dk_ref[...] = dk_acc[...].astype(BF16)
            dv_ref[...] = dv_acc[...].astype(BF16)

    qspec, kspecs, bspec = _att_specs()
    full = pl.BlockSpec((t, D_ATT), lambda g: (0, 0))
    return _call(
        body, name="att_bwd", grid=(n_blocks,),
        in_specs=[qspec] + kspecs + kspecs + [bspec, qspec], out_specs=[qspec, full, full, bspec],
        out_shape=[jax.ShapeDtypeStruct((t, D_ATT), BF16)] * 3 + [jax.ShapeDtypeStruct((HEADS, QB, KW), F32)],
        scratch_shapes=[pltpu.VMEM((t, D_ATT), F32)] * 2,
        args=(q, k, k, k, v, v, v, bias, dy), comms=comms)


def _rel_bias_grad(dbias, comms=()):
    c_in = [a for c in comms for a in c.inputs]
    c_out = [s for c in comms for s in c.out_shape]
    c_scr = [s for c in comms for s in c.scratch]

    def body(db_ref, *refs):
        cin, (cs_ref, tot_ref), cout, cscr = _split_refs(refs, [len(c_in), 2, len(c_out), len(c_scr)])
        bound = _bind(comms, cin, cout, cscr)
        for c, r in bound:
            c.begin(*r)
        lane = lax.broadcasted_iota(jnp.int32, (1, SKEW_W), 1)
        capped = (lane < N_CAP) | (lane > KW)
        pad = jnp.zeros((8, QB), F32)
        for h in range(HEADS):
            z = jnp.concatenate([db_ref[h, 0:8, :], pad], axis=1)
            for a in range(1, QB // 8):
                z = z + pltpu.roll(jnp.concatenate([db_ref[h, 8 * a:8 * a + 8, :], pad], axis=1), SKEW_W - 8 * a, 1)
            cs = z[0:1, :]
            for b in range(1, 8):
                cs = cs + pltpu.roll(z[b:b + 1, :], SKEW_W - b, 1)
            cs_ref[h:h + 1, :] = cs
            tot_ref[h:h + 1, :] = jnp.broadcast_to(jnp.sum(jnp.where(capped, cs, 0.0), axis=1, keepdims=True), (1, 128))
        for phase in ("mid", "relay", "end"):
            for c, r in bound:
                getattr(c, phase)(*r)

    whole = pl.BlockSpec(memory_space=pltpu.VMEM)
    cs, tot, *got = pl.pallas_call(
        body, name="rel_bias_grad", in_specs=[whole] + [ANY] * len(c_in), out_specs=[whole, whole] + [ANY] * len(c_out),
        out_shape=[jax.ShapeDtypeStruct((HEADS, SKEW_W), F32), jax.ShapeDtypeStruct((HEADS, 128), F32)] + c_out, scratch_shapes=c_scr,
    )(dbias, *c_in)
    return jnp.concatenate([cs[:, KW:N_CAP - 1:-1], tot[:, :1]], axis=1), got


def _sgu_mask():
    pos = np.arange(SGU_BLOCK)
    return (pos[:, None] // CHUNK) >= (pos[None, :] // CHUNK)


def _group_stack(blk, first):
    return jnp.concatenate([jnp.where(first, blk, 0), jnp.where(first, 0, blk)], axis=0)


def _sgu_norm(zs_ref, lng, lnb):
    zs = zs_ref[...].astype(F32)
    ga, th = _gelu(zs)
    u, vs = ga[:, :D_SGU], ga[:, D_SGU:]
    mu = jnp.mean(vs, axis=-1, keepdims=True)
    cen = vs - mu
    rstd = lax.rsqrt(jnp.mean(cen * cen, axis=-1, keepdims=True) + EPS)
    xhat = cen * rstd
    return zs, th, u, xhat, rstd, xhat * lng + lnb


def _sgu_mix(vb, wm2_ref, bsx, s_ref):
    first = lax.broadcasted_iota(jnp.int32, (1, 128), 1) < 64
    for n in range(TM // SGU_BLOCK):
        for p in range(4):
            blk = vb[n * 128:(n + 1) * 128, p * 128:(p + 1) * 128]
            s_ref[n * 128:(n + 1) * 128, p * 128:(p + 1) * 128] = _nn(wm2_ref[p], _group_stack(blk, first)) + bsx[:, p * 128:(p + 1) * 128]


def _merge_fwd(x, zs, gt, y_att, lng, lnb, wm2, bsx, gw):
    t = x.shape[0]

    def body(x_ref, zs_ref, gt_ref, ya_ref, lng_ref, lnb_ref, wm2_ref, bsx_ref, gw_ref, xo_ref, ys_ref, mg_ref,
             wbr, wo, s_scr, sems):
        @pl.when(pl.program_id(0) == 0)
        def _():
            for cp in _load_rows(gw_ref, wbr, OFF_BR, R_BR, sems.at[0]) + _load_rows(gw_ref, wo, OFF_WO, R_WO, sems.at[1]):
                cp.wait()

        _, _, u, _, _, vsn = _sgu_norm(zs_ref, lng_ref[...], lnb_ref[...])
        _sgu_mix(vsn.astype(BF16), wm2_ref, bsx_ref[...], s_scr)
        ys = (u * s_scr[...]).astype(BF16)
        ys_ref[...] = ys
        pa = _nt(ya_ref[...], wbr[:, :D_ATT])
        ps = _nt(ys, wbr[:, D_ATT:])
        mg = (gt_ref[:, :D].astype(F32) * pa + gt_ref[:, D:].astype(F32) * ps).astype(BF16)
        mg_ref[...] = mg
        xo_ref[...] = x_ref[...] + _nn(mg, wo[...])

    def tile(w):
        return pl.BlockSpec((TM, w), lambda i: (i, 0))

    def const(shape):
        return pl.BlockSpec(shape, lambda i: (0,) * len(shape))

    return pl.pallas_call(
        body, name="merge_fwd", grid=(t // TM,),
        in_specs=[tile(D), tile(2 * D_SGU), tile(2 * D), tile(D_ATT), const((1, D_SGU)), const((1, D_SGU)),
                  const((4, 128, 256)), const((128, D_SGU)), ANY],
        out_specs=[tile(D), tile(D_SGU), tile(D)],
        out_shape=[jax.ShapeDtypeStruct((t, D), F32), jax.ShapeDtypeStruct((t, D_SGU), BF16), jax.ShapeDtypeStruct((t, D), BF16)],
        scratch_shapes=[pltpu.VMEM((D, D), BF16), pltpu.VMEM((D, D), BF16), pltpu.VMEM((TM, D_SGU), F32),
                        pltpu.SemaphoreType.DMA((2, N_DEV))],
        compiler_params=_cparams(("arbitrary",)),
    )(x, zs, gt, y_att, lng, lnb, wm2, bsx, gw)


def _merge_bwd(dx, gt, y_att, y_sgu, merged, gw):
    t = dx.shape[0]
    n_steps = t // TM

    def body(dx_ref, gt_ref, ya_ref, ys_ref, mg_ref, gw_ref, dzg_ref, dya_ref, dys_ref, dbg_ref, ga_ref, gs_ref, go_ref,
             wbr, wo, acc_a, acc_s, acc_o, sems):
        step = pl.program_id(0)

        @pl.when(step == 0)
        def _():
            cps = _load_rows(gw_ref, wbr, OFF_BR, R_BR, sems.at[0]) + _load_rows(gw_ref, wo, OFF_WO, R_WO, sems.at[1])
            dbg_ref[...] = jnp.zeros_like(dbg_ref)
            for acc in (acc_a, acc_s, acc_o):
                acc[...] = jnp.zeros_like(acc)
            for cp in cps:
                cp.wait()

        dxb = dx_ref[...].astype(BF16)
        acc_o[...] += _tn(mg_ref[...], dxb)
        dm = _nt(dxb, wo[...])
        for half, y_ref, w, acc in ((0, ya_ref, wbr.at[:, :D_ATT], acc_a), (1, ys_ref, wbr.at[:, D_ATT:], acc_s)):
            cols = slice(half * D, (half + 1) * D)
            gate = gt_ref[:, cols].astype(F32)
            branch = _nt(y_ref[...], w[...])
            dzg = dm * branch * gate * (1.0 - gate)
            dbg_ref[:, cols] += jnp.sum(dzg, axis=0, keepdims=True)
            dzg_ref[:, cols] = dzg.astype(BF16)
            dbr = (dm * gate).astype(BF16)
            acc[...] += _tn(dbr, y_ref[...])
            dy = _nn(dbr, w[...])
            if half == 0:
                dya_ref[...] = dy.astype(BF16)
            else:
                dys_ref[...] = dy

        @pl.when(step == n_steps - 1)
        def _():
            ga_ref[...] = acc_a[...].astype(BF16)
            gs_ref[...] = acc_s[...].astype(BF16)
            go_ref[...] = acc_o[...].astype(BF16)

    def tile(w):
        return pl.BlockSpec((TM, w), lambda i: (i, 0))

    def whole(r, c):
        return pl.BlockSpec((r, c), lambda i: (0, 0))

    dzg, dya, dys, dbg, g_ba, g_bs, g_wo = pl.pallas_call(
        body, name="merge_bwd", grid=(n_steps,),
        in_specs=[tile(D), tile(2 * D), tile(D_ATT), tile(D_SGU), tile(D), ANY],
        out_specs=[tile(2 * D), tile(D_ATT), tile(D_SGU), whole(1, 2 * D), whole(D, D_ATT), whole(D, D_SGU), whole(D, D)],
        out_shape=[jax.ShapeDtypeStruct((t, 2 * D), BF16), jax.ShapeDtypeStruct((t, D_ATT), BF16), jax.ShapeDtypeStruct((t, D_SGU), F32),
                   jax.ShapeDtypeStruct((1, 2 * D), F32), jax.ShapeDtypeStruct((D, D_ATT), BF16), jax.ShapeDtypeStruct((D, D_SGU), BF16),
                   jax.ShapeDtypeStruct((D, D), BF16)],
        scratch_shapes=[pltpu.VMEM((D, D), BF16), pltpu.VMEM((D, D), BF16), pltpu.VMEM((D, D_ATT), F32), pltpu.VMEM((D, D_SGU), F32),
                        pltpu.VMEM((D, D), F32), pltpu.SemaphoreType.DMA((2, N_DEV))],
        compiler_params=_cparams(("arbitrary",)),
    )(dx, gt, y_att, y_sgu, merged, gw)
    return dzg, dya, dys, dbg, g_ba.reshape(N_DEV, R_BR, D_ATT), g_bs.reshape(N_DEV, R_BR, D_SGU), g_wo.reshape(N_DEV, R_WO, D)


def _sgu_bwd(zs, dys, lng, lnb, wm2, wmt2, bsx, comms=()):
    t = zs.shape[0]
    n_steps = t // TM

    def body(zs_ref, dys_ref, lng_ref, lnb_ref, wm2_ref, wmt2_ref, bsx_ref, dzs_ref, dw_ref, dbs_ref, dlg_ref, dlb_ref,
             s_scr, dv_scr, ds_acc):
        i = pl.program_id(0)

        @pl.when(i == 0)
        def _():
            dw_ref[...] = jnp.zeros_like(dw_ref)
            dlg_ref[...] = jnp.zeros_like(dlg_ref)
            dlb_ref[...] = jnp.zeros_like(dlb_ref)
            ds_acc[...] = jnp.zeros_like(ds_acc)

        lng = lng_ref[...]
        zs, th, u, xhat, rstd, vsn = _sgu_norm(zs_ref, lng, lnb_ref[...])
        vb = vsn.astype(BF16)
        _sgu_mix(vb, wm2_ref, bsx_ref[...], s_scr)
        dys = dys_ref[...]
        du = dys * s_scr[...]
        ds = dys * u
        dsb = ds.astype(BF16)
        first = lax.broadcasted_iota(jnp.int32, (1, 128), 1) < 64
        acc = jnp.zeros((SGU_BLOCK, D_SGU), F32)
        for n in range(TM // SGU_BLOCK):
            rows = slice(n * 128, (n + 1) * 128)
            acc = acc + ds[rows]
            for p in range(4):
                lanes = slice(p * 128, (p + 1) * 128)
                stack = _group_stack(dsb[rows, lanes], first)
                dv_scr[rows, lanes] = _nn(wmt2_ref[p], stack)
                dw_ref[p] += _nt(stack, vb[rows, lanes])
        ds_acc[...] += acc
        dvsn = dv_scr[...]
        dlg_ref[...] += jnp.sum(dvsn * xhat, axis=0, keepdims=True)
        dlb_ref[...] += jnp.sum(dvsn, axis=0, keepdims=True)
        dxh = dvsn * lng
        dvs = rstd * (dxh - jnp.mean(dxh, axis=-1, keepdims=True) - xhat * jnp.mean(dxh * xhat, axis=-1, keepdims=True))
        dga = jnp.concatenate([du, dvs], axis=1)
        dzs_ref[...] = (dga * _gelu_grad(zs, th)).astype(BF16)

        @pl.when(i == n_steps - 1)
        def _():
            r = lax.broadcasted_iota(jnp.int32, (256, 128), 0) % SGU_BLOCK
            c = lax.broadcasted_iota(jnp.int32, (256, 128), 1)
            keep = (r // CHUNK) >= (c // CHUNK)
            for p in range(4):
                dw_ref[p] = jnp.where(keep, dw_ref[p], 0.0)
            total = ds_acc[...]
            grp = lax.broadcasted_iota(jnp.int32, (SGU_BLOCK, D_SGU), 1) // 64
            lane = lax.broadcasted_iota(jnp.int32, (SGU_BLOCK, 128), 1)
            out = jnp.zeros((SGU_BLOCK, 128), F32)
            for gi in range(8):
                out = jnp.where(lane == gi, jnp.sum(jnp.where(grp == gi, total, 0.0), axis=1, keepdims=True), out)
            dbs_ref[...] = out

    def tile(w):
        return pl.BlockSpec((TM, w), lambda i: (i, 0))

    def const(shape):
        return pl.BlockSpec(shape, lambda i: (0,) * len(shape))

    return _call(
        body, name="sgu_bwd", grid=(n_steps,),
        in_specs=[tile(2 * D_SGU), tile(D_SGU), const((1, D_SGU)), const((1, D_SGU)), const((4, 128, 256)), const((4, 128, 256)),
                  const((128, D_SGU))],
        out_specs=[tile(2 * D_SGU), const((4, 256, 128)), const((128, 128)), const((1, D_SGU)), const((1, D_SGU))],
        out_shape=[jax.ShapeDtypeStruct((t, 2 * D_SGU), BF16), jax.ShapeDtypeStruct((4, 256, 128), F32),
                   jax.ShapeDtypeStruct((128, 128), F32), jax.ShapeDtypeStruct((1, D_SGU), F32), jax.ShapeDtypeStruct((1, D_SGU), F32)],
        scratch_shapes=[pltpu.VMEM((TM, D_SGU), F32), pltpu.VMEM((TM, D_SGU), F32), pltpu.VMEM((SGU_BLOCK, D_SGU), F32)],
        args=(zs, dys, lng, lnb, wm2, wmt2, bsx), comms=comms)


def _adamw(g, w, m, v):
    m = ADAM_B1 * m + (1.0 - ADAM_B1) * g
    v = ADAM_B2 * v + (1.0 - ADAM_B2) * (g * g)
    m_hat = m / (1.0 - ADAM_B1 ** ADAM_STEP)
    v_hat = v / (1.0 - ADAM_B2 ** ADAM_STEP)
    return -ADAM_LR * (m_hat / (jnp.sqrt(v_hat) + ADAM_EPS) + ADAM_WD * w), m, v


def _adamw_matrices(items, name):
    n = len(items)
    c = items[0][0].shape[2]
    tc = 256

    def body(own_ref, *refs):
        for i, (_, _, _, _, _, transposed) in enumerate(items):
            p_ref, s_ref, w_ref, m_ref, v_ref = refs[5 * i:5 * i + 5]
            g = p_ref[0].astype(F32) + p_ref[1].astype(F32) + p_ref[2].astype(F32) + s_ref[...].astype(F32)
            g = g.T if transposed else g
            res = (g,) + _adamw(g, w_ref[...], m_ref[...], v_ref[...])
            for o_ref, val in zip(refs[5 * n + 4 * i:5 * n + 4 * i + 4], res):
                o_ref[...] = val

    in_specs, out_specs, out_shape, args = [], [], [], []
    for parts, sums, w, m, v, transposed in items:
        r = parts.shape[1]
        own = pl.BlockSpec((None, tc, r), lambda i, o: (0, i, 0)) if transposed else pl.BlockSpec((None, r, tc), lambda i, o: (0, 0, i))
        in_specs += [pl.BlockSpec((3, r, tc), lambda i, o: (0, 0, i)), pl.BlockSpec((None, r, tc), lambda i, o: (o[0], 0, i)), own, own, own]
        out_specs += [own] * 4
        out_shape += [jax.ShapeDtypeStruct(w.shape, F32)] * 4
        args += [parts, sums, w, m, v]
    res = pl.pallas_call(
        body, name=name,
        grid_spec=pltpu.PrefetchScalarGridSpec(num_scalar_prefetch=1, grid=(c // tc,), in_specs=in_specs, out_specs=out_specs),
        out_shape=out_shape, compiler_params=_cparams(("arbitrary",)),
    )(_my_index(("x", 2), ("y", 1)), *args)
    return [list(res[4 * i:4 * i + 4]) for i in range(n)]


_SMALL_2D = {"norm_ffn1": (1, D), "norm_mix": (1, D), "norm_ffn2": (1, D), "norm_final": (1, D), "b_gate": (1, 2 * D),
             "sgu_ln_g": (1, D_SGU), "sgu_ln_b": (1, D_SGU), "sgu_b_s": (8, SGU_BLOCK), "rel_bias": (HEADS, N_REL),
             "sgu_w_s": (8 * SGU_BLOCK, SGU_BLOCK)}


def _adamw_small(parts, loss_parts, p):
    names = list(parts)
    n = len(names)

    def body(*refs):
        got, loss_got, wmv, outs, loss_out = refs[:n], refs[n], refs[n + 1:4 * n + 1], refs[4 * n + 1:8 * n + 1], refs[8 * n + 1]
        for i, name in enumerate(names):
            g = got[i][0]
            for k in range(1, N_DEV):
                g = g + got[i][k]
            if name == "sgu_b_s":
                g = g.T[0:8, :]
            res = (g,) + _adamw(g, wmv[3 * i][...], wmv[3 * i + 1][...], wmv[3 * i + 2][...])
            for o_ref, val in zip(outs[4 * i:4 * i + 4], res):
                o_ref[...] = val
        total = loss_got[0]
        for k in range(1, N_DEV):
            total = total + loss_got[k]
        loss_out[...] = total

    wmv = [p[pre + name].reshape(_SMALL_2D[name]) for name in names for pre in ("", "m_", "v_")]
    res = pl.pallas_call(
        body, name="adamw_small",
        out_shape=[jax.ShapeDtypeStruct(_SMALL_2D[name], F32) for name in names for _ in range(4)] + [jax.ShapeDtypeStruct((1, 128), F32)],
        compiler_params=_cparams())(*[parts[name] for name in names], loss_parts, *wmv)
    return [{name: res[4 * i + j].reshape(p[name].shape) for i, name in enumerate(names)} for j in range(4)], res[-1]


def _pack_rows(slabs, name, comms=()):
    flat = [a for groups in slabs for grp in groups for a, _ in grp]
    rows = [[grp[0][0].shape[2] if grp[0][1] else grp[0][0].shape[1] for grp in groups] for groups in slabs]
    c_in = [a for c in comms for a in c.inputs]
    c_out = [s for c in comms for s in c.out_shape]
    c_scr = [s for c in comms for s in c.scratch]

    def body(*refs):
        ins, cin, outs, cout, cscr = _split_refs(refs, [len(flat), len(c_in), len(slabs), len(c_out), len(c_scr)])
        bound = _bind(comms, cin, cout, cscr)
        for c, r in bound:
            c.begin(*r)
        pos = 0
        for groups, slab_rows, o_ref in zip(slabs, rows, outs):
            off = 0
            for grp, r in zip(groups, slab_rows):
                vals = []
                for _, transposed in grp:
                    val = ins[pos][0]
                    vals.append(val.T if transposed else val)
                    pos += 1
                o_ref[off:off + r, :] = (vals[0] if len(vals) == 1 else jnp.concatenate(vals, axis=1)).astype(BF16)
                off += r
        for phase in ("mid", "relay", "end"):
            for c, r in bound:
                getattr(c, phase)(*r)

    whole = pl.BlockSpec(memory_space=pltpu.VMEM)
    res = pl.pallas_call(
        body, name=name, in_specs=[whole] * len(flat) + [ANY] * len(c_in), out_specs=[whole] * len(slabs) + [ANY] * len(c_out),
        out_shape=[jax.ShapeDtypeStruct((sum(r), D), BF16) for r in rows] + c_out, scratch_shapes=c_scr,
        compiler_params=_cparams())(*flat, *c_in)
    return list(res[:len(slabs)]), list(res[len(slabs):])


def _step(x, target, p):
    n1, nm, n2 = p["norm_ffn1"], p["norm_mix"], p["norm_ffn2"]
    nf = p["norm_final"].reshape(1, D)
    lng, lnb = p["sgu_ln_g"], p["sgu_ln_b"]
    w_m = jnp.where(jnp.asarray(_sgu_mask())[None], p["sgu_w_s"][0], 0.0).astype(BF16)
    wm2 = jnp.concatenate([w_m[0::2], w_m[1::2]], axis=2)
    w_mt = w_m.transpose(0, 2, 1)
    wmt2 = jnp.concatenate([w_mt[0::2], w_mt[1::2]], axis=2)
    bsx = jnp.repeat(p["sgu_b_s"][0].T, 64, axis=1)
    bias = _band_bias(p["rel_bias"][0])

    def chip_sums(grads, name):
        gots = _comm_only([_SiblingSwap(grads)], "swap_" + name)
        return _pair_sums(grads, gots, "pair_sums_" + name)
    def as_rows(a):
        return jnp.swapaxes(a, 1, 2)

    def updates(parts, sums, names):
        res = {}
        for cols in sorted({pt.shape[2] for pt in parts}):
            items, group = [], [(pt, sm, n) for pt, sm, n in zip(parts, sums, names) if pt.shape[2] == cols]
            for pt, sm, n in group:
                view = as_rows if p[n].shape[1:] != pt.shape[1:] and p[n].shape[2] > 128 else (lambda a: a)
                transposed = p[n].shape[1:] != pt.shape[1:] and p[n].shape[2] <= 128
                items.append((pt, sm, view(p[n]), view(p["m_" + n]), view(p["v_" + n]), transposed))
            for (pt, sm, n), four in zip(group, _adamw_matrices(items, "adamw_" + group[0][2])):
                res[n] = [as_rows(o) if o.shape != p[n].shape else o for o in four]
        return res

    (rows1,), _ = _pack_rows([[[(as_rows(p["ffn1_w_gate"]), False)], [(as_rows(p["ffn1_w_up"]), False)], [(p["ffn1_w_down"], False)]]],
                             "pack_ffn1")
    (rows_m, rows2d, rows2gu), (gw1,) = _pack_rows(
        [[[(as_rows(p["w_in"]), False)], [(p["w_branch_att"], True), (p["w_branch_sgu"], True)], [(p["w_out"], False)]],
         [[(p["ffn2_w_down"], False)]],
         [[(as_rows(p["ffn2_w_gate"]), False)], [(as_rows(p["ffn2_w_up"]), False)]]],
        "gather_ffn1_pack_rest", [_Gather(rows1)])
    x1, ab1, h1, (gwm,) = _ffn_fwd(x, n1, gw1, gw1, 2 * R_FF, "ffn1_fwd", [_Gather(rows_m)])
    (q, k, v, zs, gt, h2), (gw2d,) = _mix_proj_fwd(x1, nm, p["b_gate"], gwm, [_Gather(rows2d)])
    y_att, (gw2gu,) = _att_fwd(q, k, v, bias, [_Gather(rows2gu)])
    x2, y_sgu, merged = _merge_fwd(x1, zs, gt, y_att, lng, lnb, wm2, bsx, gwm)
    dx3, ab2, hb, _, d_nf, loss = _ffn_fwd(x2, n2, gw2gu, gw2d, 0, "ffn2_fwd", head=(nf, target))

    (dab, g_down), _ = _ffn_bwd_hidden(ab2, dx3, gw2d, 0, "ffn2_bwd_hidden")
    (dx2, d_n2), _ = _ffn_bwd_input(x2, n2, dab, dx3, gw2gu, "ffn2_bwd")
    g_gu = _weight_grad(dab, hb, "ffn2_dw_gate_up", mats=2)
    dzg, dya, dys, d_bg, g_ba, g_bs, g_wo = _merge_bwd(dx2, gt, y_att, y_sgu, merged, gwm)
    g_late = [(g_gu, 0), (g_gu, 1), (g_down, 0), (g_ba, 0), (g_bs, 0), (g_wo, 0)]
    late = ("ffn2_w_gate", "ffn2_w_up", "ffn2_w_down", "w_branch_att", "w_branch_sgu", "w_out")
    (dzs, d_wm, d_bs, d_lng, d_lnb), gots_late = _sgu_bwd(zs, dys, lng, lnb, wm2, wmt2, bsx, [_SiblingSwap(g_late)])
    sums_late = _pair_sums(g_late, gots_late, "pair_sums_late")
    (dq, dk, dv, d_bias), parts_late = _att_bwd(q, k, v, bias, dya, [_ChipScatter(sums_late)])
    big = updates(parts_late, sums_late, late)
    dz = jnp.concatenate([dq, dk, dv, dzs, dzg], axis=1)
    g_in = [(_weight_grad(dz, h2, "dw_in"), 0)]
    d_rel, got_in = _rel_bias_grad(d_bias, [_SiblingSwap(g_in)])
    sums_in = _pair_sums(g_in, got_in, "pair_sums_w_in")
    (dx1, d_nm), parts_in = _mix_proj_bwd(dz, x1, nm, dx2, gwm, [_ChipScatter(sums_in)])
    big.update(updates(parts_in, sums_in, ("w_in",)))
    small = {"norm_ffn2": d_n2, "norm_final": d_nf, "b_gate": d_bg, "sgu_ln_g": d_lng, "sgu_ln_b": d_lnb, "sgu_b_s": d_bs,
             "rel_bias": d_rel, "sgu_w_s": d_wm.reshape(_SMALL_2D["sgu_w_s"]), "norm_mix": d_nm}

    (dab, g_down), (*small_parts, loss_parts) = _ffn_bwd_hidden(ab1, dx1, gw1, 2 * R_FF, "ffn1_bwd_hidden",
                                                                [_AllToAll(list(small.values()) + [loss])])
    sums_d = chip_sums([(g_down, 0)], "ffn1_down")
    g_gu, (parts_d,) = _weight_grad(dab, h1, "ffn1_dw_gate_up", comms=[_ChipScatter(sums_d)], mats=2)
    sums_gu = chip_sums([(g_gu, 0), (g_gu, 1)], "ffn1_gate_up")
    (dx0, d_n1), parts_gu = _ffn_bwd_input(x, n1, dab, dx1, gw1, "ffn1_bwd", [_ChipScatter(sums_gu)])
    (n1_parts,) = _comm_only([_AllToAll([d_n1])], "gather_norm_ffn1")
    big.update(updates([parts_d] + parts_gu, sums_d + sums_gu, ("ffn1_w_down", "ffn1_w_gate", "ffn1_w_up")))
    out_s, loss_sum = _adamw_small(dict(zip(small, small_parts), norm_ffn1=n1_parts), loss_parts, p)
    return dx0, loss_sum[0, 0], [{**{n: four[i] for n, four in big.items()}, **s} for i, s in enumerate(out_s)]


_OUT_ORDER = ("norm_ffn1", "ffn1_w_gate", "ffn1_w_up", "ffn1_w_down", "norm_mix", "w_in", "b_gate", "rel_bias", "sgu_ln_g", "sgu_ln_b",
              "sgu_w_s", "sgu_b_s", "w_branch_att", "w_branch_sgu", "w_out", "norm_ffn2", "ffn2_w_gate", "ffn2_w_up", "ffn2_w_down",
              "norm_final")


def kernel(x, norm_ffn1, ffn1_w_gate, ffn1_w_up, ffn1_w_down, norm_mix, w_in, b_gate, rel_bias, sgu_ln_g, sgu_ln_b, sgu_w_s, sgu_b_s, w_branch_att, w_branch_sgu, w_out, norm_ffn2, ffn2_w_gate, ffn2_w_up, ffn2_w_down, norm_final, loss_target, m_norm_ffn1, m_ffn1_w_gate, m_ffn1_w_up, m_ffn1_w_down, m_norm_mix, m_w_in, m_b_gate, m_rel_bias, m_sgu_ln_g, m_sgu_ln_b, m_sgu_w_s, m_sgu_b_s, m_w_branch_att, m_w_branch_sgu, m_w_out, m_norm_ffn2, m_ffn2_w_gate, m_ffn2_w_up, m_ffn2_w_down, m_norm_final, v_norm_ffn1, v_ffn1_w_gate, v_ffn1_w_up, v_ffn1_w_down, v_norm_mix, v_w_in, v_b_gate, v_rel_bias, v_sgu_ln_g, v_sgu_ln_b, v_sgu_w_s, v_sgu_b_s, v_w_branch_att, v_w_branch_sgu, v_w_out, v_norm_ffn2, v_ffn2_w_gate, v_ffn2_w_up, v_ffn2_w_down, v_norm_final):
    args = dict(locals())
    dx, loss, outs = _step(x[0], loss_target[0], {pre + n: args[pre + n] for pre in ("", "m_", "v_") for n in _OUT_ORDER})
    return (loss, dx[None], *[o[n] for o in outs for n in _OUT_ORDER])
```

```python
import functools

import jax
import jax.numpy as jnp
from jax import lax
from jax.experimental import pallas as pl
from jax.experimental.pallas import tpu as pltpu

F32 = jnp.float32
BF16 = jnp.bfloat16

N_DEV = 8
D = 1024
F = 2816
D_ATT = 512
D_SGU = 512
D_IN = 4608
HEADS = 8
CHUNK = 64
N_LEFT = 8
REL_CLIP = 256
N_REL = 2 * REL_CLIP + 1
SGU_BLOCK = 128
EPS = 1e-6
NEG_INF = -1e30
QB = 256
KW = 3 * QB

R_FF, R_IN, R_BR, R_WO = F // N_DEV, D_IN // N_DEV, D // N_DEV, D // N_DEV
OFF_IN, OFF_BR, OFF_WO = 0, R_IN, R_IN + R_BR

FC = 256
TM = 512
VMEM_LIMIT = 56 * 1024 * 1024

ADAM_LR, ADAM_B1, ADAM_B2, ADAM_EPS, ADAM_WD, ADAM_STEP = 0.001, 0.9, 0.999, 1e-08, 0.01, 10

MESH = pl.DeviceIdType.MESH
ANY = pl.BlockSpec(memory_space=pl.ANY)


def _nt(a, b):
    return lax.dot_general(a, b, (((1,), (1,)), ((), ())), preferred_element_type=F32)


def _tn(a, b):
    return lax.dot_general(a, b, (((0,), (0,)), ((), ())), preferred_element_type=F32)


def _nn(a, b):
    return jnp.dot(a, b, preferred_element_type=F32)


def _cparams(sem=None):
    return pltpu.CompilerParams(dimension_semantics=sem, vmem_limit_bytes=VMEM_LIMIT)


def _load_rows(gw_ref, dst, off, rows, sems):
    copies = [pltpu.make_async_copy(gw_ref.at[k, pl.ds(off, rows), :], dst.at[pl.ds(k * rows, rows), :], sems.at[k])
              for k in range(N_DEV)]
    for cp in copies:
        cp.start()
    return copies


def _rms(xv):
    r = lax.rsqrt(jnp.mean(xv * xv, axis=-1, keepdims=True) + EPS)
    return xv * r, r


def _rms_bwd(dh, xn, r, gain):
    dxn = dh * gain
    dx = r * (dxn - xn * jnp.mean(dxn * xn, axis=-1, keepdims=True))
    return dx, jnp.sum(dh * xn, axis=0, keepdims=True)


def _gelu(x):
    t = jnp.tanh(0.7978845608028654 * (x + 0.044715 * x * x * x))
    return 0.5 * x * (1.0 + t), t


def _gelu_grad(x, t):
    return 0.5 * (1.0 + t) + 0.5 * x * (1.0 - t * t) * 0.7978845608028654 * (1.0 + 3.0 * 0.044715 * x * x)


def _place():
    x, y, cc = lax.axis_index("x"), lax.axis_index("y"), lax.axis_index("c")
    return x, y, cc, [(1 - x, y), (x, 1 - y), (1 - x, 1 - y)]


class _Gather:
    def __init__(self, shard):
        self.inputs = [shard]
        self.out_shape = [jax.ShapeDtypeStruct((N_DEV,) + shard.shape, shard.dtype)]
        self.scratch = [pltpu.SemaphoreType.DMA((7,)), pltpu.SemaphoreType.DMA((7,)), pltpu.SemaphoreType.DMA]

    def _copies(self, ins, outs, scr):
        (x_ref,), (out_ref,), (send_sems, recv_sems, local_sem) = ins, outs, scr
        x, y, cc, chips = _place()

        def slab(px, py, pc):
            return out_ref.at[4 * px + 2 * py + pc]

        def copy(k, block, to, src=None):
            return pltpu.make_async_remote_copy(
                src_ref=slab(*block) if src is None else src, dst_ref=slab(*block),
                send_sem=send_sems.at[k], recv_sem=recv_sems.at[k], device_id=to, device_id_type=MESH)

        me, sibling = (x, y, cc), (x, y, 1 - cc)
        x_nbr, y_nbr, diagonal = chips
        mine = pltpu.make_async_copy(x_ref, slab(*me), local_sem)
        first = [copy(0, me, sibling, src=x_ref), copy(1, me, (*x_nbr, cc), src=x_ref), copy(2, me, (*y_nbr, cc), src=x_ref)]
        neighbours = [copy(1, (*x_nbr, cc), me), copy(2, (*y_nbr, cc), me)]
        second_hand = copy(3, (x ^ (1 - cc), y ^ cc, cc), (x ^ cc, y ^ (1 - cc), cc))
        from_diagonal = copy(3, (*diagonal, cc), me)
        passed = [copy(4 + j, (*chip, cc), sibling) for j, chip in enumerate(chips)]
        from_sibling = [copy(0, sibling, me)] + [copy(4 + j, (*chip, 1 - cc), me) for j, chip in enumerate(chips)]
        return mine, first, neighbours, second_hand, from_diagonal, passed, from_sibling

    def begin(self, *refs):
        mine, first = self._copies(*refs)[:2]
        mine.start()
        for cp in first:
            cp.start()

    def mid(self, *refs):
        _, _, neighbours, second_hand, _, passed, _ = self._copies(*refs)
        for cp in neighbours:
            cp.wait_recv()
        second_hand.start()
        passed[0].start()
        passed[1].start()

    def relay(self, *refs):
        _, _, _, _, from_diagonal, passed, _ = self._copies(*refs)
        from_diagonal.wait_recv()
        passed[2].start()

    def end(self, *refs):
        mine, first, _, second_hand, _, passed, from_sibling = self._copies(*refs)
        for cp in from_sibling:
            cp.wait_recv()
        for cp in first + [second_hand] + passed:
            cp.wait_send()
        mine.wait()


class _Direct:
    def begin(self, *refs):
        keep, give = self._copies(*refs)
        for cp in keep + give:
            cp.start()

    def mid(self, *refs):
        pass

    def relay(self, *refs):
        pass

    def end(self, *refs):
        keep, give = self._copies(*refs)
        for cp in give:
            cp.wait_recv()
        for cp in give:
            cp.wait_send()
        for cp in keep:
            cp.wait()


class _SiblingSwap(_Direct):
    def __init__(self, grads):
        n = len(grads)
        self.which = [w for _, w in grads]
        self.inputs = [g for g, _ in grads]
        self.out_shape = [jax.ShapeDtypeStruct((4,) + g.shape[1:], g.dtype) for g, _ in grads]
        self.scratch = [pltpu.SemaphoreType.DMA((n, 4)), pltpu.SemaphoreType.DMA((n, 4))]

    def _copies(self, ins, outs, scr):
        send_sems, recv_sems = scr
        x, y, cc, _ = _place()
        return [], [pltpu.make_async_remote_copy(src_ref=g_ref.at[N_DEV * w + 2 * j + 1 - cc], dst_ref=got_ref.at[j],
                                                 send_sem=send_sems.at[i, j], recv_sem=recv_sems.at[i, j], device_id=(x, y, 1 - cc),
                                                 device_id_type=MESH)
                    for i, (g_ref, got_ref, w) in enumerate(zip(ins, outs, self.which)) for j in range(4)]


class _ChipScatter(_Direct):
    def __init__(self, sums):
        n = len(sums)
        self.inputs = list(sums)
        self.out_shape = [jax.ShapeDtypeStruct((3,) + s.shape[1:], s.dtype) for s in sums]
        self.scratch = [pltpu.SemaphoreType.DMA((n, 3)), pltpu.SemaphoreType.DMA((n, 3))]

    def _copies(self, ins, outs, scr):
        send_sems, recv_sems = scr
        _, _, cc, chips = _place()
        return [], [pltpu.make_async_remote_copy(src_ref=s_ref.at[2 * px + py], dst_ref=got_ref.at[j], send_sem=send_sems.at[i, j],
                                                 recv_sem=recv_sems.at[i, j], device_id=(px, py, cc), device_id_type=MESH)
                    for i, (s_ref, got_ref) in enumerate(zip(ins, outs)) for j, (px, py) in enumerate(chips)]


class _AllToAll(_Direct):
    def __init__(self, blocks):
        n = len(blocks)
        self.inputs = list(blocks)
        self.out_shape = [jax.ShapeDtypeStruct((N_DEV,) + b.shape, b.dtype) for b in blocks]
        self.scratch = [pltpu.SemaphoreType.DMA((n, 7)), pltpu.SemaphoreType.DMA((n, 7)), pltpu.SemaphoreType.DMA((n,))]

    def _copies(self, ins, outs, scr):
        send_sems, recv_sems, local_sems = scr
        x, y, cc, _ = _place()
        me = 4 * x + 2 * y + cc
        keep = [pltpu.make_async_copy(b_ref, got_ref.at[me], local_sems.at[i]) for i, (b_ref, got_ref) in enumerate(zip(ins, outs))]
        give = [pltpu.make_async_remote_copy(src_ref=b_ref, dst_ref=got_ref.at[me], send_sem=send_sems.at[i, k - 1],
                                             recv_sem=recv_sems.at[i, k - 1],
                                             device_id=(x ^ ((k >> 2) & 1), y ^ ((k >> 1) & 1), cc ^ (k & 1)), device_id_type=MESH)
                for i, (b_ref, got_ref) in enumerate(zip(ins, outs)) for k in range(1, N_DEV)]
        return keep, give


def _split_refs(refs, counts):
    out, pos = [], 0
    for n in counts:
        out.append(list(refs[pos:pos + n]))
        pos += n
    return out


def _bind(comms, c_in, c_out, c_scr):
    ins = _split_refs(c_in, [len(c.inputs) for c in comms])
    outs = _split_refs(c_out, [len(c.out_shape) for c in comms])
    scr = _split_refs(c_scr, [len(c.scratch) for c in comms])
    return [(c, (i, o, s)) for c, i, o, s in zip(comms, ins, outs, scr)]


def _call(body, *, name, grid, in_specs, out_specs, out_shape, scratch_shapes, args, comms=()):
    c_in = [a for c in comms for a in c.inputs]
    c_out = [s for c in comms for s in c.out_shape]
    c_scr = [s for c in comms for s in c.scratch]
    counts = [len(in_specs), len(c_in), len(out_shape), len(c_out), len(scratch_shapes), len(c_scr)]

    def full(*refs):
        ins, cin, outs, cout, scr, cscr = _split_refs(refs, counts)
        bound = _bind(comms, cin, cout, cscr)
        if comms:
            def at(steps):
                return functools.reduce(jnp.logical_and, [pl.program_id(ax) == s for ax, s in enumerate(steps)])

            first, last = at([0] * len(grid)), at([n - 1 for n in grid])

            @pl.when(first)
            def _():
                for c, r in bound:
                    c.begin(*r)

            @pl.when(at([(grid[0] - 1) // 2] + [0] * (len(grid) - 1)))
            def _():
                for c, r in bound:
                    c.mid(*r)

            @pl.when(last)
            def _():
                for c, r in bound:
                    c.relay(*r)

        body(*ins, *outs, *scr)
        if comms:
            @pl.when(last)
            def _():
                for c, r in bound:
                    c.end(*r)

    res = pl.pallas_call(
        full, name=name, grid=grid,
        in_specs=list(in_specs) + [ANY] * len(c_in), out_specs=list(out_specs) + [ANY] * len(c_out),
        out_shape=list(out_shape) + c_out, scratch_shapes=list(scratch_shapes) + c_scr,
        compiler_params=_cparams(("arbitrary",) * len(grid)),
    )(*args, *c_in)
    return list(res[:len(out_shape)]), list(res[len(out_shape):])


def _comm_only(comms, name):
    c_in = [a for c in comms for a in c.inputs]
    c_out = [s for c in comms for s in c.out_shape]
    c_scr = [s for c in comms for s in c.scratch]

    def full(*refs):
        cin, cout, cscr = _split_refs(refs, [len(c_in), len(c_out), len(c_scr)])
        bound = _bind(comms, cin, cout, cscr)
        for phase in ("begin", "mid", "relay", "end"):
            for c, r in bound:
                getattr(c, phase)(*r)

    return list(pl.pallas_call(full, name=name, in_specs=[ANY] * len(c_in), out_specs=[ANY] * len(c_out), out_shape=c_out,
                               scratch_shapes=c_scr)(*c_in))


def _my_index(*axes_and_weights):
    return sum(w * lax.axis_index(a) for a, w in axes_and_weights).astype(jnp.int32).reshape(1)


def _pair_sums(grads, gots, name):
    n = len(grads)

    def body(c_ref, *refs):
        for a_ref, b_ref, o_ref in zip(refs[:n], refs[n:2 * n], refs[2 * n:]):
            o_ref[...] = (a_ref[...].astype(F32) + b_ref[...].astype(F32)).astype(BF16)

    def tile(g):
        return pl.BlockSpec((1,) + g.shape[1:], lambda j, c_ref: (j, 0, 0))

    def mine(g, w):
        return pl.BlockSpec((1, None) + g.shape[1:], lambda j, c_ref: (4 * w + j, c_ref[0], 0, 0))

    return list(pl.pallas_call(
        body, name=name,
        grid_spec=pltpu.PrefetchScalarGridSpec(num_scalar_prefetch=1, grid=(4,),
                                               in_specs=[mine(g, w) for g, w in grads] + [tile(g) for g in gots],
                                               out_specs=[tile(g) for g in gots]),
        out_shape=[jax.ShapeDtypeStruct(g.shape, BF16) for g in gots],
        compiler_params=_cparams(("arbitrary",)))(_my_index(("c", 1)), *[g.reshape((-1, 2) + g.shape[1:]) for g, _ in grads], *gots))


def _ffn_fwd(x, gain, gw_gu, gw_d, off_d, name, comms=(), head=None):
    t = x.shape[0]
    n_head = 0 if head is None else 2

    def body(x_ref, g_ref, gu_ref, d_ref, *refs):
        head_in, (o_ref, ab_ref, h_ref), head_out = refs[:n_head], refs[n_head:n_head + 3], refs[n_head + 3:2 * n_head + 3]
        wg, wu, wd, s_scr, sems = refs[2 * n_head + 3:]

        @pl.when(pl.program_id(0) == 0)
        def _():
            cps = _load_rows(gu_ref, wg, 0, R_FF, sems.at[0]) + _load_rows(gu_ref, wu, R_FF, R_FF, sems.at[1]) \
                + _load_rows(d_ref, wd, off_d, R_FF, sems.at[2])
            for o in head_out:
                o[...] = jnp.zeros_like(o)
            for cp in cps:
                cp.wait()

        xv = x_ref[...]
        xn, _ = _rms(xv)
        h = (xn * g_ref[...]).astype(BF16)
        h_ref[...] = h
        for c in range(F // FC):
            rows = pl.ds(c * FC, FC)
            a = _nt(h, wg[rows, :])
            b = _nt(h, wu[rows, :])
            ab_ref[:, c * FC:(c + 1) * FC] = a.astype(BF16)
            ab_ref[:, F + c * FC:F + (c + 1) * FC] = b.astype(BF16)
            s_scr[:, c * FC:(c + 1) * FC] = (a * jax.nn.sigmoid(a) * b).astype(BF16)
        out = xv + 0.5 * _nn(s_scr[...], wd[...])
        if head is None:
            o_ref[...] = out
        else:
            (gf_ref, t_ref), (dg_ref, loss_ref) = head_in, head_out
            gain_f = gf_ref[...]
            yn, r = _rms(out)
            err = yn * gain_f - t_ref[...]
            loss_ref[...] += 0.5 * jnp.sum(jnp.mean(err * err, axis=-1, keepdims=True), axis=0, keepdims=True)
            o_ref[...], dg = _rms_bwd(err * (1.0 / D), yn, r, gain_f)
            dg_ref[...] += dg

    tile = pl.BlockSpec((TM, D), lambda i: (i, 0))
    row = pl.BlockSpec((1, D), lambda i: (0, 0))
    head_specs = [] if head is None else [row, pl.BlockSpec((1, 128), lambda i: (0, 0))]
    head_shapes = [] if head is None else [jax.ShapeDtypeStruct((1, D), F32), jax.ShapeDtypeStruct((1, 128), F32)]
    res, got = _call(
        body, name=name, grid=(t // TM,),
        in_specs=[tile, row, ANY, ANY] + ([] if head is None else [row, tile]),
        out_specs=[tile, pl.BlockSpec((TM, 2 * F), lambda i: (i, 0)), tile] + head_specs,
        out_shape=[jax.ShapeDtypeStruct((t, D), F32), jax.ShapeDtypeStruct((t, 2 * F), BF16), jax.ShapeDtypeStruct((t, D), BF16)] + head_shapes,
        scratch_shapes=[pltpu.VMEM((F, D), BF16)] * 3 + [pltpu.VMEM((TM, F), BF16), pltpu.SemaphoreType.DMA((3, N_DEV))],
        args=(x, gain, gw_gu, gw_d) + (() if head is None else tuple(head)), comms=comms)
    return (res[0], res[1], res[2], got, *res[3:])


def _ffn_bwd_hidden(ab, dout, gw_d, off_d, name, comms=()):
    t = ab.shape[0]
    n_steps = t // TM
    row_chunks = [(r, min(512, F - r)) for r in range(0, F, 512)]

    def tile(w):
        return pl.BlockSpec((TM, w), lambda i: (i, 0))

    def hidden(ab_ref, do_ref, d_ref, dab_ref, dwd_ref, wd, s_scr, acc, sems, out_sem):
        step = pl.program_id(0)

        @pl.when(step == 0)
        def _():
            cps = _load_rows(d_ref, wd, off_d, R_FF, sems)
            acc[...] = jnp.zeros_like(acc)
            for cp in cps:
                cp.wait()

        df = (0.5 * do_ref[...]).astype(BF16)
        for c in range(F // FC):
            a = ab_ref[:, c * FC:(c + 1) * FC].astype(F32)
            b = ab_ref[:, F + c * FC:F + (c + 1) * FC].astype(F32)
            sg = jax.nn.sigmoid(a)
            sl = a * sg
            ds = _nt(df, wd[pl.ds(c * FC, FC), :])
            dab_ref[:, c * FC:(c + 1) * FC] = (ds * b * (sg * (1.0 + a * (1.0 - sg)))).astype(BF16)
            dab_ref[:, F + c * FC:F + (c + 1) * FC] = (ds * sl).astype(BF16)
            s_scr[:, c * FC:(c + 1) * FC] = (sl * b).astype(BF16)
        for r, n in row_chunks:
            acc[r:r + n, :] += _tn(s_scr[:, r:r + n], df)

        @pl.when(step == n_steps - 1)
        def _():
            wd[...] = acc[...].astype(BF16)
            out = pltpu.make_async_copy(wd, dwd_ref, out_sem)
            out.start()
            out.wait()

    (dab, dwd), got = _call(
        hidden, name=name, grid=(n_steps,),
        in_specs=[tile(2 * F), tile(D), ANY], out_specs=[tile(2 * F), ANY],
        out_shape=[jax.ShapeDtypeStruct((t, 2 * F), BF16), jax.ShapeDtypeStruct((F, D), BF16)],
        scratch_shapes=[pltpu.VMEM((F, D), BF16), pltpu.VMEM((TM, F), BF16), pltpu.VMEM((F, D), F32), pltpu.SemaphoreType.DMA((N_DEV,)),
                        pltpu.SemaphoreType.DMA],
        args=(ab, dout, gw_d), comms=comms)
    return (dab, dwd.reshape(N_DEV, R_FF, D)), got


def _ffn_bwd_input(x, gain, dab, dout, gw_gu, name, comms=()):
    t = x.shape[0]

    def body(x_ref, g_ref, dab_ref, do_ref, gu_ref, dx_ref, dg_ref, wgu, sems):
        @pl.when(pl.program_id(0) == 0)
        def _():
            cps = _load_rows(gu_ref, wgu.at[0:F], 0, R_FF, sems.at[0]) + _load_rows(gu_ref, wgu.at[F:2 * F], R_FF, R_FF, sems.at[1])
            dg_ref[...] = jnp.zeros_like(dg_ref)
            for cp in cps:
                cp.wait()

        gain_v = g_ref[...]
        xn, r = _rms(x_ref[...])
        dh = _nn(dab_ref[...], wgu[...])
        dxn, dg = _rms_bwd(dh, xn, r, gain_v)
        dg_ref[...] += dg
        dx_ref[...] = do_ref[...] + dxn

    def tile(w):
        return pl.BlockSpec((TM, w), lambda i: (i, 0))

    row = pl.BlockSpec((1, D), lambda i: (0, 0))
    return _call(
        body, name=name, grid=(t // TM,),
        in_specs=[tile(D), row, tile(2 * F), tile(D), ANY], out_specs=[tile(D), row],
        out_shape=[jax.ShapeDtypeStruct((t, D), F32), jax.ShapeDtypeStruct((1, D), F32)],
        scratch_shapes=[pltpu.VMEM((2 * F, D), BF16), pltpu.SemaphoreType.DMA((2, N_DEV))],
        args=(x, gain, dab, dout, gw_gu), comms=comms)


def _weight_grad(a, b, name, col_off=0, m=None, comms=None, mats=1):
    t = a.shape[0]
    m = a.shape[1] if m is None else m
    n = b.shape[1]
    tmm = 512 if m % 512 == 0 else 256
    first = col_off // tmm

    def body(a_ref, b_ref, o_ref):
        o_ref[...] = _tn(a_ref[...], b_ref[...]).astype(BF16)

    (out,), got = _call(
        body, name=name, grid=(m // tmm,),
        in_specs=[pl.BlockSpec((t, tmm), lambda i: (0, first + i)), pl.BlockSpec((t, n), lambda i: (0, 0))],
        out_specs=[pl.BlockSpec((tmm, n), lambda i: (i, 0))],
        out_shape=[jax.ShapeDtypeStruct((m, n), BF16)], scratch_shapes=[], args=(a, b), comms=comms or ())
    out = out.reshape(mats * N_DEV, m // (mats * N_DEV), n)
    return out if comms is None else (out, got)


def _mix_proj_fwd(x, gain, b_gate, gw, comms=()):
    t = x.shape[0]

    def body(x_ref, g_ref, bg_ref, gw_ref, q_ref, k_ref, v_ref, zs_ref, gt_ref, h_ref, win, sems):
        @pl.when(pl.program_id(0) == 0)
        def _():
            for cp in _load_rows(gw_ref, win, OFF_IN, R_IN, sems):
                cp.wait()

        xn, _ = _rms(x_ref[...])
        h = (xn * g_ref[...]).astype(BF16)
        h_ref[...] = h
        z = _nt(h, win[...])
        q_ref[...] = (z[:, 0:D_ATT] * 0.125).astype(BF16)
        k_ref[...] = z[:, D_ATT:2 * D_ATT].astype(BF16)
        v_ref[...] = z[:, 2 * D_ATT:3 * D_ATT].astype(BF16)
        zs_ref[...] = z[:, 3 * D_ATT:3 * D_ATT + 2 * D_SGU].astype(BF16)
        gt_ref[...] = jax.nn.sigmoid(z[:, 3 * D_ATT + 2 * D_SGU:] + bg_ref[...]).astype(BF16)

    def tile(w):
        return pl.BlockSpec((TM, w), lambda i: (i, 0))

    return _call(
        body, name="mix_proj_fwd", grid=(t // TM,),
        in_specs=[tile(D), pl.BlockSpec((1, D), lambda i: (0, 0)), pl.BlockSpec((1, 2 * D), lambda i: (0, 0)), ANY],
        out_specs=[tile(D_ATT), tile(D_ATT), tile(D_ATT), tile(2 * D_SGU), tile(2 * D), tile(D)],
        out_shape=[jax.ShapeDtypeStruct((t, D_ATT), BF16)] * 3 + [jax.ShapeDtypeStruct((t, 2 * D_SGU), BF16),
                                                                   jax.ShapeDtypeStruct((t, 2 * D), BF16),
                                                                   jax.ShapeDtypeStruct((t, D), BF16)],
        scratch_shapes=[pltpu.VMEM((D_IN, D), BF16), pltpu.SemaphoreType.DMA((N_DEV,))],
        args=(x, gain, b_gate, gw), comms=comms)


def _mix_proj_bwd(dz, x, gain, dres, gw, comms=()):
    t = x.shape[0]

    def body(dz_ref, x_ref, g_ref, dr_ref, gw_ref, dx_ref, dg_ref, win, sems):
        @pl.when(pl.program_id(0) == 0)
        def _():
            cps = _load_rows(gw_ref, win, OFF_IN, R_IN, sems)
            dg_ref[...] = jnp.zeros_like(dg_ref)
            for cp in cps:
                cp.wait()

        dh = _nn(dz_ref[...], win[...])
        xn, r = _rms(x_ref[...])
        dxn, dg = _rms_bwd(dh, xn, r, g_ref[...])
        dg_ref[...] += dg
        dx_ref[...] = dr_ref[...] + dxn

    def tile(w):
        return pl.BlockSpec((TM, w), lambda i: (i, 0))

    row = pl.BlockSpec((1, D), lambda i: (0, 0))
    return _call(
        body, name="mix_proj_bwd", grid=(t // TM,),
        in_specs=[tile(D_IN), tile(D), row, tile(D), ANY], out_specs=[tile(D), row],
        out_shape=[jax.ShapeDtypeStruct((t, D), F32), jax.ShapeDtypeStruct((1, D), F32)],
        scratch_shapes=[pltpu.VMEM((D_IN, D), BF16), pltpu.SemaphoreType.DMA((N_DEV,))],
        args=(dz, x, gain, dres, gw), comms=comms)


SKEW_W = KW + QB
N_CAP = 2 * QB - REL_CLIP + 1


def _tables(rel_bias, w_s, b_s):
    cap = rel_bias[:, 2 * REL_CLIP:]
    diag = jnp.concatenate([jnp.broadcast_to(cap, (HEADS, N_CAP)), rel_bias[:, 2 * REL_CLIP - 1::-1],
                            jnp.broadcast_to(cap, (HEADS, SKEW_W - N_CAP - 2 * REL_CLIP))], axis=1)

    def fill(ins, outs):
        (d_ref, ws_ref, bs_ref), (bias_ref, wm2_ref, wmt2_ref, bsx_ref) = ins, outs
        lag = lax.broadcasted_iota(jnp.int32, (QB, KW), 1) // CHUNK - lax.broadcasted_iota(jnp.int32, (QB, KW), 0) // CHUNK
        band = (lag >= 0) & (lag <= N_LEFT)
        for h in range(HEADS):
            rows = jnp.broadcast_to(d_ref[h:h + 1, :], (QB, SKEW_W))
            bias_ref[h] = jnp.where(band, pltpu.roll(rows, 0, 1, stride=1, stride_axis=0)[:, :KW], NEG_INF)
        i = lax.broadcasted_iota(jnp.int32, (SGU_BLOCK, SGU_BLOCK), 0)
        j = lax.broadcasted_iota(jnp.int32, (SGU_BLOCK, SGU_BLOCK), 1)
        causal = (i // CHUNK) >= (j // CHUNK)
        for pair in range(4):
            w0, w1 = jnp.where(causal, ws_ref[2 * pair], 0.0), jnp.where(causal, ws_ref[2 * pair + 1], 0.0)
            wm2_ref[pair] = jnp.concatenate([w0, w1], axis=1).astype(BF16)
            wmt2_ref[pair] = jnp.concatenate([w0.T, w1.T], axis=1).astype(BF16)
        by_row = jnp.concatenate([bs_ref[...], jnp.zeros((SGU_BLOCK - 8, SGU_BLOCK), F32)], axis=0).T
        group = lax.broadcasted_iota(jnp.int32, (SGU_BLOCK, D_SGU), 1) // 64
        spread = jnp.zeros((SGU_BLOCK, D_SGU), F32)
        for g in range(8):
            spread = jnp.where(group == g, by_row[:, g:g + 1], spread)
        bsx_ref[...] = spread

    shapes = [jax.ShapeDtypeStruct((HEADS, QB, KW), F32), jax.ShapeDtypeStruct((4, SGU_BLOCK, 2 * SGU_BLOCK), BF16),
              jax.ShapeDtypeStruct((4, SGU_BLOCK, 2 * SGU_BLOCK), BF16), jax.ShapeDtypeStruct((SGU_BLOCK, D_SGU), F32)]
    return [diag, w_s, b_s], shapes, fill


def _att_specs():
    qspec = pl.BlockSpec((QB, D_ATT), lambda g: (g, 0))
    kspecs = [pl.BlockSpec((QB, D_ATT), lambda g: (jnp.maximum(g - 2, 0), 0)),
              pl.BlockSpec((QB, D_ATT), lambda g: (jnp.maximum(g - 1, 0), 0)), qspec]
    bspec = pl.BlockSpec((HEADS, QB, KW), lambda g: (0, 0, 0))
    return qspec, kspecs, bspec


def _att_probs(qm, kp, bias, valid):
    s = jnp.where(valid, _nt(qm, kp) + bias, NEG_INF)
    e = jnp.exp(s - jnp.max(s, axis=-1, keepdims=True))
    return e / jnp.sum(e, axis=-1, keepdims=True)


def _att_valid():
    g = pl.program_id(0)
    blk = lax.broadcasted_iota(jnp.int32, (QB, KW), 1) // QB
    return (blk + g) >= 2


def _att_fwd(q, k, v, bias, comms=()):
    t = q.shape[0]

    def body(q_ref, k0, k1, k2, v0, v1, v2, b_ref, y_ref):
        valid = _att_valid()
        first = lax.broadcasted_iota(jnp.int32, (1, 128), 1) < 64
        for p in range(HEADS // 2):
            lanes = slice(p * 128, (p + 1) * 128)
            qp = q_ref[:, lanes]
            kp = jnp.concatenate([k0[:, lanes], k1[:, lanes], k2[:, lanes]], axis=0)
            vp = jnp.concatenate([v0[:, lanes], v1[:, lanes], v2[:, lanes]], axis=0)
            out = jnp.zeros((QB, 128), F32)
            for hh in range(2):
                mask = first if hh == 0 else jnp.logical_not(first)
                pr = _att_probs(jnp.where(mask, qp, 0), kp, b_ref[2 * p + hh], valid)
                out = out + _nn(pr.astype(BF16), jnp.where(mask, vp, 0))
            y_ref[:, lanes] = out.astype(BF16)

    qspec, kspecs, bspec = _att_specs()
    (out,), got = _call(
        body, name="att_fwd", grid=(t // QB,),
        in_specs=[qspec] + kspecs + kspecs + [bspec], out_specs=[qspec],
        out_shape=[jax.ShapeDtypeStruct((t, D_ATT), BF16)], scratch_shapes=[],
        args=(q, k, k, k, v, v, v, bias), comms=comms)
    return out, got


def _att_bwd(q, k, v, bias, dy, comms=()):
    t = q.shape[0]
    n_blocks = t // QB

    def body(q_ref, k0, k1, k2, v0, v1, v2, b_ref, dy_ref, dq_ref, dk_ref, dv_ref, db_ref, dk_acc, dv_acc):
        g = pl.program_id(0)

        @pl.when(g == 0)
        def _():
            db_ref[...] = jnp.zeros_like(db_ref)
            dk_acc[...] = jnp.zeros_like(dk_acc)
            dv_acc[...] = jnp.zeros_like(dv_acc)

        valid = _att_valid()
        first = lax.broadcasted_iota(jnp.int32, (1, 128), 1) < 64
        for p in range(HEADS // 2):
            lanes = slice(p * 128, (p + 1) * 128)
            qp = q_ref[:, lanes]
            dyp = dy_ref[:, lanes]
            kp = jnp.concatenate([k0[:, lanes], k1[:, lanes], k2[:, lanes]], axis=0)
            vp = jnp.concatenate([v0[:, lanes], v1[:, lanes], v2[:, lanes]], axis=0)
            dq = jnp.zeros((QB, 128), F32)
            dk = jnp.zeros((KW, 128), F32)
            dv = jnp.zeros((KW, 128), F32)
            for hh in range(2):
                mask = first if hh == 0 else jnp.logical_not(first)
                qm = jnp.where(mask, qp, 0)
                dym = jnp.where(mask, dyp, 0)
                pr = _att_probs(qm, kp, b_ref[2 * p + hh], valid)
                dp = _nt(dym, vp)
                ds = pr * (dp - jnp.sum(dp * pr, axis=-1, keepdims=True))
                db_ref[2 * p + hh] += ds
                dsb = ds.astype(BF16)
                dq = dq + _nn(dsb, jnp.where(mask, kp, 0))
                dk = dk + _tn(dsb, qm)
                dv = dv + _tn(pr.astype(BF16), dym)
            dq_ref[:, lanes] = (dq * 0.125).astype(BF16)
            for j in range(3):
                rows = pl.ds(pl.multiple_of(jnp.maximum(g - 2 + j, 0) * QB, QB), QB)
                dk_acc[rows, lanes] += dk[j * QB:(j + 1) * QB]
                dv_acc[rows, lanes] += dv[j * QB:(j + 1) * QB]

        @pl.when(g == n_blocks - 1)
        def _():
            dk_ref[...] = dk_acc[...].astype(BF16)
            dv_ref[...] = dv_acc[...].astype(BF16)

    qspec, kspecs, bspec = _att_specs()
    full = pl.BlockSpec((t, D_ATT), lambda g: (0, 0))
    return _call(
        body, name="att_bwd", grid=(n_blocks,),
        in_specs=[qspec] + kspecs + kspecs + [bspec, qspec], out_specs=[qspec, full, full, bspec],
        out_shape=[jax.ShapeDtypeStruct((t, D_ATT), BF16)] * 3 + [jax.ShapeDtypeStruct((HEADS, QB, KW), F32)],
        scratch_shapes=[pltpu.VMEM((t, D_ATT), F32)] * 2,
        args=(q, k, k, k, v, v, v, bias, dy), comms=comms)


def _rel_bias_grad(dbias, comms=()):
    c_in = [a for c in comms for a in c.inputs]
    c_out = [s for c in comms for s in c.out_shape]
    c_scr = [s for c in comms for s in c.scratch]

    def body(db_ref, *refs):
        cin, (cs_ref, tot_ref), cout, cscr = _split_refs(refs, [len(c_in), 2, len(c_out), len(c_scr)])
        bound = _bind(comms, cin, cout, cscr)
        for c, r in bound:
            c.begin(*r)
        lane = lax.broadcasted_iota(jnp.int32, (1, SKEW_W), 1)
        capped = (lane < N_CAP) | (lane > KW)
        pad = jnp.zeros((8, QB), F32)
        for h in range(HEADS):
            z = jnp.concatenate([db_ref[h, 0:8, :], pad], axis=1)
            for a in range(1, QB // 8):
                z = z + pltpu.roll(jnp.concatenate([db_ref[h, 8 * a:8 * a + 8, :], pad], axis=1), SKEW_W - 8 * a, 1)
            cs = z[0:1, :]
            for b in range(1, 8):
                cs = cs + pltpu.roll(z[b:b + 1, :], SKEW_W - b, 1)
            cs_ref[h:h + 1, :] = cs
            tot_ref[h:h + 1, :] = jnp.broadcast_to(jnp.sum(jnp.where(capped, cs, 0.0), axis=1, keepdims=True), (1, 128))
        for phase in ("mid", "relay", "end"):
            for c, r in bound:
                getattr(c, phase)(*r)

    whole = pl.BlockSpec(memory_space=pltpu.VMEM)
    cs, tot, *got = pl.pallas_call(
        body, name="rel_bias_grad", in_specs=[whole] + [ANY] * len(c_in), out_specs=[whole, whole] + [ANY] * len(c_out),
        out_shape=[jax.ShapeDtypeStruct((HEADS, SKEW_W), F32), jax.ShapeDtypeStruct((HEADS, 128), F32)] + c_out, scratch_shapes=c_scr,
    )(dbias, *c_in)
    return jnp.concatenate([cs[:, KW:N_CAP - 1:-1], tot[:, :1]], axis=1), got


def _group_stack(blk, first):
    return jnp.concatenate([jnp.where(first, blk, 0), jnp.where(first, 0, blk)], axis=0)


def _sgu_norm(zs_ref, lng, lnb):
    zs = zs_ref[...].astype(F32)
    ga, th = _gelu(zs)
    u, vs = ga[:, :D_SGU], ga[:, D_SGU:]
    mu = jnp.mean(vs, axis=-1, keepdims=True)
    cen = vs - mu
    rstd = lax.rsqrt(jnp.mean(cen * cen, axis=-1, keepdims=True) + EPS)
    xhat = cen * rstd
    return zs, th, u, xhat, rstd, xhat * lng + lnb


def _sgu_mix(vb, wm2_ref, bsx, s_ref):
    first = lax.broadcasted_iota(jnp.int32, (1, 128), 1) < 64
    for n in range(TM // SGU_BLOCK):
        for p in range(4):
            blk = vb[n * 128:(n + 1) * 128, p * 128:(p + 1) * 128]
            s_ref[n * 128:(n + 1) * 128, p * 128:(p + 1) * 128] = _nn(wm2_ref[p], _group_stack(blk, first)) + bsx[:, p * 128:(p + 1) * 128]


def _merge_fwd(x, zs, gt, y_att, lng, lnb, wm2, bsx, gw):
    t = x.shape[0]

    def body(x_ref, zs_ref, gt_ref, ya_ref, lng_ref, lnb_ref, wm2_ref, bsx_ref, gw_ref, xo_ref, ys_ref, mg_ref,
             wbr, wo, s_scr, sems):
        @pl.when(pl.program_id(0) == 0)
        def _():
            for cp in _load_rows(gw_ref, wbr, OFF_BR, R_BR, sems.at[0]) + _load_rows(gw_ref, wo, OFF_WO, R_WO, sems.at[1]):
                cp.wait()

        _, _, u, _, _, vsn = _sgu_norm(zs_ref, lng_ref[...], lnb_ref[...])
        _sgu_mix(vsn.astype(BF16), wm2_ref, bsx_ref[...], s_scr)
        ys = (u * s_scr[...]).astype(BF16)
        ys_ref[...] = ys
        pa = _nt(ya_ref[...], wbr[:, :D_ATT])
        ps = _nt(ys, wbr[:, D_ATT:])
        mg = (gt_ref[:, :D].astype(F32) * pa + gt_ref[:, D:].astype(F32) * ps).astype(BF16)
        mg_ref[...] = mg
        xo_ref[...] = x_ref[...] + _nn(mg, wo[...])

    def tile(w):
        return pl.BlockSpec((TM, w), lambda i: (i, 0))

    def const(shape):
        return pl.BlockSpec(shape, lambda i: (0,) * len(shape))

    return pl.pallas_call(
        body, name="merge_fwd", grid=(t // TM,),
        in_specs=[tile(D), tile(2 * D_SGU), tile(2 * D), tile(D_ATT), const((1, D_SGU)), const((1, D_SGU)),
                  const((4, 128, 256)), const((128, D_SGU)), ANY],
        out_specs=[tile(D), tile(D_SGU), tile(D)],
        out_shape=[jax.ShapeDtypeStruct((t, D), F32), jax.ShapeDtypeStruct((t, D_SGU), BF16), jax.ShapeDtypeStruct((t, D), BF16)],
        scratch_shapes=[pltpu.VMEM((D, D), BF16), pltpu.VMEM((D, D), BF16), pltpu.VMEM((TM, D_SGU), F32),
                        pltpu.SemaphoreType.DMA((2, N_DEV))],
        compiler_params=_cparams(("arbitrary",)),
    )(x, zs, gt, y_att, lng, lnb, wm2, bsx, gw)


def _merge_bwd(dx, gt, y_att, y_sgu, merged, gw):
    t = dx.shape[0]
    n_steps = t // TM

    def body(dx_ref, gt_ref, ya_ref, ys_ref, mg_ref, gw_ref, dzg_ref, dya_ref, dys_ref, dbg_ref, ga_ref, gs_ref, go_ref,
             wbr, wo, acc_a, acc_s, acc_o, sems):
        step = pl.program_id(0)

        @pl.when(step == 0)
        def _():
            cps = _load_rows(gw_ref, wbr, OFF_BR, R_BR, sems.at[0]) + _load_rows(gw_ref, wo, OFF_WO, R_WO, sems.at[1])
            dbg_ref[...] = jnp.zeros_like(dbg_ref)
            for acc in (acc_a, acc_s, acc_o):
                acc[...] = jnp.zeros_like(acc)
            for cp in cps:
                cp.wait()

        dxb = dx_ref[...].astype(BF16)
        acc_o[...] += _tn(mg_ref[...], dxb)
        dm = _nt(dxb, wo[...])
        for half, y_ref, w, acc in ((0, ya_ref, wbr.at[:, :D_ATT], acc_a), (1, ys_ref, wbr.at[:, D_ATT:], acc_s)):
            cols = slice(half * D, (half + 1) * D)
            gate = gt_ref[:, cols].astype(F32)
            branch = _nt(y_ref[...], w[...])
            dzg = dm * branch * gate * (1.0 - gate)
            dbg_ref[:, cols] += jnp.sum(dzg, axis=0, keepdims=True)
            dzg_ref[:, cols] = dzg.astype(BF16)
            dbr = (dm * gate).astype(BF16)
            acc[...] += _tn(dbr, y_ref[...])
            dy = _nn(dbr, w[...])
            if half == 0:
                dya_ref[...] = dy.astype(BF16)
            else:
                dys_ref[...] = dy

        @pl.when(step == n_steps - 1)
        def _():
            ga_ref[...] = acc_a[...].astype(BF16)
            gs_ref[...] = acc_s[...].astype(BF16)
            go_ref[...] = acc_o[...].astype(BF16)

    def tile(w):
        return pl.BlockSpec((TM, w), lambda i: (i, 0))

    def whole(r, c):
        return pl.BlockSpec((r, c), lambda i: (0, 0))

    dzg, dya, dys, dbg, g_ba, g_bs, g_wo = pl.pallas_call(
        body, name="merge_bwd", grid=(n_steps,),
        in_specs=[tile(D), tile(2 * D), tile(D_ATT), tile(D_SGU), tile(D), ANY],
        out_specs=[tile(2 * D), tile(D_ATT), tile(D_SGU), whole(1, 2 * D), whole(D, D_ATT), whole(D, D_SGU), whole(D, D)],
        out_shape=[jax.ShapeDtypeStruct((t, 2 * D), BF16), jax.ShapeDtypeStruct((t, D_ATT), BF16), jax.ShapeDtypeStruct((t, D_SGU), F32),
                   jax.ShapeDtypeStruct((1, 2 * D), F32), jax.ShapeDtypeStruct((D, D_ATT), BF16), jax.ShapeDtypeStruct((D, D_SGU), BF16),
                   jax.ShapeDtypeStruct((D, D), BF16)],
        scratch_shapes=[pltpu.VMEM((D, D), BF16), pltpu.VMEM((D, D), BF16), pltpu.VMEM((D, D_ATT), F32), pltpu.VMEM((D, D_SGU), F32),
                        pltpu.VMEM((D, D), F32), pltpu.SemaphoreType.DMA((2, N_DEV))],
        compiler_params=_cparams(("arbitrary",)),
    )(dx, gt, y_att, y_sgu, merged, gw)
    return dzg, dya, dys, dbg, g_ba.reshape(N_DEV, R_BR, D_ATT), g_bs.reshape(N_DEV, R_BR, D_SGU), g_wo.reshape(N_DEV, R_WO, D)


def _sgu_bwd(zs, dys, lng, lnb, wm2, wmt2, bsx, comms=()):
    t = zs.shape[0]
    n_steps = t // TM

    def body(zs_ref, dys_ref, lng_ref, lnb_ref, wm2_ref, wmt2_ref, bsx_ref, dzs_ref, dw_ref, dbs_ref, dlg_ref, dlb_ref,
             s_scr, dv_scr, ds_acc):
        i = pl.program_id(0)

        @pl.when(i == 0)
        def _():
            dw_ref[...] = jnp.zeros_like(dw_ref)
            dlg_ref[...] = jnp.zeros_like(dlg_ref)
            dlb_ref[...] = jnp.zeros_like(dlb_ref)
            ds_acc[...] = jnp.zeros_like(ds_acc)

        lng = lng_ref[...]
        zs, th, u, xhat, rstd, vsn = _sgu_norm(zs_ref, lng, lnb_ref[...])
        vb = vsn.astype(BF16)
        _sgu_mix(vb, wm2_ref, bsx_ref[...], s_scr)
        dys = dys_ref[...]
        du = dys * s_scr[...]
        ds = dys * u
        dsb = ds.astype(BF16)
        first = lax.broadcasted_iota(jnp.int32, (1, 128), 1) < 64
        acc = jnp.zeros((SGU_BLOCK, D_SGU), F32)
        for n in range(TM // SGU_BLOCK):
            rows = slice(n * 128, (n + 1) * 128)
            acc = acc + ds[rows]
            for p in range(4):
                lanes = slice(p * 128, (p + 1) * 128)
                stack = _group_stack(dsb[rows, lanes], first)
                dv_scr[rows, lanes] = _nn(wmt2_ref[p], stack)
                dw_ref[p] += _nt(stack, vb[rows, lanes])
        ds_acc[...] += acc
        dvsn = dv_scr[...]
        dlg_ref[...] += jnp.sum(dvsn * xhat, axis=0, keepdims=True)
        dlb_ref[...] += jnp.sum(dvsn, axis=0, keepdims=True)
        dxh = dvsn * lng
        dvs = rstd * (dxh - jnp.mean(dxh, axis=-1, keepdims=True) - xhat * jnp.mean(dxh * xhat, axis=-1, keepdims=True))
        dga = jnp.concatenate([du, dvs], axis=1)
        dzs_ref[...] = (dga * _gelu_grad(zs, th)).astype(BF16)

        @pl.when(i == n_steps - 1)
        def _():
            r = lax.broadcasted_iota(jnp.int32, (256, 128), 0) % SGU_BLOCK
            c = lax.broadcasted_iota(jnp.int32, (256, 128), 1)
            keep = (r // CHUNK) >= (c // CHUNK)
            for p in range(4):
                dw_ref[p] = jnp.where(keep, dw_ref[p], 0.0)
            total = ds_acc[...]
            grp = lax.broadcasted_iota(jnp.int32, (SGU_BLOCK, D_SGU), 1) // 64
            lane = lax.broadcasted_iota(jnp.int32, (SGU_BLOCK, 128), 1)
            out = jnp.zeros((SGU_BLOCK, 128), F32)
            for gi in range(8):
                out = jnp.where(lane == gi, jnp.sum(jnp.where(grp == gi, total, 0.0), axis=1, keepdims=True), out)
            dbs_ref[...] = out

    def tile(w):
        return pl.BlockSpec((TM, w), lambda i: (i, 0))

    def const(shape):
        return pl.BlockSpec(shape, lambda i: (0,) * len(shape))

    return _call(
        body, name="sgu_bwd", grid=(n_steps,),
        in_specs=[tile(2 * D_SGU), tile(D_SGU), const((1, D_SGU)), const((1, D_SGU)), const((4, 128, 256)), const((4, 128, 256)),
                  const((128, D_SGU))],
        out_specs=[tile(2 * D_SGU), const((4, 256, 128)), const((128, 128)), const((1, D_SGU)), const((1, D_SGU))],
        out_shape=[jax.ShapeDtypeStruct((t, 2 * D_SGU), BF16), jax.ShapeDtypeStruct((4, 256, 128), F32),
                   jax.ShapeDtypeStruct((128, 128), F32), jax.ShapeDtypeStruct((1, D_SGU), F32), jax.ShapeDtypeStruct((1, D_SGU), F32)],
        scratch_shapes=[pltpu.VMEM((TM, D_SGU), F32), pltpu.VMEM((TM, D_SGU), F32), pltpu.VMEM((SGU_BLOCK, D_SGU), F32)],
        args=(zs, dys, lng, lnb, wm2, wmt2, bsx), comms=comms)


def _adamw(g, w, m, v):
    m = ADAM_B1 * m + (1.0 - ADAM_B1) * g
    v = ADAM_B2 * v + (1.0 - ADAM_B2) * (g * g)
    m_hat = m / (1.0 - ADAM_B1 ** ADAM_STEP)
    v_hat = v / (1.0 - ADAM_B2 ** ADAM_STEP)
    return -ADAM_LR * (m_hat / (jnp.sqrt(v_hat) + ADAM_EPS) + ADAM_WD * w), m, v


def _adamw_matrices(items, name):
    n = len(items)
    c = items[0][0].shape[2]
    tc = 256

    def body(own_ref, *refs):
        for i, (_, _, _, _, _, transposed) in enumerate(items):
            p_ref, s_ref, w_ref, m_ref, v_ref = refs[5 * i:5 * i + 5]
            g = p_ref[0].astype(F32) + p_ref[1].astype(F32) + p_ref[2].astype(F32) + s_ref[...].astype(F32)
            g = g.T if transposed else g
            res = (g,) + _adamw(g, w_ref[...], m_ref[...], v_ref[...])
            for o_ref, val in zip(refs[5 * n + 4 * i:5 * n + 4 * i + 4], res):
                o_ref[...] = val

    in_specs, out_specs, out_shape, args = [], [], [], []
    for parts, sums, w, m, v, transposed in items:
        r = parts.shape[1]
        own = pl.BlockSpec((None, tc, r), lambda i, o: (0, i, 0)) if transposed else pl.BlockSpec((None, r, tc), lambda i, o: (0, 0, i))
        in_specs += [pl.BlockSpec((3, r, tc), lambda i, o: (0, 0, i)), pl.BlockSpec((None, r, tc), lambda i, o: (o[0], 0, i)), own, own, own]
        out_specs += [own] * 4
        out_shape += [jax.ShapeDtypeStruct(w.shape, F32)] * 4
        args += [parts, sums, w, m, v]
    res = pl.pallas_call(
        body, name=name,
        grid_spec=pltpu.PrefetchScalarGridSpec(num_scalar_prefetch=1, grid=(c // tc,), in_specs=in_specs, out_specs=out_specs),
        out_shape=out_shape, compiler_params=_cparams(("arbitrary",)),
    )(_my_index(("x", 2), ("y", 1)), *args)
    return [list(res[4 * i:4 * i + 4]) for i in range(n)]


_SMALL_2D = {"norm_ffn1": (1, D), "norm_mix": (1, D), "norm_ffn2": (1, D), "norm_final": (1, D), "b_gate": (1, 2 * D),
             "sgu_ln_g": (1, D_SGU), "sgu_ln_b": (1, D_SGU), "sgu_b_s": (8, SGU_BLOCK), "rel_bias": (HEADS, N_REL),
             "sgu_w_s": (8 * SGU_BLOCK, SGU_BLOCK)}


def _adamw_small(parts, loss_parts, p):
    names = list(parts)
    n = len(names)

    def body(*refs):
        got, loss_got, wmv, outs, loss_out = refs[:n], refs[n], refs[n + 1:4 * n + 1], refs[4 * n + 1:8 * n + 1], refs[8 * n + 1]
        for i, name in enumerate(names):
            g = got[i][0]
            for k in range(1, N_DEV):
                g = g + got[i][k]
            if name == "sgu_b_s":
                g = g.T[0:8, :]
            res = (g,) + _adamw(g, wmv[3 * i][...], wmv[3 * i + 1][...], wmv[3 * i + 2][...])
            for o_ref, val in zip(outs[4 * i:4 * i + 4], res):
                o_ref[...] = val
        total = loss_got[0]
        for k in range(1, N_DEV):
            total = total + loss_got[k]
        loss_out[...] = total

    wmv = [p[pre + name].reshape(_SMALL_2D[name]) for name in names for pre in ("", "m_", "v_")]
    res = pl.pallas_call(
        body, name="adamw_small",
        out_shape=[jax.ShapeDtypeStruct(_SMALL_2D[name], F32) for name in names for _ in range(4)] + [jax.ShapeDtypeStruct((1, 128), F32)],
        compiler_params=_cparams())(*[parts[name] for name in names], loss_parts, *wmv)
    return [{name: res[4 * i + j].reshape(p[name].shape) for i, name in enumerate(names)} for j in range(4)], res[-1]


def _pack_rows(slabs, name, comms=(), extra=None):
    flat = [a for groups in slabs for grp in groups for a, _ in grp]
    rows = [[grp[0][0].shape[2] if grp[0][1] else grp[0][0].shape[1] for grp in groups] for groups in slabs]
    e_in, e_out, fill = extra if extra is not None else ([], [], None)
    c_in = [a for c in comms for a in c.inputs]
    c_out = [s for c in comms for s in c.out_shape]
    c_scr = [s for c in comms for s in c.scratch]

    def body(*refs):
        ins, ein, cin, outs, eout, cout, cscr = _split_refs(
            refs, [len(flat), len(e_in), len(c_in), len(slabs), len(e_out), len(c_out), len(c_scr)])
        bound = _bind(comms, cin, cout, cscr)
        for c, r in bound:
            c.begin(*r)
        pos = 0
        for groups, slab_rows, o_ref in zip(slabs, rows, outs):
            off = 0
            for grp, r in zip(groups, slab_rows):
                vals = []
                for _, transposed in grp:
                    val = ins[pos][0]
                    vals.append(val.T if transposed else val)
                    pos += 1
                o_ref[off:off + r, :] = (vals[0] if len(vals) == 1 else jnp.concatenate(vals, axis=1)).astype(BF16)
                off += r
        if fill is not None:
            fill(ein, eout)
        for phase in ("mid", "relay", "end"):
            for c, r in bound:
                getattr(c, phase)(*r)

    whole = pl.BlockSpec(memory_space=pltpu.VMEM)
    n_plain = len(slabs) + len(e_out)
    res = pl.pallas_call(
        body, name=name, in_specs=[whole] * (len(flat) + len(e_in)) + [ANY] * len(c_in), out_specs=[whole] * n_plain + [ANY] * len(c_out),
        out_shape=[jax.ShapeDtypeStruct((sum(r), D), BF16) for r in rows] + list(e_out) + c_out, scratch_shapes=c_scr,
        compiler_params=_cparams())(*flat, *e_in, *c_in)
    return list(res[:len(slabs)]), list(res[len(slabs):n_plain]), list(res[n_plain:])


def _step(x, target, p):
    n1, nm, n2 = p["norm_ffn1"], p["norm_mix"], p["norm_ffn2"]
    nf = p["norm_final"].reshape(1, D)
    lng, lnb = p["sgu_ln_g"], p["sgu_ln_b"]

    def chip_sums(grads, name):
        gots = _comm_only([_SiblingSwap(grads)], "swap_" + name)
        return _pair_sums(grads, gots, "pair_sums_" + name)

    def as_rows(a):
        return jnp.swapaxes(a, 1, 2)

    def updates(parts, sums, names):
        res = {}
        for cols in sorted({pt.shape[2] for pt in parts}):
            items, group = [], [(pt, sm, n) for pt, sm, n in zip(parts, sums, names) if pt.shape[2] == cols]
            for pt, sm, n in group:
                view = as_rows if p[n].shape[1:] != pt.shape[1:] and p[n].shape[2] > 128 else (lambda a: a)
                transposed = p[n].shape[1:] != pt.shape[1:] and p[n].shape[2] <= 128
                items.append((pt, sm, view(p[n]), view(p["m_" + n]), view(p["v_" + n]), transposed))
            for (pt, sm, n), four in zip(group, _adamw_matrices(items, "adamw_" + group[0][2])):
                res[n] = [as_rows(o) if o.shape != p[n].shape else o for o in four]
        return res

    (rows1,), _, _ = _pack_rows([[[(as_rows(p["ffn1_w_gate"]), False)], [(as_rows(p["ffn1_w_up"]), False)], [(p["ffn1_w_down"], False)]]],
                                "pack_ffn1")
    (rows_m, rows2d, rows2gu), (bias, wm2, wmt2, bsx), (gw1,) = _pack_rows(
        [[[(as_rows(p["w_in"]), False)], [(p["w_branch_att"], True), (p["w_branch_sgu"], True)], [(p["w_out"], False)]],
         [[(p["ffn2_w_down"], False)]],
         [[(as_rows(p["ffn2_w_gate"]), False)], [(as_rows(p["ffn2_w_up"]), False)]]],
        "gather_ffn1_pack_rest", [_Gather(rows1)], _tables(p["rel_bias"][0], p["sgu_w_s"][0], p["sgu_b_s"][0]))
    x1, ab1, h1, (gwm,) = _ffn_fwd(x, n1, gw1, gw1, 2 * R_FF, "ffn1_fwd", [_Gather(rows_m)])
    (q, k, v, zs, gt, h2), (gw2d,) = _mix_proj_fwd(x1, nm, p["b_gate"], gwm, [_Gather(rows2d)])
    y_att, (gw2gu,) = _att_fwd(q, k, v, bias, [_Gather(rows2gu)])
    x2, y_sgu, merged = _merge_fwd(x1, zs, gt, y_att, lng, lnb, wm2, bsx, gwm)
    dx3, ab2, hb, _, d_nf, loss = _ffn_fwd(x2, n2, gw2gu, gw2d, 0, "ffn2_fwd", head=(nf, target))

    (dab, g_down), _ = _ffn_bwd_hidden(ab2, dx3, gw2d, 0, "ffn2_bwd_hidden")
    (dx2, d_n2), _ = _ffn_bwd_input(x2, n2, dab, dx3, gw2gu, "ffn2_bwd")
    g_gu = _weight_grad(dab, hb, "ffn2_dw_gate_up", mats=2)
    dzg, dya, dys, d_bg, g_ba, g_bs, g_wo = _merge_bwd(dx2, gt, y_att, y_sgu, merged, gwm)
    g_late = [(g_gu, 0), (g_gu, 1), (g_down, 0), (g_ba, 0), (g_bs, 0), (g_wo, 0)]
    late = ("ffn2_w_gate", "ffn2_w_up", "ffn2_w_down", "w_branch_att", "w_branch_sgu", "w_out")
    (dzs, d_wm, d_bs, d_lng, d_lnb), gots_late = _sgu_bwd(zs, dys, lng, lnb, wm2, wmt2, bsx, [_SiblingSwap(g_late)])
    sums_late = _pair_sums(g_late, gots_late, "pair_sums_late")
    (dq, dk, dv, d_bias), parts_late = _att_bwd(q, k, v, bias, dya, [_ChipScatter(sums_late)])
    big = updates(parts_late, sums_late, late)
    dz = jnp.concatenate([dq, dk, dv, dzs, dzg], axis=1)
    g_in = [(_weight_grad(dz, h2, "dw_in"), 0)]
    d_rel, got_in = _rel_bias_grad(d_bias, [_SiblingSwap(g_in)])
    sums_in = _pair_sums(g_in, got_in, "pair_sums_w_in")
    (dx1, d_nm), parts_in = _mix_proj_bwd(dz, x1, nm, dx2, gwm, [_ChipScatter(sums_in)])
    big.update(updates(parts_in, sums_in, ("w_in",)))
    small = {"norm_ffn2": d_n2, "norm_final": d_nf, "b_gate": d_bg, "sgu_ln_g": d_lng, "sgu_ln_b": d_lnb, "sgu_b_s": d_bs,
             "rel_bias": d_rel, "sgu_w_s": d_wm.reshape(_SMALL_2D["sgu_w_s"]), "norm_mix": d_nm}

    (dab, g_down), (*small_parts, loss_parts) = _ffn_bwd_hidden(ab1, dx1, gw1, 2 * R_FF, "ffn1_bwd_hidden",
                                                                [_AllToAll(list(small.values()) + [loss])])
    sums_d = chip_sums([(g_down, 0)], "ffn1_down")
    g_gu, (parts_d,) = _weight_grad(dab, h1, "ffn1_dw_gate_up", comms=[_ChipScatter(sums_d)], mats=2)
    sums_gu = chip_sums([(g_gu, 0), (g_gu, 1)], "ffn1_gate_up")
    (dx0, d_n1), parts_gu = _ffn_bwd_input(x, n1, dab, dx1, gw1, "ffn1_bwd", [_ChipScatter(sums_gu)])
    (n1_parts,) = _comm_only([_AllToAll([d_n1])], "gather_norm_ffn1")
    big.update(updates([parts_d] + parts_gu, sums_d + sums_gu, ("ffn1_w_down", "ffn1_w_gate", "ffn1_w_up")))
    out_s, loss_sum = _adamw_small(dict(zip(small, small_parts), norm_ffn1=n1_parts), loss_parts, p)
    return dx0, loss_sum[0, 0], [{**{n: four[i] for n, four in big.items()}, **s} for i, s in enumerate(out_s)]


_OUT_ORDER = ("norm_ffn1", "ffn1_w_gate", "ffn1_w_up", "ffn1_w_down", "norm_mix", "w_in", "b_gate", "rel_bias", "sgu_ln_g", "sgu_ln_b",
              "sgu_w_s", "sgu_b_s", "w_branch_att", "w_branch_sgu", "w_out", "norm_ffn2", "ffn2_w_gate", "ffn2_w_up", "ffn2_w_down",
              "norm_final")


def kernel(x, norm_ffn1, ffn1_w_gate, ffn1_w_up, ffn1_w_down, norm_mix, w_in, b_gate, rel_bias, sgu_ln_g, sgu_ln_b, sgu_w_s, sgu_b_s, w_branch_att, w_branch_sgu, w_out, norm_ffn2, ffn2_w_gate, ffn2_w_up, ffn2_w_down, norm_final, loss_target, m_norm_ffn1, m_ffn1_w_gate, m_ffn1_w_up, m_ffn1_w_down, m_norm_mix, m_w_in, m_b_gate, m_rel_bias, m_sgu_ln_g, m_sgu_ln_b, m_sgu_w_s, m_sgu_b_s, m_w_branch_att, m_w_branch_sgu, m_w_out, m_norm_ffn2, m_ffn2_w_gate, m_ffn2_w_up, m_ffn2_w_down, m_norm_final, v_norm_ffn1, v_ffn1_w_gate, v_ffn1_w_up, v_ffn1_w_down, v_norm_mix, v_w_in, v_b_gate, v_rel_bias, v_sgu_ln_g, v_sgu_ln_b, v_sgu_w_s, v_sgu_b_s, v_w_branch_att, v_w_branch_sgu, v_w_out, v_norm_ffn2, v_ffn2_w_gate, v_ffn2_w_up, v_ffn2_w_down, v_norm_final):
    args = dict(locals())
    dx, loss, outs = _step(x[0], loss_target[0], {pre + n: args[pre + n] for pre in ("", "m_", "v_") for n in _OUT_ORDER})
    return (loss, dx[None], *[o[n] for o in outs for n in _OUT_ORDER])
```

```python
import functools

import jax
import jax.numpy as jnp
from jax import lax
from jax.experimental import pallas as pl
from jax.experimental.pallas import tpu as pltpu

F32 = jnp.float32
BF16 = jnp.bfloat16

N_DEV = 8
D = 1024
F = 2816
D_ATT = 512
D_SGU = 512
D_IN = 4608
HEADS = 8
CHUNK = 64
N_LEFT = 8
REL_CLIP = 256
N_REL = 2 * REL_CLIP + 1
SGU_BLOCK = 128
EPS = 1e-6
NEG_INF = -1e30
QB = 256
KW = 3 * QB

R_FF, R_IN, R_BR, R_WO = F // N_DEV, D_IN // N_DEV, D // N_DEV, D // N_DEV
OFF_IN, OFF_BR, OFF_WO = 0, R_IN, R_IN + R_BR

FC = 256
TM = 512
VMEM_LIMIT = 56 * 1024 * 1024

ADAM_LR, ADAM_B1, ADAM_B2, ADAM_EPS, ADAM_WD, ADAM_STEP = 0.001, 0.9, 0.999, 1e-08, 0.01, 10

MESH = pl.DeviceIdType.MESH
ANY = pl.BlockSpec(memory_space=pl.ANY)


def _nt(a, b):
    return lax.dot_general(a, b, (((1,), (1,)), ((), ())), preferred_element_type=F32)


def _tn(a, b):
    return lax.dot_general(a, b, (((0,), (0,)), ((), ())), preferred_element_type=F32)


def _nn(a, b):
    return jnp.dot(a, b, preferred_element_type=F32)


def _cparams(sem=None):
    return pltpu.CompilerParams(dimension_semantics=sem, vmem_limit_bytes=VMEM_LIMIT)


def _load_rows(gw_ref, dst, off, rows, sems):
    copies = [pltpu.make_async_copy(gw_ref.at[k, pl.ds(off, rows), :], dst.at[pl.ds(k * rows, rows), :], sems.at[k])
              for k in range(N_DEV)]
    for cp in copies:
        cp.start()
    return copies


def _rms(xv):
    r = lax.rsqrt(jnp.mean(xv * xv, axis=-1, keepdims=True) + EPS)
    return xv * r, r


def _rms_bwd(dh, xn, r, gain):
    dxn = dh * gain
    dx = r * (dxn - xn * jnp.mean(dxn * xn, axis=-1, keepdims=True))
    return dx, jnp.sum(dh * xn, axis=0, keepdims=True)


def _gelu(x):
    t = jnp.tanh(0.7978845608028654 * (x + 0.044715 * x * x * x))
    return 0.5 * x * (1.0 + t), t


def _gelu_grad(x, t):
    return 0.5 * (1.0 + t) + 0.5 * x * (1.0 - t * t) * 0.7978845608028654 * (1.0 + 3.0 * 0.044715 * x * x)


def _place():
    x, y, cc = lax.axis_index("x"), lax.axis_index("y"), lax.axis_index("c")
    return x, y, cc, [(1 - x, y), (x, 1 - y), (1 - x, 1 - y)]


class _Gather:
    def __init__(self, shard):
        self.inputs = [shard]
        self.out_shape = [jax.ShapeDtypeStruct((N_DEV,) + shard.shape, shard.dtype)]
        self.scratch = [pltpu.SemaphoreType.DMA((7,)), pltpu.SemaphoreType.DMA((7,)), pltpu.SemaphoreType.DMA]

    def _copies(self, ins, outs, scr):
        (x_ref,), (out_ref,), (send_sems, recv_sems, local_sem) = ins, outs, scr
        x, y, cc, chips = _place()

        def slab(px, py, pc):
            return out_ref.at[4 * px + 2 * py + pc]

        def copy(k, block, to, src=None):
            return pltpu.make_async_remote_copy(
                src_ref=slab(*block) if src is None else src, dst_ref=slab(*block),
                send_sem=send_sems.at[k], recv_sem=recv_sems.at[k], device_id=to, device_id_type=MESH)

        me, sibling = (x, y, cc), (x, y, 1 - cc)
        x_nbr, y_nbr, diagonal = chips
        mine = pltpu.make_async_copy(x_ref, slab(*me), local_sem)
        first = [copy(0, me, sibling, src=x_ref), copy(1, me, (*x_nbr, cc), src=x_ref), copy(2, me, (*y_nbr, cc), src=x_ref)]
        neighbours = [copy(1, (*x_nbr, cc), me), copy(2, (*y_nbr, cc), me)]
        second_hand = copy(3, (x ^ (1 - cc), y ^ cc, cc), (x ^ cc, y ^ (1 - cc), cc))
        from_diagonal = copy(3, (*diagonal, cc), me)
        passed = [copy(4 + j, (*chip, cc), sibling) for j, chip in enumerate(chips)]
        from_sibling = [copy(0, sibling, me)] + [copy(4 + j, (*chip, 1 - cc), me) for j, chip in enumerate(chips)]
        return mine, first, neighbours, second_hand, from_diagonal, passed, from_sibling

    def begin(self, *refs):
        mine, first = self._copies(*refs)[:2]
        mine.start()
        for cp in first:
            cp.start()

    def mid(self, *refs):
        _, _, neighbours, second_hand, _, passed, _ = self._copies(*refs)
        for cp in neighbours:
            cp.wait_recv()
        second_hand.start()
        passed[0].start()
        passed[1].start()

    def relay(self, *refs):
        _, _, _, _, from_diagonal, passed, _ = self._copies(*refs)
        from_diagonal.wait_recv()
        passed[2].start()

    def end(self, *refs):
        mine, first, _, second_hand, _, passed, from_sibling = self._copies(*refs)
        for cp in from_sibling:
            cp.wait_recv()
        for cp in first + [second_hand] + passed:
            cp.wait_send()
        mine.wait()


class _Direct:
    def begin(self, *refs):
        keep, give = self._copies(*refs)
        for cp in keep + give:
            cp.start()

    def mid(self, *refs):
        pass

    def relay(self, *refs):
        pass

    def end(self, *refs):
        keep, give = self._copies(*refs)
        for cp in give:
            cp.wait_recv()
        for cp in give:
            cp.wait_send()
        for cp in keep:
            cp.wait()


class _SiblingSwap(_Direct):
    def __init__(self, grads):
        n = len(grads)
        self.which = [w for _, w in grads]
        self.inputs = [g for g, _ in grads]
        self.out_shape = [jax.ShapeDtypeStruct((4,) + g.shape[1:], g.dtype) for g, _ in grads]
        self.scratch = [pltpu.SemaphoreType.DMA((n, 4)), pltpu.SemaphoreType.DMA((n, 4))]

    def _copies(self, ins, outs, scr):
        send_sems, recv_sems = scr
        x, y, cc, _ = _place()
        return [], [pltpu.make_async_remote_copy(src_ref=g_ref.at[N_DEV * w + 2 * j + 1 - cc], dst_ref=got_ref.at[j],
                                                 send_sem=send_sems.at[i, j], recv_sem=recv_sems.at[i, j], device_id=(x, y, 1 - cc),
                                                 device_id_type=MESH)
                    for i, (g_ref, got_ref, w) in enumerate(zip(ins, outs, self.which)) for j in range(4)]


class _ChipScatter(_Direct):
    def __init__(self, sums):
        n = len(sums)
        self.inputs = list(sums)
        self.out_shape = [jax.ShapeDtypeStruct((3,) + s.shape[1:], s.dtype) for s in sums]
        self.scratch = [pltpu.SemaphoreType.DMA((n, 3)), pltpu.SemaphoreType.DMA((n, 3))]

    def _copies(self, ins, outs, scr):
        send_sems, recv_sems = scr
        _, _, cc, chips = _place()
        return [], [pltpu.make_async_remote_copy(src_ref=s_ref.at[2 * px + py], dst_ref=got_ref.at[j], send_sem=send_sems.at[i, j],
                                                 recv_sem=recv_sems.at[i, j], device_id=(px, py, cc), device_id_type=MESH)
                    for i, (s_ref, got_ref) in enumerate(zip(ins, outs)) for j, (px, py) in enumerate(chips)]


class _AllToAll(_Direct):
    def __init__(self, blocks):
        n = len(blocks)
        self.inputs = list(blocks)
        self.out_shape = [jax.ShapeDtypeStruct((N_DEV,) + b.shape, b.dtype) for b in blocks]
        self.scratch = [pltpu.SemaphoreType.DMA((n, 7)), pltpu.SemaphoreType.DMA((n, 7)), pltpu.SemaphoreType.DMA((n,))]

    def _copies(self, ins, outs, scr):
        send_sems, recv_sems, local_sems = scr
        x, y, cc, _ = _place()
        me = 4 * x + 2 * y + cc
        keep = [pltpu.make_async_copy(b_ref, got_ref.at[me], local_sems.at[i]) for i, (b_ref, got_ref) in enumerate(zip(ins, outs))]
        give = [pltpu.make_async_remote_copy(src_ref=b_ref, dst_ref=got_ref.at[me], send_sem=send_sems.at[i, k - 1],
                                             recv_sem=recv_sems.at[i, k - 1],
                                             device_id=(x ^ ((k >> 2) & 1), y ^ ((k >> 1) & 1), cc ^ (k & 1)), device_id_type=MESH)
                for i, (b_ref, got_ref) in enumerate(zip(ins, outs)) for k in range(1, N_DEV)]
        return keep, give


def _split_refs(refs, counts):
    out, pos = [], 0
    for n in counts:
        out.append(list(refs[pos:pos + n]))
        pos += n
    return out


def _bind(comms, c_in, c_out, c_scr):
    ins = _split_refs(c_in, [len(c.inputs) for c in comms])
    outs = _split_refs(c_out, [len(c.out_shape) for c in comms])
    scr = _split_refs(c_scr, [len(c.scratch) for c in comms])
    return [(c, (i, o, s)) for c, i, o, s in zip(comms, ins, outs, scr)]


def _call(body, *, name, grid, in_specs, out_specs, out_shape, scratch_shapes, args, comms=()):
    c_in = [a for c in comms for a in c.inputs]
    c_out = [s for c in comms for s in c.out_shape]
    c_scr = [s for c in comms for s in c.scratch]
    counts = [len(in_specs), len(c_in), len(out_shape), len(c_out), len(scratch_shapes), len(c_scr)]

    def full(*refs):
        ins, cin, outs, cout, scr, cscr = _split_refs(refs, counts)
        bound = _bind(comms, cin, cout, cscr)
        if comms:
            def at(steps):
                return functools.reduce(jnp.logical_and, [pl.program_id(ax) == s for ax, s in enumerate(steps)])

            first, last = at([0] * len(grid)), at([n - 1 for n in grid])

            @pl.when(first)
            def _():
                for c, r in bound:
                    c.begin(*r)

            @pl.when(at([(grid[0] - 1) // 2] + [0] * (len(grid) - 1)))
            def _():
                for c, r in bound:
                    c.mid(*r)

            @pl.when(last)
            def _():
                for c, r in bound:
                    c.relay(*r)

        body(*ins, *outs, *scr)
        if comms:
            @pl.when(last)
            def _():
                for c, r in bound:
                    c.end(*r)

    res = pl.pallas_call(
        full, name=name, grid=grid,
        in_specs=list(in_specs) + [ANY] * len(c_in), out_specs=list(out_specs) + [ANY] * len(c_out),
        out_shape=list(out_shape) + c_out, scratch_shapes=list(scratch_shapes) + c_scr,
        compiler_params=_cparams(("arbitrary",) * len(grid)),
    )(*args, *c_in)
    return list(res[:len(out_shape)]), list(res[len(out_shape):])


def _comm_only(comms, name):
    c_in = [a for c in comms for a in c.inputs]
    c_out = [s for c in comms for s in c.out_shape]
    c_scr = [s for c in comms for s in c.scratch]

    def full(*refs):
        cin, cout, cscr = _split_refs(refs, [len(c_in), len(c_out), len(c_scr)])
        bound = _bind(comms, cin, cout, cscr)
        for phase in ("begin", "mid", "relay", "end"):
            for c, r in bound:
                getattr(c, phase)(*r)

    return list(pl.pallas_call(full, name=name, in_specs=[ANY] * len(c_in), out_specs=[ANY] * len(c_out), out_shape=c_out,
                               scratch_shapes=c_scr)(*c_in))


def _my_index(*axes_and_weights):
    return sum(w * lax.axis_index(a) for a, w in axes_and_weights).astype(jnp.int32).reshape(1)


def _pair_sums(grads, gots, name):
    n = len(grads)

    def body(c_ref, *refs):
        for a_ref, b_ref, o_ref in zip(refs[:n], refs[n:2 * n], refs[2 * n:]):
            o_ref[...] = (a_ref[...].astype(F32) + b_ref[...].astype(F32)).astype(BF16)

    def tile(g):
        return pl.BlockSpec((1,) + g.shape[1:], lambda j, c_ref: (j, 0, 0))

    def mine(g, w):
        return pl.BlockSpec((1, None) + g.shape[1:], lambda j, c_ref: (4 * w + j, c_ref[0], 0, 0))

    return list(pl.pallas_call(
        body, name=name,
        grid_spec=pltpu.PrefetchScalarGridSpec(num_scalar_prefetch=1, grid=(4,),
                                               in_specs=[mine(g, w) for g, w in grads] + [tile(g) for g in gots],
                                               out_specs=[tile(g) for g in gots]),
        out_shape=[jax.ShapeDtypeStruct(g.shape, BF16) for g in gots],
        compiler_params=_cparams(("arbitrary",)))(_my_index(("c", 1)), *[g.reshape((-1, 2) + g.shape[1:]) for g, _ in grads], *gots))


def _ffn_fwd(x, gain, gw_gu, gw_d, off_d, name, comms=(), head=None):
    t = x.shape[0]
    n_head = 0 if head is None else 2

    def body(x_ref, g_ref, gu_ref, d_ref, *refs):
        head_in, (o_ref, ab_ref, h_ref), head_out = refs[:n_head], refs[n_head:n_head + 3], refs[n_head + 3:2 * n_head + 3]
        wg, wu, wd, s_scr, sems = refs[2 * n_head + 3:]

        @pl.when(pl.program_id(0) == 0)
        def _():
            cps = _load_rows(gu_ref, wg, 0, R_FF, sems.at[0]) + _load_rows(gu_ref, wu, R_FF, R_FF, sems.at[1]) \
                + _load_rows(d_ref, wd, off_d, R_FF, sems.at[2])
            for o in head_out:
                o[...] = jnp.zeros_like(o)
            for cp in cps:
                cp.wait()

        xv = x_ref[...]
        xn, _ = _rms(xv)
        h = (xn * g_ref[...]).astype(BF16)
        h_ref[...] = h
        for c in range(F // FC):
            rows = pl.ds(c * FC, FC)
            a = _nt(h, wg[rows, :])
            b = _nt(h, wu[rows, :])
            ab_ref[:, c * FC:(c + 1) * FC] = a.astype(BF16)
            ab_ref[:, F + c * FC:F + (c + 1) * FC] = b.astype(BF16)
            s_scr[:, c * FC:(c + 1) * FC] = (a * jax.nn.sigmoid(a) * b).astype(BF16)
        out = xv + 0.5 * _nn(s_scr[...], wd[...])
        if head is None:
            o_ref[...] = out
        else:
            (gf_ref, t_ref), (dg_ref, loss_ref) = head_in, head_out
            gain_f = gf_ref[...]
            yn, r = _rms(out)
            err = yn * gain_f - t_ref[...]
            loss_ref[...] += 0.5 * jnp.sum(jnp.mean(err * err, axis=-1, keepdims=True), axis=0, keepdims=True)
            o_ref[...], dg = _rms_bwd(err * (1.0 / D), yn, r, gain_f)
            dg_ref[...] += dg

    tile = pl.BlockSpec((TM, D), lambda i: (i, 0))
    row = pl.BlockSpec((1, D), lambda i: (0, 0))
    head_specs = [] if head is None else [row, pl.BlockSpec((1, 128), lambda i: (0, 0))]
    head_shapes = [] if head is None else [jax.ShapeDtypeStruct((1, D), F32), jax.ShapeDtypeStruct((1, 128), F32)]
    res, got = _call(
        body, name=name, grid=(t // TM,),
        in_specs=[tile, row, ANY, ANY] + ([] if head is None else [row, tile]),
        out_specs=[tile, pl.BlockSpec((TM, 2 * F), lambda i: (i, 0)), tile] + head_specs,
        out_shape=[jax.ShapeDtypeStruct((t, D), F32), jax.ShapeDtypeStruct((t, 2 * F), BF16), jax.ShapeDtypeStruct((t, D), BF16)] + head_shapes,
        scratch_shapes=[pltpu.VMEM((F, D), BF16)] * 3 + [pltpu.VMEM((TM, F), BF16), pltpu.SemaphoreType.DMA((3, N_DEV))],
        args=(x, gain, gw_gu, gw_d) + (() if head is None else tuple(head)), comms=comms)
    return (res[0], res[1], res[2], got, *res[3:])


def _ffn_bwd_hidden(ab, dout, gw_d, off_d, name, comms=()):
    t = ab.shape[0]
    n_steps = t // TM
    row_chunks = [(r, min(512, F - r)) for r in range(0, F, 512)]

    def tile(w):
        return pl.BlockSpec((TM, w), lambda i: (i, 0))

    def hidden(ab_ref, do_ref, d_ref, dab_ref, dwd_ref, wd, s_scr, acc, sems, out_sem):
        step = pl.program_id(0)

        @pl.when(step == 0)
        def _():
            cps = _load_rows(d_ref, wd, off_d, R_FF, sems)
            acc[...] = jnp.zeros_like(acc)
            for cp in cps:
                cp.wait()

        df = (0.5 * do_ref[...]).astype(BF16)
        for c in range(F // FC):
            a = ab_ref[:, c * FC:(c + 1) * FC].astype(F32)
            b = ab_ref[:, F + c * FC:F + (c + 1) * FC].astype(F32)
            sg = jax.nn.sigmoid(a)
            sl = a * sg
            ds = _nt(df, wd[pl.ds(c * FC, FC), :])
            dab_ref[:, c * FC:(c + 1) * FC] = (ds * b * (sg * (1.0 + a * (1.0 - sg)))).astype(BF16)
            dab_ref[:, F + c * FC:F + (c + 1) * FC] = (ds * sl).astype(BF16)
            s_scr[:, c * FC:(c + 1) * FC] = (sl * b).astype(BF16)
        for r, n in row_chunks:
            acc[r:r + n, :] += _tn(s_scr[:, r:r + n], df)

        @pl.when(step == n_steps - 1)
        def _():
            wd[...] = acc[...].astype(BF16)
            out = pltpu.make_async_copy(wd, dwd_ref, out_sem)
            out.start()
            out.wait()

    (dab, dwd), got = _call(
        hidden, name=name, grid=(n_steps,),
        in_specs=[tile(2 * F), tile(D), ANY], out_specs=[tile(2 * F), ANY],
        out_shape=[jax.ShapeDtypeStruct((t, 2 * F), BF16), jax.ShapeDtypeStruct((F, D), BF16)],
        scratch_shapes=[pltpu.VMEM((F, D), BF16), pltpu.VMEM((TM, F), BF16), pltpu.VMEM((F, D), F32), pltpu.SemaphoreType.DMA((N_DEV,)),
                        pltpu.SemaphoreType.DMA],
        args=(ab, dout, gw_d), comms=comms)
    return (dab, dwd.reshape(N_DEV, R_FF, D)), got


def _ffn_bwd_input(x, gain, dab, dout, gw_gu, name, comms=()):
    t = x.shape[0]

    def body(x_ref, g_ref, dab_ref, do_ref, gu_ref, dx_ref, dg_ref, wgu, sems):
        @pl.when(pl.program_id(0) == 0)
        def _():
            cps = _load_rows(gu_ref, wgu.at[0:F], 0, R_FF, sems.at[0]) + _load_rows(gu_ref, wgu.at[F:2 * F], R_FF, R_FF, sems.at[1])
            dg_ref[...] = jnp.zeros_like(dg_ref)
            for cp in cps:
                cp.wait()

        gain_v = g_ref[...]
        xn, r = _rms(x_ref[...])
        dh = _nn(dab_ref[...], wgu[...])
        dxn, dg = _rms_bwd(dh, xn, r, gain_v)
        dg_ref[...] += dg
        dx_ref[...] = do_ref[...] + dxn

    def tile(w):
        return pl.BlockSpec((TM, w), lambda i: (i, 0))

    row = pl.BlockSpec((1, D), lambda i: (0, 0))
    return _call(
        body, name=name, grid=(t // TM,),
        in_specs=[tile(D), row, tile(2 * F), tile(D), ANY], out_specs=[tile(D), row],
        out_shape=[jax.ShapeDtypeStruct((t, D), F32), jax.ShapeDtypeStruct((1, D), F32)],
        scratch_shapes=[pltpu.VMEM((2 * F, D), BF16), pltpu.SemaphoreType.DMA((2, N_DEV))],
        args=(x, gain, dab, dout, gw_gu), comms=comms)


def _weight_grad(a, b, name, col_off=0, m=None, comms=None, mats=1):
    t = a.shape[0]
    m = a.shape[1] if m is None else m
    n = b.shape[1]
    tmm = 512 if m % 512 == 0 else 256
    first = col_off // tmm

    def body(a_ref, b_ref, o_ref):
        o_ref[...] = _tn(a_ref[...], b_ref[...]).astype(BF16)

    (out,), got = _call(
        body, name=name, grid=(m // tmm,),
        in_specs=[pl.BlockSpec((t, tmm), lambda i: (0, first + i)), pl.BlockSpec((t, n), lambda i: (0, 0))],
        out_specs=[pl.BlockSpec((tmm, n), lambda i: (i, 0))],
        out_shape=[jax.ShapeDtypeStruct((m, n), BF16)], scratch_shapes=[], args=(a, b), comms=comms or ())
    out = out.reshape(mats * N_DEV, m // (mats * N_DEV), n)
    return out if comms is None else (out, got)


def _weight_grad_pieces(pieces, b, name):
    t, n = b.shape
    tmm = 512
    tiles = [(k, j) for k, a in enumerate(pieces) for j in range(a.shape[1] // tmm)]
    m = tmm * len(tiles)

    def body(*refs):
        a_refs, (b_ref, o_ref, slots, b_vmem, sems, b_sem) = refs[:len(pieces)], refs[len(pieces):]

        def fetch(s):
            k, j = tiles[s]
            cp = pltpu.make_async_copy(a_refs[k].at[:, j * tmm:(j + 1) * tmm], slots.at[s % 2], sems.at[s % 2])
            cp.start()
            return cp

        load_b = pltpu.make_async_copy(b_ref, b_vmem, b_sem)
        load_b.start()
        cp = fetch(0)
        load_b.wait()
        for s in range(len(tiles)):
            nxt = fetch(s + 1) if s + 1 < len(tiles) else None
            cp.wait()
            o_ref[s * tmm:(s + 1) * tmm, :] = _tn(slots[s % 2], b_vmem[...]).astype(BF16)
            cp = nxt

    out = pl.pallas_call(
        body, name=name, in_specs=[ANY] * (len(pieces) + 1), out_specs=pl.BlockSpec(memory_space=pltpu.VMEM),
        out_shape=jax.ShapeDtypeStruct((m, n), BF16),
        scratch_shapes=[pltpu.VMEM((2, t, tmm), BF16), pltpu.VMEM((t, n), BF16), pltpu.SemaphoreType.DMA((2,)), pltpu.SemaphoreType.DMA(())],
        compiler_params=_cparams())(*pieces, b)
    return out.reshape(N_DEV, m // N_DEV, n)


def _mix_proj_fwd(x, gain, b_gate, gw, comms=()):
    t = x.shape[0]

    def body(x_ref, g_ref, bg_ref, gw_ref, q_ref, k_ref, v_ref, zs_ref, gt_ref, h_ref, win, sems):
        @pl.when(pl.program_id(0) == 0)
        def _():
            for cp in _load_rows(gw_ref, win, OFF_IN, R_IN, sems):
                cp.wait()

        xn, _ = _rms(x_ref[...])
        h = (xn * g_ref[...]).astype(BF16)
        h_ref[...] = h
        z = _nt(h, win[...])
        q_ref[...] = (z[:, 0:D_ATT] * 0.125).astype(BF16)
        k_ref[...] = z[:, D_ATT:2 * D_ATT].astype(BF16)
        v_ref[...] = z[:, 2 * D_ATT:3 * D_ATT].astype(BF16)
        zs_ref[...] = z[:, 3 * D_ATT:3 * D_ATT + 2 * D_SGU].astype(BF16)
        gt_ref[...] = jax.nn.sigmoid(z[:, 3 * D_ATT + 2 * D_SGU:] + bg_ref[...]).astype(BF16)

    def tile(w):
        return pl.BlockSpec((TM, w), lambda i: (i, 0))

    return _call(
        body, name="mix_proj_fwd", grid=(t // TM,),
        in_specs=[tile(D), pl.BlockSpec((1, D), lambda i: (0, 0)), pl.BlockSpec((1, 2 * D), lambda i: (0, 0)), ANY],
        out_specs=[tile(D_ATT), tile(D_ATT), tile(D_ATT), tile(2 * D_SGU), tile(2 * D), tile(D)],
        out_shape=[jax.ShapeDtypeStruct((t, D_ATT), BF16)] * 3 + [jax.ShapeDtypeStruct((t, 2 * D_SGU), BF16),
                                                                   jax.ShapeDtypeStruct((t, 2 * D), BF16),
                                                                   jax.ShapeDtypeStruct((t, D), BF16)],
        scratch_shapes=[pltpu.VMEM((D_IN, D), BF16), pltpu.SemaphoreType.DMA((N_DEV,))],
        args=(x, gain, b_gate, gw), comms=comms)


def _mix_proj_bwd(dz_pieces, x, gain, dres, gw, comms=()):
    t = x.shape[0]
    widths = [a.shape[1] for a in dz_pieces]
    offs = [sum(widths[:k]) for k in range(len(widths))]

    def body(*refs):
        dz_refs, (x_ref, g_ref, dr_ref, gw_ref, dx_ref, dg_ref, win, sems) = refs[:len(widths)], refs[len(widths):]

        @pl.when(pl.program_id(0) == 0)
        def _():
            cps = _load_rows(gw_ref, win, OFF_IN, R_IN, sems)
            dg_ref[...] = jnp.zeros_like(dg_ref)
            for cp in cps:
                cp.wait()

        dh = sum(_nn(r[...], win[o:o + w, :]) for r, o, w in zip(dz_refs, offs, widths))
        xn, r = _rms(x_ref[...])
        dxn, dg = _rms_bwd(dh, xn, r, g_ref[...])
        dg_ref[...] += dg
        dx_ref[...] = dr_ref[...] + dxn

    def tile(w):
        return pl.BlockSpec((TM, w), lambda i: (i, 0))

    row = pl.BlockSpec((1, D), lambda i: (0, 0))
    return _call(
        body, name="mix_proj_bwd", grid=(t // TM,),
        in_specs=[tile(w) for w in widths] + [tile(D), row, tile(D), ANY], out_specs=[tile(D), row],
        out_shape=[jax.ShapeDtypeStruct((t, D), F32), jax.ShapeDtypeStruct((1, D), F32)],
        scratch_shapes=[pltpu.VMEM((D_IN, D), BF16), pltpu.SemaphoreType.DMA((N_DEV,))],
        args=(*dz_pieces, x, gain, dres, gw), comms=comms)


SKEW_W = KW + QB
N_CAP = 2 * QB - REL_CLIP + 1


def _tables(rel_bias, w_s, b_s):
    cap = rel_bias[:, 2 * REL_CLIP:]
    diag = jnp.concatenate([jnp.broadcast_to(cap, (HEADS, N_CAP)), rel_bias[:, 2 * REL_CLIP - 1::-1],
                            jnp.broadcast_to(cap, (HEADS, SKEW_W - N_CAP - 2 * REL_CLIP))], axis=1)

    def fill(ins, outs):
        (d_ref, ws_ref, bs_ref), (bias_ref, wm2_ref, wmt2_ref, bsx_ref) = ins, outs
        lag = lax.broadcasted_iota(jnp.int32, (QB, KW), 1) // CHUNK - lax.broadcasted_iota(jnp.int32, (QB, KW), 0) // CHUNK
        band = (lag >= 0) & (lag <= N_LEFT)
        for h in range(HEADS):
            rows = jnp.broadcast_to(d_ref[h:h + 1, :], (QB, SKEW_W))
            bias_ref[h] = jnp.where(band, pltpu.roll(rows, 0, 1, stride=1, stride_axis=0)[:, :KW], NEG_INF)
        i = lax.broadcasted_iota(jnp.int32, (SGU_BLOCK, SGU_BLOCK), 0)
        j = lax.broadcasted_iota(jnp.int32, (SGU_BLOCK, SGU_BLOCK), 1)
        causal = (i // CHUNK) >= (j // CHUNK)
        for pair in range(4):
            w0, w1 = jnp.where(causal, ws_ref[2 * pair], 0.0), jnp.where(causal, ws_ref[2 * pair + 1], 0.0)
            wm2_ref[pair] = jnp.concatenate([w0, w1], axis=1).astype(BF16)
            wmt2_ref[pair] = jnp.concatenate([w0.T, w1.T], axis=1).astype(BF16)
        by_row = jnp.concatenate([bs_ref[...], jnp.zeros((SGU_BLOCK - 8, SGU_BLOCK), F32)], axis=0).T
        group = lax.broadcasted_iota(jnp.int32, (SGU_BLOCK, D_SGU), 1) // 64
        spread = jnp.zeros((SGU_BLOCK, D_SGU), F32)
        for g in range(8):
            spread = jnp.where(group == g, by_row[:, g:g + 1], spread)
        bsx_ref[...] = spread

    shapes = [jax.ShapeDtypeStruct((HEADS, QB, KW), F32), jax.ShapeDtypeStruct((4, SGU_BLOCK, 2 * SGU_BLOCK), BF16),
              jax.ShapeDtypeStruct((4, SGU_BLOCK, 2 * SGU_BLOCK), BF16), jax.ShapeDtypeStruct((SGU_BLOCK, D_SGU), F32)]
    return [diag, w_s, b_s], shapes, fill


def _att_specs():
    qspec = pl.BlockSpec((QB, D_ATT), lambda g: (g, 0))
    kspecs = [pl.BlockSpec((QB, D_ATT), lambda g: (jnp.maximum(g - 2, 0), 0)),
              pl.BlockSpec((QB, D_ATT), lambda g: (jnp.maximum(g - 1, 0), 0)), qspec]
    bspec = pl.BlockSpec((HEADS, QB, KW), lambda g: (0, 0, 0))
    return qspec, kspecs, bspec


def _att_probs(qm, kp, bias, valid):
    s = jnp.where(valid, _nt(qm, kp) + bias, NEG_INF)
    e = jnp.exp(s - jnp.max(s, axis=-1, keepdims=True))
    return e / jnp.sum(e, axis=-1, keepdims=True)


def _att_valid():
    g = pl.program_id(0)
    blk = lax.broadcasted_iota(jnp.int32, (QB, KW), 1) // QB
    return (blk + g) >= 2


def _att_fwd(q, k, v, bias, comms=()):
    t = q.shape[0]

    def body(q_ref, k0, k1, k2, v0, v1, v2, b_ref, y_ref):
        valid = _att_valid()
        first = lax.broadcasted_iota(jnp.int32, (1, 128), 1) < 64
        for p in range(HEADS // 2):
            lanes = slice(p * 128, (p + 1) * 128)
            qp = q_ref[:, lanes]
            kp = jnp.concatenate([k0[:, lanes], k1[:, lanes], k2[:, lanes]], axis=0)
            vp = jnp.concatenate([v0[:, lanes], v1[:, lanes], v2[:, lanes]], axis=0)
            out = jnp.zeros((QB, 128), F32)
            for hh in range(2):
                mask = first if hh == 0 else jnp.logical_not(first)
                pr = _att_probs(jnp.where(mask, qp, 0), kp, b_ref[2 * p + hh], valid)
                out = out + _nn(pr.astype(BF16), jnp.where(mask, vp, 0))
            y_ref[:, lanes] = out.astype(BF16)

    qspec, kspecs, bspec = _att_specs()
    (out,), got = _call(
        body, name="att_fwd", grid=(t // QB,),
        in_specs=[qspec] + kspecs + kspecs + [bspec], out_specs=[qspec],
        out_shape=[jax.ShapeDtypeStruct((t, D_ATT), BF16)], scratch_shapes=[],
        args=(q, k, k, k, v, v, v, bias), comms=comms)
    return out, got


def _att_bwd(q, k, v, bias, dy, comms=()):
    t = q.shape[0]
    n_blocks = t // QB

    def body(q_ref, k0, k1, k2, v0, v1, v2, b_ref, dy_ref, dq_ref, dk_ref, dv_ref, db_ref, dk_acc, dv_acc):
        g = pl.program_id(0)

        @pl.when(g == 0)
        def _():
            db_ref[...] = jnp.zeros_like(db_ref)
            dk_acc[...] = jnp.zeros_like(dk_acc)
            dv_acc[...] = jnp.zeros_like(dv_acc)

        valid = _att_valid()
        first = lax.broadcasted_iota(jnp.int32, (1, 128), 1) < 64
        for p in range(HEADS // 2):
            lanes = slice(p * 128, (p + 1) * 128)
            qp = q_ref[:, lanes]
            dyp = dy_ref[:, lanes]
            kp = jnp.concatenate([k0[:, lanes], k1[:, lanes], k2[:, lanes]], axis=0)
            vp = jnp.concatenate([v0[:, lanes], v1[:, lanes], v2[:, lanes]], axis=0)
            dq = jnp.zeros((QB, 128), F32)
            dk = jnp.zeros((KW, 128), F32)
            dv = jnp.zeros((KW, 128), F32)
            for hh in range(2):
                mask = first if hh == 0 else jnp.logical_not(first)
                qm = jnp.where(mask, qp, 0)
                dym = jnp.where(mask, dyp, 0)
                pr = _att_probs(qm, kp, b_ref[2 * p + hh], valid)
                dp = _nt(dym, vp)
                ds = pr * (dp - jnp.sum(dp * pr, axis=-1, keepdims=True))
                db_ref[2 * p + hh] += ds
                dsb = ds.astype(BF16)
                dq = dq + _nn(dsb, jnp.where(mask, kp, 0))
                dk = dk + _tn(dsb, qm)
                dv = dv + _tn(pr.astype(BF16), dym)
            dq_ref[:, lanes] = (dq * 0.125).astype(BF16)
            for j in range(3):
                rows = pl.ds(pl.multiple_of(jnp.maximum(g - 2 + j, 0) * QB, QB), QB)
                dk_acc[rows, lanes] += dk[j * QB:(j + 1) * QB]
                dv_acc[rows, lanes] += dv[j * QB:(j + 1) * QB]

        @pl.when(g == n_blocks - 1)
        def _():
            dk_ref[...] = dk_acc[...].astype(BF16)
            dv_ref[...] = dv_acc[...].astype(BF16)

    qspec, kspecs, bspec = _att_specs()
    full = pl.BlockSpec((t, D_ATT), lambda g: (0, 0))
    return _call(
        body, name="att_bwd", grid=(n_blocks,),
        in_specs=[qspec] + kspecs + kspecs + [bspec, qspec], out_specs=[qspec, full, full, bspec],
        out_shape=[jax.ShapeDtypeStruct((t, D_ATT), BF16)] * 3 + [jax.ShapeDtypeStruct((HEADS, QB, KW), F32)],
        scratch_shapes=[pltpu.VMEM((t, D_ATT), F32)] * 2,
        args=(q, k, k, k, v, v, v, bias, dy), comms=comms)


def _rel_bias_grad(dbias, comms=()):
    c_in = [a for c in comms for a in c.inputs]
    c_out = [s for c in comms for s in c.out_shape]
    c_scr = [s for c in comms for s in c.scratch]

    def body(db_ref, *refs):
        cin, (cs_ref, tot_ref), cout, cscr = _split_refs(refs, [len(c_in), 2, len(c_out), len(c_scr)])
        bound = _bind(comms, cin, cout, cscr)
        for c, r in bound:
            c.begin(*r)
        lane = lax.broadcasted_iota(jnp.int32, (1, SKEW_W), 1)
        capped = (lane < N_CAP) | (lane > KW)
        pad = jnp.zeros((8, QB), F32)
        for h in range(HEADS):
            z = jnp.concatenate([db_ref[h, 0:8, :], pad], axis=1)
            for a in range(1, QB // 8):
                z = z + pltpu.roll(jnp.concatenate([db_ref[h, 8 * a:8 * a + 8, :], pad], axis=1), SKEW_W - 8 * a, 1)
            cs = z[0:1, :]
            for b in range(1, 8):
                cs = cs + pltpu.roll(z[b:b + 1, :], SKEW_W - b, 1)
            cs_ref[h:h + 1, :] = cs
            tot_ref[h:h + 1, :] = jnp.broadcast_to(jnp.sum(jnp.where(capped, cs, 0.0), axis=1, keepdims=True), (1, 128))
        for phase in ("mid", "relay", "end"):
            for c, r in bound:
                getattr(c, phase)(*r)

    whole = pl.BlockSpec(memory_space=pltpu.VMEM)
    cs, tot, *got = pl.pallas_call(
        body, name="rel_bias_grad", in_specs=[whole] + [ANY] * len(c_in), out_specs=[whole, whole] + [ANY] * len(c_out),
        out_shape=[jax.ShapeDtypeStruct((HEADS, SKEW_W), F32), jax.ShapeDtypeStruct((HEADS, 128), F32)] + c_out, scratch_shapes=c_scr,
    )(dbias, *c_in)
    return jnp.concatenate([cs[:, KW:N_CAP - 1:-1], tot[:, :1]], axis=1), got


def _group_stack(blk, first):
    return jnp.concatenate([jnp.where(first, blk, 0), jnp.where(first, 0, blk)], axis=0)


def _sgu_norm(zs_ref, lng, lnb):
    zs = zs_ref[...].astype(F32)
    ga, th = _gelu(zs)
    u, vs = ga[:, :D_SGU], ga[:, D_SGU:]
    mu = jnp.mean(vs, axis=-1, keepdims=True)
    cen = vs - mu
    rstd = lax.rsqrt(jnp.mean(cen * cen, axis=-1, keepdims=True) + EPS)
    xhat = cen * rstd
    return zs, th, u, xhat, rstd, xhat * lng + lnb


def _sgu_mix(vb, wm2_ref, bsx, s_ref):
    first = lax.broadcasted_iota(jnp.int32, (1, 128), 1) < 64
    for n in range(TM // SGU_BLOCK):
        for p in range(4):
            blk = vb[n * 128:(n + 1) * 128, p * 128:(p + 1) * 128]
            s_ref[n * 128:(n + 1) * 128, p * 128:(p + 1) * 128] = _nn(wm2_ref[p], _group_stack(blk, first)) + bsx[:, p * 128:(p + 1) * 128]


def _merge_fwd(x, zs, gt, y_att, lng, lnb, wm2, bsx, gw):
    t = x.shape[0]

    def body(x_ref, zs_ref, gt_ref, ya_ref, lng_ref, lnb_ref, wm2_ref, bsx_ref, gw_ref, xo_ref, ys_ref, mg_ref,
             wbr, wo, s_scr, sems):
        @pl.when(pl.program_id(0) == 0)
        def _():
            for cp in _load_rows(gw_ref, wbr, OFF_BR, R_BR, sems.at[0]) + _load_rows(gw_ref, wo, OFF_WO, R_WO, sems.at[1]):
                cp.wait()

        _, _, u, _, _, vsn = _sgu_norm(zs_ref, lng_ref[...], lnb_ref[...])
        _sgu_mix(vsn.astype(BF16), wm2_ref, bsx_ref[...], s_scr)
        ys = (u * s_scr[...]).astype(BF16)
        ys_ref[...] = ys
        pa = _nt(ya_ref[...], wbr[:, :D_ATT])
        ps = _nt(ys, wbr[:, D_ATT:])
        mg = (gt_ref[:, :D].astype(F32) * pa + gt_ref[:, D:].astype(F32) * ps).astype(BF16)
        mg_ref[...] = mg
        xo_ref[...] = x_ref[...] + _nn(mg, wo[...])

    def tile(w):
        return pl.BlockSpec((TM, w), lambda i: (i, 0))

    def const(shape):
        return pl.BlockSpec(shape, lambda i: (0,) * len(shape))

    return pl.pallas_call(
        body, name="merge_fwd", grid=(t // TM,),
        in_specs=[tile(D), tile(2 * D_SGU), tile(2 * D), tile(D_ATT), const((1, D_SGU)), const((1, D_SGU)),
                  const((4, 128, 256)), const((128, D_SGU)), ANY],
        out_specs=[tile(D), tile(D_SGU), tile(D)],
        out_shape=[jax.ShapeDtypeStruct((t, D), F32), jax.ShapeDtypeStruct((t, D_SGU), BF16), jax.ShapeDtypeStruct((t, D), BF16)],
        scratch_shapes=[pltpu.VMEM((D, D), BF16), pltpu.VMEM((D, D), BF16), pltpu.VMEM((TM, D_SGU), F32),
                        pltpu.SemaphoreType.DMA((2, N_DEV))],
        compiler_params=_cparams(("arbitrary",)),
    )(x, zs, gt, y_att, lng, lnb, wm2, bsx, gw)


def _merge_bwd(dx, gt, y_att, y_sgu, merged, gw):
    t = dx.shape[0]
    n_steps = t // TM

    def body(dx_ref, gt_ref, ya_ref, ys_ref, mg_ref, gw_ref, dzg_ref, dya_ref, dys_ref, dbg_ref, ga_ref, gs_ref, go_ref,
             wbr, wo, acc_a, acc_s, acc_o, sems):
        step = pl.program_id(0)

        @pl.when(step == 0)
        def _():
            cps = _load_rows(gw_ref, wbr, OFF_BR, R_BR, sems.at[0]) + _load_rows(gw_ref, wo, OFF_WO, R_WO, sems.at[1])
            dbg_ref[...] = jnp.zeros_like(dbg_ref)
            for acc in (acc_a, acc_s, acc_o):
                acc[...] = jnp.zeros_like(acc)
            for cp in cps:
                cp.wait()

        dxb = dx_ref[...].astype(BF16)
        acc_o[...] += _tn(mg_ref[...], dxb)
        dm = _nt(dxb, wo[...])
        for half, y_ref, w, acc in ((0, ya_ref, wbr.at[:, :D_ATT], acc_a), (1, ys_ref, wbr.at[:, D_ATT:], acc_s)):
            cols = slice(half * D, (half + 1) * D)
            gate = gt_ref[:, cols].astype(F32)
            branch = _nt(y_ref[...], w[...])
            dzg = dm * branch * gate * (1.0 - gate)
            dbg_ref[:, cols] += jnp.sum(dzg, axis=0, keepdims=True)
            dzg_ref[:, cols] = dzg.astype(BF16)
            dbr = (dm * gate).astype(BF16)
            acc[...] += _tn(dbr, y_ref[...])
            dy = _nn(dbr, w[...])
            if half == 0:
                dya_ref[...] = dy.astype(BF16)
            else:
                dys_ref[...] = dy

        @pl.when(step == n_steps - 1)
        def _():
            ga_ref[...] = acc_a[...].astype(BF16)
            gs_ref[...] = acc_s[...].astype(BF16)
            go_ref[...] = acc_o[...].astype(BF16)

    def tile(w):
        return pl.BlockSpec((TM, w), lambda i: (i, 0))

    def whole(r, c):
        return pl.BlockSpec((r, c), lambda i: (0, 0))

    dzg, dya, dys, dbg, g_ba, g_bs, g_wo = pl.pallas_call(
        body, name="merge_bwd", grid=(n_steps,),
        in_specs=[tile(D), tile(2 * D), tile(D_ATT), tile(D_SGU), tile(D), ANY],
        out_specs=[tile(2 * D), tile(D_ATT), tile(D_SGU), whole(1, 2 * D), whole(D, D_ATT), whole(D, D_SGU), whole(D, D)],
        out_shape=[jax.ShapeDtypeStruct((t, 2 * D), BF16), jax.ShapeDtypeStruct((t, D_ATT), BF16), jax.ShapeDtypeStruct((t, D_SGU), F32),
                   jax.ShapeDtypeStruct((1, 2 * D), F32), jax.ShapeDtypeStruct((D, D_ATT), BF16), jax.ShapeDtypeStruct((D, D_SGU), BF16),
                   jax.ShapeDtypeStruct((D, D), BF16)],
        scratch_shapes=[pltpu.VMEM((D, D), BF16), pltpu.VMEM((D, D), BF16), pltpu.VMEM((D, D_ATT), F32), pltpu.VMEM((D, D_SGU), F32),
                        pltpu.VMEM((D, D), F32), pltpu.SemaphoreType.DMA((2, N_DEV))],
        compiler_params=_cparams(("arbitrary",)),
    )(dx, gt, y_att, y_sgu, merged, gw)
    return dzg, dya, dys, dbg, g_ba.reshape(N_DEV, R_BR, D_ATT), g_bs.reshape(N_DEV, R_BR, D_SGU), g_wo.reshape(N_DEV, R_WO, D)


def _sgu_bwd(zs, dys, lng, lnb, wm2, wmt2, bsx, comms=()):
    t = zs.shape[0]
    n_steps = t // TM

    def body(zs_ref, dys_ref, lng_ref, lnb_ref, wm2_ref, wmt2_ref, bsx_ref, dzs_ref, dw_ref, dbs_ref, dlg_ref, dlb_ref,
             s_scr, dv_scr, ds_acc):
        i = pl.program_id(0)

        @pl.when(i == 0)
        def _():
            dw_ref[...] = jnp.zeros_like(dw_ref)
            dlg_ref[...] = jnp.zeros_like(dlg_ref)
            dlb_ref[...] = jnp.zeros_like(dlb_ref)
            ds_acc[...] = jnp.zeros_like(ds_acc)

        lng = lng_ref[...]
        zs, th, u, xhat, rstd, vsn = _sgu_norm(zs_ref, lng, lnb_ref[...])
        vb = vsn.astype(BF16)
        _sgu_mix(vb, wm2_ref, bsx_ref[...], s_scr)
        dys = dys_ref[...]
        du = dys * s_scr[...]
        ds = dys * u
        dsb = ds.astype(BF16)
        first = lax.broadcasted_iota(jnp.int32, (1, 128), 1) < 64
        acc = jnp.zeros((SGU_BLOCK, D_SGU), F32)
        for n in range(TM // SGU_BLOCK):
            rows = slice(n * 128, (n + 1) * 128)
            acc = acc + ds[rows]
            for p in range(4):
                lanes = slice(p * 128, (p + 1) * 128)
                stack = _group_stack(dsb[rows, lanes], first)
                dv_scr[rows, lanes] = _nn(wmt2_ref[p], stack)
                dw_ref[p] += _nt(stack, vb[rows, lanes])
        ds_acc[...] += acc
        dvsn = dv_scr[...]
        dlg_ref[...] += jnp.sum(dvsn * xhat, axis=0, keepdims=True)
        dlb_ref[...] += jnp.sum(dvsn, axis=0, keepdims=True)
        dxh = dvsn * lng
        dvs = rstd * (dxh - jnp.mean(dxh, axis=-1, keepdims=True) - xhat * jnp.mean(dxh * xhat, axis=-1, keepdims=True))
        dga = jnp.concatenate([du, dvs], axis=1)
        dzs_ref[...] = (dga * _gelu_grad(zs, th)).astype(BF16)

        @pl.when(i == n_steps - 1)
        def _():
            r = lax.broadcasted_iota(jnp.int32, (256, 128), 0) % SGU_BLOCK
            c = lax.broadcasted_iota(jnp.int32, (256, 128), 1)
            keep = (r // CHUNK) >= (c // CHUNK)
            for p in range(4):
                dw_ref[p] = jnp.where(keep, dw_ref[p], 0.0)
            total = ds_acc[...]
            grp = lax.broadcasted_iota(jnp.int32, (SGU_BLOCK, D_SGU), 1) // 64
            lane = lax.broadcasted_iota(jnp.int32, (SGU_BLOCK, 128), 1)
            out = jnp.zeros((SGU_BLOCK, 128), F32)
            for gi in range(8):
                out = jnp.where(lane == gi, jnp.sum(jnp.where(grp == gi, total, 0.0), axis=1, keepdims=True), out)
            dbs_ref[...] = out

    def tile(w):
        return pl.BlockSpec((TM, w), lambda i: (i, 0))

    def const(shape):
        return pl.BlockSpec(shape, lambda i: (0,) * len(shape))

    return _call(
        body, name="sgu_bwd", grid=(n_steps,),
        in_specs=[tile(2 * D_SGU), tile(D_SGU), const((1, D_SGU)), const((1, D_SGU)), const((4, 128, 256)), const((4, 128, 256)),
                  const((128, D_SGU))],
        out_specs=[tile(2 * D_SGU), const((4, 256, 128)), const((128, 128)), const((1, D_SGU)), const((1, D_SGU))],
        out_shape=[jax.ShapeDtypeStruct((t, 2 * D_SGU), BF16), jax.ShapeDtypeStruct((4, 256, 128), F32),
                   jax.ShapeDtypeStruct((128, 128), F32), jax.ShapeDtypeStruct((1, D_SGU), F32), jax.ShapeDtypeStruct((1, D_SGU), F32)],
        scratch_shapes=[pltpu.VMEM((TM, D_SGU), F32), pltpu.VMEM((TM, D_SGU), F32), pltpu.VMEM((SGU_BLOCK, D_SGU), F32)],
        args=(zs, dys, lng, lnb, wm2, wmt2, bsx), comms=comms)


def _adamw(g, w, m, v):
    m = ADAM_B1 * m + (1.0 - ADAM_B1) * g
    v = ADAM_B2 * v + (1.0 - ADAM_B2) * (g * g)
    m_hat = m / (1.0 - ADAM_B1 ** ADAM_STEP)
    v_hat = v / (1.0 - ADAM_B2 ** ADAM_STEP)
    return -ADAM_LR * (m_hat / (jnp.sqrt(v_hat) + ADAM_EPS) + ADAM_WD * w), m, v


def _adamw_matrices(items, name):
    n = len(items)
    c = items[0][0].shape[2]
    tc = 256

    def body(own_ref, *refs):
        for i, (_, _, _, _, _, transposed) in enumerate(items):
            p_ref, s_ref, w_ref, m_ref, v_ref = refs[5 * i:5 * i + 5]
            g = p_ref[0].astype(F32) + p_ref[1].astype(F32) + p_ref[2].astype(F32) + s_ref[...].astype(F32)
            g = g.T if transposed else g
            res = (g,) + _adamw(g, w_ref[...], m_ref[...], v_ref[...])
            for o_ref, val in zip(refs[5 * n + 4 * i:5 * n + 4 * i + 4], res):
                o_ref[...] = val

    in_specs, out_specs, out_shape, args = [], [], [], []
    for parts, sums, w, m, v, transposed in items:
        r = parts.shape[1]
        own = pl.BlockSpec((None, tc, r), lambda i, o: (0, i, 0)) if transposed else pl.BlockSpec((None, r, tc), lambda i, o: (0, 0, i))
        in_specs += [pl.BlockSpec((3, r, tc), lambda i, o: (0, 0, i)), pl.BlockSpec((None, r, tc), lambda i, o: (o[0], 0, i)), own, own, own]
        out_specs += [own] * 4
        out_shape += [jax.ShapeDtypeStruct(w.shape, F32)] * 4
        args += [parts, sums, w, m, v]
    res = pl.pallas_call(
        body, name=name,
        grid_spec=pltpu.PrefetchScalarGridSpec(num_scalar_prefetch=1, grid=(c // tc,), in_specs=in_specs, out_specs=out_specs),
        out_shape=out_shape, compiler_params=_cparams(("arbitrary",)),
    )(_my_index(("x", 2), ("y", 1)), *args)
    return [list(res[4 * i:4 * i + 4]) for i in range(n)]


_SMALL_2D = {"norm_ffn1": (1, D), "norm_mix": (1, D), "norm_ffn2": (1, D), "norm_final": (1, D), "b_gate": (1, 2 * D),
             "sgu_ln_g": (1, D_SGU), "sgu_ln_b": (1, D_SGU), "sgu_b_s": (8, SGU_BLOCK), "rel_bias": (HEADS, N_REL),
             "sgu_w_s": (8 * SGU_BLOCK, SGU_BLOCK)}


def _adamw_small(parts, loss_parts, p):
    names = list(parts)
    n = len(names)

    def body(*refs):
        got, loss_got, wmv, outs, loss_out = refs[:n], refs[n], refs[n + 1:4 * n + 1], refs[4 * n + 1:8 * n + 1], refs[8 * n + 1]
        for i, name in enumerate(names):
            g = got[i][0]
            for k in range(1, N_DEV):
                g = g + got[i][k]
            if name == "sgu_b_s":
                g = g.T[0:8, :]
            res = (g,) + _adamw(g, wmv[3 * i][...], wmv[3 * i + 1][...], wmv[3 * i + 2][...])
            for o_ref, val in zip(outs[4 * i:4 * i + 4], res):
                o_ref[...] = val
        total = loss_got[0]
        for k in range(1, N_DEV):
            total = total + loss_got[k]
        loss_out[...] = total

    wmv = [p[pre + name].reshape(_SMALL_2D[name]) for name in names for pre in ("", "m_", "v_")]
    res = pl.pallas_call(
        body, name="adamw_small",
        out_shape=[jax.ShapeDtypeStruct(_SMALL_2D[name], F32) for name in names for _ in range(4)] + [jax.ShapeDtypeStruct((1, 128), F32)],
        compiler_params=_cparams())(*[parts[name] for name in names], loss_parts, *wmv)
    return [{name: res[4 * i + j].reshape(p[name].shape) for i, name in enumerate(names)} for j in range(4)], res[-1]


def _pack_rows(slabs, name, comms=(), extra=None):
    flat = [a for groups in slabs for grp in groups for a, _ in grp]
    rows = [[grp[0][0].shape[2] if grp[0][1] else grp[0][0].shape[1] for grp in groups] for groups in slabs]
    e_in, e_out, fill = extra if extra is not None else ([], [], None)
    c_in = [a for c in comms for a in c.inputs]
    c_out = [s for c in comms for s in c.out_shape]
    c_scr = [s for c in comms for s in c.scratch]

    def body(*refs):
        ins, ein, cin, outs, eout, cout, cscr = _split_refs(
            refs, [len(flat), len(e_in), len(c_in), len(slabs), len(e_out), len(c_out), len(c_scr)])
        bound = _bind(comms, cin, cout, cscr)
        for c, r in bound:
            c.begin(*r)
        pos = 0
        for groups, slab_rows, o_ref in zip(slabs, rows, outs):
            off = 0
            for grp, r in zip(groups, slab_rows):
                vals = []
                for _, transposed in grp:
                    val = ins[pos][0]
                    vals.append(val.T if transposed else val)
                    pos += 1
                o_ref[off:off + r, :] = (vals[0] if len(vals) == 1 else jnp.concatenate(vals, axis=1)).astype(BF16)
                off += r
        if fill is not None:
            fill(ein, eout)
        for phase in ("mid", "relay", "end"):
            for c, r in bound:
                getattr(c, phase)(*r)

    whole = pl.BlockSpec(memory_space=pltpu.VMEM)
    n_plain = len(slabs) + len(e_out)
    res = pl.pallas_call(
        body, name=name, in_specs=[whole] * (len(flat) + len(e_in)) + [ANY] * len(c_in), out_specs=[whole] * n_plain + [ANY] * len(c_out),
        out_shape=[jax.ShapeDtypeStruct((sum(r), D), BF16) for r in rows] + list(e_out) + c_out, scratch_shapes=c_scr,
        compiler_params=_cparams())(*flat, *e_in, *c_in)
    return list(res[:len(slabs)]), list(res[len(slabs):n_plain]), list(res[n_plain:])


def _step(x, target, p):
    n1, nm, n2 = p["norm_ffn1"], p["norm_mix"], p["norm_ffn2"]
    nf = p["norm_final"].reshape(1, D)
    lng, lnb = p["sgu_ln_g"], p["sgu_ln_b"]

    def chip_sums(grads, name):
        gots = _comm_only([_SiblingSwap(grads)], "swap_" + name)
        return _pair_sums(grads, gots, "pair_sums_" + name)

    def as_rows(a):
        return jnp.swapaxes(a, 1, 2)

    def updates(parts, sums, names):
        res = {}
        for cols in sorted({pt.shape[2] for pt in parts}):
            items, group = [], [(pt, sm, n) for pt, sm, n in zip(parts, sums, names) if pt.shape[2] == cols]
            for pt, sm, n in group:
                view = as_rows if p[n].shape[1:] != pt.shape[1:] and p[n].shape[2] > 128 else (lambda a: a)
                transposed = p[n].shape[1:] != pt.shape[1:] and p[n].shape[2] <= 128
                items.append((pt, sm, view(p[n]), view(p["m_" + n]), view(p["v_" + n]), transposed))
            for (pt, sm, n), four in zip(group, _adamw_matrices(items, "adamw_" + group[0][2])):
                res[n] = [as_rows(o) if o.shape != p[n].shape else o for o in four]
        return res

    (rows1,), _, _ = _pack_rows([[[(as_rows(p["ffn1_w_gate"]), False)], [(as_rows(p["ffn1_w_up"]), False)], [(p["ffn1_w_down"], False)]]],
                                "pack_ffn1")
    (rows_m, rows2d, rows2gu), (bias, wm2, wmt2, bsx), (gw1,) = _pack_rows(
        [[[(as_rows(p["w_in"]), False)], [(p["w_branch_att"], True), (p["w_branch_sgu"], True)], [(p["w_out"], False)]],
         [[(p["ffn2_w_down"], False)]],
         [[(as_rows(p["ffn2_w_gate"]), False)], [(as_rows(p["ffn2_w_up"]), False)]]],
        "gather_ffn1_pack_rest", [_Gather(rows1)], _tables(p["rel_bias"][0], p["sgu_w_s"][0], p["sgu_b_s"][0]))
    x1, ab1, h1, (gwm,) = _ffn_fwd(x, n1, gw1, gw1, 2 * R_FF, "ffn1_fwd", [_Gather(rows_m)])
    (q, k, v, zs, gt, h2), (gw2d,) = _mix_proj_fwd(x1, nm, p["b_gate"], gwm, [_Gather(rows2d)])
    y_att, (gw2gu,) = _att_fwd(q, k, v, bias, [_Gather(rows2gu)])
    x2, y_sgu, merged = _merge_fwd(x1, zs, gt, y_att, lng, lnb, wm2, bsx, gwm)
    dx3, ab2, hb, _, d_nf, loss = _ffn_fwd(x2, n2, gw2gu, gw2d, 0, "ffn2_fwd", head=(nf, target))

    (dab, g_down), _ = _ffn_bwd_hidden(ab2, dx3, gw2d, 0, "ffn2_bwd_hidden")
    (dx2, d_n2), _ = _ffn_bwd_input(x2, n2, dab, dx3, gw2gu, "ffn2_bwd")
    g_gu = _weight_grad(dab, hb, "ffn2_dw_gate_up", mats=2)
    dzg, dya, dys, d_bg, g_ba, g_bs, g_wo = _merge_bwd(dx2, gt, y_att, y_sgu, merged, gwm)
    g_late = [(g_gu, 0), (g_gu, 1), (g_down, 0), (g_ba, 0), (g_bs, 0), (g_wo, 0)]
    late = ("ffn2_w_gate", "ffn2_w_up", "ffn2_w_down", "w_branch_att", "w_branch_sgu", "w_out")
    (dzs, d_wm, d_bs, d_lng, d_lnb), gots_late = _sgu_bwd(zs, dys, lng, lnb, wm2, wmt2, bsx, [_SiblingSwap(g_late)])
    sums_late = _pair_sums(g_late, gots_late, "pair_sums_late")
    (dq, dk, dv, d_bias), parts_late = _att_bwd(q, k, v, bias, dya, [_ChipScatter(sums_late)])
    big = updates(parts_late, sums_late, late)
    dz = [dq, dk, dv, dzs, dzg]
    g_in = [(_weight_grad_pieces(dz, h2, "dw_in"), 0)]
    d_rel, got_in = _rel_bias_grad(d_bias, [_SiblingSwap(g_in)])
    sums_in = _pair_sums(g_in, got_in, "pair_sums_w_in")
    (dx1, d_nm), parts_in = _mix_proj_bwd(dz, x1, nm, dx2, gwm, [_ChipScatter(sums_in)])
    big.update(updates(parts_in, sums_in, ("w_in",)))
    small = {"norm_ffn2": d_n2, "norm_final": d_nf, "b_gate": d_bg, "sgu_ln_g": d_lng, "sgu_ln_b": d_lnb, "sgu_b_s": d_bs,
             "rel_bias": d_rel, "sgu_w_s": d_wm.reshape(_SMALL_2D["sgu_w_s"]), "norm_mix": d_nm}

    (dab, g_down), (*small_parts, loss_parts) = _ffn_bwd_hidden(ab1, dx1, gw1, 2 * R_FF, "ffn1_bwd_hidden",
                                                                [_AllToAll(list(small.values()) + [loss])])
    sums_d = chip_sums([(g_down, 0)], "ffn1_down")
    g_gu, (parts_d,) = _weight_grad(dab, h1, "ffn1_dw_gate_up", comms=[_ChipScatter(sums_d)], mats=2)
    sums_gu = chip_sums([(g_gu, 0), (g_gu, 1)], "ffn1_gate_up")
    (dx0, d_n1), parts_gu = _ffn_bwd_input(x, n1, dab, dx1, gw1, "ffn1_bwd", [_ChipScatter(sums_gu)])
    (n1_parts,) = _comm_only([_AllToAll([d_n1])], "gather_norm_ffn1")
    big.update(updates([parts_d] + parts_gu, sums_d + sums_gu, ("ffn1_w_down", "ffn1_w_gate", "ffn1_w_up")))
    out_s, loss_sum = _adamw_small(dict(zip(small, small_parts), norm_ffn1=n1_parts), loss_parts, p)
    return dx0, loss_sum[0, 0], [{**{n: four[i] for n, four in big.items()}, **s} for i, s in enumerate(out_s)]


_OUT_ORDER = ("norm_ffn1", "ffn1_w_gate", "ffn1_w_up", "ffn1_w_down", "norm_mix", "w_in", "b_gate", "rel_bias", "sgu_ln_g", "sgu_ln_b",
              "sgu_w_s", "sgu_b_s", "w_branch_att", "w_branch_sgu", "w_out", "norm_ffn2", "ffn2_w_gate", "ffn2_w_up", "ffn2_w_down",
              "norm_final")


def kernel(x, norm_ffn1, ffn1_w_gate, ffn1_w_up, ffn1_w_down, norm_mix, w_in, b_gate, rel_bias, sgu_ln_g, sgu_ln_b, sgu_w_s, sgu_b_s, w_branch_att, w_branch_sgu, w_out, norm_ffn2, ffn2_w_gate, ffn2_w_up, ffn2_w_down, norm_final, loss_target, m_norm_ffn1, m_ffn1_w_gate, m_ffn1_w_up, m_ffn1_w_down, m_norm_mix, m_w_in, m_b_gate, m_rel_bias, m_sgu_ln_g, m_sgu_ln_b, m_sgu_w_s, m_sgu_b_s, m_w_branch_att, m_w_branch_sgu, m_w_out, m_norm_ffn2, m_ffn2_w_gate, m_ffn2_w_up, m_ffn2_w_down, m_norm_final, v_norm_ffn1, v_ffn1_w_gate, v_ffn1_w_up, v_ffn1_w_down, v_norm_mix, v_w_in, v_b_gate, v_rel_bias, v_sgu_ln_g, v_sgu_ln_b, v_sgu_w_s, v_sgu_b_s, v_w_branch_att, v_w_branch_sgu, v_w_out, v_norm_ffn2, v_ffn2_w_gate, v_ffn2_w_up, v_ffn2_w_down, v_norm_final):
    args = dict(locals())
    dx, loss, outs = _step(x[0], loss_target[0], {pre + n: args[pre + n] for pre in ("", "m_", "v_") for n in _OUT_ORDER})
    return (loss, dx[None], *[o[n] for o in outs for n in _OUT_ORDER])
```

```python
import functools

import jax
import jax.numpy as jnp
from jax import lax
from jax.experimental import pallas as pl
from jax.experimental.pallas import tpu as pltpu

F32 = jnp.float32
BF16 = jnp.bfloat16

N_DEV = 8
D = 1024
F = 2816
D_ATT = 512
D_SGU = 512
D_IN = 4608
HEADS = 8
CHUNK = 64
N_LEFT = 8
REL_CLIP = 256
N_REL = 2 * REL_CLIP + 1
SGU_BLOCK = 128
EPS = 1e-6
NEG_INF = -1e30
QB = 256
KW = 3 * QB

R_FF, R_IN, R_BR, R_WO = F // N_DEV, D_IN // N_DEV, D // N_DEV, D // N_DEV
OFF_IN, OFF_BR, OFF_WO = 0, R_IN, R_IN + R_BR

FC = 256
TM = 512
VMEM_LIMIT = 56 * 1024 * 1024

ADAM_LR, ADAM_B1, ADAM_B2, ADAM_EPS, ADAM_WD, ADAM_STEP = 0.001, 0.9, 0.999, 1e-08, 0.01, 10

MESH = pl.DeviceIdType.MESH
ANY = pl.BlockSpec(memory_space=pl.ANY)


def _nt(a, b):
    return lax.dot_general(a, b, (((1,), (1,)), ((), ())), preferred_element_type=F32)


def _tn(a, b):
    return lax.dot_general(a, b, (((0,), (0,)), ((), ())), preferred_element_type=F32)


def _nn(a, b):
    return jnp.dot(a, b, preferred_element_type=F32)


def _cparams(sem=None):
    return pltpu.CompilerParams(dimension_semantics=sem, vmem_limit_bytes=VMEM_LIMIT)


def _load_rows(gw_ref, dst, off, rows, sems):
    copies = [pltpu.make_async_copy(gw_ref.at[k, pl.ds(off, rows), :], dst.at[pl.ds(k * rows, rows), :], sems.at[k])
              for k in range(N_DEV)]
    for cp in copies:
        cp.start()
    return copies


def _rms(xv):
    r = lax.rsqrt(jnp.mean(xv * xv, axis=-1, keepdims=True) + EPS)
    return xv * r, r


def _rms_bwd(dh, xn, r, gain):
    dxn = dh * gain
    dx = r * (dxn - xn * jnp.mean(dxn * xn, axis=-1, keepdims=True))
    return dx, jnp.sum(dh * xn, axis=0, keepdims=True)


def _gelu(x):
    t = jnp.tanh(0.7978845608028654 * (x + 0.044715 * x * x * x))
    return 0.5 * x * (1.0 + t), t


def _gelu_grad(x, t):
    return 0.5 * (1.0 + t) + 0.5 * x * (1.0 - t * t) * 0.7978845608028654 * (1.0 + 3.0 * 0.044715 * x * x)


def _place():
    x, y, cc = lax.axis_index("x"), lax.axis_index("y"), lax.axis_index("c")
    return x, y, cc, [(1 - x, y), (x, 1 - y), (1 - x, 1 - y)]


class _Gather:
    def __init__(self, shard):
        self.inputs = [shard]
        self.out_shape = [jax.ShapeDtypeStruct((N_DEV,) + shard.shape, shard.dtype)]
        self.scratch = [pltpu.SemaphoreType.DMA((7,)), pltpu.SemaphoreType.DMA((7,)), pltpu.SemaphoreType.DMA]

    def _copies(self, ins, outs, scr):
        (x_ref,), (out_ref,), (send_sems, recv_sems, local_sem) = ins, outs, scr
        x, y, cc, chips = _place()

        def slab(px, py, pc):
            return out_ref.at[4 * px + 2 * py + pc]

        def copy(k, block, to, src=None):
            return pltpu.make_async_remote_copy(
                src_ref=slab(*block) if src is None else src, dst_ref=slab(*block),
                send_sem=send_sems.at[k], recv_sem=recv_sems.at[k], device_id=to, device_id_type=MESH)

        me, sibling = (x, y, cc), (x, y, 1 - cc)
        x_nbr, y_nbr, diagonal = chips
        mine = pltpu.make_async_copy(x_ref, slab(*me), local_sem)
        first = [copy(0, me, sibling, src=x_ref), copy(1, me, (*x_nbr, cc), src=x_ref), copy(2, me, (*y_nbr, cc), src=x_ref)]
        neighbours = [copy(1, (*x_nbr, cc), me), copy(2, (*y_nbr, cc), me)]
        second_hand = copy(3, (x ^ (1 - cc), y ^ cc, cc), (x ^ cc, y ^ (1 - cc), cc))
        from_diagonal = copy(3, (*diagonal, cc), me)
        passed = [copy(4 + j, (*chip, cc), sibling) for j, chip in enumerate(chips)]
        from_sibling = [copy(0, sibling, me)] + [copy(4 + j, (*chip, 1 - cc), me) for j, chip in enumerate(chips)]
        return mine, first, neighbours, second_hand, from_diagonal, passed, from_sibling

    def begin(self, *refs):
        mine, first = self._copies(*refs)[:2]
        mine.start()
        for cp in first:
            cp.start()

    def mid(self, *refs):
        _, _, neighbours, second_hand, _, passed, _ = self._copies(*refs)
        for cp in neighbours:
            cp.wait_recv()
        second_hand.start()
        passed[0].start()
        passed[1].start()

    def relay(self, *refs):
        _, _, _, _, from_diagonal, passed, _ = self._copies(*refs)
        from_diagonal.wait_recv()
        passed[2].start()

    def end(self, *refs):
        mine, first, _, second_hand, _, passed, from_sibling = self._copies(*refs)
        for cp in from_sibling:
            cp.wait_recv()
        for cp in first + [second_hand] + passed:
            cp.wait_send()
        mine.wait()


class _Direct:
    def begin(self, *refs):
        keep, give = self._copies(*refs)
        for cp in keep + give:
            cp.start()

    def mid(self, *refs):
        pass

    def relay(self, *refs):
        pass

    def end(self, *refs):
        keep, give = self._copies(*refs)
        for cp in give:
            cp.wait_recv()
        for cp in give:
            cp.wait_send()
        for cp in keep:
            cp.wait()


class _SiblingSwap(_Direct):
    def __init__(self, grads):
        n = len(grads)
        self.which = [w for _, w in grads]
        self.inputs = [g for g, _ in grads]
        self.out_shape = [jax.ShapeDtypeStruct((4,) + g.shape[1:], g.dtype) for g, _ in grads]
        self.scratch = [pltpu.SemaphoreType.DMA((n, 4)), pltpu.SemaphoreType.DMA((n, 4))]

    def _copies(self, ins, outs, scr):
        send_sems, recv_sems = scr
        x, y, cc, _ = _place()
        return [], [pltpu.make_async_remote_copy(src_ref=g_ref.at[N_DEV * w + 2 * j + 1 - cc], dst_ref=got_ref.at[j],
                                                 send_sem=send_sems.at[i, j], recv_sem=recv_sems.at[i, j], device_id=(x, y, 1 - cc),
                                                 device_id_type=MESH)
                    for i, (g_ref, got_ref, w) in enumerate(zip(ins, outs, self.which)) for j in range(4)]


class _ChipScatter(_Direct):
    def __init__(self, sums):
        n = len(sums)
        self.inputs = list(sums)
        self.out_shape = [jax.ShapeDtypeStruct((3,) + s.shape[1:], s.dtype) for s in sums]
        self.scratch = [pltpu.SemaphoreType.DMA((n, 3)), pltpu.SemaphoreType.DMA((n, 3))]

    def _copies(self, ins, outs, scr):
        send_sems, recv_sems = scr
        _, _, cc, chips = _place()
        return [], [pltpu.make_async_remote_copy(src_ref=s_ref.at[2 * px + py], dst_ref=got_ref.at[j], send_sem=send_sems.at[i, j],
                                                 recv_sem=recv_sems.at[i, j], device_id=(px, py, cc), device_id_type=MESH)
                    for i, (s_ref, got_ref) in enumerate(zip(ins, outs)) for j, (px, py) in enumerate(chips)]


class _AllToAll(_Direct):
    def __init__(self, blocks):
        n = len(blocks)
        self.inputs = list(blocks)
        self.out_shape = [jax.ShapeDtypeStruct((N_DEV,) + b.shape, b.dtype) for b in blocks]
        self.scratch = [pltpu.SemaphoreType.DMA((n, 7)), pltpu.SemaphoreType.DMA((n, 7)), pltpu.SemaphoreType.DMA((n,))]

    def _copies(self, ins, outs, scr):
        send_sems, recv_sems, local_sems = scr
        x, y, cc, _ = _place()
        me = 4 * x + 2 * y + cc
        keep = [pltpu.make_async_copy(b_ref, got_ref.at[me], local_sems.at[i]) for i, (b_ref, got_ref) in enumerate(zip(ins, outs))]
        give = [pltpu.make_async_remote_copy(src_ref=b_ref, dst_ref=got_ref.at[me], send_sem=send_sems.at[i, k - 1],
                                             recv_sem=recv_sems.at[i, k - 1],
                                             device_id=(x ^ ((k >> 2) & 1), y ^ ((k >> 1) & 1), cc ^ (k & 1)), device_id_type=MESH)
                for i, (b_ref, got_ref) in enumerate(zip(ins, outs)) for k in range(1, N_DEV)]
        return keep, give


def _split_refs(refs, counts):
    out, pos = [], 0
    for n in counts:
        out.append(list(refs[pos:pos + n]))
        pos += n
    return out


def _bind(comms, c_in, c_out, c_scr):
    ins = _split_refs(c_in, [len(c.inputs) for c in comms])
    outs = _split_refs(c_out, [len(c.out_shape) for c in comms])
    scr = _split_refs(c_scr, [len(c.scratch) for c in comms])
    return [(c, (i, o, s)) for c, i, o, s in zip(comms, ins, outs, scr)]


def _call(body, *, name, grid, in_specs, out_specs, out_shape, scratch_shapes, args, comms=()):
    c_in = [a for c in comms for a in c.inputs]
    c_out = [s for c in comms for s in c.out_shape]
    c_scr = [s for c in comms for s in c.scratch]
    counts = [len(in_specs), len(c_in), len(out_shape), len(c_out), len(scratch_shapes), len(c_scr)]

    def full(*refs):
        ins, cin, outs, cout, scr, cscr = _split_refs(refs, counts)
        bound = _bind(comms, cin, cout, cscr)
        if comms:
            def at(steps):
                return functools.reduce(jnp.logical_and, [pl.program_id(ax) == s for ax, s in enumerate(steps)])

            first, last = at([0] * len(grid)), at([n - 1 for n in grid])

            @pl.when(first)
            def _():
                for c, r in bound:
                    c.begin(*r)

            @pl.when(at([(grid[0] - 1) // 2] + [0] * (len(grid) - 1)))
            def _():
                for c, r in bound:
                    c.mid(*r)

            @pl.when(last)
            def _():
                for c, r in bound:
                    c.relay(*r)

        body(*ins, *outs, *scr)
        if comms:
            @pl.when(last)
            def _():
                for c, r in bound:
                    c.end(*r)

    res = pl.pallas_call(
        full, name=name, grid=grid,
        in_specs=list(in_specs) + [ANY] * len(c_in), out_specs=list(out_specs) + [ANY] * len(c_out),
        out_shape=list(out_shape) + c_out, scratch_shapes=list(scratch_shapes) + c_scr,
        compiler_params=_cparams(("arbitrary",) * len(grid)),
    )(*args, *c_in)
    return list(res[:len(out_shape)]), list(res[len(out_shape):])


def _comm_only(comms, name):
    c_in = [a for c in comms for a in c.inputs]
    c_out = [s for c in comms for s in c.out_shape]
    c_scr = [s for c in comms for s in c.scratch]

    def full(*refs):
        cin, cout, cscr = _split_refs(refs, [len(c_in), len(c_out), len(c_scr)])
        bound = _bind(comms, cin, cout, cscr)
        for phase in ("begin", "mid", "relay", "end"):
            for c, r in bound:
                getattr(c, phase)(*r)

    return list(pl.pallas_call(full, name=name, in_specs=[ANY] * len(c_in), out_specs=[ANY] * len(c_out), out_shape=c_out,
                               scratch_shapes=c_scr)(*c_in))


def _my_index(*axes_and_weights):
    return sum(w * lax.axis_index(a) for a, w in axes_and_weights).astype(jnp.int32).reshape(1)


def _pair_sums(grads, gots, name):
    n = len(grads)

    def body(c_ref, *refs):
        for a_ref, b_ref, o_ref in zip(refs[:n], refs[n:2 * n], refs[2 * n:]):
            o_ref[...] = (a_ref[...].astype(F32) + b_ref[...].astype(F32)).astype(BF16)

    def tile(g):
        return pl.BlockSpec((1,) + g.shape[1:], lambda j, c_ref: (j, 0, 0))

    def mine(g, w):
        return pl.BlockSpec((1, None) + g.shape[1:], lambda j, c_ref: (4 * w + j, c_ref[0], 0, 0))

    return list(pl.pallas_call(
        body, name=name,
        grid_spec=pltpu.PrefetchScalarGridSpec(num_scalar_prefetch=1, grid=(4,),
                                               in_specs=[mine(g, w) for g, w in grads] + [tile(g) for g in gots],
                                               out_specs=[tile(g) for g in gots]),
        out_shape=[jax.ShapeDtypeStruct(g.shape, BF16) for g in gots],
        compiler_params=_cparams(("arbitrary",)))(_my_index(("c", 1)), *[g.reshape((-1, 2) + g.shape[1:]) for g, _ in grads], *gots))


def _ffn_fwd(x, gain, gw_gu, gw_d, off_d, name, comms=(), head=None):
    t = x.shape[0]
    n_head = 0 if head is None else 2

    def body(x_ref, g_ref, gu_ref, d_ref, *refs):
        head_in, (o_ref, ab_ref, h_ref), head_out = refs[:n_head], refs[n_head:n_head + 3], refs[n_head + 3:2 * n_head + 3]
        wg, wu, wd, s_scr, sems = refs[2 * n_head + 3:]

        @pl.when(pl.program_id(0) == 0)
        def _():
            cps = _load_rows(gu_ref, wg, 0, R_FF, sems.at[0]) + _load_rows(gu_ref, wu, R_FF, R_FF, sems.at[1]) \
                + _load_rows(d_ref, wd, off_d, R_FF, sems.at[2])
            for o in head_out:
                o[...] = jnp.zeros_like(o)
            for cp in cps:
                cp.wait()

        xv = x_ref[...]
        xn, _ = _rms(xv)
        h = (xn * g_ref[...]).astype(BF16)
        h_ref[...] = h
        for c in range(F // FC):
            rows = pl.ds(c * FC, FC)
            a = _nt(h, wg[rows, :])
            b = _nt(h, wu[rows, :])
            ab_ref[:, c * FC:(c + 1) * FC] = a.astype(BF16)
            ab_ref[:, F + c * FC:F + (c + 1) * FC] = b.astype(BF16)
            s_scr[:, c * FC:(c + 1) * FC] = (a * jax.nn.sigmoid(a) * b).astype(BF16)
        out = xv + 0.5 * _nn(s_scr[...], wd[...])
        if head is None:
            o_ref[...] = out
        else:
            (gf_ref, t_ref), (dg_ref, loss_ref) = head_in, head_out
            gain_f = gf_ref[...]
            yn, r = _rms(out)
            err = yn * gain_f - t_ref[...]
            loss_ref[...] += 0.5 * jnp.sum(jnp.mean(err * err, axis=-1, keepdims=True), axis=0, keepdims=True)
            o_ref[...], dg = _rms_bwd(err * (1.0 / D), yn, r, gain_f)
            dg_ref[...] += dg

    tile = pl.BlockSpec((TM, D), lambda i: (i, 0))
    row = pl.BlockSpec((1, D), lambda i: (0, 0))
    head_specs = [] if head is None else [row, pl.BlockSpec((1, 128), lambda i: (0, 0))]
    head_shapes = [] if head is None else [jax.ShapeDtypeStruct((1, D), F32), jax.ShapeDtypeStruct((1, 128), F32)]
    res, got = _call(
        body, name=name, grid=(t // TM,),
        in_specs=[tile, row, ANY, ANY] + ([] if head is None else [row, tile]),
        out_specs=[tile, pl.BlockSpec((TM, 2 * F), lambda i: (i, 0)), tile] + head_specs,
        out_shape=[jax.ShapeDtypeStruct((t, D), F32), jax.ShapeDtypeStruct((t, 2 * F), BF16), jax.ShapeDtypeStruct((t, D), BF16)] + head_shapes,
        scratch_shapes=[pltpu.VMEM((F, D), BF16)] * 3 + [pltpu.VMEM((TM, F), BF16), pltpu.SemaphoreType.DMA((3, N_DEV))],
        args=(x, gain, gw_gu, gw_d) + (() if head is None else tuple(head)), comms=comms)
    return (res[0], res[1], res[2], got, *res[3:])


def _ffn_bwd_hidden(ab, dout, gw_d, off_d, name, comms=()):
    t = ab.shape[0]
    n_steps = t // TM
    row_chunks = [(r, min(512, F - r)) for r in range(0, F, 512)]

    def tile(w):
        return pl.BlockSpec((TM, w), lambda i: (i, 0))

    def hidden(ab_ref, do_ref, d_ref, dab_ref, dwd_ref, wd, s_scr, acc, sems, out_sem):
        step = pl.program_id(0)

        @pl.when(step == 0)
        def _():
            cps = _load_rows(d_ref, wd, off_d, R_FF, sems)
            acc[...] = jnp.zeros_like(acc)
            for cp in cps:
                cp.wait()

        df = (0.5 * do_ref[...]).astype(BF16)
        for c in range(F // FC):
            a = ab_ref[:, c * FC:(c + 1) * FC].astype(F32)
            b = ab_ref[:, F + c * FC:F + (c + 1) * FC].astype(F32)
            sg = jax.nn.sigmoid(a)
            sl = a * sg
            ds = _nt(df, wd[pl.ds(c * FC, FC), :])
            dab_ref[:, c * FC:(c + 1) * FC] = (ds * b * (sg * (1.0 + a * (1.0 - sg)))).astype(BF16)
            dab_ref[:, F + c * FC:F + (c + 1) * FC] = (ds * sl).astype(BF16)
            s_scr[:, c * FC:(c + 1) * FC] = (sl * b).astype(BF16)
        for r, n in row_chunks:
            acc[r:r + n, :] += _tn(s_scr[:, r:r + n], df)

        @pl.when(step == n_steps - 1)
        def _():
            wd[...] = acc[...].astype(BF16)
            out = pltpu.make_async_copy(wd, dwd_ref, out_sem)
            out.start()
            out.wait()

    (dab, dwd), got = _call(
        hidden, name=name, grid=(n_steps,),
        in_specs=[tile(2 * F), tile(D), ANY], out_specs=[tile(2 * F), ANY],
        out_shape=[jax.ShapeDtypeStruct((t, 2 * F), BF16), jax.ShapeDtypeStruct((F, D), BF16)],
        scratch_shapes=[pltpu.VMEM((F, D), BF16), pltpu.VMEM((TM, F), BF16), pltpu.VMEM((F, D), F32), pltpu.SemaphoreType.DMA((N_DEV,)),
                        pltpu.SemaphoreType.DMA],
        args=(ab, dout, gw_d), comms=comms)
    return (dab, dwd.reshape(N_DEV, R_FF, D)), got


def _ffn_bwd_input(x, gain, dab, dout, gw_gu, name, comms=()):
    t = x.shape[0]

    def body(x_ref, g_ref, dab_ref, do_ref, gu_ref, dx_ref, dg_ref, wgu, sems):
        @pl.when(pl.program_id(0) == 0)
        def _():
            cps = _load_rows(gu_ref, wgu.at[0:F], 0, R_FF, sems.at[0]) + _load_rows(gu_ref, wgu.at[F:2 * F], R_FF, R_FF, sems.at[1])
            dg_ref[...] = jnp.zeros_like(dg_ref)
            for cp in cps:
                cp.wait()

        gain_v = g_ref[...]
        xn, r = _rms(x_ref[...])
        dh = _nn(dab_ref[...], wgu[...])
        dxn, dg = _rms_bwd(dh, xn, r, gain_v)
        dg_ref[...] += dg
        dx_ref[...] = do_ref[...] + dxn

    def tile(w):
        return pl.BlockSpec((TM, w), lambda i: (i, 0))

    row = pl.BlockSpec((1, D), lambda i: (0, 0))
    return _call(
        body, name=name, grid=(t // TM,),
        in_specs=[tile(D), row, tile(2 * F), tile(D), ANY], out_specs=[tile(D), row],
        out_shape=[jax.ShapeDtypeStruct((t, D), F32), jax.ShapeDtypeStruct((1, D), F32)],
        scratch_shapes=[pltpu.VMEM((2 * F, D), BF16), pltpu.SemaphoreType.DMA((2, N_DEV))],
        args=(x, gain, dab, dout, gw_gu), comms=comms)


def _weight_grad(a, b, name, mats=1):
    t, m = a.shape
    n = b.shape[1]
    tmm = 512

    def body(a_ref, b_ref, o_ref):
        o_ref[...] = _tn(a_ref[...], b_ref[...]).astype(BF16)

    (out,), _ = _call(
        body, name=name, grid=(m // tmm,),
        in_specs=[pl.BlockSpec((t, tmm), lambda i: (0, i)), pl.BlockSpec((t, n), lambda i: (0, 0))],
        out_specs=[pl.BlockSpec((tmm, n), lambda i: (i, 0))],
        out_shape=[jax.ShapeDtypeStruct((m, n), BF16)], scratch_shapes=[], args=(a, b))
    return out.reshape(mats * N_DEV, m // (mats * N_DEV), n)


def _weight_grad_pieces(pieces, b, name, comms=()):
    t, n = b.shape
    tmm = 512
    tiles = [(k, lo, min(tmm, off + width - lo)) for k, (_, off, width) in enumerate(pieces) for lo in range(off, off + width, tmm)]
    rows = [sum(w for _, _, w in tiles[:s]) for s in range(len(tiles) + 1)]
    m = rows[-1]
    c_in = [a for c in comms for a in c.inputs]
    c_out = [s for c in comms for s in c.out_shape]
    c_scr = [s for c in comms for s in c.scratch]

    def body(*refs):
        a_refs, (b_ref,), cin, (o_ref,), cout, (slots, b_vmem, sems, b_sem), cscr = _split_refs(
            refs, [len(pieces), 1, len(c_in), 1, len(c_out), 4, len(c_scr)])
        bound = _bind(comms, cin, cout, cscr)

        def fetch(s):
            k, lo, w = tiles[s]
            cp = pltpu.make_async_copy(a_refs[k].at[:, lo:lo + w], slots.at[s % 2, :, 0:w], sems.at[s % 2])
            cp.start()
            return cp

        load_b = pltpu.make_async_copy(b_ref, b_vmem, b_sem)
        load_b.start()
        cp = fetch(0)
        for c, r in bound:
            c.begin(*r)
        load_b.wait()
        for s, (_, _, w) in enumerate(tiles):
            nxt = fetch(s + 1) if s + 1 < len(tiles) else None
            cp.wait()
            o_ref[rows[s]:rows[s + 1], :] = _tn(slots[s % 2, :, 0:w], b_vmem[...]).astype(BF16)
            cp = nxt
            if s == (len(tiles) - 1) // 2:
                for c, r in bound:
                    c.mid(*r)
        for phase in ("relay", "end"):
            for c, r in bound:
                getattr(c, phase)(*r)

    res = pl.pallas_call(
        body, name=name, in_specs=[ANY] * (len(pieces) + 1 + len(c_in)),
        out_specs=[pl.BlockSpec(memory_space=pltpu.VMEM)] + [ANY] * len(c_out),
        out_shape=[jax.ShapeDtypeStruct((m, n), BF16)] + c_out,
        scratch_shapes=[pltpu.VMEM((2, t, tmm), BF16), pltpu.VMEM((t, n), BF16), pltpu.SemaphoreType.DMA((2,)),
                        pltpu.SemaphoreType.DMA(())] + c_scr,
        compiler_params=_cparams())(*[a for a, _, _ in pieces], b, *c_in)
    return res[0].reshape(N_DEV, m // N_DEV, n), list(res[1:])


def _mix_proj_fwd(x, gain, b_gate, gw, comms=()):
    t = x.shape[0]

    def body(x_ref, g_ref, bg_ref, gw_ref, q_ref, k_ref, v_ref, zs_ref, gt_ref, h_ref, win, sems):
        @pl.when(pl.program_id(0) == 0)
        def _():
            for cp in _load_rows(gw_ref, win, OFF_IN, R_IN, sems):
                cp.wait()

        xn, _ = _rms(x_ref[...])
        h = (xn * g_ref[...]).astype(BF16)
        h_ref[...] = h
        z = _nt(h, win[...])
        q_ref[...] = (z[:, 0:D_ATT] * 0.125).astype(BF16)
        k_ref[...] = z[:, D_ATT:2 * D_ATT].astype(BF16)
        v_ref[...] = z[:, 2 * D_ATT:3 * D_ATT].astype(BF16)
        zs_ref[...] = z[:, 3 * D_ATT:3 * D_ATT + 2 * D_SGU].astype(BF16)
        gt_ref[...] = jax.nn.sigmoid(z[:, 3 * D_ATT + 2 * D_SGU:] + bg_ref[...]).astype(BF16)

    def tile(w):
        return pl.BlockSpec((TM, w), lambda i: (i, 0))

    return _call(
        body, name="mix_proj_fwd", grid=(t // TM,),
        in_specs=[tile(D), pl.BlockSpec((1, D), lambda i: (0, 0)), pl.BlockSpec((1, 2 * D), lambda i: (0, 0)), ANY],
        out_specs=[tile(D_ATT), tile(D_ATT), tile(D_ATT), tile(2 * D_SGU), tile(2 * D), tile(D)],
        out_shape=[jax.ShapeDtypeStruct((t, D_ATT), BF16)] * 3 + [jax.ShapeDtypeStruct((t, 2 * D_SGU), BF16),
                                                                   jax.ShapeDtypeStruct((t, 2 * D), BF16),
                                                                   jax.ShapeDtypeStruct((t, D), BF16)],
        scratch_shapes=[pltpu.VMEM((D_IN, D), BF16), pltpu.SemaphoreType.DMA((N_DEV,))],
        args=(x, gain, b_gate, gw), comms=comms)


def _mix_proj_bwd(dz_pieces, x, gain, dres, gw, comms=()):
    t = x.shape[0]
    widths = [a.shape[1] for a in dz_pieces]
    offs = [sum(widths[:k]) for k in range(len(widths))]

    def body(*refs):
        dz_refs, (x_ref, g_ref, dr_ref, gw_ref, dx_ref, dg_ref, win, sems) = refs[:len(widths)], refs[len(widths):]

        @pl.when(pl.program_id(0) == 0)
        def _():
            cps = _load_rows(gw_ref, win, OFF_IN, R_IN, sems)
            dg_ref[...] = jnp.zeros_like(dg_ref)
            for cp in cps:
                cp.wait()

        dh = sum(_nn(r[...], win[o:o + w, :]) for r, o, w in zip(dz_refs, offs, widths))
        xn, r = _rms(x_ref[...])
        dxn, dg = _rms_bwd(dh, xn, r, g_ref[...])
        dg_ref[...] += dg
        dx_ref[...] = dr_ref[...] + dxn

    def tile(w):
        return pl.BlockSpec((TM, w), lambda i: (i, 0))

    row = pl.BlockSpec((1, D), lambda i: (0, 0))
    return _call(
        body, name="mix_proj_bwd", grid=(t // TM,),
        in_specs=[tile(w) for w in widths] + [tile(D), row, tile(D), ANY], out_specs=[tile(D), row],
        out_shape=[jax.ShapeDtypeStruct((t, D), F32), jax.ShapeDtypeStruct((1, D), F32)],
        scratch_shapes=[pltpu.VMEM((D_IN, D), BF16), pltpu.SemaphoreType.DMA((N_DEV,))],
        args=(*dz_pieces, x, gain, dres, gw), comms=comms)


SKEW_W = KW + QB
N_CAP = 2 * QB - REL_CLIP + 1


def _tables(rel_bias, w_s, b_s):
    cap = rel_bias[:, 2 * REL_CLIP:]
    diag = jnp.concatenate([jnp.broadcast_to(cap, (HEADS, N_CAP)), rel_bias[:, 2 * REL_CLIP - 1::-1],
                            jnp.broadcast_to(cap, (HEADS, SKEW_W - N_CAP - 2 * REL_CLIP))], axis=1)

    def fill(ins, outs):
        (d_ref, ws_ref, bs_ref), (bias_ref, wm2_ref, wmt2_ref, bsx_ref) = ins, outs
        lag = lax.broadcasted_iota(jnp.int32, (QB, KW), 1) // CHUNK - lax.broadcasted_iota(jnp.int32, (QB, KW), 0) // CHUNK
        band = (lag >= 0) & (lag <= N_LEFT)
        for h in range(HEADS):
            rows = jnp.broadcast_to(d_ref[h:h + 1, :], (QB, SKEW_W))
            bias_ref[h] = jnp.where(band, pltpu.roll(rows, 0, 1, stride=1, stride_axis=0)[:, :KW], NEG_INF)
        i = lax.broadcasted_iota(jnp.int32, (SGU_BLOCK, SGU_BLOCK), 0)
        j = lax.broadcasted_iota(jnp.int32, (SGU_BLOCK, SGU_BLOCK), 1)
        causal = (i // CHUNK) >= (j // CHUNK)
        for pair in range(4):
            w0, w1 = jnp.where(causal, ws_ref[2 * pair], 0.0), jnp.where(causal, ws_ref[2 * pair + 1], 0.0)
            wm2_ref[pair] = jnp.concatenate([w0, w1], axis=1).astype(BF16)
            wmt2_ref[pair] = jnp.concatenate([w0.T, w1.T], axis=1).astype(BF16)
        by_row = jnp.concatenate([bs_ref[...], jnp.zeros((SGU_BLOCK - 8, SGU_BLOCK), F32)], axis=0).T
        group = lax.broadcasted_iota(jnp.int32, (SGU_BLOCK, D_SGU), 1) // 64
        spread = jnp.zeros((SGU_BLOCK, D_SGU), F32)
        for g in range(8):
            spread = jnp.where(group == g, by_row[:, g:g + 1], spread)
        bsx_ref[...] = spread

    shapes = [jax.ShapeDtypeStruct((HEADS, QB, KW), F32), jax.ShapeDtypeStruct((4, SGU_BLOCK, 2 * SGU_BLOCK), BF16),
              jax.ShapeDtypeStruct((4, SGU_BLOCK, 2 * SGU_BLOCK), BF16), jax.ShapeDtypeStruct((SGU_BLOCK, D_SGU), F32)]
    return [diag, w_s, b_s], shapes, fill


def _att_specs():
    qspec = pl.BlockSpec((QB, D_ATT), lambda g: (g, 0))
    kspecs = [pl.BlockSpec((QB, D_ATT), lambda g: (jnp.maximum(g - 2, 0), 0)),
              pl.BlockSpec((QB, D_ATT), lambda g: (jnp.maximum(g - 1, 0), 0)), qspec]
    bspec = pl.BlockSpec((HEADS, QB, KW), lambda g: (0, 0, 0))
    return qspec, kspecs, bspec


def _att_probs(qm, kp, bias, valid):
    s = jnp.where(valid, _nt(qm, kp) + bias, NEG_INF)
    e = jnp.exp(s - jnp.max(s, axis=-1, keepdims=True))
    return e / jnp.sum(e, axis=-1, keepdims=True)


def _att_valid():
    g = pl.program_id(0)
    blk = lax.broadcasted_iota(jnp.int32, (QB, KW), 1) // QB
    return (blk + g) >= 2


def _att_fwd(q, k, v, bias, comms=()):
    t = q.shape[0]

    def body(q_ref, k0, k1, k2, v0, v1, v2, b_ref, y_ref):
        valid = _att_valid()
        first = lax.broadcasted_iota(jnp.int32, (1, 128), 1) < 64
        for p in range(HEADS // 2):
            lanes = slice(p * 128, (p + 1) * 128)
            qp = q_ref[:, lanes]
            kp = jnp.concatenate([k0[:, lanes], k1[:, lanes], k2[:, lanes]], axis=0)
            vp = jnp.concatenate([v0[:, lanes], v1[:, lanes], v2[:, lanes]], axis=0)
            out = jnp.zeros((QB, 128), F32)
            for hh in range(2):
                mask = first if hh == 0 else jnp.logical_not(first)
                pr = _att_probs(jnp.where(mask, qp, 0), kp, b_ref[2 * p + hh], valid)
                out = out + _nn(pr.astype(BF16), jnp.where(mask, vp, 0))
            y_ref[:, lanes] = out.astype(BF16)

    qspec, kspecs, bspec = _att_specs()
    (out,), got = _call(
        body, name="att_fwd", grid=(t // QB,),
        in_specs=[qspec] + kspecs + kspecs + [bspec], out_specs=[qspec],
        out_shape=[jax.ShapeDtypeStruct((t, D_ATT), BF16)], scratch_shapes=[],
        args=(q, k, k, k, v, v, v, bias), comms=comms)
    return out, got


def _att_bwd(q, k, v, bias, dy, comms=()):
    t = q.shape[0]
    n_blocks = t // QB

    def body(q_ref, k0, k1, k2, v0, v1, v2, b_ref, dy_ref, dq_ref, dk_ref, dv_ref, db_ref, dk_acc, dv_acc):
        g = pl.program_id(0)

        @pl.when(g == 0)
        def _():
            db_ref[...] = jnp.zeros_like(db_ref)
            dk_acc[...] = jnp.zeros_like(dk_acc)
            dv_acc[...] = jnp.zeros_like(dv_acc)

        valid = _att_valid()
        first = lax.broadcasted_iota(jnp.int32, (1, 128), 1) < 64
        for p in range(HEADS // 2):
            lanes = slice(p * 128, (p + 1) * 128)
            qp = q_ref[:, lanes]
            dyp = dy_ref[:, lanes]
            kp = jnp.concatenate([k0[:, lanes], k1[:, lanes], k2[:, lanes]], axis=0)
            vp = jnp.concatenate([v0[:, lanes], v1[:, lanes], v2[:, lanes]], axis=0)
            dq = jnp.zeros((QB, 128), F32)
            dk = jnp.zeros((KW, 128), F32)
            dv = jnp.zeros((KW, 128), F32)
            for hh in range(2):
                mask = first if hh == 0 else jnp.logical_not(first)
                qm = jnp.where(mask, qp, 0)
                dym = jnp.where(mask, dyp, 0)
                pr = _att_probs(qm, kp, b_ref[2 * p + hh], valid)
                dp = _nt(dym, vp)
                ds = pr * (dp - jnp.sum(dp * pr, axis=-1, keepdims=True))
                db_ref[2 * p + hh] += ds
                dsb = ds.astype(BF16)
                dq = dq + _nn(dsb, jnp.where(mask, kp, 0))
                dk = dk + _tn(dsb, qm)
                dv = dv + _tn(pr.astype(BF16), dym)
            dq_ref[:, lanes] = (dq * 0.125).astype(BF16)
            for j in range(3):
                rows = pl.ds(pl.multiple_of(jnp.maximum(g - 2 + j, 0) * QB, QB), QB)
                dk_acc[rows, lanes] += dk[j * QB:(j + 1) * QB]
                dv_acc[rows, lanes] += dv[j * QB:(j + 1) * QB]

        @pl.when(g == n_blocks - 1)
        def _():
            dk_ref[...] = dk_acc[...].astype(BF16)
            dv_ref[...] = dv_acc[...].astype(BF16)

    qspec, kspecs, bspec = _att_specs()
    full = pl.BlockSpec((t, D_ATT), lambda g: (0, 0))
    return _call(
        body, name="att_bwd", grid=(n_blocks,),
        in_specs=[qspec] + kspecs + kspecs + [bspec, qspec], out_specs=[qspec, full, full, bspec],
        out_shape=[jax.ShapeDtypeStruct((t, D_ATT), BF16)] * 3 + [jax.ShapeDtypeStruct((HEADS, QB, KW), F32)],
        scratch_shapes=[pltpu.VMEM((t, D_ATT), F32)] * 2,
        args=(q, k, k, k, v, v, v, bias, dy), comms=comms)


def _rel_bias_grad(dbias, comms=()):
    c_in = [a for c in comms for a in c.inputs]
    c_out = [s for c in comms for s in c.out_shape]
    c_scr = [s for c in comms for s in c.scratch]

    def body(db_ref, *refs):
        cin, (cs_ref, tot_ref), cout, cscr = _split_refs(refs, [len(c_in), 2, len(c_out), len(c_scr)])
        bound = _bind(comms, cin, cout, cscr)
        for c, r in bound:
            c.begin(*r)
        lane = lax.broadcasted_iota(jnp.int32, (1, SKEW_W), 1)
        capped = (lane < N_CAP) | (lane > KW)
        pad = jnp.zeros((8, QB), F32)
        for h in range(HEADS):
            z = jnp.concatenate([db_ref[h, 0:8, :], pad], axis=1)
            for a in range(1, QB // 8):
                z = z + pltpu.roll(jnp.concatenate([db_ref[h, 8 * a:8 * a + 8, :], pad], axis=1), SKEW_W - 8 * a, 1)
            cs = z[0:1, :]
            for b in range(1, 8):
                cs = cs + pltpu.roll(z[b:b + 1, :], SKEW_W - b, 1)
            cs_ref[h:h + 1, :] = cs
            tot_ref[h:h + 1, :] = jnp.broadcast_to(jnp.sum(jnp.where(capped, cs, 0.0), axis=1, keepdims=True), (1, 128))
        for phase in ("mid", "relay", "end"):
            for c, r in bound:
                getattr(c, phase)(*r)

    whole = pl.BlockSpec(memory_space=pltpu.VMEM)
    cs, tot, *got = pl.pallas_call(
        body, name="rel_bias_grad", in_specs=[whole] + [ANY] * len(c_in), out_specs=[whole, whole] + [ANY] * len(c_out),
        out_shape=[jax.ShapeDtypeStruct((HEADS, SKEW_W), F32), jax.ShapeDtypeStruct((HEADS, 128), F32)] + c_out, scratch_shapes=c_scr,
    )(dbias, *c_in)
    return jnp.concatenate([cs[:, KW:N_CAP - 1:-1], tot[:, :1]], axis=1), got


def _group_stack(blk, first):
    return jnp.concatenate([jnp.where(first, blk, 0), jnp.where(first, 0, blk)], axis=0)


def _sgu_norm(zs_ref, lng, lnb):
    zs = zs_ref[...].astype(F32)
    ga, th = _gelu(zs)
    u, vs = ga[:, :D_SGU], ga[:, D_SGU:]
    mu = jnp.mean(vs, axis=-1, keepdims=True)
    cen = vs - mu
    rstd = lax.rsqrt(jnp.mean(cen * cen, axis=-1, keepdims=True) + EPS)
    xhat = cen * rstd
    return zs, th, u, xhat, rstd, xhat * lng + lnb


def _sgu_mix(vb, wm2_ref, bsx, s_ref):
    first = lax.broadcasted_iota(jnp.int32, (1, 128), 1) < 64
    for n in range(TM // SGU_BLOCK):
        for p in range(4):
            blk = vb[n * 128:(n + 1) * 128, p * 128:(p + 1) * 128]
            s_ref[n * 128:(n + 1) * 128, p * 128:(p + 1) * 128] = _nn(wm2_ref[p], _group_stack(blk, first)) + bsx[:, p * 128:(p + 1) * 128]


def _merge_fwd(x, zs, gt, y_att, lng, lnb, wm2, bsx, gw):
    t = x.shape[0]

    def body(x_ref, zs_ref, gt_ref, ya_ref, lng_ref, lnb_ref, wm2_ref, bsx_ref, gw_ref, xo_ref, ys_ref, mg_ref,
             wbr, wo, s_scr, sems):
        @pl.when(pl.program_id(0) == 0)
        def _():
            for cp in _load_rows(gw_ref, wbr, OFF_BR, R_BR, sems.at[0]) + _load_rows(gw_ref, wo, OFF_WO, R_WO, sems.at[1]):
                cp.wait()

        _, _, u, _, _, vsn = _sgu_norm(zs_ref, lng_ref[...], lnb_ref[...])
        _sgu_mix(vsn.astype(BF16), wm2_ref, bsx_ref[...], s_scr)
        ys = (u * s_scr[...]).astype(BF16)
        ys_ref[...] = ys
        pa = _nt(ya_ref[...], wbr[:, :D_ATT])
        ps = _nt(ys, wbr[:, D_ATT:])
        mg = (gt_ref[:, :D].astype(F32) * pa + gt_ref[:, D:].astype(F32) * ps).astype(BF16)
        mg_ref[...] = mg
        xo_ref[...] = x_ref[...] + _nn(mg, wo[...])

    def tile(w):
        return pl.BlockSpec((TM, w), lambda i: (i, 0))

    def const(shape):
        return pl.BlockSpec(shape, lambda i: (0,) * len(shape))

    return pl.pallas_call(
        body, name="merge_fwd", grid=(t // TM,),
        in_specs=[tile(D), tile(2 * D_SGU), tile(2 * D), tile(D_ATT), const((1, D_SGU)), const((1, D_SGU)),
                  const((4, 128, 256)), const((128, D_SGU)), ANY],
        out_specs=[tile(D), tile(D_SGU), tile(D)],
        out_shape=[jax.ShapeDtypeStruct((t, D), F32), jax.ShapeDtypeStruct((t, D_SGU), BF16), jax.ShapeDtypeStruct((t, D), BF16)],
        scratch_shapes=[pltpu.VMEM((D, D), BF16), pltpu.VMEM((D, D), BF16), pltpu.VMEM((TM, D_SGU), F32),
                        pltpu.SemaphoreType.DMA((2, N_DEV))],
        compiler_params=_cparams(("arbitrary",)),
    )(x, zs, gt, y_att, lng, lnb, wm2, bsx, gw)


def _merge_bwd(dx, gt, y_att, y_sgu, merged, gw):
    t = dx.shape[0]
    n_steps = t // TM

    def body(dx_ref, gt_ref, ya_ref, ys_ref, mg_ref, gw_ref, dzg_ref, dya_ref, dys_ref, dbg_ref, ga_ref, gs_ref, go_ref,
             wbr, wo, acc_a, acc_s, acc_o, sems):
        step = pl.program_id(0)

        @pl.when(step == 0)
        def _():
            cps = _load_rows(gw_ref, wbr, OFF_BR, R_BR, sems.at[0]) + _load_rows(gw_ref, wo, OFF_WO, R_WO, sems.at[1])
            dbg_ref[...] = jnp.zeros_like(dbg_ref)
            for acc in (acc_a, acc_s, acc_o):
                acc[...] = jnp.zeros_like(acc)
            for cp in cps:
                cp.wait()

        dxb = dx_ref[...].astype(BF16)
        acc_o[...] += _tn(mg_ref[...], dxb)
        dm = _nt(dxb, wo[...])
        for half, y_ref, w, acc in ((0, ya_ref, wbr.at[:, :D_ATT], acc_a), (1, ys_ref, wbr.at[:, D_ATT:], acc_s)):
            cols = slice(half * D, (half + 1) * D)
            gate = gt_ref[:, cols].astype(F32)
            branch = _nt(y_ref[...], w[...])
            dzg = dm * branch * gate * (1.0 - gate)
            dbg_ref[:, cols] += jnp.sum(dzg, axis=0, keepdims=True)
            dzg_ref[:, cols] = dzg.astype(BF16)
            dbr = (dm * gate).astype(BF16)
            acc[...] += _tn(dbr, y_ref[...])
            dy = _nn(dbr, w[...])
            if half == 0:
                dya_ref[...] = dy.astype(BF16)
            else:
                dys_ref[...] = dy

        @pl.when(step == n_steps - 1)
        def _():
            ga_ref[...] = acc_a[...].astype(BF16)
            gs_ref[...] = acc_s[...].astype(BF16)
            go_ref[...] = acc_o[...].astype(BF16)

    def tile(w):
        return pl.BlockSpec((TM, w), lambda i: (i, 0))

    def whole(r, c):
        return pl.BlockSpec((r, c), lambda i: (0, 0))

    dzg, dya, dys, dbg, g_ba, g_bs, g_wo = pl.pallas_call(
        body, name="merge_bwd", grid=(n_steps,),
        in_specs=[tile(D), tile(2 * D), tile(D_ATT), tile(D_SGU), tile(D), ANY],
        out_specs=[tile(2 * D), tile(D_ATT), tile(D_SGU), whole(1, 2 * D), whole(D, D_ATT), whole(D, D_SGU), whole(D, D)],
        out_shape=[jax.ShapeDtypeStruct((t, 2 * D), BF16), jax.ShapeDtypeStruct((t, D_ATT), BF16), jax.ShapeDtypeStruct((t, D_SGU), F32),
                   jax.ShapeDtypeStruct((1, 2 * D), F32), jax.ShapeDtypeStruct((D, D_ATT), BF16), jax.ShapeDtypeStruct((D, D_SGU), BF16),
                   jax.ShapeDtypeStruct((D, D), BF16)],
        scratch_shapes=[pltpu.VMEM((D, D), BF16), pltpu.VMEM((D, D), BF16), pltpu.VMEM((D, D_ATT), F32), pltpu.VMEM((D, D_SGU), F32),
                        pltpu.VMEM((D, D), F32), pltpu.SemaphoreType.DMA((2, N_DEV))],
        compiler_params=_cparams(("arbitrary",)),
    )(dx, gt, y_att, y_sgu, merged, gw)
    return dzg, dya, dys, dbg, g_ba.reshape(N_DEV, R_BR, D_ATT), g_bs.reshape(N_DEV, R_BR, D_SGU), g_wo.reshape(N_DEV, R_WO, D)


def _sgu_bwd(zs, dys, lng, lnb, wm2, wmt2, bsx, comms=()):
    t = zs.shape[0]
    n_steps = t // TM

    def body(zs_ref, dys_ref, lng_ref, lnb_ref, wm2_ref, wmt2_ref, bsx_ref, dzs_ref, dw_ref, dbs_ref, dlg_ref, dlb_ref,
             s_scr, dv_scr, ds_acc):
        i = pl.program_id(0)

        @pl.when(i == 0)
        def _():
            dw_ref[...] = jnp.zeros_like(dw_ref)
            dlg_ref[...] = jnp.zeros_like(dlg_ref)
            dlb_ref[...] = jnp.zeros_like(dlb_ref)
            ds_acc[...] = jnp.zeros_like(ds_acc)

        lng = lng_ref[...]
        zs, th, u, xhat, rstd, vsn = _sgu_norm(zs_ref, lng, lnb_ref[...])
        vb = vsn.astype(BF16)
        _sgu_mix(vb, wm2_ref, bsx_ref[...], s_scr)
        dys = dys_ref[...]
        du = dys * s_scr[...]
        ds = dys * u
        dsb = ds.astype(BF16)
        first = lax.broadcasted_iota(jnp.int32, (1, 128), 1) < 64
        acc = jnp.zeros((SGU_BLOCK, D_SGU), F32)
        for n in range(TM // SGU_BLOCK):
            rows = slice(n * 128, (n + 1) * 128)
            acc = acc + ds[rows]
            for p in range(4):
                lanes = slice(p * 128, (p + 1) * 128)
                stack = _group_stack(dsb[rows, lanes], first)
                dv_scr[rows, lanes] = _nn(wmt2_ref[p], stack)
                dw_ref[p] += _nt(stack, vb[rows, lanes])
        ds_acc[...] += acc
        dvsn = dv_scr[...]
        dlg_ref[...] += jnp.sum(dvsn * xhat, axis=0, keepdims=True)
        dlb_ref[...] += jnp.sum(dvsn, axis=0, keepdims=True)
        dxh = dvsn * lng
        dvs = rstd * (dxh - jnp.mean(dxh, axis=-1, keepdims=True) - xhat * jnp.mean(dxh * xhat, axis=-1, keepdims=True))
        dga = jnp.concatenate([du, dvs], axis=1)
        dzs_ref[...] = (dga * _gelu_grad(zs, th)).astype(BF16)

        @pl.when(i == n_steps - 1)
        def _():
            r = lax.broadcasted_iota(jnp.int32, (256, 128), 0) % SGU_BLOCK
            c = lax.broadcasted_iota(jnp.int32, (256, 128), 1)
            keep = (r // CHUNK) >= (c // CHUNK)
            for p in range(4):
                dw_ref[p] = jnp.where(keep, dw_ref[p], 0.0)
            total = ds_acc[...]
            grp = lax.broadcasted_iota(jnp.int32, (SGU_BLOCK, D_SGU), 1) // 64
            lane = lax.broadcasted_iota(jnp.int32, (SGU_BLOCK, 128), 1)
            out = jnp.zeros((SGU_BLOCK, 128), F32)
            for gi in range(8):
                out = jnp.where(lane == gi, jnp.sum(jnp.where(grp == gi, total, 0.0), axis=1, keepdims=True), out)
            dbs_ref[...] = out

    def tile(w):
        return pl.BlockSpec((TM, w), lambda i: (i, 0))

    def const(shape):
        return pl.BlockSpec(shape, lambda i: (0,) * len(shape))

    return _call(
        body, name="sgu_bwd", grid=(n_steps,),
        in_specs=[tile(2 * D_SGU), tile(D_SGU), const((1, D_SGU)), const((1, D_SGU)), const((4, 128, 256)), const((4, 128, 256)),
                  const((128, D_SGU))],
        out_specs=[tile(2 * D_SGU), const((4, 256, 128)), const((128, 128)), const((1, D_SGU)), const((1, D_SGU))],
        out_shape=[jax.ShapeDtypeStruct((t, 2 * D_SGU), BF16), jax.ShapeDtypeStruct((4, 256, 128), F32),
                   jax.ShapeDtypeStruct((128, 128), F32), jax.ShapeDtypeStruct((1, D_SGU), F32), jax.ShapeDtypeStruct((1, D_SGU), F32)],
        scratch_shapes=[pltpu.VMEM((TM, D_SGU), F32), pltpu.VMEM((TM, D_SGU), F32), pltpu.VMEM((SGU_BLOCK, D_SGU), F32)],
        args=(zs, dys, lng, lnb, wm2, wmt2, bsx), comms=comms)


def _adamw(g, w, m, v):
    m = ADAM_B1 * m + (1.0 - ADAM_B1) * g
    v = ADAM_B2 * v + (1.0 - ADAM_B2) * (g * g)
    m_hat = m / (1.0 - ADAM_B1 ** ADAM_STEP)
    v_hat = v / (1.0 - ADAM_B2 ** ADAM_STEP)
    return -ADAM_LR * (m_hat / (jnp.sqrt(v_hat) + ADAM_EPS) + ADAM_WD * w), m, v


def _adamw_matrices(items, name):
    n = len(items)
    c = items[0][0].shape[2]
    tc = 256

    def body(own_ref, *refs):
        for i, (_, _, _, _, _, transposed) in enumerate(items):
            p_ref, s_ref, w_ref, m_ref, v_ref = refs[5 * i:5 * i + 5]
            g = p_ref[0].astype(F32) + p_ref[1].astype(F32) + p_ref[2].astype(F32) + s_ref[...].astype(F32)
            g = g.T if transposed else g
            res = (g,) + _adamw(g, w_ref[...], m_ref[...], v_ref[...])
            for o_ref, val in zip(refs[5 * n + 4 * i:5 * n + 4 * i + 4], res):
                o_ref[...] = val

    in_specs, out_specs, out_shape, args = [], [], [], []
    for parts, sums, w, m, v, transposed in items:
        r = parts.shape[1]
        own = pl.BlockSpec((None, tc, r), lambda i, o: (0, i, 0)) if transposed else pl.BlockSpec((None, r, tc), lambda i, o: (0, 0, i))
        in_specs += [pl.BlockSpec((3, r, tc), lambda i, o: (0, 0, i)), pl.BlockSpec((None, r, tc), lambda i, o: (o[0], 0, i)), own, own, own]
        out_specs += [own] * 4
        out_shape += [jax.ShapeDtypeStruct(w.shape, F32)] * 4
        args += [parts, sums, w, m, v]
    res = pl.pallas_call(
        body, name=name,
        grid_spec=pltpu.PrefetchScalarGridSpec(num_scalar_prefetch=1, grid=(c // tc,), in_specs=in_specs, out_specs=out_specs),
        out_shape=out_shape, compiler_params=_cparams(("arbitrary",)),
    )(_my_index(("x", 2), ("y", 1)), *args)
    return [list(res[4 * i:4 * i + 4]) for i in range(n)]


_SMALL_2D = {"norm_ffn1": (1, D), "norm_mix": (1, D), "norm_ffn2": (1, D), "norm_final": (1, D), "b_gate": (1, 2 * D),
             "sgu_ln_g": (1, D_SGU), "sgu_ln_b": (1, D_SGU), "sgu_b_s": (8, SGU_BLOCK), "rel_bias": (HEADS, N_REL),
             "sgu_w_s": (8 * SGU_BLOCK, SGU_BLOCK)}


def _adamw_small(parts, loss_parts, p):
    names = list(parts)
    n = len(names)

    def body(*refs):
        got, loss_got, wmv, outs, loss_out = refs[:n], refs[n], refs[n + 1:4 * n + 1], refs[4 * n + 1:8 * n + 1], refs[8 * n + 1]
        for i, name in enumerate(names):
            g = got[i][0]
            for k in range(1, N_DEV):
                g = g + got[i][k]
            if name == "sgu_b_s":
                g = g.T[0:8, :]
            res = (g,) + _adamw(g, wmv[3 * i][...], wmv[3 * i + 1][...], wmv[3 * i + 2][...])
            for o_ref, val in zip(outs[4 * i:4 * i + 4], res):
                o_ref[...] = val
        total = loss_got[0]
        for k in range(1, N_DEV):
            total = total + loss_got[k]
        loss_out[...] = total

    wmv = [p[pre + name].reshape(_SMALL_2D[name]) for name in names for pre in ("", "m_", "v_")]
    res = pl.pallas_call(
        body, name="adamw_small",
        out_shape=[jax.ShapeDtypeStruct(_SMALL_2D[name], F32) for name in names for _ in range(4)] + [jax.ShapeDtypeStruct((1, 128), F32)],
        compiler_params=_cparams())(*[parts[name] for name in names], loss_parts, *wmv)
    return [{name: res[4 * i + j].reshape(p[name].shape) for i, name in enumerate(names)} for j in range(4)], res[-1]


def _pack_rows(slabs, name, comms=(), extra=None):
    flat = [a for groups in slabs for grp in groups for a, _ in grp]
    rows = [[grp[0][0].shape[2] if grp[0][1] else grp[0][0].shape[1] for grp in groups] for groups in slabs]
    e_in, e_out, fill = extra if extra is not None else ([], [], None)
    c_in = [a for c in comms for a in c.inputs]
    c_out = [s for c in comms for s in c.out_shape]
    c_scr = [s for c in comms for s in c.scratch]

    def body(*refs):
        ins, ein, cin, outs, eout, cout, cscr = _split_refs(
            refs, [len(flat), len(e_in), len(c_in), len(slabs), len(e_out), len(c_out), len(c_scr)])
        bound = _bind(comms, cin, cout, cscr)
        for c, r in bound:
            c.begin(*r)
        pos = 0
        for groups, slab_rows, o_ref in zip(slabs, rows, outs):
            off = 0
            for grp, r in zip(groups, slab_rows):
                vals = []
                for _, transposed in grp:
                    val = ins[pos][0]
                    vals.append(val.T if transposed else val)
                    pos += 1
                o_ref[off:off + r, :] = (vals[0] if len(vals) == 1 else jnp.concatenate(vals, axis=1)).astype(BF16)
                off += r
        if fill is not None:
            fill(ein, eout)
        for phase in ("mid", "relay", "end"):
            for c, r in bound:
                getattr(c, phase)(*r)

    whole = pl.BlockSpec(memory_space=pltpu.VMEM)
    n_plain = len(slabs) + len(e_out)
    res = pl.pallas_call(
        body, name=name, in_specs=[whole] * (len(flat) + len(e_in)) + [ANY] * len(c_in), out_specs=[whole] * n_plain + [ANY] * len(c_out),
        out_shape=[jax.ShapeDtypeStruct((sum(r), D), BF16) for r in rows] + list(e_out) + c_out, scratch_shapes=c_scr,
        compiler_params=_cparams())(*flat, *e_in, *c_in)
    return list(res[:len(slabs)]), list(res[len(slabs):n_plain]), list(res[n_plain:])


def _step(x, target, p):
    n1, nm, n2 = p["norm_ffn1"], p["norm_mix"], p["norm_ffn2"]
    nf = p["norm_final"].reshape(1, D)
    lng, lnb = p["sgu_ln_g"], p["sgu_ln_b"]

    def as_rows(a):
        return jnp.swapaxes(a, 1, 2)

    def updates(parts, sums, names):
        res = {}
        for cols in sorted({pt.shape[2] for pt in parts}):
            items, group = [], [(pt, sm, n) for pt, sm, n in zip(parts, sums, names) if pt.shape[2] == cols]
            for pt, sm, n in group:
                view = as_rows if p[n].shape[1:] != pt.shape[1:] and p[n].shape[2] > 128 else (lambda a: a)
                transposed = p[n].shape[1:] != pt.shape[1:] and p[n].shape[2] <= 128
                items.append((pt, sm, view(p[n]), view(p["m_" + n]), view(p["v_" + n]), transposed))
            for (pt, sm, n), four in zip(group, _adamw_matrices(items, "adamw_" + group[0][2])):
                res[n] = [as_rows(o) if o.shape != p[n].shape else o for o in four]
        return res

    (rows1,), _, _ = _pack_rows([[[(as_rows(p["ffn1_w_gate"]), False)], [(as_rows(p["ffn1_w_up"]), False)], [(p["ffn1_w_down"], False)]]],
                                "pack_ffn1")
    (rows_m, rows2d, rows2gu), (bias, wm2, wmt2, bsx), (gw1,) = _pack_rows(
        [[[(as_rows(p["w_in"]), False)], [(p["w_branch_att"], True), (p["w_branch_sgu"], True)], [(p["w_out"], False)]],
         [[(p["ffn2_w_down"], False)]],
         [[(as_rows(p["ffn2_w_gate"]), False)], [(as_rows(p["ffn2_w_up"]), False)]]],
        "gather_ffn1_pack_rest", [_Gather(rows1)], _tables(p["rel_bias"][0], p["sgu_w_s"][0], p["sgu_b_s"][0]))
    x1, ab1, h1, (gwm,) = _ffn_fwd(x, n1, gw1, gw1, 2 * R_FF, "ffn1_fwd", [_Gather(rows_m)])
    (q, k, v, zs, gt, h2), (gw2d,) = _mix_proj_fwd(x1, nm, p["b_gate"], gwm, [_Gather(rows2d)])
    y_att, (gw2gu,) = _att_fwd(q, k, v, bias, [_Gather(rows2gu)])
    x2, y_sgu, merged = _merge_fwd(x1, zs, gt, y_att, lng, lnb, wm2, bsx, gwm)
    dx3, ab2, hb, _, d_nf, loss = _ffn_fwd(x2, n2, gw2gu, gw2d, 0, "ffn2_fwd", head=(nf, target))

    (dab, g_down), _ = _ffn_bwd_hidden(ab2, dx3, gw2d, 0, "ffn2_bwd_hidden")
    (dx2, d_n2), _ = _ffn_bwd_input(x2, n2, dab, dx3, gw2gu, "ffn2_bwd")
    g_gu = _weight_grad(dab, hb, "ffn2_dw_gate_up", mats=2)
    dzg, dya, dys, d_bg, g_ba, g_bs, g_wo = _merge_bwd(dx2, gt, y_att, y_sgu, merged, gwm)
    g_late = [(g_gu, 0), (g_gu, 1), (g_down, 0), (g_ba, 0), (g_bs, 0), (g_wo, 0)]
    late = ("ffn2_w_gate", "ffn2_w_up", "ffn2_w_down", "w_branch_att", "w_branch_sgu", "w_out")
    (dzs, d_wm, d_bs, d_lng, d_lnb), gots_late = _sgu_bwd(zs, dys, lng, lnb, wm2, wmt2, bsx, [_SiblingSwap(g_late)])
    sums_late = _pair_sums(g_late, gots_late, "pair_sums_late")
    (dq, dk, dv, d_bias), parts_late = _att_bwd(q, k, v, bias, dya, [_ChipScatter(sums_late)])
    big = updates(parts_late, sums_late, late)
    dz = [dq, dk, dv, dzs, dzg]
    g_in = [(_weight_grad_pieces([(a, 0, a.shape[1]) for a in dz], h2, "dw_in")[0], 0)]
    d_rel, got_in = _rel_bias_grad(d_bias, [_SiblingSwap(g_in)])
    sums_in = _pair_sums(g_in, got_in, "pair_sums_w_in")
    (dx1, d_nm), parts_in = _mix_proj_bwd(dz, x1, nm, dx2, gwm, [_ChipScatter(sums_in)])
    big.update(updates(parts_in, sums_in, ("w_in",)))
    small = {"norm_ffn2": d_n2, "norm_final": d_nf, "b_gate": d_bg, "sgu_ln_g": d_lng, "sgu_ln_b": d_lnb, "sgu_b_s": d_bs,
             "rel_bias": d_rel, "sgu_w_s": d_wm.reshape(_SMALL_2D["sgu_w_s"]), "norm_mix": d_nm}

    (dab, g_down), (*small_parts, loss_parts) = _ffn_bwd_hidden(ab1, dx1, gw1, 2 * R_FF, "ffn1_bwd_hidden",
                                                                [_AllToAll(list(small.values()) + [loss])])
    g_gate, gots_d = _weight_grad_pieces([(dab, 0, F)], h1, "ffn1_dw_gate", [_SiblingSwap([(g_down, 0)])])
    sums_d = _pair_sums([(g_down, 0)], gots_d, "pair_sums_ffn1_down")
    g_up, (parts_d, *gots_g) = _weight_grad_pieces([(dab, F, F)], h1, "ffn1_dw_up", [_ChipScatter(sums_d), _SiblingSwap([(g_gate, 0)])])
    gots_u = _comm_only([_SiblingSwap([(g_up, 0)])], "swap_ffn1_up")
    sums_gu = _pair_sums([(g_gate, 0), (g_up, 0)], gots_g + gots_u, "pair_sums_ffn1_gate_up")
    (dx0, d_n1), parts_gu = _ffn_bwd_input(x, n1, dab, dx1, gw1, "ffn1_bwd", [_ChipScatter(sums_gu)])
    (n1_parts,) = _comm_only([_AllToAll([d_n1])], "gather_norm_ffn1")
    big.update(updates([parts_d] + parts_gu, sums_d + sums_gu, ("ffn1_w_down", "ffn1_w_gate", "ffn1_w_up")))
    out_s, loss_sum = _adamw_small(dict(zip(small, small_parts), norm_ffn1=n1_parts), loss_parts, p)
    return dx0, loss_sum[0, 0], [{**{n: four[i] for n, four in big.items()}, **s} for i, s in enumerate(out_s)]


_OUT_ORDER = ("norm_ffn1", "ffn1_w_gate", "ffn1_w_up", "ffn1_w_down", "norm_mix", "w_in", "b_gate", "rel_bias", "sgu_ln_g", "sgu_ln_b",
              "sgu_w_s", "sgu_b_s", "w_branch_att", "w_branch_sgu", "w_out", "norm_ffn2", "ffn2_w_gate", "ffn2_w_up", "ffn2_w_down",
              "norm_final")


def kernel(x, norm_ffn1, ffn1_w_gate, ffn1_w_up, ffn1_w_down, norm_mix, w_in, b_gate, rel_bias, sgu_ln_g, sgu_ln_b, sgu_w_s, sgu_b_s, w_branch_att, w_branch_sgu, w_out, norm_ffn2, ffn2_w_gate, ffn2_w_up, ffn2_w_down, norm_final, loss_target, m_norm_ffn1, m_ffn1_w_gate, m_ffn1_w_up, m_ffn1_w_down, m_norm_mix, m_w_in, m_b_gate, m_rel_bias, m_sgu_ln_g, m_sgu_ln_b, m_sgu_w_s, m_sgu_b_s, m_w_branch_att, m_w_branch_sgu, m_w_out, m_norm_ffn2, m_ffn2_w_gate, m_ffn2_w_up, m_ffn2_w_down, m_norm_final, v_norm_ffn1, v_ffn1_w_gate, v_ffn1_w_up, v_ffn1_w_down, v_norm_mix, v_w_in, v_b_gate, v_rel_bias, v_sgu_ln_g, v_sgu_ln_b, v_sgu_w_s, v_sgu_b_s, v_w_branch_att, v_w_branch_sgu, v_w_out, v_norm_ffn2, v_ffn2_w_gate, v_ffn2_w_up, v_ffn2_w_down, v_norm_final):
    args = dict(locals())
    dx, loss, outs = _step(x[0], loss_target[0], {pre + n: args[pre + n] for pre in ("", "m_", "v_") for n in _OUT_ORDER})
    return (loss, dx[None], *[o[n] for o in outs for n in _OUT_ORDER])
```

```python
import functools

import jax
import jax.numpy as jnp
from jax import lax
from jax.experimental import pallas as pl
from jax.experimental.pallas import tpu as pltpu

F32 = jnp.float32
BF16 = jnp.bfloat16

N_DEV = 8
D = 1024
F = 2816
D_ATT = 512
D_SGU = 512
D_IN = 4608
HEADS = 8
CHUNK = 64
N_LEFT = 8
REL_CLIP = 256
N_REL = 2 * REL_CLIP + 1
SGU_BLOCK = 128
EPS = 1e-6
NEG_INF = -1e30
QB = 256
KW = 3 * QB

R_FF, R_IN, R_BR, R_WO = F // N_DEV, D_IN // N_DEV, D // N_DEV, D // N_DEV
OFF_IN, OFF_BR, OFF_WO = 0, R_IN, R_IN + R_BR

FC = 256
TM = 512
VMEM_LIMIT = 56 * 1024 * 1024

ADAM_LR, ADAM_B1, ADAM_B2, ADAM_EPS, ADAM_WD, ADAM_STEP = 0.001, 0.9, 0.999, 1e-08, 0.01, 10

MESH = pl.DeviceIdType.MESH
ANY = pl.BlockSpec(memory_space=pl.ANY)


def _nt(a, b):
    return lax.dot_general(a, b, (((1,), (1,)), ((), ())), preferred_element_type=F32)


def _tn(a, b):
    return lax.dot_general(a, b, (((0,), (0,)), ((), ())), preferred_element_type=F32)


def _nn(a, b):
    return jnp.dot(a, b, preferred_element_type=F32)


def _cparams(sem=None):
    return pltpu.CompilerParams(dimension_semantics=sem, vmem_limit_bytes=VMEM_LIMIT)


def _load_rows(gw_ref, dst, off, rows, sems):
    copies = [pltpu.make_async_copy(gw_ref.at[k, pl.ds(off, rows), :], dst.at[pl.ds(k * rows, rows), :], sems.at[k])
              for k in range(N_DEV)]
    for cp in copies:
        cp.start()
    return copies


def _rms(xv):
    r = lax.rsqrt(jnp.mean(xv * xv, axis=-1, keepdims=True) + EPS)
    return xv * r, r


def _rms_bwd(dh, xn, r, gain):
    dxn = dh * gain
    dx = r * (dxn - xn * jnp.mean(dxn * xn, axis=-1, keepdims=True))
    return dx, jnp.sum(dh * xn, axis=0, keepdims=True)


def _gelu(x):
    t = jnp.tanh(0.7978845608028654 * (x + 0.044715 * x * x * x))
    return 0.5 * x * (1.0 + t), t


def _gelu_grad(x, t):
    return 0.5 * (1.0 + t) + 0.5 * x * (1.0 - t * t) * 0.7978845608028654 * (1.0 + 3.0 * 0.044715 * x * x)


def _place():
    x, y, cc = lax.axis_index("x"), lax.axis_index("y"), lax.axis_index("c")
    return x, y, cc, [(1 - x, y), (x, 1 - y), (1 - x, 1 - y)]


class _Gather:
    def __init__(self, shard):
        self.inputs = [shard]
        self.out_shape = [jax.ShapeDtypeStruct((N_DEV,) + shard.shape, shard.dtype)]
        self.scratch = [pltpu.SemaphoreType.DMA((7,)), pltpu.SemaphoreType.DMA((7,)), pltpu.SemaphoreType.DMA]

    def _copies(self, ins, outs, scr):
        (x_ref,), (out_ref,), (send_sems, recv_sems, local_sem) = ins, outs, scr
        x, y, cc, chips = _place()

        def slab(px, py, pc):
            return out_ref.at[4 * px + 2 * py + pc]

        def copy(k, block, to, src=None):
            return pltpu.make_async_remote_copy(
                src_ref=slab(*block) if src is None else src, dst_ref=slab(*block),
                send_sem=send_sems.at[k], recv_sem=recv_sems.at[k], device_id=to, device_id_type=MESH)

        me, sibling = (x, y, cc), (x, y, 1 - cc)
        x_nbr, y_nbr, diagonal = chips
        mine = pltpu.make_async_copy(x_ref, slab(*me), local_sem)
        first = [copy(0, me, sibling, src=x_ref), copy(1, me, (*x_nbr, cc), src=x_ref), copy(2, me, (*y_nbr, cc), src=x_ref)]
        neighbours = [copy(1, (*x_nbr, cc), me), copy(2, (*y_nbr, cc), me)]
        second_hand = copy(3, (x ^ (1 - cc), y ^ cc, cc), (x ^ cc, y ^ (1 - cc), cc))
        from_diagonal = copy(3, (*diagonal, cc), me)
        passed = [copy(4 + j, (*chip, cc), sibling) for j, chip in enumerate(chips)]
        from_sibling = [copy(0, sibling, me)] + [copy(4 + j, (*chip, 1 - cc), me) for j, chip in enumerate(chips)]
        return mine, first, neighbours, second_hand, from_diagonal, passed, from_sibling

    def begin(self, *refs):
        mine, first = self._copies(*refs)[:2]
        mine.start()
        for cp in first:
            cp.start()

    def mid(self, *refs):
        _, _, neighbours, second_hand, _, passed, _ = self._copies(*refs)
        for cp in neighbours:
            cp.wait_recv()
        second_hand.start()
        passed[0].start()
        passed[1].start()

    def relay(self, *refs):
        _, _, _, _, from_diagonal, passed, _ = self._copies(*refs)
        from_diagonal.wait_recv()
        passed[2].start()

    def end(self, *refs):
        mine, first, _, second_hand, _, passed, from_sibling = self._copies(*refs)
        for cp in from_sibling:
            cp.wait_recv()
        for cp in first + [second_hand] + passed:
            cp.wait_send()
        mine.wait()


class _Direct:
    def begin(self, *refs):
        keep, give = self._copies(*refs)
        for cp in keep + give:
            cp.start()

    def mid(self, *refs):
        pass

    def relay(self, *refs):
        pass

    def end(self, *refs):
        keep, give = self._copies(*refs)
        for cp in give:
            cp.wait_recv()
        for cp in give:
            cp.wait_send()
        for cp in keep:
            cp.wait()


class _SiblingSwap(_Direct):
    def __init__(self, grads):
        n = len(grads)
        self.which = [w for _, w in grads]
        self.inputs = [g for g, _ in grads]
        self.out_shape = [jax.ShapeDtypeStruct((4,) + g.shape[1:], g.dtype) for g, _ in grads]
        self.scratch = [pltpu.SemaphoreType.DMA((n, 4)), pltpu.SemaphoreType.DMA((n, 4))]

    def _copies(self, ins, outs, scr):
        send_sems, recv_sems = scr
        x, y, cc, _ = _place()
        return [], [pltpu.make_async_remote_copy(src_ref=g_ref.at[N_DEV * w + 2 * j + 1 - cc], dst_ref=got_ref.at[j],
                                                 send_sem=send_sems.at[i, j], recv_sem=recv_sems.at[i, j], device_id=(x, y, 1 - cc),
                                                 device_id_type=MESH)
                    for i, (g_ref, got_ref, w) in enumerate(zip(ins, outs, self.which)) for j in range(4)]


class _ChipScatter(_Direct):
    def __init__(self, sums):
        n = len(sums)
        self.inputs = list(sums)
        self.out_shape = [jax.ShapeDtypeStruct((3,) + s.shape[1:], s.dtype) for s in sums]
        self.scratch = [pltpu.SemaphoreType.DMA((n, 3)), pltpu.SemaphoreType.DMA((n, 3))]

    def _copies(self, ins, outs, scr):
        send_sems, recv_sems = scr
        _, _, cc, chips = _place()
        return [], [pltpu.make_async_remote_copy(src_ref=s_ref.at[2 * px + py], dst_ref=got_ref.at[j], send_sem=send_sems.at[i, j],
                                                 recv_sem=recv_sems.at[i, j], device_id=(px, py, cc), device_id_type=MESH)
                    for i, (s_ref, got_ref) in enumerate(zip(ins, outs)) for j, (px, py) in enumerate(chips)]


class _AllToAll(_Direct):
    def __init__(self, blocks):
        n = len(blocks)
        self.inputs = list(blocks)
        self.out_shape = [jax.ShapeDtypeStruct((N_DEV,) + b.shape, b.dtype) for b in blocks]
        self.scratch = [pltpu.SemaphoreType.DMA((n, 7)), pltpu.SemaphoreType.DMA((n, 7)), pltpu.SemaphoreType.DMA((n,))]

    def _copies(self, ins, outs, scr):
        send_sems, recv_sems, local_sems = scr
        x, y, cc, _ = _place()
        me = 4 * x + 2 * y + cc
        keep = [pltpu.make_async_copy(b_ref, got_ref.at[me], local_sems.at[i]) for i, (b_ref, got_ref) in enumerate(zip(ins, outs))]
        give = [pltpu.make_async_remote_copy(src_ref=b_ref, dst_ref=got_ref.at[me], send_sem=send_sems.at[i, k - 1],
                                             recv_sem=recv_sems.at[i, k - 1],
                                             device_id=(x ^ ((k >> 2) & 1), y ^ ((k >> 1) & 1), cc ^ (k & 1)), device_id_type=MESH)
                for i, (b_ref, got_ref) in enumerate(zip(ins, outs)) for k in range(1, N_DEV)]
        return keep, give


def _split_refs(refs, counts):
    out, pos = [], 0
    for n in counts:
        out.append(list(refs[pos:pos + n]))
        pos += n
    return out


def _bind(comms, c_in, c_out, c_scr):
    ins = _split_refs(c_in, [len(c.inputs) for c in comms])
    outs = _split_refs(c_out, [len(c.out_shape) for c in comms])
    scr = _split_refs(c_scr, [len(c.scratch) for c in comms])
    return [(c, (i, o, s)) for c, i, o, s in zip(comms, ins, outs, scr)]


def _call(body, *, name, grid, in_specs, out_specs, out_shape, scratch_shapes, args, comms=()):
    c_in = [a for c in comms for a in c.inputs]
    c_out = [s for c in comms for s in c.out_shape]
    c_scr = [s for c in comms for s in c.scratch]
    counts = [len(in_specs), len(c_in), len(out_shape), len(c_out), len(scratch_shapes), len(c_scr)]

    def full(*refs):
        ins, cin, outs, cout, scr, cscr = _split_refs(refs, counts)
        bound = _bind(comms, cin, cout, cscr)
        if comms:
            def at(steps):
                return functools.reduce(jnp.logical_and, [pl.program_id(ax) == s for ax, s in enumerate(steps)])

            first, last = at([0] * len(grid)), at([n - 1 for n in grid])

            @pl.when(first)
            def _():
                for c, r in bound:
                    c.begin(*r)

            @pl.when(at([(grid[0] - 1) // 2] + [0] * (len(grid) - 1)))
            def _():
                for c, r in bound:
                    c.mid(*r)

            @pl.when(last)
            def _():
                for c, r in bound:
                    c.relay(*r)

        body(*ins, *outs, *scr)
        if comms:
            @pl.when(last)
            def _():
                for c, r in bound:
                    c.end(*r)

    res = pl.pallas_call(
        full, name=name, grid=grid,
        in_specs=list(in_specs) + [ANY] * len(c_in), out_specs=list(out_specs) + [ANY] * len(c_out),
        out_shape=list(out_shape) + c_out, scratch_shapes=list(scratch_shapes) + c_scr,
        compiler_params=_cparams(("arbitrary",) * len(grid)),
    )(*args, *c_in)
    return list(res[:len(out_shape)]), list(res[len(out_shape):])


def _comm_only(comms, name):
    c_in = [a for c in comms for a in c.inputs]
    c_out = [s for c in comms for s in c.out_shape]
    c_scr = [s for c in comms for s in c.scratch]

    def full(*refs):
        cin, cout, cscr = _split_refs(refs, [len(c_in), len(c_out), len(c_scr)])
        bound = _bind(comms, cin, cout, cscr)
        for phase in ("begin", "mid", "relay", "end"):
            for c, r in bound:
                getattr(c, phase)(*r)

    return list(pl.pallas_call(full, name=name, in_specs=[ANY] * len(c_in), out_specs=[ANY] * len(c_out), out_shape=c_out,
                               scratch_shapes=c_scr)(*c_in))


def _my_index(*axes_and_weights):
    return sum(w * lax.axis_index(a) for a, w in axes_and_weights).astype(jnp.int32).reshape(1)


def _pair_sums(grads, gots, name):
    n = len(grads)

    def body(c_ref, *refs):
        for a_ref, b_ref, o_ref in zip(refs[:n], refs[n:2 * n], refs[2 * n:]):
            o_ref[...] = (a_ref[...].astype(F32) + b_ref[...].astype(F32)).astype(BF16)

    def tile(g):
        return pl.BlockSpec((1,) + g.shape[1:], lambda j, c_ref: (j, 0, 0))

    def mine(g, w):
        return pl.BlockSpec((1, None) + g.shape[1:], lambda j, c_ref: (4 * w + j, c_ref[0], 0, 0))

    return list(pl.pallas_call(
        body, name=name,
        grid_spec=pltpu.PrefetchScalarGridSpec(num_scalar_prefetch=1, grid=(4,),
                                               in_specs=[mine(g, w) for g, w in grads] + [tile(g) for g in gots],
                                               out_specs=[tile(g) for g in gots]),
        out_shape=[jax.ShapeDtypeStruct(g.shape, BF16) for g in gots],
        compiler_params=_cparams(("arbitrary",)))(_my_index(("c", 1)), *[g.reshape((-1, 2) + g.shape[1:]) for g, _ in grads], *gots))


def _ffn_fwd(x, gain, gw_gu, gw_d, off_d, name, comms=(), head=None):
    t = x.shape[0]
    n_head = 0 if head is None else 2

    def body(x_ref, g_ref, gu_ref, d_ref, *refs):
        head_in, (o_ref, ab_ref, h_ref), head_out = refs[:n_head], refs[n_head:n_head + 3], refs[n_head + 3:2 * n_head + 3]
        wg, wu, wd, s_scr, sems = refs[2 * n_head + 3:]

        @pl.when(pl.program_id(0) == 0)
        def _():
            cps = _load_rows(gu_ref, wg, 0, R_FF, sems.at[0]) + _load_rows(gu_ref, wu, R_FF, R_FF, sems.at[1]) \
                + _load_rows(d_ref, wd, off_d, R_FF, sems.at[2])
            for o in head_out:
                o[...] = jnp.zeros_like(o)
            for cp in cps:
                cp.wait()

        xv = x_ref[...]
        xn, _ = _rms(xv)
        h = (xn * g_ref[...]).astype(BF16)
        h_ref[...] = h
        for c in range(F // FC):
            rows = pl.ds(c * FC, FC)
            a = _nt(h, wg[rows, :])
            b = _nt(h, wu[rows, :])
            ab_ref[:, c * FC:(c + 1) * FC] = a.astype(BF16)
            ab_ref[:, F + c * FC:F + (c + 1) * FC] = b.astype(BF16)
            s_scr[:, c * FC:(c + 1) * FC] = (a * jax.nn.sigmoid(a) * b).astype(BF16)
        out = xv + 0.5 * _nn(s_scr[...], wd[...])
        if head is None:
            o_ref[...] = out
        else:
            (gf_ref, t_ref), (dg_ref, loss_ref) = head_in, head_out
            gain_f = gf_ref[...]
            yn, r = _rms(out)
            err = yn * gain_f - t_ref[...]
            loss_ref[...] += 0.5 * jnp.sum(jnp.mean(err * err, axis=-1, keepdims=True), axis=0, keepdims=True)
            o_ref[...], dg = _rms_bwd(err * (1.0 / D), yn, r, gain_f)
            dg_ref[...] += dg

    tile = pl.BlockSpec((TM, D), lambda i: (i, 0))
    row = pl.BlockSpec((1, D), lambda i: (0, 0))
    head_specs = [] if head is None else [row, pl.BlockSpec((1, 128), lambda i: (0, 0))]
    head_shapes = [] if head is None else [jax.ShapeDtypeStruct((1, D), F32), jax.ShapeDtypeStruct((1, 128), F32)]
    res, got = _call(
        body, name=name, grid=(t // TM,),
        in_specs=[tile, row, ANY, ANY] + ([] if head is None else [row, tile]),
        out_specs=[tile, pl.BlockSpec((TM, 2 * F), lambda i: (i, 0)), tile] + head_specs,
        out_shape=[jax.ShapeDtypeStruct((t, D), F32), jax.ShapeDtypeStruct((t, 2 * F), BF16), jax.ShapeDtypeStruct((t, D), BF16)] + head_shapes,
        scratch_shapes=[pltpu.VMEM((F, D), BF16)] * 3 + [pltpu.VMEM((TM, F), BF16), pltpu.SemaphoreType.DMA((3, N_DEV))],
        args=(x, gain, gw_gu, gw_d) + (() if head is None else tuple(head)), comms=comms)
    return (res[0], res[1], res[2], got, *res[3:])


def _ffn_bwd_hidden(ab, dout, gw_d, off_d, name, comms=()):
    t = ab.shape[0]
    n_steps = t // TM
    row_chunks = [(r, min(512, F - r)) for r in range(0, F, 512)]

    def tile(w):
        return pl.BlockSpec((TM, w), lambda i: (i, 0))

    def hidden(ab_ref, do_ref, d_ref, dab_ref, dwd_ref, wd, s_scr, acc, sems, out_sem):
        step = pl.program_id(0)

        @pl.when(step == 0)
        def _():
            cps = _load_rows(d_ref, wd, off_d, R_FF, sems)
            acc[...] = jnp.zeros_like(acc)
            for cp in cps:
                cp.wait()

        df = (0.5 * do_ref[...]).astype(BF16)
        for c in range(F // FC):
            a = ab_ref[:, c * FC:(c + 1) * FC].astype(F32)
            b = ab_ref[:, F + c * FC:F + (c + 1) * FC].astype(F32)
            sg = jax.nn.sigmoid(a)
            sl = a * sg
            ds = _nt(df, wd[pl.ds(c * FC, FC), :])
            dab_ref[:, c * FC:(c + 1) * FC] = (ds * b * (sg * (1.0 + a * (1.0 - sg)))).astype(BF16)
            dab_ref[:, F + c * FC:F + (c + 1) * FC] = (ds * sl).astype(BF16)
            s_scr[:, c * FC:(c + 1) * FC] = (sl * b).astype(BF16)
        for r, n in row_chunks:
            acc[r:r + n, :] += _tn(s_scr[:, r:r + n], df)

        @pl.when(step == n_steps - 1)
        def _():
            wd[...] = acc[...].astype(BF16)
            out = pltpu.make_async_copy(wd, dwd_ref, out_sem)
            out.start()
            out.wait()

    (dab, dwd), got = _call(
        hidden, name=name, grid=(n_steps,),
        in_specs=[tile(2 * F), tile(D), ANY], out_specs=[tile(2 * F), ANY],
        out_shape=[jax.ShapeDtypeStruct((t, 2 * F), BF16), jax.ShapeDtypeStruct((F, D), BF16)],
        scratch_shapes=[pltpu.VMEM((F, D), BF16), pltpu.VMEM((TM, F), BF16), pltpu.VMEM((F, D), F32), pltpu.SemaphoreType.DMA((N_DEV,)),
                        pltpu.SemaphoreType.DMA],
        args=(ab, dout, gw_d), comms=comms)
    return (dab, dwd.reshape(N_DEV, R_FF, D)), got


def _ffn_bwd_input(x, gain, dab, dout, gw_gu, name, comms=()):
    t = x.shape[0]

    def body(x_ref, g_ref, dab_ref, do_ref, gu_ref, dx_ref, dg_ref, wgu, sems):
        @pl.when(pl.program_id(0) == 0)
        def _():
            cps = _load_rows(gu_ref, wgu.at[0:F], 0, R_FF, sems.at[0]) + _load_rows(gu_ref, wgu.at[F:2 * F], R_FF, R_FF, sems.at[1])
            dg_ref[...] = jnp.zeros_like(dg_ref)
            for cp in cps:
                cp.wait()

        gain_v = g_ref[...]
        xn, r = _rms(x_ref[...])
        dh = _nn(dab_ref[...], wgu[...])
        dxn, dg = _rms_bwd(dh, xn, r, gain_v)
        dg_ref[...] += dg
        dx_ref[...] = do_ref[...] + dxn

    def tile(w):
        return pl.BlockSpec((TM, w), lambda i: (i, 0))

    row = pl.BlockSpec((1, D), lambda i: (0, 0))
    return _call(
        body, name=name, grid=(t // TM,),
        in_specs=[tile(D), row, tile(2 * F), tile(D), ANY], out_specs=[tile(D), row],
        out_shape=[jax.ShapeDtypeStruct((t, D), F32), jax.ShapeDtypeStruct((1, D), F32)],
        scratch_shapes=[pltpu.VMEM((2 * F, D), BF16), pltpu.SemaphoreType.DMA((2, N_DEV))],
        args=(x, gain, dab, dout, gw_gu), comms=comms)


def _weight_grad(a, b, name, col_off=0, m=None, comms=None, mats=1):
    t = a.shape[0]
    m = a.shape[1] if m is None else m
    n = b.shape[1]
    tmm = 512 if m % 512 == 0 else 256
    first = col_off // tmm

    def body(a_ref, b_ref, o_ref):
        o_ref[...] = _tn(a_ref[...], b_ref[...]).astype(BF16)

    (out,), got = _call(
        body, name=name, grid=(m // tmm,),
        in_specs=[pl.BlockSpec((t, tmm), lambda i: (0, first + i)), pl.BlockSpec((t, n), lambda i: (0, 0))],
        out_specs=[pl.BlockSpec((tmm, n), lambda i: (i, 0))],
        out_shape=[jax.ShapeDtypeStruct((m, n), BF16)], scratch_shapes=[], args=(a, b), comms=comms or ())
    out = out.reshape(mats * N_DEV, m // (mats * N_DEV), n)
    return out if comms is None else (out, got)


def _weight_grad_pieces(pieces, b, name):
    t, n = b.shape
    tmm = 512
    tiles = [(k, j) for k, a in enumerate(pieces) for j in range(a.shape[1] // tmm)]
    m = tmm * len(tiles)

    def body(*refs):
        a_refs, (b_ref, o_ref, slots, b_vmem, sems, b_sem) = refs[:len(pieces)], refs[len(pieces):]

        def fetch(s):
            k, j = tiles[s]
            cp = pltpu.make_async_copy(a_refs[k].at[:, j * tmm:(j + 1) * tmm], slots.at[s % 2], sems.at[s % 2])
            cp.start()
            return cp

        load_b = pltpu.make_async_copy(b_ref, b_vmem, b_sem)
        load_b.start()
        cp = fetch(0)
        load_b.wait()
        for s in range(len(tiles)):
            nxt = fetch(s + 1) if s + 1 < len(tiles) else None
            cp.wait()
            o_ref[s * tmm:(s + 1) * tmm, :] = _tn(slots[s % 2], b_vmem[...]).astype(BF16)
            cp = nxt

    out = pl.pallas_call(
        body, name=name, in_specs=[ANY] * (len(pieces) + 1), out_specs=pl.BlockSpec(memory_space=pltpu.VMEM),
        out_shape=jax.ShapeDtypeStruct((m, n), BF16),
        scratch_shapes=[pltpu.VMEM((2, t, tmm), BF16), pltpu.VMEM((t, n), BF16), pltpu.SemaphoreType.DMA((2,)), pltpu.SemaphoreType.DMA(())],
        compiler_params=_cparams())(*pieces, b)
    return out.reshape(N_DEV, m // N_DEV, n)


def _mix_proj_fwd(x, gain, b_gate, gw, comms=()):
    t = x.shape[0]

    def body(x_ref, g_ref, bg_ref, gw_ref, q_ref, k_ref, v_ref, zs_ref, gt_ref, h_ref, win, sems):
        @pl.when(pl.program_id(0) == 0)
        def _():
            for cp in _load_rows(gw_ref, win, OFF_IN, R_IN, sems):
                cp.wait()

        xn, _ = _rms(x_ref[...])
        h = (xn * g_ref[...]).astype(BF16)
        h_ref[...] = h
        z = _nt(h, win[...])
        q_ref[...] = (z[:, 0:D_ATT] * 0.125).astype(BF16)
        k_ref[...] = z[:, D_ATT:2 * D_ATT].astype(BF16)
        v_ref[...] = z[:, 2 * D_ATT:3 * D_ATT].astype(BF16)
        zs_ref[...] = z[:, 3 * D_ATT:3 * D_ATT + 2 * D_SGU].astype(BF16)
        gt_ref[...] = jax.nn.sigmoid(z[:, 3 * D_ATT + 2 * D_SGU:] + bg_ref[...]).astype(BF16)

    def tile(w):
        return pl.BlockSpec((TM, w), lambda i: (i, 0))

    return _call(
        body, name="mix_proj_fwd", grid=(t // TM,),
        in_specs=[tile(D), pl.BlockSpec((1, D), lambda i: (0, 0)), pl.BlockSpec((1, 2 * D), lambda i: (0, 0)), ANY],
        out_specs=[tile(D_ATT), tile(D_ATT), tile(D_ATT), tile(2 * D_SGU), tile(2 * D), tile(D)],
        out_shape=[jax.ShapeDtypeStruct((t, D_ATT), BF16)] * 3 + [jax.ShapeDtypeStruct((t, 2 * D_SGU), BF16),
                                                                   jax.ShapeDtypeStruct((t, 2 * D), BF16),
                                                                   jax.ShapeDtypeStruct((t, D), BF16)],
        scratch_shapes=[pltpu.VMEM((D_IN, D), BF16), pltpu.SemaphoreType.DMA((N_DEV,))],
        args=(x, gain, b_gate, gw), comms=comms)


def _mix_proj_bwd(dz_pieces, x, gain, dres, gw, comms=()):
    t = x.shape[0]
    widths = [a.shape[1] for a in dz_pieces]
    offs = [sum(widths[:k]) for k in range(len(widths))]

    def body(*refs):
        dz_refs, (x_ref, g_ref, dr_ref, gw_ref, dx_ref, dg_ref, win, sems) = refs[:len(widths)], refs[len(widths):]

        @pl.when(pl.program_id(0) == 0)
        def _():
            cps = _load_rows(gw_ref, win, OFF_IN, R_IN, sems)
            dg_ref[...] = jnp.zeros_like(dg_ref)
            for cp in cps:
                cp.wait()

        dh = sum(_nn(r[...], win[o:o + w, :]) for r, o, w in zip(dz_refs, offs, widths))
        xn, r = _rms(x_ref[...])
        dxn, dg = _rms_bwd(dh, xn, r, g_ref[...])
        dg_ref[...] += dg
        dx_ref[...] = dr_ref[...] + dxn

    def tile(w):
        return pl.BlockSpec((TM, w), lambda i: (i, 0))

    row = pl.BlockSpec((1, D), lambda i: (0, 0))
    return _call(
        body, name="mix_proj_bwd", grid=(t // TM,),
        in_specs=[tile(w) for w in widths] + [tile(D), row, tile(D), ANY], out_specs=[tile(D), row],
        out_shape=[jax.ShapeDtypeStruct((t, D), F32), jax.ShapeDtypeStruct((1, D), F32)],
        scratch_shapes=[pltpu.VMEM((D_IN, D), BF16), pltpu.SemaphoreType.DMA((N_DEV,))],
        args=(*dz_pieces, x, gain, dres, gw), comms=comms)


SKEW_W = KW + QB
N_CAP = 2 * QB - REL_CLIP + 1


def _tables(rel_bias, w_s, b_s):
    cap = rel_bias[:, 2 * REL_CLIP:]
    diag = jnp.concatenate([jnp.broadcast_to(cap, (HEADS, N_CAP)), rel_bias[:, 2 * REL_CLIP - 1::-1],
                            jnp.broadcast_to(cap, (HEADS, SKEW_W - N_CAP - 2 * REL_CLIP))], axis=1)

    def fill(ins, outs):
        (d_ref, ws_ref, bs_ref), (bias_ref, wm2_ref, wmt2_ref, bsx_ref) = ins, outs
        lag = lax.broadcasted_iota(jnp.int32, (QB, KW), 1) // CHUNK - lax.broadcasted_iota(jnp.int32, (QB, KW), 0) // CHUNK
        band = (lag >= 0) & (lag <= N_LEFT)
        for h in range(HEADS):
            rows = jnp.broadcast_to(d_ref[h:h + 1, :], (QB, SKEW_W))
            bias_ref[h] = jnp.where(band, pltpu.roll(rows, 0, 1, stride=1, stride_axis=0)[:, :KW], NEG_INF)
        i = lax.broadcasted_iota(jnp.int32, (SGU_BLOCK, SGU_BLOCK), 0)
        j = lax.broadcasted_iota(jnp.int32, (SGU_BLOCK, SGU_BLOCK), 1)
        causal = (i // CHUNK) >= (j // CHUNK)
        for pair in range(4):
            w0, w1 = jnp.where(causal, ws_ref[2 * pair], 0.0), jnp.where(causal, ws_ref[2 * pair + 1], 0.0)
            wm2_ref[pair] = jnp.concatenate([w0, w1], axis=1).astype(BF16)
            wmt2_ref[pair] = jnp.concatenate([w0.T, w1.T], axis=1).astype(BF16)
        by_row = jnp.concatenate([bs_ref[...], jnp.zeros((SGU_BLOCK - 8, SGU_BLOCK), F32)], axis=0).T
        group = lax.broadcasted_iota(jnp.int32, (SGU_BLOCK, D_SGU), 1) // 64
        spread = jnp.zeros((SGU_BLOCK, D_SGU), F32)
        for g in range(8):
            spread = jnp.where(group == g, by_row[:, g:g + 1], spread)
        bsx_ref[...] = spread

    shapes = [jax.ShapeDtypeStruct((HEADS, QB, KW), F32), jax.ShapeDtypeStruct((4, SGU_BLOCK, 2 * SGU_BLOCK), BF16),
              jax.ShapeDtypeStruct((4, SGU_BLOCK, 2 * SGU_BLOCK), BF16), jax.ShapeDtypeStruct((SGU_BLOCK, D_SGU), F32)]
    return [diag, w_s, b_s], shapes, fill


def _att_specs():
    qspec = pl.BlockSpec((QB, D_ATT), lambda g: (g, 0))
    kspecs = [pl.BlockSpec((QB, D_ATT), lambda g: (jnp.maximum(g - 2, 0), 0)),
              pl.BlockSpec((QB, D_ATT), lambda g: (jnp.maximum(g - 1, 0), 0)), qspec]
    bspec = pl.BlockSpec((HEADS, QB, KW), lambda g: (0, 0, 0))
    return qspec, kspecs, bspec


def _att_probs(qm, kp, bias, valid):
    s = jnp.where(valid, _nt(qm, kp) + bias, NEG_INF)
    e = jnp.exp(s - jnp.max(s, axis=-1, keepdims=True))
    return e / jnp.sum(e, axis=-1, keepdims=True)


def _att_valid():
    g = pl.program_id(0)
    blk = lax.broadcasted_iota(jnp.int32, (QB, KW), 1) // QB
    return (blk + g) >= 2


def _att_fwd(q, k, v, bias, comms=()):
    t = q.shape[0]

    def body(q_ref, k0, k1, k2, v0, v1, v2, b_ref, y_ref):
        valid = _att_valid()
        first = lax.broadcasted_iota(jnp.int32, (1, 128), 1) < 64
        for p in range(HEADS // 2):
            lanes = slice(p * 128, (p + 1) * 128)
            qp = q_ref[:, lanes]
            kp = jnp.concatenate([k0[:, lanes], k1[:, lanes], k2[:, lanes]], axis=0)
            vp = jnp.concatenate([v0[:, lanes], v1[:, lanes], v2[:, lanes]], axis=0)
            out = jnp.zeros((QB, 128), F32)
            for hh in range(2):
                mask = first if hh == 0 else jnp.logical_not(first)
                pr = _att_probs(jnp.where(mask, qp, 0), kp, b_ref[2 * p + hh], valid)
                out = out + _nn(pr.astype(BF16), jnp.where(mask, vp, 0))
            y_ref[:, lanes] = out.astype(BF16)

    qspec, kspecs, bspec = _att_specs()
    (out,), got = _call(
        body, name="att_fwd", grid=(t // QB,),
        in_specs=[qspec] + kspecs + kspecs + [bspec], out_specs=[qspec],
        out_shape=[jax.ShapeDtypeStruct((t, D_ATT), BF16)], scratch_shapes=[],
        args=(q, k, k, k, v, v, v, bias), comms=comms)
    return out, got


def _att_bwd(q, k, v, bias, dy, comms=()):
    t = q.shape[0]
    n_blocks = t // QB

    def body(q_ref, k0, k1, k2, v0, v1, v2, b_ref, dy_ref, dq_ref, dk_ref, dv_ref, db_ref, dk_acc, dv_acc):
        g = pl.program_id(0)

        @pl.when(g == 0)
        def _():
            db_ref[...] = jnp.zeros_like(db_ref)
            dk_acc[...] = jnp.zeros_like(dk_acc)
            dv_acc[...] = jnp.zeros_like(dv_acc)

        valid = _att_valid()
        first = lax.broadcasted_iota(jnp.int32, (1, 128), 1) < 64
        for p in range(HEADS // 2):
            lanes = slice(p * 128, (p + 1) * 128)
            qp = q_ref[:, lanes]
            dyp = dy_ref[:, lanes]
            kp = jnp.concatenate([k0[:, lanes], k1[:, lanes], k2[:, lanes]], axis=0)
            vp = jnp.concatenate([v0[:, lanes], v1[:, lanes], v2[:, lanes]], axis=0)
            dq = jnp.zeros((QB, 128), F32)
            dk = jnp.zeros((KW, 128), F32)
            dv = jnp.zeros((KW, 128), F32)
            for hh in range(2):
                mask = first if hh == 0 else jnp.logical_not(first)
                qm = jnp.where(mask, qp, 0)
                dym = jnp.where(mask, dyp, 0)
                pr = _att_probs(qm, kp, b_ref[2 * p + hh], valid)
                dp = _nt(dym, vp)
                ds = pr * (dp - jnp.sum(dp * pr, axis=-1, keepdims=True))
                db_ref[2 * p + hh] += ds
                dsb = ds.astype(BF16)
                dq = dq + _nn(dsb, jnp.where(mask, kp, 0))
                dk = dk + _tn(dsb, qm)
                dv = dv + _tn(pr.astype(BF16), dym)
            dq_ref[:, lanes] = (dq * 0.125).astype(BF16)
            for j in range(3):
                rows = pl.ds(pl.multiple_of(jnp.maximum(g - 2 + j, 0) * QB, QB), QB)
                dk_acc[rows, lanes] += dk[j * QB:(j + 1) * QB]
                dv_acc[rows, lanes] += dv[j * QB:(j + 1) * QB]

        @pl.when(g == n_blocks - 1)
        def _():
            dk_ref[...] = dk_acc[...].astype(BF16)
            dv_ref[...] = dv_acc[...].astype(BF16)

    qspec, kspecs, bspec = _att_specs()
    full = pl.BlockSpec((t, D_ATT), lambda g: (0, 0))
    return _call(
        body, name="att_bwd", grid=(n_blocks,),
        in_specs=[qspec] + kspecs + kspecs + [bspec, qspec], out_specs=[qspec, full, full, bspec],
        out_shape=[jax.ShapeDtypeStruct((t, D_ATT), BF16)] * 3 + [jax.ShapeDtypeStruct((HEADS, QB, KW), F32)],
        scratch_shapes=[pltpu.VMEM((t, D_ATT), F32)] * 2,
        args=(q, k, k, k, v, v, v, bias, dy), comms=comms)


def _rel_bias_grad(dbias, comms=()):
    c_in = [a for c in comms for a in c.inputs]
    c_out = [s for c in comms for s in c.out_shape]
    c_scr = [s for c in comms for s in c.scratch]

    def body(db_ref, *refs):
        cin, (cs_ref, tot_ref), cout, cscr = _split_refs(refs, [len(c_in), 2, len(c_out), len(c_scr)])
        bound = _bind(comms, cin, cout, cscr)
        for c, r in bound:
            c.begin(*r)
        lane = lax.broadcasted_iota(jnp.int32, (1, SKEW_W), 1)
        capped = (lane < N_CAP) | (lane > KW)
        pad = jnp.zeros((8, QB), F32)
        for h in range(HEADS):
            z = jnp.concatenate([db_ref[h, 0:8, :], pad], axis=1)
            for a in range(1, QB // 8):
                z = z + pltpu.roll(jnp.concatenate([db_ref[h, 8 * a:8 * a + 8, :], pad], axis=1), SKEW_W - 8 * a, 1)
            cs = z[0:1, :]
            for b in range(1, 8):
                cs = cs + pltpu.roll(z[b:b + 1, :], SKEW_W - b, 1)
            cs_ref[h:h + 1, :] = cs
            tot_ref[h:h + 1, :] = jnp.broadcast_to(jnp.sum(jnp.where(capped, cs, 0.0), axis=1, keepdims=True), (1, 128))
        for phase in ("mid", "relay", "end"):
            for c, r in bound:
                getattr(c, phase)(*r)

    whole = pl.BlockSpec(memory_space=pltpu.VMEM)
    cs, tot, *got = pl.pallas_call(
        body, name="rel_bias_grad", in_specs=[whole] + [ANY] * len(c_in), out_specs=[whole, whole] + [ANY] * len(c_out),
        out_shape=[jax.ShapeDtypeStruct((HEADS, SKEW_W), F32), jax.ShapeDtypeStruct((HEADS, 128), F32)] + c_out, scratch_shapes=c_scr,
    )(dbias, *c_in)
    return jnp.concatenate([cs[:, KW:N_CAP - 1:-1], tot[:, :1]], axis=1), got


def _group_stack(blk, first):
    return jnp.concatenate([jnp.where(first, blk, 0), jnp.where(first, 0, blk)], axis=0)


def _sgu_norm(zs_ref, lng, lnb):
    zs = zs_ref[...].astype(F32)
    ga, th = _gelu(zs)
    u, vs = ga[:, :D_SGU], ga[:, D_SGU:]
    mu = jnp.mean(vs, axis=-1, keepdims=True)
    cen = vs - mu
    rstd = lax.rsqrt(jnp.mean(cen * cen, axis=-1, keepdims=True) + EPS)
    xhat = cen * rstd
    return zs, th, u, xhat, rstd, xhat * lng + lnb


def _sgu_mix(vb, wm2_ref, bsx, s_ref):
    first = lax.broadcasted_iota(jnp.int32, (1, 128), 1) < 64
    for n in range(TM // SGU_BLOCK):
        for p in range(4):
            blk = vb[n * 128:(n + 1) * 128, p * 128:(p + 1) * 128]
            s_ref[n * 128:(n + 1) * 128, p * 128:(p + 1) * 128] = _nn(wm2_ref[p], _group_stack(blk, first)) + bsx[:, p * 128:(p + 1) * 128]


def _merge_fwd(x, zs, gt, y_att, lng, lnb, wm2, bsx, gw):
    t = x.shape[0]

    def body(x_ref, zs_ref, gt_ref, ya_ref, lng_ref, lnb_ref, wm2_ref, bsx_ref, gw_ref, xo_ref, ys_ref, mg_ref,
             wbr, wo, s_scr, sems):
        @pl.when(pl.program_id(0) == 0)
        def _():
            for cp in _load_rows(gw_ref, wbr, OFF_BR, R_BR, sems.at[0]) + _load_rows(gw_ref, wo, OFF_WO, R_WO, sems.at[1]):
                cp.wait()

        _, _, u, _, _, vsn = _sgu_norm(zs_ref, lng_ref[...], lnb_ref[...])
        _sgu_mix(vsn.astype(BF16), wm2_ref, bsx_ref[...], s_scr)
        ys = (u * s_scr[...]).astype(BF16)
        ys_ref[...] = ys
        pa = _nt(ya_ref[...], wbr[:, :D_ATT])
        ps = _nt(ys, wbr[:, D_ATT:])
        mg = (gt_ref[:, :D].astype(F32) * pa + gt_ref[:, D:].astype(F32) * ps).astype(BF16)
        mg_ref[...] = mg
        xo_ref[...] = x_ref[...] + _nn(mg, wo[...])

    def tile(w):
        return pl.BlockSpec((TM, w), lambda i: (i, 0))

    def const(shape):
        return pl.BlockSpec(shape, lambda i: (0,) * len(shape))

    return pl.pallas_call(
        body, name="merge_fwd", grid=(t // TM,),
        in_specs=[tile(D), tile(2 * D_SGU), tile(2 * D), tile(D_ATT), const((1, D_SGU)), const((1, D_SGU)),
                  const((4, 128, 256)), const((128, D_SGU)), ANY],
        out_specs=[tile(D), tile(D_SGU), tile(D)],
        out_shape=[jax.ShapeDtypeStruct((t, D), F32), jax.ShapeDtypeStruct((t, D_SGU), BF16), jax.ShapeDtypeStruct((t, D), BF16)],
        scratch_shapes=[pltpu.VMEM((D, D), BF16), pltpu.VMEM((D, D), BF16), pltpu.VMEM((TM, D_SGU), F32),
                        pltpu.SemaphoreType.DMA((2, N_DEV))],
        compiler_params=_cparams(("arbitrary",)),
    )(x, zs, gt, y_att, lng, lnb, wm2, bsx, gw)


def _merge_bwd(dx, gt, y_att, y_sgu, merged, gw):
    t = dx.shape[0]
    n_steps = t // TM

    def body(dx_ref, gt_ref, ya_ref, ys_ref, mg_ref, gw_ref, dzg_ref, dya_ref, dys_ref, dbg_ref, ga_ref, gs_ref, go_ref,
             wbr, wo, acc_a, acc_s, acc_o, sems):
        step = pl.program_id(0)

        @pl.when(step == 0)
        def _():
            cps = _load_rows(gw_ref, wbr, OFF_BR, R_BR, sems.at[0]) + _load_rows(gw_ref, wo, OFF_WO, R_WO, sems.at[1])
            dbg_ref[...] = jnp.zeros_like(dbg_ref)
            for acc in (acc_a, acc_s, acc_o):
                acc[...] = jnp.zeros_like(acc)
            for cp in cps:
                cp.wait()

        dxb = dx_ref[...].astype(BF16)
        acc_o[...] += _tn(mg_ref[...], dxb)
        dm = _nt(dxb, wo[...])
        for half, y_ref, w, acc in ((0, ya_ref, wbr.at[:, :D_ATT], acc_a), (1, ys_ref, wbr.at[:, D_ATT:], acc_s)):
            cols = slice(half * D, (half + 1) * D)
            gate = gt_ref[:, cols].astype(F32)
            branch = _nt(y_ref[...], w[...])
            dzg = dm * branch * gate * (1.0 - gate)
            dbg_ref[:, cols] += jnp.sum(dzg, axis=0, keepdims=True)
            dzg_ref[:, cols] = dzg.astype(BF16)
            dbr = (dm * gate).astype(BF16)
            acc[...] += _tn(dbr, y_ref[...])
            dy = _nn(dbr, w[...])
            if half == 0:
                dya_ref[...] = dy.astype(BF16)
            else:
                dys_ref[...] = dy

        @pl.when(step == n_steps - 1)
        def _():
            ga_ref[...] = acc_a[...].astype(BF16)
            gs_ref[...] = acc_s[...].astype(BF16)
            go_ref[...] = acc_o[...].astype(BF16)

    def tile(w):
        return pl.BlockSpec((TM, w), lambda i: (i, 0))

    def whole(r, c):
        return pl.BlockSpec((r, c), lambda i: (0, 0))

    dzg, dya, dys, dbg, g_ba, g_bs, g_wo = pl.pallas_call(
        body, name="merge_bwd", grid=(n_steps,),
        in_specs=[tile(D), tile(2 * D), tile(D_ATT), tile(D_SGU), tile(D), ANY],
        out_specs=[tile(2 * D), tile(D_ATT), tile(D_SGU), whole(1, 2 * D), whole(D, D_ATT), whole(D, D_SGU), whole(D, D)],
        out_shape=[jax.ShapeDtypeStruct((t, 2 * D), BF16), jax.ShapeDtypeStruct((t, D_ATT), BF16), jax.ShapeDtypeStruct((t, D_SGU), F32),
                   jax.ShapeDtypeStruct((1, 2 * D), F32), jax.ShapeDtypeStruct((D, D_ATT), BF16), jax.ShapeDtypeStruct((D, D_SGU), BF16),
                   jax.ShapeDtypeStruct((D, D), BF16)],
        scratch_shapes=[pltpu.VMEM((D, D), BF16), pltpu.VMEM((D, D), BF16), pltpu.VMEM((D, D_ATT), F32), pltpu.VMEM((D, D_SGU), F32),
                        pltpu.VMEM((D, D), F32), pltpu.SemaphoreType.DMA((2, N_DEV))],
        compiler_params=_cparams(("arbitrary",)),
    )(dx, gt, y_att, y_sgu, merged, gw)
    return dzg, dya, dys, dbg, g_ba.reshape(N_DEV, R_BR, D_ATT), g_bs.reshape(N_DEV, R_BR, D_SGU), g_wo.reshape(N_DEV, R_WO, D)


def _sgu_bwd(zs, dys, lng, lnb, wm2, wmt2, bsx, comms=()):
    t = zs.shape[0]
    n_steps = t // TM

    def body(zs_ref, dys_ref, lng_ref, lnb_ref, wm2_ref, wmt2_ref, bsx_ref, dzs_ref, dw_ref, dbs_ref, dlg_ref, dlb_ref,
             s_scr, dv_scr, ds_acc):
        i = pl.program_id(0)

        @pl.when(i == 0)
        def _():
            dw_ref[...] = jnp.zeros_like(dw_ref)
            dlg_ref[...] = jnp.zeros_like(dlg_ref)
            dlb_ref[...] = jnp.zeros_like(dlb_ref)
            ds_acc[...] = jnp.zeros_like(ds_acc)

        lng = lng_ref[...]
        zs, th, u, xhat, rstd, vsn = _sgu_norm(zs_ref, lng, lnb_ref[...])
        vb = vsn.astype(BF16)
        _sgu_mix(vb, wm2_ref, bsx_ref[...], s_scr)
        dys = dys_ref[...]
        du = dys * s_scr[...]
        ds = dys * u
        dsb = ds.astype(BF16)
        first = lax.broadcasted_iota(jnp.int32, (1, 128), 1) < 64
        acc = jnp.zeros((SGU_BLOCK, D_SGU), F32)
        for n in range(TM // SGU_BLOCK):
            rows = slice(n * 128, (n + 1) * 128)
            acc = acc + ds[rows]
            for p in range(4):
                lanes = slice(p * 128, (p + 1) * 128)
                stack = _group_stack(dsb[rows, lanes], first)
                dv_scr[rows, lanes] = _nn(wmt2_ref[p], stack)
                dw_ref[p] += _nt(stack, vb[rows, lanes])
        ds_acc[...] += acc
        dvsn = dv_scr[...]
        dlg_ref[...] += jnp.sum(dvsn * xhat, axis=0, keepdims=True)
        dlb_ref[...] += jnp.sum(dvsn, axis=0, keepdims=True)
        dxh = dvsn * lng
        dvs = rstd * (dxh - jnp.mean(dxh, axis=-1, keepdims=True) - xhat * jnp.mean(dxh * xhat, axis=-1, keepdims=True))
        dga = jnp.concatenate([du, dvs], axis=1)
        dzs_ref[...] = (dga * _gelu_grad(zs, th)).astype(BF16)

        @pl.when(i == n_steps - 1)
        def _():
            r = lax.broadcasted_iota(jnp.int32, (256, 128), 0) % SGU_BLOCK
            c = lax.broadcasted_iota(jnp.int32, (256, 128), 1)
            keep = (r // CHUNK) >= (c // CHUNK)
            for p in range(4):
                dw_ref[p] = jnp.where(keep, dw_ref[p], 0.0)
            total = ds_acc[...]
            grp = lax.broadcasted_iota(jnp.int32, (SGU_BLOCK, D_SGU), 1) // 64
            lane = lax.broadcasted_iota(jnp.int32, (SGU_BLOCK, 128), 1)
            out = jnp.zeros((SGU_BLOCK, 128), F32)
            for gi in range(8):
                out = jnp.where(lane == gi, jnp.sum(jnp.where(grp == gi, total, 0.0), axis=1, keepdims=True), out)
            dbs_ref[...] = out

    def tile(w):
        return pl.BlockSpec((TM, w), lambda i: (i, 0))

    def const(shape):
        return pl.BlockSpec(shape, lambda i: (0,) * len(shape))

    return _call(
        body, name="sgu_bwd", grid=(n_steps,),
        in_specs=[tile(2 * D_SGU), tile(D_SGU), const((1, D_SGU)), const((1, D_SGU)), const((4, 128, 256)), const((4, 128, 256)),
                  const((128, D_SGU))],
        out_specs=[tile(2 * D_SGU), const((4, 256, 128)), const((128, 128)), const((1, D_SGU)), const((1, D_SGU))],
        out_shape=[jax.ShapeDtypeStruct((t, 2 * D_SGU), BF16), jax.ShapeDtypeStruct((4, 256, 128), F32),
                   jax.ShapeDtypeStruct((128, 128), F32), jax.ShapeDtypeStruct((1, D_SGU), F32), jax.ShapeDtypeStruct((1, D_SGU), F32)],
        scratch_shapes=[pltpu.VMEM((TM, D_SGU), F32), pltpu.VMEM((TM, D_SGU), F32), pltpu.VMEM((SGU_BLOCK, D_SGU), F32)],
        args=(zs, dys, lng, lnb, wm2, wmt2, bsx), comms=comms)


def _adamw(g, w, m, v):
    m = ADAM_B1 * m + (1.0 - ADAM_B1) * g
    v = ADAM_B2 * v + (1.0 - ADAM_B2) * (g * g)
    m_hat = m / (1.0 - ADAM_B1 ** ADAM_STEP)
    v_hat = v / (1.0 - ADAM_B2 ** ADAM_STEP)
    return -ADAM_LR * (m_hat / (jnp.sqrt(v_hat) + ADAM_EPS) + ADAM_WD * w), m, v


def _adamw_matrices(items, name):
    n = len(items)
    c = items[0][0].shape[2]
    tc = 256

    def body(own_ref, *refs):
        for i, (_, _, _, _, _, transposed) in enumerate(items):
            p_ref, s_ref, w_ref, m_ref, v_ref = refs[5 * i:5 * i + 5]
            g = p_ref[0].astype(F32) + p_ref[1].astype(F32) + p_ref[2].astype(F32) + s_ref[...].astype(F32)
            g = g.T if transposed else g
            res = (g,) + _adamw(g, w_ref[...], m_ref[...], v_ref[...])
            for o_ref, val in zip(refs[5 * n + 4 * i:5 * n + 4 * i + 4], res):
                o_ref[...] = val

    in_specs, out_specs, out_shape, args = [], [], [], []
    for parts, sums, w, m, v, transposed in items:
        r = parts.shape[1]
        own = pl.BlockSpec((None, tc, r), lambda i, o: (0, i, 0)) if transposed else pl.BlockSpec((None, r, tc), lambda i, o: (0, 0, i))
        in_specs += [pl.BlockSpec((3, r, tc), lambda i, o: (0, 0, i)), pl.BlockSpec((None, r, tc), lambda i, o: (o[0], 0, i)), own, own, own]
        out_specs += [own] * 4
        out_shape += [jax.ShapeDtypeStruct(w.shape, F32)] * 4
        args += [parts, sums, w, m, v]
    res = pl.pallas_call(
        body, name=name,
        grid_spec=pltpu.PrefetchScalarGridSpec(num_scalar_prefetch=1, grid=(c // tc,), in_specs=in_specs, out_specs=out_specs),
        out_shape=out_shape, compiler_params=_cparams(("arbitrary",)),
    )(_my_index(("x", 2), ("y", 1)), *args)
    return [list(res[4 * i:4 * i + 4]) for i in range(n)]


_SMALL_2D = {"norm_ffn1": (1, D), "norm_mix": (1, D), "norm_ffn2": (1, D), "norm_final": (1, D), "b_gate": (1, 2 * D),
             "sgu_ln_g": (1, D_SGU), "sgu_ln_b": (1, D_SGU), "sgu_b_s": (8, SGU_BLOCK), "rel_bias": (HEADS, N_REL),
             "sgu_w_s": (8 * SGU_BLOCK, SGU_BLOCK)}


def _adamw_small(parts, loss_parts, p):
    names = list(parts)
    n = len(names)

    def body(*refs):
        got, loss_got, wmv, outs, loss_out = refs[:n], refs[n], refs[n + 1:4 * n + 1], refs[4 * n + 1:8 * n + 1], refs[8 * n + 1]
        for i, name in enumerate(names):
            g = got[i][0]
            for k in range(1, N_DEV):
                g = g + got[i][k]
            if name == "sgu_b_s":
                g = g.T[0:8, :]
            res = (g,) + _adamw(g, wmv[3 * i][...], wmv[3 * i + 1][...], wmv[3 * i + 2][...])
            for o_ref, val in zip(outs[4 * i:4 * i + 4], res):
                o_ref[...] = val
        total = loss_got[0]
        for k in range(1, N_DEV):
            total = total + loss_got[k]
        loss_out[...] = total

    wmv = [p[pre + name].reshape(_SMALL_2D[name]) for name in names for pre in ("", "m_", "v_")]
    res = pl.pallas_call(
        body, name="adamw_small",
        out_shape=[jax.ShapeDtypeStruct(_SMALL_2D[name], F32) for name in names for _ in range(4)] + [jax.ShapeDtypeStruct((1, 128), F32)],
        compiler_params=_cparams())(*[parts[name] for name in names], loss_parts, *wmv)
    return [{name: res[4 * i + j].reshape(p[name].shape) for i, name in enumerate(names)} for j in range(4)], res[-1]


def _pack_rows(slabs, name, comms=(), extra=None):
    flat = [a for groups in slabs for grp in groups for a, _ in grp]
    rows = [[grp[0][0].shape[2] if grp[0][1] else grp[0][0].shape[1] for grp in groups] for groups in slabs]
    e_in, e_out, fill = extra if extra is not None else ([], [], None)
    c_in = [a for c in comms for a in c.inputs]
    c_out = [s for c in comms for s in c.out_shape]
    c_scr = [s for c in comms for s in c.scratch]

    def body(*refs):
        ins, ein, cin, outs, eout, cout, cscr = _split_refs(
            refs, [len(flat), len(e_in), len(c_in), len(slabs), len(e_out), len(c_out), len(c_scr)])
        bound = _bind(comms, cin, cout, cscr)
        for c, r in bound:
            c.begin(*r)
        pos = 0
        for groups, slab_rows, o_ref in zip(slabs, rows, outs):
            off = 0
            for grp, r in zip(groups, slab_rows):
                vals = []
                for _, transposed in grp:
                    val = ins[pos][0]
                    vals.append(val.T if transposed else val)
                    pos += 1
                o_ref[off:off + r, :] = (vals[0] if len(vals) == 1 else jnp.concatenate(vals, axis=1)).astype(BF16)
                off += r
        if fill is not None:
            fill(ein, eout)
        for phase in ("mid", "relay", "end"):
            for c, r in bound:
                getattr(c, phase)(*r)

    whole = pl.BlockSpec(memory_space=pltpu.VMEM)
    n_plain = len(slabs) + len(e_out)
    res = pl.pallas_call(
        body, name=name, in_specs=[whole] * (len(flat) + len(e_in)) + [ANY] * len(c_in), out_specs=[whole] * n_plain + [ANY] * len(c_out),
        out_shape=[jax.ShapeDtypeStruct((sum(r), D), BF16) for r in rows] + list(e_out) + c_out, scratch_shapes=c_scr,
        compiler_params=_cparams())(*flat, *e_in, *c_in)
    return list(res[:len(slabs)]), list(res[len(slabs):n_plain]), list(res[n_plain:])


def _step(x, target, p):
    n1, nm, n2 = p["norm_ffn1"], p["norm_mix"], p["norm_ffn2"]
    nf = p["norm_final"].reshape(1, D)
    lng, lnb = p["sgu_ln_g"], p["sgu_ln_b"]

    def chip_sums(grads, name):
        gots = _comm_only([_SiblingSwap(grads)], "swap_" + name)
        return _pair_sums(grads, gots, "pair_sums_" + name)

    def as_rows(a):
        return jnp.swapaxes(a, 1, 2)

    def updates(parts, sums, names):
        res = {}
        for cols in sorted({pt.shape[2] for pt in parts}):
            items, group = [], [(pt, sm, n) for pt, sm, n in zip(parts, sums, names) if pt.shape[2] == cols]
            for pt, sm, n in group:
                view = as_rows if p[n].shape[1:] != pt.shape[1:] and p[n].shape[2] > 128 else (lambda a: a)
                transposed = p[n].shape[1:] != pt.shape[1:] and p[n].shape[2] <= 128
                items.append((pt, sm, view(p[n]), view(p["m_" + n]), view(p["v_" + n]), transposed))
            for (pt, sm, n), four in zip(group, _adamw_matrices(items, "adamw_" + group[0][2])):
                res[n] = [as_rows(o) if o.shape != p[n].shape else o for o in four]
        return res

    (rows1,), _, _ = _pack_rows([[[(as_rows(p["ffn1_w_gate"]), False)], [(as_rows(p["ffn1_w_up"]), False)], [(p["ffn1_w_down"], False)]]],
                                "pack_ffn1")
    (rows_m, rows2d, rows2gu), (bias, wm2, wmt2, bsx), (gw1,) = _pack_rows(
        [[[(as_rows(p["w_in"]), False)], [(p["w_branch_att"], True), (p["w_branch_sgu"], True)], [(p["w_out"], False)]],
         [[(p["ffn2_w_down"], False)]],
         [[(as_rows(p["ffn2_w_gate"]), False)], [(as_rows(p["ffn2_w_up"]), False)]]],
        "gather_ffn1_pack_rest", [_Gather(rows1)], _tables(p["rel_bias"][0], p["sgu_w_s"][0], p["sgu_b_s"][0]))
    x1, ab1, h1, (gwm,) = _ffn_fwd(x, n1, gw1, gw1, 2 * R_FF, "ffn1_fwd", [_Gather(rows_m)])
    (q, k, v, zs, gt, h2), (gw2d,) = _mix_proj_fwd(x1, nm, p["b_gate"], gwm, [_Gather(rows2d)])
    y_att, (gw2gu,) = _att_fwd(q, k, v, bias, [_Gather(rows2gu)])
    x2, y_sgu, merged = _merge_fwd(x1, zs, gt, y_att, lng, lnb, wm2, bsx, gwm)
    dx3, ab2, hb, _, d_nf, loss = _ffn_fwd(x2, n2, gw2gu, gw2d, 0, "ffn2_fwd", head=(nf, target))

    (dab, g_down), _ = _ffn_bwd_hidden(ab2, dx3, gw2d, 0, "ffn2_bwd_hidden")
    (dx2, d_n2), _ = _ffn_bwd_input(x2, n2, dab, dx3, gw2gu, "ffn2_bwd")
    g_gu = _weight_grad(dab, hb, "ffn2_dw_gate_up", mats=2)
    dzg, dya, dys, d_bg, g_ba, g_bs, g_wo = _merge_bwd(dx2, gt, y_att, y_sgu, merged, gwm)
    g_late = [(g_gu, 0), (g_gu, 1), (g_down, 0), (g_ba, 0), (g_bs, 0), (g_wo, 0)]
    late = ("ffn2_w_gate", "ffn2_w_up", "ffn2_w_down", "w_branch_att", "w_branch_sgu", "w_out")
    (dzs, d_wm, d_bs, d_lng, d_lnb), gots_late = _sgu_bwd(zs, dys, lng, lnb, wm2, wmt2, bsx, [_SiblingSwap(g_late)])
    sums_late = _pair_sums(g_late, gots_late, "pair_sums_late")
    (dq, dk, dv, d_bias), parts_late = _att_bwd(q, k, v, bias, dya, [_ChipScatter(sums_late)])
    dz = [dq, dk, dv, dzs, dzg]
    g_in = [(_weight_grad_pieces(dz, h2, "dw_in"), 0)]
    d_rel, got_in = _rel_bias_grad(d_bias, [_SiblingSwap(g_in)])
    sums_in = _pair_sums(g_in, got_in, "pair_sums_w_in")
    small = {"norm_ffn2": d_n2, "norm_final": d_nf, "b_gate": d_bg, "sgu_ln_g": d_lng, "sgu_ln_b": d_lnb, "sgu_b_s": d_bs,
             "rel_bias": d_rel, "sgu_w_s": d_wm.reshape(_SMALL_2D["sgu_w_s"])}
    (dx1, d_nm), (part_in, *small_parts, loss_parts) = _mix_proj_bwd(
        dz, x1, nm, dx2, gwm, [_ChipScatter(sums_in), _AllToAll(list(small.values()) + [loss])])
    big = updates(parts_late + [part_in], sums_late + sums_in, late + ("w_in",))

    (dab, g_down), _ = _ffn_bwd_hidden(ab1, dx1, gw1, 2 * R_FF, "ffn1_bwd_hidden")
    sums_d = chip_sums([(g_down, 0)], "ffn1_down")
    g_gu, (parts_d,) = _weight_grad(dab, h1, "ffn1_dw_gate_up", comms=[_ChipScatter(sums_d)], mats=2)
    sums_gu = chip_sums([(g_gu, 0), (g_gu, 1)], "ffn1_gate_up")
    (dx0, d_n1), parts_gu = _ffn_bwd_input(x, n1, dab, dx1, gw1, "ffn1_bwd", [_ChipScatter(sums_gu)])
    n1_parts, nm_parts = _comm_only([_AllToAll([d_n1, d_nm])], "gather_norm_ffn1")
    big.update(updates([parts_d] + parts_gu, sums_d + sums_gu, ("ffn1_w_down", "ffn1_w_gate", "ffn1_w_up")))
    out_s, loss_sum = _adamw_small(dict(zip(small, small_parts), norm_ffn1=n1_parts, norm_mix=nm_parts), loss_parts, p)
    return dx0, loss_sum[0, 0], [{**{n: four[i] for n, four in big.items()}, **s} for i, s in enumerate(out_s)]


_OUT_ORDER = ("norm_ffn1", "ffn1_w_gate", "ffn1_w_up", "ffn1_w_down", "norm_mix", "w_in", "b_gate", "rel_bias", "sgu_ln_g", "sgu_ln_b",
              "sgu_w_s", "sgu_b_s", "w_branch_att", "w_branch_sgu", "w_out", "norm_ffn2", "ffn2_w_gate", "ffn2_w_up", "ffn2_w_down",
              "norm_final")


def kernel(x, norm_ffn1, ffn1_w_gate, ffn1_w_up, ffn1_w_down, norm_mix, w_in, b_gate, rel_bias, sgu_ln_g, sgu_ln_b, sgu_w_s, sgu_b_s, w_branch_att, w_branch_sgu, w_out, norm_ffn2, ffn2_w_gate, ffn2_w_up, ffn2_w_down, norm_final, loss_target, m_norm_ffn1, m_ffn1_w_gate, m_ffn1_w_up, m_ffn1_w_down, m_norm_mix, m_w_in, m_b_gate, m_rel_bias, m_sgu_ln_g, m_sgu_ln_b, m_sgu_w_s, m_sgu_b_s, m_w_branch_att, m_w_branch_sgu, m_w_out, m_norm_ffn2, m_ffn2_w_gate, m_ffn2_w_up, m_ffn2_w_down, m_norm_final, v_norm_ffn1, v_ffn1_w_gate, v_ffn1_w_up, v_ffn1_w_down, v_norm_mix, v_w_in, v_b_gate, v_rel_bias, v_sgu_ln_g, v_sgu_ln_b, v_sgu_w_s, v_sgu_b_s, v_w_branch_att, v_w_branch_sgu, v_w_out, v_norm_ffn2, v_ffn2_w_gate, v_ffn2_w_up, v_ffn2_w_down, v_norm_final):
    args = dict(locals())
    dx, loss, outs = _step(x[0], loss_target[0], {pre + n: args[pre + n] for pre in ("", "m_", "v_") for n in _OUT_ORDER})
    return (loss, dx[None], *[o[n] for o in outs for n in _OUT_ORDER])
```

```python
import functools

import jax
import jax.numpy as jnp
from jax import lax
from jax.experimental import pallas as pl
from jax.experimental.pallas import tpu as pltpu

F32 = jnp.float32
BF16 = jnp.bfloat16

N_DEV = 8
D = 1024
F = 2816
D_ATT = 512
D_SGU = 512
D_IN = 4608
HEADS = 8
CHUNK = 64
N_LEFT = 8
REL_CLIP = 256
N_REL = 2 * REL_CLIP + 1
SGU_BLOCK = 128
EPS = 1e-6
NEG_INF = -1e30
QB = 256
KW = 3 * QB

R_FF, R_IN, R_BR, R_WO = F // N_DEV, D_IN // N_DEV, D // N_DEV, D // N_DEV
OFF_IN, OFF_BR, OFF_WO = 0, R_IN, R_IN + R_BR

FC = 256
TM = 512
VMEM_LIMIT = 56 * 1024 * 1024

ADAM_LR, ADAM_B1, ADAM_B2, ADAM_EPS, ADAM_WD, ADAM_STEP = 0.001, 0.9, 0.999, 1e-08, 0.01, 10

MESH = pl.DeviceIdType.MESH
ANY = pl.BlockSpec(memory_space=pl.ANY)


def _nt(a, b):
    return lax.dot_general(a, b, (((1,), (1,)), ((), ())), preferred_element_type=F32)


def _tn(a, b):
    return lax.dot_general(a, b, (((0,), (0,)), ((), ())), preferred_element_type=F32)


def _nn(a, b):
    return jnp.dot(a, b, preferred_element_type=F32)


def _cparams(sem=None):
    return pltpu.CompilerParams(dimension_semantics=sem, vmem_limit_bytes=VMEM_LIMIT)


def _load_rows(gw_ref, dst, off, rows, sems):
    copies = [pltpu.make_async_copy(gw_ref.at[k, pl.ds(off, rows), :], dst.at[pl.ds(k * rows, rows), :], sems.at[k])
              for k in range(N_DEV)]
    for cp in copies:
        cp.start()
    return copies


def _rms(xv):
    r = lax.rsqrt(jnp.mean(xv * xv, axis=-1, keepdims=True) + EPS)
    return xv * r, r


def _rms_bwd(dh, xn, r, gain):
    dxn = dh * gain
    dx = r * (dxn - xn * jnp.mean(dxn * xn, axis=-1, keepdims=True))
    return dx, jnp.sum(dh * xn, axis=0, keepdims=True)


def _gelu(x):
    t = jnp.tanh(0.7978845608028654 * (x + 0.044715 * x * x * x))
    return 0.5 * x * (1.0 + t), t


def _gelu_grad(x, t):
    return 0.5 * (1.0 + t) + 0.5 * x * (1.0 - t * t) * 0.7978845608028654 * (1.0 + 3.0 * 0.044715 * x * x)


def _place():
    x, y, cc = lax.axis_index("x"), lax.axis_index("y"), lax.axis_index("c")
    return x, y, cc, [(1 - x, y), (x, 1 - y), (1 - x, 1 - y)]


class _Gather:
    def __init__(self, shard):
        self.inputs = [shard]
        self.out_shape = [jax.ShapeDtypeStruct((N_DEV,) + shard.shape, shard.dtype)]
        self.scratch = [pltpu.SemaphoreType.DMA((7,)), pltpu.SemaphoreType.DMA((7,)), pltpu.SemaphoreType.DMA]

    def _copies(self, ins, outs, scr):
        (x_ref,), (out_ref,), (send_sems, recv_sems, local_sem) = ins, outs, scr
        x, y, cc, chips = _place()

        def slab(px, py, pc):
            return out_ref.at[4 * px + 2 * py + pc]

        def copy(k, block, to, src=None):
            return pltpu.make_async_remote_copy(
                src_ref=slab(*block) if src is None else src, dst_ref=slab(*block),
                send_sem=send_sems.at[k], recv_sem=recv_sems.at[k], device_id=to, device_id_type=MESH)

        me, sibling = (x, y, cc), (x, y, 1 - cc)
        x_nbr, y_nbr, diagonal = chips
        mine = pltpu.make_async_copy(x_ref, slab(*me), local_sem)
        first = [copy(0, me, sibling, src=x_ref), copy(1, me, (*x_nbr, cc), src=x_ref), copy(2, me, (*y_nbr, cc), src=x_ref)]
        neighbours = [copy(1, (*x_nbr, cc), me), copy(2, (*y_nbr, cc), me)]
        second_hand = copy(3, (x ^ (1 - cc), y ^ cc, cc), (x ^ cc, y ^ (1 - cc), cc))
        from_diagonal = copy(3, (*diagonal, cc), me)
        passed = [copy(4 + j, (*chip, cc), sibling) for j, chip in enumerate(chips)]
        from_sibling = [copy(0, sibling, me)] + [copy(4 + j, (*chip, 1 - cc), me) for j, chip in enumerate(chips)]
        return mine, first, neighbours, second_hand, from_diagonal, passed, from_sibling

    def begin(self, *refs):
        mine, first = self._copies(*refs)[:2]
        mine.start()
        for cp in first:
            cp.start()

    def mid(self, *refs):
        _, _, neighbours, second_hand, _, passed, _ = self._copies(*refs)
        for cp in neighbours:
            cp.wait_recv()
        second_hand.start()
        passed[0].start()
        passed[1].start()

    def relay(self, *refs):
        _, _, _, _, from_diagonal, passed, _ = self._copies(*refs)
        from_diagonal.wait_recv()
        passed[2].start()

    def end(self, *refs):
        mine, first, _, second_hand, _, passed, from_sibling = self._copies(*refs)
        for cp in from_sibling:
            cp.wait_recv()
        for cp in first + [second_hand] + passed:
            cp.wait_send()
        mine.wait()


class _Direct:
    def begin(self, *refs):
        keep, give = self._copies(*refs)
        for cp in keep + give:
            cp.start()

    def mid(self, *refs):
        pass

    def relay(self, *refs):
        pass

    def end(self, *refs):
        keep, give = self._copies(*refs)
        for cp in give:
            cp.wait_recv()
        for cp in give:
            cp.wait_send()
        for cp in keep:
            cp.wait()


class _SiblingSwap(_Direct):
    def __init__(self, grads):
        n = len(grads)
        self.which = [w for _, w in grads]
        self.inputs = [g for g, _ in grads]
        self.out_shape = [jax.ShapeDtypeStruct((4,) + g.shape[1:], g.dtype) for g, _ in grads]
        self.scratch = [pltpu.SemaphoreType.DMA((n, 4)), pltpu.SemaphoreType.DMA((n, 4))]

    def _copies(self, ins, outs, scr):
        send_sems, recv_sems = scr
        x, y, cc, _ = _place()
        return [], [pltpu.make_async_remote_copy(src_ref=g_ref.at[N_DEV * w + 2 * j + 1 - cc], dst_ref=got_ref.at[j],
                                                 send_sem=send_sems.at[i, j], recv_sem=recv_sems.at[i, j], device_id=(x, y, 1 - cc),
                                                 device_id_type=MESH)
                    for i, (g_ref, got_ref, w) in enumerate(zip(ins, outs, self.which)) for j in range(4)]


class _ChipScatter(_Direct):
    def __init__(self, sums):
        n = len(sums)
        self.inputs = list(sums)
        self.out_shape = [jax.ShapeDtypeStruct((3,) + s.shape[1:], s.dtype) for s in sums]
        self.scratch = [pltpu.SemaphoreType.DMA((n, 3)), pltpu.SemaphoreType.DMA((n, 3))]

    def _copies(self, ins, outs, scr):
        send_sems, recv_sems = scr
        _, _, cc, chips = _place()
        return [], [pltpu.make_async_remote_copy(src_ref=s_ref.at[2 * px + py], dst_ref=got_ref.at[j], send_sem=send_sems.at[i, j],
                                                 recv_sem=recv_sems.at[i, j], device_id=(px, py, cc), device_id_type=MESH)
                    for i, (s_ref, got_ref) in enumerate(zip(ins, outs)) for j, (px, py) in enumerate(chips)]


class _AllToAll(_Direct):
    def __init__(self, blocks):
        n = len(blocks)
        self.inputs = list(blocks)
        self.out_shape = [jax.ShapeDtypeStruct((N_DEV,) + b.shape, b.dtype) for b in blocks]
        self.scratch = [pltpu.SemaphoreType.DMA((n, 7)), pltpu.SemaphoreType.DMA((n, 7)), pltpu.SemaphoreType.DMA((n,))]

    def _copies(self, ins, outs, scr):
        send_sems, recv_sems, local_sems = scr
        x, y, cc, _ = _place()
        me = 4 * x + 2 * y + cc
        keep = [pltpu.make_async_copy(b_ref, got_ref.at[me], local_sems.at[i]) for i, (b_ref, got_ref) in enumerate(zip(ins, outs))]
        give = [pltpu.make_async_remote_copy(src_ref=b_ref, dst_ref=got_ref.at[me], send_sem=send_sems.at[i, k - 1],
                                             recv_sem=recv_sems.at[i, k - 1],
                                             device_id=(x ^ ((k >> 2) & 1), y ^ ((k >> 1) & 1), cc ^ (k & 1)), device_id_type=MESH)
                for i, (b_ref, got_ref) in enumerate(zip(ins, outs)) for k in range(1, N_DEV)]
        return keep, give


def _split_refs(refs, counts):
    out, pos = [], 0
    for n in counts:
        out.append(list(refs[pos:pos + n]))
        pos += n
    return out


def _bind(comms, c_in, c_out, c_scr):
    ins = _split_refs(c_in, [len(c.inputs) for c in comms])
    outs = _split_refs(c_out, [len(c.out_shape) for c in comms])
    scr = _split_refs(c_scr, [len(c.scratch) for c in comms])
    return [(c, (i, o, s)) for c, i, o, s in zip(comms, ins, outs, scr)]


def _call(body, *, name, grid, in_specs, out_specs, out_shape, scratch_shapes, args, comms=()):
    c_in = [a for c in comms for a in c.inputs]
    c_out = [s for c in comms for s in c.out_shape]
    c_scr = [s for c in comms for s in c.scratch]
    counts = [len(in_specs), len(c_in), len(out_shape), len(c_out), len(scratch_shapes), len(c_scr)]

    def full(*refs):
        ins, cin, outs, cout, scr, cscr = _split_refs(refs, counts)
        bound = _bind(comms, cin, cout, cscr)
        if comms:
            def at(steps):
                return functools.reduce(jnp.logical_and, [pl.program_id(ax) == s for ax, s in enumerate(steps)])

            first, last = at([0] * len(grid)), at([n - 1 for n in grid])

            @pl.when(first)
            def _():
                for c, r in bound:
                    c.begin(*r)

            @pl.when(at([(grid[0] - 1) // 2] + [0] * (len(grid) - 1)))
            def _():
                for c, r in bound:
                    c.mid(*r)

            @pl.when(last)
            def _():
                for c, r in bound:
                    c.relay(*r)

        body(*ins, *outs, *scr)
        if comms:
            @pl.when(last)
            def _():
                for c, r in bound:
                    c.end(*r)

    res = pl.pallas_call(
        full, name=name, grid=grid,
        in_specs=list(in_specs) + [ANY] * len(c_in), out_specs=list(out_specs) + [ANY] * len(c_out),
        out_shape=list(out_shape) + c_out, scratch_shapes=list(scratch_shapes) + c_scr,
        compiler_params=_cparams(("arbitrary",) * len(grid)),
    )(*args, *c_in)
    return list(res[:len(out_shape)]), list(res[len(out_shape):])


def _comm_only(comms, name):
    c_in = [a for c in comms for a in c.inputs]
    c_out = [s for c in comms for s in c.out_shape]
    c_scr = [s for c in comms for s in c.scratch]

    def full(*refs):
        cin, cout, cscr = _split_refs(refs, [len(c_in), len(c_out), len(c_scr)])
        bound = _bind(comms, cin, cout, cscr)
        for phase in ("begin", "mid", "relay", "end"):
            for c, r in bound:
                getattr(c, phase)(*r)

    return list(pl.pallas_call(full, name=name, in_specs=[ANY] * len(c_in), out_specs=[ANY] * len(c_out), out_shape=c_out,
                               scratch_shapes=c_scr)(*c_in))


def _my_index(*axes_and_weights):
    return sum(w * lax.axis_index(a) for a, w in axes_and_weights).astype(jnp.int32).reshape(1)


def _pair_sums(grads, gots, name):
    n = len(grads)

    def body(c_ref, *refs):
        for a_ref, b_ref, o_ref in zip(refs[:n], refs[n:2 * n], refs[2 * n:]):
            o_ref[...] = (a_ref[...].astype(F32) + b_ref[...].astype(F32)).astype(BF16)

    def tile(g):
        return pl.BlockSpec((1,) + g.shape[1:], lambda j, c_ref: (j, 0, 0))

    def mine(g, w):
        return pl.BlockSpec((1, None) + g.shape[1:], lambda j, c_ref: (4 * w + j, c_ref[0], 0, 0))

    return list(pl.pallas_call(
        body, name=name,
        grid_spec=pltpu.PrefetchScalarGridSpec(num_scalar_prefetch=1, grid=(4,),
                                               in_specs=[mine(g, w) for g, w in grads] + [tile(g) for g in gots],
                                               out_specs=[tile(g) for g in gots]),
        out_shape=[jax.ShapeDtypeStruct(g.shape, BF16) for g in gots],
        compiler_params=_cparams(("arbitrary",)))(_my_index(("c", 1)), *[g.reshape((-1, 2) + g.shape[1:]) for g, _ in grads], *gots))


def _ffn_fwd(x, gain, gw_gu, gw_d, off_d, name, comms=(), head=None):
    t = x.shape[0]
    n_head = 0 if head is None else 2

    def body(x_ref, g_ref, gu_ref, d_ref, *refs):
        head_in, (o_ref, ab_ref, h_ref), head_out = refs[:n_head], refs[n_head:n_head + 3], refs[n_head + 3:2 * n_head + 3]
        wg, wu, wd, s_scr, sems = refs[2 * n_head + 3:]

        @pl.when(pl.program_id(0) == 0)
        def _():
            cps = _load_rows(gu_ref, wg, 0, R_FF, sems.at[0]) + _load_rows(gu_ref, wu, R_FF, R_FF, sems.at[1]) \
                + _load_rows(d_ref, wd, off_d, R_FF, sems.at[2])
            for o in head_out:
                o[...] = jnp.zeros_like(o)
            for cp in cps:
                cp.wait()

        xv = x_ref[...]
        xn, _ = _rms(xv)
        h = (xn * g_ref[...]).astype(BF16)
        h_ref[...] = h
        for c in range(F // FC):
            rows = pl.ds(c * FC, FC)
            a = _nt(h, wg[rows, :])
            b = _nt(h, wu[rows, :])
            ab_ref[:, c * FC:(c + 1) * FC] = a.astype(BF16)
            ab_ref[:, F + c * FC:F + (c + 1) * FC] = b.astype(BF16)
            s_scr[:, c * FC:(c + 1) * FC] = (a * jax.nn.sigmoid(a) * b).astype(BF16)
        out = xv + 0.5 * _nn(s_scr[...], wd[...])
        if head is None:
            o_ref[...] = out
        else:
            (gf_ref, t_ref), (dg_ref, loss_ref) = head_in, head_out
            gain_f = gf_ref[...]
            yn, r = _rms(out)
            err = yn * gain_f - t_ref[...]
            loss_ref[...] += 0.5 * jnp.sum(jnp.mean(err * err, axis=-1, keepdims=True), axis=0, keepdims=True)
            o_ref[...], dg = _rms_bwd(err * (1.0 / D), yn, r, gain_f)
            dg_ref[...] += dg

    tile = pl.BlockSpec((TM, D), lambda i: (i, 0))
    row = pl.BlockSpec((1, D), lambda i: (0, 0))
    head_specs = [] if head is None else [row, pl.BlockSpec((1, 128), lambda i: (0, 0))]
    head_shapes = [] if head is None else [jax.ShapeDtypeStruct((1, D), F32), jax.ShapeDtypeStruct((1, 128), F32)]
    res, got = _call(
        body, name=name, grid=(t // TM,),
        in_specs=[tile, row, ANY, ANY] + ([] if head is None else [row, tile]),
        out_specs=[tile, pl.BlockSpec((TM, 2 * F), lambda i: (i, 0)), tile] + head_specs,
        out_shape=[jax.ShapeDtypeStruct((t, D), F32), jax.ShapeDtypeStruct((t, 2 * F), BF16), jax.ShapeDtypeStruct((t, D), BF16)] + head_shapes,
        scratch_shapes=[pltpu.VMEM((F, D), BF16)] * 3 + [pltpu.VMEM((TM, F), BF16), pltpu.SemaphoreType.DMA((3, N_DEV))],
        args=(x, gain, gw_gu, gw_d) + (() if head is None else tuple(head)), comms=comms)
    return (res[0], res[1], res[2], got, *res[3:])


def _ffn_bwd_hidden(ab, dout, gw_d, off_d, name, comms=()):
    t = ab.shape[0]
    n_steps = t // TM
    row_chunks = [(r, min(512, F - r)) for r in range(0, F, 512)]

    def tile(w):
        return pl.BlockSpec((TM, w), lambda i: (i, 0))

    def hidden(ab_ref, do_ref, d_ref, dab_ref, dwd_ref, wd, s_scr, acc, sems, out_sem):
        step = pl.program_id(0)

        @pl.when(step == 0)
        def _():
            cps = _load_rows(d_ref, wd, off_d, R_FF, sems)
            acc[...] = jnp.zeros_like(acc)
            for cp in cps:
                cp.wait()

        df = (0.5 * do_ref[...]).astype(BF16)
        for c in range(F // FC):
            a = ab_ref[:, c * FC:(c + 1) * FC].astype(F32)
            b = ab_ref[:, F + c * FC:F + (c + 1) * FC].astype(F32)
            sg = jax.nn.sigmoid(a)
            sl = a * sg
            ds = _nt(df, wd[pl.ds(c * FC, FC), :])
            dab_ref[:, c * FC:(c + 1) * FC] = (ds * b * (sg * (1.0 + a * (1.0 - sg)))).astype(BF16)
            dab_ref[:, F + c * FC:F + (c + 1) * FC] = (ds * sl).astype(BF16)
            s_scr[:, c * FC:(c + 1) * FC] = (sl * b).astype(BF16)
        for r, n in row_chunks:
            acc[r:r + n, :] += _tn(s_scr[:, r:r + n], df)

        @pl.when(step == n_steps - 1)
        def _():
            wd[...] = acc[...].astype(BF16)
            out = pltpu.make_async_copy(wd, dwd_ref, out_sem)
            out.start()
            out.wait()

    (dab, dwd), got = _call(
        hidden, name=name, grid=(n_steps,),
        in_specs=[tile(2 * F), tile(D), ANY], out_specs=[tile(2 * F), ANY],
        out_shape=[jax.ShapeDtypeStruct((t, 2 * F), BF16), jax.ShapeDtypeStruct((F, D), BF16)],
        scratch_shapes=[pltpu.VMEM((F, D), BF16), pltpu.VMEM((TM, F), BF16), pltpu.VMEM((F, D), F32), pltpu.SemaphoreType.DMA((N_DEV,)),
                        pltpu.SemaphoreType.DMA],
        args=(ab, dout, gw_d), comms=comms)
    return (dab, dwd.reshape(N_DEV, R_FF, D)), got


def _ffn_bwd_input(x, gain, dab, dout, gw_gu, name, comms=()):
    t = x.shape[0]

    def body(x_ref, g_ref, dab_ref, do_ref, gu_ref, dx_ref, dg_ref, wgu, sems):
        @pl.when(pl.program_id(0) == 0)
        def _():
            cps = _load_rows(gu_ref, wgu.at[0:F], 0, R_FF, sems.at[0]) + _load_rows(gu_ref, wgu.at[F:2 * F], R_FF, R_FF, sems.at[1])
            dg_ref[...] = jnp.zeros_like(dg_ref)
            for cp in cps:
                cp.wait()

        gain_v = g_ref[...]
        xn, r = _rms(x_ref[...])
        dh = _nn(dab_ref[...], wgu[...])
        dxn, dg = _rms_bwd(dh, xn, r, gain_v)
        dg_ref[...] += dg
        dx_ref[...] = do_ref[...] + dxn

    def tile(w):
        return pl.BlockSpec((TM, w), lambda i: (i, 0))

    row = pl.BlockSpec((1, D), lambda i: (0, 0))
    return _call(
        body, name=name, grid=(t // TM,),
        in_specs=[tile(D), row, tile(2 * F), tile(D), ANY], out_specs=[tile(D), row],
        out_shape=[jax.ShapeDtypeStruct((t, D), F32), jax.ShapeDtypeStruct((1, D), F32)],
        scratch_shapes=[pltpu.VMEM((2 * F, D), BF16), pltpu.SemaphoreType.DMA((2, N_DEV))],
        args=(x, gain, dab, dout, gw_gu), comms=comms)


def _weight_grad(a, b, name, col_off=0, m=None, comms=None, mats=1):
    t = a.shape[0]
    m = a.shape[1] if m is None else m
    n = b.shape[1]
    tmm = 512 if m % 512 == 0 else 256
    first = col_off // tmm

    def body(a_ref, b_ref, o_ref):
        o_ref[...] = _tn(a_ref[...], b_ref[...]).astype(BF16)

    (out,), got = _call(
        body, name=name, grid=(m // tmm,),
        in_specs=[pl.BlockSpec((t, tmm), lambda i: (0, first + i)), pl.BlockSpec((t, n), lambda i: (0, 0))],
        out_specs=[pl.BlockSpec((tmm, n), lambda i: (i, 0))],
        out_shape=[jax.ShapeDtypeStruct((m, n), BF16)], scratch_shapes=[], args=(a, b), comms=comms or ())
    out = out.reshape(mats * N_DEV, m // (mats * N_DEV), n)
    return out if comms is None else (out, got)


def _weight_grad_pieces(pieces, b, name, comms=()):
    t, n = b.shape
    tmm = 512
    tiles = [(k, j) for k, a in enumerate(pieces) for j in range(a.shape[1] // tmm)]
    m = tmm * len(tiles)
    c_in = [a for c in comms for a in c.inputs]
    c_out = [s for c in comms for s in c.out_shape]
    c_scr = [s for c in comms for s in c.scratch]

    def body(*refs):
        a_refs, (b_ref,), cin, (o_ref,), cout, (slots, b_vmem, sems, b_sem), cscr = _split_refs(
            refs, [len(pieces), 1, len(c_in), 1, len(c_out), 4, len(c_scr)])
        bound = _bind(comms, cin, cout, cscr)

        def fetch(s):
            k, j = tiles[s]
            cp = pltpu.make_async_copy(a_refs[k].at[:, j * tmm:(j + 1) * tmm], slots.at[s % 2], sems.at[s % 2])
            cp.start()
            return cp

        load_b = pltpu.make_async_copy(b_ref, b_vmem, b_sem)
        load_b.start()
        cp = fetch(0)
        for c, r in bound:
            c.begin(*r)
        load_b.wait()
        for s in range(len(tiles)):
            nxt = fetch(s + 1) if s + 1 < len(tiles) else None
            cp.wait()
            o_ref[s * tmm:(s + 1) * tmm, :] = _tn(slots[s % 2], b_vmem[...]).astype(BF16)
            cp = nxt
        for phase in ("mid", "relay", "end"):
            for c, r in bound:
                getattr(c, phase)(*r)

    res = pl.pallas_call(
        body, name=name, in_specs=[ANY] * (len(pieces) + 1 + len(c_in)),
        out_specs=[pl.BlockSpec(memory_space=pltpu.VMEM)] + [ANY] * len(c_out),
        out_shape=[jax.ShapeDtypeStruct((m, n), BF16)] + c_out,
        scratch_shapes=[pltpu.VMEM((2, t, tmm), BF16), pltpu.VMEM((t, n), BF16), pltpu.SemaphoreType.DMA((2,)),
                        pltpu.SemaphoreType.DMA(())] + c_scr,
        compiler_params=_cparams())(*pieces, b, *c_in)
    return res[0].reshape(N_DEV, m // N_DEV, n), list(res[1:])


def _mix_proj_fwd(x, gain, b_gate, gw, comms=()):
    t = x.shape[0]

    def body(x_ref, g_ref, bg_ref, gw_ref, q_ref, k_ref, v_ref, zs_ref, gt_ref, h_ref, win, sems):
        @pl.when(pl.program_id(0) == 0)
        def _():
            for cp in _load_rows(gw_ref, win, OFF_IN, R_IN, sems):
                cp.wait()

        xn, _ = _rms(x_ref[...])
        h = (xn * g_ref[...]).astype(BF16)
        h_ref[...] = h
        z = _nt(h, win[...])
        q_ref[...] = (z[:, 0:D_ATT] * 0.125).astype(BF16)
        k_ref[...] = z[:, D_ATT:2 * D_ATT].astype(BF16)
        v_ref[...] = z[:, 2 * D_ATT:3 * D_ATT].astype(BF16)
        zs_ref[...] = z[:, 3 * D_ATT:3 * D_ATT + 2 * D_SGU].astype(BF16)
        gt_ref[...] = jax.nn.sigmoid(z[:, 3 * D_ATT + 2 * D_SGU:] + bg_ref[...]).astype(BF16)

    def tile(w):
        return pl.BlockSpec((TM, w), lambda i: (i, 0))

    return _call(
        body, name="mix_proj_fwd", grid=(t // TM,),
        in_specs=[tile(D), pl.BlockSpec((1, D), lambda i: (0, 0)), pl.BlockSpec((1, 2 * D), lambda i: (0, 0)), ANY],
        out_specs=[tile(D_ATT), tile(D_ATT), tile(D_ATT), tile(2 * D_SGU), tile(2 * D), tile(D)],
        out_shape=[jax.ShapeDtypeStruct((t, D_ATT), BF16)] * 3 + [jax.ShapeDtypeStruct((t, 2 * D_SGU), BF16),
                                                                   jax.ShapeDtypeStruct((t, 2 * D), BF16),
                                                                   jax.ShapeDtypeStruct((t, D), BF16)],
        scratch_shapes=[pltpu.VMEM((D_IN, D), BF16), pltpu.SemaphoreType.DMA((N_DEV,))],
        args=(x, gain, b_gate, gw), comms=comms)


def _mix_proj_bwd(dz_pieces, x, gain, dres, gw, comms=()):
    t = x.shape[0]
    widths = [a.shape[1] for a in dz_pieces]
    offs = [sum(widths[:k]) for k in range(len(widths))]

    def body(*refs):
        dz_refs, (x_ref, g_ref, dr_ref, gw_ref, dx_ref, dg_ref, win, sems) = refs[:len(widths)], refs[len(widths):]

        @pl.when(pl.program_id(0) == 0)
        def _():
            cps = _load_rows(gw_ref, win, OFF_IN, R_IN, sems)
            dg_ref[...] = jnp.zeros_like(dg_ref)
            for cp in cps:
                cp.wait()

        dh = sum(_nn(r[...], win[o:o + w, :]) for r, o, w in zip(dz_refs, offs, widths))
        xn, r = _rms(x_ref[...])
        dxn, dg = _rms_bwd(dh, xn, r, g_ref[...])
        dg_ref[...] += dg
        dx_ref[...] = dr_ref[...] + dxn

    def tile(w):
        return pl.BlockSpec((TM, w), lambda i: (i, 0))

    row = pl.BlockSpec((1, D), lambda i: (0, 0))
    return _call(
        body, name="mix_proj_bwd", grid=(t // TM,),
        in_specs=[tile(w) for w in widths] + [tile(D), row, tile(D), ANY], out_specs=[tile(D), row],
        out_shape=[jax.ShapeDtypeStruct((t, D), F32), jax.ShapeDtypeStruct((1, D), F32)],
        scratch_shapes=[pltpu.VMEM((D_IN, D), BF16), pltpu.SemaphoreType.DMA((N_DEV,))],
        args=(*dz_pieces, x, gain, dres, gw), comms=comms)


SKEW_W = KW + QB
N_CAP = 2 * QB - REL_CLIP + 1


def _tables(rel_bias, w_s, b_s):
    cap = rel_bias[:, 2 * REL_CLIP:]
    diag = jnp.concatenate([jnp.broadcast_to(cap, (HEADS, N_CAP)), rel_bias[:, 2 * REL_CLIP - 1::-1],
                            jnp.broadcast_to(cap, (HEADS, SKEW_W - N_CAP - 2 * REL_CLIP))], axis=1)

    def fill(ins, outs):
        (d_ref, ws_ref, bs_ref), (bias_ref, wm2_ref, wmt2_ref, bsx_ref) = ins, outs
        lag = lax.broadcasted_iota(jnp.int32, (QB, KW), 1) // CHUNK - lax.broadcasted_iota(jnp.int32, (QB, KW), 0) // CHUNK
        band = (lag >= 0) & (lag <= N_LEFT)
        for h in range(HEADS):
            rows = jnp.broadcast_to(d_ref[h:h + 1, :], (QB, SKEW_W))
            bias_ref[h] = jnp.where(band, pltpu.roll(rows, 0, 1, stride=1, stride_axis=0)[:, :KW], NEG_INF)
        i = lax.broadcasted_iota(jnp.int32, (SGU_BLOCK, SGU_BLOCK), 0)
        j = lax.broadcasted_iota(jnp.int32, (SGU_BLOCK, SGU_BLOCK), 1)
        causal = (i // CHUNK) >= (j // CHUNK)
        for pair in range(4):
            w0, w1 = jnp.where(causal, ws_ref[2 * pair], 0.0), jnp.where(causal, ws_ref[2 * pair + 1], 0.0)
            wm2_ref[pair] = jnp.concatenate([w0, w1], axis=1).astype(BF16)
            wmt2_ref[pair] = jnp.concatenate([w0.T, w1.T], axis=1).astype(BF16)
        by_row = jnp.concatenate([bs_ref[...], jnp.zeros((SGU_BLOCK - 8, SGU_BLOCK), F32)], axis=0).T
        group = lax.broadcasted_iota(jnp.int32, (SGU_BLOCK, D_SGU), 1) // 64
        spread = jnp.zeros((SGU_BLOCK, D_SGU), F32)
        for g in range(8):
            spread = jnp.where(group == g, by_row[:, g:g + 1], spread)
        bsx_ref[...] = spread

    shapes = [jax.ShapeDtypeStruct((HEADS, QB, KW), F32), jax.ShapeDtypeStruct((4, SGU_BLOCK, 2 * SGU_BLOCK), BF16),
              jax.ShapeDtypeStruct((4, SGU_BLOCK, 2 * SGU_BLOCK), BF16), jax.ShapeDtypeStruct((SGU_BLOCK, D_SGU), F32)]
    return [diag, w_s, b_s], shapes, fill


def _att_specs():
    qspec = pl.BlockSpec((QB, D_ATT), lambda g: (g, 0))
    kspecs = [pl.BlockSpec((QB, D_ATT), lambda g: (jnp.maximum(g - 2, 0), 0)),
              pl.BlockSpec((QB, D_ATT), lambda g: (jnp.maximum(g - 1, 0), 0)), qspec]
    bspec = pl.BlockSpec((HEADS, QB, KW), lambda g: (0, 0, 0))
    return qspec, kspecs, bspec


def _att_probs(qm, kp, bias, valid):
    s = jnp.where(valid, _nt(qm, kp) + bias, NEG_INF)
    e = jnp.exp(s - jnp.max(s, axis=-1, keepdims=True))
    return e / jnp.sum(e, axis=-1, keepdims=True)


def _att_valid():
    g = pl.program_id(0)
    blk = lax.broadcasted_iota(jnp.int32, (QB, KW), 1) // QB
    return (blk + g) >= 2


def _att_fwd(q, k, v, bias, comms=()):
    t = q.shape[0]

    def body(q_ref, k0, k1, k2, v0, v1, v2, b_ref, y_ref):
        valid = _att_valid()
        first = lax.broadcasted_iota(jnp.int32, (1, 128), 1) < 64
        for p in range(HEADS // 2):
            lanes = slice(p * 128, (p + 1) * 128)
            qp = q_ref[:, lanes]
            kp = jnp.concatenate([k0[:, lanes], k1[:, lanes], k2[:, lanes]], axis=0)
            vp = jnp.concatenate([v0[:, lanes], v1[:, lanes], v2[:, lanes]], axis=0)
            out = jnp.zeros((QB, 128), F32)
            for hh in range(2):
                mask = first if hh == 0 else jnp.logical_not(first)
                pr = _att_probs(jnp.where(mask, qp, 0), kp, b_ref[2 * p + hh], valid)
                out = out + _nn(pr.astype(BF16), jnp.where(mask, vp, 0))
            y_ref[:, lanes] = out.astype(BF16)

    qspec, kspecs, bspec = _att_specs()
    (out,), got = _call(
        body, name="att_fwd", grid=(t // QB,),
        in_specs=[qspec] + kspecs + kspecs + [bspec], out_specs=[qspec],
        out_shape=[jax.ShapeDtypeStruct((t, D_ATT), BF16)], scratch_shapes=[],
        args=(q, k, k, k, v, v, v, bias), comms=comms)
    return out, got


def _att_bwd(q, k, v, bias, dy, comms=()):
    t = q.shape[0]
    n_blocks = t // QB

    def body(q_ref, k0, k1, k2, v0, v1, v2, b_ref, dy_ref, dq_ref, dk_ref, dv_ref, db_ref, dk_acc, dv_acc):
        g = pl.program_id(0)

        @pl.when(g == 0)
        def _():
            db_ref[...] = jnp.zeros_like(db_ref)
            dk_acc[...] = jnp.zeros_like(dk_acc)
            dv_acc[...] = jnp.zeros_like(dv_acc)

        valid = _att_valid()
        first = lax.broadcasted_iota(jnp.int32, (1, 128), 1) < 64
        for p in range(HEADS // 2):
            lanes = slice(p * 128, (p + 1) * 128)
            qp = q_ref[:, lanes]
            dyp = dy_ref[:, lanes]
            kp = jnp.concatenate([k0[:, lanes], k1[:, lanes], k2[:, lanes]], axis=0)
            vp = jnp.concatenate([v0[:, lanes], v1[:, lanes], v2[:, lanes]], axis=0)
            dq = jnp.zeros((QB, 128), F32)
            dk = jnp.zeros((KW, 128), F32)
            dv = jnp.zeros((KW, 128), F32)
            for hh in range(2):
                mask = first if hh == 0 else jnp.logical_not(first)
                qm = jnp.where(mask, qp, 0)
                dym = jnp.where(mask, dyp, 0)
                pr = _att_probs(qm, kp, b_ref[2 * p + hh], valid)
                dp = _nt(dym, vp)
                ds = pr * (dp - jnp.sum(dp * pr, axis=-1, keepdims=True))
                db_ref[2 * p + hh] += ds
                dsb = ds.astype(BF16)
                dq = dq + _nn(dsb, jnp.where(mask, kp, 0))
                dk = dk + _tn(dsb, qm)
                dv = dv + _tn(pr.astype(BF16), dym)
            dq_ref[:, lanes] = (dq * 0.125).astype(BF16)
            for j in range(3):
                rows = pl.ds(pl.multiple_of(jnp.maximum(g - 2 + j, 0) * QB, QB), QB)
                dk_acc[rows, lanes] += dk[j * QB:(j + 1) * QB]
                dv_acc[rows, lanes] += dv[j * QB:(j + 1) * QB]

        @pl.when(g == n_blocks - 1)
        def _():
            dk_ref[...] = dk_acc[...].astype(BF16)
            dv_ref[...] = dv_acc[...].astype(BF16)

    qspec, kspecs, bspec = _att_specs()
    full = pl.BlockSpec((t, D_ATT), lambda g: (0, 0))
    return _call(
        body, name="att_bwd", grid=(n_blocks,),
        in_specs=[qspec] + kspecs + kspecs + [bspec, qspec], out_specs=[qspec, full, full, bspec],
        out_shape=[jax.ShapeDtypeStruct((t, D_ATT), BF16)] * 3 + [jax.ShapeDtypeStruct((HEADS, QB, KW), F32)],
        scratch_shapes=[pltpu.VMEM((t, D_ATT), F32)] * 2,
        args=(q, k, k, k, v, v, v, bias, dy), comms=comms)


def _rel_bias_grad(dbias, comms=()):
    c_in = [a for c in comms for a in c.inputs]
    c_out = [s for c in comms for s in c.out_shape]
    c_scr = [s for c in comms for s in c.scratch]

    def body(db_ref, *refs):
        cin, (cs_ref, tot_ref), cout, cscr = _split_refs(refs, [len(c_in), 2, len(c_out), len(c_scr)])
        bound = _bind(comms, cin, cout, cscr)
        for c, r in bound:
            c.begin(*r)
        lane = lax.broadcasted_iota(jnp.int32, (1, SKEW_W), 1)
        capped = (lane < N_CAP) | (lane > KW)
        pad = jnp.zeros((8, QB), F32)
        for h in range(HEADS):
            z = jnp.concatenate([db_ref[h, 0:8, :], pad], axis=1)
            for a in range(1, QB // 8):
                z = z + pltpu.roll(jnp.concatenate([db_ref[h, 8 * a:8 * a + 8, :], pad], axis=1), SKEW_W - 8 * a, 1)
            cs = z[0:1, :]
            for b in range(1, 8):
                cs = cs + pltpu.roll(z[b:b + 1, :], SKEW_W - b, 1)
            cs_ref[h:h + 1, :] = cs
            tot_ref[h:h + 1, :] = jnp.broadcast_to(jnp.sum(jnp.where(capped, cs, 0.0), axis=1, keepdims=True), (1, 128))
        for phase in ("mid", "relay", "end"):
            for c, r in bound:
                getattr(c, phase)(*r)

    whole = pl.BlockSpec(memory_space=pltpu.VMEM)
    cs, tot, *got = pl.pallas_call(
        body, name="rel_bias_grad", in_specs=[whole] + [ANY] * len(c_in), out_specs=[whole, whole] + [ANY] * len(c_out),
        out_shape=[jax.ShapeDtypeStruct((HEADS, SKEW_W), F32), jax.ShapeDtypeStruct((HEADS, 128), F32)] + c_out, scratch_shapes=c_scr,
    )(dbias, *c_in)
    return jnp.concatenate([cs[:, KW:N_CAP - 1:-1], tot[:, :1]], axis=1), got


def _group_stack(blk, first):
    return jnp.concatenate([jnp.where(first, blk, 0), jnp.where(first, 0, blk)], axis=0)


def _sgu_norm(zs_ref, lng, lnb):
    zs = zs_ref[...].astype(F32)
    ga, th = _gelu(zs)
    u, vs = ga[:, :D_SGU], ga[:, D_SGU:]
    mu = jnp.mean(vs, axis=-1, keepdims=True)
    cen = vs - mu
    rstd = lax.rsqrt(jnp.mean(cen * cen, axis=-1, keepdims=True) + EPS)
    xhat = cen * rstd
    return zs, th, u, xhat, rstd, xhat * lng + lnb


def _sgu_mix(vb, wm2_ref, bsx, s_ref):
    first = lax.broadcasted_iota(jnp.int32, (1, 128), 1) < 64
    for n in range(TM // SGU_BLOCK):
        for p in range(4):
            blk = vb[n * 128:(n + 1) * 128, p * 128:(p + 1) * 128]
            s_ref[n * 128:(n + 1) * 128, p * 128:(p + 1) * 128] = _nn(wm2_ref[p], _group_stack(blk, first)) + bsx[:, p * 128:(p + 1) * 128]


def _merge_fwd(x, zs, gt, y_att, lng, lnb, wm2, bsx, gw):
    t = x.shape[0]

    def body(x_ref, zs_ref, gt_ref, ya_ref, lng_ref, lnb_ref, wm2_ref, bsx_ref, gw_ref, xo_ref, ys_ref, mg_ref,
             wbr, wo, s_scr, sems):
        @pl.when(pl.program_id(0) == 0)
        def _():
            for cp in _load_rows(gw_ref, wbr, OFF_BR, R_BR, sems.at[0]) + _load_rows(gw_ref, wo, OFF_WO, R_WO, sems.at[1]):
                cp.wait()

        _, _, u, _, _, vsn = _sgu_norm(zs_ref, lng_ref[...], lnb_ref[...])
        _sgu_mix(vsn.astype(BF16), wm2_ref, bsx_ref[...], s_scr)
        ys = (u * s_scr[...]).astype(BF16)
        ys_ref[...] = ys
        pa = _nt(ya_ref[...], wbr[:, :D_ATT])
        ps = _nt(ys, wbr[:, D_ATT:])
        mg = (gt_ref[:, :D].astype(F32) * pa + gt_ref[:, D:].astype(F32) * ps).astype(BF16)
        mg_ref[...] = mg
        xo_ref[...] = x_ref[...] + _nn(mg, wo[...])

    def tile(w):
        return pl.BlockSpec((TM, w), lambda i: (i, 0))

    def const(shape):
        return pl.BlockSpec(shape, lambda i: (0,) * len(shape))

    return pl.pallas_call(
        body, name="merge_fwd", grid=(t // TM,),
        in_specs=[tile(D), tile(2 * D_SGU), tile(2 * D), tile(D_ATT), const((1, D_SGU)), const((1, D_SGU)),
                  const((4, 128, 256)), const((128, D_SGU)), ANY],
        out_specs=[tile(D), tile(D_SGU), tile(D)],
        out_shape=[jax.ShapeDtypeStruct((t, D), F32), jax.ShapeDtypeStruct((t, D_SGU), BF16), jax.ShapeDtypeStruct((t, D), BF16)],
        scratch_shapes=[pltpu.VMEM((D, D), BF16), pltpu.VMEM((D, D), BF16), pltpu.VMEM((TM, D_SGU), F32),
                        pltpu.SemaphoreType.DMA((2, N_DEV))],
        compiler_params=_cparams(("arbitrary",)),
    )(x, zs, gt, y_att, lng, lnb, wm2, bsx, gw)


def _merge_bwd(dx, gt, y_att, y_sgu, merged, gw):
    t = dx.shape[0]
    n_steps = t // TM

    def body(dx_ref, gt_ref, ya_ref, ys_ref, mg_ref, gw_ref, dzg_ref, dya_ref, dys_ref, dbg_ref, ga_ref, gs_ref, go_ref,
             wbr, wo, acc_a, acc_s, acc_o, sems):
        step = pl.program_id(0)

        @pl.when(step == 0)
        def _():
            cps = _load_rows(gw_ref, wbr, OFF_BR, R_BR, sems.at[0]) + _load_rows(gw_ref, wo, OFF_WO, R_WO, sems.at[1])
            dbg_ref[...] = jnp.zeros_like(dbg_ref)
            for acc in (acc_a, acc_s, acc_o):
                acc[...] = jnp.zeros_like(acc)
            for cp in cps:
                cp.wait()

        dxb = dx_ref[...].astype(BF16)
        acc_o[...] += _tn(mg_ref[...], dxb)
        dm = _nt(dxb, wo[...])
        for half, y_ref, w, acc in ((0, ya_ref, wbr.at[:, :D_ATT], acc_a), (1, ys_ref, wbr.at[:, D_ATT:], acc_s)):
            cols = slice(half * D, (half + 1) * D)
            gate = gt_ref[:, cols].astype(F32)
            branch = _nt(y_ref[...], w[...])
            dzg = dm * branch * gate * (1.0 - gate)
            dbg_ref[:, cols] += jnp.sum(dzg, axis=0, keepdims=True)
            dzg_ref[:, cols] = dzg.astype(BF16)
            dbr = (dm * gate).astype(BF16)
            acc[...] += _tn(dbr, y_ref[...])
            dy = _nn(dbr, w[...])
            if half == 0:
                dya_ref[...] = dy.astype(BF16)
            else:
                dys_ref[...] = dy

        @pl.when(step == n_steps - 1)
        def _():
            ga_ref[...] = acc_a[...].astype(BF16)
            gs_ref[...] = acc_s[...].astype(BF16)
            go_ref[...] = acc_o[...].astype(BF16)

    def tile(w):
        return pl.BlockSpec((TM, w), lambda i: (i, 0))

    def whole(r, c):
        return pl.BlockSpec((r, c), lambda i: (0, 0))

    dzg, dya, dys, dbg, g_ba, g_bs, g_wo = pl.pallas_call(
        body, name="merge_bwd", grid=(n_steps,),
        in_specs=[tile(D), tile(2 * D), tile(D_ATT), tile(D_SGU), tile(D), ANY],
        out_specs=[tile(2 * D), tile(D_ATT), tile(D_SGU), whole(1, 2 * D), whole(D, D_ATT), whole(D, D_SGU), whole(D, D)],
        out_shape=[jax.ShapeDtypeStruct((t, 2 * D), BF16), jax.ShapeDtypeStruct((t, D_ATT), BF16), jax.ShapeDtypeStruct((t, D_SGU), F32),
                   jax.ShapeDtypeStruct((1, 2 * D), F32), jax.ShapeDtypeStruct((D, D_ATT), BF16), jax.ShapeDtypeStruct((D, D_SGU), BF16),
                   jax.ShapeDtypeStruct((D, D), BF16)],
        scratch_shapes=[pltpu.VMEM((D, D), BF16), pltpu.VMEM((D, D), BF16), pltpu.VMEM((D, D_ATT), F32), pltpu.VMEM((D, D_SGU), F32),
                        pltpu.VMEM((D, D), F32), pltpu.SemaphoreType.DMA((2, N_DEV))],
        compiler_params=_cparams(("arbitrary",)),
    )(dx, gt, y_att, y_sgu, merged, gw)
    return dzg, dya, dys, dbg, g_ba.reshape(N_DEV, R_BR, D_ATT), g_bs.reshape(N_DEV, R_BR, D_SGU), g_wo.reshape(N_DEV, R_WO, D)


def _sgu_bwd(zs, dys, lng, lnb, wm2, wmt2, bsx, comms=()):
    t = zs.shape[0]
    n_steps = t // TM

    def body(zs_ref, dys_ref, lng_ref, lnb_ref, wm2_ref, wmt2_ref, bsx_ref, dzs_ref, dw_ref, dbs_ref, dlg_ref, dlb_ref,
             s_scr, dv_scr, ds_acc):
        i = pl.program_id(0)

        @pl.when(i == 0)
        def _():
            dw_ref[...] = jnp.zeros_like(dw_ref)
            dlg_ref[...] = jnp.zeros_like(dlg_ref)
            dlb_ref[...] = jnp.zeros_like(dlb_ref)
            ds_acc[...] = jnp.zeros_like(ds_acc)

        lng = lng_ref[...]
        zs, th, u, xhat, rstd, vsn = _sgu_norm(zs_ref, lng, lnb_ref[...])
        vb = vsn.astype(BF16)
        _sgu_mix(vb, wm2_ref, bsx_ref[...], s_scr)
        dys = dys_ref[...]
        du = dys * s_scr[...]
        ds = dys * u
        dsb = ds.astype(BF16)
        first = lax.broadcasted_iota(jnp.int32, (1, 128), 1) < 64
        acc = jnp.zeros((SGU_BLOCK, D_SGU), F32)
        for n in range(TM // SGU_BLOCK):
            rows = slice(n * 128, (n + 1) * 128)
            acc = acc + ds[rows]
            for p in range(4):
                lanes = slice(p * 128, (p + 1) * 128)
                stack = _group_stack(dsb[rows, lanes], first)
                dv_scr[rows, lanes] = _nn(wmt2_ref[p], stack)
                dw_ref[p] += _nt(stack, vb[rows, lanes])
        ds_acc[...] += acc
        dvsn = dv_scr[...]
        dlg_ref[...] += jnp.sum(dvsn * xhat, axis=0, keepdims=True)
        dlb_ref[...] += jnp.sum(dvsn, axis=0, keepdims=True)
        dxh = dvsn * lng
        dvs = rstd * (dxh - jnp.mean(dxh, axis=-1, keepdims=True) - xhat * jnp.mean(dxh * xhat, axis=-1, keepdims=True))
        dga = jnp.concatenate([du, dvs], axis=1)
        dzs_ref[...] = (dga * _gelu_grad(zs, th)).astype(BF16)

        @pl.when(i == n_steps - 1)
        def _():
            r = lax.broadcasted_iota(jnp.int32, (256, 128), 0) % SGU_BLOCK
            c = lax.broadcasted_iota(jnp.int32, (256, 128), 1)
            keep = (r // CHUNK) >= (c // CHUNK)
            for p in range(4):
                dw_ref[p] = jnp.where(keep, dw_ref[p], 0.0)
            total = ds_acc[...]
            grp = lax.broadcasted_iota(jnp.int32, (SGU_BLOCK, D_SGU), 1) // 64
            lane = lax.broadcasted_iota(jnp.int32, (SGU_BLOCK, 128), 1)
            out = jnp.zeros((SGU_BLOCK, 128), F32)
            for gi in range(8):
                out = jnp.where(lane == gi, jnp.sum(jnp.where(grp == gi, total, 0.0), axis=1, keepdims=True), out)
            dbs_ref[...] = out

    def tile(w):
        return pl.BlockSpec((TM, w), lambda i: (i, 0))

    def const(shape):
        return pl.BlockSpec(shape, lambda i: (0,) * len(shape))

    return _call(
        body, name="sgu_bwd", grid=(n_steps,),
        in_specs=[tile(2 * D_SGU), tile(D_SGU), const((1, D_SGU)), const((1, D_SGU)), const((4, 128, 256)), const((4, 128, 256)),
                  const((128, D_SGU))],
        out_specs=[tile(2 * D_SGU), const((4, 256, 128)), const((128, 128)), const((1, D_SGU)), const((1, D_SGU))],
        out_shape=[jax.ShapeDtypeStruct((t, 2 * D_SGU), BF16), jax.ShapeDtypeStruct((4, 256, 128), F32),
                   jax.ShapeDtypeStruct((128, 128), F32), jax.ShapeDtypeStruct((1, D_SGU), F32), jax.ShapeDtypeStruct((1, D_SGU), F32)],
        scratch_shapes=[pltpu.VMEM((TM, D_SGU), F32), pltpu.VMEM((TM, D_SGU), F32), pltpu.VMEM((SGU_BLOCK, D_SGU), F32)],
        args=(zs, dys, lng, lnb, wm2, wmt2, bsx), comms=comms)


def _adamw(g, w, m, v):
    m = ADAM_B1 * m + (1.0 - ADAM_B1) * g
    v = ADAM_B2 * v + (1.0 - ADAM_B2) * (g * g)
    m_hat = m / (1.0 - ADAM_B1 ** ADAM_STEP)
    v_hat = v / (1.0 - ADAM_B2 ** ADAM_STEP)
    return -ADAM_LR * (m_hat / (jnp.sqrt(v_hat) + ADAM_EPS) + ADAM_WD * w), m, v


def _adamw_matrices(items, name):
    n = len(items)
    c = items[0][0].shape[2]
    tc = 256

    def body(own_ref, *refs):
        for i, (_, _, _, _, _, transposed) in enumerate(items):
            p_ref, s_ref, w_ref, m_ref, v_ref = refs[5 * i:5 * i + 5]
            g = p_ref[0].astype(F32) + p_ref[1].astype(F32) + p_ref[2].astype(F32) + s_ref[...].astype(F32)
            g = g.T if transposed else g
            res = (g,) + _adamw(g, w_ref[...], m_ref[...], v_ref[...])
            for o_ref, val in zip(refs[5 * n + 4 * i:5 * n + 4 * i + 4], res):
                o_ref[...] = val

    in_specs, out_specs, out_shape, args = [], [], [], []
    for parts, sums, w, m, v, transposed in items:
        r = parts.shape[1]
        own = pl.BlockSpec((None, tc, r), lambda i, o: (0, i, 0)) if transposed else pl.BlockSpec((None, r, tc), lambda i, o: (0, 0, i))
        in_specs += [pl.BlockSpec((3, r, tc), lambda i, o: (0, 0, i)), pl.BlockSpec((None, r, tc), lambda i, o: (o[0], 0, i)), own, own, own]
        out_specs += [own] * 4
        out_shape += [jax.ShapeDtypeStruct(w.shape, F32)] * 4
        args += [parts, sums, w, m, v]
    res = pl.pallas_call(
        body, name=name,
        grid_spec=pltpu.PrefetchScalarGridSpec(num_scalar_prefetch=1, grid=(c // tc,), in_specs=in_specs, out_specs=out_specs),
        out_shape=out_shape, compiler_params=_cparams(("arbitrary",)),
    )(_my_index(("x", 2), ("y", 1)), *args)
    return [list(res[4 * i:4 * i + 4]) for i in range(n)]


_SMALL_2D = {"norm_ffn1": (1, D), "norm_mix": (1, D), "norm_ffn2": (1, D), "norm_final": (1, D), "b_gate": (1, 2 * D),
             "sgu_ln_g": (1, D_SGU), "sgu_ln_b": (1, D_SGU), "sgu_b_s": (8, SGU_BLOCK), "rel_bias": (HEADS, N_REL),
             "sgu_w_s": (8 * SGU_BLOCK, SGU_BLOCK)}


def _adamw_small(parts, loss_parts, p):
    names = list(parts)
    n = len(names)

    def body(*refs):
        got, loss_got, wmv, outs, loss_out = refs[:n], refs[n], refs[n + 1:4 * n + 1], refs[4 * n + 1:8 * n + 1], refs[8 * n + 1]
        for i, name in enumerate(names):
            g = got[i][0]
            for k in range(1, N_DEV):
                g = g + got[i][k]
            if name == "sgu_b_s":
                g = g.T[0:8, :]
            res = (g,) + _adamw(g, wmv[3 * i][...], wmv[3 * i + 1][...], wmv[3 * i + 2][...])
            for o_ref, val in zip(outs[4 * i:4 * i + 4], res):
                o_ref[...] = val
        total = loss_got[0]
        for k in range(1, N_DEV):
            total = total + loss_got[k]
        loss_out[...] = total

    wmv = [p[pre + name].reshape(_SMALL_2D[name]) for name in names for pre in ("", "m_", "v_")]
    res = pl.pallas_call(
        body, name="adamw_small",
        out_shape=[jax.ShapeDtypeStruct(_SMALL_2D[name], F32) for name in names for _ in range(4)] + [jax.ShapeDtypeStruct((1, 128), F32)],
        compiler_params=_cparams())(*[parts[name] for name in names], loss_parts, *wmv)
    return [{name: res[4 * i + j].reshape(p[name].shape) for i, name in enumerate(names)} for j in range(4)], res[-1]


def _pack_rows(slabs, name, comms=(), extra=None):
    flat = [a for groups in slabs for grp in groups for a, _ in grp]
    rows = [[grp[0][0].shape[2] if grp[0][1] else grp[0][0].shape[1] for grp in groups] for groups in slabs]
    e_in, e_out, fill = extra if extra is not None else ([], [], None)
    c_in = [a for c in comms for a in c.inputs]
    c_out = [s for c in comms for s in c.out_shape]
    c_scr = [s for c in comms for s in c.scratch]

    def body(*refs):
        ins, ein, cin, outs, eout, cout, cscr = _split_refs(
            refs, [len(flat), len(e_in), len(c_in), len(slabs), len(e_out), len(c_out), len(c_scr)])
        bound = _bind(comms, cin, cout, cscr)
        for c, r in bound:
            c.begin(*r)
        pos = 0
        for groups, slab_rows, o_ref in zip(slabs, rows, outs):
            off = 0
            for grp, r in zip(groups, slab_rows):
                vals = []
                for _, transposed in grp:
                    val = ins[pos][0]
                    vals.append(val.T if transposed else val)
                    pos += 1
                o_ref[off:off + r, :] = (vals[0] if len(vals) == 1 else jnp.concatenate(vals, axis=1)).astype(BF16)
                off += r
        if fill is not None:
            fill(ein, eout)
        for phase in ("mid", "relay", "end"):
            for c, r in bound:
                getattr(c, phase)(*r)

    whole = pl.BlockSpec(memory_space=pltpu.VMEM)
    n_plain = len(slabs) + len(e_out)
    res = pl.pallas_call(
        body, name=name, in_specs=[whole] * (len(flat) + len(e_in)) + [ANY] * len(c_in), out_specs=[whole] * n_plain + [ANY] * len(c_out),
        out_shape=[jax.ShapeDtypeStruct((sum(r), D), BF16) for r in rows] + list(e_out) + c_out, scratch_shapes=c_scr,
        compiler_params=_cparams())(*flat, *e_in, *c_in)
    return list(res[:len(slabs)]), list(res[len(slabs):n_plain]), list(res[n_plain:])


def _step(x, target, p):
    n1, nm, n2 = p["norm_ffn1"], p["norm_mix"], p["norm_ffn2"]
    nf = p["norm_final"].reshape(1, D)
    lng, lnb = p["sgu_ln_g"], p["sgu_ln_b"]

    def chip_sums(grads, name):
        gots = _comm_only([_SiblingSwap(grads)], "swap_" + name)
        return _pair_sums(grads, gots, "pair_sums_" + name)

    def as_rows(a):
        return jnp.swapaxes(a, 1, 2)

    def updates(parts, sums, names):
        res = {}
        for cols in sorted({pt.shape[2] for pt in parts}):
            items, group = [], [(pt, sm, n) for pt, sm, n in zip(parts, sums, names) if pt.shape[2] == cols]
            for pt, sm, n in group:
                view = as_rows if p[n].shape[1:] != pt.shape[1:] and p[n].shape[2] > 128 else (lambda a: a)
                transposed = p[n].shape[1:] != pt.shape[1:] and p[n].shape[2] <= 128
                items.append((pt, sm, view(p[n]), view(p["m_" + n]), view(p["v_" + n]), transposed))
            for (pt, sm, n), four in zip(group, _adamw_matrices(items, "adamw_" + group[0][2])):
                res[n] = [as_rows(o) if o.shape != p[n].shape else o for o in four]
        return res

    (rows1,), _, _ = _pack_rows([[[(as_rows(p["ffn1_w_gate"]), False)], [(as_rows(p["ffn1_w_up"]), False)], [(p["ffn1_w_down"], False)]]],
                                "pack_ffn1")
    (rows_m, rows2d, rows2gu), (bias, wm2, wmt2, bsx), (gw1,) = _pack_rows(
        [[[(as_rows(p["w_in"]), False)], [(p["w_branch_att"], True), (p["w_branch_sgu"], True)], [(p["w_out"], False)]],
         [[(p["ffn2_w_down"], False)]],
         [[(as_rows(p["ffn2_w_gate"]), False)], [(as_rows(p["ffn2_w_up"]), False)]]],
        "gather_ffn1_pack_rest", [_Gather(rows1)], _tables(p["rel_bias"][0], p["sgu_w_s"][0], p["sgu_b_s"][0]))
    x1, ab1, h1, (gwm,) = _ffn_fwd(x, n1, gw1, gw1, 2 * R_FF, "ffn1_fwd", [_Gather(rows_m)])
    (q, k, v, zs, gt, h2), (gw2d,) = _mix_proj_fwd(x1, nm, p["b_gate"], gwm, [_Gather(rows2d)])
    y_att, (gw2gu,) = _att_fwd(q, k, v, bias, [_Gather(rows2gu)])
    x2, y_sgu, merged = _merge_fwd(x1, zs, gt, y_att, lng, lnb, wm2, bsx, gwm)
    dx3, ab2, hb, _, d_nf, loss = _ffn_fwd(x2, n2, gw2gu, gw2d, 0, "ffn2_fwd", head=(nf, target))

    (dab, g_down), _ = _ffn_bwd_hidden(ab2, dx3, gw2d, 0, "ffn2_bwd_hidden")
    (dx2, d_n2), _ = _ffn_bwd_input(x2, n2, dab, dx3, gw2gu, "ffn2_bwd")
    g_gu = _weight_grad(dab, hb, "ffn2_dw_gate_up", mats=2)
    dzg, dya, dys, d_bg, g_ba, g_bs, g_wo = _merge_bwd(dx2, gt, y_att, y_sgu, merged, gwm)
    g_late = [(g_gu, 0), (g_gu, 1), (g_down, 0), (g_ba, 0), (g_bs, 0), (g_wo, 0)]
    late = ("ffn2_w_gate", "ffn2_w_up", "ffn2_w_down", "w_branch_att", "w_branch_sgu", "w_out")
    (dzs, d_wm, d_bs, d_lng, d_lnb), gots_late = _sgu_bwd(zs, dys, lng, lnb, wm2, wmt2, bsx, [_SiblingSwap(g_late)])
    sums_late = _pair_sums(g_late, gots_late, "pair_sums_late")
    (dq, dk, dv, d_bias), parts_late = _att_bwd(q, k, v, bias, dya, [_ChipScatter(sums_late)])
    big = updates(parts_late, sums_late, late)
    small = {"norm_ffn2": d_n2, "norm_final": d_nf, "b_gate": d_bg, "sgu_ln_g": d_lng, "sgu_ln_b": d_lnb, "sgu_b_s": d_bs,
             "sgu_w_s": d_wm.reshape(_SMALL_2D["sgu_w_s"])}
    dz = [dq, dk, dv, dzs, dzg]
    g_w_in, (*small_parts, loss_parts) = _weight_grad_pieces(dz, h2, "dw_in", [_AllToAll(list(small.values()) + [loss])])
    g_in = [(g_w_in, 0)]
    d_rel, got_in = _rel_bias_grad(d_bias, [_SiblingSwap(g_in)])
    sums_in = _pair_sums(g_in, got_in, "pair_sums_w_in")
    (dx1, d_nm), parts_in = _mix_proj_bwd(dz, x1, nm, dx2, gwm, [_ChipScatter(sums_in)])
    big.update(updates(parts_in, sums_in, ("w_in",)))

    (dab, g_down), _ = _ffn_bwd_hidden(ab1, dx1, gw1, 2 * R_FF, "ffn1_bwd_hidden")
    sums_d = chip_sums([(g_down, 0)], "ffn1_down")
    g_gu, (parts_d,) = _weight_grad(dab, h1, "ffn1_dw_gate_up", comms=[_ChipScatter(sums_d)], mats=2)
    sums_gu = chip_sums([(g_gu, 0), (g_gu, 1)], "ffn1_gate_up")
    (dx0, d_n1), parts_gu = _ffn_bwd_input(x, n1, dab, dx1, gw1, "ffn1_bwd", [_ChipScatter(sums_gu)])
    last = {"norm_ffn1": d_n1, "norm_mix": d_nm, "rel_bias": d_rel}
    last_parts = _comm_only([_AllToAll(list(last.values()))], "gather_norm_ffn1")
    big.update(updates([parts_d] + parts_gu, sums_d + sums_gu, ("ffn1_w_down", "ffn1_w_gate", "ffn1_w_up")))
    out_s, loss_sum = _adamw_small({**dict(zip(small, small_parts)), **dict(zip(last, last_parts))}, loss_parts, p)
    return dx0, loss_sum[0, 0], [{**{n: four[i] for n, four in big.items()}, **s} for i, s in enumerate(out_s)]


_OUT_ORDER = ("norm_ffn1", "ffn1_w_gate", "ffn1_w_up", "ffn1_w_down", "norm_mix", "w_in", "b_gate", "rel_bias", "sgu_ln_g", "sgu_ln_b",
              "sgu_w_s", "sgu_b_s", "w_branch_att", "w_branch_sgu", "w_out", "norm_ffn2", "ffn2_w_gate", "ffn2_w_up", "ffn2_w_down",
              "norm_final")


def kernel(x, norm_ffn1, ffn1_w_gate, ffn1_w_up, ffn1_w_down, norm_mix, w_in, b_gate, rel_bias, sgu_ln_g, sgu_ln_b, sgu_w_s, sgu_b_s, w_branch_att, w_branch_sgu, w_out, norm_ffn2, ffn2_w_gate, ffn2_w_up, ffn2_w_down, norm_final, loss_target, m_norm_ffn1, m_ffn1_w_gate, m_ffn1_w_up, m_ffn1_w_down, m_norm_mix, m_w_in, m_b_gate, m_rel_bias, m_sgu_ln_g, m_sgu_ln_b, m_sgu_w_s, m_sgu_b_s, m_w_branch_att, m_w_branch_sgu, m_w_out, m_norm_ffn2, m_ffn2_w_gate, m_ffn2_w_up, m_ffn2_w_down, m_norm_final, v_norm_ffn1, v_ffn1_w_gate, v_ffn1_w_up, v_ffn1_w_down, v_norm_mix, v_w_in, v_b_gate, v_rel_bias, v_sgu_ln_g, v_sgu_ln_b, v_sgu_w_s, v_sgu_b_s, v_w_branch_att, v_w_branch_sgu, v_w_out, v_norm_ffn2, v_ffn2_w_gate, v_ffn2_w_up, v_ffn2_w_down, v_norm_final):
    args = dict(locals())
    dx, loss, outs = _step(x[0], loss_target[0], {pre + n: args[pre + n] for pre in ("", "m_", "v_") for n in _OUT_ORDER})
    return (loss, dx[None], *[o[n] for o in outs for n in _OUT_ORDER])
```

```python
import functools

import jax
import jax.numpy as jnp
from jax import lax
from jax.experimental import pallas as pl
from jax.experimental.pallas import tpu as pltpu

F32 = jnp.float32
BF16 = jnp.bfloat16

N_DEV = 8
D = 1024
F = 2816
D_ATT = 512
D_SGU = 512
D_IN = 4608
HEADS = 8
CHUNK = 64
N_LEFT = 8
REL_CLIP = 256
N_REL = 2 * REL_CLIP + 1
SGU_BLOCK = 128
EPS = 1e-6
NEG_INF = -1e30
QB = 256
KW = 3 * QB

R_FF, R_IN, R_BR, R_WO = F // N_DEV, D_IN // N_DEV, D // N_DEV, D // N_DEV
OFF_IN, OFF_BR, OFF_WO = 0, R_IN, R_IN + R_BR

FC = 256
TM = 512
VMEM_LIMIT = 56 * 1024 * 1024

ADAM_LR, ADAM_B1, ADAM_B2, ADAM_EPS, ADAM_WD, ADAM_STEP = 0.001, 0.9, 0.999, 1e-08, 0.01, 10

MESH = pl.DeviceIdType.MESH
ANY = pl.BlockSpec(memory_space=pl.ANY)


def _nt(a, b):
    return lax.dot_general(a, b, (((1,), (1,)), ((), ())), preferred_element_type=F32)


def _tn(a, b):
    return lax.dot_general(a, b, (((0,), (0,)), ((), ())), preferred_element_type=F32)


def _nn(a, b):
    return jnp.dot(a, b, preferred_element_type=F32)


def _cparams(sem=None):
    return pltpu.CompilerParams(dimension_semantics=sem, vmem_limit_bytes=VMEM_LIMIT)


def _load_rows(gw_ref, dst, off, rows, sems):
    copies = [pltpu.make_async_copy(gw_ref.at[k, pl.ds(off, rows), :], dst.at[pl.ds(k * rows, rows), :], sems.at[k])
              for k in range(N_DEV)]
    for cp in copies:
        cp.start()
    return copies


def _rms(xv):
    r = lax.rsqrt(jnp.mean(xv * xv, axis=-1, keepdims=True) + EPS)
    return xv * r, r


def _rms_bwd(dh, xn, r, gain):
    dxn = dh * gain
    dx = r * (dxn - xn * jnp.mean(dxn * xn, axis=-1, keepdims=True))
    return dx, jnp.sum(dh * xn, axis=0, keepdims=True)


def _gelu(x):
    t = jnp.tanh(0.7978845608028654 * (x + 0.044715 * x * x * x))
    return 0.5 * x * (1.0 + t), t


def _gelu_grad(x, t):
    return 0.5 * (1.0 + t) + 0.5 * x * (1.0 - t * t) * 0.7978845608028654 * (1.0 + 3.0 * 0.044715 * x * x)


def _place():
    x, y, cc = lax.axis_index("x"), lax.axis_index("y"), lax.axis_index("c")
    return x, y, cc, [(1 - x, y), (x, 1 - y), (1 - x, 1 - y)]


class _Gather:
    def __init__(self, shard):
        self.inputs = [shard]
        self.out_shape = [jax.ShapeDtypeStruct((N_DEV,) + shard.shape, shard.dtype)]
        self.scratch = [pltpu.SemaphoreType.DMA((7,)), pltpu.SemaphoreType.DMA((7,)), pltpu.SemaphoreType.DMA]

    def _copies(self, ins, outs, scr):
        (x_ref,), (out_ref,), (send_sems, recv_sems, local_sem) = ins, outs, scr
        x, y, cc, chips = _place()

        def slab(px, py, pc):
            return out_ref.at[4 * px + 2 * py + pc]

        def copy(k, block, to, src=None):
            return pltpu.make_async_remote_copy(
                src_ref=slab(*block) if src is None else src, dst_ref=slab(*block),
                send_sem=send_sems.at[k], recv_sem=recv_sems.at[k], device_id=to, device_id_type=MESH)

        me, sibling = (x, y, cc), (x, y, 1 - cc)
        x_nbr, y_nbr, diagonal = chips
        mine = pltpu.make_async_copy(x_ref, slab(*me), local_sem)
        first = [copy(0, me, sibling, src=x_ref), copy(1, me, (*x_nbr, cc), src=x_ref), copy(2, me, (*y_nbr, cc), src=x_ref)]
        neighbours = [copy(1, (*x_nbr, cc), me), copy(2, (*y_nbr, cc), me)]
        second_hand = copy(3, (x ^ (1 - cc), y ^ cc, cc), (x ^ cc, y ^ (1 - cc), cc))
        from_diagonal = copy(3, (*diagonal, cc), me)
        passed = [copy(4 + j, (*chip, cc), sibling) for j, chip in enumerate(chips)]
        from_sibling = [copy(0, sibling, me)] + [copy(4 + j, (*chip, 1 - cc), me) for j, chip in enumerate(chips)]
        return mine, first, neighbours, second_hand, from_diagonal, passed, from_sibling

    def begin(self, *refs):
        mine, first = self._copies(*refs)[:2]
        mine.start()
        for cp in first:
            cp.start()

    def mid(self, *refs):
        _, _, neighbours, second_hand, _, passed, _ = self._copies(*refs)
        for cp in neighbours:
            cp.wait_recv()
        second_hand.start()
        passed[0].start()
        passed[1].start()

    def relay(self, *refs):
        _, _, _, _, from_diagonal, passed, _ = self._copies(*refs)
        from_diagonal.wait_recv()
        passed[2].start()

    def end(self, *refs):
        mine, first, _, second_hand, _, passed, from_sibling = self._copies(*refs)
        for cp in from_sibling:
            cp.wait_recv()
        for cp in first + [second_hand] + passed:
            cp.wait_send()
        mine.wait()


class _Direct:
    def begin(self, *refs):
        keep, give = self._copies(*refs)
        for cp in keep + give:
            cp.start()

    def mid(self, *refs):
        pass

    def relay(self, *refs):
        pass

    def end(self, *refs):
        keep, give = self._copies(*refs)
        for cp in give:
            cp.wait_recv()
        for cp in give:
            cp.wait_send()
        for cp in keep:
            cp.wait()


class _SiblingSwap(_Direct):
    def __init__(self, grads):
        n = len(grads)
        self.which = [w for _, w in grads]
        self.inputs = [g for g, _ in grads]
        self.out_shape = [jax.ShapeDtypeStruct((4,) + g.shape[1:], g.dtype) for g, _ in grads]
        self.scratch = [pltpu.SemaphoreType.DMA((n, 4)), pltpu.SemaphoreType.DMA((n, 4))]

    def _copies(self, ins, outs, scr):
        send_sems, recv_sems = scr
        x, y, cc, _ = _place()
        return [], [pltpu.make_async_remote_copy(src_ref=g_ref.at[N_DEV * w + 2 * j + 1 - cc], dst_ref=got_ref.at[j],
                                                 send_sem=send_sems.at[i, j], recv_sem=recv_sems.at[i, j], device_id=(x, y, 1 - cc),
                                                 device_id_type=MESH)
                    for i, (g_ref, got_ref, w) in enumerate(zip(ins, outs, self.which)) for j in range(4)]


class _ChipScatter(_Direct):
    def __init__(self, sums):
        n = len(sums)
        self.inputs = list(sums)
        self.out_shape = [jax.ShapeDtypeStruct((3,) + s.shape[1:], s.dtype) for s in sums]
        self.scratch = [pltpu.SemaphoreType.DMA((n, 3)), pltpu.SemaphoreType.DMA((n, 3))]

    def _copies(self, ins, outs, scr):
        send_sems, recv_sems = scr
        _, _, cc, chips = _place()
        return [], [pltpu.make_async_remote_copy(src_ref=s_ref.at[2 * px + py], dst_ref=got_ref.at[j], send_sem=send_sems.at[i, j],
                                                 recv_sem=recv_sems.at[i, j], device_id=(px, py, cc), device_id_type=MESH)
                    for i, (s_ref, got_ref) in enumerate(zip(ins, outs)) for j, (px, py) in enumerate(chips)]


class _AllToAll(_Direct):
    def __init__(self, blocks):
        n = len(blocks)
        self.inputs = list(blocks)
        self.out_shape = [jax.ShapeDtypeStruct((N_DEV,) + b.shape, b.dtype) for b in blocks]
        self.scratch = [pltpu.SemaphoreType.DMA((n, 7)), pltpu.SemaphoreType.DMA((n, 7)), pltpu.SemaphoreType.DMA((n,))]

    def _copies(self, ins, outs, scr):
        send_sems, recv_sems, local_sems = scr
        x, y, cc, _ = _place()
        me = 4 * x + 2 * y + cc
        keep = [pltpu.make_async_copy(b_ref, got_ref.at[me], local_sems.at[i]) for i, (b_ref, got_ref) in enumerate(zip(ins, outs))]
        give = [pltpu.make_async_remote_copy(src_ref=b_ref, dst_ref=got_ref.at[me], send_sem=send_sems.at[i, k - 1],
                                             recv_sem=recv_sems.at[i, k - 1],
                                             device_id=(x ^ ((k >> 2) & 1), y ^ ((k >> 1) & 1), cc ^ (k & 1)), device_id_type=MESH)
                for i, (b_ref, got_ref) in enumerate(zip(ins, outs)) for k in range(1, N_DEV)]
        return keep, give


def _split_refs(refs, counts):
    out, pos = [], 0
    for n in counts:
        out.append(list(refs[pos:pos + n]))
        pos += n
    return out


def _bind(comms, c_in, c_out, c_scr):
    ins = _split_refs(c_in, [len(c.inputs) for c in comms])
    outs = _split_refs(c_out, [len(c.out_shape) for c in comms])
    scr = _split_refs(c_scr, [len(c.scratch) for c in comms])
    return [(c, (i, o, s)) for c, i, o, s in zip(comms, ins, outs, scr)]


def _call(body, *, name, grid, in_specs, out_specs, out_shape, scratch_shapes, args, comms=()):
    c_in = [a for c in comms for a in c.inputs]
    c_out = [s for c in comms for s in c.out_shape]
    c_scr = [s for c in comms for s in c.scratch]
    counts = [len(in_specs), len(c_in), len(out_shape), len(c_out), len(scratch_shapes), len(c_scr)]

    def full(*refs):
        ins, cin, outs, cout, scr, cscr = _split_refs(refs, counts)
        bound = _bind(comms, cin, cout, cscr)
        if comms:
            def at(steps):
                return functools.reduce(jnp.logical_and, [pl.program_id(ax) == s for ax, s in enumerate(steps)])

            first, last = at([0] * len(grid)), at([n - 1 for n in grid])

            @pl.when(first)
            def _():
                for c, r in bound:
                    c.begin(*r)

            @pl.when(at([(grid[0] - 1) // 2] + [0] * (len(grid) - 1)))
            def _():
                for c, r in bound:
                    c.mid(*r)

            @pl.when(last)
            def _():
                for c, r in bound:
                    c.relay(*r)

        body(*ins, *outs, *scr)
        if comms:
            @pl.when(last)
            def _():
                for c, r in bound:
                    c.end(*r)

    res = pl.pallas_call(
        full, name=name, grid=grid,
        in_specs=list(in_specs) + [ANY] * len(c_in), out_specs=list(out_specs) + [ANY] * len(c_out),
        out_shape=list(out_shape) + c_out, scratch_shapes=list(scratch_shapes) + c_scr,
        compiler_params=_cparams(("arbitrary",) * len(grid)),
    )(*args, *c_in)
    return list(res[:len(out_shape)]), list(res[len(out_shape):])


def _comm_only(comms, name):
    c_in = [a for c in comms for a in c.inputs]
    c_out = [s for c in comms for s in c.out_shape]
    c_scr = [s for c in comms for s in c.scratch]

    def full(*refs):
        cin, cout, cscr = _split_refs(refs, [len(c_in), len(c_out), len(c_scr)])
        bound = _bind(comms, cin, cout, cscr)
        for phase in ("begin", "mid", "relay", "end"):
            for c, r in bound:
                getattr(c, phase)(*r)

    return list(pl.pallas_call(full, name=name, in_specs=[ANY] * len(c_in), out_specs=[ANY] * len(c_out), out_shape=c_out,
                               scratch_shapes=c_scr)(*c_in))


def _my_index(*axes_and_weights):
    return sum(w * lax.axis_index(a) for a, w in axes_and_weights).astype(jnp.int32).reshape(1)


def _pair_sums(grads, gots, name):
    n = len(grads)

    def body(c_ref, *refs):
        for a_ref, b_ref, o_ref in zip(refs[:n], refs[n:2 * n], refs[2 * n:]):
            o_ref[...] = (a_ref[...].astype(F32) + b_ref[...].astype(F32)).astype(BF16)

    def tile(g):
        return pl.BlockSpec((1,) + g.shape[1:], lambda j, c_ref: (j, 0, 0))

    def mine(g, w):
        return pl.BlockSpec((1, None) + g.shape[1:], lambda j, c_ref: (4 * w + j, c_ref[0], 0, 0))

    return list(pl.pallas_call(
        body, name=name,
        grid_spec=pltpu.PrefetchScalarGridSpec(num_scalar_prefetch=1, grid=(4,),
                                               in_specs=[mine(g, w) for g, w in grads] + [tile(g) for g in gots],
                                               out_specs=[tile(g) for g in gots]),
        out_shape=[jax.ShapeDtypeStruct(g.shape, BF16) for g in gots],
        compiler_params=_cparams(("arbitrary",)))(_my_index(("c", 1)), *[g.reshape((-1, 2) + g.shape[1:]) for g, _ in grads], *gots))


def _ffn_fwd(x, gain, gw_gu, gw_d, off_d, name, comms=(), head=None):
    t = x.shape[0]
    n_head = 0 if head is None else 2

    def body(x_ref, g_ref, gu_ref, d_ref, *refs):
        head_in, (o_ref, ab_ref, h_ref), head_out = refs[:n_head], refs[n_head:n_head + 3], refs[n_head + 3:2 * n_head + 3]
        wg, wu, wd, s_scr, sems = refs[2 * n_head + 3:]

        @pl.when(pl.program_id(0) == 0)
        def _():
            cps = _load_rows(gu_ref, wg, 0, R_FF, sems.at[0]) + _load_rows(gu_ref, wu, R_FF, R_FF, sems.at[1]) \
                + _load_rows(d_ref, wd, off_d, R_FF, sems.at[2])
            for o in head_out:
                o[...] = jnp.zeros_like(o)
            for cp in cps:
                cp.wait()

        xv = x_ref[...]
        xn, _ = _rms(xv)
        h = (xn * g_ref[...]).astype(BF16)
        h_ref[...] = h
        for c in range(F // FC):
            rows = pl.ds(c * FC, FC)
            a = _nt(h, wg[rows, :])
            b = _nt(h, wu[rows, :])
            ab_ref[:, c * FC:(c + 1) * FC] = a.astype(BF16)
            ab_ref[:, F + c * FC:F + (c + 1) * FC] = b.astype(BF16)
            s_scr[:, c * FC:(c + 1) * FC] = (a * jax.nn.sigmoid(a) * b).astype(BF16)
        out = xv + 0.5 * _nn(s_scr[...], wd[...])
        if head is None:
            o_ref[...] = out
        else:
            (gf_ref, t_ref), (dg_ref, loss_ref) = head_in, head_out
            gain_f = gf_ref[...]
            yn, r = _rms(out)
            err = yn * gain_f - t_ref[...]
            loss_ref[...] += 0.5 * jnp.sum(jnp.mean(err * err, axis=-1, keepdims=True), axis=0, keepdims=True)
            o_ref[...], dg = _rms_bwd(err * (1.0 / D), yn, r, gain_f)
            dg_ref[...] += dg

    tile = pl.BlockSpec((TM, D), lambda i: (i, 0))
    row = pl.BlockSpec((1, D), lambda i: (0, 0))
    head_specs = [] if head is None else [row, pl.BlockSpec((1, 128), lambda i: (0, 0))]
    head_shapes = [] if head is None else [jax.ShapeDtypeStruct((1, D), F32), jax.ShapeDtypeStruct((1, 128), F32)]
    res, got = _call(
        body, name=name, grid=(t // TM,),
        in_specs=[tile, row, ANY, ANY] + ([] if head is None else [row, tile]),
        out_specs=[tile, pl.BlockSpec((TM, 2 * F), lambda i: (i, 0)), tile] + head_specs,
        out_shape=[jax.ShapeDtypeStruct((t, D), F32), jax.ShapeDtypeStruct((t, 2 * F), BF16), jax.ShapeDtypeStruct((t, D), BF16)] + head_shapes,
        scratch_shapes=[pltpu.VMEM((F, D), BF16)] * 3 + [pltpu.VMEM((TM, F), BF16), pltpu.SemaphoreType.DMA((3, N_DEV))],
        args=(x, gain, gw_gu, gw_d) + (() if head is None else tuple(head)), comms=comms)
    return (res[0], res[1], res[2], got, *res[3:])


def _ffn_bwd_hidden(ab, dout, gw_d, off_d, name, comms=()):
    t = ab.shape[0]
    n_steps = t // TM
    row_chunks = [(r, min(512, F - r)) for r in range(0, F, 512)]

    def tile(w):
        return pl.BlockSpec((TM, w), lambda i: (i, 0))

    def hidden(ab_ref, do_ref, d_ref, dab_ref, dwd_ref, wd, s_scr, acc, sems, out_sem):
        step = pl.program_id(0)

        @pl.when(step == 0)
        def _():
            cps = _load_rows(d_ref, wd, off_d, R_FF, sems)
            acc[...] = jnp.zeros_like(acc)
            for cp in cps:
                cp.wait()

        df = (0.5 * do_ref[...]).astype(BF16)
        for c in range(F // FC):
            a = ab_ref[:, c * FC:(c + 1) * FC].astype(F32)
            b = ab_ref[:, F + c * FC:F + (c + 1) * FC].astype(F32)
            sg = jax.nn.sigmoid(a)
            sl = a * sg
            ds = _nt(df, wd[pl.ds(c * FC, FC), :])
            dab_ref[:, c * FC:(c + 1) * FC] = (ds * b * (sg * (1.0 + a * (1.0 - sg)))).astype(BF16)
            dab_ref[:, F + c * FC:F + (c + 1) * FC] = (ds * sl).astype(BF16)
            s_scr[:, c * FC:(c + 1) * FC] = (sl * b).astype(BF16)
        for r, n in row_chunks:
            acc[r:r + n, :] += _tn(s_scr[:, r:r + n], df)

        @pl.when(step == n_steps - 1)
        def _():
            wd[...] = acc[...].astype(BF16)
            out = pltpu.make_async_copy(wd, dwd_ref, out_sem)
            out.start()
            out.wait()

    (dab, dwd), got = _call(
        hidden, name=name, grid=(n_steps,),
        in_specs=[tile(2 * F), tile(D), ANY], out_specs=[tile(2 * F), ANY],
        out_shape=[jax.ShapeDtypeStruct((t, 2 * F), BF16), jax.ShapeDtypeStruct((F, D), BF16)],
        scratch_shapes=[pltpu.VMEM((F, D), BF16), pltpu.VMEM((TM, F), BF16), pltpu.VMEM((F, D), F32), pltpu.SemaphoreType.DMA((N_DEV,)),
                        pltpu.SemaphoreType.DMA],
        args=(ab, dout, gw_d), comms=comms)
    return (dab, dwd.reshape(N_DEV, R_FF, D)), got


def _ffn_bwd_input(x, gain, dab, dout, gw_gu, name, comms=()):
    t = x.shape[0]

    def body(x_ref, g_ref, dab_ref, do_ref, gu_ref, dx_ref, dg_ref, wgu, sems):
        @pl.when(pl.program_id(0) == 0)
        def _():
            cps = _load_rows(gu_ref, wgu.at[0:F], 0, R_FF, sems.at[0]) + _load_rows(gu_ref, wgu.at[F:2 * F], R_FF, R_FF, sems.at[1])
            dg_ref[...] = jnp.zeros_like(dg_ref)
            for cp in cps:
                cp.wait()

        gain_v = g_ref[...]
        xn, r = _rms(x_ref[...])
        dh = _nn(dab_ref[...], wgu[...])
        dxn, dg = _rms_bwd(dh, xn, r, gain_v)
        dg_ref[...] += dg
        dx_ref[...] = do_ref[...] + dxn

    def tile(w):
        return pl.BlockSpec((TM, w), lambda i: (i, 0))

    row = pl.BlockSpec((1, D), lambda i: (0, 0))
    return _call(
        body, name=name, grid=(t // TM,),
        in_specs=[tile(D), row, tile(2 * F), tile(D), ANY], out_specs=[tile(D), row],
        out_shape=[jax.ShapeDtypeStruct((t, D), F32), jax.ShapeDtypeStruct((1, D), F32)],
        scratch_shapes=[pltpu.VMEM((2 * F, D), BF16), pltpu.SemaphoreType.DMA((2, N_DEV))],
        args=(x, gain, dab, dout, gw_gu), comms=comms)


def _weight_grad(a, b, name, col_off=0, m=None, comms=None, mats=1):
    t = a.shape[0]
    m = a.shape[1] if m is None else m
    n = b.shape[1]
    tmm = 512 if m % 512 == 0 else 256
    first = col_off // tmm

    def body(a_ref, b_ref, o_ref):
        o_ref[...] = _tn(a_ref[...], b_ref[...]).astype(BF16)

    (out,), got = _call(
        body, name=name, grid=(m // tmm,),
        in_specs=[pl.BlockSpec((t, tmm), lambda i: (0, first + i)), pl.BlockSpec((t, n), lambda i: (0, 0))],
        out_specs=[pl.BlockSpec((tmm, n), lambda i: (i, 0))],
        out_shape=[jax.ShapeDtypeStruct((m, n), BF16)], scratch_shapes=[], args=(a, b), comms=comms or ())
    out = out.reshape(mats * N_DEV, m // (mats * N_DEV), n)
    return out if comms is None else (out, got)


def _weight_grad_pieces(pieces, b, name):
    t, n = b.shape
    tmm = 512
    tiles = [(k, j) for k, a in enumerate(pieces) for j in range(a.shape[1] // tmm)]
    m = tmm * len(tiles)

    def body(*refs):
        a_refs, (b_ref, o_ref, slots, b_vmem, sems, b_sem, stage, out_sems) = refs[:len(pieces)], refs[len(pieces):]

        def fetch(s):
            k, j = tiles[s]
            cp = pltpu.make_async_copy(a_refs[k].at[:, j * tmm:(j + 1) * tmm], slots.at[s % 2], sems.at[s % 2])
            cp.start()
            return cp

        load_b = pltpu.make_async_copy(b_ref, b_vmem, b_sem)
        load_b.start()
        cp = fetch(0)
        load_b.wait()
        leaving = []
        for s in range(len(tiles)):
            nxt = fetch(s + 1) if s + 1 < len(tiles) else None
            cp.wait()
            if s >= 2:
                leaving[s - 2].wait()
            stage[s % 2] = _tn(slots[s % 2], b_vmem[...]).astype(BF16)
            leaving.append(pltpu.make_async_copy(stage.at[s % 2], o_ref.at[s * tmm:(s + 1) * tmm, :], out_sems.at[s % 2]))
            leaving[s].start()
            cp = nxt
        for out_cp in leaving[-2:]:
            out_cp.wait()

    out = pl.pallas_call(
        body, name=name, in_specs=[ANY] * (len(pieces) + 1), out_specs=ANY,
        out_shape=jax.ShapeDtypeStruct((m, n), BF16),
        scratch_shapes=[pltpu.VMEM((2, t, tmm), BF16), pltpu.VMEM((t, n), BF16), pltpu.SemaphoreType.DMA((2,)), pltpu.SemaphoreType.DMA(()),
                        pltpu.VMEM((2, tmm, n), BF16), pltpu.SemaphoreType.DMA((2,))],
        compiler_params=_cparams())(*pieces, b)
    return out.reshape(N_DEV, m // N_DEV, n)


def _mix_proj_fwd(x, gain, b_gate, gw, comms=()):
    t = x.shape[0]

    def body(x_ref, g_ref, bg_ref, gw_ref, q_ref, k_ref, v_ref, zs_ref, gt_ref, h_ref, win, sems):
        @pl.when(pl.program_id(0) == 0)
        def _():
            for cp in _load_rows(gw_ref, win, OFF_IN, R_IN, sems):
                cp.wait()

        xn, _ = _rms(x_ref[...])
        h = (xn * g_ref[...]).astype(BF16)
        h_ref[...] = h
        z = _nt(h, win[...])
        q_ref[...] = (z[:, 0:D_ATT] * 0.125).astype(BF16)
        k_ref[...] = z[:, D_ATT:2 * D_ATT].astype(BF16)
        v_ref[...] = z[:, 2 * D_ATT:3 * D_ATT].astype(BF16)
        zs_ref[...] = z[:, 3 * D_ATT:3 * D_ATT + 2 * D_SGU].astype(BF16)
        gt_ref[...] = jax.nn.sigmoid(z[:, 3 * D_ATT + 2 * D_SGU:] + bg_ref[...]).astype(BF16)

    def tile(w):
        return pl.BlockSpec((TM, w), lambda i: (i, 0))

    return _call(
        body, name="mix_proj_fwd", grid=(t // TM,),
        in_specs=[tile(D), pl.BlockSpec((1, D), lambda i: (0, 0)), pl.BlockSpec((1, 2 * D), lambda i: (0, 0)), ANY],
        out_specs=[tile(D_ATT), tile(D_ATT), tile(D_ATT), tile(2 * D_SGU), tile(2 * D), tile(D)],
        out_shape=[jax.ShapeDtypeStruct((t, D_ATT), BF16)] * 3 + [jax.ShapeDtypeStruct((t, 2 * D_SGU), BF16),
                                                                   jax.ShapeDtypeStruct((t, 2 * D), BF16),
                                                                   jax.ShapeDtypeStruct((t, D), BF16)],
        scratch_shapes=[pltpu.VMEM((D_IN, D), BF16), pltpu.SemaphoreType.DMA((N_DEV,))],
        args=(x, gain, b_gate, gw), comms=comms)


def _mix_proj_bwd(dz_pieces, x, gain, dres, gw, comms=()):
    t = x.shape[0]
    widths = [a.shape[1] for a in dz_pieces]
    offs = [sum(widths[:k]) for k in range(len(widths))]

    def body(*refs):
        dz_refs, (x_ref, g_ref, dr_ref, gw_ref, dx_ref, dg_ref, win, sems) = refs[:len(widths)], refs[len(widths):]

        @pl.when(pl.program_id(0) == 0)
        def _():
            cps = _load_rows(gw_ref, win, OFF_IN, R_IN, sems)
            dg_ref[...] = jnp.zeros_like(dg_ref)
            for cp in cps:
                cp.wait()

        dh = sum(_nn(r[...], win[o:o + w, :]) for r, o, w in zip(dz_refs, offs, widths))
        xn, r = _rms(x_ref[...])
        dxn, dg = _rms_bwd(dh, xn, r, g_ref[...])
        dg_ref[...] += dg
        dx_ref[...] = dr_ref[...] + dxn

    def tile(w):
        return pl.BlockSpec((TM, w), lambda i: (i, 0))

    row = pl.BlockSpec((1, D), lambda i: (0, 0))
    return _call(
        body, name="mix_proj_bwd", grid=(t // TM,),
        in_specs=[tile(w) for w in widths] + [tile(D), row, tile(D), ANY], out_specs=[tile(D), row],
        out_shape=[jax.ShapeDtypeStruct((t, D), F32), jax.ShapeDtypeStruct((1, D), F32)],
        scratch_shapes=[pltpu.VMEM((D_IN, D), BF16), pltpu.SemaphoreType.DMA((N_DEV,))],
        args=(*dz_pieces, x, gain, dres, gw), comms=comms)


SKEW_W = KW + QB
N_CAP = 2 * QB - REL_CLIP + 1


def _tables(rel_bias, w_s, b_s):
    cap = rel_bias[:, 2 * REL_CLIP:]
    diag = jnp.concatenate([jnp.broadcast_to(cap, (HEADS, N_CAP)), rel_bias[:, 2 * REL_CLIP - 1::-1],
                            jnp.broadcast_to(cap, (HEADS, SKEW_W - N_CAP - 2 * REL_CLIP))], axis=1)

    def fill(ins, outs):
        (d_ref, ws_ref, bs_ref), (bias_ref, wm2_ref, wmt2_ref, bsx_ref) = ins, outs
        lag = lax.broadcasted_iota(jnp.int32, (QB, KW), 1) // CHUNK - lax.broadcasted_iota(jnp.int32, (QB, KW), 0) // CHUNK
        band = (lag >= 0) & (lag <= N_LEFT)
        for h in range(HEADS):
            rows = jnp.broadcast_to(d_ref[h:h + 1, :], (QB, SKEW_W))
            bias_ref[h] = jnp.where(band, pltpu.roll(rows, 0, 1, stride=1, stride_axis=0)[:, :KW], NEG_INF)
        i = lax.broadcasted_iota(jnp.int32, (SGU_BLOCK, SGU_BLOCK), 0)
        j = lax.broadcasted_iota(jnp.int32, (SGU_BLOCK, SGU_BLOCK), 1)
        causal = (i // CHUNK) >= (j // CHUNK)
        for pair in range(4):
            w0, w1 = jnp.where(causal, ws_ref[2 * pair], 0.0), jnp.where(causal, ws_ref[2 * pair + 1], 0.0)
            wm2_ref[pair] = jnp.concatenate([w0, w1], axis=1).astype(BF16)
            wmt2_ref[pair] = jnp.concatenate([w0.T, w1.T], axis=1).astype(BF16)
        by_row = jnp.concatenate([bs_ref[...], jnp.zeros((SGU_BLOCK - 8, SGU_BLOCK), F32)], axis=0).T
        group = lax.broadcasted_iota(jnp.int32, (SGU_BLOCK, D_SGU), 1) // 64
        spread = jnp.zeros((SGU_BLOCK, D_SGU), F32)
        for g in range(8):
            spread = jnp.where(group == g, by_row[:, g:g + 1], spread)
        bsx_ref[...] = spread

    shapes = [jax.ShapeDtypeStruct((HEADS, QB, KW), F32), jax.ShapeDtypeStruct((4, SGU_BLOCK, 2 * SGU_BLOCK), BF16),
              jax.ShapeDtypeStruct((4, SGU_BLOCK, 2 * SGU_BLOCK), BF16), jax.ShapeDtypeStruct((SGU_BLOCK, D_SGU), F32)]
    return [diag, w_s, b_s], shapes, fill


def _att_specs():
    qspec = pl.BlockSpec((QB, D_ATT), lambda g: (g, 0))
    kspecs = [pl.BlockSpec((QB, D_ATT), lambda g: (jnp.maximum(g - 2, 0), 0)),
              pl.BlockSpec((QB, D_ATT), lambda g: (jnp.maximum(g - 1, 0), 0)), qspec]
    bspec = pl.BlockSpec((HEADS, QB, KW), lambda g: (0, 0, 0))
    return qspec, kspecs, bspec


def _att_probs(qm, kp, bias, valid):
    s = jnp.where(valid, _nt(qm, kp) + bias, NEG_INF)
    e = jnp.exp(s - jnp.max(s, axis=-1, keepdims=True))
    return e / jnp.sum(e, axis=-1, keepdims=True)


def _att_valid():
    g = pl.program_id(0)
    blk = lax.broadcasted_iota(jnp.int32, (QB, KW), 1) // QB
    return (blk + g) >= 2


def _att_fwd(q, k, v, bias, comms=()):
    t = q.shape[0]

    def body(q_ref, k0, k1, k2, v0, v1, v2, b_ref, y_ref):
        valid = _att_valid()
        first = lax.broadcasted_iota(jnp.int32, (1, 128), 1) < 64
        for p in range(HEADS // 2):
            lanes = slice(p * 128, (p + 1) * 128)
            qp = q_ref[:, lanes]
            kp = jnp.concatenate([k0[:, lanes], k1[:, lanes], k2[:, lanes]], axis=0)
            vp = jnp.concatenate([v0[:, lanes], v1[:, lanes], v2[:, lanes]], axis=0)
            out = jnp.zeros((QB, 128), F32)
            for hh in range(2):
                mask = first if hh == 0 else jnp.logical_not(first)
                pr = _att_probs(jnp.where(mask, qp, 0), kp, b_ref[2 * p + hh], valid)
                out = out + _nn(pr.astype(BF16), jnp.where(mask, vp, 0))
            y_ref[:, lanes] = out.astype(BF16)

    qspec, kspecs, bspec = _att_specs()
    (out,), got = _call(
        body, name="att_fwd", grid=(t // QB,),
        in_specs=[qspec] + kspecs + kspecs + [bspec], out_specs=[qspec],
        out_shape=[jax.ShapeDtypeStruct((t, D_ATT), BF16)], scratch_shapes=[],
        args=(q, k, k, k, v, v, v, bias), comms=comms)
    return out, got


def _att_bwd(q, k, v, bias, dy, comms=()):
    t = q.shape[0]
    n_blocks = t // QB

    def body(q_ref, k0, k1, k2, v0, v1, v2, b_ref, dy_ref, dq_ref, dk_ref, dv_ref, db_ref, dk_acc, dv_acc):
        g = pl.program_id(0)

        @pl.when(g == 0)
        def _():
            db_ref[...] = jnp.zeros_like(db_ref)
            dk_acc[...] = jnp.zeros_like(dk_acc)
            dv_acc[...] = jnp.zeros_like(dv_acc)

        valid = _att_valid()
        first = lax.broadcasted_iota(jnp.int32, (1, 128), 1) < 64
        for p in range(HEADS // 2):
            lanes = slice(p * 128, (p + 1) * 128)
            qp = q_ref[:, lanes]
            dyp = dy_ref[:, lanes]
            kp = jnp.concatenate([k0[:, lanes], k1[:, lanes], k2[:, lanes]], axis=0)
            vp = jnp.concatenate([v0[:, lanes], v1[:, lanes], v2[:, lanes]], axis=0)
            dq = jnp.zeros((QB, 128), F32)
            dk = jnp.zeros((KW, 128), F32)
            dv = jnp.zeros((KW, 128), F32)
            for hh in range(2):
                mask = first if hh == 0 else jnp.logical_not(first)
                qm = jnp.where(mask, qp, 0)
                dym = jnp.where(mask, dyp, 0)
                pr = _att_probs(qm, kp, b_ref[2 * p + hh], valid)
                dp = _nt(dym, vp)
                ds = pr * (dp - jnp.sum(dp * pr, axis=-1, keepdims=True))
                db_ref[2 * p + hh] += ds
                dsb = ds.astype(BF16)
                dq = dq + _nn(dsb, jnp.where(mask, kp, 0))
                dk = dk + _tn(dsb, qm)
                dv = dv + _tn(pr.astype(BF16), dym)
            dq_ref[:, lanes] = (dq * 0.125).astype(BF16)
            for j in range(3):
                rows = pl.ds(pl.multiple_of(jnp.maximum(g - 2 + j, 0) * QB, QB), QB)
                dk_acc[rows, lanes] += dk[j * QB:(j + 1) * QB]
                dv_acc[rows, lanes] += dv[j * QB:(j + 1) * QB]

        @pl.when(g == n_blocks - 1)
        def _():
            dk_ref[...] = dk_acc[...].astype(BF16)
            dv_ref[...] = dv_acc[...].astype(BF16)

    qspec, kspecs, bspec = _att_specs()
    full = pl.BlockSpec((t, D_ATT), lambda g: (0, 0))
    return _call(
        body, name="att_bwd", grid=(n_blocks,),
        in_specs=[qspec] + kspecs + kspecs + [bspec, qspec], out_specs=[qspec, full, full, bspec],
        out_shape=[jax.ShapeDtypeStruct((t, D_ATT), BF16)] * 3 + [jax.ShapeDtypeStruct((HEADS, QB, KW), F32)],
        scratch_shapes=[pltpu.VMEM((t, D_ATT), F32)] * 2,
        args=(q, k, k, k, v, v, v, bias, dy), comms=comms)


def _rel_bias_grad(dbias, comms=()):
    c_in = [a for c in comms for a in c.inputs]
    c_out = [s for c in comms for s in c.out_shape]
    c_scr = [s for c in comms for s in c.scratch]

    def body(db_ref, *refs):
        cin, (cs_ref, tot_ref), cout, cscr = _split_refs(refs, [len(c_in), 2, len(c_out), len(c_scr)])
        bound = _bind(comms, cin, cout, cscr)
        for c, r in bound:
            c.begin(*r)
        lane = lax.broadcasted_iota(jnp.int32, (1, SKEW_W), 1)
        capped = (lane < N_CAP) | (lane > KW)
        pad = jnp.zeros((8, QB), F32)
        for h in range(HEADS):
            z = jnp.concatenate([db_ref[h, 0:8, :], pad], axis=1)
            for a in range(1, QB // 8):
                z = z + pltpu.roll(jnp.concatenate([db_ref[h, 8 * a:8 * a + 8, :], pad], axis=1), SKEW_W - 8 * a, 1)
            cs = z[0:1, :]
            for b in range(1, 8):
                cs = cs + pltpu.roll(z[b:b + 1, :], SKEW_W - b, 1)
            cs_ref[h:h + 1, :] = cs
            tot_ref[h:h + 1, :] = jnp.broadcast_to(jnp.sum(jnp.where(capped, cs, 0.0), axis=1, keepdims=True), (1, 128))
        for phase in ("mid", "relay", "end"):
            for c, r in bound:
                getattr(c, phase)(*r)

    whole = pl.BlockSpec(memory_space=pltpu.VMEM)
    cs, tot, *got = pl.pallas_call(
        body, name="rel_bias_grad", in_specs=[whole] + [ANY] * len(c_in), out_specs=[whole, whole] + [ANY] * len(c_out),
        out_shape=[jax.ShapeDtypeStruct((HEADS, SKEW_W), F32), jax.ShapeDtypeStruct((HEADS, 128), F32)] + c_out, scratch_shapes=c_scr,
    )(dbias, *c_in)
    return jnp.concatenate([cs[:, KW:N_CAP - 1:-1], tot[:, :1]], axis=1), got


def _group_stack(blk, first):
    return jnp.concatenate([jnp.where(first, blk, 0), jnp.where(first, 0, blk)], axis=0)


def _sgu_norm(zs_ref, lng, lnb):
    zs = zs_ref[...].astype(F32)
    ga, th = _gelu(zs)
    u, vs = ga[:, :D_SGU], ga[:, D_SGU:]
    mu = jnp.mean(vs, axis=-1, keepdims=True)
    cen = vs - mu
    rstd = lax.rsqrt(jnp.mean(cen * cen, axis=-1, keepdims=True) + EPS)
    xhat = cen * rstd
    return zs, th, u, xhat, rstd, xhat * lng + lnb


def _sgu_mix(vb, wm2_ref, bsx, s_ref):
    first = lax.broadcasted_iota(jnp.int32, (1, 128), 1) < 64
    for n in range(TM // SGU_BLOCK):
        for p in range(4):
            blk = vb[n * 128:(n + 1) * 128, p * 128:(p + 1) * 128]
            s_ref[n * 128:(n + 1) * 128, p * 128:(p + 1) * 128] = _nn(wm2_ref[p], _group_stack(blk, first)) + bsx[:, p * 128:(p + 1) * 128]


def _merge_fwd(x, zs, gt, y_att, lng, lnb, wm2, bsx, gw):
    t = x.shape[0]

    def body(x_ref, zs_ref, gt_ref, ya_ref, lng_ref, lnb_ref, wm2_ref, bsx_ref, gw_ref, xo_ref, ys_ref, mg_ref,
             wbr, wo, s_scr, sems):
        @pl.when(pl.program_id(0) == 0)
        def _():
            for cp in _load_rows(gw_ref, wbr, OFF_BR, R_BR, sems.at[0]) + _load_rows(gw_ref, wo, OFF_WO, R_WO, sems.at[1]):
                cp.wait()

        _, _, u, _, _, vsn = _sgu_norm(zs_ref, lng_ref[...], lnb_ref[...])
        _sgu_mix(vsn.astype(BF16), wm2_ref, bsx_ref[...], s_scr)
        ys = (u * s_scr[...]).astype(BF16)
        ys_ref[...] = ys
        pa = _nt(ya_ref[...], wbr[:, :D_ATT])
        ps = _nt(ys, wbr[:, D_ATT:])
        mg = (gt_ref[:, :D].astype(F32) * pa + gt_ref[:, D:].astype(F32) * ps).astype(BF16)
        mg_ref[...] = mg
        xo_ref[...] = x_ref[...] + _nn(mg, wo[...])

    def tile(w):
        return pl.BlockSpec((TM, w), lambda i: (i, 0))

    def const(shape):
        return pl.BlockSpec(shape, lambda i: (0,) * len(shape))

    return pl.pallas_call(
        body, name="merge_fwd", grid=(t // TM,),
        in_specs=[tile(D), tile(2 * D_SGU), tile(2 * D), tile(D_ATT), const((1, D_SGU)), const((1, D_SGU)),
                  const((4, 128, 256)), const((128, D_SGU)), ANY],
        out_specs=[tile(D), tile(D_SGU), tile(D)],
        out_shape=[jax.ShapeDtypeStruct((t, D), F32), jax.ShapeDtypeStruct((t, D_SGU), BF16), jax.ShapeDtypeStruct((t, D), BF16)],
        scratch_shapes=[pltpu.VMEM((D, D), BF16), pltpu.VMEM((D, D), BF16), pltpu.VMEM((TM, D_SGU), F32),
                        pltpu.SemaphoreType.DMA((2, N_DEV))],
        compiler_params=_cparams(("arbitrary",)),
    )(x, zs, gt, y_att, lng, lnb, wm2, bsx, gw)


def _merge_bwd(dx, gt, y_att, y_sgu, merged, gw):
    t = dx.shape[0]
    n_steps = t // TM

    def body(dx_ref, gt_ref, ya_ref, ys_ref, mg_ref, gw_ref, dzg_ref, dya_ref, dys_ref, dbg_ref, ga_ref, gs_ref, go_ref,
             wbr, wo, acc_a, acc_s, acc_o, sems):
        step = pl.program_id(0)

        @pl.when(step == 0)
        def _():
            cps = _load_rows(gw_ref, wbr, OFF_BR, R_BR, sems.at[0]) + _load_rows(gw_ref, wo, OFF_WO, R_WO, sems.at[1])
            dbg_ref[...] = jnp.zeros_like(dbg_ref)
            for acc in (acc_a, acc_s, acc_o):
                acc[...] = jnp.zeros_like(acc)
            for cp in cps:
                cp.wait()

        dxb = dx_ref[...].astype(BF16)
        acc_o[...] += _tn(mg_ref[...], dxb)
        dm = _nt(dxb, wo[...])
        for half, y_ref, w, acc in ((0, ya_ref, wbr.at[:, :D_ATT], acc_a), (1, ys_ref, wbr.at[:, D_ATT:], acc_s)):
            cols = slice(half * D, (half + 1) * D)
            gate = gt_ref[:, cols].astype(F32)
            branch = _nt(y_ref[...], w[...])
            dzg = dm * branch * gate * (1.0 - gate)
            dbg_ref[:, cols] += jnp.sum(dzg, axis=0, keepdims=True)
            dzg_ref[:, cols] = dzg.astype(BF16)
            dbr = (dm * gate).astype(BF16)
            acc[...] += _tn(dbr, y_ref[...])
            dy = _nn(dbr, w[...])
            if half == 0:
                dya_ref[...] = dy.astype(BF16)
            else:
                dys_ref[...] = dy

        @pl.when(step == n_steps - 1)
        def _():
            ga_ref[...] = acc_a[...].astype(BF16)
            gs_ref[...] = acc_s[...].astype(BF16)
            go_ref[...] = acc_o[...].astype(BF16)

    def tile(w):
        return pl.BlockSpec((TM, w), lambda i: (i, 0))

    def whole(r, c):
        return pl.BlockSpec((r, c), lambda i: (0, 0))

    dzg, dya, dys, dbg, g_ba, g_bs, g_wo = pl.pallas_call(
        body, name="merge_bwd", grid=(n_steps,),
        in_specs=[tile(D), tile(2 * D), tile(D_ATT), tile(D_SGU), tile(D), ANY],
        out_specs=[tile(2 * D), tile(D_ATT), tile(D_SGU), whole(1, 2 * D), whole(D, D_ATT), whole(D, D_SGU), whole(D, D)],
        out_shape=[jax.ShapeDtypeStruct((t, 2 * D), BF16), jax.ShapeDtypeStruct((t, D_ATT), BF16), jax.ShapeDtypeStruct((t, D_SGU), F32),
                   jax.ShapeDtypeStruct((1, 2 * D), F32), jax.ShapeDtypeStruct((D, D_ATT), BF16), jax.ShapeDtypeStruct((D, D_SGU), BF16),
                   jax.ShapeDtypeStruct((D, D), BF16)],
        scratch_shapes=[pltpu.VMEM((D, D), BF16), pltpu.VMEM((D, D), BF16), pltpu.VMEM((D, D_ATT), F32), pltpu.VMEM((D, D_SGU), F32),
                        pltpu.VMEM((D, D), F32), pltpu.SemaphoreType.DMA((2, N_DEV))],
        compiler_params=_cparams(("arbitrary",)),
    )(dx, gt, y_att, y_sgu, merged, gw)
    return dzg, dya, dys, dbg, g_ba.reshape(N_DEV, R_BR, D_ATT), g_bs.reshape(N_DEV, R_BR, D_SGU), g_wo.reshape(N_DEV, R_WO, D)


def _sgu_bwd(zs, dys, lng, lnb, wm2, wmt2, bsx, comms=()):
    t = zs.shape[0]
    n_steps = t // TM

    def body(zs_ref, dys_ref, lng_ref, lnb_ref, wm2_ref, wmt2_ref, bsx_ref, dzs_ref, dw_ref, dbs_ref, dlg_ref, dlb_ref,
             s_scr, dv_scr, ds_acc):
        i = pl.program_id(0)

        @pl.when(i == 0)
        def _():
            dw_ref[...] = jnp.zeros_like(dw_ref)
            dlg_ref[...] = jnp.zeros_like(dlg_ref)
            dlb_ref[...] = jnp.zeros_like(dlb_ref)
            ds_acc[...] = jnp.zeros_like(ds_acc)

        lng = lng_ref[...]
        zs, th, u, xhat, rstd, vsn = _sgu_norm(zs_ref, lng, lnb_ref[...])
        vb = vsn.astype(BF16)
        _sgu_mix(vb, wm2_ref, bsx_ref[...], s_scr)
        dys = dys_ref[...]
        du = dys * s_scr[...]
        ds = dys * u
        dsb = ds.astype(BF16)
        first = lax.broadcasted_iota(jnp.int32, (1, 128), 1) < 64
        acc = jnp.zeros((SGU_BLOCK, D_SGU), F32)
        for n in range(TM // SGU_BLOCK):
            rows = slice(n * 128, (n + 1) * 128)
            acc = acc + ds[rows]
            for p in range(4):
                lanes = slice(p * 128, (p + 1) * 128)
                stack = _group_stack(dsb[rows, lanes], first)
                dv_scr[rows, lanes] = _nn(wmt2_ref[p], stack)
                dw_ref[p] += _nt(stack, vb[rows, lanes])
        ds_acc[...] += acc
        dvsn = dv_scr[...]
        dlg_ref[...] += jnp.sum(dvsn * xhat, axis=0, keepdims=True)
        dlb_ref[...] += jnp.sum(dvsn, axis=0, keepdims=True)
        dxh = dvsn * lng
        dvs = rstd * (dxh - jnp.mean(dxh, axis=-1, keepdims=True) - xhat * jnp.mean(dxh * xhat, axis=-1, keepdims=True))
        dga = jnp.concatenate([du, dvs], axis=1)
        dzs_ref[...] = (dga * _gelu_grad(zs, th)).astype(BF16)

        @pl.when(i == n_steps - 1)
        def _():
            r = lax.broadcasted_iota(jnp.int32, (256, 128), 0) % SGU_BLOCK
            c = lax.broadcasted_iota(jnp.int32, (256, 128), 1)
            keep = (r // CHUNK) >= (c // CHUNK)
            for p in range(4):
                dw_ref[p] = jnp.where(keep, dw_ref[p], 0.0)
            total = ds_acc[...]
            grp = lax.broadcasted_iota(jnp.int32, (SGU_BLOCK, D_SGU), 1) // 64
            lane = lax.broadcasted_iota(jnp.int32, (SGU_BLOCK, 128), 1)
            out = jnp.zeros((SGU_BLOCK, 128), F32)
            for gi in range(8):
                out = jnp.where(lane == gi, jnp.sum(jnp.where(grp == gi, total, 0.0), axis=1, keepdims=True), out)
            dbs_ref[...] = out

    def tile(w):
        return pl.BlockSpec((TM, w), lambda i: (i, 0))

    def const(shape):
        return pl.BlockSpec(shape, lambda i: (0,) * len(shape))

    return _call(
        body, name="sgu_bwd", grid=(n_steps,),
        in_specs=[tile(2 * D_SGU), tile(D_SGU), const((1, D_SGU)), const((1, D_SGU)), const((4, 128, 256)), const((4, 128, 256)),
                  const((128, D_SGU))],
        out_specs=[tile(2 * D_SGU), const((4, 256, 128)), const((128, 128)), const((1, D_SGU)), const((1, D_SGU))],
        out_shape=[jax.ShapeDtypeStruct((t, 2 * D_SGU), BF16), jax.ShapeDtypeStruct((4, 256, 128), F32),
                   jax.ShapeDtypeStruct((128, 128), F32), jax.ShapeDtypeStruct((1, D_SGU), F32), jax.ShapeDtypeStruct((1, D_SGU), F32)],
        scratch_shapes=[pltpu.VMEM((TM, D_SGU), F32), pltpu.VMEM((TM, D_SGU), F32), pltpu.VMEM((SGU_BLOCK, D_SGU), F32)],
        args=(zs, dys, lng, lnb, wm2, wmt2, bsx), comms=comms)


def _adamw(g, w, m, v):
    m = ADAM_B1 * m + (1.0 - ADAM_B1) * g
    v = ADAM_B2 * v + (1.0 - ADAM_B2) * (g * g)
    m_hat = m / (1.0 - ADAM_B1 ** ADAM_STEP)
    v_hat = v / (1.0 - ADAM_B2 ** ADAM_STEP)
    return -ADAM_LR * (m_hat / (jnp.sqrt(v_hat) + ADAM_EPS) + ADAM_WD * w), m, v


def _adamw_matrices(items, name):
    n = len(items)
    c = items[0][0].shape[2]
    tc = 256

    def body(own_ref, *refs):
        for i, (_, _, _, _, _, transposed) in enumerate(items):
            p_ref, s_ref, w_ref, m_ref, v_ref = refs[5 * i:5 * i + 5]
            g = p_ref[0].astype(F32) + p_ref[1].astype(F32) + p_ref[2].astype(F32) + s_ref[...].astype(F32)
            g = g.T if transposed else g
            res = (g,) + _adamw(g, w_ref[...], m_ref[...], v_ref[...])
            for o_ref, val in zip(refs[5 * n + 4 * i:5 * n + 4 * i + 4], res):
                o_ref[...] = val

    in_specs, out_specs, out_shape, args = [], [], [], []
    for parts, sums, w, m, v, transposed in items:
        r = parts.shape[1]
        own = pl.BlockSpec((None, tc, r), lambda i, o: (0, i, 0)) if transposed else pl.BlockSpec((None, r, tc), lambda i, o: (0, 0, i))
        in_specs += [pl.BlockSpec((3, r, tc), lambda i, o: (0, 0, i)), pl.BlockSpec((None, r, tc), lambda i, o: (o[0], 0, i)), own, own, own]
        out_specs += [own] * 4
        out_shape += [jax.ShapeDtypeStruct(w.shape, F32)] * 4
        args += [parts, sums, w, m, v]
    res = pl.pallas_call(
        body, name=name,
        grid_spec=pltpu.PrefetchScalarGridSpec(num_scalar_prefetch=1, grid=(c // tc,), in_specs=in_specs, out_specs=out_specs),
        out_shape=out_shape, compiler_params=_cparams(("arbitrary",)),
    )(_my_index(("x", 2), ("y", 1)), *args)
    return [list(res[4 * i:4 * i + 4]) for i in range(n)]


_SMALL_2D = {"norm_ffn1": (1, D), "norm_mix": (1, D), "norm_ffn2": (1, D), "norm_final": (1, D), "b_gate": (1, 2 * D),
             "sgu_ln_g": (1, D_SGU), "sgu_ln_b": (1, D_SGU), "sgu_b_s": (8, SGU_BLOCK), "rel_bias": (HEADS, N_REL),
             "sgu_w_s": (8 * SGU_BLOCK, SGU_BLOCK)}


def _adamw_small(parts, loss_parts, p):
    names = list(parts)
    n = len(names)

    def body(*refs):
        got, loss_got, wmv, outs, loss_out = refs[:n], refs[n], refs[n + 1:4 * n + 1], refs[4 * n + 1:8 * n + 1], refs[8 * n + 1]
        for i, name in enumerate(names):
            g = got[i][0]
            for k in range(1, N_DEV):
                g = g + got[i][k]
            if name == "sgu_b_s":
                g = g.T[0:8, :]
            res = (g,) + _adamw(g, wmv[3 * i][...], wmv[3 * i + 1][...], wmv[3 * i + 2][...])
            for o_ref, val in zip(outs[4 * i:4 * i + 4], res):
                o_ref[...] = val
        total = loss_got[0]
        for k in range(1, N_DEV):
            total = total + loss_got[k]
        loss_out[...] = total

    wmv = [p[pre + name].reshape(_SMALL_2D[name]) for name in names for pre in ("", "m_", "v_")]
    res = pl.pallas_call(
        body, name="adamw_small",
        out_shape=[jax.ShapeDtypeStruct(_SMALL_2D[name], F32) for name in names for _ in range(4)] + [jax.ShapeDtypeStruct((1, 128), F32)],
        compiler_params=_cparams())(*[parts[name] for name in names], loss_parts, *wmv)
    return [{name: res[4 * i + j].reshape(p[name].shape) for i, name in enumerate(names)} for j in range(4)], res[-1]


def _pack_rows(slabs, name, comms=(), extra=None):
    flat = [a for groups in slabs for grp in groups for a, _ in grp]
    rows = [[grp[0][0].shape[2] if grp[0][1] else grp[0][0].shape[1] for grp in groups] for groups in slabs]
    e_in, e_out, fill = extra if extra is not None else ([], [], None)
    c_in = [a for c in comms for a in c.inputs]
    c_out = [s for c in comms for s in c.out_shape]
    c_scr = [s for c in comms for s in c.scratch]

    def body(*refs):
        ins, ein, cin, outs, eout, cout, cscr = _split_refs(
            refs, [len(flat), len(e_in), len(c_in), len(slabs), len(e_out), len(c_out), len(c_scr)])
        bound = _bind(comms, cin, cout, cscr)
        for c, r in bound:
            c.begin(*r)
        pos = 0
        for groups, slab_rows, o_ref in zip(slabs, rows, outs):
            off = 0
            for grp, r in zip(groups, slab_rows):
                vals = []
                for _, transposed in grp:
                    val = ins[pos][0]
                    vals.append(val.T if transposed else val)
                    pos += 1
                o_ref[off:off + r, :] = (vals[0] if len(vals) == 1 else jnp.concatenate(vals, axis=1)).astype(BF16)
                off += r
        if fill is not None:
            fill(ein, eout)
        for phase in ("mid", "relay", "end"):
            for c, r in bound:
                getattr(c, phase)(*r)

    whole = pl.BlockSpec(memory_space=pltpu.VMEM)
    n_plain = len(slabs) + len(e_out)
    res = pl.pallas_call(
        body, name=name, in_specs=[whole] * (len(flat) + len(e_in)) + [ANY] * len(c_in), out_specs=[whole] * n_plain + [ANY] * len(c_out),
        out_shape=[jax.ShapeDtypeStruct((sum(r), D), BF16) for r in rows] + list(e_out) + c_out, scratch_shapes=c_scr,
        compiler_params=_cparams())(*flat, *e_in, *c_in)
    return list(res[:len(slabs)]), list(res[len(slabs):n_plain]), list(res[n_plain:])


def _step(x, target, p):
    n1, nm, n2 = p["norm_ffn1"], p["norm_mix"], p["norm_ffn2"]
    nf = p["norm_final"].reshape(1, D)
    lng, lnb = p["sgu_ln_g"], p["sgu_ln_b"]

    def chip_sums(grads, name):
        gots = _comm_only([_SiblingSwap(grads)], "swap_" + name)
        return _pair_sums(grads, gots, "pair_sums_" + name)

    def as_rows(a):
        return jnp.swapaxes(a, 1, 2)

    def updates(parts, sums, names):
        res = {}
        for cols in sorted({pt.shape[2] for pt in parts}):
            items, group = [], [(pt, sm, n) for pt, sm, n in zip(parts, sums, names) if pt.shape[2] == cols]
            for pt, sm, n in group:
                view = as_rows if p[n].shape[1:] != pt.shape[1:] and p[n].shape[2] > 128 else (lambda a: a)
                transposed = p[n].shape[1:] != pt.shape[1:] and p[n].shape[2] <= 128
                items.append((pt, sm, view(p[n]), view(p["m_" + n]), view(p["v_" + n]), transposed))
            for (pt, sm, n), four in zip(group, _adamw_matrices(items, "adamw_" + group[0][2])):
                res[n] = [as_rows(o) if o.shape != p[n].shape else o for o in four]
        return res

    (rows1,), _, _ = _pack_rows([[[(as_rows(p["ffn1_w_gate"]), False)], [(as_rows(p["ffn1_w_up"]), False)], [(p["ffn1_w_down"], False)]]],
                                "pack_ffn1")
    (rows_m, rows2d, rows2gu), (bias, wm2, wmt2, bsx), (gw1,) = _pack_rows(
        [[[(as_rows(p["w_in"]), False)], [(p["w_branch_att"], True), (p["w_branch_sgu"], True)], [(p["w_out"], False)]],
         [[(p["ffn2_w_down"], False)]],
         [[(as_rows(p["ffn2_w_gate"]), False)], [(as_rows(p["ffn2_w_up"]), False)]]],
        "gather_ffn1_pack_rest", [_Gather(rows1)], _tables(p["rel_bias"][0], p["sgu_w_s"][0], p["sgu_b_s"][0]))
    x1, ab1, h1, (gwm,) = _ffn_fwd(x, n1, gw1, gw1, 2 * R_FF, "ffn1_fwd", [_Gather(rows_m)])
    (q, k, v, zs, gt, h2), (gw2d,) = _mix_proj_fwd(x1, nm, p["b_gate"], gwm, [_Gather(rows2d)])
    y_att, (gw2gu,) = _att_fwd(q, k, v, bias, [_Gather(rows2gu)])
    x2, y_sgu, merged = _merge_fwd(x1, zs, gt, y_att, lng, lnb, wm2, bsx, gwm)
    dx3, ab2, hb, _, d_nf, loss = _ffn_fwd(x2, n2, gw2gu, gw2d, 0, "ffn2_fwd", head=(nf, target))

    (dab, g_down), _ = _ffn_bwd_hidden(ab2, dx3, gw2d, 0, "ffn2_bwd_hidden")
    (dx2, d_n2), _ = _ffn_bwd_input(x2, n2, dab, dx3, gw2gu, "ffn2_bwd")
    g_gu = _weight_grad(dab, hb, "ffn2_dw_gate_up", mats=2)
    dzg, dya, dys, d_bg, g_ba, g_bs, g_wo = _merge_bwd(dx2, gt, y_att, y_sgu, merged, gwm)
    g_late = [(g_gu, 0), (g_gu, 1), (g_down, 0), (g_ba, 0), (g_bs, 0), (g_wo, 0)]
    late = ("ffn2_w_gate", "ffn2_w_up", "ffn2_w_down", "w_branch_att", "w_branch_sgu", "w_out")
    (dzs, d_wm, d_bs, d_lng, d_lnb), gots_late = _sgu_bwd(zs, dys, lng, lnb, wm2, wmt2, bsx, [_SiblingSwap(g_late)])
    sums_late = _pair_sums(g_late, gots_late, "pair_sums_late")
    (dq, dk, dv, d_bias), parts_late = _att_bwd(q, k, v, bias, dya, [_ChipScatter(sums_late)])
    big = updates(parts_late, sums_late, late)
    dz = [dq, dk, dv, dzs, dzg]
    g_in = [(_weight_grad_pieces(dz, h2, "dw_in"), 0)]
    d_rel, got_in = _rel_bias_grad(d_bias, [_SiblingSwap(g_in)])
    sums_in = _pair_sums(g_in, got_in, "pair_sums_w_in")
    (dx1, d_nm), parts_in = _mix_proj_bwd(dz, x1, nm, dx2, gwm, [_ChipScatter(sums_in)])
    big.update(updates(parts_in, sums_in, ("w_in",)))
    small = {"norm_ffn2": d_n2, "norm_final": d_nf, "b_gate": d_bg, "sgu_ln_g": d_lng, "sgu_ln_b": d_lnb, "sgu_b_s": d_bs,
             "rel_bias": d_rel, "sgu_w_s": d_wm.reshape(_SMALL_2D["sgu_w_s"]), "norm_mix": d_nm}

    (dab, g_down), (*small_parts, loss_parts) = _ffn_bwd_hidden(ab1, dx1, gw1, 2 * R_FF, "ffn1_bwd_hidden",
                                                                [_AllToAll(list(small.values()) + [loss])])
    sums_d = chip_sums([(g_down, 0)], "ffn1_down")
    g_gu, (parts_d,) = _weight_grad(dab, h1, "ffn1_dw_gate_up", comms=[_ChipScatter(sums_d)], mats=2)
    sums_gu = chip_sums([(g_gu, 0), (g_gu, 1)], "ffn1_gate_up")
    (dx0, d_n1), parts_gu = _ffn_bwd_input(x, n1, dab, dx1, gw1, "ffn1_bwd", [_ChipScatter(sums_gu)])
    (n1_parts,) = _comm_only([_AllToAll([d_n1])], "gather_norm_ffn1")
    big.update(updates([parts_d] + parts_gu, sums_d + sums_gu, ("ffn1_w_down", "ffn1_w_gate", "ffn1_w_up")))
    out_s, loss_sum = _adamw_small(dict(zip(small, small_parts), norm_ffn1=n1_parts), loss_parts, p)
    return dx0, loss_sum[0, 0], [{**{n: four[i] for n, four in big.items()}, **s} for i, s in enumerate(out_s)]


_OUT_ORDER = ("norm_ffn1", "ffn1_w_gate", "ffn1_w_up", "ffn1_w_down", "norm_mix", "w_in", "b_gate", "rel_bias", "sgu_ln_g", "sgu_ln_b",
              "sgu_w_s", "sgu_b_s", "w_branch_att", "w_branch_sgu", "w_out", "norm_ffn2", "ffn2_w_gate", "ffn2_w_up", "ffn2_w_down",
              "norm_final")


def kernel(x, norm_ffn1, ffn1_w_gate, ffn1_w_up, ffn1_w_down, norm_mix, w_in, b_gate, rel_bias, sgu_ln_g, sgu_ln_b, sgu_w_s, sgu_b_s, w_branch_att, w_branch_sgu, w_out, norm_ffn2, ffn2_w_gate, ffn2_w_up, ffn2_w_down, norm_final, loss_target, m_norm_ffn1, m_ffn1_w_gate, m_ffn1_w_up, m_ffn1_w_down, m_norm_mix, m_w_in, m_b_gate, m_rel_bias, m_sgu_ln_g, m_sgu_ln_b, m_sgu_w_s, m_sgu_b_s, m_w_branch_att, m_w_branch_sgu, m_w_out, m_norm_ffn2, m_ffn2_w_gate, m_ffn2_w_up, m_ffn2_w_down, m_norm_final, v_norm_ffn1, v_ffn1_w_gate, v_ffn1_w_up, v_ffn1_w_down, v_norm_mix, v_w_in, v_b_gate, v_rel_bias, v_sgu_ln_g, v_sgu_ln_b, v_sgu_w_s, v_sgu_b_s, v_w_branch_att, v_w_branch_sgu, v_w_out, v_norm_ffn2, v_ffn2_w_gate, v_ffn2_w_up, v_ffn2_w_down, v_norm_final):
    args = dict(locals())
    dx, loss, outs = _step(x[0], loss_target[0], {pre + n: args[pre + n] for pre in ("", "m_", "v_") for n in _OUT_ORDER})
    return (loss, dx[None], *[o[n] for o in outs for n in _OUT_ORDER])
```
